```python
import math
import jax, jax.numpy as jnp
from jax import lax
import numpy as np

D_MODEL = 2048
BATCH = 8
SEQ = 4096
DEPTH = 1

HEAD_DIM = 64
N_Q_HEADS = D_MODEL // 128
N_KV_HEADS = 4
Q_PER_KV = N_Q_HEADS // N_KV_HEADS
ATTN_W = N_Q_HEADS * HEAD_DIM
KV_W = N_KV_HEADS * HEAD_DIM
WINDOW = 128
BLOCK = 128
SSM_W = D_MODEL // 2
GROUP = 16
N_GROUPS = SSM_W // GROUP
STATE = 64
IN_SIZES = (ATTN_W, KV_W, KV_W, ATTN_W, SSM_W, SSM_W, D_MODEL, D_MODEL)
IN_W = sum(IN_SIZES)
NORM_EPS = 1e-6

kernel_name = "hybrid_swa_sink_s5_gated_merge"


def rms_norm(x, w):
    xf = x.astype(jnp.float32)
    y = xf * lax.rsqrt(jnp.mean(xf * xf, axis=-1, keepdims=True) + NORM_EPS)
    return (y * w.astype(jnp.float32)).astype(x.dtype)


def sliding_window_attention(q, k, v, sinks):
    b, l = q.shape[0], q.shape[1]
    nb = l // BLOCK
    qb = q.reshape(b, nb, BLOCK, N_KV_HEADS, Q_PER_KV, HEAD_DIM)
    kb = k.reshape(b, nb, BLOCK, N_KV_HEADS, HEAD_DIM)
    vb = v.reshape(b, nb, BLOCK, N_KV_HEADS, HEAD_DIM)
    k_prev = jnp.concatenate([jnp.zeros_like(kb[:, :1]), kb[:, :-1]], axis=1)
    v_prev = jnp.concatenate([jnp.zeros_like(vb[:, :1]), vb[:, :-1]], axis=1)
    kk = jnp.concatenate([k_prev, kb], axis=2)
    vv = jnp.concatenate([v_prev, vb], axis=2)
    scale = 1.0 / math.sqrt(HEAD_DIM)
    scores = jnp.einsum('bnqgrd,bnsgd->bngrqs', qb, kk).astype(jnp.float32) * scale
    q_loc = jnp.arange(BLOCK)[:, None] + BLOCK
    k_loc = jnp.arange(2 * BLOCK)[None, :]
    diff = q_loc - k_loc
    k_abs = (jnp.arange(nb)[:, None, None] - 1) * BLOCK + k_loc[None]
    valid = (diff >= 0)[None] & (diff < WINDOW)[None] & (k_abs >= 0)
    scores = jnp.where(valid[None, :, None, None], scores, -1e30)
    sink = jnp.broadcast_to(
        sinks.astype(jnp.float32).reshape(1, 1, N_KV_HEADS, Q_PER_KV, 1, 1),
        scores.shape[:-1] + (1,))
    probs = jax.nn.softmax(jnp.concatenate([scores, sink], axis=-1), axis=-1)[..., :-1]
    out = jnp.einsum('bngrqs,bnsgd->bnqgrd', probs.astype(v.dtype), vv)
    return out.reshape(b, l, ATTN_W)


def s5_ssm(u, A_re, A_im, log_dt, B_re, B_im, C_re, C_im, D_skip):
    dt = jnp.exp(log_dt)[:, None]
    mag = jnp.exp(dt * A_re)
    ab_re = mag * jnp.cos(dt * A_im)
    ab_im = mag * jnp.sin(dt * A_im)
    num_re = ab_re - 1.0
    num_im = ab_im
    den = A_re * A_re + A_im * A_im
    cf_re = (num_re * A_re + num_im * A_im) / den
    cf_im = (num_im * A_re - num_re * A_im) / den
    bu_re = jnp.einsum('blgh,gph->blgp', u, B_re)
    bu_im = jnp.einsum('blgh,gph->blgp', u, B_im)
    b_re = cf_re * bu_re - cf_im * bu_im
    b_im = cf_re * bu_im + cf_im * bu_re
    a_re = jnp.broadcast_to(ab_re, b_re.shape)
    a_im = jnp.broadcast_to(ab_im, b_im.shape)

    def combine(e1, e2):
        a1r, a1i, b1r, b1i = e1
        a2r, a2i, b2r, b2i = e2
        return (a2r * a1r - a2i * a1i,
                a2r * a1i + a2i * a1r,
                a2r * b1r - a2i * b1i + b2r,
                a2r * b1i + a2i * b1r + b2i)

    _, _, s_re, s_im = lax.associative_scan(combine, (a_re, a_im, b_re, b_im), axis=1)
    y = (jnp.einsum('blgp,ghp->blgh', s_re, C_re)
         - jnp.einsum('blgp,ghp->blgh', s_im, C_im)
         + D_skip * u)
    return y


def _fwd_setup_inputs(seed: int = 0) -> dict:
    key = jax.random.key(seed)
    ks = jax.random.split(key, 20)
    f32 = jnp.float32
    n = jnp.arange(STATE, dtype=f32)
    x = jax.random.normal(ks[0], (BATCH, SEQ, D_MODEL), f32)
    norm_w = 1.0 + 0.02 * jax.random.normal(ks[1], (D_MODEL,), f32)
    w_in = jax.random.normal(ks[2], (D_MODEL, IN_W), f32) * D_MODEL ** -0.5
    q_norm_w = 1.0 + 0.02 * jax.random.normal(ks[3], (HEAD_DIM,), f32)
    k_norm_w = 1.0 + 0.02 * jax.random.normal(ks[4], (HEAD_DIM,), f32)
    sinks = jax.random.normal(ks[5], (N_Q_HEADS,), f32)
    w_attn_proj = jax.random.normal(ks[6], (ATTN_W, D_MODEL), f32) * ATTN_W ** -0.5
    A_re = -0.5 + 0.01 * jax.random.normal(ks[7], (N_GROUPS, STATE), f32)
    A_im = math.pi * n[None, :] + 0.01 * jax.random.normal(ks[8], (N_GROUPS, STATE), f32)
    log_dt = jax.random.uniform(ks[9], (N_GROUPS,), f32, math.log(1e-3), math.log(1e-1))
    b_scale = (2.0 * GROUP) ** -0.5
    B_re = jax.random.normal(ks[10], (N_GROUPS, STATE, GROUP), f32) * b_scale
    B_im = jax.random.normal(ks[11], (N_GROUPS, STATE, GROUP), f32) * b_scale
    c_scale = (2.0 * STATE) ** -0.5
    C_re = jax.random.normal(ks[12], (N_GROUPS, GROUP, STATE), f32) * c_scale
    C_im = jax.random.normal(ks[13], (N_GROUPS, GROUP, STATE), f32) * c_scale
    D_skip = jax.random.normal(ks[14], (N_GROUPS, GROUP), f32)
    w_glu = jax.random.normal(ks[15], (SSM_W, 2 * SSM_W), f32) * SSM_W ** -0.5
    b_glu = 0.01 * jax.random.normal(ks[16], (2 * SSM_W,), f32)
    w_ssm_proj = jax.random.normal(ks[17], (SSM_W, D_MODEL), f32) * SSM_W ** -0.5
    w_out = jax.random.normal(ks[18], (D_MODEL, D_MODEL), f32) * D_MODEL ** -0.5
    return {"x": x, "norm_w": norm_w, "w_in": w_in, "q_norm_w": q_norm_w,
            "k_norm_w": k_norm_w, "sinks": sinks, "w_attn_proj": w_attn_proj,
            "A_re": A_re, "A_im": A_im, "log_dt": log_dt, "B_re": B_re, "B_im": B_im,
            "C_re": C_re, "C_im": C_im, "D_skip": D_skip, "w_glu": w_glu,
            "b_glu": b_glu, "w_ssm_proj": w_ssm_proj, "w_out": w_out}


def _fwd_reference(x, norm_w, w_in, q_norm_w, k_norm_w, sinks, w_attn_proj, A_re, A_im,
              log_dt, B_re, B_im, C_re, C_im, D_skip, w_glu, b_glu, w_ssm_proj, w_out):
    b, l, _ = x.shape
    split_pts = list(np.cumsum(IN_SIZES)[:-1])
    f32 = jnp.float32
    for _layer in range(DEPTH):
        h = rms_norm(x, norm_w)
        proj = h @ w_in
        q, k, v, a_gate, u, z, g_a, g_s = jnp.split(proj, split_pts, axis=-1)
        q = rms_norm(q.reshape(b, l, N_Q_HEADS, HEAD_DIM), q_norm_w)
        k = rms_norm(k.reshape(b, l, N_KV_HEADS, HEAD_DIM), k_norm_w)
        v = v.reshape(b, l, N_KV_HEADS, HEAD_DIM)
        attn = sliding_window_attention(q, k, v, sinks)
        y_a = (attn * jax.nn.silu(a_gate)) @ w_attn_proj
        u_g = u.reshape(b, l, N_GROUPS, GROUP).astype(f32)
        y_ssm = s5_ssm(u_g, A_re.astype(f32), A_im.astype(f32), log_dt.astype(f32),
                       B_re.astype(f32), B_im.astype(f32), C_re.astype(f32),
                       C_im.astype(f32), D_skip.astype(f32))
        y_ssm = jax.nn.gelu(y_ssm.reshape(b, l, SSM_W)).astype(x.dtype)
        glu_a, glu_b = jnp.split(y_ssm @ w_glu + b_glu, 2, axis=-1)
        y_s = (glu_a * jax.nn.sigmoid(glu_b) * jax.nn.silu(z)) @ w_ssm_proj
        merged = jax.nn.sigmoid(g_a) * y_a + jax.nn.sigmoid(g_s) * y_s
        x = x + merged @ w_out
    return x


import jax as _jax
import jax.numpy as _jnp

TWIN_FORMAT = 'train_step'
FWD_PARAMS = ['x', 'norm_w', 'w_in', 'q_norm_w', 'k_norm_w', 'sinks', 'w_attn_proj', 'A_re', 'A_im', 'log_dt', 'B_re', 'B_im', 'C_re', 'C_im', 'D_skip', 'w_glu', 'b_glu', 'w_ssm_proj', 'w_out']
TWIN_WEIGHTS = ['norm_w', 'w_in', 'q_norm_w', 'k_norm_w', 'sinks', 'w_attn_proj', 'A_re', 'A_im', 'log_dt', 'B_re', 'B_im', 'C_re', 'C_im', 'D_skip', 'w_glu', 'b_glu', 'w_ssm_proj', 'w_out']
TWIN_DIFF_INPUT = 'x'
TWIN_INPUTS = ['x', 'norm_w', 'w_in', 'q_norm_w', 'k_norm_w', 'sinks', 'w_attn_proj', 'A_re', 'A_im', 'log_dt', 'B_re', 'B_im', 'C_re', 'C_im', 'D_skip', 'w_glu', 'b_glu', 'w_ssm_proj', 'w_out', 'loss_target', 'm_norm_w', 'm_w_in', 'm_q_norm_w', 'm_k_norm_w', 'm_sinks', 'm_w_attn_proj', 'm_A_re', 'm_A_im', 'm_log_dt', 'm_B_re', 'm_B_im', 'm_C_re', 'm_C_im', 'm_D_skip', 'm_w_glu', 'm_b_glu', 'm_w_ssm_proj', 'm_w_out', 'v_norm_w', 'v_w_in', 'v_q_norm_w', 'v_k_norm_w', 'v_sinks', 'v_w_attn_proj', 'v_A_re', 'v_A_im', 'v_log_dt', 'v_B_re', 'v_B_im', 'v_C_re', 'v_C_im', 'v_D_skip', 'v_w_glu', 'v_b_glu', 'v_w_ssm_proj', 'v_w_out']
TWIN_OUTPUTS = ['loss', 'grad_x', 'grad_norm_w', 'grad_w_in', 'grad_q_norm_w', 'grad_k_norm_w', 'grad_sinks', 'grad_w_attn_proj', 'grad_A_re', 'grad_A_im', 'grad_log_dt', 'grad_B_re', 'grad_B_im', 'grad_C_re', 'grad_C_im', 'grad_D_skip', 'grad_w_glu', 'grad_b_glu', 'grad_w_ssm_proj', 'grad_w_out', 'delta_norm_w', 'delta_w_in', 'delta_q_norm_w', 'delta_k_norm_w', 'delta_sinks', 'delta_w_attn_proj', 'delta_A_re', 'delta_A_im', 'delta_log_dt', 'delta_B_re', 'delta_B_im', 'delta_C_re', 'delta_C_im', 'delta_D_skip', 'delta_w_glu', 'delta_b_glu', 'delta_w_ssm_proj', 'delta_w_out', 'new_m_norm_w', 'new_m_w_in', 'new_m_q_norm_w', 'new_m_k_norm_w', 'new_m_sinks', 'new_m_w_attn_proj', 'new_m_A_re', 'new_m_A_im', 'new_m_log_dt', 'new_m_B_re', 'new_m_B_im', 'new_m_C_re', 'new_m_C_im', 'new_m_D_skip', 'new_m_w_glu', 'new_m_b_glu', 'new_m_w_ssm_proj', 'new_m_w_out', 'new_v_norm_w', 'new_v_w_in', 'new_v_q_norm_w', 'new_v_k_norm_w', 'new_v_sinks', 'new_v_w_attn_proj', 'new_v_A_re', 'new_v_A_im', 'new_v_log_dt', 'new_v_B_re', 'new_v_B_im', 'new_v_C_re', 'new_v_C_im', 'new_v_D_skip', 'new_v_w_glu', 'new_v_b_glu', 'new_v_w_ssm_proj', 'new_v_w_out']
TWIN_LEAF_KINDS = {'loss': 'loss', 'grad_x': 'grad_x', 'grad_norm_w': 'grad_w', 'grad_w_in': 'grad_w', 'grad_q_norm_w': 'grad_w', 'grad_k_norm_w': 'grad_w', 'grad_sinks': 'grad_w', 'grad_w_attn_proj': 'grad_w', 'grad_A_re': 'grad_w', 'grad_A_im': 'grad_w', 'grad_log_dt': 'grad_w', 'grad_B_re': 'grad_w', 'grad_B_im': 'grad_w', 'grad_C_re': 'grad_w', 'grad_C_im': 'grad_w', 'grad_D_skip': 'grad_w', 'grad_w_glu': 'grad_w', 'grad_b_glu': 'grad_w', 'grad_w_ssm_proj': 'grad_w', 'grad_w_out': 'grad_w', 'delta_norm_w': 'delta_w', 'delta_w_in': 'delta_w', 'delta_q_norm_w': 'delta_w', 'delta_k_norm_w': 'delta_w', 'delta_sinks': 'delta_w', 'delta_w_attn_proj': 'delta_w', 'delta_A_re': 'delta_w', 'delta_A_im': 'delta_w', 'delta_log_dt': 'delta_w', 'delta_B_re': 'delta_w', 'delta_B_im': 'delta_w', 'delta_C_re': 'delta_w', 'delta_C_im': 'delta_w', 'delta_D_skip': 'delta_w', 'delta_w_glu': 'delta_w', 'delta_b_glu': 'delta_w', 'delta_w_ssm_proj': 'delta_w', 'delta_w_out': 'delta_w', 'new_m_norm_w': 'new_m', 'new_m_w_in': 'new_m', 'new_m_q_norm_w': 'new_m', 'new_m_k_norm_w': 'new_m', 'new_m_sinks': 'new_m', 'new_m_w_attn_proj': 'new_m', 'new_m_A_re': 'new_m', 'new_m_A_im': 'new_m', 'new_m_log_dt': 'new_m', 'new_m_B_re': 'new_m', 'new_m_B_im': 'new_m', 'new_m_C_re': 'new_m', 'new_m_C_im': 'new_m', 'new_m_D_skip': 'new_m', 'new_m_w_glu': 'new_m', 'new_m_b_glu': 'new_m', 'new_m_w_ssm_proj': 'new_m', 'new_m_w_out': 'new_m', 'new_v_norm_w': 'new_v', 'new_v_w_in': 'new_v', 'new_v_q_norm_w': 'new_v', 'new_v_k_norm_w': 'new_v', 'new_v_sinks': 'new_v', 'new_v_w_attn_proj': 'new_v', 'new_v_A_re': 'new_v', 'new_v_A_im': 'new_v', 'new_v_log_dt': 'new_v', 'new_v_B_re': 'new_v', 'new_v_B_im': 'new_v', 'new_v_C_re': 'new_v', 'new_v_C_im': 'new_v', 'new_v_D_skip': 'new_v', 'new_v_w_glu': 'new_v', 'new_v_b_glu': 'new_v', 'new_v_w_ssm_proj': 'new_v', 'new_v_w_out': 'new_v'}


def _forward(args):
    return _fwd_reference(*[args[k] for k in FWD_PARAMS])


def _output_shape():
    def fwd():
        inp = _fwd_setup_inputs(0)
        return _fwd_reference(*[inp[k] for k in FWD_PARAMS])
    out = _jax.eval_shape(fwd)
    return out.shape, out.dtype

N_MICROBATCH = 1
ADAM_LR = 0.001
ADAM_B1 = 0.9
ADAM_B2 = 0.999
ADAM_EPS = 1e-08
ADAM_WD = 0.01
ADAM_STEP = 10
PER_EXAMPLE_BATCH_AXIS = {'x': 0, 'loss_target': 0}
SHARED_INPUTS = []
_WEIGHT_DTYPES = {'norm_w': _jnp.float32, 'w_in': _jnp.float32, 'q_norm_w': _jnp.float32, 'k_norm_w': _jnp.float32, 'sinks': _jnp.float32, 'w_attn_proj': _jnp.float32, 'A_re': _jnp.float32, 'A_im': _jnp.float32, 'log_dt': _jnp.float32, 'B_re': _jnp.float32, 'B_im': _jnp.float32, 'C_re': _jnp.float32, 'C_im': _jnp.float32, 'D_skip': _jnp.float32, 'w_glu': _jnp.float32, 'b_glu': _jnp.float32, 'w_ssm_proj': _jnp.float32, 'w_out': _jnp.float32}
MOMENT_SCALE = {'norm_w': 5.415651e-01, 'w_in': 1.454204e-02, 'q_norm_w': 7.166631e-01, 'k_norm_w': 7.176503e-01, 'sinks': 1.346698e-01, 'w_attn_proj': 8.135552e-03, 'A_re': 1.261739e-03, 'A_im': 1.021121e-03, 'log_dt': 9.567170e-01, 'B_re': 9.767128e-04, 'B_im': 9.783162e-04, 'C_re': 1.907451e-03, 'C_im': 1.941580e-03, 'D_skip': 5.021597e-01, 'w_glu': 7.813848e-02, 'b_glu': 2.550773e-01, 'w_ssm_proj': 2.072099e-02, 'w_out': 1.991139e-02}


def _to_microbatches(a, axis):
    t = _jnp.moveaxis(a, axis, 0)
    t = t.reshape((N_MICROBATCH, t.shape[0] // N_MICROBATCH) + t.shape[1:])
    return _jnp.moveaxis(t, 1, axis + 1)


def setup_inputs(seed: int = 0) -> dict:
    inp = _fwd_setup_inputs(seed)
    key = _jax.random.fold_in(_jax.random.key(seed), 7919)
    shape, _ = _output_shape()
    out = dict(inp)
    out["loss_target"] = _jax.random.normal(_jax.random.fold_in(key, 0), shape, _jnp.float32)
    for i, name in enumerate(TWIN_WEIGHTS):
        w = inp[name].astype(_jnp.float32)
        if MOMENT_SCALE is None:
            s = _jnp.sqrt(_jnp.mean(_jnp.square(w)) + 1e-30)
        else:
            s = MOMENT_SCALE[name]
        km, kv = _jax.random.split(_jax.random.fold_in(key, i + 1))
        out[name] = w
        out["m_" + name] = s * _jax.random.normal(km, w.shape, _jnp.float32)
        out["v_" + name] = (s * s) * _jax.random.uniform(kv, w.shape, _jnp.float32, 0.5, 1.5)
    if N_MICROBATCH > 1:
        for name, axis in PER_EXAMPLE_BATCH_AXIS.items():
            out[name] = _to_microbatches(out[name], axis)
    return {'x': out['x'], 'norm_w': out['norm_w'], 'w_in': out['w_in'], 'q_norm_w': out['q_norm_w'], 'k_norm_w': out['k_norm_w'], 'sinks': out['sinks'], 'w_attn_proj': out['w_attn_proj'], 'A_re': out['A_re'], 'A_im': out['A_im'], 'log_dt': out['log_dt'], 'B_re': out['B_re'], 'B_im': out['B_im'], 'C_re': out['C_re'], 'C_im': out['C_im'], 'D_skip': out['D_skip'], 'w_glu': out['w_glu'], 'b_glu': out['b_glu'], 'w_ssm_proj': out['w_ssm_proj'], 'w_out': out['w_out'], 'loss_target': out['loss_target'], 'm_norm_w': out['m_norm_w'], 'm_w_in': out['m_w_in'], 'm_q_norm_w': out['m_q_norm_w'], 'm_k_norm_w': out['m_k_norm_w'], 'm_sinks': out['m_sinks'], 'm_w_attn_proj': out['m_w_attn_proj'], 'm_A_re': out['m_A_re'], 'm_A_im': out['m_A_im'], 'm_log_dt': out['m_log_dt'], 'm_B_re': out['m_B_re'], 'm_B_im': out['m_B_im'], 'm_C_re': out['m_C_re'], 'm_C_im': out['m_C_im'], 'm_D_skip': out['m_D_skip'], 'm_w_glu': out['m_w_glu'], 'm_b_glu': out['m_b_glu'], 'm_w_ssm_proj': out['m_w_ssm_proj'], 'm_w_out': out['m_w_out'], 'v_norm_w': out['v_norm_w'], 'v_w_in': out['v_w_in'], 'v_q_norm_w': out['v_q_norm_w'], 'v_k_norm_w': out['v_k_norm_w'], 'v_sinks': out['v_sinks'], 'v_w_attn_proj': out['v_w_attn_proj'], 'v_A_re': out['v_A_re'], 'v_A_im': out['v_A_im'], 'v_log_dt': out['v_log_dt'], 'v_B_re': out['v_B_re'], 'v_B_im': out['v_B_im'], 'v_C_re': out['v_C_re'], 'v_C_im': out['v_C_im'], 'v_D_skip': out['v_D_skip'], 'v_w_glu': out['v_w_glu'], 'v_b_glu': out['v_b_glu'], 'v_w_ssm_proj': out['v_w_ssm_proj'], 'v_w_out': out['v_w_out']}


def _loss(weights, diff, rest, loss_target):
    with _jax.named_scope("forward"):
        args = {**rest, TWIN_DIFF_INPUT: diff, **{k: w.astype(_WEIGHT_DTYPES[k]) for k, w in weights.items()}}
        y = _forward(args)
    with _jax.named_scope("loss_head"):
        err = _jnp.square(y.astype(_jnp.float32) - loss_target)
        return 0.5 * _jnp.sum(_jnp.mean(err, axis=-1)) if err.ndim else 0.5 * err


def _adamw(w, g, m, v):
    m = ADAM_B1 * m + (1.0 - ADAM_B1) * g
    v = ADAM_B2 * v + (1.0 - ADAM_B2) * _jnp.square(g)
    m_hat = m / (1.0 - ADAM_B1 ** ADAM_STEP)
    v_hat = v / (1.0 - ADAM_B2 ** ADAM_STEP)
    delta = -ADAM_LR * (m_hat / (_jnp.sqrt(v_hat) + ADAM_EPS) + ADAM_WD * w)
    return delta, m, v


def reference(x, norm_w, w_in, q_norm_w, k_norm_w, sinks, w_attn_proj, A_re, A_im, log_dt, B_re, B_im, C_re, C_im, D_skip, w_glu, b_glu, w_ssm_proj, w_out, loss_target, m_norm_w, m_w_in, m_q_norm_w, m_k_norm_w, m_sinks, m_w_attn_proj, m_A_re, m_A_im, m_log_dt, m_B_re, m_B_im, m_C_re, m_C_im, m_D_skip, m_w_glu, m_b_glu, m_w_ssm_proj, m_w_out, v_norm_w, v_w_in, v_q_norm_w, v_k_norm_w, v_sinks, v_w_attn_proj, v_A_re, v_A_im, v_log_dt, v_B_re, v_B_im, v_C_re, v_C_im, v_D_skip, v_w_glu, v_b_glu, v_w_ssm_proj, v_w_out):
    given = dict(x=x, norm_w=norm_w, w_in=w_in, q_norm_w=q_norm_w, k_norm_w=k_norm_w, sinks=sinks, w_attn_proj=w_attn_proj, A_re=A_re, A_im=A_im, log_dt=log_dt, B_re=B_re, B_im=B_im, C_re=C_re, C_im=C_im, D_skip=D_skip, w_glu=w_glu, b_glu=b_glu, w_ssm_proj=w_ssm_proj, w_out=w_out, loss_target=loss_target, m_norm_w=m_norm_w, m_w_in=m_w_in, m_q_norm_w=m_q_norm_w, m_k_norm_w=m_k_norm_w, m_sinks=m_sinks, m_w_attn_proj=m_w_attn_proj, m_A_re=m_A_re, m_A_im=m_A_im, m_log_dt=m_log_dt, m_B_re=m_B_re, m_B_im=m_B_im, m_C_re=m_C_re, m_C_im=m_C_im, m_D_skip=m_D_skip, m_w_glu=m_w_glu, m_b_glu=m_b_glu, m_w_ssm_proj=m_w_ssm_proj, m_w_out=m_w_out, v_norm_w=v_norm_w, v_w_in=v_w_in, v_q_norm_w=v_q_norm_w, v_k_norm_w=v_k_norm_w, v_sinks=v_sinks, v_w_attn_proj=v_w_attn_proj, v_A_re=v_A_re, v_A_im=v_A_im, v_log_dt=v_log_dt, v_B_re=v_B_re, v_B_im=v_B_im, v_C_re=v_C_re, v_C_im=v_C_im, v_D_skip=v_D_skip, v_w_glu=v_w_glu, v_b_glu=v_b_glu, v_w_ssm_proj=v_w_ssm_proj, v_w_out=v_w_out)
    weights = {n: given[n] for n in TWIN_WEIGHTS}
    shared = {n: given[n] for n in SHARED_INPUTS}
    per_example = {n: given[n] for n in ['x']}
    grad_fn = _jax.value_and_grad(_loss, argnums=(0, 1))

    def one_microbatch(ex, loss_target):
        ex = dict(ex)
        diff = ex.pop(TWIN_DIFF_INPUT)
        return grad_fn(weights, diff, {**shared, **ex}, loss_target)

    if N_MICROBATCH == 1:
        loss, (grad_w, grad_x) = one_microbatch(per_example, given["loss_target"])
    else:
        def body(carry, xs):
            loss_sum, grad_sum = carry
            l_k, (gw_k, gx_k) = one_microbatch(xs[0], xs[1])
            with _jax.named_scope("update"):
                return (loss_sum + l_k, _jax.tree.map(_jnp.add, grad_sum, gw_k)), gx_k

        init = (_jnp.zeros((), _jnp.float32), _jax.tree.map(_jnp.zeros_like, weights))
        (loss, grad_w), grad_x = _jax.lax.scan(body, init, (per_example, given["loss_target"]))
    with _jax.named_scope("update"):
        delta_w, new_m, new_v = {}, {}, {}
        for n in TWIN_WEIGHTS:
            delta_w[n], new_m[n], new_v[n] = _adamw(weights[n], grad_w[n], given["m_" + n], given["v_" + n])
    return (loss, grad_x, *[grad_w[n] for n in TWIN_WEIGHTS], *[delta_w[n] for n in TWIN_WEIGHTS],
            *[new_m[n] for n in TWIN_WEIGHTS], *[new_v[n] for n in TWIN_WEIGHTS])
```

```python
import math
from typing import NamedTuple

import jax
import jax.numpy as jnp
from jax import lax
from jax.experimental import pallas as pl
from jax.experimental.pallas import tpu as pltpu

F32 = jnp.float32
BF16 = jnp.bfloat16
MESH = pl.DeviceIdType.MESH

HEAD_DIM = 64
N_KV_HEADS = 4
GROUP = 16
STATE = 64
BLOCK = 128
NORM_EPS = 1e-6
N_DEV = 8
N_CHIPS = 4
LANES = 128
SUBLANES = 8
MXU_DIM = 256
VMEM_BYTES = 64 * 1024 * 1024
VMEM_CAP = VMEM_BYTES - 8 * 1024 * 1024

ADAM_LR = 0.001
ADAM_B1 = 0.9
ADAM_B2 = 0.999
ADAM_EPS = 1e-08
ADAM_WD = 0.01
ADAM_STEP = 10

GELU_C = math.sqrt(2.0 / math.pi)
GELU_K = 0.044715


def _tile(dim, pref, mult=LANES):
    if dim <= pref:
        return dim
    best = None
    for d in range(mult, pref + 1, mult):
        if dim % d == 0:
            best = d
    assert best is not None, (dim, pref, mult)
    return best


def _params(semantics=None, vmem=None):
    kw = {}
    if semantics is not None:
        kw["dimension_semantics"] = semantics
    if vmem is not None:
        kw["vmem_limit_bytes"] = int(min(VMEM_CAP, max(vmem, 32 * 1024 * 1024)))
    return pltpu.CompilerParams(**kw)


def _nbytes(shape, dtype):
    return math.prod(shape) * jnp.dtype(dtype).itemsize


def _sigmoid(x):
    return 1.0 / (1.0 + jnp.exp(-x))


def _silu(x):
    return x * _sigmoid(x)


def _dsilu(x):
    s = _sigmoid(x)
    return s * (1.0 + x * (1.0 - s))


def _gelu(x):
    return 0.5 * x * (1.0 + jnp.tanh(GELU_C * (x + GELU_K * x * x * x)))


def _dgelu(x):
    t = jnp.tanh(GELU_C * (x + GELU_K * x * x * x))
    return 0.5 * (1.0 + t) + 0.5 * x * (1.0 - t * t) * GELU_C * (1.0 + 3.0 * GELU_K * x * x)


def _dot(a, b, dims):
    return lax.dot_general(a, b, (dims, ((), ())), preferred_element_type=F32)


NN = ((1,), (0,))
NT = ((1,), (1,))
TN = ((0,), (0,))


def _any_spec():
    return pl.BlockSpec(memory_space=pl.ANY)


def _all_gather(shards, name):
    n = len(shards)

    def body(*refs):
        ins, outs = refs[:n], refs[n:2 * n]
        send_sems, recv_sems, local_sems = refs[2 * n:]
        x, y, c = lax.axis_index("x"), lax.axis_index("y"), lax.axis_index("c")
        me, sibling = (x, y, c), (x, y, 1 - c)
        chips = [(1 - x, y), (x, 1 - y), (1 - x, 1 - y)]

        def rows(k, px, py, pc):
            r = shards[k].shape[0]
            return outs[k].at[pl.ds((4 * px + 2 * py + pc) * r, r), :]

        def copy(k, s, block, to, src=None):
            return pltpu.make_async_remote_copy(
                src_ref=rows(k, *block) if src is None else src, dst_ref=rows(k, *block),
                send_sem=send_sems.at[7 * k + s], recv_sem=recv_sems.at[7 * k + s],
                device_id=to, device_id_type=MESH)

        mine = [pltpu.make_async_copy(ins[k], rows(k, *me), local_sems.at[k]) for k in range(n)]
        for cp in mine:
            cp.start()
        sent = []
        for k in range(n):
            sent.append(copy(k, 0, me, sibling, src=ins[k]))
            sent += [copy(k, 1 + j, me, (*chip, c), src=ins[k]) for j, chip in enumerate(chips)]
        for cp in sent:
            cp.start()
        for j, chip in enumerate(chips):
            for k in range(n):
                copy(k, 1 + j, (*chip, c), me).wait_recv()
                fwd = copy(k, 4 + j, (*chip, c), sibling)
                fwd.start()
                sent.append(fwd)
        for k in range(n):
            copy(k, 0, sibling, me).wait_recv()
            for j, chip in enumerate(chips):
                copy(k, 4 + j, (*chip, 1 - c), me).wait_recv()
        for cp in sent:
            cp.wait_send()
        for cp in mine:
            cp.wait()

    return pl.pallas_call(
        body, name=name,
        out_shape=[jax.ShapeDtypeStruct((N_DEV * s.shape[0], s.shape[1]), s.dtype) for s in shards],
        in_specs=[_any_spec() for _ in shards], out_specs=[_any_spec() for _ in shards],
        scratch_shapes=[pltpu.SemaphoreType.DMA((7 * n,)), pltpu.SemaphoreType.DMA((7 * n,)),
                        pltpu.SemaphoreType.DMA((n,))],
    )(*shards)


def _sibling_exchange(grads, name):
    n = len(grads)

    def body(*refs):
        ins, outs = refs[:n], refs[n:2 * n]
        send_sems, recv_sems = refs[2 * n:]
        x, y, c = lax.axis_index("x"), lax.axis_index("y"), lax.axis_index("c")
        sibling = (x, y, 1 - c)
        copies = []
        for k in range(n):
            r = grads[k].shape[0] // N_DEV
            for j in range(N_CHIPS):
                copies.append(pltpu.make_async_remote_copy(
                    src_ref=ins[k].at[pl.ds((2 * j + 1 - c) * r, r), :],
                    dst_ref=outs[k].at[pl.ds(j * r, r), :],
                    send_sem=send_sems.at[N_CHIPS * k + j], recv_sem=recv_sems.at[N_CHIPS * k + j],
                    device_id=sibling, device_id_type=MESH))
        for cp in copies:
            cp.start()
        for cp in copies:
            cp.wait()

    return pl.pallas_call(
        body, name=name,
        out_shape=[jax.ShapeDtypeStruct((g.shape[0] // 2, g.shape[1]), g.dtype) for g in grads],
        in_specs=[_any_spec() for _ in grads], out_specs=[_any_spec() for _ in grads],
        scratch_shapes=[pltpu.SemaphoreType.DMA((N_CHIPS * n,)), pltpu.SemaphoreType.DMA((N_CHIPS * n,))],
    )(*grads)


def _chip_exchange(parts, name):
    n = len(parts)

    def body(*refs):
        ins, outs = refs[:n], refs[n:2 * n]
        send_sems, recv_sems, local_sems = refs[2 * n:]
        x, y, c = lax.axis_index("x"), lax.axis_index("y"), lax.axis_index("c")
        my_chip = 2 * x + y
        chips = [(1 - x, y), (x, 1 - y), (1 - x, 1 - y)]
        local, sent = [], []
        for k in range(n):
            r = parts[k].shape[0] // N_CHIPS
            mine = pl.ds(my_chip * r, r)
            local.append(pltpu.make_async_copy(ins[k].at[mine, :], outs[k].at[mine, :], local_sems.at[k]))
            for s, (px, py) in enumerate(chips):
                sent.append(pltpu.make_async_remote_copy(
                    src_ref=ins[k].at[pl.ds((2 * px + py) * r, r), :], dst_ref=outs[k].at[mine, :],
                    send_sem=send_sems.at[3 * k + s], recv_sem=recv_sems.at[3 * k + s],
                    device_id=(px, py, c), device_id_type=MESH))
        for cp in local + sent:
            cp.start()
        for cp in sent:
            cp.wait()
        for cp in local:
            cp.wait()

    return pl.pallas_call(
        body, name=name,
        out_shape=[jax.ShapeDtypeStruct(p.shape, p.dtype) for p in parts],
        in_specs=[_any_spec() for _ in parts], out_specs=[_any_spec() for _ in parts],
        scratch_shapes=[pltpu.SemaphoreType.DMA((3 * n,)), pltpu.SemaphoreType.DMA((3 * n,)),
                        pltpu.SemaphoreType.DMA((n,))],
    )(*parts)


class Cols(NamedTuple):
    arr: jax.Array
    off: int
    width: int


def _cols(a):
    return a if isinstance(a, Cols) else Cols(a, 0, a.shape[1])


def _matmul(a, b, *, mode, name, out_dtype=F32, tm=1024, tn=1024, tk=2048, bias=None, out_cols=None, into=None):
    a = _cols(a)
    if mode == "nn":
        (m, k), (k2, n) = (a.arr.shape[0], a.width), b.shape
    elif mode == "nt":
        (m, k), (n, k2) = (a.arr.shape[0], a.width), b.shape
    else:
        (k, m), (k2, n) = (a.arr.shape[0], a.width), b.shape
    assert k == k2, (a.arr.shape, b.shape, mode)
    tm, tn, tk = _tile(m, tm), _tile(n, tn), _tile(k, tk)
    nk = k // tk
    dims = {"nn": NN, "nt": NT, "tn": TN}[mode]
    if mode == "tn":
        assert a.off % tm == 0
        a_spec = pl.BlockSpec((tk, tm), lambda i, j, kk, o=a.off // tm: (kk, i + o))
    else:
        assert a.off % tk == 0
        a_spec = pl.BlockSpec((tm, tk), lambda i, j, kk, o=a.off // tk: (i, kk + o))
    b_spec = (pl.BlockSpec((tn, tk), lambda i, j, kk: (j, kk)) if mode == "nt"
              else pl.BlockSpec((tk, tn), lambda i, j, kk: (kk, j)))
    in_specs, operands = [a_spec, b_spec], [a.arr, b]
    if bias is not None:
        in_specs.append(pl.BlockSpec((1, tn), lambda i, j, kk: (0, j)))
        operands.append(bias)
    total_w, o_off = out_cols if out_cols is not None else (n, 0)
    assert o_off % tn == 0
    aliases = {}
    if into is not None:
        assert into.shape == (m, total_w) and into.dtype == out_dtype
        in_specs.append(_any_spec())
        operands.append(into)
        aliases = {len(operands) - 1: 0}
    n_in = len(operands)

    def body(*refs):
        a_ref, b_ref = refs[0], refs[1]
        bias_ref = refs[2] if bias is not None else None
        o_ref = refs[n_in]
        acc_ref = refs[-1] if nk > 1 else None
        part = _dot(a_ref[...].astype(BF16), b_ref[...].astype(BF16), dims)

        def finish(acc):
            if bias_ref is not None:
                acc = acc + bias_ref[...]
            o_ref[...] = acc.astype(out_dtype)

        if nk == 1:
            finish(part)
        else:
            kk = pl.program_id(2)

            @pl.when(kk == 0)
            def _():
                acc_ref[...] = part

            @pl.when(kk > 0)
            def _():
                acc_ref[...] += part

            @pl.when(kk == nk - 1)
            def _():
                finish(acc_ref[...])

    vmem = 2 * (_nbytes((tm, tk), a.arr.dtype) + _nbytes((tk, tn), b.dtype) + _nbytes((tm, tn), out_dtype))
    vmem += 3 * _nbytes((tm, tn), F32)
    return pl.pallas_call(
        body, name=name, out_shape=jax.ShapeDtypeStruct((m, total_w), out_dtype),
        grid=(m // tm, n // tn, nk), in_specs=in_specs,
        out_specs=pl.BlockSpec((tm, tn), lambda i, j, kk, o=o_off // tn: (i, j + o)),
        scratch_shapes=[pltpu.VMEM((tm, tn), F32)] if nk > 1 else [],
        input_output_aliases=aliases,
        compiler_params=_params(("parallel", "parallel", "arbitrary"), vmem),
    )(*operands)


def _ew(fn, *, name, rows, width, tiles, vecs=(), outs, accs=0, tl=512, cw=512, into=None, with_col=False):
    tl, cw = _tile(rows, tl, SUBLANES), _tile(width, cw)
    ncol = width // cw
    nt_, nv = len(tiles), len(vecs)
    into = list(into) if into is not None else [None] * len(outs)
    aliased = [t for t in into if t is not None]

    def off(o):
        assert o % cw == 0, (name, o, cw)
        return o // cw

    in_specs, vmem = [], 0
    for t in tiles:
        arr, o = t[0], off(t[1])
        wrap = t[2] // cw if len(t) > 2 else ncol
        in_specs.append(pl.BlockSpec((tl, cw), lambda j, i, o=o, wrap=wrap: (i, o + j % wrap)))
        vmem += _nbytes((tl, cw), arr.dtype)
    in_specs += [pl.BlockSpec((1, cw), lambda j, i, o=off(o): (0, j + o)) for _, o in vecs]
    in_specs += [_any_spec() for _ in aliased]
    out_shape, out_specs, aliases = [], [], {}
    n_in = nt_ + nv
    for idx, ((dt, tw, o), tgt) in enumerate(zip(outs, into)):
        out_shape.append(jax.ShapeDtypeStruct((rows, tw), dt))
        out_specs.append(pl.BlockSpec((tl, cw), lambda j, i, o=off(o): (i, j + o)))
        vmem += _nbytes((tl, cw), dt)
        if tgt is not None:
            assert tgt.shape == (rows, tw) and tgt.dtype == dt, (name, tgt.shape, tgt.dtype)
            aliases[n_in + len(aliases)] = idx
    for _ in range(accs):
        out_shape.append(jax.ShapeDtypeStruct((1, width), F32))
        out_specs.append(pl.BlockSpec((1, cw), lambda j, i: (0, j)))
    n_out = len(outs)

    def body(*refs):
        vals = [r[...] for r in refs[:n_in]]
        out_refs = refs[n_in + len(aliased):]
        res = fn(pl.program_id(0), *vals) if with_col else fn(*vals)
        res = res if isinstance(res, (tuple, list)) else (res,)
        assert len(res) == n_out + accs, (name, len(res))
        for r, v in zip(out_refs[:n_out], res[:n_out]):
            r[...] = v.astype(r.dtype)
        first = pl.program_id(1) == 0
        for r, v in zip(out_refs[n_out:], res[n_out:]):
            s = jnp.sum(v, axis=0, keepdims=True)

            @pl.when(first)
            def _(r=r, s=s):
                r[...] = s

            @pl.when(jnp.logical_not(first))
            def _(r=r, s=s):
                r[...] += s

    return pl.pallas_call(
        body, name=name, out_shape=out_shape, grid=(ncol, rows // tl),
        in_specs=in_specs, out_specs=out_specs, input_output_aliases=aliases,
        compiler_params=_params(("parallel", "arbitrary"), 3 * vmem),
    )(*[t[0] for t in tiles], *[v for v, _ in vecs], *aliased)


def _rmsnorm_fwd(x, w_row, name):
    rows, d = x.shape
    tl = _tile(rows, 512, SUBLANES)

    def body(x_ref, w_ref, h_ref):
        xv = x_ref[...]
        rstd = lax.rsqrt(jnp.mean(xv * xv, axis=-1, keepdims=True) + NORM_EPS)
        h_ref[...] = (xv * rstd * w_ref[...]).astype(BF16)

    return pl.pallas_call(
        body, name=name, out_shape=jax.ShapeDtypeStruct((rows, d), BF16), grid=(rows // tl,),
        in_specs=[pl.BlockSpec((tl, d), lambda i: (i, 0)), pl.BlockSpec((1, d), lambda i: (0, 0))],
        out_specs=pl.BlockSpec((tl, d), lambda i: (i, 0)),
        compiler_params=_params(("parallel",)),
    )(x, w_row)


def _rmsnorm_bwd(x, w_row, dh, dout, name):
    rows, d = x.shape
    tl = _tile(rows, 256, SUBLANES)

    def body(x_ref, w_ref, dh_ref, dout_ref, gx_ref, gw_ref):
        xv = x_ref[...]
        rstd = lax.rsqrt(jnp.mean(xv * xv, axis=-1, keepdims=True) + NORM_EPS)
        xn = xv * rstd
        dhv = dh_ref[...]
        dxn = dhv * w_ref[...]
        dx = rstd * (dxn - xn * jnp.mean(dxn * xn, axis=-1, keepdims=True))
        gx_ref[...] = dout_ref[...] + dx
        gw = jnp.sum(dhv * xn, axis=0, keepdims=True)

        @pl.when(pl.program_id(0) == 0)
        def _():
            gw_ref[...] = gw

        @pl.when(pl.program_id(0) > 0)
        def _():
            gw_ref[...] += gw

    tile = pl.BlockSpec((tl, d), lambda i: (i, 0))
    row = pl.BlockSpec((1, d), lambda i: (0, 0))
    return pl.pallas_call(
        body, name=name,
        out_shape=[jax.ShapeDtypeStruct((rows, d), F32), jax.ShapeDtypeStruct((1, d), F32)],
        grid=(rows // tl,), in_specs=[tile, row, tile, tile], out_specs=[tile, row],
        compiler_params=_params(("arbitrary",)),
    )(x, w_row, dh, dout)


def _head_norm(v, w_row):
    rstd = lax.rsqrt(jnp.mean(v * v, axis=-1, keepdims=True) + NORM_EPS)
    vn = v * rstd
    return vn, vn * w_row, rstd


def _attn_specs(attn_w, kv_w):
    half = attn_w // 2
    kcol, vcol = attn_w // kv_w, attn_w // kv_w + 1
    gcol = (attn_w + 2 * kv_w) // half
    prev = lambda i: jnp.maximum(i - 1, 0)
    return [
        pl.BlockSpec((BLOCK, attn_w), lambda i: (i, 0)),
        pl.BlockSpec((BLOCK, kv_w), lambda i: (prev(i), kcol)),
        pl.BlockSpec((BLOCK, kv_w), lambda i: (i, kcol)),
        pl.BlockSpec((BLOCK, kv_w), lambda i: (prev(i), vcol)),
        pl.BlockSpec((BLOCK, kv_w), lambda i: (i, vcol)),
        pl.BlockSpec((BLOCK, half), lambda i: (i, gcol)),
        pl.BlockSpec((BLOCK, half), lambda i: (i, gcol + 1)),
    ]


def _band_mask(i):
    q_loc = lax.broadcasted_iota(jnp.int32, (BLOCK, 2 * BLOCK), 0) + BLOCK
    k_loc = lax.broadcasted_iota(jnp.int32, (BLOCK, 2 * BLOCK), 1)
    diff = q_loc - k_loc
    first_key = jnp.where(i == 0, BLOCK, 0)
    return (diff >= 0) & (diff < BLOCK) & (k_loc >= first_key)


def _softmax_with_sink(s, sink):
    m = jnp.maximum(jnp.max(s, axis=-1, keepdims=True), sink)
    p = jnp.exp(s - m)
    e_sink = jnp.exp(sink - m)
    den = jnp.sum(p, axis=-1, keepdims=True) + e_sink
    inv = 1.0 / den
    return p * inv, e_sink * inv


def _attention_fwd(proj, qw_row, kw_row, sinks, *, attn_w, kv_w, name):
    rows = proj.shape[0]
    n_q = attn_w // HEAD_DIM
    per_kv = n_q // N_KV_HEADS
    scale = 1.0 / math.sqrt(HEAD_DIM)

    def body(q_ref, kp_ref, kc_ref, vp_ref, vc_ref, glo_ref, ghi_ref, qw_ref, kw_ref, sink_ref, o_ref):
        i = pl.program_id(0)
        valid = _band_mask(i)
        q = q_ref[...]
        kk = jnp.concatenate([kp_ref[...], kc_ref[...]], axis=0)
        vv = jnp.concatenate([vp_ref[...], vc_ref[...]], axis=0)
        gate = jnp.concatenate([glo_ref[...], ghi_ref[...]], axis=1)
        heads = []
        for g in range(N_KV_HEADS):
            sl = slice(g * HEAD_DIM, (g + 1) * HEAD_DIM)
            _, kh, _ = _head_norm(kk[:, sl], kw_ref[...])
            kh = kh.astype(BF16)
            vh = vv[:, sl].astype(BF16)
            for r in range(per_kv):
                h = g * per_kv + r
                hs = slice(h * HEAD_DIM, (h + 1) * HEAD_DIM)
                _, qh, _ = _head_norm(q[:, hs], qw_ref[...])
                s = _dot(qh.astype(BF16), kh, NT) * scale
                s = jnp.where(valid, s, -1e30)
                p, _ = _softmax_with_sink(s, sink_ref[h])
                heads.append(_dot(p.astype(BF16), vh, NN))
        attn = jnp.concatenate(heads, axis=1)
        o_ref[...] = (attn * _silu(gate)).astype(BF16)

    vec = pl.BlockSpec((1, HEAD_DIM), lambda i: (0, 0))
    return pl.pallas_call(
        body, name=name, out_shape=jax.ShapeDtypeStruct((rows, attn_w), BF16), grid=(rows // BLOCK,),
        in_specs=_attn_specs(attn_w, kv_w) + [vec, vec, pl.BlockSpec(memory_space=pltpu.SMEM)],
        out_specs=pl.BlockSpec((BLOCK, attn_w), lambda i: (i, 0)),
        compiler_params=_params(("parallel",)),
    )(proj, proj, proj, proj, proj, proj, proj, qw_row, kw_row, sinks)


def _attention_bwd(proj, d_ag, dproj, qw_row, kw_row, sinks, *, attn_w, kv_w, name):
    rows = proj.shape[0]
    nb = rows // BLOCK
    n_q = attn_w // HEAD_DIM
    per_kv = n_q // N_KV_HEADS
    scale = 1.0 / math.sqrt(HEAD_DIM)
    w_out = 2 * attn_w + 2 * kv_w

    def body(q_ref, kp_ref, kc_ref, vp_ref, vc_ref, glo_ref, ghi_ref, dag_ref, qw_ref, kw_ref, sink_ref, _,
             dp_ref, dkv_ref, gqw_ref, gkw_ref, gs_ref):
        i = pl.program_id(0)
        valid = _band_mask(i)
        q = q_ref[...]
        kk = jnp.concatenate([kp_ref[...], kc_ref[...]], axis=0)
        vv = jnp.concatenate([vp_ref[...], vc_ref[...]], axis=0)
        gate = jnp.concatenate([glo_ref[...], ghi_ref[...]], axis=1)
        d_ag_v = dag_ref[...]
        qw, kw = qw_ref[...], kw_ref[...]
        lane = lax.broadcasted_iota(jnp.int32, (SUBLANES, LANES), 1)
        sub = lax.broadcasted_iota(jnp.int32, (SUBLANES, LANES), 0)
        gqw = jnp.zeros((1, HEAD_DIM), F32)
        gkw = jnp.zeros((1, HEAD_DIM), F32)
        gsink = jnp.zeros((SUBLANES, LANES), F32)
        dq_heads, dgate_heads, dk_heads, dv_heads = [], [], [], []
        for g in range(N_KV_HEADS):
            sl = slice(g * HEAD_DIM, (g + 1) * HEAD_DIM)
            kn, kh, k_rstd = _head_norm(kk[:, sl], kw)
            kh = kh.astype(BF16)
            vh = vv[:, sl].astype(BF16)
            dkh = jnp.zeros((2 * BLOCK, HEAD_DIM), F32)
            dvh = jnp.zeros((2 * BLOCK, HEAD_DIM), F32)
            for r in range(per_kv):
                h = g * per_kv + r
                hs = slice(h * HEAD_DIM, (h + 1) * HEAD_DIM)
                qn, qh, q_rstd = _head_norm(q[:, hs], qw)
                qh = qh.astype(BF16)
                s = _dot(qh, kh, NT) * scale
                s = jnp.where(valid, s, -1e30)
                p, p_sink = _softmax_with_sink(s, sink_ref[h])
                pb = p.astype(BF16)
                o = _dot(pb, vh, NN)
                gate_h = gate[:, hs]
                d_ag_h = d_ag_v[:, hs]
                dgate_heads.append(d_ag_h * o * _dsilu(gate_h))
                do = (d_ag_h * _silu(gate_h)).astype(BF16)
                dp = _dot(do, vh, NT)
                delta = jnp.sum(p * dp, axis=-1, keepdims=True)
                ds = (p * (dp - delta) * scale).astype(BF16)
                gs_h = jnp.sum(-p_sink * delta, axis=0, keepdims=True)
                gsink = gsink + jnp.where((lane == h) & (sub == 0), gs_h, 0.0)
                dvh = dvh + _dot(pb, do, TN)
                dkh = dkh + _dot(ds, qh, TN)
                dqh = _dot(ds, kh, NN)
                gqw = gqw + jnp.sum(dqh * qn, axis=0, keepdims=True)
                dqn = dqh * qw
                dq_heads.append(q_rstd * (dqn - qn * jnp.mean(dqn * qn, axis=-1, keepdims=True)))
            gkw = gkw + jnp.sum(dkh * kn, axis=0, keepdims=True)
            dkn = dkh * kw
            dk_heads.append(k_rstd * (dkn - kn * jnp.mean(dkn * kn, axis=-1, keepdims=True)))
            dv_heads.append(dvh)
        dp_ref[:, 0:attn_w] = jnp.concatenate(dq_heads, axis=1).astype(BF16)
        dp_ref[:, attn_w:attn_w + 2 * kv_w] = jnp.zeros((BLOCK, 2 * kv_w), BF16)
        dp_ref[:, attn_w + 2 * kv_w:w_out] = jnp.concatenate(dgate_heads, axis=1).astype(BF16)
        dkv_ref[0] = jnp.concatenate(dk_heads + dv_heads, axis=1)

        @pl.when(i == 0)
        def _():
            gqw_ref[...] = gqw
            gkw_ref[...] = gkw
            gs_ref[...] = gsink

        @pl.when(i > 0)
        def _():
            gqw_ref[...] += gqw
            gkw_ref[...] += gkw
            gs_ref[...] += gsink

    vec = pl.BlockSpec((1, HEAD_DIM), lambda i: (0, 0))
    return pl.pallas_call(
        body, name=name,
        out_shape=[jax.ShapeDtypeStruct(dproj.shape, BF16),
                   jax.ShapeDtypeStruct((nb, 2 * BLOCK, 2 * kv_w), F32),
                   jax.ShapeDtypeStruct((1, HEAD_DIM), F32), jax.ShapeDtypeStruct((1, HEAD_DIM), F32),
                   jax.ShapeDtypeStruct((SUBLANES, LANES), F32)],
        grid=(nb,),
        in_specs=_attn_specs(attn_w, kv_w) + [pl.BlockSpec((BLOCK, attn_w), lambda i: (i, 0)), vec, vec,
                                              pl.BlockSpec(memory_space=pltpu.SMEM), _any_spec()],
        out_specs=[pl.BlockSpec((BLOCK, w_out), lambda i: (i, 0)),
                   pl.BlockSpec((1, 2 * BLOCK, 2 * kv_w), lambda i: (i, 0, 0)),
                   vec, vec, pl.BlockSpec((SUBLANES, LANES), lambda i: (0, 0))],
        input_output_aliases={11: 0},
        compiler_params=_params(("arbitrary",), 40 * 1024 * 1024),
    )(proj, proj, proj, proj, proj, proj, proj, d_ag, qw_row, kw_row, sinks, dproj)


def _attention_dkv(dproj, dkv, *, attn_w, kv_w, name):
    rows = dproj.shape[0]
    nb = rows // BLOCK
    col = attn_w // (2 * kv_w)

    def body(cur_ref, nxt_ref, _, o_ref):
        i = pl.program_id(0)
        nxt = jnp.where(i < nb - 1, nxt_ref[0, 0:BLOCK, :], 0.0)
        o_ref[...] = (cur_ref[0, BLOCK:2 * BLOCK, :] + nxt).astype(BF16)

    blk = lambda f: pl.BlockSpec((1, 2 * BLOCK, 2 * kv_w), f)
    return pl.pallas_call(
        body, name=name, out_shape=jax.ShapeDtypeStruct(dproj.shape, BF16), grid=(nb,),
        in_specs=[blk(lambda i: (i, 0, 0)), blk(lambda i: (jnp.minimum(i + 1, nb - 1), 0, 0)), _any_spec()],
        out_specs=pl.BlockSpec((BLOCK, 2 * kv_w), lambda i: (i, col)),
        input_output_aliases={2: 0},
        compiler_params=_params(("parallel",)),
    )(dkv, dkv, dproj)


def _cmul(ar, ai, br, bi):
    return ar * br - ai * bi, ar * bi + ai * br


def _ssm_prep(a_re, a_im, log_dt_col, steps, name):
    assert steps & (steps - 1) == 0

    def body(are_ref, aim_ref, ldt_ref, abr_ref, abi_ref, cfr_ref, cfi_ref, apr_ref, api_ref):
        are, aim = are_ref[...], aim_ref[...]
        dt = jnp.exp(ldt_ref[...])
        mag = jnp.exp(dt * are)
        abr = mag * jnp.cos(dt * aim)
        abi = mag * jnp.sin(dt * aim)
        num_re, num_im = abr - 1.0, abi
        den = are * are + aim * aim
        abr_ref[...] = abr
        abi_ref[...] = abi
        cfr_ref[...] = (num_re * are + num_im * aim) / den
        cfi_ref[...] = (num_im * are - num_re * aim) / den
        pr, pi = abr, abi
        n = steps
        while n > 1:
            pr, pi = _cmul(pr, pi, pr, pi)
            n //= 2
        apr_ref[...] = pr
        api_ref[...] = pi

    shp = jax.ShapeDtypeStruct(a_re.shape, F32)
    return pl.pallas_call(body, name=name, out_shape=[shp] * 6)(a_re, a_im, log_dt_col)


def _ssm_param_bwd(a_re, a_im, log_dt_col, d_ab_re, d_ab_im, d_cf_re, d_cf_im, name):
    def body(are_ref, aim_ref, ldt_ref, gabr_ref, gabi_ref, gcfr_ref, gcfi_ref, dar_ref, dai_ref, dldt_ref):
        are, aim = are_ref[...], aim_ref[...]
        dt = jnp.exp(ldt_ref[...])
        mag = jnp.exp(dt * are)
        abr = mag * jnp.cos(dt * aim)
        abi = mag * jnp.sin(dt * aim)
        den = are * are + aim * aim
        cfr = ((abr - 1.0) * are + abi * aim) / den
        cfi = (abi * are - (abr - 1.0) * aim) / den
        gabr, gabi = jnp.sum(gabr_ref[...], axis=0), jnp.sum(gabi_ref[...], axis=0)
        gcfr, gcfi = jnp.sum(gcfr_ref[...], axis=0), jnp.sum(gcfi_ref[...], axis=0)
        inv_r, inv_i = are / den, -aim / den
        t_r, t_i = _cmul(inv_r, -inv_i, gcfr, gcfi)
        gabr, gabi = gabr + t_r, gabi + t_i
        q_r, q_i = _cmul(cfr, cfi, inv_r, inv_i)
        da_r, da_i = _cmul(-q_r, q_i, gcfr, gcfi)
        gz_r, gz_i = _cmul(abr, -abi, gabr, gabi)
        dar_ref[...] = da_r + dt * gz_r
        dai_ref[...] = da_i + dt * gz_i
        dldt_ref[...] = dt * jnp.sum(are * gz_r + aim * gz_i, axis=-1, keepdims=True)

    shp = jax.ShapeDtypeStruct(a_re.shape, F32)
    return pl.pallas_call(body, name=name, out_shape=[shp, shp, jax.ShapeDtypeStruct(log_dt_col.shape, F32)])(
        a_re, a_im, log_dt_col, d_ab_re, d_ab_im, d_cf_re, d_cf_im)


SCAN_LANES = 512


def _scan_segments(xr_ref, xi_ref, a_re, a_im, ap_re, ap_im, carry_re, carry_im, cm_re, cm_im, steps, reverse):
    n = xr_ref.shape[1]
    order = range(steps - 1, -1, -1) if reverse else range(steps)
    seg_order = range(SUBLANES - 1, -1, -1) if reverse else range(SUBLANES)
    for c0 in range(0, n, SCAN_LANES):
        ls = slice(c0, c0 + SCAN_LANES)
        ar = jnp.broadcast_to(a_re[:, ls], (SUBLANES, SCAN_LANES))
        ai = jnp.broadcast_to(a_im[:, ls], (SUBLANES, SCAN_LANES))

        def local(t, s, ar=ar, ai=ai, ls=ls):
            j = steps - 1 - t if reverse else t
            r0 = pl.multiple_of(j * SUBLANES, SUBLANES)
            sr, si = _cmul(ar, ai, s[0], s[1])
            sr = sr + xr_ref[pl.ds(r0, SUBLANES), ls]
            si = si + xi_ref[pl.ds(r0, SUBLANES), ls]
            xr_ref[pl.ds(r0, SUBLANES), ls] = sr
            xi_ref[pl.ds(r0, SUBLANES), ls] = si
            return sr, si

        zero = jnp.zeros((SUBLANES, SCAN_LANES), F32)
        end_r, end_i = lax.fori_loop(0, steps, local, (zero, zero))
        cr, ci = carry_re[:, ls], carry_im[:, ls]
        apr, api = ap_re[:, ls], ap_im[:, ls]
        for r in seg_order:
            cm_re[r:r + 1, ls] = cr
            cm_im[r:r + 1, ls] = ci
            tr, ti = _cmul(apr, api, cr, ci)
            cr, ci = end_r[r:r + 1, :] + tr, end_i[r:r + 1, :] + ti
        carry_re[:, ls] = cr
        carry_im[:, ls] = ci

        def fix(t, s, ar=ar, ai=ai, ls=ls):
            j = steps - 1 - t if reverse else t
            r0 = pl.multiple_of(j * SUBLANES, SUBLANES)
            sr, si = _cmul(ar, ai, s[0], s[1])
            xr_ref[pl.ds(r0, SUBLANES), ls] += sr
            xi_ref[pl.ds(r0, SUBLANES), ls] += si
            return sr, si

        lax.fori_loop(0, steps, fix, (cm_re[:, ls], cm_im[:, ls]))
    del order


def _ssm_blocks(b_re, b_im, c_re, c_im):
    g = b_re.shape[0]
    per = MXU_DIM // GROUP
    nsb = g // per
    eye = jnp.eye(per, dtype=F32)

    def b_blocks(b):
        bt = b.reshape(nsb, per, STATE, GROUP).transpose(0, 1, 3, 2)
        return (bt[:, :, :, None, :] * eye[None, :, None, :, None]).reshape(nsb, per * GROUP, per * STATE).astype(BF16)

    def c_blocks(cm):
        ct = cm.reshape(nsb, per, GROUP, STATE).transpose(0, 1, 3, 2)
        return (ct[:, :, :, None, :] * eye[None, :, None, :, None]).reshape(nsb, per * STATE, per * GROUP).astype(BF16)

    return b_blocks(b_re), b_blocks(b_im), c_blocks(c_re), c_blocks(c_im)


def _unblock_b(db):
    nsb = db.shape[0]
    per = MXU_DIM // GROUP
    d = db.reshape(nsb, per, GROUP, per, STATE)
    d = jnp.stack([d[:, k, :, k, :] for k in range(per)], axis=1)
    return d.transpose(0, 1, 3, 2).reshape(nsb * per, STATE, GROUP)


def _unblock_c(dc):
    nsb = dc.shape[0]
    per = MXU_DIM // GROUP
    d = dc.reshape(nsb, per, STATE, per, GROUP)
    d = jnp.stack([d[:, k, :, k, :] for k in range(per)], axis=1)
    return d.transpose(0, 1, 3, 2).reshape(nsb * per, GROUP, STATE)


def _to_segments(v, chunk):
    rows, w = v.shape
    return v.reshape(rows // chunk, SUBLANES, chunk // SUBLANES, w).transpose(0, 2, 1, 3).reshape(rows, w)


def _from_segments(v, chunk):
    rows, w = v.shape
    return v.reshape(rows // chunk, chunk // SUBLANES, SUBLANES, w).transpose(0, 2, 1, 3).reshape(rows, w)


def _ssm_fwd(u, blocks, rows_p, d_row, *, chunk, name):
    rows, w = u.shape
    nc = rows // chunk
    steps = chunk // SUBLANES
    bre, bim, cre, cim = blocks
    nsb = bre.shape[0]
    n_state = nsb * bre.shape[2]
    sbw, sbs = bre.shape[1], bre.shape[2]

    def body(u_ref, bre_hbm, bim_hbm, cre_hbm, cim_hbm, abr_ref, abi_ref, cfr_ref, cfi_ref, apr_ref, api_ref,
             d_ref, y_ref, str_ref, sti_ref, bre_ref, bim_ref, cre_ref, cim_ref, sr, si, carry_r, carry_i,
             cm_r, cm_i):
        @pl.when(pl.program_id(0) == 0)
        def _():
            for src, dst in ((bre_hbm, bre_ref), (bim_hbm, bim_ref), (cre_hbm, cre_ref), (cim_hbm, cim_ref)):
                pltpu.sync_copy(src, dst)
            carry_r[...] = jnp.zeros_like(carry_r)
            carry_i[...] = jnp.zeros_like(carry_i)

        str_ref[0] = carry_r[...]
        sti_ref[0] = carry_i[...]
        for sb in range(nsb):
            us = slice(sb * sbw, (sb + 1) * sbw)
            ss = slice(sb * sbs, (sb + 1) * sbs)
            ub = u_ref[:, us].astype(BF16)
            bur = _dot(ub, bre_ref[sb], NN)
            bui = _dot(ub, bim_ref[sb], NN)
            xr, xi = _cmul(cfr_ref[:, ss], cfi_ref[:, ss], bur, bui)
            sr[:, ss] = xr
            si[:, ss] = xi
        _scan_segments(sr, si, abr_ref[...], abi_ref[...], apr_ref[...], api_ref[...],
                       carry_r, carry_i, cm_r, cm_i, steps, False)
        for sb in range(nsb):
            us = slice(sb * sbw, (sb + 1) * sbw)
            ss = slice(sb * sbs, (sb + 1) * sbs)
            y = _dot(sr[:, ss].astype(BF16), cre_ref[sb], NN) - _dot(si[:, ss].astype(BF16), cim_ref[sb], NN)
            y_ref[:, us] = y + d_ref[:, us] * u_ref[:, us]

    row_n = pl.BlockSpec((1, n_state), lambda c: (0, 0))
    st = pl.BlockSpec((1, 1, n_state), lambda c: (c, 0, 0))
    held = [pltpu.VMEM(b.shape, BF16) for b in blocks]
    vmem = 2 * sum(_nbytes(b.shape, BF16) for b in blocks) + 3 * _nbytes((chunk, n_state), F32)
    return pl.pallas_call(
        body, name=name,
        out_shape=[jax.ShapeDtypeStruct((rows, w), F32), jax.ShapeDtypeStruct((nc, 1, n_state), F32),
                   jax.ShapeDtypeStruct((nc, 1, n_state), F32)],
        grid=(nc,),
        in_specs=[pl.BlockSpec((chunk, w), lambda c: (c, 0))] + [_any_spec()] * 4
        + [row_n] * 6 + [pl.BlockSpec((1, w), lambda c: (0, 0))],
        out_specs=[pl.BlockSpec((chunk, w), lambda c: (c, 0)), st, st],
        scratch_shapes=held + [pltpu.VMEM((chunk, n_state), F32), pltpu.VMEM((chunk, n_state), F32),
                               pltpu.VMEM((1, n_state), F32), pltpu.VMEM((1, n_state), F32),
                               pltpu.VMEM((SUBLANES, n_state), F32), pltpu.VMEM((SUBLANES, n_state), F32)],
        compiler_params=_params(("arbitrary",), vmem),
    )(u, bre, bim, cre, cim, *rows_p, d_row)


def _ssm_bwd(u, y, dyg, st_re, st_im, blocks, rows_p, d_row, *, chunk, name):
    rows, w = u.shape
    nc = rows // chunk
    steps = chunk // SUBLANES
    bre, bim, cre, cim = blocks
    nsb = bre.shape[0]
    sbw, sbs = bre.shape[1], bre.shape[2]
    n_state = nsb * sbs

    def body(u_ref, y_ref, dyg_ref, str_ref, sti_ref, bre_hbm, bim_hbm, cre_hbm, cim_hbm,
             abr_ref, abi_ref, cfr_ref, cfi_ref, apr_ref, api_ref, d_ref,
             du_ref, dbre_hbm, dbim_hbm, dcre_hbm, dcim_hbm, gabr_ref, gabi_ref, gcfr_ref, gcfi_ref, dd_ref,
             bre_ref, bim_ref, cre_ref, cim_ref, dbre_ref, dbim_ref, dcre_ref, dcim_ref,
             bur, bui, sr, si, lr, li, carry_r, carry_i, lam_r, lam_i, cm_r, cm_i, cl_r, cl_i):
        first = pl.program_id(0) == 0

        @pl.when(first)
        def _():
            for src, dst in ((bre_hbm, bre_ref), (bim_hbm, bim_ref), (cre_hbm, cre_ref), (cim_hbm, cim_ref)):
                pltpu.sync_copy(src, dst)
            lam_r[...] = jnp.zeros_like(lam_r)
            lam_i[...] = jnp.zeros_like(lam_i)
            for ref in (dbre_ref, dbim_ref, dcre_ref, dcim_ref, gabr_ref, gabi_ref, gcfr_ref, gcfi_ref, dd_ref):
                ref[...] = jnp.zeros_like(ref)

        uv = u_ref[...]
        dy = dyg_ref[...] * _dgelu(y_ref[...])
        dd_ref[...] += jnp.sum(dy * uv, axis=0, keepdims=True)
        dyb = dy.astype(BF16)
        ub = uv.astype(BF16)
        carry_r[...] = str_ref[0]
        carry_i[...] = sti_ref[0]
        for sb in range(nsb):
            us = slice(sb * sbw, (sb + 1) * sbw)
            ss = slice(sb * sbs, (sb + 1) * sbs)
            br = _dot(ub[:, us], bre_ref[sb], NN)
            bi = _dot(ub[:, us], bim_ref[sb], NN)
            bur[:, ss] = br
            bui[:, ss] = bi
            xr, xi = _cmul(cfr_ref[:, ss], cfi_ref[:, ss], br, bi)
            sr[:, ss] = xr
            si[:, ss] = xi
            lr[:, ss] = _dot(dyb[:, us], cre_ref[sb], NT)
            li[:, ss] = -_dot(dyb[:, us], cim_ref[sb], NT)
        abr, abi = abr_ref[...], abi_ref[...]
        apr, api = apr_ref[...], api_ref[...]
        _scan_segments(sr, si, abr, abi, apr, api, carry_r, carry_i, cm_r, cm_i, steps, False)
        for sb in range(nsb):
            us = slice(sb * sbw, (sb + 1) * sbw)
            ss = slice(sb * sbs, (sb + 1) * sbs)
            dcre_ref[sb] += _dot(sr[:, ss].astype(BF16), dyb[:, us], TN)
            dcim_ref[sb] -= _dot(si[:, ss].astype(BF16), dyb[:, us], TN)
        _scan_segments(lr, li, abr, -abi, apr, -api, lam_r, lam_i, cl_r, cl_i, steps, True)
        for c0 in range(0, n_state, SCAN_LANES):
            ls = slice(c0, c0 + SCAN_LANES)
            cfr = jnp.broadcast_to(cfr_ref[:, ls], (SUBLANES, SCAN_LANES))
            cfi = jnp.broadcast_to(cfi_ref[:, ls], (SUBLANES, SCAN_LANES))

            def step(j, acc, ls=ls, cfr=cfr, cfi=cfi):
                gar, gai, gcr, gci, pr, pi = acc
                r0 = pl.multiple_of(j * SUBLANES, SUBLANES)
                rws = pl.ds(r0, SUBLANES)
                l_r, l_i = lr[rws, ls], li[rws, ls]
                t_r, t_i = _cmul(pr, -pi, l_r, l_i)
                b_r, b_i = bur[rws, ls], bui[rws, ls]
                c_r, c_i = _cmul(b_r, -b_i, l_r, l_i)
                x_r, x_i = _cmul(cfr, -cfi, l_r, l_i)
                bur[rws, ls] = x_r
                bui[rws, ls] = x_i
                return gar + t_r, gai + t_i, gcr + c_r, gci + c_i, sr[rws, ls], si[rws, ls]

            zero = jnp.zeros((SUBLANES, SCAN_LANES), F32)
            gar, gai, gcr, gci, _, _ = lax.fori_loop(
                0, steps, step, (zero, zero, zero, zero, cm_r[:, ls], cm_i[:, ls]))
            gabr_ref[:, ls] += gar
            gabi_ref[:, ls] += gai
            gcfr_ref[:, ls] += gcr
            gcfi_ref[:, ls] += gci
        for sb in range(nsb):
            us = slice(sb * sbw, (sb + 1) * sbw)
            ss = slice(sb * sbs, (sb + 1) * sbs)
            xr, xi = bur[:, ss].astype(BF16), bui[:, ss].astype(BF16)
            du = _dot(xr, bre_ref[sb], NT) + _dot(xi, bim_ref[sb], NT)
            du_ref[:, us] = du + d_ref[:, us] * dy[:, us]
            dbre_ref[sb] += _dot(ub[:, us], xr, TN)
            dbim_ref[sb] += _dot(ub[:, us], xi, TN)

        @pl.when(pl.program_id(0) == nc - 1)
        def _():
            for src, dst in ((dbre_ref, dbre_hbm), (dbim_ref, dbim_hbm), (dcre_ref, dcre_hbm), (dcim_ref, dcim_hbm)):
                pltpu.sync_copy(src, dst)

    rev = lambda c: nc - 1 - c
    tile = pl.BlockSpec((chunk, w), lambda c: (rev(c), 0))
    row_n = pl.BlockSpec((1, n_state), lambda c: (0, 0))
    row_w = pl.BlockSpec((1, w), lambda c: (0, 0))
    st = pl.BlockSpec((1, 1, n_state), lambda c: (rev(c), 0, 0))
    acc8 = pl.BlockSpec((SUBLANES, n_state), lambda c: (0, 0))
    big = pltpu.VMEM((chunk, n_state), F32)
    row = pltpu.VMEM((1, n_state), F32)
    eight = pltpu.VMEM((SUBLANES, n_state), F32)
    f32 = lambda a: jax.ShapeDtypeStruct(a.shape, F32)
    held = [pltpu.VMEM(b.shape, BF16) for b in blocks] + [pltpu.VMEM(b.shape, F32) for b in blocks]
    vmem = (sum(_nbytes(b.shape, BF16) + _nbytes(b.shape, F32) for b in blocks)
            + 7 * _nbytes((chunk, n_state), F32) + 16 * _nbytes((chunk, w), F32))
    return pl.pallas_call(
        body, name=name,
        out_shape=[jax.ShapeDtypeStruct((rows, w), F32), f32(bre), f32(bim), f32(cre), f32(cim)]
        + [jax.ShapeDtypeStruct((SUBLANES, n_state), F32)] * 4 + [jax.ShapeDtypeStruct((1, w), F32)],
        grid=(nc,),
        in_specs=[tile, tile, tile, st, st] + [_any_spec()] * 4 + [row_n] * 6 + [row_w],
        out_specs=[tile] + [_any_spec()] * 4 + [acc8] * 4 + [row_w],
        scratch_shapes=held + [big] * 6 + [row] * 4 + [eight] * 4,
        compiler_params=_params(("arbitrary",), vmem),
    )(u, y, dyg, st_re, st_im, bre, bim, cre, cim, *rows_p, d_row)


def _loss_grad(x, mm, target, name):
    rows, d = x.shape

    def fn(xv, mv, tv):
        err = xv + mv - tv
        g = err * (1.0 / d)
        return g, g, 0.5 * err * g

    return _ew(fn, name=name, rows=rows, width=d, tiles=[(x, 0), (mm, 0), (target, 0)],
               outs=[(F32, d, 0), (BF16, d, 0)], accs=1)


def _pair_sum(grad, recv, name):
    r4, cdim = recv.shape
    r = r4 // N_CHIPS
    tr = _tile(r, 256, 16)
    g4 = grad.reshape(N_CHIPS, 2, r, cdim)
    r3 = recv.reshape(N_CHIPS, r, cdim)
    core = jnp.reshape(lax.axis_index("c"), (1,)).astype(jnp.int32)

    def body(c_ref, g_ref, r_ref, o_ref):
        o_ref[...] = (g_ref[0] + r_ref[...]).astype(BF16)

    out = pl.pallas_call(
        body, name=name, out_shape=jax.ShapeDtypeStruct((N_CHIPS, r, cdim), BF16),
        grid_spec=pltpu.PrefetchScalarGridSpec(
            num_scalar_prefetch=1, grid=(N_CHIPS, r // tr),
            in_specs=[pl.BlockSpec((1, 1, tr, cdim), lambda j, i, c: (j, c[0], i, 0)),
                      pl.BlockSpec((1, tr, cdim), lambda j, i, c: (j, i, 0))],
            out_specs=pl.BlockSpec((1, tr, cdim), lambda j, i, c: (j, i, 0))),
        compiler_params=_params(("parallel", "parallel")),
    )(core, g4, r3)
    return out.reshape(r4, cdim)


def _chip_sum(recv, name):
    r4, cdim = recv.shape
    r = r4 // N_CHIPS
    tr = _tile(r, 256, 16)
    r3 = recv.reshape(N_CHIPS, r, cdim)

    def body(r_ref, o_ref):
        acc = r_ref[0].astype(F32)
        for j in range(1, N_CHIPS):
            acc = acc + r_ref[j].astype(F32)
        o_ref[...] = acc

    return pl.pallas_call(
        body, name=name, out_shape=jax.ShapeDtypeStruct((r, cdim), F32), grid=(r // tr,),
        in_specs=[pl.BlockSpec((N_CHIPS, tr, cdim), lambda i: (0, i, 0))],
        out_specs=pl.BlockSpec((tr, cdim), lambda i: (i, 0)),
        compiler_params=_params(("parallel",)),
    )(r3)


def _adamw_math(w, g, m, v):
    m = ADAM_B1 * m + (1.0 - ADAM_B1) * g
    v = ADAM_B2 * v + (1.0 - ADAM_B2) * (g * g)
    m_hat = m / (1.0 - ADAM_B1 ** ADAM_STEP)
    v_hat = v / (1.0 - ADAM_B2 ** ADAM_STEP)
    delta = -ADAM_LR * (m_hat / (jnp.sqrt(v_hat) + ADAM_EPS) + ADAM_WD * w)
    return delta, m, v


def _adamw(w, g, m, v, name):
    rows, cols = w.shape
    tr = _tile(rows, 256, SUBLANES)

    def body(w_ref, g_ref, m_ref, v_ref, d_ref, nm_ref, nv_ref):
        d, nm, nv = _adamw_math(w_ref[...], g_ref[...], m_ref[...], v_ref[...])
        d_ref[...] = d
        nm_ref[...] = nm
        nv_ref[...] = nv

    spec = pl.BlockSpec((tr, cols), lambda i: (i, 0))
    shp = jax.ShapeDtypeStruct((rows, cols), F32)
    return pl.pallas_call(
        body, name=name, out_shape=[shp] * 3, grid=(rows // tr,), in_specs=[spec] * 4, out_specs=[spec] * 3,
        compiler_params=_params(("parallel",)),
    )(w, g, m, v)


def _adamw_small(w, parts, m, v, name):
    rows, cols = w.shape
    p3 = parts.reshape(N_DEV, rows, cols)

    def body(w_ref, p_ref, m_ref, v_ref, g_ref, d_ref, nm_ref, nv_ref):
        g = p_ref[0]
        for k in range(1, N_DEV):
            g = g + p_ref[k]
        d, nm, nv = _adamw_math(w_ref[...], g, m_ref[...], v_ref[...])
        g_ref[...] = g
        d_ref[...] = d
        nm_ref[...] = nm
        nv_ref[...] = nv

    shp = jax.ShapeDtypeStruct((rows, cols), F32)
    return pl.pallas_call(body, name=name, out_shape=[shp] * 4)(w, p3, m, v)


SMALL = ("norm_w", "q_norm_w", "k_norm_w", "sinks", "A_re", "A_im", "log_dt", "B_re", "B_im", "C_re", "C_im",
         "D_skip", "b_glu")
LARGE = ("w_in", "w_attn_proj", "w_glu", "w_ssm_proj", "w_out")
ORDER = ("norm_w", "w_in", "q_norm_w", "k_norm_w", "sinks", "w_attn_proj", "A_re", "A_im", "log_dt", "B_re", "B_im",
         "C_re", "C_im", "D_skip", "w_glu", "b_glu", "w_ssm_proj", "w_out")


def _pack(named):
    flat = jnp.concatenate([named[k].reshape(-1).astype(F32) for k in SMALL])
    n = flat.shape[0]
    rows = -(-n // (LANES * SUBLANES)) * SUBLANES
    return jnp.pad(flat, (0, rows * LANES - n)).reshape(rows, LANES)


def _unpack(packed, like):
    flat = packed.reshape(-1)
    out, o = {}, 0
    for k in SMALL:
        n = like[k].size
        out[k] = flat[o:o + n].reshape(like[k].shape)
        o += n
    return out


def _local_step(xs, target, p, full):
    w_in_t, w_ap_t, w_glu_t, w_sp_t, w_o = full
    seq, d = xs.shape
    attn_w = (d // 128) * HEAD_DIM
    n_q = attn_w // HEAD_DIM
    kv_w = N_KV_HEADS * HEAD_DIM
    ssm_w = d // 2
    n_groups = ssm_w // GROUP
    n_state = n_groups * STATE
    in_w = w_in_t.shape[0]
    assert in_w == 2 * attn_w + 2 * kv_w + 2 * ssm_w + 2 * d
    o_u = 2 * attn_w + 2 * kv_w
    o_z = o_u + ssm_w
    o_ga = o_z + ssm_w
    chunk = min(BLOCK, seq)
    cw = d // 4

    norm_row = p["norm_w"].reshape(1, d)
    h = _rmsnorm_fwd(xs, norm_row, "rmsnorm_fwd")
    proj = _matmul(h, w_in_t, mode="nt", name="in_proj", tn=512)
    qw_row, kw_row = p["q_norm_w"].reshape(1, HEAD_DIM), p["k_norm_w"].reshape(1, HEAD_DIM)
    ag = _attention_fwd(proj, qw_row, kw_row, p["sinks"], attn_w=attn_w, kv_w=kv_w, name="attention_fwd")

    log_dt_col = p["log_dt"].reshape(n_groups, 1)
    prep = _ssm_prep(p["A_re"], p["A_im"], log_dt_col, chunk // SUBLANES, "ssm_prep")
    rows_p = [v.reshape(1, n_state) for v in prep]
    blocks = _ssm_blocks(p["B_re"], p["B_im"], p["C_re"], p["C_im"])
    d_row = p["D_skip"].reshape(1, ssm_w)
    u_seg = _to_segments(proj[:, o_u:o_u + ssm_w], chunk)
    y_seg, st_re, st_im = _ssm_fwd(u_seg, blocks, rows_p, d_row, chunk=chunk, name="ssm_fwd")
    y_ssm = _from_segments(y_seg, chunk)
    (yg,) = _ew(_gelu, name="gelu", rows=seq, width=ssm_w, tiles=[(y_ssm, 0)], outs=[(BF16, ssm_w, 0)], cw=cw)
    glu = _matmul(yg, w_glu_t, mode="nt", name="glu_proj", bias=p["b_glu"].reshape(1, 2 * ssm_w))
    (ts,) = _ew(lambda ga, gb, z: ga * _sigmoid(gb) * _silu(z), name="glu_gate", rows=seq, width=ssm_w,
                tiles=[(glu, 0), (glu, ssm_w), (proj, o_z)], outs=[(BF16, ssm_w, 0)], cw=cw)
    yy = _matmul(ag, w_ap_t, mode="nt", name="attn_proj", out_cols=(2 * d, 0))
    yy = _matmul(ts, w_sp_t, mode="nt", name="ssm_proj", out_cols=(2 * d, d), into=yy)
    (merged,) = _ew(lambda ya, ys, ga, gs: _sigmoid(ga) * ya + _sigmoid(gs) * ys, name="merge", rows=seq, width=d,
                    tiles=[(yy, 0), (yy, d), (proj, o_ga), (proj, o_ga + d)], outs=[(BF16, d, 0)], cw=cw)
    mm = _matmul(merged, w_o, mode="nn", name="out_proj")
    dout, dout_b, loss_cols = _loss_grad(xs, mm, target, "loss_grad")
    loss_local = jnp.sum(loss_cols)

    g_w_o = _matmul(merged, dout_b, mode="tn", name="grad_w_out", tk=1024)
    dmerged = _matmul(dout_b, w_o, mode="nt", name="d_merged")

    def merge_bwd(dm, y, g):
        s = _sigmoid(g)
        return dm * s, dm * y * s * (1.0 - s)

    dyy, dproj = _ew(merge_bwd, name="merge_bwd", rows=seq, width=2 * d,
                     tiles=[(dmerged, 0, d), (yy, 0), (proj, o_ga)],
                     outs=[(BF16, 2 * d, 0), (BF16, in_w, o_ga)], cw=cw)
    dy_a, dy_s = Cols(dyy, 0, d), Cols(dyy, d, d)
    g_w_ap_t = _matmul(dy_a, ag, mode="tn", name="grad_w_attn_proj", tk=1024)
    g_w_sp_t = _matmul(dy_s, ts, mode="tn", name="grad_w_ssm_proj", tk=1024)
    d_ag = _matmul(dy_a, w_ap_t, mode="nn", name="d_attn_gated")
    d_ts = _matmul(dy_s, w_sp_t, mode="nn", name="d_ssm_gated")

    dproj, dkv, g_qw, g_kw, g_sinks = _attention_bwd(proj, d_ag, dproj, qw_row, kw_row, p["sinks"], attn_w=attn_w,
                                                     kv_w=kv_w, name="attention_bwd")
    dproj = _attention_dkv(dproj, dkv, attn_w=attn_w, kv_w=kv_w, name="attention_dkv")

    n_half = ssm_w // _tile(2 * ssm_w, cw)

    def glu_bwd(j, dt, ga, gb, z):
        sb, sz = _sigmoid(gb), _silu(z)
        dg = jnp.where(j < n_half, dt * sb * sz, dt * ga * sb * (1.0 - sb) * sz)
        return dg, dg

    glu_ops = [(d_ts, 0, ssm_w), (glu, 0, ssm_w), (glu, ssm_w, ssm_w), (proj, o_z, ssm_w)]
    dglu, g_bglu = _ew(glu_bwd, name="glu_bwd", rows=seq, width=2 * ssm_w, tiles=glu_ops,
                       outs=[(BF16, 2 * ssm_w, 0)], accs=1, cw=cw, with_col=True)
    (dproj,) = _ew(lambda dt, ga, gb, z: dt * ga * _sigmoid(gb) * _dsilu(z), name="glu_bwd_z", rows=seq,
                   width=ssm_w, tiles=glu_ops, outs=[(BF16, in_w, o_z)], into=[dproj], cw=cw)
    g_w_glu_t = _matmul(dglu, yg, mode="tn", name="grad_w_glu", tk=1024)
    d_yg = _matmul(dglu, w_glu_t, mode="nn", name="d_gelu")
    (du_seg, db_re, db_im, dc_re, dc_im, gabr, gabi, gcfr, gcfi, g_d) = _ssm_bwd(
        u_seg, y_seg, _to_segments(d_yg, chunk), st_re, st_im, blocks, rows_p, d_row, chunk=chunk, name="ssm_bwd")
    (dproj,) = _ew(lambda v: v, name="du_store", rows=seq, width=ssm_w, tiles=[(_from_segments(du_seg, chunk), 0)],
                   outs=[(BF16, in_w, o_u)], into=[dproj], cw=cw)
    g_w_in_t = _matmul(dproj, h, mode="tn", name="grad_w_in", tk=1024)
    dh = _matmul(dproj, w_in_t, mode="nn", name="d_normed", tk=2176)
    grad_x, g_norm = _rmsnorm_bwd(xs, norm_row, dh, dout, "rmsnorm_bwd")
    g_a_re, g_a_im, g_log_dt = _ssm_param_bwd(
        p["A_re"], p["A_im"], log_dt_col, *[g.reshape(SUBLANES, n_groups, STATE) for g in (gabr, gabi, gcfr, gcfi)],
        "ssm_param_bwd")
    small_grads = dict(
        norm_w=g_norm.reshape(d), q_norm_w=g_qw.reshape(HEAD_DIM), k_norm_w=g_kw.reshape(HEAD_DIM),
        sinks=g_sinks[0, :n_q], A_re=g_a_re, A_im=g_a_im, log_dt=g_log_dt.reshape(n_groups),
        B_re=_unblock_b(db_re), B_im=_unblock_b(db_im), C_re=_unblock_c(dc_re), C_im=_unblock_c(dc_im),
        D_skip=g_d.reshape(n_groups, GROUP), b_glu=g_bglu.reshape(2 * ssm_w))
    return loss_local, grad_x, [g_w_in_t, g_w_ap_t, g_w_glu_t, g_w_sp_t, g_w_o], small_grads


def kernel(x, norm_w, w_in, q_norm_w, k_norm_w, sinks, w_attn_proj, A_re, A_im, log_dt, B_re, B_im, C_re, C_im, D_skip, w_glu, b_glu, w_ssm_proj, w_out, loss_target, m_norm_w, m_w_in, m_q_norm_w, m_k_norm_w, m_sinks, m_w_attn_proj, m_A_re, m_A_im, m_log_dt, m_B_re, m_B_im, m_C_re, m_C_im, m_D_skip, m_w_glu, m_b_glu, m_w_ssm_proj, m_w_out, v_norm_w, v_w_in, v_q_norm_w, v_k_norm_w, v_sinks, v_w_attn_proj, v_A_re, v_A_im, v_log_dt, v_B_re, v_B_im, v_C_re, v_C_im, v_D_skip, v_w_glu, v_b_glu, v_w_ssm_proj, v_w_out):
    weights = dict(norm_w=norm_w, w_in=w_in, q_norm_w=q_norm_w, k_norm_w=k_norm_w, sinks=sinks,
                   w_attn_proj=w_attn_proj, A_re=A_re, A_im=A_im, log_dt=log_dt, B_re=B_re, B_im=B_im, C_re=C_re,
                   C_im=C_im, D_skip=D_skip, w_glu=w_glu, b_glu=b_glu, w_ssm_proj=w_ssm_proj, w_out=w_out)
    m_in = dict(norm_w=m_norm_w, w_in=m_w_in, q_norm_w=m_q_norm_w, k_norm_w=m_k_norm_w, sinks=m_sinks,
                w_attn_proj=m_w_attn_proj, A_re=m_A_re, A_im=m_A_im, log_dt=m_log_dt, B_re=m_B_re, B_im=m_B_im,
                C_re=m_C_re, C_im=m_C_im, D_skip=m_D_skip, w_glu=m_w_glu, b_glu=m_b_glu, w_ssm_proj=m_w_ssm_proj,
                w_out=m_w_out)
    v_in = dict(norm_w=v_norm_w, w_in=v_w_in, q_norm_w=v_q_norm_w, k_norm_w=v_k_norm_w, sinks=v_sinks,
                w_attn_proj=v_w_attn_proj, A_re=v_A_re, A_im=v_A_im, log_dt=v_log_dt, B_re=v_B_re, B_im=v_B_im,
                C_re=v_C_re, C_im=v_C_im, D_skip=v_D_skip, w_glu=v_w_glu, b_glu=v_b_glu, w_ssm_proj=v_w_ssm_proj,
                w_out=v_w_out)

    _, seq, d = x.shape

    shards = [w_in.T.astype(BF16), w_attn_proj.T.astype(BF16), w_glu.T.astype(BF16), w_ssm_proj.T.astype(BF16),
              w_out.astype(BF16)]
    full = _all_gather(shards, "gather_weights")

    small = {k: weights[k] for k in SMALL}
    loss_local, grad_x, big_grads, small_grads = _local_step(x.reshape(seq, d), loss_target.reshape(seq, d),
                                                             small, full)
    loss = lax.psum(loss_local, ("x", "y", "c"))

    from_sibling = _sibling_exchange(big_grads, "grads_sibling_exchange")
    pair = [_pair_sum(g, r, "grads_pair_sum_%d" % i) for i, (g, r) in enumerate(zip(big_grads, from_sibling))]
    from_chips = _chip_exchange(pair, "grads_chip_exchange")
    summed = [_chip_sum(r, "grads_chip_sum_%d" % i) for i, r in enumerate(from_chips)]
    grads = dict(zip(LARGE, [summed[0].T, summed[1].T, summed[2].T, summed[3].T, summed[4]]))

    (parts,) = _all_gather([_pack(small_grads)], "gather_small_grads")
    packed = _adamw_small(_pack(small), parts, _pack({k: m_in[k] for k in SMALL}),
                          _pack({k: v_in[k] for k in SMALL}), "adamw_small")
    g_small, d_small, m_small, v_small = [_unpack(p, small) for p in packed]
    grads.update(g_small)

    delta, new_m, new_v = dict(d_small), dict(m_small), dict(v_small)
    for k in LARGE:
        delta[k], new_m[k], new_v[k] = _adamw(weights[k], grads[k], m_in[k], v_in[k], "adamw_" + k)

    return (loss, grad_x.reshape(x.shape), *[grads[k] for k in ORDER], *[delta[k] for k in ORDER],
            *[new_m[k] for k in ORDER], *[new_v[k] for k in ORDER])
```

```python
import math
from typing import Callable, NamedTuple

import jax
import jax.numpy as jnp
from jax import lax
from jax.experimental import pallas as pl
from jax.experimental.pallas import tpu as pltpu

F32 = jnp.float32
BF16 = jnp.bfloat16
MESH = pl.DeviceIdType.MESH

HEAD_DIM = 64
N_KV_HEADS = 4
GROUP = 16
STATE = 64
BLOCK = 128
NORM_EPS = 1e-6
N_DEV = 8
N_CHIPS = 4
LANES = 128
SUBLANES = 8
MXU_DIM = 256
VMEM_BYTES = 64 * 1024 * 1024
VMEM_CAP = VMEM_BYTES - 8 * 1024 * 1024

ADAM_LR = 0.001
ADAM_B1 = 0.9
ADAM_B2 = 0.999
ADAM_EPS = 1e-08
ADAM_WD = 0.01
ADAM_STEP = 10

GELU_C = math.sqrt(2.0 / math.pi)
GELU_K = 0.044715


def _tile(dim, pref, mult=LANES):
    if dim <= pref:
        return dim
    best = None
    for d in range(mult, pref + 1, mult):
        if dim % d == 0:
            best = d
    assert best is not None, (dim, pref, mult)
    return best


def _params(semantics=None, vmem=None):
    kw = {}
    if semantics is not None:
        kw["dimension_semantics"] = semantics
    if vmem is not None:
        kw["vmem_limit_bytes"] = int(min(VMEM_CAP, max(vmem, 32 * 1024 * 1024)))
    return pltpu.CompilerParams(**kw)


def _nbytes(shape, dtype):
    return math.prod(shape) * jnp.dtype(dtype).itemsize


def _sigmoid(x):
    return 1.0 / (1.0 + jnp.exp(-x))


def _silu(x):
    return x * _sigmoid(x)


def _dsilu(x):
    s = _sigmoid(x)
    return s * (1.0 + x * (1.0 - s))


def _gelu(x):
    return 0.5 * x * (1.0 + jnp.tanh(GELU_C * (x + GELU_K * x * x * x)))


def _dgelu(x):
    t = jnp.tanh(GELU_C * (x + GELU_K * x * x * x))
    return 0.5 * (1.0 + t) + 0.5 * x * (1.0 - t * t) * GELU_C * (1.0 + 3.0 * GELU_K * x * x)


def _dot(a, b, dims):
    return lax.dot_general(a, b, (dims, ((), ())), preferred_element_type=F32)


NN = ((1,), (0,))
NT = ((1,), (1,))
TN = ((0,), (0,))


def _any_spec():
    return pl.BlockSpec(memory_space=pl.ANY)


class Rider(NamedTuple):
    operands: tuple
    out_shapes: tuple
    sems: tuple
    start: Callable
    finish: Callable


def _all_gather(shards):
    n = len(shards)

    def copies(ins, outs, sems):
        send_sems, recv_sems, local_sems = sems
        x, y, c = lax.axis_index("x"), lax.axis_index("y"), lax.axis_index("c")
        me, sibling = (x, y, c), (x, y, 1 - c)
        chips = [(1 - x, y), (x, 1 - y), (1 - x, 1 - y)]

        def rows(k, px, py, pc):
            r = shards[k].shape[0]
            return outs[k].at[pl.ds((4 * px + 2 * py + pc) * r, r), :]

        def copy(k, s, block, to, src=None):
            return pltpu.make_async_remote_copy(
                src_ref=rows(k, *block) if src is None else src, dst_ref=rows(k, *block),
                send_sem=send_sems.at[7 * k + s], recv_sem=recv_sems.at[7 * k + s],
                device_id=to, device_id_type=MESH)

        mine = [pltpu.make_async_copy(ins[k], rows(k, *me), local_sems.at[k]) for k in range(n)]
        first = []
        for k in range(n):
            first.append(copy(k, 0, me, sibling, src=ins[k]))
            first += [copy(k, 1 + j, me, (*chip, c), src=ins[k]) for j, chip in enumerate(chips)]
        return me, sibling, chips, c, copy, mine, first

    def start(ins, outs, sems):
        *_, mine, first = copies(ins, outs, sems)
        for cp in mine + first:
            cp.start()

    def finish(ins, outs, sems):
        me, sibling, chips, c, copy, mine, first = copies(ins, outs, sems)
        passed = []
        for j, chip in enumerate(chips):
            for k in range(n):
                copy(k, 1 + j, (*chip, c), me).wait_recv()
                fwd = copy(k, 4 + j, (*chip, c), sibling)
                fwd.start()
                passed.append(fwd)
        for k in range(n):
            copy(k, 0, sibling, me).wait_recv()
            for j, chip in enumerate(chips):
                copy(k, 4 + j, (*chip, 1 - c), me).wait_recv()
        for cp in first + passed:
            cp.wait_send()
        for cp in mine:
            cp.wait()

    return Rider(
        tuple(shards),
        tuple(jax.ShapeDtypeStruct((N_DEV * s.shape[0], s.shape[1]), s.dtype) for s in shards),
        (pltpu.SemaphoreType.DMA((7 * n,)), pltpu.SemaphoreType.DMA((7 * n,)), pltpu.SemaphoreType.DMA((n,))),
        start, finish)


def _sibling_exchange(grads):
    n = len(grads)

    def copies(ins, outs, sems):
        send_sems, recv_sems = sems
        x, y, c = lax.axis_index("x"), lax.axis_index("y"), lax.axis_index("c")
        out = []
        for k in range(n):
            r = grads[k].shape[0] // N_DEV
            for j in range(N_CHIPS):
                out.append(pltpu.make_async_remote_copy(
                    src_ref=ins[k].at[pl.ds((2 * j + 1 - c) * r, r), :],
                    dst_ref=outs[k].at[pl.ds(j * r, r), :],
                    send_sem=send_sems.at[N_CHIPS * k + j], recv_sem=recv_sems.at[N_CHIPS * k + j],
                    device_id=(x, y, 1 - c), device_id_type=MESH))
        return out

    def start(ins, outs, sems):
        for cp in copies(ins, outs, sems):
            cp.start()

    def finish(ins, outs, sems):
        for cp in copies(ins, outs, sems):
            cp.wait()

    return Rider(
        tuple(grads), tuple(jax.ShapeDtypeStruct((g.shape[0] // 2, g.shape[1]), g.dtype) for g in grads),
        (pltpu.SemaphoreType.DMA((N_CHIPS * n,)), pltpu.SemaphoreType.DMA((N_CHIPS * n,))), start, finish)


def _chip_exchange(parts):
    n = len(parts)

    def copies(ins, outs, sems):
        send_sems, recv_sems, local_sems = sems
        x, y, c = lax.axis_index("x"), lax.axis_index("y"), lax.axis_index("c")
        my_chip = 2 * x + y
        chips = [(1 - x, y), (x, 1 - y), (1 - x, 1 - y)]
        local, sent = [], []
        for k in range(n):
            r = parts[k].shape[0] // N_CHIPS
            mine = pl.ds(my_chip * r, r)
            local.append(pltpu.make_async_copy(ins[k].at[mine, :], outs[k].at[mine, :], local_sems.at[k]))
            for s, (px, py) in enumerate(chips):
                sent.append(pltpu.make_async_remote_copy(
                    src_ref=ins[k].at[pl.ds((2 * px + py) * r, r), :], dst_ref=outs[k].at[mine, :],
                    send_sem=send_sems.at[3 * k + s], recv_sem=recv_sems.at[3 * k + s],
                    device_id=(px, py, c), device_id_type=MESH))
        return local, sent

    def start(ins, outs, sems):
        local, sent = copies(ins, outs, sems)
        for cp in local + sent:
            cp.start()

    def finish(ins, outs, sems):
        local, sent = copies(ins, outs, sems)
        for cp in sent + local:
            cp.wait()

    return Rider(
        tuple(parts), tuple(jax.ShapeDtypeStruct(p.shape, p.dtype) for p in parts),
        (pltpu.SemaphoreType.DMA((3 * n,)), pltpu.SemaphoreType.DMA((3 * n,)), pltpu.SemaphoreType.DMA((n,))),
        start, finish)


def _join(*riders):
    cuts_in, cuts_out, cuts_sem = [0], [0], [0]
    for r in riders:
        cuts_in.append(cuts_in[-1] + len(r.operands))
        cuts_out.append(cuts_out[-1] + len(r.out_shapes))
        cuts_sem.append(cuts_sem[-1] + len(r.sems))

    def each(which):
        def run(ins, outs, sems):
            for i, r in enumerate(riders):
                getattr(r, which)(ins[cuts_in[i]:cuts_in[i + 1]], outs[cuts_out[i]:cuts_out[i + 1]],
                                  sems[cuts_sem[i]:cuts_sem[i + 1]])
        return run

    return Rider(sum((r.operands for r in riders), ()), sum((r.out_shapes for r in riders), ()),
                 sum((r.sems for r in riders), ()), each("start"), each("finish"))


def _call(body, operands, *, name, out_shape, grid, in_specs, out_specs, scratch_shapes=(), aliases=None,
          semantics=None, vmem=None, rider=None):
    operands, out_shape, scratch_shapes = list(operands), list(out_shape), list(scratch_shapes)
    in_specs, out_specs = list(in_specs), list(out_specs)
    if rider is None:
        res = pl.pallas_call(
            body, name=name, out_shape=out_shape, grid=grid, in_specs=in_specs, out_specs=out_specs,
            scratch_shapes=scratch_shapes, input_output_aliases=aliases or {},
            compiler_params=_params(semantics, vmem))(*operands)
        return list(res), []
    n_in, n_out, n_scr = len(operands), len(out_shape), len(scratch_shapes)
    ri, ro = len(rider.operands), len(rider.out_shapes)

    def carried(*refs):
        a, b = n_in, n_in + ri
        c, d = b + n_out, b + n_out + ro
        e = d + n_scr
        ids = [pl.program_id(k) for k in range(len(grid))]
        first = ids[0] == 0
        last = ids[0] == grid[0] - 1
        for k in range(1, len(grid)):
            first = jnp.logical_and(first, ids[k] == 0)
            last = jnp.logical_and(last, ids[k] == grid[k] - 1)

        @pl.when(first)
        def _():
            rider.start(refs[a:b], refs[c:d], refs[e:])

        body(*refs[:a], *refs[b:c], *refs[d:e])

        @pl.when(last)
        def _():
            rider.finish(refs[a:b], refs[c:d], refs[e:])

    res = pl.pallas_call(
        carried, name=name, out_shape=out_shape + list(rider.out_shapes), grid=grid,
        in_specs=in_specs + [_any_spec()] * ri, out_specs=out_specs + [_any_spec()] * ro,
        scratch_shapes=scratch_shapes + list(rider.sems), input_output_aliases=aliases or {},
        compiler_params=_params(("arbitrary",) * len(grid), vmem))(*operands, *rider.operands)
    return list(res[:n_out]), list(res[n_out:])


def _exchange(rider, name):
    ri, ro = len(rider.operands), len(rider.out_shapes)

    def body(*refs):
        rider.start(refs[:ri], refs[ri:ri + ro], refs[ri + ro:])
        rider.finish(refs[:ri], refs[ri:ri + ro], refs[ri + ro:])

    return pl.pallas_call(
        body, name=name, out_shape=list(rider.out_shapes), in_specs=[_any_spec()] * ri,
        out_specs=[_any_spec()] * ro, scratch_shapes=list(rider.sems))(*rider.operands)


class Cols(NamedTuple):
    arr: jax.Array
    off: int
    width: int


def _cols(a):
    return a if isinstance(a, Cols) else Cols(a, 0, a.shape[1])


def _matmul(a, b, *, mode, name, out_dtype=F32, tm=1024, tn=1024, tk=2048, bias=None, out_cols=None, into=None,
            rider=None):
    a = _cols(a)
    if mode == "nn":
        (m, k), (k2, n) = (a.arr.shape[0], a.width), b.shape
    elif mode == "nt":
        (m, k), (n, k2) = (a.arr.shape[0], a.width), b.shape
    else:
        (k, m), (k2, n) = (a.arr.shape[0], a.width), b.shape
    assert k == k2, (a.arr.shape, b.shape, mode)
    tm, tn, tk = _tile(m, tm), _tile(n, tn), _tile(k, tk)
    nk = k // tk
    dims = {"nn": NN, "nt": NT, "tn": TN}[mode]
    if mode == "tn":
        assert a.off % tm == 0
        a_spec = pl.BlockSpec((tk, tm), lambda i, j, kk, o=a.off // tm: (kk, i + o))
    else:
        assert a.off % tk == 0
        a_spec = pl.BlockSpec((tm, tk), lambda i, j, kk, o=a.off // tk: (i, kk + o))
    b_spec = (pl.BlockSpec((tn, tk), lambda i, j, kk: (j, kk)) if mode == "nt"
              else pl.BlockSpec((tk, tn), lambda i, j, kk: (kk, j)))
    in_specs, operands = [a_spec, b_spec], [a.arr, b]
    if bias is not None:
        in_specs.append(pl.BlockSpec((1, tn), lambda i, j, kk: (0, j)))
        operands.append(bias)
    total_w, o_off = out_cols if out_cols is not None else (n, 0)
    assert o_off % tn == 0
    aliases = {}
    if into is not None:
        assert into.shape == (m, total_w) and into.dtype == out_dtype
        in_specs.append(_any_spec())
        operands.append(into)
        aliases = {len(operands) - 1: 0}
    n_in = len(operands)

    def body(*refs):
        a_ref, b_ref = refs[0], refs[1]
        bias_ref = refs[2] if bias is not None else None
        o_ref = refs[n_in]
        acc_ref = refs[-1] if nk > 1 else None
        part = _dot(a_ref[...].astype(BF16), b_ref[...].astype(BF16), dims)

        def finish(acc):
            if bias_ref is not None:
                acc = acc + bias_ref[...]
            o_ref[...] = acc.astype(out_dtype)

        if nk == 1:
            finish(part)
        else:
            kk = pl.program_id(2)

            @pl.when(kk == 0)
            def _():
                acc_ref[...] = part

            @pl.when(kk > 0)
            def _():
                acc_ref[...] += part

            @pl.when(kk == nk - 1)
            def _():
                finish(acc_ref[...])

    vmem = 2 * (_nbytes((tm, tk), a.arr.dtype) + _nbytes((tk, tn), b.dtype) + _nbytes((tm, tn), out_dtype))
    vmem += 3 * _nbytes((tm, tn), F32)
    (out,), landed = _call(
        body, operands, name=name, out_shape=[jax.ShapeDtypeStruct((m, total_w), out_dtype)],
        grid=(m // tm, n // tn, nk), in_specs=in_specs,
        out_specs=[pl.BlockSpec((tm, tn), lambda i, j, kk, o=o_off // tn: (i, j + o))],
        scratch_shapes=[pltpu.VMEM((tm, tn), F32)] if nk > 1 else [], aliases=aliases,
        semantics=("parallel", "parallel", "arbitrary"), vmem=vmem, rider=rider)
    return out if rider is None else (out, landed)


def _ew(fn, *, name, rows, width, tiles, vecs=(), outs, accs=0, tl=512, cw=512, into=None, with_col=False):
    tl, cw = _tile(rows, tl, SUBLANES), _tile(width, cw)
    ncol = width // cw
    nt_, nv = len(tiles), len(vecs)
    into = list(into) if into is not None else [None] * len(outs)
    aliased = [t for t in into if t is not None]

    def off(o):
        assert o % cw == 0, (name, o, cw)
        return o // cw

    in_specs, vmem = [], 0
    for t in tiles:
        arr, o = t[0], off(t[1])
        wrap = t[2] // cw if len(t) > 2 else ncol
        in_specs.append(pl.BlockSpec((tl, cw), lambda j, i, o=o, wrap=wrap: (i, o + j % wrap)))
        vmem += _nbytes((tl, cw), arr.dtype)
    in_specs += [pl.BlockSpec((1, cw), lambda j, i, o=off(o): (0, j + o)) for _, o in vecs]
    in_specs += [_any_spec() for _ in aliased]
    out_shape, out_specs, aliases = [], [], {}
    n_in = nt_ + nv
    for idx, ((dt, tw, o), tgt) in enumerate(zip(outs, into)):
        out_shape.append(jax.ShapeDtypeStruct((rows, tw), dt))
        out_specs.append(pl.BlockSpec((tl, cw), lambda j, i, o=off(o): (i, j + o)))
        vmem += _nbytes((tl, cw), dt)
        if tgt is not None:
            assert tgt.shape == (rows, tw) and tgt.dtype == dt, (name, tgt.shape, tgt.dtype)
            aliases[n_in + len(aliases)] = idx
    for _ in range(accs):
        out_shape.append(jax.ShapeDtypeStruct((1, width), F32))
        out_specs.append(pl.BlockSpec((1, cw), lambda j, i: (0, j)))
    n_out = len(outs)

    def body(*refs):
        vals = [r[...] for r in refs[:n_in]]
        out_refs = refs[n_in + len(aliased):]
        res = fn(pl.program_id(0), *vals) if with_col else fn(*vals)
        res = res if isinstance(res, (tuple, list)) else (res,)
        assert len(res) == n_out + accs, (name, len(res))
        for r, v in zip(out_refs[:n_out], res[:n_out]):
            r[...] = v.astype(r.dtype)
        first = pl.program_id(1) == 0
        for r, v in zip(out_refs[n_out:], res[n_out:]):
            s = jnp.sum(v, axis=0, keepdims=True)

            @pl.when(first)
            def _(r=r, s=s):
                r[...] = s

            @pl.when(jnp.logical_not(first))
            def _(r=r, s=s):
                r[...] += s

    return pl.pallas_call(
        body, name=name, out_shape=out_shape, grid=(ncol, rows // tl),
        in_specs=in_specs, out_specs=out_specs, input_output_aliases=aliases,
        compiler_params=_params(("parallel", "arbitrary"), 3 * vmem),
    )(*[t[0] for t in tiles], *[v for v, _ in vecs], *aliased)


def _rmsnorm_fwd(x, w_row, name):
    rows, d = x.shape
    tl = _tile(rows, 512, SUBLANES)

    def body(x_ref, w_ref, h_ref):
        xv = x_ref[...]
        rstd = lax.rsqrt(jnp.mean(xv * xv, axis=-1, keepdims=True) + NORM_EPS)
        h_ref[...] = (xv * rstd * w_ref[...]).astype(BF16)

    return pl.pallas_call(
        body, name=name, out_shape=jax.ShapeDtypeStruct((rows, d), BF16), grid=(rows // tl,),
        in_specs=[pl.BlockSpec((tl, d), lambda i: (i, 0)), pl.BlockSpec((1, d), lambda i: (0, 0))],
        out_specs=pl.BlockSpec((tl, d), lambda i: (i, 0)),
        compiler_params=_params(("parallel",)),
    )(x, w_row)


def _rmsnorm_bwd(x, w_row, dh, dout, name, rider=None):
    rows, d = x.shape
    tl = _tile(rows, 256, SUBLANES)

    def body(x_ref, w_ref, dh_ref, dout_ref, gx_ref, gw_ref):
        xv = x_ref[...]
        rstd = lax.rsqrt(jnp.mean(xv * xv, axis=-1, keepdims=True) + NORM_EPS)
        xn = xv * rstd
        dhv = dh_ref[...]
        dxn = dhv * w_ref[...]
        dx = rstd * (dxn - xn * jnp.mean(dxn * xn, axis=-1, keepdims=True))
        gx_ref[...] = dout_ref[...] + dx
        gw = jnp.sum(dhv * xn, axis=0, keepdims=True)

        @pl.when(pl.program_id(0) == 0)
        def _():
            gw_ref[...] = gw

        @pl.when(pl.program_id(0) > 0)
        def _():
            gw_ref[...] += gw

    tile = pl.BlockSpec((tl, d), lambda i: (i, 0))
    row = pl.BlockSpec((1, d), lambda i: (0, 0))
    res, landed = _call(
        body, [x, w_row, dh, dout], name=name,
        out_shape=[jax.ShapeDtypeStruct((rows, d), F32), jax.ShapeDtypeStruct((1, d), F32)],
        grid=(rows // tl,), in_specs=[tile, row, tile, tile], out_specs=[tile, row],
        semantics=("arbitrary",), rider=rider)
    return res if rider is None else (res, landed)


def _head_norm(v, w_row):
    rstd = lax.rsqrt(jnp.mean(v * v, axis=-1, keepdims=True) + NORM_EPS)
    vn = v * rstd
    return vn, vn * w_row, rstd


def _attn_specs(attn_w, kv_w):
    half = attn_w // 2
    kcol, vcol = attn_w // kv_w, attn_w // kv_w + 1
    gcol = (attn_w + 2 * kv_w) // half
    prev = lambda i: jnp.maximum(i - 1, 0)
    return [
        pl.BlockSpec((BLOCK, attn_w), lambda i: (i, 0)),
        pl.BlockSpec((BLOCK, kv_w), lambda i: (prev(i), kcol)),
        pl.BlockSpec((BLOCK, kv_w), lambda i: (i, kcol)),
        pl.BlockSpec((BLOCK, kv_w), lambda i: (prev(i), vcol)),
        pl.BlockSpec((BLOCK, kv_w), lambda i: (i, vcol)),
        pl.BlockSpec((BLOCK, half), lambda i: (i, gcol)),
        pl.BlockSpec((BLOCK, half), lambda i: (i, gcol + 1)),
    ]


def _band_mask(i):
    q_loc = lax.broadcasted_iota(jnp.int32, (BLOCK, 2 * BLOCK), 0) + BLOCK
    k_loc = lax.broadcasted_iota(jnp.int32, (BLOCK, 2 * BLOCK), 1)
    diff = q_loc - k_loc
    first_key = jnp.where(i == 0, BLOCK, 0)
    return (diff >= 0) & (diff < BLOCK) & (k_loc >= first_key)


def _softmax_with_sink(s, sink):
    m = jnp.maximum(jnp.max(s, axis=-1, keepdims=True), sink)
    p = jnp.exp(s - m)
    e_sink = jnp.exp(sink - m)
    den = jnp.sum(p, axis=-1, keepdims=True) + e_sink
    inv = 1.0 / den
    return p * inv, e_sink * inv


def _attention_fwd(proj, qw_row, kw_row, sinks, *, attn_w, kv_w, name):
    rows = proj.shape[0]
    n_q = attn_w // HEAD_DIM
    per_kv = n_q // N_KV_HEADS
    scale = 1.0 / math.sqrt(HEAD_DIM)

    def body(q_ref, kp_ref, kc_ref, vp_ref, vc_ref, glo_ref, ghi_ref, qw_ref, kw_ref, sink_ref, o_ref):
        i = pl.program_id(0)
        valid = _band_mask(i)
        q = q_ref[...]
        kk = jnp.concatenate([kp_ref[...], kc_ref[...]], axis=0)
        vv = jnp.concatenate([vp_ref[...], vc_ref[...]], axis=0)
        gate = jnp.concatenate([glo_ref[...], ghi_ref[...]], axis=1)
        heads = []
        for g in range(N_KV_HEADS):
            sl = slice(g * HEAD_DIM, (g + 1) * HEAD_DIM)
            _, kh, _ = _head_norm(kk[:, sl], kw_ref[...])
            kh = kh.astype(BF16)
            vh = vv[:, sl].astype(BF16)
            for r in range(per_kv):
                h = g * per_kv + r
                hs = slice(h * HEAD_DIM, (h + 1) * HEAD_DIM)
                _, qh, _ = _head_norm(q[:, hs], qw_ref[...])
                s = _dot(qh.astype(BF16), kh, NT) * scale
                s = jnp.where(valid, s, -1e30)
                p, _ = _softmax_with_sink(s, sink_ref[h])
                heads.append(_dot(p.astype(BF16), vh, NN))
        attn = jnp.concatenate(heads, axis=1)
        o_ref[...] = (attn * _silu(gate)).astype(BF16)

    vec = pl.BlockSpec((1, HEAD_DIM), lambda i: (0, 0))
    return pl.pallas_call(
        body, name=name, out_shape=jax.ShapeDtypeStruct((rows, attn_w), BF16), grid=(rows // BLOCK,),
        in_specs=_attn_specs(attn_w, kv_w) + [vec, vec, pl.BlockSpec(memory_space=pltpu.SMEM)],
        out_specs=pl.BlockSpec((BLOCK, attn_w), lambda i: (i, 0)),
        compiler_params=_params(("parallel",)),
    )(proj, proj, proj, proj, proj, proj, proj, qw_row, kw_row, sinks)


def _attention_bwd(proj, d_ag, dproj, qw_row, kw_row, sinks, *, attn_w, kv_w, name, rider=None):
    rows = proj.shape[0]
    nb = rows // BLOCK
    n_q = attn_w // HEAD_DIM
    per_kv = n_q // N_KV_HEADS
    scale = 1.0 / math.sqrt(HEAD_DIM)
    w_out = 2 * attn_w + 2 * kv_w

    def body(q_ref, kp_ref, kc_ref, vp_ref, vc_ref, glo_ref, ghi_ref, dag_ref, qw_ref, kw_ref, sink_ref, _,
             dp_ref, dkv_ref, gqw_ref, gkw_ref, gs_ref):
        i = pl.program_id(0)
        valid = _band_mask(i)
        q = q_ref[...]
        kk = jnp.concatenate([kp_ref[...], kc_ref[...]], axis=0)
        vv = jnp.concatenate([vp_ref[...], vc_ref[...]], axis=0)
        gate = jnp.concatenate([glo_ref[...], ghi_ref[...]], axis=1)
        d_ag_v = dag_ref[...]
        qw, kw = qw_ref[...], kw_ref[...]
        lane = lax.broadcasted_iota(jnp.int32, (SUBLANES, LANES), 1)
        sub = lax.broadcasted_iota(jnp.int32, (SUBLANES, LANES), 0)
        gqw = jnp.zeros((1, HEAD_DIM), F32)
        gkw = jnp.zeros((1, HEAD_DIM), F32)
        gsink = jnp.zeros((SUBLANES, LANES), F32)
        dq_heads, dgate_heads, dk_heads, dv_heads = [], [], [], []
        for g in range(N_KV_HEADS):
            sl = slice(g * HEAD_DIM, (g + 1) * HEAD_DIM)
            kn, kh, k_rstd = _head_norm(kk[:, sl], kw)
            kh = kh.astype(BF16)
            vh = vv[:, sl].astype(BF16)
            dkh = jnp.zeros((2 * BLOCK, HEAD_DIM), F32)
            dvh = jnp.zeros((2 * BLOCK, HEAD_DIM), F32)
            for r in range(per_kv):
                h = g * per_kv + r
                hs = slice(h * HEAD_DIM, (h + 1) * HEAD_DIM)
                qn, qh, q_rstd = _head_norm(q[:, hs], qw)
                qh = qh.astype(BF16)
                s = _dot(qh, kh, NT) * scale
                s = jnp.where(valid, s, -1e30)
                p, p_sink = _softmax_with_sink(s, sink_ref[h])
                pb = p.astype(BF16)
                o = _dot(pb, vh, NN)
                gate_h = gate[:, hs]
                d_ag_h = d_ag_v[:, hs]
                dgate_heads.append(d_ag_h * o * _dsilu(gate_h))
                do = (d_ag_h * _silu(gate_h)).astype(BF16)
                dp = _dot(do, vh, NT)
                delta = jnp.sum(p * dp, axis=-1, keepdims=True)
                ds = (p * (dp - delta) * scale).astype(BF16)
                gs_h = jnp.sum(-p_sink * delta, axis=0, keepdims=True)
                gsink = gsink + jnp.where((lane == h) & (sub == 0), gs_h, 0.0)
                dvh = dvh + _dot(pb, do, TN)
                dkh = dkh + _dot(ds, qh, TN)
                dqh = _dot(ds, kh, NN)
                gqw = gqw + jnp.sum(dqh * qn, axis=0, keepdims=True)
                dqn = dqh * qw
                dq_heads.append(q_rstd * (dqn - qn * jnp.mean(dqn * qn, axis=-1, keepdims=True)))
            gkw = gkw + jnp.sum(dkh * kn, axis=0, keepdims=True)
            dkn = dkh * kw
            dk_heads.append(k_rstd * (dkn - kn * jnp.mean(dkn * kn, axis=-1, keepdims=True)))
            dv_heads.append(dvh)
        dp_ref[:, 0:attn_w] = jnp.concatenate(dq_heads, axis=1).astype(BF16)
        dp_ref[:, attn_w:attn_w + 2 * kv_w] = jnp.zeros((BLOCK, 2 * kv_w), BF16)
        dp_ref[:, attn_w + 2 * kv_w:w_out] = jnp.concatenate(dgate_heads, axis=1).astype(BF16)
        dkv_ref[0] = jnp.concatenate(dk_heads + dv_heads, axis=1)

        @pl.when(i == 0)
        def _():
            gqw_ref[...] = gqw
            gkw_ref[...] = gkw
            gs_ref[...] = gsink

        @pl.when(i > 0)
        def _():
            gqw_ref[...] += gqw
            gkw_ref[...] += gkw
            gs_ref[...] += gsink

    vec = pl.BlockSpec((1, HEAD_DIM), lambda i: (0, 0))
    res, landed = _call(
        body, [proj, proj, proj, proj, proj, proj, proj, d_ag, qw_row, kw_row, sinks, dproj], name=name,
        out_shape=[jax.ShapeDtypeStruct(dproj.shape, BF16),
                   jax.ShapeDtypeStruct((nb, 2 * BLOCK, 2 * kv_w), F32),
                   jax.ShapeDtypeStruct((1, HEAD_DIM), F32), jax.ShapeDtypeStruct((1, HEAD_DIM), F32),
                   jax.ShapeDtypeStruct((SUBLANES, LANES), F32)],
        grid=(nb,),
        in_specs=_attn_specs(attn_w, kv_w) + [pl.BlockSpec((BLOCK, attn_w), lambda i: (i, 0)), vec, vec,
                                              pl.BlockSpec(memory_space=pltpu.SMEM), _any_spec()],
        out_specs=[pl.BlockSpec((BLOCK, w_out), lambda i: (i, 0)),
                   pl.BlockSpec((1, 2 * BLOCK, 2 * kv_w), lambda i: (i, 0, 0)),
                   vec, vec, pl.BlockSpec((SUBLANES, LANES), lambda i: (0, 0))],
        aliases={11: 0}, semantics=("arbitrary",), vmem=40 * 1024 * 1024, rider=rider)
    return res if rider is None else (res, landed)


def _attention_dkv(dproj, dkv, *, attn_w, kv_w, name):
    rows = dproj.shape[0]
    nb = rows // BLOCK
    col = attn_w // (2 * kv_w)

    def body(cur_ref, nxt_ref, _, o_ref):
        i = pl.program_id(0)
        nxt = jnp.where(i < nb - 1, nxt_ref[0, 0:BLOCK, :], 0.0)
        o_ref[...] = (cur_ref[0, BLOCK:2 * BLOCK, :] + nxt).astype(BF16)

    blk = lambda f: pl.BlockSpec((1, 2 * BLOCK, 2 * kv_w), f)
    return pl.pallas_call(
        body, name=name, out_shape=jax.ShapeDtypeStruct(dproj.shape, BF16), grid=(nb,),
        in_specs=[blk(lambda i: (i, 0, 0)), blk(lambda i: (jnp.minimum(i + 1, nb - 1), 0, 0)), _any_spec()],
        out_specs=pl.BlockSpec((BLOCK, 2 * kv_w), lambda i: (i, col)),
        input_output_aliases={2: 0},
        compiler_params=_params(("parallel",)),
    )(dkv, dkv, dproj)


def _cmul(ar, ai, br, bi):
    return ar * br - ai * bi, ar * bi + ai * br


def _ssm_prep(a_re, a_im, log_dt_col, steps, name):
    assert steps & (steps - 1) == 0

    def body(are_ref, aim_ref, ldt_ref, abr_ref, abi_ref, cfr_ref, cfi_ref, apr_ref, api_ref):
        are, aim = are_ref[...], aim_ref[...]
        dt = jnp.exp(ldt_ref[...])
        mag = jnp.exp(dt * are)
        abr = mag * jnp.cos(dt * aim)
        abi = mag * jnp.sin(dt * aim)
        num_re, num_im = abr - 1.0, abi
        den = are * are + aim * aim
        abr_ref[...] = abr
        abi_ref[...] = abi
        cfr_ref[...] = (num_re * are + num_im * aim) / den
        cfi_ref[...] = (num_im * are - num_re * aim) / den
        pr, pi = abr, abi
        n = steps
        while n > 1:
            pr, pi = _cmul(pr, pi, pr, pi)
            n //= 2
        apr_ref[...] = pr
        api_ref[...] = pi

    shp = jax.ShapeDtypeStruct(a_re.shape, F32)
    return pl.pallas_call(body, name=name, out_shape=[shp] * 6)(a_re, a_im, log_dt_col)


def _ssm_param_bwd(a_re, a_im, log_dt_col, d_ab_re, d_ab_im, d_cf_re, d_cf_im, name):
    def body(are_ref, aim_ref, ldt_ref, gabr_ref, gabi_ref, gcfr_ref, gcfi_ref, dar_ref, dai_ref, dldt_ref):
        are, aim = are_ref[...], aim_ref[...]
        dt = jnp.exp(ldt_ref[...])
        mag = jnp.exp(dt * are)
        abr = mag * jnp.cos(dt * aim)
        abi = mag * jnp.sin(dt * aim)
        den = are * are + aim * aim
        cfr = ((abr - 1.0) * are + abi * aim) / den
        cfi = (abi * are - (abr - 1.0) * aim) / den
        gabr, gabi = jnp.sum(gabr_ref[...], axis=0), jnp.sum(gabi_ref[...], axis=0)
        gcfr, gcfi = jnp.sum(gcfr_ref[...], axis=0), jnp.sum(gcfi_ref[...], axis=0)
        inv_r, inv_i = are / den, -aim / den
        t_r, t_i = _cmul(inv_r, -inv_i, gcfr, gcfi)
        gabr, gabi = gabr + t_r, gabi + t_i
        q_r, q_i = _cmul(cfr, cfi, inv_r, inv_i)
        da_r, da_i = _cmul(-q_r, q_i, gcfr, gcfi)
        gz_r, gz_i = _cmul(abr, -abi, gabr, gabi)
        dar_ref[...] = da_r + dt * gz_r
        dai_ref[...] = da_i + dt * gz_i
        dldt_ref[...] = dt * jnp.sum(are * gz_r + aim * gz_i, axis=-1, keepdims=True)

    shp = jax.ShapeDtypeStruct(a_re.shape, F32)
    return pl.pallas_call(body, name=name, out_shape=[shp, shp, jax.ShapeDtypeStruct(log_dt_col.shape, F32)])(
        a_re, a_im, log_dt_col, d_ab_re, d_ab_im, d_cf_re, d_cf_im)


SCAN_LANES = 512


def _scan_segments(xr_ref, xi_ref, a_re, a_im, ap_re, ap_im, carry_re, carry_im, cm_re, cm_im, steps, reverse):
    n = xr_ref.shape[1]
    order = range(steps - 1, -1, -1) if reverse else range(steps)
    seg_order = range(SUBLANES - 1, -1, -1) if reverse else range(SUBLANES)
    for c0 in range(0, n, SCAN_LANES):
        ls = slice(c0, c0 + SCAN_LANES)
        ar = jnp.broadcast_to(a_re[:, ls], (SUBLANES, SCAN_LANES))
        ai = jnp.broadcast_to(a_im[:, ls], (SUBLANES, SCAN_LANES))

        def local(t, s, ar=ar, ai=ai, ls=ls):
            j = steps - 1 - t if reverse else t
            r0 = pl.multiple_of(j * SUBLANES, SUBLANES)
            sr, si = _cmul(ar, ai, s[0], s[1])
            sr = sr + xr_ref[pl.ds(r0, SUBLANES), ls]
            si = si + xi_ref[pl.ds(r0, SUBLANES), ls]
            xr_ref[pl.ds(r0, SUBLANES), ls] = sr
            xi_ref[pl.ds(r0, SUBLANES), ls] = si
            return sr, si

        zero = jnp.zeros((SUBLANES, SCAN_LANES), F32)
        end_r, end_i = lax.fori_loop(0, steps, local, (zero, zero))
        cr, ci = carry_re[:, ls], carry_im[:, ls]
        apr, api = ap_re[:, ls], ap_im[:, ls]
        for r in seg_order:
            cm_re[r:r + 1, ls] = cr
            cm_im[r:r + 1, ls] = ci
            tr, ti = _cmul(apr, api, cr, ci)
            cr, ci = end_r[r:r + 1, :] + tr, end_i[r:r + 1, :] + ti
        carry_re[:, ls] = cr
        carry_im[:, ls] = ci

        def fix(t, s, ar=ar, ai=ai, ls=ls):
            j = steps - 1 - t if reverse else t
            r0 = pl.multiple_of(j * SUBLANES, SUBLANES)
            sr, si = _cmul(ar, ai, s[0], s[1])
            xr_ref[pl.ds(r0, SUBLANES), ls] += sr
            xi_ref[pl.ds(r0, SUBLANES), ls] += si
            return sr, si

        lax.fori_loop(0, steps, fix, (cm_re[:, ls], cm_im[:, ls]))
    del order


def _ssm_blocks(b_re, b_im, c_re, c_im):
    g = b_re.shape[0]
    per = MXU_DIM // GROUP
    nsb = g // per
    eye = jnp.eye(per, dtype=F32)

    def b_blocks(b):
        bt = b.reshape(nsb, per, STATE, GROUP).transpose(0, 1, 3, 2)
        return (bt[:, :, :, None, :] * eye[None, :, None, :, None]).reshape(nsb, per * GROUP, per * STATE).astype(BF16)

    def c_blocks(cm):
        ct = cm.reshape(nsb, per, GROUP, STATE).transpose(0, 1, 3, 2)
        return (ct[:, :, :, None, :] * eye[None, :, None, :, None]).reshape(nsb, per * STATE, per * GROUP).astype(BF16)

    return b_blocks(b_re), b_blocks(b_im), c_blocks(c_re), c_blocks(c_im)


def _unblock_b(db):
    nsb = db.shape[0]
    per = MXU_DIM // GROUP
    d = db.reshape(nsb, per, GROUP, per, STATE)
    d = jnp.stack([d[:, k, :, k, :] for k in range(per)], axis=1)
    return d.transpose(0, 1, 3, 2).reshape(nsb * per, STATE, GROUP)


def _unblock_c(dc):
    nsb = dc.shape[0]
    per = MXU_DIM // GROUP
    d = dc.reshape(nsb, per, STATE, per, GROUP)
    d = jnp.stack([d[:, k, :, k, :] for k in range(per)], axis=1)
    return d.transpose(0, 1, 3, 2).reshape(nsb * per, GROUP, STATE)


def _to_segments(v, chunk):
    rows, w = v.shape
    return v.reshape(rows // chunk, SUBLANES, chunk // SUBLANES, w).transpose(0, 2, 1, 3).reshape(rows, w)


def _from_segments(v, chunk):
    rows, w = v.shape
    return v.reshape(rows // chunk, chunk // SUBLANES, SUBLANES, w).transpose(0, 2, 1, 3).reshape(rows, w)


def _ssm_fwd(u, blocks, rows_p, d_row, *, chunk, name):
    rows, w = u.shape
    nc = rows // chunk
    steps = chunk // SUBLANES
    bre, bim, cre, cim = blocks
    nsb = bre.shape[0]
    n_state = nsb * bre.shape[2]
    sbw, sbs = bre.shape[1], bre.shape[2]

    def body(u_ref, bre_hbm, bim_hbm, cre_hbm, cim_hbm, abr_ref, abi_ref, cfr_ref, cfi_ref, apr_ref, api_ref,
             d_ref, y_ref, str_ref, sti_ref, bre_ref, bim_ref, cre_ref, cim_ref, sr, si, carry_r, carry_i,
             cm_r, cm_i):
        @pl.when(pl.program_id(0) == 0)
        def _():
            for src, dst in ((bre_hbm, bre_ref), (bim_hbm, bim_ref), (cre_hbm, cre_ref), (cim_hbm, cim_ref)):
                pltpu.sync_copy(src, dst)
            carry_r[...] = jnp.zeros_like(carry_r)
            carry_i[...] = jnp.zeros_like(carry_i)

        str_ref[0] = carry_r[...]
        sti_ref[0] = carry_i[...]
        for sb in range(nsb):
            us = slice(sb * sbw, (sb + 1) * sbw)
            ss = slice(sb * sbs, (sb + 1) * sbs)
            ub = u_ref[:, us].astype(BF16)
            bur = _dot(ub, bre_ref[sb], NN)
            bui = _dot(ub, bim_ref[sb], NN)
            xr, xi = _cmul(cfr_ref[:, ss], cfi_ref[:, ss], bur, bui)
            sr[:, ss] = xr
            si[:, ss] = xi
        _scan_segments(sr, si, abr_ref[...], abi_ref[...], apr_ref[...], api_ref[...],
                       carry_r, carry_i, cm_r, cm_i, steps, False)
        for sb in range(nsb):
            us = slice(sb * sbw, (sb + 1) * sbw)
            ss = slice(sb * sbs, (sb + 1) * sbs)
            y = _dot(sr[:, ss].astype(BF16), cre_ref[sb], NN) - _dot(si[:, ss].astype(BF16), cim_ref[sb], NN)
            y_ref[:, us] = y + d_ref[:, us] * u_ref[:, us]

    row_n = pl.BlockSpec((1, n_state), lambda c: (0, 0))
    st = pl.BlockSpec((1, 1, n_state), lambda c: (c, 0, 0))
    held = [pltpu.VMEM(b.shape, BF16) for b in blocks]
    vmem = 2 * sum(_nbytes(b.shape, BF16) for b in blocks) + 3 * _nbytes((chunk, n_state), F32)
    return pl.pallas_call(
        body, name=name,
        out_shape=[jax.ShapeDtypeStruct((rows, w), F32), jax.ShapeDtypeStruct((nc, 1, n_state), F32),
                   jax.ShapeDtypeStruct((nc, 1, n_state), F32)],
        grid=(nc,),
        in_specs=[pl.BlockSpec((chunk, w), lambda c: (c, 0))] + [_any_spec()] * 4
        + [row_n] * 6 + [pl.BlockSpec((1, w), lambda c: (0, 0))],
        out_specs=[pl.BlockSpec((chunk, w), lambda c: (c, 0)), st, st],
        scratch_shapes=held + [pltpu.VMEM((chunk, n_state), F32), pltpu.VMEM((chunk, n_state), F32),
                               pltpu.VMEM((1, n_state), F32), pltpu.VMEM((1, n_state), F32),
                               pltpu.VMEM((SUBLANES, n_state), F32), pltpu.VMEM((SUBLANES, n_state), F32)],
        compiler_params=_params(("arbitrary",), vmem),
    )(u, bre, bim, cre, cim, *rows_p, d_row)


def _ssm_bwd(u, y, dyg, st_re, st_im, blocks, rows_p, d_row, *, chunk, name, rider=None):
    rows, w = u.shape
    nc = rows // chunk
    steps = chunk // SUBLANES
    bre, bim, cre, cim = blocks
    nsb = bre.shape[0]
    sbw, sbs = bre.shape[1], bre.shape[2]
    n_state = nsb * sbs

    def body(u_ref, y_ref, dyg_ref, str_ref, sti_ref, bre_hbm, bim_hbm, cre_hbm, cim_hbm,
             abr_ref, abi_ref, cfr_ref, cfi_ref, apr_ref, api_ref, d_ref,
             du_ref, dbre_hbm, dbim_hbm, dcre_hbm, dcim_hbm, gabr_ref, gabi_ref, gcfr_ref, gcfi_ref, dd_ref,
             bre_ref, bim_ref, cre_ref, cim_ref, dbre_ref, dbim_ref, dcre_ref, dcim_ref,
             bur, bui, sr, si, lr, li, carry_r, carry_i, lam_r, lam_i, cm_r, cm_i, cl_r, cl_i):
        first = pl.program_id(0) == 0

        @pl.when(first)
        def _():
            for src, dst in ((bre_hbm, bre_ref), (bim_hbm, bim_ref), (cre_hbm, cre_ref), (cim_hbm, cim_ref)):
                pltpu.sync_copy(src, dst)
            lam_r[...] = jnp.zeros_like(lam_r)
            lam_i[...] = jnp.zeros_like(lam_i)
            for ref in (dbre_ref, dbim_ref, dcre_ref, dcim_ref, gabr_ref, gabi_ref, gcfr_ref, gcfi_ref, dd_ref):
                ref[...] = jnp.zeros_like(ref)

        uv = u_ref[...]
        dy = dyg_ref[...] * _dgelu(y_ref[...])
        dd_ref[...] += jnp.sum(dy * uv, axis=0, keepdims=True)
        dyb = dy.astype(BF16)
        ub = uv.astype(BF16)
        carry_r[...] = str_ref[0]
        carry_i[...] = sti_ref[0]
        for sb in range(nsb):
            us = slice(sb * sbw, (sb + 1) * sbw)
            ss = slice(sb * sbs, (sb + 1) * sbs)
            br = _dot(ub[:, us], bre_ref[sb], NN)
            bi = _dot(ub[:, us], bim_ref[sb], NN)
            bur[:, ss] = br
            bui[:, ss] = bi
            xr, xi = _cmul(cfr_ref[:, ss], cfi_ref[:, ss], br, bi)
            sr[:, ss] = xr
            si[:, ss] = xi
            lr[:, ss] = _dot(dyb[:, us], cre_ref[sb], NT)
            li[:, ss] = -_dot(dyb[:, us], cim_ref[sb], NT)
        abr, abi = abr_ref[...], abi_ref[...]
        apr, api = apr_ref[...], api_ref[...]
        _scan_segments(sr, si, abr, abi, apr, api, carry_r, carry_i, cm_r, cm_i, steps, False)
        for sb in range(nsb):
            us = slice(sb * sbw, (sb + 1) * sbw)
            ss = slice(sb * sbs, (sb + 1) * sbs)
            dcre_ref[sb] += _dot(sr[:, ss].astype(BF16), dyb[:, us], TN)
            dcim_ref[sb] -= _dot(si[:, ss].astype(BF16), dyb[:, us], TN)
        _scan_segments(lr, li, abr, -abi, apr, -api, lam_r, lam_i, cl_r, cl_i, steps, True)
        for c0 in range(0, n_state, SCAN_LANES):
            ls = slice(c0, c0 + SCAN_LANES)
            cfr = jnp.broadcast_to(cfr_ref[:, ls], (SUBLANES, SCAN_LANES))
            cfi = jnp.broadcast_to(cfi_ref[:, ls], (SUBLANES, SCAN_LANES))

            def step(j, acc, ls=ls, cfr=cfr, cfi=cfi):
                gar, gai, gcr, gci, pr, pi = acc
                r0 = pl.multiple_of(j * SUBLANES, SUBLANES)
                rws = pl.ds(r0, SUBLANES)
                l_r, l_i = lr[rws, ls], li[rws, ls]
                t_r, t_i = _cmul(pr, -pi, l_r, l_i)
                b_r, b_i = bur[rws, ls], bui[rws, ls]
                c_r, c_i = _cmul(b_r, -b_i, l_r, l_i)
                x_r, x_i = _cmul(cfr, -cfi, l_r, l_i)
                bur[rws, ls] = x_r
                bui[rws, ls] = x_i
                return gar + t_r, gai + t_i, gcr + c_r, gci + c_i, sr[rws, ls], si[rws, ls]

            zero = jnp.zeros((SUBLANES, SCAN_LANES), F32)
            gar, gai, gcr, gci, _, _ = lax.fori_loop(
                0, steps, step, (zero, zero, zero, zero, cm_r[:, ls], cm_i[:, ls]))
            gabr_ref[:, ls] += gar
            gabi_ref[:, ls] += gai
            gcfr_ref[:, ls] += gcr
            gcfi_ref[:, ls] += gci
        for sb in range(nsb):
            us = slice(sb * sbw, (sb + 1) * sbw)
            ss = slice(sb * sbs, (sb + 1) * sbs)
            xr, xi = bur[:, ss].astype(BF16), bui[:, ss].astype(BF16)
            du = _dot(xr, bre_ref[sb], NT) + _dot(xi, bim_ref[sb], NT)
            du_ref[:, us] = du + d_ref[:, us] * dy[:, us]
            dbre_ref[sb] += _dot(ub[:, us], xr, TN)
            dbim_ref[sb] += _dot(ub[:, us], xi, TN)

        @pl.when(pl.program_id(0) == nc - 1)
        def _():
            for src, dst in ((dbre_ref, dbre_hbm), (dbim_ref, dbim_hbm), (dcre_ref, dcre_hbm), (dcim_ref, dcim_hbm)):
                pltpu.sync_copy(src, dst)

    rev = lambda c: nc - 1 - c
    tile = pl.BlockSpec((chunk, w), lambda c: (rev(c), 0))
    row_n = pl.BlockSpec((1, n_state), lambda c: (0, 0))
    row_w = pl.BlockSpec((1, w), lambda c: (0, 0))
    st = pl.BlockSpec((1, 1, n_state), lambda c: (rev(c), 0, 0))
    acc8 = pl.BlockSpec((SUBLANES, n_state), lambda c: (0, 0))
    big = pltpu.VMEM((chunk, n_state), F32)
    row = pltpu.VMEM((1, n_state), F32)
    eight = pltpu.VMEM((SUBLANES, n_state), F32)
    f32 = lambda a: jax.ShapeDtypeStruct(a.shape, F32)
    held = [pltpu.VMEM(b.shape, BF16) for b in blocks] + [pltpu.VMEM(b.shape, F32) for b in blocks]
    vmem = (sum(_nbytes(b.shape, BF16) + _nbytes(b.shape, F32) for b in blocks)
            + 7 * _nbytes((chunk, n_state), F32) + 16 * _nbytes((chunk, w), F32))
    res, landed = _call(
        body, [u, y, dyg, st_re, st_im, bre, bim, cre, cim, *rows_p, d_row], name=name,
        out_shape=[jax.ShapeDtypeStruct((rows, w), F32), f32(bre), f32(bim), f32(cre), f32(cim)]
        + [jax.ShapeDtypeStruct((SUBLANES, n_state), F32)] * 4 + [jax.ShapeDtypeStruct((1, w), F32)],
        grid=(nc,),
        in_specs=[tile, tile, tile, st, st] + [_any_spec()] * 4 + [row_n] * 6 + [row_w],
        out_specs=[tile] + [_any_spec()] * 4 + [acc8] * 4 + [row_w],
        scratch_shapes=held + [big] * 6 + [row] * 4 + [eight] * 4,
        semantics=("arbitrary",), vmem=vmem, rider=rider)
    return res if rider is None else (res, landed)


def _loss_grad(x, mm, target, name):
    rows, d = x.shape

    def fn(xv, mv, tv):
        err = xv + mv - tv
        g = err * (1.0 / d)
        return g, g, 0.5 * err * g

    return _ew(fn, name=name, rows=rows, width=d, tiles=[(x, 0), (mm, 0), (target, 0)],
               outs=[(F32, d, 0), (BF16, d, 0)], accs=1)


def _pair_sum(grad, recv, name):
    r4, cdim = recv.shape
    r = r4 // N_CHIPS
    tr = _tile(r, 256, 16)
    g4 = grad.reshape(N_CHIPS, 2, r, cdim)
    r3 = recv.reshape(N_CHIPS, r, cdim)
    core = jnp.reshape(lax.axis_index("c"), (1,)).astype(jnp.int32)

    def body(c_ref, g_ref, r_ref, o_ref):
        o_ref[...] = (g_ref[0] + r_ref[...]).astype(BF16)

    out = pl.pallas_call(
        body, name=name, out_shape=jax.ShapeDtypeStruct((N_CHIPS, r, cdim), BF16),
        grid_spec=pltpu.PrefetchScalarGridSpec(
            num_scalar_prefetch=1, grid=(N_CHIPS, r // tr),
            in_specs=[pl.BlockSpec((1, 1, tr, cdim), lambda j, i, c: (j, c[0], i, 0)),
                      pl.BlockSpec((1, tr, cdim), lambda j, i, c: (j, i, 0))],
            out_specs=pl.BlockSpec((1, tr, cdim), lambda j, i, c: (j, i, 0))),
        compiler_params=_params(("parallel", "parallel")),
    )(core, g4, r3)
    return out.reshape(r4, cdim)


def _chip_sum(recv, name):
    r4, cdim = recv.shape
    r = r4 // N_CHIPS
    tr = _tile(r, 256, 16)
    r3 = recv.reshape(N_CHIPS, r, cdim)

    def body(r_ref, o_ref):
        acc = r_ref[0].astype(F32)
        for j in range(1, N_CHIPS):
            acc = acc + r_ref[j].astype(F32)
        o_ref[...] = acc

    return pl.pallas_call(
        body, name=name, out_shape=jax.ShapeDtypeStruct((r, cdim), F32), grid=(r // tr,),
        in_specs=[pl.BlockSpec((N_CHIPS, tr, cdim), lambda i: (0, i, 0))],
        out_specs=pl.BlockSpec((tr, cdim), lambda i: (i, 0)),
        compiler_params=_params(("parallel",)),
    )(r3)


def _adamw_math(w, g, m, v):
    m = ADAM_B1 * m + (1.0 - ADAM_B1) * g
    v = ADAM_B2 * v + (1.0 - ADAM_B2) * (g * g)
    m_hat = m / (1.0 - ADAM_B1 ** ADAM_STEP)
    v_hat = v / (1.0 - ADAM_B2 ** ADAM_STEP)
    delta = -ADAM_LR * (m_hat / (jnp.sqrt(v_hat) + ADAM_EPS) + ADAM_WD * w)
    return delta, m, v


def _adamw(w, g, m, v, name):
    rows, cols = w.shape
    tr = _tile(rows, 256, SUBLANES)

    def body(w_ref, g_ref, m_ref, v_ref, d_ref, nm_ref, nv_ref):
        d, nm, nv = _adamw_math(w_ref[...], g_ref[...], m_ref[...], v_ref[...])
        d_ref[...] = d
        nm_ref[...] = nm
        nv_ref[...] = nv

    spec = pl.BlockSpec((tr, cols), lambda i: (i, 0))
    shp = jax.ShapeDtypeStruct((rows, cols), F32)
    return pl.pallas_call(
        body, name=name, out_shape=[shp] * 3, grid=(rows // tr,), in_specs=[spec] * 4, out_specs=[spec] * 3,
        compiler_params=_params(("parallel",)),
    )(w, g, m, v)


def _adamw_small(w, parts, m, v, name):
    rows, cols = w.shape
    p3 = parts.reshape(N_DEV, rows, cols)

    def body(w_ref, p_ref, m_ref, v_ref, g_ref, d_ref, nm_ref, nv_ref):
        g = p_ref[0]
        for k in range(1, N_DEV):
            g = g + p_ref[k]
        d, nm, nv = _adamw_math(w_ref[...], g, m_ref[...], v_ref[...])
        g_ref[...] = g
        d_ref[...] = d
        nm_ref[...] = nm
        nv_ref[...] = nv

    shp = jax.ShapeDtypeStruct((rows, cols), F32)
    return pl.pallas_call(body, name=name, out_shape=[shp] * 4)(w, p3, m, v)


SMALL = ("norm_w", "q_norm_w", "k_norm_w", "sinks", "A_re", "A_im", "log_dt", "B_re", "B_im", "C_re", "C_im",
         "D_skip", "b_glu")
LARGE = ("w_in", "w_attn_proj", "w_glu", "w_ssm_proj", "w_out")
ORDER = ("norm_w", "w_in", "q_norm_w", "k_norm_w", "sinks", "w_attn_proj", "A_re", "A_im", "log_dt", "B_re", "B_im",
         "C_re", "C_im", "D_skip", "w_glu", "b_glu", "w_ssm_proj", "w_out")


SMALL_REST = ("loss",) + SMALL[1:]


def _pack(named, keys):
    flat = jnp.concatenate([named[k].reshape(-1).astype(F32) for k in keys])
    n = flat.shape[0]
    rows = -(-n // (LANES * SUBLANES)) * SUBLANES
    return jnp.pad(flat, (0, rows * LANES - n)).reshape(rows, LANES)


def _unpack(packed, like, keys):
    flat = packed.reshape(-1)
    out, o = {}, 0
    for k in keys:
        n = like[k].size
        out[k] = flat[o:o + n].reshape(like[k].shape)
        o += n
    return out


def _step(xs, target, p, shards):
    s_in, s_ap, s_glu, s_sp, s_o = shards
    (w_in_t,) = _exchange(_all_gather([s_in]), "gather_w_in")
    seq, d = xs.shape
    attn_w = (d // 128) * HEAD_DIM
    n_q = attn_w // HEAD_DIM
    kv_w = N_KV_HEADS * HEAD_DIM
    ssm_w = d // 2
    n_groups = ssm_w // GROUP
    n_state = n_groups * STATE
    in_w = w_in_t.shape[0]
    assert in_w == 2 * attn_w + 2 * kv_w + 2 * ssm_w + 2 * d
    o_u = 2 * attn_w + 2 * kv_w
    o_z = o_u + ssm_w
    o_ga = o_z + ssm_w
    chunk = min(BLOCK, seq)
    cw = d // 4

    norm_row = p["norm_w"].reshape(1, d)
    h = _rmsnorm_fwd(xs, norm_row, "rmsnorm_fwd")
    proj, (w_ap_t, w_glu_t, w_sp_t, w_o) = _matmul(h, w_in_t, mode="nt", name="in_proj", tn=512,
                                                   rider=_all_gather([s_ap, s_glu, s_sp, s_o]))
    qw_row, kw_row = p["q_norm_w"].reshape(1, HEAD_DIM), p["k_norm_w"].reshape(1, HEAD_DIM)
    ag = _attention_fwd(proj, qw_row, kw_row, p["sinks"], attn_w=attn_w, kv_w=kv_w, name="attention_fwd")

    log_dt_col = p["log_dt"].reshape(n_groups, 1)
    prep = _ssm_prep(p["A_re"], p["A_im"], log_dt_col, chunk // SUBLANES, "ssm_prep")
    rows_p = [v.reshape(1, n_state) for v in prep]
    blocks = _ssm_blocks(p["B_re"], p["B_im"], p["C_re"], p["C_im"])
    d_row = p["D_skip"].reshape(1, ssm_w)
    u_seg = _to_segments(proj[:, o_u:o_u + ssm_w], chunk)
    y_seg, st_re, st_im = _ssm_fwd(u_seg, blocks, rows_p, d_row, chunk=chunk, name="ssm_fwd")
    y_ssm = _from_segments(y_seg, chunk)
    (yg,) = _ew(_gelu, name="gelu", rows=seq, width=ssm_w, tiles=[(y_ssm, 0)], outs=[(BF16, ssm_w, 0)], cw=cw)
    glu = _matmul(yg, w_glu_t, mode="nt", name="glu_proj", bias=p["b_glu"].reshape(1, 2 * ssm_w))
    (ts,) = _ew(lambda ga, gb, z: ga * _sigmoid(gb) * _silu(z), name="glu_gate", rows=seq, width=ssm_w,
                tiles=[(glu, 0), (glu, ssm_w), (proj, o_z)], outs=[(BF16, ssm_w, 0)], cw=cw)
    yy = _matmul(ag, w_ap_t, mode="nt", name="attn_proj", out_cols=(2 * d, 0))
    yy = _matmul(ts, w_sp_t, mode="nt", name="ssm_proj", out_cols=(2 * d, d), into=yy)
    (merged,) = _ew(lambda ya, ys, ga, gs: _sigmoid(ga) * ya + _sigmoid(gs) * ys, name="merge", rows=seq, width=d,
                    tiles=[(yy, 0), (yy, d), (proj, o_ga), (proj, o_ga + d)], outs=[(BF16, d, 0)], cw=cw)
    mm = _matmul(merged, w_o, mode="nn", name="out_proj")
    dout, dout_b, loss_cols = _loss_grad(xs, mm, target, "loss_grad")
    loss_local = jnp.sum(loss_cols)

    g_w_o = _matmul(merged, dout_b, mode="tn", name="grad_w_out", tk=1024)
    dmerged, (sib_o,) = _matmul(dout_b, w_o, mode="nt", name="d_merged", rider=_sibling_exchange([g_w_o]))
    pair_o = _pair_sum(g_w_o, sib_o, "pair_sum_w_out")

    def merge_bwd(dm, y, g):
        s = _sigmoid(g)
        return dm * s, dm * y * s * (1.0 - s)

    dyy, dproj = _ew(merge_bwd, name="merge_bwd", rows=seq, width=2 * d,
                     tiles=[(dmerged, 0, d), (yy, 0), (proj, o_ga)],
                     outs=[(BF16, 2 * d, 0), (BF16, in_w, o_ga)], cw=cw)
    dy_a, dy_s = Cols(dyy, 0, d), Cols(dyy, d, d)
    g_w_ap_t = _matmul(dy_a, ag, mode="tn", name="grad_w_attn_proj", tk=1024)
    g_w_sp_t = _matmul(dy_s, ts, mode="tn", name="grad_w_ssm_proj", tk=1024)
    d_ag = _matmul(dy_a, w_ap_t, mode="nn", name="d_attn_gated")
    d_ts = _matmul(dy_s, w_sp_t, mode="nn", name="d_ssm_gated")

    (dproj, dkv, g_qw, g_kw, g_sinks), (chips_o, sib_ap, sib_sp) = _attention_bwd(
        proj, d_ag, dproj, qw_row, kw_row, p["sinks"], attn_w=attn_w, kv_w=kv_w, name="attention_bwd",
        rider=_join(_chip_exchange([pair_o]), _sibling_exchange([g_w_ap_t, g_w_sp_t])))
    pair_ap = _pair_sum(g_w_ap_t, sib_ap, "pair_sum_w_attn_proj")
    pair_sp = _pair_sum(g_w_sp_t, sib_sp, "pair_sum_w_ssm_proj")
    dproj = _attention_dkv(dproj, dkv, attn_w=attn_w, kv_w=kv_w, name="attention_dkv")

    n_half = ssm_w // _tile(2 * ssm_w, cw)

    def glu_bwd(j, dt, ga, gb, z):
        sb, sz = _sigmoid(gb), _silu(z)
        dg = jnp.where(j < n_half, dt * sb * sz, dt * ga * sb * (1.0 - sb) * sz)
        return dg, dg

    glu_ops = [(d_ts, 0, ssm_w), (glu, 0, ssm_w), (glu, ssm_w, ssm_w), (proj, o_z, ssm_w)]
    dglu, g_bglu = _ew(glu_bwd, name="glu_bwd", rows=seq, width=2 * ssm_w, tiles=glu_ops,
                       outs=[(BF16, 2 * ssm_w, 0)], accs=1, cw=cw, with_col=True)
    (dproj,) = _ew(lambda dt, ga, gb, z: dt * ga * _sigmoid(gb) * _dsilu(z), name="glu_bwd_z", rows=seq,
                   width=ssm_w, tiles=glu_ops, outs=[(BF16, in_w, o_z)], into=[dproj], cw=cw)
    g_w_glu_t = _matmul(dglu, yg, mode="tn", name="grad_w_glu", tk=1024)
    d_yg = _matmul(dglu, w_glu_t, mode="nn", name="d_gelu")
    ((du_seg, db_re, db_im, dc_re, dc_im, gabr, gabi, gcfr, gcfi, g_d), (chips_ap, chips_sp, sib_glu)) = _ssm_bwd(
        u_seg, y_seg, _to_segments(d_yg, chunk), st_re, st_im, blocks, rows_p, d_row, chunk=chunk, name="ssm_bwd",
        rider=_join(_chip_exchange([pair_ap, pair_sp]), _sibling_exchange([g_w_glu_t])))
    pair_glu = _pair_sum(g_w_glu_t, sib_glu, "pair_sum_w_glu")
    (dproj,) = _ew(lambda v: v, name="du_store", rows=seq, width=ssm_w, tiles=[(_from_segments(du_seg, chunk), 0)],
                   outs=[(BF16, in_w, o_u)], into=[dproj], cw=cw)
    g_w_in_t, (chips_glu,) = _matmul(dproj, h, mode="tn", name="grad_w_in", tk=1024,
                                     rider=_chip_exchange([pair_glu]))
    g_a_re, g_a_im, g_log_dt = _ssm_param_bwd(
        p["A_re"], p["A_im"], log_dt_col, *[g.reshape(SUBLANES, n_groups, STATE) for g in (gabr, gabi, gcfr, gcfi)],
        "ssm_param_bwd")
    small_grads = dict(
        loss=loss_local, q_norm_w=g_qw.reshape(HEAD_DIM), k_norm_w=g_kw.reshape(HEAD_DIM),
        sinks=g_sinks[0, :n_q], A_re=g_a_re, A_im=g_a_im, log_dt=g_log_dt.reshape(n_groups),
        B_re=_unblock_b(db_re), B_im=_unblock_b(db_im), C_re=_unblock_c(dc_re), C_im=_unblock_c(dc_im),
        D_skip=g_d.reshape(n_groups, GROUP), b_glu=g_bglu.reshape(2 * ssm_w))
    dh, (sib_in,) = _matmul(dproj, w_in_t, mode="nn", name="d_normed", tk=2176,
                            rider=_sibling_exchange([g_w_in_t]))
    pair_in = _pair_sum(g_w_in_t, sib_in, "pair_sum_w_in")
    (grad_x, g_norm), (chips_in, small_parts) = _rmsnorm_bwd(
        xs, norm_row, dh, dout, "rmsnorm_bwd",
        rider=_join(_chip_exchange([pair_in]), _all_gather([_pack(small_grads, SMALL_REST)])))
    (norm_parts,) = _exchange(_all_gather([_pack(dict(norm_w=g_norm), ("norm_w",))]), "gather_norm_grad")
    summed = [_chip_sum(c, "chip_sum_" + k) for k, c in zip(LARGE, (chips_in, chips_ap, chips_glu, chips_sp, chips_o))]
    return grad_x, summed, small_parts, norm_parts


def kernel(x, norm_w, w_in, q_norm_w, k_norm_w, sinks, w_attn_proj, A_re, A_im, log_dt, B_re, B_im, C_re, C_im, D_skip, w_glu, b_glu, w_ssm_proj, w_out, loss_target, m_norm_w, m_w_in, m_q_norm_w, m_k_norm_w, m_sinks, m_w_attn_proj, m_A_re, m_A_im, m_log_dt, m_B_re, m_B_im, m_C_re, m_C_im, m_D_skip, m_w_glu, m_b_glu, m_w_ssm_proj, m_w_out, v_norm_w, v_w_in, v_q_norm_w, v_k_norm_w, v_sinks, v_w_attn_proj, v_A_re, v_A_im, v_log_dt, v_B_re, v_B_im, v_C_re, v_C_im, v_D_skip, v_w_glu, v_b_glu, v_w_ssm_proj, v_w_out):
    weights = dict(norm_w=norm_w, w_in=w_in, q_norm_w=q_norm_w, k_norm_w=k_norm_w, sinks=sinks,
                   w_attn_proj=w_attn_proj, A_re=A_re, A_im=A_im, log_dt=log_dt, B_re=B_re, B_im=B_im, C_re=C_re,
                   C_im=C_im, D_skip=D_skip, w_glu=w_glu, b_glu=b_glu, w_ssm_proj=w_ssm_proj, w_out=w_out)
    m_in = dict(norm_w=m_norm_w, w_in=m_w_in, q_norm_w=m_q_norm_w, k_norm_w=m_k_norm_w, sinks=m_sinks,
                w_attn_proj=m_w_attn_proj, A_re=m_A_re, A_im=m_A_im, log_dt=m_log_dt, B_re=m_B_re, B_im=m_B_im,
                C_re=m_C_re, C_im=m_C_im, D_skip=m_D_skip, w_glu=m_w_glu, b_glu=m_b_glu, w_ssm_proj=m_w_ssm_proj,
                w_out=m_w_out)
    v_in = dict(norm_w=v_norm_w, w_in=v_w_in, q_norm_w=v_q_norm_w, k_norm_w=v_k_norm_w, sinks=v_sinks,
                w_attn_proj=v_w_attn_proj, A_re=v_A_re, A_im=v_A_im, log_dt=v_log_dt, B_re=v_B_re, B_im=v_B_im,
                C_re=v_C_re, C_im=v_C_im, D_skip=v_D_skip, w_glu=v_w_glu, b_glu=v_b_glu, w_ssm_proj=v_w_ssm_proj,
                w_out=v_w_out)

    _, seq, d = x.shape
    column_sharded = LARGE[:4]
    as_rows = lambda k, a: a.T if k in column_sharded else a
    shards = [as_rows(k, weights[k]).astype(BF16) for k in LARGE]
    small = {k: weights[k] for k in SMALL}
    grad_x, summed, small_parts, norm_parts = _step(x.reshape(seq, d), loss_target.reshape(seq, d), small, shards)

    grads, delta, new_m, new_v = {}, {}, {}, {}
    for k, g in zip(LARGE, summed):
        if k == "w_in":
            upd = _adamw(weights[k].T, g, m_in[k].T, v_in[k].T, "adamw_" + k)
            grads[k], delta[k], new_m[k], new_v[k] = [a.T for a in (g, *upd)]
        else:
            grads[k] = as_rows(k, g)
            delta[k], new_m[k], new_v[k] = _adamw(weights[k], grads[k], m_in[k], v_in[k], "adamw_" + k)

    zero = jnp.zeros((), F32)
    for keys, parts in ((SMALL_REST, small_parts), (("norm_w",), norm_parts)):
        like = dict(small, loss=zero)
        packs = [_pack(dict(src, loss=zero), keys) for src in (weights, m_in, v_in)]
        res = _adamw_small(packs[0], parts, packs[1], packs[2], "adamw_small_%d" % len(keys))
        for dst, r in zip((grads, delta, new_m, new_v), res):
            dst.update(_unpack(r, like, keys))
    loss = grads["loss"]

    return (loss, grad_x.reshape(x.shape), *[grads[k] for k in ORDER], *[delta[k] for k in ORDER],
            *[new_m[k] for k in ORDER], *[new_v[k] for k in ORDER])
```

```python
import math
from typing import Callable, NamedTuple

import jax
import jax.numpy as jnp
from jax import lax
from jax.experimental import pallas as pl
from jax.experimental.pallas import tpu as pltpu

F32 = jnp.float32
BF16 = jnp.bfloat16
MESH = pl.DeviceIdType.MESH

HEAD_DIM = 64
N_KV_HEADS = 4
GROUP = 16
STATE = 64
BLOCK = 128
NORM_EPS = 1e-6
N_DEV = 8
N_CHIPS = 4
LANES = 128
SUBLANES = 8
MXU_DIM = 256
VMEM_BYTES = 64 * 1024 * 1024
VMEM_CAP = VMEM_BYTES - 8 * 1024 * 1024

ADAM_LR = 0.001
ADAM_B1 = 0.9
ADAM_B2 = 0.999
ADAM_EPS = 1e-08
ADAM_WD = 0.01
ADAM_STEP = 10

GELU_C = math.sqrt(2.0 / math.pi)
GELU_K = 0.044715


def _tile(dim, pref, mult=LANES):
    if dim <= pref:
        return dim
    best = None
    for d in range(mult, pref + 1, mult):
        if dim % d == 0:
            best = d
    assert best is not None, (dim, pref, mult)
    return best


def _params(semantics=None, vmem=None):
    kw = {}
    if semantics is not None:
        kw["dimension_semantics"] = semantics
    if vmem is not None:
        kw["vmem_limit_bytes"] = int(min(VMEM_CAP, max(vmem, 32 * 1024 * 1024)))
    return pltpu.CompilerParams(**kw)


def _nbytes(shape, dtype):
    return math.prod(shape) * jnp.dtype(dtype).itemsize


def _sigmoid(x):
    return 1.0 / (1.0 + jnp.exp(-x))


def _silu(x):
    return x * _sigmoid(x)


def _dsilu(x):
    s = _sigmoid(x)
    return s * (1.0 + x * (1.0 - s))


def _gelu(x):
    return 0.5 * x * (1.0 + jnp.tanh(GELU_C * (x + GELU_K * x * x * x)))


def _dgelu(x):
    t = jnp.tanh(GELU_C * (x + GELU_K * x * x * x))
    return 0.5 * (1.0 + t) + 0.5 * x * (1.0 - t * t) * GELU_C * (1.0 + 3.0 * GELU_K * x * x)


def _dot(a, b, dims):
    return lax.dot_general(a, b, (dims, ((), ())), preferred_element_type=F32)


NN = ((1,), (0,))
NT = ((1,), (1,))
TN = ((0,), (0,))


def _any_spec():
    return pl.BlockSpec(memory_space=pl.ANY)


class Rider(NamedTuple):
    operands: tuple
    out_shapes: tuple
    sems: tuple
    start: Callable
    finish: Callable


def _all_gather(shards):
    n = len(shards)

    def copies(ins, outs, sems):
        send_sems, recv_sems, local_sems = sems
        x, y, c = lax.axis_index("x"), lax.axis_index("y"), lax.axis_index("c")
        me, sibling = (x, y, c), (x, y, 1 - c)
        chips = [(1 - x, y), (x, 1 - y), (1 - x, 1 - y)]

        def rows(k, px, py, pc):
            r = shards[k].shape[0]
            return outs[k].at[pl.ds((4 * px + 2 * py + pc) * r, r), :]

        def copy(k, s, block, to, src=None):
            return pltpu.make_async_remote_copy(
                src_ref=rows(k, *block) if src is None else src, dst_ref=rows(k, *block),
                send_sem=send_sems.at[7 * k + s], recv_sem=recv_sems.at[7 * k + s],
                device_id=to, device_id_type=MESH)

        mine = [pltpu.make_async_copy(ins[k], rows(k, *me), local_sems.at[k]) for k in range(n)]
        first = []
        for k in range(n):
            first.append(copy(k, 0, me, sibling, src=ins[k]))
            first += [copy(k, 1 + j, me, (*chip, c), src=ins[k]) for j, chip in enumerate(chips)]
        return me, sibling, chips, c, copy, mine, first

    def start(ins, outs, sems):
        *_, mine, first = copies(ins, outs, sems)
        for cp in mine + first:
            cp.start()

    def finish(ins, outs, sems):
        me, sibling, chips, c, copy, mine, first = copies(ins, outs, sems)
        passed = []
        for j, chip in enumerate(chips):
            for k in range(n):
                copy(k, 1 + j, (*chip, c), me).wait_recv()
                fwd = copy(k, 4 + j, (*chip, c), sibling)
                fwd.start()
                passed.append(fwd)
        for k in range(n):
            copy(k, 0, sibling, me).wait_recv()
            for j, chip in enumerate(chips):
                copy(k, 4 + j, (*chip, 1 - c), me).wait_recv()
        for cp in first + passed:
            cp.wait_send()
        for cp in mine:
            cp.wait()

    return Rider(
        tuple(shards),
        tuple(jax.ShapeDtypeStruct((N_DEV * s.shape[0], s.shape[1]), s.dtype) for s in shards),
        (pltpu.SemaphoreType.DMA((7 * n,)), pltpu.SemaphoreType.DMA((7 * n,)), pltpu.SemaphoreType.DMA((n,))),
        start, finish)


def _sibling_exchange(grads):
    n = len(grads)

    def copies(ins, outs, sems):
        send_sems, recv_sems = sems
        x, y, c = lax.axis_index("x"), lax.axis_index("y"), lax.axis_index("c")
        out = []
        for k in range(n):
            r = grads[k].shape[0] // N_DEV
            for j in range(N_CHIPS):
                out.append(pltpu.make_async_remote_copy(
                    src_ref=ins[k].at[pl.ds((2 * j + 1 - c) * r, r), :],
                    dst_ref=outs[k].at[pl.ds(j * r, r), :],
                    send_sem=send_sems.at[N_CHIPS * k + j], recv_sem=recv_sems.at[N_CHIPS * k + j],
                    device_id=(x, y, 1 - c), device_id_type=MESH))
        return out

    def start(ins, outs, sems):
        for cp in copies(ins, outs, sems):
            cp.start()

    def finish(ins, outs, sems):
        for cp in copies(ins, outs, sems):
            cp.wait()

    return Rider(
        tuple(grads), tuple(jax.ShapeDtypeStruct((g.shape[0] // 2, g.shape[1]), g.dtype) for g in grads),
        (pltpu.SemaphoreType.DMA((N_CHIPS * n,)), pltpu.SemaphoreType.DMA((N_CHIPS * n,))), start, finish)


def _chip_exchange(parts):
    n = len(parts)

    def copies(ins, outs, sems):
        send_sems, recv_sems, local_sems = sems
        x, y, c = lax.axis_index("x"), lax.axis_index("y"), lax.axis_index("c")
        my_chip = 2 * x + y
        chips = [(1 - x, y), (x, 1 - y), (1 - x, 1 - y)]
        local, sent = [], []
        for k in range(n):
            r = parts[k].shape[0] // N_CHIPS
            mine = pl.ds(my_chip * r, r)
            local.append(pltpu.make_async_copy(ins[k].at[mine, :], outs[k].at[mine, :], local_sems.at[k]))
            for s, (px, py) in enumerate(chips):
                sent.append(pltpu.make_async_remote_copy(
                    src_ref=ins[k].at[pl.ds((2 * px + py) * r, r), :], dst_ref=outs[k].at[mine, :],
                    send_sem=send_sems.at[3 * k + s], recv_sem=recv_sems.at[3 * k + s],
                    device_id=(px, py, c), device_id_type=MESH))
        return local, sent

    def start(ins, outs, sems):
        local, sent = copies(ins, outs, sems)
        for cp in local + sent:
            cp.start()

    def finish(ins, outs, sems):
        local, sent = copies(ins, outs, sems)
        for cp in sent + local:
            cp.wait()

    return Rider(
        tuple(parts), tuple(jax.ShapeDtypeStruct(p.shape, p.dtype) for p in parts),
        (pltpu.SemaphoreType.DMA((3 * n,)), pltpu.SemaphoreType.DMA((3 * n,)), pltpu.SemaphoreType.DMA((n,))),
        start, finish)


def _join(*riders):
    cuts_in, cuts_out, cuts_sem = [0], [0], [0]
    for r in riders:
        cuts_in.append(cuts_in[-1] + len(r.operands))
        cuts_out.append(cuts_out[-1] + len(r.out_shapes))
        cuts_sem.append(cuts_sem[-1] + len(r.sems))

    def each(which):
        def run(ins, outs, sems):
            for i, r in enumerate(riders):
                getattr(r, which)(ins[cuts_in[i]:cuts_in[i + 1]], outs[cuts_out[i]:cuts_out[i + 1]],
                                  sems[cuts_sem[i]:cuts_sem[i + 1]])
        return run

    return Rider(sum((r.operands for r in riders), ()), sum((r.out_shapes for r in riders), ()),
                 sum((r.sems for r in riders), ()), each("start"), each("finish"))


def _call(body, operands, *, name, out_shape, grid, in_specs, out_specs, scratch_shapes=(), aliases=None,
          semantics=None, vmem=None, rider=None):
    operands, out_shape, scratch_shapes = list(operands), list(out_shape), list(scratch_shapes)
    in_specs, out_specs = list(in_specs), list(out_specs)
    if rider is None:
        res = pl.pallas_call(
            body, name=name, out_shape=out_shape, grid=grid, in_specs=in_specs, out_specs=out_specs,
            scratch_shapes=scratch_shapes, input_output_aliases=aliases or {},
            compiler_params=_params(semantics, vmem))(*operands)
        return list(res), []
    n_in, n_out, n_scr = len(operands), len(out_shape), len(scratch_shapes)
    ri, ro = len(rider.operands), len(rider.out_shapes)

    def carried(*refs):
        a, b = n_in, n_in + ri
        c, d = b + n_out, b + n_out + ro
        e = d + n_scr
        ids = [pl.program_id(k) for k in range(len(grid))]
        first = ids[0] == 0
        last = ids[0] == grid[0] - 1
        for k in range(1, len(grid)):
            first = jnp.logical_and(first, ids[k] == 0)
            last = jnp.logical_and(last, ids[k] == grid[k] - 1)

        @pl.when(first)
        def _():
            rider.start(refs[a:b], refs[c:d], refs[e:])

        body(*refs[:a], *refs[b:c], *refs[d:e])

        @pl.when(last)
        def _():
            rider.finish(refs[a:b], refs[c:d], refs[e:])

    res = pl.pallas_call(
        carried, name=name, out_shape=out_shape + list(rider.out_shapes), grid=grid,
        in_specs=in_specs + [_any_spec()] * ri, out_specs=out_specs + [_any_spec()] * ro,
        scratch_shapes=scratch_shapes + list(rider.sems), input_output_aliases=aliases or {},
        compiler_params=_params(("arbitrary",) * len(grid), vmem))(*operands, *rider.operands)
    return list(res[:n_out]), list(res[n_out:])


def _exchange(rider, name):
    ri, ro = len(rider.operands), len(rider.out_shapes)

    def body(*refs):
        rider.start(refs[:ri], refs[ri:ri + ro], refs[ri + ro:])
        rider.finish(refs[:ri], refs[ri:ri + ro], refs[ri + ro:])

    return pl.pallas_call(
        body, name=name, out_shape=list(rider.out_shapes), in_specs=[_any_spec()] * ri,
        out_specs=[_any_spec()] * ro, scratch_shapes=list(rider.sems))(*rider.operands)


class Cols(NamedTuple):
    arr: jax.Array
    off: int
    width: int


def _cols(a):
    return a if isinstance(a, Cols) else Cols(a, 0, a.shape[1])


def _matmul(a, b, *, mode, name, out_dtype=F32, tm=1024, tn=1024, tk=2048, bias=None, out_cols=None, into=None,
            rider=None):
    a, b = _cols(a), _cols(b)
    if mode == "nn":
        (m, k), (k2, n) = (a.arr.shape[0], a.width), (b.arr.shape[0], b.width)
    elif mode == "nt":
        (m, k), (n, k2) = (a.arr.shape[0], a.width), (b.arr.shape[0], b.width)
    else:
        (k, m), (k2, n) = (a.arr.shape[0], a.width), (b.arr.shape[0], b.width)
    assert k == k2, (a.arr.shape, b.arr.shape, mode)
    tm, tn, tk = _tile(m, tm), _tile(n, tn), _tile(k, tk)
    nk = k // tk
    dims = {"nn": NN, "nt": NT, "tn": TN}[mode]
    if mode == "tn":
        assert a.off % tm == 0
        a_spec = pl.BlockSpec((tk, tm), lambda i, j, kk, o=a.off // tm: (kk, i + o))
    else:
        assert a.off % tk == 0
        a_spec = pl.BlockSpec((tm, tk), lambda i, j, kk, o=a.off // tk: (i, kk + o))
    if mode == "nt":
        assert b.off % tk == 0
        b_spec = pl.BlockSpec((tn, tk), lambda i, j, kk, o=b.off // tk: (j, kk + o))
    else:
        assert b.off % tn == 0
        b_spec = pl.BlockSpec((tk, tn), lambda i, j, kk, o=b.off // tn: (kk, j + o))
    in_specs, operands = [a_spec, b_spec], [a.arr, b.arr]
    if bias is not None:
        in_specs.append(pl.BlockSpec((1, tn), lambda i, j, kk: (0, j)))
        operands.append(bias)
    total_w, o_off = out_cols if out_cols is not None else (n, 0)
    assert o_off % tn == 0
    aliases = {}
    if into is not None:
        assert into.shape == (m, total_w) and into.dtype == out_dtype
        in_specs.append(_any_spec())
        operands.append(into)
        aliases = {len(operands) - 1: 0}
    n_in = len(operands)

    def body(*refs):
        a_ref, b_ref = refs[0], refs[1]
        bias_ref = refs[2] if bias is not None else None
        o_ref = refs[n_in]
        acc_ref = refs[-1] if nk > 1 else None
        part = _dot(a_ref[...].astype(BF16), b_ref[...].astype(BF16), dims)

        def finish(acc):
            if bias_ref is not None:
                acc = acc + bias_ref[...]
            o_ref[...] = acc.astype(out_dtype)

        if nk == 1:
            finish(part)
        else:
            kk = pl.program_id(2)

            @pl.when(kk == 0)
            def _():
                acc_ref[...] = part

            @pl.when(kk > 0)
            def _():
                acc_ref[...] += part

            @pl.when(kk == nk - 1)
            def _():
                finish(acc_ref[...])

    vmem = 2 * (_nbytes((tm, tk), a.arr.dtype) + _nbytes((tk, tn), b.arr.dtype) + _nbytes((tm, tn), out_dtype))
    vmem += 3 * _nbytes((tm, tn), F32)
    (out,), landed = _call(
        body, operands, name=name, out_shape=[jax.ShapeDtypeStruct((m, total_w), out_dtype)],
        grid=(m // tm, n // tn, nk), in_specs=in_specs,
        out_specs=[pl.BlockSpec((tm, tn), lambda i, j, kk, o=o_off // tn: (i, j + o))],
        scratch_shapes=[pltpu.VMEM((tm, tn), F32)] if nk > 1 else [], aliases=aliases,
        semantics=("parallel", "parallel", "arbitrary"), vmem=vmem, rider=rider)
    return out if rider is None else (out, landed)


def _ew(fn, *, name, rows, width, tiles, vecs=(), outs, accs=0, tl=512, cw=512, into=None, with_col=False):
    tl, cw = _tile(rows, tl, SUBLANES), _tile(width, cw)
    ncol = width // cw
    nt_, nv = len(tiles), len(vecs)
    into = list(into) if into is not None else [None] * len(outs)
    aliased = [t for t in into if t is not None]

    def off(o):
        assert o % cw == 0, (name, o, cw)
        return o // cw

    in_specs, vmem = [], 0
    for t in tiles:
        arr, o = t[0], off(t[1])
        wrap = t[2] // cw if len(t) > 2 else ncol
        in_specs.append(pl.BlockSpec((tl, cw), lambda j, i, o=o, wrap=wrap: (i, o + j % wrap)))
        vmem += _nbytes((tl, cw), arr.dtype)
    in_specs += [pl.BlockSpec((1, cw), lambda j, i, o=off(o): (0, j + o)) for _, o in vecs]
    in_specs += [_any_spec() for _ in aliased]
    out_shape, out_specs, aliases = [], [], {}
    n_in = nt_ + nv
    for idx, ((dt, tw, o), tgt) in enumerate(zip(outs, into)):
        out_shape.append(jax.ShapeDtypeStruct((rows, tw), dt))
        out_specs.append(pl.BlockSpec((tl, cw), lambda j, i, o=off(o): (i, j + o)))
        vmem += _nbytes((tl, cw), dt)
        if tgt is not None:
            assert tgt.shape == (rows, tw) and tgt.dtype == dt, (name, tgt.shape, tgt.dtype)
            aliases[n_in + len(aliases)] = idx
    for _ in range(accs):
        out_shape.append(jax.ShapeDtypeStruct((1, width), F32))
        out_specs.append(pl.BlockSpec((1, cw), lambda j, i: (0, j)))
    n_out = len(outs)

    def body(*refs):
        vals = [r[...] for r in refs[:n_in]]
        out_refs = refs[n_in + len(aliased):]
        res = fn(pl.program_id(0), *vals) if with_col else fn(*vals)
        res = res if isinstance(res, (tuple, list)) else (res,)
        assert len(res) == n_out + accs, (name, len(res))
        for r, v in zip(out_refs[:n_out], res[:n_out]):
            r[...] = v.astype(r.dtype)
        first = pl.program_id(1) == 0
        for r, v in zip(out_refs[n_out:], res[n_out:]):
            s = jnp.sum(v, axis=0, keepdims=True)

            @pl.when(first)
            def _(r=r, s=s):
                r[...] = s

            @pl.when(jnp.logical_not(first))
            def _(r=r, s=s):
                r[...] += s

    return pl.pallas_call(
        body, name=name, out_shape=out_shape, grid=(ncol, rows // tl),
        in_specs=in_specs, out_specs=out_specs, input_output_aliases=aliases,
        compiler_params=_params(("parallel", "arbitrary"), 3 * vmem),
    )(*[t[0] for t in tiles], *[v for v, _ in vecs], *aliased)


def _rmsnorm_fwd(x, w_row, name):
    rows, d = x.shape
    tl = _tile(rows, 512, SUBLANES)

    def body(x_ref, w_ref, h_ref):
        xv = x_ref[...]
        rstd = lax.rsqrt(jnp.mean(xv * xv, axis=-1, keepdims=True) + NORM_EPS)
        h_ref[...] = (xv * rstd * w_ref[...]).astype(BF16)

    return pl.pallas_call(
        body, name=name, out_shape=jax.ShapeDtypeStruct((rows, d), BF16), grid=(rows // tl,),
        in_specs=[pl.BlockSpec((tl, d), lambda i: (i, 0)), pl.BlockSpec((1, d), lambda i: (0, 0))],
        out_specs=pl.BlockSpec((tl, d), lambda i: (i, 0)),
        compiler_params=_params(("parallel",)),
    )(x, w_row)


def _rmsnorm_bwd(x, w_row, dh, dout, name, rider=None):
    rows, d = x.shape
    tl = _tile(rows, 256, SUBLANES)

    def body(x_ref, w_ref, dh_ref, dout_ref, gx_ref, gw_ref):
        xv = x_ref[...]
        rstd = lax.rsqrt(jnp.mean(xv * xv, axis=-1, keepdims=True) + NORM_EPS)
        xn = xv * rstd
        dhv = dh_ref[...]
        dxn = dhv * w_ref[...]
        dx = rstd * (dxn - xn * jnp.mean(dxn * xn, axis=-1, keepdims=True))
        gx_ref[...] = dout_ref[...] + dx
        gw = jnp.sum(dhv * xn, axis=0, keepdims=True)

        @pl.when(pl.program_id(0) == 0)
        def _():
            gw_ref[...] = gw

        @pl.when(pl.program_id(0) > 0)
        def _():
            gw_ref[...] += gw

    tile = pl.BlockSpec((tl, d), lambda i: (i, 0))
    row = pl.BlockSpec((1, d), lambda i: (0, 0))
    res, landed = _call(
        body, [x, w_row, dh, dout], name=name,
        out_shape=[jax.ShapeDtypeStruct((rows, d), F32), jax.ShapeDtypeStruct((1, d), F32)],
        grid=(rows // tl,), in_specs=[tile, row, tile, tile], out_specs=[tile, row],
        semantics=("arbitrary",), rider=rider)
    return res if rider is None else (res, landed)


def _head_norm(v, w_row):
    rstd = lax.rsqrt(jnp.mean(v * v, axis=-1, keepdims=True) + NORM_EPS)
    vn = v * rstd
    return vn, vn * w_row, rstd


def _attn_specs(attn_w, kv_w):
    half = attn_w // 2
    kcol, vcol = attn_w // kv_w, attn_w // kv_w + 1
    gcol = (attn_w + 2 * kv_w) // half
    prev = lambda i: jnp.maximum(i - 1, 0)
    return [
        pl.BlockSpec((BLOCK, attn_w), lambda i: (i, 0)),
        pl.BlockSpec((BLOCK, kv_w), lambda i: (prev(i), kcol)),
        pl.BlockSpec((BLOCK, kv_w), lambda i: (i, kcol)),
        pl.BlockSpec((BLOCK, kv_w), lambda i: (prev(i), vcol)),
        pl.BlockSpec((BLOCK, kv_w), lambda i: (i, vcol)),
        pl.BlockSpec((BLOCK, half), lambda i: (i, gcol)),
        pl.BlockSpec((BLOCK, half), lambda i: (i, gcol + 1)),
    ]


def _band_mask(i):
    q_loc = lax.broadcasted_iota(jnp.int32, (BLOCK, 2 * BLOCK), 0) + BLOCK
    k_loc = lax.broadcasted_iota(jnp.int32, (BLOCK, 2 * BLOCK), 1)
    diff = q_loc - k_loc
    first_key = jnp.where(i == 0, BLOCK, 0)
    return (diff >= 0) & (diff < BLOCK) & (k_loc >= first_key)


def _softmax_with_sink(s, sink):
    m = jnp.maximum(jnp.max(s, axis=-1, keepdims=True), sink)
    p = jnp.exp(s - m)
    e_sink = jnp.exp(sink - m)
    den = jnp.sum(p, axis=-1, keepdims=True) + e_sink
    inv = 1.0 / den
    return p * inv, e_sink * inv


def _attention_fwd(proj, qw_row, kw_row, sinks, *, attn_w, kv_w, name):
    rows = proj.shape[0]
    n_q = attn_w // HEAD_DIM
    per_kv = n_q // N_KV_HEADS
    scale = 1.0 / math.sqrt(HEAD_DIM)

    def body(q_ref, kp_ref, kc_ref, vp_ref, vc_ref, glo_ref, ghi_ref, qw_ref, kw_ref, sink_ref, o_ref):
        i = pl.program_id(0)
        valid = _band_mask(i)
        q = q_ref[...]
        kk = jnp.concatenate([kp_ref[...], kc_ref[...]], axis=0)
        vv = jnp.concatenate([vp_ref[...], vc_ref[...]], axis=0)
        gate = jnp.concatenate([glo_ref[...], ghi_ref[...]], axis=1)
        heads = []
        for g in range(N_KV_HEADS):
            sl = slice(g * HEAD_DIM, (g + 1) * HEAD_DIM)
            _, kh, _ = _head_norm(kk[:, sl], kw_ref[...])
            kh = kh.astype(BF16)
            vh = vv[:, sl].astype(BF16)
            for r in range(per_kv):
                h = g * per_kv + r
                hs = slice(h * HEAD_DIM, (h + 1) * HEAD_DIM)
                _, qh, _ = _head_norm(q[:, hs], qw_ref[...])
                s = _dot(qh.astype(BF16), kh, NT) * scale
                s = jnp.where(valid, s, -1e30)
                p, _ = _softmax_with_sink(s, sink_ref[h])
                heads.append(_dot(p.astype(BF16), vh, NN))
        attn = jnp.concatenate(heads, axis=1)
        o_ref[...] = (attn * _silu(gate)).astype(BF16)

    vec = pl.BlockSpec((1, HEAD_DIM), lambda i: (0, 0))
    return pl.pallas_call(
        body, name=name, out_shape=jax.ShapeDtypeStruct((rows, attn_w), BF16), grid=(rows // BLOCK,),
        in_specs=_attn_specs(attn_w, kv_w) + [vec, vec, pl.BlockSpec(memory_space=pltpu.SMEM)],
        out_specs=pl.BlockSpec((BLOCK, attn_w), lambda i: (i, 0)),
        compiler_params=_params(("parallel",)),
    )(proj, proj, proj, proj, proj, proj, proj, qw_row, kw_row, sinks)


def _attention_bwd(proj, d_ag, dproj, qw_row, kw_row, sinks, *, attn_w, kv_w, name, rider=None):
    rows = proj.shape[0]
    nb = rows // BLOCK
    n_q = attn_w // HEAD_DIM
    per_kv = n_q // N_KV_HEADS
    scale = 1.0 / math.sqrt(HEAD_DIM)
    w_out = 2 * attn_w + 2 * kv_w

    def body(q_ref, kp_ref, kc_ref, vp_ref, vc_ref, glo_ref, ghi_ref, dag_ref, qw_ref, kw_ref, sink_ref, _,
             dp_ref, dkv_ref, gqw_ref, gkw_ref, gs_ref):
        i = pl.program_id(0)
        valid = _band_mask(i)
        q = q_ref[...]
        kk = jnp.concatenate([kp_ref[...], kc_ref[...]], axis=0)
        vv = jnp.concatenate([vp_ref[...], vc_ref[...]], axis=0)
        gate = jnp.concatenate([glo_ref[...], ghi_ref[...]], axis=1)
        d_ag_v = dag_ref[...]
        qw, kw = qw_ref[...], kw_ref[...]
        lane = lax.broadcasted_iota(jnp.int32, (SUBLANES, LANES), 1)
        sub = lax.broadcasted_iota(jnp.int32, (SUBLANES, LANES), 0)
        gqw = jnp.zeros((1, HEAD_DIM), F32)
        gkw = jnp.zeros((1, HEAD_DIM), F32)
        gsink = jnp.zeros((SUBLANES, LANES), F32)
        dq_heads, dgate_heads, dk_heads, dv_heads = [], [], [], []
        for g in range(N_KV_HEADS):
            sl = slice(g * HEAD_DIM, (g + 1) * HEAD_DIM)
            kn, kh, k_rstd = _head_norm(kk[:, sl], kw)
            kh = kh.astype(BF16)
            vh = vv[:, sl].astype(BF16)
            dkh = jnp.zeros((2 * BLOCK, HEAD_DIM), F32)
            dvh = jnp.zeros((2 * BLOCK, HEAD_DIM), F32)
            for r in range(per_kv):
                h = g * per_kv + r
                hs = slice(h * HEAD_DIM, (h + 1) * HEAD_DIM)
                qn, qh, q_rstd = _head_norm(q[:, hs], qw)
                qh = qh.astype(BF16)
                s = _dot(qh, kh, NT) * scale
                s = jnp.where(valid, s, -1e30)
                p, p_sink = _softmax_with_sink(s, sink_ref[h])
                pb = p.astype(BF16)
                o = _dot(pb, vh, NN)
                gate_h = gate[:, hs]
                d_ag_h = d_ag_v[:, hs]
                dgate_heads.append(d_ag_h * o * _dsilu(gate_h))
                do = (d_ag_h * _silu(gate_h)).astype(BF16)
                dp = _dot(do, vh, NT)
                delta = jnp.sum(p * dp, axis=-1, keepdims=True)
                ds = (p * (dp - delta) * scale).astype(BF16)
                gs_h = jnp.sum(-p_sink * delta, axis=0, keepdims=True)
                gsink = gsink + jnp.where((lane == h) & (sub == 0), gs_h, 0.0)
                dvh = dvh + _dot(pb, do, TN)
                dkh = dkh + _dot(ds, qh, TN)
                dqh = _dot(ds, kh, NN)
                gqw = gqw + jnp.sum(dqh * qn, axis=0, keepdims=True)
                dqn = dqh * qw
                dq_heads.append(q_rstd * (dqn - qn * jnp.mean(dqn * qn, axis=-1, keepdims=True)))
            gkw = gkw + jnp.sum(dkh * kn, axis=0, keepdims=True)
            dkn = dkh * kw
            dk_heads.append(k_rstd * (dkn - kn * jnp.mean(dkn * kn, axis=-1, keepdims=True)))
            dv_heads.append(dvh)
        dp_ref[:, 0:attn_w] = jnp.concatenate(dq_heads, axis=1).astype(BF16)
        dp_ref[:, attn_w:attn_w + 2 * kv_w] = jnp.zeros((BLOCK, 2 * kv_w), BF16)
        dp_ref[:, attn_w + 2 * kv_w:w_out] = jnp.concatenate(dgate_heads, axis=1).astype(BF16)
        dkv_ref[0] = jnp.concatenate(dk_heads + dv_heads, axis=1)

        @pl.when(i == 0)
        def _():
            gqw_ref[...] = gqw
            gkw_ref[...] = gkw
            gs_ref[...] = gsink

        @pl.when(i > 0)
        def _():
            gqw_ref[...] += gqw
            gkw_ref[...] += gkw
            gs_ref[...] += gsink

    vec = pl.BlockSpec((1, HEAD_DIM), lambda i: (0, 0))
    res, landed = _call(
        body, [proj, proj, proj, proj, proj, proj, proj, d_ag, qw_row, kw_row, sinks, dproj], name=name,
        out_shape=[jax.ShapeDtypeStruct(dproj.shape, BF16),
                   jax.ShapeDtypeStruct((nb, 2 * BLOCK, 2 * kv_w), F32),
                   jax.ShapeDtypeStruct((1, HEAD_DIM), F32), jax.ShapeDtypeStruct((1, HEAD_DIM), F32),
                   jax.ShapeDtypeStruct((SUBLANES, LANES), F32)],
        grid=(nb,),
        in_specs=_attn_specs(attn_w, kv_w) + [pl.BlockSpec((BLOCK, attn_w), lambda i: (i, 0)), vec, vec,
                                              pl.BlockSpec(memory_space=pltpu.SMEM), _any_spec()],
        out_specs=[pl.BlockSpec((BLOCK, w_out), lambda i: (i, 0)),
                   pl.BlockSpec((1, 2 * BLOCK, 2 * kv_w), lambda i: (i, 0, 0)),
                   vec, vec, pl.BlockSpec((SUBLANES, LANES), lambda i: (0, 0))],
        aliases={11: 0}, semantics=("arbitrary",), vmem=40 * 1024 * 1024, rider=rider)
    return res if rider is None else (res, landed)


def _attention_dkv(dproj, dkv, *, attn_w, kv_w, name):
    rows = dproj.shape[0]
    nb = rows // BLOCK
    col = attn_w // (2 * kv_w)

    def body(cur_ref, nxt_ref, _, o_ref):
        i = pl.program_id(0)
        nxt = jnp.where(i < nb - 1, nxt_ref[0, 0:BLOCK, :], 0.0)
        o_ref[...] = (cur_ref[0, BLOCK:2 * BLOCK, :] + nxt).astype(BF16)

    blk = lambda f: pl.BlockSpec((1, 2 * BLOCK, 2 * kv_w), f)
    return pl.pallas_call(
        body, name=name, out_shape=jax.ShapeDtypeStruct(dproj.shape, BF16), grid=(nb,),
        in_specs=[blk(lambda i: (i, 0, 0)), blk(lambda i: (jnp.minimum(i + 1, nb - 1), 0, 0)), _any_spec()],
        out_specs=pl.BlockSpec((BLOCK, 2 * kv_w), lambda i: (i, col)),
        input_output_aliases={2: 0},
        compiler_params=_params(("parallel",)),
    )(dkv, dkv, dproj)


def _cmul(ar, ai, br, bi):
    return ar * br - ai * bi, ar * bi + ai * br


def _ssm_prep(a_re, a_im, log_dt_col, steps, name):
    assert steps & (steps - 1) == 0

    def body(are_ref, aim_ref, ldt_ref, abr_ref, abi_ref, cfr_ref, cfi_ref, apr_ref, api_ref):
        are, aim = are_ref[...], aim_ref[...]
        dt = jnp.exp(ldt_ref[...])
        mag = jnp.exp(dt * are)
        abr = mag * jnp.cos(dt * aim)
        abi = mag * jnp.sin(dt * aim)
        num_re, num_im = abr - 1.0, abi
        den = are * are + aim * aim
        abr_ref[...] = abr
        abi_ref[...] = abi
        cfr_ref[...] = (num_re * are + num_im * aim) / den
        cfi_ref[...] = (num_im * are - num_re * aim) / den
        pr, pi = abr, abi
        n = steps
        while n > 1:
            pr, pi = _cmul(pr, pi, pr, pi)
            n //= 2
        apr_ref[...] = pr
        api_ref[...] = pi

    shp = jax.ShapeDtypeStruct(a_re.shape, F32)
    return pl.pallas_call(body, name=name, out_shape=[shp] * 6)(a_re, a_im, log_dt_col)


def _ssm_param_bwd(a_re, a_im, log_dt_col, d_ab_re, d_ab_im, d_cf_re, d_cf_im, name):
    def body(are_ref, aim_ref, ldt_ref, gabr_ref, gabi_ref, gcfr_ref, gcfi_ref, dar_ref, dai_ref, dldt_ref):
        are, aim = are_ref[...], aim_ref[...]
        dt = jnp.exp(ldt_ref[...])
        mag = jnp.exp(dt * are)
        abr = mag * jnp.cos(dt * aim)
        abi = mag * jnp.sin(dt * aim)
        den = are * are + aim * aim
        cfr = ((abr - 1.0) * are + abi * aim) / den
        cfi = (abi * are - (abr - 1.0) * aim) / den
        gabr, gabi = jnp.sum(gabr_ref[...], axis=0), jnp.sum(gabi_ref[...], axis=0)
        gcfr, gcfi = jnp.sum(gcfr_ref[...], axis=0), jnp.sum(gcfi_ref[...], axis=0)
        inv_r, inv_i = are / den, -aim / den
        t_r, t_i = _cmul(inv_r, -inv_i, gcfr, gcfi)
        gabr, gabi = gabr + t_r, gabi + t_i
        q_r, q_i = _cmul(cfr, cfi, inv_r, inv_i)
        da_r, da_i = _cmul(-q_r, q_i, gcfr, gcfi)
        gz_r, gz_i = _cmul(abr, -abi, gabr, gabi)
        dar_ref[...] = da_r + dt * gz_r
        dai_ref[...] = da_i + dt * gz_i
        dldt_ref[...] = dt * jnp.sum(are * gz_r + aim * gz_i, axis=-1, keepdims=True)

    shp = jax.ShapeDtypeStruct(a_re.shape, F32)
    return pl.pallas_call(body, name=name, out_shape=[shp, shp, jax.ShapeDtypeStruct(log_dt_col.shape, F32)])(
        a_re, a_im, log_dt_col, d_ab_re, d_ab_im, d_cf_re, d_cf_im)


SCAN_LANES = 512


def _scan_segments(xr_ref, xi_ref, a_re, a_im, ap_re, ap_im, carry_re, carry_im, cm_re, cm_im, steps, reverse):
    n = xr_ref.shape[1]
    order = range(steps - 1, -1, -1) if reverse else range(steps)
    seg_order = range(SUBLANES - 1, -1, -1) if reverse else range(SUBLANES)
    for c0 in range(0, n, SCAN_LANES):
        ls = slice(c0, c0 + SCAN_LANES)
        ar = jnp.broadcast_to(a_re[:, ls], (SUBLANES, SCAN_LANES))
        ai = jnp.broadcast_to(a_im[:, ls], (SUBLANES, SCAN_LANES))

        def local(t, s, ar=ar, ai=ai, ls=ls):
            j = steps - 1 - t if reverse else t
            r0 = pl.multiple_of(j * SUBLANES, SUBLANES)
            sr, si = _cmul(ar, ai, s[0], s[1])
            sr = sr + xr_ref[pl.ds(r0, SUBLANES), ls]
            si = si + xi_ref[pl.ds(r0, SUBLANES), ls]
            xr_ref[pl.ds(r0, SUBLANES), ls] = sr
            xi_ref[pl.ds(r0, SUBLANES), ls] = si
            return sr, si

        zero = jnp.zeros((SUBLANES, SCAN_LANES), F32)
        end_r, end_i = lax.fori_loop(0, steps, local, (zero, zero))
        cr, ci = carry_re[:, ls], carry_im[:, ls]
        apr, api = ap_re[:, ls], ap_im[:, ls]
        for r in seg_order:
            cm_re[r:r + 1, ls] = cr
            cm_im[r:r + 1, ls] = ci
            tr, ti = _cmul(apr, api, cr, ci)
            cr, ci = end_r[r:r + 1, :] + tr, end_i[r:r + 1, :] + ti
        carry_re[:, ls] = cr
        carry_im[:, ls] = ci

        def fix(t, s, ar=ar, ai=ai, ls=ls):
            j = steps - 1 - t if reverse else t
            r0 = pl.multiple_of(j * SUBLANES, SUBLANES)
            sr, si = _cmul(ar, ai, s[0], s[1])
            xr_ref[pl.ds(r0, SUBLANES), ls] += sr
            xi_ref[pl.ds(r0, SUBLANES), ls] += si
            return sr, si

        lax.fori_loop(0, steps, fix, (cm_re[:, ls], cm_im[:, ls]))
    del order


def _ssm_blocks(b_re, b_im, c_re, c_im):
    g = b_re.shape[0]
    per = MXU_DIM // GROUP
    nsb = g // per
    eye = jnp.eye(per, dtype=F32)

    def b_blocks(b):
        bt = b.reshape(nsb, per, STATE, GROUP).transpose(0, 1, 3, 2)
        return (bt[:, :, :, None, :] * eye[None, :, None, :, None]).reshape(nsb, per * GROUP, per * STATE).astype(BF16)

    def c_blocks(cm):
        ct = cm.reshape(nsb, per, GROUP, STATE).transpose(0, 1, 3, 2)
        return (ct[:, :, :, None, :] * eye[None, :, None, :, None]).reshape(nsb, per * STATE, per * GROUP).astype(BF16)

    return b_blocks(b_re), b_blocks(b_im), c_blocks(c_re), c_blocks(c_im)


def _unblock_b(db):
    nsb = db.shape[0]
    per = MXU_DIM // GROUP
    d = db.reshape(nsb, per, GROUP, per, STATE)
    d = jnp.stack([d[:, k, :, k, :] for k in range(per)], axis=1)
    return d.transpose(0, 1, 3, 2).reshape(nsb * per, STATE, GROUP)


def _unblock_c(dc):
    nsb = dc.shape[0]
    per = MXU_DIM // GROUP
    d = dc.reshape(nsb, per, STATE, per, GROUP)
    d = jnp.stack([d[:, k, :, k, :] for k in range(per)], axis=1)
    return d.transpose(0, 1, 3, 2).reshape(nsb * per, GROUP, STATE)


def _to_segments(v, chunk):
    rows, w = v.shape
    return v.reshape(rows // chunk, SUBLANES, chunk // SUBLANES, w).transpose(0, 2, 1, 3).reshape(rows, w)


def _from_segments(v, chunk):
    rows, w = v.shape
    return v.reshape(rows // chunk, chunk // SUBLANES, SUBLANES, w).transpose(0, 2, 1, 3).reshape(rows, w)


def _ssm_fwd(u, blocks, rows_p, d_row, *, chunk, name):
    rows, w = u.shape
    nc = rows // chunk
    steps = chunk // SUBLANES
    bre, bim, cre, cim = blocks
    nsb = bre.shape[0]
    n_state = nsb * bre.shape[2]
    sbw, sbs = bre.shape[1], bre.shape[2]

    def body(u_ref, bre_hbm, bim_hbm, cre_hbm, cim_hbm, abr_ref, abi_ref, cfr_ref, cfi_ref, apr_ref, api_ref,
             d_ref, y_ref, str_ref, sti_ref, bre_ref, bim_ref, cre_ref, cim_ref, sr, si, carry_r, carry_i,
             cm_r, cm_i):
        @pl.when(pl.program_id(0) == 0)
        def _():
            for src, dst in ((bre_hbm, bre_ref), (bim_hbm, bim_ref), (cre_hbm, cre_ref), (cim_hbm, cim_ref)):
                pltpu.sync_copy(src, dst)
            carry_r[...] = jnp.zeros_like(carry_r)
            carry_i[...] = jnp.zeros_like(carry_i)

        str_ref[0] = carry_r[...]
        sti_ref[0] = carry_i[...]
        for sb in range(nsb):
            us = slice(sb * sbw, (sb + 1) * sbw)
            ss = slice(sb * sbs, (sb + 1) * sbs)
            ub = u_ref[:, us].astype(BF16)
            bur = _dot(ub, bre_ref[sb], NN)
            bui = _dot(ub, bim_ref[sb], NN)
            xr, xi = _cmul(cfr_ref[:, ss], cfi_ref[:, ss], bur, bui)
            sr[:, ss] = xr
            si[:, ss] = xi
        _scan_segments(sr, si, abr_ref[...], abi_ref[...], apr_ref[...], api_ref[...],
                       carry_r, carry_i, cm_r, cm_i, steps, False)
        for sb in range(nsb):
            us = slice(sb * sbw, (sb + 1) * sbw)
            ss = slice(sb * sbs, (sb + 1) * sbs)
            y = _dot(sr[:, ss].astype(BF16), cre_ref[sb], NN) - _dot(si[:, ss].astype(BF16), cim_ref[sb], NN)
            y_ref[:, us] = y + d_ref[:, us] * u_ref[:, us]

    row_n = pl.BlockSpec((1, n_state), lambda c: (0, 0))
    st = pl.BlockSpec((1, 1, n_state), lambda c: (c, 0, 0))
    held = [pltpu.VMEM(b.shape, BF16) for b in blocks]
    vmem = 2 * sum(_nbytes(b.shape, BF16) for b in blocks) + 3 * _nbytes((chunk, n_state), F32)
    return pl.pallas_call(
        body, name=name,
        out_shape=[jax.ShapeDtypeStruct((rows, w), F32), jax.ShapeDtypeStruct((nc, 1, n_state), F32),
                   jax.ShapeDtypeStruct((nc, 1, n_state), F32)],
        grid=(nc,),
        in_specs=[pl.BlockSpec((chunk, w), lambda c: (c, 0))] + [_any_spec()] * 4
        + [row_n] * 6 + [pl.BlockSpec((1, w), lambda c: (0, 0))],
        out_specs=[pl.BlockSpec((chunk, w), lambda c: (c, 0)), st, st],
        scratch_shapes=held + [pltpu.VMEM((chunk, n_state), F32), pltpu.VMEM((chunk, n_state), F32),
                               pltpu.VMEM((1, n_state), F32), pltpu.VMEM((1, n_state), F32),
                               pltpu.VMEM((SUBLANES, n_state), F32), pltpu.VMEM((SUBLANES, n_state), F32)],
        compiler_params=_params(("arbitrary",), vmem),
    )(u, bre, bim, cre, cim, *rows_p, d_row)


def _ssm_bwd(u, y, dyg, st_re, st_im, blocks, rows_p, d_row, *, chunk, name, rider=None):
    rows, w = u.shape
    nc = rows // chunk
    steps = chunk // SUBLANES
    bre, bim, cre, cim = blocks
    nsb = bre.shape[0]
    sbw, sbs = bre.shape[1], bre.shape[2]
    n_state = nsb * sbs

    def body(u_ref, y_ref, dyg_ref, str_ref, sti_ref, bre_hbm, bim_hbm, cre_hbm, cim_hbm,
             abr_ref, abi_ref, cfr_ref, cfi_ref, apr_ref, api_ref, d_ref,
             du_ref, dbre_hbm, dbim_hbm, dcre_hbm, dcim_hbm, gabr_ref, gabi_ref, gcfr_ref, gcfi_ref, dd_ref,
             bre_ref, bim_ref, cre_ref, cim_ref, dbre_ref, dbim_ref, dcre_ref, dcim_ref,
             bur, bui, sr, si, lr, li, carry_r, carry_i, lam_r, lam_i, cm_r, cm_i, cl_r, cl_i):
        first = pl.program_id(0) == 0

        @pl.when(first)
        def _():
            for src, dst in ((bre_hbm, bre_ref), (bim_hbm, bim_ref), (cre_hbm, cre_ref), (cim_hbm, cim_ref)):
                pltpu.sync_copy(src, dst)
            lam_r[...] = jnp.zeros_like(lam_r)
            lam_i[...] = jnp.zeros_like(lam_i)
            for ref in (dbre_ref, dbim_ref, dcre_ref, dcim_ref, gabr_ref, gabi_ref, gcfr_ref, gcfi_ref, dd_ref):
                ref[...] = jnp.zeros_like(ref)

        uv = u_ref[...]
        dy = dyg_ref[...] * _dgelu(y_ref[...])
        dd_ref[...] += jnp.sum(dy * uv, axis=0, keepdims=True)
        dyb = dy.astype(BF16)
        ub = uv.astype(BF16)
        carry_r[...] = str_ref[0]
        carry_i[...] = sti_ref[0]
        for sb in range(nsb):
            us = slice(sb * sbw, (sb + 1) * sbw)
            ss = slice(sb * sbs, (sb + 1) * sbs)
            br = _dot(ub[:, us], bre_ref[sb], NN)
            bi = _dot(ub[:, us], bim_ref[sb], NN)
            bur[:, ss] = br
            bui[:, ss] = bi
            xr, xi = _cmul(cfr_ref[:, ss], cfi_ref[:, ss], br, bi)
            sr[:, ss] = xr
            si[:, ss] = xi
            lr[:, ss] = _dot(dyb[:, us], cre_ref[sb], NT)
            li[:, ss] = -_dot(dyb[:, us], cim_ref[sb], NT)
        abr, abi = abr_ref[...], abi_ref[...]
        apr, api = apr_ref[...], api_ref[...]
        _scan_segments(sr, si, abr, abi, apr, api, carry_r, carry_i, cm_r, cm_i, steps, False)
        for sb in range(nsb):
            us = slice(sb * sbw, (sb + 1) * sbw)
            ss = slice(sb * sbs, (sb + 1) * sbs)
            dcre_ref[sb] += _dot(sr[:, ss].astype(BF16), dyb[:, us], TN)
            dcim_ref[sb] -= _dot(si[:, ss].astype(BF16), dyb[:, us], TN)
        _scan_segments(lr, li, abr, -abi, apr, -api, lam_r, lam_i, cl_r, cl_i, steps, True)
        for c0 in range(0, n_state, SCAN_LANES):
            ls = slice(c0, c0 + SCAN_LANES)
            cfr = jnp.broadcast_to(cfr_ref[:, ls], (SUBLANES, SCAN_LANES))
            cfi = jnp.broadcast_to(cfi_ref[:, ls], (SUBLANES, SCAN_LANES))

            def step(j, acc, ls=ls, cfr=cfr, cfi=cfi):
                gar, gai, gcr, gci, pr, pi = acc
                r0 = pl.multiple_of(j * SUBLANES, SUBLANES)
                rws = pl.ds(r0, SUBLANES)
                l_r, l_i = lr[rws, ls], li[rws, ls]
                t_r, t_i = _cmul(pr, -pi, l_r, l_i)
                b_r, b_i = bur[rws, ls], bui[rws, ls]
                c_r, c_i = _cmul(b_r, -b_i, l_r, l_i)
                x_r, x_i = _cmul(cfr, -cfi, l_r, l_i)
                bur[rws, ls] = x_r
                bui[rws, ls] = x_i
                return gar + t_r, gai + t_i, gcr + c_r, gci + c_i, sr[rws, ls], si[rws, ls]

            zero = jnp.zeros((SUBLANES, SCAN_LANES), F32)
            gar, gai, gcr, gci, _, _ = lax.fori_loop(
                0, steps, step, (zero, zero, zero, zero, cm_r[:, ls], cm_i[:, ls]))
            gabr_ref[:, ls] += gar
            gabi_ref[:, ls] += gai
            gcfr_ref[:, ls] += gcr
            gcfi_ref[:, ls] += gci
        for sb in range(nsb):
            us = slice(sb * sbw, (sb + 1) * sbw)
            ss = slice(sb * sbs, (sb + 1) * sbs)
            xr, xi = bur[:, ss].astype(BF16), bui[:, ss].astype(BF16)
            du = _dot(xr, bre_ref[sb], NT) + _dot(xi, bim_ref[sb], NT)
            du_ref[:, us] = du + d_ref[:, us] * dy[:, us]
            dbre_ref[sb] += _dot(ub[:, us], xr, TN)
            dbim_ref[sb] += _dot(ub[:, us], xi, TN)

        @pl.when(pl.program_id(0) == nc - 1)
        def _():
            for src, dst in ((dbre_ref, dbre_hbm), (dbim_ref, dbim_hbm), (dcre_ref, dcre_hbm), (dcim_ref, dcim_hbm)):
                pltpu.sync_copy(src, dst)

    rev = lambda c: nc - 1 - c
    tile = pl.BlockSpec((chunk, w), lambda c: (rev(c), 0))
    row_n = pl.BlockSpec((1, n_state), lambda c: (0, 0))
    row_w = pl.BlockSpec((1, w), lambda c: (0, 0))
    st = pl.BlockSpec((1, 1, n_state), lambda c: (rev(c), 0, 0))
    acc8 = pl.BlockSpec((SUBLANES, n_state), lambda c: (0, 0))
    big = pltpu.VMEM((chunk, n_state), F32)
    row = pltpu.VMEM((1, n_state), F32)
    eight = pltpu.VMEM((SUBLANES, n_state), F32)
    f32 = lambda a: jax.ShapeDtypeStruct(a.shape, F32)
    held = [pltpu.VMEM(b.shape, BF16) for b in blocks] + [pltpu.VMEM(b.shape, F32) for b in blocks]
    vmem = (sum(_nbytes(b.shape, BF16) + _nbytes(b.shape, F32) for b in blocks)
            + 7 * _nbytes((chunk, n_state), F32) + 16 * _nbytes((chunk, w), F32))
    res, landed = _call(
        body, [u, y, dyg, st_re, st_im, bre, bim, cre, cim, *rows_p, d_row], name=name,
        out_shape=[jax.ShapeDtypeStruct((rows, w), F32), f32(bre), f32(bim), f32(cre), f32(cim)]
        + [jax.ShapeDtypeStruct((SUBLANES, n_state), F32)] * 4 + [jax.ShapeDtypeStruct((1, w), F32)],
        grid=(nc,),
        in_specs=[tile, tile, tile, st, st] + [_any_spec()] * 4 + [row_n] * 6 + [row_w],
        out_specs=[tile] + [_any_spec()] * 4 + [acc8] * 4 + [row_w],
        scratch_shapes=held + [big] * 6 + [row] * 4 + [eight] * 4,
        semantics=("arbitrary",), vmem=vmem, rider=rider)
    return res if rider is None else (res, landed)


def _loss_grad(x, mm, target, name):
    rows, d = x.shape

    def fn(xv, mv, tv):
        err = xv + mv - tv
        g = err * (1.0 / d)
        return g, g, 0.5 * err * g

    return _ew(fn, name=name, rows=rows, width=d, tiles=[(x, 0), (mm, 0), (target, 0)],
               outs=[(F32, d, 0), (BF16, d, 0)], accs=1)


def _pair_sum(grad, recv, name):
    r4, cdim = recv.shape
    r = r4 // N_CHIPS
    tr = _tile(r, 544, 16)
    g4 = grad.reshape(N_CHIPS, 2, r, cdim)
    r3 = recv.reshape(N_CHIPS, r, cdim)
    core = jnp.reshape(lax.axis_index("c"), (1,)).astype(jnp.int32)

    def body(c_ref, g_ref, r_ref, o_ref):
        o_ref[...] = (g_ref[0] + r_ref[...]).astype(BF16)

    out = pl.pallas_call(
        body, name=name, out_shape=jax.ShapeDtypeStruct((N_CHIPS, r, cdim), BF16),
        grid_spec=pltpu.PrefetchScalarGridSpec(
            num_scalar_prefetch=1, grid=(N_CHIPS, r // tr),
            in_specs=[pl.BlockSpec((1, 1, tr, cdim), lambda j, i, c: (j, c[0], i, 0)),
                      pl.BlockSpec((1, tr, cdim), lambda j, i, c: (j, i, 0))],
            out_specs=pl.BlockSpec((1, tr, cdim), lambda j, i, c: (j, i, 0))),
        compiler_params=_params(("parallel", "parallel"), 6 * _nbytes((tr, cdim), F32)),
    )(core, g4, r3)
    return out.reshape(r4, cdim)


def _chip_sum(recv, name):
    r4, cdim = recv.shape
    r = r4 // N_CHIPS
    tr = _tile(r, 544, 16)
    r3 = recv.reshape(N_CHIPS, r, cdim)

    def body(r_ref, o_ref):
        acc = r_ref[0].astype(F32)
        for j in range(1, N_CHIPS):
            acc = acc + r_ref[j].astype(F32)
        o_ref[...] = acc

    return pl.pallas_call(
        body, name=name, out_shape=jax.ShapeDtypeStruct((r, cdim), F32), grid=(r // tr,),
        in_specs=[pl.BlockSpec((N_CHIPS, tr, cdim), lambda i: (0, i, 0))],
        out_specs=pl.BlockSpec((tr, cdim), lambda i: (i, 0)),
        compiler_params=_params(("parallel",), 8 * _nbytes((tr, cdim), F32)),
    )(r3)


def _adamw_math(w, g, m, v):
    m = ADAM_B1 * m + (1.0 - ADAM_B1) * g
    v = ADAM_B2 * v + (1.0 - ADAM_B2) * (g * g)
    m_hat = m / (1.0 - ADAM_B1 ** ADAM_STEP)
    v_hat = v / (1.0 - ADAM_B2 ** ADAM_STEP)
    delta = -ADAM_LR * (m_hat / (jnp.sqrt(v_hat) + ADAM_EPS) + ADAM_WD * w)
    return delta, m, v


def _adamw(w, g, m, v, name):
    rows, cols = w.shape
    tr = _tile(rows, 256, SUBLANES)

    def body(w_ref, g_ref, m_ref, v_ref, d_ref, nm_ref, nv_ref):
        d, nm, nv = _adamw_math(w_ref[...], g_ref[...], m_ref[...], v_ref[...])
        d_ref[...] = d
        nm_ref[...] = nm
        nv_ref[...] = nv

    spec = pl.BlockSpec((tr, cols), lambda i: (i, 0))
    shp = jax.ShapeDtypeStruct((rows, cols), F32)
    return pl.pallas_call(
        body, name=name, out_shape=[shp] * 3, grid=(rows // tr,), in_specs=[spec] * 4, out_specs=[spec] * 3,
        compiler_params=_params(("parallel",)),
    )(w, g, m, v)


def _adamw_small(w, parts, m, v, name):
    rows, cols = w.shape
    p3 = parts.reshape(N_DEV, rows, cols)

    def body(w_ref, p_ref, m_ref, v_ref, g_ref, d_ref, nm_ref, nv_ref):
        g = p_ref[0]
        for k in range(1, N_DEV):
            g = g + p_ref[k]
        d, nm, nv = _adamw_math(w_ref[...], g, m_ref[...], v_ref[...])
        g_ref[...] = g
        d_ref[...] = d
        nm_ref[...] = nm
        nv_ref[...] = nv

    shp = jax.ShapeDtypeStruct((rows, cols), F32)
    return pl.pallas_call(body, name=name, out_shape=[shp] * 4)(w, p3, m, v)


SMALL = ("norm_w", "q_norm_w", "k_norm_w", "sinks", "A_re", "A_im", "log_dt", "B_re", "B_im", "C_re", "C_im",
         "D_skip", "b_glu")
LARGE = ("w_in", "w_attn_proj", "w_glu", "w_ssm_proj", "w_out")
ORDER = ("norm_w", "w_in", "q_norm_w", "k_norm_w", "sinks", "w_attn_proj", "A_re", "A_im", "log_dt", "B_re", "B_im",
         "C_re", "C_im", "D_skip", "w_glu", "b_glu", "w_ssm_proj", "w_out")


SMALL_REST = ("loss",) + SMALL[1:]


def _pack(named, keys):
    flat = jnp.concatenate([named[k].reshape(-1).astype(F32) for k in keys])
    n = flat.shape[0]
    rows = -(-n // (LANES * SUBLANES)) * SUBLANES
    return jnp.pad(flat, (0, rows * LANES - n)).reshape(rows, LANES)


def _unpack(packed, like, keys):
    flat = packed.reshape(-1)
    out, o = {}, 0
    for k in keys:
        n = like[k].size
        out[k] = flat[o:o + n].reshape(like[k].shape)
        o += n
    return out


def _step(xs, target, p, shards):
    s_in, s_ap, s_glu, s_sp, s_o = shards
    (w_in_t,) = _exchange(_all_gather([s_in]), "gather_w_in")
    seq, d = xs.shape
    attn_w = (d // 128) * HEAD_DIM
    n_q = attn_w // HEAD_DIM
    kv_w = N_KV_HEADS * HEAD_DIM
    ssm_w = d // 2
    n_groups = ssm_w // GROUP
    n_state = n_groups * STATE
    in_w = w_in_t.shape[0]
    assert in_w == 2 * attn_w + 2 * kv_w + 2 * ssm_w + 2 * d
    o_u = 2 * attn_w + 2 * kv_w
    o_z = o_u + ssm_w
    o_ga = o_z + ssm_w
    chunk = min(BLOCK, seq)
    cw = d // 4

    norm_row = p["norm_w"].reshape(1, d)
    h = _rmsnorm_fwd(xs, norm_row, "rmsnorm_fwd")
    proj, (w_ap_t, w_glu_t, w_sp_t, w_o) = _matmul(h, w_in_t, mode="nt", name="in_proj", tn=512,
                                                   rider=_all_gather([s_ap, s_glu, s_sp, s_o]))
    qw_row, kw_row = p["q_norm_w"].reshape(1, HEAD_DIM), p["k_norm_w"].reshape(1, HEAD_DIM)
    ag = _attention_fwd(proj, qw_row, kw_row, p["sinks"], attn_w=attn_w, kv_w=kv_w, name="attention_fwd")

    log_dt_col = p["log_dt"].reshape(n_groups, 1)
    prep = _ssm_prep(p["A_re"], p["A_im"], log_dt_col, chunk // SUBLANES, "ssm_prep")
    rows_p = [v.reshape(1, n_state) for v in prep]
    blocks = _ssm_blocks(p["B_re"], p["B_im"], p["C_re"], p["C_im"])
    d_row = p["D_skip"].reshape(1, ssm_w)
    u_seg = _to_segments(proj[:, o_u:o_u + ssm_w], chunk)
    y_seg, st_re, st_im = _ssm_fwd(u_seg, blocks, rows_p, d_row, chunk=chunk, name="ssm_fwd")
    y_ssm = _from_segments(y_seg, chunk)
    (yg,) = _ew(_gelu, name="gelu", rows=seq, width=ssm_w, tiles=[(y_ssm, 0)], outs=[(BF16, ssm_w, 0)], cw=cw)
    glu = _matmul(yg, w_glu_t, mode="nt", name="glu_proj", bias=p["b_glu"].reshape(1, 2 * ssm_w))
    (ts,) = _ew(lambda ga, gb, z: ga * _sigmoid(gb) * _silu(z), name="glu_gate", rows=seq, width=ssm_w,
                tiles=[(glu, 0), (glu, ssm_w), (proj, o_z)], outs=[(BF16, ssm_w, 0)], cw=cw)
    yy = _matmul(ag, w_ap_t, mode="nt", name="attn_proj", out_cols=(2 * d, 0))
    yy = _matmul(ts, w_sp_t, mode="nt", name="ssm_proj", out_cols=(2 * d, d), into=yy)
    (merged,) = _ew(lambda ya, ys, ga, gs: _sigmoid(ga) * ya + _sigmoid(gs) * ys, name="merge", rows=seq, width=d,
                    tiles=[(yy, 0), (yy, d), (proj, o_ga), (proj, o_ga + d)], outs=[(BF16, d, 0)], cw=cw)
    mm = _matmul(merged, w_o, mode="nn", name="out_proj")
    dout, dout_b, loss_cols = _loss_grad(xs, mm, target, "loss_grad")
    loss_local = jnp.sum(loss_cols)

    g_w_o = _matmul(merged, dout_b, mode="tn", name="grad_w_out", tk=1024)
    dmerged, (sib_o,) = _matmul(dout_b, w_o, mode="nt", name="d_merged", rider=_sibling_exchange([g_w_o]))
    pair_o = _pair_sum(g_w_o, sib_o, "pair_sum_w_out")

    def merge_bwd(dm, y, g):
        s = _sigmoid(g)
        return dm * s, dm * y * s * (1.0 - s)

    dyy, dproj = _ew(merge_bwd, name="merge_bwd", rows=seq, width=2 * d,
                     tiles=[(dmerged, 0, d), (yy, 0), (proj, o_ga)],
                     outs=[(BF16, 2 * d, 0), (BF16, in_w, o_ga)], cw=cw)
    dy_a, dy_s = Cols(dyy, 0, d), Cols(dyy, d, d)
    g_w_ap_t = _matmul(dy_a, ag, mode="tn", name="grad_w_attn_proj", tk=1024)
    g_w_sp_t = _matmul(dy_s, ts, mode="tn", name="grad_w_ssm_proj", tk=1024)
    d_ag = _matmul(dy_a, w_ap_t, mode="nn", name="d_attn_gated")
    d_ts = _matmul(dy_s, w_sp_t, mode="nn", name="d_ssm_gated")

    (dproj, dkv, g_qw, g_kw, g_sinks), (chips_o, sib_ap, sib_sp) = _attention_bwd(
        proj, d_ag, dproj, qw_row, kw_row, p["sinks"], attn_w=attn_w, kv_w=kv_w, name="attention_bwd",
        rider=_join(_chip_exchange([pair_o]), _sibling_exchange([g_w_ap_t, g_w_sp_t])))
    pair_ap = _pair_sum(g_w_ap_t, sib_ap, "pair_sum_w_attn_proj")
    pair_sp = _pair_sum(g_w_sp_t, sib_sp, "pair_sum_w_ssm_proj")
    dproj = _attention_dkv(dproj, dkv, attn_w=attn_w, kv_w=kv_w, name="attention_dkv")

    n_half = ssm_w // _tile(2 * ssm_w, cw)

    def glu_bwd(j, dt, ga, gb, z):
        sb, sz = _sigmoid(gb), _silu(z)
        dg = jnp.where(j < n_half, dt * sb * sz, dt * ga * sb * (1.0 - sb) * sz)
        return dg, dg

    glu_ops = [(d_ts, 0, ssm_w), (glu, 0, ssm_w), (glu, ssm_w, ssm_w), (proj, o_z, ssm_w)]
    dglu, g_bglu = _ew(glu_bwd, name="glu_bwd", rows=seq, width=2 * ssm_w, tiles=glu_ops,
                       outs=[(BF16, 2 * ssm_w, 0)], accs=1, cw=cw, with_col=True)
    (dproj,) = _ew(lambda dt, ga, gb, z: dt * ga * _sigmoid(gb) * _dsilu(z), name="glu_bwd_z", rows=seq,
                   width=ssm_w, tiles=glu_ops, outs=[(BF16, in_w, o_z)], into=[dproj], cw=cw)
    g_w_glu_t = _matmul(dglu, yg, mode="tn", name="grad_w_glu", tk=1024)
    d_yg = _matmul(dglu, w_glu_t, mode="nn", name="d_gelu")
    ((du_seg, db_re, db_im, dc_re, dc_im, gabr, gabi, gcfr, gcfi, g_d), (chips_ap, chips_sp, sib_glu)) = _ssm_bwd(
        u_seg, y_seg, _to_segments(d_yg, chunk), st_re, st_im, blocks, rows_p, d_row, chunk=chunk, name="ssm_bwd",
        rider=_join(_chip_exchange([pair_ap, pair_sp]), _sibling_exchange([g_w_glu_t])))
    pair_glu = _pair_sum(g_w_glu_t, sib_glu, "pair_sum_w_glu")
    (dproj,) = _ew(lambda v: v, name="du_store", rows=seq, width=ssm_w, tiles=[(_from_segments(du_seg, chunk), 0)],
                   outs=[(BF16, in_w, o_u)], into=[dproj], cw=cw)
    half = d // 2
    g_in_lo, (chips_glu,) = _matmul(dproj, Cols(h, 0, half), mode="tn", name="grad_w_in_lo", tk=1024,
                                    rider=_chip_exchange([pair_glu]))
    g_in_hi, (sib_lo,) = _matmul(dproj, Cols(h, half, half), mode="tn", name="grad_w_in_hi", tk=1024,
                                 rider=_sibling_exchange([g_in_lo]))
    pair_lo = _pair_sum(g_in_lo, sib_lo, "pair_sum_w_in_lo")
    g_a_re, g_a_im, g_log_dt = _ssm_param_bwd(
        p["A_re"], p["A_im"], log_dt_col, *[g.reshape(SUBLANES, n_groups, STATE) for g in (gabr, gabi, gcfr, gcfi)],
        "ssm_param_bwd")
    small_grads = dict(
        loss=loss_local, q_norm_w=g_qw.reshape(HEAD_DIM), k_norm_w=g_kw.reshape(HEAD_DIM),
        sinks=g_sinks[0, :n_q], A_re=g_a_re, A_im=g_a_im, log_dt=g_log_dt.reshape(n_groups),
        B_re=_unblock_b(db_re), B_im=_unblock_b(db_im), C_re=_unblock_c(dc_re), C_im=_unblock_c(dc_im),
        D_skip=g_d.reshape(n_groups, GROUP), b_glu=g_bglu.reshape(2 * ssm_w))
    dh, (chips_lo, sib_hi) = _matmul(dproj, w_in_t, mode="nn", name="d_normed", tk=2176,
                                     rider=_join(_chip_exchange([pair_lo]), _sibling_exchange([g_in_hi])))
    pair_hi = _pair_sum(g_in_hi, sib_hi, "pair_sum_w_in_hi")
    (grad_x, g_norm), (chips_hi, small_parts) = _rmsnorm_bwd(
        xs, norm_row, dh, dout, "rmsnorm_bwd",
        rider=_join(_chip_exchange([pair_hi]), _all_gather([_pack(small_grads, SMALL_REST)])))
    (norm_parts,) = _exchange(_all_gather([_pack(dict(norm_w=g_norm), ("norm_w",))]), "gather_norm_grad")
    g_in = jnp.concatenate([_chip_sum(chips_lo, "chip_sum_w_in_lo"), _chip_sum(chips_hi, "chip_sum_w_in_hi")], axis=1)
    summed = [g_in] + [_chip_sum(c, "chip_sum_" + k)
                       for k, c in zip(LARGE[1:], (chips_ap, chips_glu, chips_sp, chips_o))]
    return grad_x, summed, small_parts, norm_parts


def kernel(x, norm_w, w_in, q_norm_w, k_norm_w, sinks, w_attn_proj, A_re, A_im, log_dt, B_re, B_im, C_re, C_im, D_skip, w_glu, b_glu, w_ssm_proj, w_out, loss_target, m_norm_w, m_w_in, m_q_norm_w, m_k_norm_w, m_sinks, m_w_attn_proj, m_A_re, m_A_im, m_log_dt, m_B_re, m_B_im, m_C_re, m_C_im, m_D_skip, m_w_glu, m_b_glu, m_w_ssm_proj, m_w_out, v_norm_w, v_w_in, v_q_norm_w, v_k_norm_w, v_sinks, v_w_attn_proj, v_A_re, v_A_im, v_log_dt, v_B_re, v_B_im, v_C_re, v_C_im, v_D_skip, v_w_glu, v_b_glu, v_w_ssm_proj, v_w_out):
    weights = dict(norm_w=norm_w, w_in=w_in, q_norm_w=q_norm_w, k_norm_w=k_norm_w, sinks=sinks,
                   w_attn_proj=w_attn_proj, A_re=A_re, A_im=A_im, log_dt=log_dt, B_re=B_re, B_im=B_im, C_re=C_re,
                   C_im=C_im, D_skip=D_skip, w_glu=w_glu, b_glu=b_glu, w_ssm_proj=w_ssm_proj, w_out=w_out)
    m_in = dict(norm_w=m_norm_w, w_in=m_w_in, q_norm_w=m_q_norm_w, k_norm_w=m_k_norm_w, sinks=m_sinks,
                w_attn_proj=m_w_attn_proj, A_re=m_A_re, A_im=m_A_im, log_dt=m_log_dt, B_re=m_B_re, B_im=m_B_im,
                C_re=m_C_re, C_im=m_C_im, D_skip=m_D_skip, w_glu=m_w_glu, b_glu=m_b_glu, w_ssm_proj=m_w_ssm_proj,
                w_out=m_w_out)
    v_in = dict(norm_w=v_norm_w, w_in=v_w_in, q_norm_w=v_q_norm_w, k_norm_w=v_k_norm_w, sinks=v_sinks,
                w_attn_proj=v_w_attn_proj, A_re=v_A_re, A_im=v_A_im, log_dt=v_log_dt, B_re=v_B_re, B_im=v_B_im,
                C_re=v_C_re, C_im=v_C_im, D_skip=v_D_skip, w_glu=v_w_glu, b_glu=v_b_glu, w_ssm_proj=v_w_ssm_proj,
                w_out=v_w_out)

    _, seq, d = x.shape
    column_sharded = LARGE[:4]
    as_rows = lambda k, a: a.T if k in column_sharded else a
    shards = [as_rows(k, weights[k]).astype(BF16) for k in LARGE]
    small = {k: weights[k] for k in SMALL}
    grad_x, summed, small_parts, norm_parts = _step(x.reshape(seq, d), loss_target.reshape(seq, d), small, shards)

    grads, delta, new_m, new_v = {}, {}, {}, {}
    for k, g in zip(LARGE, summed):
        if k == "w_in":
            upd = _adamw(weights[k].T, g, m_in[k].T, v_in[k].T, "adamw_" + k)
            grads[k], delta[k], new_m[k], new_v[k] = [a.T for a in (g, *upd)]
        else:
            grads[k] = as_rows(k, g)
            delta[k], new_m[k], new_v[k] = _adamw(weights[k], grads[k], m_in[k], v_in[k], "adamw_" + k)

    zero = jnp.zeros((), F32)
    for keys, parts in ((SMALL_REST, small_parts), (("norm_w",), norm_parts)):
        like = dict(small, loss=zero)
        packs = [_pack(dict(src, loss=zero), keys) for src in (weights, m_in, v_in)]
        res = _adamw_small(packs[0], parts, packs[1], packs[2], "adamw_small_%d" % len(keys))
        for dst, r in zip((grads, delta, new_m, new_v), res):
            dst.update(_unpack(r, like, keys))
    loss = grads["loss"]

    return (loss, grad_x.reshape(x.shape), *[grads[k] for k in ORDER], *[delta[k] for k in ORDER],
            *[new_m[k] for k in ORDER], *[new_v[k] for k in ORDER])
```

```python
import math
from typing import Callable, NamedTuple

import jax
import jax.numpy as jnp
from jax import lax
from jax.experimental import pallas as pl
from jax.experimental.pallas import tpu as pltpu

F32 = jnp.float32
BF16 = jnp.bfloat16
MESH = pl.DeviceIdType.MESH

HEAD_DIM = 64
N_KV_HEADS = 4
GROUP = 16
STATE = 64
BLOCK = 128
NORM_EPS = 1e-6
N_DEV = 8
N_CHIPS = 4
LANES = 128
SUBLANES = 8
MXU_DIM = 256
VMEM_BYTES = 64 * 1024 * 1024
VMEM_CAP = VMEM_BYTES - 8 * 1024 * 1024

ADAM_LR = 0.001
ADAM_B1 = 0.9
ADAM_B2 = 0.999
ADAM_EPS = 1e-08
ADAM_WD = 0.01
ADAM_STEP = 10

GELU_C = math.sqrt(2.0 / math.pi)
GELU_K = 0.044715


def _tile(dim, pref, mult=LANES):
    if dim <= pref:
        return dim
    best = None
    for d in range(mult, pref + 1, mult):
        if dim % d == 0:
            best = d
    assert best is not None, (dim, pref, mult)
    return best


def _params(semantics=None, vmem=None):
    kw = {}
    if semantics is not None:
        kw["dimension_semantics"] = semantics
    if vmem is not None:
        kw["vmem_limit_bytes"] = int(min(VMEM_CAP, max(vmem, 32 * 1024 * 1024)))
    return pltpu.CompilerParams(**kw)


def _nbytes(shape, dtype):
    return math.prod(shape) * jnp.dtype(dtype).itemsize


def _sigmoid(x):
    return 1.0 / (1.0 + jnp.exp(-x))


def _silu(x):
    return x * _sigmoid(x)


def _dsilu(x):
    s = _sigmoid(x)
    return s * (1.0 + x * (1.0 - s))


def _gelu(x):
    return 0.5 * x * (1.0 + jnp.tanh(GELU_C * (x + GELU_K * x * x * x)))


def _dgelu(x):
    t = jnp.tanh(GELU_C * (x + GELU_K * x * x * x))
    return 0.5 * (1.0 + t) + 0.5 * x * (1.0 - t * t) * GELU_C * (1.0 + 3.0 * GELU_K * x * x)


def _dot(a, b, dims):
    return lax.dot_general(a, b, (dims, ((), ())), preferred_element_type=F32)


NN = ((1,), (0,))
NT = ((1,), (1,))
TN = ((0,), (0,))


def _any_spec():
    return pl.BlockSpec(memory_space=pl.ANY)


class Rider(NamedTuple):
    operands: tuple
    out_shapes: tuple
    sems: tuple
    start: Callable
    finish: Callable


def _all_gather(shards):
    n = len(shards)

    def copies(ins, outs, sems):
        send_sems, recv_sems, local_sems = sems
        x, y, c = lax.axis_index("x"), lax.axis_index("y"), lax.axis_index("c")
        me, sibling = (x, y, c), (x, y, 1 - c)
        chips = [(1 - x, y), (x, 1 - y), (1 - x, 1 - y)]

        def rows(k, px, py, pc):
            r = shards[k].shape[0]
            return outs[k].at[pl.ds((4 * px + 2 * py + pc) * r, r), :]

        def copy(k, s, block, to, src=None):
            return pltpu.make_async_remote_copy(
                src_ref=rows(k, *block) if src is None else src, dst_ref=rows(k, *block),
                send_sem=send_sems.at[7 * k + s], recv_sem=recv_sems.at[7 * k + s],
                device_id=to, device_id_type=MESH)

        mine = [pltpu.make_async_copy(ins[k], rows(k, *me), local_sems.at[k]) for k in range(n)]
        first = []
        for k in range(n):
            first.append(copy(k, 0, me, sibling, src=ins[k]))
            first += [copy(k, 1 + j, me, (*chip, c), src=ins[k]) for j, chip in enumerate(chips)]
        return me, sibling, chips, c, copy, mine, first

    def start(ins, outs, sems):
        *_, mine, first = copies(ins, outs, sems)
        for cp in mine + first:
            cp.start()

    def finish(ins, outs, sems):
        me, sibling, chips, c, copy, mine, first = copies(ins, outs, sems)
        passed = []
        for j, chip in enumerate(chips):
            for k in range(n):
                copy(k, 1 + j, (*chip, c), me).wait_recv()
                fwd = copy(k, 4 + j, (*chip, c), sibling)
                fwd.start()
                passed.append(fwd)
        for k in range(n):
            copy(k, 0, sibling, me).wait_recv()
            for j, chip in enumerate(chips):
                copy(k, 4 + j, (*chip, 1 - c), me).wait_recv()
        for cp in first + passed:
            cp.wait_send()
        for cp in mine:
            cp.wait()

    return Rider(
        tuple(shards),
        tuple(jax.ShapeDtypeStruct((N_DEV * s.shape[0], s.shape[1]), s.dtype) for s in shards),
        (pltpu.SemaphoreType.DMA((7 * n,)), pltpu.SemaphoreType.DMA((7 * n,)), pltpu.SemaphoreType.DMA((n,))),
        start, finish)


def _sibling_exchange(grads):
    n = len(grads)

    def copies(ins, outs, sems):
        send_sems, recv_sems = sems
        x, y, c = lax.axis_index("x"), lax.axis_index("y"), lax.axis_index("c")
        out = []
        for k in range(n):
            r = grads[k].shape[0] // N_DEV
            for j in range(N_CHIPS):
                out.append(pltpu.make_async_remote_copy(
                    src_ref=ins[k].at[pl.ds((2 * j + 1 - c) * r, r), :],
                    dst_ref=outs[k].at[pl.ds(j * r, r), :],
                    send_sem=send_sems.at[N_CHIPS * k + j], recv_sem=recv_sems.at[N_CHIPS * k + j],
                    device_id=(x, y, 1 - c), device_id_type=MESH))
        return out

    def start(ins, outs, sems):
        for cp in copies(ins, outs, sems):
            cp.start()

    def finish(ins, outs, sems):
        for cp in copies(ins, outs, sems):
            cp.wait()

    return Rider(
        tuple(grads), tuple(jax.ShapeDtypeStruct((g.shape[0] // 2, g.shape[1]), g.dtype) for g in grads),
        (pltpu.SemaphoreType.DMA((N_CHIPS * n,)), pltpu.SemaphoreType.DMA((N_CHIPS * n,))), start, finish)


def _chip_exchange(parts):
    n = len(parts)

    def copies(ins, outs, sems):
        send_sems, recv_sems, local_sems = sems
        x, y, c = lax.axis_index("x"), lax.axis_index("y"), lax.axis_index("c")
        my_chip = 2 * x + y
        chips = [(1 - x, y), (x, 1 - y), (1 - x, 1 - y)]
        local, sent = [], []
        for k in range(n):
            r = parts[k].shape[0] // N_CHIPS
            mine = pl.ds(my_chip * r, r)
            local.append(pltpu.make_async_copy(ins[k].at[mine, :], outs[k].at[mine, :], local_sems.at[k]))
            for s, (px, py) in enumerate(chips):
                sent.append(pltpu.make_async_remote_copy(
                    src_ref=ins[k].at[pl.ds((2 * px + py) * r, r), :], dst_ref=outs[k].at[mine, :],
                    send_sem=send_sems.at[3 * k + s], recv_sem=recv_sems.at[3 * k + s],
                    device_id=(px, py, c), device_id_type=MESH))
        return local, sent

    def start(ins, outs, sems):
        local, sent = copies(ins, outs, sems)
        for cp in local + sent:
            cp.start()

    def finish(ins, outs, sems):
        local, sent = copies(ins, outs, sems)
        for cp in sent + local:
            cp.wait()

    return Rider(
        tuple(parts), tuple(jax.ShapeDtypeStruct(p.shape, p.dtype) for p in parts),
        (pltpu.SemaphoreType.DMA((3 * n,)), pltpu.SemaphoreType.DMA((3 * n,)), pltpu.SemaphoreType.DMA((n,))),
        start, finish)


def _join(*riders):
    cuts_in, cuts_out, cuts_sem = [0], [0], [0]
    for r in riders:
        cuts_in.append(cuts_in[-1] + len(r.operands))
        cuts_out.append(cuts_out[-1] + len(r.out_shapes))
        cuts_sem.append(cuts_sem[-1] + len(r.sems))

    def each(which):
        def run(ins, outs, sems):
            for i, r in enumerate(riders):
                getattr(r, which)(ins[cuts_in[i]:cuts_in[i + 1]], outs[cuts_out[i]:cuts_out[i + 1]],
                                  sems[cuts_sem[i]:cuts_sem[i + 1]])
        return run

    return Rider(sum((r.operands for r in riders), ()), sum((r.out_shapes for r in riders), ()),
                 sum((r.sems for r in riders), ()), each("start"), each("finish"))


def _call(body, operands, *, name, out_shape, grid, in_specs, out_specs, scratch_shapes=(), aliases=None,
          semantics=None, vmem=None, rider=None):
    operands, out_shape, scratch_shapes = list(operands), list(out_shape), list(scratch_shapes)
    in_specs, out_specs = list(in_specs), list(out_specs)
    if rider is None:
        res = pl.pallas_call(
            body, name=name, out_shape=out_shape, grid=grid, in_specs=in_specs, out_specs=out_specs,
            scratch_shapes=scratch_shapes, input_output_aliases=aliases or {},
            compiler_params=_params(semantics, vmem))(*operands)
        return list(res), []
    n_in, n_out, n_scr = len(operands), len(out_shape), len(scratch_shapes)
    ri, ro = len(rider.operands), len(rider.out_shapes)

    def carried(*refs):
        a, b = n_in, n_in + ri
        c, d = b + n_out, b + n_out + ro
        e = d + n_scr
        ids = [pl.program_id(k) for k in range(len(grid))]
        first = ids[0] == 0
        last = ids[0] == grid[0] - 1
        for k in range(1, len(grid)):
            first = jnp.logical_and(first, ids[k] == 0)
            last = jnp.logical_and(last, ids[k] == grid[k] - 1)

        @pl.when(first)
        def _():
            rider.start(refs[a:b], refs[c:d], refs[e:])

        body(*refs[:a], *refs[b:c], *refs[d:e])

        @pl.when(last)
        def _():
            rider.finish(refs[a:b], refs[c:d], refs[e:])

    res = pl.pallas_call(
        carried, name=name, out_shape=out_shape + list(rider.out_shapes), grid=grid,
        in_specs=in_specs + [_any_spec()] * ri, out_specs=out_specs + [_any_spec()] * ro,
        scratch_shapes=scratch_shapes + list(rider.sems), input_output_aliases=aliases or {},
        compiler_params=_params(("arbitrary",) * len(grid), vmem))(*operands, *rider.operands)
    return list(res[:n_out]), list(res[n_out:])


def _exchange(rider, name):
    ri, ro = len(rider.operands), len(rider.out_shapes)

    def body(*refs):
        rider.start(refs[:ri], refs[ri:ri + ro], refs[ri + ro:])
        rider.finish(refs[:ri], refs[ri:ri + ro], refs[ri + ro:])

    return pl.pallas_call(
        body, name=name, out_shape=list(rider.out_shapes), in_specs=[_any_spec()] * ri,
        out_specs=[_any_spec()] * ro, scratch_shapes=list(rider.sems))(*rider.operands)


class Cols(NamedTuple):
    arr: jax.Array
    off: int
    width: int


def _cols(a):
    return a if isinstance(a, Cols) else Cols(a, 0, a.shape[1])


def _matmul(a, b, *, mode, name, out_dtype=F32, tm=1024, tn=1024, tk=2048, bias=None, out_cols=None, into=None,
            rider=None):
    a, b = _cols(a), _cols(b)
    if mode == "nn":
        (m, k), (k2, n) = (a.arr.shape[0], a.width), (b.arr.shape[0], b.width)
    elif mode == "nt":
        (m, k), (n, k2) = (a.arr.shape[0], a.width), (b.arr.shape[0], b.width)
    else:
        (k, m), (k2, n) = (a.arr.shape[0], a.width), (b.arr.shape[0], b.width)
    assert k == k2, (a.arr.shape, b.arr.shape, mode)
    tm, tn, tk = _tile(m, tm), _tile(n, tn), _tile(k, tk)
    nk = k // tk
    dims = {"nn": NN, "nt": NT, "tn": TN}[mode]
    if mode == "tn":
        assert a.off % tm == 0
        a_spec = pl.BlockSpec((tk, tm), lambda i, j, kk, o=a.off // tm: (kk, i + o))
    else:
        assert a.off % tk == 0
        a_spec = pl.BlockSpec((tm, tk), lambda i, j, kk, o=a.off // tk: (i, kk + o))
    if mode == "nt":
        assert b.off % tk == 0
        b_spec = pl.BlockSpec((tn, tk), lambda i, j, kk, o=b.off // tk: (j, kk + o))
    else:
        assert b.off % tn == 0
        b_spec = pl.BlockSpec((tk, tn), lambda i, j, kk, o=b.off // tn: (kk, j + o))
    in_specs, operands = [a_spec, b_spec], [a.arr, b.arr]
    if bias is not None:
        in_specs.append(pl.BlockSpec((1, tn), lambda i, j, kk: (0, j)))
        operands.append(bias)
    total_w, o_off = out_cols if out_cols is not None else (n, 0)
    assert o_off % tn == 0
    aliases = {}
    if into is not None:
        assert into.shape == (m, total_w) and into.dtype == out_dtype
        in_specs.append(_any_spec())
        operands.append(into)
        aliases = {len(operands) - 1: 0}
    n_in = len(operands)

    def body(*refs):
        a_ref, b_ref = refs[0], refs[1]
        bias_ref = refs[2] if bias is not None else None
        o_ref = refs[n_in]
        acc_ref = refs[-1] if nk > 1 else None
        part = _dot(a_ref[...].astype(BF16), b_ref[...].astype(BF16), dims)

        def finish(acc):
            if bias_ref is not None:
                acc = acc + bias_ref[...]
            o_ref[...] = acc.astype(out_dtype)

        if nk == 1:
            finish(part)
        else:
            kk = pl.program_id(2)

            @pl.when(kk == 0)
            def _():
                acc_ref[...] = part

            @pl.when(kk > 0)
            def _():
                acc_ref[...] += part

            @pl.when(kk == nk - 1)
            def _():
                finish(acc_ref[...])

    vmem = 2 * (_nbytes((tm, tk), a.arr.dtype) + _nbytes((tk, tn), b.arr.dtype) + _nbytes((tm, tn), out_dtype))
    vmem += 3 * _nbytes((tm, tn), F32)
    (out,), landed = _call(
        body, operands, name=name, out_shape=[jax.ShapeDtypeStruct((m, total_w), out_dtype)],
        grid=(m // tm, n // tn, nk), in_specs=in_specs,
        out_specs=[pl.BlockSpec((tm, tn), lambda i, j, kk, o=o_off // tn: (i, j + o))],
        scratch_shapes=[pltpu.VMEM((tm, tn), F32)] if nk > 1 else [], aliases=aliases,
        semantics=("parallel", "parallel", "arbitrary"), vmem=vmem, rider=rider)
    return out if rider is None else (out, landed)


def _ew(fn, *, name, rows, width, tiles, vecs=(), outs, accs=0, tl=1024, cw=512, into=None, with_col=False):
    tl, cw = _tile(rows, tl, SUBLANES), _tile(width, cw)
    ncol = width // cw
    nt_, nv = len(tiles), len(vecs)
    into = list(into) if into is not None else [None] * len(outs)
    aliased = [t for t in into if t is not None]

    def off(o):
        assert o % cw == 0, (name, o, cw)
        return o // cw

    in_specs, vmem = [], 0
    for t in tiles:
        arr, o = t[0], off(t[1])
        wrap = t[2] // cw if len(t) > 2 else ncol
        in_specs.append(pl.BlockSpec((tl, cw), lambda j, i, o=o, wrap=wrap: (i, o + j % wrap)))
        vmem += _nbytes((tl, cw), arr.dtype)
    in_specs += [pl.BlockSpec((1, cw), lambda j, i, o=off(o): (0, j + o)) for _, o in vecs]
    in_specs += [_any_spec() for _ in aliased]
    out_shape, out_specs, aliases = [], [], {}
    n_in = nt_ + nv
    for idx, ((dt, tw, o), tgt) in enumerate(zip(outs, into)):
        out_shape.append(jax.ShapeDtypeStruct((rows, tw), dt))
        out_specs.append(pl.BlockSpec((tl, cw), lambda j, i, o=off(o): (i, j + o)))
        vmem += _nbytes((tl, cw), dt)
        if tgt is not None:
            assert tgt.shape == (rows, tw) and tgt.dtype == dt, (name, tgt.shape, tgt.dtype)
            aliases[n_in + len(aliases)] = idx
    for _ in range(accs):
        out_shape.append(jax.ShapeDtypeStruct((1, width), F32))
        out_specs.append(pl.BlockSpec((1, cw), lambda j, i: (0, j)))
    n_out = len(outs)

    def body(*refs):
        vals = [r[...] for r in refs[:n_in]]
        out_refs = refs[n_in + len(aliased):]
        res = fn(pl.program_id(0), *vals) if with_col else fn(*vals)
        res = res if isinstance(res, (tuple, list)) else (res,)
        assert len(res) == n_out + accs, (name, len(res))
        for r, v in zip(out_refs[:n_out], res[:n_out]):
            r[...] = v.astype(r.dtype)
        first = pl.program_id(1) == 0
        for r, v in zip(out_refs[n_out:], res[n_out:]):
            s = jnp.sum(v, axis=0, keepdims=True)

            @pl.when(first)
            def _(r=r, s=s):
                r[...] = s

            @pl.when(jnp.logical_not(first))
            def _(r=r, s=s):
                r[...] += s

    return pl.pallas_call(
        body, name=name, out_shape=out_shape, grid=(ncol, rows // tl),
        in_specs=in_specs, out_specs=out_specs, input_output_aliases=aliases,
        compiler_params=_params(("parallel", "arbitrary"), 3 * vmem),
    )(*[t[0] for t in tiles], *[v for v, _ in vecs], *aliased)


def _rmsnorm_fwd(x, w_row, name):
    rows, d = x.shape
    tl = _tile(rows, 512, SUBLANES)

    def body(x_ref, w_ref, h_ref):
        xv = x_ref[...]
        rstd = lax.rsqrt(jnp.mean(xv * xv, axis=-1, keepdims=True) + NORM_EPS)
        h_ref[...] = (xv * rstd * w_ref[...]).astype(BF16)

    return pl.pallas_call(
        body, name=name, out_shape=jax.ShapeDtypeStruct((rows, d), BF16), grid=(rows // tl,),
        in_specs=[pl.BlockSpec((tl, d), lambda i: (i, 0)), pl.BlockSpec((1, d), lambda i: (0, 0))],
        out_specs=pl.BlockSpec((tl, d), lambda i: (i, 0)),
        compiler_params=_params(("parallel",)),
    )(x, w_row)


def _rmsnorm_bwd(x, w_row, dh, dout, name, rider=None):
    rows, d = x.shape
    tl = _tile(rows, 256, SUBLANES)

    def body(x_ref, w_ref, dh_ref, dout_ref, gx_ref, gw_ref):
        xv = x_ref[...]
        rstd = lax.rsqrt(jnp.mean(xv * xv, axis=-1, keepdims=True) + NORM_EPS)
        xn = xv * rstd
        dhv = dh_ref[...]
        dxn = dhv * w_ref[...]
        dx = rstd * (dxn - xn * jnp.mean(dxn * xn, axis=-1, keepdims=True))
        gx_ref[...] = dout_ref[...] + dx
        gw = jnp.sum(dhv * xn, axis=0, keepdims=True)

        @pl.when(pl.program_id(0) == 0)
        def _():
            gw_ref[...] = gw

        @pl.when(pl.program_id(0) > 0)
        def _():
            gw_ref[...] += gw

    tile = pl.BlockSpec((tl, d), lambda i: (i, 0))
    row = pl.BlockSpec((1, d), lambda i: (0, 0))
    res, landed = _call(
        body, [x, w_row, dh, dout], name=name,
        out_shape=[jax.ShapeDtypeStruct((rows, d), F32), jax.ShapeDtypeStruct((1, d), F32)],
        grid=(rows // tl,), in_specs=[tile, row, tile, tile], out_specs=[tile, row],
        semantics=("arbitrary",), rider=rider)
    return res if rider is None else (res, landed)


def _head_norm(v, w_row):
    rstd = lax.rsqrt(jnp.mean(v * v, axis=-1, keepdims=True) + NORM_EPS)
    vn = v * rstd
    return vn, vn * w_row, rstd


def _attn_specs(attn_w, kv_w):
    half = attn_w // 2
    kcol, vcol = attn_w // kv_w, attn_w // kv_w + 1
    gcol = (attn_w + 2 * kv_w) // half
    prev = lambda i: jnp.maximum(i - 1, 0)
    return [
        pl.BlockSpec((BLOCK, attn_w), lambda i: (i, 0)),
        pl.BlockSpec((BLOCK, kv_w), lambda i: (prev(i), kcol)),
        pl.BlockSpec((BLOCK, kv_w), lambda i: (i, kcol)),
        pl.BlockSpec((BLOCK, kv_w), lambda i: (prev(i), vcol)),
        pl.BlockSpec((BLOCK, kv_w), lambda i: (i, vcol)),
        pl.BlockSpec((BLOCK, half), lambda i: (i, gcol)),
        pl.BlockSpec((BLOCK, half), lambda i: (i, gcol + 1)),
    ]


def _band_mask(i):
    q_loc = lax.broadcasted_iota(jnp.int32, (BLOCK, 2 * BLOCK), 0) + BLOCK
    k_loc = lax.broadcasted_iota(jnp.int32, (BLOCK, 2 * BLOCK), 1)
    diff = q_loc - k_loc
    first_key = jnp.where(i == 0, BLOCK, 0)
    return (diff >= 0) & (diff < BLOCK) & (k_loc >= first_key)


def _softmax_with_sink(s, sink):
    m = jnp.maximum(jnp.max(s, axis=-1, keepdims=True), sink)
    p = jnp.exp(s - m)
    e_sink = jnp.exp(sink - m)
    den = jnp.sum(p, axis=-1, keepdims=True) + e_sink
    inv = 1.0 / den
    return p * inv, e_sink * inv


def _attention_fwd(proj, qw_row, kw_row, sinks, *, attn_w, kv_w, name):
    rows = proj.shape[0]
    n_q = attn_w // HEAD_DIM
    per_kv = n_q // N_KV_HEADS
    scale = 1.0 / math.sqrt(HEAD_DIM)

    def body(q_ref, kp_ref, kc_ref, vp_ref, vc_ref, glo_ref, ghi_ref, qw_ref, kw_ref, sink_ref, o_ref):
        i = pl.program_id(0)
        valid = _band_mask(i)
        q = q_ref[...]
        kk = jnp.concatenate([kp_ref[...], kc_ref[...]], axis=0)
        vv = jnp.concatenate([vp_ref[...], vc_ref[...]], axis=0)
        gate = jnp.concatenate([glo_ref[...], ghi_ref[...]], axis=1)
        heads = []
        for g in range(N_KV_HEADS):
            sl = slice(g * HEAD_DIM, (g + 1) * HEAD_DIM)
            _, kh, _ = _head_norm(kk[:, sl], kw_ref[...])
            kh = kh.astype(BF16)
            vh = vv[:, sl].astype(BF16)
            for r in range(per_kv):
                h = g * per_kv + r
                hs = slice(h * HEAD_DIM, (h + 1) * HEAD_DIM)
                _, qh, _ = _head_norm(q[:, hs], qw_ref[...])
                s = _dot(qh.astype(BF16), kh, NT) * scale
                s = jnp.where(valid, s, -1e30)
                p, _ = _softmax_with_sink(s, sink_ref[h])
                heads.append(_dot(p.astype(BF16), vh, NN))
        attn = jnp.concatenate(heads, axis=1)
        o_ref[...] = (attn * _silu(gate)).astype(BF16)

    vec = pl.BlockSpec((1, HEAD_DIM), lambda i: (0, 0))
    return pl.pallas_call(
        body, name=name, out_shape=jax.ShapeDtypeStruct((rows, attn_w), BF16), grid=(rows // BLOCK,),
        in_specs=_attn_specs(attn_w, kv_w) + [vec, vec, pl.BlockSpec(memory_space=pltpu.SMEM)],
        out_specs=pl.BlockSpec((BLOCK, attn_w), lambda i: (i, 0)),
        compiler_params=_params(("parallel",)),
    )(proj, proj, proj, proj, proj, proj, proj, qw_row, kw_row, sinks)


def _attention_bwd(proj, d_ag, dproj, qw_row, kw_row, sinks, *, attn_w, kv_w, name, rider=None):
    rows = proj.shape[0]
    nb = rows // BLOCK
    n_q = attn_w // HEAD_DIM
    per_kv = n_q // N_KV_HEADS
    scale = 1.0 / math.sqrt(HEAD_DIM)
    w_out = 2 * attn_w + 2 * kv_w

    def body(q_ref, kp_ref, kc_ref, vp_ref, vc_ref, glo_ref, ghi_ref, dag_ref, qw_ref, kw_ref, sink_ref, _,
             dp_ref, dkv_ref, gqw_ref, gkw_ref, gs_ref):
        i = pl.program_id(0)
        valid = _band_mask(i)
        q = q_ref[...]
        kk = jnp.concatenate([kp_ref[...], kc_ref[...]], axis=0)
        vv = jnp.concatenate([vp_ref[...], vc_ref[...]], axis=0)
        gate = jnp.concatenate([glo_ref[...], ghi_ref[...]], axis=1)
        d_ag_v = dag_ref[...]
        qw, kw = qw_ref[...], kw_ref[...]
        lane = lax.broadcasted_iota(jnp.int32, (SUBLANES, LANES), 1)
        sub = lax.broadcasted_iota(jnp.int32, (SUBLANES, LANES), 0)
        gqw = jnp.zeros((1, HEAD_DIM), F32)
        gkw = jnp.zeros((1, HEAD_DIM), F32)
        gsink = jnp.zeros((SUBLANES, LANES), F32)
        dq_heads, dgate_heads, dk_heads, dv_heads = [], [], [], []
        for g in range(N_KV_HEADS):
            sl = slice(g * HEAD_DIM, (g + 1) * HEAD_DIM)
            kn, kh, k_rstd = _head_norm(kk[:, sl], kw)
            kh = kh.astype(BF16)
            vh = vv[:, sl].astype(BF16)
            dkh = jnp.zeros((2 * BLOCK, HEAD_DIM), F32)
            dvh = jnp.zeros((2 * BLOCK, HEAD_DIM), F32)
            for r in range(per_kv):
                h = g * per_kv + r
                hs = slice(h * HEAD_DIM, (h + 1) * HEAD_DIM)
                qn, qh, q_rstd = _head_norm(q[:, hs], qw)
                qh = qh.astype(BF16)
                s = _dot(qh, kh, NT) * scale
                s = jnp.where(valid, s, -1e30)
                p, p_sink = _softmax_with_sink(s, sink_ref[h])
                pb = p.astype(BF16)
                o = _dot(pb, vh, NN)
                gate_h = gate[:, hs]
                d_ag_h = d_ag_v[:, hs]
                dgate_heads.append(d_ag_h * o * _dsilu(gate_h))
                do = (d_ag_h * _silu(gate_h)).astype(BF16)
                dp = _dot(do, vh, NT)
                delta = jnp.sum(p * dp, axis=-1, keepdims=True)
                ds = (p * (dp - delta) * scale).astype(BF16)
                gs_h = jnp.sum(-p_sink * delta, axis=0, keepdims=True)
                gsink = gsink + jnp.where((lane == h) & (sub == 0), gs_h, 0.0)
                dvh = dvh + _dot(pb, do, TN)
                dkh = dkh + _dot(ds, qh, TN)
                dqh = _dot(ds, kh, NN)
                gqw = gqw + jnp.sum(dqh * qn, axis=0, keepdims=True)
                dqn = dqh * qw
                dq_heads.append(q_rstd * (dqn - qn * jnp.mean(dqn * qn, axis=-1, keepdims=True)))
            gkw = gkw + jnp.sum(dkh * kn, axis=0, keepdims=True)
            dkn = dkh * kw
            dk_heads.append(k_rstd * (dkn - kn * jnp.mean(dkn * kn, axis=-1, keepdims=True)))
            dv_heads.append(dvh)
        dp_ref[:, 0:attn_w] = jnp.concatenate(dq_heads, axis=1).astype(BF16)
        dp_ref[:, attn_w:attn_w + 2 * kv_w] = jnp.zeros((BLOCK, 2 * kv_w), BF16)
        dp_ref[:, attn_w + 2 * kv_w:w_out] = jnp.concatenate(dgate_heads, axis=1).astype(BF16)
        dkv_ref[0] = jnp.concatenate(dk_heads + dv_heads, axis=1)

        @pl.when(i == 0)
        def _():
            gqw_ref[...] = gqw
            gkw_ref[...] = gkw
            gs_ref[...] = gsink

        @pl.when(i > 0)
        def _():
            gqw_ref[...] += gqw
            gkw_ref[...] += gkw
            gs_ref[...] += gsink

    vec = pl.BlockSpec((1, HEAD_DIM), lambda i: (0, 0))
    res, landed = _call(
        body, [proj, proj, proj, proj, proj, proj, proj, d_ag, qw_row, kw_row, sinks, dproj], name=name,
        out_shape=[jax.ShapeDtypeStruct(dproj.shape, BF16),
                   jax.ShapeDtypeStruct((nb, 2 * BLOCK, 2 * kv_w), F32),
                   jax.ShapeDtypeStruct((1, HEAD_DIM), F32), jax.ShapeDtypeStruct((1, HEAD_DIM), F32),
                   jax.ShapeDtypeStruct((SUBLANES, LANES), F32)],
        grid=(nb,),
        in_specs=_attn_specs(attn_w, kv_w) + [pl.BlockSpec((BLOCK, attn_w), lambda i: (i, 0)), vec, vec,
                                              pl.BlockSpec(memory_space=pltpu.SMEM), _any_spec()],
        out_specs=[pl.BlockSpec((BLOCK, w_out), lambda i: (i, 0)),
                   pl.BlockSpec((1, 2 * BLOCK, 2 * kv_w), lambda i: (i, 0, 0)),
                   vec, vec, pl.BlockSpec((SUBLANES, LANES), lambda i: (0, 0))],
        aliases={11: 0}, semantics=("arbitrary",), vmem=40 * 1024 * 1024, rider=rider)
    return res if rider is None else (res, landed)


def _attention_dkv(dproj, dkv, *, attn_w, kv_w, name):
    rows = dproj.shape[0]
    nb = rows // BLOCK
    col = attn_w // (2 * kv_w)

    def body(cur_ref, nxt_ref, _, o_ref):
        i = pl.program_id(0)
        nxt = jnp.where(i < nb - 1, nxt_ref[0, 0:BLOCK, :], 0.0)
        o_ref[...] = (cur_ref[0, BLOCK:2 * BLOCK, :] + nxt).astype(BF16)

    blk = lambda f: pl.BlockSpec((1, 2 * BLOCK, 2 * kv_w), f)
    return pl.pallas_call(
        body, name=name, out_shape=jax.ShapeDtypeStruct(dproj.shape, BF16), grid=(nb,),
        in_specs=[blk(lambda i: (i, 0, 0)), blk(lambda i: (jnp.minimum(i + 1, nb - 1), 0, 0)), _any_spec()],
        out_specs=pl.BlockSpec((BLOCK, 2 * kv_w), lambda i: (i, col)),
        input_output_aliases={2: 0},
        compiler_params=_params(("parallel",)),
    )(dkv, dkv, dproj)


def _cmul(ar, ai, br, bi):
    return ar * br - ai * bi, ar * bi + ai * br


def _ssm_prep(a_re, a_im, log_dt_col, steps, name):
    assert steps & (steps - 1) == 0

    def body(are_ref, aim_ref, ldt_ref, abr_ref, abi_ref, cfr_ref, cfi_ref, apr_ref, api_ref):
        are, aim = are_ref[...], aim_ref[...]
        dt = jnp.exp(ldt_ref[...])
        mag = jnp.exp(dt * are)
        abr = mag * jnp.cos(dt * aim)
        abi = mag * jnp.sin(dt * aim)
        num_re, num_im = abr - 1.0, abi
        den = are * are + aim * aim
        abr_ref[...] = abr
        abi_ref[...] = abi
        cfr_ref[...] = (num_re * are + num_im * aim) / den
        cfi_ref[...] = (num_im * are - num_re * aim) / den
        pr, pi = abr, abi
        n = steps
        while n > 1:
            pr, pi = _cmul(pr, pi, pr, pi)
            n //= 2
        apr_ref[...] = pr
        api_ref[...] = pi

    shp = jax.ShapeDtypeStruct(a_re.shape, F32)
    return pl.pallas_call(body, name=name, out_shape=[shp] * 6)(a_re, a_im, log_dt_col)


def _ssm_param_bwd(a_re, a_im, log_dt_col, d_ab_re, d_ab_im, d_cf_re, d_cf_im, name):
    def body(are_ref, aim_ref, ldt_ref, gabr_ref, gabi_ref, gcfr_ref, gcfi_ref, dar_ref, dai_ref, dldt_ref):
        are, aim = are_ref[...], aim_ref[...]
        dt = jnp.exp(ldt_ref[...])
        mag = jnp.exp(dt * are)
        abr = mag * jnp.cos(dt * aim)
        abi = mag * jnp.sin(dt * aim)
        den = are * are + aim * aim
        cfr = ((abr - 1.0) * are + abi * aim) / den
        cfi = (abi * are - (abr - 1.0) * aim) / den
        gabr, gabi = jnp.sum(gabr_ref[...], axis=0), jnp.sum(gabi_ref[...], axis=0)
        gcfr, gcfi = jnp.sum(gcfr_ref[...], axis=0), jnp.sum(gcfi_ref[...], axis=0)
        inv_r, inv_i = are / den, -aim / den
        t_r, t_i = _cmul(inv_r, -inv_i, gcfr, gcfi)
        gabr, gabi = gabr + t_r, gabi + t_i
        q_r, q_i = _cmul(cfr, cfi, inv_r, inv_i)
        da_r, da_i = _cmul(-q_r, q_i, gcfr, gcfi)
        gz_r, gz_i = _cmul(abr, -abi, gabr, gabi)
        dar_ref[...] = da_r + dt * gz_r
        dai_ref[...] = da_i + dt * gz_i
        dldt_ref[...] = dt * jnp.sum(are * gz_r + aim * gz_i, axis=-1, keepdims=True)

    shp = jax.ShapeDtypeStruct(a_re.shape, F32)
    return pl.pallas_call(body, name=name, out_shape=[shp, shp, jax.ShapeDtypeStruct(log_dt_col.shape, F32)])(
        a_re, a_im, log_dt_col, d_ab_re, d_ab_im, d_cf_re, d_cf_im)


SCAN_LANES = 512
W_IN_GRAD_PARTS = 4


def _scan_segments(xr_ref, xi_ref, a_re, a_im, ap_re, ap_im, carry_re, carry_im, cm_re, cm_im, steps, reverse):
    n = xr_ref.shape[1]
    order = range(steps - 1, -1, -1) if reverse else range(steps)
    seg_order = range(SUBLANES - 1, -1, -1) if reverse else range(SUBLANES)
    for c0 in range(0, n, SCAN_LANES):
        ls = slice(c0, c0 + SCAN_LANES)
        ar = jnp.broadcast_to(a_re[:, ls], (SUBLANES, SCAN_LANES))
        ai = jnp.broadcast_to(a_im[:, ls], (SUBLANES, SCAN_LANES))

        def local(t, s, ar=ar, ai=ai, ls=ls):
            j = steps - 1 - t if reverse else t
            r0 = pl.multiple_of(j * SUBLANES, SUBLANES)
            sr, si = _cmul(ar, ai, s[0], s[1])
            sr = sr + xr_ref[pl.ds(r0, SUBLANES), ls]
            si = si + xi_ref[pl.ds(r0, SUBLANES), ls]
            xr_ref[pl.ds(r0, SUBLANES), ls] = sr
            xi_ref[pl.ds(r0, SUBLANES), ls] = si
            return sr, si

        zero = jnp.zeros((SUBLANES, SCAN_LANES), F32)
        end_r, end_i = lax.fori_loop(0, steps, local, (zero, zero))
        cr, ci = carry_re[:, ls], carry_im[:, ls]
        apr, api = ap_re[:, ls], ap_im[:, ls]
        for r in seg_order:
            cm_re[r:r + 1, ls] = cr
            cm_im[r:r + 1, ls] = ci
            tr, ti = _cmul(apr, api, cr, ci)
            cr, ci = end_r[r:r + 1, :] + tr, end_i[r:r + 1, :] + ti
        carry_re[:, ls] = cr
        carry_im[:, ls] = ci

        def fix(t, s, ar=ar, ai=ai, ls=ls):
            j = steps - 1 - t if reverse else t
            r0 = pl.multiple_of(j * SUBLANES, SUBLANES)
            sr, si = _cmul(ar, ai, s[0], s[1])
            xr_ref[pl.ds(r0, SUBLANES), ls] += sr
            xi_ref[pl.ds(r0, SUBLANES), ls] += si
            return sr, si

        lax.fori_loop(0, steps, fix, (cm_re[:, ls], cm_im[:, ls]))
    del order


def _ssm_blocks(b_re, b_im, c_re, c_im):
    g = b_re.shape[0]
    per = MXU_DIM // GROUP
    nsb = g // per
    eye = jnp.eye(per, dtype=F32)

    def b_blocks(b):
        bt = b.reshape(nsb, per, STATE, GROUP).transpose(0, 1, 3, 2)
        return (bt[:, :, :, None, :] * eye[None, :, None, :, None]).reshape(nsb, per * GROUP, per * STATE).astype(BF16)

    def c_blocks(cm):
        ct = cm.reshape(nsb, per, GROUP, STATE).transpose(0, 1, 3, 2)
        return (ct[:, :, :, None, :] * eye[None, :, None, :, None]).reshape(nsb, per * STATE, per * GROUP).astype(BF16)

    return b_blocks(b_re), b_blocks(b_im), c_blocks(c_re), c_blocks(c_im)


def _unblock_b(db):
    nsb = db.shape[0]
    per = MXU_DIM // GROUP
    d = db.reshape(nsb, per, GROUP, per, STATE)
    d = jnp.stack([d[:, k, :, k, :] for k in range(per)], axis=1)
    return d.transpose(0, 1, 3, 2).reshape(nsb * per, STATE, GROUP)


def _unblock_c(dc):
    nsb = dc.shape[0]
    per = MXU_DIM // GROUP
    d = dc.reshape(nsb, per, STATE, per, GROUP)
    d = jnp.stack([d[:, k, :, k, :] for k in range(per)], axis=1)
    return d.transpose(0, 1, 3, 2).reshape(nsb * per, GROUP, STATE)


def _to_segments(v, chunk):
    rows, w = v.shape
    return v.reshape(rows // chunk, SUBLANES, chunk // SUBLANES, w).transpose(0, 2, 1, 3).reshape(rows, w)


def _from_segments(v, chunk):
    rows, w = v.shape
    return v.reshape(rows // chunk, chunk // SUBLANES, SUBLANES, w).transpose(0, 2, 1, 3).reshape(rows, w)


def _ssm_fwd(u, blocks, rows_p, d_row, *, chunk, name):
    rows, w = u.shape
    nc = rows // chunk
    steps = chunk // SUBLANES
    bre, bim, cre, cim = blocks
    nsb = bre.shape[0]
    n_state = nsb * bre.shape[2]
    sbw, sbs = bre.shape[1], bre.shape[2]

    def body(u_ref, bre_hbm, bim_hbm, cre_hbm, cim_hbm, abr_ref, abi_ref, cfr_ref, cfi_ref, apr_ref, api_ref,
             d_ref, y_ref, str_ref, sti_ref, bre_ref, bim_ref, cre_ref, cim_ref, sr, si, carry_r, carry_i,
             cm_r, cm_i):
        @pl.when(pl.program_id(0) == 0)
        def _():
            for src, dst in ((bre_hbm, bre_ref), (bim_hbm, bim_ref), (cre_hbm, cre_ref), (cim_hbm, cim_ref)):
                pltpu.sync_copy(src, dst)
            carry_r[...] = jnp.zeros_like(carry_r)
            carry_i[...] = jnp.zeros_like(carry_i)

        str_ref[0] = carry_r[...]
        sti_ref[0] = carry_i[...]
        for sb in range(nsb):
            us = slice(sb * sbw, (sb + 1) * sbw)
            ss = slice(sb * sbs, (sb + 1) * sbs)
            ub = u_ref[:, us].astype(BF16)
            bur = _dot(ub, bre_ref[sb], NN)
            bui = _dot(ub, bim_ref[sb], NN)
            xr, xi = _cmul(cfr_ref[:, ss], cfi_ref[:, ss], bur, bui)
            sr[:, ss] = xr
            si[:, ss] = xi
        _scan_segments(sr, si, abr_ref[...], abi_ref[...], apr_ref[...], api_ref[...],
                       carry_r, carry_i, cm_r, cm_i, steps, False)
        for sb in range(nsb):
            us = slice(sb * sbw, (sb + 1) * sbw)
            ss = slice(sb * sbs, (sb + 1) * sbs)
            y = _dot(sr[:, ss].astype(BF16), cre_ref[sb], NN) - _dot(si[:, ss].astype(BF16), cim_ref[sb], NN)
            y_ref[:, us] = y + d_ref[:, us] * u_ref[:, us]

    row_n = pl.BlockSpec((1, n_state), lambda c: (0, 0))
    st = pl.BlockSpec((1, 1, n_state), lambda c: (c, 0, 0))
    held = [pltpu.VMEM(b.shape, BF16) for b in blocks]
    vmem = 2 * sum(_nbytes(b.shape, BF16) for b in blocks) + 3 * _nbytes((chunk, n_state), F32)
    return pl.pallas_call(
        body, name=name,
        out_shape=[jax.ShapeDtypeStruct((rows, w), F32), jax.ShapeDtypeStruct((nc, 1, n_state), F32),
                   jax.ShapeDtypeStruct((nc, 1, n_state), F32)],
        grid=(nc,),
        in_specs=[pl.BlockSpec((chunk, w), lambda c: (c, 0))] + [_any_spec()] * 4
        + [row_n] * 6 + [pl.BlockSpec((1, w), lambda c: (0, 0))],
        out_specs=[pl.BlockSpec((chunk, w), lambda c: (c, 0)), st, st],
        scratch_shapes=held + [pltpu.VMEM((chunk, n_state), F32), pltpu.VMEM((chunk, n_state), F32),
                               pltpu.VMEM((1, n_state), F32), pltpu.VMEM((1, n_state), F32),
                               pltpu.VMEM((SUBLANES, n_state), F32), pltpu.VMEM((SUBLANES, n_state), F32)],
        compiler_params=_params(("arbitrary",), vmem),
    )(u, bre, bim, cre, cim, *rows_p, d_row)


def _ssm_bwd(u, y, dyg, st_re, st_im, blocks, rows_p, d_row, *, chunk, name, rider=None):
    rows, w = u.shape
    nc = rows // chunk
    steps = chunk // SUBLANES
    bre, bim, cre, cim = blocks
    nsb = bre.shape[0]
    sbw, sbs = bre.shape[1], bre.shape[2]
    n_state = nsb * sbs

    def body(u_ref, y_ref, dyg_ref, str_ref, sti_ref, bre_hbm, bim_hbm, cre_hbm, cim_hbm,
             abr_ref, abi_ref, cfr_ref, cfi_ref, apr_ref, api_ref, d_ref,
             du_ref, dbre_hbm, dbim_hbm, dcre_hbm, dcim_hbm, gabr_ref, gabi_ref, gcfr_ref, gcfi_ref, dd_ref,
             bre_ref, bim_ref, cre_ref, cim_ref, dbre_ref, dbim_ref, dcre_ref, dcim_ref,
             bur, bui, sr, si, lr, li, carry_r, carry_i, lam_r, lam_i, cm_r, cm_i, cl_r, cl_i):
        first = pl.program_id(0) == 0

        @pl.when(first)
        def _():
            for src, dst in ((bre_hbm, bre_ref), (bim_hbm, bim_ref), (cre_hbm, cre_ref), (cim_hbm, cim_ref)):
                pltpu.sync_copy(src, dst)
            lam_r[...] = jnp.zeros_like(lam_r)
            lam_i[...] = jnp.zeros_like(lam_i)
            for ref in (dbre_ref, dbim_ref, dcre_ref, dcim_ref, gabr_ref, gabi_ref, gcfr_ref, gcfi_ref, dd_ref):
                ref[...] = jnp.zeros_like(ref)

        uv = u_ref[...]
        dy = dyg_ref[...] * _dgelu(y_ref[...])
        dd_ref[...] += jnp.sum(dy * uv, axis=0, keepdims=True)
        dyb = dy.astype(BF16)
        ub = uv.astype(BF16)
        carry_r[...] = str_ref[0]
        carry_i[...] = sti_ref[0]
        for sb in range(nsb):
            us = slice(sb * sbw, (sb + 1) * sbw)
            ss = slice(sb * sbs, (sb + 1) * sbs)
            br = _dot(ub[:, us], bre_ref[sb], NN)
            bi = _dot(ub[:, us], bim_ref[sb], NN)
            bur[:, ss] = br
            bui[:, ss] = bi
            xr, xi = _cmul(cfr_ref[:, ss], cfi_ref[:, ss], br, bi)
            sr[:, ss] = xr
            si[:, ss] = xi
            lr[:, ss] = _dot(dyb[:, us], cre_ref[sb], NT)
            li[:, ss] = -_dot(dyb[:, us], cim_ref[sb], NT)
        abr, abi = abr_ref[...], abi_ref[...]
        apr, api = apr_ref[...], api_ref[...]
        _scan_segments(sr, si, abr, abi, apr, api, carry_r, carry_i, cm_r, cm_i, steps, False)
        for sb in range(nsb):
            us = slice(sb * sbw, (sb + 1) * sbw)
            ss = slice(sb * sbs, (sb + 1) * sbs)
            dcre_ref[sb] += _dot(sr[:, ss].astype(BF16), dyb[:, us], TN)
            dcim_ref[sb] -= _dot(si[:, ss].astype(BF16), dyb[:, us], TN)
        _scan_segments(lr, li, abr, -abi, apr, -api, lam_r, lam_i, cl_r, cl_i, steps, True)
        for c0 in range(0, n_state, SCAN_LANES):
            ls = slice(c0, c0 + SCAN_LANES)
            cfr = jnp.broadcast_to(cfr_ref[:, ls], (SUBLANES, SCAN_LANES))
            cfi = jnp.broadcast_to(cfi_ref[:, ls], (SUBLANES, SCAN_LANES))

            def step(j, acc, ls=ls, cfr=cfr, cfi=cfi):
                gar, gai, gcr, gci, pr, pi = acc
                r0 = pl.multiple_of(j * SUBLANES, SUBLANES)
                rws = pl.ds(r0, SUBLANES)
                l_r, l_i = lr[rws, ls], li[rws, ls]
                t_r, t_i = _cmul(pr, -pi, l_r, l_i)
                b_r, b_i = bur[rws, ls], bui[rws, ls]
                c_r, c_i = _cmul(b_r, -b_i, l_r, l_i)
                x_r, x_i = _cmul(cfr, -cfi, l_r, l_i)
                bur[rws, ls] = x_r
                bui[rws, ls] = x_i
                return gar + t_r, gai + t_i, gcr + c_r, gci + c_i, sr[rws, ls], si[rws, ls]

            zero = jnp.zeros((SUBLANES, SCAN_LANES), F32)
            gar, gai, gcr, gci, _, _ = lax.fori_loop(
                0, steps, step, (zero, zero, zero, zero, cm_r[:, ls], cm_i[:, ls]))
            gabr_ref[:, ls] += gar
            gabi_ref[:, ls] += gai
            gcfr_ref[:, ls] += gcr
            gcfi_ref[:, ls] += gci
        for sb in range(nsb):
            us = slice(sb * sbw, (sb + 1) * sbw)
            ss = slice(sb * sbs, (sb + 1) * sbs)
            xr, xi = bur[:, ss].astype(BF16), bui[:, ss].astype(BF16)
            du = _dot(xr, bre_ref[sb], NT) + _dot(xi, bim_ref[sb], NT)
            du_ref[:, us] = du + d_ref[:, us] * dy[:, us]
            dbre_ref[sb] += _dot(ub[:, us], xr, TN)
            dbim_ref[sb] += _dot(ub[:, us], xi, TN)

        @pl.when(pl.program_id(0) == nc - 1)
        def _():
            for src, dst in ((dbre_ref, dbre_hbm), (dbim_ref, dbim_hbm), (dcre_ref, dcre_hbm), (dcim_ref, dcim_hbm)):
                pltpu.sync_copy(src, dst)

    rev = lambda c: nc - 1 - c
    tile = pl.BlockSpec((chunk, w), lambda c: (rev(c), 0))
    row_n = pl.BlockSpec((1, n_state), lambda c: (0, 0))
    row_w = pl.BlockSpec((1, w), lambda c: (0, 0))
    st = pl.BlockSpec((1, 1, n_state), lambda c: (rev(c), 0, 0))
    acc8 = pl.BlockSpec((SUBLANES, n_state), lambda c: (0, 0))
    big = pltpu.VMEM((chunk, n_state), F32)
    row = pltpu.VMEM((1, n_state), F32)
    eight = pltpu.VMEM((SUBLANES, n_state), F32)
    f32 = lambda a: jax.ShapeDtypeStruct(a.shape, F32)
    held = [pltpu.VMEM(b.shape, BF16) for b in blocks] + [pltpu.VMEM(b.shape, F32) for b in blocks]
    vmem = (sum(_nbytes(b.shape, BF16) + _nbytes(b.shape, F32) for b in blocks)
            + 7 * _nbytes((chunk, n_state), F32) + 16 * _nbytes((chunk, w), F32))
    res, landed = _call(
        body, [u, y, dyg, st_re, st_im, bre, bim, cre, cim, *rows_p, d_row], name=name,
        out_shape=[jax.ShapeDtypeStruct((rows, w), F32), f32(bre), f32(bim), f32(cre), f32(cim)]
        + [jax.ShapeDtypeStruct((SUBLANES, n_state), F32)] * 4 + [jax.ShapeDtypeStruct((1, w), F32)],
        grid=(nc,),
        in_specs=[tile, tile, tile, st, st] + [_any_spec()] * 4 + [row_n] * 6 + [row_w],
        out_specs=[tile] + [_any_spec()] * 4 + [acc8] * 4 + [row_w],
        scratch_shapes=held + [big] * 6 + [row] * 4 + [eight] * 4,
        semantics=("arbitrary",), vmem=vmem, rider=rider)
    return res if rider is None else (res, landed)


def _loss_grad(x, mm, target, name):
    rows, d = x.shape

    def fn(xv, mv, tv):
        err = xv + mv - tv
        g = err * (1.0 / d)
        return g, g, 0.5 * err * g

    return _ew(fn, name=name, rows=rows, width=d, tiles=[(x, 0), (mm, 0), (target, 0)],
               outs=[(F32, d, 0), (BF16, d, 0)], accs=1)


def _pair_sum(grad, recv, name):
    r4, cdim = recv.shape
    r = r4 // N_CHIPS
    tr = _tile(r, 544, 16)
    g4 = grad.reshape(N_CHIPS, 2, r, cdim)
    r3 = recv.reshape(N_CHIPS, r, cdim)
    core = jnp.reshape(lax.axis_index("c"), (1,)).astype(jnp.int32)

    def body(c_ref, g_ref, r_ref, o_ref):
        o_ref[...] = (g_ref[0] + r_ref[...]).astype(BF16)

    out = pl.pallas_call(
        body, name=name, out_shape=jax.ShapeDtypeStruct((N_CHIPS, r, cdim), BF16),
        grid_spec=pltpu.PrefetchScalarGridSpec(
            num_scalar_prefetch=1, grid=(N_CHIPS, r // tr),
            in_specs=[pl.BlockSpec((1, 1, tr, cdim), lambda j, i, c: (j, c[0], i, 0)),
                      pl.BlockSpec((1, tr, cdim), lambda j, i, c: (j, i, 0))],
            out_specs=pl.BlockSpec((1, tr, cdim), lambda j, i, c: (j, i, 0))),
        compiler_params=_params(("parallel", "parallel"), 6 * _nbytes((tr, cdim), F32)),
    )(core, g4, r3)
    return out.reshape(r4, cdim)


def _chip_sum(recv, name):
    r4, cdim = recv.shape
    r = r4 // N_CHIPS
    tr = _tile(r, 544, 16)
    r3 = recv.reshape(N_CHIPS, r, cdim)

    def body(r_ref, o_ref):
        acc = r_ref[0].astype(F32)
        for j in range(1, N_CHIPS):
            acc = acc + r_ref[j].astype(F32)
        o_ref[...] = acc

    return pl.pallas_call(
        body, name=name, out_shape=jax.ShapeDtypeStruct((r, cdim), F32), grid=(r // tr,),
        in_specs=[pl.BlockSpec((N_CHIPS, tr, cdim), lambda i: (0, i, 0))],
        out_specs=pl.BlockSpec((tr, cdim), lambda i: (i, 0)),
        compiler_params=_params(("parallel",), 8 * _nbytes((tr, cdim), F32)),
    )(r3)


def _adamw_math(w, g, m, v):
    m = ADAM_B1 * m + (1.0 - ADAM_B1) * g
    v = ADAM_B2 * v + (1.0 - ADAM_B2) * (g * g)
    m_hat = m / (1.0 - ADAM_B1 ** ADAM_STEP)
    v_hat = v / (1.0 - ADAM_B2 ** ADAM_STEP)
    delta = -ADAM_LR * (m_hat / (jnp.sqrt(v_hat) + ADAM_EPS) + ADAM_WD * w)
    return delta, m, v


def _adamw(w, g, m, v, name):
    rows, cols = w.shape
    tr = _tile(rows, 256, SUBLANES)

    def body(w_ref, g_ref, m_ref, v_ref, d_ref, nm_ref, nv_ref):
        d, nm, nv = _adamw_math(w_ref[...], g_ref[...], m_ref[...], v_ref[...])
        d_ref[...] = d
        nm_ref[...] = nm
        nv_ref[...] = nv

    spec = pl.BlockSpec((tr, cols), lambda i: (i, 0))
    shp = jax.ShapeDtypeStruct((rows, cols), F32)
    return pl.pallas_call(
        body, name=name, out_shape=[shp] * 3, grid=(rows // tr,), in_specs=[spec] * 4, out_specs=[spec] * 3,
        compiler_params=_params(("parallel",)),
    )(w, g, m, v)


def _adamw_small(w, parts, m, v, name):
    rows, cols = w.shape
    p3 = parts.reshape(N_DEV, rows, cols)

    def body(w_ref, p_ref, m_ref, v_ref, g_ref, d_ref, nm_ref, nv_ref):
        g = p_ref[0]
        for k in range(1, N_DEV):
            g = g + p_ref[k]
        d, nm, nv = _adamw_math(w_ref[...], g, m_ref[...], v_ref[...])
        g_ref[...] = g
        d_ref[...] = d
        nm_ref[...] = nm
        nv_ref[...] = nv

    shp = jax.ShapeDtypeStruct((rows, cols), F32)
    return pl.pallas_call(body, name=name, out_shape=[shp] * 4)(w, p3, m, v)


SMALL = ("norm_w", "q_norm_w", "k_norm_w", "sinks", "A_re", "A_im", "log_dt", "B_re", "B_im", "C_re", "C_im",
         "D_skip", "b_glu")
LARGE = ("w_in", "w_attn_proj", "w_glu", "w_ssm_proj", "w_out")
ORDER = ("norm_w", "w_in", "q_norm_w", "k_norm_w", "sinks", "w_attn_proj", "A_re", "A_im", "log_dt", "B_re", "B_im",
         "C_re", "C_im", "D_skip", "w_glu", "b_glu", "w_ssm_proj", "w_out")


SMALL_REST = ("loss",) + SMALL[1:]


def _pack(named, keys):
    flat = jnp.concatenate([named[k].reshape(-1).astype(F32) for k in keys])
    n = flat.shape[0]
    rows = -(-n // (LANES * SUBLANES)) * SUBLANES
    return jnp.pad(flat, (0, rows * LANES - n)).reshape(rows, LANES)


def _unpack(packed, like, keys):
    flat = packed.reshape(-1)
    out, o = {}, 0
    for k in keys:
        n = like[k].size
        out[k] = flat[o:o + n].reshape(like[k].shape)
        o += n
    return out


def _step(xs, target, p, shards):
    s_in, s_ap, s_glu, s_sp, s_o = shards
    (w_in_t,) = _exchange(_all_gather([s_in]), "gather_w_in")
    seq, d = xs.shape
    attn_w = (d // 128) * HEAD_DIM
    n_q = attn_w // HEAD_DIM
    kv_w = N_KV_HEADS * HEAD_DIM
    ssm_w = d // 2
    n_groups = ssm_w // GROUP
    n_state = n_groups * STATE
    in_w = w_in_t.shape[0]
    assert in_w == 2 * attn_w + 2 * kv_w + 2 * ssm_w + 2 * d
    o_u = 2 * attn_w + 2 * kv_w
    o_z = o_u + ssm_w
    o_ga = o_z + ssm_w
    chunk = min(BLOCK, seq)
    cw = d // 4

    norm_row = p["norm_w"].reshape(1, d)
    h = _rmsnorm_fwd(xs, norm_row, "rmsnorm_fwd")
    proj, (w_ap_t, w_glu_t, w_sp_t, w_o) = _matmul(h, w_in_t, mode="nt", name="in_proj", tn=512,
                                                   rider=_all_gather([s_ap, s_glu, s_sp, s_o]))
    qw_row, kw_row = p["q_norm_w"].reshape(1, HEAD_DIM), p["k_norm_w"].reshape(1, HEAD_DIM)
    ag = _attention_fwd(proj, qw_row, kw_row, p["sinks"], attn_w=attn_w, kv_w=kv_w, name="attention_fwd")

    log_dt_col = p["log_dt"].reshape(n_groups, 1)
    prep = _ssm_prep(p["A_re"], p["A_im"], log_dt_col, chunk // SUBLANES, "ssm_prep")
    rows_p = [v.reshape(1, n_state) for v in prep]
    blocks = _ssm_blocks(p["B_re"], p["B_im"], p["C_re"], p["C_im"])
    d_row = p["D_skip"].reshape(1, ssm_w)
    u_seg = _to_segments(proj[:, o_u:o_u + ssm_w], chunk)
    y_seg, st_re, st_im = _ssm_fwd(u_seg, blocks, rows_p, d_row, chunk=chunk, name="ssm_fwd")
    y_ssm = _from_segments(y_seg, chunk)
    (yg,) = _ew(_gelu, name="gelu", rows=seq, width=ssm_w, tiles=[(y_ssm, 0)], outs=[(BF16, ssm_w, 0)], cw=cw)
    glu = _matmul(yg, w_glu_t, mode="nt", name="glu_proj", bias=p["b_glu"].reshape(1, 2 * ssm_w))
    (ts,) = _ew(lambda ga, gb, z: ga * _sigmoid(gb) * _silu(z), name="glu_gate", rows=seq, width=ssm_w,
                tiles=[(glu, 0), (glu, ssm_w), (proj, o_z)], outs=[(BF16, ssm_w, 0)], cw=cw)
    yy = _matmul(ag, w_ap_t, mode="nt", name="attn_proj", out_cols=(2 * d, 0))
    yy = _matmul(ts, w_sp_t, mode="nt", name="ssm_proj", out_cols=(2 * d, d), into=yy)
    (merged,) = _ew(lambda ya, ys, ga, gs: _sigmoid(ga) * ya + _sigmoid(gs) * ys, name="merge", rows=seq, width=d,
                    tiles=[(yy, 0), (yy, d), (proj, o_ga), (proj, o_ga + d)], outs=[(BF16, d, 0)], cw=cw)
    mm = _matmul(merged, w_o, mode="nn", name="out_proj")
    dout, dout_b, loss_cols = _loss_grad(xs, mm, target, "loss_grad")
    loss_local = jnp.sum(loss_cols)

    g_w_o = _matmul(merged, dout_b, mode="tn", name="grad_w_out", tk=1024)
    dmerged, (sib_o,) = _matmul(dout_b, w_o, mode="nt", name="d_merged", rider=_sibling_exchange([g_w_o]))
    pair_o = _pair_sum(g_w_o, sib_o, "pair_sum_w_out")

    def merge_bwd(dm, y, g):
        s = _sigmoid(g)
        return dm * s, dm * y * s * (1.0 - s)

    dyy, dproj = _ew(merge_bwd, name="merge_bwd", rows=seq, width=2 * d,
                     tiles=[(dmerged, 0, d), (yy, 0), (proj, o_ga)],
                     outs=[(BF16, 2 * d, 0), (BF16, in_w, o_ga)], cw=cw)
    dy_a, dy_s = Cols(dyy, 0, d), Cols(dyy, d, d)
    g_w_ap_t = _matmul(dy_a, ag, mode="tn", name="grad_w_attn_proj", tk=1024)
    g_w_sp_t = _matmul(dy_s, ts, mode="tn", name="grad_w_ssm_proj", tk=1024)
    d_ag = _matmul(dy_a, w_ap_t, mode="nn", name="d_attn_gated")
    d_ts = _matmul(dy_s, w_sp_t, mode="nn", name="d_ssm_gated")

    (dproj, dkv, g_qw, g_kw, g_sinks), (chips_o, sib_ap, sib_sp) = _attention_bwd(
        proj, d_ag, dproj, qw_row, kw_row, p["sinks"], attn_w=attn_w, kv_w=kv_w, name="attention_bwd",
        rider=_join(_chip_exchange([pair_o]), _sibling_exchange([g_w_ap_t, g_w_sp_t])))
    pair_ap = _pair_sum(g_w_ap_t, sib_ap, "pair_sum_w_attn_proj")
    pair_sp = _pair_sum(g_w_sp_t, sib_sp, "pair_sum_w_ssm_proj")
    dproj = _attention_dkv(dproj, dkv, attn_w=attn_w, kv_w=kv_w, name="attention_dkv")

    n_half = ssm_w // _tile(2 * ssm_w, cw)

    def glu_bwd(j, dt, ga, gb, z):
        sb, sz = _sigmoid(gb), _silu(z)
        dg = jnp.where(j < n_half, dt * sb * sz, dt * ga * sb * (1.0 - sb) * sz)
        return dg, dg

    glu_ops = [(d_ts, 0, ssm_w), (glu, 0, ssm_w), (glu, ssm_w, ssm_w), (proj, o_z, ssm_w)]
    dglu, g_bglu = _ew(glu_bwd, name="glu_bwd", rows=seq, width=2 * ssm_w, tiles=glu_ops,
                       outs=[(BF16, 2 * ssm_w, 0)], accs=1, cw=cw, with_col=True)
    (dproj,) = _ew(lambda dt, ga, gb, z: dt * ga * _sigmoid(gb) * _dsilu(z), name="glu_bwd_z", rows=seq,
                   width=ssm_w, tiles=glu_ops, outs=[(BF16, in_w, o_z)], into=[dproj], cw=cw)
    g_w_glu_t = _matmul(dglu, yg, mode="tn", name="grad_w_glu", tk=1024)
    d_yg = _matmul(dglu, w_glu_t, mode="nn", name="d_gelu")
    ((du_seg, db_re, db_im, dc_re, dc_im, gabr, gabi, gcfr, gcfi, g_d), (chips_ap, chips_sp, sib_glu)) = _ssm_bwd(
        u_seg, y_seg, _to_segments(d_yg, chunk), st_re, st_im, blocks, rows_p, d_row, chunk=chunk, name="ssm_bwd",
        rider=_join(_chip_exchange([pair_ap, pair_sp]), _sibling_exchange([g_w_glu_t])))
    pair_glu = _pair_sum(g_w_glu_t, sib_glu, "pair_sum_w_glu")
    (dproj,) = _ew(lambda v: v, name="du_store", rows=seq, width=ssm_w, tiles=[(_from_segments(du_seg, chunk), 0)],
                   outs=[(BF16, in_w, o_u)], into=[dproj], cw=cw)
    g_a_re, g_a_im, g_log_dt = _ssm_param_bwd(
        p["A_re"], p["A_im"], log_dt_col, *[g.reshape(SUBLANES, n_groups, STATE) for g in (gabr, gabi, gcfr, gcfi)],
        "ssm_param_bwd")
    small_grads = dict(
        loss=loss_local, q_norm_w=g_qw.reshape(HEAD_DIM), k_norm_w=g_kw.reshape(HEAD_DIM),
        sinks=g_sinks[0, :n_q], A_re=g_a_re, A_im=g_a_im, log_dt=g_log_dt.reshape(n_groups),
        B_re=_unblock_b(db_re), B_im=_unblock_b(db_im), C_re=_unblock_c(dc_re), C_im=_unblock_c(dc_im),
        D_skip=g_d.reshape(n_groups, GROUP), b_glu=g_bglu.reshape(2 * ssm_w))

    n_parts = W_IN_GRAD_PARTS
    wq = d // n_parts
    g_parts, pair_parts, chip_parts = [], [], []
    extra = [_chip_exchange([pair_glu]), _all_gather([_pack(small_grads, SMALL_REST)])]
    chips_glu = small_parts = dh = grad_x = g_norm = None
    for step in range(n_parts + 2):
        riders = list(extra) if step == 0 else []
        if 0 <= step - 2 < n_parts:
            riders.append(_chip_exchange([pair_parts[step - 2]]))
        if 0 <= step - 1 < n_parts:
            riders.append(_sibling_exchange([g_parts[step - 1]]))
        rider = _join(*riders) if riders else None
        if step < n_parts:
            res = _matmul(dproj, Cols(h, step * wq, wq), mode="tn", name="grad_w_in_%d" % step, tk=1024, rider=rider)
            out, landed = res if rider is not None else (res, [])
            g_parts.append(out)
        elif step == n_parts:
            dh, landed = _matmul(dproj, w_in_t, mode="nn", name="d_normed", tk=2176, rider=rider)
        else:
            (grad_x, g_norm), landed = _rmsnorm_bwd(xs, norm_row, dh, dout, "rmsnorm_bwd", rider=rider)
        landed = list(landed)
        if step == 0:
            chips_glu, small_parts = landed[:2]
            landed = landed[2:]
        if 0 <= step - 2 < n_parts:
            chip_parts.append(landed.pop(0))
        if 0 <= step - 1 < n_parts:
            pair_parts.append(_pair_sum(g_parts[step - 1], landed.pop(0), "pair_sum_w_in_%d" % (step - 1)))
    (norm_parts,) = _exchange(_all_gather([_pack(dict(norm_w=g_norm), ("norm_w",))]), "gather_norm_grad")
    g_in = jnp.concatenate([_chip_sum(c, "chip_sum_w_in_%d" % q) for q, c in enumerate(chip_parts)], axis=1)
    summed = [g_in] + [_chip_sum(c, "chip_sum_" + k)
                       for k, c in zip(LARGE[1:], (chips_ap, chips_glu, chips_sp, chips_o))]
    return grad_x, summed, small_parts, norm_parts


def kernel(x, norm_w, w_in, q_norm_w, k_norm_w, sinks, w_attn_proj, A_re, A_im, log_dt, B_re, B_im, C_re, C_im, D_skip, w_glu, b_glu, w_ssm_proj, w_out, loss_target, m_norm_w, m_w_in, m_q_norm_w, m_k_norm_w, m_sinks, m_w_attn_proj, m_A_re, m_A_im, m_log_dt, m_B_re, m_B_im, m_C_re, m_C_im, m_D_skip, m_w_glu, m_b_glu, m_w_ssm_proj, m_w_out, v_norm_w, v_w_in, v_q_norm_w, v_k_norm_w, v_sinks, v_w_attn_proj, v_A_re, v_A_im, v_log_dt, v_B_re, v_B_im, v_C_re, v_C_im, v_D_skip, v_w_glu, v_b_glu, v_w_ssm_proj, v_w_out):
    weights = dict(norm_w=norm_w, w_in=w_in, q_norm_w=q_norm_w, k_norm_w=k_norm_w, sinks=sinks,
                   w_attn_proj=w_attn_proj, A_re=A_re, A_im=A_im, log_dt=log_dt, B_re=B_re, B_im=B_im, C_re=C_re,
                   C_im=C_im, D_skip=D_skip, w_glu=w_glu, b_glu=b_glu, w_ssm_proj=w_ssm_proj, w_out=w_out)
    m_in = dict(norm_w=m_norm_w, w_in=m_w_in, q_norm_w=m_q_norm_w, k_norm_w=m_k_norm_w, sinks=m_sinks,
                w_attn_proj=m_w_attn_proj, A_re=m_A_re, A_im=m_A_im, log_dt=m_log_dt, B_re=m_B_re, B_im=m_B_im,
                C_re=m_C_re, C_im=m_C_im, D_skip=m_D_skip, w_glu=m_w_glu, b_glu=m_b_glu, w_ssm_proj=m_w_ssm_proj,
                w_out=m_w_out)
    v_in = dict(norm_w=v_norm_w, w_in=v_w_in, q_norm_w=v_q_norm_w, k_norm_w=v_k_norm_w, sinks=v_sinks,
                w_attn_proj=v_w_attn_proj, A_re=v_A_re, A_im=v_A_im, log_dt=v_log_dt, B_re=v_B_re, B_im=v_B_im,
                C_re=v_C_re, C_im=v_C_im, D_skip=v_D_skip, w_glu=v_w_glu, b_glu=v_b_glu, w_ssm_proj=v_w_ssm_proj,
                w_out=v_w_out)

    _, seq, d = x.shape
    column_sharded = LARGE[:4]
    as_rows = lambda k, a: a.T if k in column_sharded else a
    shards = [as_rows(k, weights[k]).astype(BF16) for k in LARGE]
    small = {k: weights[k] for k in SMALL}
    grad_x, summed, small_parts, norm_parts = _step(x.reshape(seq, d), loss_target.reshape(seq, d), small, shards)

    grads, delta, new_m, new_v = {}, {}, {}, {}
    for k, g in zip(LARGE, summed):
        if k == "w_in":
            upd = _adamw(weights[k].T, g, m_in[k].T, v_in[k].T, "adamw_" + k)
            grads[k], delta[k], new_m[k], new_v[k] = [a.T for a in (g, *upd)]
        else:
            grads[k] = as_rows(k, g)
            delta[k], new_m[k], new_v[k] = _adamw(weights[k], grads[k], m_in[k], v_in[k], "adamw_" + k)

    zero = jnp.zeros((), F32)
    for keys, parts in ((SMALL_REST, small_parts), (("norm_w",), norm_parts)):
        like = dict(small, loss=zero)
        packs = [_pack(dict(src, loss=zero), keys) for src in (weights, m_in, v_in)]
        res = _adamw_small(packs[0], parts, packs[1], packs[2], "adamw_small_%d" % len(keys))
        for dst, r in zip((grads, delta, new_m, new_v), res):
            dst.update(_unpack(r, like, keys))
    loss = grads["loss"]

    return (loss, grad_x.reshape(x.shape), *[grads[k] for k in ORDER], *[delta[k] for k in ORDER],
            *[new_m[k] for k in ORDER], *[new_v[k] for k in ORDER])
```

```python
import math
from typing import Callable, NamedTuple

import jax
import jax.numpy as jnp
from jax import lax
from jax.experimental import pallas as pl
from jax.experimental.pallas import tpu as pltpu

F32 = jnp.float32
BF16 = jnp.bfloat16
MESH = pl.DeviceIdType.MESH

HEAD_DIM = 64
N_KV_HEADS = 4
GROUP = 16
STATE = 64
BLOCK = 128
NORM_EPS = 1e-6
N_DEV = 8
N_CHIPS = 4
LANES = 128
SUBLANES = 8
MXU_DIM = 256
VMEM_BYTES = 64 * 1024 * 1024
VMEM_CAP = VMEM_BYTES - 8 * 1024 * 1024

ADAM_LR = 0.001
ADAM_B1 = 0.9
ADAM_B2 = 0.999
ADAM_EPS = 1e-08
ADAM_WD = 0.01
ADAM_STEP = 10

GELU_C = math.sqrt(2.0 / math.pi)
GELU_K = 0.044715


def _tile(dim, pref, mult=LANES):
    if dim <= pref:
        return dim
    best = None
    for d in range(mult, pref + 1, mult):
        if dim % d == 0:
            best = d
    assert best is not None, (dim, pref, mult)
    return best


def _params(semantics=None, vmem=None):
    kw = {}
    if semantics is not None:
        kw["dimension_semantics"] = semantics
    if vmem is not None:
        kw["vmem_limit_bytes"] = int(min(VMEM_CAP, max(vmem, 32 * 1024 * 1024)))
    return pltpu.CompilerParams(**kw)


def _nbytes(shape, dtype):
    return math.prod(shape) * jnp.dtype(dtype).itemsize


def _sigmoid(x):
    return 1.0 / (1.0 + jnp.exp(-x))


def _silu(x):
    return x * _sigmoid(x)


def _dsilu(x):
    s = _sigmoid(x)
    return s * (1.0 + x * (1.0 - s))


def _gelu(x):
    return 0.5 * x * (1.0 + jnp.tanh(GELU_C * (x + GELU_K * x * x * x)))


def _dgelu(x):
    t = jnp.tanh(GELU_C * (x + GELU_K * x * x * x))
    return 0.5 * (1.0 + t) + 0.5 * x * (1.0 - t * t) * GELU_C * (1.0 + 3.0 * GELU_K * x * x)


def _dot(a, b, dims):
    return lax.dot_general(a, b, (dims, ((), ())), preferred_element_type=F32)


NN = ((1,), (0,))
NT = ((1,), (1,))
TN = ((0,), (0,))


def _any_spec():
    return pl.BlockSpec(memory_space=pl.ANY)


class Rider(NamedTuple):
    operands: tuple
    out_shapes: tuple
    sems: tuple
    start: Callable
    finish: Callable


def _all_gather(shards):
    n = len(shards)

    def copies(ins, outs, sems):
        send_sems, recv_sems, local_sems = sems
        x, y, c = lax.axis_index("x"), lax.axis_index("y"), lax.axis_index("c")
        me, sibling = (x, y, c), (x, y, 1 - c)
        chips = [(1 - x, y), (x, 1 - y), (1 - x, 1 - y)]

        def rows(k, px, py, pc):
            r = shards[k].shape[0]
            return outs[k].at[pl.ds((4 * px + 2 * py + pc) * r, r), :]

        def copy(k, s, block, to, src=None):
            return pltpu.make_async_remote_copy(
                src_ref=rows(k, *block) if src is None else src, dst_ref=rows(k, *block),
                send_sem=send_sems.at[7 * k + s], recv_sem=recv_sems.at[7 * k + s],
                device_id=to, device_id_type=MESH)

        mine = [pltpu.make_async_copy(ins[k], rows(k, *me), local_sems.at[k]) for k in range(n)]
        first = []
        for k in range(n):
            first.append(copy(k, 0, me, sibling, src=ins[k]))
            first += [copy(k, 1 + j, me, (*chip, c), src=ins[k]) for j, chip in enumerate(chips)]
        return me, sibling, chips, c, copy, mine, first

    def start(ins, outs, sems):
        *_, mine, first = copies(ins, outs, sems)
        for cp in mine + first:
            cp.start()

    def finish(ins, outs, sems):
        me, sibling, chips, c, copy, mine, first = copies(ins, outs, sems)
        passed = []
        for j, chip in enumerate(chips):
            for k in range(n):
                copy(k, 1 + j, (*chip, c), me).wait_recv()
                fwd = copy(k, 4 + j, (*chip, c), sibling)
                fwd.start()
                passed.append(fwd)
        for k in range(n):
            copy(k, 0, sibling, me).wait_recv()
            for j, chip in enumerate(chips):
                copy(k, 4 + j, (*chip, 1 - c), me).wait_recv()
        for cp in first + passed:
            cp.wait_send()
        for cp in mine:
            cp.wait()

    return Rider(
        tuple(shards),
        tuple(jax.ShapeDtypeStruct((N_DEV * s.shape[0], s.shape[1]), s.dtype) for s in shards),
        (pltpu.SemaphoreType.DMA((7 * n,)), pltpu.SemaphoreType.DMA((7 * n,)), pltpu.SemaphoreType.DMA((n,))),
        start, finish)


def _sibling_exchange(grads):
    n = len(grads)

    def copies(ins, outs, sems):
        send_sems, recv_sems = sems
        x, y, c = lax.axis_index("x"), lax.axis_index("y"), lax.axis_index("c")
        out = []
        for k in range(n):
            r = grads[k].shape[0] // N_DEV
            for j in range(N_CHIPS):
                out.append(pltpu.make_async_remote_copy(
                    src_ref=ins[k].at[pl.ds((2 * j + 1 - c) * r, r), :],
                    dst_ref=outs[k].at[pl.ds(j * r, r), :],
                    send_sem=send_sems.at[N_CHIPS * k + j], recv_sem=recv_sems.at[N_CHIPS * k + j],
                    device_id=(x, y, 1 - c), device_id_type=MESH))
        return out

    def start(ins, outs, sems):
        for cp in copies(ins, outs, sems):
            cp.start()

    def finish(ins, outs, sems):
        for cp in copies(ins, outs, sems):
            cp.wait()

    return Rider(
        tuple(grads), tuple(jax.ShapeDtypeStruct((g.shape[0] // 2, g.shape[1]), g.dtype) for g in grads),
        (pltpu.SemaphoreType.DMA((N_CHIPS * n,)), pltpu.SemaphoreType.DMA((N_CHIPS * n,))), start, finish)


def _chip_exchange(parts):
    n = len(parts)

    def copies(ins, outs, sems):
        send_sems, recv_sems, local_sems = sems
        x, y, c = lax.axis_index("x"), lax.axis_index("y"), lax.axis_index("c")
        my_chip = 2 * x + y
        chips = [(1 - x, y), (x, 1 - y), (1 - x, 1 - y)]
        local, sent = [], []
        for k in range(n):
            r = parts[k].shape[0] // N_CHIPS
            mine = pl.ds(my_chip * r, r)
            local.append(pltpu.make_async_copy(ins[k].at[mine, :], outs[k].at[mine, :], local_sems.at[k]))
            for s, (px, py) in enumerate(chips):
                sent.append(pltpu.make_async_remote_copy(
                    src_ref=ins[k].at[pl.ds((2 * px + py) * r, r), :], dst_ref=outs[k].at[mine, :],
                    send_sem=send_sems.at[3 * k + s], recv_sem=recv_sems.at[3 * k + s],
                    device_id=(px, py, c), device_id_type=MESH))
        return local, sent

    def start(ins, outs, sems):
        local, sent = copies(ins, outs, sems)
        for cp in local + sent:
            cp.start()

    def finish(ins, outs, sems):
        local, sent = copies(ins, outs, sems)
        for cp in sent + local:
            cp.wait()

    return Rider(
        tuple(parts), tuple(jax.ShapeDtypeStruct(p.shape, p.dtype) for p in parts),
        (pltpu.SemaphoreType.DMA((3 * n,)), pltpu.SemaphoreType.DMA((3 * n,)), pltpu.SemaphoreType.DMA((n,))),
        start, finish)


def _join(*riders):
    cuts_in, cuts_out, cuts_sem = [0], [0], [0]
    for r in riders:
        cuts_in.append(cuts_in[-1] + len(r.operands))
        cuts_out.append(cuts_out[-1] + len(r.out_shapes))
        cuts_sem.append(cuts_sem[-1] + len(r.sems))

    def each(which):
        def run(ins, outs, sems):
            for i, r in enumerate(riders):
                getattr(r, which)(ins[cuts_in[i]:cuts_in[i + 1]], outs[cuts_out[i]:cuts_out[i + 1]],
                                  sems[cuts_sem[i]:cuts_sem[i + 1]])
        return run

    return Rider(sum((r.operands for r in riders), ()), sum((r.out_shapes for r in riders), ()),
                 sum((r.sems for r in riders), ()), each("start"), each("finish"))


def _call(body, operands, *, name, out_shape, grid, in_specs, out_specs, scratch_shapes=(), aliases=None,
          semantics=None, vmem=None, rider=None):
    operands, out_shape, scratch_shapes = list(operands), list(out_shape), list(scratch_shapes)
    in_specs, out_specs = list(in_specs), list(out_specs)
    if rider is None:
        res = pl.pallas_call(
            body, name=name, out_shape=out_shape, grid=grid, in_specs=in_specs, out_specs=out_specs,
            scratch_shapes=scratch_shapes, input_output_aliases=aliases or {},
            compiler_params=_params(semantics, vmem))(*operands)
        return list(res), []
    n_in, n_out, n_scr = len(operands), len(out_shape), len(scratch_shapes)
    ri, ro = len(rider.operands), len(rider.out_shapes)

    def carried(*refs):
        a, b = n_in, n_in + ri
        c, d = b + n_out, b + n_out + ro
        e = d + n_scr
        ids = [pl.program_id(k) for k in range(len(grid))]
        first = ids[0] == 0
        last = ids[0] == grid[0] - 1
        for k in range(1, len(grid)):
            first = jnp.logical_and(first, ids[k] == 0)
            last = jnp.logical_and(last, ids[k] == grid[k] - 1)

        @pl.when(first)
        def _():
            rider.start(refs[a:b], refs[c:d], refs[e:])

        body(*refs[:a], *refs[b:c], *refs[d:e])

        @pl.when(last)
        def _():
            rider.finish(refs[a:b], refs[c:d], refs[e:])

    res = pl.pallas_call(
        carried, name=name, out_shape=out_shape + list(rider.out_shapes), grid=grid,
        in_specs=in_specs + [_any_spec()] * ri, out_specs=out_specs + [_any_spec()] * ro,
        scratch_shapes=scratch_shapes + list(rider.sems), input_output_aliases=aliases or {},
        compiler_params=_params(("arbitrary",) * len(grid), vmem))(*operands, *rider.operands)
    return list(res[:n_out]), list(res[n_out:])


def _exchange(rider, name):
    ri, ro = len(rider.operands), len(rider.out_shapes)

    def body(*refs):
        rider.start(refs[:ri], refs[ri:ri + ro], refs[ri + ro:])
        rider.finish(refs[:ri], refs[ri:ri + ro], refs[ri + ro:])

    return pl.pallas_call(
        body, name=name, out_shape=list(rider.out_shapes), in_specs=[_any_spec()] * ri,
        out_specs=[_any_spec()] * ro, scratch_shapes=list(rider.sems))(*rider.operands)


class Cols(NamedTuple):
    arr: jax.Array
    off: int
    width: int


def _cols(a):
    return a if isinstance(a, Cols) else Cols(a, 0, a.shape[1])


def _matmul(a, b, *, mode, name, out_dtype=F32, tm=1024, tn=1024, tk=2048, bias=None, out_cols=None, into=None,
            rider=None):
    a, b = _cols(a), _cols(b)
    if mode == "nn":
        (m, k), (k2, n) = (a.arr.shape[0], a.width), (b.arr.shape[0], b.width)
    elif mode == "nt":
        (m, k), (n, k2) = (a.arr.shape[0], a.width), (b.arr.shape[0], b.width)
    else:
        (k, m), (k2, n) = (a.arr.shape[0], a.width), (b.arr.shape[0], b.width)
    assert k == k2, (a.arr.shape, b.arr.shape, mode)
    tm, tn, tk = _tile(m, tm), _tile(n, tn), _tile(k, tk)
    nk = k // tk
    dims = {"nn": NN, "nt": NT, "tn": TN}[mode]
    if mode == "tn":
        assert a.off % tm == 0
        a_spec = pl.BlockSpec((tk, tm), lambda i, j, kk, o=a.off // tm: (kk, i + o))
    else:
        assert a.off % tk == 0
        a_spec = pl.BlockSpec((tm, tk), lambda i, j, kk, o=a.off // tk: (i, kk + o))
    if mode == "nt":
        assert b.off % tk == 0
        b_spec = pl.BlockSpec((tn, tk), lambda i, j, kk, o=b.off // tk: (j, kk + o))
    else:
        assert b.off % tn == 0
        b_spec = pl.BlockSpec((tk, tn), lambda i, j, kk, o=b.off // tn: (kk, j + o))
    in_specs, operands = [a_spec, b_spec], [a.arr, b.arr]
    if bias is not None:
        in_specs.append(pl.BlockSpec((1, tn), lambda i, j, kk: (0, j)))
        operands.append(bias)
    total_w, o_off = out_cols if out_cols is not None else (n, 0)
    assert o_off % tn == 0
    aliases = {}
    if into is not None:
        assert into.shape == (m, total_w) and into.dtype == out_dtype
        in_specs.append(_any_spec())
        operands.append(into)
        aliases = {len(operands) - 1: 0}
    n_in = len(operands)

    def body(*refs):
        a_ref, b_ref = refs[0], refs[1]
        bias_ref = refs[2] if bias is not None else None
        o_ref = refs[n_in]
        acc_ref = refs[-1] if nk > 1 else None
        part = _dot(a_ref[...].astype(BF16), b_ref[...].astype(BF16), dims)

        def finish(acc):
            if bias_ref is not None:
                acc = acc + bias_ref[...]
            o_ref[...] = acc.astype(out_dtype)

        if nk == 1:
            finish(part)
        else:
            kk = pl.program_id(2)

            @pl.when(kk == 0)
            def _():
                acc_ref[...] = part

            @pl.when(kk > 0)
            def _():
                acc_ref[...] += part

            @pl.when(kk == nk - 1)
            def _():
                finish(acc_ref[...])

    vmem = 2 * (_nbytes((tm, tk), a.arr.dtype) + _nbytes((tk, tn), b.arr.dtype) + _nbytes((tm, tn), out_dtype))
    vmem += 3 * _nbytes((tm, tn), F32)
    (out,), landed = _call(
        body, operands, name=name, out_shape=[jax.ShapeDtypeStruct((m, total_w), out_dtype)],
        grid=(m // tm, n // tn, nk), in_specs=in_specs,
        out_specs=[pl.BlockSpec((tm, tn), lambda i, j, kk, o=o_off // tn: (i, j + o))],
        scratch_shapes=[pltpu.VMEM((tm, tn), F32)] if nk > 1 else [], aliases=aliases,
        semantics=("parallel", "parallel", "arbitrary"), vmem=vmem, rider=rider)
    return out if rider is None else (out, landed)


def _ew(fn, *, name, rows, width, tiles, vecs=(), outs, accs=0, tl=1024, cw=512, into=None, with_col=False):
    tl, cw = _tile(rows, tl, SUBLANES), _tile(width, cw)
    ncol = width // cw
    nt_, nv = len(tiles), len(vecs)
    into = list(into) if into is not None else [None] * len(outs)
    aliased = [t for t in into if t is not None]

    def off(o):
        assert o % cw == 0, (name, o, cw)
        return o // cw

    in_specs, vmem = [], 0
    for t in tiles:
        arr, o = t[0], off(t[1])
        wrap = t[2] // cw if len(t) > 2 else ncol
        in_specs.append(pl.BlockSpec((tl, cw), lambda j, i, o=o, wrap=wrap: (i, o + j % wrap)))
        vmem += _nbytes((tl, cw), arr.dtype)
    in_specs += [pl.BlockSpec((1, cw), lambda j, i, o=off(o): (0, j + o)) for _, o in vecs]
    in_specs += [_any_spec() for _ in aliased]
    out_shape, out_specs, aliases = [], [], {}
    n_in = nt_ + nv
    for idx, ((dt, tw, o), tgt) in enumerate(zip(outs, into)):
        out_shape.append(jax.ShapeDtypeStruct((rows, tw), dt))
        out_specs.append(pl.BlockSpec((tl, cw), lambda j, i, o=off(o): (i, j + o)))
        vmem += _nbytes((tl, cw), dt)
        if tgt is not None:
            assert tgt.shape == (rows, tw) and tgt.dtype == dt, (name, tgt.shape, tgt.dtype)
            aliases[n_in + len(aliases)] = idx
    for _ in range(accs):
        out_shape.append(jax.ShapeDtypeStruct((1, width), F32))
        out_specs.append(pl.BlockSpec((1, cw), lambda j, i: (0, j)))
    n_out = len(outs)

    def body(*refs):
        vals = [r[...] for r in refs[:n_in]]
        out_refs = refs[n_in + len(aliased):]
        res = fn(pl.program_id(0), *vals) if with_col else fn(*vals)
        res = res if isinstance(res, (tuple, list)) else (res,)
        assert len(res) == n_out + accs, (name, len(res))
        for r, v in zip(out_refs[:n_out], res[:n_out]):
            r[...] = v.astype(r.dtype)
        first = pl.program_id(1) == 0
        for r, v in zip(out_refs[n_out:], res[n_out:]):
            s = jnp.sum(v, axis=0, keepdims=True)

            @pl.when(first)
            def _(r=r, s=s):
                r[...] = s

            @pl.when(jnp.logical_not(first))
            def _(r=r, s=s):
                r[...] += s

    return pl.pallas_call(
        body, name=name, out_shape=out_shape, grid=(ncol, rows // tl),
        in_specs=in_specs, out_specs=out_specs, input_output_aliases=aliases,
        compiler_params=_params(("parallel", "arbitrary"), 3 * vmem),
    )(*[t[0] for t in tiles], *[v for v, _ in vecs], *aliased)


def _rmsnorm_fwd(x, w_row, name):
    rows, d = x.shape
    tl = _tile(rows, 512, SUBLANES)

    def body(x_ref, w_ref, h_ref):
        xv = x_ref[...]
        rstd = lax.rsqrt(jnp.mean(xv * xv, axis=-1, keepdims=True) + NORM_EPS)
        h_ref[...] = (xv * rstd * w_ref[...]).astype(BF16)

    return pl.pallas_call(
        body, name=name, out_shape=jax.ShapeDtypeStruct((rows, d), BF16), grid=(rows // tl,),
        in_specs=[pl.BlockSpec((tl, d), lambda i: (i, 0)), pl.BlockSpec((1, d), lambda i: (0, 0))],
        out_specs=pl.BlockSpec((tl, d), lambda i: (i, 0)),
        compiler_params=_params(("parallel",)),
    )(x, w_row)


def _rmsnorm_bwd(x, w_row, dh, dout, name, rider=None):
    rows, d = x.shape
    tl = _tile(rows, 256, SUBLANES)

    def body(x_ref, w_ref, dh_ref, dout_ref, gx_ref, gw_ref):
        xv = x_ref[...]
        rstd = lax.rsqrt(jnp.mean(xv * xv, axis=-1, keepdims=True) + NORM_EPS)
        xn = xv * rstd
        dhv = dh_ref[...]
        dxn = dhv * w_ref[...]
        dx = rstd * (dxn - xn * jnp.mean(dxn * xn, axis=-1, keepdims=True))
        gx_ref[...] = dout_ref[...] + dx
        gw = jnp.sum(dhv * xn, axis=0, keepdims=True)

        @pl.when(pl.program_id(0) == 0)
        def _():
            gw_ref[...] = gw

        @pl.when(pl.program_id(0) > 0)
        def _():
            gw_ref[...] += gw

    tile = pl.BlockSpec((tl, d), lambda i: (i, 0))
    row = pl.BlockSpec((1, d), lambda i: (0, 0))
    res, landed = _call(
        body, [x, w_row, dh, dout], name=name,
        out_shape=[jax.ShapeDtypeStruct((rows, d), F32), jax.ShapeDtypeStruct((1, d), F32)],
        grid=(rows // tl,), in_specs=[tile, row, tile, tile], out_specs=[tile, row],
        semantics=("arbitrary",), rider=rider)
    return res if rider is None else (res, landed)


def _head_mean(x, gmat):
    hi = x.astype(BF16)
    lo = (x - hi.astype(F32)).astype(BF16)
    out = []
    for s in range(x.shape[1] // MXU_DIM):
        sl = slice(s * MXU_DIM, (s + 1) * MXU_DIM)
        out.append(_dot(hi[:, sl], gmat, NN) + _dot(lo[:, sl], gmat, NN))
    return out[0] if len(out) == 1 else jnp.concatenate(out, axis=1)


def _head_mean_matrix():
    blk = jnp.arange(MXU_DIM) // HEAD_DIM
    return jnp.where(blk[:, None] == blk[None, :], 1.0 / HEAD_DIM, 0.0).astype(BF16)


def _spread_head(x, g, width):
    col = x[:, (g // 2) * LANES:(g // 2 + 1) * LANES]
    other = pltpu.roll(col, HEAD_DIM, axis=1)
    low = lax.broadcasted_iota(jnp.int32, col.shape, 1) < HEAD_DIM
    both = jnp.where(low, col, other) if g % 2 == 0 else jnp.where(low, other, col)
    return both if width == LANES else jnp.concatenate([both] * (width // LANES), axis=1)


def _head_diagonal(t, per_kv):
    head = lax.broadcasted_iota(jnp.int32, t.shape, 1) // HEAD_DIM
    zero = jnp.zeros_like(t)
    return jnp.concatenate([jnp.where(head == r, t, zero) for r in range(per_kv)], axis=0)


def _fold_heads(x, per_kv):
    rows = x.shape[0] // per_kv
    head = lax.broadcasted_iota(jnp.int32, (rows, x.shape[1]), 1) // HEAD_DIM
    acc = jnp.where(head == 0, x[0:rows], 0.0)
    for r in range(1, per_kv):
        acc = acc + jnp.where(head == r, x[r * rows:(r + 1) * rows], 0.0)
    while acc.shape[1] > LANES:
        half = acc.shape[1] // 2
        acc = acc[:, :half] + acc[:, half:]
    return acc + pltpu.roll(acc, HEAD_DIM, axis=1)


def _join_heads(parts):
    low = lax.broadcasted_iota(jnp.int32, parts[0].shape, 1) < HEAD_DIM
    cols = [jnp.where(low, parts[2 * j], parts[2 * j + 1]) for j in range(len(parts) // 2)]
    return cols[0] if len(cols) == 1 else jnp.concatenate(cols, axis=1)


def _attn_specs(attn_w, kv_w):
    half = attn_w // 2
    kcol, vcol = attn_w // kv_w, attn_w // kv_w + 1
    gcol = (attn_w + 2 * kv_w) // half
    prev = lambda i: jnp.maximum(i - 1, 0)
    return [
        pl.BlockSpec((BLOCK, attn_w), lambda i: (i, 0)),
        pl.BlockSpec((BLOCK, kv_w), lambda i: (prev(i), kcol)),
        pl.BlockSpec((BLOCK, kv_w), lambda i: (i, kcol)),
        pl.BlockSpec((BLOCK, kv_w), lambda i: (prev(i), vcol)),
        pl.BlockSpec((BLOCK, kv_w), lambda i: (i, vcol)),
        pl.BlockSpec((BLOCK, half), lambda i: (i, gcol)),
        pl.BlockSpec((BLOCK, half), lambda i: (i, gcol + 1)),
    ]


def _band_mask(i):
    q_loc = lax.broadcasted_iota(jnp.int32, (BLOCK, 2 * BLOCK), 0) + BLOCK
    k_loc = lax.broadcasted_iota(jnp.int32, (BLOCK, 2 * BLOCK), 1)
    diff = q_loc - k_loc
    first_key = jnp.where(i == 0, BLOCK, 0)
    return (diff >= 0) & (diff < BLOCK) & (k_loc >= first_key)


def _softmax_with_sink(s, sink):
    m = jnp.maximum(jnp.max(s, axis=-1, keepdims=True), sink)
    p = jnp.exp(s - m)
    e_sink = jnp.exp(sink - m)
    den = jnp.sum(p, axis=-1, keepdims=True) + e_sink
    inv = 1.0 / den
    return p * inv, e_sink * inv


def _attn_block(i, q, kk, vv, qw, kw, gmat, sink_ref, per_kv):
    scale = 1.0 / math.sqrt(HEAD_DIM)
    keys = 2 * BLOCK
    valid = _band_mask(i)
    q_rstd = lax.rsqrt(_head_mean(q * q, gmat) + NORM_EPS)
    qn = q * q_rstd
    qh = (qn * qw).astype(BF16)
    k_rstd = lax.rsqrt(_head_mean(kk * kk, gmat) + NORM_EPS)
    kn = kk * k_rstd
    kh = kn * kw
    gw = per_kv * HEAD_DIM
    groups = []
    for g in range(N_KV_HEADS):
        kd = _head_diagonal(_spread_head(kh, g, gw).astype(BF16), per_kv)
        vd = _head_diagonal(_spread_head(vv, g, gw).astype(BF16), per_kv)
        qg = qh[:, g * gw:(g + 1) * gw]
        s_all = _dot(qg, kd, NT) * scale
        ps, p_sinks = [], []
        for r in range(per_kv):
            s = jnp.where(valid, s_all[:, r * keys:(r + 1) * keys], -1e30)
            p, p_sink = _softmax_with_sink(s, sink_ref[g * per_kv + r])
            ps.append(p)
            p_sinks.append(p_sink)
        pb = jnp.concatenate(ps, axis=1).astype(BF16)
        groups.append((kd, vd, qg, ps, p_sinks, pb, _dot(pb, vd, NN)))
    return qn, q_rstd, kn, k_rstd, groups


def _attention_fwd(proj, qw_row, kw_row, gmat, sinks, *, attn_w, kv_w, name):
    rows = proj.shape[0]
    per_kv = attn_w // HEAD_DIM // N_KV_HEADS

    def body(q_ref, kp_ref, kc_ref, vp_ref, vc_ref, glo_ref, ghi_ref, qw_ref, kw_ref, gm_ref, sink_ref, o_ref):
        kk = jnp.concatenate([kp_ref[...], kc_ref[...]], axis=0)
        vv = jnp.concatenate([vp_ref[...], vc_ref[...]], axis=0)
        gate = jnp.concatenate([glo_ref[...], ghi_ref[...]], axis=1)
        *_, groups = _attn_block(pl.program_id(0), q_ref[...], kk, vv, qw_ref[...], kw_ref[...], gm_ref[...],
                                 sink_ref, per_kv)
        attn = jnp.concatenate([grp[-1] for grp in groups], axis=1)
        o_ref[...] = (attn * _silu(gate)).astype(BF16)

    const = lambda a: pl.BlockSpec(a.shape, lambda i: (0, 0))
    return pl.pallas_call(
        body, name=name, out_shape=jax.ShapeDtypeStruct((rows, attn_w), BF16), grid=(rows // BLOCK,),
        in_specs=_attn_specs(attn_w, kv_w) + [const(qw_row), const(kw_row), const(gmat),
                                              pl.BlockSpec(memory_space=pltpu.SMEM)],
        out_specs=pl.BlockSpec((BLOCK, attn_w), lambda i: (i, 0)),
        compiler_params=_params(("parallel",), 40 * 1024 * 1024),
    )(proj, proj, proj, proj, proj, proj, proj, qw_row, kw_row, gmat, sinks)


def _attention_bwd(proj, d_ag, dproj, qw_row, kw_row, gmat, sinks, *, attn_w, kv_w, name, rider=None):
    rows = proj.shape[0]
    nb = rows // BLOCK
    per_kv = attn_w // HEAD_DIM // N_KV_HEADS
    gw = per_kv * HEAD_DIM
    keys = 2 * BLOCK
    scale = 1.0 / math.sqrt(HEAD_DIM)
    w_out = 2 * attn_w + 2 * kv_w

    def body(q_ref, kp_ref, kc_ref, vp_ref, vc_ref, glo_ref, ghi_ref, dag_ref, qw_ref, kw_ref, gm_ref, sink_ref, _,
             dp_ref, dkv_ref, gqw_ref, gkw_ref, gs_ref):
        i = pl.program_id(0)
        kk = jnp.concatenate([kp_ref[...], kc_ref[...]], axis=0)
        vv = jnp.concatenate([vp_ref[...], vc_ref[...]], axis=0)
        gate = jnp.concatenate([glo_ref[...], ghi_ref[...]], axis=1)
        d_ag_v = dag_ref[...]
        qw, kw, gmat_v = qw_ref[...], kw_ref[...], gm_ref[...]
        qn, q_rstd, kn, k_rstd, groups = _attn_block(i, q_ref[...], kk, vv, qw, kw, gmat_v, sink_ref, per_kv)
        lane = lax.broadcasted_iota(jnp.int32, (SUBLANES, LANES), 1)
        sub = lax.broadcasted_iota(jnp.int32, (SUBLANES, LANES), 0)
        gsink = jnp.zeros((SUBLANES, LANES), F32)
        dq_groups, dgate_groups, dk_heads, dv_heads = [], [], [], []
        for g, (kd, vd, qg, ps, p_sinks, pb, o) in enumerate(groups):
            cs = slice(g * gw, (g + 1) * gw)
            gate_g, d_ag_g = gate[:, cs], d_ag_v[:, cs]
            dgate_groups.append(d_ag_g * o * _dsilu(gate_g))
            do = (d_ag_g * _silu(gate_g)).astype(BF16)
            dp_all = _dot(do, vd, NT)
            dss = []
            for r in range(per_kv):
                p, dp = ps[r], dp_all[:, r * keys:(r + 1) * keys]
                delta = jnp.sum(p * dp, axis=-1, keepdims=True)
                dss.append(p * (dp - delta) * scale)
                gs_h = jnp.sum(-p_sinks[r] * delta, axis=0, keepdims=True)
                gsink = gsink + jnp.where((lane == g * per_kv + r) & (sub == 0), gs_h, 0.0)
            ds = jnp.concatenate(dss, axis=1).astype(BF16)
            dq_groups.append(_dot(ds, kd, NN))
            dk_heads.append(_fold_heads(_dot(ds, qg, TN), per_kv))
            dv_heads.append(_fold_heads(_dot(pb, do, TN), per_kv))
        dqh = jnp.concatenate(dq_groups, axis=1)
        gqw = jnp.sum(dqh * qn, axis=0, keepdims=True)
        dqn = dqh * qw
        dq = q_rstd * (dqn - qn * _head_mean(dqn * qn, gmat_v))
        dkh = _join_heads(dk_heads)
        gkw = jnp.sum(dkh * kn, axis=0, keepdims=True)
        dkn = dkh * kw
        dk = k_rstd * (dkn - kn * _head_mean(dkn * kn, gmat_v))
        dp_ref[:, 0:attn_w] = dq.astype(BF16)
        dp_ref[:, attn_w:attn_w + 2 * kv_w] = jnp.zeros((BLOCK, 2 * kv_w), BF16)
        dp_ref[:, attn_w + 2 * kv_w:w_out] = jnp.concatenate(dgate_groups, axis=1).astype(BF16)
        dkv_ref[0] = jnp.concatenate([dk, _join_heads(dv_heads)], axis=1)

        @pl.when(i == 0)
        def _():
            gqw_ref[...] = gqw
            gkw_ref[...] = gkw
            gs_ref[...] = gsink

        @pl.when(i > 0)
        def _():
            gqw_ref[...] += gqw
            gkw_ref[...] += gkw
            gs_ref[...] += gsink

    const = lambda a: pl.BlockSpec(a.shape, lambda i: (0, 0))
    res, landed = _call(
        body, [proj, proj, proj, proj, proj, proj, proj, d_ag, qw_row, kw_row, gmat, sinks, dproj], name=name,
        out_shape=[jax.ShapeDtypeStruct(dproj.shape, BF16),
                   jax.ShapeDtypeStruct((nb, 2 * BLOCK, 2 * kv_w), F32),
                   jax.ShapeDtypeStruct(qw_row.shape, F32), jax.ShapeDtypeStruct(kw_row.shape, F32),
                   jax.ShapeDtypeStruct((SUBLANES, LANES), F32)],
        grid=(nb,),
        in_specs=_attn_specs(attn_w, kv_w) + [pl.BlockSpec((BLOCK, attn_w), lambda i: (i, 0)), const(qw_row),
                                              const(kw_row), const(gmat), pl.BlockSpec(memory_space=pltpu.SMEM),
                                              _any_spec()],
        out_specs=[pl.BlockSpec((BLOCK, w_out), lambda i: (i, 0)),
                   pl.BlockSpec((1, 2 * BLOCK, 2 * kv_w), lambda i: (i, 0, 0)),
                   const(qw_row), const(kw_row), pl.BlockSpec((SUBLANES, LANES), lambda i: (0, 0))],
        aliases={12: 0}, semantics=("arbitrary",), vmem=48 * 1024 * 1024, rider=rider)
    return res if rider is None else (res, landed)


def _attention_dkv(dproj, dkv, *, attn_w, kv_w, name):
    rows = dproj.shape[0]
    nb = rows // BLOCK
    col = attn_w // (2 * kv_w)

    def body(cur_ref, nxt_ref, _, o_ref):
        i = pl.program_id(0)
        nxt = jnp.where(i < nb - 1, nxt_ref[0, 0:BLOCK, :], 0.0)
        o_ref[...] = (cur_ref[0, BLOCK:2 * BLOCK, :] + nxt).astype(BF16)

    blk = lambda f: pl.BlockSpec((1, 2 * BLOCK, 2 * kv_w), f)
    return pl.pallas_call(
        body, name=name, out_shape=jax.ShapeDtypeStruct(dproj.shape, BF16), grid=(nb,),
        in_specs=[blk(lambda i: (i, 0, 0)), blk(lambda i: (jnp.minimum(i + 1, nb - 1), 0, 0)), _any_spec()],
        out_specs=pl.BlockSpec((BLOCK, 2 * kv_w), lambda i: (i, col)),
        input_output_aliases={2: 0},
        compiler_params=_params(("parallel",)),
    )(dkv, dkv, dproj)


def _cmul(ar, ai, br, bi):
    return ar * br - ai * bi, ar * bi + ai * br


def _ssm_prep(a_re, a_im, log_dt_col, steps, name):
    assert steps & (steps - 1) == 0

    def body(are_ref, aim_ref, ldt_ref, abr_ref, abi_ref, cfr_ref, cfi_ref, apr_ref, api_ref):
        are, aim = are_ref[...], aim_ref[...]
        dt = jnp.exp(ldt_ref[...])
        mag = jnp.exp(dt * are)
        abr = mag * jnp.cos(dt * aim)
        abi = mag * jnp.sin(dt * aim)
        num_re, num_im = abr - 1.0, abi
        den = are * are + aim * aim
        abr_ref[...] = abr
        abi_ref[...] = abi
        cfr_ref[...] = (num_re * are + num_im * aim) / den
        cfi_ref[...] = (num_im * are - num_re * aim) / den
        pr, pi = abr, abi
        n = steps
        while n > 1:
            pr, pi = _cmul(pr, pi, pr, pi)
            n //= 2
        apr_ref[...] = pr
        api_ref[...] = pi

    shp = jax.ShapeDtypeStruct(a_re.shape, F32)
    return pl.pallas_call(body, name=name, out_shape=[shp] * 6)(a_re, a_im, log_dt_col)


def _ssm_param_bwd(a_re, a_im, log_dt_col, d_ab_re, d_ab_im, d_cf_re, d_cf_im, name):
    def body(are_ref, aim_ref, ldt_ref, gabr_ref, gabi_ref, gcfr_ref, gcfi_ref, dar_ref, dai_ref, dldt_ref):
        are, aim = are_ref[...], aim_ref[...]
        dt = jnp.exp(ldt_ref[...])
        mag = jnp.exp(dt * are)
        abr = mag * jnp.cos(dt * aim)
        abi = mag * jnp.sin(dt * aim)
        den = are * are + aim * aim
        cfr = ((abr - 1.0) * are + abi * aim) / den
        cfi = (abi * are - (abr - 1.0) * aim) / den
        gabr, gabi = jnp.sum(gabr_ref[...], axis=0), jnp.sum(gabi_ref[...], axis=0)
        gcfr, gcfi = jnp.sum(gcfr_ref[...], axis=0), jnp.sum(gcfi_ref[...], axis=0)
        inv_r, inv_i = are / den, -aim / den
        t_r, t_i = _cmul(inv_r, -inv_i, gcfr, gcfi)
        gabr, gabi = gabr + t_r, gabi + t_i
        q_r, q_i = _cmul(cfr, cfi, inv_r, inv_i)
        da_r, da_i = _cmul(-q_r, q_i, gcfr, gcfi)
        gz_r, gz_i = _cmul(abr, -abi, gabr, gabi)
        dar_ref[...] = da_r + dt * gz_r
        dai_ref[...] = da_i + dt * gz_i
        dldt_ref[...] = dt * jnp.sum(are * gz_r + aim * gz_i, axis=-1, keepdims=True)

    shp = jax.ShapeDtypeStruct(a_re.shape, F32)
    return pl.pallas_call(body, name=name, out_shape=[shp, shp, jax.ShapeDtypeStruct(log_dt_col.shape, F32)])(
        a_re, a_im, log_dt_col, d_ab_re, d_ab_im, d_cf_re, d_cf_im)


SCAN_LANES = 512
W_IN_GRAD_PARTS = 2


def _scan_segments(xr_ref, xi_ref, a_re, a_im, ap_re, ap_im, carry_re, carry_im, cm_re, cm_im, steps, reverse):
    n = xr_ref.shape[1]
    order = range(steps - 1, -1, -1) if reverse else range(steps)
    seg_order = range(SUBLANES - 1, -1, -1) if reverse else range(SUBLANES)
    for c0 in range(0, n, SCAN_LANES):
        ls = slice(c0, c0 + SCAN_LANES)
        ar = jnp.broadcast_to(a_re[:, ls], (SUBLANES, SCAN_LANES))
        ai = jnp.broadcast_to(a_im[:, ls], (SUBLANES, SCAN_LANES))

        def local(t, s, ar=ar, ai=ai, ls=ls):
            j = steps - 1 - t if reverse else t
            r0 = pl.multiple_of(j * SUBLANES, SUBLANES)
            sr, si = _cmul(ar, ai, s[0], s[1])
            sr = sr + xr_ref[pl.ds(r0, SUBLANES), ls]
            si = si + xi_ref[pl.ds(r0, SUBLANES), ls]
            xr_ref[pl.ds(r0, SUBLANES), ls] = sr
            xi_ref[pl.ds(r0, SUBLANES), ls] = si
            return sr, si

        zero = jnp.zeros((SUBLANES, SCAN_LANES), F32)
        end_r, end_i = lax.fori_loop(0, steps, local, (zero, zero))
        cr, ci = carry_re[:, ls], carry_im[:, ls]
        apr, api = ap_re[:, ls], ap_im[:, ls]
        for r in seg_order:
            cm_re[r:r + 1, ls] = cr
            cm_im[r:r + 1, ls] = ci
            tr, ti = _cmul(apr, api, cr, ci)
            cr, ci = end_r[r:r + 1, :] + tr, end_i[r:r + 1, :] + ti
        carry_re[:, ls] = cr
        carry_im[:, ls] = ci

        def fix(t, s, ar=ar, ai=ai, ls=ls):
            j = steps - 1 - t if reverse else t
            r0 = pl.multiple_of(j * SUBLANES, SUBLANES)
            sr, si = _cmul(ar, ai, s[0], s[1])
            xr_ref[pl.ds(r0, SUBLANES), ls] += sr
            xi_ref[pl.ds(r0, SUBLANES), ls] += si
            return sr, si

        lax.fori_loop(0, steps, fix, (cm_re[:, ls], cm_im[:, ls]))
    del order


def _ssm_blocks(b_re, b_im, c_re, c_im):
    g = b_re.shape[0]
    per = MXU_DIM // GROUP
    nsb = g // per
    eye = jnp.eye(per, dtype=F32)

    def b_blocks(b):
        bt = b.reshape(nsb, per, STATE, GROUP).transpose(0, 1, 3, 2)
        return (bt[:, :, :, None, :] * eye[None, :, None, :, None]).reshape(nsb, per * GROUP, per * STATE).astype(BF16)

    def c_blocks(cm):
        ct = cm.reshape(nsb, per, GROUP, STATE).transpose(0, 1, 3, 2)
        return (ct[:, :, :, None, :] * eye[None, :, None, :, None]).reshape(nsb, per * STATE, per * GROUP).astype(BF16)

    return b_blocks(b_re), b_blocks(b_im), c_blocks(c_re), c_blocks(c_im)


def _unblock_b(db):
    nsb = db.shape[0]
    per = MXU_DIM // GROUP
    d = db.reshape(nsb, per, GROUP, per, STATE)
    d = jnp.stack([d[:, k, :, k, :] for k in range(per)], axis=1)
    return d.transpose(0, 1, 3, 2).reshape(nsb * per, STATE, GROUP)


def _unblock_c(dc):
    nsb = dc.shape[0]
    per = MXU_DIM // GROUP
    d = dc.reshape(nsb, per, STATE, per, GROUP)
    d = jnp.stack([d[:, k, :, k, :] for k in range(per)], axis=1)
    return d.transpose(0, 1, 3, 2).reshape(nsb * per, GROUP, STATE)


def _to_segments(v, chunk):
    rows, w = v.shape
    return v.reshape(rows // chunk, SUBLANES, chunk // SUBLANES, w).transpose(0, 2, 1, 3).reshape(rows, w)


def _from_segments(v, chunk):
    rows, w = v.shape
    return v.reshape(rows // chunk, chunk // SUBLANES, SUBLANES, w).transpose(0, 2, 1, 3).reshape(rows, w)


def _ssm_fwd(u, blocks, rows_p, d_row, *, chunk, name):
    rows, w = u.shape
    nc = rows // chunk
    steps = chunk // SUBLANES
    bre, bim, cre, cim = blocks
    nsb = bre.shape[0]
    n_state = nsb * bre.shape[2]
    sbw, sbs = bre.shape[1], bre.shape[2]

    def body(u_ref, bre_hbm, bim_hbm, cre_hbm, cim_hbm, abr_ref, abi_ref, cfr_ref, cfi_ref, apr_ref, api_ref,
             d_ref, y_ref, str_ref, sti_ref, bre_ref, bim_ref, cre_ref, cim_ref, sr, si, carry_r, carry_i,
             cm_r, cm_i):
        @pl.when(pl.program_id(0) == 0)
        def _():
            for src, dst in ((bre_hbm, bre_ref), (bim_hbm, bim_ref), (cre_hbm, cre_ref), (cim_hbm, cim_ref)):
                pltpu.sync_copy(src, dst)
            carry_r[...] = jnp.zeros_like(carry_r)
            carry_i[...] = jnp.zeros_like(carry_i)

        str_ref[0] = carry_r[...]
        sti_ref[0] = carry_i[...]
        for sb in range(nsb):
            us = slice(sb * sbw, (sb + 1) * sbw)
            ss = slice(sb * sbs, (sb + 1) * sbs)
            ub = u_ref[:, us].astype(BF16)
            bur = _dot(ub, bre_ref[sb], NN)
            bui = _dot(ub, bim_ref[sb], NN)
            xr, xi = _cmul(cfr_ref[:, ss], cfi_ref[:, ss], bur, bui)
            sr[:, ss] = xr
            si[:, ss] = xi
        _scan_segments(sr, si, abr_ref[...], abi_ref[...], apr_ref[...], api_ref[...],
                       carry_r, carry_i, cm_r, cm_i, steps, False)
        for sb in range(nsb):
            us = slice(sb * sbw, (sb + 1) * sbw)
            ss = slice(sb * sbs, (sb + 1) * sbs)
            y = _dot(sr[:, ss].astype(BF16), cre_ref[sb], NN) - _dot(si[:, ss].astype(BF16), cim_ref[sb], NN)
            y_ref[:, us] = y + d_ref[:, us] * u_ref[:, us]

    row_n = pl.BlockSpec((1, n_state), lambda c: (0, 0))
    st = pl.BlockSpec((1, 1, n_state), lambda c: (c, 0, 0))
    held = [pltpu.VMEM(b.shape, BF16) for b in blocks]
    vmem = 2 * sum(_nbytes(b.shape, BF16) for b in blocks) + 3 * _nbytes((chunk, n_state), F32)
    return pl.pallas_call(
        body, name=name,
        out_shape=[jax.ShapeDtypeStruct((rows, w), F32), jax.ShapeDtypeStruct((nc, 1, n_state), F32),
                   jax.ShapeDtypeStruct((nc, 1, n_state), F32)],
        grid=(nc,),
        in_specs=[pl.BlockSpec((chunk, w), lambda c: (c, 0))] + [_any_spec()] * 4
        + [row_n] * 6 + [pl.BlockSpec((1, w), lambda c: (0, 0))],
        out_specs=[pl.BlockSpec((chunk, w), lambda c: (c, 0)), st, st],
        scratch_shapes=held + [pltpu.VMEM((chunk, n_state), F32), pltpu.VMEM((chunk, n_state), F32),
                               pltpu.VMEM((1, n_state), F32), pltpu.VMEM((1, n_state), F32),
                               pltpu.VMEM((SUBLANES, n_state), F32), pltpu.VMEM((SUBLANES, n_state), F32)],
        compiler_params=_params(("arbitrary",), vmem),
    )(u, bre, bim, cre, cim, *rows_p, d_row)


def _ssm_bwd(u, y, dyg, st_re, st_im, blocks, rows_p, d_row, *, chunk, name, rider=None):
    rows, w = u.shape
    nc = rows // chunk
    steps = chunk // SUBLANES
    bre, bim, cre, cim = blocks
    nsb = bre.shape[0]
    sbw, sbs = bre.shape[1], bre.shape[2]
    n_state = nsb * sbs

    def body(u_ref, y_ref, dyg_ref, str_ref, sti_ref, bre_hbm, bim_hbm, cre_hbm, cim_hbm,
             abr_ref, abi_ref, cfr_ref, cfi_ref, apr_ref, api_ref, d_ref,
             du_ref, dbre_hbm, dbim_hbm, dcre_hbm, dcim_hbm, gabr_ref, gabi_ref, gcfr_ref, gcfi_ref, dd_ref,
             bre_ref, bim_ref, cre_ref, cim_ref, dbre_ref, dbim_ref, dcre_ref, dcim_ref,
             bur, bui, sr, si, lr, li, carry_r, carry_i, lam_r, lam_i, cm_r, cm_i, cl_r, cl_i):
        first = pl.program_id(0) == 0

        @pl.when(first)
        def _():
            for src, dst in ((bre_hbm, bre_ref), (bim_hbm, bim_ref), (cre_hbm, cre_ref), (cim_hbm, cim_ref)):
                pltpu.sync_copy(src, dst)
            lam_r[...] = jnp.zeros_like(lam_r)
            lam_i[...] = jnp.zeros_like(lam_i)
            for ref in (dbre_ref, dbim_ref, dcre_ref, dcim_ref, gabr_ref, gabi_ref, gcfr_ref, gcfi_ref, dd_ref):
                ref[...] = jnp.zeros_like(ref)

        uv = u_ref[...]
        dy = dyg_ref[...] * _dgelu(y_ref[...])
        dd_ref[...] += jnp.sum(dy * uv, axis=0, keepdims=True)
        dyb = dy.astype(BF16)
        ub = uv.astype(BF16)
        carry_r[...] = str_ref[0]
        carry_i[...] = sti_ref[0]
        for sb in range(nsb):
            us = slice(sb * sbw, (sb + 1) * sbw)
            ss = slice(sb * sbs, (sb + 1) * sbs)
            br = _dot(ub[:, us], bre_ref[sb], NN)
            bi = _dot(ub[:, us], bim_ref[sb], NN)
            bur[:, ss] = br
            bui[:, ss] = bi
            xr, xi = _cmul(cfr_ref[:, ss], cfi_ref[:, ss], br, bi)
            sr[:, ss] = xr
            si[:, ss] = xi
            lr[:, ss] = _dot(dyb[:, us], cre_ref[sb], NT)
            li[:, ss] = -_dot(dyb[:, us], cim_ref[sb], NT)
        abr, abi = abr_ref[...], abi_ref[...]
        apr, api = apr_ref[...], api_ref[...]
        _scan_segments(sr, si, abr, abi, apr, api, carry_r, carry_i, cm_r, cm_i, steps, False)
        for sb in range(nsb):
            us = slice(sb * sbw, (sb + 1) * sbw)
            ss = slice(sb * sbs, (sb + 1) * sbs)
            dcre_ref[sb] += _dot(sr[:, ss].astype(BF16), dyb[:, us], TN)
            dcim_ref[sb] -= _dot(si[:, ss].astype(BF16), dyb[:, us], TN)
        _scan_segments(lr, li, abr, -abi, apr, -api, lam_r, lam_i, cl_r, cl_i, steps, True)
        for c0 in range(0, n_state, SCAN_LANES):
            ls = slice(c0, c0 + SCAN_LANES)
            cfr = jnp.broadcast_to(cfr_ref[:, ls], (SUBLANES, SCAN_LANES))
            cfi = jnp.broadcast_to(cfi_ref[:, ls], (SUBLANES, SCAN_LANES))

            def step(j, acc, ls=ls, cfr=cfr, cfi=cfi):
                gar, gai, gcr, gci, pr, pi = acc
                r0 = pl.multiple_of(j * SUBLANES, SUBLANES)
                rws = pl.ds(r0, SUBLANES)
                l_r, l_i = lr[rws, ls], li[rws, ls]
                t_r, t_i = _cmul(pr, -pi, l_r, l_i)
                b_r, b_i = bur[rws, ls], bui[rws, ls]
                c_r, c_i = _cmul(b_r, -b_i, l_r, l_i)
                x_r, x_i = _cmul(cfr, -cfi, l_r, l_i)
                bur[rws, ls] = x_r
                bui[rws, ls] = x_i
                return gar + t_r, gai + t_i, gcr + c_r, gci + c_i, sr[rws, ls], si[rws, ls]

            zero = jnp.zeros((SUBLANES, SCAN_LANES), F32)
            gar, gai, gcr, gci, _, _ = lax.fori_loop(
                0, steps, step, (zero, zero, zero, zero, cm_r[:, ls], cm_i[:, ls]))
            gabr_ref[:, ls] += gar
            gabi_ref[:, ls] += gai
            gcfr_ref[:, ls] += gcr
            gcfi_ref[:, ls] += gci
        for sb in range(nsb):
            us = slice(sb * sbw, (sb + 1) * sbw)
            ss = slice(sb * sbs, (sb + 1) * sbs)
            xr, xi = bur[:, ss].astype(BF16), bui[:, ss].astype(BF16)
            du = _dot(xr, bre_ref[sb], NT) + _dot(xi, bim_ref[sb], NT)
            du_ref[:, us] = du + d_ref[:, us] * dy[:, us]
            dbre_ref[sb] += _dot(ub[:, us], xr, TN)
            dbim_ref[sb] += _dot(ub[:, us], xi, TN)

        @pl.when(pl.program_id(0) == nc - 1)
        def _():
            for src, dst in ((dbre_ref, dbre_hbm), (dbim_ref, dbim_hbm), (dcre_ref, dcre_hbm), (dcim_ref, dcim_hbm)):
                pltpu.sync_copy(src, dst)

    rev = lambda c: nc - 1 - c
    tile = pl.BlockSpec((chunk, w), lambda c: (rev(c), 0))
    row_n = pl.BlockSpec((1, n_state), lambda c: (0, 0))
    row_w = pl.BlockSpec((1, w), lambda c: (0, 0))
    st = pl.BlockSpec((1, 1, n_state), lambda c: (rev(c), 0, 0))
    acc8 = pl.BlockSpec((SUBLANES, n_state), lambda c: (0, 0))
    big = pltpu.VMEM((chunk, n_state), F32)
    row = pltpu.VMEM((1, n_state), F32)
    eight = pltpu.VMEM((SUBLANES, n_state), F32)
    f32 = lambda a: jax.ShapeDtypeStruct(a.shape, F32)
    held = [pltpu.VMEM(b.shape, BF16) for b in blocks] + [pltpu.VMEM(b.shape, F32) for b in blocks]
    vmem = (sum(_nbytes(b.shape, BF16) + _nbytes(b.shape, F32) for b in blocks)
            + 7 * _nbytes((chunk, n_state), F32) + 16 * _nbytes((chunk, w), F32))
    res, landed = _call(
        body, [u, y, dyg, st_re, st_im, bre, bim, cre, cim, *rows_p, d_row], name=name,
        out_shape=[jax.ShapeDtypeStruct((rows, w), F32), f32(bre), f32(bim), f32(cre), f32(cim)]
        + [jax.ShapeDtypeStruct((SUBLANES, n_state), F32)] * 4 + [jax.ShapeDtypeStruct((1, w), F32)],
        grid=(nc,),
        in_specs=[tile, tile, tile, st, st] + [_any_spec()] * 4 + [row_n] * 6 + [row_w],
        out_specs=[tile] + [_any_spec()] * 4 + [acc8] * 4 + [row_w],
        scratch_shapes=held + [big] * 6 + [row] * 4 + [eight] * 4,
        semantics=("arbitrary",), vmem=vmem, rider=rider)
    return res if rider is None else (res, landed)


def _loss_grad(x, mm, target, name):
    rows, d = x.shape

    def fn(xv, mv, tv):
        err = xv + mv - tv
        g = err * (1.0 / d)
        return g, g, 0.5 * err * g

    return _ew(fn, name=name, rows=rows, width=d, tiles=[(x, 0), (mm, 0), (target, 0)],
               outs=[(F32, d, 0), (BF16, d, 0)], accs=1)


def _pair_sum(grad, recv, name):
    r4, cdim = recv.shape
    r = r4 // N_CHIPS
    tr = _tile(r, 544, 16)
    g4 = grad.reshape(N_CHIPS, 2, r, cdim)
    r3 = recv.reshape(N_CHIPS, r, cdim)
    core = jnp.reshape(lax.axis_index("c"), (1,)).astype(jnp.int32)

    def body(c_ref, g_ref, r_ref, o_ref):
        o_ref[...] = (g_ref[0] + r_ref[...]).astype(BF16)

    out = pl.pallas_call(
        body, name=name, out_shape=jax.ShapeDtypeStruct((N_CHIPS, r, cdim), BF16),
        grid_spec=pltpu.PrefetchScalarGridSpec(
            num_scalar_prefetch=1, grid=(N_CHIPS, r // tr),
            in_specs=[pl.BlockSpec((1, 1, tr, cdim), lambda j, i, c: (j, c[0], i, 0)),
                      pl.BlockSpec((1, tr, cdim), lambda j, i, c: (j, i, 0))],
            out_specs=pl.BlockSpec((1, tr, cdim), lambda j, i, c: (j, i, 0))),
        compiler_params=_params(("parallel", "parallel"), 6 * _nbytes((tr, cdim), F32)),
    )(core, g4, r3)
    return out.reshape(r4, cdim)


def _chip_sum(recv, name):
    r4, cdim = recv.shape
    r = r4 // N_CHIPS
    tr = _tile(r, 544, 16)
    r3 = recv.reshape(N_CHIPS, r, cdim)

    def body(r_ref, o_ref):
        acc = r_ref[0].astype(F32)
        for j in range(1, N_CHIPS):
            acc = acc + r_ref[j].astype(F32)
        o_ref[...] = acc

    return pl.pallas_call(
        body, name=name, out_shape=jax.ShapeDtypeStruct((r, cdim), F32), grid=(r // tr,),
        in_specs=[pl.BlockSpec((N_CHIPS, tr, cdim), lambda i: (0, i, 0))],
        out_specs=pl.BlockSpec((tr, cdim), lambda i: (i, 0)),
        compiler_params=_params(("parallel",), 8 * _nbytes((tr, cdim), F32)),
    )(r3)


def _adamw_math(w, g, m, v):
    m = ADAM_B1 * m + (1.0 - ADAM_B1) * g
    v = ADAM_B2 * v + (1.0 - ADAM_B2) * (g * g)
    m_hat = m / (1.0 - ADAM_B1 ** ADAM_STEP)
    v_hat = v / (1.0 - ADAM_B2 ** ADAM_STEP)
    delta = -ADAM_LR * (m_hat / (jnp.sqrt(v_hat) + ADAM_EPS) + ADAM_WD * w)
    return delta, m, v


def _adamw(w, g, m, v, name):
    rows, cols = w.shape
    tr = _tile(rows, 256, SUBLANES)

    def body(w_ref, g_ref, m_ref, v_ref, d_ref, nm_ref, nv_ref):
        d, nm, nv = _adamw_math(w_ref[...], g_ref[...], m_ref[...], v_ref[...])
        d_ref[...] = d
        nm_ref[...] = nm
        nv_ref[...] = nv

    spec = pl.BlockSpec((tr, cols), lambda i: (i, 0))
    shp = jax.ShapeDtypeStruct((rows, cols), F32)
    return pl.pallas_call(
        body, name=name, out_shape=[shp] * 3, grid=(rows // tr,), in_specs=[spec] * 4, out_specs=[spec] * 3,
        compiler_params=_params(("parallel",)),
    )(w, g, m, v)


def _adamw_small(w, parts, m, v, name):
    rows, cols = w.shape
    p3 = parts.reshape(N_DEV, rows, cols)

    def body(w_ref, p_ref, m_ref, v_ref, g_ref, d_ref, nm_ref, nv_ref):
        g = p_ref[0]
        for k in range(1, N_DEV):
            g = g + p_ref[k]
        d, nm, nv = _adamw_math(w_ref[...], g, m_ref[...], v_ref[...])
        g_ref[...] = g
        d_ref[...] = d
        nm_ref[...] = nm
        nv_ref[...] = nv

    shp = jax.ShapeDtypeStruct((rows, cols), F32)
    return pl.pallas_call(body, name=name, out_shape=[shp] * 4)(w, p3, m, v)


SMALL = ("norm_w", "q_norm_w", "k_norm_w", "sinks", "A_re", "A_im", "log_dt", "B_re", "B_im", "C_re", "C_im",
         "D_skip", "b_glu")
LARGE = ("w_in", "w_attn_proj", "w_glu", "w_ssm_proj", "w_out")
ORDER = ("norm_w", "w_in", "q_norm_w", "k_norm_w", "sinks", "w_attn_proj", "A_re", "A_im", "log_dt", "B_re", "B_im",
         "C_re", "C_im", "D_skip", "w_glu", "b_glu", "w_ssm_proj", "w_out")


SMALL_REST = ("loss",) + SMALL[1:]


def _pack(named, keys):
    flat = jnp.concatenate([named[k].reshape(-1).astype(F32) for k in keys])
    n = flat.shape[0]
    rows = -(-n // (LANES * SUBLANES)) * SUBLANES
    return jnp.pad(flat, (0, rows * LANES - n)).reshape(rows, LANES)


def _unpack(packed, like, keys):
    flat = packed.reshape(-1)
    out, o = {}, 0
    for k in keys:
        n = like[k].size
        out[k] = flat[o:o + n].reshape(like[k].shape)
        o += n
    return out


def _step(xs, target, p, shards):
    s_in, s_ap, s_glu, s_sp, s_o = shards
    (w_in_t,) = _exchange(_all_gather([s_in]), "gather_w_in")
    seq, d = xs.shape
    attn_w = (d // 128) * HEAD_DIM
    n_q = attn_w // HEAD_DIM
    kv_w = N_KV_HEADS * HEAD_DIM
    ssm_w = d // 2
    n_groups = ssm_w // GROUP
    n_state = n_groups * STATE
    in_w = w_in_t.shape[0]
    assert in_w == 2 * attn_w + 2 * kv_w + 2 * ssm_w + 2 * d
    o_u = 2 * attn_w + 2 * kv_w
    o_z = o_u + ssm_w
    o_ga = o_z + ssm_w
    chunk = min(BLOCK, seq)
    cw = d // 4

    norm_row = p["norm_w"].reshape(1, d)
    h = _rmsnorm_fwd(xs, norm_row, "rmsnorm_fwd")
    proj, (w_ap_t, w_glu_t, w_sp_t, w_o) = _matmul(h, w_in_t, mode="nt", name="in_proj", tn=512,
                                                   rider=_all_gather([s_ap, s_glu, s_sp, s_o]))
    qw_row = jnp.tile(p["q_norm_w"], n_q).reshape(1, attn_w)
    kw_row = jnp.tile(p["k_norm_w"], N_KV_HEADS).reshape(1, kv_w)
    gmat = _head_mean_matrix()
    ag = _attention_fwd(proj, qw_row, kw_row, gmat, p["sinks"], attn_w=attn_w, kv_w=kv_w, name="attention_fwd")

    log_dt_col = p["log_dt"].reshape(n_groups, 1)
    prep = _ssm_prep(p["A_re"], p["A_im"], log_dt_col, chunk // SUBLANES, "ssm_prep")
    rows_p = [v.reshape(1, n_state) for v in prep]
    blocks = _ssm_blocks(p["B_re"], p["B_im"], p["C_re"], p["C_im"])
    d_row = p["D_skip"].reshape(1, ssm_w)
    u_seg = _to_segments(proj[:, o_u:o_u + ssm_w], chunk)
    y_seg, st_re, st_im = _ssm_fwd(u_seg, blocks, rows_p, d_row, chunk=chunk, name="ssm_fwd")
    y_ssm = _from_segments(y_seg, chunk)
    (yg,) = _ew(_gelu, name="gelu", rows=seq, width=ssm_w, tiles=[(y_ssm, 0)], outs=[(BF16, ssm_w, 0)], cw=cw)
    glu = _matmul(yg, w_glu_t, mode="nt", name="glu_proj", bias=p["b_glu"].reshape(1, 2 * ssm_w))
    (ts,) = _ew(lambda ga, gb, z: ga * _sigmoid(gb) * _silu(z), name="glu_gate", rows=seq, width=ssm_w,
                tiles=[(glu, 0), (glu, ssm_w), (proj, o_z)], outs=[(BF16, ssm_w, 0)], cw=cw)
    yy = _matmul(ag, w_ap_t, mode="nt", name="attn_proj", out_cols=(2 * d, 0))
    yy = _matmul(ts, w_sp_t, mode="nt", name="ssm_proj", out_cols=(2 * d, d), into=yy)
    (merged,) = _ew(lambda ya, ys, ga, gs: _sigmoid(ga) * ya + _sigmoid(gs) * ys, name="merge", rows=seq, width=d,
                    tiles=[(yy, 0), (yy, d), (proj, o_ga), (proj, o_ga + d)], outs=[(BF16, d, 0)], cw=cw)
    mm = _matmul(merged, w_o, mode="nn", name="out_proj")
    dout, dout_b, loss_cols = _loss_grad(xs, mm, target, "loss_grad")
    loss_local = jnp.sum(loss_cols)

    g_w_o = _matmul(merged, dout_b, mode="tn", name="grad_w_out", tk=1024)
    dmerged, (sib_o,) = _matmul(dout_b, w_o, mode="nt", name="d_merged", rider=_sibling_exchange([g_w_o]))
    pair_o = _pair_sum(g_w_o, sib_o, "pair_sum_w_out")

    def merge_bwd(dm, y, g):
        s = _sigmoid(g)
        return dm * s, dm * y * s * (1.0 - s)

    dyy, dproj = _ew(merge_bwd, name="merge_bwd", rows=seq, width=2 * d,
                     tiles=[(dmerged, 0, d), (yy, 0), (proj, o_ga)],
                     outs=[(BF16, 2 * d, 0), (BF16, in_w, o_ga)], cw=cw)
    dy_a, dy_s = Cols(dyy, 0, d), Cols(dyy, d, d)
    g_w_ap_t = _matmul(dy_a, ag, mode="tn", name="grad_w_attn_proj", tk=1024)
    g_w_sp_t = _matmul(dy_s, ts, mode="tn", name="grad_w_ssm_proj", tk=1024)
    d_ag = _matmul(dy_a, w_ap_t, mode="nn", name="d_attn_gated")
    d_ts = _matmul(dy_s, w_sp_t, mode="nn", name="d_ssm_gated")

    (dproj, dkv, g_qw, g_kw, g_sinks), (chips_o, sib_ap, sib_sp) = _attention_bwd(
        proj, d_ag, dproj, qw_row, kw_row, gmat, p["sinks"], attn_w=attn_w, kv_w=kv_w, name="attention_bwd",
        rider=_join(_chip_exchange([pair_o]), _sibling_exchange([g_w_ap_t, g_w_sp_t])))
    pair_ap = _pair_sum(g_w_ap_t, sib_ap, "pair_sum_w_attn_proj")
    pair_sp = _pair_sum(g_w_sp_t, sib_sp, "pair_sum_w_ssm_proj")
    dproj = _attention_dkv(dproj, dkv, attn_w=attn_w, kv_w=kv_w, name="attention_dkv")

    n_half = ssm_w // _tile(2 * ssm_w, cw)

    def glu_bwd(j, dt, ga, gb, z):
        sb, sz = _sigmoid(gb), _silu(z)
        dg = jnp.where(j < n_half, dt * sb * sz, dt * ga * sb * (1.0 - sb) * sz)
        return dg, dg

    glu_ops = [(d_ts, 0, ssm_w), (glu, 0, ssm_w), (glu, ssm_w, ssm_w), (proj, o_z, ssm_w)]
    dglu, g_bglu = _ew(glu_bwd, name="glu_bwd", rows=seq, width=2 * ssm_w, tiles=glu_ops,
                       outs=[(BF16, 2 * ssm_w, 0)], accs=1, cw=cw, with_col=True)
    (dproj,) = _ew(lambda dt, ga, gb, z: dt * ga * _sigmoid(gb) * _dsilu(z), name="glu_bwd_z", rows=seq,
                   width=ssm_w, tiles=glu_ops, outs=[(BF16, in_w, o_z)], into=[dproj], cw=cw)
    g_w_glu_t = _matmul(dglu, yg, mode="tn", name="grad_w_glu", tk=1024)
    d_yg = _matmul(dglu, w_glu_t, mode="nn", name="d_gelu")
    ((du_seg, db_re, db_im, dc_re, dc_im, gabr, gabi, gcfr, gcfi, g_d), (chips_ap, chips_sp, sib_glu)) = _ssm_bwd(
        u_seg, y_seg, _to_segments(d_yg, chunk), st_re, st_im, blocks, rows_p, d_row, chunk=chunk, name="ssm_bwd",
        rider=_join(_chip_exchange([pair_ap, pair_sp]), _sibling_exchange([g_w_glu_t])))
    pair_glu = _pair_sum(g_w_glu_t, sib_glu, "pair_sum_w_glu")
    (dproj,) = _ew(lambda v: v, name="du_store", rows=seq, width=ssm_w, tiles=[(_from_segments(du_seg, chunk), 0)],
                   outs=[(BF16, in_w, o_u)], into=[dproj], cw=cw)
    g_a_re, g_a_im, g_log_dt = _ssm_param_bwd(
        p["A_re"], p["A_im"], log_dt_col, *[g.reshape(SUBLANES, n_groups, STATE) for g in (gabr, gabi, gcfr, gcfi)],
        "ssm_param_bwd")
    small_grads = dict(
        loss=loss_local, q_norm_w=g_qw.reshape(n_q, HEAD_DIM).sum(0), k_norm_w=g_kw.reshape(N_KV_HEADS, HEAD_DIM).sum(0),
        sinks=g_sinks[0, :n_q], A_re=g_a_re, A_im=g_a_im, log_dt=g_log_dt.reshape(n_groups),
        B_re=_unblock_b(db_re), B_im=_unblock_b(db_im), C_re=_unblock_c(dc_re), C_im=_unblock_c(dc_im),
        D_skip=g_d.reshape(n_groups, GROUP), b_glu=g_bglu.reshape(2 * ssm_w))

    n_parts = W_IN_GRAD_PARTS
    wq = d // n_parts
    g_parts, pair_parts, chip_parts = [], [], []
    extra = [_chip_exchange([pair_glu]), _all_gather([_pack(small_grads, SMALL_REST)])]
    chips_glu = small_parts = dh = grad_x = g_norm = None
    for step in range(n_parts + 2):
        riders = list(extra) if step == 0 else []
        if 0 <= step - 2 < n_parts:
            riders.append(_chip_exchange([pair_parts[step - 2]]))
        if 0 <= step - 1 < n_parts:
            riders.append(_sibling_exchange([g_parts[step - 1]]))
        rider = _join(*riders) if riders else None
        if step < n_parts:
            res = _matmul(dproj, Cols(h, step * wq, wq), mode="tn", name="grad_w_in_%d" % step, tk=1024, rider=rider)
            out, landed = res if rider is not None else (res, [])
            g_parts.append(out)
        elif step == n_parts:
            dh, landed = _matmul(dproj, w_in_t, mode="nn", name="d_normed", tk=2176, rider=rider)
        else:
            (grad_x, g_norm), landed = _rmsnorm_bwd(xs, norm_row, dh, dout, "rmsnorm_bwd", rider=rider)
        landed = list(landed)
        if step == 0:
            chips_glu, small_parts = landed[:2]
            landed = landed[2:]
        if 0 <= step - 2 < n_parts:
            chip_parts.append(landed.pop(0))
        if 0 <= step - 1 < n_parts:
            pair_parts.append(_pair_sum(g_parts[step - 1], landed.pop(0), "pair_sum_w_in_%d" % (step - 1)))
    (norm_parts,) = _exchange(_all_gather([_pack(dict(norm_w=g_norm), ("norm_w",))]), "gather_norm_grad")
    g_in = jnp.concatenate([_chip_sum(c, "chip_sum_w_in_%d" % q) for q, c in enumerate(chip_parts)], axis=1)
    summed = [g_in] + [_chip_sum(c, "chip_sum_" + k)
                       for k, c in zip(LARGE[1:], (chips_ap, chips_glu, chips_sp, chips_o))]
    return grad_x, summed, small_parts, norm_parts


def kernel(x, norm_w, w_in, q_norm_w, k_norm_w, sinks, w_attn_proj, A_re, A_im, log_dt, B_re, B_im, C_re, C_im, D_skip, w_glu, b_glu, w_ssm_proj, w_out, loss_target, m_norm_w, m_w_in, m_q_norm_w, m_k_norm_w, m_sinks, m_w_attn_proj, m_A_re, m_A_im, m_log_dt, m_B_re, m_B_im, m_C_re, m_C_im, m_D_skip, m_w_glu, m_b_glu, m_w_ssm_proj, m_w_out, v_norm_w, v_w_in, v_q_norm_w, v_k_norm_w, v_sinks, v_w_attn_proj, v_A_re, v_A_im, v_log_dt, v_B_re, v_B_im, v_C_re, v_C_im, v_D_skip, v_w_glu, v_b_glu, v_w_ssm_proj, v_w_out):
    weights = dict(norm_w=norm_w, w_in=w_in, q_norm_w=q_norm_w, k_norm_w=k_norm_w, sinks=sinks,
                   w_attn_proj=w_attn_proj, A_re=A_re, A_im=A_im, log_dt=log_dt, B_re=B_re, B_im=B_im, C_re=C_re,
                   C_im=C_im, D_skip=D_skip, w_glu=w_glu, b_glu=b_glu, w_ssm_proj=w_ssm_proj, w_out=w_out)
    m_in = dict(norm_w=m_norm_w, w_in=m_w_in, q_norm_w=m_q_norm_w, k_norm_w=m_k_norm_w, sinks=m_sinks,
                w_attn_proj=m_w_attn_proj, A_re=m_A_re, A_im=m_A_im, log_dt=m_log_dt, B_re=m_B_re, B_im=m_B_im,
                C_re=m_C_re, C_im=m_C_im, D_skip=m_D_skip, w_glu=m_w_glu, b_glu=m_b_glu, w_ssm_proj=m_w_ssm_proj,
                w_out=m_w_out)
    v_in = dict(norm_w=v_norm_w, w_in=v_w_in, q_norm_w=v_q_norm_w, k_norm_w=v_k_norm_w, sinks=v_sinks,
                w_attn_proj=v_w_attn_proj, A_re=v_A_re, A_im=v_A_im, log_dt=v_log_dt, B_re=v_B_re, B_im=v_B_im,
                C_re=v_C_re, C_im=v_C_im, D_skip=v_D_skip, w_glu=v_w_glu, b_glu=v_b_glu, w_ssm_proj=v_w_ssm_proj,
                w_out=v_w_out)

    _, seq, d = x.shape
    column_sharded = LARGE[:4]
    as_rows = lambda k, a: a.T if k in column_sharded else a
    shards = [as_rows(k, weights[k]).astype(BF16) for k in LARGE]
    small = {k: weights[k] for k in SMALL}
    grad_x, summed, small_parts, norm_parts = _step(x.reshape(seq, d), loss_target.reshape(seq, d), small, shards)

    grads, delta, new_m, new_v = {}, {}, {}, {}
    for k, g in zip(LARGE, summed):
        if k == "w_in":
            upd = _adamw(weights[k].T, g, m_in[k].T, v_in[k].T, "adamw_" + k)
            grads[k], delta[k], new_m[k], new_v[k] = [a.T for a in (g, *upd)]
        else:
            grads[k] = as_rows(k, g)
            delta[k], new_m[k], new_v[k] = _adamw(weights[k], grads[k], m_in[k], v_in[k], "adamw_" + k)

    zero = jnp.zeros((), F32)
    for keys, parts in ((SMALL_REST, small_parts), (("norm_w",), norm_parts)):
        like = dict(small, loss=zero)
        packs = [_pack(dict(src, loss=zero), keys) for src in (weights, m_in, v_in)]
        res = _adamw_small(packs[0], parts, packs[1], packs[2], "adamw_small_%d" % len(keys))
        for dst, r in zip((grads, delta, new_m, new_v), res):
            dst.update(_unpack(r, like, keys))
    loss = grads["loss"]

    return (loss, grad_x.reshape(x.shape), *[grads[k] for k in ORDER], *[delta[k] for k in ORDER],
            *[new_m[k] for k in ORDER], *[new_v[k] for k in ORDER])
```

```python
import math
from typing import Callable, NamedTuple

import jax
import jax.numpy as jnp
from jax import lax
from jax.experimental import pallas as pl
from jax.experimental.pallas import tpu as pltpu

F32 = jnp.float32
BF16 = jnp.bfloat16
MESH = pl.DeviceIdType.MESH

HEAD_DIM = 64
N_KV_HEADS = 4
GROUP = 16
STATE = 64
BLOCK = 128
NORM_EPS = 1e-6
N_DEV = 8
N_CHIPS = 4
LANES = 128
SUBLANES = 8
MXU_DIM = 256
VMEM_BYTES = 64 * 1024 * 1024
VMEM_CAP = VMEM_BYTES - 8 * 1024 * 1024

ADAM_LR = 0.001
ADAM_B1 = 0.9
ADAM_B2 = 0.999
ADAM_EPS = 1e-08
ADAM_WD = 0.01
ADAM_STEP = 10

GELU_C = math.sqrt(2.0 / math.pi)
GELU_K = 0.044715


def _tile(dim, pref, mult=LANES):
    if dim <= pref:
        return dim
    best = None
    for d in range(mult, pref + 1, mult):
        if dim % d == 0:
            best = d
    assert best is not None, (dim, pref, mult)
    return best


def _params(semantics=None, vmem=None):
    kw = {}
    if semantics is not None:
        kw["dimension_semantics"] = semantics
    if vmem is not None:
        kw["vmem_limit_bytes"] = int(min(VMEM_CAP, max(vmem, 32 * 1024 * 1024)))
    return pltpu.CompilerParams(**kw)


def _nbytes(shape, dtype):
    return math.prod(shape) * jnp.dtype(dtype).itemsize


def _sigmoid(x):
    return 1.0 / (1.0 + jnp.exp(-x))


def _silu(x):
    return x * _sigmoid(x)


def _dsilu(x):
    s = _sigmoid(x)
    return s * (1.0 + x * (1.0 - s))


def _gelu(x):
    return 0.5 * x * (1.0 + jnp.tanh(GELU_C * (x + GELU_K * x * x * x)))


def _dgelu(x):
    t = jnp.tanh(GELU_C * (x + GELU_K * x * x * x))
    return 0.5 * (1.0 + t) + 0.5 * x * (1.0 - t * t) * GELU_C * (1.0 + 3.0 * GELU_K * x * x)


def _dot(a, b, dims):
    return lax.dot_general(a, b, (dims, ((), ())), preferred_element_type=F32)


NN = ((1,), (0,))
NT = ((1,), (1,))
TN = ((0,), (0,))


def _any_spec():
    return pl.BlockSpec(memory_space=pl.ANY)


def _pallas(body, **kw):
    pin = lambda s: pltpu.HBM(s.shape, s.dtype) if isinstance(s, jax.ShapeDtypeStruct) else s
    out_shape = kw.pop("out_shape")
    out_shape = [pin(s) for s in out_shape] if isinstance(out_shape, (list, tuple)) else pin(out_shape)
    call = pl.pallas_call(body, out_shape=out_shape, **kw)

    def run(*operands):
        pinned = [pltpu.with_memory_space_constraint(o, pltpu.HBM) if jnp.issubdtype(o.dtype, jnp.floating) else o
                  for o in operands]
        return call(*pinned)

    return run


class Rider(NamedTuple):
    operands: tuple
    out_shapes: tuple
    sems: tuple
    start: Callable
    finish: Callable


def _all_gather(shards):
    n = len(shards)

    def copies(ins, outs, sems):
        send_sems, recv_sems, local_sems = sems
        x, y, c = lax.axis_index("x"), lax.axis_index("y"), lax.axis_index("c")
        me, sibling = (x, y, c), (x, y, 1 - c)
        chips = [(1 - x, y), (x, 1 - y), (1 - x, 1 - y)]

        def rows(k, px, py, pc):
            r = shards[k].shape[0]
            return outs[k].at[pl.ds((4 * px + 2 * py + pc) * r, r), :]

        def copy(k, s, block, to, src=None):
            return pltpu.make_async_remote_copy(
                src_ref=rows(k, *block) if src is None else src, dst_ref=rows(k, *block),
                send_sem=send_sems.at[7 * k + s], recv_sem=recv_sems.at[7 * k + s],
                device_id=to, device_id_type=MESH)

        mine = [pltpu.make_async_copy(ins[k], rows(k, *me), local_sems.at[k]) for k in range(n)]
        first = []
        for k in range(n):
            first.append(copy(k, 0, me, sibling, src=ins[k]))
            first += [copy(k, 1 + j, me, (*chip, c), src=ins[k]) for j, chip in enumerate(chips)]
        return me, sibling, chips, c, copy, mine, first

    def start(ins, outs, sems):
        *_, mine, first = copies(ins, outs, sems)
        for cp in mine + first:
            cp.start()

    def finish(ins, outs, sems):
        me, sibling, chips, c, copy, mine, first = copies(ins, outs, sems)
        passed = []
        for j, chip in enumerate(chips):
            for k in range(n):
                copy(k, 1 + j, (*chip, c), me).wait_recv()
                fwd = copy(k, 4 + j, (*chip, c), sibling)
                fwd.start()
                passed.append(fwd)
        for k in range(n):
            copy(k, 0, sibling, me).wait_recv()
            for j, chip in enumerate(chips):
                copy(k, 4 + j, (*chip, 1 - c), me).wait_recv()
        for cp in first + passed:
            cp.wait_send()
        for cp in mine:
            cp.wait()

    return Rider(
        tuple(shards),
        tuple(jax.ShapeDtypeStruct((N_DEV * s.shape[0], s.shape[1]), s.dtype) for s in shards),
        (pltpu.SemaphoreType.DMA((7 * n,)), pltpu.SemaphoreType.DMA((7 * n,)), pltpu.SemaphoreType.DMA((n,))),
        start, finish)


def _sibling_exchange(grads):
    n = len(grads)

    def copies(ins, outs, sems):
        send_sems, recv_sems = sems
        x, y, c = lax.axis_index("x"), lax.axis_index("y"), lax.axis_index("c")
        out = []
        for k in range(n):
            r = grads[k].shape[0] // N_DEV
            for j in range(N_CHIPS):
                out.append(pltpu.make_async_remote_copy(
                    src_ref=ins[k].at[pl.ds((2 * j + 1 - c) * r, r), :],
                    dst_ref=outs[k].at[pl.ds(j * r, r), :],
                    send_sem=send_sems.at[N_CHIPS * k + j], recv_sem=recv_sems.at[N_CHIPS * k + j],
                    device_id=(x, y, 1 - c), device_id_type=MESH))
        return out

    def start(ins, outs, sems):
        for cp in copies(ins, outs, sems):
            cp.start()

    def finish(ins, outs, sems):
        for cp in copies(ins, outs, sems):
            cp.wait()

    return Rider(
        tuple(grads), tuple(jax.ShapeDtypeStruct((g.shape[0] // 2, g.shape[1]), g.dtype) for g in grads),
        (pltpu.SemaphoreType.DMA((N_CHIPS * n,)), pltpu.SemaphoreType.DMA((N_CHIPS * n,))), start, finish)


def _chip_exchange(parts):
    n = len(parts)

    def copies(ins, outs, sems):
        send_sems, recv_sems, local_sems = sems
        x, y, c = lax.axis_index("x"), lax.axis_index("y"), lax.axis_index("c")
        my_chip = 2 * x + y
        chips = [(1 - x, y), (x, 1 - y), (1 - x, 1 - y)]
        local, sent = [], []
        for k in range(n):
            r = parts[k].shape[0] // N_CHIPS
            mine = pl.ds(my_chip * r, r)
            local.append(pltpu.make_async_copy(ins[k].at[mine, :], outs[k].at[mine, :], local_sems.at[k]))
            for s, (px, py) in enumerate(chips):
                sent.append(pltpu.make_async_remote_copy(
                    src_ref=ins[k].at[pl.ds((2 * px + py) * r, r), :], dst_ref=outs[k].at[mine, :],
                    send_sem=send_sems.at[3 * k + s], recv_sem=recv_sems.at[3 * k + s],
                    device_id=(px, py, c), device_id_type=MESH))
        return local, sent

    def start(ins, outs, sems):
        local, sent = copies(ins, outs, sems)
        for cp in local + sent:
            cp.start()

    def finish(ins, outs, sems):
        local, sent = copies(ins, outs, sems)
        for cp in sent + local:
            cp.wait()

    return Rider(
        tuple(parts), tuple(jax.ShapeDtypeStruct(p.shape, p.dtype) for p in parts),
        (pltpu.SemaphoreType.DMA((3 * n,)), pltpu.SemaphoreType.DMA((3 * n,)), pltpu.SemaphoreType.DMA((n,))),
        start, finish)


def _join(*riders):
    cuts_in, cuts_out, cuts_sem = [0], [0], [0]
    for r in riders:
        cuts_in.append(cuts_in[-1] + len(r.operands))
        cuts_out.append(cuts_out[-1] + len(r.out_shapes))
        cuts_sem.append(cuts_sem[-1] + len(r.sems))

    def each(which):
        def run(ins, outs, sems):
            for i, r in enumerate(riders):
                getattr(r, which)(ins[cuts_in[i]:cuts_in[i + 1]], outs[cuts_out[i]:cuts_out[i + 1]],
                                  sems[cuts_sem[i]:cuts_sem[i + 1]])
        return run

    return Rider(sum((r.operands for r in riders), ()), sum((r.out_shapes for r in riders), ()),
                 sum((r.sems for r in riders), ()), each("start"), each("finish"))


def _call(body, operands, *, name, out_shape, grid, in_specs, out_specs, scratch_shapes=(), aliases=None,
          semantics=None, vmem=None, rider=None):
    operands, out_shape, scratch_shapes = list(operands), list(out_shape), list(scratch_shapes)
    in_specs, out_specs = list(in_specs), list(out_specs)
    if rider is None:
        res = _pallas(
            body, name=name, out_shape=out_shape, grid=grid, in_specs=in_specs, out_specs=out_specs,
            scratch_shapes=scratch_shapes, input_output_aliases=aliases or {},
            compiler_params=_params(semantics, vmem))(*operands)
        return list(res), []
    n_in, n_out, n_scr = len(operands), len(out_shape), len(scratch_shapes)
    ri, ro = len(rider.operands), len(rider.out_shapes)

    def carried(*refs):
        a, b = n_in, n_in + ri
        c, d = b + n_out, b + n_out + ro
        e = d + n_scr
        ids = [pl.program_id(k) for k in range(len(grid))]
        first = ids[0] == 0
        last = ids[0] == grid[0] - 1
        for k in range(1, len(grid)):
            first = jnp.logical_and(first, ids[k] == 0)
            last = jnp.logical_and(last, ids[k] == grid[k] - 1)

        @pl.when(first)
        def _():
            rider.start(refs[a:b], refs[c:d], refs[e:])

        body(*refs[:a], *refs[b:c], *refs[d:e])

        @pl.when(last)
        def _():
            rider.finish(refs[a:b], refs[c:d], refs[e:])

    res = _pallas(
        carried, name=name, out_shape=out_shape + list(rider.out_shapes), grid=grid,
        in_specs=in_specs + [_any_spec()] * ri, out_specs=out_specs + [_any_spec()] * ro,
        scratch_shapes=scratch_shapes + list(rider.sems), input_output_aliases=aliases or {},
        compiler_params=_params(("arbitrary",) * len(grid), vmem))(*operands, *rider.operands)
    return list(res[:n_out]), list(res[n_out:])


def _exchange(rider, name):
    ri, ro = len(rider.operands), len(rider.out_shapes)

    def body(*refs):
        rider.start(refs[:ri], refs[ri:ri + ro], refs[ri + ro:])
        rider.finish(refs[:ri], refs[ri:ri + ro], refs[ri + ro:])

    return _pallas(
        body, name=name, out_shape=list(rider.out_shapes), in_specs=[_any_spec()] * ri,
        out_specs=[_any_spec()] * ro, scratch_shapes=list(rider.sems))(*rider.operands)


class Cols(NamedTuple):
    arr: jax.Array
    off: int
    width: int


def _cols(a):
    return a if isinstance(a, Cols) else Cols(a, 0, a.shape[1])


def _matmul(a, b, *, mode, name, out_dtype=F32, tm=1024, tn=1024, tk=2048, bias=None, out_cols=None, into=None,
            rider=None):
    a, b = _cols(a), _cols(b)
    if mode == "nn":
        (m, k), (k2, n) = (a.arr.shape[0], a.width), (b.arr.shape[0], b.width)
    elif mode == "nt":
        (m, k), (n, k2) = (a.arr.shape[0], a.width), (b.arr.shape[0], b.width)
    else:
        (k, m), (k2, n) = (a.arr.shape[0], a.width), (b.arr.shape[0], b.width)
    assert k == k2, (a.arr.shape, b.arr.shape, mode)
    tm, tn, tk = _tile(m, tm), _tile(n, tn), _tile(k, tk)
    nk = k // tk
    dims = {"nn": NN, "nt": NT, "tn": TN}[mode]
    if mode == "tn":
        assert a.off % tm == 0
        a_spec = pl.BlockSpec((tk, tm), lambda i, j, kk, o=a.off // tm: (kk, i + o))
    else:
        assert a.off % tk == 0
        a_spec = pl.BlockSpec((tm, tk), lambda i, j, kk, o=a.off // tk: (i, kk + o))
    if mode == "nt":
        assert b.off % tk == 0
        b_spec = pl.BlockSpec((tn, tk), lambda i, j, kk, o=b.off // tk: (j, kk + o))
    else:
        assert b.off % tn == 0
        b_spec = pl.BlockSpec((tk, tn), lambda i, j, kk, o=b.off // tn: (kk, j + o))
    in_specs, operands = [a_spec, b_spec], [a.arr, b.arr]
    if bias is not None:
        in_specs.append(pl.BlockSpec((1, tn), lambda i, j, kk: (0, j)))
        operands.append(bias)
    total_w, o_off = out_cols if out_cols is not None else (n, 0)
    assert o_off % tn == 0
    aliases = {}
    if into is not None:
        assert into.shape == (m, total_w) and into.dtype == out_dtype
        in_specs.append(_any_spec())
        operands.append(into)
        aliases = {len(operands) - 1: 0}
    n_in = len(operands)

    def body(*refs):
        a_ref, b_ref = refs[0], refs[1]
        bias_ref = refs[2] if bias is not None else None
        o_ref = refs[n_in]
        acc_ref = refs[-1] if nk > 1 else None
        part = _dot(a_ref[...].astype(BF16), b_ref[...].astype(BF16), dims)

        def finish(acc):
            if bias_ref is not None:
                acc = acc + bias_ref[...]
            o_ref[...] = acc.astype(out_dtype)

        if nk == 1:
            finish(part)
        else:
            kk = pl.program_id(2)

            @pl.when(kk == 0)
            def _():
                acc_ref[...] = part

            @pl.when(kk > 0)
            def _():
                acc_ref[...] += part

            @pl.when(kk == nk - 1)
            def _():
                finish(acc_ref[...])

    vmem = 2 * (_nbytes((tm, tk), a.arr.dtype) + _nbytes((tk, tn), b.arr.dtype) + _nbytes((tm, tn), out_dtype))
    vmem += 3 * _nbytes((tm, tn), F32)
    (out,), landed = _call(
        body, operands, name=name, out_shape=[jax.ShapeDtypeStruct((m, total_w), out_dtype)],
        grid=(m // tm, n // tn, nk), in_specs=in_specs,
        out_specs=[pl.BlockSpec((tm, tn), lambda i, j, kk, o=o_off // tn: (i, j + o))],
        scratch_shapes=[pltpu.VMEM((tm, tn), F32)] if nk > 1 else [], aliases=aliases,
        semantics=("parallel", "parallel", "arbitrary"), vmem=vmem, rider=rider)
    return out if rider is None else (out, landed)


def _ew(fn, *, name, rows, width, tiles, vecs=(), outs, accs=0, tl=1024, cw=512, into=None, with_col=False):
    tl, cw = _tile(rows, tl, SUBLANES), _tile(width, cw)
    ncol = width // cw
    nt_, nv = len(tiles), len(vecs)
    into = list(into) if into is not None else [None] * len(outs)
    aliased = [t for t in into if t is not None]

    def off(o):
        assert o % cw == 0, (name, o, cw)
        return o // cw

    in_specs, vmem = [], 0
    for t in tiles:
        arr, o = t[0], off(t[1])
        wrap = t[2] // cw if len(t) > 2 else ncol
        in_specs.append(pl.BlockSpec((tl, cw), lambda j, i, o=o, wrap=wrap: (i, o + j % wrap)))
        vmem += _nbytes((tl, cw), arr.dtype)
    in_specs += [pl.BlockSpec((1, cw), lambda j, i, o=off(o): (0, j + o)) for _, o in vecs]
    in_specs += [_any_spec() for _ in aliased]
    out_shape, out_specs, aliases = [], [], {}
    n_in = nt_ + nv
    for idx, ((dt, tw, o), tgt) in enumerate(zip(outs, into)):
        out_shape.append(jax.ShapeDtypeStruct((rows, tw), dt))
        out_specs.append(pl.BlockSpec((tl, cw), lambda j, i, o=off(o): (i, j + o)))
        vmem += _nbytes((tl, cw), dt)
        if tgt is not None:
            assert tgt.shape == (rows, tw) and tgt.dtype == dt, (name, tgt.shape, tgt.dtype)
            aliases[n_in + len(aliases)] = idx
    for _ in range(accs):
        out_shape.append(jax.ShapeDtypeStruct((1, width), F32))
        out_specs.append(pl.BlockSpec((1, cw), lambda j, i: (0, j)))
    n_out = len(outs)

    def body(*refs):
        vals = [r[...] for r in refs[:n_in]]
        out_refs = refs[n_in + len(aliased):]
        res = fn(pl.program_id(0), *vals) if with_col else fn(*vals)
        res = res if isinstance(res, (tuple, list)) else (res,)
        assert len(res) == n_out + accs, (name, len(res))
        for r, v in zip(out_refs[:n_out], res[:n_out]):
            r[...] = v.astype(r.dtype)
        first = pl.program_id(1) == 0
        for r, v in zip(out_refs[n_out:], res[n_out:]):
            s = jnp.sum(v, axis=0, keepdims=True)

            @pl.when(first)
            def _(r=r, s=s):
                r[...] = s

            @pl.when(jnp.logical_not(first))
            def _(r=r, s=s):
                r[...] += s

    return _pallas(
        body, name=name, out_shape=out_shape, grid=(ncol, rows // tl),
        in_specs=in_specs, out_specs=out_specs, input_output_aliases=aliases,
        compiler_params=_params(("parallel", "arbitrary"), 3 * vmem),
    )(*[t[0] for t in tiles], *[v for v, _ in vecs], *aliased)


def _rmsnorm_fwd(x, w_row, name):
    rows, d = x.shape
    tl = _tile(rows, 512, SUBLANES)

    def body(x_ref, w_ref, h_ref):
        xv = x_ref[...]
        rstd = lax.rsqrt(jnp.mean(xv * xv, axis=-1, keepdims=True) + NORM_EPS)
        h_ref[...] = (xv * rstd * w_ref[...]).astype(BF16)

    return _pallas(
        body, name=name, out_shape=jax.ShapeDtypeStruct((rows, d), BF16), grid=(rows // tl,),
        in_specs=[pl.BlockSpec((tl, d), lambda i: (i, 0)), pl.BlockSpec((1, d), lambda i: (0, 0))],
        out_specs=pl.BlockSpec((tl, d), lambda i: (i, 0)),
        compiler_params=_params(("parallel",)),
    )(x, w_row)


def _rmsnorm_bwd(x, w_row, dh, dout, name, rider=None):
    rows, d = x.shape
    tl = _tile(rows, 256, SUBLANES)

    def body(x_ref, w_ref, dh_ref, dout_ref, gx_ref, gw_ref):
        xv = x_ref[...]
        rstd = lax.rsqrt(jnp.mean(xv * xv, axis=-1, keepdims=True) + NORM_EPS)
        xn = xv * rstd
        dhv = dh_ref[...]
        dxn = dhv * w_ref[...]
        dx = rstd * (dxn - xn * jnp.mean(dxn * xn, axis=-1, keepdims=True))
        gx_ref[...] = dout_ref[...] + dx
        gw = jnp.sum(dhv * xn, axis=0, keepdims=True)

        @pl.when(pl.program_id(0) == 0)
        def _():
            gw_ref[...] = gw

        @pl.when(pl.program_id(0) > 0)
        def _():
            gw_ref[...] += gw

    tile = pl.BlockSpec((tl, d), lambda i: (i, 0))
    row = pl.BlockSpec((1, d), lambda i: (0, 0))
    res, landed = _call(
        body, [x, w_row, dh, dout], name=name,
        out_shape=[jax.ShapeDtypeStruct((rows, d), F32), jax.ShapeDtypeStruct((1, d), F32)],
        grid=(rows // tl,), in_specs=[tile, row, tile, tile], out_specs=[tile, row],
        semantics=("arbitrary",), rider=rider)
    return res if rider is None else (res, landed)


def _head_mean(x, gmat):
    hi = x.astype(BF16)
    lo = (x - hi.astype(F32)).astype(BF16)
    out = []
    for s in range(x.shape[1] // MXU_DIM):
        sl = slice(s * MXU_DIM, (s + 1) * MXU_DIM)
        out.append(_dot(hi[:, sl], gmat, NN) + _dot(lo[:, sl], gmat, NN))
    return out[0] if len(out) == 1 else jnp.concatenate(out, axis=1)


def _head_mean_matrix():
    blk = jnp.arange(MXU_DIM) // HEAD_DIM
    return jnp.where(blk[:, None] == blk[None, :], 1.0 / HEAD_DIM, 0.0).astype(BF16)


def _spread_head(x, g, width):
    col = x[:, (g // 2) * LANES:(g // 2 + 1) * LANES]
    other = pltpu.roll(col, HEAD_DIM, axis=1)
    low = lax.broadcasted_iota(jnp.int32, col.shape, 1) < HEAD_DIM
    both = jnp.where(low, col, other) if g % 2 == 0 else jnp.where(low, other, col)
    return both if width == LANES else jnp.concatenate([both] * (width // LANES), axis=1)


def _head_diagonal(t, per_kv):
    head = lax.broadcasted_iota(jnp.int32, t.shape, 1) // HEAD_DIM
    zero = jnp.zeros_like(t)
    return jnp.concatenate([jnp.where(head == r, t, zero) for r in range(per_kv)], axis=0)


def _fold_heads(x, per_kv):
    rows = x.shape[0] // per_kv
    head = lax.broadcasted_iota(jnp.int32, (rows, x.shape[1]), 1) // HEAD_DIM
    acc = jnp.where(head == 0, x[0:rows], 0.0)
    for r in range(1, per_kv):
        acc = acc + jnp.where(head == r, x[r * rows:(r + 1) * rows], 0.0)
    while acc.shape[1] > LANES:
        half = acc.shape[1] // 2
        acc = acc[:, :half] + acc[:, half:]
    return acc + pltpu.roll(acc, HEAD_DIM, axis=1)


def _join_heads(parts):
    low = lax.broadcasted_iota(jnp.int32, parts[0].shape, 1) < HEAD_DIM
    cols = [jnp.where(low, parts[2 * j], parts[2 * j + 1]) for j in range(len(parts) // 2)]
    return cols[0] if len(cols) == 1 else jnp.concatenate(cols, axis=1)


def _attn_specs(attn_w, kv_w):
    half = attn_w // 2
    kcol, vcol = attn_w // kv_w, attn_w // kv_w + 1
    gcol = (attn_w + 2 * kv_w) // half
    prev = lambda i: jnp.maximum(i - 1, 0)
    return [
        pl.BlockSpec((BLOCK, attn_w), lambda i: (i, 0)),
        pl.BlockSpec((BLOCK, kv_w), lambda i: (prev(i), kcol)),
        pl.BlockSpec((BLOCK, kv_w), lambda i: (i, kcol)),
        pl.BlockSpec((BLOCK, kv_w), lambda i: (prev(i), vcol)),
        pl.BlockSpec((BLOCK, kv_w), lambda i: (i, vcol)),
        pl.BlockSpec((BLOCK, half), lambda i: (i, gcol)),
        pl.BlockSpec((BLOCK, half), lambda i: (i, gcol + 1)),
    ]


def _band_mask(i):
    q_loc = lax.broadcasted_iota(jnp.int32, (BLOCK, 2 * BLOCK), 0) + BLOCK
    k_loc = lax.broadcasted_iota(jnp.int32, (BLOCK, 2 * BLOCK), 1)
    diff = q_loc - k_loc
    first_key = jnp.where(i == 0, BLOCK, 0)
    return (diff >= 0) & (diff < BLOCK) & (k_loc >= first_key)


def _softmax_with_sink(s, sink):
    m = jnp.maximum(jnp.max(s, axis=-1, keepdims=True), sink)
    p = jnp.exp(s - m)
    e_sink = jnp.exp(sink - m)
    den = jnp.sum(p, axis=-1, keepdims=True) + e_sink
    inv = 1.0 / den
    return p * inv, e_sink * inv


def _attn_block(i, q, kk, vv, qw, kw, gmat, sink_ref, per_kv):
    scale = 1.0 / math.sqrt(HEAD_DIM)
    keys = 2 * BLOCK
    valid = _band_mask(i)
    q_rstd = lax.rsqrt(_head_mean(q * q, gmat) + NORM_EPS)
    qn = q * q_rstd
    qh = (qn * qw).astype(BF16)
    k_rstd = lax.rsqrt(_head_mean(kk * kk, gmat) + NORM_EPS)
    kn = kk * k_rstd
    kh = kn * kw
    gw = per_kv * HEAD_DIM
    groups = []
    for g in range(N_KV_HEADS):
        kd = _head_diagonal(_spread_head(kh, g, gw).astype(BF16), per_kv)
        vd = _head_diagonal(_spread_head(vv, g, gw).astype(BF16), per_kv)
        qg = qh[:, g * gw:(g + 1) * gw]
        s_all = _dot(qg, kd, NT) * scale
        ps, p_sinks = [], []
        for r in range(per_kv):
            s = jnp.where(valid, s_all[:, r * keys:(r + 1) * keys], -1e30)
            p, p_sink = _softmax_with_sink(s, sink_ref[g * per_kv + r])
            ps.append(p)
            p_sinks.append(p_sink)
        pb = jnp.concatenate(ps, axis=1).astype(BF16)
        groups.append((kd, vd, qg, ps, p_sinks, pb, _dot(pb, vd, NN)))
    return qn, q_rstd, kn, k_rstd, groups


def _attention_fwd(proj, qw_row, kw_row, gmat, sinks, *, attn_w, kv_w, name):
    rows = proj.shape[0]
    per_kv = attn_w // HEAD_DIM // N_KV_HEADS

    def body(q_ref, kp_ref, kc_ref, vp_ref, vc_ref, glo_ref, ghi_ref, qw_ref, kw_ref, gm_ref, sink_ref, o_ref):
        kk = jnp.concatenate([kp_ref[...], kc_ref[...]], axis=0)
        vv = jnp.concatenate([vp_ref[...], vc_ref[...]], axis=0)
        gate = jnp.concatenate([glo_ref[...], ghi_ref[...]], axis=1)
        *_, groups = _attn_block(pl.program_id(0), q_ref[...], kk, vv, qw_ref[...], kw_ref[...], gm_ref[...],
                                 sink_ref, per_kv)
        attn = jnp.concatenate([grp[-1] for grp in groups], axis=1)
        o_ref[...] = (attn * _silu(gate)).astype(BF16)

    const = lambda a: pl.BlockSpec(a.shape, lambda i: (0, 0))
    return _pallas(
        body, name=name, out_shape=jax.ShapeDtypeStruct((rows, attn_w), BF16), grid=(rows // BLOCK,),
        in_specs=_attn_specs(attn_w, kv_w) + [const(qw_row), const(kw_row), const(gmat),
                                              pl.BlockSpec(memory_space=pltpu.SMEM)],
        out_specs=pl.BlockSpec((BLOCK, attn_w), lambda i: (i, 0)),
        compiler_params=_params(("parallel",), 40 * 1024 * 1024),
    )(proj, proj, proj, proj, proj, proj, proj, qw_row, kw_row, gmat, sinks)


def _attention_bwd(proj, d_ag, dproj, qw_row, kw_row, gmat, sinks, *, attn_w, kv_w, name, rider=None):
    rows = proj.shape[0]
    nb = rows // BLOCK
    per_kv = attn_w // HEAD_DIM // N_KV_HEADS
    gw = per_kv * HEAD_DIM
    keys = 2 * BLOCK
    scale = 1.0 / math.sqrt(HEAD_DIM)
    w_out = 2 * attn_w + 2 * kv_w

    def body(q_ref, kp_ref, kc_ref, vp_ref, vc_ref, glo_ref, ghi_ref, dag_ref, qw_ref, kw_ref, gm_ref, sink_ref, _,
             dp_ref, dkv_ref, gqw_ref, gkw_ref, gs_ref):
        i = pl.program_id(0)
        kk = jnp.concatenate([kp_ref[...], kc_ref[...]], axis=0)
        vv = jnp.concatenate([vp_ref[...], vc_ref[...]], axis=0)
        gate = jnp.concatenate([glo_ref[...], ghi_ref[...]], axis=1)
        d_ag_v = dag_ref[...]
        qw, kw, gmat_v = qw_ref[...], kw_ref[...], gm_ref[...]
        qn, q_rstd, kn, k_rstd, groups = _attn_block(i, q_ref[...], kk, vv, qw, kw, gmat_v, sink_ref, per_kv)
        lane = lax.broadcasted_iota(jnp.int32, (SUBLANES, LANES), 1)
        sub = lax.broadcasted_iota(jnp.int32, (SUBLANES, LANES), 0)
        gsink = jnp.zeros((SUBLANES, LANES), F32)
        dq_groups, dgate_groups, dk_heads, dv_heads = [], [], [], []
        for g, (kd, vd, qg, ps, p_sinks, pb, o) in enumerate(groups):
            cs = slice(g * gw, (g + 1) * gw)
            gate_g, d_ag_g = gate[:, cs], d_ag_v[:, cs]
            dgate_groups.append(d_ag_g * o * _dsilu(gate_g))
            do = (d_ag_g * _silu(gate_g)).astype(BF16)
            dp_all = _dot(do, vd, NT)
            dss = []
            for r in range(per_kv):
                p, dp = ps[r], dp_all[:, r * keys:(r + 1) * keys]
                delta = jnp.sum(p * dp, axis=-1, keepdims=True)
                dss.append(p * (dp - delta) * scale)
                gs_h = jnp.sum(-p_sinks[r] * delta, axis=0, keepdims=True)
                gsink = gsink + jnp.where((lane == g * per_kv + r) & (sub == 0), gs_h, 0.0)
            ds = jnp.concatenate(dss, axis=1).astype(BF16)
            dq_groups.append(_dot(ds, kd, NN))
            dk_heads.append(_fold_heads(_dot(ds, qg, TN), per_kv))
            dv_heads.append(_fold_heads(_dot(pb, do, TN), per_kv))
        dqh = jnp.concatenate(dq_groups, axis=1)
        gqw = jnp.sum(dqh * qn, axis=0, keepdims=True)
        dqn = dqh * qw
        dq = q_rstd * (dqn - qn * _head_mean(dqn * qn, gmat_v))
        dkh = _join_heads(dk_heads)
        gkw = jnp.sum(dkh * kn, axis=0, keepdims=True)
        dkn = dkh * kw
        dk = k_rstd * (dkn - kn * _head_mean(dkn * kn, gmat_v))
        dp_ref[:, 0:attn_w] = dq.astype(BF16)
        dp_ref[:, attn_w:attn_w + 2 * kv_w] = jnp.zeros((BLOCK, 2 * kv_w), BF16)
        dp_ref[:, attn_w + 2 * kv_w:w_out] = jnp.concatenate(dgate_groups, axis=1).astype(BF16)
        dkv_ref[0] = jnp.concatenate([dk, _join_heads(dv_heads)], axis=1)

        @pl.when(i == 0)
        def _():
            gqw_ref[...] = gqw
            gkw_ref[...] = gkw
            gs_ref[...] = gsink

        @pl.when(i > 0)
        def _():
            gqw_ref[...] += gqw
            gkw_ref[...] += gkw
            gs_ref[...] += gsink

    const = lambda a: pl.BlockSpec(a.shape, lambda i: (0, 0))
    res, landed = _call(
        body, [proj, proj, proj, proj, proj, proj, proj, d_ag, qw_row, kw_row, gmat, sinks, dproj], name=name,
        out_shape=[jax.ShapeDtypeStruct(dproj.shape, BF16),
                   jax.ShapeDtypeStruct((nb, 2 * BLOCK, 2 * kv_w), F32),
                   jax.ShapeDtypeStruct(qw_row.shape, F32), jax.ShapeDtypeStruct(kw_row.shape, F32),
                   jax.ShapeDtypeStruct((SUBLANES, LANES), F32)],
        grid=(nb,),
        in_specs=_attn_specs(attn_w, kv_w) + [pl.BlockSpec((BLOCK, attn_w), lambda i: (i, 0)), const(qw_row),
                                              const(kw_row), const(gmat), pl.BlockSpec(memory_space=pltpu.SMEM),
                                              _any_spec()],
        out_specs=[pl.BlockSpec((BLOCK, w_out), lambda i: (i, 0)),
                   pl.BlockSpec((1, 2 * BLOCK, 2 * kv_w), lambda i: (i, 0, 0)),
                   const(qw_row), const(kw_row), pl.BlockSpec((SUBLANES, LANES), lambda i: (0, 0))],
        aliases={12: 0}, semantics=("arbitrary",), vmem=48 * 1024 * 1024, rider=rider)
    return res if rider is None else (res, landed)


def _attention_dkv(dproj, dkv, *, attn_w, kv_w, name):
    rows = dproj.shape[0]
    nb = rows // BLOCK
    col = attn_w // (2 * kv_w)

    def body(cur_ref, nxt_ref, _, o_ref):
        i = pl.program_id(0)
        nxt = jnp.where(i < nb - 1, nxt_ref[0, 0:BLOCK, :], 0.0)
        o_ref[...] = (cur_ref[0, BLOCK:2 * BLOCK, :] + nxt).astype(BF16)

    blk = lambda f: pl.BlockSpec((1, 2 * BLOCK, 2 * kv_w), f)
    return _pallas(
        body, name=name, out_shape=jax.ShapeDtypeStruct(dproj.shape, BF16), grid=(nb,),
        in_specs=[blk(lambda i: (i, 0, 0)), blk(lambda i: (jnp.minimum(i + 1, nb - 1), 0, 0)), _any_spec()],
        out_specs=pl.BlockSpec((BLOCK, 2 * kv_w), lambda i: (i, col)),
        input_output_aliases={2: 0},
        compiler_params=_params(("parallel",)),
    )(dkv, dkv, dproj)


def _cmul(ar, ai, br, bi):
    return ar * br - ai * bi, ar * bi + ai * br


def _ssm_prep(a_re, a_im, log_dt_col, steps, name):
    assert steps & (steps - 1) == 0

    def body(are_ref, aim_ref, ldt_ref, abr_ref, abi_ref, cfr_ref, cfi_ref, apr_ref, api_ref):
        are, aim = are_ref[...], aim_ref[...]
        dt = jnp.exp(ldt_ref[...])
        mag = jnp.exp(dt * are)
        abr = mag * jnp.cos(dt * aim)
        abi = mag * jnp.sin(dt * aim)
        num_re, num_im = abr - 1.0, abi
        den = are * are + aim * aim
        abr_ref[...] = abr
        abi_ref[...] = abi
        cfr_ref[...] = (num_re * are + num_im * aim) / den
        cfi_ref[...] = (num_im * are - num_re * aim) / den
        pr, pi = abr, abi
        n = steps
        while n > 1:
            pr, pi = _cmul(pr, pi, pr, pi)
            n //= 2
        apr_ref[...] = pr
        api_ref[...] = pi

    shp = jax.ShapeDtypeStruct(a_re.shape, F32)
    return _pallas(body, name=name, out_shape=[shp] * 6)(a_re, a_im, log_dt_col)


def _ssm_param_bwd(a_re, a_im, log_dt_col, d_ab_re, d_ab_im, d_cf_re, d_cf_im, name):
    def body(are_ref, aim_ref, ldt_ref, gabr_ref, gabi_ref, gcfr_ref, gcfi_ref, dar_ref, dai_ref, dldt_ref):
        are, aim = are_ref[...], aim_ref[...]
        dt = jnp.exp(ldt_ref[...])
        mag = jnp.exp(dt * are)
        abr = mag * jnp.cos(dt * aim)
        abi = mag * jnp.sin(dt * aim)
        den = are * are + aim * aim
        cfr = ((abr - 1.0) * are + abi * aim) / den
        cfi = (abi * are - (abr - 1.0) * aim) / den
        gabr, gabi = jnp.sum(gabr_ref[...], axis=0), jnp.sum(gabi_ref[...], axis=0)
        gcfr, gcfi = jnp.sum(gcfr_ref[...], axis=0), jnp.sum(gcfi_ref[...], axis=0)
        inv_r, inv_i = are / den, -aim / den
        t_r, t_i = _cmul(inv_r, -inv_i, gcfr, gcfi)
        gabr, gabi = gabr + t_r, gabi + t_i
        q_r, q_i = _cmul(cfr, cfi, inv_r, inv_i)
        da_r, da_i = _cmul(-q_r, q_i, gcfr, gcfi)
        gz_r, gz_i = _cmul(abr, -abi, gabr, gabi)
        dar_ref[...] = da_r + dt * gz_r
        dai_ref[...] = da_i + dt * gz_i
        dldt_ref[...] = dt * jnp.sum(are * gz_r + aim * gz_i, axis=-1, keepdims=True)

    shp = jax.ShapeDtypeStruct(a_re.shape, F32)
    return _pallas(body, name=name, out_shape=[shp, shp, jax.ShapeDtypeStruct(log_dt_col.shape, F32)])(
        a_re, a_im, log_dt_col, d_ab_re, d_ab_im, d_cf_re, d_cf_im)


SCAN_LANES = 512
W_IN_GRAD_PARTS = 2


def _scan_segments(xr_ref, xi_ref, a_re, a_im, ap_re, ap_im, carry_re, carry_im, cm_re, cm_im, steps, reverse):
    n = xr_ref.shape[1]
    order = range(steps - 1, -1, -1) if reverse else range(steps)
    seg_order = range(SUBLANES - 1, -1, -1) if reverse else range(SUBLANES)
    for c0 in range(0, n, SCAN_LANES):
        ls = slice(c0, c0 + SCAN_LANES)
        ar = jnp.broadcast_to(a_re[:, ls], (SUBLANES, SCAN_LANES))
        ai = jnp.broadcast_to(a_im[:, ls], (SUBLANES, SCAN_LANES))

        def local(t, s, ar=ar, ai=ai, ls=ls):
            j = steps - 1 - t if reverse else t
            r0 = pl.multiple_of(j * SUBLANES, SUBLANES)
            sr, si = _cmul(ar, ai, s[0], s[1])
            sr = sr + xr_ref[pl.ds(r0, SUBLANES), ls]
            si = si + xi_ref[pl.ds(r0, SUBLANES), ls]
            xr_ref[pl.ds(r0, SUBLANES), ls] = sr
            xi_ref[pl.ds(r0, SUBLANES), ls] = si
            return sr, si

        zero = jnp.zeros((SUBLANES, SCAN_LANES), F32)
        end_r, end_i = lax.fori_loop(0, steps, local, (zero, zero))
        cr, ci = carry_re[:, ls], carry_im[:, ls]
        apr, api = ap_re[:, ls], ap_im[:, ls]
        for r in seg_order:
            cm_re[r:r + 1, ls] = cr
            cm_im[r:r + 1, ls] = ci
            tr, ti = _cmul(apr, api, cr, ci)
            cr, ci = end_r[r:r + 1, :] + tr, end_i[r:r + 1, :] + ti
        carry_re[:, ls] = cr
        carry_im[:, ls] = ci

        def fix(t, s, ar=ar, ai=ai, ls=ls):
            j = steps - 1 - t if reverse else t
            r0 = pl.multiple_of(j * SUBLANES, SUBLANES)
            sr, si = _cmul(ar, ai, s[0], s[1])
            xr_ref[pl.ds(r0, SUBLANES), ls] += sr
            xi_ref[pl.ds(r0, SUBLANES), ls] += si
            return sr, si

        lax.fori_loop(0, steps, fix, (cm_re[:, ls], cm_im[:, ls]))
    del order


SB_GROUPS = MXU_DIM // GROUP
SB_STATE = SB_GROUPS * STATE


def _ssm_rows(b_re, b_im, c_re, c_im):
    def rows(m):
        flat = m.reshape(-1, STATE).astype(F32)
        return jnp.concatenate([flat, flat], axis=1)
    return rows(b_re.transpose(0, 2, 1)), rows(b_im.transpose(0, 2, 1)), rows(c_re), rows(c_im)


def _from_ssm_rows(rows, transpose):
    g = rows[:, :STATE].reshape(-1, GROUP, STATE)
    return g.transpose(0, 2, 1) if transpose else g


def _own_group(shape):
    row_g = lax.broadcasted_iota(jnp.int32, shape, 0) // GROUP
    col_g = lax.broadcasted_iota(jnp.int32, shape, 1) // STATE
    return row_g == col_g


def _block_diagonal(rows):
    tiled = jnp.concatenate([rows] * (SB_STATE // LANES), axis=1)
    return jnp.where(_own_group(tiled.shape), tiled, 0.0).astype(BF16)


def _block_rows(acc):
    x = jnp.where(_own_group(acc.shape), acc, 0.0)
    while x.shape[1] > LANES:
        half = x.shape[1] // 2
        x = x[:, :half] + x[:, half:]
    return x + pltpu.roll(x, STATE, axis=1)


def _rows_to_segments(dst, srcs, steps, stage):
    for ref, off in srcs:
        for k in range(ref.shape[1] // LANES):
            stage[off // LANES + k] = ref[:, k * LANES:(k + 1) * LANES]
    for k in range(dst.shape[1] // LANES):
        for j in range(steps):
            dst[j * SUBLANES:(j + 1) * SUBLANES, k * LANES:(k + 1) * LANES] = (
                stage[k, pl.ds(j, SUBLANES, stride=steps), :])


def _segments_to_rows(dst, src, steps, stage):
    for k in range(src.shape[1] // LANES):
        for j in range(steps):
            stage[k, pl.ds(j, SUBLANES, stride=steps), :] = (
                src[j * SUBLANES:(j + 1) * SUBLANES, k * LANES:(k + 1) * LANES])
    for k in range(src.shape[1] // LANES):
        dst[:, k * LANES:(k + 1) * LANES] = stage[k]


def _u_specs(w, o_u, chunk, index):
    half = w // 2
    assert o_u % half == 0
    return [pl.BlockSpec((chunk, half), lambda c, k=k: (index(c), o_u // half + k)) for k in range(2)]


def _ssm_fwd(proj, o_u, bc_rows, rows_p, d_row, *, chunk, name):
    rows = proj.shape[0]
    w = d_row.shape[1]
    nc = rows // chunk
    steps = chunk // SUBLANES
    nsb = w // MXU_DIM
    n_state = nsb * SB_STATE

    def body(ulo_ref, uhi_ref, b2r_ref, b2i_ref, c2r_ref, c2i_ref, abr_ref, abi_ref, cfr_ref, cfi_ref, apr_ref,
             api_ref, d_ref, y_ref, str_ref, sti_ref, bre_ref, bim_ref, cre_ref, cim_ref, useg, yseg, stage, sr, si,
             carry_r, carry_i, cm_r, cm_i):
        @pl.when(pl.program_id(0) == 0)
        def _():
            for src, dst in ((b2r_ref, bre_ref), (b2i_ref, bim_ref), (c2r_ref, cre_ref), (c2i_ref, cim_ref)):
                for sb in range(nsb):
                    dst[sb] = _block_diagonal(src[sb * MXU_DIM:(sb + 1) * MXU_DIM, :])
            carry_r[...] = jnp.zeros_like(carry_r)
            carry_i[...] = jnp.zeros_like(carry_i)

        str_ref[0] = carry_r[...]
        sti_ref[0] = carry_i[...]
        _rows_to_segments(useg, [(ulo_ref, 0), (uhi_ref, w // 2)], steps, stage)
        for sb in range(nsb):
            us = slice(sb * MXU_DIM, (sb + 1) * MXU_DIM)
            ss = slice(sb * SB_STATE, (sb + 1) * SB_STATE)
            ub = useg[:, us].astype(BF16)
            bur = _dot(ub, bre_ref[sb], NN)
            bui = _dot(ub, bim_ref[sb], NN)
            xr, xi = _cmul(cfr_ref[:, ss], cfi_ref[:, ss], bur, bui)
            sr[:, ss] = xr
            si[:, ss] = xi
        _scan_segments(sr, si, abr_ref[...], abi_ref[...], apr_ref[...], api_ref[...],
                       carry_r, carry_i, cm_r, cm_i, steps, False)
        for sb in range(nsb):
            us = slice(sb * MXU_DIM, (sb + 1) * MXU_DIM)
            ss = slice(sb * SB_STATE, (sb + 1) * SB_STATE)
            y = _dot(sr[:, ss].astype(BF16), cre_ref[sb], NT) - _dot(si[:, ss].astype(BF16), cim_ref[sb], NT)
            yseg[:, us] = y + d_ref[:, us] * useg[:, us]
        _segments_to_rows(y_ref, yseg, steps, stage)

    const = lambda a: pl.BlockSpec(a.shape, lambda c: (0,) * a.ndim)
    row_n = pl.BlockSpec((1, n_state), lambda c: (0, 0))
    st = pl.BlockSpec((1, 1, n_state), lambda c: (c, 0, 0))
    held = [pltpu.VMEM((nsb, MXU_DIM, SB_STATE), BF16)] * 4
    vmem = 4 * _nbytes((nsb, MXU_DIM, SB_STATE), BF16) + 3 * _nbytes((chunk, n_state), F32)
    return _pallas(
        body, name=name,
        out_shape=[jax.ShapeDtypeStruct((rows, w), F32), jax.ShapeDtypeStruct((nc, 1, n_state), F32),
                   jax.ShapeDtypeStruct((nc, 1, n_state), F32)],
        grid=(nc,),
        in_specs=_u_specs(w, o_u, chunk, lambda c: c) + [const(b) for b in bc_rows]
        + [row_n] * 6 + [pl.BlockSpec((1, w), lambda c: (0, 0))],
        out_specs=[pl.BlockSpec((chunk, w), lambda c: (c, 0)), st, st],
        scratch_shapes=held + [pltpu.VMEM((chunk, w), F32), pltpu.VMEM((chunk, w), F32),
                               pltpu.VMEM((w // LANES, chunk, LANES), F32),
                               pltpu.VMEM((chunk, n_state), F32), pltpu.VMEM((chunk, n_state), F32),
                               pltpu.VMEM((1, n_state), F32), pltpu.VMEM((1, n_state), F32),
                               pltpu.VMEM((SUBLANES, n_state), F32), pltpu.VMEM((SUBLANES, n_state), F32)],
        compiler_params=_params(("arbitrary",), vmem),
    )(proj, proj, *bc_rows, *rows_p, d_row)


def _ssm_bwd(proj, o_u, y, dyg, st_re, st_im, bc_rows, rows_p, d_row, *, chunk, name, rider=None):
    rows = proj.shape[0]
    w = d_row.shape[1]
    nc = rows // chunk
    steps = chunk // SUBLANES
    nsb = w // MXU_DIM
    n_state = nsb * SB_STATE

    def body(ulo_ref, uhi_ref, y_ref, dyg_ref, str_ref, sti_ref, b2r_ref, b2i_ref, c2r_ref, c2i_ref,
             abr_ref, abi_ref, cfr_ref, cfi_ref, apr_ref, api_ref, d_ref,
             du_ref, gb2r_ref, gb2i_ref, gc2r_ref, gc2i_ref, gabr_ref, gabi_ref, gcfr_ref, gcfi_ref, dd_ref,
             bre_ref, bim_ref, cre_ref, cim_ref, dbre_ref, dbim_ref, dcre_ref, dcim_ref, useg, dyseg, dynat, stage,
             bur, bui, sr, si, lr, li, carry_r, carry_i, lam_r, lam_i, cm_r, cm_i, cl_r, cl_i):
        first = pl.program_id(0) == 0

        @pl.when(first)
        def _():
            for src, dst in ((b2r_ref, bre_ref), (b2i_ref, bim_ref), (c2r_ref, cre_ref), (c2i_ref, cim_ref)):
                for sb in range(nsb):
                    dst[sb] = _block_diagonal(src[sb * MXU_DIM:(sb + 1) * MXU_DIM, :])
            lam_r[...] = jnp.zeros_like(lam_r)
            lam_i[...] = jnp.zeros_like(lam_i)
            for ref in (dbre_ref, dbim_ref, dcre_ref, dcim_ref, gabr_ref, gabi_ref, gcfr_ref, gcfi_ref, dd_ref):
                ref[...] = jnp.zeros_like(ref)

        dynat[...] = dyg_ref[...] * _dgelu(y_ref[...])
        half = w // 2
        dd_ref[:, :half] += jnp.sum(dynat[:, :half] * ulo_ref[...], axis=0, keepdims=True)
        dd_ref[:, half:] += jnp.sum(dynat[:, half:] * uhi_ref[...], axis=0, keepdims=True)
        _rows_to_segments(useg, [(ulo_ref, 0), (uhi_ref, half)], steps, stage)
        _rows_to_segments(dyseg, [(dynat, 0)], steps, stage)
        dy = dyseg[...]
        dyb = dy.astype(BF16)
        ub = useg[...].astype(BF16)
        carry_r[...] = str_ref[0]
        carry_i[...] = sti_ref[0]
        for sb in range(nsb):
            us = slice(sb * MXU_DIM, (sb + 1) * MXU_DIM)
            ss = slice(sb * SB_STATE, (sb + 1) * SB_STATE)
            br = _dot(ub[:, us], bre_ref[sb], NN)
            bi = _dot(ub[:, us], bim_ref[sb], NN)
            bur[:, ss] = br
            bui[:, ss] = bi
            xr, xi = _cmul(cfr_ref[:, ss], cfi_ref[:, ss], br, bi)
            sr[:, ss] = xr
            si[:, ss] = xi
            lr[:, ss] = _dot(dyb[:, us], cre_ref[sb], NN)
            li[:, ss] = -_dot(dyb[:, us], cim_ref[sb], NN)
        abr, abi = abr_ref[...], abi_ref[...]
        apr, api = apr_ref[...], api_ref[...]
        _scan_segments(sr, si, abr, abi, apr, api, carry_r, carry_i, cm_r, cm_i, steps, False)
        for sb in range(nsb):
            us = slice(sb * MXU_DIM, (sb + 1) * MXU_DIM)
            ss = slice(sb * SB_STATE, (sb + 1) * SB_STATE)
            dcre_ref[sb] += _dot(dyb[:, us], sr[:, ss].astype(BF16), TN)
            dcim_ref[sb] -= _dot(dyb[:, us], si[:, ss].astype(BF16), TN)
        _scan_segments(lr, li, abr, -abi, apr, -api, lam_r, lam_i, cl_r, cl_i, steps, True)
        for c0 in range(0, n_state, SCAN_LANES):
            ls = slice(c0, c0 + SCAN_LANES)
            cfr = jnp.broadcast_to(cfr_ref[:, ls], (SUBLANES, SCAN_LANES))
            cfi = jnp.broadcast_to(cfi_ref[:, ls], (SUBLANES, SCAN_LANES))

            def step(j, acc, ls=ls, cfr=cfr, cfi=cfi):
                gar, gai, gcr, gci, pr, pi = acc
                r0 = pl.multiple_of(j * SUBLANES, SUBLANES)
                rws = pl.ds(r0, SUBLANES)
                l_r, l_i = lr[rws, ls], li[rws, ls]
                t_r, t_i = _cmul(pr, -pi, l_r, l_i)
                b_r, b_i = bur[rws, ls], bui[rws, ls]
                c_r, c_i = _cmul(b_r, -b_i, l_r, l_i)
                x_r, x_i = _cmul(cfr, -cfi, l_r, l_i)
                bur[rws, ls] = x_r
                bui[rws, ls] = x_i
                return gar + t_r, gai + t_i, gcr + c_r, gci + c_i, sr[rws, ls], si[rws, ls]

            zero = jnp.zeros((SUBLANES, SCAN_LANES), F32)
            gar, gai, gcr, gci, _, _ = lax.fori_loop(
                0, steps, step, (zero, zero, zero, zero, cm_r[:, ls], cm_i[:, ls]))
            gabr_ref[:, ls] += gar
            gabi_ref[:, ls] += gai
            gcfr_ref[:, ls] += gcr
            gcfi_ref[:, ls] += gci
        for sb in range(nsb):
            us = slice(sb * MXU_DIM, (sb + 1) * MXU_DIM)
            ss = slice(sb * SB_STATE, (sb + 1) * SB_STATE)
            xr, xi = bur[:, ss].astype(BF16), bui[:, ss].astype(BF16)
            du = _dot(xr, bre_ref[sb], NT) + _dot(xi, bim_ref[sb], NT)
            useg[:, us] = du + d_ref[:, us] * dy[:, us]
            dbre_ref[sb] += _dot(ub[:, us], xr, TN)
            dbim_ref[sb] += _dot(ub[:, us], xi, TN)
        _segments_to_rows(du_ref, useg, steps, stage)

        @pl.when(pl.program_id(0) == nc - 1)
        def _():
            for src, dst in ((dbre_ref, gb2r_ref), (dbim_ref, gb2i_ref), (dcre_ref, gc2r_ref), (dcim_ref, gc2i_ref)):
                for sb in range(nsb):
                    dst[sb * MXU_DIM:(sb + 1) * MXU_DIM, :] = _block_rows(src[sb])

    rev = lambda c: nc - 1 - c
    const = lambda a: pl.BlockSpec(a.shape, lambda c: (0,) * a.ndim)
    tile = pl.BlockSpec((chunk, w), lambda c: (rev(c), 0))
    row_n = pl.BlockSpec((1, n_state), lambda c: (0, 0))
    row_w = pl.BlockSpec((1, w), lambda c: (0, 0))
    st = pl.BlockSpec((1, 1, n_state), lambda c: (rev(c), 0, 0))
    acc8 = pl.BlockSpec((SUBLANES, n_state), lambda c: (0, 0))
    big = pltpu.VMEM((chunk, n_state), F32)
    small = pltpu.VMEM((chunk, w), F32)
    row = pltpu.VMEM((1, n_state), F32)
    eight = pltpu.VMEM((SUBLANES, n_state), F32)
    blk = (nsb, MXU_DIM, SB_STATE)
    held = [pltpu.VMEM(blk, BF16)] * 4 + [pltpu.VMEM(blk, F32)] * 4
    vmem = (4 * (_nbytes(blk, BF16) + _nbytes(blk, F32)) + 7 * _nbytes((chunk, n_state), F32)
            + 20 * _nbytes((chunk, w), F32) + 16 * _nbytes(bc_rows[0].shape, F32))
    res, landed = _call(
        body, [proj, proj, y, dyg, st_re, st_im, *bc_rows, *rows_p, d_row], name=name,
        out_shape=[jax.ShapeDtypeStruct((rows, w), F32)] + [jax.ShapeDtypeStruct(b.shape, F32) for b in bc_rows]
        + [jax.ShapeDtypeStruct((SUBLANES, n_state), F32)] * 4 + [jax.ShapeDtypeStruct((1, w), F32)],
        grid=(nc,),
        in_specs=_u_specs(w, o_u, chunk, rev) + [tile, tile, st, st] + [const(b) for b in bc_rows]
        + [row_n] * 6 + [row_w],
        out_specs=[tile] + [const(b) for b in bc_rows] + [acc8] * 4 + [row_w],
        scratch_shapes=held + [small] * 3 + [pltpu.VMEM((w // LANES, chunk, LANES), F32)] + [big] * 6 + [row] * 4
        + [eight] * 4,
        semantics=("arbitrary",), vmem=vmem, rider=rider)
    return res if rider is None else (res, landed)


def _loss_grad(x, mm, target, name):
    rows, d = x.shape

    def fn(xv, mv, tv):
        err = xv + mv - tv
        g = err * (1.0 / d)
        return g, g, 0.5 * err * g

    return _ew(fn, name=name, rows=rows, width=d, tiles=[(x, 0), (mm, 0), (target, 0)],
               outs=[(F32, d, 0), (BF16, d, 0)], accs=1)


def _pair_sum(grad, recv, name):
    r4, cdim = recv.shape
    r = r4 // N_CHIPS
    tr = _tile(r, 544, 16)
    g4 = grad.reshape(N_CHIPS, 2, r, cdim)
    r3 = recv.reshape(N_CHIPS, r, cdim)
    core = jnp.reshape(lax.axis_index("c"), (1,)).astype(jnp.int32)

    def body(c_ref, g_ref, r_ref, o_ref):
        o_ref[...] = (g_ref[0] + r_ref[...]).astype(BF16)

    out = _pallas(
        body, name=name, out_shape=jax.ShapeDtypeStruct((N_CHIPS, r, cdim), BF16),
        grid_spec=pltpu.PrefetchScalarGridSpec(
            num_scalar_prefetch=1, grid=(N_CHIPS, r // tr),
            in_specs=[pl.BlockSpec((1, 1, tr, cdim), lambda j, i, c: (j, c[0], i, 0)),
                      pl.BlockSpec((1, tr, cdim), lambda j, i, c: (j, i, 0))],
            out_specs=pl.BlockSpec((1, tr, cdim), lambda j, i, c: (j, i, 0))),
        compiler_params=_params(("parallel", "parallel"), 6 * _nbytes((tr, cdim), F32)),
    )(core, g4, r3)
    return out.reshape(r4, cdim)


def _chip_sum(recv, name):
    r4, cdim = recv.shape
    r = r4 // N_CHIPS
    tr = _tile(r, 544, 16)
    r3 = recv.reshape(N_CHIPS, r, cdim)

    def body(r_ref, o_ref):
        acc = r_ref[0].astype(F32)
        for j in range(1, N_CHIPS):
            acc = acc + r_ref[j].astype(F32)
        o_ref[...] = acc

    return _pallas(
        body, name=name, out_shape=jax.ShapeDtypeStruct((r, cdim), F32), grid=(r // tr,),
        in_specs=[pl.BlockSpec((N_CHIPS, tr, cdim), lambda i: (0, i, 0))],
        out_specs=pl.BlockSpec((tr, cdim), lambda i: (i, 0)),
        compiler_params=_params(("parallel",), 8 * _nbytes((tr, cdim), F32)),
    )(r3)


def _adamw_math(w, g, m, v):
    m = ADAM_B1 * m + (1.0 - ADAM_B1) * g
    v = ADAM_B2 * v + (1.0 - ADAM_B2) * (g * g)
    m_hat = m / (1.0 - ADAM_B1 ** ADAM_STEP)
    v_hat = v / (1.0 - ADAM_B2 ** ADAM_STEP)
    delta = -ADAM_LR * (m_hat / (jnp.sqrt(v_hat) + ADAM_EPS) + ADAM_WD * w)
    return delta, m, v


def _adamw(w, g, m, v, name):
    rows, cols = w.shape
    tr = _tile(rows, 256, SUBLANES)

    def body(w_ref, g_ref, m_ref, v_ref, d_ref, nm_ref, nv_ref):
        d, nm, nv = _adamw_math(w_ref[...], g_ref[...], m_ref[...], v_ref[...])
        d_ref[...] = d
        nm_ref[...] = nm
        nv_ref[...] = nv

    spec = pl.BlockSpec((tr, cols), lambda i: (i, 0))
    shp = jax.ShapeDtypeStruct((rows, cols), F32)
    return _pallas(
        body, name=name, out_shape=[shp] * 3, grid=(rows // tr,), in_specs=[spec] * 4, out_specs=[spec] * 3,
        compiler_params=_params(("parallel",)),
    )(w, g, m, v)


def _adamw_small(w, parts, m, v, name):
    rows, cols = w.shape
    p3 = parts.reshape(N_DEV, rows, cols)

    def body(w_ref, p_ref, m_ref, v_ref, g_ref, d_ref, nm_ref, nv_ref):
        g = p_ref[0]
        for k in range(1, N_DEV):
            g = g + p_ref[k]
        d, nm, nv = _adamw_math(w_ref[...], g, m_ref[...], v_ref[...])
        g_ref[...] = g
        d_ref[...] = d
        nm_ref[...] = nm
        nv_ref[...] = nv

    shp = jax.ShapeDtypeStruct((rows, cols), F32)
    return _pallas(body, name=name, out_shape=[shp] * 4)(w, p3, m, v)


SMALL = ("norm_w", "q_norm_w", "k_norm_w", "sinks", "A_re", "A_im", "log_dt", "B_re", "B_im", "C_re", "C_im",
         "D_skip", "b_glu")
LARGE = ("w_in", "w_attn_proj", "w_glu", "w_ssm_proj", "w_out")
ORDER = ("norm_w", "w_in", "q_norm_w", "k_norm_w", "sinks", "w_attn_proj", "A_re", "A_im", "log_dt", "B_re", "B_im",
         "C_re", "C_im", "D_skip", "w_glu", "b_glu", "w_ssm_proj", "w_out")


SMALL_REST = ("loss",) + SMALL[1:]


def _pack(named, keys):
    flat = jnp.concatenate([named[k].reshape(-1).astype(F32) for k in keys])
    n = flat.shape[0]
    rows = -(-n // (LANES * SUBLANES)) * SUBLANES
    return jnp.pad(flat, (0, rows * LANES - n)).reshape(rows, LANES)


def _unpack(packed, like, keys):
    flat = packed.reshape(-1)
    out, o = {}, 0
    for k in keys:
        n = like[k].size
        out[k] = flat[o:o + n].reshape(like[k].shape)
        o += n
    return out


def _step(xs, target, p, shards):
    s_in, s_ap, s_glu, s_sp, s_o = shards
    (w_in_t,) = _exchange(_all_gather([s_in]), "gather_w_in")
    seq, d = xs.shape
    attn_w = (d // 128) * HEAD_DIM
    n_q = attn_w // HEAD_DIM
    kv_w = N_KV_HEADS * HEAD_DIM
    ssm_w = d // 2
    n_groups = ssm_w // GROUP
    n_state = n_groups * STATE
    in_w = w_in_t.shape[0]
    assert in_w == 2 * attn_w + 2 * kv_w + 2 * ssm_w + 2 * d
    o_u = 2 * attn_w + 2 * kv_w
    o_z = o_u + ssm_w
    o_ga = o_z + ssm_w
    chunk = min(BLOCK, seq)
    cw = d // 4

    norm_row = p["norm_w"].reshape(1, d)
    h = _rmsnorm_fwd(xs, norm_row, "rmsnorm_fwd")
    proj, (w_ap_t, w_glu_t, w_sp_t, w_o) = _matmul(h, w_in_t, mode="nt", name="in_proj", tn=512,
                                                   rider=_all_gather([s_ap, s_glu, s_sp, s_o]))
    qw_row = jnp.tile(p["q_norm_w"], n_q).reshape(1, attn_w)
    kw_row = jnp.tile(p["k_norm_w"], N_KV_HEADS).reshape(1, kv_w)
    gmat = _head_mean_matrix()
    ag = _attention_fwd(proj, qw_row, kw_row, gmat, p["sinks"], attn_w=attn_w, kv_w=kv_w, name="attention_fwd")

    log_dt_col = p["log_dt"].reshape(n_groups, 1)
    prep = _ssm_prep(p["A_re"], p["A_im"], log_dt_col, chunk // SUBLANES, "ssm_prep")
    rows_p = [v.reshape(1, n_state) for v in prep]
    bc_rows = _ssm_rows(p["B_re"], p["B_im"], p["C_re"], p["C_im"])
    d_row = p["D_skip"].reshape(1, ssm_w)
    y_ssm, st_re, st_im = _ssm_fwd(proj, o_u, bc_rows, rows_p, d_row, chunk=chunk, name="ssm_fwd")
    (yg,) = _ew(_gelu, name="gelu", rows=seq, width=ssm_w, tiles=[(y_ssm, 0)], outs=[(BF16, ssm_w, 0)], cw=cw)
    glu = _matmul(yg, w_glu_t, mode="nt", name="glu_proj", bias=p["b_glu"].reshape(1, 2 * ssm_w))
    (ts,) = _ew(lambda ga, gb, z: ga * _sigmoid(gb) * _silu(z), name="glu_gate", rows=seq, width=ssm_w,
                tiles=[(glu, 0), (glu, ssm_w), (proj, o_z)], outs=[(BF16, ssm_w, 0)], cw=cw)
    yy = _matmul(ag, w_ap_t, mode="nt", name="attn_proj", out_cols=(2 * d, 0))
    yy = _matmul(ts, w_sp_t, mode="nt", name="ssm_proj", out_cols=(2 * d, d), into=yy)
    (merged,) = _ew(lambda ya, ys, ga, gs: _sigmoid(ga) * ya + _sigmoid(gs) * ys, name="merge", rows=seq, width=d,
                    tiles=[(yy, 0), (yy, d), (proj, o_ga), (proj, o_ga + d)], outs=[(BF16, d, 0)], cw=cw)
    mm = _matmul(merged, w_o, mode="nn", name="out_proj")
    dout, dout_b, loss_cols = _loss_grad(xs, mm, target, "loss_grad")
    loss_local = jnp.sum(loss_cols)

    g_w_o = _matmul(merged, dout_b, mode="tn", name="grad_w_out", tk=1024)
    dmerged, (sib_o,) = _matmul(dout_b, w_o, mode="nt", name="d_merged", rider=_sibling_exchange([g_w_o]))
    pair_o = _pair_sum(g_w_o, sib_o, "pair_sum_w_out")

    def merge_bwd(dm, y, g):
        s = _sigmoid(g)
        return dm * s, dm * y * s * (1.0 - s)

    dyy, dproj = _ew(merge_bwd, name="merge_bwd", rows=seq, width=2 * d,
                     tiles=[(dmerged, 0, d), (yy, 0), (proj, o_ga)],
                     outs=[(BF16, 2 * d, 0), (BF16, in_w, o_ga)], cw=cw)
    dy_a, dy_s = Cols(dyy, 0, d), Cols(dyy, d, d)
    g_w_ap_t = _matmul(dy_a, ag, mode="tn", name="grad_w_attn_proj", tk=1024)
    g_w_sp_t = _matmul(dy_s, ts, mode="tn", name="grad_w_ssm_proj", tk=1024)
    d_ag = _matmul(dy_a, w_ap_t, mode="nn", name="d_attn_gated")
    d_ts = _matmul(dy_s, w_sp_t, mode="nn", name="d_ssm_gated")

    (dproj, dkv, g_qw, g_kw, g_sinks), (chips_o, sib_ap, sib_sp) = _attention_bwd(
        proj, d_ag, dproj, qw_row, kw_row, gmat, p["sinks"], attn_w=attn_w, kv_w=kv_w, name="attention_bwd",
        rider=_join(_chip_exchange([pair_o]), _sibling_exchange([g_w_ap_t, g_w_sp_t])))
    pair_ap = _pair_sum(g_w_ap_t, sib_ap, "pair_sum_w_attn_proj")
    pair_sp = _pair_sum(g_w_sp_t, sib_sp, "pair_sum_w_ssm_proj")
    dproj = _attention_dkv(dproj, dkv, attn_w=attn_w, kv_w=kv_w, name="attention_dkv")

    n_half = ssm_w // _tile(2 * ssm_w, cw)

    def glu_bwd(j, dt, ga, gb, z):
        sb, sz = _sigmoid(gb), _silu(z)
        dg = jnp.where(j < n_half, dt * sb * sz, dt * ga * sb * (1.0 - sb) * sz)
        return dg, dg

    glu_ops = [(d_ts, 0, ssm_w), (glu, 0, ssm_w), (glu, ssm_w, ssm_w), (proj, o_z, ssm_w)]
    dglu, g_bglu = _ew(glu_bwd, name="glu_bwd", rows=seq, width=2 * ssm_w, tiles=glu_ops,
                       outs=[(BF16, 2 * ssm_w, 0)], accs=1, cw=cw, with_col=True)
    (dproj,) = _ew(lambda dt, ga, gb, z: dt * ga * _sigmoid(gb) * _dsilu(z), name="glu_bwd_z", rows=seq,
                   width=ssm_w, tiles=glu_ops, outs=[(BF16, in_w, o_z)], into=[dproj], cw=cw)
    g_w_glu_t = _matmul(dglu, yg, mode="tn", name="grad_w_glu", tk=1024)
    d_yg = _matmul(dglu, w_glu_t, mode="nn", name="d_gelu")
    ((du, db_re, db_im, dc_re, dc_im, gabr, gabi, gcfr, gcfi, g_d), (chips_ap, chips_sp, sib_glu)) = _ssm_bwd(
        proj, o_u, y_ssm, d_yg, st_re, st_im, bc_rows, rows_p, d_row, chunk=chunk, name="ssm_bwd",
        rider=_join(_chip_exchange([pair_ap, pair_sp]), _sibling_exchange([g_w_glu_t])))
    pair_glu = _pair_sum(g_w_glu_t, sib_glu, "pair_sum_w_glu")
    (dproj,) = _ew(lambda v: v, name="du_store", rows=seq, width=ssm_w, tiles=[(du, 0)],
                   outs=[(BF16, in_w, o_u)], into=[dproj], cw=cw)
    g_a_re, g_a_im, g_log_dt = _ssm_param_bwd(
        p["A_re"], p["A_im"], log_dt_col, *[g.reshape(SUBLANES, n_groups, STATE) for g in (gabr, gabi, gcfr, gcfi)],
        "ssm_param_bwd")
    small_grads = dict(
        loss=loss_local, q_norm_w=g_qw.reshape(n_q, HEAD_DIM).sum(0), k_norm_w=g_kw.reshape(N_KV_HEADS, HEAD_DIM).sum(0),
        sinks=g_sinks[0, :n_q], A_re=g_a_re, A_im=g_a_im, log_dt=g_log_dt.reshape(n_groups),
        B_re=_from_ssm_rows(db_re, True), B_im=_from_ssm_rows(db_im, True),
        C_re=_from_ssm_rows(dc_re, False), C_im=_from_ssm_rows(dc_im, False),
        D_skip=g_d.reshape(n_groups, GROUP), b_glu=g_bglu.reshape(2 * ssm_w))

    n_parts = W_IN_GRAD_PARTS
    wq = d // n_parts
    g_parts, pair_parts, chip_parts = [], [], []
    extra = [_chip_exchange([pair_glu]), _all_gather([_pack(small_grads, SMALL_REST)])]
    chips_glu = small_parts = dh = grad_x = g_norm = None
    for step in range(n_parts + 2):
        riders = list(extra) if step == 0 else []
        if 0 <= step - 2 < n_parts:
            riders.append(_chip_exchange([pair_parts[step - 2]]))
        if 0 <= step - 1 < n_parts:
            riders.append(_sibling_exchange([g_parts[step - 1]]))
        rider = _join(*riders) if riders else None
        if step < n_parts:
            res = _matmul(dproj, Cols(h, step * wq, wq), mode="tn", name="grad_w_in_%d" % step, tk=1024, rider=rider)
            out, landed = res if rider is not None else (res, [])
            g_parts.append(out)
        elif step == n_parts:
            dh, landed = _matmul(dproj, w_in_t, mode="nn", name="d_normed", tk=2176, rider=rider)
        else:
            (grad_x, g_norm), landed = _rmsnorm_bwd(xs, norm_row, dh, dout, "rmsnorm_bwd", rider=rider)
        landed = list(landed)
        if step == 0:
            chips_glu, small_parts = landed[:2]
            landed = landed[2:]
        if 0 <= step - 2 < n_parts:
            chip_parts.append(landed.pop(0))
        if 0 <= step - 1 < n_parts:
            pair_parts.append(_pair_sum(g_parts[step - 1], landed.pop(0), "pair_sum_w_in_%d" % (step - 1)))
    (norm_parts,) = _exchange(_all_gather([_pack(dict(norm_w=g_norm), ("norm_w",))]), "gather_norm_grad")
    g_in = jnp.concatenate([_chip_sum(c, "chip_sum_w_in_%d" % q) for q, c in enumerate(chip_parts)], axis=1)
    summed = [g_in] + [_chip_sum(c, "chip_sum_" + k)
                       for k, c in zip(LARGE[1:], (chips_ap, chips_glu, chips_sp, chips_o))]
    return grad_x, summed, small_parts, norm_parts


def kernel(x, norm_w, w_in, q_norm_w, k_norm_w, sinks, w_attn_proj, A_re, A_im, log_dt, B_re, B_im, C_re, C_im, D_skip, w_glu, b_glu, w_ssm_proj, w_out, loss_target, m_norm_w, m_w_in, m_q_norm_w, m_k_norm_w, m_sinks, m_w_attn_proj, m_A_re, m_A_im, m_log_dt, m_B_re, m_B_im, m_C_re, m_C_im, m_D_skip, m_w_glu, m_b_glu, m_w_ssm_proj, m_w_out, v_norm_w, v_w_in, v_q_norm_w, v_k_norm_w, v_sinks, v_w_attn_proj, v_A_re, v_A_im, v_log_dt, v_B_re, v_B_im, v_C_re, v_C_im, v_D_skip, v_w_glu, v_b_glu, v_w_ssm_proj, v_w_out):
    weights = dict(norm_w=norm_w, w_in=w_in, q_norm_w=q_norm_w, k_norm_w=k_norm_w, sinks=sinks,
                   w_attn_proj=w_attn_proj, A_re=A_re, A_im=A_im, log_dt=log_dt, B_re=B_re, B_im=B_im, C_re=C_re,
                   C_im=C_im, D_skip=D_skip, w_glu=w_glu, b_glu=b_glu, w_ssm_proj=w_ssm_proj, w_out=w_out)
    m_in = dict(norm_w=m_norm_w, w_in=m_w_in, q_norm_w=m_q_norm_w, k_norm_w=m_k_norm_w, sinks=m_sinks,
                w_attn_proj=m_w_attn_proj, A_re=m_A_re, A_im=m_A_im, log_dt=m_log_dt, B_re=m_B_re, B_im=m_B_im,
                C_re=m_C_re, C_im=m_C_im, D_skip=m_D_skip, w_glu=m_w_glu, b_glu=m_b_glu, w_ssm_proj=m_w_ssm_proj,
                w_out=m_w_out)
    v_in = dict(norm_w=v_norm_w, w_in=v_w_in, q_norm_w=v_q_norm_w, k_norm_w=v_k_norm_w, sinks=v_sinks,
                w_attn_proj=v_w_attn_proj, A_re=v_A_re, A_im=v_A_im, log_dt=v_log_dt, B_re=v_B_re, B_im=v_B_im,
                C_re=v_C_re, C_im=v_C_im, D_skip=v_D_skip, w_glu=v_w_glu, b_glu=v_b_glu, w_ssm_proj=v_w_ssm_proj,
                w_out=v_w_out)

    _, seq, d = x.shape
    column_sharded = LARGE[:4]
    as_rows = lambda k, a: a.T if k in column_sharded else a
    shards = [as_rows(k, weights[k]).astype(BF16) for k in LARGE]
    small = {k: weights[k] for k in SMALL}
    grad_x, summed, small_parts, norm_parts = _step(x.reshape(seq, d), loss_target.reshape(seq, d), small, shards)

    grads, delta, new_m, new_v = {}, {}, {}, {}
    for k, g in zip(LARGE, summed):
        if k == "w_in":
            upd = _adamw(weights[k].T, g, m_in[k].T, v_in[k].T, "adamw_" + k)
            grads[k], delta[k], new_m[k], new_v[k] = [a.T for a in (g, *upd)]
        else:
            grads[k] = as_rows(k, g)
            delta[k], new_m[k], new_v[k] = _adamw(weights[k], grads[k], m_in[k], v_in[k], "adamw_" + k)

    zero = jnp.zeros((), F32)
    for keys, parts in ((SMALL_REST, small_parts), (("norm_w",), norm_parts)):
        like = dict(small, loss=zero)
        packs = [_pack(dict(src, loss=zero), keys) for src in (weights, m_in, v_in)]
        res = _adamw_small(packs[0], parts, packs[1], packs[2], "adamw_small_%d" % len(keys))
        for dst, r in zip((grads, delta, new_m, new_v), res):
            dst.update(_unpack(r, like, keys))
    loss = grads["loss"]

    return (loss, grad_x.reshape(x.shape), *[grads[k] for k in ORDER], *[delta[k] for k in ORDER],
            *[new_m[k] for k in ORDER], *[new_v[k] for k in ORDER])
```

```python
import math
from typing import Callable, NamedTuple

import jax
import jax.numpy as jnp
from jax import lax
from jax.experimental import pallas as pl
from jax.experimental.pallas import tpu as pltpu

F32 = jnp.float32
BF16 = jnp.bfloat16
MESH = pl.DeviceIdType.MESH

HEAD_DIM = 64
N_KV_HEADS = 4
GROUP = 16
STATE = 64
BLOCK = 128
NORM_EPS = 1e-6
N_DEV = 8
N_CHIPS = 4
LANES = 128
SUBLANES = 8
MXU_DIM = 256
VMEM_BYTES = 64 * 1024 * 1024
VMEM_CAP = VMEM_BYTES - 8 * 1024 * 1024

ADAM_LR = 0.001
ADAM_B1 = 0.9
ADAM_B2 = 0.999
ADAM_EPS = 1e-08
ADAM_WD = 0.01
ADAM_STEP = 10

GELU_C = math.sqrt(2.0 / math.pi)
GELU_K = 0.044715


def _tile(dim, pref, mult=LANES):
    if dim <= pref:
        return dim
    best = None
    for d in range(mult, pref + 1, mult):
        if dim % d == 0:
            best = d
    assert best is not None, (dim, pref, mult)
    return best


def _params(semantics=None, vmem=None):
    kw = {}
    if semantics is not None:
        kw["dimension_semantics"] = semantics
    if vmem is not None:
        kw["vmem_limit_bytes"] = int(min(VMEM_CAP, max(vmem, 32 * 1024 * 1024)))
    return pltpu.CompilerParams(**kw)


def _nbytes(shape, dtype):
    return math.prod(shape) * jnp.dtype(dtype).itemsize


def _sigmoid(x):
    return 1.0 / (1.0 + jnp.exp(-x))


def _silu(x):
    return x * _sigmoid(x)


def _dsilu(x):
    s = _sigmoid(x)
    return s * (1.0 + x * (1.0 - s))


def _gelu(x):
    return 0.5 * x * (1.0 + jnp.tanh(GELU_C * (x + GELU_K * x * x * x)))


def _dgelu(x):
    t = jnp.tanh(GELU_C * (x + GELU_K * x * x * x))
    return 0.5 * (1.0 + t) + 0.5 * x * (1.0 - t * t) * GELU_C * (1.0 + 3.0 * GELU_K * x * x)


def _dot(a, b, dims):
    return lax.dot_general(a, b, (dims, ((), ())), preferred_element_type=F32)


NN = ((1,), (0,))
NT = ((1,), (1,))
TN = ((0,), (0,))


def _any_spec():
    return pl.BlockSpec(memory_space=pl.ANY)


def _pallas(body, **kw):
    pin = lambda s: pltpu.HBM(s.shape, s.dtype) if isinstance(s, jax.ShapeDtypeStruct) else s
    out_shape = kw.pop("out_shape")
    out_shape = [pin(s) for s in out_shape] if isinstance(out_shape, (list, tuple)) else pin(out_shape)
    call = pl.pallas_call(body, out_shape=out_shape, **kw)

    def run(*operands):
        pinned = [pltpu.with_memory_space_constraint(o, pltpu.HBM) if jnp.issubdtype(o.dtype, jnp.floating) else o
                  for o in operands]
        return call(*pinned)

    return run


class Rider(NamedTuple):
    operands: tuple
    out_shapes: tuple
    sems: tuple
    start: Callable
    finish: Callable


def _all_gather(shards):
    n = len(shards)

    def copies(ins, outs, sems):
        send_sems, recv_sems, local_sems = sems
        x, y, c = lax.axis_index("x"), lax.axis_index("y"), lax.axis_index("c")
        me, sibling = (x, y, c), (x, y, 1 - c)
        chips = [(1 - x, y), (x, 1 - y), (1 - x, 1 - y)]

        def rows(k, px, py, pc):
            r = shards[k].shape[0]
            return outs[k].at[pl.ds((4 * px + 2 * py + pc) * r, r), :]

        def copy(k, s, block, to, src=None):
            return pltpu.make_async_remote_copy(
                src_ref=rows(k, *block) if src is None else src, dst_ref=rows(k, *block),
                send_sem=send_sems.at[7 * k + s], recv_sem=recv_sems.at[7 * k + s],
                device_id=to, device_id_type=MESH)

        mine = [pltpu.make_async_copy(ins[k], rows(k, *me), local_sems.at[k]) for k in range(n)]
        first = []
        for k in range(n):
            first.append(copy(k, 0, me, sibling, src=ins[k]))
            first += [copy(k, 1 + j, me, (*chip, c), src=ins[k]) for j, chip in enumerate(chips)]
        return me, sibling, chips, c, copy, mine, first

    def start(ins, outs, sems):
        *_, mine, first = copies(ins, outs, sems)
        for cp in mine + first:
            cp.start()

    def finish(ins, outs, sems):
        me, sibling, chips, c, copy, mine, first = copies(ins, outs, sems)
        passed = []
        for j, chip in enumerate(chips):
            for k in range(n):
                copy(k, 1 + j, (*chip, c), me).wait_recv()
                fwd = copy(k, 4 + j, (*chip, c), sibling)
                fwd.start()
                passed.append(fwd)
        for k in range(n):
            copy(k, 0, sibling, me).wait_recv()
            for j, chip in enumerate(chips):
                copy(k, 4 + j, (*chip, 1 - c), me).wait_recv()
        for cp in first + passed:
            cp.wait_send()
        for cp in mine:
            cp.wait()

    return Rider(
        tuple(shards),
        tuple(jax.ShapeDtypeStruct((N_DEV * s.shape[0], s.shape[1]), s.dtype) for s in shards),
        (pltpu.SemaphoreType.DMA((7 * n,)), pltpu.SemaphoreType.DMA((7 * n,)), pltpu.SemaphoreType.DMA((n,))),
        start, finish)


def _sibling_exchange(grads):
    n = len(grads)

    def copies(ins, outs, sems):
        send_sems, recv_sems = sems
        x, y, c = lax.axis_index("x"), lax.axis_index("y"), lax.axis_index("c")
        out = []
        for k in range(n):
            r = grads[k].shape[0] // N_DEV
            for j in range(N_CHIPS):
                out.append(pltpu.make_async_remote_copy(
                    src_ref=ins[k].at[pl.ds((2 * j + 1 - c) * r, r), :],
                    dst_ref=outs[k].at[pl.ds(j * r, r), :],
                    send_sem=send_sems.at[N_CHIPS * k + j], recv_sem=recv_sems.at[N_CHIPS * k + j],
                    device_id=(x, y, 1 - c), device_id_type=MESH))
        return out

    def start(ins, outs, sems):
        for cp in copies(ins, outs, sems):
            cp.start()

    def finish(ins, outs, sems):
        for cp in copies(ins, outs, sems):
            cp.wait()

    return Rider(
        tuple(grads), tuple(jax.ShapeDtypeStruct((g.shape[0] // 2, g.shape[1]), g.dtype) for g in grads),
        (pltpu.SemaphoreType.DMA((N_CHIPS * n,)), pltpu.SemaphoreType.DMA((N_CHIPS * n,))), start, finish)


def _chip_exchange(parts):
    n = len(parts)

    def copies(ins, outs, sems):
        send_sems, recv_sems, local_sems = sems
        x, y, c = lax.axis_index("x"), lax.axis_index("y"), lax.axis_index("c")
        my_chip = 2 * x + y
        chips = [(1 - x, y), (x, 1 - y), (1 - x, 1 - y)]
        local, sent = [], []
        for k in range(n):
            r = parts[k].shape[0] // N_CHIPS
            mine = pl.ds(my_chip * r, r)
            local.append(pltpu.make_async_copy(ins[k].at[mine, :], outs[k].at[mine, :], local_sems.at[k]))
            for s, (px, py) in enumerate(chips):
                sent.append(pltpu.make_async_remote_copy(
                    src_ref=ins[k].at[pl.ds((2 * px + py) * r, r), :], dst_ref=outs[k].at[mine, :],
                    send_sem=send_sems.at[3 * k + s], recv_sem=recv_sems.at[3 * k + s],
                    device_id=(px, py, c), device_id_type=MESH))
        return local, sent

    def start(ins, outs, sems):
        local, sent = copies(ins, outs, sems)
        for cp in local + sent:
            cp.start()

    def finish(ins, outs, sems):
        local, sent = copies(ins, outs, sems)
        for cp in sent + local:
            cp.wait()

    return Rider(
        tuple(parts), tuple(jax.ShapeDtypeStruct(p.shape, p.dtype) for p in parts),
        (pltpu.SemaphoreType.DMA((3 * n,)), pltpu.SemaphoreType.DMA((3 * n,)), pltpu.SemaphoreType.DMA((n,))),
        start, finish)


def _join(*riders):
    cuts_in, cuts_out, cuts_sem = [0], [0], [0]
    for r in riders:
        cuts_in.append(cuts_in[-1] + len(r.operands))
        cuts_out.append(cuts_out[-1] + len(r.out_shapes))
        cuts_sem.append(cuts_sem[-1] + len(r.sems))

    def each(which):
        def run(ins, outs, sems):
            for i, r in enumerate(riders):
                getattr(r, which)(ins[cuts_in[i]:cuts_in[i + 1]], outs[cuts_out[i]:cuts_out[i + 1]],
                                  sems[cuts_sem[i]:cuts_sem[i + 1]])
        return run

    return Rider(sum((r.operands for r in riders), ()), sum((r.out_shapes for r in riders), ()),
                 sum((r.sems for r in riders), ()), each("start"), each("finish"))


def _call(body, operands, *, name, out_shape, grid, in_specs, out_specs, scratch_shapes=(), aliases=None,
          semantics=None, vmem=None, rider=None):
    operands, out_shape, scratch_shapes = list(operands), list(out_shape), list(scratch_shapes)
    in_specs, out_specs = list(in_specs), list(out_specs)
    if rider is None:
        res = _pallas(
            body, name=name, out_shape=out_shape, grid=grid, in_specs=in_specs, out_specs=out_specs,
            scratch_shapes=scratch_shapes, input_output_aliases=aliases or {},
            compiler_params=_params(semantics, vmem))(*operands)
        return list(res), []
    n_in, n_out, n_scr = len(operands), len(out_shape), len(scratch_shapes)
    ri, ro = len(rider.operands), len(rider.out_shapes)

    def carried(*refs):
        a, b = n_in, n_in + ri
        c, d = b + n_out, b + n_out + ro
        e = d + n_scr
        ids = [pl.program_id(k) for k in range(len(grid))]
        first = ids[0] == 0
        last = ids[0] == grid[0] - 1
        for k in range(1, len(grid)):
            first = jnp.logical_and(first, ids[k] == 0)
            last = jnp.logical_and(last, ids[k] == grid[k] - 1)

        @pl.when(first)
        def _():
            rider.start(refs[a:b], refs[c:d], refs[e:])

        body(*refs[:a], *refs[b:c], *refs[d:e])

        @pl.when(last)
        def _():
            rider.finish(refs[a:b], refs[c:d], refs[e:])

    res = _pallas(
        carried, name=name, out_shape=out_shape + list(rider.out_shapes), grid=grid,
        in_specs=in_specs + [_any_spec()] * ri, out_specs=out_specs + [_any_spec()] * ro,
        scratch_shapes=scratch_shapes + list(rider.sems), input_output_aliases=aliases or {},
        compiler_params=_params(("arbitrary",) * len(grid), vmem))(*operands, *rider.operands)
    return list(res[:n_out]), list(res[n_out:])


def _exchange(rider, name):
    ri, ro = len(rider.operands), len(rider.out_shapes)

    def body(*refs):
        rider.start(refs[:ri], refs[ri:ri + ro], refs[ri + ro:])
        rider.finish(refs[:ri], refs[ri:ri + ro], refs[ri + ro:])

    return _pallas(
        body, name=name, out_shape=list(rider.out_shapes), in_specs=[_any_spec()] * ri,
        out_specs=[_any_spec()] * ro, scratch_shapes=list(rider.sems))(*rider.operands)


class Cols(NamedTuple):
    arr: jax.Array
    off: int
    width: int


def _cols(a):
    return a if isinstance(a, Cols) else Cols(a, 0, a.shape[1])


def _matmul(a, b, *, mode, name, out_dtype=F32, tm=1024, tn=1024, tk=2048, bias=None, out_cols=None, into=None,
            rider=None):
    a, b = _cols(a), _cols(b)
    if mode == "nn":
        (m, k), (k2, n) = (a.arr.shape[0], a.width), (b.arr.shape[0], b.width)
    elif mode == "nt":
        (m, k), (n, k2) = (a.arr.shape[0], a.width), (b.arr.shape[0], b.width)
    else:
        (k, m), (k2, n) = (a.arr.shape[0], a.width), (b.arr.shape[0], b.width)
    assert k == k2, (a.arr.shape, b.arr.shape, mode)
    tm, tn, tk = _tile(m, tm), _tile(n, tn), _tile(k, tk)
    nk = k // tk
    dims = {"nn": NN, "nt": NT, "tn": TN}[mode]
    if mode == "tn":
        assert a.off % tm == 0
        a_spec = pl.BlockSpec((tk, tm), lambda i, j, kk, o=a.off // tm: (kk, i + o))
    else:
        assert a.off % tk == 0
        a_spec = pl.BlockSpec((tm, tk), lambda i, j, kk, o=a.off // tk: (i, kk + o))
    if mode == "nt":
        assert b.off % tk == 0
        b_spec = pl.BlockSpec((tn, tk), lambda i, j, kk, o=b.off // tk: (j, kk + o))
    else:
        assert b.off % tn == 0
        b_spec = pl.BlockSpec((tk, tn), lambda i, j, kk, o=b.off // tn: (kk, j + o))
    in_specs, operands = [a_spec, b_spec], [a.arr, b.arr]
    if bias is not None:
        in_specs.append(pl.BlockSpec((1, tn), lambda i, j, kk: (0, j)))
        operands.append(bias)
    total_w, o_off = out_cols if out_cols is not None else (n, 0)
    assert o_off % tn == 0
    aliases = {}
    if into is not None:
        assert into.shape == (m, total_w) and into.dtype == out_dtype
        in_specs.append(_any_spec())
        operands.append(into)
        aliases = {len(operands) - 1: 0}
    n_in = len(operands)

    def body(*refs):
        a_ref, b_ref = refs[0], refs[1]
        bias_ref = refs[2] if bias is not None else None
        o_ref = refs[n_in]
        acc_ref = refs[-1] if nk > 1 else None
        part = _dot(a_ref[...].astype(BF16), b_ref[...].astype(BF16), dims)

        def finish(acc):
            if bias_ref is not None:
                acc = acc + bias_ref[...]
            o_ref[...] = acc.astype(out_dtype)

        if nk == 1:
            finish(part)
        else:
            kk = pl.program_id(2)

            @pl.when(kk == 0)
            def _():
                acc_ref[...] = part

            @pl.when(kk > 0)
            def _():
                acc_ref[...] += part

            @pl.when(kk == nk - 1)
            def _():
                finish(acc_ref[...])

    vmem = 2 * (_nbytes((tm, tk), a.arr.dtype) + _nbytes((tk, tn), b.arr.dtype) + _nbytes((tm, tn), out_dtype))
    vmem += 3 * _nbytes((tm, tn), F32)
    (out,), landed = _call(
        body, operands, name=name, out_shape=[jax.ShapeDtypeStruct((m, total_w), out_dtype)],
        grid=(m // tm, n // tn, nk), in_specs=in_specs,
        out_specs=[pl.BlockSpec((tm, tn), lambda i, j, kk, o=o_off // tn: (i, j + o))],
        scratch_shapes=[pltpu.VMEM((tm, tn), F32)] if nk > 1 else [], aliases=aliases,
        semantics=("parallel", "parallel", "arbitrary"), vmem=vmem, rider=rider)
    return out if rider is None else (out, landed)


def _ew(fn, *, name, rows, width, tiles, vecs=(), outs, accs=0, tl=1024, cw=512, into=None, with_col=False):
    tl, cw = _tile(rows, tl, SUBLANES), _tile(width, cw)
    ncol = width // cw
    nt_, nv = len(tiles), len(vecs)
    into = list(into) if into is not None else [None] * len(outs)
    aliased = [t for t in into if t is not None]

    def off(o):
        assert o % cw == 0, (name, o, cw)
        return o // cw

    in_specs, vmem = [], 0
    for t in tiles:
        arr, o = t[0], off(t[1])
        wrap = t[2] // cw if len(t) > 2 else ncol
        in_specs.append(pl.BlockSpec((tl, cw), lambda j, i, o=o, wrap=wrap: (i, o + j % wrap)))
        vmem += _nbytes((tl, cw), arr.dtype)
    in_specs += [pl.BlockSpec((1, cw), lambda j, i, o=off(o): (0, j + o)) for _, o in vecs]
    in_specs += [_any_spec() for _ in aliased]
    out_shape, out_specs, aliases = [], [], {}
    n_in = nt_ + nv
    for idx, ((dt, tw, o), tgt) in enumerate(zip(outs, into)):
        out_shape.append(jax.ShapeDtypeStruct((rows, tw), dt))
        out_specs.append(pl.BlockSpec((tl, cw), lambda j, i, o=off(o): (i, j + o)))
        vmem += _nbytes((tl, cw), dt)
        if tgt is not None:
            assert tgt.shape == (rows, tw) and tgt.dtype == dt, (name, tgt.shape, tgt.dtype)
            aliases[n_in + len(aliases)] = idx
    for _ in range(accs):
        out_shape.append(jax.ShapeDtypeStruct((1, width), F32))
        out_specs.append(pl.BlockSpec((1, cw), lambda j, i: (0, j)))
    n_out = len(outs)

    def body(*refs):
        vals = [r[...].astype(F32) for r in refs[:n_in]]
        out_refs = refs[n_in + len(aliased):]
        res = fn(pl.program_id(0), *vals) if with_col else fn(*vals)
        res = res if isinstance(res, (tuple, list)) else (res,)
        assert len(res) == n_out + accs, (name, len(res))
        for r, v in zip(out_refs[:n_out], res[:n_out]):
            r[...] = v.astype(r.dtype)
        first = pl.program_id(1) == 0
        for r, v in zip(out_refs[n_out:], res[n_out:]):
            s = jnp.sum(v, axis=0, keepdims=True)

            @pl.when(first)
            def _(r=r, s=s):
                r[...] = s

            @pl.when(jnp.logical_not(first))
            def _(r=r, s=s):
                r[...] += s

    return _pallas(
        body, name=name, out_shape=out_shape, grid=(ncol, rows // tl),
        in_specs=in_specs, out_specs=out_specs, input_output_aliases=aliases,
        compiler_params=_params(("parallel", "arbitrary"), 3 * vmem),
    )(*[t[0] for t in tiles], *[v for v, _ in vecs], *aliased)


def _rmsnorm_fwd(x, w_row, name):
    rows, d = x.shape
    tl = _tile(rows, 512, SUBLANES)

    def body(x_ref, w_ref, h_ref):
        xv = x_ref[...]
        rstd = lax.rsqrt(jnp.mean(xv * xv, axis=-1, keepdims=True) + NORM_EPS)
        h_ref[...] = (xv * rstd * w_ref[...]).astype(BF16)

    return _pallas(
        body, name=name, out_shape=jax.ShapeDtypeStruct((rows, d), BF16), grid=(rows // tl,),
        in_specs=[pl.BlockSpec((tl, d), lambda i: (i, 0)), pl.BlockSpec((1, d), lambda i: (0, 0))],
        out_specs=pl.BlockSpec((tl, d), lambda i: (i, 0)),
        compiler_params=_params(("parallel",)),
    )(x, w_row)


def _rmsnorm_bwd(x, w_row, dh, dout, name, rider=None):
    rows, d = x.shape
    tl = _tile(rows, 256, SUBLANES)

    def body(x_ref, w_ref, dh_ref, dout_ref, gx_ref, gw_ref):
        xv = x_ref[...]
        rstd = lax.rsqrt(jnp.mean(xv * xv, axis=-1, keepdims=True) + NORM_EPS)
        xn = xv * rstd
        dhv = dh_ref[...]
        dxn = dhv * w_ref[...]
        dx = rstd * (dxn - xn * jnp.mean(dxn * xn, axis=-1, keepdims=True))
        gx_ref[...] = dout_ref[...] + dx
        gw = jnp.sum(dhv * xn, axis=0, keepdims=True)

        @pl.when(pl.program_id(0) == 0)
        def _():
            gw_ref[...] = gw

        @pl.when(pl.program_id(0) > 0)
        def _():
            gw_ref[...] += gw

    tile = pl.BlockSpec((tl, d), lambda i: (i, 0))
    row = pl.BlockSpec((1, d), lambda i: (0, 0))
    res, landed = _call(
        body, [x, w_row, dh, dout], name=name,
        out_shape=[jax.ShapeDtypeStruct((rows, d), F32), jax.ShapeDtypeStruct((1, d), F32)],
        grid=(rows // tl,), in_specs=[tile, row, tile, tile], out_specs=[tile, row],
        semantics=("arbitrary",), rider=rider)
    return res if rider is None else (res, landed)


def _head_mean(x, gmat):
    hi = x.astype(BF16)
    lo = (x - hi.astype(F32)).astype(BF16)
    out = []
    for s in range(x.shape[1] // MXU_DIM):
        sl = slice(s * MXU_DIM, (s + 1) * MXU_DIM)
        out.append(_dot(hi[:, sl], gmat, NN) + _dot(lo[:, sl], gmat, NN))
    return out[0] if len(out) == 1 else jnp.concatenate(out, axis=1)


def _head_mean_matrix():
    blk = jnp.arange(MXU_DIM) // HEAD_DIM
    return jnp.where(blk[:, None] == blk[None, :], 1.0 / HEAD_DIM, 0.0).astype(BF16)


def _spread_head(x, g, width):
    col = x[:, (g // 2) * LANES:(g // 2 + 1) * LANES]
    other = pltpu.roll(col, HEAD_DIM, axis=1)
    low = lax.broadcasted_iota(jnp.int32, col.shape, 1) < HEAD_DIM
    both = jnp.where(low, col, other) if g % 2 == 0 else jnp.where(low, other, col)
    return both if width == LANES else jnp.concatenate([both] * (width // LANES), axis=1)


def _head_diagonal(t, per_kv):
    head = lax.broadcasted_iota(jnp.int32, t.shape, 1) // HEAD_DIM
    zero = jnp.zeros_like(t)
    return jnp.concatenate([jnp.where(head == r, t, zero) for r in range(per_kv)], axis=0)


def _fold_heads(x, per_kv):
    rows = x.shape[0] // per_kv
    head = lax.broadcasted_iota(jnp.int32, (rows, x.shape[1]), 1) // HEAD_DIM
    acc = jnp.where(head == 0, x[0:rows], 0.0)
    for r in range(1, per_kv):
        acc = acc + jnp.where(head == r, x[r * rows:(r + 1) * rows], 0.0)
    while acc.shape[1] > LANES:
        half = acc.shape[1] // 2
        acc = acc[:, :half] + acc[:, half:]
    return acc + pltpu.roll(acc, HEAD_DIM, axis=1)


def _join_heads(parts):
    low = lax.broadcasted_iota(jnp.int32, parts[0].shape, 1) < HEAD_DIM
    cols = [jnp.where(low, parts[2 * j], parts[2 * j + 1]) for j in range(len(parts) // 2)]
    return cols[0] if len(cols) == 1 else jnp.concatenate(cols, axis=1)


def _attn_specs(attn_w, kv_w):
    half = attn_w // 2
    kcol, vcol = attn_w // kv_w, attn_w // kv_w + 1
    gcol = (attn_w + 2 * kv_w) // half
    prev = lambda i: jnp.maximum(i - 1, 0)
    return [
        pl.BlockSpec((BLOCK, attn_w), lambda i: (i, 0)),
        pl.BlockSpec((BLOCK, kv_w), lambda i: (prev(i), kcol)),
        pl.BlockSpec((BLOCK, kv_w), lambda i: (i, kcol)),
        pl.BlockSpec((BLOCK, kv_w), lambda i: (prev(i), vcol)),
        pl.BlockSpec((BLOCK, kv_w), lambda i: (i, vcol)),
        pl.BlockSpec((BLOCK, half), lambda i: (i, gcol)),
        pl.BlockSpec((BLOCK, half), lambda i: (i, gcol + 1)),
    ]


def _band_mask(i):
    q_loc = lax.broadcasted_iota(jnp.int32, (BLOCK, 2 * BLOCK), 0) + BLOCK
    k_loc = lax.broadcasted_iota(jnp.int32, (BLOCK, 2 * BLOCK), 1)
    diff = q_loc - k_loc
    first_key = jnp.where(i == 0, BLOCK, 0)
    return (diff >= 0) & (diff < BLOCK) & (k_loc >= first_key)


def _softmax_with_sink(s, sink):
    m = jnp.maximum(jnp.max(s, axis=-1, keepdims=True), sink)
    p = jnp.exp(s - m)
    e_sink = jnp.exp(sink - m)
    den = jnp.sum(p, axis=-1, keepdims=True) + e_sink
    inv = 1.0 / den
    return p * inv, e_sink * inv


def _attn_block(i, q, kk, vv, qw, kw, gmat, sink_ref, per_kv):
    scale = 1.0 / math.sqrt(HEAD_DIM)
    keys = 2 * BLOCK
    valid = _band_mask(i)
    q_rstd = lax.rsqrt(_head_mean(q * q, gmat) + NORM_EPS)
    qn = q * q_rstd
    qh = (qn * qw).astype(BF16)
    k_rstd = lax.rsqrt(_head_mean(kk * kk, gmat) + NORM_EPS)
    kn = kk * k_rstd
    kh = kn * kw
    gw = per_kv * HEAD_DIM
    groups = []
    for g in range(N_KV_HEADS):
        kd = _head_diagonal(_spread_head(kh, g, gw).astype(BF16), per_kv)
        vd = _head_diagonal(_spread_head(vv, g, gw).astype(BF16), per_kv)
        qg = qh[:, g * gw:(g + 1) * gw]
        s_all = _dot(qg, kd, NT) * scale
        ps, p_sinks = [], []
        for r in range(per_kv):
            s = jnp.where(valid, s_all[:, r * keys:(r + 1) * keys], -1e30)
            p, p_sink = _softmax_with_sink(s, sink_ref[g * per_kv + r])
            ps.append(p)
            p_sinks.append(p_sink)
        pb = jnp.concatenate(ps, axis=1).astype(BF16)
        groups.append((kd, vd, qg, ps, p_sinks, pb, _dot(pb, vd, NN)))
    return qn, q_rstd, kn, k_rstd, groups


def _attention_fwd(proj, qw_row, kw_row, gmat, sinks, *, attn_w, kv_w, name):
    rows = proj.shape[0]
    per_kv = attn_w // HEAD_DIM // N_KV_HEADS

    def body(q_ref, kp_ref, kc_ref, vp_ref, vc_ref, glo_ref, ghi_ref, qw_ref, kw_ref, gm_ref, sink_ref, o_ref):
        kk = jnp.concatenate([kp_ref[...], kc_ref[...]], axis=0).astype(F32)
        vv = jnp.concatenate([vp_ref[...], vc_ref[...]], axis=0).astype(F32)
        gate = jnp.concatenate([glo_ref[...], ghi_ref[...]], axis=1).astype(F32)
        *_, groups = _attn_block(pl.program_id(0), q_ref[...].astype(F32), kk, vv, qw_ref[...], kw_ref[...], gm_ref[...],
                                 sink_ref, per_kv)
        attn = jnp.concatenate([grp[-1] for grp in groups], axis=1)
        o_ref[...] = (attn * _silu(gate)).astype(BF16)

    const = lambda a: pl.BlockSpec(a.shape, lambda i: (0, 0))
    return _pallas(
        body, name=name, out_shape=jax.ShapeDtypeStruct((rows, attn_w), BF16), grid=(rows // BLOCK,),
        in_specs=_attn_specs(attn_w, kv_w) + [const(qw_row), const(kw_row), const(gmat),
                                              pl.BlockSpec(memory_space=pltpu.SMEM)],
        out_specs=pl.BlockSpec((BLOCK, attn_w), lambda i: (i, 0)),
        compiler_params=_params(("parallel",), 40 * 1024 * 1024),
    )(proj, proj, proj, proj, proj, proj, proj, qw_row, kw_row, gmat, sinks)


def _attention_bwd(proj, d_ag, dproj, qw_row, kw_row, gmat, sinks, *, attn_w, kv_w, name, rider=None):
    rows = proj.shape[0]
    nb = rows // BLOCK
    per_kv = attn_w // HEAD_DIM // N_KV_HEADS
    gw = per_kv * HEAD_DIM
    keys = 2 * BLOCK
    scale = 1.0 / math.sqrt(HEAD_DIM)
    w_out = 2 * attn_w + 2 * kv_w

    def body(q_ref, kp_ref, kc_ref, vp_ref, vc_ref, glo_ref, ghi_ref, dag_ref, qw_ref, kw_ref, gm_ref, sink_ref, _,
             dp_ref, dkv_ref, gqw_ref, gkw_ref, gs_ref):
        i = pl.program_id(0)
        kk = jnp.concatenate([kp_ref[...], kc_ref[...]], axis=0).astype(F32)
        vv = jnp.concatenate([vp_ref[...], vc_ref[...]], axis=0).astype(F32)
        gate = jnp.concatenate([glo_ref[...], ghi_ref[...]], axis=1).astype(F32)
        d_ag_v = dag_ref[...].astype(F32)
        qw, kw, gmat_v = qw_ref[...], kw_ref[...], gm_ref[...]
        qn, q_rstd, kn, k_rstd, groups = _attn_block(i, q_ref[...].astype(F32), kk, vv, qw, kw, gmat_v, sink_ref,
                                                     per_kv)
        lane = lax.broadcasted_iota(jnp.int32, (SUBLANES, LANES), 1)
        sub = lax.broadcasted_iota(jnp.int32, (SUBLANES, LANES), 0)
        gsink = jnp.zeros((SUBLANES, LANES), F32)
        dq_groups, dgate_groups, dk_heads, dv_heads = [], [], [], []
        for g, (kd, vd, qg, ps, p_sinks, pb, o) in enumerate(groups):
            cs = slice(g * gw, (g + 1) * gw)
            gate_g, d_ag_g = gate[:, cs], d_ag_v[:, cs]
            dgate_groups.append(d_ag_g * o * _dsilu(gate_g))
            do = (d_ag_g * _silu(gate_g)).astype(BF16)
            dp_all = _dot(do, vd, NT)
            dss = []
            for r in range(per_kv):
                p, dp = ps[r], dp_all[:, r * keys:(r + 1) * keys]
                delta = jnp.sum(p * dp, axis=-1, keepdims=True)
                dss.append(p * (dp - delta) * scale)
                gs_h = jnp.sum(-p_sinks[r] * delta, axis=0, keepdims=True)
                gsink = gsink + jnp.where((lane == g * per_kv + r) & (sub == 0), gs_h, 0.0)
            ds = jnp.concatenate(dss, axis=1).astype(BF16)
            dq_groups.append(_dot(ds, kd, NN))
            dk_heads.append(_fold_heads(_dot(ds, qg, TN), per_kv))
            dv_heads.append(_fold_heads(_dot(pb, do, TN), per_kv))
        dqh = jnp.concatenate(dq_groups, axis=1)
        gqw = jnp.sum(dqh * qn, axis=0, keepdims=True)
        dqn = dqh * qw
        dq = q_rstd * (dqn - qn * _head_mean(dqn * qn, gmat_v))
        dkh = _join_heads(dk_heads)
        gkw = jnp.sum(dkh * kn, axis=0, keepdims=True)
        dkn = dkh * kw
        dk = k_rstd * (dkn - kn * _head_mean(dkn * kn, gmat_v))
        dp_ref[:, 0:attn_w] = dq.astype(BF16)
        dp_ref[:, attn_w:attn_w + 2 * kv_w] = jnp.zeros((BLOCK, 2 * kv_w), BF16)
        dp_ref[:, attn_w + 2 * kv_w:w_out] = jnp.concatenate(dgate_groups, axis=1).astype(BF16)
        dkv_ref[0] = jnp.concatenate([dk, _join_heads(dv_heads)], axis=1)

        @pl.when(i == 0)
        def _():
            gqw_ref[...] = gqw
            gkw_ref[...] = gkw
            gs_ref[...] = gsink

        @pl.when(i > 0)
        def _():
            gqw_ref[...] += gqw
            gkw_ref[...] += gkw
            gs_ref[...] += gsink

    const = lambda a: pl.BlockSpec(a.shape, lambda i: (0, 0))
    res, landed = _call(
        body, [proj, proj, proj, proj, proj, proj, proj, d_ag, qw_row, kw_row, gmat, sinks, dproj], name=name,
        out_shape=[jax.ShapeDtypeStruct(dproj.shape, BF16),
                   jax.ShapeDtypeStruct((nb, 2 * BLOCK, 2 * kv_w), F32),
                   jax.ShapeDtypeStruct(qw_row.shape, F32), jax.ShapeDtypeStruct(kw_row.shape, F32),
                   jax.ShapeDtypeStruct((SUBLANES, LANES), F32)],
        grid=(nb,),
        in_specs=_attn_specs(attn_w, kv_w) + [pl.BlockSpec((BLOCK, attn_w), lambda i: (i, 0)), const(qw_row),
                                              const(kw_row), const(gmat), pl.BlockSpec(memory_space=pltpu.SMEM),
                                              _any_spec()],
        out_specs=[pl.BlockSpec((BLOCK, w_out), lambda i: (i, 0)),
                   pl.BlockSpec((1, 2 * BLOCK, 2 * kv_w), lambda i: (i, 0, 0)),
                   const(qw_row), const(kw_row), pl.BlockSpec((SUBLANES, LANES), lambda i: (0, 0))],
        aliases={12: 0}, semantics=("arbitrary",), vmem=48 * 1024 * 1024, rider=rider)
    return res if rider is None else (res, landed)


def _attention_dkv(dproj, dkv, *, attn_w, kv_w, name):
    rows = dproj.shape[0]
    nb = rows // BLOCK
    col = attn_w // (2 * kv_w)

    def body(cur_ref, nxt_ref, _, o_ref):
        i = pl.program_id(0)
        nxt = jnp.where(i < nb - 1, nxt_ref[0, 0:BLOCK, :], 0.0)
        o_ref[...] = (cur_ref[0, BLOCK:2 * BLOCK, :] + nxt).astype(BF16)

    blk = lambda f: pl.BlockSpec((1, 2 * BLOCK, 2 * kv_w), f)
    return _pallas(
        body, name=name, out_shape=jax.ShapeDtypeStruct(dproj.shape, BF16), grid=(nb,),
        in_specs=[blk(lambda i: (i, 0, 0)), blk(lambda i: (jnp.minimum(i + 1, nb - 1), 0, 0)), _any_spec()],
        out_specs=pl.BlockSpec((BLOCK, 2 * kv_w), lambda i: (i, col)),
        input_output_aliases={2: 0},
        compiler_params=_params(("parallel",)),
    )(dkv, dkv, dproj)


def _cmul(ar, ai, br, bi):
    return ar * br - ai * bi, ar * bi + ai * br


def _ssm_prep(a_re, a_im, log_dt_col, steps, name):
    assert steps & (steps - 1) == 0

    def body(are_ref, aim_ref, ldt_ref, abr_ref, abi_ref, cfr_ref, cfi_ref, apr_ref, api_ref):
        are, aim = are_ref[...], aim_ref[...]
        dt = jnp.exp(ldt_ref[...])
        mag = jnp.exp(dt * are)
        abr = mag * jnp.cos(dt * aim)
        abi = mag * jnp.sin(dt * aim)
        num_re, num_im = abr - 1.0, abi
        den = are * are + aim * aim
        abr_ref[...] = abr
        abi_ref[...] = abi
        cfr_ref[...] = (num_re * are + num_im * aim) / den
        cfi_ref[...] = (num_im * are - num_re * aim) / den
        pr, pi = abr, abi
        n = steps
        while n > 1:
            pr, pi = _cmul(pr, pi, pr, pi)
            n //= 2
        apr_ref[...] = pr
        api_ref[...] = pi

    shp = jax.ShapeDtypeStruct(a_re.shape, F32)
    return _pallas(body, name=name, out_shape=[shp] * 6)(a_re, a_im, log_dt_col)


def _ssm_param_bwd(a_re, a_im, log_dt_col, d_ab_re, d_ab_im, d_cf_re, d_cf_im, name):
    def body(are_ref, aim_ref, ldt_ref, gabr_ref, gabi_ref, gcfr_ref, gcfi_ref, dar_ref, dai_ref, dldt_ref):
        are, aim = are_ref[...], aim_ref[...]
        dt = jnp.exp(ldt_ref[...])
        mag = jnp.exp(dt * are)
        abr = mag * jnp.cos(dt * aim)
        abi = mag * jnp.sin(dt * aim)
        den = are * are + aim * aim
        cfr = ((abr - 1.0) * are + abi * aim) / den
        cfi = (abi * are - (abr - 1.0) * aim) / den
        gabr, gabi = jnp.sum(gabr_ref[...], axis=0), jnp.sum(gabi_ref[...], axis=0)
        gcfr, gcfi = jnp.sum(gcfr_ref[...], axis=0), jnp.sum(gcfi_ref[...], axis=0)
        inv_r, inv_i = are / den, -aim / den
        t_r, t_i = _cmul(inv_r, -inv_i, gcfr, gcfi)
        gabr, gabi = gabr + t_r, gabi + t_i
        q_r, q_i = _cmul(cfr, cfi, inv_r, inv_i)
        da_r, da_i = _cmul(-q_r, q_i, gcfr, gcfi)
        gz_r, gz_i = _cmul(abr, -abi, gabr, gabi)
        dar_ref[...] = da_r + dt * gz_r
        dai_ref[...] = da_i + dt * gz_i
        dldt_ref[...] = dt * jnp.sum(are * gz_r + aim * gz_i, axis=-1, keepdims=True)

    shp = jax.ShapeDtypeStruct(a_re.shape, F32)
    return _pallas(body, name=name, out_shape=[shp, shp, jax.ShapeDtypeStruct(log_dt_col.shape, F32)])(
        a_re, a_im, log_dt_col, d_ab_re, d_ab_im, d_cf_re, d_cf_im)


SCAN_LANES = 512
W_IN_GRAD_PARTS = 2


def _scan_segments(xr_ref, xi_ref, a_re, a_im, ap_re, ap_im, carry_re, carry_im, cm_re, cm_im, steps, reverse):
    n = xr_ref.shape[1]
    order = range(steps - 1, -1, -1) if reverse else range(steps)
    seg_order = range(SUBLANES - 1, -1, -1) if reverse else range(SUBLANES)
    for c0 in range(0, n, SCAN_LANES):
        ls = slice(c0, c0 + SCAN_LANES)
        ar = jnp.broadcast_to(a_re[:, ls], (SUBLANES, SCAN_LANES))
        ai = jnp.broadcast_to(a_im[:, ls], (SUBLANES, SCAN_LANES))

        def local(t, s, ar=ar, ai=ai, ls=ls):
            j = steps - 1 - t if reverse else t
            r0 = pl.multiple_of(j * SUBLANES, SUBLANES)
            sr, si = _cmul(ar, ai, s[0], s[1])
            sr = sr + xr_ref[pl.ds(r0, SUBLANES), ls]
            si = si + xi_ref[pl.ds(r0, SUBLANES), ls]
            xr_ref[pl.ds(r0, SUBLANES), ls] = sr
            xi_ref[pl.ds(r0, SUBLANES), ls] = si
            return sr, si

        zero = jnp.zeros((SUBLANES, SCAN_LANES), F32)
        end_r, end_i = lax.fori_loop(0, steps, local, (zero, zero))
        cr, ci = carry_re[:, ls], carry_im[:, ls]
        apr, api = ap_re[:, ls], ap_im[:, ls]
        for r in seg_order:
            cm_re[r:r + 1, ls] = cr
            cm_im[r:r + 1, ls] = ci
            tr, ti = _cmul(apr, api, cr, ci)
            cr, ci = end_r[r:r + 1, :] + tr, end_i[r:r + 1, :] + ti
        carry_re[:, ls] = cr
        carry_im[:, ls] = ci

        def fix(t, s, ar=ar, ai=ai, ls=ls):
            j = steps - 1 - t if reverse else t
            r0 = pl.multiple_of(j * SUBLANES, SUBLANES)
            sr, si = _cmul(ar, ai, s[0], s[1])
            xr_ref[pl.ds(r0, SUBLANES), ls] += sr
            xi_ref[pl.ds(r0, SUBLANES), ls] += si
            return sr, si

        lax.fori_loop(0, steps, fix, (cm_re[:, ls], cm_im[:, ls]))
    del order


SB_GROUPS = MXU_DIM // GROUP
SB_STATE = SB_GROUPS * STATE


def _ssm_rows(b_re, b_im, c_re, c_im):
    def rows(m):
        flat = m.reshape(-1, STATE).astype(F32)
        return jnp.concatenate([flat, flat], axis=1)
    return rows(b_re.transpose(0, 2, 1)), rows(b_im.transpose(0, 2, 1)), rows(c_re), rows(c_im)


def _from_ssm_rows(rows, transpose):
    g = rows[:, :STATE].reshape(-1, GROUP, STATE)
    return g.transpose(0, 2, 1) if transpose else g


def _own_group(shape):
    row_g = lax.broadcasted_iota(jnp.int32, shape, 0) // GROUP
    col_g = lax.broadcasted_iota(jnp.int32, shape, 1) // STATE
    return row_g == col_g


def _block_diagonal(rows):
    tiled = jnp.concatenate([rows] * (SB_STATE // LANES), axis=1)
    return jnp.where(_own_group(tiled.shape), tiled, 0.0).astype(BF16)


def _block_rows(acc):
    x = jnp.where(_own_group(acc.shape), acc, 0.0)
    while x.shape[1] > LANES:
        half = x.shape[1] // 2
        x = x[:, :half] + x[:, half:]
    return x + pltpu.roll(x, STATE, axis=1)


def _rows_to_segments(dst, srcs, steps, stage):
    for ref, off in srcs:
        for k in range(ref.shape[1] // LANES):
            stage[off // LANES + k] = ref[:, k * LANES:(k + 1) * LANES].astype(F32)
    for k in range(dst.shape[1] // LANES):
        for j in range(steps):
            dst[j * SUBLANES:(j + 1) * SUBLANES, k * LANES:(k + 1) * LANES] = (
                stage[k, pl.ds(j, SUBLANES, stride=steps), :])


def _segments_to_rows(dst, src, steps, stage):
    for k in range(src.shape[1] // LANES):
        for j in range(steps):
            stage[k, pl.ds(j, SUBLANES, stride=steps), :] = (
                src[j * SUBLANES:(j + 1) * SUBLANES, k * LANES:(k + 1) * LANES])
    for k in range(src.shape[1] // LANES):
        dst[:, k * LANES:(k + 1) * LANES] = stage[k]


def _u_specs(w, o_u, chunk, index):
    half = w // 2
    assert o_u % half == 0
    return [pl.BlockSpec((chunk, half), lambda c, k=k: (index(c), o_u // half + k)) for k in range(2)]


def _ssm_fwd(proj, o_u, bc_rows, rows_p, d_row, *, chunk, name):
    rows = proj.shape[0]
    w = d_row.shape[1]
    nc = rows // chunk
    steps = chunk // SUBLANES
    nsb = w // MXU_DIM
    n_state = nsb * SB_STATE

    def body(ulo_ref, uhi_ref, b2r_ref, b2i_ref, c2r_ref, c2i_ref, abr_ref, abi_ref, cfr_ref, cfi_ref, apr_ref,
             api_ref, d_ref, y_ref, str_ref, sti_ref, bre_ref, bim_ref, cre_ref, cim_ref, useg, yseg, stage, sr, si,
             carry_r, carry_i, cm_r, cm_i):
        @pl.when(pl.program_id(0) == 0)
        def _():
            for src, dst in ((b2r_ref, bre_ref), (b2i_ref, bim_ref), (c2r_ref, cre_ref), (c2i_ref, cim_ref)):
                for sb in range(nsb):
                    dst[sb] = _block_diagonal(src[sb * MXU_DIM:(sb + 1) * MXU_DIM, :])
            carry_r[...] = jnp.zeros_like(carry_r)
            carry_i[...] = jnp.zeros_like(carry_i)

        str_ref[0] = carry_r[...]
        sti_ref[0] = carry_i[...]
        _rows_to_segments(useg, [(ulo_ref, 0), (uhi_ref, w // 2)], steps, stage)
        for sb in range(nsb):
            us = slice(sb * MXU_DIM, (sb + 1) * MXU_DIM)
            ss = slice(sb * SB_STATE, (sb + 1) * SB_STATE)
            ub = useg[:, us].astype(BF16)
            bur = _dot(ub, bre_ref[sb], NN)
            bui = _dot(ub, bim_ref[sb], NN)
            xr, xi = _cmul(cfr_ref[:, ss], cfi_ref[:, ss], bur, bui)
            sr[:, ss] = xr
            si[:, ss] = xi
        _scan_segments(sr, si, abr_ref[...], abi_ref[...], apr_ref[...], api_ref[...],
                       carry_r, carry_i, cm_r, cm_i, steps, False)
        for sb in range(nsb):
            us = slice(sb * MXU_DIM, (sb + 1) * MXU_DIM)
            ss = slice(sb * SB_STATE, (sb + 1) * SB_STATE)
            y = _dot(sr[:, ss].astype(BF16), cre_ref[sb], NT) - _dot(si[:, ss].astype(BF16), cim_ref[sb], NT)
            yseg[:, us] = y + d_ref[:, us] * useg[:, us]
        _segments_to_rows(y_ref, yseg, steps, stage)

    const = lambda a: pl.BlockSpec(a.shape, lambda c: (0,) * a.ndim)
    row_n = pl.BlockSpec((1, n_state), lambda c: (0, 0))
    st = pl.BlockSpec((1, 1, n_state), lambda c: (c, 0, 0))
    held = [pltpu.VMEM((nsb, MXU_DIM, SB_STATE), BF16)] * 4
    vmem = 4 * _nbytes((nsb, MXU_DIM, SB_STATE), BF16) + 3 * _nbytes((chunk, n_state), F32)
    return _pallas(
        body, name=name,
        out_shape=[jax.ShapeDtypeStruct((rows, w), F32), jax.ShapeDtypeStruct((nc, 1, n_state), F32),
                   jax.ShapeDtypeStruct((nc, 1, n_state), F32)],
        grid=(nc,),
        in_specs=_u_specs(w, o_u, chunk, lambda c: c) + [const(b) for b in bc_rows]
        + [row_n] * 6 + [pl.BlockSpec((1, w), lambda c: (0, 0))],
        out_specs=[pl.BlockSpec((chunk, w), lambda c: (c, 0)), st, st],
        scratch_shapes=held + [pltpu.VMEM((chunk, w), F32), pltpu.VMEM((chunk, w), F32),
                               pltpu.VMEM((w // LANES, chunk, LANES), F32),
                               pltpu.VMEM((chunk, n_state), F32), pltpu.VMEM((chunk, n_state), F32),
                               pltpu.VMEM((1, n_state), F32), pltpu.VMEM((1, n_state), F32),
                               pltpu.VMEM((SUBLANES, n_state), F32), pltpu.VMEM((SUBLANES, n_state), F32)],
        compiler_params=_params(("arbitrary",), vmem),
    )(proj, proj, *bc_rows, *rows_p, d_row)


def _ssm_bwd(proj, o_u, y, dyg, st_re, st_im, bc_rows, rows_p, d_row, *, chunk, name, rider=None):
    rows = proj.shape[0]
    w = d_row.shape[1]
    nc = rows // chunk
    steps = chunk // SUBLANES
    nsb = w // MXU_DIM
    n_state = nsb * SB_STATE

    def body(ulo_ref, uhi_ref, y_ref, dyg_ref, str_ref, sti_ref, b2r_ref, b2i_ref, c2r_ref, c2i_ref,
             abr_ref, abi_ref, cfr_ref, cfi_ref, apr_ref, api_ref, d_ref,
             du_ref, gb2r_ref, gb2i_ref, gc2r_ref, gc2i_ref, gabr_ref, gabi_ref, gcfr_ref, gcfi_ref, dd_ref,
             bre_ref, bim_ref, cre_ref, cim_ref, dbre_ref, dbim_ref, dcre_ref, dcim_ref, useg, dyseg, dynat, stage,
             bur, bui, sr, si, lr, li, carry_r, carry_i, lam_r, lam_i, cm_r, cm_i, cl_r, cl_i):
        first = pl.program_id(0) == 0

        @pl.when(first)
        def _():
            for src, dst in ((b2r_ref, bre_ref), (b2i_ref, bim_ref), (c2r_ref, cre_ref), (c2i_ref, cim_ref)):
                for sb in range(nsb):
                    dst[sb] = _block_diagonal(src[sb * MXU_DIM:(sb + 1) * MXU_DIM, :])
            lam_r[...] = jnp.zeros_like(lam_r)
            lam_i[...] = jnp.zeros_like(lam_i)
            for ref in (dbre_ref, dbim_ref, dcre_ref, dcim_ref, gabr_ref, gabi_ref, gcfr_ref, gcfi_ref, dd_ref):
                ref[...] = jnp.zeros_like(ref)

        dynat[...] = dyg_ref[...].astype(F32) * _dgelu(y_ref[...])
        half = w // 2
        dd_ref[:, :half] += jnp.sum(dynat[:, :half] * ulo_ref[...].astype(F32), axis=0, keepdims=True)
        dd_ref[:, half:] += jnp.sum(dynat[:, half:] * uhi_ref[...].astype(F32), axis=0, keepdims=True)
        _rows_to_segments(useg, [(ulo_ref, 0), (uhi_ref, half)], steps, stage)
        _rows_to_segments(dyseg, [(dynat, 0)], steps, stage)
        dy = dyseg[...]
        dyb = dy.astype(BF16)
        ub = useg[...].astype(BF16)
        carry_r[...] = str_ref[0]
        carry_i[...] = sti_ref[0]
        for sb in range(nsb):
            us = slice(sb * MXU_DIM, (sb + 1) * MXU_DIM)
            ss = slice(sb * SB_STATE, (sb + 1) * SB_STATE)
            br = _dot(ub[:, us], bre_ref[sb], NN)
            bi = _dot(ub[:, us], bim_ref[sb], NN)
            bur[:, ss] = br
            bui[:, ss] = bi
            xr, xi = _cmul(cfr_ref[:, ss], cfi_ref[:, ss], br, bi)
            sr[:, ss] = xr
            si[:, ss] = xi
            lr[:, ss] = _dot(dyb[:, us], cre_ref[sb], NN)
            li[:, ss] = -_dot(dyb[:, us], cim_ref[sb], NN)
        abr, abi = abr_ref[...], abi_ref[...]
        apr, api = apr_ref[...], api_ref[...]
        _scan_segments(sr, si, abr, abi, apr, api, carry_r, carry_i, cm_r, cm_i, steps, False)
        for sb in range(nsb):
            us = slice(sb * MXU_DIM, (sb + 1) * MXU_DIM)
            ss = slice(sb * SB_STATE, (sb + 1) * SB_STATE)
            dcre_ref[sb] += _dot(dyb[:, us], sr[:, ss].astype(BF16), TN)
            dcim_ref[sb] -= _dot(dyb[:, us], si[:, ss].astype(BF16), TN)
        _scan_segments(lr, li, abr, -abi, apr, -api, lam_r, lam_i, cl_r, cl_i, steps, True)
        for c0 in range(0, n_state, SCAN_LANES):
            ls = slice(c0, c0 + SCAN_LANES)
            cfr = jnp.broadcast_to(cfr_ref[:, ls], (SUBLANES, SCAN_LANES))
            cfi = jnp.broadcast_to(cfi_ref[:, ls], (SUBLANES, SCAN_LANES))

            def step(j, acc, ls=ls, cfr=cfr, cfi=cfi):
                gar, gai, gcr, gci, pr, pi = acc
                r0 = pl.multiple_of(j * SUBLANES, SUBLANES)
                rws = pl.ds(r0, SUBLANES)
                l_r, l_i = lr[rws, ls], li[rws, ls]
                t_r, t_i = _cmul(pr, -pi, l_r, l_i)
                b_r, b_i = bur[rws, ls], bui[rws, ls]
                c_r, c_i = _cmul(b_r, -b_i, l_r, l_i)
                x_r, x_i = _cmul(cfr, -cfi, l_r, l_i)
                bur[rws, ls] = x_r
                bui[rws, ls] = x_i
                return gar + t_r, gai + t_i, gcr + c_r, gci + c_i, sr[rws, ls], si[rws, ls]

            zero = jnp.zeros((SUBLANES, SCAN_LANES), F32)
            gar, gai, gcr, gci, _, _ = lax.fori_loop(
                0, steps, step, (zero, zero, zero, zero, cm_r[:, ls], cm_i[:, ls]))
            gabr_ref[:, ls] += gar
            gabi_ref[:, ls] += gai
            gcfr_ref[:, ls] += gcr
            gcfi_ref[:, ls] += gci
        for sb in range(nsb):
            us = slice(sb * MXU_DIM, (sb + 1) * MXU_DIM)
            ss = slice(sb * SB_STATE, (sb + 1) * SB_STATE)
            xr, xi = bur[:, ss].astype(BF16), bui[:, ss].astype(BF16)
            du = _dot(xr, bre_ref[sb], NT) + _dot(xi, bim_ref[sb], NT)
            useg[:, us] = du + d_ref[:, us] * dy[:, us]
            dbre_ref[sb] += _dot(ub[:, us], xr, TN)
            dbim_ref[sb] += _dot(ub[:, us], xi, TN)
        _segments_to_rows(du_ref, useg, steps, stage)

        @pl.when(pl.program_id(0) == nc - 1)
        def _():
            for src, dst in ((dbre_ref, gb2r_ref), (dbim_ref, gb2i_ref), (dcre_ref, gc2r_ref), (dcim_ref, gc2i_ref)):
                for sb in range(nsb):
                    dst[sb * MXU_DIM:(sb + 1) * MXU_DIM, :] = _block_rows(src[sb])

    rev = lambda c: nc - 1 - c
    const = lambda a: pl.BlockSpec(a.shape, lambda c: (0,) * a.ndim)
    tile = pl.BlockSpec((chunk, w), lambda c: (rev(c), 0))
    row_n = pl.BlockSpec((1, n_state), lambda c: (0, 0))
    row_w = pl.BlockSpec((1, w), lambda c: (0, 0))
    st = pl.BlockSpec((1, 1, n_state), lambda c: (rev(c), 0, 0))
    acc8 = pl.BlockSpec((SUBLANES, n_state), lambda c: (0, 0))
    big = pltpu.VMEM((chunk, n_state), F32)
    small = pltpu.VMEM((chunk, w), F32)
    row = pltpu.VMEM((1, n_state), F32)
    eight = pltpu.VMEM((SUBLANES, n_state), F32)
    blk = (nsb, MXU_DIM, SB_STATE)
    held = [pltpu.VMEM(blk, BF16)] * 4 + [pltpu.VMEM(blk, F32)] * 4
    vmem = (4 * (_nbytes(blk, BF16) + _nbytes(blk, F32)) + 7 * _nbytes((chunk, n_state), F32)
            + 20 * _nbytes((chunk, w), F32) + 16 * _nbytes(bc_rows[0].shape, F32))
    res, landed = _call(
        body, [proj, proj, y, dyg, st_re, st_im, *bc_rows, *rows_p, d_row], name=name,
        out_shape=[jax.ShapeDtypeStruct((rows, w), F32)] + [jax.ShapeDtypeStruct(b.shape, F32) for b in bc_rows]
        + [jax.ShapeDtypeStruct((SUBLANES, n_state), F32)] * 4 + [jax.ShapeDtypeStruct((1, w), F32)],
        grid=(nc,),
        in_specs=_u_specs(w, o_u, chunk, rev) + [tile, tile, st, st] + [const(b) for b in bc_rows]
        + [row_n] * 6 + [row_w],
        out_specs=[tile] + [const(b) for b in bc_rows] + [acc8] * 4 + [row_w],
        scratch_shapes=held + [small] * 3 + [pltpu.VMEM((w // LANES, chunk, LANES), F32)] + [big] * 6 + [row] * 4
        + [eight] * 4,
        semantics=("arbitrary",), vmem=vmem, rider=rider)
    return res if rider is None else (res, landed)


def _loss_grad(x, mm, target, name):
    rows, d = x.shape

    def fn(xv, mv, tv):
        err = xv + mv - tv
        g = err * (1.0 / d)
        return g, g, 0.5 * err * g

    return _ew(fn, name=name, rows=rows, width=d, tiles=[(x, 0), (mm, 0), (target, 0)],
               outs=[(F32, d, 0), (BF16, d, 0)], accs=1)


def _pair_sum(grad, recv, name):
    r4, cdim = recv.shape
    r = r4 // N_CHIPS
    tr = _tile(r, 544, 16)
    g4 = grad.reshape(N_CHIPS, 2, r, cdim)
    r3 = recv.reshape(N_CHIPS, r, cdim)
    core = jnp.reshape(lax.axis_index("c"), (1,)).astype(jnp.int32)

    def body(c_ref, g_ref, r_ref, o_ref):
        o_ref[...] = (g_ref[0] + r_ref[...]).astype(BF16)

    out = _pallas(
        body, name=name, out_shape=jax.ShapeDtypeStruct((N_CHIPS, r, cdim), BF16),
        grid_spec=pltpu.PrefetchScalarGridSpec(
            num_scalar_prefetch=1, grid=(N_CHIPS, r // tr),
            in_specs=[pl.BlockSpec((1, 1, tr, cdim), lambda j, i, c: (j, c[0], i, 0)),
                      pl.BlockSpec((1, tr, cdim), lambda j, i, c: (j, i, 0))],
            out_specs=pl.BlockSpec((1, tr, cdim), lambda j, i, c: (j, i, 0))),
        compiler_params=_params(("parallel", "parallel"), 6 * _nbytes((tr, cdim), F32)),
    )(core, g4, r3)
    return out.reshape(r4, cdim)


def _chip_sum(recv, name):
    r4, cdim = recv.shape
    r = r4 // N_CHIPS
    tr = _tile(r, 544, 16)
    r3 = recv.reshape(N_CHIPS, r, cdim)

    def body(r_ref, o_ref):
        acc = r_ref[0].astype(F32)
        for j in range(1, N_CHIPS):
            acc = acc + r_ref[j].astype(F32)
        o_ref[...] = acc

    return _pallas(
        body, name=name, out_shape=jax.ShapeDtypeStruct((r, cdim), F32), grid=(r // tr,),
        in_specs=[pl.BlockSpec((N_CHIPS, tr, cdim), lambda i: (0, i, 0))],
        out_specs=pl.BlockSpec((tr, cdim), lambda i: (i, 0)),
        compiler_params=_params(("parallel",), 8 * _nbytes((tr, cdim), F32)),
    )(r3)


def _adamw_math(w, g, m, v):
    m = ADAM_B1 * m + (1.0 - ADAM_B1) * g
    v = ADAM_B2 * v + (1.0 - ADAM_B2) * (g * g)
    m_hat = m / (1.0 - ADAM_B1 ** ADAM_STEP)
    v_hat = v / (1.0 - ADAM_B2 ** ADAM_STEP)
    delta = -ADAM_LR * (m_hat / (jnp.sqrt(v_hat) + ADAM_EPS) + ADAM_WD * w)
    return delta, m, v


def _adamw(w, g, m, v, name):
    rows, cols = w.shape
    tr = _tile(rows, 256, SUBLANES)

    def body(w_ref, g_ref, m_ref, v_ref, d_ref, nm_ref, nv_ref):
        d, nm, nv = _adamw_math(w_ref[...], g_ref[...], m_ref[...], v_ref[...])
        d_ref[...] = d
        nm_ref[...] = nm
        nv_ref[...] = nv

    spec = pl.BlockSpec((tr, cols), lambda i: (i, 0))
    shp = jax.ShapeDtypeStruct((rows, cols), F32)
    return _pallas(
        body, name=name, out_shape=[shp] * 3, grid=(rows // tr,), in_specs=[spec] * 4, out_specs=[spec] * 3,
        compiler_params=_params(("parallel",)),
    )(w, g, m, v)


def _adamw_small(w, parts, m, v, name):
    rows, cols = w.shape
    p3 = parts.reshape(N_DEV, rows, cols)

    def body(w_ref, p_ref, m_ref, v_ref, g_ref, d_ref, nm_ref, nv_ref):
        g = p_ref[0]
        for k in range(1, N_DEV):
            g = g + p_ref[k]
        d, nm, nv = _adamw_math(w_ref[...], g, m_ref[...], v_ref[...])
        g_ref[...] = g
        d_ref[...] = d
        nm_ref[...] = nm
        nv_ref[...] = nv

    shp = jax.ShapeDtypeStruct((rows, cols), F32)
    return _pallas(body, name=name, out_shape=[shp] * 4)(w, p3, m, v)


SMALL = ("norm_w", "q_norm_w", "k_norm_w", "sinks", "A_re", "A_im", "log_dt", "B_re", "B_im", "C_re", "C_im",
         "D_skip", "b_glu")
LARGE = ("w_in", "w_attn_proj", "w_glu", "w_ssm_proj", "w_out")
ORDER = ("norm_w", "w_in", "q_norm_w", "k_norm_w", "sinks", "w_attn_proj", "A_re", "A_im", "log_dt", "B_re", "B_im",
         "C_re", "C_im", "D_skip", "w_glu", "b_glu", "w_ssm_proj", "w_out")


SMALL_REST = ("loss",) + SMALL[1:]


def _pack(named, keys):
    flat = jnp.concatenate([named[k].reshape(-1).astype(F32) for k in keys])
    n = flat.shape[0]
    rows = -(-n // (LANES * SUBLANES)) * SUBLANES
    return jnp.pad(flat, (0, rows * LANES - n)).reshape(rows, LANES)


def _unpack(packed, like, keys):
    flat = packed.reshape(-1)
    out, o = {}, 0
    for k in keys:
        n = like[k].size
        out[k] = flat[o:o + n].reshape(like[k].shape)
        o += n
    return out


def _step(xs, target, p, shards):
    s_in, s_ap, s_glu, s_sp, s_o = shards
    (w_in_t,) = _exchange(_all_gather([s_in]), "gather_w_in")
    seq, d = xs.shape
    attn_w = (d // 128) * HEAD_DIM
    n_q = attn_w // HEAD_DIM
    kv_w = N_KV_HEADS * HEAD_DIM
    ssm_w = d // 2
    n_groups = ssm_w // GROUP
    n_state = n_groups * STATE
    in_w = w_in_t.shape[0]
    assert in_w == 2 * attn_w + 2 * kv_w + 2 * ssm_w + 2 * d
    o_u = 2 * attn_w + 2 * kv_w
    o_z = o_u + ssm_w
    o_ga = o_z + ssm_w
    chunk = min(BLOCK, seq)
    cw = d // 4

    norm_row = p["norm_w"].reshape(1, d)
    h = _rmsnorm_fwd(xs, norm_row, "rmsnorm_fwd")
    proj, (w_ap_t, w_glu_t, w_sp_t, w_o) = _matmul(h, w_in_t, mode="nt", name="in_proj", tn=512, out_dtype=BF16,
                                                   rider=_all_gather([s_ap, s_glu, s_sp, s_o]))
    qw_row = jnp.tile(p["q_norm_w"], n_q).reshape(1, attn_w)
    kw_row = jnp.tile(p["k_norm_w"], N_KV_HEADS).reshape(1, kv_w)
    gmat = _head_mean_matrix()
    ag = _attention_fwd(proj, qw_row, kw_row, gmat, p["sinks"], attn_w=attn_w, kv_w=kv_w, name="attention_fwd")

    log_dt_col = p["log_dt"].reshape(n_groups, 1)
    prep = _ssm_prep(p["A_re"], p["A_im"], log_dt_col, chunk // SUBLANES, "ssm_prep")
    rows_p = [v.reshape(1, n_state) for v in prep]
    bc_rows = _ssm_rows(p["B_re"], p["B_im"], p["C_re"], p["C_im"])
    d_row = p["D_skip"].reshape(1, ssm_w)
    y_ssm, st_re, st_im = _ssm_fwd(proj, o_u, bc_rows, rows_p, d_row, chunk=chunk, name="ssm_fwd")
    (yg,) = _ew(_gelu, name="gelu", rows=seq, width=ssm_w, tiles=[(y_ssm, 0)], outs=[(BF16, ssm_w, 0)], cw=cw)
    glu = _matmul(yg, w_glu_t, mode="nt", name="glu_proj", out_dtype=BF16, bias=p["b_glu"].reshape(1, 2 * ssm_w))
    (ts,) = _ew(lambda ga, gb, z: ga * _sigmoid(gb) * _silu(z), name="glu_gate", rows=seq, width=ssm_w,
                tiles=[(glu, 0), (glu, ssm_w), (proj, o_z)], outs=[(BF16, ssm_w, 0)], cw=cw)
    yy = _matmul(ag, w_ap_t, mode="nt", name="attn_proj", out_dtype=BF16, out_cols=(2 * d, 0))
    yy = _matmul(ts, w_sp_t, mode="nt", name="ssm_proj", out_dtype=BF16, out_cols=(2 * d, d), into=yy)
    (merged,) = _ew(lambda ya, ys, ga, gs: _sigmoid(ga) * ya + _sigmoid(gs) * ys, name="merge", rows=seq, width=d,
                    tiles=[(yy, 0), (yy, d), (proj, o_ga), (proj, o_ga + d)], outs=[(BF16, d, 0)], cw=cw)
    mm = _matmul(merged, w_o, mode="nn", name="out_proj")
    dout, dout_b, loss_cols = _loss_grad(xs, mm, target, "loss_grad")
    loss_local = jnp.sum(loss_cols)

    g_w_o = _matmul(merged, dout_b, mode="tn", name="grad_w_out", tm=512, tk=4096)
    dmerged, (sib_o,) = _matmul(dout_b, w_o, mode="nt", name="d_merged", out_dtype=BF16,
                                rider=_sibling_exchange([g_w_o]))
    pair_o = _pair_sum(g_w_o, sib_o, "pair_sum_w_out")

    def merge_bwd(dm, y, g):
        s = _sigmoid(g)
        return dm * s, dm * y * s * (1.0 - s)

    dyy, dproj = _ew(merge_bwd, name="merge_bwd", rows=seq, width=2 * d,
                     tiles=[(dmerged, 0, d), (yy, 0), (proj, o_ga)],
                     outs=[(BF16, 2 * d, 0), (BF16, in_w, o_ga)], cw=cw)
    dy_a, dy_s = Cols(dyy, 0, d), Cols(dyy, d, d)
    g_w_ap_t = _matmul(dy_a, ag, mode="tn", name="grad_w_attn_proj", tm=512, tk=4096)
    g_w_sp_t = _matmul(dy_s, ts, mode="tn", name="grad_w_ssm_proj", tm=512, tk=4096)
    d_ag = _matmul(dy_a, w_ap_t, mode="nn", name="d_attn_gated", out_dtype=BF16)
    d_ts = _matmul(dy_s, w_sp_t, mode="nn", name="d_ssm_gated", out_dtype=BF16)

    (dproj, dkv, g_qw, g_kw, g_sinks), (chips_o, sib_ap, sib_sp) = _attention_bwd(
        proj, d_ag, dproj, qw_row, kw_row, gmat, p["sinks"], attn_w=attn_w, kv_w=kv_w, name="attention_bwd",
        rider=_join(_chip_exchange([pair_o]), _sibling_exchange([g_w_ap_t, g_w_sp_t])))
    pair_ap = _pair_sum(g_w_ap_t, sib_ap, "pair_sum_w_attn_proj")
    pair_sp = _pair_sum(g_w_sp_t, sib_sp, "pair_sum_w_ssm_proj")
    dproj = _attention_dkv(dproj, dkv, attn_w=attn_w, kv_w=kv_w, name="attention_dkv")

    n_half = ssm_w // _tile(2 * ssm_w, cw)

    def glu_bwd(j, dt, ga, gb, z):
        sb, sz = _sigmoid(gb), _silu(z)
        dg = jnp.where(j < n_half, dt * sb * sz, dt * ga * sb * (1.0 - sb) * sz)
        return dg, dg

    glu_ops = [(d_ts, 0, ssm_w), (glu, 0, ssm_w), (glu, ssm_w, ssm_w), (proj, o_z, ssm_w)]
    dglu, g_bglu = _ew(glu_bwd, name="glu_bwd", rows=seq, width=2 * ssm_w, tiles=glu_ops,
                       outs=[(BF16, 2 * ssm_w, 0)], accs=1, cw=cw, with_col=True)
    (dproj,) = _ew(lambda dt, ga, gb, z: dt * ga * _sigmoid(gb) * _dsilu(z), name="glu_bwd_z", rows=seq,
                   width=ssm_w, tiles=glu_ops, outs=[(BF16, in_w, o_z)], into=[dproj], cw=cw)
    g_w_glu_t = _matmul(dglu, yg, mode="tn", name="grad_w_glu", tm=512, tk=4096)
    d_yg = _matmul(dglu, w_glu_t, mode="nn", name="d_gelu", out_dtype=BF16)
    ((du, db_re, db_im, dc_re, dc_im, gabr, gabi, gcfr, gcfi, g_d), (chips_ap, chips_sp, sib_glu)) = _ssm_bwd(
        proj, o_u, y_ssm, d_yg, st_re, st_im, bc_rows, rows_p, d_row, chunk=chunk, name="ssm_bwd",
        rider=_join(_chip_exchange([pair_ap, pair_sp]), _sibling_exchange([g_w_glu_t])))
    pair_glu = _pair_sum(g_w_glu_t, sib_glu, "pair_sum_w_glu")
    (dproj,) = _ew(lambda v: v, name="du_store", rows=seq, width=ssm_w, tiles=[(du, 0)],
                   outs=[(BF16, in_w, o_u)], into=[dproj], cw=cw)
    g_a_re, g_a_im, g_log_dt = _ssm_param_bwd(
        p["A_re"], p["A_im"], log_dt_col, *[g.reshape(SUBLANES, n_groups, STATE) for g in (gabr, gabi, gcfr, gcfi)],
        "ssm_param_bwd")
    small_grads = dict(
        loss=loss_local, q_norm_w=g_qw.reshape(n_q, HEAD_DIM).sum(0), k_norm_w=g_kw.reshape(N_KV_HEADS, HEAD_DIM).sum(0),
        sinks=g_sinks[0, :n_q], A_re=g_a_re, A_im=g_a_im, log_dt=g_log_dt.reshape(n_groups),
        B_re=_from_ssm_rows(db_re, True), B_im=_from_ssm_rows(db_im, True),
        C_re=_from_ssm_rows(dc_re, False), C_im=_from_ssm_rows(dc_im, False),
        D_skip=g_d.reshape(n_groups, GROUP), b_glu=g_bglu.reshape(2 * ssm_w))

    n_parts = W_IN_GRAD_PARTS
    wq = d // n_parts
    g_parts, pair_parts, chip_parts = [], [], []
    extra = [_chip_exchange([pair_glu]), _all_gather([_pack(small_grads, SMALL_REST)])]
    chips_glu = small_parts = dh = grad_x = g_norm = None
    for step in range(n_parts + 2):
        riders = list(extra) if step == 0 else []
        if 0 <= step - 2 < n_parts:
            riders.append(_chip_exchange([pair_parts[step - 2]]))
        if 0 <= step - 1 < n_parts:
            riders.append(_sibling_exchange([g_parts[step - 1]]))
        rider = _join(*riders) if riders else None
        if step < n_parts:
            res = _matmul(dproj, Cols(h, step * wq, wq), mode="tn", name="grad_w_in_%d" % step, tk=4096, rider=rider)
            out, landed = res if rider is not None else (res, [])
            g_parts.append(out)
        elif step == n_parts:
            dh, landed = _matmul(dproj, w_in_t, mode="nn", name="d_normed", tk=2176, rider=rider)
        else:
            (grad_x, g_norm), landed = _rmsnorm_bwd(xs, norm_row, dh, dout, "rmsnorm_bwd", rider=rider)
        landed = list(landed)
        if step == 0:
            chips_glu, small_parts = landed[:2]
            landed = landed[2:]
        if 0 <= step - 2 < n_parts:
            chip_parts.append(landed.pop(0))
        if 0 <= step - 1 < n_parts:
            pair_parts.append(_pair_sum(g_parts[step - 1], landed.pop(0), "pair_sum_w_in_%d" % (step - 1)))
    (norm_parts,) = _exchange(_all_gather([_pack(dict(norm_w=g_norm), ("norm_w",))]), "gather_norm_grad")
    g_in = jnp.concatenate([_chip_sum(c, "chip_sum_w_in_%d" % q) for q, c in enumerate(chip_parts)], axis=1)
    summed = [g_in] + [_chip_sum(c, "chip_sum_" + k)
                       for k, c in zip(LARGE[1:], (chips_ap, chips_glu, chips_sp, chips_o))]
    return grad_x, summed, small_parts, norm_parts


def kernel(x, norm_w, w_in, q_norm_w, k_norm_w, sinks, w_attn_proj, A_re, A_im, log_dt, B_re, B_im, C_re, C_im, D_skip, w_glu, b_glu, w_ssm_proj, w_out, loss_target, m_norm_w, m_w_in, m_q_norm_w, m_k_norm_w, m_sinks, m_w_attn_proj, m_A_re, m_A_im, m_log_dt, m_B_re, m_B_im, m_C_re, m_C_im, m_D_skip, m_w_glu, m_b_glu, m_w_ssm_proj, m_w_out, v_norm_w, v_w_in, v_q_norm_w, v_k_norm_w, v_sinks, v_w_attn_proj, v_A_re, v_A_im, v_log_dt, v_B_re, v_B_im, v_C_re, v_C_im, v_D_skip, v_w_glu, v_b_glu, v_w_ssm_proj, v_w_out):
    weights = dict(norm_w=norm_w, w_in=w_in, q_norm_w=q_norm_w, k_norm_w=k_norm_w, sinks=sinks,
                   w_attn_proj=w_attn_proj, A_re=A_re, A_im=A_im, log_dt=log_dt, B_re=B_re, B_im=B_im, C_re=C_re,
                   C_im=C_im, D_skip=D_skip, w_glu=w_glu, b_glu=b_glu, w_ssm_proj=w_ssm_proj, w_out=w_out)
    m_in = dict(norm_w=m_norm_w, w_in=m_w_in, q_norm_w=m_q_norm_w, k_norm_w=m_k_norm_w, sinks=m_sinks,
                w_attn_proj=m_w_attn_proj, A_re=m_A_re, A_im=m_A_im, log_dt=m_log_dt, B_re=m_B_re, B_im=m_B_im,
                C_re=m_C_re, C_im=m_C_im, D_skip=m_D_skip, w_glu=m_w_glu, b_glu=m_b_glu, w_ssm_proj=m_w_ssm_proj,
                w_out=m_w_out)
    v_in = dict(norm_w=v_norm_w, w_in=v_w_in, q_norm_w=v_q_norm_w, k_norm_w=v_k_norm_w, sinks=v_sinks,
                w_attn_proj=v_w_attn_proj, A_re=v_A_re, A_im=v_A_im, log_dt=v_log_dt, B_re=v_B_re, B_im=v_B_im,
                C_re=v_C_re, C_im=v_C_im, D_skip=v_D_skip, w_glu=v_w_glu, b_glu=v_b_glu, w_ssm_proj=v_w_ssm_proj,
                w_out=v_w_out)

    _, seq, d = x.shape
    column_sharded = LARGE[:4]
    as_rows = lambda k, a: a.T if k in column_sharded else a
    shards = [as_rows(k, weights[k]).astype(BF16) for k in LARGE]
    small = {k: weights[k] for k in SMALL}
    grad_x, summed, small_parts, norm_parts = _step(x.reshape(seq, d), loss_target.reshape(seq, d), small, shards)

    grads, delta, new_m, new_v = {}, {}, {}, {}
    for k, g in zip(LARGE, summed):
        if k == "w_in":
            upd = _adamw(weights[k].T, g, m_in[k].T, v_in[k].T, "adamw_" + k)
            grads[k], delta[k], new_m[k], new_v[k] = [a.T for a in (g, *upd)]
        else:
            grads[k] = as_rows(k, g)
            delta[k], new_m[k], new_v[k] = _adamw(weights[k], grads[k], m_in[k], v_in[k], "adamw_" + k)

    zero = jnp.zeros((), F32)
    for keys, parts in ((SMALL_REST, small_parts), (("norm_w",), norm_parts)):
        like = dict(small, loss=zero)
        packs = [_pack(dict(src, loss=zero), keys) for src in (weights, m_in, v_in)]
        res = _adamw_small(packs[0], parts, packs[1], packs[2], "adamw_small_%d" % len(keys))
        for dst, r in zip((grads, delta, new_m, new_v), res):
            dst.update(_unpack(r, like, keys))
    loss = grads["loss"]

    return (loss, grad_x.reshape(x.shape), *[grads[k] for k in ORDER], *[delta[k] for k in ORDER],
            *[new_m[k] for k in ORDER], *[new_v[k] for k in ORDER])
```

```python
import math
from typing import Callable, NamedTuple

import jax
import jax.numpy as jnp
import numpy as np
from jax import lax
from jax.experimental import pallas as pl
from jax.experimental.pallas import tpu as pltpu

F32 = jnp.float32
BF16 = jnp.bfloat16
MESH = pl.DeviceIdType.MESH

HEAD_DIM = 64
N_KV_HEADS = 4
GROUP = 16
STATE = 64
BLOCK = 128
NORM_EPS = 1e-6
N_DEV = 8
N_CHIPS = 4
LANES = 128
SUBLANES = 8
MXU_DIM = 256
VMEM_BYTES = 64 * 1024 * 1024
VMEM_CAP = VMEM_BYTES - 8 * 1024 * 1024

ADAM_LR = 0.001
ADAM_B1 = 0.9
ADAM_B2 = 0.999
ADAM_EPS = 1e-08
ADAM_WD = 0.01
ADAM_STEP = 10

GELU_C = math.sqrt(2.0 / math.pi)
GELU_K = 0.044715


def _tile(dim, pref, mult=LANES):
    if dim <= pref:
        return dim
    best = None
    for d in range(mult, pref + 1, mult):
        if dim % d == 0:
            best = d
    assert best is not None, (dim, pref, mult)
    return best


def _params(semantics=None, vmem=None):
    kw = {}
    if semantics is not None:
        kw["dimension_semantics"] = semantics
    if vmem is not None:
        kw["vmem_limit_bytes"] = int(min(VMEM_CAP, max(vmem, 32 * 1024 * 1024)))
    return pltpu.CompilerParams(**kw)


def _nbytes(shape, dtype):
    return math.prod(shape) * jnp.dtype(dtype).itemsize


def _sigmoid(x):
    return 1.0 / (1.0 + jnp.exp(-x))


def _silu(x):
    return x * _sigmoid(x)


def _dsilu(x):
    s = _sigmoid(x)
    return s * (1.0 + x * (1.0 - s))


def _gelu(x):
    return 0.5 * x * (1.0 + jnp.tanh(GELU_C * (x + GELU_K * x * x * x)))


def _dgelu(x):
    t = jnp.tanh(GELU_C * (x + GELU_K * x * x * x))
    return 0.5 * (1.0 + t) + 0.5 * x * (1.0 - t * t) * GELU_C * (1.0 + 3.0 * GELU_K * x * x)


def _dot(a, b, dims):
    return lax.dot_general(a, b, (dims, ((), ())), preferred_element_type=F32)


NN = ((1,), (0,))
NT = ((1,), (1,))
TN = ((0,), (0,))


def _any_spec():
    return pl.BlockSpec(memory_space=pl.ANY)


def _pallas(body, **kw):
    pin = lambda s: pltpu.HBM(s.shape, s.dtype) if isinstance(s, jax.ShapeDtypeStruct) else s
    out_shape = kw.pop("out_shape")
    out_shape = [pin(s) for s in out_shape] if isinstance(out_shape, (list, tuple)) else pin(out_shape)
    call = pl.pallas_call(body, out_shape=out_shape, **kw)

    def run(*operands):
        pinned = [pltpu.with_memory_space_constraint(o, pltpu.HBM) if jnp.issubdtype(o.dtype, jnp.floating) else o
                  for o in operands]
        return call(*pinned)

    return run


class Rider(NamedTuple):
    operands: tuple
    out_shapes: tuple
    sems: tuple
    start: Callable
    finish: Callable


def _all_gather(shards):
    n = len(shards)

    def copies(ins, outs, sems):
        send_sems, recv_sems, local_sems = sems
        x, y, c = lax.axis_index("x"), lax.axis_index("y"), lax.axis_index("c")
        me, sibling = (x, y, c), (x, y, 1 - c)
        chips = [(1 - x, y), (x, 1 - y), (1 - x, 1 - y)]

        def rows(k, px, py, pc):
            r = shards[k].shape[0]
            return outs[k].at[pl.ds((4 * px + 2 * py + pc) * r, r), :]

        def copy(k, s, block, to, src=None):
            return pltpu.make_async_remote_copy(
                src_ref=rows(k, *block) if src is None else src, dst_ref=rows(k, *block),
                send_sem=send_sems.at[7 * k + s], recv_sem=recv_sems.at[7 * k + s],
                device_id=to, device_id_type=MESH)

        mine = [pltpu.make_async_copy(ins[k], rows(k, *me), local_sems.at[k]) for k in range(n)]
        first = []
        for k in range(n):
            first.append(copy(k, 0, me, sibling, src=ins[k]))
            first += [copy(k, 1 + j, me, (*chip, c), src=ins[k]) for j, chip in enumerate(chips)]
        return me, sibling, chips, c, copy, mine, first

    def start(ins, outs, sems):
        *_, mine, first = copies(ins, outs, sems)
        for cp in mine + first:
            cp.start()

    def finish(ins, outs, sems):
        me, sibling, chips, c, copy, mine, first = copies(ins, outs, sems)
        passed = []
        for j, chip in enumerate(chips):
            for k in range(n):
                copy(k, 1 + j, (*chip, c), me).wait_recv()
                fwd = copy(k, 4 + j, (*chip, c), sibling)
                fwd.start()
                passed.append(fwd)
        for k in range(n):
            copy(k, 0, sibling, me).wait_recv()
            for j, chip in enumerate(chips):
                copy(k, 4 + j, (*chip, 1 - c), me).wait_recv()
        for cp in first + passed:
            cp.wait_send()
        for cp in mine:
            cp.wait()

    return Rider(
        tuple(shards),
        tuple(jax.ShapeDtypeStruct((N_DEV * s.shape[0], s.shape[1]), s.dtype) for s in shards),
        (pltpu.SemaphoreType.DMA((7 * n,)), pltpu.SemaphoreType.DMA((7 * n,)), pltpu.SemaphoreType.DMA((n,))),
        start, finish)


def _sibling_exchange(grads):
    n = len(grads)

    def copies(ins, outs, sems):
        send_sems, recv_sems = sems
        x, y, c = lax.axis_index("x"), lax.axis_index("y"), lax.axis_index("c")
        out = []
        for k in range(n):
            r = grads[k].shape[0] // N_DEV
            for j in range(N_CHIPS):
                out.append(pltpu.make_async_remote_copy(
                    src_ref=ins[k].at[pl.ds((2 * j + 1 - c) * r, r), :],
                    dst_ref=outs[k].at[pl.ds(j * r, r), :],
                    send_sem=send_sems.at[N_CHIPS * k + j], recv_sem=recv_sems.at[N_CHIPS * k + j],
                    device_id=(x, y, 1 - c), device_id_type=MESH))
        return out

    def start(ins, outs, sems):
        for cp in copies(ins, outs, sems):
            cp.start()

    def finish(ins, outs, sems):
        for cp in copies(ins, outs, sems):
            cp.wait()

    return Rider(
        tuple(grads), tuple(jax.ShapeDtypeStruct((g.shape[0] // 2, g.shape[1]), g.dtype) for g in grads),
        (pltpu.SemaphoreType.DMA((N_CHIPS * n,)), pltpu.SemaphoreType.DMA((N_CHIPS * n,))), start, finish)


def _chip_exchange(parts):
    n = len(parts)

    def copies(ins, outs, sems):
        send_sems, recv_sems, local_sems = sems
        x, y, c = lax.axis_index("x"), lax.axis_index("y"), lax.axis_index("c")
        my_chip = 2 * x + y
        chips = [(1 - x, y), (x, 1 - y), (1 - x, 1 - y)]
        local, sent = [], []
        for k in range(n):
            r = parts[k].shape[0] // N_CHIPS
            mine = pl.ds(my_chip * r, r)
            local.append(pltpu.make_async_copy(ins[k].at[mine, :], outs[k].at[mine, :], local_sems.at[k]))
            for s, (px, py) in enumerate(chips):
                sent.append(pltpu.make_async_remote_copy(
                    src_ref=ins[k].at[pl.ds((2 * px + py) * r, r), :], dst_ref=outs[k].at[mine, :],
                    send_sem=send_sems.at[3 * k + s], recv_sem=recv_sems.at[3 * k + s],
                    device_id=(px, py, c), device_id_type=MESH))
        return local, sent

    def start(ins, outs, sems):
        local, sent = copies(ins, outs, sems)
        for cp in local + sent:
            cp.start()

    def finish(ins, outs, sems):
        local, sent = copies(ins, outs, sems)
        for cp in sent + local:
            cp.wait()

    return Rider(
        tuple(parts), tuple(jax.ShapeDtypeStruct(p.shape, p.dtype) for p in parts),
        (pltpu.SemaphoreType.DMA((3 * n,)), pltpu.SemaphoreType.DMA((3 * n,)), pltpu.SemaphoreType.DMA((n,))),
        start, finish)


def _join(*riders):
    cuts_in, cuts_out, cuts_sem = [0], [0], [0]
    for r in riders:
        cuts_in.append(cuts_in[-1] + len(r.operands))
        cuts_out.append(cuts_out[-1] + len(r.out_shapes))
        cuts_sem.append(cuts_sem[-1] + len(r.sems))

    def each(which):
        def run(ins, outs, sems):
            for i, r in enumerate(riders):
                getattr(r, which)(ins[cuts_in[i]:cuts_in[i + 1]], outs[cuts_out[i]:cuts_out[i + 1]],
                                  sems[cuts_sem[i]:cuts_sem[i + 1]])
        return run

    return Rider(sum((r.operands for r in riders), ()), sum((r.out_shapes for r in riders), ()),
                 sum((r.sems for r in riders), ()), each("start"), each("finish"))


def _call(body, operands, *, name, out_shape, grid, in_specs, out_specs, scratch_shapes=(), aliases=None,
          semantics=None, vmem=None, rider=None):
    operands, out_shape, scratch_shapes = list(operands), list(out_shape), list(scratch_shapes)
    in_specs, out_specs = list(in_specs), list(out_specs)
    if rider is None:
        res = _pallas(
            body, name=name, out_shape=out_shape, grid=grid, in_specs=in_specs, out_specs=out_specs,
            scratch_shapes=scratch_shapes, input_output_aliases=aliases or {},
            compiler_params=_params(semantics, vmem))(*operands)
        return list(res), []
    n_in, n_out, n_scr = len(operands), len(out_shape), len(scratch_shapes)
    ri, ro = len(rider.operands), len(rider.out_shapes)

    def carried(*refs):
        a, b = n_in, n_in + ri
        c, d = b + n_out, b + n_out + ro
        e = d + n_scr
        ids = [pl.program_id(k) for k in range(len(grid))]
        first = ids[0] == 0
        last = ids[0] == grid[0] - 1
        for k in range(1, len(grid)):
            first = jnp.logical_and(first, ids[k] == 0)
            last = jnp.logical_and(last, ids[k] == grid[k] - 1)

        @pl.when(first)
        def _():
            rider.start(refs[a:b], refs[c:d], refs[e:])

        body(*refs[:a], *refs[b:c], *refs[d:e])

        @pl.when(last)
        def _():
            rider.finish(refs[a:b], refs[c:d], refs[e:])

    res = _pallas(
        carried, name=name, out_shape=out_shape + list(rider.out_shapes), grid=grid,
        in_specs=in_specs + [_any_spec()] * ri, out_specs=out_specs + [_any_spec()] * ro,
        scratch_shapes=scratch_shapes + list(rider.sems), input_output_aliases=aliases or {},
        compiler_params=_params(("arbitrary",) * len(grid), vmem))(*operands, *rider.operands)
    return list(res[:n_out]), list(res[n_out:])


def _exchange(rider, name):
    ri, ro = len(rider.operands), len(rider.out_shapes)

    def body(*refs):
        rider.start(refs[:ri], refs[ri:ri + ro], refs[ri + ro:])
        rider.finish(refs[:ri], refs[ri:ri + ro], refs[ri + ro:])

    return _pallas(
        body, name=name, out_shape=list(rider.out_shapes), in_specs=[_any_spec()] * ri,
        out_specs=[_any_spec()] * ro, scratch_shapes=list(rider.sems))(*rider.operands)


class Cols(NamedTuple):
    arr: jax.Array
    off: int
    width: int


def _cols(a):
    return a if isinstance(a, Cols) else Cols(a, 0, a.shape[1])


def _matmul(a, b, *, mode, name, out_dtype=F32, tm=1024, tn=1024, tk=2048, bias=None, out_cols=None, into=None,
            rider=None):
    a, b = _cols(a), _cols(b)
    if mode == "nn":
        (m, k), (k2, n) = (a.arr.shape[0], a.width), (b.arr.shape[0], b.width)
    elif mode == "nt":
        (m, k), (n, k2) = (a.arr.shape[0], a.width), (b.arr.shape[0], b.width)
    else:
        (k, m), (k2, n) = (a.arr.shape[0], a.width), (b.arr.shape[0], b.width)
    assert k == k2, (a.arr.shape, b.arr.shape, mode)
    tm, tn, tk = _tile(m, tm), _tile(n, tn), _tile(k, tk)
    nk = k // tk
    dims = {"nn": NN, "nt": NT, "tn": TN}[mode]
    if mode == "tn":
        assert a.off % tm == 0
        a_spec = pl.BlockSpec((tk, tm), lambda i, j, kk, o=a.off // tm: (kk, i + o))
    else:
        assert a.off % tk == 0
        a_spec = pl.BlockSpec((tm, tk), lambda i, j, kk, o=a.off // tk: (i, kk + o))
    if mode == "nt":
        assert b.off % tk == 0
        b_spec = pl.BlockSpec((tn, tk), lambda i, j, kk, o=b.off // tk: (j, kk + o))
    else:
        assert b.off % tn == 0
        b_spec = pl.BlockSpec((tk, tn), lambda i, j, kk, o=b.off // tn: (kk, j + o))
    in_specs, operands = [a_spec, b_spec], [a.arr, b.arr]
    if bias is not None:
        in_specs.append(pl.BlockSpec((1, tn), lambda i, j, kk: (0, j)))
        operands.append(bias)
    total_w, o_off = out_cols if out_cols is not None else (n, 0)
    assert o_off % tn == 0
    aliases = {}
    if into is not None:
        assert into.shape == (m, total_w) and into.dtype == out_dtype
        in_specs.append(_any_spec())
        operands.append(into)
        aliases = {len(operands) - 1: 0}
    n_in = len(operands)

    def body(*refs):
        a_ref, b_ref = refs[0], refs[1]
        bias_ref = refs[2] if bias is not None else None
        o_ref = refs[n_in]
        acc_ref = refs[-1] if nk > 1 else None
        part = _dot(a_ref[...].astype(BF16), b_ref[...].astype(BF16), dims)

        def finish(acc):
            if bias_ref is not None:
                acc = acc + bias_ref[...]
            o_ref[...] = acc.astype(out_dtype)

        if nk == 1:
            finish(part)
        else:
            kk = pl.program_id(2)

            @pl.when(kk == 0)
            def _():
                acc_ref[...] = part

            @pl.when(kk > 0)
            def _():
                acc_ref[...] += part

            @pl.when(kk == nk - 1)
            def _():
                finish(acc_ref[...])

    vmem = 2 * (_nbytes((tm, tk), a.arr.dtype) + _nbytes((tk, tn), b.arr.dtype) + _nbytes((tm, tn), out_dtype))
    vmem += 3 * _nbytes((tm, tn), F32)
    (out,), landed = _call(
        body, operands, name=name, out_shape=[jax.ShapeDtypeStruct((m, total_w), out_dtype)],
        grid=(m // tm, n // tn, nk), in_specs=in_specs,
        out_specs=[pl.BlockSpec((tm, tn), lambda i, j, kk, o=o_off // tn: (i, j + o))],
        scratch_shapes=[pltpu.VMEM((tm, tn), F32)] if nk > 1 else [], aliases=aliases,
        semantics=("parallel", "parallel", "arbitrary"), vmem=vmem, rider=rider)
    return out if rider is None else (out, landed)


IN_PROJ_TILE = 256
GATHER_SLOTS = 8


def _gather_order(n_tiles, tile, shard):
    priority = [0, 1, 2, 5, 3, 6, 4, 7]
    rank = {s: k for k, s in enumerate(priority)}
    order = np.zeros((N_DEV, n_tiles), np.int32)
    flags = np.zeros((N_DEV, n_tiles, GATHER_SLOTS), np.int32)
    for dev in range(N_DEV):
        x, y, c = dev >> 2, (dev >> 1) & 1, dev & 1
        blocks = [(x, y, c), (x, y, 1 - c), (1 - x, y, c), (x, 1 - y, c), (1 - x, 1 - y, c),
                  (1 - x, y, 1 - c), (x, 1 - y, 1 - c), (1 - x, 1 - y, 1 - c)]
        slot_of = {4 * bx + 2 * by + bc: s for s, (bx, by, bc) in enumerate(blocks)}

        def needs(t):
            return sorted({slot_of[(t * tile) // shard], slot_of[(t * tile + tile - 1) // shard]})

        seen = set()
        for step, t in enumerate(sorted(range(n_tiles), key=lambda t: (max(rank[s] for s in needs(t)), t))):
            order[dev, step] = t
            for s in needs(t):
                if s not in seen:
                    flags[dev, step, s] = 1
                    seen.add(s)
        assert len(seen) == GATHER_SLOTS
    return order, flags


def _gather_in_proj(h, shard, *, name, rider):
    rows, d = h.shape
    r = shard.shape[0]
    in_w = N_DEV * r
    tn = IN_PROJ_TILE
    n_tiles = in_w // tn
    tm = _tile(rows, 1024, SUBLANES)
    order, flags = _gather_order(n_tiles, tn, r)
    dev = 4 * lax.axis_index("x") + 2 * lax.axis_index("y") + lax.axis_index("c")
    my_order = jnp.asarray(order)[dev]
    my_flags = jnp.asarray(flags.reshape(N_DEV, -1))[dev]

    def body(order_ref, flags_ref, h_ref, shard_hbm, proj_hbm, w_hbm, b_buf, o_buf, send_sems, recv_sems, local_sem,
             b_sem, o_sems):
        s = pl.program_id(0)
        x, y, c = lax.axis_index("x"), lax.axis_index("y"), lax.axis_index("c")
        me, sibling = (x, y, c), (x, y, 1 - c)
        chips = [(1 - x, y), (x, 1 - y), (1 - x, 1 - y)]

        def block(px, py, pc):
            return w_hbm.at[pl.ds((4 * px + 2 * py + pc) * r, r), :]

        def copy(k, blk, to, src=None):
            return pltpu.make_async_remote_copy(
                src_ref=block(*blk) if src is None else src, dst_ref=block(*blk),
                send_sem=send_sems.at[k], recv_sem=recv_sems.at[k], device_id=to, device_id_type=MESH)

        mine = pltpu.make_async_copy(shard_hbm, block(*me), local_sem)
        first = [copy(0, me, sibling, src=shard_hbm)]
        first += [copy(1 + j, me, (*chip, c), src=shard_hbm) for j, chip in enumerate(chips)]
        passed = [copy(4 + j, (*chip, c), sibling) for j, chip in enumerate(chips)]

        @pl.when(s == 0)
        def _():
            for cp in [mine] + first:
                cp.start()

        def due(slot):
            return flags_ref[s * GATHER_SLOTS + slot] == 1

        @pl.when(due(0))
        def _():
            mine.wait()

        @pl.when(due(1))
        def _():
            copy(0, sibling, me).wait_recv()

        for j, chip in enumerate(chips):
            @pl.when(due(2 + j))
            def _(j=j, chip=chip):
                copy(1 + j, (*chip, c), me).wait_recv()
                passed[j].start()

            @pl.when(due(5 + j))
            def _(j=j, chip=chip):
                copy(4 + j, (*chip, 1 - c), me).wait_recv()

        col = pl.multiple_of(order_ref[s] * tn, tn)
        fetch = pltpu.make_async_copy(w_hbm.at[pl.ds(col, tn), :], b_buf, b_sem)
        fetch.start()
        slot = s % 2

        def put(at):
            return pltpu.make_async_copy(o_buf.at[slot], proj_hbm.at[:, pl.ds(at, tn)], o_sems.at[slot])

        @pl.when(s >= 2)
        def _():
            put(0).wait()

        fetch.wait()
        for m in range(rows // tm):
            rs = slice(m * tm, (m + 1) * tm)
            o_buf[slot, rs, :] = _dot(h_ref[rs, :], b_buf[...], NT).astype(BF16)
        put(col).start()

        @pl.when(s == n_tiles - 1)
        def _():
            pltpu.make_async_copy(o_buf.at[1 - slot], proj_hbm.at[:, pl.ds(0, tn)], o_sems.at[1 - slot]).wait()
            put(0).wait()
            for cp in first + passed:
                cp.wait_send()

    smem = pl.BlockSpec(memory_space=pltpu.SMEM)
    vmem = 2 * _nbytes((rows, d), BF16) + 3 * _nbytes((rows, tn), BF16) + _nbytes((tn, d), BF16) + 4 * _nbytes((tm, tn), F32)
    (proj, w_full), landed = _call(
        body, [my_order, my_flags, h, shard], name=name,
        out_shape=[jax.ShapeDtypeStruct((rows, in_w), BF16), jax.ShapeDtypeStruct((in_w, d), BF16)],
        grid=(n_tiles,), in_specs=[smem, smem, pl.BlockSpec((rows, d), lambda s: (0, 0)), _any_spec()],
        out_specs=[_any_spec(), _any_spec()],
        scratch_shapes=[pltpu.VMEM((tn, d), BF16), pltpu.VMEM((2, rows, tn), BF16),
                        pltpu.SemaphoreType.DMA((7,)), pltpu.SemaphoreType.DMA((7,)), pltpu.SemaphoreType.DMA,
                        pltpu.SemaphoreType.DMA, pltpu.SemaphoreType.DMA((2,))],
        semantics=("arbitrary",), vmem=vmem, rider=rider)
    return proj, w_full, landed


def _ew(fn, *, name, rows, width, tiles, vecs=(), outs, accs=0, tl=1024, cw=512, into=None, with_col=False):
    tl, cw = _tile(rows, tl, SUBLANES), _tile(width, cw)
    ncol = width // cw
    nt_, nv = len(tiles), len(vecs)
    into = list(into) if into is not None else [None] * len(outs)
    aliased = [t for t in into if t is not None]

    def off(o):
        assert o % cw == 0, (name, o, cw)
        return o // cw

    in_specs, vmem = [], 0
    for t in tiles:
        arr, o = t[0], off(t[1])
        wrap = t[2] // cw if len(t) > 2 else ncol
        in_specs.append(pl.BlockSpec((tl, cw), lambda j, i, o=o, wrap=wrap: (i, o + j % wrap)))
        vmem += _nbytes((tl, cw), arr.dtype)
    in_specs += [pl.BlockSpec((1, cw), lambda j, i, o=off(o): (0, j + o)) for _, o in vecs]
    in_specs += [_any_spec() for _ in aliased]
    out_shape, out_specs, aliases = [], [], {}
    n_in = nt_ + nv
    for idx, ((dt, tw, o), tgt) in enumerate(zip(outs, into)):
        out_shape.append(jax.ShapeDtypeStruct((rows, tw), dt))
        out_specs.append(pl.BlockSpec((tl, cw), lambda j, i, o=off(o): (i, j + o)))
        vmem += _nbytes((tl, cw), dt)
        if tgt is not None:
            assert tgt.shape == (rows, tw) and tgt.dtype == dt, (name, tgt.shape, tgt.dtype)
            aliases[n_in + len(aliases)] = idx
    for _ in range(accs):
        out_shape.append(jax.ShapeDtypeStruct((1, width), F32))
        out_specs.append(pl.BlockSpec((1, cw), lambda j, i: (0, j)))
    n_out = len(outs)

    def body(*refs):
        vals = [r[...].astype(F32) for r in refs[:n_in]]
        out_refs = refs[n_in + len(aliased):]
        res = fn(pl.program_id(0), *vals) if with_col else fn(*vals)
        res = res if isinstance(res, (tuple, list)) else (res,)
        assert len(res) == n_out + accs, (name, len(res))
        for r, v in zip(out_refs[:n_out], res[:n_out]):
            r[...] = v.astype(r.dtype)
        first = pl.program_id(1) == 0
        for r, v in zip(out_refs[n_out:], res[n_out:]):
            s = jnp.sum(v, axis=0, keepdims=True)

            @pl.when(first)
            def _(r=r, s=s):
                r[...] = s

            @pl.when(jnp.logical_not(first))
            def _(r=r, s=s):
                r[...] += s

    return _pallas(
        body, name=name, out_shape=out_shape, grid=(ncol, rows // tl),
        in_specs=in_specs, out_specs=out_specs, input_output_aliases=aliases,
        compiler_params=_params(("parallel", "arbitrary"), 3 * vmem),
    )(*[t[0] for t in tiles], *[v for v, _ in vecs], *aliased)


def _rmsnorm_fwd(x, w_row, name):
    rows, d = x.shape
    tl = _tile(rows, 512, SUBLANES)

    def body(x_ref, w_ref, h_ref):
        xv = x_ref[...]
        rstd = lax.rsqrt(jnp.mean(xv * xv, axis=-1, keepdims=True) + NORM_EPS)
        h_ref[...] = (xv * rstd * w_ref[...]).astype(BF16)

    return _pallas(
        body, name=name, out_shape=jax.ShapeDtypeStruct((rows, d), BF16), grid=(rows // tl,),
        in_specs=[pl.BlockSpec((tl, d), lambda i: (i, 0)), pl.BlockSpec((1, d), lambda i: (0, 0))],
        out_specs=pl.BlockSpec((tl, d), lambda i: (i, 0)),
        compiler_params=_params(("parallel",)),
    )(x, w_row)


def _rmsnorm_bwd(x, w_row, dh, dout, name, rider=None):
    rows, d = x.shape
    tl = _tile(rows, 256, SUBLANES)

    def body(x_ref, w_ref, dh_ref, dout_ref, gx_ref, gw_ref):
        xv = x_ref[...]
        rstd = lax.rsqrt(jnp.mean(xv * xv, axis=-1, keepdims=True) + NORM_EPS)
        xn = xv * rstd
        dhv = dh_ref[...]
        dxn = dhv * w_ref[...]
        dx = rstd * (dxn - xn * jnp.mean(dxn * xn, axis=-1, keepdims=True))
        gx_ref[...] = dout_ref[...] + dx
        gw = jnp.sum(dhv * xn, axis=0, keepdims=True)

        @pl.when(pl.program_id(0) == 0)
        def _():
            gw_ref[...] = gw

        @pl.when(pl.program_id(0) > 0)
        def _():
            gw_ref[...] += gw

    tile = pl.BlockSpec((tl, d), lambda i: (i, 0))
    row = pl.BlockSpec((1, d), lambda i: (0, 0))
    res, landed = _call(
        body, [x, w_row, dh, dout], name=name,
        out_shape=[jax.ShapeDtypeStruct((rows, d), F32), jax.ShapeDtypeStruct((1, d), F32)],
        grid=(rows // tl,), in_specs=[tile, row, tile, tile], out_specs=[tile, row],
        semantics=("arbitrary",), rider=rider)
    return res if rider is None else (res, landed)


def _head_mean(x, gmat):
    hi = x.astype(BF16)
    lo = (x - hi.astype(F32)).astype(BF16)
    out = []
    for s in range(x.shape[1] // MXU_DIM):
        sl = slice(s * MXU_DIM, (s + 1) * MXU_DIM)
        out.append(_dot(hi[:, sl], gmat, NN) + _dot(lo[:, sl], gmat, NN))
    return out[0] if len(out) == 1 else jnp.concatenate(out, axis=1)


def _head_mean_matrix():
    blk = jnp.arange(MXU_DIM) // HEAD_DIM
    return jnp.where(blk[:, None] == blk[None, :], 1.0 / HEAD_DIM, 0.0).astype(BF16)


def _spread_head(x, g, width):
    col = x[:, (g // 2) * LANES:(g // 2 + 1) * LANES]
    other = pltpu.roll(col, HEAD_DIM, axis=1)
    low = lax.broadcasted_iota(jnp.int32, col.shape, 1) < HEAD_DIM
    both = jnp.where(low, col, other) if g % 2 == 0 else jnp.where(low, other, col)
    return both if width == LANES else jnp.concatenate([both] * (width // LANES), axis=1)


def _head_diagonal(t, per_kv):
    head = lax.broadcasted_iota(jnp.int32, t.shape, 1) // HEAD_DIM
    zero = jnp.zeros_like(t)
    return jnp.concatenate([jnp.where(head == r, t, zero) for r in range(per_kv)], axis=0)


def _fold_heads(x, per_kv):
    rows = x.shape[0] // per_kv
    head = lax.broadcasted_iota(jnp.int32, (rows, x.shape[1]), 1) // HEAD_DIM
    acc = jnp.where(head == 0, x[0:rows], 0.0)
    for r in range(1, per_kv):
        acc = acc + jnp.where(head == r, x[r * rows:(r + 1) * rows], 0.0)
    while acc.shape[1] > LANES:
        half = acc.shape[1] // 2
        acc = acc[:, :half] + acc[:, half:]
    return acc + pltpu.roll(acc, HEAD_DIM, axis=1)


def _join_heads(parts):
    low = lax.broadcasted_iota(jnp.int32, parts[0].shape, 1) < HEAD_DIM
    cols = [jnp.where(low, parts[2 * j], parts[2 * j + 1]) for j in range(len(parts) // 2)]
    return cols[0] if len(cols) == 1 else jnp.concatenate(cols, axis=1)


def _attn_specs(attn_w, kv_w):
    half = attn_w // 2
    kcol, vcol = attn_w // kv_w, attn_w // kv_w + 1
    gcol = (attn_w + 2 * kv_w) // half
    prev = lambda i: jnp.maximum(i - 1, 0)
    return [
        pl.BlockSpec((BLOCK, attn_w), lambda i: (i, 0)),
        pl.BlockSpec((BLOCK, kv_w), lambda i: (prev(i), kcol)),
        pl.BlockSpec((BLOCK, kv_w), lambda i: (i, kcol)),
        pl.BlockSpec((BLOCK, kv_w), lambda i: (prev(i), vcol)),
        pl.BlockSpec((BLOCK, kv_w), lambda i: (i, vcol)),
        pl.BlockSpec((BLOCK, half), lambda i: (i, gcol)),
        pl.BlockSpec((BLOCK, half), lambda i: (i, gcol + 1)),
    ]


def _band_mask(i):
    q_loc = lax.broadcasted_iota(jnp.int32, (BLOCK, 2 * BLOCK), 0) + BLOCK
    k_loc = lax.broadcasted_iota(jnp.int32, (BLOCK, 2 * BLOCK), 1)
    diff = q_loc - k_loc
    first_key = jnp.where(i == 0, BLOCK, 0)
    return (diff >= 0) & (diff < BLOCK) & (k_loc >= first_key)


def _softmax_with_sink(s, sink):
    m = jnp.maximum(jnp.max(s, axis=-1, keepdims=True), sink)
    p = jnp.exp(s - m)
    e_sink = jnp.exp(sink - m)
    den = jnp.sum(p, axis=-1, keepdims=True) + e_sink
    inv = 1.0 / den
    return p * inv, e_sink * inv


def _attn_block(i, q, kk, vv, qw, kw, gmat, sink_ref, per_kv):
    scale = 1.0 / math.sqrt(HEAD_DIM)
    keys = 2 * BLOCK
    valid = _band_mask(i)
    q_rstd = lax.rsqrt(_head_mean(q * q, gmat) + NORM_EPS)
    qn = q * q_rstd
    qh = (qn * qw).astype(BF16)
    k_rstd = lax.rsqrt(_head_mean(kk * kk, gmat) + NORM_EPS)
    kn = kk * k_rstd
    kh = kn * kw
    gw = per_kv * HEAD_DIM
    groups = []
    for g in range(N_KV_HEADS):
        kd = _head_diagonal(_spread_head(kh, g, gw).astype(BF16), per_kv)
        vd = _head_diagonal(_spread_head(vv, g, gw).astype(BF16), per_kv)
        qg = qh[:, g * gw:(g + 1) * gw]
        s_all = _dot(qg, kd, NT) * scale
        ps, p_sinks = [], []
        for r in range(per_kv):
            s = jnp.where(valid, s_all[:, r * keys:(r + 1) * keys], -1e30)
            p, p_sink = _softmax_with_sink(s, sink_ref[g * per_kv + r])
            ps.append(p)
            p_sinks.append(p_sink)
        pb = jnp.concatenate(ps, axis=1).astype(BF16)
        groups.append((kd, vd, qg, ps, p_sinks, pb, _dot(pb, vd, NN)))
    return qn, q_rstd, kn, k_rstd, groups


def _attention_fwd(proj, qw_row, kw_row, gmat, sinks, *, attn_w, kv_w, name):
    rows = proj.shape[0]
    per_kv = attn_w // HEAD_DIM // N_KV_HEADS

    def body(q_ref, kp_ref, kc_ref, vp_ref, vc_ref, glo_ref, ghi_ref, qw_ref, kw_ref, gm_ref, sink_ref, o_ref):
        kk = jnp.concatenate([kp_ref[...], kc_ref[...]], axis=0).astype(F32)
        vv = jnp.concatenate([vp_ref[...], vc_ref[...]], axis=0).astype(F32)
        gate = jnp.concatenate([glo_ref[...], ghi_ref[...]], axis=1).astype(F32)
        *_, groups = _attn_block(pl.program_id(0), q_ref[...].astype(F32), kk, vv, qw_ref[...], kw_ref[...], gm_ref[...],
                                 sink_ref, per_kv)
        attn = jnp.concatenate([grp[-1] for grp in groups], axis=1)
        o_ref[...] = (attn * _silu(gate)).astype(BF16)

    const = lambda a: pl.BlockSpec(a.shape, lambda i: (0, 0))
    return _pallas(
        body, name=name, out_shape=jax.ShapeDtypeStruct((rows, attn_w), BF16), grid=(rows // BLOCK,),
        in_specs=_attn_specs(attn_w, kv_w) + [const(qw_row), const(kw_row), const(gmat),
                                              pl.BlockSpec(memory_space=pltpu.SMEM)],
        out_specs=pl.BlockSpec((BLOCK, attn_w), lambda i: (i, 0)),
        compiler_params=_params(("parallel",), 40 * 1024 * 1024),
    )(proj, proj, proj, proj, proj, proj, proj, qw_row, kw_row, gmat, sinks)


def _attention_bwd(proj, d_ag, dproj, qw_row, kw_row, gmat, sinks, *, attn_w, kv_w, name, rider=None):
    rows = proj.shape[0]
    nb = rows // BLOCK
    per_kv = attn_w // HEAD_DIM // N_KV_HEADS
    gw = per_kv * HEAD_DIM
    keys = 2 * BLOCK
    scale = 1.0 / math.sqrt(HEAD_DIM)
    w_out = 2 * attn_w + 2 * kv_w

    def body(q_ref, kp_ref, kc_ref, vp_ref, vc_ref, glo_ref, ghi_ref, dag_ref, qw_ref, kw_ref, gm_ref, sink_ref, _,
             dp_ref, dkv_ref, gqw_ref, gkw_ref, gs_ref):
        i = pl.program_id(0)
        kk = jnp.concatenate([kp_ref[...], kc_ref[...]], axis=0).astype(F32)
        vv = jnp.concatenate([vp_ref[...], vc_ref[...]], axis=0).astype(F32)
        gate = jnp.concatenate([glo_ref[...], ghi_ref[...]], axis=1).astype(F32)
        d_ag_v = dag_ref[...].astype(F32)
        qw, kw, gmat_v = qw_ref[...], kw_ref[...], gm_ref[...]
        qn, q_rstd, kn, k_rstd, groups = _attn_block(i, q_ref[...].astype(F32), kk, vv, qw, kw, gmat_v, sink_ref,
                                                     per_kv)
        lane = lax.broadcasted_iota(jnp.int32, (SUBLANES, LANES), 1)
        sub = lax.broadcasted_iota(jnp.int32, (SUBLANES, LANES), 0)
        gsink = jnp.zeros((SUBLANES, LANES), F32)
        dq_groups, dgate_groups, dk_heads, dv_heads = [], [], [], []
        for g, (kd, vd, qg, ps, p_sinks, pb, o) in enumerate(groups):
            cs = slice(g * gw, (g + 1) * gw)
            gate_g, d_ag_g = gate[:, cs], d_ag_v[:, cs]
            dgate_groups.append(d_ag_g * o * _dsilu(gate_g))
            do = (d_ag_g * _silu(gate_g)).astype(BF16)
            dp_all = _dot(do, vd, NT)
            dss = []
            for r in range(per_kv):
                p, dp = ps[r], dp_all[:, r * keys:(r + 1) * keys]
                delta = jnp.sum(p * dp, axis=-1, keepdims=True)
                dss.append(p * (dp - delta) * scale)
                gs_h = jnp.sum(-p_sinks[r] * delta, axis=0, keepdims=True)
                gsink = gsink + jnp.where((lane == g * per_kv + r) & (sub == 0), gs_h, 0.0)
            ds = jnp.concatenate(dss, axis=1).astype(BF16)
            dq_groups.append(_dot(ds, kd, NN))
            dk_heads.append(_fold_heads(_dot(ds, qg, TN), per_kv))
            dv_heads.append(_fold_heads(_dot(pb, do, TN), per_kv))
        dqh = jnp.concatenate(dq_groups, axis=1)
        gqw = jnp.sum(dqh * qn, axis=0, keepdims=True)
        dqn = dqh * qw
        dq = q_rstd * (dqn - qn * _head_mean(dqn * qn, gmat_v))
        dkh = _join_heads(dk_heads)
        gkw = jnp.sum(dkh * kn, axis=0, keepdims=True)
        dkn = dkh * kw
        dk = k_rstd * (dkn - kn * _head_mean(dkn * kn, gmat_v))
        dp_ref[:, 0:attn_w] = dq.astype(BF16)
        dp_ref[:, attn_w:attn_w + 2 * kv_w] = jnp.zeros((BLOCK, 2 * kv_w), BF16)
        dp_ref[:, attn_w + 2 * kv_w:w_out] = jnp.concatenate(dgate_groups, axis=1).astype(BF16)
        dkv_ref[0] = jnp.concatenate([dk, _join_heads(dv_heads)], axis=1)

        @pl.when(i == 0)
        def _():
            gqw_ref[...] = gqw
            gkw_ref[...] = gkw
            gs_ref[...] = gsink

        @pl.when(i > 0)
        def _():
            gqw_ref[...] += gqw
            gkw_ref[...] += gkw
            gs_ref[...] += gsink

    const = lambda a: pl.BlockSpec(a.shape, lambda i: (0, 0))
    res, landed = _call(
        body, [proj, proj, proj, proj, proj, proj, proj, d_ag, qw_row, kw_row, gmat, sinks, dproj], name=name,
        out_shape=[jax.ShapeDtypeStruct(dproj.shape, BF16),
                   jax.ShapeDtypeStruct((nb, 2 * BLOCK, 2 * kv_w), F32),
                   jax.ShapeDtypeStruct(qw_row.shape, F32), jax.ShapeDtypeStruct(kw_row.shape, F32),
                   jax.ShapeDtypeStruct((SUBLANES, LANES), F32)],
        grid=(nb,),
        in_specs=_attn_specs(attn_w, kv_w) + [pl.BlockSpec((BLOCK, attn_w), lambda i: (i, 0)), const(qw_row),
                                              const(kw_row), const(gmat), pl.BlockSpec(memory_space=pltpu.SMEM),
                                              _any_spec()],
        out_specs=[pl.BlockSpec((BLOCK, w_out), lambda i: (i, 0)),
                   pl.BlockSpec((1, 2 * BLOCK, 2 * kv_w), lambda i: (i, 0, 0)),
                   const(qw_row), const(kw_row), pl.BlockSpec((SUBLANES, LANES), lambda i: (0, 0))],
        aliases={12: 0}, semantics=("arbitrary",), vmem=48 * 1024 * 1024, rider=rider)
    return res if rider is None else (res, landed)


def _attention_dkv(dproj, dkv, *, attn_w, kv_w, name):
    rows = dproj.shape[0]
    nb = rows // BLOCK
    col = attn_w // (2 * kv_w)

    def body(cur_ref, nxt_ref, _, o_ref):
        i = pl.program_id(0)
        nxt = jnp.where(i < nb - 1, nxt_ref[0, 0:BLOCK, :], 0.0)
        o_ref[...] = (cur_ref[0, BLOCK:2 * BLOCK, :] + nxt).astype(BF16)

    blk = lambda f: pl.BlockSpec((1, 2 * BLOCK, 2 * kv_w), f)
    return _pallas(
        body, name=name, out_shape=jax.ShapeDtypeStruct(dproj.shape, BF16), grid=(nb,),
        in_specs=[blk(lambda i: (i, 0, 0)), blk(lambda i: (jnp.minimum(i + 1, nb - 1), 0, 0)), _any_spec()],
        out_specs=pl.BlockSpec((BLOCK, 2 * kv_w), lambda i: (i, col)),
        input_output_aliases={2: 0},
        compiler_params=_params(("parallel",)),
    )(dkv, dkv, dproj)


def _cmul(ar, ai, br, bi):
    return ar * br - ai * bi, ar * bi + ai * br


def _ssm_prep(a_re, a_im, log_dt_col, steps, name):
    assert steps & (steps - 1) == 0

    def body(are_ref, aim_ref, ldt_ref, abr_ref, abi_ref, cfr_ref, cfi_ref, apr_ref, api_ref):
        are, aim = are_ref[...], aim_ref[...]
        dt = jnp.exp(ldt_ref[...])
        mag = jnp.exp(dt * are)
        abr = mag * jnp.cos(dt * aim)
        abi = mag * jnp.sin(dt * aim)
        num_re, num_im = abr - 1.0, abi
        den = are * are + aim * aim
        abr_ref[...] = abr
        abi_ref[...] = abi
        cfr_ref[...] = (num_re * are + num_im * aim) / den
        cfi_ref[...] = (num_im * are - num_re * aim) / den
        pr, pi = abr, abi
        n = steps
        while n > 1:
            pr, pi = _cmul(pr, pi, pr, pi)
            n //= 2
        apr_ref[...] = pr
        api_ref[...] = pi

    shp = jax.ShapeDtypeStruct(a_re.shape, F32)
    return _pallas(body, name=name, out_shape=[shp] * 6)(a_re, a_im, log_dt_col)


def _ssm_param_bwd(a_re, a_im, log_dt_col, d_ab_re, d_ab_im, d_cf_re, d_cf_im, name):
    def body(are_ref, aim_ref, ldt_ref, gabr_ref, gabi_ref, gcfr_ref, gcfi_ref, dar_ref, dai_ref, dldt_ref):
        are, aim = are_ref[...], aim_ref[...]
        dt = jnp.exp(ldt_ref[...])
        mag = jnp.exp(dt * are)
        abr = mag * jnp.cos(dt * aim)
        abi = mag * jnp.sin(dt * aim)
        den = are * are + aim * aim
        cfr = ((abr - 1.0) * are + abi * aim) / den
        cfi = (abi * are - (abr - 1.0) * aim) / den
        gabr, gabi = jnp.sum(gabr_ref[...], axis=0), jnp.sum(gabi_ref[...], axis=0)
        gcfr, gcfi = jnp.sum(gcfr_ref[...], axis=0), jnp.sum(gcfi_ref[...], axis=0)
        inv_r, inv_i = are / den, -aim / den
        t_r, t_i = _cmul(inv_r, -inv_i, gcfr, gcfi)
        gabr, gabi = gabr + t_r, gabi + t_i
        q_r, q_i = _cmul(cfr, cfi, inv_r, inv_i)
        da_r, da_i = _cmul(-q_r, q_i, gcfr, gcfi)
        gz_r, gz_i = _cmul(abr, -abi, gabr, gabi)
        dar_ref[...] = da_r + dt * gz_r
        dai_ref[...] = da_i + dt * gz_i
        dldt_ref[...] = dt * jnp.sum(are * gz_r + aim * gz_i, axis=-1, keepdims=True)

    shp = jax.ShapeDtypeStruct(a_re.shape, F32)
    return _pallas(body, name=name, out_shape=[shp, shp, jax.ShapeDtypeStruct(log_dt_col.shape, F32)])(
        a_re, a_im, log_dt_col, d_ab_re, d_ab_im, d_cf_re, d_cf_im)


SCAN_LANES = 512
W_IN_GRAD_PARTS = 2


def _scan_segments(xr_ref, xi_ref, a_re, a_im, ap_re, ap_im, carry_re, carry_im, cm_re, cm_im, steps, reverse):
    n = xr_ref.shape[1]
    order = range(steps - 1, -1, -1) if reverse else range(steps)
    seg_order = range(SUBLANES - 1, -1, -1) if reverse else range(SUBLANES)
    for c0 in range(0, n, SCAN_LANES):
        ls = slice(c0, c0 + SCAN_LANES)
        ar = jnp.broadcast_to(a_re[:, ls], (SUBLANES, SCAN_LANES))
        ai = jnp.broadcast_to(a_im[:, ls], (SUBLANES, SCAN_LANES))

        def local(t, s, ar=ar, ai=ai, ls=ls):
            j = steps - 1 - t if reverse else t
            r0 = pl.multiple_of(j * SUBLANES, SUBLANES)
            sr, si = _cmul(ar, ai, s[0], s[1])
            sr = sr + xr_ref[pl.ds(r0, SUBLANES), ls]
            si = si + xi_ref[pl.ds(r0, SUBLANES), ls]
            xr_ref[pl.ds(r0, SUBLANES), ls] = sr
            xi_ref[pl.ds(r0, SUBLANES), ls] = si
            return sr, si

        zero = jnp.zeros((SUBLANES, SCAN_LANES), F32)
        end_r, end_i = lax.fori_loop(0, steps, local, (zero, zero))
        cr, ci = carry_re[:, ls], carry_im[:, ls]
        apr, api = ap_re[:, ls], ap_im[:, ls]
        for r in seg_order:
            cm_re[r:r + 1, ls] = cr
            cm_im[r:r + 1, ls] = ci
            tr, ti = _cmul(apr, api, cr, ci)
            cr, ci = end_r[r:r + 1, :] + tr, end_i[r:r + 1, :] + ti
        carry_re[:, ls] = cr
        carry_im[:, ls] = ci

        def fix(t, s, ar=ar, ai=ai, ls=ls):
            j = steps - 1 - t if reverse else t
            r0 = pl.multiple_of(j * SUBLANES, SUBLANES)
            sr, si = _cmul(ar, ai, s[0], s[1])
            xr_ref[pl.ds(r0, SUBLANES), ls] += sr
            xi_ref[pl.ds(r0, SUBLANES), ls] += si
            return sr, si

        lax.fori_loop(0, steps, fix, (cm_re[:, ls], cm_im[:, ls]))
    del order


SB_GROUPS = MXU_DIM // GROUP
SB_STATE = SB_GROUPS * STATE


def _ssm_rows(b_re, b_im, c_re, c_im):
    def rows(m):
        flat = m.reshape(-1, STATE).astype(F32)
        return jnp.concatenate([flat, flat], axis=1)
    return rows(b_re.transpose(0, 2, 1)), rows(b_im.transpose(0, 2, 1)), rows(c_re), rows(c_im)


def _from_ssm_rows(rows, transpose):
    g = rows[:, :STATE].reshape(-1, GROUP, STATE)
    return g.transpose(0, 2, 1) if transpose else g


def _own_group(shape):
    row_g = lax.broadcasted_iota(jnp.int32, shape, 0) // GROUP
    col_g = lax.broadcasted_iota(jnp.int32, shape, 1) // STATE
    return row_g == col_g


def _block_diagonal(rows):
    tiled = jnp.concatenate([rows] * (SB_STATE // LANES), axis=1)
    return jnp.where(_own_group(tiled.shape), tiled, 0.0).astype(BF16)


def _block_rows(acc):
    x = jnp.where(_own_group(acc.shape), acc, 0.0)
    while x.shape[1] > LANES:
        half = x.shape[1] // 2
        x = x[:, :half] + x[:, half:]
    return x + pltpu.roll(x, STATE, axis=1)


def _rows_to_segments(dst, srcs, steps, stage):
    for ref, off in srcs:
        for k in range(ref.shape[1] // LANES):
            stage[off // LANES + k] = ref[:, k * LANES:(k + 1) * LANES].astype(F32)
    for k in range(dst.shape[1] // LANES):
        for j in range(steps):
            dst[j * SUBLANES:(j + 1) * SUBLANES, k * LANES:(k + 1) * LANES] = (
                stage[k, pl.ds(j, SUBLANES, stride=steps), :])


def _segments_to_rows(dst, src, steps, stage):
    for k in range(src.shape[1] // LANES):
        for j in range(steps):
            stage[k, pl.ds(j, SUBLANES, stride=steps), :] = (
                src[j * SUBLANES:(j + 1) * SUBLANES, k * LANES:(k + 1) * LANES])
    for k in range(src.shape[1] // LANES):
        dst[:, k * LANES:(k + 1) * LANES] = stage[k]


def _u_specs(w, o_u, chunk, index):
    half = w // 2
    assert o_u % half == 0
    return [pl.BlockSpec((chunk, half), lambda c, k=k: (index(c), o_u // half + k)) for k in range(2)]


def _ssm_fwd(proj, o_u, bc_rows, rows_p, d_row, *, chunk, name):
    rows = proj.shape[0]
    w = d_row.shape[1]
    nc = rows // chunk
    steps = chunk // SUBLANES
    nsb = w // MXU_DIM
    n_state = nsb * SB_STATE

    def body(ulo_ref, uhi_ref, b2r_ref, b2i_ref, c2r_ref, c2i_ref, abr_ref, abi_ref, cfr_ref, cfi_ref, apr_ref,
             api_ref, d_ref, y_ref, str_ref, sti_ref, bre_ref, bim_ref, cre_ref, cim_ref, useg, yseg, stage, sr, si,
             carry_r, carry_i, cm_r, cm_i):
        @pl.when(pl.program_id(0) == 0)
        def _():
            for src, dst in ((b2r_ref, bre_ref), (b2i_ref, bim_ref), (c2r_ref, cre_ref), (c2i_ref, cim_ref)):
                for sb in range(nsb):
                    dst[sb] = _block_diagonal(src[sb * MXU_DIM:(sb + 1) * MXU_DIM, :])
            carry_r[...] = jnp.zeros_like(carry_r)
            carry_i[...] = jnp.zeros_like(carry_i)

        str_ref[0] = carry_r[...]
        sti_ref[0] = carry_i[...]
        _rows_to_segments(useg, [(ulo_ref, 0), (uhi_ref, w // 2)], steps, stage)
        for sb in range(nsb):
            us = slice(sb * MXU_DIM, (sb + 1) * MXU_DIM)
            ss = slice(sb * SB_STATE, (sb + 1) * SB_STATE)
            ub = useg[:, us].astype(BF16)
            bur = _dot(ub, bre_ref[sb], NN)
            bui = _dot(ub, bim_ref[sb], NN)
            xr, xi = _cmul(cfr_ref[:, ss], cfi_ref[:, ss], bur, bui)
            sr[:, ss] = xr
            si[:, ss] = xi
        _scan_segments(sr, si, abr_ref[...], abi_ref[...], apr_ref[...], api_ref[...],
                       carry_r, carry_i, cm_r, cm_i, steps, False)
        for sb in range(nsb):
            us = slice(sb * MXU_DIM, (sb + 1) * MXU_DIM)
            ss = slice(sb * SB_STATE, (sb + 1) * SB_STATE)
            y = _dot(sr[:, ss].astype(BF16), cre_ref[sb], NT) - _dot(si[:, ss].astype(BF16), cim_ref[sb], NT)
            yseg[:, us] = y + d_ref[:, us] * useg[:, us]
        _segments_to_rows(y_ref, yseg, steps, stage)

    const = lambda a: pl.BlockSpec(a.shape, lambda c: (0,) * a.ndim)
    row_n = pl.BlockSpec((1, n_state), lambda c: (0, 0))
    st = pl.BlockSpec((1, 1, n_state), lambda c: (c, 0, 0))
    held = [pltpu.VMEM((nsb, MXU_DIM, SB_STATE), BF16)] * 4
    vmem = 4 * _nbytes((nsb, MXU_DIM, SB_STATE), BF16) + 3 * _nbytes((chunk, n_state), F32)
    return _pallas(
        body, name=name,
        out_shape=[jax.ShapeDtypeStruct((rows, w), F32), jax.ShapeDtypeStruct((nc, 1, n_state), F32),
                   jax.ShapeDtypeStruct((nc, 1, n_state), F32)],
        grid=(nc,),
        in_specs=_u_specs(w, o_u, chunk, lambda c: c) + [const(b) for b in bc_rows]
        + [row_n] * 6 + [pl.BlockSpec((1, w), lambda c: (0, 0))],
        out_specs=[pl.BlockSpec((chunk, w), lambda c: (c, 0)), st, st],
        scratch_shapes=held + [pltpu.VMEM((chunk, w), F32), pltpu.VMEM((chunk, w), F32),
                               pltpu.VMEM((w // LANES, chunk, LANES), F32),
                               pltpu.VMEM((chunk, n_state), F32), pltpu.VMEM((chunk, n_state), F32),
                               pltpu.VMEM((1, n_state), F32), pltpu.VMEM((1, n_state), F32),
                               pltpu.VMEM((SUBLANES, n_state), F32), pltpu.VMEM((SUBLANES, n_state), F32)],
        compiler_params=_params(("arbitrary",), vmem),
    )(proj, proj, *bc_rows, *rows_p, d_row)


def _ssm_bwd(proj, o_u, y, dyg, st_re, st_im, bc_rows, rows_p, d_row, *, chunk, name, rider=None):
    rows = proj.shape[0]
    w = d_row.shape[1]
    nc = rows // chunk
    steps = chunk // SUBLANES
    nsb = w // MXU_DIM
    n_state = nsb * SB_STATE

    def body(ulo_ref, uhi_ref, y_ref, dyg_ref, str_ref, sti_ref, b2r_ref, b2i_ref, c2r_ref, c2i_ref,
             abr_ref, abi_ref, cfr_ref, cfi_ref, apr_ref, api_ref, d_ref,
             du_ref, gb2r_ref, gb2i_ref, gc2r_ref, gc2i_ref, gabr_ref, gabi_ref, gcfr_ref, gcfi_ref, dd_ref,
             bre_ref, bim_ref, cre_ref, cim_ref, dbre_ref, dbim_ref, dcre_ref, dcim_ref, useg, dyseg, dynat, stage,
             bur, bui, sr, si, lr, li, carry_r, carry_i, lam_r, lam_i, cm_r, cm_i, cl_r, cl_i):
        first = pl.program_id(0) == 0

        @pl.when(first)
        def _():
            for src, dst in ((b2r_ref, bre_ref), (b2i_ref, bim_ref), (c2r_ref, cre_ref), (c2i_ref, cim_ref)):
                for sb in range(nsb):
                    dst[sb] = _block_diagonal(src[sb * MXU_DIM:(sb + 1) * MXU_DIM, :])
            lam_r[...] = jnp.zeros_like(lam_r)
            lam_i[...] = jnp.zeros_like(lam_i)
            for ref in (dbre_ref, dbim_ref, dcre_ref, dcim_ref, gabr_ref, gabi_ref, gcfr_ref, gcfi_ref, dd_ref):
                ref[...] = jnp.zeros_like(ref)

        dynat[...] = dyg_ref[...].astype(F32) * _dgelu(y_ref[...])
        half = w // 2
        dd_ref[:, :half] += jnp.sum(dynat[:, :half] * ulo_ref[...].astype(F32), axis=0, keepdims=True)
        dd_ref[:, half:] += jnp.sum(dynat[:, half:] * uhi_ref[...].astype(F32), axis=0, keepdims=True)
        _rows_to_segments(useg, [(ulo_ref, 0), (uhi_ref, half)], steps, stage)
        _rows_to_segments(dyseg, [(dynat, 0)], steps, stage)
        dy = dyseg[...]
        dyb = dy.astype(BF16)
        ub = useg[...].astype(BF16)
        carry_r[...] = str_ref[0]
        carry_i[...] = sti_ref[0]
        for sb in range(nsb):
            us = slice(sb * MXU_DIM, (sb + 1) * MXU_DIM)
            ss = slice(sb * SB_STATE, (sb + 1) * SB_STATE)
            br = _dot(ub[:, us], bre_ref[sb], NN)
            bi = _dot(ub[:, us], bim_ref[sb], NN)
            bur[:, ss] = br
            bui[:, ss] = bi
            xr, xi = _cmul(cfr_ref[:, ss], cfi_ref[:, ss], br, bi)
            sr[:, ss] = xr
            si[:, ss] = xi
            lr[:, ss] = _dot(dyb[:, us], cre_ref[sb], NN)
            li[:, ss] = -_dot(dyb[:, us], cim_ref[sb], NN)
        abr, abi = abr_ref[...], abi_ref[...]
        apr, api = apr_ref[...], api_ref[...]
        _scan_segments(sr, si, abr, abi, apr, api, carry_r, carry_i, cm_r, cm_i, steps, False)
        for sb in range(nsb):
            us = slice(sb * MXU_DIM, (sb + 1) * MXU_DIM)
            ss = slice(sb * SB_STATE, (sb + 1) * SB_STATE)
            dcre_ref[sb] += _dot(dyb[:, us], sr[:, ss].astype(BF16), TN)
            dcim_ref[sb] -= _dot(dyb[:, us], si[:, ss].astype(BF16), TN)
        _scan_segments(lr, li, abr, -abi, apr, -api, lam_r, lam_i, cl_r, cl_i, steps, True)
        for c0 in range(0, n_state, SCAN_LANES):
            ls = slice(c0, c0 + SCAN_LANES)
            cfr = jnp.broadcast_to(cfr_ref[:, ls], (SUBLANES, SCAN_LANES))
            cfi = jnp.broadcast_to(cfi_ref[:, ls], (SUBLANES, SCAN_LANES))

            def step(j, acc, ls=ls, cfr=cfr, cfi=cfi):
                gar, gai, gcr, gci, pr, pi = acc
                r0 = pl.multiple_of(j * SUBLANES, SUBLANES)
                rws = pl.ds(r0, SUBLANES)
                l_r, l_i = lr[rws, ls], li[rws, ls]
                t_r, t_i = _cmul(pr, -pi, l_r, l_i)
                b_r, b_i = bur[rws, ls], bui[rws, ls]
                c_r, c_i = _cmul(b_r, -b_i, l_r, l_i)
                x_r, x_i = _cmul(cfr, -cfi, l_r, l_i)
                bur[rws, ls] = x_r
                bui[rws, ls] = x_i
                return gar + t_r, gai + t_i, gcr + c_r, gci + c_i, sr[rws, ls], si[rws, ls]

            zero = jnp.zeros((SUBLANES, SCAN_LANES), F32)
            gar, gai, gcr, gci, _, _ = lax.fori_loop(
                0, steps, step, (zero, zero, zero, zero, cm_r[:, ls], cm_i[:, ls]))
            gabr_ref[:, ls] += gar
            gabi_ref[:, ls] += gai
            gcfr_ref[:, ls] += gcr
            gcfi_ref[:, ls] += gci
        for sb in range(nsb):
            us = slice(sb * MXU_DIM, (sb + 1) * MXU_DIM)
            ss = slice(sb * SB_STATE, (sb + 1) * SB_STATE)
            xr, xi = bur[:, ss].astype(BF16), bui[:, ss].astype(BF16)
            du = _dot(xr, bre_ref[sb], NT) + _dot(xi, bim_ref[sb], NT)
            useg[:, us] = du + d_ref[:, us] * dy[:, us]
            dbre_ref[sb] += _dot(ub[:, us], xr, TN)
            dbim_ref[sb] += _dot(ub[:, us], xi, TN)
        _segments_to_rows(du_ref, useg, steps, stage)

        @pl.when(pl.program_id(0) == nc - 1)
        def _():
            for src, dst in ((dbre_ref, gb2r_ref), (dbim_ref, gb2i_ref), (dcre_ref, gc2r_ref), (dcim_ref, gc2i_ref)):
                for sb in range(nsb):
                    dst[sb * MXU_DIM:(sb + 1) * MXU_DIM, :] = _block_rows(src[sb])

    rev = lambda c: nc - 1 - c
    const = lambda a: pl.BlockSpec(a.shape, lambda c: (0,) * a.ndim)
    tile = pl.BlockSpec((chunk, w), lambda c: (rev(c), 0))
    row_n = pl.BlockSpec((1, n_state), lambda c: (0, 0))
    row_w = pl.BlockSpec((1, w), lambda c: (0, 0))
    st = pl.BlockSpec((1, 1, n_state), lambda c: (rev(c), 0, 0))
    acc8 = pl.BlockSpec((SUBLANES, n_state), lambda c: (0, 0))
    big = pltpu.VMEM((chunk, n_state), F32)
    small = pltpu.VMEM((chunk, w), F32)
    row = pltpu.VMEM((1, n_state), F32)
    eight = pltpu.VMEM((SUBLANES, n_state), F32)
    blk = (nsb, MXU_DIM, SB_STATE)
    held = [pltpu.VMEM(blk, BF16)] * 4 + [pltpu.VMEM(blk, F32)] * 4
    vmem = (4 * (_nbytes(blk, BF16) + _nbytes(blk, F32)) + 7 * _nbytes((chunk, n_state), F32)
            + 20 * _nbytes((chunk, w), F32) + 16 * _nbytes(bc_rows[0].shape, F32))
    res, landed = _call(
        body, [proj, proj, y, dyg, st_re, st_im, *bc_rows, *rows_p, d_row], name=name,
        out_shape=[jax.ShapeDtypeStruct((rows, w), F32)] + [jax.ShapeDtypeStruct(b.shape, F32) for b in bc_rows]
        + [jax.ShapeDtypeStruct((SUBLANES, n_state), F32)] * 4 + [jax.ShapeDtypeStruct((1, w), F32)],
        grid=(nc,),
        in_specs=_u_specs(w, o_u, chunk, rev) + [tile, tile, st, st] + [const(b) for b in bc_rows]
        + [row_n] * 6 + [row_w],
        out_specs=[tile] + [const(b) for b in bc_rows] + [acc8] * 4 + [row_w],
        scratch_shapes=held + [small] * 3 + [pltpu.VMEM((w // LANES, chunk, LANES), F32)] + [big] * 6 + [row] * 4
        + [eight] * 4,
        semantics=("arbitrary",), vmem=vmem, rider=rider)
    return res if rider is None else (res, landed)


def _loss_grad(x, mm, target, name):
    rows, d = x.shape

    def fn(xv, mv, tv):
        err = xv + mv - tv
        g = err * (1.0 / d)
        return g, g, 0.5 * err * g

    return _ew(fn, name=name, rows=rows, width=d, tiles=[(x, 0), (mm, 0), (target, 0)],
               outs=[(F32, d, 0), (BF16, d, 0)], accs=1)


def _pair_sum(grad, recv, name):
    r4, cdim = recv.shape
    r = r4 // N_CHIPS
    tr = _tile(r, 544, 16)
    g4 = grad.reshape(N_CHIPS, 2, r, cdim)
    r3 = recv.reshape(N_CHIPS, r, cdim)
    core = jnp.reshape(lax.axis_index("c"), (1,)).astype(jnp.int32)

    def body(c_ref, g_ref, r_ref, o_ref):
        o_ref[...] = (g_ref[0] + r_ref[...]).astype(BF16)

    out = _pallas(
        body, name=name, out_shape=jax.ShapeDtypeStruct((N_CHIPS, r, cdim), BF16),
        grid_spec=pltpu.PrefetchScalarGridSpec(
            num_scalar_prefetch=1, grid=(N_CHIPS, r // tr),
            in_specs=[pl.BlockSpec((1, 1, tr, cdim), lambda j, i, c: (j, c[0], i, 0)),
                      pl.BlockSpec((1, tr, cdim), lambda j, i, c: (j, i, 0))],
            out_specs=pl.BlockSpec((1, tr, cdim), lambda j, i, c: (j, i, 0))),
        compiler_params=_params(("parallel", "parallel"), 6 * _nbytes((tr, cdim), F32)),
    )(core, g4, r3)
    return out.reshape(r4, cdim)


def _chip_sum(recv, name):
    r4, cdim = recv.shape
    r = r4 // N_CHIPS
    tr = _tile(r, 544, 16)
    r3 = recv.reshape(N_CHIPS, r, cdim)

    def body(r_ref, o_ref):
        acc = r_ref[0].astype(F32)
        for j in range(1, N_CHIPS):
            acc = acc + r_ref[j].astype(F32)
        o_ref[...] = acc

    return _pallas(
        body, name=name, out_shape=jax.ShapeDtypeStruct((r, cdim), F32), grid=(r // tr,),
        in_specs=[pl.BlockSpec((N_CHIPS, tr, cdim), lambda i: (0, i, 0))],
        out_specs=pl.BlockSpec((tr, cdim), lambda i: (i, 0)),
        compiler_params=_params(("parallel",), 8 * _nbytes((tr, cdim), F32)),
    )(r3)


def _adamw_math(w, g, m, v):
    m = ADAM_B1 * m + (1.0 - ADAM_B1) * g
    v = ADAM_B2 * v + (1.0 - ADAM_B2) * (g * g)
    m_hat = m / (1.0 - ADAM_B1 ** ADAM_STEP)
    v_hat = v / (1.0 - ADAM_B2 ** ADAM_STEP)
    delta = -ADAM_LR * (m_hat / (jnp.sqrt(v_hat) + ADAM_EPS) + ADAM_WD * w)
    return delta, m, v


def _adamw(w, g, m, v, name):
    rows, cols = w.shape
    tr = _tile(rows, 256, SUBLANES)

    def body(w_ref, g_ref, m_ref, v_ref, d_ref, nm_ref, nv_ref):
        d, nm, nv = _adamw_math(w_ref[...], g_ref[...], m_ref[...], v_ref[...])
        d_ref[...] = d
        nm_ref[...] = nm
        nv_ref[...] = nv

    spec = pl.BlockSpec((tr, cols), lambda i: (i, 0))
    shp = jax.ShapeDtypeStruct((rows, cols), F32)
    return _pallas(
        body, name=name, out_shape=[shp] * 3, grid=(rows // tr,), in_specs=[spec] * 4, out_specs=[spec] * 3,
        compiler_params=_params(("parallel",)),
    )(w, g, m, v)


def _adamw_small(w, parts, m, v, name):
    rows, cols = w.shape
    p3 = parts.reshape(N_DEV, rows, cols)

    def body(w_ref, p_ref, m_ref, v_ref, g_ref, d_ref, nm_ref, nv_ref):
        g = p_ref[0]
        for k in range(1, N_DEV):
            g = g + p_ref[k]
        d, nm, nv = _adamw_math(w_ref[...], g, m_ref[...], v_ref[...])
        g_ref[...] = g
        d_ref[...] = d
        nm_ref[...] = nm
        nv_ref[...] = nv

    shp = jax.ShapeDtypeStruct((rows, cols), F32)
    return _pallas(body, name=name, out_shape=[shp] * 4)(w, p3, m, v)


SMALL = ("norm_w", "q_norm_w", "k_norm_w", "sinks", "A_re", "A_im", "log_dt", "B_re", "B_im", "C_re", "C_im",
         "D_skip", "b_glu")
LARGE = ("w_in", "w_attn_proj", "w_glu", "w_ssm_proj", "w_out")
ORDER = ("norm_w", "w_in", "q_norm_w", "k_norm_w", "sinks", "w_attn_proj", "A_re", "A_im", "log_dt", "B_re", "B_im",
         "C_re", "C_im", "D_skip", "w_glu", "b_glu", "w_ssm_proj", "w_out")


SMALL_REST = ("loss",) + SMALL[1:]


def _pack(named, keys):
    flat = jnp.concatenate([named[k].reshape(-1).astype(F32) for k in keys])
    n = flat.shape[0]
    rows = -(-n // (LANES * SUBLANES)) * SUBLANES
    return jnp.pad(flat, (0, rows * LANES - n)).reshape(rows, LANES)


def _unpack(packed, like, keys):
    flat = packed.reshape(-1)
    out, o = {}, 0
    for k in keys:
        n = like[k].size
        out[k] = flat[o:o + n].reshape(like[k].shape)
        o += n
    return out


def _step(xs, target, p, shards):
    s_in, s_ap, s_glu, s_sp, s_o = shards
    seq, d = xs.shape
    attn_w = (d // 128) * HEAD_DIM
    n_q = attn_w // HEAD_DIM
    kv_w = N_KV_HEADS * HEAD_DIM
    ssm_w = d // 2
    n_groups = ssm_w // GROUP
    n_state = n_groups * STATE
    in_w = N_DEV * s_in.shape[0]
    assert in_w == 2 * attn_w + 2 * kv_w + 2 * ssm_w + 2 * d
    o_u = 2 * attn_w + 2 * kv_w
    o_z = o_u + ssm_w
    o_ga = o_z + ssm_w
    chunk = min(BLOCK, seq)
    cw = d // 4

    norm_row = p["norm_w"].reshape(1, d)
    h = _rmsnorm_fwd(xs, norm_row, "rmsnorm_fwd")
    proj, w_in_t, (w_ap_t, w_glu_t, w_sp_t, w_o) = _gather_in_proj(
        h, s_in, name="gather_in_proj", rider=_all_gather([s_ap, s_glu, s_sp, s_o]))
    qw_row = jnp.tile(p["q_norm_w"], n_q).reshape(1, attn_w)
    kw_row = jnp.tile(p["k_norm_w"], N_KV_HEADS).reshape(1, kv_w)
    gmat = _head_mean_matrix()
    ag = _attention_fwd(proj, qw_row, kw_row, gmat, p["sinks"], attn_w=attn_w, kv_w=kv_w, name="attention_fwd")

    log_dt_col = p["log_dt"].reshape(n_groups, 1)
    prep = _ssm_prep(p["A_re"], p["A_im"], log_dt_col, chunk // SUBLANES, "ssm_prep")
    rows_p = [v.reshape(1, n_state) for v in prep]
    bc_rows = _ssm_rows(p["B_re"], p["B_im"], p["C_re"], p["C_im"])
    d_row = p["D_skip"].reshape(1, ssm_w)
    y_ssm, st_re, st_im = _ssm_fwd(proj, o_u, bc_rows, rows_p, d_row, chunk=chunk, name="ssm_fwd")
    (yg,) = _ew(_gelu, name="gelu", rows=seq, width=ssm_w, tiles=[(y_ssm, 0)], outs=[(BF16, ssm_w, 0)], cw=cw)
    glu = _matmul(yg, w_glu_t, mode="nt", name="glu_proj", out_dtype=BF16, bias=p["b_glu"].reshape(1, 2 * ssm_w))
    (ts,) = _ew(lambda ga, gb, z: ga * _sigmoid(gb) * _silu(z), name="glu_gate", rows=seq, width=ssm_w,
                tiles=[(glu, 0), (glu, ssm_w), (proj, o_z)], outs=[(BF16, ssm_w, 0)], cw=cw)
    yy = _matmul(ag, w_ap_t, mode="nt", name="attn_proj", out_dtype=BF16, out_cols=(2 * d, 0))
    yy = _matmul(ts, w_sp_t, mode="nt", name="ssm_proj", out_dtype=BF16, out_cols=(2 * d, d), into=yy)
    (merged,) = _ew(lambda ya, ys, ga, gs: _sigmoid(ga) * ya + _sigmoid(gs) * ys, name="merge", rows=seq, width=d,
                    tiles=[(yy, 0), (yy, d), (proj, o_ga), (proj, o_ga + d)], outs=[(BF16, d, 0)], cw=cw)
    mm = _matmul(merged, w_o, mode="nn", name="out_proj")
    dout, dout_b, loss_cols = _loss_grad(xs, mm, target, "loss_grad")
    loss_local = jnp.sum(loss_cols)

    g_w_o = _matmul(merged, dout_b, mode="tn", name="grad_w_out", tm=512, tk=4096)
    dmerged, (sib_o,) = _matmul(dout_b, w_o, mode="nt", name="d_merged", out_dtype=BF16,
                                rider=_sibling_exchange([g_w_o]))
    pair_o = _pair_sum(g_w_o, sib_o, "pair_sum_w_out")

    def merge_bwd(dm, y, g):
        s = _sigmoid(g)
        return dm * s, dm * y * s * (1.0 - s)

    dyy, dproj = _ew(merge_bwd, name="merge_bwd", rows=seq, width=2 * d,
                     tiles=[(dmerged, 0, d), (yy, 0), (proj, o_ga)],
                     outs=[(BF16, 2 * d, 0), (BF16, in_w, o_ga)], cw=cw)
    dy_a, dy_s = Cols(dyy, 0, d), Cols(dyy, d, d)
    g_w_ap_t = _matmul(dy_a, ag, mode="tn", name="grad_w_attn_proj", tm=512, tk=4096)
    g_w_sp_t = _matmul(dy_s, ts, mode="tn", name="grad_w_ssm_proj", tm=512, tk=4096)
    d_ag = _matmul(dy_a, w_ap_t, mode="nn", name="d_attn_gated", out_dtype=BF16)
    d_ts = _matmul(dy_s, w_sp_t, mode="nn", name="d_ssm_gated", out_dtype=BF16)

    (dproj, dkv, g_qw, g_kw, g_sinks), (chips_o, sib_ap, sib_sp) = _attention_bwd(
        proj, d_ag, dproj, qw_row, kw_row, gmat, p["sinks"], attn_w=attn_w, kv_w=kv_w, name="attention_bwd",
        rider=_join(_chip_exchange([pair_o]), _sibling_exchange([g_w_ap_t, g_w_sp_t])))
    pair_ap = _pair_sum(g_w_ap_t, sib_ap, "pair_sum_w_attn_proj")
    pair_sp = _pair_sum(g_w_sp_t, sib_sp, "pair_sum_w_ssm_proj")
    dproj = _attention_dkv(dproj, dkv, attn_w=attn_w, kv_w=kv_w, name="attention_dkv")

    n_half = ssm_w // _tile(2 * ssm_w, cw)

    def glu_bwd(j, dt, ga, gb, z):
        sb, sz = _sigmoid(gb), _silu(z)
        dg = jnp.where(j < n_half, dt * sb * sz, dt * ga * sb * (1.0 - sb) * sz)
        return dg, dg

    glu_ops = [(d_ts, 0, ssm_w), (glu, 0, ssm_w), (glu, ssm_w, ssm_w), (proj, o_z, ssm_w)]
    dglu, g_bglu = _ew(glu_bwd, name="glu_bwd", rows=seq, width=2 * ssm_w, tiles=glu_ops,
                       outs=[(BF16, 2 * ssm_w, 0)], accs=1, cw=cw, with_col=True)
    (dproj,) = _ew(lambda dt, ga, gb, z: dt * ga * _sigmoid(gb) * _dsilu(z), name="glu_bwd_z", rows=seq,
                   width=ssm_w, tiles=glu_ops, outs=[(BF16, in_w, o_z)], into=[dproj], cw=cw)
    g_w_glu_t = _matmul(dglu, yg, mode="tn", name="grad_w_glu", tm=512, tk=4096)
    d_yg = _matmul(dglu, w_glu_t, mode="nn", name="d_gelu", out_dtype=BF16)
    ((du, db_re, db_im, dc_re, dc_im, gabr, gabi, gcfr, gcfi, g_d), (chips_ap, chips_sp, sib_glu)) = _ssm_bwd(
        proj, o_u, y_ssm, d_yg, st_re, st_im, bc_rows, rows_p, d_row, chunk=chunk, name="ssm_bwd",
        rider=_join(_chip_exchange([pair_ap, pair_sp]), _sibling_exchange([g_w_glu_t])))
    pair_glu = _pair_sum(g_w_glu_t, sib_glu, "pair_sum_w_glu")
    (dproj,) = _ew(lambda v: v, name="du_store", rows=seq, width=ssm_w, tiles=[(du, 0)],
                   outs=[(BF16, in_w, o_u)], into=[dproj], cw=cw)
    g_a_re, g_a_im, g_log_dt = _ssm_param_bwd(
        p["A_re"], p["A_im"], log_dt_col, *[g.reshape(SUBLANES, n_groups, STATE) for g in (gabr, gabi, gcfr, gcfi)],
        "ssm_param_bwd")
    small_grads = dict(
        loss=loss_local, q_norm_w=g_qw.reshape(n_q, HEAD_DIM).sum(0), k_norm_w=g_kw.reshape(N_KV_HEADS, HEAD_DIM).sum(0),
        sinks=g_sinks[0, :n_q], A_re=g_a_re, A_im=g_a_im, log_dt=g_log_dt.reshape(n_groups),
        B_re=_from_ssm_rows(db_re, True), B_im=_from_ssm_rows(db_im, True),
        C_re=_from_ssm_rows(dc_re, False), C_im=_from_ssm_rows(dc_im, False),
        D_skip=g_d.reshape(n_groups, GROUP), b_glu=g_bglu.reshape(2 * ssm_w))

    n_parts = W_IN_GRAD_PARTS
    wq = d // n_parts
    g_parts, pair_parts, chip_parts = [], [], []
    extra = [_chip_exchange([pair_glu]), _all_gather([_pack(small_grads, SMALL_REST)])]
    chips_glu = small_parts = dh = grad_x = g_norm = None
    for step in range(n_parts + 2):
        riders = list(extra) if step == 0 else []
        if 0 <= step - 2 < n_parts:
            riders.append(_chip_exchange([pair_parts[step - 2]]))
        if 0 <= step - 1 < n_parts:
            riders.append(_sibling_exchange([g_parts[step - 1]]))
        rider = _join(*riders) if riders else None
        if step < n_parts:
            res = _matmul(dproj, Cols(h, step * wq, wq), mode="tn", name="grad_w_in_%d" % step, tk=4096, rider=rider)
            out, landed = res if rider is not None else (res, [])
            g_parts.append(out)
        elif step == n_parts:
            dh, landed = _matmul(dproj, w_in_t, mode="nn", name="d_normed", tk=2176, rider=rider)
        else:
            (grad_x, g_norm), landed = _rmsnorm_bwd(xs, norm_row, dh, dout, "rmsnorm_bwd", rider=rider)
        landed = list(landed)
        if step == 0:
            chips_glu, small_parts = landed[:2]
            landed = landed[2:]
        if 0 <= step - 2 < n_parts:
            chip_parts.append(landed.pop(0))
        if 0 <= step - 1 < n_parts:
            pair_parts.append(_pair_sum(g_parts[step - 1], landed.pop(0), "pair_sum_w_in_%d" % (step - 1)))
    (norm_parts,) = _exchange(_all_gather([_pack(dict(norm_w=g_norm), ("norm_w",))]), "gather_norm_grad")
    g_in = jnp.concatenate([_chip_sum(c, "chip_sum_w_in_%d" % q) for q, c in enumerate(chip_parts)], axis=1)
    summed = [g_in] + [_chip_sum(c, "chip_sum_" + k)
                       for k, c in zip(LARGE[1:], (chips_ap, chips_glu, chips_sp, chips_o))]
    return grad_x, summed, small_parts, norm_parts


def kernel(x, norm_w, w_in, q_norm_w, k_norm_w, sinks, w_attn_proj, A_re, A_im, log_dt, B_re, B_im, C_re, C_im, D_skip, w_glu, b_glu, w_ssm_proj, w_out, loss_target, m_norm_w, m_w_in, m_q_norm_w, m_k_norm_w, m_sinks, m_w_attn_proj, m_A_re, m_A_im, m_log_dt, m_B_re, m_B_im, m_C_re, m_C_im, m_D_skip, m_w_glu, m_b_glu, m_w_ssm_proj, m_w_out, v_norm_w, v_w_in, v_q_norm_w, v_k_norm_w, v_sinks, v_w_attn_proj, v_A_re, v_A_im, v_log_dt, v_B_re, v_B_im, v_C_re, v_C_im, v_D_skip, v_w_glu, v_b_glu, v_w_ssm_proj, v_w_out):
    weights = dict(norm_w=norm_w, w_in=w_in, q_norm_w=q_norm_w, k_norm_w=k_norm_w, sinks=sinks,
                   w_attn_proj=w_attn_proj, A_re=A_re, A_im=A_im, log_dt=log_dt, B_re=B_re, B_im=B_im, C_re=C_re,
                   C_im=C_im, D_skip=D_skip, w_glu=w_glu, b_glu=b_glu, w_ssm_proj=w_ssm_proj, w_out=w_out)
    m_in = dict(norm_w=m_norm_w, w_in=m_w_in, q_norm_w=m_q_norm_w, k_norm_w=m_k_norm_w, sinks=m_sinks,
                w_attn_proj=m_w_attn_proj, A_re=m_A_re, A_im=m_A_im, log_dt=m_log_dt, B_re=m_B_re, B_im=m_B_im,
                C_re=m_C_re, C_im=m_C_im, D_skip=m_D_skip, w_glu=m_w_glu, b_glu=m_b_glu, w_ssm_proj=m_w_ssm_proj,
                w_out=m_w_out)
    v_in = dict(norm_w=v_norm_w, w_in=v_w_in, q_norm_w=v_q_norm_w, k_norm_w=v_k_norm_w, sinks=v_sinks,
                w_attn_proj=v_w_attn_proj, A_re=v_A_re, A_im=v_A_im, log_dt=v_log_dt, B_re=v_B_re, B_im=v_B_im,
                C_re=v_C_re, C_im=v_C_im, D_skip=v_D_skip, w_glu=v_w_glu, b_glu=v_b_glu, w_ssm_proj=v_w_ssm_proj,
                w_out=v_w_out)

    _, seq, d = x.shape
    column_sharded = LARGE[:4]
    as_rows = lambda k, a: a.T if k in column_sharded else a
    shards = [as_rows(k, weights[k]).astype(BF16) for k in LARGE]
    small = {k: weights[k] for k in SMALL}
    grad_x, summed, small_parts, norm_parts = _step(x.reshape(seq, d), loss_target.reshape(seq, d), small, shards)

    grads, delta, new_m, new_v = {}, {}, {}, {}
    for k, g in zip(LARGE, summed):
        if k == "w_in":
            upd = _adamw(weights[k].T, g, m_in[k].T, v_in[k].T, "adamw_" + k)
            grads[k], delta[k], new_m[k], new_v[k] = [a.T for a in (g, *upd)]
        else:
            grads[k] = as_rows(k, g)
            delta[k], new_m[k], new_v[k] = _adamw(weights[k], grads[k], m_in[k], v_in[k], "adamw_" + k)

    zero = jnp.zeros((), F32)
    for keys, parts in ((SMALL_REST, small_parts), (("norm_w",), norm_parts)):
        like = dict(small, loss=zero)
        packs = [_pack(dict(src, loss=zero), keys) for src in (weights, m_in, v_in)]
        res = _adamw_small(packs[0], parts, packs[1], packs[2], "adamw_small_%d" % len(keys))
        for dst, r in zip((grads, delta, new_m, new_v), res):
            dst.update(_unpack(r, like, keys))
    loss = grads["loss"]

    return (loss, grad_x.reshape(x.shape), *[grads[k] for k in ORDER], *[delta[k] for k in ORDER],
            *[new_m[k] for k in ORDER], *[new_v[k] for k in ORDER])
```

```python
import math
from typing import Callable, NamedTuple

import jax
import jax.numpy as jnp
import numpy as np
from jax import lax
from jax.experimental import pallas as pl
from jax.experimental.pallas import tpu as pltpu

F32 = jnp.float32
BF16 = jnp.bfloat16
MESH = pl.DeviceIdType.MESH

HEAD_DIM = 64
N_KV_HEADS = 4
GROUP = 16
STATE = 64
BLOCK = 128
NORM_EPS = 1e-6
N_DEV = 8
N_CHIPS = 4
LANES = 128
SUBLANES = 8
MXU_DIM = 256
VMEM_BYTES = 64 * 1024 * 1024
VMEM_CAP = VMEM_BYTES - 8 * 1024 * 1024

ADAM_LR = 0.001
ADAM_B1 = 0.9
ADAM_B2 = 0.999
ADAM_EPS = 1e-08
ADAM_WD = 0.01
ADAM_STEP = 10

GELU_C = math.sqrt(2.0 / math.pi)
GELU_K = 0.044715


def _tile(dim, pref, mult=LANES):
    if dim <= pref:
        return dim
    best = None
    for d in range(mult, pref + 1, mult):
        if dim % d == 0:
            best = d
    assert best is not None, (dim, pref, mult)
    return best


def _params(semantics=None, vmem=None):
    kw = {}
    if semantics is not None:
        kw["dimension_semantics"] = semantics
    if vmem is not None:
        kw["vmem_limit_bytes"] = int(min(VMEM_CAP, max(vmem, 32 * 1024 * 1024)))
    return pltpu.CompilerParams(**kw)


def _nbytes(shape, dtype):
    return math.prod(shape) * jnp.dtype(dtype).itemsize


def _sigmoid(x):
    return 1.0 / (1.0 + jnp.exp(-x))


def _silu(x):
    return x * _sigmoid(x)


def _dsilu(x):
    s = _sigmoid(x)
    return s * (1.0 + x * (1.0 - s))


def _gelu(x):
    return 0.5 * x * (1.0 + jnp.tanh(GELU_C * (x + GELU_K * x * x * x)))


def _dgelu(x):
    t = jnp.tanh(GELU_C * (x + GELU_K * x * x * x))
    return 0.5 * (1.0 + t) + 0.5 * x * (1.0 - t * t) * GELU_C * (1.0 + 3.0 * GELU_K * x * x)


def _dot(a, b, dims):
    return lax.dot_general(a, b, (dims, ((), ())), preferred_element_type=F32)


NN = ((1,), (0,))
NT = ((1,), (1,))
TN = ((0,), (0,))


def _any_spec():
    return pl.BlockSpec(memory_space=pl.ANY)


def _pallas(body, **kw):
    pin = lambda s: pltpu.HBM(s.shape, s.dtype) if isinstance(s, jax.ShapeDtypeStruct) else s
    out_shape = kw.pop("out_shape")
    out_shape = [pin(s) for s in out_shape] if isinstance(out_shape, (list, tuple)) else pin(out_shape)
    call = pl.pallas_call(body, out_shape=out_shape, **kw)

    def run(*operands):
        pinned = [pltpu.with_memory_space_constraint(o, pltpu.HBM) if jnp.issubdtype(o.dtype, jnp.floating) else o
                  for o in operands]
        return call(*pinned)

    return run


class Rider(NamedTuple):
    operands: tuple
    out_shapes: tuple
    sems: tuple
    start: Callable
    finish: Callable


def _all_gather(shards):
    n = len(shards)

    def copies(ins, outs, sems):
        send_sems, recv_sems, local_sems = sems
        x, y, c = lax.axis_index("x"), lax.axis_index("y"), lax.axis_index("c")
        me, sibling = (x, y, c), (x, y, 1 - c)
        chips = [(1 - x, y), (x, 1 - y), (1 - x, 1 - y)]

        def rows(k, px, py, pc):
            r = shards[k].shape[0]
            return outs[k].at[pl.ds((4 * px + 2 * py + pc) * r, r), :]

        def copy(k, s, block, to, src=None):
            return pltpu.make_async_remote_copy(
                src_ref=rows(k, *block) if src is None else src, dst_ref=rows(k, *block),
                send_sem=send_sems.at[7 * k + s], recv_sem=recv_sems.at[7 * k + s],
                device_id=to, device_id_type=MESH)

        mine = [pltpu.make_async_copy(ins[k], rows(k, *me), local_sems.at[k]) for k in range(n)]
        first = []
        for k in range(n):
            first.append(copy(k, 0, me, sibling, src=ins[k]))
            first += [copy(k, 1 + j, me, (*chip, c), src=ins[k]) for j, chip in enumerate(chips)]
        return me, sibling, chips, c, copy, mine, first

    def start(ins, outs, sems):
        *_, mine, first = copies(ins, outs, sems)
        for cp in mine + first:
            cp.start()

    def finish(ins, outs, sems):
        me, sibling, chips, c, copy, mine, first = copies(ins, outs, sems)
        passed = []
        for j, chip in enumerate(chips):
            for k in range(n):
                copy(k, 1 + j, (*chip, c), me).wait_recv()
                fwd = copy(k, 4 + j, (*chip, c), sibling)
                fwd.start()
                passed.append(fwd)
        for k in range(n):
            copy(k, 0, sibling, me).wait_recv()
            for j, chip in enumerate(chips):
                copy(k, 4 + j, (*chip, 1 - c), me).wait_recv()
        for cp in first + passed:
            cp.wait_send()
        for cp in mine:
            cp.wait()

    return Rider(
        tuple(shards),
        tuple(jax.ShapeDtypeStruct((N_DEV * s.shape[0], s.shape[1]), s.dtype) for s in shards),
        (pltpu.SemaphoreType.DMA((7 * n,)), pltpu.SemaphoreType.DMA((7 * n,)), pltpu.SemaphoreType.DMA((n,))),
        start, finish)


def _sibling_exchange(grads):
    n = len(grads)

    def copies(ins, outs, sems):
        send_sems, recv_sems = sems
        x, y, c = lax.axis_index("x"), lax.axis_index("y"), lax.axis_index("c")
        out = []
        for k in range(n):
            r = grads[k].shape[0] // N_DEV
            for j in range(N_CHIPS):
                out.append(pltpu.make_async_remote_copy(
                    src_ref=ins[k].at[pl.ds((2 * j + 1 - c) * r, r), :],
                    dst_ref=outs[k].at[pl.ds(j * r, r), :],
                    send_sem=send_sems.at[N_CHIPS * k + j], recv_sem=recv_sems.at[N_CHIPS * k + j],
                    device_id=(x, y, 1 - c), device_id_type=MESH))
        return out

    def start(ins, outs, sems):
        for cp in copies(ins, outs, sems):
            cp.start()

    def finish(ins, outs, sems):
        for cp in copies(ins, outs, sems):
            cp.wait()

    return Rider(
        tuple(grads), tuple(jax.ShapeDtypeStruct((g.shape[0] // 2, g.shape[1]), g.dtype) for g in grads),
        (pltpu.SemaphoreType.DMA((N_CHIPS * n,)), pltpu.SemaphoreType.DMA((N_CHIPS * n,))), start, finish)


def _chip_exchange(parts):
    n = len(parts)

    def copies(ins, outs, sems):
        send_sems, recv_sems, local_sems = sems
        x, y, c = lax.axis_index("x"), lax.axis_index("y"), lax.axis_index("c")
        my_chip = 2 * x + y
        chips = [(1 - x, y), (x, 1 - y), (1 - x, 1 - y)]
        local, sent = [], []
        for k in range(n):
            r = parts[k].shape[0] // N_CHIPS
            mine = pl.ds(my_chip * r, r)
            local.append(pltpu.make_async_copy(ins[k].at[mine, :], outs[k].at[mine, :], local_sems.at[k]))
            for s, (px, py) in enumerate(chips):
                sent.append(pltpu.make_async_remote_copy(
                    src_ref=ins[k].at[pl.ds((2 * px + py) * r, r), :], dst_ref=outs[k].at[mine, :],
                    send_sem=send_sems.at[3 * k + s], recv_sem=recv_sems.at[3 * k + s],
                    device_id=(px, py, c), device_id_type=MESH))
        return local, sent

    def start(ins, outs, sems):
        local, sent = copies(ins, outs, sems)
        for cp in local + sent:
            cp.start()

    def finish(ins, outs, sems):
        local, sent = copies(ins, outs, sems)
        for cp in sent + local:
            cp.wait()

    return Rider(
        tuple(parts), tuple(jax.ShapeDtypeStruct(p.shape, p.dtype) for p in parts),
        (pltpu.SemaphoreType.DMA((3 * n,)), pltpu.SemaphoreType.DMA((3 * n,)), pltpu.SemaphoreType.DMA((n,))),
        start, finish)


def _join(*riders):
    cuts_in, cuts_out, cuts_sem = [0], [0], [0]
    for r in riders:
        cuts_in.append(cuts_in[-1] + len(r.operands))
        cuts_out.append(cuts_out[-1] + len(r.out_shapes))
        cuts_sem.append(cuts_sem[-1] + len(r.sems))

    def each(which):
        def run(ins, outs, sems):
            for i, r in enumerate(riders):
                getattr(r, which)(ins[cuts_in[i]:cuts_in[i + 1]], outs[cuts_out[i]:cuts_out[i + 1]],
                                  sems[cuts_sem[i]:cuts_sem[i + 1]])
        return run

    return Rider(sum((r.operands for r in riders), ()), sum((r.out_shapes for r in riders), ()),
                 sum((r.sems for r in riders), ()), each("start"), each("finish"))


def _call(body, operands, *, name, out_shape, grid, in_specs, out_specs, scratch_shapes=(), aliases=None,
          semantics=None, vmem=None, rider=None):
    operands, out_shape, scratch_shapes = list(operands), list(out_shape), list(scratch_shapes)
    in_specs, out_specs = list(in_specs), list(out_specs)
    if rider is None:
        res = _pallas(
            body, name=name, out_shape=out_shape, grid=grid, in_specs=in_specs, out_specs=out_specs,
            scratch_shapes=scratch_shapes, input_output_aliases=aliases or {},
            compiler_params=_params(semantics, vmem))(*operands)
        return list(res), []
    n_in, n_out, n_scr = len(operands), len(out_shape), len(scratch_shapes)
    ri, ro = len(rider.operands), len(rider.out_shapes)

    def carried(*refs):
        a, b = n_in, n_in + ri
        c, d = b + n_out, b + n_out + ro
        e = d + n_scr
        ids = [pl.program_id(k) for k in range(len(grid))]
        first = ids[0] == 0
        last = ids[0] == grid[0] - 1
        for k in range(1, len(grid)):
            first = jnp.logical_and(first, ids[k] == 0)
            last = jnp.logical_and(last, ids[k] == grid[k] - 1)

        @pl.when(first)
        def _():
            rider.start(refs[a:b], refs[c:d], refs[e:])

        body(*refs[:a], *refs[b:c], *refs[d:e])

        @pl.when(last)
        def _():
            rider.finish(refs[a:b], refs[c:d], refs[e:])

    res = _pallas(
        carried, name=name, out_shape=out_shape + list(rider.out_shapes), grid=grid,
        in_specs=in_specs + [_any_spec()] * ri, out_specs=out_specs + [_any_spec()] * ro,
        scratch_shapes=scratch_shapes + list(rider.sems), input_output_aliases=aliases or {},
        compiler_params=_params(("arbitrary",) * len(grid), vmem))(*operands, *rider.operands)
    return list(res[:n_out]), list(res[n_out:])


def _exchange(rider, name):
    ri, ro = len(rider.operands), len(rider.out_shapes)

    def body(*refs):
        rider.start(refs[:ri], refs[ri:ri + ro], refs[ri + ro:])
        rider.finish(refs[:ri], refs[ri:ri + ro], refs[ri + ro:])

    return _pallas(
        body, name=name, out_shape=list(rider.out_shapes), in_specs=[_any_spec()] * ri,
        out_specs=[_any_spec()] * ro, scratch_shapes=list(rider.sems))(*rider.operands)


class Cols(NamedTuple):
    arr: jax.Array
    off: int
    width: int


def _cols(a):
    return a if isinstance(a, Cols) else Cols(a, 0, a.shape[1])


def _matmul(a, b, *, mode, name, out_dtype=F32, tm=1024, tn=1024, tk=2048, bias=None, out_cols=None, into=None,
            rider=None):
    a, b = _cols(a), _cols(b)
    if mode == "nn":
        (m, k), (k2, n) = (a.arr.shape[0], a.width), (b.arr.shape[0], b.width)
    elif mode == "nt":
        (m, k), (n, k2) = (a.arr.shape[0], a.width), (b.arr.shape[0], b.width)
    else:
        (k, m), (k2, n) = (a.arr.shape[0], a.width), (b.arr.shape[0], b.width)
    assert k == k2, (a.arr.shape, b.arr.shape, mode)
    tm, tn, tk = _tile(m, tm), _tile(n, tn), _tile(k, tk)
    nk = k // tk
    dims = {"nn": NN, "nt": NT, "tn": TN}[mode]
    if mode == "tn":
        assert a.off % tm == 0
        a_spec = pl.BlockSpec((tk, tm), lambda i, j, kk, o=a.off // tm: (kk, i + o))
    else:
        assert a.off % tk == 0
        a_spec = pl.BlockSpec((tm, tk), lambda i, j, kk, o=a.off // tk: (i, kk + o))
    if mode == "nt":
        assert b.off % tk == 0
        b_spec = pl.BlockSpec((tn, tk), lambda i, j, kk, o=b.off // tk: (j, kk + o))
    else:
        assert b.off % tn == 0
        b_spec = pl.BlockSpec((tk, tn), lambda i, j, kk, o=b.off // tn: (kk, j + o))
    in_specs, operands = [a_spec, b_spec], [a.arr, b.arr]
    if bias is not None:
        in_specs.append(pl.BlockSpec((1, tn), lambda i, j, kk: (0, j)))
        operands.append(bias)
    total_w, o_off = out_cols if out_cols is not None else (n, 0)
    assert o_off % tn == 0
    aliases = {}
    if into is not None:
        assert into.shape == (m, total_w) and into.dtype == out_dtype
        in_specs.append(_any_spec())
        operands.append(into)
        aliases = {len(operands) - 1: 0}
    n_in = len(operands)

    def body(*refs):
        a_ref, b_ref = refs[0], refs[1]
        bias_ref = refs[2] if bias is not None else None
        o_ref = refs[n_in]
        acc_ref = refs[-1] if nk > 1 else None
        part = _dot(a_ref[...].astype(BF16), b_ref[...].astype(BF16), dims)

        def finish(acc):
            if bias_ref is not None:
                acc = acc + bias_ref[...]
            o_ref[...] = acc.astype(out_dtype)

        if nk == 1:
            finish(part)
        else:
            kk = pl.program_id(2)

            @pl.when(kk == 0)
            def _():
                acc_ref[...] = part

            @pl.when(kk > 0)
            def _():
                acc_ref[...] += part

            @pl.when(kk == nk - 1)
            def _():
                finish(acc_ref[...])

    vmem = 2 * (_nbytes((tm, tk), a.arr.dtype) + _nbytes((tk, tn), b.arr.dtype) + _nbytes((tm, tn), out_dtype))
    vmem += 3 * _nbytes((tm, tn), F32)
    (out,), landed = _call(
        body, operands, name=name, out_shape=[jax.ShapeDtypeStruct((m, total_w), out_dtype)],
        grid=(m // tm, n // tn, nk), in_specs=in_specs,
        out_specs=[pl.BlockSpec((tm, tn), lambda i, j, kk, o=o_off // tn: (i, j + o))],
        scratch_shapes=[pltpu.VMEM((tm, tn), F32)] if nk > 1 else [], aliases=aliases,
        semantics=("parallel", "parallel", "arbitrary"), vmem=vmem, rider=rider)
    return out if rider is None else (out, landed)


IN_PROJ_TILE = 256
GATHER_SLOTS = 8


def _gather_order(n_tiles, tile, shard):
    priority = [0, 1, 2, 5, 3, 6, 4, 7]
    rank = {s: k for k, s in enumerate(priority)}
    order = np.zeros((N_DEV, n_tiles), np.int32)
    flags = np.zeros((N_DEV, n_tiles, GATHER_SLOTS), np.int32)
    for dev in range(N_DEV):
        x, y, c = dev >> 2, (dev >> 1) & 1, dev & 1
        blocks = [(x, y, c), (x, y, 1 - c), (1 - x, y, c), (x, 1 - y, c), (1 - x, 1 - y, c),
                  (1 - x, y, 1 - c), (x, 1 - y, 1 - c), (1 - x, 1 - y, 1 - c)]
        slot_of = {4 * bx + 2 * by + bc: s for s, (bx, by, bc) in enumerate(blocks)}

        def needs(t):
            return sorted({slot_of[(t * tile) // shard], slot_of[(t * tile + tile - 1) // shard]})

        seen = set()
        for step, t in enumerate(sorted(range(n_tiles), key=lambda t: (max(rank[s] for s in needs(t)), t))):
            order[dev, step] = t
            for s in needs(t):
                if s not in seen:
                    flags[dev, step, s] = 1
                    seen.add(s)
        assert len(seen) == GATHER_SLOTS
    return order, flags


def _gather_in_proj(h, shard, *, name, rider):
    rows, d = h.shape
    r = shard.shape[0]
    in_w = N_DEV * r
    tn = IN_PROJ_TILE
    n_tiles = in_w // tn
    tm = _tile(rows, 1024, SUBLANES)
    order, flags = _gather_order(n_tiles, tn, r)
    dev = 4 * lax.axis_index("x") + 2 * lax.axis_index("y") + lax.axis_index("c")
    my_order = jnp.asarray(order)[dev]
    my_flags = jnp.asarray(flags.reshape(N_DEV, -1))[dev]

    def body(order_ref, flags_ref, h_ref, shard_hbm, proj_hbm, w_hbm, b_buf, o_buf, send_sems, recv_sems, local_sem,
             b_sem, o_sems):
        s = pl.program_id(0)
        x, y, c = lax.axis_index("x"), lax.axis_index("y"), lax.axis_index("c")
        me, sibling = (x, y, c), (x, y, 1 - c)
        chips = [(1 - x, y), (x, 1 - y), (1 - x, 1 - y)]

        def block(px, py, pc):
            return w_hbm.at[pl.ds((4 * px + 2 * py + pc) * r, r), :]

        def copy(k, blk, to, src=None):
            return pltpu.make_async_remote_copy(
                src_ref=block(*blk) if src is None else src, dst_ref=block(*blk),
                send_sem=send_sems.at[k], recv_sem=recv_sems.at[k], device_id=to, device_id_type=MESH)

        mine = pltpu.make_async_copy(shard_hbm, block(*me), local_sem)
        first = [copy(0, me, sibling, src=shard_hbm)]
        first += [copy(1 + j, me, (*chip, c), src=shard_hbm) for j, chip in enumerate(chips)]
        passed = [copy(4 + j, (*chip, c), sibling) for j, chip in enumerate(chips)]

        @pl.when(s == 0)
        def _():
            for cp in [mine] + first:
                cp.start()

        def due(slot):
            return flags_ref[s * GATHER_SLOTS + slot] == 1

        @pl.when(due(0))
        def _():
            mine.wait()

        @pl.when(due(1))
        def _():
            copy(0, sibling, me).wait_recv()

        for j, chip in enumerate(chips):
            @pl.when(due(2 + j))
            def _(j=j, chip=chip):
                copy(1 + j, (*chip, c), me).wait_recv()
                passed[j].start()

            @pl.when(due(5 + j))
            def _(j=j, chip=chip):
                copy(4 + j, (*chip, 1 - c), me).wait_recv()

        col = pl.multiple_of(order_ref[s] * tn, tn)
        fetch = pltpu.make_async_copy(w_hbm.at[pl.ds(col, tn), :], b_buf, b_sem)
        fetch.start()
        slot = s % 2

        def put(at):
            return pltpu.make_async_copy(o_buf.at[slot], proj_hbm.at[:, pl.ds(at, tn)], o_sems.at[slot])

        @pl.when(s >= 2)
        def _():
            put(0).wait()

        fetch.wait()
        for m in range(rows // tm):
            rs = slice(m * tm, (m + 1) * tm)
            o_buf[slot, rs, :] = _dot(h_ref[rs, :], b_buf[...], NT).astype(BF16)
        put(col).start()

        @pl.when(s == n_tiles - 1)
        def _():
            pltpu.make_async_copy(o_buf.at[1 - slot], proj_hbm.at[:, pl.ds(0, tn)], o_sems.at[1 - slot]).wait()
            put(0).wait()
            for cp in first + passed:
                cp.wait_send()

    smem = pl.BlockSpec(memory_space=pltpu.SMEM)
    vmem = 2 * _nbytes((rows, d), BF16) + 3 * _nbytes((rows, tn), BF16) + _nbytes((tn, d), BF16) + 4 * _nbytes((tm, tn), F32)
    (proj, w_full), landed = _call(
        body, [my_order, my_flags, h, shard], name=name,
        out_shape=[jax.ShapeDtypeStruct((rows, in_w), BF16), jax.ShapeDtypeStruct((in_w, d), BF16)],
        grid=(n_tiles,), in_specs=[smem, smem, pl.BlockSpec((rows, d), lambda s: (0, 0)), _any_spec()],
        out_specs=[_any_spec(), _any_spec()],
        scratch_shapes=[pltpu.VMEM((tn, d), BF16), pltpu.VMEM((2, rows, tn), BF16),
                        pltpu.SemaphoreType.DMA((7,)), pltpu.SemaphoreType.DMA((7,)), pltpu.SemaphoreType.DMA,
                        pltpu.SemaphoreType.DMA, pltpu.SemaphoreType.DMA((2,))],
        semantics=("arbitrary",), vmem=vmem, rider=rider)
    return proj, w_full, landed


def _ew(fn, *, name, rows, width, tiles, vecs=(), outs, accs=0, tl=1024, cw=512, into=None, with_col=False):
    tl, cw = _tile(rows, tl, SUBLANES), _tile(width, cw)
    ncol = width // cw
    nt_, nv = len(tiles), len(vecs)
    into = list(into) if into is not None else [None] * len(outs)
    aliased = [t for t in into if t is not None]

    def off(o):
        assert o % cw == 0, (name, o, cw)
        return o // cw

    in_specs, vmem = [], 0
    for t in tiles:
        arr, o = t[0], off(t[1])
        wrap = t[2] // cw if len(t) > 2 else ncol
        in_specs.append(pl.BlockSpec((tl, cw), lambda j, i, o=o, wrap=wrap: (i, o + j % wrap)))
        vmem += _nbytes((tl, cw), arr.dtype)
    in_specs += [pl.BlockSpec((1, cw), lambda j, i, o=off(o): (0, j + o)) for _, o in vecs]
    in_specs += [_any_spec() for _ in aliased]
    out_shape, out_specs, aliases = [], [], {}
    n_in = nt_ + nv
    for idx, ((dt, tw, o), tgt) in enumerate(zip(outs, into)):
        out_shape.append(jax.ShapeDtypeStruct((rows, tw), dt))
        out_specs.append(pl.BlockSpec((tl, cw), lambda j, i, o=off(o): (i, j + o)))
        vmem += _nbytes((tl, cw), dt)
        if tgt is not None:
            assert tgt.shape == (rows, tw) and tgt.dtype == dt, (name, tgt.shape, tgt.dtype)
            aliases[n_in + len(aliases)] = idx
    for _ in range(accs):
        out_shape.append(jax.ShapeDtypeStruct((1, width), F32))
        out_specs.append(pl.BlockSpec((1, cw), lambda j, i: (0, j)))
    n_out = len(outs)

    def body(*refs):
        vals = [r[...].astype(F32) for r in refs[:n_in]]
        out_refs = refs[n_in + len(aliased):]
        res = fn(pl.program_id(0), *vals) if with_col else fn(*vals)
        res = res if isinstance(res, (tuple, list)) else (res,)
        assert len(res) == n_out + accs, (name, len(res))
        for r, v in zip(out_refs[:n_out], res[:n_out]):
            r[...] = v.astype(r.dtype)
        first = pl.program_id(1) == 0
        for r, v in zip(out_refs[n_out:], res[n_out:]):
            s = jnp.sum(v, axis=0, keepdims=True)

            @pl.when(first)
            def _(r=r, s=s):
                r[...] = s

            @pl.when(jnp.logical_not(first))
            def _(r=r, s=s):
                r[...] += s

    return _pallas(
        body, name=name, out_shape=out_shape, grid=(ncol, rows // tl),
        in_specs=in_specs, out_specs=out_specs, input_output_aliases=aliases,
        compiler_params=_params(("parallel", "arbitrary"), 3 * vmem),
    )(*[t[0] for t in tiles], *[v for v, _ in vecs], *aliased)


def _rmsnorm_fwd(x, w_row, name):
    rows, d = x.shape
    tl = _tile(rows, 512, SUBLANES)

    def body(x_ref, w_ref, h_ref):
        xv = x_ref[...]
        rstd = lax.rsqrt(jnp.mean(xv * xv, axis=-1, keepdims=True) + NORM_EPS)
        h_ref[...] = (xv * rstd * w_ref[...]).astype(BF16)

    return _pallas(
        body, name=name, out_shape=jax.ShapeDtypeStruct((rows, d), BF16), grid=(rows // tl,),
        in_specs=[pl.BlockSpec((tl, d), lambda i: (i, 0)), pl.BlockSpec((1, d), lambda i: (0, 0))],
        out_specs=pl.BlockSpec((tl, d), lambda i: (i, 0)),
        compiler_params=_params(("parallel",)),
    )(x, w_row)


def _rmsnorm_bwd(x, w_row, dh, dout, name, rider=None):
    rows, d = x.shape
    tl = _tile(rows, 256, SUBLANES)

    def body(x_ref, w_ref, dh_ref, dout_ref, gx_ref, gw_ref):
        xv = x_ref[...]
        rstd = lax.rsqrt(jnp.mean(xv * xv, axis=-1, keepdims=True) + NORM_EPS)
        xn = xv * rstd
        dhv = dh_ref[...]
        dxn = dhv * w_ref[...]
        dx = rstd * (dxn - xn * jnp.mean(dxn * xn, axis=-1, keepdims=True))
        gx_ref[...] = dout_ref[...] + dx
        gw = jnp.sum(dhv * xn, axis=0, keepdims=True)

        @pl.when(pl.program_id(0) == 0)
        def _():
            gw_ref[...] = gw

        @pl.when(pl.program_id(0) > 0)
        def _():
            gw_ref[...] += gw

    tile = pl.BlockSpec((tl, d), lambda i: (i, 0))
    row = pl.BlockSpec((1, d), lambda i: (0, 0))
    res, landed = _call(
        body, [x, w_row, dh, dout], name=name,
        out_shape=[jax.ShapeDtypeStruct((rows, d), F32), jax.ShapeDtypeStruct((1, d), F32)],
        grid=(rows // tl,), in_specs=[tile, row, tile, tile], out_specs=[tile, row],
        semantics=("arbitrary",), rider=rider)
    return res if rider is None else (res, landed)


def _head_mean(x, gmat):
    hi = x.astype(BF16)
    lo = (x - hi.astype(F32)).astype(BF16)
    out = []
    for s in range(x.shape[1] // MXU_DIM):
        sl = slice(s * MXU_DIM, (s + 1) * MXU_DIM)
        out.append(_dot(hi[:, sl], gmat, NN) + _dot(lo[:, sl], gmat, NN))
    return out[0] if len(out) == 1 else jnp.concatenate(out, axis=1)


def _head_mean_matrix():
    blk = jnp.arange(MXU_DIM) // HEAD_DIM
    return jnp.where(blk[:, None] == blk[None, :], 1.0 / HEAD_DIM, 0.0).astype(BF16)


def _spread_head(x, g, width):
    col = x[:, (g // 2) * LANES:(g // 2 + 1) * LANES]
    other = pltpu.roll(col, HEAD_DIM, axis=1)
    low = lax.broadcasted_iota(jnp.int32, col.shape, 1) < HEAD_DIM
    both = jnp.where(low, col, other) if g % 2 == 0 else jnp.where(low, other, col)
    return both if width == LANES else jnp.concatenate([both] * (width // LANES), axis=1)


def _head_diagonal(t, per_kv):
    head = lax.broadcasted_iota(jnp.int32, t.shape, 1) // HEAD_DIM
    zero = jnp.zeros_like(t)
    return jnp.concatenate([jnp.where(head == r, t, zero) for r in range(per_kv)], axis=0)


def _fold_heads(x, per_kv):
    rows = x.shape[0] // per_kv
    head = lax.broadcasted_iota(jnp.int32, (rows, x.shape[1]), 1) // HEAD_DIM
    acc = jnp.where(head == 0, x[0:rows], 0.0)
    for r in range(1, per_kv):
        acc = acc + jnp.where(head == r, x[r * rows:(r + 1) * rows], 0.0)
    while acc.shape[1] > LANES:
        half = acc.shape[1] // 2
        acc = acc[:, :half] + acc[:, half:]
    return acc + pltpu.roll(acc, HEAD_DIM, axis=1)


def _join_heads(parts):
    low = lax.broadcasted_iota(jnp.int32, parts[0].shape, 1) < HEAD_DIM
    cols = [jnp.where(low, parts[2 * j], parts[2 * j + 1]) for j in range(len(parts) // 2)]
    return cols[0] if len(cols) == 1 else jnp.concatenate(cols, axis=1)


def _attn_specs(attn_w, kv_w):
    half = attn_w // 2
    kcol, vcol = attn_w // kv_w, attn_w // kv_w + 1
    gcol = (attn_w + 2 * kv_w) // half
    prev = lambda i: jnp.maximum(i - 1, 0)
    return [
        pl.BlockSpec((BLOCK, attn_w), lambda i: (i, 0)),
        pl.BlockSpec((BLOCK, kv_w), lambda i: (prev(i), kcol)),
        pl.BlockSpec((BLOCK, kv_w), lambda i: (i, kcol)),
        pl.BlockSpec((BLOCK, kv_w), lambda i: (prev(i), vcol)),
        pl.BlockSpec((BLOCK, kv_w), lambda i: (i, vcol)),
        pl.BlockSpec((BLOCK, half), lambda i: (i, gcol)),
        pl.BlockSpec((BLOCK, half), lambda i: (i, gcol + 1)),
    ]


def _band_mask(i):
    q_loc = lax.broadcasted_iota(jnp.int32, (BLOCK, 2 * BLOCK), 0) + BLOCK
    k_loc = lax.broadcasted_iota(jnp.int32, (BLOCK, 2 * BLOCK), 1)
    diff = q_loc - k_loc
    first_key = jnp.where(i == 0, BLOCK, 0)
    return (diff >= 0) & (diff < BLOCK) & (k_loc >= first_key)


def _softmax_with_sink(s, sink):
    m = jnp.maximum(jnp.max(s, axis=-1, keepdims=True), sink)
    p = jnp.exp(s - m)
    e_sink = jnp.exp(sink - m)
    den = jnp.sum(p, axis=-1, keepdims=True) + e_sink
    inv = 1.0 / den
    return p * inv, e_sink * inv


def _attn_block(i, q, kk, vv, qw, kw, gmat, sink_ref, per_kv):
    scale = 1.0 / math.sqrt(HEAD_DIM)
    keys = 2 * BLOCK
    valid = _band_mask(i)
    q_rstd = lax.rsqrt(_head_mean(q * q, gmat) + NORM_EPS)
    qn = q * q_rstd
    qh = (qn * qw).astype(BF16)
    k_rstd = lax.rsqrt(_head_mean(kk * kk, gmat) + NORM_EPS)
    kn = kk * k_rstd
    kh = kn * kw
    gw = per_kv * HEAD_DIM
    groups = []
    for g in range(N_KV_HEADS):
        kd = _head_diagonal(_spread_head(kh, g, gw).astype(BF16), per_kv)
        vd = _head_diagonal(_spread_head(vv, g, gw).astype(BF16), per_kv)
        qg = qh[:, g * gw:(g + 1) * gw]
        s_all = _dot(qg, kd, NT) * scale
        ps, p_sinks = [], []
        for r in range(per_kv):
            s = jnp.where(valid, s_all[:, r * keys:(r + 1) * keys], -1e30)
            p, p_sink = _softmax_with_sink(s, sink_ref[g * per_kv + r])
            ps.append(p)
            p_sinks.append(p_sink)
        pb = jnp.concatenate(ps, axis=1).astype(BF16)
        groups.append((kd, vd, qg, ps, p_sinks, pb, _dot(pb, vd, NN)))
    return qn, q_rstd, kn, k_rstd, groups


def _attention_fwd(proj, qw_row, kw_row, gmat, sinks, *, attn_w, kv_w, name):
    rows = proj.shape[0]
    per_kv = attn_w // HEAD_DIM // N_KV_HEADS

    def body(q_ref, kp_ref, kc_ref, vp_ref, vc_ref, glo_ref, ghi_ref, qw_ref, kw_ref, gm_ref, sink_ref, o_ref):
        kk = jnp.concatenate([kp_ref[...], kc_ref[...]], axis=0).astype(F32)
        vv = jnp.concatenate([vp_ref[...], vc_ref[...]], axis=0).astype(F32)
        gate = jnp.concatenate([glo_ref[...], ghi_ref[...]], axis=1).astype(F32)
        *_, groups = _attn_block(pl.program_id(0), q_ref[...].astype(F32), kk, vv, qw_ref[...], kw_ref[...], gm_ref[...],
                                 sink_ref, per_kv)
        attn = jnp.concatenate([grp[-1] for grp in groups], axis=1)
        o_ref[...] = (attn * _silu(gate)).astype(BF16)

    const = lambda a: pl.BlockSpec(a.shape, lambda i: (0, 0))
    return _pallas(
        body, name=name, out_shape=jax.ShapeDtypeStruct((rows, attn_w), BF16), grid=(rows // BLOCK,),
        in_specs=_attn_specs(attn_w, kv_w) + [const(qw_row), const(kw_row), const(gmat),
                                              pl.BlockSpec(memory_space=pltpu.SMEM)],
        out_specs=pl.BlockSpec((BLOCK, attn_w), lambda i: (i, 0)),
        compiler_params=_params(("parallel",), 40 * 1024 * 1024),
    )(proj, proj, proj, proj, proj, proj, proj, qw_row, kw_row, gmat, sinks)


def _attention_bwd(proj, d_ag, dproj, qw_row, kw_row, gmat, sinks, *, attn_w, kv_w, name, rider=None):
    rows = proj.shape[0]
    nb = rows // BLOCK
    per_kv = attn_w // HEAD_DIM // N_KV_HEADS
    gw = per_kv * HEAD_DIM
    keys = 2 * BLOCK
    scale = 1.0 / math.sqrt(HEAD_DIM)
    w_out = 2 * attn_w + 2 * kv_w

    def body(q_ref, kp_ref, kc_ref, vp_ref, vc_ref, glo_ref, ghi_ref, dag_ref, qw_ref, kw_ref, gm_ref, sink_ref, _,
             dp_ref, dkv_ref, gqw_ref, gkw_ref, gs_ref):
        i = pl.program_id(0)
        kk = jnp.concatenate([kp_ref[...], kc_ref[...]], axis=0).astype(F32)
        vv = jnp.concatenate([vp_ref[...], vc_ref[...]], axis=0).astype(F32)
        gate = jnp.concatenate([glo_ref[...], ghi_ref[...]], axis=1).astype(F32)
        d_ag_v = dag_ref[...].astype(F32)
        qw, kw, gmat_v = qw_ref[...], kw_ref[...], gm_ref[...]
        qn, q_rstd, kn, k_rstd, groups = _attn_block(i, q_ref[...].astype(F32), kk, vv, qw, kw, gmat_v, sink_ref,
                                                     per_kv)
        lane = lax.broadcasted_iota(jnp.int32, (SUBLANES, LANES), 1)
        sub = lax.broadcasted_iota(jnp.int32, (SUBLANES, LANES), 0)
        gsink = jnp.zeros((SUBLANES, LANES), F32)
        dq_groups, dgate_groups, dk_heads, dv_heads = [], [], [], []
        for g, (kd, vd, qg, ps, p_sinks, pb, o) in enumerate(groups):
            cs = slice(g * gw, (g + 1) * gw)
            gate_g, d_ag_g = gate[:, cs], d_ag_v[:, cs]
            dgate_groups.append(d_ag_g * o * _dsilu(gate_g))
            do = (d_ag_g * _silu(gate_g)).astype(BF16)
            dp_all = _dot(do, vd, NT)
            dss = []
            for r in range(per_kv):
                p, dp = ps[r], dp_all[:, r * keys:(r + 1) * keys]
                delta = jnp.sum(p * dp, axis=-1, keepdims=True)
                dss.append(p * (dp - delta) * scale)
                gs_h = jnp.sum(-p_sinks[r] * delta, axis=0, keepdims=True)
                gsink = gsink + jnp.where((lane == g * per_kv + r) & (sub == 0), gs_h, 0.0)
            ds = jnp.concatenate(dss, axis=1).astype(BF16)
            dq_groups.append(_dot(ds, kd, NN))
            dk_heads.append(_fold_heads(_dot(ds, qg, TN), per_kv))
            dv_heads.append(_fold_heads(_dot(pb, do, TN), per_kv))
        dqh = jnp.concatenate(dq_groups, axis=1)
        gqw = jnp.sum(dqh * qn, axis=0, keepdims=True)
        dqn = dqh * qw
        dq = q_rstd * (dqn - qn * _head_mean(dqn * qn, gmat_v))
        dkh = _join_heads(dk_heads)
        gkw = jnp.sum(dkh * kn, axis=0, keepdims=True)
        dkn = dkh * kw
        dk = k_rstd * (dkn - kn * _head_mean(dkn * kn, gmat_v))
        dp_ref[:, 0:attn_w] = dq.astype(BF16)
        dp_ref[:, attn_w:attn_w + 2 * kv_w] = jnp.zeros((BLOCK, 2 * kv_w), BF16)
        dp_ref[:, attn_w + 2 * kv_w:w_out] = jnp.concatenate(dgate_groups, axis=1).astype(BF16)
        dkv_ref[0] = jnp.concatenate([dk, _join_heads(dv_heads)], axis=1)

        @pl.when(i == 0)
        def _():
            gqw_ref[...] = gqw
            gkw_ref[...] = gkw
            gs_ref[...] = gsink

        @pl.when(i > 0)
        def _():
            gqw_ref[...] += gqw
            gkw_ref[...] += gkw
            gs_ref[...] += gsink

    const = lambda a: pl.BlockSpec(a.shape, lambda i: (0, 0))
    res, landed = _call(
        body, [proj, proj, proj, proj, proj, proj, proj, d_ag, qw_row, kw_row, gmat, sinks, dproj], name=name,
        out_shape=[jax.ShapeDtypeStruct(dproj.shape, BF16),
                   jax.ShapeDtypeStruct((nb, 2 * BLOCK, 2 * kv_w), F32),
                   jax.ShapeDtypeStruct(qw_row.shape, F32), jax.ShapeDtypeStruct(kw_row.shape, F32),
                   jax.ShapeDtypeStruct((SUBLANES, LANES), F32)],
        grid=(nb,),
        in_specs=_attn_specs(attn_w, kv_w) + [pl.BlockSpec((BLOCK, attn_w), lambda i: (i, 0)), const(qw_row),
                                              const(kw_row), const(gmat), pl.BlockSpec(memory_space=pltpu.SMEM),
                                              _any_spec()],
        out_specs=[pl.BlockSpec((BLOCK, w_out), lambda i: (i, 0)),
                   pl.BlockSpec((1, 2 * BLOCK, 2 * kv_w), lambda i: (i, 0, 0)),
                   const(qw_row), const(kw_row), pl.BlockSpec((SUBLANES, LANES), lambda i: (0, 0))],
        aliases={12: 0}, semantics=("arbitrary",), vmem=48 * 1024 * 1024, rider=rider)
    return res if rider is None else (res, landed)


def _attention_dkv(dproj, dkv, *, attn_w, kv_w, name):
    rows = dproj.shape[0]
    nb = rows // BLOCK
    col = attn_w // (2 * kv_w)

    def body(cur_ref, nxt_ref, _, o_ref):
        i = pl.program_id(0)
        nxt = jnp.where(i < nb - 1, nxt_ref[0, 0:BLOCK, :], 0.0)
        o_ref[...] = (cur_ref[0, BLOCK:2 * BLOCK, :] + nxt).astype(BF16)

    blk = lambda f: pl.BlockSpec((1, 2 * BLOCK, 2 * kv_w), f)
    return _pallas(
        body, name=name, out_shape=jax.ShapeDtypeStruct(dproj.shape, BF16), grid=(nb,),
        in_specs=[blk(lambda i: (i, 0, 0)), blk(lambda i: (jnp.minimum(i + 1, nb - 1), 0, 0)), _any_spec()],
        out_specs=pl.BlockSpec((BLOCK, 2 * kv_w), lambda i: (i, col)),
        input_output_aliases={2: 0},
        compiler_params=_params(("parallel",)),
    )(dkv, dkv, dproj)


def _cmul(ar, ai, br, bi):
    return ar * br - ai * bi, ar * bi + ai * br


def _ssm_prep(a_re, a_im, log_dt_col, steps, name):
    assert steps & (steps - 1) == 0

    def body(are_ref, aim_ref, ldt_ref, abr_ref, abi_ref, cfr_ref, cfi_ref, apr_ref, api_ref):
        are, aim = are_ref[...], aim_ref[...]
        dt = jnp.exp(ldt_ref[...])
        mag = jnp.exp(dt * are)
        abr = mag * jnp.cos(dt * aim)
        abi = mag * jnp.sin(dt * aim)
        num_re, num_im = abr - 1.0, abi
        den = are * are + aim * aim
        abr_ref[...] = abr
        abi_ref[...] = abi
        cfr_ref[...] = (num_re * are + num_im * aim) / den
        cfi_ref[...] = (num_im * are - num_re * aim) / den
        pr, pi = abr, abi
        n = steps
        while n > 1:
            pr, pi = _cmul(pr, pi, pr, pi)
            n //= 2
        apr_ref[...] = pr
        api_ref[...] = pi

    shp = jax.ShapeDtypeStruct(a_re.shape, F32)
    return _pallas(body, name=name, out_shape=[shp] * 6)(a_re, a_im, log_dt_col)


def _ssm_param_bwd(a_re, a_im, log_dt_col, d_ab_re, d_ab_im, d_cf_re, d_cf_im, name):
    def body(are_ref, aim_ref, ldt_ref, gabr_ref, gabi_ref, gcfr_ref, gcfi_ref, dar_ref, dai_ref, dldt_ref):
        are, aim = are_ref[...], aim_ref[...]
        dt = jnp.exp(ldt_ref[...])
        mag = jnp.exp(dt * are)
        abr = mag * jnp.cos(dt * aim)
        abi = mag * jnp.sin(dt * aim)
        den = are * are + aim * aim
        cfr = ((abr - 1.0) * are + abi * aim) / den
        cfi = (abi * are - (abr - 1.0) * aim) / den
        gabr, gabi = jnp.sum(gabr_ref[...], axis=0), jnp.sum(gabi_ref[...], axis=0)
        gcfr, gcfi = jnp.sum(gcfr_ref[...], axis=0), jnp.sum(gcfi_ref[...], axis=0)
        inv_r, inv_i = are / den, -aim / den
        t_r, t_i = _cmul(inv_r, -inv_i, gcfr, gcfi)
        gabr, gabi = gabr + t_r, gabi + t_i
        q_r, q_i = _cmul(cfr, cfi, inv_r, inv_i)
        da_r, da_i = _cmul(-q_r, q_i, gcfr, gcfi)
        gz_r, gz_i = _cmul(abr, -abi, gabr, gabi)
        dar_ref[...] = da_r + dt * gz_r
        dai_ref[...] = da_i + dt * gz_i
        dldt_ref[...] = dt * jnp.sum(are * gz_r + aim * gz_i, axis=-1, keepdims=True)

    shp = jax.ShapeDtypeStruct(a_re.shape, F32)
    return _pallas(body, name=name, out_shape=[shp, shp, jax.ShapeDtypeStruct(log_dt_col.shape, F32)])(
        a_re, a_im, log_dt_col, d_ab_re, d_ab_im, d_cf_re, d_cf_im)


SCAN_LANES = 512
W_IN_GRAD_PARTS = 2


def _scan_segments(xr_ref, xi_ref, a_re, a_im, ap_re, ap_im, carry_re, carry_im, cm_re, cm_im, steps, reverse):
    n = xr_ref.shape[1]
    order = range(steps - 1, -1, -1) if reverse else range(steps)
    seg_order = range(SUBLANES - 1, -1, -1) if reverse else range(SUBLANES)
    for c0 in range(0, n, SCAN_LANES):
        ls = slice(c0, c0 + SCAN_LANES)
        ar = jnp.broadcast_to(a_re[:, ls], (SUBLANES, SCAN_LANES))
        ai = jnp.broadcast_to(a_im[:, ls], (SUBLANES, SCAN_LANES))

        def local(t, s, ar=ar, ai=ai, ls=ls):
            j = steps - 1 - t if reverse else t
            r0 = pl.multiple_of(j * SUBLANES, SUBLANES)
            sr, si = _cmul(ar, ai, s[0], s[1])
            sr = sr + xr_ref[pl.ds(r0, SUBLANES), ls]
            si = si + xi_ref[pl.ds(r0, SUBLANES), ls]
            xr_ref[pl.ds(r0, SUBLANES), ls] = sr
            xi_ref[pl.ds(r0, SUBLANES), ls] = si
            return sr, si

        zero = jnp.zeros((SUBLANES, SCAN_LANES), F32)
        end_r, end_i = lax.fori_loop(0, steps, local, (zero, zero))
        cr, ci = carry_re[:, ls], carry_im[:, ls]
        apr, api = ap_re[:, ls], ap_im[:, ls]
        for r in seg_order:
            cm_re[r:r + 1, ls] = cr
            cm_im[r:r + 1, ls] = ci
            tr, ti = _cmul(apr, api, cr, ci)
            cr, ci = end_r[r:r + 1, :] + tr, end_i[r:r + 1, :] + ti
        carry_re[:, ls] = cr
        carry_im[:, ls] = ci

        def fix(t, s, ar=ar, ai=ai, ls=ls):
            j = steps - 1 - t if reverse else t
            r0 = pl.multiple_of(j * SUBLANES, SUBLANES)
            sr, si = _cmul(ar, ai, s[0], s[1])
            xr_ref[pl.ds(r0, SUBLANES), ls] += sr
            xi_ref[pl.ds(r0, SUBLANES), ls] += si
            return sr, si

        lax.fori_loop(0, steps, fix, (cm_re[:, ls], cm_im[:, ls]))
    del order


SB_GROUPS = MXU_DIM // GROUP
SB_STATE = SB_GROUPS * STATE


def _ssm_rows(b_re, b_im, c_re, c_im):
    def rows(m):
        flat = m.reshape(-1, STATE).astype(F32)
        return jnp.concatenate([flat, flat], axis=1)
    return rows(b_re.transpose(0, 2, 1)), rows(b_im.transpose(0, 2, 1)), rows(c_re), rows(c_im)


def _from_ssm_rows(rows, transpose):
    g = rows[:, :STATE].reshape(-1, GROUP, STATE)
    return g.transpose(0, 2, 1) if transpose else g


def _own_group(shape):
    row_g = lax.broadcasted_iota(jnp.int32, shape, 0) // GROUP
    col_g = lax.broadcasted_iota(jnp.int32, shape, 1) // STATE
    return row_g == col_g


def _block_diagonal(rows):
    tiled = jnp.concatenate([rows] * (SB_STATE // LANES), axis=1)
    return jnp.where(_own_group(tiled.shape), tiled, 0.0).astype(BF16)


def _block_rows(acc):
    x = jnp.where(_own_group(acc.shape), acc, 0.0)
    while x.shape[1] > LANES:
        half = x.shape[1] // 2
        x = x[:, :half] + x[:, half:]
    return x + pltpu.roll(x, STATE, axis=1)


def _rows_to_segments(dst, srcs, steps, stage):
    for ref, off in srcs:
        for k in range(ref.shape[1] // LANES):
            stage[off // LANES + k] = ref[:, k * LANES:(k + 1) * LANES].astype(F32)
    for k in range(dst.shape[1] // LANES):
        for j in range(steps):
            dst[j * SUBLANES:(j + 1) * SUBLANES, k * LANES:(k + 1) * LANES] = (
                stage[k, pl.ds(j, SUBLANES, stride=steps), :])


def _segments_to_rows(dst, src, steps, stage):
    for k in range(src.shape[1] // LANES):
        for j in range(steps):
            stage[k, pl.ds(j, SUBLANES, stride=steps), :] = (
                src[j * SUBLANES:(j + 1) * SUBLANES, k * LANES:(k + 1) * LANES])
    for k in range(src.shape[1] // LANES):
        dst[:, k * LANES:(k + 1) * LANES] = stage[k]


def _u_specs(w, o_u, chunk, index):
    half = w // 2
    assert o_u % half == 0
    return [pl.BlockSpec((chunk, half), lambda c, k=k: (index(c), o_u // half + k)) for k in range(2)]


def _ssm_fwd(proj, o_u, bc_rows, rows_p, d_row, *, chunk, name, rider=None):
    rows = proj.shape[0]
    w = d_row.shape[1]
    nc = rows // chunk
    steps = chunk // SUBLANES
    nsb = w // MXU_DIM
    n_state = nsb * SB_STATE

    def body(ulo_ref, uhi_ref, b2r_ref, b2i_ref, c2r_ref, c2i_ref, abr_ref, abi_ref, cfr_ref, cfi_ref, apr_ref,
             api_ref, d_ref, y_ref, str_ref, sti_ref, bre_ref, bim_ref, cre_ref, cim_ref, useg, yseg, stage, sr, si,
             carry_r, carry_i, cm_r, cm_i):
        @pl.when(pl.program_id(0) == 0)
        def _():
            for src, dst in ((b2r_ref, bre_ref), (b2i_ref, bim_ref), (c2r_ref, cre_ref), (c2i_ref, cim_ref)):
                for sb in range(nsb):
                    dst[sb] = _block_diagonal(src[sb * MXU_DIM:(sb + 1) * MXU_DIM, :])
            carry_r[...] = jnp.zeros_like(carry_r)
            carry_i[...] = jnp.zeros_like(carry_i)

        str_ref[0] = carry_r[...]
        sti_ref[0] = carry_i[...]
        _rows_to_segments(useg, [(ulo_ref, 0), (uhi_ref, w // 2)], steps, stage)
        for sb in range(nsb):
            us = slice(sb * MXU_DIM, (sb + 1) * MXU_DIM)
            ss = slice(sb * SB_STATE, (sb + 1) * SB_STATE)
            ub = useg[:, us].astype(BF16)
            bur = _dot(ub, bre_ref[sb], NN)
            bui = _dot(ub, bim_ref[sb], NN)
            xr, xi = _cmul(cfr_ref[:, ss], cfi_ref[:, ss], bur, bui)
            sr[:, ss] = xr
            si[:, ss] = xi
        _scan_segments(sr, si, abr_ref[...], abi_ref[...], apr_ref[...], api_ref[...],
                       carry_r, carry_i, cm_r, cm_i, steps, False)
        for sb in range(nsb):
            us = slice(sb * MXU_DIM, (sb + 1) * MXU_DIM)
            ss = slice(sb * SB_STATE, (sb + 1) * SB_STATE)
            y = _dot(sr[:, ss].astype(BF16), cre_ref[sb], NT) - _dot(si[:, ss].astype(BF16), cim_ref[sb], NT)
            yseg[:, us] = y + d_ref[:, us] * useg[:, us]
        _segments_to_rows(y_ref, yseg, steps, stage)

    const = lambda a: pl.BlockSpec(a.shape, lambda c: (0,) * a.ndim)
    row_n = pl.BlockSpec((1, n_state), lambda c: (0, 0))
    st = pl.BlockSpec((1, 1, n_state), lambda c: (c, 0, 0))
    held = [pltpu.VMEM((nsb, MXU_DIM, SB_STATE), BF16)] * 4
    vmem = 4 * _nbytes((nsb, MXU_DIM, SB_STATE), BF16) + 3 * _nbytes((chunk, n_state), F32)
    res, landed = _call(
        body, [proj, proj, *bc_rows, *rows_p, d_row], name=name,
        out_shape=[jax.ShapeDtypeStruct((rows, w), F32), jax.ShapeDtypeStruct((nc, 1, n_state), F32),
                   jax.ShapeDtypeStruct((nc, 1, n_state), F32)],
        grid=(nc,),
        in_specs=_u_specs(w, o_u, chunk, lambda c: c) + [const(b) for b in bc_rows]
        + [row_n] * 6 + [pl.BlockSpec((1, w), lambda c: (0, 0))],
        out_specs=[pl.BlockSpec((chunk, w), lambda c: (c, 0)), st, st],
        scratch_shapes=held + [pltpu.VMEM((chunk, w), F32), pltpu.VMEM((chunk, w), F32),
                               pltpu.VMEM((w // LANES, chunk, LANES), F32),
                               pltpu.VMEM((chunk, n_state), F32), pltpu.VMEM((chunk, n_state), F32),
                               pltpu.VMEM((1, n_state), F32), pltpu.VMEM((1, n_state), F32),
                               pltpu.VMEM((SUBLANES, n_state), F32), pltpu.VMEM((SUBLANES, n_state), F32)],
        semantics=("arbitrary",), vmem=vmem, rider=rider)
    return res if rider is None else (res, landed)


def _ssm_bwd(proj, o_u, y, dyg, st_re, st_im, bc_rows, rows_p, d_row, *, chunk, name, rider=None):
    rows = proj.shape[0]
    w = d_row.shape[1]
    nc = rows // chunk
    steps = chunk // SUBLANES
    nsb = w // MXU_DIM
    n_state = nsb * SB_STATE

    def body(ulo_ref, uhi_ref, y_ref, dyg_ref, str_ref, sti_ref, b2r_ref, b2i_ref, c2r_ref, c2i_ref,
             abr_ref, abi_ref, cfr_ref, cfi_ref, apr_ref, api_ref, d_ref,
             du_ref, gb2r_ref, gb2i_ref, gc2r_ref, gc2i_ref, gabr_ref, gabi_ref, gcfr_ref, gcfi_ref, dd_ref,
             bre_ref, bim_ref, cre_ref, cim_ref, dbre_ref, dbim_ref, dcre_ref, dcim_ref, useg, dyseg, dynat, stage,
             bur, bui, sr, si, lr, li, carry_r, carry_i, lam_r, lam_i, cm_r, cm_i, cl_r, cl_i):
        first = pl.program_id(0) == 0

        @pl.when(first)
        def _():
            for src, dst in ((b2r_ref, bre_ref), (b2i_ref, bim_ref), (c2r_ref, cre_ref), (c2i_ref, cim_ref)):
                for sb in range(nsb):
                    dst[sb] = _block_diagonal(src[sb * MXU_DIM:(sb + 1) * MXU_DIM, :])
            lam_r[...] = jnp.zeros_like(lam_r)
            lam_i[...] = jnp.zeros_like(lam_i)
            for ref in (dbre_ref, dbim_ref, dcre_ref, dcim_ref, gabr_ref, gabi_ref, gcfr_ref, gcfi_ref, dd_ref):
                ref[...] = jnp.zeros_like(ref)

        dynat[...] = dyg_ref[...].astype(F32) * _dgelu(y_ref[...])
        half = w // 2
        dd_ref[:, :half] += jnp.sum(dynat[:, :half] * ulo_ref[...].astype(F32), axis=0, keepdims=True)
        dd_ref[:, half:] += jnp.sum(dynat[:, half:] * uhi_ref[...].astype(F32), axis=0, keepdims=True)
        _rows_to_segments(useg, [(ulo_ref, 0), (uhi_ref, half)], steps, stage)
        _rows_to_segments(dyseg, [(dynat, 0)], steps, stage)
        dy = dyseg[...]
        dyb = dy.astype(BF16)
        ub = useg[...].astype(BF16)
        carry_r[...] = str_ref[0]
        carry_i[...] = sti_ref[0]
        for sb in range(nsb):
            us = slice(sb * MXU_DIM, (sb + 1) * MXU_DIM)
            ss = slice(sb * SB_STATE, (sb + 1) * SB_STATE)
            br = _dot(ub[:, us], bre_ref[sb], NN)
            bi = _dot(ub[:, us], bim_ref[sb], NN)
            bur[:, ss] = br
            bui[:, ss] = bi
            xr, xi = _cmul(cfr_ref[:, ss], cfi_ref[:, ss], br, bi)
            sr[:, ss] = xr
            si[:, ss] = xi
            lr[:, ss] = _dot(dyb[:, us], cre_ref[sb], NN)
            li[:, ss] = -_dot(dyb[:, us], cim_ref[sb], NN)
        abr, abi = abr_ref[...], abi_ref[...]
        apr, api = apr_ref[...], api_ref[...]
        _scan_segments(sr, si, abr, abi, apr, api, carry_r, carry_i, cm_r, cm_i, steps, False)
        for sb in range(nsb):
            us = slice(sb * MXU_DIM, (sb + 1) * MXU_DIM)
            ss = slice(sb * SB_STATE, (sb + 1) * SB_STATE)
            dcre_ref[sb] += _dot(dyb[:, us], sr[:, ss].astype(BF16), TN)
            dcim_ref[sb] -= _dot(dyb[:, us], si[:, ss].astype(BF16), TN)
        _scan_segments(lr, li, abr, -abi, apr, -api, lam_r, lam_i, cl_r, cl_i, steps, True)
        for c0 in range(0, n_state, SCAN_LANES):
            ls = slice(c0, c0 + SCAN_LANES)
            cfr = jnp.broadcast_to(cfr_ref[:, ls], (SUBLANES, SCAN_LANES))
            cfi = jnp.broadcast_to(cfi_ref[:, ls], (SUBLANES, SCAN_LANES))

            def step(j, acc, ls=ls, cfr=cfr, cfi=cfi):
                gar, gai, gcr, gci, pr, pi = acc
                r0 = pl.multiple_of(j * SUBLANES, SUBLANES)
                rws = pl.ds(r0, SUBLANES)
                l_r, l_i = lr[rws, ls], li[rws, ls]
                t_r, t_i = _cmul(pr, -pi, l_r, l_i)
                b_r, b_i = bur[rws, ls], bui[rws, ls]
                c_r, c_i = _cmul(b_r, -b_i, l_r, l_i)
                x_r, x_i = _cmul(cfr, -cfi, l_r, l_i)
                bur[rws, ls] = x_r
                bui[rws, ls] = x_i
                return gar + t_r, gai + t_i, gcr + c_r, gci + c_i, sr[rws, ls], si[rws, ls]

            zero = jnp.zeros((SUBLANES, SCAN_LANES), F32)
            gar, gai, gcr, gci, _, _ = lax.fori_loop(
                0, steps, step, (zero, zero, zero, zero, cm_r[:, ls], cm_i[:, ls]))
            gabr_ref[:, ls] += gar
            gabi_ref[:, ls] += gai
            gcfr_ref[:, ls] += gcr
            gcfi_ref[:, ls] += gci
        for sb in range(nsb):
            us = slice(sb * MXU_DIM, (sb + 1) * MXU_DIM)
            ss = slice(sb * SB_STATE, (sb + 1) * SB_STATE)
            xr, xi = bur[:, ss].astype(BF16), bui[:, ss].astype(BF16)
            du = _dot(xr, bre_ref[sb], NT) + _dot(xi, bim_ref[sb], NT)
            useg[:, us] = du + d_ref[:, us] * dy[:, us]
            dbre_ref[sb] += _dot(ub[:, us], xr, TN)
            dbim_ref[sb] += _dot(ub[:, us], xi, TN)
        _segments_to_rows(du_ref, useg, steps, stage)

        @pl.when(pl.program_id(0) == nc - 1)
        def _():
            for src, dst in ((dbre_ref, gb2r_ref), (dbim_ref, gb2i_ref), (dcre_ref, gc2r_ref), (dcim_ref, gc2i_ref)):
                for sb in range(nsb):
                    dst[sb * MXU_DIM:(sb + 1) * MXU_DIM, :] = _block_rows(src[sb])

    rev = lambda c: nc - 1 - c
    const = lambda a: pl.BlockSpec(a.shape, lambda c: (0,) * a.ndim)
    tile = pl.BlockSpec((chunk, w), lambda c: (rev(c), 0))
    row_n = pl.BlockSpec((1, n_state), lambda c: (0, 0))
    row_w = pl.BlockSpec((1, w), lambda c: (0, 0))
    st = pl.BlockSpec((1, 1, n_state), lambda c: (rev(c), 0, 0))
    acc8 = pl.BlockSpec((SUBLANES, n_state), lambda c: (0, 0))
    big = pltpu.VMEM((chunk, n_state), F32)
    small = pltpu.VMEM((chunk, w), F32)
    row = pltpu.VMEM((1, n_state), F32)
    eight = pltpu.VMEM((SUBLANES, n_state), F32)
    blk = (nsb, MXU_DIM, SB_STATE)
    held = [pltpu.VMEM(blk, BF16)] * 4 + [pltpu.VMEM(blk, F32)] * 4
    vmem = (4 * (_nbytes(blk, BF16) + _nbytes(blk, F32)) + 7 * _nbytes((chunk, n_state), F32)
            + 20 * _nbytes((chunk, w), F32) + 16 * _nbytes(bc_rows[0].shape, F32))
    res, landed = _call(
        body, [proj, proj, y, dyg, st_re, st_im, *bc_rows, *rows_p, d_row], name=name,
        out_shape=[jax.ShapeDtypeStruct((rows, w), F32)] + [jax.ShapeDtypeStruct(b.shape, F32) for b in bc_rows]
        + [jax.ShapeDtypeStruct((SUBLANES, n_state), F32)] * 4 + [jax.ShapeDtypeStruct((1, w), F32)],
        grid=(nc,),
        in_specs=_u_specs(w, o_u, chunk, rev) + [tile, tile, st, st] + [const(b) for b in bc_rows]
        + [row_n] * 6 + [row_w],
        out_specs=[tile] + [const(b) for b in bc_rows] + [acc8] * 4 + [row_w],
        scratch_shapes=held + [small] * 3 + [pltpu.VMEM((w // LANES, chunk, LANES), F32)] + [big] * 6 + [row] * 4
        + [eight] * 4,
        semantics=("arbitrary",), vmem=vmem, rider=rider)
    return res if rider is None else (res, landed)


def _loss_grad(x, mm, target, name):
    rows, d = x.shape

    def fn(xv, mv, tv):
        err = xv + mv - tv
        g = err * (1.0 / d)
        return g, g, 0.5 * err * g

    return _ew(fn, name=name, rows=rows, width=d, tiles=[(x, 0), (mm, 0), (target, 0)],
               outs=[(F32, d, 0), (BF16, d, 0)], accs=1)


def _pair_sum(grad, recv, name):
    r4, cdim = recv.shape
    r = r4 // N_CHIPS
    tr = _tile(r, 544, 16)
    g4 = grad.reshape(N_CHIPS, 2, r, cdim)
    r3 = recv.reshape(N_CHIPS, r, cdim)
    core = jnp.reshape(lax.axis_index("c"), (1,)).astype(jnp.int32)

    def body(c_ref, g_ref, r_ref, o_ref):
        o_ref[...] = (g_ref[0] + r_ref[...]).astype(BF16)

    out = _pallas(
        body, name=name, out_shape=jax.ShapeDtypeStruct((N_CHIPS, r, cdim), BF16),
        grid_spec=pltpu.PrefetchScalarGridSpec(
            num_scalar_prefetch=1, grid=(N_CHIPS, r // tr),
            in_specs=[pl.BlockSpec((1, 1, tr, cdim), lambda j, i, c: (j, c[0], i, 0)),
                      pl.BlockSpec((1, tr, cdim), lambda j, i, c: (j, i, 0))],
            out_specs=pl.BlockSpec((1, tr, cdim), lambda j, i, c: (j, i, 0))),
        compiler_params=_params(("parallel", "parallel"), 6 * _nbytes((tr, cdim), F32)),
    )(core, g4, r3)
    return out.reshape(r4, cdim)


def _chip_sum(recv, name):
    r4, cdim = recv.shape
    r = r4 // N_CHIPS
    tr = _tile(r, 544, 16)
    r3 = recv.reshape(N_CHIPS, r, cdim)

    def body(r_ref, o_ref):
        acc = r_ref[0].astype(F32)
        for j in range(1, N_CHIPS):
            acc = acc + r_ref[j].astype(F32)
        o_ref[...] = acc

    return _pallas(
        body, name=name, out_shape=jax.ShapeDtypeStruct((r, cdim), F32), grid=(r // tr,),
        in_specs=[pl.BlockSpec((N_CHIPS, tr, cdim), lambda i: (0, i, 0))],
        out_specs=pl.BlockSpec((tr, cdim), lambda i: (i, 0)),
        compiler_params=_params(("parallel",), 8 * _nbytes((tr, cdim), F32)),
    )(r3)


def _adamw_math(w, g, m, v):
    m = ADAM_B1 * m + (1.0 - ADAM_B1) * g
    v = ADAM_B2 * v + (1.0 - ADAM_B2) * (g * g)
    m_hat = m / (1.0 - ADAM_B1 ** ADAM_STEP)
    v_hat = v / (1.0 - ADAM_B2 ** ADAM_STEP)
    delta = -ADAM_LR * (m_hat / (jnp.sqrt(v_hat) + ADAM_EPS) + ADAM_WD * w)
    return delta, m, v


def _adamw(w, g, m, v, name):
    rows, cols = w.shape
    tr = _tile(rows, 256, SUBLANES)

    def body(w_ref, g_ref, m_ref, v_ref, d_ref, nm_ref, nv_ref):
        d, nm, nv = _adamw_math(w_ref[...], g_ref[...], m_ref[...], v_ref[...])
        d_ref[...] = d
        nm_ref[...] = nm
        nv_ref[...] = nv

    spec = pl.BlockSpec((tr, cols), lambda i: (i, 0))
    shp = jax.ShapeDtypeStruct((rows, cols), F32)
    return _pallas(
        body, name=name, out_shape=[shp] * 3, grid=(rows // tr,), in_specs=[spec] * 4, out_specs=[spec] * 3,
        compiler_params=_params(("parallel",)),
    )(w, g, m, v)


def _adamw_small(w, parts, m, v, name):
    rows, cols = w.shape
    p3 = parts.reshape(N_DEV, rows, cols)

    def body(w_ref, p_ref, m_ref, v_ref, g_ref, d_ref, nm_ref, nv_ref):
        g = p_ref[0]
        for k in range(1, N_DEV):
            g = g + p_ref[k]
        d, nm, nv = _adamw_math(w_ref[...], g, m_ref[...], v_ref[...])
        g_ref[...] = g
        d_ref[...] = d
        nm_ref[...] = nm
        nv_ref[...] = nv

    shp = jax.ShapeDtypeStruct((rows, cols), F32)
    return _pallas(body, name=name, out_shape=[shp] * 4)(w, p3, m, v)


SMALL = ("norm_w", "q_norm_w", "k_norm_w", "sinks", "A_re", "A_im", "log_dt", "B_re", "B_im", "C_re", "C_im",
         "D_skip", "b_glu")
LARGE = ("w_in", "w_attn_proj", "w_glu", "w_ssm_proj", "w_out")
ORDER = ("norm_w", "w_in", "q_norm_w", "k_norm_w", "sinks", "w_attn_proj", "A_re", "A_im", "log_dt", "B_re", "B_im",
         "C_re", "C_im", "D_skip", "w_glu", "b_glu", "w_ssm_proj", "w_out")


SMALL_REST = ("loss",) + SMALL[1:]


def _pack(named, keys):
    flat = jnp.concatenate([named[k].reshape(-1).astype(F32) for k in keys])
    n = flat.shape[0]
    rows = -(-n // (LANES * SUBLANES)) * SUBLANES
    return jnp.pad(flat, (0, rows * LANES - n)).reshape(rows, LANES)


def _unpack(packed, like, keys):
    flat = packed.reshape(-1)
    out, o = {}, 0
    for k in keys:
        n = like[k].size
        out[k] = flat[o:o + n].reshape(like[k].shape)
        o += n
    return out


def _step(xs, target, p, shards):
    s_in, s_ap, s_glu, s_sp, s_o = shards
    seq, d = xs.shape
    attn_w = (d // 128) * HEAD_DIM
    n_q = attn_w // HEAD_DIM
    kv_w = N_KV_HEADS * HEAD_DIM
    ssm_w = d // 2
    n_groups = ssm_w // GROUP
    n_state = n_groups * STATE
    in_w = N_DEV * s_in.shape[0]
    assert in_w == 2 * attn_w + 2 * kv_w + 2 * ssm_w + 2 * d
    o_u = 2 * attn_w + 2 * kv_w
    o_z = o_u + ssm_w
    o_ga = o_z + ssm_w
    chunk = min(BLOCK, seq)
    cw = d // 4

    norm_row = p["norm_w"].reshape(1, d)
    h = _rmsnorm_fwd(xs, norm_row, "rmsnorm_fwd")
    proj, w_in_t, _ = _gather_in_proj(h, s_in, name="gather_in_proj", rider=None)
    qw_row = jnp.tile(p["q_norm_w"], n_q).reshape(1, attn_w)
    kw_row = jnp.tile(p["k_norm_w"], N_KV_HEADS).reshape(1, kv_w)
    gmat = _head_mean_matrix()
    ag = _attention_fwd(proj, qw_row, kw_row, gmat, p["sinks"], attn_w=attn_w, kv_w=kv_w, name="attention_fwd")

    log_dt_col = p["log_dt"].reshape(n_groups, 1)
    prep = _ssm_prep(p["A_re"], p["A_im"], log_dt_col, chunk // SUBLANES, "ssm_prep")
    rows_p = [v.reshape(1, n_state) for v in prep]
    bc_rows = _ssm_rows(p["B_re"], p["B_im"], p["C_re"], p["C_im"])
    d_row = p["D_skip"].reshape(1, ssm_w)
    (y_ssm, st_re, st_im), (w_ap_t, w_glu_t, w_sp_t, w_o) = _ssm_fwd(
        proj, o_u, bc_rows, rows_p, d_row, chunk=chunk, name="ssm_fwd", rider=_all_gather([s_ap, s_glu, s_sp, s_o]))
    (yg,) = _ew(_gelu, name="gelu", rows=seq, width=ssm_w, tiles=[(y_ssm, 0)], outs=[(BF16, ssm_w, 0)], cw=cw)
    glu = _matmul(yg, w_glu_t, mode="nt", name="glu_proj", out_dtype=BF16, bias=p["b_glu"].reshape(1, 2 * ssm_w))
    (ts,) = _ew(lambda ga, gb, z: ga * _sigmoid(gb) * _silu(z), name="glu_gate", rows=seq, width=ssm_w,
                tiles=[(glu, 0), (glu, ssm_w), (proj, o_z)], outs=[(BF16, ssm_w, 0)], cw=cw)
    yy = _matmul(ag, w_ap_t, mode="nt", name="attn_proj", out_dtype=BF16, out_cols=(2 * d, 0))
    yy = _matmul(ts, w_sp_t, mode="nt", name="ssm_proj", out_dtype=BF16, out_cols=(2 * d, d), into=yy)
    (merged,) = _ew(lambda ya, ys, ga, gs: _sigmoid(ga) * ya + _sigmoid(gs) * ys, name="merge", rows=seq, width=d,
                    tiles=[(yy, 0), (yy, d), (proj, o_ga), (proj, o_ga + d)], outs=[(BF16, d, 0)], cw=cw)
    mm = _matmul(merged, w_o, mode="nn", name="out_proj")
    dout, dout_b, loss_cols = _loss_grad(xs, mm, target, "loss_grad")
    loss_local = jnp.sum(loss_cols)

    g_w_o = _matmul(merged, dout_b, mode="tn", name="grad_w_out", tm=512, tk=4096)
    dmerged, (sib_o,) = _matmul(dout_b, w_o, mode="nt", name="d_merged", out_dtype=BF16,
                                rider=_sibling_exchange([g_w_o]))
    pair_o = _pair_sum(g_w_o, sib_o, "pair_sum_w_out")

    def merge_bwd(dm, y, g):
        s = _sigmoid(g)
        return dm * s, dm * y * s * (1.0 - s)

    dyy, dproj = _ew(merge_bwd, name="merge_bwd", rows=seq, width=2 * d,
                     tiles=[(dmerged, 0, d), (yy, 0), (proj, o_ga)],
                     outs=[(BF16, 2 * d, 0), (BF16, in_w, o_ga)], cw=cw)
    dy_a, dy_s = Cols(dyy, 0, d), Cols(dyy, d, d)
    g_w_ap_t = _matmul(dy_a, ag, mode="tn", name="grad_w_attn_proj", tm=512, tk=4096)
    g_w_sp_t = _matmul(dy_s, ts, mode="tn", name="grad_w_ssm_proj", tm=512, tk=4096)
    d_ag = _matmul(dy_a, w_ap_t, mode="nn", name="d_attn_gated", out_dtype=BF16)
    d_ts = _matmul(dy_s, w_sp_t, mode="nn", name="d_ssm_gated", out_dtype=BF16)

    (dproj, dkv, g_qw, g_kw, g_sinks), (chips_o, sib_ap, sib_sp) = _attention_bwd(
        proj, d_ag, dproj, qw_row, kw_row, gmat, p["sinks"], attn_w=attn_w, kv_w=kv_w, name="attention_bwd",
        rider=_join(_chip_exchange([pair_o]), _sibling_exchange([g_w_ap_t, g_w_sp_t])))
    pair_ap = _pair_sum(g_w_ap_t, sib_ap, "pair_sum_w_attn_proj")
    pair_sp = _pair_sum(g_w_sp_t, sib_sp, "pair_sum_w_ssm_proj")
    dproj = _attention_dkv(dproj, dkv, attn_w=attn_w, kv_w=kv_w, name="attention_dkv")

    n_half = ssm_w // _tile(2 * ssm_w, cw)

    def glu_bwd(j, dt, ga, gb, z):
        sb, sz = _sigmoid(gb), _silu(z)
        dg = jnp.where(j < n_half, dt * sb * sz, dt * ga * sb * (1.0 - sb) * sz)
        return dg, dg

    glu_ops = [(d_ts, 0, ssm_w), (glu, 0, ssm_w), (glu, ssm_w, ssm_w), (proj, o_z, ssm_w)]
    dglu, g_bglu = _ew(glu_bwd, name="glu_bwd", rows=seq, width=2 * ssm_w, tiles=glu_ops,
                       outs=[(BF16, 2 * ssm_w, 0)], accs=1, cw=cw, with_col=True)
    (dproj,) = _ew(lambda dt, ga, gb, z: dt * ga * _sigmoid(gb) * _dsilu(z), name="glu_bwd_z", rows=seq,
                   width=ssm_w, tiles=glu_ops, outs=[(BF16, in_w, o_z)], into=[dproj], cw=cw)
    g_w_glu_t = _matmul(dglu, yg, mode="tn", name="grad_w_glu", tm=512, tk=4096)
    d_yg = _matmul(dglu, w_glu_t, mode="nn", name="d_gelu", out_dtype=BF16)
    ((du, db_re, db_im, dc_re, dc_im, gabr, gabi, gcfr, gcfi, g_d), (chips_ap, chips_sp, sib_glu)) = _ssm_bwd(
        proj, o_u, y_ssm, d_yg, st_re, st_im, bc_rows, rows_p, d_row, chunk=chunk, name="ssm_bwd",
        rider=_join(_chip_exchange([pair_ap, pair_sp]), _sibling_exchange([g_w_glu_t])))
    pair_glu = _pair_sum(g_w_glu_t, sib_glu, "pair_sum_w_glu")
    (dproj,) = _ew(lambda v: v, name="du_store", rows=seq, width=ssm_w, tiles=[(du, 0)],
                   outs=[(BF16, in_w, o_u)], into=[dproj], cw=cw)
    g_a_re, g_a_im, g_log_dt = _ssm_param_bwd(
        p["A_re"], p["A_im"], log_dt_col, *[g.reshape(SUBLANES, n_groups, STATE) for g in (gabr, gabi, gcfr, gcfi)],
        "ssm_param_bwd")
    small_grads = dict(
        loss=loss_local, q_norm_w=g_qw.reshape(n_q, HEAD_DIM).sum(0), k_norm_w=g_kw.reshape(N_KV_HEADS, HEAD_DIM).sum(0),
        sinks=g_sinks[0, :n_q], A_re=g_a_re, A_im=g_a_im, log_dt=g_log_dt.reshape(n_groups),
        B_re=_from_ssm_rows(db_re, True), B_im=_from_ssm_rows(db_im, True),
        C_re=_from_ssm_rows(dc_re, False), C_im=_from_ssm_rows(dc_im, False),
        D_skip=g_d.reshape(n_groups, GROUP), b_glu=g_bglu.reshape(2 * ssm_w))

    n_parts = W_IN_GRAD_PARTS
    wq = d // n_parts
    g_parts, pair_parts, chip_parts = [], [], []
    extra = [_chip_exchange([pair_glu]), _all_gather([_pack(small_grads, SMALL_REST)])]
    chips_glu = small_parts = dh = grad_x = g_norm = None
    for step in range(n_parts + 2):
        riders = list(extra) if step == 0 else []
        if 0 <= step - 2 < n_parts:
            riders.append(_chip_exchange([pair_parts[step - 2]]))
        if 0 <= step - 1 < n_parts:
            riders.append(_sibling_exchange([g_parts[step - 1]]))
        rider = _join(*riders) if riders else None
        if step < n_parts:
            res = _matmul(dproj, Cols(h, step * wq, wq), mode="tn", name="grad_w_in_%d" % step, tk=4096, rider=rider)
            out, landed = res if rider is not None else (res, [])
            g_parts.append(out)
        elif step == n_parts:
            dh, landed = _matmul(dproj, w_in_t, mode="nn", name="d_normed", tk=2176, rider=rider)
        else:
            (grad_x, g_norm), landed = _rmsnorm_bwd(xs, norm_row, dh, dout, "rmsnorm_bwd", rider=rider)
        landed = list(landed)
        if step == 0:
            chips_glu, small_parts = landed[:2]
            landed = landed[2:]
        if 0 <= step - 2 < n_parts:
            chip_parts.append(landed.pop(0))
        if 0 <= step - 1 < n_parts:
            pair_parts.append(_pair_sum(g_parts[step - 1], landed.pop(0), "pair_sum_w_in_%d" % (step - 1)))
    (norm_parts,) = _exchange(_all_gather([_pack(dict(norm_w=g_norm), ("norm_w",))]), "gather_norm_grad")
    g_in = jnp.concatenate([_chip_sum(c, "chip_sum_w_in_%d" % q) for q, c in enumerate(chip_parts)], axis=1)
    summed = [g_in] + [_chip_sum(c, "chip_sum_" + k)
                       for k, c in zip(LARGE[1:], (chips_ap, chips_glu, chips_sp, chips_o))]
    return grad_x, summed, small_parts, norm_parts


def kernel(x, norm_w, w_in, q_norm_w, k_norm_w, sinks, w_attn_proj, A_re, A_im, log_dt, B_re, B_im, C_re, C_im, D_skip, w_glu, b_glu, w_ssm_proj, w_out, loss_target, m_norm_w, m_w_in, m_q_norm_w, m_k_norm_w, m_sinks, m_w_attn_proj, m_A_re, m_A_im, m_log_dt, m_B_re, m_B_im, m_C_re, m_C_im, m_D_skip, m_w_glu, m_b_glu, m_w_ssm_proj, m_w_out, v_norm_w, v_w_in, v_q_norm_w, v_k_norm_w, v_sinks, v_w_attn_proj, v_A_re, v_A_im, v_log_dt, v_B_re, v_B_im, v_C_re, v_C_im, v_D_skip, v_w_glu, v_b_glu, v_w_ssm_proj, v_w_out):
    weights = dict(norm_w=norm_w, w_in=w_in, q_norm_w=q_norm_w, k_norm_w=k_norm_w, sinks=sinks,
                   w_attn_proj=w_attn_proj, A_re=A_re, A_im=A_im, log_dt=log_dt, B_re=B_re, B_im=B_im, C_re=C_re,
                   C_im=C_im, D_skip=D_skip, w_glu=w_glu, b_glu=b_glu, w_ssm_proj=w_ssm_proj, w_out=w_out)
    m_in = dict(norm_w=m_norm_w, w_in=m_w_in, q_norm_w=m_q_norm_w, k_norm_w=m_k_norm_w, sinks=m_sinks,
                w_attn_proj=m_w_attn_proj, A_re=m_A_re, A_im=m_A_im, log_dt=m_log_dt, B_re=m_B_re, B_im=m_B_im,
                C_re=m_C_re, C_im=m_C_im, D_skip=m_D_skip, w_glu=m_w_glu, b_glu=m_b_glu, w_ssm_proj=m_w_ssm_proj,
                w_out=m_w_out)
    v_in = dict(norm_w=v_norm_w, w_in=v_w_in, q_norm_w=v_q_norm_w, k_norm_w=v_k_norm_w, sinks=v_sinks,
                w_attn_proj=v_w_attn_proj, A_re=v_A_re, A_im=v_A_im, log_dt=v_log_dt, B_re=v_B_re, B_im=v_B_im,
                C_re=v_C_re, C_im=v_C_im, D_skip=v_D_skip, w_glu=v_w_glu, b_glu=v_b_glu, w_ssm_proj=v_w_ssm_proj,
                w_out=v_w_out)

    _, seq, d = x.shape
    column_sharded = LARGE[:4]
    as_rows = lambda k, a: a.T if k in column_sharded else a
    shards = [as_rows(k, weights[k]).astype(BF16) for k in LARGE]
    small = {k: weights[k] for k in SMALL}
    grad_x, summed, small_parts, norm_parts = _step(x.reshape(seq, d), loss_target.reshape(seq, d), small, shards)

    grads, delta, new_m, new_v = {}, {}, {}, {}
    for k, g in zip(LARGE, summed):
        if k == "w_in":
            upd = _adamw(weights[k].T, g, m_in[k].T, v_in[k].T, "adamw_" + k)
            grads[k], delta[k], new_m[k], new_v[k] = [a.T for a in (g, *upd)]
        else:
            grads[k] = as_rows(k, g)
            delta[k], new_m[k], new_v[k] = _adamw(weights[k], grads[k], m_in[k], v_in[k], "adamw_" + k)

    zero = jnp.zeros((), F32)
    for keys, parts in ((SMALL_REST, small_parts), (("norm_w",), norm_parts)):
        like = dict(small, loss=zero)
        packs = [_pack(dict(src, loss=zero), keys) for src in (weights, m_in, v_in)]
        res = _adamw_small(packs[0], parts, packs[1], packs[2], "adamw_small_%d" % len(keys))
        for dst, r in zip((grads, delta, new_m, new_v), res):
            dst.update(_unpack(r, like, keys))
    loss = grads["loss"]

    return (loss, grad_x.reshape(x.shape), *[grads[k] for k in ORDER], *[delta[k] for k in ORDER],
            *[new_m[k] for k in ORDER], *[new_v[k] for k in ORDER])
```

```python
import math
from typing import Callable, NamedTuple

import jax
import jax.numpy as jnp
import numpy as np
from jax import lax
from jax.experimental import pallas as pl
from jax.experimental.pallas import tpu as pltpu

F32 = jnp.float32
BF16 = jnp.bfloat16
MESH = pl.DeviceIdType.MESH

HEAD_DIM = 64
N_KV_HEADS = 4
GROUP = 16
STATE = 64
BLOCK = 128
NORM_EPS = 1e-6
N_DEV = 8
N_CHIPS = 4
LANES = 128
SUBLANES = 8
MXU_DIM = 256
VMEM_BYTES = 64 * 1024 * 1024
VMEM_CAP = VMEM_BYTES - 8 * 1024 * 1024

ADAM_LR = 0.001
ADAM_B1 = 0.9
ADAM_B2 = 0.999
ADAM_EPS = 1e-08
ADAM_WD = 0.01
ADAM_STEP = 10

GELU_C = math.sqrt(2.0 / math.pi)
GELU_K = 0.044715


def _tile(dim, pref, mult=LANES):
    if dim <= pref:
        return dim
    best = None
    for d in range(mult, pref + 1, mult):
        if dim % d == 0:
            best = d
    assert best is not None, (dim, pref, mult)
    return best


def _params(semantics=None, vmem=None):
    kw = {}
    if semantics is not None:
        kw["dimension_semantics"] = semantics
    if vmem is not None:
        kw["vmem_limit_bytes"] = int(min(VMEM_CAP, max(vmem, 32 * 1024 * 1024)))
    return pltpu.CompilerParams(**kw)


def _nbytes(shape, dtype):
    return math.prod(shape) * jnp.dtype(dtype).itemsize


def _sigmoid(x):
    return 1.0 / (1.0 + jnp.exp(-x))


def _silu(x):
    return x * _sigmoid(x)


def _dsilu(x):
    s = _sigmoid(x)
    return s * (1.0 + x * (1.0 - s))


def _gelu(x):
    return 0.5 * x * (1.0 + jnp.tanh(GELU_C * (x + GELU_K * x * x * x)))


def _dgelu(x):
    t = jnp.tanh(GELU_C * (x + GELU_K * x * x * x))
    return 0.5 * (1.0 + t) + 0.5 * x * (1.0 - t * t) * GELU_C * (1.0 + 3.0 * GELU_K * x * x)


def _dot(a, b, dims):
    return lax.dot_general(a, b, (dims, ((), ())), preferred_element_type=F32)


NN = ((1,), (0,))
NT = ((1,), (1,))
TN = ((0,), (0,))


def _any_spec():
    return pl.BlockSpec(memory_space=pl.ANY)


def _pallas(body, **kw):
    pin = lambda s: pltpu.HBM(s.shape, s.dtype) if isinstance(s, jax.ShapeDtypeStruct) else s
    out_shape = kw.pop("out_shape")
    out_shape = [pin(s) for s in out_shape] if isinstance(out_shape, (list, tuple)) else pin(out_shape)
    call = pl.pallas_call(body, out_shape=out_shape, **kw)

    def run(*operands):
        pinned = [pltpu.with_memory_space_constraint(o, pltpu.HBM) if jnp.issubdtype(o.dtype, jnp.floating) else o
                  for o in operands]
        return call(*pinned)

    return run


class Rider(NamedTuple):
    operands: tuple
    out_shapes: tuple
    sems: tuple
    start: Callable
    finish: Callable


def _all_gather(shards):
    n = len(shards)

    def copies(ins, outs, sems):
        send_sems, recv_sems, local_sems = sems
        x, y, c = lax.axis_index("x"), lax.axis_index("y"), lax.axis_index("c")
        me, sibling = (x, y, c), (x, y, 1 - c)
        chips = [(1 - x, y), (x, 1 - y), (1 - x, 1 - y)]

        def rows(k, px, py, pc):
            r = shards[k].shape[0]
            return outs[k].at[pl.ds((4 * px + 2 * py + pc) * r, r), :]

        def copy(k, s, block, to, src=None):
            return pltpu.make_async_remote_copy(
                src_ref=rows(k, *block) if src is None else src, dst_ref=rows(k, *block),
                send_sem=send_sems.at[7 * k + s], recv_sem=recv_sems.at[7 * k + s],
                device_id=to, device_id_type=MESH)

        mine = [pltpu.make_async_copy(ins[k], rows(k, *me), local_sems.at[k]) for k in range(n)]
        first = []
        for k in range(n):
            first.append(copy(k, 0, me, sibling, src=ins[k]))
            first += [copy(k, 1 + j, me, (*chip, c), src=ins[k]) for j, chip in enumerate(chips)]
        return me, sibling, chips, c, copy, mine, first

    def start(ins, outs, sems):
        *_, mine, first = copies(ins, outs, sems)
        for cp in mine + first:
            cp.start()

    def finish(ins, outs, sems):
        me, sibling, chips, c, copy, mine, first = copies(ins, outs, sems)
        passed = []
        for j, chip in enumerate(chips):
            for k in range(n):
                copy(k, 1 + j, (*chip, c), me).wait_recv()
                fwd = copy(k, 4 + j, (*chip, c), sibling)
                fwd.start()
                passed.append(fwd)
        for k in range(n):
            copy(k, 0, sibling, me).wait_recv()
            for j, chip in enumerate(chips):
                copy(k, 4 + j, (*chip, 1 - c), me).wait_recv()
        for cp in first + passed:
            cp.wait_send()
        for cp in mine:
            cp.wait()

    return Rider(
        tuple(shards),
        tuple(jax.ShapeDtypeStruct((N_DEV * s.shape[0], s.shape[1]), s.dtype) for s in shards),
        (pltpu.SemaphoreType.DMA((7 * n,)), pltpu.SemaphoreType.DMA((7 * n,)), pltpu.SemaphoreType.DMA((n,))),
        start, finish)


def _sibling_exchange(grads):
    n = len(grads)

    def copies(ins, outs, sems):
        send_sems, recv_sems = sems
        x, y, c = lax.axis_index("x"), lax.axis_index("y"), lax.axis_index("c")
        out = []
        for k in range(n):
            r = grads[k].shape[0] // N_DEV
            for j in range(N_CHIPS):
                out.append(pltpu.make_async_remote_copy(
                    src_ref=ins[k].at[pl.ds((2 * j + 1 - c) * r, r), :],
                    dst_ref=outs[k].at[pl.ds(j * r, r), :],
                    send_sem=send_sems.at[N_CHIPS * k + j], recv_sem=recv_sems.at[N_CHIPS * k + j],
                    device_id=(x, y, 1 - c), device_id_type=MESH))
        return out

    def start(ins, outs, sems):
        for cp in copies(ins, outs, sems):
            cp.start()

    def finish(ins, outs, sems):
        for cp in copies(ins, outs, sems):
            cp.wait()

    return Rider(
        tuple(grads), tuple(jax.ShapeDtypeStruct((g.shape[0] // 2, g.shape[1]), g.dtype) for g in grads),
        (pltpu.SemaphoreType.DMA((N_CHIPS * n,)), pltpu.SemaphoreType.DMA((N_CHIPS * n,))), start, finish)


def _chip_exchange(parts):
    n = len(parts)

    def copies(ins, outs, sems):
        send_sems, recv_sems, local_sems = sems
        x, y, c = lax.axis_index("x"), lax.axis_index("y"), lax.axis_index("c")
        my_chip = 2 * x + y
        chips = [(1 - x, y), (x, 1 - y), (1 - x, 1 - y)]
        local, sent = [], []
        for k in range(n):
            r = parts[k].shape[0] // N_CHIPS
            mine = pl.ds(my_chip * r, r)
            local.append(pltpu.make_async_copy(ins[k].at[mine, :], outs[k].at[mine, :], local_sems.at[k]))
            for s, (px, py) in enumerate(chips):
                sent.append(pltpu.make_async_remote_copy(
                    src_ref=ins[k].at[pl.ds((2 * px + py) * r, r), :], dst_ref=outs[k].at[mine, :],
                    send_sem=send_sems.at[3 * k + s], recv_sem=recv_sems.at[3 * k + s],
                    device_id=(px, py, c), device_id_type=MESH))
        return local, sent

    def start(ins, outs, sems):
        local, sent = copies(ins, outs, sems)
        for cp in local + sent:
            cp.start()

    def finish(ins, outs, sems):
        local, sent = copies(ins, outs, sems)
        for cp in sent + local:
            cp.wait()

    return Rider(
        tuple(parts), tuple(jax.ShapeDtypeStruct(p.shape, p.dtype) for p in parts),
        (pltpu.SemaphoreType.DMA((3 * n,)), pltpu.SemaphoreType.DMA((3 * n,)), pltpu.SemaphoreType.DMA((n,))),
        start, finish)


def _join(*riders):
    cuts_in, cuts_out, cuts_sem = [0], [0], [0]
    for r in riders:
        cuts_in.append(cuts_in[-1] + len(r.operands))
        cuts_out.append(cuts_out[-1] + len(r.out_shapes))
        cuts_sem.append(cuts_sem[-1] + len(r.sems))

    def each(which):
        def run(ins, outs, sems):
            for i, r in enumerate(riders):
                getattr(r, which)(ins[cuts_in[i]:cuts_in[i + 1]], outs[cuts_out[i]:cuts_out[i + 1]],
                                  sems[cuts_sem[i]:cuts_sem[i + 1]])
        return run

    return Rider(sum((r.operands for r in riders), ()), sum((r.out_shapes for r in riders), ()),
                 sum((r.sems for r in riders), ()), each("start"), each("finish"))


def _call(body, operands, *, name, out_shape, grid, in_specs, out_specs, scratch_shapes=(), aliases=None,
          semantics=None, vmem=None, rider=None):
    operands, out_shape, scratch_shapes = list(operands), list(out_shape), list(scratch_shapes)
    in_specs, out_specs = list(in_specs), list(out_specs)
    if rider is None:
        res = _pallas(
            body, name=name, out_shape=out_shape, grid=grid, in_specs=in_specs, out_specs=out_specs,
            scratch_shapes=scratch_shapes, input_output_aliases=aliases or {},
            compiler_params=_params(semantics, vmem))(*operands)
        return list(res), []
    n_in, n_out, n_scr = len(operands), len(out_shape), len(scratch_shapes)
    ri, ro = len(rider.operands), len(rider.out_shapes)

    def carried(*refs):
        a, b = n_in, n_in + ri
        c, d = b + n_out, b + n_out + ro
        e = d + n_scr
        ids = [pl.program_id(k) for k in range(len(grid))]
        first = ids[0] == 0
        last = ids[0] == grid[0] - 1
        for k in range(1, len(grid)):
            first = jnp.logical_and(first, ids[k] == 0)
            last = jnp.logical_and(last, ids[k] == grid[k] - 1)

        @pl.when(first)
        def _():
            rider.start(refs[a:b], refs[c:d], refs[e:])

        body(*refs[:a], *refs[b:c], *refs[d:e])

        @pl.when(last)
        def _():
            rider.finish(refs[a:b], refs[c:d], refs[e:])

    res = _pallas(
        carried, name=name, out_shape=out_shape + list(rider.out_shapes), grid=grid,
        in_specs=in_specs + [_any_spec()] * ri, out_specs=out_specs + [_any_spec()] * ro,
        scratch_shapes=scratch_shapes + list(rider.sems), input_output_aliases=aliases or {},
        compiler_params=_params(("arbitrary",) * len(grid), vmem))(*operands, *rider.operands)
    return list(res[:n_out]), list(res[n_out:])


def _exchange(rider, name):
    ri, ro = len(rider.operands), len(rider.out_shapes)

    def body(*refs):
        rider.start(refs[:ri], refs[ri:ri + ro], refs[ri + ro:])
        rider.finish(refs[:ri], refs[ri:ri + ro], refs[ri + ro:])

    return _pallas(
        body, name=name, out_shape=list(rider.out_shapes), in_specs=[_any_spec()] * ri,
        out_specs=[_any_spec()] * ro, scratch_shapes=list(rider.sems))(*rider.operands)


class Cols(NamedTuple):
    arr: jax.Array
    off: int
    width: int


def _cols(a):
    return a if isinstance(a, Cols) else Cols(a, 0, a.shape[1])


def _matmul(a, b, *, mode, name, out_dtype=F32, tm=1024, tn=1024, tk=2048, bias=None, out_cols=None, into=None,
            rider=None):
    a, b = _cols(a), _cols(b)
    if mode == "nn":
        (m, k), (k2, n) = (a.arr.shape[0], a.width), (b.arr.shape[0], b.width)
    elif mode == "nt":
        (m, k), (n, k2) = (a.arr.shape[0], a.width), (b.arr.shape[0], b.width)
    else:
        (k, m), (k2, n) = (a.arr.shape[0], a.width), (b.arr.shape[0], b.width)
    assert k == k2, (a.arr.shape, b.arr.shape, mode)
    tm, tn, tk = _tile(m, tm), _tile(n, tn), _tile(k, tk)
    nk = k // tk
    dims = {"nn": NN, "nt": NT, "tn": TN}[mode]
    if mode == "tn":
        assert a.off % tm == 0
        a_spec = pl.BlockSpec((tk, tm), lambda i, j, kk, o=a.off // tm: (kk, i + o))
    else:
        assert a.off % tk == 0
        a_spec = pl.BlockSpec((tm, tk), lambda i, j, kk, o=a.off // tk: (i, kk + o))
    if mode == "nt":
        assert b.off % tk == 0
        b_spec = pl.BlockSpec((tn, tk), lambda i, j, kk, o=b.off // tk: (j, kk + o))
    else:
        assert b.off % tn == 0
        b_spec = pl.BlockSpec((tk, tn), lambda i, j, kk, o=b.off // tn: (kk, j + o))
    in_specs, operands = [a_spec, b_spec], [a.arr, b.arr]
    if bias is not None:
        in_specs.append(pl.BlockSpec((1, tn), lambda i, j, kk: (0, j)))
        operands.append(bias)
    total_w, o_off = out_cols if out_cols is not None else (n, 0)
    assert o_off % tn == 0
    aliases = {}
    if into is not None:
        assert into.shape == (m, total_w) and into.dtype == out_dtype
        in_specs.append(_any_spec())
        operands.append(into)
        aliases = {len(operands) - 1: 0}
    n_in = len(operands)

    def body(*refs):
        a_ref, b_ref = refs[0], refs[1]
        bias_ref = refs[2] if bias is not None else None
        o_ref = refs[n_in]
        acc_ref = refs[-1] if nk > 1 else None
        part = _dot(a_ref[...].astype(BF16), b_ref[...].astype(BF16), dims)

        def finish(acc):
            if bias_ref is not None:
                acc = acc + bias_ref[...]
            o_ref[...] = acc.astype(out_dtype)

        if nk == 1:
            finish(part)
        else:
            kk = pl.program_id(2)

            @pl.when(kk == 0)
            def _():
                acc_ref[...] = part

            @pl.when(kk > 0)
            def _():
                acc_ref[...] += part

            @pl.when(kk == nk - 1)
            def _():
                finish(acc_ref[...])

    vmem = 2 * (_nbytes((tm, tk), a.arr.dtype) + _nbytes((tk, tn), b.arr.dtype) + _nbytes((tm, tn), out_dtype))
    vmem += 3 * _nbytes((tm, tn), F32)
    (out,), landed = _call(
        body, operands, name=name, out_shape=[jax.ShapeDtypeStruct((m, total_w), out_dtype)],
        grid=(m // tm, n // tn, nk), in_specs=in_specs,
        out_specs=[pl.BlockSpec((tm, tn), lambda i, j, kk, o=o_off // tn: (i, j + o))],
        scratch_shapes=[pltpu.VMEM((tm, tn), F32)] if nk > 1 else [], aliases=aliases,
        semantics=("parallel", "parallel", "arbitrary"), vmem=vmem, rider=rider)
    return out if rider is None else (out, landed)


IN_PROJ_TILE = 256
GATHER_SLOTS = 8


def _gather_order(n_tiles, tile, shard):
    priority = [0, 1, 2, 5, 3, 6, 4, 7]
    rank = {s: k for k, s in enumerate(priority)}
    order = np.zeros((N_DEV, n_tiles), np.int32)
    flags = np.zeros((N_DEV, n_tiles, GATHER_SLOTS), np.int32)
    for dev in range(N_DEV):
        x, y, c = dev >> 2, (dev >> 1) & 1, dev & 1
        blocks = [(x, y, c), (x, y, 1 - c), (1 - x, y, c), (x, 1 - y, c), (1 - x, 1 - y, c),
                  (1 - x, y, 1 - c), (x, 1 - y, 1 - c), (1 - x, 1 - y, 1 - c)]
        slot_of = {4 * bx + 2 * by + bc: s for s, (bx, by, bc) in enumerate(blocks)}

        def needs(t):
            return sorted({slot_of[(t * tile) // shard], slot_of[(t * tile + tile - 1) // shard]})

        seen = set()
        for step, t in enumerate(sorted(range(n_tiles), key=lambda t: (max(rank[s] for s in needs(t)), t))):
            order[dev, step] = t
            for s in needs(t):
                if s not in seen:
                    flags[dev, step, s] = 1
                    seen.add(s)
        assert len(seen) == GATHER_SLOTS
    return order, flags


def _gather_in_proj(h, shard, *, name, rider):
    rows, d = h.shape
    r = shard.shape[0]
    in_w = N_DEV * r
    tn = IN_PROJ_TILE
    n_tiles = in_w // tn
    tm = _tile(rows, 1024, SUBLANES)
    order, flags = _gather_order(n_tiles, tn, r)
    dev = 4 * lax.axis_index("x") + 2 * lax.axis_index("y") + lax.axis_index("c")
    my_order = jnp.asarray(order)[dev]
    my_flags = jnp.asarray(flags.reshape(N_DEV, -1))[dev]

    def body(order_ref, flags_ref, h_ref, shard_hbm, proj_hbm, w_hbm, b_buf, o_buf, send_sems, recv_sems, local_sem,
             b_sem, o_sems):
        s = pl.program_id(0)
        x, y, c = lax.axis_index("x"), lax.axis_index("y"), lax.axis_index("c")
        me, sibling = (x, y, c), (x, y, 1 - c)
        chips = [(1 - x, y), (x, 1 - y), (1 - x, 1 - y)]

        def block(px, py, pc):
            return w_hbm.at[pl.ds((4 * px + 2 * py + pc) * r, r), :]

        def copy(k, blk, to, src=None):
            return pltpu.make_async_remote_copy(
                src_ref=block(*blk) if src is None else src, dst_ref=block(*blk),
                send_sem=send_sems.at[k], recv_sem=recv_sems.at[k], device_id=to, device_id_type=MESH)

        mine = pltpu.make_async_copy(shard_hbm, block(*me), local_sem)
        first = [copy(0, me, sibling, src=shard_hbm)]
        first += [copy(1 + j, me, (*chip, c), src=shard_hbm) for j, chip in enumerate(chips)]
        passed = [copy(4 + j, (*chip, c), sibling) for j, chip in enumerate(chips)]

        @pl.when(s == 0)
        def _():
            for cp in [mine] + first[:3]:
                cp.start()

        def due(slot):
            return flags_ref[s * GATHER_SLOTS + slot] == 1

        @pl.when(due(0))
        def _():
            mine.wait()

        @pl.when(due(1))
        def _():
            copy(0, sibling, me).wait_recv()

        for j, chip in enumerate(chips):
            @pl.when(due(2 + j))
            def _(j=j, chip=chip):
                copy(1 + j, (*chip, c), me).wait_recv()
                passed[j].start()
                if j == 0:
                    first[1].wait_send()
                    first[2].wait_send()
                    first[3].start()

            @pl.when(due(5 + j))
            def _(j=j, chip=chip):
                copy(4 + j, (*chip, 1 - c), me).wait_recv()

        col = pl.multiple_of(order_ref[s] * tn, tn)
        fetch = pltpu.make_async_copy(w_hbm.at[pl.ds(col, tn), :], b_buf, b_sem)
        fetch.start()
        slot = s % 2

        def put(at):
            return pltpu.make_async_copy(o_buf.at[slot], proj_hbm.at[:, pl.ds(at, tn)], o_sems.at[slot])

        @pl.when(s >= 2)
        def _():
            put(0).wait()

        fetch.wait()
        for m in range(rows // tm):
            rs = slice(m * tm, (m + 1) * tm)
            o_buf[slot, rs, :] = _dot(h_ref[rs, :], b_buf[...], NT).astype(BF16)
        put(col).start()

        @pl.when(s == n_tiles - 1)
        def _():
            pltpu.make_async_copy(o_buf.at[1 - slot], proj_hbm.at[:, pl.ds(0, tn)], o_sems.at[1 - slot]).wait()
            put(0).wait()
            for cp in [first[0], first[3]] + passed:
                cp.wait_send()

    smem = pl.BlockSpec(memory_space=pltpu.SMEM)
    vmem = 2 * _nbytes((rows, d), BF16) + 3 * _nbytes((rows, tn), BF16) + _nbytes((tn, d), BF16) + 4 * _nbytes((tm, tn), F32)
    (proj, w_full), landed = _call(
        body, [my_order, my_flags, h, shard], name=name,
        out_shape=[jax.ShapeDtypeStruct((rows, in_w), BF16), jax.ShapeDtypeStruct((in_w, d), BF16)],
        grid=(n_tiles,), in_specs=[smem, smem, pl.BlockSpec((rows, d), lambda s: (0, 0)), _any_spec()],
        out_specs=[_any_spec(), _any_spec()],
        scratch_shapes=[pltpu.VMEM((tn, d), BF16), pltpu.VMEM((2, rows, tn), BF16),
                        pltpu.SemaphoreType.DMA((7,)), pltpu.SemaphoreType.DMA((7,)), pltpu.SemaphoreType.DMA,
                        pltpu.SemaphoreType.DMA, pltpu.SemaphoreType.DMA((2,))],
        semantics=("arbitrary",), vmem=vmem, rider=rider)
    return proj, w_full, landed


def _ew(fn, *, name, rows, width, tiles, vecs=(), outs, accs=0, tl=1024, cw=512, into=None, with_col=False):
    tl, cw = _tile(rows, tl, SUBLANES), _tile(width, cw)
    ncol = width // cw
    nt_, nv = len(tiles), len(vecs)
    into = list(into) if into is not None else [None] * len(outs)
    aliased = [t for t in into if t is not None]

    def off(o):
        assert o % cw == 0, (name, o, cw)
        return o // cw

    in_specs, vmem = [], 0
    for t in tiles:
        arr, o = t[0], off(t[1])
        wrap = t[2] // cw if len(t) > 2 else ncol
        in_specs.append(pl.BlockSpec((tl, cw), lambda j, i, o=o, wrap=wrap: (i, o + j % wrap)))
        vmem += _nbytes((tl, cw), arr.dtype)
    in_specs += [pl.BlockSpec((1, cw), lambda j, i, o=off(o): (0, j + o)) for _, o in vecs]
    in_specs += [_any_spec() for _ in aliased]
    out_shape, out_specs, aliases = [], [], {}
    n_in = nt_ + nv
    for idx, ((dt, tw, o), tgt) in enumerate(zip(outs, into)):
        out_shape.append(jax.ShapeDtypeStruct((rows, tw), dt))
        out_specs.append(pl.BlockSpec((tl, cw), lambda j, i, o=off(o): (i, j + o)))
        vmem += _nbytes((tl, cw), dt)
        if tgt is not None:
            assert tgt.shape == (rows, tw) and tgt.dtype == dt, (name, tgt.shape, tgt.dtype)
            aliases[n_in + len(aliases)] = idx
    for _ in range(accs):
        out_shape.append(jax.ShapeDtypeStruct((1, width), F32))
        out_specs.append(pl.BlockSpec((1, cw), lambda j, i: (0, j)))
    n_out = len(outs)

    def body(*refs):
        vals = [r[...].astype(F32) for r in refs[:n_in]]
        out_refs = refs[n_in + len(aliased):]
        res = fn(pl.program_id(0), *vals) if with_col else fn(*vals)
        res = res if isinstance(res, (tuple, list)) else (res,)
        assert len(res) == n_out + accs, (name, len(res))
        for r, v in zip(out_refs[:n_out], res[:n_out]):
            r[...] = v.astype(r.dtype)
        first = pl.program_id(1) == 0
        for r, v in zip(out_refs[n_out:], res[n_out:]):
            s = jnp.sum(v, axis=0, keepdims=True)

            @pl.when(first)
            def _(r=r, s=s):
                r[...] = s

            @pl.when(jnp.logical_not(first))
            def _(r=r, s=s):
                r[...] += s

    return _pallas(
        body, name=name, out_shape=out_shape, grid=(ncol, rows // tl),
        in_specs=in_specs, out_specs=out_specs, input_output_aliases=aliases,
        compiler_params=_params(("parallel", "arbitrary"), 3 * vmem),
    )(*[t[0] for t in tiles], *[v for v, _ in vecs], *aliased)


def _rmsnorm_fwd(x, w_row, name):
    rows, d = x.shape
    tl = _tile(rows, 512, SUBLANES)

    def body(x_ref, w_ref, h_ref):
        xv = x_ref[...]
        rstd = lax.rsqrt(jnp.mean(xv * xv, axis=-1, keepdims=True) + NORM_EPS)
        h_ref[...] = (xv * rstd * w_ref[...]).astype(BF16)

    return _pallas(
        body, name=name, out_shape=jax.ShapeDtypeStruct((rows, d), BF16), grid=(rows // tl,),
        in_specs=[pl.BlockSpec((tl, d), lambda i: (i, 0)), pl.BlockSpec((1, d), lambda i: (0, 0))],
        out_specs=pl.BlockSpec((tl, d), lambda i: (i, 0)),
        compiler_params=_params(("parallel",)),
    )(x, w_row)


def _rmsnorm_bwd(x, w_row, dh, dout, name, rider=None):
    rows, d = x.shape
    tl = _tile(rows, 256, SUBLANES)

    def body(x_ref, w_ref, dh_ref, dout_ref, gx_ref, gw_ref):
        xv = x_ref[...]
        rstd = lax.rsqrt(jnp.mean(xv * xv, axis=-1, keepdims=True) + NORM_EPS)
        xn = xv * rstd
        dhv = dh_ref[...]
        dxn = dhv * w_ref[...]
        dx = rstd * (dxn - xn * jnp.mean(dxn * xn, axis=-1, keepdims=True))
        gx_ref[...] = dout_ref[...] + dx
        gw = jnp.sum(dhv * xn, axis=0, keepdims=True)

        @pl.when(pl.program_id(0) == 0)
        def _():
            gw_ref[...] = gw

        @pl.when(pl.program_id(0) > 0)
        def _():
            gw_ref[...] += gw

    tile = pl.BlockSpec((tl, d), lambda i: (i, 0))
    row = pl.BlockSpec((1, d), lambda i: (0, 0))
    res, landed = _call(
        body, [x, w_row, dh, dout], name=name,
        out_shape=[jax.ShapeDtypeStruct((rows, d), F32), jax.ShapeDtypeStruct((1, d), F32)],
        grid=(rows // tl,), in_specs=[tile, row, tile, tile], out_specs=[tile, row],
        semantics=("arbitrary",), rider=rider)
    return res if rider is None else (res, landed)


def _head_mean(x, gmat):
    hi = x.astype(BF16)
    lo = (x - hi.astype(F32)).astype(BF16)
    out = []
    for s in range(x.shape[1] // MXU_DIM):
        sl = slice(s * MXU_DIM, (s + 1) * MXU_DIM)
        out.append(_dot(hi[:, sl], gmat, NN) + _dot(lo[:, sl], gmat, NN))
    return out[0] if len(out) == 1 else jnp.concatenate(out, axis=1)


def _head_mean_matrix():
    blk = jnp.arange(MXU_DIM) // HEAD_DIM
    return jnp.where(blk[:, None] == blk[None, :], 1.0 / HEAD_DIM, 0.0).astype(BF16)


def _spread_head(x, g, width):
    col = x[:, (g // 2) * LANES:(g // 2 + 1) * LANES]
    other = pltpu.roll(col, HEAD_DIM, axis=1)
    low = lax.broadcasted_iota(jnp.int32, col.shape, 1) < HEAD_DIM
    both = jnp.where(low, col, other) if g % 2 == 0 else jnp.where(low, other, col)
    return both if width == LANES else jnp.concatenate([both] * (width // LANES), axis=1)


def _head_diagonal(t, per_kv):
    head = lax.broadcasted_iota(jnp.int32, t.shape, 1) // HEAD_DIM
    zero = jnp.zeros_like(t)
    return jnp.concatenate([jnp.where(head == r, t, zero) for r in range(per_kv)], axis=0)


def _fold_heads(x, per_kv):
    rows = x.shape[0] // per_kv
    head = lax.broadcasted_iota(jnp.int32, (rows, x.shape[1]), 1) // HEAD_DIM
    acc = jnp.where(head == 0, x[0:rows], 0.0)
    for r in range(1, per_kv):
        acc = acc + jnp.where(head == r, x[r * rows:(r + 1) * rows], 0.0)
    while acc.shape[1] > LANES:
        half = acc.shape[1] // 2
        acc = acc[:, :half] + acc[:, half:]
    return acc + pltpu.roll(acc, HEAD_DIM, axis=1)


def _join_heads(parts):
    low = lax.broadcasted_iota(jnp.int32, parts[0].shape, 1) < HEAD_DIM
    cols = [jnp.where(low, parts[2 * j], parts[2 * j + 1]) for j in range(len(parts) // 2)]
    return cols[0] if len(cols) == 1 else jnp.concatenate(cols, axis=1)


def _attn_specs(attn_w, kv_w):
    half = attn_w // 2
    kcol, vcol = attn_w // kv_w, attn_w // kv_w + 1
    gcol = (attn_w + 2 * kv_w) // half
    prev = lambda i: jnp.maximum(i - 1, 0)
    return [
        pl.BlockSpec((BLOCK, attn_w), lambda i: (i, 0)),
        pl.BlockSpec((BLOCK, kv_w), lambda i: (prev(i), kcol)),
        pl.BlockSpec((BLOCK, kv_w), lambda i: (i, kcol)),
        pl.BlockSpec((BLOCK, kv_w), lambda i: (prev(i), vcol)),
        pl.BlockSpec((BLOCK, kv_w), lambda i: (i, vcol)),
        pl.BlockSpec((BLOCK, half), lambda i: (i, gcol)),
        pl.BlockSpec((BLOCK, half), lambda i: (i, gcol + 1)),
    ]


def _band_mask(i):
    q_loc = lax.broadcasted_iota(jnp.int32, (BLOCK, 2 * BLOCK), 0) + BLOCK
    k_loc = lax.broadcasted_iota(jnp.int32, (BLOCK, 2 * BLOCK), 1)
    diff = q_loc - k_loc
    first_key = jnp.where(i == 0, BLOCK, 0)
    return (diff >= 0) & (diff < BLOCK) & (k_loc >= first_key)


def _softmax_with_sink(s, sink):
    m = jnp.maximum(jnp.max(s, axis=-1, keepdims=True), sink)
    p = jnp.exp(s - m)
    e_sink = jnp.exp(sink - m)
    den = jnp.sum(p, axis=-1, keepdims=True) + e_sink
    inv = 1.0 / den
    return p * inv, e_sink * inv


def _attn_block(i, q, kk, vv, qw, kw, gmat, sink_ref, per_kv):
    scale = 1.0 / math.sqrt(HEAD_DIM)
    keys = 2 * BLOCK
    valid = _band_mask(i)
    q_rstd = lax.rsqrt(_head_mean(q * q, gmat) + NORM_EPS)
    qn = q * q_rstd
    qh = (qn * qw).astype(BF16)
    k_rstd = lax.rsqrt(_head_mean(kk * kk, gmat) + NORM_EPS)
    kn = kk * k_rstd
    kh = kn * kw
    gw = per_kv * HEAD_DIM
    groups = []
    for g in range(N_KV_HEADS):
        kd = _head_diagonal(_spread_head(kh, g, gw).astype(BF16), per_kv)
        vd = _head_diagonal(_spread_head(vv, g, gw).astype(BF16), per_kv)
        qg = qh[:, g * gw:(g + 1) * gw]
        s_all = _dot(qg, kd, NT) * scale
        ps, p_sinks = [], []
        for r in range(per_kv):
            s = jnp.where(valid, s_all[:, r * keys:(r + 1) * keys], -1e30)
            p, p_sink = _softmax_with_sink(s, sink_ref[g * per_kv + r])
            ps.append(p)
            p_sinks.append(p_sink)
        pb = jnp.concatenate(ps, axis=1).astype(BF16)
        groups.append((kd, vd, qg, ps, p_sinks, pb, _dot(pb, vd, NN)))
    return qn, q_rstd, kn, k_rstd, groups


def _attention_fwd(proj, qw_row, kw_row, gmat, sinks, *, attn_w, kv_w, name):
    rows = proj.shape[0]
    per_kv = attn_w // HEAD_DIM // N_KV_HEADS

    def body(q_ref, kp_ref, kc_ref, vp_ref, vc_ref, glo_ref, ghi_ref, qw_ref, kw_ref, gm_ref, sink_ref, o_ref):
        kk = jnp.concatenate([kp_ref[...], kc_ref[...]], axis=0).astype(F32)
        vv = jnp.concatenate([vp_ref[...], vc_ref[...]], axis=0).astype(F32)
        gate = jnp.concatenate([glo_ref[...], ghi_ref[...]], axis=1).astype(F32)
        *_, groups = _attn_block(pl.program_id(0), q_ref[...].astype(F32), kk, vv, qw_ref[...], kw_ref[...], gm_ref[...],
                                 sink_ref, per_kv)
        attn = jnp.concatenate([grp[-1] for grp in groups], axis=1)
        o_ref[...] = (attn * _silu(gate)).astype(BF16)

    const = lambda a: pl.BlockSpec(a.shape, lambda i: (0, 0))
    return _pallas(
        body, name=name, out_shape=jax.ShapeDtypeStruct((rows, attn_w), BF16), grid=(rows // BLOCK,),
        in_specs=_attn_specs(attn_w, kv_w) + [const(qw_row), const(kw_row), const(gmat),
                                              pl.BlockSpec(memory_space=pltpu.SMEM)],
        out_specs=pl.BlockSpec((BLOCK, attn_w), lambda i: (i, 0)),
        compiler_params=_params(("parallel",), 40 * 1024 * 1024),
    )(proj, proj, proj, proj, proj, proj, proj, qw_row, kw_row, gmat, sinks)


def _attention_bwd(proj, d_ag, dproj, qw_row, kw_row, gmat, sinks, *, attn_w, kv_w, name, rider=None):
    rows = proj.shape[0]
    nb = rows // BLOCK
    per_kv = attn_w // HEAD_DIM // N_KV_HEADS
    gw = per_kv * HEAD_DIM
    keys = 2 * BLOCK
    scale = 1.0 / math.sqrt(HEAD_DIM)
    w_out = 2 * attn_w + 2 * kv_w

    def body(q_ref, kp_ref, kc_ref, vp_ref, vc_ref, glo_ref, ghi_ref, dag_ref, qw_ref, kw_ref, gm_ref, sink_ref, _,
             dp_ref, dkv_ref, gqw_ref, gkw_ref, gs_ref):
        i = pl.program_id(0)
        kk = jnp.concatenate([kp_ref[...], kc_ref[...]], axis=0).astype(F32)
        vv = jnp.concatenate([vp_ref[...], vc_ref[...]], axis=0).astype(F32)
        gate = jnp.concatenate([glo_ref[...], ghi_ref[...]], axis=1).astype(F32)
        d_ag_v = dag_ref[...].astype(F32)
        qw, kw, gmat_v = qw_ref[...], kw_ref[...], gm_ref[...]
        qn, q_rstd, kn, k_rstd, groups = _attn_block(i, q_ref[...].astype(F32), kk, vv, qw, kw, gmat_v, sink_ref,
                                                     per_kv)
        lane = lax.broadcasted_iota(jnp.int32, (SUBLANES, LANES), 1)
        sub = lax.broadcasted_iota(jnp.int32, (SUBLANES, LANES), 0)
        gsink = jnp.zeros((SUBLANES, LANES), F32)
        dq_groups, dgate_groups, dk_heads, dv_heads = [], [], [], []
        for g, (kd, vd, qg, ps, p_sinks, pb, o) in enumerate(groups):
            cs = slice(g * gw, (g + 1) * gw)
            gate_g, d_ag_g = gate[:, cs], d_ag_v[:, cs]
            dgate_groups.append(d_ag_g * o * _dsilu(gate_g))
            do = (d_ag_g * _silu(gate_g)).astype(BF16)
            dp_all = _dot(do, vd, NT)
            dss = []
            for r in range(per_kv):
                p, dp = ps[r], dp_all[:, r * keys:(r + 1) * keys]
                delta = jnp.sum(p * dp, axis=-1, keepdims=True)
                dss.append(p * (dp - delta) * scale)
                gs_h = jnp.sum(-p_sinks[r] * delta, axis=0, keepdims=True)
                gsink = gsink + jnp.where((lane == g * per_kv + r) & (sub == 0), gs_h, 0.0)
            ds = jnp.concatenate(dss, axis=1).astype(BF16)
            dq_groups.append(_dot(ds, kd, NN))
            dk_heads.append(_fold_heads(_dot(ds, qg, TN), per_kv))
            dv_heads.append(_fold_heads(_dot(pb, do, TN), per_kv))
        dqh = jnp.concatenate(dq_groups, axis=1)
        gqw = jnp.sum(dqh * qn, axis=0, keepdims=True)
        dqn = dqh * qw
        dq = q_rstd * (dqn - qn * _head_mean(dqn * qn, gmat_v))
        dkh = _join_heads(dk_heads)
        gkw = jnp.sum(dkh * kn, axis=0, keepdims=True)
        dkn = dkh * kw
        dk = k_rstd * (dkn - kn * _head_mean(dkn * kn, gmat_v))
        dp_ref[:, 0:attn_w] = dq.astype(BF16)
        dp_ref[:, attn_w:attn_w + 2 * kv_w] = jnp.zeros((BLOCK, 2 * kv_w), BF16)
        dp_ref[:, attn_w + 2 * kv_w:w_out] = jnp.concatenate(dgate_groups, axis=1).astype(BF16)
        dkv_ref[0] = jnp.concatenate([dk, _join_heads(dv_heads)], axis=1)

        @pl.when(i == 0)
        def _():
            gqw_ref[...] = gqw
            gkw_ref[...] = gkw
            gs_ref[...] = gsink

        @pl.when(i > 0)
        def _():
            gqw_ref[...] += gqw
            gkw_ref[...] += gkw
            gs_ref[...] += gsink

    const = lambda a: pl.BlockSpec(a.shape, lambda i: (0, 0))
    res, landed = _call(
        body, [proj, proj, proj, proj, proj, proj, proj, d_ag, qw_row, kw_row, gmat, sinks, dproj], name=name,
        out_shape=[jax.ShapeDtypeStruct(dproj.shape, BF16),
                   jax.ShapeDtypeStruct((nb, 2 * BLOCK, 2 * kv_w), F32),
                   jax.ShapeDtypeStruct(qw_row.shape, F32), jax.ShapeDtypeStruct(kw_row.shape, F32),
                   jax.ShapeDtypeStruct((SUBLANES, LANES), F32)],
        grid=(nb,),
        in_specs=_attn_specs(attn_w, kv_w) + [pl.BlockSpec((BLOCK, attn_w), lambda i: (i, 0)), const(qw_row),
                                              const(kw_row), const(gmat), pl.BlockSpec(memory_space=pltpu.SMEM),
                                              _any_spec()],
        out_specs=[pl.BlockSpec((BLOCK, w_out), lambda i: (i, 0)),
                   pl.BlockSpec((1, 2 * BLOCK, 2 * kv_w), lambda i: (i, 0, 0)),
                   const(qw_row), const(kw_row), pl.BlockSpec((SUBLANES, LANES), lambda i: (0, 0))],
        aliases={12: 0}, semantics=("arbitrary",), vmem=48 * 1024 * 1024, rider=rider)
    return res if rider is None else (res, landed)


def _attention_dkv(dproj, dkv, *, attn_w, kv_w, name):
    rows = dproj.shape[0]
    nb = rows // BLOCK
    col = attn_w // (2 * kv_w)

    def body(cur_ref, nxt_ref, _, o_ref):
        i = pl.program_id(0)
        nxt = jnp.where(i < nb - 1, nxt_ref[0, 0:BLOCK, :], 0.0)
        o_ref[...] = (cur_ref[0, BLOCK:2 * BLOCK, :] + nxt).astype(BF16)

    blk = lambda f: pl.BlockSpec((1, 2 * BLOCK, 2 * kv_w), f)
    return _pallas(
        body, name=name, out_shape=jax.ShapeDtypeStruct(dproj.shape, BF16), grid=(nb,),
        in_specs=[blk(lambda i: (i, 0, 0)), blk(lambda i: (jnp.minimum(i + 1, nb - 1), 0, 0)), _any_spec()],
        out_specs=pl.BlockSpec((BLOCK, 2 * kv_w), lambda i: (i, col)),
        input_output_aliases={2: 0},
        compiler_params=_params(("parallel",)),
    )(dkv, dkv, dproj)


def _cmul(ar, ai, br, bi):
    return ar * br - ai * bi, ar * bi + ai * br


def _ssm_prep(a_re, a_im, log_dt_col, steps, name):
    assert steps & (steps - 1) == 0

    def body(are_ref, aim_ref, ldt_ref, abr_ref, abi_ref, cfr_ref, cfi_ref, apr_ref, api_ref):
        are, aim = are_ref[...], aim_ref[...]
        dt = jnp.exp(ldt_ref[...])
        mag = jnp.exp(dt * are)
        abr = mag * jnp.cos(dt * aim)
        abi = mag * jnp.sin(dt * aim)
        num_re, num_im = abr - 1.0, abi
        den = are * are + aim * aim
        abr_ref[...] = abr
        abi_ref[...] = abi
        cfr_ref[...] = (num_re * are + num_im * aim) / den
        cfi_ref[...] = (num_im * are - num_re * aim) / den
        pr, pi = abr, abi
        n = steps
        while n > 1:
            pr, pi = _cmul(pr, pi, pr, pi)
            n //= 2
        apr_ref[...] = pr
        api_ref[...] = pi

    shp = jax.ShapeDtypeStruct(a_re.shape, F32)
    return _pallas(body, name=name, out_shape=[shp] * 6)(a_re, a_im, log_dt_col)


def _ssm_param_bwd(a_re, a_im, log_dt_col, d_ab_re, d_ab_im, d_cf_re, d_cf_im, name):
    def body(are_ref, aim_ref, ldt_ref, gabr_ref, gabi_ref, gcfr_ref, gcfi_ref, dar_ref, dai_ref, dldt_ref):
        are, aim = are_ref[...], aim_ref[...]
        dt = jnp.exp(ldt_ref[...])
        mag = jnp.exp(dt * are)
        abr = mag * jnp.cos(dt * aim)
        abi = mag * jnp.sin(dt * aim)
        den = are * are + aim * aim
        cfr = ((abr - 1.0) * are + abi * aim) / den
        cfi = (abi * are - (abr - 1.0) * aim) / den
        gabr, gabi = jnp.sum(gabr_ref[...], axis=0), jnp.sum(gabi_ref[...], axis=0)
        gcfr, gcfi = jnp.sum(gcfr_ref[...], axis=0), jnp.sum(gcfi_ref[...], axis=0)
        inv_r, inv_i = are / den, -aim / den
        t_r, t_i = _cmul(inv_r, -inv_i, gcfr, gcfi)
        gabr, gabi = gabr + t_r, gabi + t_i
        q_r, q_i = _cmul(cfr, cfi, inv_r, inv_i)
        da_r, da_i = _cmul(-q_r, q_i, gcfr, gcfi)
        gz_r, gz_i = _cmul(abr, -abi, gabr, gabi)
        dar_ref[...] = da_r + dt * gz_r
        dai_ref[...] = da_i + dt * gz_i
        dldt_ref[...] = dt * jnp.sum(are * gz_r + aim * gz_i, axis=-1, keepdims=True)

    shp = jax.ShapeDtypeStruct(a_re.shape, F32)
    return _pallas(body, name=name, out_shape=[shp, shp, jax.ShapeDtypeStruct(log_dt_col.shape, F32)])(
        a_re, a_im, log_dt_col, d_ab_re, d_ab_im, d_cf_re, d_cf_im)


SCAN_LANES = 512
W_IN_GRAD_PARTS = 2


def _scan_segments(xr_ref, xi_ref, a_re, a_im, ap_re, ap_im, carry_re, carry_im, cm_re, cm_im, steps, reverse):
    n = xr_ref.shape[1]
    order = range(steps - 1, -1, -1) if reverse else range(steps)
    seg_order = range(SUBLANES - 1, -1, -1) if reverse else range(SUBLANES)
    for c0 in range(0, n, SCAN_LANES):
        ls = slice(c0, c0 + SCAN_LANES)
        ar = jnp.broadcast_to(a_re[:, ls], (SUBLANES, SCAN_LANES))
        ai = jnp.broadcast_to(a_im[:, ls], (SUBLANES, SCAN_LANES))

        def local(t, s, ar=ar, ai=ai, ls=ls):
            j = steps - 1 - t if reverse else t
            r0 = pl.multiple_of(j * SUBLANES, SUBLANES)
            sr, si = _cmul(ar, ai, s[0], s[1])
            sr = sr + xr_ref[pl.ds(r0, SUBLANES), ls]
            si = si + xi_ref[pl.ds(r0, SUBLANES), ls]
            xr_ref[pl.ds(r0, SUBLANES), ls] = sr
            xi_ref[pl.ds(r0, SUBLANES), ls] = si
            return sr, si

        zero = jnp.zeros((SUBLANES, SCAN_LANES), F32)
        end_r, end_i = lax.fori_loop(0, steps, local, (zero, zero))
        cr, ci = carry_re[:, ls], carry_im[:, ls]
        apr, api = ap_re[:, ls], ap_im[:, ls]
        for r in seg_order:
            cm_re[r:r + 1, ls] = cr
            cm_im[r:r + 1, ls] = ci
            tr, ti = _cmul(apr, api, cr, ci)
            cr, ci = end_r[r:r + 1, :] + tr, end_i[r:r + 1, :] + ti
        carry_re[:, ls] = cr
        carry_im[:, ls] = ci

        def fix(t, s, ar=ar, ai=ai, ls=ls):
            j = steps - 1 - t if reverse else t
            r0 = pl.multiple_of(j * SUBLANES, SUBLANES)
            sr, si = _cmul(ar, ai, s[0], s[1])
            xr_ref[pl.ds(r0, SUBLANES), ls] += sr
            xi_ref[pl.ds(r0, SUBLANES), ls] += si
            return sr, si

        lax.fori_loop(0, steps, fix, (cm_re[:, ls], cm_im[:, ls]))
    del order


SB_GROUPS = MXU_DIM // GROUP
SB_STATE = SB_GROUPS * STATE


def _ssm_rows(b_re, b_im, c_re, c_im):
    def rows(m):
        flat = m.reshape(-1, STATE).astype(F32)
        return jnp.concatenate([flat, flat], axis=1)
    return rows(b_re.transpose(0, 2, 1)), rows(b_im.transpose(0, 2, 1)), rows(c_re), rows(c_im)


def _from_ssm_rows(rows, transpose):
    g = rows[:, :STATE].reshape(-1, GROUP, STATE)
    return g.transpose(0, 2, 1) if transpose else g


def _own_group(shape):
    row_g = lax.broadcasted_iota(jnp.int32, shape, 0) // GROUP
    col_g = lax.broadcasted_iota(jnp.int32, shape, 1) // STATE
    return row_g == col_g


def _block_diagonal(rows):
    tiled = jnp.concatenate([rows] * (SB_STATE // LANES), axis=1)
    return jnp.where(_own_group(tiled.shape), tiled, 0.0).astype(BF16)


def _block_rows(acc):
    x = jnp.where(_own_group(acc.shape), acc, 0.0)
    while x.shape[1] > LANES:
        half = x.shape[1] // 2
        x = x[:, :half] + x[:, half:]
    return x + pltpu.roll(x, STATE, axis=1)


def _rows_to_segments(dst, srcs, steps, stage):
    for ref, off in srcs:
        for k in range(ref.shape[1] // LANES):
            stage[off // LANES + k] = ref[:, k * LANES:(k + 1) * LANES].astype(F32)
    for k in range(dst.shape[1] // LANES):
        for j in range(steps):
            dst[j * SUBLANES:(j + 1) * SUBLANES, k * LANES:(k + 1) * LANES] = (
                stage[k, pl.ds(j, SUBLANES, stride=steps), :])


def _segments_to_rows(dst, src, steps, stage):
    for k in range(src.shape[1] // LANES):
        for j in range(steps):
            stage[k, pl.ds(j, SUBLANES, stride=steps), :] = (
                src[j * SUBLANES:(j + 1) * SUBLANES, k * LANES:(k + 1) * LANES])
    for k in range(src.shape[1] // LANES):
        dst[:, k * LANES:(k + 1) * LANES] = stage[k]


def _u_specs(w, o_u, chunk, index):
    half = w // 2
    assert o_u % half == 0
    return [pl.BlockSpec((chunk, half), lambda c, k=k: (index(c), o_u // half + k)) for k in range(2)]


def _ssm_fwd(proj, o_u, bc_rows, rows_p, d_row, *, chunk, name, rider=None):
    rows = proj.shape[0]
    w = d_row.shape[1]
    nc = rows // chunk
    steps = chunk // SUBLANES
    nsb = w // MXU_DIM
    n_state = nsb * SB_STATE

    def body(ulo_ref, uhi_ref, b2r_ref, b2i_ref, c2r_ref, c2i_ref, abr_ref, abi_ref, cfr_ref, cfi_ref, apr_ref,
             api_ref, d_ref, y_ref, str_ref, sti_ref, yg_ref, bre_ref, bim_ref, cre_ref, cim_ref, useg, yseg, stage, sr, si,
             carry_r, carry_i, cm_r, cm_i):
        @pl.when(pl.program_id(0) == 0)
        def _():
            for src, dst in ((b2r_ref, bre_ref), (b2i_ref, bim_ref), (c2r_ref, cre_ref), (c2i_ref, cim_ref)):
                for sb in range(nsb):
                    dst[sb] = _block_diagonal(src[sb * MXU_DIM:(sb + 1) * MXU_DIM, :])
            carry_r[...] = jnp.zeros_like(carry_r)
            carry_i[...] = jnp.zeros_like(carry_i)

        str_ref[0] = carry_r[...]
        sti_ref[0] = carry_i[...]
        _rows_to_segments(useg, [(ulo_ref, 0), (uhi_ref, w // 2)], steps, stage)
        for sb in range(nsb):
            us = slice(sb * MXU_DIM, (sb + 1) * MXU_DIM)
            ss = slice(sb * SB_STATE, (sb + 1) * SB_STATE)
            ub = useg[:, us].astype(BF16)
            bur = _dot(ub, bre_ref[sb], NN)
            bui = _dot(ub, bim_ref[sb], NN)
            xr, xi = _cmul(cfr_ref[:, ss], cfi_ref[:, ss], bur, bui)
            sr[:, ss] = xr
            si[:, ss] = xi
        _scan_segments(sr, si, abr_ref[...], abi_ref[...], apr_ref[...], api_ref[...],
                       carry_r, carry_i, cm_r, cm_i, steps, False)
        for sb in range(nsb):
            us = slice(sb * MXU_DIM, (sb + 1) * MXU_DIM)
            ss = slice(sb * SB_STATE, (sb + 1) * SB_STATE)
            y = _dot(sr[:, ss].astype(BF16), cre_ref[sb], NT) - _dot(si[:, ss].astype(BF16), cim_ref[sb], NT)
            yseg[:, us] = y + d_ref[:, us] * useg[:, us]
        _segments_to_rows(y_ref, yseg, steps, stage)
        yg_ref[...] = _gelu(y_ref[...]).astype(BF16)

    const = lambda a: pl.BlockSpec(a.shape, lambda c: (0,) * a.ndim)
    row_n = pl.BlockSpec((1, n_state), lambda c: (0, 0))
    st = pl.BlockSpec((1, 1, n_state), lambda c: (c, 0, 0))
    held = [pltpu.VMEM((nsb, MXU_DIM, SB_STATE), BF16)] * 4
    vmem = 4 * _nbytes((nsb, MXU_DIM, SB_STATE), BF16) + 3 * _nbytes((chunk, n_state), F32)
    res, landed = _call(
        body, [proj, proj, *bc_rows, *rows_p, d_row], name=name,
        out_shape=[jax.ShapeDtypeStruct((rows, w), F32), jax.ShapeDtypeStruct((nc, 1, n_state), F32),
                   jax.ShapeDtypeStruct((nc, 1, n_state), F32), jax.ShapeDtypeStruct((rows, w), BF16)],
        grid=(nc,),
        in_specs=_u_specs(w, o_u, chunk, lambda c: c) + [const(b) for b in bc_rows]
        + [row_n] * 6 + [pl.BlockSpec((1, w), lambda c: (0, 0))],
        out_specs=[pl.BlockSpec((chunk, w), lambda c: (c, 0)), st, st, pl.BlockSpec((chunk, w), lambda c: (c, 0))],
        scratch_shapes=held + [pltpu.VMEM((chunk, w), F32), pltpu.VMEM((chunk, w), F32),
                               pltpu.VMEM((w // LANES, chunk, LANES), F32),
                               pltpu.VMEM((chunk, n_state), F32), pltpu.VMEM((chunk, n_state), F32),
                               pltpu.VMEM((1, n_state), F32), pltpu.VMEM((1, n_state), F32),
                               pltpu.VMEM((SUBLANES, n_state), F32), pltpu.VMEM((SUBLANES, n_state), F32)],
        semantics=("arbitrary",), vmem=vmem, rider=rider)
    return res if rider is None else (res, landed)


def _ssm_bwd(proj, o_u, y, dyg, st_re, st_im, bc_rows, rows_p, d_row, *, chunk, name, rider=None):
    rows = proj.shape[0]
    w = d_row.shape[1]
    nc = rows // chunk
    steps = chunk // SUBLANES
    nsb = w // MXU_DIM
    n_state = nsb * SB_STATE

    def body(ulo_ref, uhi_ref, y_ref, dyg_ref, str_ref, sti_ref, b2r_ref, b2i_ref, c2r_ref, c2i_ref,
             abr_ref, abi_ref, cfr_ref, cfi_ref, apr_ref, api_ref, d_ref,
             du_ref, gb2r_ref, gb2i_ref, gc2r_ref, gc2i_ref, gabr_ref, gabi_ref, gcfr_ref, gcfi_ref, dd_ref,
             bre_ref, bim_ref, cre_ref, cim_ref, dbre_ref, dbim_ref, dcre_ref, dcim_ref, useg, dyseg, dynat, stage,
             bur, bui, sr, si, lr, li, carry_r, carry_i, lam_r, lam_i, cm_r, cm_i, cl_r, cl_i):
        first = pl.program_id(0) == 0

        @pl.when(first)
        def _():
            for src, dst in ((b2r_ref, bre_ref), (b2i_ref, bim_ref), (c2r_ref, cre_ref), (c2i_ref, cim_ref)):
                for sb in range(nsb):
                    dst[sb] = _block_diagonal(src[sb * MXU_DIM:(sb + 1) * MXU_DIM, :])
            lam_r[...] = jnp.zeros_like(lam_r)
            lam_i[...] = jnp.zeros_like(lam_i)
            for ref in (dbre_ref, dbim_ref, dcre_ref, dcim_ref, gabr_ref, gabi_ref, gcfr_ref, gcfi_ref, dd_ref):
                ref[...] = jnp.zeros_like(ref)

        dynat[...] = dyg_ref[...].astype(F32) * _dgelu(y_ref[...])
        half = w // 2
        dd_ref[:, :half] += jnp.sum(dynat[:, :half] * ulo_ref[...].astype(F32), axis=0, keepdims=True)
        dd_ref[:, half:] += jnp.sum(dynat[:, half:] * uhi_ref[...].astype(F32), axis=0, keepdims=True)
        _rows_to_segments(useg, [(ulo_ref, 0), (uhi_ref, half)], steps, stage)
        _rows_to_segments(dyseg, [(dynat, 0)], steps, stage)
        dy = dyseg[...]
        dyb = dy.astype(BF16)
        ub = useg[...].astype(BF16)
        carry_r[...] = str_ref[0]
        carry_i[...] = sti_ref[0]
        for sb in range(nsb):
            us = slice(sb * MXU_DIM, (sb + 1) * MXU_DIM)
            ss = slice(sb * SB_STATE, (sb + 1) * SB_STATE)
            br = _dot(ub[:, us], bre_ref[sb], NN)
            bi = _dot(ub[:, us], bim_ref[sb], NN)
            bur[:, ss] = br
            bui[:, ss] = bi
            xr, xi = _cmul(cfr_ref[:, ss], cfi_ref[:, ss], br, bi)
            sr[:, ss] = xr
            si[:, ss] = xi
            lr[:, ss] = _dot(dyb[:, us], cre_ref[sb], NN)
            li[:, ss] = -_dot(dyb[:, us], cim_ref[sb], NN)
        abr, abi = abr_ref[...], abi_ref[...]
        apr, api = apr_ref[...], api_ref[...]
        _scan_segments(sr, si, abr, abi, apr, api, carry_r, carry_i, cm_r, cm_i, steps, False)
        for sb in range(nsb):
            us = slice(sb * MXU_DIM, (sb + 1) * MXU_DIM)
            ss = slice(sb * SB_STATE, (sb + 1) * SB_STATE)
            dcre_ref[sb] += _dot(dyb[:, us], sr[:, ss].astype(BF16), TN)
            dcim_ref[sb] -= _dot(dyb[:, us], si[:, ss].astype(BF16), TN)
        _scan_segments(lr, li, abr, -abi, apr, -api, lam_r, lam_i, cl_r, cl_i, steps, True)
        for c0 in range(0, n_state, SCAN_LANES):
            ls = slice(c0, c0 + SCAN_LANES)
            cfr = jnp.broadcast_to(cfr_ref[:, ls], (SUBLANES, SCAN_LANES))
            cfi = jnp.broadcast_to(cfi_ref[:, ls], (SUBLANES, SCAN_LANES))

            def step(j, acc, ls=ls, cfr=cfr, cfi=cfi):
                gar, gai, gcr, gci, pr, pi = acc
                r0 = pl.multiple_of(j * SUBLANES, SUBLANES)
                rws = pl.ds(r0, SUBLANES)
                l_r, l_i = lr[rws, ls], li[rws, ls]
                t_r, t_i = _cmul(pr, -pi, l_r, l_i)
                b_r, b_i = bur[rws, ls], bui[rws, ls]
                c_r, c_i = _cmul(b_r, -b_i, l_r, l_i)
                x_r, x_i = _cmul(cfr, -cfi, l_r, l_i)
                bur[rws, ls] = x_r
                bui[rws, ls] = x_i
                return gar + t_r, gai + t_i, gcr + c_r, gci + c_i, sr[rws, ls], si[rws, ls]

            zero = jnp.zeros((SUBLANES, SCAN_LANES), F32)
            gar, gai, gcr, gci, _, _ = lax.fori_loop(
                0, steps, step, (zero, zero, zero, zero, cm_r[:, ls], cm_i[:, ls]))
            gabr_ref[:, ls] += gar
            gabi_ref[:, ls] += gai
            gcfr_ref[:, ls] += gcr
            gcfi_ref[:, ls] += gci
        for sb in range(nsb):
            us = slice(sb * MXU_DIM, (sb + 1) * MXU_DIM)
            ss = slice(sb * SB_STATE, (sb + 1) * SB_STATE)
            xr, xi = bur[:, ss].astype(BF16), bui[:, ss].astype(BF16)
            du = _dot(xr, bre_ref[sb], NT) + _dot(xi, bim_ref[sb], NT)
            useg[:, us] = du + d_ref[:, us] * dy[:, us]
            dbre_ref[sb] += _dot(ub[:, us], xr, TN)
            dbim_ref[sb] += _dot(ub[:, us], xi, TN)
        _segments_to_rows(du_ref, useg, steps, stage)

        @pl.when(pl.program_id(0) == nc - 1)
        def _():
            for src, dst in ((dbre_ref, gb2r_ref), (dbim_ref, gb2i_ref), (dcre_ref, gc2r_ref), (dcim_ref, gc2i_ref)):
                for sb in range(nsb):
                    dst[sb * MXU_DIM:(sb + 1) * MXU_DIM, :] = _block_rows(src[sb])

    rev = lambda c: nc - 1 - c
    const = lambda a: pl.BlockSpec(a.shape, lambda c: (0,) * a.ndim)
    tile = pl.BlockSpec((chunk, w), lambda c: (rev(c), 0))
    row_n = pl.BlockSpec((1, n_state), lambda c: (0, 0))
    row_w = pl.BlockSpec((1, w), lambda c: (0, 0))
    st = pl.BlockSpec((1, 1, n_state), lambda c: (rev(c), 0, 0))
    acc8 = pl.BlockSpec((SUBLANES, n_state), lambda c: (0, 0))
    big = pltpu.VMEM((chunk, n_state), F32)
    small = pltpu.VMEM((chunk, w), F32)
    row = pltpu.VMEM((1, n_state), F32)
    eight = pltpu.VMEM((SUBLANES, n_state), F32)
    blk = (nsb, MXU_DIM, SB_STATE)
    held = [pltpu.VMEM(blk, BF16)] * 4 + [pltpu.VMEM(blk, F32)] * 4
    vmem = (4 * (_nbytes(blk, BF16) + _nbytes(blk, F32)) + 7 * _nbytes((chunk, n_state), F32)
            + 20 * _nbytes((chunk, w), F32) + 16 * _nbytes(bc_rows[0].shape, F32))
    res, landed = _call(
        body, [proj, proj, y, dyg, st_re, st_im, *bc_rows, *rows_p, d_row], name=name,
        out_shape=[jax.ShapeDtypeStruct((rows, w), F32)] + [jax.ShapeDtypeStruct(b.shape, F32) for b in bc_rows]
        + [jax.ShapeDtypeStruct((SUBLANES, n_state), F32)] * 4 + [jax.ShapeDtypeStruct((1, w), F32)],
        grid=(nc,),
        in_specs=_u_specs(w, o_u, chunk, rev) + [tile, tile, st, st] + [const(b) for b in bc_rows]
        + [row_n] * 6 + [row_w],
        out_specs=[tile] + [const(b) for b in bc_rows] + [acc8] * 4 + [row_w],
        scratch_shapes=held + [small] * 3 + [pltpu.VMEM((w // LANES, chunk, LANES), F32)] + [big] * 6 + [row] * 4
        + [eight] * 4,
        semantics=("arbitrary",), vmem=vmem, rider=rider)
    return res if rider is None else (res, landed)


def _loss_grad(x, mm, target, name):
    rows, d = x.shape

    def fn(xv, mv, tv):
        err = xv + mv - tv
        g = err * (1.0 / d)
        return g, g, 0.5 * err * g

    return _ew(fn, name=name, rows=rows, width=d, tiles=[(x, 0), (mm, 0), (target, 0)],
               outs=[(F32, d, 0), (BF16, d, 0)], accs=1)


def _pair_sum(grad, recv, name):
    r4, cdim = recv.shape
    r = r4 // N_CHIPS
    tr = _tile(r, 544, 16)
    g4 = grad.reshape(N_CHIPS, 2, r, cdim)
    r3 = recv.reshape(N_CHIPS, r, cdim)
    core = jnp.reshape(lax.axis_index("c"), (1,)).astype(jnp.int32)

    def body(c_ref, g_ref, r_ref, o_ref):
        o_ref[...] = (g_ref[0] + r_ref[...]).astype(BF16)

    out = _pallas(
        body, name=name, out_shape=jax.ShapeDtypeStruct((N_CHIPS, r, cdim), BF16),
        grid_spec=pltpu.PrefetchScalarGridSpec(
            num_scalar_prefetch=1, grid=(N_CHIPS, r // tr),
            in_specs=[pl.BlockSpec((1, 1, tr, cdim), lambda j, i, c: (j, c[0], i, 0)),
                      pl.BlockSpec((1, tr, cdim), lambda j, i, c: (j, i, 0))],
            out_specs=pl.BlockSpec((1, tr, cdim), lambda j, i, c: (j, i, 0))),
        compiler_params=_params(("parallel", "parallel"), 6 * _nbytes((tr, cdim), F32)),
    )(core, g4, r3)
    return out.reshape(r4, cdim)


def _chip_sum(recv, name):
    r4, cdim = recv.shape
    r = r4 // N_CHIPS
    tr = _tile(r, 544, 16)
    r3 = recv.reshape(N_CHIPS, r, cdim)

    def body(r_ref, o_ref):
        acc = r_ref[0].astype(F32)
        for j in range(1, N_CHIPS):
            acc = acc + r_ref[j].astype(F32)
        o_ref[...] = acc

    return _pallas(
        body, name=name, out_shape=jax.ShapeDtypeStruct((r, cdim), F32), grid=(r // tr,),
        in_specs=[pl.BlockSpec((N_CHIPS, tr, cdim), lambda i: (0, i, 0))],
        out_specs=pl.BlockSpec((tr, cdim), lambda i: (i, 0)),
        compiler_params=_params(("parallel",), 8 * _nbytes((tr, cdim), F32)),
    )(r3)


def _adamw_math(w, g, m, v):
    m = ADAM_B1 * m + (1.0 - ADAM_B1) * g
    v = ADAM_B2 * v + (1.0 - ADAM_B2) * (g * g)
    m_hat = m / (1.0 - ADAM_B1 ** ADAM_STEP)
    v_hat = v / (1.0 - ADAM_B2 ** ADAM_STEP)
    delta = -ADAM_LR * (m_hat / (jnp.sqrt(v_hat) + ADAM_EPS) + ADAM_WD * w)
    return delta, m, v


def _adamw(w, g, m, v, name):
    rows, cols = w.shape
    tr = _tile(rows, 256, SUBLANES)

    def body(w_ref, g_ref, m_ref, v_ref, d_ref, nm_ref, nv_ref):
        d, nm, nv = _adamw_math(w_ref[...], g_ref[...], m_ref[...], v_ref[...])
        d_ref[...] = d
        nm_ref[...] = nm
        nv_ref[...] = nv

    spec = pl.BlockSpec((tr, cols), lambda i: (i, 0))
    shp = jax.ShapeDtypeStruct((rows, cols), F32)
    return _pallas(
        body, name=name, out_shape=[shp] * 3, grid=(rows // tr,), in_specs=[spec] * 4, out_specs=[spec] * 3,
        compiler_params=_params(("parallel",)),
    )(w, g, m, v)


def _adamw_chips(w, parts, m, v, name):
    rows, cols = w.shape
    assert sum(p.shape[1] for p in parts) == cols
    tr = _tile(rows, 64, 16)
    n = len(parts)

    def body(*refs):
        w_ref, m_ref, v_ref = refs[0], refs[1 + n], refs[2 + n]
        g_ref, d_ref, nm_ref, nv_ref = refs[3 + n:]
        cols_g = []
        for p_ref in refs[1:1 + n]:
            acc = p_ref[0].astype(F32)
            for j in range(1, N_CHIPS):
                acc = acc + p_ref[j].astype(F32)
            cols_g.append(acc)
        g = cols_g[0] if n == 1 else jnp.concatenate(cols_g, axis=1)
        d, nm, nv = _adamw_math(w_ref[...], g, m_ref[...], v_ref[...])
        g_ref[...] = g
        d_ref[...] = d
        nm_ref[...] = nm
        nv_ref[...] = nv

    spec = pl.BlockSpec((tr, cols), lambda i: (i, 0))
    part_specs = [pl.BlockSpec((N_CHIPS, tr, p.shape[1]), lambda i: (0, i, 0)) for p in parts]
    shp = jax.ShapeDtypeStruct((rows, cols), F32)
    return _pallas(
        body, name=name, out_shape=[shp] * 4, grid=(rows // tr,),
        in_specs=[spec] + part_specs + [spec, spec], out_specs=[spec] * 4,
        compiler_params=_params(("parallel",)),
    )(w, *[p.reshape(N_CHIPS, rows, p.shape[1]) for p in parts], m, v)


def _adamw_small(w, parts, m, v, name):
    rows, cols = w.shape
    p3 = parts.reshape(N_DEV, rows, cols)

    def body(w_ref, p_ref, m_ref, v_ref, g_ref, d_ref, nm_ref, nv_ref):
        g = p_ref[0]
        for k in range(1, N_DEV):
            g = g + p_ref[k]
        d, nm, nv = _adamw_math(w_ref[...], g, m_ref[...], v_ref[...])
        g_ref[...] = g
        d_ref[...] = d
        nm_ref[...] = nm
        nv_ref[...] = nv

    shp = jax.ShapeDtypeStruct((rows, cols), F32)
    return _pallas(body, name=name, out_shape=[shp] * 4)(w, p3, m, v)


SMALL = ("norm_w", "q_norm_w", "k_norm_w", "sinks", "A_re", "A_im", "log_dt", "B_re", "B_im", "C_re", "C_im",
         "D_skip", "b_glu")
LARGE = ("w_in", "w_attn_proj", "w_glu", "w_ssm_proj", "w_out")
ORDER = ("norm_w", "w_in", "q_norm_w", "k_norm_w", "sinks", "w_attn_proj", "A_re", "A_im", "log_dt", "B_re", "B_im",
         "C_re", "C_im", "D_skip", "w_glu", "b_glu", "w_ssm_proj", "w_out")


SMALL_REST = ("loss",) + SMALL[1:]


def _pack(named, keys):
    flat = jnp.concatenate([named[k].reshape(-1).astype(F32) for k in keys])
    n = flat.shape[0]
    rows = -(-n // (LANES * SUBLANES)) * SUBLANES
    return jnp.pad(flat, (0, rows * LANES - n)).reshape(rows, LANES)


def _unpack(packed, like, keys):
    flat = packed.reshape(-1)
    out, o = {}, 0
    for k in keys:
        n = like[k].size
        out[k] = flat[o:o + n].reshape(like[k].shape)
        o += n
    return out


def _step(xs, target, p, shards):
    s_in, s_ap, s_glu, s_sp, s_o = shards
    seq, d = xs.shape
    attn_w = (d // 128) * HEAD_DIM
    n_q = attn_w // HEAD_DIM
    kv_w = N_KV_HEADS * HEAD_DIM
    ssm_w = d // 2
    n_groups = ssm_w // GROUP
    n_state = n_groups * STATE
    in_w = N_DEV * s_in.shape[0]
    assert in_w == 2 * attn_w + 2 * kv_w + 2 * ssm_w + 2 * d
    o_u = 2 * attn_w + 2 * kv_w
    o_z = o_u + ssm_w
    o_ga = o_z + ssm_w
    chunk = min(BLOCK, seq)
    cw = d // 4

    norm_row = p["norm_w"].reshape(1, d)
    h = _rmsnorm_fwd(xs, norm_row, "rmsnorm_fwd")
    proj, w_in_t, _ = _gather_in_proj(h, s_in, name="gather_in_proj", rider=None)
    qw_row = jnp.tile(p["q_norm_w"], n_q).reshape(1, attn_w)
    kw_row = jnp.tile(p["k_norm_w"], N_KV_HEADS).reshape(1, kv_w)
    gmat = _head_mean_matrix()
    ag = _attention_fwd(proj, qw_row, kw_row, gmat, p["sinks"], attn_w=attn_w, kv_w=kv_w, name="attention_fwd")

    log_dt_col = p["log_dt"].reshape(n_groups, 1)
    prep = _ssm_prep(p["A_re"], p["A_im"], log_dt_col, chunk // SUBLANES, "ssm_prep")
    rows_p = [v.reshape(1, n_state) for v in prep]
    bc_rows = _ssm_rows(p["B_re"], p["B_im"], p["C_re"], p["C_im"])
    d_row = p["D_skip"].reshape(1, ssm_w)
    (y_ssm, st_re, st_im, yg), (w_ap_t, w_glu_t, w_sp_t, w_o) = _ssm_fwd(
        proj, o_u, bc_rows, rows_p, d_row, chunk=chunk, name="ssm_fwd", rider=_all_gather([s_ap, s_glu, s_sp, s_o]))
    glu = _matmul(yg, w_glu_t, mode="nt", name="glu_proj", out_dtype=BF16, bias=p["b_glu"].reshape(1, 2 * ssm_w))
    (ts,) = _ew(lambda ga, gb, z: ga * _sigmoid(gb) * _silu(z), name="glu_gate", rows=seq, width=ssm_w,
                tiles=[(glu, 0), (glu, ssm_w), (proj, o_z)], outs=[(BF16, ssm_w, 0)], cw=cw)
    yy = _matmul(ag, w_ap_t, mode="nt", name="attn_proj", out_dtype=BF16, out_cols=(2 * d, 0))
    yy = _matmul(ts, w_sp_t, mode="nt", name="ssm_proj", out_dtype=BF16, out_cols=(2 * d, d), into=yy)
    (merged,) = _ew(lambda ya, ys, ga, gs: _sigmoid(ga) * ya + _sigmoid(gs) * ys, name="merge", rows=seq, width=d,
                    tiles=[(yy, 0), (yy, d), (proj, o_ga), (proj, o_ga + d)], outs=[(BF16, d, 0)], cw=cw)
    mm = _matmul(merged, w_o, mode="nn", name="out_proj")
    dout, dout_b, loss_cols = _loss_grad(xs, mm, target, "loss_grad")
    loss_local = jnp.sum(loss_cols)

    g_w_o = _matmul(merged, dout_b, mode="tn", name="grad_w_out", tm=512, tk=4096)
    dmerged, (sib_o,) = _matmul(dout_b, w_o, mode="nt", name="d_merged", out_dtype=BF16,
                                rider=_sibling_exchange([g_w_o]))
    pair_o = _pair_sum(g_w_o, sib_o, "pair_sum_w_out")

    def merge_bwd(dm, y, g):
        s = _sigmoid(g)
        return dm * s, dm * y * s * (1.0 - s)

    dyy, dproj = _ew(merge_bwd, name="merge_bwd", rows=seq, width=2 * d,
                     tiles=[(dmerged, 0, d), (yy, 0), (proj, o_ga)],
                     outs=[(BF16, 2 * d, 0), (BF16, in_w, o_ga)], cw=cw)
    dy_a, dy_s = Cols(dyy, 0, d), Cols(dyy, d, d)
    g_w_ap_t = _matmul(dy_a, ag, mode="tn", name="grad_w_attn_proj", tm=512, tk=4096)
    g_w_sp_t = _matmul(dy_s, ts, mode="tn", name="grad_w_ssm_proj", tm=512, tk=4096)
    d_ag = _matmul(dy_a, w_ap_t, mode="nn", name="d_attn_gated", out_dtype=BF16)
    d_ts = _matmul(dy_s, w_sp_t, mode="nn", name="d_ssm_gated", out_dtype=BF16)

    (dproj, dkv, g_qw, g_kw, g_sinks), (chips_o, sib_ap, sib_sp) = _attention_bwd(
        proj, d_ag, dproj, qw_row, kw_row, gmat, p["sinks"], attn_w=attn_w, kv_w=kv_w, name="attention_bwd",
        rider=_join(_chip_exchange([pair_o]), _sibling_exchange([g_w_ap_t, g_w_sp_t])))
    pair_ap = _pair_sum(g_w_ap_t, sib_ap, "pair_sum_w_attn_proj")
    pair_sp = _pair_sum(g_w_sp_t, sib_sp, "pair_sum_w_ssm_proj")
    dproj = _attention_dkv(dproj, dkv, attn_w=attn_w, kv_w=kv_w, name="attention_dkv")

    n_half = ssm_w // _tile(2 * ssm_w, cw)

    def glu_bwd(j, dt, ga, gb, z):
        sb, sz = _sigmoid(gb), _silu(z)
        dg = jnp.where(j < n_half, dt * sb * sz, dt * ga * sb * (1.0 - sb) * sz)
        return dg, dg

    glu_ops = [(d_ts, 0, ssm_w), (glu, 0, ssm_w), (glu, ssm_w, ssm_w), (proj, o_z, ssm_w)]
    dglu, g_bglu = _ew(glu_bwd, name="glu_bwd", rows=seq, width=2 * ssm_w, tiles=glu_ops,
                       outs=[(BF16, 2 * ssm_w, 0)], accs=1, cw=cw, with_col=True)
    (dproj,) = _ew(lambda dt, ga, gb, z: dt * ga * _sigmoid(gb) * _dsilu(z), name="glu_bwd_z", rows=seq,
                   width=ssm_w, tiles=glu_ops, outs=[(BF16, in_w, o_z)], into=[dproj], cw=cw)
    g_w_glu_t = _matmul(dglu, yg, mode="tn", name="grad_w_glu", tm=512, tk=4096)
    d_yg = _matmul(dglu, w_glu_t, mode="nn", name="d_gelu", out_dtype=BF16)
    ((du, db_re, db_im, dc_re, dc_im, gabr, gabi, gcfr, gcfi, g_d), (chips_ap, chips_sp, sib_glu)) = _ssm_bwd(
        proj, o_u, y_ssm, d_yg, st_re, st_im, bc_rows, rows_p, d_row, chunk=chunk, name="ssm_bwd",
        rider=_join(_chip_exchange([pair_ap, pair_sp]), _sibling_exchange([g_w_glu_t])))
    pair_glu = _pair_sum(g_w_glu_t, sib_glu, "pair_sum_w_glu")
    (dproj,) = _ew(lambda v: v, name="du_store", rows=seq, width=ssm_w, tiles=[(du, 0)],
                   outs=[(BF16, in_w, o_u)], into=[dproj], cw=cw)
    g_a_re, g_a_im, g_log_dt = _ssm_param_bwd(
        p["A_re"], p["A_im"], log_dt_col, *[g.reshape(SUBLANES, n_groups, STATE) for g in (gabr, gabi, gcfr, gcfi)],
        "ssm_param_bwd")
    small_grads = dict(
        loss=loss_local, q_norm_w=g_qw.reshape(n_q, HEAD_DIM).sum(0), k_norm_w=g_kw.reshape(N_KV_HEADS, HEAD_DIM).sum(0),
        sinks=g_sinks[0, :n_q], A_re=g_a_re, A_im=g_a_im, log_dt=g_log_dt.reshape(n_groups),
        B_re=_from_ssm_rows(db_re, True), B_im=_from_ssm_rows(db_im, True),
        C_re=_from_ssm_rows(dc_re, False), C_im=_from_ssm_rows(dc_im, False),
        D_skip=g_d.reshape(n_groups, GROUP), b_glu=g_bglu.reshape(2 * ssm_w))

    n_parts = W_IN_GRAD_PARTS
    wq = d // n_parts
    g_parts, pair_parts, chip_parts = [], [], []
    extra = [_chip_exchange([pair_glu]), _all_gather([_pack(small_grads, SMALL_REST)])]
    chips_glu = small_parts = dh = grad_x = g_norm = None
    for step in range(n_parts + 2):
        riders = list(extra) if step == 0 else []
        if 0 <= step - 2 < n_parts:
            riders.append(_chip_exchange([pair_parts[step - 2]]))
        if 0 <= step - 1 < n_parts:
            riders.append(_sibling_exchange([g_parts[step - 1]]))
        rider = _join(*riders) if riders else None
        if step < n_parts:
            res = _matmul(dproj, Cols(h, step * wq, wq), mode="tn", name="grad_w_in_%d" % step, tk=4096, rider=rider)
            out, landed = res if rider is not None else (res, [])
            g_parts.append(out)
        elif step == n_parts:
            dh, landed = _matmul(dproj, w_in_t, mode="nn", name="d_normed", tk=2176, rider=rider)
        else:
            (grad_x, g_norm), landed = _rmsnorm_bwd(xs, norm_row, dh, dout, "rmsnorm_bwd", rider=rider)
        landed = list(landed)
        if step == 0:
            chips_glu, small_parts = landed[:2]
            landed = landed[2:]
        if 0 <= step - 2 < n_parts:
            chip_parts.append(landed.pop(0))
        if 0 <= step - 1 < n_parts:
            pair_parts.append(_pair_sum(g_parts[step - 1], landed.pop(0), "pair_sum_w_in_%d" % (step - 1)))
    (norm_parts,) = _exchange(_all_gather([_pack(dict(norm_w=g_norm), ("norm_w",))]), "gather_norm_grad")
    from_chips = dict(zip(LARGE, (chip_parts, [chips_ap], [chips_glu], [chips_sp], [chips_o])))
    return grad_x, from_chips, small_parts, norm_parts


def kernel(x, norm_w, w_in, q_norm_w, k_norm_w, sinks, w_attn_proj, A_re, A_im, log_dt, B_re, B_im, C_re, C_im, D_skip, w_glu, b_glu, w_ssm_proj, w_out, loss_target, m_norm_w, m_w_in, m_q_norm_w, m_k_norm_w, m_sinks, m_w_attn_proj, m_A_re, m_A_im, m_log_dt, m_B_re, m_B_im, m_C_re, m_C_im, m_D_skip, m_w_glu, m_b_glu, m_w_ssm_proj, m_w_out, v_norm_w, v_w_in, v_q_norm_w, v_k_norm_w, v_sinks, v_w_attn_proj, v_A_re, v_A_im, v_log_dt, v_B_re, v_B_im, v_C_re, v_C_im, v_D_skip, v_w_glu, v_b_glu, v_w_ssm_proj, v_w_out):
    weights = dict(norm_w=norm_w, w_in=w_in, q_norm_w=q_norm_w, k_norm_w=k_norm_w, sinks=sinks,
                   w_attn_proj=w_attn_proj, A_re=A_re, A_im=A_im, log_dt=log_dt, B_re=B_re, B_im=B_im, C_re=C_re,
                   C_im=C_im, D_skip=D_skip, w_glu=w_glu, b_glu=b_glu, w_ssm_proj=w_ssm_proj, w_out=w_out)
    m_in = dict(norm_w=m_norm_w, w_in=m_w_in, q_norm_w=m_q_norm_w, k_norm_w=m_k_norm_w, sinks=m_sinks,
                w_attn_proj=m_w_attn_proj, A_re=m_A_re, A_im=m_A_im, log_dt=m_log_dt, B_re=m_B_re, B_im=m_B_im,
                C_re=m_C_re, C_im=m_C_im, D_skip=m_D_skip, w_glu=m_w_glu, b_glu=m_b_glu, w_ssm_proj=m_w_ssm_proj,
                w_out=m_w_out)
    v_in = dict(norm_w=v_norm_w, w_in=v_w_in, q_norm_w=v_q_norm_w, k_norm_w=v_k_norm_w, sinks=v_sinks,
                w_attn_proj=v_w_attn_proj, A_re=v_A_re, A_im=v_A_im, log_dt=v_log_dt, B_re=v_B_re, B_im=v_B_im,
                C_re=v_C_re, C_im=v_C_im, D_skip=v_D_skip, w_glu=v_w_glu, b_glu=v_b_glu, w_ssm_proj=v_w_ssm_proj,
                w_out=v_w_out)

    _, seq, d = x.shape
    column_sharded = LARGE[:4]
    as_rows = lambda k, a: a.T if k in column_sharded else a
    shards = [as_rows(k, weights[k]).astype(BF16) for k in LARGE]
    small = {k: weights[k] for k in SMALL}
    grad_x, from_chips, small_parts, norm_parts = _step(x.reshape(seq, d), loss_target.reshape(seq, d), small,
                                                        shards)

    grads, delta, new_m, new_v = {}, {}, {}, {}
    for k in LARGE:
        if k == "w_in":
            res = _adamw_chips(weights[k].T, from_chips[k], m_in[k].T, v_in[k].T, "adamw_" + k)
            grads[k], delta[k], new_m[k], new_v[k] = [a.T for a in res]
        elif k == "w_out":
            grads[k], delta[k], new_m[k], new_v[k] = _adamw_chips(weights[k], from_chips[k], m_in[k], v_in[k],
                                                                  "adamw_" + k)
        else:
            grads[k] = _chip_sum(from_chips[k][0], "chip_sum_" + k).T
            delta[k], new_m[k], new_v[k] = _adamw(weights[k], grads[k], m_in[k], v_in[k], "adamw_" + k)

    zero = jnp.zeros((), F32)
    for keys, parts in ((SMALL_REST, small_parts), (("norm_w",), norm_parts)):
        like = dict(small, loss=zero)
        packs = [_pack(dict(src, loss=zero), keys) for src in (weights, m_in, v_in)]
        res = _adamw_small(packs[0], parts, packs[1], packs[2], "adamw_small_%d" % len(keys))
        for dst, r in zip((grads, delta, new_m, new_v), res):
            dst.update(_unpack(r, like, keys))
    loss = grads["loss"]

    return (loss, grad_x.reshape(x.shape), *[grads[k] for k in ORDER], *[delta[k] for k in ORDER],
            *[new_m[k] for k in ORDER], *[new_v[k] for k in ORDER])
```

```python
import math
from typing import Callable, NamedTuple

import jax
import jax.numpy as jnp
import numpy as np
from jax import lax
from jax.experimental import pallas as pl
from jax.experimental.pallas import tpu as pltpu

F32 = jnp.float32
BF16 = jnp.bfloat16
MESH = pl.DeviceIdType.MESH

HEAD_DIM = 64
N_KV_HEADS = 4
GROUP = 16
STATE = 64
BLOCK = 128
NORM_EPS = 1e-6
N_DEV = 8
N_CHIPS = 4
LANES = 128
SUBLANES = 8
MXU_DIM = 256
VMEM_BYTES = 64 * 1024 * 1024
VMEM_CAP = VMEM_BYTES - 8 * 1024 * 1024

ADAM_LR = 0.001
ADAM_B1 = 0.9
ADAM_B2 = 0.999
ADAM_EPS = 1e-08
ADAM_WD = 0.01
ADAM_STEP = 10

GELU_C = math.sqrt(2.0 / math.pi)
GELU_K = 0.044715


def _tile(dim, pref, mult=LANES):
    if dim <= pref:
        return dim
    best = None
    for d in range(mult, pref + 1, mult):
        if dim % d == 0:
            best = d
    assert best is not None, (dim, pref, mult)
    return best


def _params(semantics=None, vmem=None):
    kw = {}
    if semantics is not None:
        kw["dimension_semantics"] = semantics
    if vmem is not None:
        kw["vmem_limit_bytes"] = int(min(VMEM_CAP, max(vmem, 32 * 1024 * 1024)))
    return pltpu.CompilerParams(**kw)


def _nbytes(shape, dtype):
    return math.prod(shape) * jnp.dtype(dtype).itemsize


def _sigmoid(x):
    return 1.0 / (1.0 + jnp.exp(-x))


def _silu(x):
    return x * _sigmoid(x)


def _dsilu(x):
    s = _sigmoid(x)
    return s * (1.0 + x * (1.0 - s))


def _gelu(x):
    return 0.5 * x * (1.0 + jnp.tanh(GELU_C * (x + GELU_K * x * x * x)))


def _dgelu(x):
    t = jnp.tanh(GELU_C * (x + GELU_K * x * x * x))
    return 0.5 * (1.0 + t) + 0.5 * x * (1.0 - t * t) * GELU_C * (1.0 + 3.0 * GELU_K * x * x)


def _dot(a, b, dims):
    return lax.dot_general(a, b, (dims, ((), ())), preferred_element_type=F32)


NN = ((1,), (0,))
NT = ((1,), (1,))
TN = ((0,), (0,))


def _any_spec():
    return pl.BlockSpec(memory_space=pl.ANY)


def _pallas(body, **kw):
    pin = lambda s: pltpu.HBM(s.shape, s.dtype) if isinstance(s, jax.ShapeDtypeStruct) else s
    out_shape = kw.pop("out_shape")
    out_shape = [pin(s) for s in out_shape] if isinstance(out_shape, (list, tuple)) else pin(out_shape)
    call = pl.pallas_call(body, out_shape=out_shape, **kw)

    def run(*operands):
        pinned = [pltpu.with_memory_space_constraint(o, pltpu.HBM) if jnp.issubdtype(o.dtype, jnp.floating) else o
                  for o in operands]
        return call(*pinned)

    return run


class Rider(NamedTuple):
    operands: tuple
    out_shapes: tuple
    sems: tuple
    start: Callable
    finish: Callable


def _all_gather(shards):
    n = len(shards)

    def copies(ins, outs, sems):
        send_sems, recv_sems, local_sems = sems
        x, y, c = lax.axis_index("x"), lax.axis_index("y"), lax.axis_index("c")
        me, sibling = (x, y, c), (x, y, 1 - c)
        chips = [(1 - x, y), (x, 1 - y), (1 - x, 1 - y)]

        def rows(k, px, py, pc):
            r = shards[k].shape[0]
            return outs[k].at[pl.ds((4 * px + 2 * py + pc) * r, r), :]

        def copy(k, s, block, to, src=None):
            return pltpu.make_async_remote_copy(
                src_ref=rows(k, *block) if src is None else src, dst_ref=rows(k, *block),
                send_sem=send_sems.at[7 * k + s], recv_sem=recv_sems.at[7 * k + s],
                device_id=to, device_id_type=MESH)

        mine = [pltpu.make_async_copy(ins[k], rows(k, *me), local_sems.at[k]) for k in range(n)]
        first = []
        for k in range(n):
            first.append(copy(k, 0, me, sibling, src=ins[k]))
            first += [copy(k, 1 + j, me, (*chip, c), src=ins[k]) for j, chip in enumerate(chips)]
        return me, sibling, chips, c, copy, mine, first

    def start(ins, outs, sems):
        *_, mine, first = copies(ins, outs, sems)
        for cp in mine + first:
            cp.start()

    def finish(ins, outs, sems):
        me, sibling, chips, c, copy, mine, first = copies(ins, outs, sems)
        passed = []
        for j, chip in enumerate(chips):
            for k in range(n):
                copy(k, 1 + j, (*chip, c), me).wait_recv()
                fwd = copy(k, 4 + j, (*chip, c), sibling)
                fwd.start()
                passed.append(fwd)
        for k in range(n):
            copy(k, 0, sibling, me).wait_recv()
            for j, chip in enumerate(chips):
                copy(k, 4 + j, (*chip, 1 - c), me).wait_recv()
        for cp in first + passed:
            cp.wait_send()
        for cp in mine:
            cp.wait()

    return Rider(
        tuple(shards),
        tuple(jax.ShapeDtypeStruct((N_DEV * s.shape[0], s.shape[1]), s.dtype) for s in shards),
        (pltpu.SemaphoreType.DMA((7 * n,)), pltpu.SemaphoreType.DMA((7 * n,)), pltpu.SemaphoreType.DMA((n,))),
        start, finish)


def _sibling_exchange(grads):
    n = len(grads)

    def copies(ins, outs, sems):
        send_sems, recv_sems = sems
        x, y, c = lax.axis_index("x"), lax.axis_index("y"), lax.axis_index("c")
        out = []
        for k in range(n):
            r = grads[k].shape[0] // N_DEV
            for j in range(N_CHIPS):
                out.append(pltpu.make_async_remote_copy(
                    src_ref=ins[k].at[pl.ds((2 * j + 1 - c) * r, r), :],
                    dst_ref=outs[k].at[pl.ds(j * r, r), :],
                    send_sem=send_sems.at[N_CHIPS * k + j], recv_sem=recv_sems.at[N_CHIPS * k + j],
                    device_id=(x, y, 1 - c), device_id_type=MESH))
        return out

    def start(ins, outs, sems):
        for cp in copies(ins, outs, sems):
            cp.start()

    def finish(ins, outs, sems):
        for cp in copies(ins, outs, sems):
            cp.wait()

    return Rider(
        tuple(grads), tuple(jax.ShapeDtypeStruct((g.shape[0] // 2, g.shape[1]), g.dtype) for g in grads),
        (pltpu.SemaphoreType.DMA((N_CHIPS * n,)), pltpu.SemaphoreType.DMA((N_CHIPS * n,))), start, finish)


def _chip_exchange(parts):
    n = len(parts)

    def copies(ins, outs, sems):
        send_sems, recv_sems, local_sems = sems
        x, y, c = lax.axis_index("x"), lax.axis_index("y"), lax.axis_index("c")
        my_chip = 2 * x + y
        chips = [(1 - x, y), (x, 1 - y), (1 - x, 1 - y)]
        local, sent = [], []
        for k in range(n):
            r = parts[k].shape[0] // N_CHIPS
            mine = pl.ds(my_chip * r, r)
            local.append(pltpu.make_async_copy(ins[k].at[mine, :], outs[k].at[mine, :], local_sems.at[k]))
            for s, (px, py) in enumerate(chips):
                sent.append(pltpu.make_async_remote_copy(
                    src_ref=ins[k].at[pl.ds((2 * px + py) * r, r), :], dst_ref=outs[k].at[mine, :],
                    send_sem=send_sems.at[3 * k + s], recv_sem=recv_sems.at[3 * k + s],
                    device_id=(px, py, c), device_id_type=MESH))
        return local, sent

    def start(ins, outs, sems):
        local, sent = copies(ins, outs, sems)
        for cp in local + sent:
            cp.start()

    def finish(ins, outs, sems):
        local, sent = copies(ins, outs, sems)
        for cp in sent + local:
            cp.wait()

    return Rider(
        tuple(parts), tuple(jax.ShapeDtypeStruct(p.shape, p.dtype) for p in parts),
        (pltpu.SemaphoreType.DMA((3 * n,)), pltpu.SemaphoreType.DMA((3 * n,)), pltpu.SemaphoreType.DMA((n,))),
        start, finish)


def _join(*riders):
    cuts_in, cuts_out, cuts_sem = [0], [0], [0]
    for r in riders:
        cuts_in.append(cuts_in[-1] + len(r.operands))
        cuts_out.append(cuts_out[-1] + len(r.out_shapes))
        cuts_sem.append(cuts_sem[-1] + len(r.sems))

    def each(which):
        def run(ins, outs, sems):
            for i, r in enumerate(riders):
                getattr(r, which)(ins[cuts_in[i]:cuts_in[i + 1]], outs[cuts_out[i]:cuts_out[i + 1]],
                                  sems[cuts_sem[i]:cuts_sem[i + 1]])
        return run

    return Rider(sum((r.operands for r in riders), ()), sum((r.out_shapes for r in riders), ()),
                 sum((r.sems for r in riders), ()), each("start"), each("finish"))


def _call(body, operands, *, name, out_shape, grid, in_specs, out_specs, scratch_shapes=(), aliases=None,
          semantics=None, vmem=None, rider=None):
    operands, out_shape, scratch_shapes = list(operands), list(out_shape), list(scratch_shapes)
    in_specs, out_specs = list(in_specs), list(out_specs)
    if rider is None:
        res = _pallas(
            body, name=name, out_shape=out_shape, grid=grid, in_specs=in_specs, out_specs=out_specs,
            scratch_shapes=scratch_shapes, input_output_aliases=aliases or {},
            compiler_params=_params(semantics, vmem))(*operands)
        return list(res), []
    n_in, n_out, n_scr = len(operands), len(out_shape), len(scratch_shapes)
    ri, ro = len(rider.operands), len(rider.out_shapes)

    def carried(*refs):
        a, b = n_in, n_in + ri
        c, d = b + n_out, b + n_out + ro
        e = d + n_scr
        ids = [pl.program_id(k) for k in range(len(grid))]
        first = ids[0] == 0
        last = ids[0] == grid[0] - 1
        for k in range(1, len(grid)):
            first = jnp.logical_and(first, ids[k] == 0)
            last = jnp.logical_and(last, ids[k] == grid[k] - 1)

        @pl.when(first)
        def _():
            rider.start(refs[a:b], refs[c:d], refs[e:])

        body(*refs[:a], *refs[b:c], *refs[d:e])

        @pl.when(last)
        def _():
            rider.finish(refs[a:b], refs[c:d], refs[e:])

    res = _pallas(
        carried, name=name, out_shape=out_shape + list(rider.out_shapes), grid=grid,
        in_specs=in_specs + [_any_spec()] * ri, out_specs=out_specs + [_any_spec()] * ro,
        scratch_shapes=scratch_shapes + list(rider.sems), input_output_aliases=aliases or {},
        compiler_params=_params(("arbitrary",) * len(grid), vmem))(*operands, *rider.operands)
    return list(res[:n_out]), list(res[n_out:])


def _exchange(rider, name):
    ri, ro = len(rider.operands), len(rider.out_shapes)

    def body(*refs):
        rider.start(refs[:ri], refs[ri:ri + ro], refs[ri + ro:])
        rider.finish(refs[:ri], refs[ri:ri + ro], refs[ri + ro:])

    return _pallas(
        body, name=name, out_shape=list(rider.out_shapes), in_specs=[_any_spec()] * ri,
        out_specs=[_any_spec()] * ro, scratch_shapes=list(rider.sems))(*rider.operands)


class Cols(NamedTuple):
    arr: jax.Array
    off: int
    width: int


def _cols(a):
    return a if isinstance(a, Cols) else Cols(a, 0, a.shape[1])


def _matmul(a, b, *, mode, name, out_dtype=F32, tm=1024, tn=1024, tk=2048, bias=None, add=None, out_cols=None,
            into=None, rider=None):
    a, b = _cols(a), _cols(b)
    if mode == "nn":
        (m, k), (k2, n) = (a.arr.shape[0], a.width), (b.arr.shape[0], b.width)
    elif mode == "nt":
        (m, k), (n, k2) = (a.arr.shape[0], a.width), (b.arr.shape[0], b.width)
    else:
        (k, m), (k2, n) = (a.arr.shape[0], a.width), (b.arr.shape[0], b.width)
    assert k == k2, (a.arr.shape, b.arr.shape, mode)
    tm, tn, tk = _tile(m, tm), _tile(n, tn), _tile(k, tk)
    nk = k // tk
    dims = {"nn": NN, "nt": NT, "tn": TN}[mode]
    if mode == "tn":
        assert a.off % tm == 0
        a_spec = pl.BlockSpec((tk, tm), lambda i, j, kk, o=a.off // tm: (kk, i + o))
    else:
        assert a.off % tk == 0
        a_spec = pl.BlockSpec((tm, tk), lambda i, j, kk, o=a.off // tk: (i, kk + o))
    if mode == "nt":
        assert b.off % tk == 0
        b_spec = pl.BlockSpec((tn, tk), lambda i, j, kk, o=b.off // tk: (j, kk + o))
    else:
        assert b.off % tn == 0
        b_spec = pl.BlockSpec((tk, tn), lambda i, j, kk, o=b.off // tn: (kk, j + o))
    in_specs, operands = [a_spec, b_spec], [a.arr, b.arr]
    assert bias is None or add is None
    if bias is not None:
        in_specs.append(pl.BlockSpec((1, tn), lambda i, j, kk: (0, j)))
        operands.append(bias)
    if add is not None:
        assert add.shape == (m, n)
        in_specs.append(pl.BlockSpec((tm, tn), lambda i, j, kk: (i, j)))
        operands.append(add)
    total_w, o_off = out_cols if out_cols is not None else (n, 0)
    assert o_off % tn == 0
    aliases = {}
    if into is not None:
        assert into.shape == (m, total_w) and into.dtype == out_dtype
        in_specs.append(_any_spec())
        operands.append(into)
        aliases = {len(operands) - 1: 0}
    n_in = len(operands)

    def body(*refs):
        a_ref, b_ref = refs[0], refs[1]
        bias_ref = refs[2] if bias is not None or add is not None else None
        o_ref = refs[n_in]
        acc_ref = refs[-1] if nk > 1 else None
        part = _dot(a_ref[...].astype(BF16), b_ref[...].astype(BF16), dims)

        def finish(acc):
            if bias_ref is not None:
                acc = acc + bias_ref[...]
            o_ref[...] = acc.astype(out_dtype)

        if nk == 1:
            finish(part)
        else:
            kk = pl.program_id(2)

            @pl.when(kk == 0)
            def _():
                acc_ref[...] = part

            @pl.when(kk > 0)
            def _():
                acc_ref[...] += part

            @pl.when(kk == nk - 1)
            def _():
                finish(acc_ref[...])

    vmem = 2 * (_nbytes((tm, tk), a.arr.dtype) + _nbytes((tk, tn), b.arr.dtype) + _nbytes((tm, tn), out_dtype))
    vmem += 3 * _nbytes((tm, tn), F32)
    (out,), landed = _call(
        body, operands, name=name, out_shape=[jax.ShapeDtypeStruct((m, total_w), out_dtype)],
        grid=(m // tm, n // tn, nk), in_specs=in_specs,
        out_specs=[pl.BlockSpec((tm, tn), lambda i, j, kk, o=o_off // tn: (i, j + o))],
        scratch_shapes=[pltpu.VMEM((tm, tn), F32)] if nk > 1 else [], aliases=aliases,
        semantics=("parallel", "parallel", "arbitrary"), vmem=vmem, rider=rider)
    return out if rider is None else (out, landed)


def _ew(fn, *, name, rows, width, tiles, vecs=(), outs, accs=0, tl=1024, cw=512, into=None, with_col=False):
    tl, cw = _tile(rows, tl, SUBLANES), _tile(width, cw)
    ncol = width // cw
    nt_, nv = len(tiles), len(vecs)
    into = list(into) if into is not None else [None] * len(outs)
    aliased = [t for t in into if t is not None]

    def off(o):
        assert o % cw == 0, (name, o, cw)
        return o // cw

    in_specs, vmem = [], 0
    for t in tiles:
        arr, o = t[0], off(t[1])
        wrap = t[2] // cw if len(t) > 2 else ncol
        in_specs.append(pl.BlockSpec((tl, cw), lambda j, i, o=o, wrap=wrap: (i, o + j % wrap)))
        vmem += _nbytes((tl, cw), arr.dtype)
    in_specs += [pl.BlockSpec((1, cw), lambda j, i, o=off(o): (0, j + o)) for _, o in vecs]
    in_specs += [_any_spec() for _ in aliased]
    out_shape, out_specs, aliases = [], [], {}
    n_in = nt_ + nv
    for idx, ((dt, tw, o), tgt) in enumerate(zip(outs, into)):
        out_shape.append(jax.ShapeDtypeStruct((rows, tw), dt))
        out_specs.append(pl.BlockSpec((tl, cw), lambda j, i, o=off(o): (i, j + o)))
        vmem += _nbytes((tl, cw), dt)
        if tgt is not None:
            assert tgt.shape == (rows, tw) and tgt.dtype == dt, (name, tgt.shape, tgt.dtype)
            aliases[n_in + len(aliases)] = idx
    for _ in range(accs):
        out_shape.append(jax.ShapeDtypeStruct((1, width), F32))
        out_specs.append(pl.BlockSpec((1, cw), lambda j, i: (0, j)))
    n_out = len(outs)

    def body(*refs):
        vals = [r[...].astype(F32) for r in refs[:n_in]]
        out_refs = refs[n_in + len(aliased):]
        res = fn(pl.program_id(0), *vals) if with_col else fn(*vals)
        res = res if isinstance(res, (tuple, list)) else (res,)
        assert len(res) == n_out + accs, (name, len(res))
        for r, v in zip(out_refs[:n_out], res[:n_out]):
            r[...] = v.astype(r.dtype)
        first = pl.program_id(1) == 0
        for r, v in zip(out_refs[n_out:], res[n_out:]):
            s = jnp.sum(v, axis=0, keepdims=True)

            @pl.when(first)
            def _(r=r, s=s):
                r[...] = s

            @pl.when(jnp.logical_not(first))
            def _(r=r, s=s):
                r[...] += s

    return _pallas(
        body, name=name, out_shape=out_shape, grid=(ncol, rows // tl),
        in_specs=in_specs, out_specs=out_specs, input_output_aliases=aliases,
        compiler_params=_params(("parallel", "arbitrary"), 3 * vmem),
    )(*[t[0] for t in tiles], *[v for v, _ in vecs], *aliased)


def _rmsnorm_fwd(x, w_row, name):
    rows, d = x.shape
    tl = _tile(rows, 512, SUBLANES)

    def body(x_ref, w_ref, h_ref):
        xv = x_ref[...]
        rstd = lax.rsqrt(jnp.mean(xv * xv, axis=-1, keepdims=True) + NORM_EPS)
        h_ref[...] = (xv * rstd * w_ref[...]).astype(BF16)

    return _pallas(
        body, name=name, out_shape=jax.ShapeDtypeStruct((rows, d), BF16), grid=(rows // tl,),
        in_specs=[pl.BlockSpec((tl, d), lambda i: (i, 0)), pl.BlockSpec((1, d), lambda i: (0, 0))],
        out_specs=pl.BlockSpec((tl, d), lambda i: (i, 0)),
        compiler_params=_params(("parallel",)),
    )(x, w_row)


def _rmsnorm_bwd(x, w_row, dh, dout, name, rider=None):
    rows, d = x.shape
    tl = _tile(rows, 256, SUBLANES)

    def body(x_ref, w_ref, dh_ref, dout_ref, gx_ref, gw_ref):
        xv = x_ref[...]
        rstd = lax.rsqrt(jnp.mean(xv * xv, axis=-1, keepdims=True) + NORM_EPS)
        xn = xv * rstd
        dhv = dh_ref[...]
        dxn = dhv * w_ref[...]
        dx = rstd * (dxn - xn * jnp.mean(dxn * xn, axis=-1, keepdims=True))
        gx_ref[...] = dout_ref[...] + dx
        gw = jnp.sum(dhv * xn, axis=0, keepdims=True)

        @pl.when(pl.program_id(0) == 0)
        def _():
            gw_ref[...] = gw

        @pl.when(pl.program_id(0) > 0)
        def _():
            gw_ref[...] += gw

    tile = pl.BlockSpec((tl, d), lambda i: (i, 0))
    row = pl.BlockSpec((1, d), lambda i: (0, 0))
    res, landed = _call(
        body, [x, w_row, dh, dout], name=name,
        out_shape=[jax.ShapeDtypeStruct((rows, d), F32), jax.ShapeDtypeStruct((1, d), F32)],
        grid=(rows // tl,), in_specs=[tile, row, tile, tile], out_specs=[tile, row],
        semantics=("arbitrary",), rider=rider)
    return res if rider is None else (res, landed)


def _head_mean(x, gmat):
    hi = x.astype(BF16)
    lo = (x - hi.astype(F32)).astype(BF16)
    out = []
    for s in range(x.shape[1] // MXU_DIM):
        sl = slice(s * MXU_DIM, (s + 1) * MXU_DIM)
        out.append(_dot(hi[:, sl], gmat, NN) + _dot(lo[:, sl], gmat, NN))
    return out[0] if len(out) == 1 else jnp.concatenate(out, axis=1)


def _head_mean_matrix():
    blk = jnp.arange(MXU_DIM) // HEAD_DIM
    return jnp.where(blk[:, None] == blk[None, :], 1.0 / HEAD_DIM, 0.0).astype(BF16)


def _spread_head(x, g, width):
    col = x[:, (g // 2) * LANES:(g // 2 + 1) * LANES]
    other = pltpu.roll(col, HEAD_DIM, axis=1)
    low = lax.broadcasted_iota(jnp.int32, col.shape, 1) < HEAD_DIM
    both = jnp.where(low, col, other) if g % 2 == 0 else jnp.where(low, other, col)
    return both if width == LANES else jnp.concatenate([both] * (width // LANES), axis=1)


def _head_diagonal(t, per_kv):
    head = lax.broadcasted_iota(jnp.int32, t.shape, 1) // HEAD_DIM
    zero = jnp.zeros_like(t)
    return jnp.concatenate([jnp.where(head == r, t, zero) for r in range(per_kv)], axis=0)


def _fold_heads(x, per_kv):
    rows = x.shape[0] // per_kv
    head = lax.broadcasted_iota(jnp.int32, (rows, x.shape[1]), 1) // HEAD_DIM
    acc = jnp.where(head == 0, x[0:rows], 0.0)
    for r in range(1, per_kv):
        acc = acc + jnp.where(head == r, x[r * rows:(r + 1) * rows], 0.0)
    while acc.shape[1] > LANES:
        half = acc.shape[1] // 2
        acc = acc[:, :half] + acc[:, half:]
    return acc + pltpu.roll(acc, HEAD_DIM, axis=1)


def _join_heads(parts):
    low = lax.broadcasted_iota(jnp.int32, parts[0].shape, 1) < HEAD_DIM
    cols = [jnp.where(low, parts[2 * j], parts[2 * j + 1]) for j in range(len(parts) // 2)]
    return cols[0] if len(cols) == 1 else jnp.concatenate(cols, axis=1)


def _attn_specs(attn_w, kv_w):
    half = attn_w // 2
    kcol, vcol = attn_w // kv_w, attn_w // kv_w + 1
    gcol = (attn_w + 2 * kv_w) // half
    prev = lambda i: jnp.maximum(i - 1, 0)
    return [
        pl.BlockSpec((BLOCK, attn_w), lambda i: (i, 0)),
        pl.BlockSpec((BLOCK, kv_w), lambda i: (prev(i), kcol)),
        pl.BlockSpec((BLOCK, kv_w), lambda i: (i, kcol)),
        pl.BlockSpec((BLOCK, kv_w), lambda i: (prev(i), vcol)),
        pl.BlockSpec((BLOCK, kv_w), lambda i: (i, vcol)),
        pl.BlockSpec((BLOCK, half), lambda i: (i, gcol)),
        pl.BlockSpec((BLOCK, half), lambda i: (i, gcol + 1)),
    ]


def _band_mask(i):
    q_loc = lax.broadcasted_iota(jnp.int32, (BLOCK, 2 * BLOCK), 0) + BLOCK
    k_loc = lax.broadcasted_iota(jnp.int32, (BLOCK, 2 * BLOCK), 1)
    diff = q_loc - k_loc
    first_key = jnp.where(i == 0, BLOCK, 0)
    return (diff >= 0) & (diff < BLOCK) & (k_loc >= first_key)


def _softmax_with_sink(s, sink):
    m = jnp.maximum(jnp.max(s, axis=-1, keepdims=True), sink)
    p = jnp.exp(s - m)
    e_sink = jnp.exp(sink - m)
    den = jnp.sum(p, axis=-1, keepdims=True) + e_sink
    inv = 1.0 / den
    return p * inv, e_sink * inv


def _attn_block(i, q, kk, vv, qw, kw, gmat, sink_ref, per_kv):
    scale = 1.0 / math.sqrt(HEAD_DIM)
    keys = 2 * BLOCK
    valid = _band_mask(i)
    q_rstd = lax.rsqrt(_head_mean(q * q, gmat) + NORM_EPS)
    qn = q * q_rstd
    qh = (qn * qw).astype(BF16)
    k_rstd = lax.rsqrt(_head_mean(kk * kk, gmat) + NORM_EPS)
    kn = kk * k_rstd
    kh = kn * kw
    gw = per_kv * HEAD_DIM
    groups = []
    for g in range(N_KV_HEADS):
        kd = _head_diagonal(_spread_head(kh, g, gw).astype(BF16), per_kv)
        vd = _head_diagonal(_spread_head(vv, g, gw).astype(BF16), per_kv)
        qg = qh[:, g * gw:(g + 1) * gw]
        s_all = _dot(qg, kd, NT) * scale
        ps, p_sinks = [], []
        for r in range(per_kv):
            s = jnp.where(valid, s_all[:, r * keys:(r + 1) * keys], -1e30)
            p, p_sink = _softmax_with_sink(s, sink_ref[g * per_kv + r])
            ps.append(p)
            p_sinks.append(p_sink)
        pb = jnp.concatenate(ps, axis=1).astype(BF16)
        groups.append((kd, vd, qg, ps, p_sinks, pb, _dot(pb, vd, NN)))
    return qn, q_rstd, kn, k_rstd, groups


def _attention_fwd(proj, qw_row, kw_row, gmat, sinks, *, attn_w, kv_w, name):
    rows = proj.shape[0]
    per_kv = attn_w // HEAD_DIM // N_KV_HEADS

    def body(q_ref, kp_ref, kc_ref, vp_ref, vc_ref, glo_ref, ghi_ref, qw_ref, kw_ref, gm_ref, sink_ref, o_ref):
        kk = jnp.concatenate([kp_ref[...], kc_ref[...]], axis=0).astype(F32)
        vv = jnp.concatenate([vp_ref[...], vc_ref[...]], axis=0).astype(F32)
        gate = jnp.concatenate([glo_ref[...], ghi_ref[...]], axis=1).astype(F32)
        *_, groups = _attn_block(pl.program_id(0), q_ref[...].astype(F32), kk, vv, qw_ref[...], kw_ref[...], gm_ref[...],
                                 sink_ref, per_kv)
        attn = jnp.concatenate([grp[-1] for grp in groups], axis=1)
        o_ref[...] = (attn * _silu(gate)).astype(BF16)

    const = lambda a: pl.BlockSpec(a.shape, lambda i: (0, 0))
    return _pallas(
        body, name=name, out_shape=jax.ShapeDtypeStruct((rows, attn_w), BF16), grid=(rows // BLOCK,),
        in_specs=_attn_specs(attn_w, kv_w) + [const(qw_row), const(kw_row), const(gmat),
                                              pl.BlockSpec(memory_space=pltpu.SMEM)],
        out_specs=pl.BlockSpec((BLOCK, attn_w), lambda i: (i, 0)),
        compiler_params=_params(("parallel",), 40 * 1024 * 1024),
    )(proj, proj, proj, proj, proj, proj, proj, qw_row, kw_row, gmat, sinks)


def _attention_bwd(proj, d_ag, dproj, qw_row, kw_row, gmat, sinks, *, attn_w, kv_w, name, rider=None):
    rows = proj.shape[0]
    nb = rows // BLOCK
    per_kv = attn_w // HEAD_DIM // N_KV_HEADS
    gw = per_kv * HEAD_DIM
    keys = 2 * BLOCK
    scale = 1.0 / math.sqrt(HEAD_DIM)
    w_out = 2 * attn_w + 2 * kv_w

    def body(q_ref, kp_ref, kc_ref, vp_ref, vc_ref, glo_ref, ghi_ref, dag_ref, qw_ref, kw_ref, gm_ref, sink_ref, _,
             dp_ref, dkv_ref, gqw_ref, gkw_ref, gs_ref):
        i = pl.program_id(0)
        kk = jnp.concatenate([kp_ref[...], kc_ref[...]], axis=0).astype(F32)
        vv = jnp.concatenate([vp_ref[...], vc_ref[...]], axis=0).astype(F32)
        gate = jnp.concatenate([glo_ref[...], ghi_ref[...]], axis=1).astype(F32)
        d_ag_v = dag_ref[...].astype(F32)
        qw, kw, gmat_v = qw_ref[...], kw_ref[...], gm_ref[...]
        qn, q_rstd, kn, k_rstd, groups = _attn_block(i, q_ref[...].astype(F32), kk, vv, qw, kw, gmat_v, sink_ref,
                                                     per_kv)
        lane = lax.broadcasted_iota(jnp.int32, (SUBLANES, LANES), 1)
        sub = lax.broadcasted_iota(jnp.int32, (SUBLANES, LANES), 0)
        gsink = jnp.zeros((SUBLANES, LANES), F32)
        dq_groups, dgate_groups, dk_heads, dv_heads = [], [], [], []
        for g, (kd, vd, qg, ps, p_sinks, pb, o) in enumerate(groups):
            cs = slice(g * gw, (g + 1) * gw)
            gate_g, d_ag_g = gate[:, cs], d_ag_v[:, cs]
            dgate_groups.append(d_ag_g * o * _dsilu(gate_g))
            do = (d_ag_g * _silu(gate_g)).astype(BF16)
            dp_all = _dot(do, vd, NT)
            dss = []
            for r in range(per_kv):
                p, dp = ps[r], dp_all[:, r * keys:(r + 1) * keys]
                delta = jnp.sum(p * dp, axis=-1, keepdims=True)
                dss.append(p * (dp - delta) * scale)
                gs_h = jnp.sum(-p_sinks[r] * delta, axis=0, keepdims=True)
                gsink = gsink + jnp.where((lane == g * per_kv + r) & (sub == 0), gs_h, 0.0)
            ds = jnp.concatenate(dss, axis=1).astype(BF16)
            dq_groups.append(_dot(ds, kd, NN))
            dk_heads.append(_fold_heads(_dot(ds, qg, TN), per_kv))
            dv_heads.append(_fold_heads(_dot(pb, do, TN), per_kv))
        dqh = jnp.concatenate(dq_groups, axis=1)
        gqw = jnp.sum(dqh * qn, axis=0, keepdims=True)
        dqn = dqh * qw
        dq = q_rstd * (dqn - qn * _head_mean(dqn * qn, gmat_v))
        dkh = _join_heads(dk_heads)
        gkw = jnp.sum(dkh * kn, axis=0, keepdims=True)
        dkn = dkh * kw
        dk = k_rstd * (dkn - kn * _head_mean(dkn * kn, gmat_v))
        dp_ref[:, 0:attn_w] = dq.astype(BF16)
        dp_ref[:, attn_w:attn_w + 2 * kv_w] = jnp.zeros((BLOCK, 2 * kv_w), BF16)
        dp_ref[:, attn_w + 2 * kv_w:w_out] = jnp.concatenate(dgate_groups, axis=1).astype(BF16)
        dkv_ref[0] = jnp.concatenate([dk, _join_heads(dv_heads)], axis=1)

        @pl.when(i == 0)
        def _():
            gqw_ref[...] = gqw
            gkw_ref[...] = gkw
            gs_ref[...] = gsink

        @pl.when(i > 0)
        def _():
            gqw_ref[...] += gqw
            gkw_ref[...] += gkw
            gs_ref[...] += gsink

    const = lambda a: pl.BlockSpec(a.shape, lambda i: (0, 0))
    res, landed = _call(
        body, [proj, proj, proj, proj, proj, proj, proj, d_ag, qw_row, kw_row, gmat, sinks, dproj], name=name,
        out_shape=[jax.ShapeDtypeStruct(dproj.shape, BF16),
                   jax.ShapeDtypeStruct((nb, 2 * BLOCK, 2 * kv_w), F32),
                   jax.ShapeDtypeStruct(qw_row.shape, F32), jax.ShapeDtypeStruct(kw_row.shape, F32),
                   jax.ShapeDtypeStruct((SUBLANES, LANES), F32)],
        grid=(nb,),
        in_specs=_attn_specs(attn_w, kv_w) + [pl.BlockSpec((BLOCK, attn_w), lambda i: (i, 0)), const(qw_row),
                                              const(kw_row), const(gmat), pl.BlockSpec(memory_space=pltpu.SMEM),
                                              _any_spec()],
        out_specs=[pl.BlockSpec((BLOCK, w_out), lambda i: (i, 0)),
                   pl.BlockSpec((1, 2 * BLOCK, 2 * kv_w), lambda i: (i, 0, 0)),
                   const(qw_row), const(kw_row), pl.BlockSpec((SUBLANES, LANES), lambda i: (0, 0))],
        aliases={12: 0}, semantics=("arbitrary",), vmem=48 * 1024 * 1024, rider=rider)
    return res if rider is None else (res, landed)


def _attention_dkv(dproj, dkv, *, attn_w, kv_w, name):
    rows = dproj.shape[0]
    nb = rows // BLOCK
    col = attn_w // (2 * kv_w)

    def body(cur_ref, nxt_ref, _, o_ref):
        i = pl.program_id(0)
        nxt = jnp.where(i < nb - 1, nxt_ref[0, 0:BLOCK, :], 0.0)
        o_ref[...] = (cur_ref[0, BLOCK:2 * BLOCK, :] + nxt).astype(BF16)

    blk = lambda f: pl.BlockSpec((1, 2 * BLOCK, 2 * kv_w), f)
    return _pallas(
        body, name=name, out_shape=jax.ShapeDtypeStruct(dproj.shape, BF16), grid=(nb,),
        in_specs=[blk(lambda i: (i, 0, 0)), blk(lambda i: (jnp.minimum(i + 1, nb - 1), 0, 0)), _any_spec()],
        out_specs=pl.BlockSpec((BLOCK, 2 * kv_w), lambda i: (i, col)),
        input_output_aliases={2: 0},
        compiler_params=_params(("parallel",)),
    )(dkv, dkv, dproj)


def _cmul(ar, ai, br, bi):
    return ar * br - ai * bi, ar * bi + ai * br


def _ssm_prep(a_re, a_im, log_dt_col, steps, name):
    assert steps & (steps - 1) == 0

    def body(are_ref, aim_ref, ldt_ref, abr_ref, abi_ref, cfr_ref, cfi_ref, apr_ref, api_ref):
        are, aim = are_ref[...], aim_ref[...]
        dt = jnp.exp(ldt_ref[...])
        mag = jnp.exp(dt * are)
        abr = mag * jnp.cos(dt * aim)
        abi = mag * jnp.sin(dt * aim)
        num_re, num_im = abr - 1.0, abi
        den = are * are + aim * aim
        abr_ref[...] = abr
        abi_ref[...] = abi
        cfr_ref[...] = (num_re * are + num_im * aim) / den
        cfi_ref[...] = (num_im * are - num_re * aim) / den
        pr, pi = abr, abi
        n = steps
        while n > 1:
            pr, pi = _cmul(pr, pi, pr, pi)
            n //= 2
        apr_ref[...] = pr
        api_ref[...] = pi

    shp = jax.ShapeDtypeStruct(a_re.shape, F32)
    return _pallas(body, name=name, out_shape=[shp] * 6)(a_re, a_im, log_dt_col)


def _ssm_param_bwd(a_re, a_im, log_dt_col, d_ab_re, d_ab_im, d_cf_re, d_cf_im, name):
    def body(are_ref, aim_ref, ldt_ref, gabr_ref, gabi_ref, gcfr_ref, gcfi_ref, dar_ref, dai_ref, dldt_ref):
        are, aim = are_ref[...], aim_ref[...]
        dt = jnp.exp(ldt_ref[...])
        mag = jnp.exp(dt * are)
        abr = mag * jnp.cos(dt * aim)
        abi = mag * jnp.sin(dt * aim)
        den = are * are + aim * aim
        cfr = ((abr - 1.0) * are + abi * aim) / den
        cfi = (abi * are - (abr - 1.0) * aim) / den
        gabr, gabi = jnp.sum(gabr_ref[...], axis=0), jnp.sum(gabi_ref[...], axis=0)
        gcfr, gcfi = jnp.sum(gcfr_ref[...], axis=0), jnp.sum(gcfi_ref[...], axis=0)
        inv_r, inv_i = are / den, -aim / den
        t_r, t_i = _cmul(inv_r, -inv_i, gcfr, gcfi)
        gabr, gabi = gabr + t_r, gabi + t_i
        q_r, q_i = _cmul(cfr, cfi, inv_r, inv_i)
        da_r, da_i = _cmul(-q_r, q_i, gcfr, gcfi)
        gz_r, gz_i = _cmul(abr, -abi, gabr, gabi)
        dar_ref[...] = da_r + dt * gz_r
        dai_ref[...] = da_i + dt * gz_i
        dldt_ref[...] = dt * jnp.sum(are * gz_r + aim * gz_i, axis=-1, keepdims=True)

    shp = jax.ShapeDtypeStruct(a_re.shape, F32)
    return _pallas(body, name=name, out_shape=[shp, shp, jax.ShapeDtypeStruct(log_dt_col.shape, F32)])(
        a_re, a_im, log_dt_col, d_ab_re, d_ab_im, d_cf_re, d_cf_im)


SCAN_LANES = 512
W_IN_PARTS = 2


def _scan_segments(xr_ref, xi_ref, a_re, a_im, ap_re, ap_im, carry_re, carry_im, cm_re, cm_im, steps, reverse):
    n = xr_ref.shape[1]
    order = range(steps - 1, -1, -1) if reverse else range(steps)
    seg_order = range(SUBLANES - 1, -1, -1) if reverse else range(SUBLANES)
    for c0 in range(0, n, SCAN_LANES):
        ls = slice(c0, c0 + SCAN_LANES)
        ar = jnp.broadcast_to(a_re[:, ls], (SUBLANES, SCAN_LANES))
        ai = jnp.broadcast_to(a_im[:, ls], (SUBLANES, SCAN_LANES))

        def local(t, s, ar=ar, ai=ai, ls=ls):
            j = steps - 1 - t if reverse else t
            r0 = pl.multiple_of(j * SUBLANES, SUBLANES)
            sr, si = _cmul(ar, ai, s[0], s[1])
            sr = sr + xr_ref[pl.ds(r0, SUBLANES), ls]
            si = si + xi_ref[pl.ds(r0, SUBLANES), ls]
            xr_ref[pl.ds(r0, SUBLANES), ls] = sr
            xi_ref[pl.ds(r0, SUBLANES), ls] = si
            return sr, si

        zero = jnp.zeros((SUBLANES, SCAN_LANES), F32)
        end_r, end_i = lax.fori_loop(0, steps, local, (zero, zero))
        cr, ci = carry_re[:, ls], carry_im[:, ls]
        apr, api = ap_re[:, ls], ap_im[:, ls]
        for r in seg_order:
            cm_re[r:r + 1, ls] = cr
            cm_im[r:r + 1, ls] = ci
            tr, ti = _cmul(apr, api, cr, ci)
            cr, ci = end_r[r:r + 1, :] + tr, end_i[r:r + 1, :] + ti
        carry_re[:, ls] = cr
        carry_im[:, ls] = ci

        def fix(t, s, ar=ar, ai=ai, ls=ls):
            j = steps - 1 - t if reverse else t
            r0 = pl.multiple_of(j * SUBLANES, SUBLANES)
            sr, si = _cmul(ar, ai, s[0], s[1])
            xr_ref[pl.ds(r0, SUBLANES), ls] += sr
            xi_ref[pl.ds(r0, SUBLANES), ls] += si
            return sr, si

        lax.fori_loop(0, steps, fix, (cm_re[:, ls], cm_im[:, ls]))
    del order


SB_GROUPS = MXU_DIM // GROUP
SB_STATE = SB_GROUPS * STATE


def _ssm_rows(b_re, b_im, c_re, c_im):
    def rows(m):
        flat = m.reshape(-1, STATE).astype(F32)
        return jnp.concatenate([flat, flat], axis=1)
    return rows(b_re.transpose(0, 2, 1)), rows(b_im.transpose(0, 2, 1)), rows(c_re), rows(c_im)


def _from_ssm_rows(rows, transpose):
    g = rows[:, :STATE].reshape(-1, GROUP, STATE)
    return g.transpose(0, 2, 1) if transpose else g


def _own_group(shape):
    row_g = lax.broadcasted_iota(jnp.int32, shape, 0) // GROUP
    col_g = lax.broadcasted_iota(jnp.int32, shape, 1) // STATE
    return row_g == col_g


def _block_diagonal(rows):
    tiled = jnp.concatenate([rows] * (SB_STATE // LANES), axis=1)
    return jnp.where(_own_group(tiled.shape), tiled, 0.0).astype(BF16)


def _block_rows(acc):
    x = jnp.where(_own_group(acc.shape), acc, 0.0)
    while x.shape[1] > LANES:
        half = x.shape[1] // 2
        x = x[:, :half] + x[:, half:]
    return x + pltpu.roll(x, STATE, axis=1)


def _rows_to_segments(dst, srcs, steps, stage):
    for ref, off in srcs:
        for k in range(ref.shape[1] // LANES):
            stage[off // LANES + k] = ref[:, k * LANES:(k + 1) * LANES].astype(F32)
    for k in range(dst.shape[1] // LANES):
        for j in range(steps):
            dst[j * SUBLANES:(j + 1) * SUBLANES, k * LANES:(k + 1) * LANES] = (
                stage[k, pl.ds(j, SUBLANES, stride=steps), :])


def _segments_to_rows(dst, src, steps, stage):
    for k in range(src.shape[1] // LANES):
        for j in range(steps):
            stage[k, pl.ds(j, SUBLANES, stride=steps), :] = (
                src[j * SUBLANES:(j + 1) * SUBLANES, k * LANES:(k + 1) * LANES])
    for k in range(src.shape[1] // LANES):
        dst[:, k * LANES:(k + 1) * LANES] = stage[k]


def _u_specs(w, o_u, chunk, index):
    half = w // 2
    assert o_u % half == 0
    return [pl.BlockSpec((chunk, half), lambda c, k=k: (index(c), o_u // half + k)) for k in range(2)]


def _ssm_fwd(proj, o_u, bc_rows, rows_p, d_row, *, chunk, name, rider=None):
    rows = proj.shape[0]
    w = d_row.shape[1]
    nc = rows // chunk
    steps = chunk // SUBLANES
    nsb = w // MXU_DIM
    n_state = nsb * SB_STATE

    def body(ulo_ref, uhi_ref, b2r_ref, b2i_ref, c2r_ref, c2i_ref, abr_ref, abi_ref, cfr_ref, cfi_ref, apr_ref,
             api_ref, d_ref, y_ref, str_ref, sti_ref, yg_ref, bre_ref, bim_ref, cre_ref, cim_ref, useg, yseg, stage, sr, si,
             carry_r, carry_i, cm_r, cm_i):
        @pl.when(pl.program_id(0) == 0)
        def _():
            for src, dst in ((b2r_ref, bre_ref), (b2i_ref, bim_ref), (c2r_ref, cre_ref), (c2i_ref, cim_ref)):
                for sb in range(nsb):
                    dst[sb] = _block_diagonal(src[sb * MXU_DIM:(sb + 1) * MXU_DIM, :])
            carry_r[...] = jnp.zeros_like(carry_r)
            carry_i[...] = jnp.zeros_like(carry_i)

        str_ref[0] = carry_r[...]
        sti_ref[0] = carry_i[...]
        _rows_to_segments(useg, [(ulo_ref, 0), (uhi_ref, w // 2)], steps, stage)
        for sb in range(nsb):
            us = slice(sb * MXU_DIM, (sb + 1) * MXU_DIM)
            ss = slice(sb * SB_STATE, (sb + 1) * SB_STATE)
            ub = useg[:, us].astype(BF16)
            bur = _dot(ub, bre_ref[sb], NN)
            bui = _dot(ub, bim_ref[sb], NN)
            xr, xi = _cmul(cfr_ref[:, ss], cfi_ref[:, ss], bur, bui)
            sr[:, ss] = xr
            si[:, ss] = xi
        _scan_segments(sr, si, abr_ref[...], abi_ref[...], apr_ref[...], api_ref[...],
                       carry_r, carry_i, cm_r, cm_i, steps, False)
        for sb in range(nsb):
            us = slice(sb * MXU_DIM, (sb + 1) * MXU_DIM)
            ss = slice(sb * SB_STATE, (sb + 1) * SB_STATE)
            y = _dot(sr[:, ss].astype(BF16), cre_ref[sb], NT) - _dot(si[:, ss].astype(BF16), cim_ref[sb], NT)
            yseg[:, us] = y + d_ref[:, us] * useg[:, us]
        _segments_to_rows(y_ref, yseg, steps, stage)
        yg_ref[...] = _gelu(y_ref[...]).astype(BF16)

    const = lambda a: pl.BlockSpec(a.shape, lambda c: (0,) * a.ndim)
    row_n = pl.BlockSpec((1, n_state), lambda c: (0, 0))
    st = pl.BlockSpec((1, 1, n_state), lambda c: (c, 0, 0))
    held = [pltpu.VMEM((nsb, MXU_DIM, SB_STATE), BF16)] * 4
    vmem = 4 * _nbytes((nsb, MXU_DIM, SB_STATE), BF16) + 3 * _nbytes((chunk, n_state), F32)
    res, landed = _call(
        body, [proj, proj, *bc_rows, *rows_p, d_row], name=name,
        out_shape=[jax.ShapeDtypeStruct((rows, w), F32), jax.ShapeDtypeStruct((nc, 1, n_state), F32),
                   jax.ShapeDtypeStruct((nc, 1, n_state), F32), jax.ShapeDtypeStruct((rows, w), BF16)],
        grid=(nc,),
        in_specs=_u_specs(w, o_u, chunk, lambda c: c) + [const(b) for b in bc_rows]
        + [row_n] * 6 + [pl.BlockSpec((1, w), lambda c: (0, 0))],
        out_specs=[pl.BlockSpec((chunk, w), lambda c: (c, 0)), st, st, pl.BlockSpec((chunk, w), lambda c: (c, 0))],
        scratch_shapes=held + [pltpu.VMEM((chunk, w), F32), pltpu.VMEM((chunk, w), F32),
                               pltpu.VMEM((w // LANES, chunk, LANES), F32),
                               pltpu.VMEM((chunk, n_state), F32), pltpu.VMEM((chunk, n_state), F32),
                               pltpu.VMEM((1, n_state), F32), pltpu.VMEM((1, n_state), F32),
                               pltpu.VMEM((SUBLANES, n_state), F32), pltpu.VMEM((SUBLANES, n_state), F32)],
        semantics=("arbitrary",), vmem=vmem, rider=rider)
    return res if rider is None else (res, landed)


def _ssm_bwd(proj, o_u, y, dyg, st_re, st_im, bc_rows, rows_p, d_row, *, chunk, name, rider=None):
    rows = proj.shape[0]
    w = d_row.shape[1]
    nc = rows // chunk
    steps = chunk // SUBLANES
    nsb = w // MXU_DIM
    n_state = nsb * SB_STATE

    def body(ulo_ref, uhi_ref, y_ref, dyg_ref, str_ref, sti_ref, b2r_ref, b2i_ref, c2r_ref, c2i_ref,
             abr_ref, abi_ref, cfr_ref, cfi_ref, apr_ref, api_ref, d_ref,
             du_ref, gb2r_ref, gb2i_ref, gc2r_ref, gc2i_ref, gabr_ref, gabi_ref, gcfr_ref, gcfi_ref, dd_ref,
             bre_ref, bim_ref, cre_ref, cim_ref, dbre_ref, dbim_ref, dcre_ref, dcim_ref, useg, dyseg, dynat, stage,
             bur, bui, sr, si, lr, li, carry_r, carry_i, lam_r, lam_i, cm_r, cm_i, cl_r, cl_i):
        first = pl.program_id(0) == 0

        @pl.when(first)
        def _():
            for src, dst in ((b2r_ref, bre_ref), (b2i_ref, bim_ref), (c2r_ref, cre_ref), (c2i_ref, cim_ref)):
                for sb in range(nsb):
                    dst[sb] = _block_diagonal(src[sb * MXU_DIM:(sb + 1) * MXU_DIM, :])
            lam_r[...] = jnp.zeros_like(lam_r)
            lam_i[...] = jnp.zeros_like(lam_i)
            for ref in (dbre_ref, dbim_ref, dcre_ref, dcim_ref, gabr_ref, gabi_ref, gcfr_ref, gcfi_ref, dd_ref):
                ref[...] = jnp.zeros_like(ref)

        dynat[...] = dyg_ref[...].astype(F32) * _dgelu(y_ref[...])
        half = w // 2
        dd_ref[:, :half] += jnp.sum(dynat[:, :half] * ulo_ref[...].astype(F32), axis=0, keepdims=True)
        dd_ref[:, half:] += jnp.sum(dynat[:, half:] * uhi_ref[...].astype(F32), axis=0, keepdims=True)
        _rows_to_segments(useg, [(ulo_ref, 0), (uhi_ref, half)], steps, stage)
        _rows_to_segments(dyseg, [(dynat, 0)], steps, stage)
        dy = dyseg[...]
        dyb = dy.astype(BF16)
        ub = useg[...].astype(BF16)
        carry_r[...] = str_ref[0]
        carry_i[...] = sti_ref[0]
        for sb in range(nsb):
            us = slice(sb * MXU_DIM, (sb + 1) * MXU_DIM)
            ss = slice(sb * SB_STATE, (sb + 1) * SB_STATE)
            br = _dot(ub[:, us], bre_ref[sb], NN)
            bi = _dot(ub[:, us], bim_ref[sb], NN)
            bur[:, ss] = br
            bui[:, ss] = bi
            xr, xi = _cmul(cfr_ref[:, ss], cfi_ref[:, ss], br, bi)
            sr[:, ss] = xr
            si[:, ss] = xi
            lr[:, ss] = _dot(dyb[:, us], cre_ref[sb], NN)
            li[:, ss] = -_dot(dyb[:, us], cim_ref[sb], NN)
        abr, abi = abr_ref[...], abi_ref[...]
        apr, api = apr_ref[...], api_ref[...]
        _scan_segments(sr, si, abr, abi, apr, api, carry_r, carry_i, cm_r, cm_i, steps, False)
        for sb in range(nsb):
            us = slice(sb * MXU_DIM, (sb + 1) * MXU_DIM)
            ss = slice(sb * SB_STATE, (sb + 1) * SB_STATE)
            dcre_ref[sb] += _dot(dyb[:, us], sr[:, ss].astype(BF16), TN)
            dcim_ref[sb] -= _dot(dyb[:, us], si[:, ss].astype(BF16), TN)
        _scan_segments(lr, li, abr, -abi, apr, -api, lam_r, lam_i, cl_r, cl_i, steps, True)
        for c0 in range(0, n_state, SCAN_LANES):
            ls = slice(c0, c0 + SCAN_LANES)
            cfr = jnp.broadcast_to(cfr_ref[:, ls], (SUBLANES, SCAN_LANES))
            cfi = jnp.broadcast_to(cfi_ref[:, ls], (SUBLANES, SCAN_LANES))

            def step(j, acc, ls=ls, cfr=cfr, cfi=cfi):
                gar, gai, gcr, gci, pr, pi = acc
                r0 = pl.multiple_of(j * SUBLANES, SUBLANES)
                rws = pl.ds(r0, SUBLANES)
                l_r, l_i = lr[rws, ls], li[rws, ls]
                t_r, t_i = _cmul(pr, -pi, l_r, l_i)
                b_r, b_i = bur[rws, ls], bui[rws, ls]
                c_r, c_i = _cmul(b_r, -b_i, l_r, l_i)
                x_r, x_i = _cmul(cfr, -cfi, l_r, l_i)
                bur[rws, ls] = x_r
                bui[rws, ls] = x_i
                return gar + t_r, gai + t_i, gcr + c_r, gci + c_i, sr[rws, ls], si[rws, ls]

            zero = jnp.zeros((SUBLANES, SCAN_LANES), F32)
            gar, gai, gcr, gci, _, _ = lax.fori_loop(
                0, steps, step, (zero, zero, zero, zero, cm_r[:, ls], cm_i[:, ls]))
            gabr_ref[:, ls] += gar
            gabi_ref[:, ls] += gai
            gcfr_ref[:, ls] += gcr
            gcfi_ref[:, ls] += gci
        for sb in range(nsb):
            us = slice(sb * MXU_DIM, (sb + 1) * MXU_DIM)
            ss = slice(sb * SB_STATE, (sb + 1) * SB_STATE)
            xr, xi = bur[:, ss].astype(BF16), bui[:, ss].astype(BF16)
            du = _dot(xr, bre_ref[sb], NT) + _dot(xi, bim_ref[sb], NT)
            useg[:, us] = du + d_ref[:, us] * dy[:, us]
            dbre_ref[sb] += _dot(ub[:, us], xr, TN)
            dbim_ref[sb] += _dot(ub[:, us], xi, TN)
        _segments_to_rows(du_ref, useg, steps, stage)

        @pl.when(pl.program_id(0) == nc - 1)
        def _():
            for src, dst in ((dbre_ref, gb2r_ref), (dbim_ref, gb2i_ref), (dcre_ref, gc2r_ref), (dcim_ref, gc2i_ref)):
                for sb in range(nsb):
                    dst[sb * MXU_DIM:(sb + 1) * MXU_DIM, :] = _block_rows(src[sb])

    rev = lambda c: nc - 1 - c
    const = lambda a: pl.BlockSpec(a.shape, lambda c: (0,) * a.ndim)
    tile = pl.BlockSpec((chunk, w), lambda c: (rev(c), 0))
    row_n = pl.BlockSpec((1, n_state), lambda c: (0, 0))
    row_w = pl.BlockSpec((1, w), lambda c: (0, 0))
    st = pl.BlockSpec((1, 1, n_state), lambda c: (rev(c), 0, 0))
    acc8 = pl.BlockSpec((SUBLANES, n_state), lambda c: (0, 0))
    big = pltpu.VMEM((chunk, n_state), F32)
    small = pltpu.VMEM((chunk, w), F32)
    row = pltpu.VMEM((1, n_state), F32)
    eight = pltpu.VMEM((SUBLANES, n_state), F32)
    blk = (nsb, MXU_DIM, SB_STATE)
    held = [pltpu.VMEM(blk, BF16)] * 4 + [pltpu.VMEM(blk, F32)] * 4
    vmem = (4 * (_nbytes(blk, BF16) + _nbytes(blk, F32)) + 7 * _nbytes((chunk, n_state), F32)
            + 20 * _nbytes((chunk, w), F32) + 16 * _nbytes(bc_rows[0].shape, F32))
    res, landed = _call(
        body, [proj, proj, y, dyg, st_re, st_im, *bc_rows, *rows_p, d_row], name=name,
        out_shape=[jax.ShapeDtypeStruct((rows, w), F32)] + [jax.ShapeDtypeStruct(b.shape, F32) for b in bc_rows]
        + [jax.ShapeDtypeStruct((SUBLANES, n_state), F32)] * 4 + [jax.ShapeDtypeStruct((1, w), F32)],
        grid=(nc,),
        in_specs=_u_specs(w, o_u, chunk, rev) + [tile, tile, st, st] + [const(b) for b in bc_rows]
        + [row_n] * 6 + [row_w],
        out_specs=[tile] + [const(b) for b in bc_rows] + [acc8] * 4 + [row_w],
        scratch_shapes=held + [small] * 3 + [pltpu.VMEM((w // LANES, chunk, LANES), F32)] + [big] * 6 + [row] * 4
        + [eight] * 4,
        semantics=("arbitrary",), vmem=vmem, rider=rider)
    return res if rider is None else (res, landed)


def _loss_grad(x, mm, target, name):
    rows, d = x.shape

    def fn(xv, mv, tv):
        err = xv + mv - tv
        g = err * (1.0 / d)
        return g, g, 0.5 * err * g

    return _ew(fn, name=name, rows=rows, width=d, tiles=[(x, 0), (mm, 0), (target, 0)],
               outs=[(F32, d, 0), (BF16, d, 0)], accs=1)


def _pair_sum(grad, recv, name):
    r4, cdim = recv.shape
    r = r4 // N_CHIPS
    tr = _tile(r, 544, 16)
    g4 = grad.reshape(N_CHIPS, 2, r, cdim)
    r3 = recv.reshape(N_CHIPS, r, cdim)
    core = jnp.reshape(lax.axis_index("c"), (1,)).astype(jnp.int32)

    def body(c_ref, g_ref, r_ref, o_ref):
        o_ref[...] = (g_ref[0] + r_ref[...]).astype(BF16)

    out = _pallas(
        body, name=name, out_shape=jax.ShapeDtypeStruct((N_CHIPS, r, cdim), BF16),
        grid_spec=pltpu.PrefetchScalarGridSpec(
            num_scalar_prefetch=1, grid=(N_CHIPS, r // tr),
            in_specs=[pl.BlockSpec((1, 1, tr, cdim), lambda j, i, c: (j, c[0], i, 0)),
                      pl.BlockSpec((1, tr, cdim), lambda j, i, c: (j, i, 0))],
            out_specs=pl.BlockSpec((1, tr, cdim), lambda j, i, c: (j, i, 0))),
        compiler_params=_params(("parallel", "parallel"), 6 * _nbytes((tr, cdim), F32)),
    )(core, g4, r3)
    return out.reshape(r4, cdim)


def _chip_sum(recv, name):
    r4, cdim = recv.shape
    r = r4 // N_CHIPS
    tr = _tile(r, 544, 16)
    r3 = recv.reshape(N_CHIPS, r, cdim)

    def body(r_ref, o_ref):
        acc = r_ref[0].astype(F32)
        for j in range(1, N_CHIPS):
            acc = acc + r_ref[j].astype(F32)
        o_ref[...] = acc

    return _pallas(
        body, name=name, out_shape=jax.ShapeDtypeStruct((r, cdim), F32), grid=(r // tr,),
        in_specs=[pl.BlockSpec((N_CHIPS, tr, cdim), lambda i: (0, i, 0))],
        out_specs=pl.BlockSpec((tr, cdim), lambda i: (i, 0)),
        compiler_params=_params(("parallel",), 8 * _nbytes((tr, cdim), F32)),
    )(r3)


def _adamw_math(w, g, m, v):
    m = ADAM_B1 * m + (1.0 - ADAM_B1) * g
    v = ADAM_B2 * v + (1.0 - ADAM_B2) * (g * g)
    m_hat = m / (1.0 - ADAM_B1 ** ADAM_STEP)
    v_hat = v / (1.0 - ADAM_B2 ** ADAM_STEP)
    delta = -ADAM_LR * (m_hat / (jnp.sqrt(v_hat) + ADAM_EPS) + ADAM_WD * w)
    return delta, m, v


def _adamw(w, g, m, v, name):
    rows, cols = w.shape
    tr = _tile(rows, 256, SUBLANES)

    def body(w_ref, g_ref, m_ref, v_ref, d_ref, nm_ref, nv_ref):
        d, nm, nv = _adamw_math(w_ref[...], g_ref[...], m_ref[...], v_ref[...])
        d_ref[...] = d
        nm_ref[...] = nm
        nv_ref[...] = nv

    spec = pl.BlockSpec((tr, cols), lambda i: (i, 0))
    shp = jax.ShapeDtypeStruct((rows, cols), F32)
    return _pallas(
        body, name=name, out_shape=[shp] * 3, grid=(rows // tr,), in_specs=[spec] * 4, out_specs=[spec] * 3,
        compiler_params=_params(("parallel",)),
    )(w, g, m, v)


def _adamw_chips(w, parts, m, v, name):
    rows, cols = w.shape
    assert sum(p.shape[1] for p in parts) == cols
    tr = _tile(rows, 64, 16)
    n = len(parts)

    def body(*refs):
        w_ref, m_ref, v_ref = refs[0], refs[1 + n], refs[2 + n]
        g_ref, d_ref, nm_ref, nv_ref = refs[3 + n:]
        cols_g = []
        for p_ref in refs[1:1 + n]:
            acc = p_ref[0].astype(F32)
            for j in range(1, N_CHIPS):
                acc = acc + p_ref[j].astype(F32)
            cols_g.append(acc)
        g = cols_g[0] if n == 1 else jnp.concatenate(cols_g, axis=1)
        d, nm, nv = _adamw_math(w_ref[...], g, m_ref[...], v_ref[...])
        g_ref[...] = g
        d_ref[...] = d
        nm_ref[...] = nm
        nv_ref[...] = nv

    spec = pl.BlockSpec((tr, cols), lambda i: (i, 0))
    part_specs = [pl.BlockSpec((N_CHIPS, tr, p.shape[1]), lambda i: (0, i, 0)) for p in parts]
    shp = jax.ShapeDtypeStruct((rows, cols), F32)
    return _pallas(
        body, name=name, out_shape=[shp] * 4, grid=(rows // tr,),
        in_specs=[spec] + part_specs + [spec, spec], out_specs=[spec] * 4,
        compiler_params=_params(("parallel",)),
    )(w, *[p.reshape(N_CHIPS, rows, p.shape[1]) for p in parts], m, v)


def _adamw_small(w, parts, m, v, name):
    rows, cols = w.shape
    p3 = parts.reshape(N_DEV, rows, cols)

    def body(w_ref, p_ref, m_ref, v_ref, g_ref, d_ref, nm_ref, nv_ref):
        g = p_ref[0]
        for k in range(1, N_DEV):
            g = g + p_ref[k]
        d, nm, nv = _adamw_math(w_ref[...], g, m_ref[...], v_ref[...])
        g_ref[...] = g
        d_ref[...] = d
        nm_ref[...] = nm
        nv_ref[...] = nv

    shp = jax.ShapeDtypeStruct((rows, cols), F32)
    return _pallas(body, name=name, out_shape=[shp] * 4)(w, p3, m, v)


SMALL = ("norm_w", "q_norm_w", "k_norm_w", "sinks", "A_re", "A_im", "log_dt", "B_re", "B_im", "C_re", "C_im",
         "D_skip", "b_glu")
LARGE = ("w_in", "w_attn_proj", "w_glu", "w_ssm_proj", "w_out")
ORDER = ("norm_w", "w_in", "q_norm_w", "k_norm_w", "sinks", "w_attn_proj", "A_re", "A_im", "log_dt", "B_re", "B_im",
         "C_re", "C_im", "D_skip", "w_glu", "b_glu", "w_ssm_proj", "w_out")


SMALL_REST = ("loss",) + SMALL[1:]


def _pack(named, keys):
    flat = jnp.concatenate([named[k].reshape(-1).astype(F32) for k in keys])
    n = flat.shape[0]
    rows = -(-n // (LANES * SUBLANES)) * SUBLANES
    return jnp.pad(flat, (0, rows * LANES - n)).reshape(rows, LANES)


def _unpack(packed, like, keys):
    flat = packed.reshape(-1)
    out, o = {}, 0
    for k in keys:
        n = like[k].size
        out[k] = flat[o:o + n].reshape(like[k].shape)
        o += n
    return out


def _step(xs, target, p, shards):
    s_in, s_ap, s_glu, s_sp, s_o = shards
    seq, d = xs.shape
    attn_w = (d // 128) * HEAD_DIM
    n_q = attn_w // HEAD_DIM
    kv_w = N_KV_HEADS * HEAD_DIM
    ssm_w = d // 2
    n_groups = ssm_w // GROUP
    n_state = n_groups * STATE
    in_w = N_DEV * s_in.shape[0]
    assert in_w == 2 * attn_w + 2 * kv_w + 2 * ssm_w + 2 * d
    o_u = 2 * attn_w + 2 * kv_w
    o_z = o_u + ssm_w
    o_ga = o_z + ssm_w
    chunk = min(BLOCK, seq)
    cw = d // 4

    norm_row = p["norm_w"].reshape(1, d)
    h = _rmsnorm_fwd(xs, norm_row, "rmsnorm_fwd")
    half = d // W_IN_PARTS
    assert W_IN_PARTS == 2
    s_in_parts = [s_in[:, :half], s_in[:, half:]]
    (w_lo,) = _exchange(_all_gather(s_in_parts[:1]), "gather_w_in_0")
    part, (w_hi,) = _matmul(Cols(h, 0, half), w_lo, mode="nt", name="in_proj_0", tn=512,
                            rider=_all_gather(s_in_parts[1:]))
    proj = _matmul(Cols(h, half, half), w_hi, mode="nt", name="in_proj_1", tn=512, out_dtype=BF16, add=part)
    w_in_parts = [w_lo, w_hi]
    qw_row = jnp.tile(p["q_norm_w"], n_q).reshape(1, attn_w)
    kw_row = jnp.tile(p["k_norm_w"], N_KV_HEADS).reshape(1, kv_w)
    gmat = _head_mean_matrix()
    ag = _attention_fwd(proj, qw_row, kw_row, gmat, p["sinks"], attn_w=attn_w, kv_w=kv_w, name="attention_fwd")

    log_dt_col = p["log_dt"].reshape(n_groups, 1)
    prep = _ssm_prep(p["A_re"], p["A_im"], log_dt_col, chunk // SUBLANES, "ssm_prep")
    rows_p = [v.reshape(1, n_state) for v in prep]
    bc_rows = _ssm_rows(p["B_re"], p["B_im"], p["C_re"], p["C_im"])
    d_row = p["D_skip"].reshape(1, ssm_w)
    (y_ssm, st_re, st_im, yg), (w_ap_t, w_glu_t, w_sp_t, w_o) = _ssm_fwd(
        proj, o_u, bc_rows, rows_p, d_row, chunk=chunk, name="ssm_fwd", rider=_all_gather([s_ap, s_glu, s_sp, s_o]))
    glu = _matmul(yg, w_glu_t, mode="nt", name="glu_proj", out_dtype=BF16, bias=p["b_glu"].reshape(1, 2 * ssm_w))
    (ts,) = _ew(lambda ga, gb, z: ga * _sigmoid(gb) * _silu(z), name="glu_gate", rows=seq, width=ssm_w,
                tiles=[(glu, 0), (glu, ssm_w), (proj, o_z)], outs=[(BF16, ssm_w, 0)], cw=cw)
    yy = _matmul(ag, w_ap_t, mode="nt", name="attn_proj", out_dtype=BF16, out_cols=(2 * d, 0))
    yy = _matmul(ts, w_sp_t, mode="nt", name="ssm_proj", out_dtype=BF16, out_cols=(2 * d, d), into=yy)
    (merged,) = _ew(lambda ya, ys, ga, gs: _sigmoid(ga) * ya + _sigmoid(gs) * ys, name="merge", rows=seq, width=d,
                    tiles=[(yy, 0), (yy, d), (proj, o_ga), (proj, o_ga + d)], outs=[(BF16, d, 0)], cw=cw)
    mm = _matmul(merged, w_o, mode="nn", name="out_proj")
    dout, dout_b, loss_cols = _loss_grad(xs, mm, target, "loss_grad")
    loss_local = jnp.sum(loss_cols)

    g_w_o = _matmul(merged, dout_b, mode="tn", name="grad_w_out", tm=512, tk=4096)
    dmerged, (sib_o,) = _matmul(dout_b, w_o, mode="nt", name="d_merged", out_dtype=BF16,
                                rider=_sibling_exchange([g_w_o]))
    pair_o = _pair_sum(g_w_o, sib_o, "pair_sum_w_out")

    def merge_bwd(dm, y, g):
        s = _sigmoid(g)
        return dm * s, dm * y * s * (1.0 - s)

    dyy, dproj = _ew(merge_bwd, name="merge_bwd", rows=seq, width=2 * d,
                     tiles=[(dmerged, 0, d), (yy, 0), (proj, o_ga)],
                     outs=[(BF16, 2 * d, 0), (BF16, in_w, o_ga)], cw=cw)
    dy_a, dy_s = Cols(dyy, 0, d), Cols(dyy, d, d)
    g_w_ap_t = _matmul(dy_a, ag, mode="tn", name="grad_w_attn_proj", tm=512, tk=4096)
    g_w_sp_t = _matmul(dy_s, ts, mode="tn", name="grad_w_ssm_proj", tm=512, tk=4096)
    d_ag = _matmul(dy_a, w_ap_t, mode="nn", name="d_attn_gated", out_dtype=BF16)
    d_ts = _matmul(dy_s, w_sp_t, mode="nn", name="d_ssm_gated", out_dtype=BF16)

    (dproj, dkv, g_qw, g_kw, g_sinks), (chips_o, sib_ap, sib_sp) = _attention_bwd(
        proj, d_ag, dproj, qw_row, kw_row, gmat, p["sinks"], attn_w=attn_w, kv_w=kv_w, name="attention_bwd",
        rider=_join(_chip_exchange([pair_o]), _sibling_exchange([g_w_ap_t, g_w_sp_t])))
    pair_ap = _pair_sum(g_w_ap_t, sib_ap, "pair_sum_w_attn_proj")
    pair_sp = _pair_sum(g_w_sp_t, sib_sp, "pair_sum_w_ssm_proj")
    dproj = _attention_dkv(dproj, dkv, attn_w=attn_w, kv_w=kv_w, name="attention_dkv")

    n_half = ssm_w // _tile(2 * ssm_w, cw)

    def glu_bwd(j, dt, ga, gb, z):
        sb, sz = _sigmoid(gb), _silu(z)
        dg = jnp.where(j < n_half, dt * sb * sz, dt * ga * sb * (1.0 - sb) * sz)
        return dg, dg

    glu_ops = [(d_ts, 0, ssm_w), (glu, 0, ssm_w), (glu, ssm_w, ssm_w), (proj, o_z, ssm_w)]
    dglu, g_bglu = _ew(glu_bwd, name="glu_bwd", rows=seq, width=2 * ssm_w, tiles=glu_ops,
                       outs=[(BF16, 2 * ssm_w, 0)], accs=1, cw=cw, with_col=True)
    (dproj,) = _ew(lambda dt, ga, gb, z: dt * ga * _sigmoid(gb) * _dsilu(z), name="glu_bwd_z", rows=seq,
                   width=ssm_w, tiles=glu_ops, outs=[(BF16, in_w, o_z)], into=[dproj], cw=cw)
    g_w_glu_t = _matmul(dglu, yg, mode="tn", name="grad_w_glu", tm=512, tk=4096)
    d_yg = _matmul(dglu, w_glu_t, mode="nn", name="d_gelu", out_dtype=BF16)
    ((du, db_re, db_im, dc_re, dc_im, gabr, gabi, gcfr, gcfi, g_d), (chips_ap, chips_sp, sib_glu)) = _ssm_bwd(
        proj, o_u, y_ssm, d_yg, st_re, st_im, bc_rows, rows_p, d_row, chunk=chunk, name="ssm_bwd",
        rider=_join(_chip_exchange([pair_ap, pair_sp]), _sibling_exchange([g_w_glu_t])))
    pair_glu = _pair_sum(g_w_glu_t, sib_glu, "pair_sum_w_glu")
    (dproj,) = _ew(lambda v: v, name="du_store", rows=seq, width=ssm_w, tiles=[(du, 0)],
                   outs=[(BF16, in_w, o_u)], into=[dproj], cw=cw)
    g_a_re, g_a_im, g_log_dt = _ssm_param_bwd(
        p["A_re"], p["A_im"], log_dt_col, *[g.reshape(SUBLANES, n_groups, STATE) for g in (gabr, gabi, gcfr, gcfi)],
        "ssm_param_bwd")
    small_grads = dict(
        loss=loss_local, q_norm_w=g_qw.reshape(n_q, HEAD_DIM).sum(0), k_norm_w=g_kw.reshape(N_KV_HEADS, HEAD_DIM).sum(0),
        sinks=g_sinks[0, :n_q], A_re=g_a_re, A_im=g_a_im, log_dt=g_log_dt.reshape(n_groups),
        B_re=_from_ssm_rows(db_re, True), B_im=_from_ssm_rows(db_im, True),
        C_re=_from_ssm_rows(dc_re, False), C_im=_from_ssm_rows(dc_im, False),
        D_skip=g_d.reshape(n_groups, GROUP), b_glu=g_bglu.reshape(2 * ssm_w))

    n_parts = W_IN_PARTS
    wq = d // n_parts
    g_parts, pair_parts, chip_parts = [], [], []
    extra = [_chip_exchange([pair_glu]), _all_gather([_pack(small_grads, SMALL_REST)])]
    chips_glu = small_parts = dh = None
    for step in range(n_parts + 2):
        riders = list(extra) if step == 0 else []
        if 0 <= step - 2 < n_parts:
            riders.append(_chip_exchange([pair_parts[step - 2]]))
        if 0 <= step - 1 < n_parts:
            riders.append(_sibling_exchange([g_parts[step - 1]]))
        rider = _join(*riders) if riders else None
        if step < n_parts:
            res = _matmul(dproj, Cols(h, step * wq, wq), mode="tn", name="grad_w_in_%d" % step, tk=4096, rider=rider)
            out, landed = res if rider is not None else (res, [])
            g_parts.append(out)
        else:
            q = step - n_parts
            dh, landed = _matmul(dproj, w_in_parts[q], mode="nn", name="d_normed_%d" % q, tk=2176,
                                 out_cols=(d, q * wq), into=dh, rider=rider)
        landed = list(landed)
        if step == 0:
            chips_glu, small_parts = landed[:2]
            landed = landed[2:]
        if 0 <= step - 2 < n_parts:
            chip_parts.append(landed.pop(0))
        if 0 <= step - 1 < n_parts:
            pair_parts.append(_pair_sum(g_parts[step - 1], landed.pop(0), "pair_sum_w_in_%d" % (step - 1)))
    grad_x, g_norm = _rmsnorm_bwd(xs, norm_row, dh, dout, "rmsnorm_bwd")
    (norm_parts,) = _exchange(_all_gather([_pack(dict(norm_w=g_norm), ("norm_w",))]), "gather_norm_grad")
    from_chips = dict(zip(LARGE, (chip_parts, [chips_ap], [chips_glu], [chips_sp], [chips_o])))
    return grad_x, from_chips, small_parts, norm_parts


def kernel(x, norm_w, w_in, q_norm_w, k_norm_w, sinks, w_attn_proj, A_re, A_im, log_dt, B_re, B_im, C_re, C_im, D_skip, w_glu, b_glu, w_ssm_proj, w_out, loss_target, m_norm_w, m_w_in, m_q_norm_w, m_k_norm_w, m_sinks, m_w_attn_proj, m_A_re, m_A_im, m_log_dt, m_B_re, m_B_im, m_C_re, m_C_im, m_D_skip, m_w_glu, m_b_glu, m_w_ssm_proj, m_w_out, v_norm_w, v_w_in, v_q_norm_w, v_k_norm_w, v_sinks, v_w_attn_proj, v_A_re, v_A_im, v_log_dt, v_B_re, v_B_im, v_C_re, v_C_im, v_D_skip, v_w_glu, v_b_glu, v_w_ssm_proj, v_w_out):
    weights = dict(norm_w=norm_w, w_in=w_in, q_norm_w=q_norm_w, k_norm_w=k_norm_w, sinks=sinks,
                   w_attn_proj=w_attn_proj, A_re=A_re, A_im=A_im, log_dt=log_dt, B_re=B_re, B_im=B_im, C_re=C_re,
                   C_im=C_im, D_skip=D_skip, w_glu=w_glu, b_glu=b_glu, w_ssm_proj=w_ssm_proj, w_out=w_out)
    m_in = dict(norm_w=m_norm_w, w_in=m_w_in, q_norm_w=m_q_norm_w, k_norm_w=m_k_norm_w, sinks=m_sinks,
                w_attn_proj=m_w_attn_proj, A_re=m_A_re, A_im=m_A_im, log_dt=m_log_dt, B_re=m_B_re, B_im=m_B_im,
                C_re=m_C_re, C_im=m_C_im, D_skip=m_D_skip, w_glu=m_w_glu, b_glu=m_b_glu, w_ssm_proj=m_w_ssm_proj,
                w_out=m_w_out)
    v_in = dict(norm_w=v_norm_w, w_in=v_w_in, q_norm_w=v_q_norm_w, k_norm_w=v_k_norm_w, sinks=v_sinks,
                w_attn_proj=v_w_attn_proj, A_re=v_A_re, A_im=v_A_im, log_dt=v_log_dt, B_re=v_B_re, B_im=v_B_im,
                C_re=v_C_re, C_im=v_C_im, D_skip=v_D_skip, w_glu=v_w_glu, b_glu=v_b_glu, w_ssm_proj=v_w_ssm_proj,
                w_out=v_w_out)

    _, seq, d = x.shape
    column_sharded = LARGE[:4]
    as_rows = lambda k, a: a.T if k in column_sharded else a
    shards = [as_rows(k, weights[k]).astype(BF16) for k in LARGE]
    small = {k: weights[k] for k in SMALL}
    grad_x, from_chips, small_parts, norm_parts = _step(x.reshape(seq, d), loss_target.reshape(seq, d), small,
                                                        shards)

    grads, delta, new_m, new_v = {}, {}, {}, {}
    for k in LARGE:
        if k == "w_in":
            res = _adamw_chips(weights[k].T, from_chips[k], m_in[k].T, v_in[k].T, "adamw_" + k)
            grads[k], delta[k], new_m[k], new_v[k] = [a.T for a in res]
        elif k == "w_out":
            grads[k], delta[k], new_m[k], new_v[k] = _adamw_chips(weights[k], from_chips[k], m_in[k], v_in[k],
                                                                  "adamw_" + k)
        else:
            grads[k] = _chip_sum(from_chips[k][0], "chip_sum_" + k).T
            delta[k], new_m[k], new_v[k] = _adamw(weights[k], grads[k], m_in[k], v_in[k], "adamw_" + k)

    zero = jnp.zeros((), F32)
    for keys, parts in ((SMALL_REST, small_parts), (("norm_w",), norm_parts)):
        like = dict(small, loss=zero)
        packs = [_pack(dict(src, loss=zero), keys) for src in (weights, m_in, v_in)]
        res = _adamw_small(packs[0], parts, packs[1], packs[2], "adamw_small_%d" % len(keys))
        for dst, r in zip((grads, delta, new_m, new_v), res):
            dst.update(_unpack(r, like, keys))
    loss = grads["loss"]

    return (loss, grad_x.reshape(x.shape), *[grads[k] for k in ORDER], *[delta[k] for k in ORDER],
            *[new_m[k] for k in ORDER], *[new_v[k] for k in ORDER])
```

```python
import math
from typing import Callable, NamedTuple

import jax
import jax.numpy as jnp
import numpy as np
from jax import lax
from jax.experimental import pallas as pl
from jax.experimental.pallas import tpu as pltpu

F32 = jnp.float32
BF16 = jnp.bfloat16
MESH = pl.DeviceIdType.MESH

HEAD_DIM = 64
N_KV_HEADS = 4
GROUP = 16
STATE = 64
BLOCK = 128
NORM_EPS = 1e-6
N_DEV = 8
N_CHIPS = 4
LANES = 128
SUBLANES = 8
MXU_DIM = 256
VMEM_BYTES = 64 * 1024 * 1024
VMEM_CAP = VMEM_BYTES - 8 * 1024 * 1024

ADAM_LR = 0.001
ADAM_B1 = 0.9
ADAM_B2 = 0.999
ADAM_EPS = 1e-08
ADAM_WD = 0.01
ADAM_STEP = 10

GELU_C = math.sqrt(2.0 / math.pi)
GELU_K = 0.044715


def _tile(dim, pref, mult=LANES):
    if dim <= pref:
        return dim
    best = None
    for d in range(mult, pref + 1, mult):
        if dim % d == 0:
            best = d
    assert best is not None, (dim, pref, mult)
    return best


def _params(semantics=None, vmem=None):
    kw = {}
    if semantics is not None:
        kw["dimension_semantics"] = semantics
    if vmem is not None:
        kw["vmem_limit_bytes"] = int(min(VMEM_CAP, max(vmem, 32 * 1024 * 1024)))
    return pltpu.CompilerParams(**kw)


def _nbytes(shape, dtype):
    return math.prod(shape) * jnp.dtype(dtype).itemsize


def _sigmoid(x):
    return 1.0 / (1.0 + jnp.exp(-x))


def _silu(x):
    return x * _sigmoid(x)


def _dsilu(x):
    s = _sigmoid(x)
    return s * (1.0 + x * (1.0 - s))


def _gelu(x):
    return 0.5 * x * (1.0 + jnp.tanh(GELU_C * (x + GELU_K * x * x * x)))


def _dgelu(x):
    t = jnp.tanh(GELU_C * (x + GELU_K * x * x * x))
    return 0.5 * (1.0 + t) + 0.5 * x * (1.0 - t * t) * GELU_C * (1.0 + 3.0 * GELU_K * x * x)


def _dot(a, b, dims):
    return lax.dot_general(a, b, (dims, ((), ())), preferred_element_type=F32)


NN = ((1,), (0,))
NT = ((1,), (1,))
TN = ((0,), (0,))


def _any_spec():
    return pl.BlockSpec(memory_space=pl.ANY)


def _pallas(body, **kw):
    pin = lambda s: pltpu.HBM(s.shape, s.dtype) if isinstance(s, jax.ShapeDtypeStruct) else s
    out_shape = kw.pop("out_shape")
    out_shape = [pin(s) for s in out_shape] if isinstance(out_shape, (list, tuple)) else pin(out_shape)
    call = pl.pallas_call(body, out_shape=out_shape, **kw)

    def run(*operands):
        pinned = [pltpu.with_memory_space_constraint(o, pltpu.HBM) if jnp.issubdtype(o.dtype, jnp.floating) else o
                  for o in operands]
        return call(*pinned)

    return run


class Rider(NamedTuple):
    operands: tuple
    out_shapes: tuple
    sems: tuple
    start: Callable
    finish: Callable


def _all_gather(shards):
    n = len(shards)

    def copies(ins, outs, sems):
        send_sems, recv_sems, local_sems = sems
        x, y, c = lax.axis_index("x"), lax.axis_index("y"), lax.axis_index("c")
        me, sibling = (x, y, c), (x, y, 1 - c)
        chips = [(1 - x, y), (x, 1 - y), (1 - x, 1 - y)]

        def rows(k, px, py, pc):
            r = shards[k].shape[0]
            return outs[k].at[pl.ds((4 * px + 2 * py + pc) * r, r), :]

        def copy(k, s, block, to, src=None):
            return pltpu.make_async_remote_copy(
                src_ref=rows(k, *block) if src is None else src, dst_ref=rows(k, *block),
                send_sem=send_sems.at[7 * k + s], recv_sem=recv_sems.at[7 * k + s],
                device_id=to, device_id_type=MESH)

        mine = [pltpu.make_async_copy(ins[k], rows(k, *me), local_sems.at[k]) for k in range(n)]
        first = []
        for k in range(n):
            first.append(copy(k, 0, me, sibling, src=ins[k]))
            first += [copy(k, 1 + j, me, (*chip, c), src=ins[k]) for j, chip in enumerate(chips)]
        return me, sibling, chips, c, copy, mine, first

    def start(ins, outs, sems):
        *_, mine, first = copies(ins, outs, sems)
        for cp in mine + first:
            cp.start()

    def finish(ins, outs, sems):
        me, sibling, chips, c, copy, mine, first = copies(ins, outs, sems)
        passed = []
        for j, chip in enumerate(chips):
            for k in range(n):
                copy(k, 1 + j, (*chip, c), me).wait_recv()
                fwd = copy(k, 4 + j, (*chip, c), sibling)
                fwd.start()
                passed.append(fwd)
        for k in range(n):
            copy(k, 0, sibling, me).wait_recv()
            for j, chip in enumerate(chips):
                copy(k, 4 + j, (*chip, 1 - c), me).wait_recv()
        for cp in first + passed:
            cp.wait_send()
        for cp in mine:
            cp.wait()

    return Rider(
        tuple(shards),
        tuple(jax.ShapeDtypeStruct((N_DEV * s.shape[0], s.shape[1]), s.dtype) for s in shards),
        (pltpu.SemaphoreType.DMA((7 * n,)), pltpu.SemaphoreType.DMA((7 * n,)), pltpu.SemaphoreType.DMA((n,))),
        start, finish)


def _sibling_exchange(grads):
    n = len(grads)

    def copies(ins, outs, sems):
        send_sems, recv_sems = sems
        x, y, c = lax.axis_index("x"), lax.axis_index("y"), lax.axis_index("c")
        out = []
        for k in range(n):
            r = grads[k].shape[0] // N_DEV
            for j in range(N_CHIPS):
                out.append(pltpu.make_async_remote_copy(
                    src_ref=ins[k].at[pl.ds((2 * j + 1 - c) * r, r), :],
                    dst_ref=outs[k].at[pl.ds(j * r, r), :],
                    send_sem=send_sems.at[N_CHIPS * k + j], recv_sem=recv_sems.at[N_CHIPS * k + j],
                    device_id=(x, y, 1 - c), device_id_type=MESH))
        return out

    def start(ins, outs, sems):
        for cp in copies(ins, outs, sems):
            cp.start()

    def finish(ins, outs, sems):
        for cp in copies(ins, outs, sems):
            cp.wait()

    return Rider(
        tuple(grads), tuple(jax.ShapeDtypeStruct((g.shape[0] // 2, g.shape[1]), g.dtype) for g in grads),
        (pltpu.SemaphoreType.DMA((N_CHIPS * n,)), pltpu.SemaphoreType.DMA((N_CHIPS * n,))), start, finish)


def _chip_exchange(parts):
    n = len(parts)

    def copies(ins, outs, sems):
        send_sems, recv_sems, local_sems = sems
        x, y, c = lax.axis_index("x"), lax.axis_index("y"), lax.axis_index("c")
        my_chip = 2 * x + y
        chips = [(1 - x, y), (x, 1 - y), (1 - x, 1 - y)]
        local, sent = [], []
        for k in range(n):
            r = parts[k].shape[0] // N_CHIPS
            mine = pl.ds(my_chip * r, r)
            local.append(pltpu.make_async_copy(ins[k].at[mine, :], outs[k].at[mine, :], local_sems.at[k]))
            for s, (px, py) in enumerate(chips):
                sent.append(pltpu.make_async_remote_copy(
                    src_ref=ins[k].at[pl.ds((2 * px + py) * r, r), :], dst_ref=outs[k].at[mine, :],
                    send_sem=send_sems.at[3 * k + s], recv_sem=recv_sems.at[3 * k + s],
                    device_id=(px, py, c), device_id_type=MESH))
        return local, sent

    def start(ins, outs, sems):
        local, sent = copies(ins, outs, sems)
        for cp in local + sent:
            cp.start()

    def finish(ins, outs, sems):
        local, sent = copies(ins, outs, sems)
        for cp in sent + local:
            cp.wait()

    return Rider(
        tuple(parts), tuple(jax.ShapeDtypeStruct(p.shape, p.dtype) for p in parts),
        (pltpu.SemaphoreType.DMA((3 * n,)), pltpu.SemaphoreType.DMA((3 * n,)), pltpu.SemaphoreType.DMA((n,))),
        start, finish)


def _join(*riders):
    cuts_in, cuts_out, cuts_sem = [0], [0], [0]
    for r in riders:
        cuts_in.append(cuts_in[-1] + len(r.operands))
        cuts_out.append(cuts_out[-1] + len(r.out_shapes))
        cuts_sem.append(cuts_sem[-1] + len(r.sems))

    def each(which):
        def run(ins, outs, sems):
            for i, r in enumerate(riders):
                getattr(r, which)(ins[cuts_in[i]:cuts_in[i + 1]], outs[cuts_out[i]:cuts_out[i + 1]],
                                  sems[cuts_sem[i]:cuts_sem[i + 1]])
        return run

    return Rider(sum((r.operands for r in riders), ()), sum((r.out_shapes for r in riders), ()),
                 sum((r.sems for r in riders), ()), each("start"), each("finish"))


def _call(body, operands, *, name, out_shape, grid, in_specs, out_specs, scratch_shapes=(), aliases=None,
          semantics=None, vmem=None, rider=None):
    operands, out_shape, scratch_shapes = list(operands), list(out_shape), list(scratch_shapes)
    in_specs, out_specs = list(in_specs), list(out_specs)
    if rider is None:
        res = _pallas(
            body, name=name, out_shape=out_shape, grid=grid, in_specs=in_specs, out_specs=out_specs,
            scratch_shapes=scratch_shapes, input_output_aliases=aliases or {},
            compiler_params=_params(semantics, vmem))(*operands)
        return list(res), []
    n_in, n_out, n_scr = len(operands), len(out_shape), len(scratch_shapes)
    ri, ro = len(rider.operands), len(rider.out_shapes)

    def carried(*refs):
        a, b = n_in, n_in + ri
        c, d = b + n_out, b + n_out + ro
        e = d + n_scr
        ids = [pl.program_id(k) for k in range(len(grid))]
        first = ids[0] == 0
        last = ids[0] == grid[0] - 1
        for k in range(1, len(grid)):
            first = jnp.logical_and(first, ids[k] == 0)
            last = jnp.logical_and(last, ids[k] == grid[k] - 1)

        @pl.when(first)
        def _():
            rider.start(refs[a:b], refs[c:d], refs[e:])

        body(*refs[:a], *refs[b:c], *refs[d:e])

        @pl.when(last)
        def _():
            rider.finish(refs[a:b], refs[c:d], refs[e:])

    res = _pallas(
        carried, name=name, out_shape=out_shape + list(rider.out_shapes), grid=grid,
        in_specs=in_specs + [_any_spec()] * ri, out_specs=out_specs + [_any_spec()] * ro,
        scratch_shapes=scratch_shapes + list(rider.sems), input_output_aliases=aliases or {},
        compiler_params=_params(("arbitrary",) * len(grid), vmem))(*operands, *rider.operands)
    return list(res[:n_out]), list(res[n_out:])


def _exchange(rider, name):
    ri, ro = len(rider.operands), len(rider.out_shapes)

    def body(*refs):
        rider.start(refs[:ri], refs[ri:ri + ro], refs[ri + ro:])
        rider.finish(refs[:ri], refs[ri:ri + ro], refs[ri + ro:])

    return _pallas(
        body, name=name, out_shape=list(rider.out_shapes), in_specs=[_any_spec()] * ri,
        out_specs=[_any_spec()] * ro, scratch_shapes=list(rider.sems))(*rider.operands)


class Cols(NamedTuple):
    arr: jax.Array
    off: int
    width: int


def _cols(a):
    return a if isinstance(a, Cols) else Cols(a, 0, a.shape[1])


def _matmul(a, b, *, mode, name, out_dtype=F32, tm=1024, tn=1024, tk=2048, bias=None, add=None, out_cols=None,
            into=None, rider=None):
    a, b = _cols(a), _cols(b)
    if mode == "nn":
        (m, k), (k2, n) = (a.arr.shape[0], a.width), (b.arr.shape[0], b.width)
    elif mode == "nt":
        (m, k), (n, k2) = (a.arr.shape[0], a.width), (b.arr.shape[0], b.width)
    else:
        (k, m), (k2, n) = (a.arr.shape[0], a.width), (b.arr.shape[0], b.width)
    assert k == k2, (a.arr.shape, b.arr.shape, mode)
    tm, tn, tk = _tile(m, tm), _tile(n, tn), _tile(k, tk)
    nk = k // tk
    dims = {"nn": NN, "nt": NT, "tn": TN}[mode]
    if mode == "tn":
        assert a.off % tm == 0
        a_spec = pl.BlockSpec((tk, tm), lambda i, j, kk, o=a.off // tm: (kk, i + o))
    else:
        assert a.off % tk == 0
        a_spec = pl.BlockSpec((tm, tk), lambda i, j, kk, o=a.off // tk: (i, kk + o))
    if mode == "nt":
        assert b.off % tk == 0
        b_spec = pl.BlockSpec((tn, tk), lambda i, j, kk, o=b.off // tk: (j, kk + o))
    else:
        assert b.off % tn == 0
        b_spec = pl.BlockSpec((tk, tn), lambda i, j, kk, o=b.off // tn: (kk, j + o))
    in_specs, operands = [a_spec, b_spec], [a.arr, b.arr]
    assert bias is None or add is None
    if bias is not None:
        in_specs.append(pl.BlockSpec((1, tn), lambda i, j, kk: (0, j)))
        operands.append(bias)
    if add is not None:
        assert add.shape == (m, n)
        in_specs.append(pl.BlockSpec((tm, tn), lambda i, j, kk: (i, j)))
        operands.append(add)
    total_w, o_off = out_cols if out_cols is not None else (n, 0)
    assert o_off % tn == 0
    aliases = {}
    if into is not None:
        assert into.shape == (m, total_w) and into.dtype == out_dtype
        in_specs.append(_any_spec())
        operands.append(into)
        aliases = {len(operands) - 1: 0}
    n_in = len(operands)

    def body(*refs):
        a_ref, b_ref = refs[0], refs[1]
        bias_ref = refs[2] if bias is not None or add is not None else None
        o_ref = refs[n_in]
        acc_ref = refs[-1] if nk > 1 else None
        part = _dot(a_ref[...].astype(BF16), b_ref[...].astype(BF16), dims)

        def finish(acc):
            if bias_ref is not None:
                acc = acc + bias_ref[...]
            o_ref[...] = acc.astype(out_dtype)

        if nk == 1:
            finish(part)
        else:
            kk = pl.program_id(2)

            @pl.when(kk == 0)
            def _():
                acc_ref[...] = part

            @pl.when(kk > 0)
            def _():
                acc_ref[...] += part

            @pl.when(kk == nk - 1)
            def _():
                finish(acc_ref[...])

    vmem = 2 * (_nbytes((tm, tk), a.arr.dtype) + _nbytes((tk, tn), b.arr.dtype) + _nbytes((tm, tn), out_dtype))
    vmem += 3 * _nbytes((tm, tn), F32)
    (out,), landed = _call(
        body, operands, name=name, out_shape=[jax.ShapeDtypeStruct((m, total_w), out_dtype)],
        grid=(m // tm, n // tn, nk), in_specs=in_specs,
        out_specs=[pl.BlockSpec((tm, tn), lambda i, j, kk, o=o_off // tn: (i, j + o))],
        scratch_shapes=[pltpu.VMEM((tm, tn), F32)] if nk > 1 else [], aliases=aliases,
        semantics=("parallel", "parallel", "arbitrary"), vmem=vmem, rider=rider)
    return out if rider is None else (out, landed)


def _ew(fn, *, name, rows, width, tiles, vecs=(), outs, accs=0, tl=1024, cw=512, into=None, with_col=False):
    tl, cw = _tile(rows, tl, SUBLANES), _tile(width, cw)
    ncol = width // cw
    nt_, nv = len(tiles), len(vecs)
    into = list(into) if into is not None else [None] * len(outs)
    aliased = [t for t in into if t is not None]

    def off(o):
        assert o % cw == 0, (name, o, cw)
        return o // cw

    in_specs, vmem = [], 0
    for t in tiles:
        arr, o = t[0], off(t[1])
        wrap = t[2] // cw if len(t) > 2 else ncol
        in_specs.append(pl.BlockSpec((tl, cw), lambda j, i, o=o, wrap=wrap: (i, o + j % wrap)))
        vmem += _nbytes((tl, cw), arr.dtype)
    in_specs += [pl.BlockSpec((1, cw), lambda j, i, o=off(o): (0, j + o)) for _, o in vecs]
    in_specs += [_any_spec() for _ in aliased]
    out_shape, out_specs, aliases = [], [], {}
    n_in = nt_ + nv
    for idx, ((dt, tw, o), tgt) in enumerate(zip(outs, into)):
        out_shape.append(jax.ShapeDtypeStruct((rows, tw), dt))
        out_specs.append(pl.BlockSpec((tl, cw), lambda j, i, o=off(o): (i, j + o)))
        vmem += _nbytes((tl, cw), dt)
        if tgt is not None:
            assert tgt.shape == (rows, tw) and tgt.dtype == dt, (name, tgt.shape, tgt.dtype)
            aliases[n_in + len(aliases)] = idx
    for _ in range(accs):
        out_shape.append(jax.ShapeDtypeStruct((1, width), F32))
        out_specs.append(pl.BlockSpec((1, cw), lambda j, i: (0, j)))
    n_out = len(outs)

    def body(*refs):
        vals = [r[...].astype(F32) for r in refs[:n_in]]
        out_refs = refs[n_in + len(aliased):]
        res = fn(pl.program_id(0), *vals) if with_col else fn(*vals)
        res = res if isinstance(res, (tuple, list)) else (res,)
        assert len(res) == n_out + accs, (name, len(res))
        for r, v in zip(out_refs[:n_out], res[:n_out]):
            r[...] = v.astype(r.dtype)
        first = pl.program_id(1) == 0
        for r, v in zip(out_refs[n_out:], res[n_out:]):
            s = jnp.sum(v, axis=0, keepdims=True)

            @pl.when(first)
            def _(r=r, s=s):
                r[...] = s

            @pl.when(jnp.logical_not(first))
            def _(r=r, s=s):
                r[...] += s

    return _pallas(
        body, name=name, out_shape=out_shape, grid=(ncol, rows // tl),
        in_specs=in_specs, out_specs=out_specs, input_output_aliases=aliases,
        compiler_params=_params(("parallel", "arbitrary"), 3 * vmem),
    )(*[t[0] for t in tiles], *[v for v, _ in vecs], *aliased)


def _rmsnorm_fwd(x, w_row, name):
    rows, d = x.shape
    tl = _tile(rows, 512, SUBLANES)

    def body(x_ref, w_ref, h_ref):
        xv = x_ref[...]
        rstd = lax.rsqrt(jnp.mean(xv * xv, axis=-1, keepdims=True) + NORM_EPS)
        h_ref[...] = (xv * rstd * w_ref[...]).astype(BF16)

    return _pallas(
        body, name=name, out_shape=jax.ShapeDtypeStruct((rows, d), BF16), grid=(rows // tl,),
        in_specs=[pl.BlockSpec((tl, d), lambda i: (i, 0)), pl.BlockSpec((1, d), lambda i: (0, 0))],
        out_specs=pl.BlockSpec((tl, d), lambda i: (i, 0)),
        compiler_params=_params(("parallel",)),
    )(x, w_row)


def _rmsnorm_bwd(x, w_row, dh, dout, name, rider=None):
    rows, d = x.shape
    tl = _tile(rows, 256, SUBLANES)

    def body(x_ref, w_ref, dh_ref, dout_ref, gx_ref, gw_ref):
        xv = x_ref[...]
        rstd = lax.rsqrt(jnp.mean(xv * xv, axis=-1, keepdims=True) + NORM_EPS)
        xn = xv * rstd
        dhv = dh_ref[...]
        dxn = dhv * w_ref[...]
        dx = rstd * (dxn - xn * jnp.mean(dxn * xn, axis=-1, keepdims=True))
        gx_ref[...] = dout_ref[...] + dx
        gw = jnp.sum(dhv * xn, axis=0, keepdims=True)

        @pl.when(pl.program_id(0) == 0)
        def _():
            gw_ref[...] = gw

        @pl.when(pl.program_id(0) > 0)
        def _():
            gw_ref[...] += gw

    tile = pl.BlockSpec((tl, d), lambda i: (i, 0))
    row = pl.BlockSpec((1, d), lambda i: (0, 0))
    res, landed = _call(
        body, [x, w_row, dh, dout], name=name,
        out_shape=[jax.ShapeDtypeStruct((rows, d), F32), jax.ShapeDtypeStruct((1, d), F32)],
        grid=(rows // tl,), in_specs=[tile, row, tile, tile], out_specs=[tile, row],
        semantics=("arbitrary",), rider=rider)
    return res if rider is None else (res, landed)


def _head_mean(x, gmat):
    hi = x.astype(BF16)
    lo = (x - hi.astype(F32)).astype(BF16)
    out = []
    for s in range(x.shape[1] // MXU_DIM):
        sl = slice(s * MXU_DIM, (s + 1) * MXU_DIM)
        out.append(_dot(hi[:, sl], gmat, NN) + _dot(lo[:, sl], gmat, NN))
    return out[0] if len(out) == 1 else jnp.concatenate(out, axis=1)


def _head_mean_matrix():
    blk = jnp.arange(MXU_DIM) // HEAD_DIM
    return jnp.where(blk[:, None] == blk[None, :], 1.0 / HEAD_DIM, 0.0).astype(BF16)


def _spread_head(x, g, width):
    col = x[:, (g // 2) * LANES:(g // 2 + 1) * LANES]
    other = pltpu.roll(col, HEAD_DIM, axis=1)
    low = lax.broadcasted_iota(jnp.int32, col.shape, 1) < HEAD_DIM
    both = jnp.where(low, col, other) if g % 2 == 0 else jnp.where(low, other, col)
    return both if width == LANES else jnp.concatenate([both] * (width // LANES), axis=1)


def _head_diagonal(t, per_kv):
    head = lax.broadcasted_iota(jnp.int32, t.shape, 1) // HEAD_DIM
    zero = jnp.zeros_like(t)
    return jnp.concatenate([jnp.where(head == r, t, zero) for r in range(per_kv)], axis=0)


def _fold_heads(x, per_kv):
    rows = x.shape[0] // per_kv
    head = lax.broadcasted_iota(jnp.int32, (rows, x.shape[1]), 1) // HEAD_DIM
    acc = jnp.where(head == 0, x[0:rows], 0.0)
    for r in range(1, per_kv):
        acc = acc + jnp.where(head == r, x[r * rows:(r + 1) * rows], 0.0)
    while acc.shape[1] > LANES:
        half = acc.shape[1] // 2
        acc = acc[:, :half] + acc[:, half:]
    return acc + pltpu.roll(acc, HEAD_DIM, axis=1)


def _join_heads(parts):
    low = lax.broadcasted_iota(jnp.int32, parts[0].shape, 1) < HEAD_DIM
    cols = [jnp.where(low, parts[2 * j], parts[2 * j + 1]) for j in range(len(parts) // 2)]
    return cols[0] if len(cols) == 1 else jnp.concatenate(cols, axis=1)


def _attn_specs(attn_w, kv_w):
    half = attn_w // 2
    kcol, vcol = attn_w // kv_w, attn_w // kv_w + 1
    gcol = (attn_w + 2 * kv_w) // half
    prev = lambda i: jnp.maximum(i - 1, 0)
    return [
        pl.BlockSpec((BLOCK, attn_w), lambda i: (i, 0)),
        pl.BlockSpec((BLOCK, kv_w), lambda i: (prev(i), kcol)),
        pl.BlockSpec((BLOCK, kv_w), lambda i: (i, kcol)),
        pl.BlockSpec((BLOCK, kv_w), lambda i: (prev(i), vcol)),
        pl.BlockSpec((BLOCK, kv_w), lambda i: (i, vcol)),
        pl.BlockSpec((BLOCK, half), lambda i: (i, gcol)),
        pl.BlockSpec((BLOCK, half), lambda i: (i, gcol + 1)),
    ]


def _band_mask(i):
    q_loc = lax.broadcasted_iota(jnp.int32, (BLOCK, 2 * BLOCK), 0) + BLOCK
    k_loc = lax.broadcasted_iota(jnp.int32, (BLOCK, 2 * BLOCK), 1)
    diff = q_loc - k_loc
    first_key = jnp.where(i == 0, BLOCK, 0)
    return (diff >= 0) & (diff < BLOCK) & (k_loc >= first_key)


def _softmax_with_sink(s, sink):
    m = jnp.maximum(jnp.max(s, axis=-1, keepdims=True), sink)
    p = jnp.exp(s - m)
    e_sink = jnp.exp(sink - m)
    den = jnp.sum(p, axis=-1, keepdims=True) + e_sink
    inv = 1.0 / den
    return p * inv, e_sink * inv


def _attn_block(i, q, kk, vv, qw, kw, gmat, sink_ref, per_kv):
    scale = 1.0 / math.sqrt(HEAD_DIM)
    keys = 2 * BLOCK
    valid = _band_mask(i)
    q_rstd = lax.rsqrt(_head_mean(q * q, gmat) + NORM_EPS)
    qn = q * q_rstd
    qh = (qn * qw).astype(BF16)
    k_rstd = lax.rsqrt(_head_mean(kk * kk, gmat) + NORM_EPS)
    kn = kk * k_rstd
    kh = kn * kw
    gw = per_kv * HEAD_DIM
    groups = []
    for g in range(N_KV_HEADS):
        kd = _head_diagonal(_spread_head(kh, g, gw).astype(BF16), per_kv)
        vd = _head_diagonal(_spread_head(vv, g, gw).astype(BF16), per_kv)
        qg = qh[:, g * gw:(g + 1) * gw]
        s_all = _dot(qg, kd, NT) * scale
        ps, p_sinks = [], []
        for r in range(per_kv):
            s = jnp.where(valid, s_all[:, r * keys:(r + 1) * keys], -1e30)
            p, p_sink = _softmax_with_sink(s, sink_ref[g * per_kv + r])
            ps.append(p)
            p_sinks.append(p_sink)
        pb = jnp.concatenate(ps, axis=1).astype(BF16)
        groups.append((kd, vd, qg, ps, p_sinks, pb, _dot(pb, vd, NN)))
    return qn, q_rstd, kn, k_rstd, groups


def _attention_fwd(proj, qw_row, kw_row, gmat, sinks, *, attn_w, kv_w, name):
    rows = proj.shape[0]
    per_kv = attn_w // HEAD_DIM // N_KV_HEADS

    def body(q_ref, kp_ref, kc_ref, vp_ref, vc_ref, glo_ref, ghi_ref, qw_ref, kw_ref, gm_ref, sink_ref, o_ref):
        kk = jnp.concatenate([kp_ref[...], kc_ref[...]], axis=0).astype(F32)
        vv = jnp.concatenate([vp_ref[...], vc_ref[...]], axis=0).astype(F32)
        gate = jnp.concatenate([glo_ref[...], ghi_ref[...]], axis=1).astype(F32)
        *_, groups = _attn_block(pl.program_id(0), q_ref[...].astype(F32), kk, vv, qw_ref[...], kw_ref[...], gm_ref[...],
                                 sink_ref, per_kv)
        attn = jnp.concatenate([grp[-1] for grp in groups], axis=1)
        o_ref[...] = (attn * _silu(gate)).astype(BF16)

    const = lambda a: pl.BlockSpec(a.shape, lambda i: (0, 0))
    return _pallas(
        body, name=name, out_shape=jax.ShapeDtypeStruct((rows, attn_w), BF16), grid=(rows // BLOCK,),
        in_specs=_attn_specs(attn_w, kv_w) + [const(qw_row), const(kw_row), const(gmat),
                                              pl.BlockSpec(memory_space=pltpu.SMEM)],
        out_specs=pl.BlockSpec((BLOCK, attn_w), lambda i: (i, 0)),
        compiler_params=_params(("parallel",), 40 * 1024 * 1024),
    )(proj, proj, proj, proj, proj, proj, proj, qw_row, kw_row, gmat, sinks)


def _attention_bwd(proj, d_ag, dproj, qw_row, kw_row, gmat, sinks, *, attn_w, kv_w, name, rider=None):
    rows = proj.shape[0]
    nb = rows // BLOCK
    per_kv = attn_w // HEAD_DIM // N_KV_HEADS
    gw = per_kv * HEAD_DIM
    keys = 2 * BLOCK
    scale = 1.0 / math.sqrt(HEAD_DIM)
    w_out = 2 * attn_w + 2 * kv_w

    def body(q_ref, kp_ref, kc_ref, vp_ref, vc_ref, glo_ref, ghi_ref, dag_ref, qw_ref, kw_ref, gm_ref, sink_ref, _,
             dp_ref, dkv_ref, gqw_ref, gkw_ref, gs_ref):
        i = pl.program_id(0)
        kk = jnp.concatenate([kp_ref[...], kc_ref[...]], axis=0).astype(F32)
        vv = jnp.concatenate([vp_ref[...], vc_ref[...]], axis=0).astype(F32)
        gate = jnp.concatenate([glo_ref[...], ghi_ref[...]], axis=1).astype(F32)
        d_ag_v = dag_ref[...].astype(F32)
        qw, kw, gmat_v = qw_ref[...], kw_ref[...], gm_ref[...]
        qn, q_rstd, kn, k_rstd, groups = _attn_block(i, q_ref[...].astype(F32), kk, vv, qw, kw, gmat_v, sink_ref,
                                                     per_kv)
        lane = lax.broadcasted_iota(jnp.int32, (SUBLANES, LANES), 1)
        sub = lax.broadcasted_iota(jnp.int32, (SUBLANES, LANES), 0)
        gsink = jnp.zeros((SUBLANES, LANES), F32)
        dq_groups, dgate_groups, dk_heads, dv_heads = [], [], [], []
        for g, (kd, vd, qg, ps, p_sinks, pb, o) in enumerate(groups):
            cs = slice(g * gw, (g + 1) * gw)
            gate_g, d_ag_g = gate[:, cs], d_ag_v[:, cs]
            dgate_groups.append(d_ag_g * o * _dsilu(gate_g))
            do = (d_ag_g * _silu(gate_g)).astype(BF16)
            dp_all = _dot(do, vd, NT)
            dss = []
            for r in range(per_kv):
                p, dp = ps[r], dp_all[:, r * keys:(r + 1) * keys]
                delta = jnp.sum(p * dp, axis=-1, keepdims=True)
                dss.append(p * (dp - delta) * scale)
                gs_h = jnp.sum(-p_sinks[r] * delta, axis=0, keepdims=True)
                gsink = gsink + jnp.where((lane == g * per_kv + r) & (sub == 0), gs_h, 0.0)
            ds = jnp.concatenate(dss, axis=1).astype(BF16)
            dq_groups.append(_dot(ds, kd, NN))
            dk_heads.append(_fold_heads(_dot(ds, qg, TN), per_kv))
            dv_heads.append(_fold_heads(_dot(pb, do, TN), per_kv))
        dqh = jnp.concatenate(dq_groups, axis=1)
        gqw = jnp.sum(dqh * qn, axis=0, keepdims=True)
        dqn = dqh * qw
        dq = q_rstd * (dqn - qn * _head_mean(dqn * qn, gmat_v))
        dkh = _join_heads(dk_heads)
        gkw = jnp.sum(dkh * kn, axis=0, keepdims=True)
        dkn = dkh * kw
        dk = k_rstd * (dkn - kn * _head_mean(dkn * kn, gmat_v))
        dp_ref[:, 0:attn_w] = dq.astype(BF16)
        dp_ref[:, attn_w:attn_w + 2 * kv_w] = jnp.zeros((BLOCK, 2 * kv_w), BF16)
        dp_ref[:, attn_w + 2 * kv_w:w_out] = jnp.concatenate(dgate_groups, axis=1).astype(BF16)
        dkv_ref[0] = jnp.concatenate([dk, _join_heads(dv_heads)], axis=1)

        @pl.when(i == 0)
        def _():
            gqw_ref[...] = gqw
            gkw_ref[...] = gkw
            gs_ref[...] = gsink

        @pl.when(i > 0)
        def _():
            gqw_ref[...] += gqw
            gkw_ref[...] += gkw
            gs_ref[...] += gsink

    const = lambda a: pl.BlockSpec(a.shape, lambda i: (0, 0))
    res, landed = _call(
        body, [proj, proj, proj, proj, proj, proj, proj, d_ag, qw_row, kw_row, gmat, sinks, dproj], name=name,
        out_shape=[jax.ShapeDtypeStruct(dproj.shape, BF16),
                   jax.ShapeDtypeStruct((nb, 2 * BLOCK, 2 * kv_w), F32),
                   jax.ShapeDtypeStruct(qw_row.shape, F32), jax.ShapeDtypeStruct(kw_row.shape, F32),
                   jax.ShapeDtypeStruct((SUBLANES, LANES), F32)],
        grid=(nb,),
        in_specs=_attn_specs(attn_w, kv_w) + [pl.BlockSpec((BLOCK, attn_w), lambda i: (i, 0)), const(qw_row),
                                              const(kw_row), const(gmat), pl.BlockSpec(memory_space=pltpu.SMEM),
                                              _any_spec()],
        out_specs=[pl.BlockSpec((BLOCK, w_out), lambda i: (i, 0)),
                   pl.BlockSpec((1, 2 * BLOCK, 2 * kv_w), lambda i: (i, 0, 0)),
                   const(qw_row), const(kw_row), pl.BlockSpec((SUBLANES, LANES), lambda i: (0, 0))],
        aliases={12: 0}, semantics=("arbitrary",), vmem=48 * 1024 * 1024, rider=rider)
    return res if rider is None else (res, landed)


def _attention_dkv(dproj, dkv, *, attn_w, kv_w, name):
    rows = dproj.shape[0]
    nb = rows // BLOCK
    col = attn_w // (2 * kv_w)

    def body(cur_ref, nxt_ref, _, o_ref):
        i = pl.program_id(0)
        nxt = jnp.where(i < nb - 1, nxt_ref[0, 0:BLOCK, :], 0.0)
        o_ref[...] = (cur_ref[0, BLOCK:2 * BLOCK, :] + nxt).astype(BF16)

    blk = lambda f: pl.BlockSpec((1, 2 * BLOCK, 2 * kv_w), f)
    return _pallas(
        body, name=name, out_shape=jax.ShapeDtypeStruct(dproj.shape, BF16), grid=(nb,),
        in_specs=[blk(lambda i: (i, 0, 0)), blk(lambda i: (jnp.minimum(i + 1, nb - 1), 0, 0)), _any_spec()],
        out_specs=pl.BlockSpec((BLOCK, 2 * kv_w), lambda i: (i, col)),
        input_output_aliases={2: 0},
        compiler_params=_params(("parallel",)),
    )(dkv, dkv, dproj)


def _cmul(ar, ai, br, bi):
    return ar * br - ai * bi, ar * bi + ai * br


def _ssm_prep(a_re, a_im, log_dt_col, steps, name):
    assert steps & (steps - 1) == 0

    def body(are_ref, aim_ref, ldt_ref, abr_ref, abi_ref, cfr_ref, cfi_ref, apr_ref, api_ref):
        are, aim = are_ref[...], aim_ref[...]
        dt = jnp.exp(ldt_ref[...])
        mag = jnp.exp(dt * are)
        abr = mag * jnp.cos(dt * aim)
        abi = mag * jnp.sin(dt * aim)
        num_re, num_im = abr - 1.0, abi
        den = are * are + aim * aim
        abr_ref[...] = abr
        abi_ref[...] = abi
        cfr_ref[...] = (num_re * are + num_im * aim) / den
        cfi_ref[...] = (num_im * are - num_re * aim) / den
        pr, pi = abr, abi
        n = steps
        while n > 1:
            pr, pi = _cmul(pr, pi, pr, pi)
            n //= 2
        apr_ref[...] = pr
        api_ref[...] = pi

    shp = jax.ShapeDtypeStruct(a_re.shape, F32)
    return _pallas(body, name=name, out_shape=[shp] * 6)(a_re, a_im, log_dt_col)


def _ssm_param_bwd(a_re, a_im, log_dt_col, d_ab_re, d_ab_im, d_cf_re, d_cf_im, name):
    def body(are_ref, aim_ref, ldt_ref, gabr_ref, gabi_ref, gcfr_ref, gcfi_ref, dar_ref, dai_ref, dldt_ref):
        are, aim = are_ref[...], aim_ref[...]
        dt = jnp.exp(ldt_ref[...])
        mag = jnp.exp(dt * are)
        abr = mag * jnp.cos(dt * aim)
        abi = mag * jnp.sin(dt * aim)
        den = are * are + aim * aim
        cfr = ((abr - 1.0) * are + abi * aim) / den
        cfi = (abi * are - (abr - 1.0) * aim) / den
        gabr, gabi = jnp.sum(gabr_ref[...], axis=0), jnp.sum(gabi_ref[...], axis=0)
        gcfr, gcfi = jnp.sum(gcfr_ref[...], axis=0), jnp.sum(gcfi_ref[...], axis=0)
        inv_r, inv_i = are / den, -aim / den
        t_r, t_i = _cmul(inv_r, -inv_i, gcfr, gcfi)
        gabr, gabi = gabr + t_r, gabi + t_i
        q_r, q_i = _cmul(cfr, cfi, inv_r, inv_i)
        da_r, da_i = _cmul(-q_r, q_i, gcfr, gcfi)
        gz_r, gz_i = _cmul(abr, -abi, gabr, gabi)
        dar_ref[...] = da_r + dt * gz_r
        dai_ref[...] = da_i + dt * gz_i
        dldt_ref[...] = dt * jnp.sum(are * gz_r + aim * gz_i, axis=-1, keepdims=True)

    shp = jax.ShapeDtypeStruct(a_re.shape, F32)
    return _pallas(body, name=name, out_shape=[shp, shp, jax.ShapeDtypeStruct(log_dt_col.shape, F32)])(
        a_re, a_im, log_dt_col, d_ab_re, d_ab_im, d_cf_re, d_cf_im)


SCAN_LANES = 512
SSM_CHUNK = 256
W_IN_PARTS = 2


def _scan_segments(xr_ref, xi_ref, a_re, a_im, ap_re, ap_im, carry_re, carry_im, cm_re, cm_im, steps, reverse, base):
    n = xr_ref.shape[1]
    seg_order = range(SUBLANES - 1, -1, -1) if reverse else range(SUBLANES)
    for c0 in range(0, n, SCAN_LANES):
        ls = slice(c0, c0 + SCAN_LANES)
        gs = slice(base + c0, base + c0 + SCAN_LANES)
        ar = jnp.broadcast_to(a_re[:, gs], (SUBLANES, SCAN_LANES))
        ai = jnp.broadcast_to(a_im[:, gs], (SUBLANES, SCAN_LANES))

        def local(t, s, ar=ar, ai=ai, ls=ls):
            j = steps - 1 - t if reverse else t
            r0 = pl.multiple_of(j * SUBLANES, SUBLANES)
            sr, si = _cmul(ar, ai, s[0], s[1])
            sr = sr + xr_ref[pl.ds(r0, SUBLANES), ls]
            si = si + xi_ref[pl.ds(r0, SUBLANES), ls]
            xr_ref[pl.ds(r0, SUBLANES), ls] = sr
            xi_ref[pl.ds(r0, SUBLANES), ls] = si
            return sr, si

        zero = jnp.zeros((SUBLANES, SCAN_LANES), F32)
        end_r, end_i = lax.fori_loop(0, steps, local, (zero, zero))
        cr, ci = carry_re[:, gs], carry_im[:, gs]
        apr, api = ap_re[:, gs], ap_im[:, gs]
        for r in seg_order:
            cm_re[r:r + 1, gs] = cr
            cm_im[r:r + 1, gs] = ci
            tr, ti = _cmul(apr, api, cr, ci)
            cr, ci = end_r[r:r + 1, :] + tr, end_i[r:r + 1, :] + ti
        carry_re[:, gs] = cr
        carry_im[:, gs] = ci

        def fix(t, s, ar=ar, ai=ai, ls=ls):
            j = steps - 1 - t if reverse else t
            r0 = pl.multiple_of(j * SUBLANES, SUBLANES)
            sr, si = _cmul(ar, ai, s[0], s[1])
            xr_ref[pl.ds(r0, SUBLANES), ls] += sr
            xi_ref[pl.ds(r0, SUBLANES), ls] += si
            return sr, si

        lax.fori_loop(0, steps, fix, (cm_re[:, gs], cm_im[:, gs]))


SB_GROUPS = MXU_DIM // GROUP
SB_STATE = SB_GROUPS * STATE


def _ssm_rows(b_re, b_im, c_re, c_im):
    def rows(m):
        flat = m.reshape(-1, STATE).astype(F32)
        return jnp.concatenate([flat, flat], axis=1)
    return rows(b_re.transpose(0, 2, 1)), rows(b_im.transpose(0, 2, 1)), rows(c_re), rows(c_im)


def _from_ssm_rows(rows, transpose):
    g = rows[:, :STATE].reshape(-1, GROUP, STATE)
    return g.transpose(0, 2, 1) if transpose else g


def _own_group(shape):
    row_g = lax.broadcasted_iota(jnp.int32, shape, 0) // GROUP
    col_g = lax.broadcasted_iota(jnp.int32, shape, 1) // STATE
    return row_g == col_g


def _block_diagonal(rows):
    tiled = jnp.concatenate([rows] * (SB_STATE // LANES), axis=1)
    return jnp.where(_own_group(tiled.shape), tiled, 0.0).astype(BF16)


def _block_rows(acc):
    x = jnp.where(_own_group(acc.shape), acc, 0.0)
    while x.shape[1] > LANES:
        half = x.shape[1] // 2
        x = x[:, :half] + x[:, half:]
    return x + pltpu.roll(x, STATE, axis=1)


def _rows_to_segments(dst, srcs, steps, stage):
    for ref, off in srcs:
        for k in range(ref.shape[1] // LANES):
            stage[off // LANES + k] = ref[:, k * LANES:(k + 1) * LANES].astype(F32)
    for k in range(dst.shape[1] // LANES):
        for j in range(steps):
            dst[j * SUBLANES:(j + 1) * SUBLANES, k * LANES:(k + 1) * LANES] = (
                stage[k, pl.ds(j, SUBLANES, stride=steps), :])


def _segments_to_rows(dst, src, steps, stage):
    for k in range(src.shape[1] // LANES):
        for j in range(steps):
            stage[k, pl.ds(j, SUBLANES, stride=steps), :] = (
                src[j * SUBLANES:(j + 1) * SUBLANES, k * LANES:(k + 1) * LANES])
    for k in range(src.shape[1] // LANES):
        dst[:, k * LANES:(k + 1) * LANES] = stage[k]


def _u_specs(w, o_u, chunk, index):
    half = w // 2
    assert o_u % half == 0
    return [pl.BlockSpec((chunk, half), lambda c, k=k: (index(c), o_u // half + k)) for k in range(2)]


def _ssm_fwd(proj, o_u, bc_rows, rows_p, d_row, *, chunk, name, rider=None):
    rows = proj.shape[0]
    w = d_row.shape[1]
    nc = rows // chunk
    steps = chunk // SUBLANES
    nsb = w // MXU_DIM
    n_state = nsb * SB_STATE

    def body(ulo_ref, uhi_ref, b2r_ref, b2i_ref, c2r_ref, c2i_ref, abr_ref, abi_ref, cfr_ref, cfi_ref, apr_ref,
             api_ref, d_ref, y_ref, str_ref, sti_ref, yg_ref, bre_ref, bim_ref, cre_ref, cim_ref, useg, yseg, stage, sr, si,
             carry_r, carry_i, cm_r, cm_i):
        @pl.when(pl.program_id(0) == 0)
        def _():
            for src, dst in ((b2r_ref, bre_ref), (b2i_ref, bim_ref), (c2r_ref, cre_ref), (c2i_ref, cim_ref)):
                for sb in range(nsb):
                    dst[sb] = _block_diagonal(src[sb * MXU_DIM:(sb + 1) * MXU_DIM, :])
            carry_r[...] = jnp.zeros_like(carry_r)
            carry_i[...] = jnp.zeros_like(carry_i)

        str_ref[0] = carry_r[...]
        sti_ref[0] = carry_i[...]
        _rows_to_segments(useg, [(ulo_ref, 0), (uhi_ref, w // 2)], steps, stage)
        for sb in range(nsb):
            us = slice(sb * MXU_DIM, (sb + 1) * MXU_DIM)
            ss = slice(sb * SB_STATE, (sb + 1) * SB_STATE)
            ub = useg[:, us].astype(BF16)
            bur = _dot(ub, bre_ref[sb], NN)
            bui = _dot(ub, bim_ref[sb], NN)
            xr, xi = _cmul(cfr_ref[:, ss], cfi_ref[:, ss], bur, bui)
            sr[...] = xr
            si[...] = xi
            _scan_segments(sr, si, abr_ref[...], abi_ref[...], apr_ref[...], api_ref[...],
                           carry_r, carry_i, cm_r, cm_i, steps, False, sb * SB_STATE)
            y = _dot(sr[...].astype(BF16), cre_ref[sb], NT) - _dot(si[...].astype(BF16), cim_ref[sb], NT)
            yseg[:, us] = y + d_ref[:, us] * useg[:, us]
        _segments_to_rows(y_ref, yseg, steps, stage)
        yg_ref[...] = _gelu(y_ref[...]).astype(BF16)

    const = lambda a: pl.BlockSpec(a.shape, lambda c: (0,) * a.ndim)
    row_n = pl.BlockSpec((1, n_state), lambda c: (0, 0))
    st = pl.BlockSpec((1, 1, n_state), lambda c: (c, 0, 0))
    held = [pltpu.VMEM((nsb, MXU_DIM, SB_STATE), BF16)] * 4
    vmem = (4 * _nbytes((nsb, MXU_DIM, SB_STATE), BF16) + 4 * _nbytes((chunk, SB_STATE), F32)
            + 12 * _nbytes((chunk, w), F32) + 8 * _nbytes(bc_rows[0].shape, F32))
    res, landed = _call(
        body, [proj, proj, *bc_rows, *rows_p, d_row], name=name,
        out_shape=[jax.ShapeDtypeStruct((rows, w), F32), jax.ShapeDtypeStruct((nc, 1, n_state), F32),
                   jax.ShapeDtypeStruct((nc, 1, n_state), F32), jax.ShapeDtypeStruct((rows, w), BF16)],
        grid=(nc,),
        in_specs=_u_specs(w, o_u, chunk, lambda c: c) + [const(b) for b in bc_rows]
        + [row_n] * 6 + [pl.BlockSpec((1, w), lambda c: (0, 0))],
        out_specs=[pl.BlockSpec((chunk, w), lambda c: (c, 0)), st, st, pl.BlockSpec((chunk, w), lambda c: (c, 0))],
        scratch_shapes=held + [pltpu.VMEM((chunk, w), F32), pltpu.VMEM((chunk, w), F32),
                               pltpu.VMEM((w // LANES, chunk, LANES), F32),
                               pltpu.VMEM((chunk, SB_STATE), F32), pltpu.VMEM((chunk, SB_STATE), F32),
                               pltpu.VMEM((1, n_state), F32), pltpu.VMEM((1, n_state), F32),
                               pltpu.VMEM((SUBLANES, n_state), F32), pltpu.VMEM((SUBLANES, n_state), F32)],
        semantics=("arbitrary",), vmem=vmem, rider=rider)
    return res if rider is None else (res, landed)


def _ssm_bwd(proj, o_u, y, dyg, st_re, st_im, bc_rows, rows_p, d_row, *, chunk, name, rider=None):
    rows = proj.shape[0]
    w = d_row.shape[1]
    nc = rows // chunk
    steps = chunk // SUBLANES
    nsb = w // MXU_DIM
    n_state = nsb * SB_STATE

    def body(ulo_ref, uhi_ref, y_ref, dyg_ref, str_ref, sti_ref, b2r_ref, b2i_ref, c2r_ref, c2i_ref,
             abr_ref, abi_ref, cfr_ref, cfi_ref, apr_ref, api_ref, d_ref,
             du_ref, gb2r_ref, gb2i_ref, gc2r_ref, gc2i_ref, gabr_ref, gabi_ref, gcfr_ref, gcfi_ref, dd_ref,
             bre_ref, bim_ref, cre_ref, cim_ref, dbre_ref, dbim_ref, dcre_ref, dcim_ref, useg, dyseg, dynat, stage,
             bur, bui, sr, si, lr, li, carry_r, carry_i, lam_r, lam_i, cm_r, cm_i, cl_r, cl_i):
        first = pl.program_id(0) == 0

        @pl.when(first)
        def _():
            for src, dst in ((b2r_ref, bre_ref), (b2i_ref, bim_ref), (c2r_ref, cre_ref), (c2i_ref, cim_ref)):
                for sb in range(nsb):
                    dst[sb] = _block_diagonal(src[sb * MXU_DIM:(sb + 1) * MXU_DIM, :])
            lam_r[...] = jnp.zeros_like(lam_r)
            lam_i[...] = jnp.zeros_like(lam_i)
            for ref in (dbre_ref, dbim_ref, dcre_ref, dcim_ref, gabr_ref, gabi_ref, gcfr_ref, gcfi_ref, dd_ref):
                ref[...] = jnp.zeros_like(ref)

        dynat[...] = dyg_ref[...].astype(F32) * _dgelu(y_ref[...])
        half = w // 2
        dd_ref[:, :half] += jnp.sum(dynat[:, :half] * ulo_ref[...].astype(F32), axis=0, keepdims=True)
        dd_ref[:, half:] += jnp.sum(dynat[:, half:] * uhi_ref[...].astype(F32), axis=0, keepdims=True)
        _rows_to_segments(useg, [(ulo_ref, 0), (uhi_ref, half)], steps, stage)
        _rows_to_segments(dyseg, [(dynat, 0)], steps, stage)
        dy = dyseg[...]
        dyb = dy.astype(BF16)
        ub = useg[...].astype(BF16)
        carry_r[...] = str_ref[0]
        carry_i[...] = sti_ref[0]
        abr, abi = abr_ref[...], abi_ref[...]
        apr, api = apr_ref[...], api_ref[...]
        for sb in range(nsb):
            us = slice(sb * MXU_DIM, (sb + 1) * MXU_DIM)
            ss = slice(sb * SB_STATE, (sb + 1) * SB_STATE)
            base = sb * SB_STATE
            br = _dot(ub[:, us], bre_ref[sb], NN)
            bi = _dot(ub[:, us], bim_ref[sb], NN)
            bur[...] = br
            bui[...] = bi
            xr, xi = _cmul(cfr_ref[:, ss], cfi_ref[:, ss], br, bi)
            sr[...] = xr
            si[...] = xi
            lr[...] = _dot(dyb[:, us], cre_ref[sb], NN)
            li[...] = -_dot(dyb[:, us], cim_ref[sb], NN)
            _scan_segments(sr, si, abr, abi, apr, api, carry_r, carry_i, cm_r, cm_i, steps, False, base)
            dcre_ref[sb] += _dot(dyb[:, us], sr[...].astype(BF16), TN)
            dcim_ref[sb] -= _dot(dyb[:, us], si[...].astype(BF16), TN)
            _scan_segments(lr, li, abr, -abi, apr, -api, lam_r, lam_i, cl_r, cl_i, steps, True, base)
            for c0 in range(0, SB_STATE, SCAN_LANES):
                ls = slice(c0, c0 + SCAN_LANES)
                gs = slice(base + c0, base + c0 + SCAN_LANES)
                cfr = jnp.broadcast_to(cfr_ref[:, gs], (SUBLANES, SCAN_LANES))
                cfi = jnp.broadcast_to(cfi_ref[:, gs], (SUBLANES, SCAN_LANES))

                def step(j, acc, ls=ls, cfr=cfr, cfi=cfi):
                    gar, gai, gcr, gci, pr, pi = acc
                    r0 = pl.multiple_of(j * SUBLANES, SUBLANES)
                    rws = pl.ds(r0, SUBLANES)
                    l_r, l_i = lr[rws, ls], li[rws, ls]
                    t_r, t_i = _cmul(pr, -pi, l_r, l_i)
                    b_r, b_i = bur[rws, ls], bui[rws, ls]
                    c_r, c_i = _cmul(b_r, -b_i, l_r, l_i)
                    x_r, x_i = _cmul(cfr, -cfi, l_r, l_i)
                    bur[rws, ls] = x_r
                    bui[rws, ls] = x_i
                    return gar + t_r, gai + t_i, gcr + c_r, gci + c_i, sr[rws, ls], si[rws, ls]

                zero = jnp.zeros((SUBLANES, SCAN_LANES), F32)
                gar, gai, gcr, gci, _, _ = lax.fori_loop(
                    0, steps, step, (zero, zero, zero, zero, cm_r[:, gs], cm_i[:, gs]))
                gabr_ref[:, gs] += gar
                gabi_ref[:, gs] += gai
                gcfr_ref[:, gs] += gcr
                gcfi_ref[:, gs] += gci
            xr, xi = bur[...].astype(BF16), bui[...].astype(BF16)
            du = _dot(xr, bre_ref[sb], NT) + _dot(xi, bim_ref[sb], NT)
            useg[:, us] = du + d_ref[:, us] * dy[:, us]
            dbre_ref[sb] += _dot(ub[:, us], xr, TN)
            dbim_ref[sb] += _dot(ub[:, us], xi, TN)
        _segments_to_rows(du_ref, useg, steps, stage)

        @pl.when(pl.program_id(0) == nc - 1)
        def _():
            for src, dst in ((dbre_ref, gb2r_ref), (dbim_ref, gb2i_ref), (dcre_ref, gc2r_ref), (dcim_ref, gc2i_ref)):
                for sb in range(nsb):
                    dst[sb * MXU_DIM:(sb + 1) * MXU_DIM, :] = _block_rows(src[sb])

    rev = lambda c: nc - 1 - c
    const = lambda a: pl.BlockSpec(a.shape, lambda c: (0,) * a.ndim)
    tile = pl.BlockSpec((chunk, w), lambda c: (rev(c), 0))
    row_n = pl.BlockSpec((1, n_state), lambda c: (0, 0))
    row_w = pl.BlockSpec((1, w), lambda c: (0, 0))
    st = pl.BlockSpec((1, 1, n_state), lambda c: (rev(c), 0, 0))
    acc8 = pl.BlockSpec((SUBLANES, n_state), lambda c: (0, 0))
    big = pltpu.VMEM((chunk, SB_STATE), F32)
    small = pltpu.VMEM((chunk, w), F32)
    row = pltpu.VMEM((1, n_state), F32)
    eight = pltpu.VMEM((SUBLANES, n_state), F32)
    blk = (nsb, MXU_DIM, SB_STATE)
    held = [pltpu.VMEM(blk, BF16)] * 4 + [pltpu.VMEM(blk, F32)] * 4
    vmem = (4 * (_nbytes(blk, BF16) + _nbytes(blk, F32)) + 7 * _nbytes((chunk, SB_STATE), F32)
            + 12 * _nbytes((chunk, w), F32) + 16 * _nbytes(bc_rows[0].shape, F32))
    res, landed = _call(
        body, [proj, proj, y, dyg, st_re, st_im, *bc_rows, *rows_p, d_row], name=name,
        out_shape=[jax.ShapeDtypeStruct((rows, w), F32)] + [jax.ShapeDtypeStruct(b.shape, F32) for b in bc_rows]
        + [jax.ShapeDtypeStruct((SUBLANES, n_state), F32)] * 4 + [jax.ShapeDtypeStruct((1, w), F32)],
        grid=(nc,),
        in_specs=_u_specs(w, o_u, chunk, rev) + [tile, tile, st, st] + [const(b) for b in bc_rows]
        + [row_n] * 6 + [row_w],
        out_specs=[tile] + [const(b) for b in bc_rows] + [acc8] * 4 + [row_w],
        scratch_shapes=held + [small] * 3 + [pltpu.VMEM((w // LANES, chunk, LANES), F32)] + [big] * 6 + [row] * 4
        + [eight] * 4,
        semantics=("arbitrary",), vmem=vmem, rider=rider)
    return res if rider is None else (res, landed)


def _loss_grad(x, mm, target, name):
    rows, d = x.shape

    def fn(xv, mv, tv):
        err = xv + mv - tv
        g = err * (1.0 / d)
        return g, g, 0.5 * err * g

    return _ew(fn, name=name, rows=rows, width=d, tiles=[(x, 0), (mm, 0), (target, 0)],
               outs=[(F32, d, 0), (BF16, d, 0)], accs=1)


def _pair_sum(grad, recv, name):
    r4, cdim = recv.shape
    r = r4 // N_CHIPS
    tr = _tile(r, 544, 16)
    g4 = grad.reshape(N_CHIPS, 2, r, cdim)
    r3 = recv.reshape(N_CHIPS, r, cdim)
    core = jnp.reshape(lax.axis_index("c"), (1,)).astype(jnp.int32)

    def body(c_ref, g_ref, r_ref, o_ref):
        o_ref[...] = (g_ref[0] + r_ref[...]).astype(BF16)

    out = _pallas(
        body, name=name, out_shape=jax.ShapeDtypeStruct((N_CHIPS, r, cdim), BF16),
        grid_spec=pltpu.PrefetchScalarGridSpec(
            num_scalar_prefetch=1, grid=(N_CHIPS, r // tr),
            in_specs=[pl.BlockSpec((1, 1, tr, cdim), lambda j, i, c: (j, c[0], i, 0)),
                      pl.BlockSpec((1, tr, cdim), lambda j, i, c: (j, i, 0))],
            out_specs=pl.BlockSpec((1, tr, cdim), lambda j, i, c: (j, i, 0))),
        compiler_params=_params(("parallel", "parallel"), 6 * _nbytes((tr, cdim), F32)),
    )(core, g4, r3)
    return out.reshape(r4, cdim)


def _chip_sum(recv, name):
    r4, cdim = recv.shape
    r = r4 // N_CHIPS
    tr = _tile(r, 544, 16)
    r3 = recv.reshape(N_CHIPS, r, cdim)

    def body(r_ref, o_ref):
        acc = r_ref[0].astype(F32)
        for j in range(1, N_CHIPS):
            acc = acc + r_ref[j].astype(F32)
        o_ref[...] = acc

    return _pallas(
        body, name=name, out_shape=jax.ShapeDtypeStruct((r, cdim), F32), grid=(r // tr,),
        in_specs=[pl.BlockSpec((N_CHIPS, tr, cdim), lambda i: (0, i, 0))],
        out_specs=pl.BlockSpec((tr, cdim), lambda i: (i, 0)),
        compiler_params=_params(("parallel",), 8 * _nbytes((tr, cdim), F32)),
    )(r3)


def _adamw_math(w, g, m, v):
    m = ADAM_B1 * m + (1.0 - ADAM_B1) * g
    v = ADAM_B2 * v + (1.0 - ADAM_B2) * (g * g)
    m_hat = m / (1.0 - ADAM_B1 ** ADAM_STEP)
    v_hat = v / (1.0 - ADAM_B2 ** ADAM_STEP)
    delta = -ADAM_LR * (m_hat / (jnp.sqrt(v_hat) + ADAM_EPS) + ADAM_WD * w)
    return delta, m, v


def _adamw(w, g, m, v, name):
    rows, cols = w.shape
    tr = _tile(rows, 256, SUBLANES)

    def body(w_ref, g_ref, m_ref, v_ref, d_ref, nm_ref, nv_ref):
        d, nm, nv = _adamw_math(w_ref[...], g_ref[...], m_ref[...], v_ref[...])
        d_ref[...] = d
        nm_ref[...] = nm
        nv_ref[...] = nv

    spec = pl.BlockSpec((tr, cols), lambda i: (i, 0))
    shp = jax.ShapeDtypeStruct((rows, cols), F32)
    return _pallas(
        body, name=name, out_shape=[shp] * 3, grid=(rows // tr,), in_specs=[spec] * 4, out_specs=[spec] * 3,
        compiler_params=_params(("parallel",)),
    )(w, g, m, v)


def _adamw_chips(w, parts, m, v, name):
    rows, cols = w.shape
    assert sum(p.shape[1] for p in parts) == cols
    tr = _tile(rows, 64, 16)
    n = len(parts)

    def body(*refs):
        w_ref, m_ref, v_ref = refs[0], refs[1 + n], refs[2 + n]
        g_ref, d_ref, nm_ref, nv_ref = refs[3 + n:]
        cols_g = []
        for p_ref in refs[1:1 + n]:
            acc = p_ref[0].astype(F32)
            for j in range(1, N_CHIPS):
                acc = acc + p_ref[j].astype(F32)
            cols_g.append(acc)
        g = cols_g[0] if n == 1 else jnp.concatenate(cols_g, axis=1)
        d, nm, nv = _adamw_math(w_ref[...], g, m_ref[...], v_ref[...])
        g_ref[...] = g
        d_ref[...] = d
        nm_ref[...] = nm
        nv_ref[...] = nv

    spec = pl.BlockSpec((tr, cols), lambda i: (i, 0))
    part_specs = [pl.BlockSpec((N_CHIPS, tr, p.shape[1]), lambda i: (0, i, 0)) for p in parts]
    shp = jax.ShapeDtypeStruct((rows, cols), F32)
    return _pallas(
        body, name=name, out_shape=[shp] * 4, grid=(rows // tr,),
        in_specs=[spec] + part_specs + [spec, spec], out_specs=[spec] * 4,
        compiler_params=_params(("parallel",)),
    )(w, *[p.reshape(N_CHIPS, rows, p.shape[1]) for p in parts], m, v)


def _adamw_small(w, parts, m, v, name):
    rows, cols = w.shape
    p3 = parts.reshape(N_DEV, rows, cols)

    def body(w_ref, p_ref, m_ref, v_ref, g_ref, d_ref, nm_ref, nv_ref):
        g = p_ref[0]
        for k in range(1, N_DEV):
            g = g + p_ref[k]
        d, nm, nv = _adamw_math(w_ref[...], g, m_ref[...], v_ref[...])
        g_ref[...] = g
        d_ref[...] = d
        nm_ref[...] = nm
        nv_ref[...] = nv

    shp = jax.ShapeDtypeStruct((rows, cols), F32)
    return _pallas(body, name=name, out_shape=[shp] * 4)(w, p3, m, v)


SMALL = ("norm_w", "q_norm_w", "k_norm_w", "sinks", "A_re", "A_im", "log_dt", "B_re", "B_im", "C_re", "C_im",
         "D_skip", "b_glu")
LARGE = ("w_in", "w_attn_proj", "w_glu", "w_ssm_proj", "w_out")
ORDER = ("norm_w", "w_in", "q_norm_w", "k_norm_w", "sinks", "w_attn_proj", "A_re", "A_im", "log_dt", "B_re", "B_im",
         "C_re", "C_im", "D_skip", "w_glu", "b_glu", "w_ssm_proj", "w_out")


SMALL_REST = ("loss",) + SMALL[1:]


def _pack(named, keys):
    flat = jnp.concatenate([named[k].reshape(-1).astype(F32) for k in keys])
    n = flat.shape[0]
    rows = -(-n // (LANES * SUBLANES)) * SUBLANES
    return jnp.pad(flat, (0, rows * LANES - n)).reshape(rows, LANES)


def _unpack(packed, like, keys):
    flat = packed.reshape(-1)
    out, o = {}, 0
    for k in keys:
        n = like[k].size
        out[k] = flat[o:o + n].reshape(like[k].shape)
        o += n
    return out


def _step(xs, target, p, shards):
    s_in, s_ap, s_glu, s_sp, s_o = shards
    seq, d = xs.shape
    attn_w = (d // 128) * HEAD_DIM
    n_q = attn_w // HEAD_DIM
    kv_w = N_KV_HEADS * HEAD_DIM
    ssm_w = d // 2
    n_groups = ssm_w // GROUP
    n_state = n_groups * STATE
    in_w = N_DEV * s_in.shape[0]
    assert in_w == 2 * attn_w + 2 * kv_w + 2 * ssm_w + 2 * d
    o_u = 2 * attn_w + 2 * kv_w
    o_z = o_u + ssm_w
    o_ga = o_z + ssm_w
    chunk = min(SSM_CHUNK, seq)
    cw = d // 4

    norm_row = p["norm_w"].reshape(1, d)
    h = _rmsnorm_fwd(xs, norm_row, "rmsnorm_fwd")
    half = d // W_IN_PARTS
    assert W_IN_PARTS == 2
    s_in_parts = [s_in[:, :half], s_in[:, half:]]
    (w_lo,) = _exchange(_all_gather(s_in_parts[:1]), "gather_w_in_0")
    part, (w_hi,) = _matmul(Cols(h, 0, half), w_lo, mode="nt", name="in_proj_0", tn=512, out_dtype=BF16,
                            rider=_all_gather(s_in_parts[1:]))
    proj = _matmul(Cols(h, half, half), w_hi, mode="nt", name="in_proj_1", tn=512, out_dtype=BF16, add=part)
    w_in_parts = [w_lo, w_hi]
    qw_row = jnp.tile(p["q_norm_w"], n_q).reshape(1, attn_w)
    kw_row = jnp.tile(p["k_norm_w"], N_KV_HEADS).reshape(1, kv_w)
    gmat = _head_mean_matrix()
    ag = _attention_fwd(proj, qw_row, kw_row, gmat, p["sinks"], attn_w=attn_w, kv_w=kv_w, name="attention_fwd")

    log_dt_col = p["log_dt"].reshape(n_groups, 1)
    prep = _ssm_prep(p["A_re"], p["A_im"], log_dt_col, chunk // SUBLANES, "ssm_prep")
    rows_p = [v.reshape(1, n_state) for v in prep]
    bc_rows = _ssm_rows(p["B_re"], p["B_im"], p["C_re"], p["C_im"])
    d_row = p["D_skip"].reshape(1, ssm_w)
    (y_ssm, st_re, st_im, yg), (w_ap_t, w_glu_t, w_sp_t, w_o) = _ssm_fwd(
        proj, o_u, bc_rows, rows_p, d_row, chunk=chunk, name="ssm_fwd", rider=_all_gather([s_ap, s_glu, s_sp, s_o]))
    glu = _matmul(yg, w_glu_t, mode="nt", name="glu_proj", out_dtype=BF16, bias=p["b_glu"].reshape(1, 2 * ssm_w))
    (ts,) = _ew(lambda ga, gb, z: ga * _sigmoid(gb) * _silu(z), name="glu_gate", rows=seq, width=ssm_w,
                tiles=[(glu, 0), (glu, ssm_w), (proj, o_z)], outs=[(BF16, ssm_w, 0)], cw=cw)
    yy = _matmul(ag, w_ap_t, mode="nt", name="attn_proj", out_dtype=BF16, out_cols=(2 * d, 0))
    yy = _matmul(ts, w_sp_t, mode="nt", name="ssm_proj", out_dtype=BF16, out_cols=(2 * d, d), into=yy)
    (merged,) = _ew(lambda ya, ys, ga, gs: _sigmoid(ga) * ya + _sigmoid(gs) * ys, name="merge", rows=seq, width=d,
                    tiles=[(yy, 0), (yy, d), (proj, o_ga), (proj, o_ga + d)], outs=[(BF16, d, 0)], cw=cw)
    mm = _matmul(merged, w_o, mode="nn", name="out_proj")
    dout, dout_b, loss_cols = _loss_grad(xs, mm, target, "loss_grad")
    loss_local = jnp.sum(loss_cols)

    g_w_o = _matmul(merged, dout_b, mode="tn", name="grad_w_out", tm=512, tk=4096)
    dmerged, (sib_o,) = _matmul(dout_b, w_o, mode="nt", name="d_merged", out_dtype=BF16,
                                rider=_sibling_exchange([g_w_o]))
    pair_o = _pair_sum(g_w_o, sib_o, "pair_sum_w_out")

    def merge_bwd(dm, y, g):
        s = _sigmoid(g)
        return dm * s, dm * y * s * (1.0 - s)

    dyy, dproj = _ew(merge_bwd, name="merge_bwd", rows=seq, width=2 * d,
                     tiles=[(dmerged, 0, d), (yy, 0), (proj, o_ga)],
                     outs=[(BF16, 2 * d, 0), (BF16, in_w, o_ga)], cw=cw)
    dy_a, dy_s = Cols(dyy, 0, d), Cols(dyy, d, d)
    g_w_ap_t = _matmul(dy_a, ag, mode="tn", name="grad_w_attn_proj", tm=512, tk=4096)
    g_w_sp_t = _matmul(dy_s, ts, mode="tn", name="grad_w_ssm_proj", tm=512, tk=4096)
    d_ag = _matmul(dy_a, w_ap_t, mode="nn", name="d_attn_gated", out_dtype=BF16)
    d_ts = _matmul(dy_s, w_sp_t, mode="nn", name="d_ssm_gated", out_dtype=BF16)

    (dproj, dkv, g_qw, g_kw, g_sinks), (chips_o, sib_ap, sib_sp) = _attention_bwd(
        proj, d_ag, dproj, qw_row, kw_row, gmat, p["sinks"], attn_w=attn_w, kv_w=kv_w, name="attention_bwd",
        rider=_join(_chip_exchange([pair_o]), _sibling_exchange([g_w_ap_t, g_w_sp_t])))
    pair_ap = _pair_sum(g_w_ap_t, sib_ap, "pair_sum_w_attn_proj")
    pair_sp = _pair_sum(g_w_sp_t, sib_sp, "pair_sum_w_ssm_proj")
    dproj = _attention_dkv(dproj, dkv, attn_w=attn_w, kv_w=kv_w, name="attention_dkv")

    n_half = ssm_w // _tile(2 * ssm_w, cw)

    def glu_bwd(j, dt, ga, gb, z):
        sb, sz = _sigmoid(gb), _silu(z)
        dg = jnp.where(j < n_half, dt * sb * sz, dt * ga * sb * (1.0 - sb) * sz)
        return dg, dg

    glu_ops = [(d_ts, 0, ssm_w), (glu, 0, ssm_w), (glu, ssm_w, ssm_w), (proj, o_z, ssm_w)]
    dglu, g_bglu = _ew(glu_bwd, name="glu_bwd", rows=seq, width=2 * ssm_w, tiles=glu_ops,
                       outs=[(BF16, 2 * ssm_w, 0)], accs=1, cw=cw, with_col=True)
    (dproj,) = _ew(lambda dt, ga, gb, z: dt * ga * _sigmoid(gb) * _dsilu(z), name="glu_bwd_z", rows=seq,
                   width=ssm_w, tiles=glu_ops, outs=[(BF16, in_w, o_z)], into=[dproj], cw=cw)
    g_w_glu_t = _matmul(dglu, yg, mode="tn", name="grad_w_glu", tm=512, tk=4096)
    d_yg = _matmul(dglu, w_glu_t, mode="nn", name="d_gelu", out_dtype=BF16)
    ((du, db_re, db_im, dc_re, dc_im, gabr, gabi, gcfr, gcfi, g_d), (chips_ap, chips_sp, sib_glu)) = _ssm_bwd(
        proj, o_u, y_ssm, d_yg, st_re, st_im, bc_rows, rows_p, d_row, chunk=chunk, name="ssm_bwd",
        rider=_join(_chip_exchange([pair_ap, pair_sp]), _sibling_exchange([g_w_glu_t])))
    pair_glu = _pair_sum(g_w_glu_t, sib_glu, "pair_sum_w_glu")
    (dproj,) = _ew(lambda v: v, name="du_store", rows=seq, width=ssm_w, tiles=[(du, 0)],
                   outs=[(BF16, in_w, o_u)], into=[dproj], cw=cw)
    g_a_re, g_a_im, g_log_dt = _ssm_param_bwd(
        p["A_re"], p["A_im"], log_dt_col, *[g.reshape(SUBLANES, n_groups, STATE) for g in (gabr, gabi, gcfr, gcfi)],
        "ssm_param_bwd")
    small_grads = dict(
        loss=loss_local, q_norm_w=g_qw.reshape(n_q, HEAD_DIM).sum(0), k_norm_w=g_kw.reshape(N_KV_HEADS, HEAD_DIM).sum(0),
        sinks=g_sinks[0, :n_q], A_re=g_a_re, A_im=g_a_im, log_dt=g_log_dt.reshape(n_groups),
        B_re=_from_ssm_rows(db_re, True), B_im=_from_ssm_rows(db_im, True),
        C_re=_from_ssm_rows(dc_re, False), C_im=_from_ssm_rows(dc_im, False),
        D_skip=g_d.reshape(n_groups, GROUP), b_glu=g_bglu.reshape(2 * ssm_w))

    n_parts = W_IN_PARTS
    wq = d // n_parts
    g_parts, pair_parts, chip_parts = [], [], []
    extra = [_chip_exchange([pair_glu]), _all_gather([_pack(small_grads, SMALL_REST)])]
    chips_glu = small_parts = dh = None
    for step in range(n_parts + 2):
        riders = list(extra) if step == 0 else []
        if 0 <= step - 2 < n_parts:
            riders.append(_chip_exchange([pair_parts[step - 2]]))
        if 0 <= step - 1 < n_parts:
            riders.append(_sibling_exchange([g_parts[step - 1]]))
        rider = _join(*riders) if riders else None
        if step < n_parts:
            res = _matmul(dproj, Cols(h, step * wq, wq), mode="tn", name="grad_w_in_%d" % step, tk=4096, rider=rider)
            out, landed = res if rider is not None else (res, [])
            g_parts.append(out)
        else:
            q = step - n_parts
            dh, landed = _matmul(dproj, w_in_parts[q], mode="nn", name="d_normed_%d" % q, tk=2176,
                                 out_cols=(d, q * wq), into=dh, rider=rider)
        landed = list(landed)
        if step == 0:
            chips_glu, small_parts = landed[:2]
            landed = landed[2:]
        if 0 <= step - 2 < n_parts:
            chip_parts.append(landed.pop(0))
        if 0 <= step - 1 < n_parts:
            pair_parts.append(_pair_sum(g_parts[step - 1], landed.pop(0), "pair_sum_w_in_%d" % (step - 1)))
    grad_x, g_norm = _rmsnorm_bwd(xs, norm_row, dh, dout, "rmsnorm_bwd")
    (norm_parts,) = _exchange(_all_gather([_pack(dict(norm_w=g_norm), ("norm_w",))]), "gather_norm_grad")
    from_chips = dict(zip(LARGE, (chip_parts, [chips_ap], [chips_glu], [chips_sp], [chips_o])))
    return grad_x, from_chips, small_parts, norm_parts


def kernel(x, norm_w, w_in, q_norm_w, k_norm_w, sinks, w_attn_proj, A_re, A_im, log_dt, B_re, B_im, C_re, C_im, D_skip, w_glu, b_glu, w_ssm_proj, w_out, loss_target, m_norm_w, m_w_in, m_q_norm_w, m_k_norm_w, m_sinks, m_w_attn_proj, m_A_re, m_A_im, m_log_dt, m_B_re, m_B_im, m_C_re, m_C_im, m_D_skip, m_w_glu, m_b_glu, m_w_ssm_proj, m_w_out, v_norm_w, v_w_in, v_q_norm_w, v_k_norm_w, v_sinks, v_w_attn_proj, v_A_re, v_A_im, v_log_dt, v_B_re, v_B_im, v_C_re, v_C_im, v_D_skip, v_w_glu, v_b_glu, v_w_ssm_proj, v_w_out):
    weights = dict(norm_w=norm_w, w_in=w_in, q_norm_w=q_norm_w, k_norm_w=k_norm_w, sinks=sinks,
                   w_attn_proj=w_attn_proj, A_re=A_re, A_im=A_im, log_dt=log_dt, B_re=B_re, B_im=B_im, C_re=C_re,
                   C_im=C_im, D_skip=D_skip, w_glu=w_glu, b_glu=b_glu, w_ssm_proj=w_ssm_proj, w_out=w_out)
    m_in = dict(norm_w=m_norm_w, w_in=m_w_in, q_norm_w=m_q_norm_w, k_norm_w=m_k_norm_w, sinks=m_sinks,
                w_attn_proj=m_w_attn_proj, A_re=m_A_re, A_im=m_A_im, log_dt=m_log_dt, B_re=m_B_re, B_im=m_B_im,
                C_re=m_C_re, C_im=m_C_im, D_skip=m_D_skip, w_glu=m_w_glu, b_glu=m_b_glu, w_ssm_proj=m_w_ssm_proj,
                w_out=m_w_out)
    v_in = dict(norm_w=v_norm_w, w_in=v_w_in, q_norm_w=v_q_norm_w, k_norm_w=v_k_norm_w, sinks=v_sinks,
                w_attn_proj=v_w_attn_proj, A_re=v_A_re, A_im=v_A_im, log_dt=v_log_dt, B_re=v_B_re, B_im=v_B_im,
                C_re=v_C_re, C_im=v_C_im, D_skip=v_D_skip, w_glu=v_w_glu, b_glu=v_b_glu, w_ssm_proj=v_w_ssm_proj,
                w_out=v_w_out)

    _, seq, d = x.shape
    column_sharded = LARGE[:4]
    as_rows = lambda k, a: a.T if k in column_sharded else a
    shards = [as_rows(k, weights[k]).astype(BF16) for k in LARGE]
    small = {k: weights[k] for k in SMALL}
    grad_x, from_chips, small_parts, norm_parts = _step(x.reshape(seq, d), loss_target.reshape(seq, d), small,
                                                        shards)

    grads, delta, new_m, new_v = {}, {}, {}, {}
    for k in LARGE:
        if k == "w_in":
            res = _adamw_chips(weights[k].T, from_chips[k], m_in[k].T, v_in[k].T, "adamw_" + k)
            grads[k], delta[k], new_m[k], new_v[k] = [a.T for a in res]
        elif k == "w_out":
            grads[k], delta[k], new_m[k], new_v[k] = _adamw_chips(weights[k], from_chips[k], m_in[k], v_in[k],
                                                                  "adamw_" + k)
        else:
            grads[k] = _chip_sum(from_chips[k][0], "chip_sum_" + k).T
            delta[k], new_m[k], new_v[k] = _adamw(weights[k], grads[k], m_in[k], v_in[k], "adamw_" + k)

    zero = jnp.zeros((), F32)
    for keys, parts in ((SMALL_REST, small_parts), (("norm_w",), norm_parts)):
        like = dict(small, loss=zero)
        packs = [_pack(dict(src, loss=zero), keys) for src in (weights, m_in, v_in)]
        res = _adamw_small(packs[0], parts, packs[1], packs[2], "adamw_small_%d" % len(keys))
        for dst, r in zip((grads, delta, new_m, new_v), res):
            dst.update(_unpack(r, like, keys))
    loss = grads["loss"]

    return (loss, grad_x.reshape(x.shape), *[grads[k] for k in ORDER], *[delta[k] for k in ORDER],
            *[new_m[k] for k in ORDER], *[new_v[k] for k in ORDER])
```

```python
import math
from typing import Callable, NamedTuple

import jax
import jax.numpy as jnp
import numpy as np
from jax import lax
from jax.experimental import pallas as pl
from jax.experimental.pallas import tpu as pltpu

F32 = jnp.float32
BF16 = jnp.bfloat16
MESH = pl.DeviceIdType.MESH

HEAD_DIM = 64
N_KV_HEADS = 4
GROUP = 16
STATE = 64
BLOCK = 128
NORM_EPS = 1e-6
N_DEV = 8
N_CHIPS = 4
LANES = 128
SUBLANES = 8
MXU_DIM = 256
VMEM_BYTES = 64 * 1024 * 1024
VMEM_CAP = VMEM_BYTES - 8 * 1024 * 1024

ADAM_LR = 0.001
ADAM_B1 = 0.9
ADAM_B2 = 0.999
ADAM_EPS = 1e-08
ADAM_WD = 0.01
ADAM_STEP = 10

GELU_C = math.sqrt(2.0 / math.pi)
GELU_K = 0.044715


def _tile(dim, pref, mult=LANES):
    if dim <= pref:
        return dim
    best = None
    for d in range(mult, pref + 1, mult):
        if dim % d == 0:
            best = d
    assert best is not None, (dim, pref, mult)
    return best


def _params(semantics=None, vmem=None):
    kw = {}
    if semantics is not None:
        kw["dimension_semantics"] = semantics
    if vmem is not None:
        kw["vmem_limit_bytes"] = int(min(VMEM_CAP, max(vmem, 32 * 1024 * 1024)))
    return pltpu.CompilerParams(**kw)


def _nbytes(shape, dtype):
    return math.prod(shape) * jnp.dtype(dtype).itemsize


def _sigmoid(x):
    return 1.0 / (1.0 + jnp.exp(-x))


def _silu(x):
    return x * _sigmoid(x)


def _dsilu(x):
    s = _sigmoid(x)
    return s * (1.0 + x * (1.0 - s))


def _gelu(x):
    return 0.5 * x * (1.0 + jnp.tanh(GELU_C * (x + GELU_K * x * x * x)))


def _dgelu(x):
    t = jnp.tanh(GELU_C * (x + GELU_K * x * x * x))
    return 0.5 * (1.0 + t) + 0.5 * x * (1.0 - t * t) * GELU_C * (1.0 + 3.0 * GELU_K * x * x)


def _dot(a, b, dims):
    return lax.dot_general(a, b, (dims, ((), ())), preferred_element_type=F32)


NN = ((1,), (0,))
NT = ((1,), (1,))
TN = ((0,), (0,))


def _any_spec():
    return pl.BlockSpec(memory_space=pl.ANY)


def _pallas(body, **kw):
    pin = lambda s: pltpu.HBM(s.shape, s.dtype) if isinstance(s, jax.ShapeDtypeStruct) else s
    out_shape = kw.pop("out_shape")
    out_shape = [pin(s) for s in out_shape] if isinstance(out_shape, (list, tuple)) else pin(out_shape)
    call = pl.pallas_call(body, out_shape=out_shape, **kw)

    def run(*operands):
        pinned = [pltpu.with_memory_space_constraint(o, pltpu.HBM) if jnp.issubdtype(o.dtype, jnp.floating) else o
                  for o in operands]
        return call(*pinned)

    return run


class Rider(NamedTuple):
    operands: tuple
    out_shapes: tuple
    sems: tuple
    start: Callable
    finish: Callable


def _all_gather(shards):
    n = len(shards)

    def copies(ins, outs, sems):
        send_sems, recv_sems, local_sems = sems
        x, y, c = lax.axis_index("x"), lax.axis_index("y"), lax.axis_index("c")
        me, sibling = (x, y, c), (x, y, 1 - c)
        chips = [(1 - x, y), (x, 1 - y), (1 - x, 1 - y)]

        def rows(k, px, py, pc):
            r = shards[k].shape[0]
            return outs[k].at[pl.ds((4 * px + 2 * py + pc) * r, r), :]

        def copy(k, s, block, to, src=None):
            return pltpu.make_async_remote_copy(
                src_ref=rows(k, *block) if src is None else src, dst_ref=rows(k, *block),
                send_sem=send_sems.at[7 * k + s], recv_sem=recv_sems.at[7 * k + s],
                device_id=to, device_id_type=MESH)

        mine = [pltpu.make_async_copy(ins[k], rows(k, *me), local_sems.at[k]) for k in range(n)]
        first = []
        for k in range(n):
            first.append(copy(k, 0, me, sibling, src=ins[k]))
            first += [copy(k, 1 + j, me, (*chip, c), src=ins[k]) for j, chip in enumerate(chips)]
        return me, sibling, chips, c, copy, mine, first

    def start(ins, outs, sems):
        *_, mine, first = copies(ins, outs, sems)
        for cp in mine + first:
            cp.start()

    def finish(ins, outs, sems):
        me, sibling, chips, c, copy, mine, first = copies(ins, outs, sems)
        passed = []
        for j, chip in enumerate(chips):
            for k in range(n):
                copy(k, 1 + j, (*chip, c), me).wait_recv()
                fwd = copy(k, 4 + j, (*chip, c), sibling)
                fwd.start()
                passed.append(fwd)
        for k in range(n):
            copy(k, 0, sibling, me).wait_recv()
            for j, chip in enumerate(chips):
                copy(k, 4 + j, (*chip, 1 - c), me).wait_recv()
        for cp in first + passed:
            cp.wait_send()
        for cp in mine:
            cp.wait()

    return Rider(
        tuple(shards),
        tuple(jax.ShapeDtypeStruct((N_DEV * s.shape[0], s.shape[1]), s.dtype) for s in shards),
        (pltpu.SemaphoreType.DMA((7 * n,)), pltpu.SemaphoreType.DMA((7 * n,)), pltpu.SemaphoreType.DMA((n,))),
        start, finish)


def _sibling_exchange(grads):
    n = len(grads)

    def copies(ins, outs, sems):
        send_sems, recv_sems = sems
        x, y, c = lax.axis_index("x"), lax.axis_index("y"), lax.axis_index("c")
        out = []
        for k in range(n):
            r = grads[k].shape[0] // N_DEV
            for j in range(N_CHIPS):
                out.append(pltpu.make_async_remote_copy(
                    src_ref=ins[k].at[pl.ds((2 * j + 1 - c) * r, r), :],
                    dst_ref=outs[k].at[pl.ds(j * r, r), :],
                    send_sem=send_sems.at[N_CHIPS * k + j], recv_sem=recv_sems.at[N_CHIPS * k + j],
                    device_id=(x, y, 1 - c), device_id_type=MESH))
        return out

    def start(ins, outs, sems):
        for cp in copies(ins, outs, sems):
            cp.start()

    def finish(ins, outs, sems):
        for cp in copies(ins, outs, sems):
            cp.wait()

    return Rider(
        tuple(grads), tuple(jax.ShapeDtypeStruct((g.shape[0] // 2, g.shape[1]), g.dtype) for g in grads),
        (pltpu.SemaphoreType.DMA((N_CHIPS * n,)), pltpu.SemaphoreType.DMA((N_CHIPS * n,))), start, finish)


def _chip_exchange(parts):
    n = len(parts)

    def copies(ins, outs, sems):
        send_sems, recv_sems, local_sems = sems
        x, y, c = lax.axis_index("x"), lax.axis_index("y"), lax.axis_index("c")
        my_chip = 2 * x + y
        chips = [(1 - x, y), (x, 1 - y), (1 - x, 1 - y)]
        local, sent = [], []
        for k in range(n):
            r = parts[k].shape[0] // N_CHIPS
            mine = pl.ds(my_chip * r, r)
            local.append(pltpu.make_async_copy(ins[k].at[mine, :], outs[k].at[mine, :], local_sems.at[k]))
            for s, (px, py) in enumerate(chips):
                sent.append(pltpu.make_async_remote_copy(
                    src_ref=ins[k].at[pl.ds((2 * px + py) * r, r), :], dst_ref=outs[k].at[mine, :],
                    send_sem=send_sems.at[3 * k + s], recv_sem=recv_sems.at[3 * k + s],
                    device_id=(px, py, c), device_id_type=MESH))
        return local, sent

    def start(ins, outs, sems):
        local, sent = copies(ins, outs, sems)
        for cp in local + sent:
            cp.start()

    def finish(ins, outs, sems):
        local, sent = copies(ins, outs, sems)
        for cp in sent + local:
            cp.wait()

    return Rider(
        tuple(parts), tuple(jax.ShapeDtypeStruct(p.shape, p.dtype) for p in parts),
        (pltpu.SemaphoreType.DMA((3 * n,)), pltpu.SemaphoreType.DMA((3 * n,)), pltpu.SemaphoreType.DMA((n,))),
        start, finish)


def _join(*riders):
    cuts_in, cuts_out, cuts_sem = [0], [0], [0]
    for r in riders:
        cuts_in.append(cuts_in[-1] + len(r.operands))
        cuts_out.append(cuts_out[-1] + len(r.out_shapes))
        cuts_sem.append(cuts_sem[-1] + len(r.sems))

    def each(which):
        def run(ins, outs, sems):
            for i, r in enumerate(riders):
                getattr(r, which)(ins[cuts_in[i]:cuts_in[i + 1]], outs[cuts_out[i]:cuts_out[i + 1]],
                                  sems[cuts_sem[i]:cuts_sem[i + 1]])
        return run

    return Rider(sum((r.operands for r in riders), ()), sum((r.out_shapes for r in riders), ()),
                 sum((r.sems for r in riders), ()), each("start"), each("finish"))


def _call(body, operands, *, name, out_shape, grid, in_specs, out_specs, scratch_shapes=(), aliases=None,
          semantics=None, vmem=None, rider=None):
    operands, out_shape, scratch_shapes = list(operands), list(out_shape), list(scratch_shapes)
    in_specs, out_specs = list(in_specs), list(out_specs)
    if rider is None:
        res = _pallas(
            body, name=name, out_shape=out_shape, grid=grid, in_specs=in_specs, out_specs=out_specs,
            scratch_shapes=scratch_shapes, input_output_aliases=aliases or {},
            compiler_params=_params(semantics, vmem))(*operands)
        return list(res), []
    n_in, n_out, n_scr = len(operands), len(out_shape), len(scratch_shapes)
    ri, ro = len(rider.operands), len(rider.out_shapes)

    def carried(*refs):
        a, b = n_in, n_in + ri
        c, d = b + n_out, b + n_out + ro
        e = d + n_scr
        ids = [pl.program_id(k) for k in range(len(grid))]
        first = ids[0] == 0
        last = ids[0] == grid[0] - 1
        for k in range(1, len(grid)):
            first = jnp.logical_and(first, ids[k] == 0)
            last = jnp.logical_and(last, ids[k] == grid[k] - 1)

        @pl.when(first)
        def _():
            rider.start(refs[a:b], refs[c:d], refs[e:])

        body(*refs[:a], *refs[b:c], *refs[d:e])

        @pl.when(last)
        def _():
            rider.finish(refs[a:b], refs[c:d], refs[e:])

    res = _pallas(
        carried, name=name, out_shape=out_shape + list(rider.out_shapes), grid=grid,
        in_specs=in_specs + [_any_spec()] * ri, out_specs=out_specs + [_any_spec()] * ro,
        scratch_shapes=scratch_shapes + list(rider.sems), input_output_aliases=aliases or {},
        compiler_params=_params(("arbitrary",) * len(grid), vmem))(*operands, *rider.operands)
    return list(res[:n_out]), list(res[n_out:])


def _exchange(rider, name):
    ri, ro = len(rider.operands), len(rider.out_shapes)

    def body(*refs):
        rider.start(refs[:ri], refs[ri:ri + ro], refs[ri + ro:])
        rider.finish(refs[:ri], refs[ri:ri + ro], refs[ri + ro:])

    return _pallas(
        body, name=name, out_shape=list(rider.out_shapes), in_specs=[_any_spec()] * ri,
        out_specs=[_any_spec()] * ro, scratch_shapes=list(rider.sems))(*rider.operands)


class Cols(NamedTuple):
    arr: jax.Array
    off: int
    width: int


def _cols(a):
    return a if isinstance(a, Cols) else Cols(a, 0, a.shape[1])


def _matmul(a, b, *, mode, name, out_dtype=F32, tm=1024, tn=1024, tk=2048, bias=None, add=None, out_cols=None,
            into=None, rider=None):
    a, b = _cols(a), _cols(b)
    if mode == "nn":
        (m, k), (k2, n) = (a.arr.shape[0], a.width), (b.arr.shape[0], b.width)
    elif mode == "nt":
        (m, k), (n, k2) = (a.arr.shape[0], a.width), (b.arr.shape[0], b.width)
    else:
        (k, m), (k2, n) = (a.arr.shape[0], a.width), (b.arr.shape[0], b.width)
    assert k == k2, (a.arr.shape, b.arr.shape, mode)
    tm, tn, tk = _tile(m, tm), _tile(n, tn), _tile(k, tk)
    nk = k // tk
    dims = {"nn": NN, "nt": NT, "tn": TN}[mode]
    if mode == "tn":
        assert a.off % tm == 0
        a_spec = pl.BlockSpec((tk, tm), lambda i, j, kk, o=a.off // tm: (kk, i + o))
    else:
        assert a.off % tk == 0
        a_spec = pl.BlockSpec((tm, tk), lambda i, j, kk, o=a.off // tk: (i, kk + o))
    if mode == "nt":
        assert b.off % tk == 0
        b_spec = pl.BlockSpec((tn, tk), lambda i, j, kk, o=b.off // tk: (j, kk + o))
    else:
        assert b.off % tn == 0
        b_spec = pl.BlockSpec((tk, tn), lambda i, j, kk, o=b.off // tn: (kk, j + o))
    in_specs, operands = [a_spec, b_spec], [a.arr, b.arr]
    assert bias is None or add is None
    if bias is not None:
        in_specs.append(pl.BlockSpec((1, tn), lambda i, j, kk: (0, j)))
        operands.append(bias)
    if add is not None:
        assert add.shape == (m, n)
        in_specs.append(pl.BlockSpec((tm, tn), lambda i, j, kk: (i, j)))
        operands.append(add)
    total_w, o_off = out_cols if out_cols is not None else (n, 0)
    assert o_off % tn == 0
    aliases = {}
    if into is not None:
        assert into.shape == (m, total_w) and into.dtype == out_dtype
        in_specs.append(_any_spec())
        operands.append(into)
        aliases = {len(operands) - 1: 0}
    n_in = len(operands)

    def body(*refs):
        a_ref, b_ref = refs[0], refs[1]
        bias_ref = refs[2] if bias is not None or add is not None else None
        o_ref = refs[n_in]
        acc_ref = refs[-1] if nk > 1 else None
        part = _dot(a_ref[...].astype(BF16), b_ref[...].astype(BF16), dims)

        def finish(acc):
            if bias_ref is not None:
                acc = acc + bias_ref[...]
            o_ref[...] = acc.astype(out_dtype)

        if nk == 1:
            finish(part)
        else:
            kk = pl.program_id(2)

            @pl.when(kk == 0)
            def _():
                acc_ref[...] = part

            @pl.when(kk > 0)
            def _():
                acc_ref[...] += part

            @pl.when(kk == nk - 1)
            def _():
                finish(acc_ref[...])

    vmem = 2 * (_nbytes((tm, tk), a.arr.dtype) + _nbytes((tk, tn), b.arr.dtype) + _nbytes((tm, tn), out_dtype))
    vmem += 3 * _nbytes((tm, tn), F32)
    (out,), landed = _call(
        body, operands, name=name, out_shape=[jax.ShapeDtypeStruct((m, total_w), out_dtype)],
        grid=(m // tm, n // tn, nk), in_specs=in_specs,
        out_specs=[pl.BlockSpec((tm, tn), lambda i, j, kk, o=o_off // tn: (i, j + o))],
        scratch_shapes=[pltpu.VMEM((tm, tn), F32)] if nk > 1 else [], aliases=aliases,
        semantics=("parallel", "parallel", "arbitrary"), vmem=vmem, rider=rider)
    return out if rider is None else (out, landed)


def _ew(fn, *, name, rows, width, tiles, vecs=(), outs, accs=0, tl=1024, cw=512, into=None, with_col=False):
    tl, cw = _tile(rows, tl, SUBLANES), _tile(width, cw)
    ncol = width // cw
    nt_, nv = len(tiles), len(vecs)
    into = list(into) if into is not None else [None] * len(outs)
    aliased = [t for t in into if t is not None]

    def off(o):
        assert o % cw == 0, (name, o, cw)
        return o // cw

    in_specs, vmem = [], 0
    for t in tiles:
        arr, o = t[0], off(t[1])
        wrap = t[2] // cw if len(t) > 2 else ncol
        in_specs.append(pl.BlockSpec((tl, cw), lambda j, i, o=o, wrap=wrap: (i, o + j % wrap)))
        vmem += _nbytes((tl, cw), arr.dtype)
    in_specs += [pl.BlockSpec((1, cw), lambda j, i, o=off(o): (0, j + o)) for _, o in vecs]
    in_specs += [_any_spec() for _ in aliased]
    out_shape, out_specs, aliases = [], [], {}
    n_in = nt_ + nv
    for idx, ((dt, tw, o), tgt) in enumerate(zip(outs, into)):
        out_shape.append(jax.ShapeDtypeStruct((rows, tw), dt))
        out_specs.append(pl.BlockSpec((tl, cw), lambda j, i, o=off(o): (i, j + o)))
        vmem += _nbytes((tl, cw), dt)
        if tgt is not None:
            assert tgt.shape == (rows, tw) and tgt.dtype == dt, (name, tgt.shape, tgt.dtype)
            aliases[n_in + len(aliases)] = idx
    for _ in range(accs):
        out_shape.append(jax.ShapeDtypeStruct((1, width), F32))
        out_specs.append(pl.BlockSpec((1, cw), lambda j, i: (0, j)))
    n_out = len(outs)

    def body(*refs):
        vals = [r[...].astype(F32) for r in refs[:n_in]]
        out_refs = refs[n_in + len(aliased):]
        res = fn(pl.program_id(0), *vals) if with_col else fn(*vals)
        res = res if isinstance(res, (tuple, list)) else (res,)
        assert len(res) == n_out + accs, (name, len(res))
        for r, v in zip(out_refs[:n_out], res[:n_out]):
            r[...] = v.astype(r.dtype)
        first = pl.program_id(1) == 0
        for r, v in zip(out_refs[n_out:], res[n_out:]):
            s = jnp.sum(v, axis=0, keepdims=True)

            @pl.when(first)
            def _(r=r, s=s):
                r[...] = s

            @pl.when(jnp.logical_not(first))
            def _(r=r, s=s):
                r[...] += s

    return _pallas(
        body, name=name, out_shape=out_shape, grid=(ncol, rows // tl),
        in_specs=in_specs, out_specs=out_specs, input_output_aliases=aliases,
        compiler_params=_params(("parallel", "arbitrary"), 3 * vmem),
    )(*[t[0] for t in tiles], *[v for v, _ in vecs], *aliased)


def _rmsnorm_fwd(x, w_row, name):
    rows, d = x.shape
    tl = _tile(rows, 512, SUBLANES)

    def body(x_ref, w_ref, h_ref):
        xv = x_ref[...]
        rstd = lax.rsqrt(jnp.mean(xv * xv, axis=-1, keepdims=True) + NORM_EPS)
        h_ref[...] = (xv * rstd * w_ref[...]).astype(BF16)

    return _pallas(
        body, name=name, out_shape=jax.ShapeDtypeStruct((rows, d), BF16), grid=(rows // tl,),
        in_specs=[pl.BlockSpec((tl, d), lambda i: (i, 0)), pl.BlockSpec((1, d), lambda i: (0, 0))],
        out_specs=pl.BlockSpec((tl, d), lambda i: (i, 0)),
        compiler_params=_params(("parallel",)),
    )(x, w_row)


def _rmsnorm_bwd(x, w_row, dh, dout, name, rider=None):
    rows, d = x.shape
    tl = _tile(rows, 256, SUBLANES)

    def body(x_ref, w_ref, dh_ref, dout_ref, gx_ref, gw_ref):
        xv = x_ref[...]
        rstd = lax.rsqrt(jnp.mean(xv * xv, axis=-1, keepdims=True) + NORM_EPS)
        xn = xv * rstd
        dhv = dh_ref[...]
        dxn = dhv * w_ref[...]
        dx = rstd * (dxn - xn * jnp.mean(dxn * xn, axis=-1, keepdims=True))
        gx_ref[...] = dout_ref[...] + dx
        gw = jnp.sum(dhv * xn, axis=0, keepdims=True)

        @pl.when(pl.program_id(0) == 0)
        def _():
            gw_ref[...] = gw

        @pl.when(pl.program_id(0) > 0)
        def _():
            gw_ref[...] += gw

    tile = pl.BlockSpec((tl, d), lambda i: (i, 0))
    row = pl.BlockSpec((1, d), lambda i: (0, 0))
    res, landed = _call(
        body, [x, w_row, dh, dout], name=name,
        out_shape=[jax.ShapeDtypeStruct((rows, d), F32), jax.ShapeDtypeStruct((1, d), F32)],
        grid=(rows // tl,), in_specs=[tile, row, tile, tile], out_specs=[tile, row],
        semantics=("arbitrary",), rider=rider)
    return res if rider is None else (res, landed)


def _head_mean(x, gmat):
    hi = x.astype(BF16)
    lo = (x - hi.astype(F32)).astype(BF16)
    out = []
    for s in range(x.shape[1] // MXU_DIM):
        sl = slice(s * MXU_DIM, (s + 1) * MXU_DIM)
        out.append(_dot(hi[:, sl], gmat, NN) + _dot(lo[:, sl], gmat, NN))
    return out[0] if len(out) == 1 else jnp.concatenate(out, axis=1)


def _head_mean_matrix():
    blk = jnp.arange(MXU_DIM) // HEAD_DIM
    return jnp.where(blk[:, None] == blk[None, :], 1.0 / HEAD_DIM, 0.0).astype(BF16)


def _spread_head(x, g, width):
    col = x[:, (g // 2) * LANES:(g // 2 + 1) * LANES]
    other = pltpu.roll(col, HEAD_DIM, axis=1)
    low = lax.broadcasted_iota(jnp.int32, col.shape, 1) < HEAD_DIM
    both = jnp.where(low, col, other) if g % 2 == 0 else jnp.where(low, other, col)
    return both if width == LANES else jnp.concatenate([both] * (width // LANES), axis=1)


def _head_diagonal(t, per_kv):
    head = lax.broadcasted_iota(jnp.int32, t.shape, 1) // HEAD_DIM
    zero = jnp.zeros_like(t)
    return jnp.concatenate([jnp.where(head == r, t, zero) for r in range(per_kv)], axis=0)


def _fold_heads(x, per_kv):
    rows = x.shape[0] // per_kv
    head = lax.broadcasted_iota(jnp.int32, (rows, x.shape[1]), 1) // HEAD_DIM
    acc = jnp.where(head == 0, x[0:rows], 0.0)
    for r in range(1, per_kv):
        acc = acc + jnp.where(head == r, x[r * rows:(r + 1) * rows], 0.0)
    while acc.shape[1] > LANES:
        half = acc.shape[1] // 2
        acc = acc[:, :half] + acc[:, half:]
    return acc + pltpu.roll(acc, HEAD_DIM, axis=1)


def _join_heads(parts):
    low = lax.broadcasted_iota(jnp.int32, parts[0].shape, 1) < HEAD_DIM
    cols = [jnp.where(low, parts[2 * j], parts[2 * j + 1]) for j in range(len(parts) // 2)]
    return cols[0] if len(cols) == 1 else jnp.concatenate(cols, axis=1)


def _attn_specs(attn_w, kv_w):
    half = attn_w // 2
    kcol, vcol = attn_w // kv_w, attn_w // kv_w + 1
    gcol = (attn_w + 2 * kv_w) // half
    prev = lambda i: jnp.maximum(i - 1, 0)
    return [
        pl.BlockSpec((BLOCK, attn_w), lambda i: (i, 0)),
        pl.BlockSpec((BLOCK, kv_w), lambda i: (prev(i), kcol)),
        pl.BlockSpec((BLOCK, kv_w), lambda i: (i, kcol)),
        pl.BlockSpec((BLOCK, kv_w), lambda i: (prev(i), vcol)),
        pl.BlockSpec((BLOCK, kv_w), lambda i: (i, vcol)),
        pl.BlockSpec((BLOCK, half), lambda i: (i, gcol)),
        pl.BlockSpec((BLOCK, half), lambda i: (i, gcol + 1)),
    ]


def _band_mask(i):
    q_loc = lax.broadcasted_iota(jnp.int32, (BLOCK, 2 * BLOCK), 0) + BLOCK
    k_loc = lax.broadcasted_iota(jnp.int32, (BLOCK, 2 * BLOCK), 1)
    diff = q_loc - k_loc
    first_key = jnp.where(i == 0, BLOCK, 0)
    return (diff >= 0) & (diff < BLOCK) & (k_loc >= first_key)


def _softmax_with_sink(s, sink):
    m = jnp.maximum(jnp.max(s, axis=-1, keepdims=True), sink)
    p = jnp.exp(s - m)
    e_sink = jnp.exp(sink - m)
    den = jnp.sum(p, axis=-1, keepdims=True) + e_sink
    inv = 1.0 / den
    return p * inv, e_sink * inv


def _attn_block(i, q, kk, vv, qw, kw, gmat, sink_ref, per_kv):
    scale = 1.0 / math.sqrt(HEAD_DIM)
    keys = 2 * BLOCK
    valid = _band_mask(i)
    q_rstd = lax.rsqrt(_head_mean(q * q, gmat) + NORM_EPS)
    qn = q * q_rstd
    qh = (qn * qw).astype(BF16)
    k_rstd = lax.rsqrt(_head_mean(kk * kk, gmat) + NORM_EPS)
    kn = kk * k_rstd
    kh = kn * kw
    gw = per_kv * HEAD_DIM
    groups = []
    for g in range(N_KV_HEADS):
        kd = _head_diagonal(_spread_head(kh, g, gw).astype(BF16), per_kv)
        vd = _head_diagonal(_spread_head(vv, g, gw).astype(BF16), per_kv)
        qg = qh[:, g * gw:(g + 1) * gw]
        s_all = _dot(qg, kd, NT) * scale
        ps, p_sinks = [], []
        for r in range(per_kv):
            s = jnp.where(valid, s_all[:, r * keys:(r + 1) * keys], -1e30)
            p, p_sink = _softmax_with_sink(s, sink_ref[g * per_kv + r])
            ps.append(p)
            p_sinks.append(p_sink)
        pb = jnp.concatenate(ps, axis=1).astype(BF16)
        groups.append((kd, vd, qg, ps, p_sinks, pb, _dot(pb, vd, NN)))
    return qn, q_rstd, kn, k_rstd, groups


def _attention_fwd(proj, qw_row, kw_row, gmat, sinks, *, attn_w, kv_w, name):
    rows = proj.shape[0]
    per_kv = attn_w // HEAD_DIM // N_KV_HEADS

    def body(q_ref, kp_ref, kc_ref, vp_ref, vc_ref, glo_ref, ghi_ref, qw_ref, kw_ref, gm_ref, sink_ref, o_ref):
        kk = jnp.concatenate([kp_ref[...], kc_ref[...]], axis=0).astype(F32)
        vv = jnp.concatenate([vp_ref[...], vc_ref[...]], axis=0).astype(F32)
        gate = jnp.concatenate([glo_ref[...], ghi_ref[...]], axis=1).astype(F32)
        *_, groups = _attn_block(pl.program_id(0), q_ref[...].astype(F32), kk, vv, qw_ref[...], kw_ref[...], gm_ref[...],
                                 sink_ref, per_kv)
        attn = jnp.concatenate([grp[-1] for grp in groups], axis=1)
        o_ref[...] = (attn * _silu(gate)).astype(BF16)

    const = lambda a: pl.BlockSpec(a.shape, lambda i: (0, 0))
    return _pallas(
        body, name=name, out_shape=jax.ShapeDtypeStruct((rows, attn_w), BF16), grid=(rows // BLOCK,),
        in_specs=_attn_specs(attn_w, kv_w) + [const(qw_row), const(kw_row), const(gmat),
                                              pl.BlockSpec(memory_space=pltpu.SMEM)],
        out_specs=pl.BlockSpec((BLOCK, attn_w), lambda i: (i, 0)),
        compiler_params=_params(("parallel",), 40 * 1024 * 1024),
    )(proj, proj, proj, proj, proj, proj, proj, qw_row, kw_row, gmat, sinks)


def _attention_bwd(proj, d_ag, dproj, qw_row, kw_row, gmat, sinks, *, attn_w, kv_w, name, rider=None):
    rows = proj.shape[0]
    nb = rows // BLOCK
    per_kv = attn_w // HEAD_DIM // N_KV_HEADS
    gw = per_kv * HEAD_DIM
    keys = 2 * BLOCK
    scale = 1.0 / math.sqrt(HEAD_DIM)
    w_out = 2 * attn_w + 2 * kv_w

    def body(q_ref, kp_ref, kc_ref, vp_ref, vc_ref, glo_ref, ghi_ref, dag_ref, qw_ref, kw_ref, gm_ref, sink_ref, _,
             dp_ref, dkv_ref, gqw_ref, gkw_ref, gs_ref):
        i = pl.program_id(0)
        kk = jnp.concatenate([kp_ref[...], kc_ref[...]], axis=0).astype(F32)
        vv = jnp.concatenate([vp_ref[...], vc_ref[...]], axis=0).astype(F32)
        gate = jnp.concatenate([glo_ref[...], ghi_ref[...]], axis=1).astype(F32)
        d_ag_v = dag_ref[...].astype(F32)
        qw, kw, gmat_v = qw_ref[...], kw_ref[...], gm_ref[...]
        qn, q_rstd, kn, k_rstd, groups = _attn_block(i, q_ref[...].astype(F32), kk, vv, qw, kw, gmat_v, sink_ref,
                                                     per_kv)
        lane = lax.broadcasted_iota(jnp.int32, (SUBLANES, LANES), 1)
        sub = lax.broadcasted_iota(jnp.int32, (SUBLANES, LANES), 0)
        gsink = jnp.zeros((SUBLANES, LANES), F32)
        dq_groups, dgate_groups, dk_heads, dv_heads = [], [], [], []
        for g, (kd, vd, qg, ps, p_sinks, pb, o) in enumerate(groups):
            cs = slice(g * gw, (g + 1) * gw)
            gate_g, d_ag_g = gate[:, cs], d_ag_v[:, cs]
            dgate_groups.append(d_ag_g * o * _dsilu(gate_g))
            do = (d_ag_g * _silu(gate_g)).astype(BF16)
            dp_all = _dot(do, vd, NT)
            dss = []
            for r in range(per_kv):
                p, dp = ps[r], dp_all[:, r * keys:(r + 1) * keys]
                delta = jnp.sum(p * dp, axis=-1, keepdims=True)
                dss.append(p * (dp - delta) * scale)
                gs_h = jnp.sum(-p_sinks[r] * delta, axis=0, keepdims=True)
                gsink = gsink + jnp.where((lane == g * per_kv + r) & (sub == 0), gs_h, 0.0)
            ds = jnp.concatenate(dss, axis=1).astype(BF16)
            dq_groups.append(_dot(ds, kd, NN))
            dk_heads.append(_fold_heads(_dot(ds, qg, TN), per_kv))
            dv_heads.append(_fold_heads(_dot(pb, do, TN), per_kv))
        dqh = jnp.concatenate(dq_groups, axis=1)
        gqw = jnp.sum(dqh * qn, axis=0, keepdims=True)
        dqn = dqh * qw
        dq = q_rstd * (dqn - qn * _head_mean(dqn * qn, gmat_v))
        dkh = _join_heads(dk_heads)
        gkw = jnp.sum(dkh * kn, axis=0, keepdims=True)
        dkn = dkh * kw
        dk = k_rstd * (dkn - kn * _head_mean(dkn * kn, gmat_v))
        dp_ref[:, 0:attn_w] = dq.astype(BF16)
        dp_ref[:, attn_w:attn_w + 2 * kv_w] = jnp.zeros((BLOCK, 2 * kv_w), BF16)
        dp_ref[:, attn_w + 2 * kv_w:w_out] = jnp.concatenate(dgate_groups, axis=1).astype(BF16)
        dkv_ref[0] = jnp.concatenate([dk, _join_heads(dv_heads)], axis=1)

        @pl.when(i == 0)
        def _():
            gqw_ref[...] = gqw
            gkw_ref[...] = gkw
            gs_ref[...] = gsink

        @pl.when(i > 0)
        def _():
            gqw_ref[...] += gqw
            gkw_ref[...] += gkw
            gs_ref[...] += gsink

    const = lambda a: pl.BlockSpec(a.shape, lambda i: (0, 0))
    res, landed = _call(
        body, [proj, proj, proj, proj, proj, proj, proj, d_ag, qw_row, kw_row, gmat, sinks, dproj], name=name,
        out_shape=[jax.ShapeDtypeStruct(dproj.shape, BF16),
                   jax.ShapeDtypeStruct((nb, 2 * BLOCK, 2 * kv_w), F32),
                   jax.ShapeDtypeStruct(qw_row.shape, F32), jax.ShapeDtypeStruct(kw_row.shape, F32),
                   jax.ShapeDtypeStruct((SUBLANES, LANES), F32)],
        grid=(nb,),
        in_specs=_attn_specs(attn_w, kv_w) + [pl.BlockSpec((BLOCK, attn_w), lambda i: (i, 0)), const(qw_row),
                                              const(kw_row), const(gmat), pl.BlockSpec(memory_space=pltpu.SMEM),
                                              _any_spec()],
        out_specs=[pl.BlockSpec((BLOCK, w_out), lambda i: (i, 0)),
                   pl.BlockSpec((1, 2 * BLOCK, 2 * kv_w), lambda i: (i, 0, 0)),
                   const(qw_row), const(kw_row), pl.BlockSpec((SUBLANES, LANES), lambda i: (0, 0))],
        aliases={12: 0}, semantics=("arbitrary",), vmem=48 * 1024 * 1024, rider=rider)
    return res if rider is None else (res, landed)


def _attention_dkv(dproj, dkv, *, attn_w, kv_w, name):
    rows = dproj.shape[0]
    nb = rows // BLOCK
    col = attn_w // (2 * kv_w)

    def body(cur_ref, nxt_ref, _, o_ref):
        i = pl.program_id(0)
        nxt = jnp.where(i < nb - 1, nxt_ref[0, 0:BLOCK, :], 0.0)
        o_ref[...] = (cur_ref[0, BLOCK:2 * BLOCK, :] + nxt).astype(BF16)

    blk = lambda f: pl.BlockSpec((1, 2 * BLOCK, 2 * kv_w), f)
    return _pallas(
        body, name=name, out_shape=jax.ShapeDtypeStruct(dproj.shape, BF16), grid=(nb,),
        in_specs=[blk(lambda i: (i, 0, 0)), blk(lambda i: (jnp.minimum(i + 1, nb - 1), 0, 0)), _any_spec()],
        out_specs=pl.BlockSpec((BLOCK, 2 * kv_w), lambda i: (i, col)),
        input_output_aliases={2: 0},
        compiler_params=_params(("parallel",)),
    )(dkv, dkv, dproj)


def _cmul(ar, ai, br, bi):
    return ar * br - ai * bi, ar * bi + ai * br


def _ssm_prep(a_re, a_im, log_dt_col, steps, name):
    def body(are_ref, aim_ref, ldt_ref, abr_ref, abi_ref, cfr_ref, cfi_ref, apr_ref, api_ref, pwr_ref, pwi_ref):
        are, aim = are_ref[...], aim_ref[...]
        dt = jnp.exp(ldt_ref[...])
        mag = jnp.exp(dt * are)
        abr = mag * jnp.cos(dt * aim)
        abi = mag * jnp.sin(dt * aim)
        num_re, num_im = abr - 1.0, abi
        den = are * are + aim * aim
        abr_ref[...] = abr
        abi_ref[...] = abi
        cfr_ref[...] = (num_re * are + num_im * aim) / den
        cfi_ref[...] = (num_im * are - num_re * aim) / den
        pr, pi = jnp.ones_like(abr), jnp.zeros_like(abr)
        for k in range(steps):
            pwr_ref[k] = pr
            pwi_ref[k] = pi
            pr, pi = _cmul(pr, pi, abr, abi)
        apr_ref[...] = pr
        api_ref[...] = pi

    shp = jax.ShapeDtypeStruct(a_re.shape, F32)
    pows = jax.ShapeDtypeStruct((steps,) + a_re.shape, F32)
    return _pallas(body, name=name, out_shape=[shp] * 6 + [pows] * 2)(a_re, a_im, log_dt_col)


def _ssm_param_bwd(a_re, a_im, log_dt_col, d_ab_re, d_ab_im, b_re, b_im, dbt_re, dbt_im, name):
    def body(are_ref, aim_ref, ldt_ref, gabr_ref, gabi_ref, br_ref, bi_ref, tr_ref, ti_ref,
             dar_ref, dai_ref, dldt_ref, dbr_ref, dbi_ref):
        are, aim = are_ref[...], aim_ref[...]
        dt = jnp.exp(ldt_ref[...])
        mag = jnp.exp(dt * are)
        abr = mag * jnp.cos(dt * aim)
        abi = mag * jnp.sin(dt * aim)
        den = are * are + aim * aim
        cfr = ((abr - 1.0) * are + abi * aim) / den
        cfi = (abi * are - (abr - 1.0) * aim) / den
        gabr, gabi = jnp.sum(gabr_ref[...], axis=0), jnp.sum(gabi_ref[...], axis=0)
        t_re, t_im = tr_ref[...], ti_ref[...]
        g_r, g_i = _cmul(br_ref[...], -bi_ref[...], t_re, t_im)
        gcfr, gcfi = jnp.sum(g_r, axis=1), jnp.sum(g_i, axis=1)
        dbr, dbi = _cmul(cfr[:, None, :], -cfi[:, None, :], t_re, t_im)
        dbr_ref[...] = dbr
        dbi_ref[...] = dbi
        inv_r, inv_i = are / den, -aim / den
        t_r, t_i = _cmul(inv_r, -inv_i, gcfr, gcfi)
        gabr, gabi = gabr + t_r, gabi + t_i
        q_r, q_i = _cmul(cfr, cfi, inv_r, inv_i)
        da_r, da_i = _cmul(-q_r, q_i, gcfr, gcfi)
        gz_r, gz_i = _cmul(abr, -abi, gabr, gabi)
        dar_ref[...] = da_r + dt * gz_r
        dai_ref[...] = da_i + dt * gz_i
        dldt_ref[...] = dt * jnp.sum(are * gz_r + aim * gz_i, axis=-1, keepdims=True)

    shp = jax.ShapeDtypeStruct(a_re.shape, F32)
    bshp = jax.ShapeDtypeStruct(b_re.shape, F32)
    return _pallas(body, name=name,
                   out_shape=[shp, shp, jax.ShapeDtypeStruct(log_dt_col.shape, F32), bshp, bshp])(
        a_re, a_im, log_dt_col, d_ab_re, d_ab_im, b_re, b_im, dbt_re, dbt_im)


SCAN_LANES = 512
SSM_CHUNK = 256
W_IN_PARTS = 2


def _scan_segments(xr_ref, xi_ref, a_re, a_im, ap_re, ap_im, pw_re, pw_im, carry_re, carry_im, cm_re, cm_im, steps,
                   reverse, base):
    n = xr_ref.shape[1]
    seg_order = range(SUBLANES - 1, -1, -1) if reverse else range(SUBLANES)
    sign = -1.0 if reverse else 1.0
    for c0 in range(0, n, SCAN_LANES):
        ls = slice(c0, c0 + SCAN_LANES)
        gs = slice(base + c0, base + c0 + SCAN_LANES)
        ar = jnp.broadcast_to(a_re[:, gs], (SUBLANES, SCAN_LANES))
        ai = jnp.broadcast_to(a_im[:, gs], (SUBLANES, SCAN_LANES))
        end_r = jnp.zeros((SUBLANES, SCAN_LANES), F32)
        end_i = jnp.zeros((SUBLANES, SCAN_LANES), F32)
        for j in range(steps):
            k = j if reverse else steps - 1 - j
            rws = slice(j * SUBLANES, (j + 1) * SUBLANES)
            tr, ti = _cmul(pw_re[k:k + 1, gs], sign * pw_im[k:k + 1, gs], xr_ref[rws, ls], xi_ref[rws, ls])
            end_r, end_i = end_r + tr, end_i + ti
        cr, ci = carry_re[:, gs], carry_im[:, gs]
        apr, api = ap_re[:, gs], ap_im[:, gs]
        for r in seg_order:
            cm_re[r:r + 1, gs] = cr
            cm_im[r:r + 1, gs] = ci
            tr, ti = _cmul(apr, api, cr, ci)
            cr, ci = end_r[r:r + 1, :] + tr, end_i[r:r + 1, :] + ti
        carry_re[:, gs] = cr
        carry_im[:, gs] = ci

        def run(t, s, ar=ar, ai=ai, ls=ls):
            j = steps - 1 - t if reverse else t
            r0 = pl.multiple_of(j * SUBLANES, SUBLANES)
            sr, si = _cmul(ar, ai, s[0], s[1])
            sr = sr + xr_ref[pl.ds(r0, SUBLANES), ls]
            si = si + xi_ref[pl.ds(r0, SUBLANES), ls]
            xr_ref[pl.ds(r0, SUBLANES), ls] = sr
            xi_ref[pl.ds(r0, SUBLANES), ls] = si
            return sr, si

        lax.fori_loop(0, steps, run, (cm_re[:, gs], cm_im[:, gs]))


SB_GROUPS = MXU_DIM // GROUP
SB_STATE = SB_GROUPS * STATE


def _ssm_rows(m):
    flat = m.reshape(-1, STATE).astype(F32)
    return jnp.concatenate([flat, flat], axis=1)


def _from_ssm_rows(rows):
    return rows[:, :STATE].reshape(-1, GROUP, STATE)


def _own_group(shape):
    row_g = lax.broadcasted_iota(jnp.int32, shape, 0) // GROUP
    col_g = lax.broadcasted_iota(jnp.int32, shape, 1) // STATE
    return row_g == col_g


def _block_diagonal(rows):
    tiled = jnp.concatenate([rows] * (SB_STATE // LANES), axis=1)
    return jnp.where(_own_group(tiled.shape), tiled, 0.0).astype(BF16)


def _block_rows(acc):
    x = jnp.where(_own_group(acc.shape), acc, 0.0)
    while x.shape[1] > LANES:
        half = x.shape[1] // 2
        x = x[:, :half] + x[:, half:]
    return x + pltpu.roll(x, STATE, axis=1)


def _rows_to_segments(dst, srcs, steps, stage):
    for ref, off in srcs:
        for k in range(ref.shape[1] // LANES):
            stage[off // LANES + k] = ref[:, k * LANES:(k + 1) * LANES].astype(F32)
    for k in range(dst.shape[1] // LANES):
        for j in range(steps):
            dst[j * SUBLANES:(j + 1) * SUBLANES, k * LANES:(k + 1) * LANES] = (
                stage[k, pl.ds(j, SUBLANES, stride=steps), :])


def _segments_to_rows(dst, src, steps, stage):
    for k in range(src.shape[1] // LANES):
        for j in range(steps):
            stage[k, pl.ds(j, SUBLANES, stride=steps), :] = (
                src[j * SUBLANES:(j + 1) * SUBLANES, k * LANES:(k + 1) * LANES])
    for k in range(src.shape[1] // LANES):
        dst[:, k * LANES:(k + 1) * LANES] = stage[k]


def _u_specs(w, o_u, chunk, index):
    half = w // 2
    assert o_u % half == 0
    return [pl.BlockSpec((chunk, half), lambda c, k=k: (index(c), o_u // half + k)) for k in range(2)]


def _ssm_fwd(proj, o_u, bc_rows, rows_p, d_row, *, chunk, name, rider=None):
    rows = proj.shape[0]
    w = d_row.shape[1]
    nc = rows // chunk
    steps = chunk // SUBLANES
    nsb = w // MXU_DIM
    n_state = nsb * SB_STATE

    def body(ulo_ref, uhi_ref, b2r_ref, b2i_ref, c2r_ref, c2i_ref, abr_ref, abi_ref, cfr_ref, cfi_ref, apr_ref,
             api_ref, pwr_ref, pwi_ref, d_ref, y_ref, str_ref, sti_ref, yg_ref, bre_ref, bim_ref, cre_ref, cim_ref,
             useg, yseg, stage, sr, si,
             carry_r, carry_i, cm_r, cm_i):
        @pl.when(pl.program_id(0) == 0)
        def _():
            for src, dst in ((b2r_ref, bre_ref), (b2i_ref, bim_ref), (c2r_ref, cre_ref), (c2i_ref, cim_ref)):
                for sb in range(nsb):
                    dst[sb] = _block_diagonal(src[sb * MXU_DIM:(sb + 1) * MXU_DIM, :])
            carry_r[...] = jnp.zeros_like(carry_r)
            carry_i[...] = jnp.zeros_like(carry_i)

        str_ref[0] = carry_r[...]
        sti_ref[0] = carry_i[...]
        _rows_to_segments(useg, [(ulo_ref, 0), (uhi_ref, w // 2)], steps, stage)
        for sb in range(nsb):
            us = slice(sb * MXU_DIM, (sb + 1) * MXU_DIM)
            ss = slice(sb * SB_STATE, (sb + 1) * SB_STATE)
            ub = useg[:, us].astype(BF16)
            bur = _dot(ub, bre_ref[sb], NN)
            bui = _dot(ub, bim_ref[sb], NN)
            xr, xi = _cmul(cfr_ref[:, ss], cfi_ref[:, ss], bur, bui)
            sr[...] = xr
            si[...] = xi
            _scan_segments(sr, si, abr_ref[...], abi_ref[...], apr_ref[...], api_ref[...], pwr_ref, pwi_ref,
                           carry_r, carry_i, cm_r, cm_i, steps, False, sb * SB_STATE)
            y = _dot(sr[...].astype(BF16), cre_ref[sb], NT) - _dot(si[...].astype(BF16), cim_ref[sb], NT)
            yseg[:, us] = y + d_ref[:, us] * useg[:, us]
        _segments_to_rows(y_ref, yseg, steps, stage)
        yg_ref[...] = _gelu(y_ref[...]).astype(BF16)

    const = lambda a: pl.BlockSpec(a.shape, lambda c: (0,) * a.ndim)
    row_n = pl.BlockSpec((1, n_state), lambda c: (0, 0))
    st = pl.BlockSpec((1, 1, n_state), lambda c: (c, 0, 0))
    held = [pltpu.VMEM((nsb, MXU_DIM, SB_STATE), BF16)] * 4
    vmem = (4 * _nbytes((nsb, MXU_DIM, SB_STATE), BF16) + 4 * _nbytes((chunk, SB_STATE), F32)
            + 12 * _nbytes((chunk, w), F32) + 8 * _nbytes(bc_rows[0].shape, F32))
    res, landed = _call(
        body, [proj, proj, *bc_rows, *rows_p, d_row], name=name,
        out_shape=[jax.ShapeDtypeStruct((rows, w), F32), jax.ShapeDtypeStruct((nc, 1, n_state), F32),
                   jax.ShapeDtypeStruct((nc, 1, n_state), F32), jax.ShapeDtypeStruct((rows, w), BF16)],
        grid=(nc,),
        in_specs=_u_specs(w, o_u, chunk, lambda c: c) + [const(b) for b in bc_rows]
        + [row_n] * 6 + [pl.BlockSpec((steps, n_state), lambda c: (0, 0))] * 2 + [pl.BlockSpec((1, w), lambda c: (0, 0))],
        out_specs=[pl.BlockSpec((chunk, w), lambda c: (c, 0)), st, st, pl.BlockSpec((chunk, w), lambda c: (c, 0))],
        scratch_shapes=held + [pltpu.VMEM((chunk, w), F32), pltpu.VMEM((chunk, w), F32),
                               pltpu.VMEM((w // LANES, chunk, LANES), F32),
                               pltpu.VMEM((chunk, SB_STATE), F32), pltpu.VMEM((chunk, SB_STATE), F32),
                               pltpu.VMEM((1, n_state), F32), pltpu.VMEM((1, n_state), F32),
                               pltpu.VMEM((SUBLANES, n_state), F32), pltpu.VMEM((SUBLANES, n_state), F32)],
        semantics=("arbitrary",), vmem=vmem, rider=rider)
    return res if rider is None else (res, landed)


def _ssm_bwd(proj, o_u, y, dyg, st_re, st_im, bc_rows, rows_p, d_row, *, chunk, name, rider=None):
    rows = proj.shape[0]
    w = d_row.shape[1]
    nc = rows // chunk
    steps = chunk // SUBLANES
    nsb = w // MXU_DIM
    n_state = nsb * SB_STATE

    def body(ulo_ref, uhi_ref, y_ref, dyg_ref, str_ref, sti_ref, b2r_ref, b2i_ref, c2r_ref, c2i_ref, t2r_ref,
             t2i_ref, abr_ref, abi_ref, cfr_ref, cfi_ref, apr_ref, api_ref, pwr_ref, pwi_ref, d_ref,
             du_ref, gb2r_ref, gb2i_ref, gc2r_ref, gc2i_ref, gabr_ref, gabi_ref, dd_ref,
             bre_ref, bim_ref, cre_ref, cim_ref, btr_ref, bti_ref, dbre_ref, dbim_ref, dcre_ref, dcim_ref,
             useg, dyseg, dynat, stage, sr, si, lr, li, carry_r, carry_i, lam_r, lam_i, cm_r, cm_i, cl_r, cl_i):
        first = pl.program_id(0) == 0

        @pl.when(first)
        def _():
            for src, dst in ((b2r_ref, bre_ref), (b2i_ref, bim_ref), (c2r_ref, cre_ref), (c2i_ref, cim_ref),
                             (t2r_ref, btr_ref), (t2i_ref, bti_ref)):
                for sb in range(nsb):
                    dst[sb] = _block_diagonal(src[sb * MXU_DIM:(sb + 1) * MXU_DIM, :])
            lam_r[...] = jnp.zeros_like(lam_r)
            lam_i[...] = jnp.zeros_like(lam_i)
            for ref in (dbre_ref, dbim_ref, dcre_ref, dcim_ref, gabr_ref, gabi_ref, dd_ref):
                ref[...] = jnp.zeros_like(ref)

        dynat[...] = dyg_ref[...].astype(F32) * _dgelu(y_ref[...])
        half = w // 2
        dd_ref[:, :half] += jnp.sum(dynat[:, :half] * ulo_ref[...].astype(F32), axis=0, keepdims=True)
        dd_ref[:, half:] += jnp.sum(dynat[:, half:] * uhi_ref[...].astype(F32), axis=0, keepdims=True)
        _rows_to_segments(useg, [(ulo_ref, 0), (uhi_ref, half)], steps, stage)
        _rows_to_segments(dyseg, [(dynat, 0)], steps, stage)
        dy = dyseg[...]
        dyb = dy.astype(BF16)
        ub = useg[...].astype(BF16)
        carry_r[...] = str_ref[0]
        carry_i[...] = sti_ref[0]
        abr, abi = abr_ref[...], abi_ref[...]
        apr, api = apr_ref[...], api_ref[...]
        for sb in range(nsb):
            us = slice(sb * MXU_DIM, (sb + 1) * MXU_DIM)
            ss = slice(sb * SB_STATE, (sb + 1) * SB_STATE)
            base = sb * SB_STATE
            br = _dot(ub[:, us], bre_ref[sb], NN)
            bi = _dot(ub[:, us], bim_ref[sb], NN)
            xr, xi = _cmul(cfr_ref[:, ss], cfi_ref[:, ss], br, bi)
            sr[...] = xr
            si[...] = xi
            lr[...] = _dot(dyb[:, us], cre_ref[sb], NN)
            li[...] = -_dot(dyb[:, us], cim_ref[sb], NN)
            _scan_segments(sr, si, abr, abi, apr, api, pwr_ref, pwi_ref, carry_r, carry_i, cm_r, cm_i, steps, False,
                           base)
            dcre_ref[sb] += _dot(dyb[:, us], sr[...].astype(BF16), TN)
            dcim_ref[sb] -= _dot(dyb[:, us], si[...].astype(BF16), TN)
            _scan_segments(lr, li, abr, -abi, apr, -api, pwr_ref, pwi_ref, lam_r, lam_i, cl_r, cl_i, steps, True,
                           base)
            for c0 in range(0, SB_STATE, SCAN_LANES):
                ls = slice(c0, c0 + SCAN_LANES)
                gs = slice(base + c0, base + c0 + SCAN_LANES)

                def step(j, acc, ls=ls):
                    gar, gai, pr, pi = acc
                    r0 = pl.multiple_of(j * SUBLANES, SUBLANES)
                    rws = pl.ds(r0, SUBLANES)
                    t_r, t_i = _cmul(pr, -pi, lr[rws, ls], li[rws, ls])
                    return gar + t_r, gai + t_i, sr[rws, ls], si[rws, ls]

                zero = jnp.zeros((SUBLANES, SCAN_LANES), F32)
                gar, gai, _, _ = lax.fori_loop(0, steps, step, (zero, zero, cm_r[:, gs], cm_i[:, gs]))
                gabr_ref[:, gs] += gar
                gabi_ref[:, gs] += gai
            xr, xi = lr[...].astype(BF16), li[...].astype(BF16)
            du = _dot(xr, btr_ref[sb], NT) + _dot(xi, bti_ref[sb], NT)
            useg[:, us] = du + d_ref[:, us] * dy[:, us]
            dbre_ref[sb] += _dot(ub[:, us], xr, TN)
            dbim_ref[sb] += _dot(ub[:, us], xi, TN)
        _segments_to_rows(du_ref, useg, steps, stage)

        @pl.when(pl.program_id(0) == nc - 1)
        def _():
            for src, dst in ((dbre_ref, gb2r_ref), (dbim_ref, gb2i_ref), (dcre_ref, gc2r_ref), (dcim_ref, gc2i_ref)):
                for sb in range(nsb):
                    dst[sb * MXU_DIM:(sb + 1) * MXU_DIM, :] = _block_rows(src[sb])

    rev = lambda c: nc - 1 - c
    const = lambda a: pl.BlockSpec(a.shape, lambda c: (0,) * a.ndim)
    tile = pl.BlockSpec((chunk, w), lambda c: (rev(c), 0))
    row_n = pl.BlockSpec((1, n_state), lambda c: (0, 0))
    row_w = pl.BlockSpec((1, w), lambda c: (0, 0))
    st = pl.BlockSpec((1, 1, n_state), lambda c: (rev(c), 0, 0))
    acc8 = pl.BlockSpec((SUBLANES, n_state), lambda c: (0, 0))
    big = pltpu.VMEM((chunk, SB_STATE), F32)
    small = pltpu.VMEM((chunk, w), F32)
    row = pltpu.VMEM((1, n_state), F32)
    eight = pltpu.VMEM((SUBLANES, n_state), F32)
    blk = (nsb, MXU_DIM, SB_STATE)
    held = [pltpu.VMEM(blk, BF16)] * 6 + [pltpu.VMEM(blk, F32)] * 4
    vmem = (6 * _nbytes(blk, BF16) + 4 * _nbytes(blk, F32) + 5 * _nbytes((chunk, SB_STATE), F32)
            + 12 * _nbytes((chunk, w), F32) + 20 * _nbytes(bc_rows[0].shape, F32))
    res, landed = _call(
        body, [proj, proj, y, dyg, st_re, st_im, *bc_rows, *rows_p, d_row], name=name,
        out_shape=[jax.ShapeDtypeStruct((rows, w), F32)] + [jax.ShapeDtypeStruct(b.shape, F32) for b in bc_rows[:4]]
        + [jax.ShapeDtypeStruct((SUBLANES, n_state), F32)] * 2 + [jax.ShapeDtypeStruct((1, w), F32)],
        grid=(nc,),
        in_specs=_u_specs(w, o_u, chunk, rev) + [tile, tile, st, st] + [const(b) for b in bc_rows]
        + [row_n] * 6 + [pl.BlockSpec((steps, n_state), lambda c: (0, 0))] * 2 + [row_w],
        out_specs=[tile] + [const(b) for b in bc_rows[:4]] + [acc8] * 2 + [row_w],
        scratch_shapes=held + [small] * 3 + [pltpu.VMEM((w // LANES, chunk, LANES), F32)] + [big] * 4 + [row] * 4
        + [eight] * 4,
        semantics=("arbitrary",), vmem=vmem, rider=rider)
    return res if rider is None else (res, landed)


def _loss_grad(x, mm, target, name):
    rows, d = x.shape

    def fn(xv, mv, tv):
        err = xv + mv - tv
        g = err * (1.0 / d)
        return g, g, 0.5 * err * g

    return _ew(fn, name=name, rows=rows, width=d, tiles=[(x, 0), (mm, 0), (target, 0)],
               outs=[(F32, d, 0), (BF16, d, 0)], accs=1)


def _pair_sum(grad, recv, name):
    r4, cdim = recv.shape
    r = r4 // N_CHIPS
    tr = _tile(r, 544, 16)
    g4 = grad.reshape(N_CHIPS, 2, r, cdim)
    r3 = recv.reshape(N_CHIPS, r, cdim)
    core = jnp.reshape(lax.axis_index("c"), (1,)).astype(jnp.int32)

    def body(c_ref, g_ref, r_ref, o_ref):
        o_ref[...] = (g_ref[0] + r_ref[...]).astype(BF16)

    out = _pallas(
        body, name=name, out_shape=jax.ShapeDtypeStruct((N_CHIPS, r, cdim), BF16),
        grid_spec=pltpu.PrefetchScalarGridSpec(
            num_scalar_prefetch=1, grid=(N_CHIPS, r // tr),
            in_specs=[pl.BlockSpec((1, 1, tr, cdim), lambda j, i, c: (j, c[0], i, 0)),
                      pl.BlockSpec((1, tr, cdim), lambda j, i, c: (j, i, 0))],
            out_specs=pl.BlockSpec((1, tr, cdim), lambda j, i, c: (j, i, 0))),
        compiler_params=_params(("parallel", "parallel"), 6 * _nbytes((tr, cdim), F32)),
    )(core, g4, r3)
    return out.reshape(r4, cdim)


def _chip_sum(recv, name):
    r4, cdim = recv.shape
    r = r4 // N_CHIPS
    tr = _tile(r, 544, 16)
    r3 = recv.reshape(N_CHIPS, r, cdim)

    def body(r_ref, o_ref):
        acc = r_ref[0].astype(F32)
        for j in range(1, N_CHIPS):
            acc = acc + r_ref[j].astype(F32)
        o_ref[...] = acc

    return _pallas(
        body, name=name, out_shape=jax.ShapeDtypeStruct((r, cdim), F32), grid=(r // tr,),
        in_specs=[pl.BlockSpec((N_CHIPS, tr, cdim), lambda i: (0, i, 0))],
        out_specs=pl.BlockSpec((tr, cdim), lambda i: (i, 0)),
        compiler_params=_params(("parallel",), 8 * _nbytes((tr, cdim), F32)),
    )(r3)


def _adamw_math(w, g, m, v):
    m = ADAM_B1 * m + (1.0 - ADAM_B1) * g
    v = ADAM_B2 * v + (1.0 - ADAM_B2) * (g * g)
    m_hat = m / (1.0 - ADAM_B1 ** ADAM_STEP)
    v_hat = v / (1.0 - ADAM_B2 ** ADAM_STEP)
    delta = -ADAM_LR * (m_hat / (jnp.sqrt(v_hat) + ADAM_EPS) + ADAM_WD * w)
    return delta, m, v


def _adamw(w, g, m, v, name):
    rows, cols = w.shape
    tr = _tile(rows, 256, SUBLANES)

    def body(w_ref, g_ref, m_ref, v_ref, d_ref, nm_ref, nv_ref):
        d, nm, nv = _adamw_math(w_ref[...], g_ref[...], m_ref[...], v_ref[...])
        d_ref[...] = d
        nm_ref[...] = nm
        nv_ref[...] = nv

    spec = pl.BlockSpec((tr, cols), lambda i: (i, 0))
    shp = jax.ShapeDtypeStruct((rows, cols), F32)
    return _pallas(
        body, name=name, out_shape=[shp] * 3, grid=(rows // tr,), in_specs=[spec] * 4, out_specs=[spec] * 3,
        compiler_params=_params(("parallel",)),
    )(w, g, m, v)


def _adamw_chips(w, parts, m, v, name):
    rows, cols = w.shape
    assert sum(p.shape[1] for p in parts) == cols
    tr = _tile(rows, 64, 16)
    n = len(parts)

    def body(*refs):
        w_ref, m_ref, v_ref = refs[0], refs[1 + n], refs[2 + n]
        g_ref, d_ref, nm_ref, nv_ref = refs[3 + n:]
        cols_g = []
        for p_ref in refs[1:1 + n]:
            acc = p_ref[0].astype(F32)
            for j in range(1, N_CHIPS):
                acc = acc + p_ref[j].astype(F32)
            cols_g.append(acc)
        g = cols_g[0] if n == 1 else jnp.concatenate(cols_g, axis=1)
        d, nm, nv = _adamw_math(w_ref[...], g, m_ref[...], v_ref[...])
        g_ref[...] = g
        d_ref[...] = d
        nm_ref[...] = nm
        nv_ref[...] = nv

    spec = pl.BlockSpec((tr, cols), lambda i: (i, 0))
    part_specs = [pl.BlockSpec((N_CHIPS, tr, p.shape[1]), lambda i: (0, i, 0)) for p in parts]
    shp = jax.ShapeDtypeStruct((rows, cols), F32)
    return _pallas(
        body, name=name, out_shape=[shp] * 4, grid=(rows // tr,),
        in_specs=[spec] + part_specs + [spec, spec], out_specs=[spec] * 4,
        compiler_params=_params(("parallel",)),
    )(w, *[p.reshape(N_CHIPS, rows, p.shape[1]) for p in parts], m, v)


def _adamw_small(w, parts, m, v, name):
    rows, cols = w.shape
    p3 = parts.reshape(N_DEV, rows, cols)

    def body(w_ref, p_ref, m_ref, v_ref, g_ref, d_ref, nm_ref, nv_ref):
        g = p_ref[0]
        for k in range(1, N_DEV):
            g = g + p_ref[k]
        d, nm, nv = _adamw_math(w_ref[...], g, m_ref[...], v_ref[...])
        g_ref[...] = g
        d_ref[...] = d
        nm_ref[...] = nm
        nv_ref[...] = nv

    shp = jax.ShapeDtypeStruct((rows, cols), F32)
    return _pallas(body, name=name, out_shape=[shp] * 4)(w, p3, m, v)


SMALL = ("norm_w", "q_norm_w", "k_norm_w", "sinks", "A_re", "A_im", "log_dt", "B_re", "B_im", "C_re", "C_im",
         "D_skip", "b_glu")
LARGE = ("w_in", "w_attn_proj", "w_glu", "w_ssm_proj", "w_out")
ORDER = ("norm_w", "w_in", "q_norm_w", "k_norm_w", "sinks", "w_attn_proj", "A_re", "A_im", "log_dt", "B_re", "B_im",
         "C_re", "C_im", "D_skip", "w_glu", "b_glu", "w_ssm_proj", "w_out")


SMALL_REST = ("loss",) + SMALL[1:]


def _pack(named, keys):
    flat = jnp.concatenate([named[k].reshape(-1).astype(F32) for k in keys])
    n = flat.shape[0]
    rows = -(-n // (LANES * SUBLANES)) * SUBLANES
    return jnp.pad(flat, (0, rows * LANES - n)).reshape(rows, LANES)


def _unpack(packed, like, keys):
    flat = packed.reshape(-1)
    out, o = {}, 0
    for k in keys:
        n = like[k].size
        out[k] = flat[o:o + n].reshape(like[k].shape)
        o += n
    return out


def _step(xs, target, p, shards):
    s_in, s_ap, s_glu, s_sp, s_o = shards
    seq, d = xs.shape
    attn_w = (d // 128) * HEAD_DIM
    n_q = attn_w // HEAD_DIM
    kv_w = N_KV_HEADS * HEAD_DIM
    ssm_w = d // 2
    n_groups = ssm_w // GROUP
    n_state = n_groups * STATE
    in_w = N_DEV * s_in.shape[0]
    assert in_w == 2 * attn_w + 2 * kv_w + 2 * ssm_w + 2 * d
    o_u = 2 * attn_w + 2 * kv_w
    o_z = o_u + ssm_w
    o_ga = o_z + ssm_w
    chunk = min(SSM_CHUNK, seq)
    cw = d // 4

    norm_row = p["norm_w"].reshape(1, d)
    h = _rmsnorm_fwd(xs, norm_row, "rmsnorm_fwd")
    half = d // W_IN_PARTS
    assert W_IN_PARTS == 2
    s_in_parts = [s_in[:, :half], s_in[:, half:]]
    (w_lo,) = _exchange(_all_gather(s_in_parts[:1]), "gather_w_in_0")
    part, (w_hi,) = _matmul(Cols(h, 0, half), w_lo, mode="nt", name="in_proj_0", tn=512, out_dtype=BF16,
                            rider=_all_gather(s_in_parts[1:]))
    proj = _matmul(Cols(h, half, half), w_hi, mode="nt", name="in_proj_1", tn=512, out_dtype=BF16, add=part)
    w_in_parts = [w_lo, w_hi]
    qw_row = jnp.tile(p["q_norm_w"], n_q).reshape(1, attn_w)
    kw_row = jnp.tile(p["k_norm_w"], N_KV_HEADS).reshape(1, kv_w)
    gmat = _head_mean_matrix()
    ag = _attention_fwd(proj, qw_row, kw_row, gmat, p["sinks"], attn_w=attn_w, kv_w=kv_w, name="attention_fwd")

    log_dt_col = p["log_dt"].reshape(n_groups, 1)
    prep = _ssm_prep(p["A_re"], p["A_im"], log_dt_col, chunk // SUBLANES, "ssm_prep")
    rows_p = [v.reshape(1, n_state) for v in prep[:6]] + [v.reshape(-1, n_state) for v in prep[6:]]
    bt_re, bt_im = p["B_re"].transpose(0, 2, 1), p["B_im"].transpose(0, 2, 1)
    cf_re, cf_im = prep[2][:, None, :], prep[3][:, None, :]
    bc_rows = [_ssm_rows(m) for m in (bt_re, bt_im, p["C_re"], p["C_im"],
                                      cf_re * bt_re - cf_im * bt_im, cf_re * bt_im + cf_im * bt_re)]
    d_row = p["D_skip"].reshape(1, ssm_w)
    (y_ssm, st_re, st_im, yg), (w_ap_t, w_glu_t, w_sp_t, w_o) = _ssm_fwd(
        proj, o_u, bc_rows[:4], rows_p, d_row, chunk=chunk, name="ssm_fwd",
        rider=_all_gather([s_ap, s_glu, s_sp, s_o]))
    glu = _matmul(yg, w_glu_t, mode="nt", name="glu_proj", out_dtype=BF16, bias=p["b_glu"].reshape(1, 2 * ssm_w))
    (ts,) = _ew(lambda ga, gb, z: ga * _sigmoid(gb) * _silu(z), name="glu_gate", rows=seq, width=ssm_w,
                tiles=[(glu, 0), (glu, ssm_w), (proj, o_z)], outs=[(BF16, ssm_w, 0)], cw=cw)
    yy = _matmul(ag, w_ap_t, mode="nt", name="attn_proj", out_dtype=BF16, out_cols=(2 * d, 0))
    yy = _matmul(ts, w_sp_t, mode="nt", name="ssm_proj", out_dtype=BF16, out_cols=(2 * d, d), into=yy)
    (merged,) = _ew(lambda ya, ys, ga, gs: _sigmoid(ga) * ya + _sigmoid(gs) * ys, name="merge", rows=seq, width=d,
                    tiles=[(yy, 0), (yy, d), (proj, o_ga), (proj, o_ga + d)], outs=[(BF16, d, 0)], cw=cw)
    mm = _matmul(merged, w_o, mode="nn", name="out_proj")
    dout, dout_b, loss_cols = _loss_grad(xs, mm, target, "loss_grad")
    loss_local = jnp.sum(loss_cols)

    g_w_o = _matmul(merged, dout_b, mode="tn", name="grad_w_out", tm=512, tk=4096)
    dmerged, (sib_o,) = _matmul(dout_b, w_o, mode="nt", name="d_merged", out_dtype=BF16,
                                rider=_sibling_exchange([g_w_o]))
    pair_o = _pair_sum(g_w_o, sib_o, "pair_sum_w_out")

    def merge_bwd(dm, y, g):
        s = _sigmoid(g)
        return dm * s, dm * y * s * (1.0 - s)

    dyy, dproj = _ew(merge_bwd, name="merge_bwd", rows=seq, width=2 * d,
                     tiles=[(dmerged, 0, d), (yy, 0), (proj, o_ga)],
                     outs=[(BF16, 2 * d, 0), (BF16, in_w, o_ga)], cw=cw)
    dy_a, dy_s = Cols(dyy, 0, d), Cols(dyy, d, d)
    g_w_ap_t = _matmul(dy_a, ag, mode="tn", name="grad_w_attn_proj", tm=512, tk=4096)
    g_w_sp_t = _matmul(dy_s, ts, mode="tn", name="grad_w_ssm_proj", tm=512, tk=4096)
    d_ag = _matmul(dy_a, w_ap_t, mode="nn", name="d_attn_gated", out_dtype=BF16)
    d_ts = _matmul(dy_s, w_sp_t, mode="nn", name="d_ssm_gated", out_dtype=BF16)

    (dproj, dkv, g_qw, g_kw, g_sinks), (chips_o, sib_ap, sib_sp) = _attention_bwd(
        proj, d_ag, dproj, qw_row, kw_row, gmat, p["sinks"], attn_w=attn_w, kv_w=kv_w, name="attention_bwd",
        rider=_join(_chip_exchange([pair_o]), _sibling_exchange([g_w_ap_t, g_w_sp_t])))
    pair_ap = _pair_sum(g_w_ap_t, sib_ap, "pair_sum_w_attn_proj")
    pair_sp = _pair_sum(g_w_sp_t, sib_sp, "pair_sum_w_ssm_proj")
    dproj = _attention_dkv(dproj, dkv, attn_w=attn_w, kv_w=kv_w, name="attention_dkv")

    n_half = ssm_w // _tile(2 * ssm_w, cw)

    def glu_bwd(j, dt, ga, gb, z):
        sb, sz = _sigmoid(gb), _silu(z)
        dg = jnp.where(j < n_half, dt * sb * sz, dt * ga * sb * (1.0 - sb) * sz)
        return dg, dg

    glu_ops = [(d_ts, 0, ssm_w), (glu, 0, ssm_w), (glu, ssm_w, ssm_w), (proj, o_z, ssm_w)]
    dglu, g_bglu = _ew(glu_bwd, name="glu_bwd", rows=seq, width=2 * ssm_w, tiles=glu_ops,
                       outs=[(BF16, 2 * ssm_w, 0)], accs=1, cw=cw, with_col=True)
    (dproj,) = _ew(lambda dt, ga, gb, z: dt * ga * _sigmoid(gb) * _dsilu(z), name="glu_bwd_z", rows=seq,
                   width=ssm_w, tiles=glu_ops, outs=[(BF16, in_w, o_z)], into=[dproj], cw=cw)
    g_w_glu_t = _matmul(dglu, yg, mode="tn", name="grad_w_glu", tm=512, tk=4096)
    d_yg = _matmul(dglu, w_glu_t, mode="nn", name="d_gelu", out_dtype=BF16)
    ((du, dbt_re, dbt_im, dc_re, dc_im, gabr, gabi, g_d), (chips_ap, chips_sp, sib_glu)) = _ssm_bwd(
        proj, o_u, y_ssm, d_yg, st_re, st_im, bc_rows, rows_p, d_row, chunk=chunk, name="ssm_bwd",
        rider=_join(_chip_exchange([pair_ap, pair_sp]), _sibling_exchange([g_w_glu_t])))
    pair_glu = _pair_sum(g_w_glu_t, sib_glu, "pair_sum_w_glu")
    (dproj,) = _ew(lambda v: v, name="du_store", rows=seq, width=ssm_w, tiles=[(du, 0)],
                   outs=[(BF16, in_w, o_u)], into=[dproj], cw=cw)
    g_a_re, g_a_im, g_log_dt, g_bt_re, g_bt_im = _ssm_param_bwd(
        p["A_re"], p["A_im"], log_dt_col, *[g.reshape(SUBLANES, n_groups, STATE) for g in (gabr, gabi)],
        bt_re, bt_im, _from_ssm_rows(dbt_re), _from_ssm_rows(dbt_im), "ssm_param_bwd")
    small_grads = dict(
        loss=loss_local, q_norm_w=g_qw.reshape(n_q, HEAD_DIM).sum(0), k_norm_w=g_kw.reshape(N_KV_HEADS, HEAD_DIM).sum(0),
        sinks=g_sinks[0, :n_q], A_re=g_a_re, A_im=g_a_im, log_dt=g_log_dt.reshape(n_groups),
        B_re=g_bt_re.transpose(0, 2, 1), B_im=g_bt_im.transpose(0, 2, 1),
        C_re=_from_ssm_rows(dc_re), C_im=_from_ssm_rows(dc_im),
        D_skip=g_d.reshape(n_groups, GROUP), b_glu=g_bglu.reshape(2 * ssm_w))

    n_parts = W_IN_PARTS
    wq = d // n_parts
    g_parts, pair_parts, chip_parts = [], [], []
    extra = [_chip_exchange([pair_glu]), _all_gather([_pack(small_grads, SMALL_REST)])]
    chips_glu = small_parts = dh = None
    for step in range(n_parts + 2):
        riders = list(extra) if step == 0 else []
        if 0 <= step - 2 < n_parts:
            riders.append(_chip_exchange([pair_parts[step - 2]]))
        if 0 <= step - 1 < n_parts:
            riders.append(_sibling_exchange([g_parts[step - 1]]))
        rider = _join(*riders) if riders else None
        if step < n_parts:
            res = _matmul(dproj, Cols(h, step * wq, wq), mode="tn", name="grad_w_in_%d" % step, tk=4096, rider=rider)
            out, landed = res if rider is not None else (res, [])
            g_parts.append(out)
        else:
            q = step - n_parts
            dh, landed = _matmul(dproj, w_in_parts[q], mode="nn", name="d_normed_%d" % q, tk=2176,
                                 out_cols=(d, q * wq), into=dh, rider=rider)
        landed = list(landed)
        if step == 0:
            chips_glu, small_parts = landed[:2]
            landed = landed[2:]
        if 0 <= step - 2 < n_parts:
            chip_parts.append(landed.pop(0))
        if 0 <= step - 1 < n_parts:
            pair_parts.append(_pair_sum(g_parts[step - 1], landed.pop(0), "pair_sum_w_in_%d" % (step - 1)))
    grad_x, g_norm = _rmsnorm_bwd(xs, norm_row, dh, dout, "rmsnorm_bwd")
    (norm_parts,) = _exchange(_all_gather([_pack(dict(norm_w=g_norm), ("norm_w",))]), "gather_norm_grad")
    from_chips = dict(zip(LARGE, (chip_parts, [chips_ap], [chips_glu], [chips_sp], [chips_o])))
    return grad_x, from_chips, small_parts, norm_parts


def kernel(x, norm_w, w_in, q_norm_w, k_norm_w, sinks, w_attn_proj, A_re, A_im, log_dt, B_re, B_im, C_re, C_im, D_skip, w_glu, b_glu, w_ssm_proj, w_out, loss_target, m_norm_w, m_w_in, m_q_norm_w, m_k_norm_w, m_sinks, m_w_attn_proj, m_A_re, m_A_im, m_log_dt, m_B_re, m_B_im, m_C_re, m_C_im, m_D_skip, m_w_glu, m_b_glu, m_w_ssm_proj, m_w_out, v_norm_w, v_w_in, v_q_norm_w, v_k_norm_w, v_sinks, v_w_attn_proj, v_A_re, v_A_im, v_log_dt, v_B_re, v_B_im, v_C_re, v_C_im, v_D_skip, v_w_glu, v_b_glu, v_w_ssm_proj, v_w_out):
    weights = dict(norm_w=norm_w, w_in=w_in, q_norm_w=q_norm_w, k_norm_w=k_norm_w, sinks=sinks,
                   w_attn_proj=w_attn_proj, A_re=A_re, A_im=A_im, log_dt=log_dt, B_re=B_re, B_im=B_im, C_re=C_re,
                   C_im=C_im, D_skip=D_skip, w_glu=w_glu, b_glu=b_glu, w_ssm_proj=w_ssm_proj, w_out=w_out)
    m_in = dict(norm_w=m_norm_w, w_in=m_w_in, q_norm_w=m_q_norm_w, k_norm_w=m_k_norm_w, sinks=m_sinks,
                w_attn_proj=m_w_attn_proj, A_re=m_A_re, A_im=m_A_im, log_dt=m_log_dt, B_re=m_B_re, B_im=m_B_im,
                C_re=m_C_re, C_im=m_C_im, D_skip=m_D_skip, w_glu=m_w_glu, b_glu=m_b_glu, w_ssm_proj=m_w_ssm_proj,
                w_out=m_w_out)
    v_in = dict(norm_w=v_norm_w, w_in=v_w_in, q_norm_w=v_q_norm_w, k_norm_w=v_k_norm_w, sinks=v_sinks,
                w_attn_proj=v_w_attn_proj, A_re=v_A_re, A_im=v_A_im, log_dt=v_log_dt, B_re=v_B_re, B_im=v_B_im,
                C_re=v_C_re, C_im=v_C_im, D_skip=v_D_skip, w_glu=v_w_glu, b_glu=v_b_glu, w_ssm_proj=v_w_ssm_proj,
                w_out=v_w_out)

    _, seq, d = x.shape
    column_sharded = LARGE[:4]
    as_rows = lambda k, a: a.T if k in column_sharded else a
    shards = [as_rows(k, weights[k]).astype(BF16) for k in LARGE]
    small = {k: weights[k] for k in SMALL}
    grad_x, from_chips, small_parts, norm_parts = _step(x.reshape(seq, d), loss_target.reshape(seq, d), small,
                                                        shards)

    grads, delta, new_m, new_v = {}, {}, {}, {}
    for k in LARGE:
        if k == "w_in":
            res = _adamw_chips(weights[k].T, from_chips[k], m_in[k].T, v_in[k].T, "adamw_" + k)
            grads[k], delta[k], new_m[k], new_v[k] = [a.T for a in res]
        elif k == "w_out":
            grads[k], delta[k], new_m[k], new_v[k] = _adamw_chips(weights[k], from_chips[k], m_in[k], v_in[k],
                                                                  "adamw_" + k)
        else:
            grads[k] = _chip_sum(from_chips[k][0], "chip_sum_" + k).T
            delta[k], new_m[k], new_v[k] = _adamw(weights[k], grads[k], m_in[k], v_in[k], "adamw_" + k)

    zero = jnp.zeros((), F32)
    for keys, parts in ((SMALL_REST, small_parts), (("norm_w",), norm_parts)):
        like = dict(small, loss=zero)
        packs = [_pack(dict(src, loss=zero), keys) for src in (weights, m_in, v_in)]
        res = _adamw_small(packs[0], parts, packs[1], packs[2], "adamw_small_%d" % len(keys))
        for dst, r in zip((grads, delta, new_m, new_v), res):
            dst.update(_unpack(r, like, keys))
    loss = grads["loss"]

    return (loss, grad_x.reshape(x.shape), *[grads[k] for k in ORDER], *[delta[k] for k in ORDER],
            *[new_m[k] for k in ORDER], *[new_v[k] for k in ORDER])
```

```python
import math
from typing import Callable, NamedTuple

import jax
import jax.numpy as jnp
import numpy as np
from jax import lax
from jax.experimental import pallas as pl
from jax.experimental.pallas import tpu as pltpu

F32 = jnp.float32
BF16 = jnp.bfloat16
MESH = pl.DeviceIdType.MESH

HEAD_DIM = 64
N_KV_HEADS = 4
GROUP = 16
STATE = 64
BLOCK = 128
NORM_EPS = 1e-6
N_DEV = 8
N_CHIPS = 4
LANES = 128
SUBLANES = 8
MXU_DIM = 256
VMEM_BYTES = 64 * 1024 * 1024
VMEM_CAP = VMEM_BYTES - 8 * 1024 * 1024

ADAM_LR = 0.001
ADAM_B1 = 0.9
ADAM_B2 = 0.999
ADAM_EPS = 1e-08
ADAM_WD = 0.01
ADAM_STEP = 10

GELU_C = math.sqrt(2.0 / math.pi)
GELU_K = 0.044715


def _tile(dim, pref, mult=LANES):
    if dim <= pref:
        return dim
    best = None
    for d in range(mult, pref + 1, mult):
        if dim % d == 0:
            best = d
    assert best is not None, (dim, pref, mult)
    return best


def _params(semantics=None, vmem=None):
    kw = {}
    if semantics is not None:
        kw["dimension_semantics"] = semantics
    if vmem is not None:
        kw["vmem_limit_bytes"] = int(min(VMEM_CAP, max(vmem, 32 * 1024 * 1024)))
    return pltpu.CompilerParams(**kw)


def _nbytes(shape, dtype):
    return math.prod(shape) * jnp.dtype(dtype).itemsize


def _sigmoid(x):
    return 1.0 / (1.0 + jnp.exp(-x))


def _silu(x):
    return x * _sigmoid(x)


def _dsilu(x):
    s = _sigmoid(x)
    return s * (1.0 + x * (1.0 - s))


def _gelu(x):
    return 0.5 * x * (1.0 + jnp.tanh(GELU_C * (x + GELU_K * x * x * x)))


def _dgelu(x):
    t = jnp.tanh(GELU_C * (x + GELU_K * x * x * x))
    return 0.5 * (1.0 + t) + 0.5 * x * (1.0 - t * t) * GELU_C * (1.0 + 3.0 * GELU_K * x * x)


def _dot(a, b, dims):
    return lax.dot_general(a, b, (dims, ((), ())), preferred_element_type=F32)


NN = ((1,), (0,))
NT = ((1,), (1,))
TN = ((0,), (0,))


def _any_spec():
    return pl.BlockSpec(memory_space=pl.ANY)


def _pallas(body, **kw):
    pin = lambda s: pltpu.HBM(s.shape, s.dtype) if isinstance(s, jax.ShapeDtypeStruct) else s
    out_shape = kw.pop("out_shape")
    out_shape = [pin(s) for s in out_shape] if isinstance(out_shape, (list, tuple)) else pin(out_shape)
    call = pl.pallas_call(body, out_shape=out_shape, **kw)

    def run(*operands):
        pinned = [pltpu.with_memory_space_constraint(o, pltpu.HBM) if jnp.issubdtype(o.dtype, jnp.floating) else o
                  for o in operands]
        return call(*pinned)

    return run


class Rider(NamedTuple):
    operands: tuple
    out_shapes: tuple
    sems: tuple
    start: Callable
    finish: Callable


def _all_gather(shards):
    n = len(shards)

    def copies(ins, outs, sems):
        send_sems, recv_sems, local_sems = sems
        x, y, c = lax.axis_index("x"), lax.axis_index("y"), lax.axis_index("c")
        me, sibling = (x, y, c), (x, y, 1 - c)
        chips = [(1 - x, y), (x, 1 - y), (1 - x, 1 - y)]

        def rows(k, px, py, pc):
            r = shards[k].shape[0]
            return outs[k].at[pl.ds((4 * px + 2 * py + pc) * r, r), :]

        def copy(k, s, block, to, src=None):
            return pltpu.make_async_remote_copy(
                src_ref=rows(k, *block) if src is None else src, dst_ref=rows(k, *block),
                send_sem=send_sems.at[7 * k + s], recv_sem=recv_sems.at[7 * k + s],
                device_id=to, device_id_type=MESH)

        mine = [pltpu.make_async_copy(ins[k], rows(k, *me), local_sems.at[k]) for k in range(n)]
        first = []
        for k in range(n):
            first.append(copy(k, 0, me, sibling, src=ins[k]))
            first += [copy(k, 1 + j, me, (*chip, c), src=ins[k]) for j, chip in enumerate(chips)]
        return me, sibling, chips, c, copy, mine, first

    def start(ins, outs, sems):
        *_, mine, first = copies(ins, outs, sems)
        for cp in mine + first:
            cp.start()

    def finish(ins, outs, sems):
        me, sibling, chips, c, copy, mine, first = copies(ins, outs, sems)
        passed = []
        for j, chip in enumerate(chips):
            for k in range(n):
                copy(k, 1 + j, (*chip, c), me).wait_recv()
                fwd = copy(k, 4 + j, (*chip, c), sibling)
                fwd.start()
                passed.append(fwd)
        for k in range(n):
            copy(k, 0, sibling, me).wait_recv()
            for j, chip in enumerate(chips):
                copy(k, 4 + j, (*chip, 1 - c), me).wait_recv()
        for cp in first + passed:
            cp.wait_send()
        for cp in mine:
            cp.wait()

    return Rider(
        tuple(shards),
        tuple(jax.ShapeDtypeStruct((N_DEV * s.shape[0], s.shape[1]), s.dtype) for s in shards),
        (pltpu.SemaphoreType.DMA((7 * n,)), pltpu.SemaphoreType.DMA((7 * n,)), pltpu.SemaphoreType.DMA((n,))),
        start, finish)


def _sibling_exchange(grads):
    n = len(grads)

    def copies(ins, outs, sems):
        send_sems, recv_sems = sems
        x, y, c = lax.axis_index("x"), lax.axis_index("y"), lax.axis_index("c")
        out = []
        for k in range(n):
            r = grads[k].shape[0] // N_DEV
            for j in range(N_CHIPS):
                out.append(pltpu.make_async_remote_copy(
                    src_ref=ins[k].at[pl.ds((2 * j + 1 - c) * r, r), :],
                    dst_ref=outs[k].at[pl.ds(j * r, r), :],
                    send_sem=send_sems.at[N_CHIPS * k + j], recv_sem=recv_sems.at[N_CHIPS * k + j],
                    device_id=(x, y, 1 - c), device_id_type=MESH))
        return out

    def start(ins, outs, sems):
        for cp in copies(ins, outs, sems):
            cp.start()

    def finish(ins, outs, sems):
        for cp in copies(ins, outs, sems):
            cp.wait()

    return Rider(
        tuple(grads), tuple(jax.ShapeDtypeStruct((g.shape[0] // 2, g.shape[1]), g.dtype) for g in grads),
        (pltpu.SemaphoreType.DMA((N_CHIPS * n,)), pltpu.SemaphoreType.DMA((N_CHIPS * n,))), start, finish)


def _chip_exchange(parts):
    n = len(parts)

    def copies(ins, outs, sems):
        send_sems, recv_sems, local_sems = sems
        x, y, c = lax.axis_index("x"), lax.axis_index("y"), lax.axis_index("c")
        my_chip = 2 * x + y
        chips = [(1 - x, y), (x, 1 - y), (1 - x, 1 - y)]
        local, sent = [], []
        for k in range(n):
            r = parts[k].shape[0] // N_CHIPS
            mine = pl.ds(my_chip * r, r)
            local.append(pltpu.make_async_copy(ins[k].at[mine, :], outs[k].at[mine, :], local_sems.at[k]))
            for s, (px, py) in enumerate(chips):
                sent.append(pltpu.make_async_remote_copy(
                    src_ref=ins[k].at[pl.ds((2 * px + py) * r, r), :], dst_ref=outs[k].at[mine, :],
                    send_sem=send_sems.at[3 * k + s], recv_sem=recv_sems.at[3 * k + s],
                    device_id=(px, py, c), device_id_type=MESH))
        return local, sent

    def start(ins, outs, sems):
        local, sent = copies(ins, outs, sems)
        for cp in local + sent:
            cp.start()

    def finish(ins, outs, sems):
        local, sent = copies(ins, outs, sems)
        for cp in sent + local:
            cp.wait()

    return Rider(
        tuple(parts), tuple(jax.ShapeDtypeStruct(p.shape, p.dtype) for p in parts),
        (pltpu.SemaphoreType.DMA((3 * n,)), pltpu.SemaphoreType.DMA((3 * n,)), pltpu.SemaphoreType.DMA((n,))),
        start, finish)


def _join(*riders):
    cuts_in, cuts_out, cuts_sem = [0], [0], [0]
    for r in riders:
        cuts_in.append(cuts_in[-1] + len(r.operands))
        cuts_out.append(cuts_out[-1] + len(r.out_shapes))
        cuts_sem.append(cuts_sem[-1] + len(r.sems))

    def each(which):
        def run(ins, outs, sems):
            for i, r in enumerate(riders):
                getattr(r, which)(ins[cuts_in[i]:cuts_in[i + 1]], outs[cuts_out[i]:cuts_out[i + 1]],
                                  sems[cuts_sem[i]:cuts_sem[i + 1]])
        return run

    return Rider(sum((r.operands for r in riders), ()), sum((r.out_shapes for r in riders), ()),
                 sum((r.sems for r in riders), ()), each("start"), each("finish"))


def _call(body, operands, *, name, out_shape, grid, in_specs, out_specs, scratch_shapes=(), aliases=None,
          semantics=None, vmem=None, rider=None):
    operands, out_shape, scratch_shapes = list(operands), list(out_shape), list(scratch_shapes)
    in_specs, out_specs = list(in_specs), list(out_specs)
    if rider is None:
        res = _pallas(
            body, name=name, out_shape=out_shape, grid=grid, in_specs=in_specs, out_specs=out_specs,
            scratch_shapes=scratch_shapes, input_output_aliases=aliases or {},
            compiler_params=_params(semantics, vmem))(*operands)
        return list(res), []
    n_in, n_out, n_scr = len(operands), len(out_shape), len(scratch_shapes)
    ri, ro = len(rider.operands), len(rider.out_shapes)

    def carried(*refs):
        a, b = n_in, n_in + ri
        c, d = b + n_out, b + n_out + ro
        e = d + n_scr
        ids = [pl.program_id(k) for k in range(len(grid))]
        first = ids[0] == 0
        last = ids[0] == grid[0] - 1
        for k in range(1, len(grid)):
            first = jnp.logical_and(first, ids[k] == 0)
            last = jnp.logical_and(last, ids[k] == grid[k] - 1)

        @pl.when(first)
        def _():
            rider.start(refs[a:b], refs[c:d], refs[e:])

        body(*refs[:a], *refs[b:c], *refs[d:e])

        @pl.when(last)
        def _():
            rider.finish(refs[a:b], refs[c:d], refs[e:])

    res = _pallas(
        carried, name=name, out_shape=out_shape + list(rider.out_shapes), grid=grid,
        in_specs=in_specs + [_any_spec()] * ri, out_specs=out_specs + [_any_spec()] * ro,
        scratch_shapes=scratch_shapes + list(rider.sems), input_output_aliases=aliases or {},
        compiler_params=_params(("arbitrary",) * len(grid), vmem))(*operands, *rider.operands)
    return list(res[:n_out]), list(res[n_out:])


def _exchange(rider, name):
    ri, ro = len(rider.operands), len(rider.out_shapes)

    def body(*refs):
        rider.start(refs[:ri], refs[ri:ri + ro], refs[ri + ro:])
        rider.finish(refs[:ri], refs[ri:ri + ro], refs[ri + ro:])

    return _pallas(
        body, name=name, out_shape=list(rider.out_shapes), in_specs=[_any_spec()] * ri,
        out_specs=[_any_spec()] * ro, scratch_shapes=list(rider.sems))(*rider.operands)


class Cols(NamedTuple):
    arr: jax.Array
    off: int
    width: int


def _cols(a):
    return a if isinstance(a, Cols) else Cols(a, 0, a.shape[1])


def _matmul(a, b, *, mode, name, out_dtype=F32, tm=1024, tn=1024, tk=2048, bias=None, add=None, out_cols=None,
            into=None, rider=None):
    a, b = _cols(a), _cols(b)
    if mode == "nn":
        (m, k), (k2, n) = (a.arr.shape[0], a.width), (b.arr.shape[0], b.width)
    elif mode == "nt":
        (m, k), (n, k2) = (a.arr.shape[0], a.width), (b.arr.shape[0], b.width)
    else:
        (k, m), (k2, n) = (a.arr.shape[0], a.width), (b.arr.shape[0], b.width)
    assert k == k2, (a.arr.shape, b.arr.shape, mode)
    tm, tn, tk = _tile(m, tm), _tile(n, tn), _tile(k, tk)
    nk = k // tk
    dims = {"nn": NN, "nt": NT, "tn": TN}[mode]
    if mode == "tn":
        assert a.off % tm == 0
        a_spec = pl.BlockSpec((tk, tm), lambda i, j, kk, o=a.off // tm: (kk, i + o))
    else:
        assert a.off % tk == 0
        a_spec = pl.BlockSpec((tm, tk), lambda i, j, kk, o=a.off // tk: (i, kk + o))
    if mode == "nt":
        assert b.off % tk == 0
        b_spec = pl.BlockSpec((tn, tk), lambda i, j, kk, o=b.off // tk: (j, kk + o))
    else:
        assert b.off % tn == 0
        b_spec = pl.BlockSpec((tk, tn), lambda i, j, kk, o=b.off // tn: (kk, j + o))
    in_specs, operands = [a_spec, b_spec], [a.arr, b.arr]
    assert bias is None or add is None
    if bias is not None:
        in_specs.append(pl.BlockSpec((1, tn), lambda i, j, kk: (0, j)))
        operands.append(bias)
    if add is not None:
        assert add.shape == (m, n)
        in_specs.append(pl.BlockSpec((tm, tn), lambda i, j, kk: (i, j)))
        operands.append(add)
    total_w, o_off = out_cols if out_cols is not None else (n, 0)
    assert o_off % tn == 0
    aliases = {}
    if into is not None:
        assert into.shape == (m, total_w) and into.dtype == out_dtype
        in_specs.append(_any_spec())
        operands.append(into)
        aliases = {len(operands) - 1: 0}
    n_in = len(operands)

    def body(*refs):
        a_ref, b_ref = refs[0], refs[1]
        bias_ref = refs[2] if bias is not None or add is not None else None
        o_ref = refs[n_in]
        acc_ref = refs[-1] if nk > 1 else None
        part = _dot(a_ref[...].astype(BF16), b_ref[...].astype(BF16), dims)

        def finish(acc):
            if bias_ref is not None:
                acc = acc + bias_ref[...]
            o_ref[...] = acc.astype(out_dtype)

        if nk == 1:
            finish(part)
        else:
            kk = pl.program_id(2)

            @pl.when(kk == 0)
            def _():
                acc_ref[...] = part

            @pl.when(kk > 0)
            def _():
                acc_ref[...] += part

            @pl.when(kk == nk - 1)
            def _():
                finish(acc_ref[...])

    vmem = 2 * (_nbytes((tm, tk), a.arr.dtype) + _nbytes((tk, tn), b.arr.dtype) + _nbytes((tm, tn), out_dtype))
    vmem += 3 * _nbytes((tm, tn), F32)
    (out,), landed = _call(
        body, operands, name=name, out_shape=[jax.ShapeDtypeStruct((m, total_w), out_dtype)],
        grid=(m // tm, n // tn, nk), in_specs=in_specs,
        out_specs=[pl.BlockSpec((tm, tn), lambda i, j, kk, o=o_off // tn: (i, j + o))],
        scratch_shapes=[pltpu.VMEM((tm, tn), F32)] if nk > 1 else [], aliases=aliases,
        semantics=("parallel", "parallel", "arbitrary"), vmem=vmem, rider=rider)
    return out if rider is None else (out, landed)


def _ew(fn, *, name, rows, width, tiles, vecs=(), outs, accs=0, tl=1024, cw=512, into=None, with_col=False):
    tl, cw = _tile(rows, tl, SUBLANES), _tile(width, cw)
    ncol = width // cw
    nt_, nv = len(tiles), len(vecs)
    into = list(into) if into is not None else [None] * len(outs)
    aliased = [t for t in into if t is not None]

    def off(o):
        assert o % cw == 0, (name, o, cw)
        return o // cw

    in_specs, vmem = [], 0
    for t in tiles:
        arr, o = t[0], off(t[1])
        wrap = t[2] // cw if len(t) > 2 else ncol
        in_specs.append(pl.BlockSpec((tl, cw), lambda j, i, o=o, wrap=wrap: (i, o + j % wrap)))
        vmem += _nbytes((tl, cw), arr.dtype)
    in_specs += [pl.BlockSpec((1, cw), lambda j, i, o=off(o): (0, j + o)) for _, o in vecs]
    in_specs += [_any_spec() for _ in aliased]
    out_shape, out_specs, aliases = [], [], {}
    n_in = nt_ + nv
    for idx, ((dt, tw, o), tgt) in enumerate(zip(outs, into)):
        out_shape.append(jax.ShapeDtypeStruct((rows, tw), dt))
        out_specs.append(pl.BlockSpec((tl, cw), lambda j, i, o=off(o): (i, j + o)))
        vmem += _nbytes((tl, cw), dt)
        if tgt is not None:
            assert tgt.shape == (rows, tw) and tgt.dtype == dt, (name, tgt.shape, tgt.dtype)
            aliases[n_in + len(aliases)] = idx
    for _ in range(accs):
        out_shape.append(jax.ShapeDtypeStruct((1, width), F32))
        out_specs.append(pl.BlockSpec((1, cw), lambda j, i: (0, j)))
    n_out = len(outs)

    def body(*refs):
        vals = [r[...].astype(F32) for r in refs[:n_in]]
        out_refs = refs[n_in + len(aliased):]
        res = fn(pl.program_id(0), *vals) if with_col else fn(*vals)
        res = res if isinstance(res, (tuple, list)) else (res,)
        assert len(res) == n_out + accs, (name, len(res))
        for r, v in zip(out_refs[:n_out], res[:n_out]):
            r[...] = v.astype(r.dtype)
        first = pl.program_id(1) == 0
        for r, v in zip(out_refs[n_out:], res[n_out:]):
            s = jnp.sum(v, axis=0, keepdims=True)

            @pl.when(first)
            def _(r=r, s=s):
                r[...] = s

            @pl.when(jnp.logical_not(first))
            def _(r=r, s=s):
                r[...] += s

    return _pallas(
        body, name=name, out_shape=out_shape, grid=(ncol, rows // tl),
        in_specs=in_specs, out_specs=out_specs, input_output_aliases=aliases,
        compiler_params=_params(("parallel", "arbitrary"), 3 * vmem),
    )(*[t[0] for t in tiles], *[v for v, _ in vecs], *aliased)


def _rmsnorm_fwd(x, w_row, name):
    rows, d = x.shape
    tl = _tile(rows, 512, SUBLANES)

    def body(x_ref, w_ref, h_ref):
        xv = x_ref[...]
        rstd = lax.rsqrt(jnp.mean(xv * xv, axis=-1, keepdims=True) + NORM_EPS)
        h_ref[...] = (xv * rstd * w_ref[...]).astype(BF16)

    return _pallas(
        body, name=name, out_shape=jax.ShapeDtypeStruct((rows, d), BF16), grid=(rows // tl,),
        in_specs=[pl.BlockSpec((tl, d), lambda i: (i, 0)), pl.BlockSpec((1, d), lambda i: (0, 0))],
        out_specs=pl.BlockSpec((tl, d), lambda i: (i, 0)),
        compiler_params=_params(("parallel",)),
    )(x, w_row)


def _rmsnorm_bwd(x, w_row, dh, dout, name, rider=None):
    rows, d = x.shape
    tl = _tile(rows, 256, SUBLANES)

    def body(x_ref, w_ref, dh_ref, dout_ref, gx_ref, gw_ref):
        xv = x_ref[...]
        rstd = lax.rsqrt(jnp.mean(xv * xv, axis=-1, keepdims=True) + NORM_EPS)
        xn = xv * rstd
        dhv = dh_ref[...]
        dxn = dhv * w_ref[...]
        dx = rstd * (dxn - xn * jnp.mean(dxn * xn, axis=-1, keepdims=True))
        gx_ref[...] = dout_ref[...] + dx
        gw = jnp.sum(dhv * xn, axis=0, keepdims=True)

        @pl.when(pl.program_id(0) == 0)
        def _():
            gw_ref[...] = gw

        @pl.when(pl.program_id(0) > 0)
        def _():
            gw_ref[...] += gw

    tile = pl.BlockSpec((tl, d), lambda i: (i, 0))
    row = pl.BlockSpec((1, d), lambda i: (0, 0))
    res, landed = _call(
        body, [x, w_row, dh, dout], name=name,
        out_shape=[jax.ShapeDtypeStruct((rows, d), F32), jax.ShapeDtypeStruct((1, d), F32)],
        grid=(rows // tl,), in_specs=[tile, row, tile, tile], out_specs=[tile, row],
        semantics=("arbitrary",), rider=rider)
    return res if rider is None else (res, landed)


def _head_mean(x, gmat):
    hi = x.astype(BF16)
    lo = (x - hi.astype(F32)).astype(BF16)
    out = []
    for s in range(x.shape[1] // MXU_DIM):
        sl = slice(s * MXU_DIM, (s + 1) * MXU_DIM)
        out.append(_dot(hi[:, sl], gmat, NN) + _dot(lo[:, sl], gmat, NN))
    return out[0] if len(out) == 1 else jnp.concatenate(out, axis=1)


def _head_mean_matrix():
    blk = jnp.arange(MXU_DIM) // HEAD_DIM
    return jnp.where(blk[:, None] == blk[None, :], 1.0 / HEAD_DIM, 0.0).astype(BF16)


def _spread_head(x, g, width):
    col = x[:, (g // 2) * LANES:(g // 2 + 1) * LANES]
    other = pltpu.roll(col, HEAD_DIM, axis=1)
    low = lax.broadcasted_iota(jnp.int32, col.shape, 1) < HEAD_DIM
    both = jnp.where(low, col, other) if g % 2 == 0 else jnp.where(low, other, col)
    return both if width == LANES else jnp.concatenate([both] * (width // LANES), axis=1)


def _head_diagonal(t, per_kv):
    head = lax.broadcasted_iota(jnp.int32, t.shape, 1) // HEAD_DIM
    zero = jnp.zeros_like(t)
    return jnp.concatenate([jnp.where(head == r, t, zero) for r in range(per_kv)], axis=0)


def _fold_heads(x, per_kv):
    rows = x.shape[0] // per_kv
    head = lax.broadcasted_iota(jnp.int32, (rows, x.shape[1]), 1) // HEAD_DIM
    acc = jnp.where(head == 0, x[0:rows], 0.0)
    for r in range(1, per_kv):
        acc = acc + jnp.where(head == r, x[r * rows:(r + 1) * rows], 0.0)
    while acc.shape[1] > LANES:
        half = acc.shape[1] // 2
        acc = acc[:, :half] + acc[:, half:]
    return acc + pltpu.roll(acc, HEAD_DIM, axis=1)


def _join_heads(parts):
    low = lax.broadcasted_iota(jnp.int32, parts[0].shape, 1) < HEAD_DIM
    cols = [jnp.where(low, parts[2 * j], parts[2 * j + 1]) for j in range(len(parts) // 2)]
    return cols[0] if len(cols) == 1 else jnp.concatenate(cols, axis=1)


def _attn_specs(attn_w, kv_w):
    half = attn_w // 2
    kcol, vcol = attn_w // kv_w, attn_w // kv_w + 1
    gcol = (attn_w + 2 * kv_w) // half
    prev = lambda i: jnp.maximum(i - 1, 0)
    return [
        pl.BlockSpec((BLOCK, attn_w), lambda i: (i, 0)),
        pl.BlockSpec((BLOCK, kv_w), lambda i: (prev(i), kcol)),
        pl.BlockSpec((BLOCK, kv_w), lambda i: (i, kcol)),
        pl.BlockSpec((BLOCK, kv_w), lambda i: (prev(i), vcol)),
        pl.BlockSpec((BLOCK, kv_w), lambda i: (i, vcol)),
        pl.BlockSpec((BLOCK, half), lambda i: (i, gcol)),
        pl.BlockSpec((BLOCK, half), lambda i: (i, gcol + 1)),
    ]


def _band_mask(i):
    q_loc = lax.broadcasted_iota(jnp.int32, (BLOCK, 2 * BLOCK), 0) + BLOCK
    k_loc = lax.broadcasted_iota(jnp.int32, (BLOCK, 2 * BLOCK), 1)
    diff = q_loc - k_loc
    first_key = jnp.where(i == 0, BLOCK, 0)
    return (diff >= 0) & (diff < BLOCK) & (k_loc >= first_key)


def _softmax_with_sink(s, sink):
    m = jnp.maximum(jnp.max(s, axis=-1, keepdims=True), sink)
    p = jnp.exp(s - m)
    e_sink = jnp.exp(sink - m)
    den = jnp.sum(p, axis=-1, keepdims=True) + e_sink
    inv = 1.0 / den
    return p * inv, e_sink * inv


def _attn_block(i, q, kk, vv, qw, kw, gmat, sink_ref, per_kv):
    scale = 1.0 / math.sqrt(HEAD_DIM)
    keys = 2 * BLOCK
    valid = _band_mask(i)
    q_rstd = lax.rsqrt(_head_mean(q * q, gmat) + NORM_EPS)
    qn = q * q_rstd
    qh = (qn * qw).astype(BF16)
    k_rstd = lax.rsqrt(_head_mean(kk * kk, gmat) + NORM_EPS)
    kn = kk * k_rstd
    kh = kn * kw
    gw = per_kv * HEAD_DIM
    groups = []
    for g in range(N_KV_HEADS):
        kd = _head_diagonal(_spread_head(kh, g, gw).astype(BF16), per_kv)
        vd = _head_diagonal(_spread_head(vv, g, gw).astype(BF16), per_kv)
        qg = qh[:, g * gw:(g + 1) * gw]
        s_all = _dot(qg, kd, NT) * scale
        ps, p_sinks = [], []
        for r in range(per_kv):
            s = jnp.where(valid, s_all[:, r * keys:(r + 1) * keys], -1e30)
            p, p_sink = _softmax_with_sink(s, sink_ref[g * per_kv + r])
            ps.append(p)
            p_sinks.append(p_sink)
        pb = jnp.concatenate(ps, axis=1).astype(BF16)
        groups.append((kd, vd, qg, ps, p_sinks, pb, _dot(pb, vd, NN)))
    return qn, q_rstd, kn, k_rstd, groups


def _attention_fwd(proj, qw_row, kw_row, gmat, sinks, *, attn_w, kv_w, name):
    rows = proj.shape[0]
    per_kv = attn_w // HEAD_DIM // N_KV_HEADS

    def body(q_ref, kp_ref, kc_ref, vp_ref, vc_ref, glo_ref, ghi_ref, qw_ref, kw_ref, gm_ref, sink_ref, o_ref):
        kk = jnp.concatenate([kp_ref[...], kc_ref[...]], axis=0).astype(F32)
        vv = jnp.concatenate([vp_ref[...], vc_ref[...]], axis=0).astype(F32)
        gate = jnp.concatenate([glo_ref[...], ghi_ref[...]], axis=1).astype(F32)
        *_, groups = _attn_block(pl.program_id(0), q_ref[...].astype(F32), kk, vv, qw_ref[...], kw_ref[...], gm_ref[...],
                                 sink_ref, per_kv)
        attn = jnp.concatenate([grp[-1] for grp in groups], axis=1)
        o_ref[...] = (attn * _silu(gate)).astype(BF16)

    const = lambda a: pl.BlockSpec(a.shape, lambda i: (0, 0))
    return _pallas(
        body, name=name, out_shape=jax.ShapeDtypeStruct((rows, attn_w), BF16), grid=(rows // BLOCK,),
        in_specs=_attn_specs(attn_w, kv_w) + [const(qw_row), const(kw_row), const(gmat),
                                              pl.BlockSpec(memory_space=pltpu.SMEM)],
        out_specs=pl.BlockSpec((BLOCK, attn_w), lambda i: (i, 0)),
        compiler_params=_params(("parallel",), 40 * 1024 * 1024),
    )(proj, proj, proj, proj, proj, proj, proj, qw_row, kw_row, gmat, sinks)


def _attention_bwd(proj, d_ag, dproj, qw_row, kw_row, gmat, sinks, *, attn_w, kv_w, name, rider=None):
    rows = proj.shape[0]
    nb = rows // BLOCK
    per_kv = attn_w // HEAD_DIM // N_KV_HEADS
    gw = per_kv * HEAD_DIM
    keys = 2 * BLOCK
    scale = 1.0 / math.sqrt(HEAD_DIM)
    w_out = 2 * attn_w + 2 * kv_w

    def body(q_ref, kp_ref, kc_ref, vp_ref, vc_ref, glo_ref, ghi_ref, dag_ref, qw_ref, kw_ref, gm_ref, sink_ref, _,
             dp_ref, dkv_ref, gqw_ref, gkw_ref, gs_ref):
        i = pl.program_id(0)
        kk = jnp.concatenate([kp_ref[...], kc_ref[...]], axis=0).astype(F32)
        vv = jnp.concatenate([vp_ref[...], vc_ref[...]], axis=0).astype(F32)
        gate = jnp.concatenate([glo_ref[...], ghi_ref[...]], axis=1).astype(F32)
        d_ag_v = dag_ref[...].astype(F32)
        qw, kw, gmat_v = qw_ref[...], kw_ref[...], gm_ref[...]
        qn, q_rstd, kn, k_rstd, groups = _attn_block(i, q_ref[...].astype(F32), kk, vv, qw, kw, gmat_v, sink_ref,
                                                     per_kv)
        lane = lax.broadcasted_iota(jnp.int32, (SUBLANES, LANES), 1)
        sub = lax.broadcasted_iota(jnp.int32, (SUBLANES, LANES), 0)
        gsink = jnp.zeros((SUBLANES, LANES), F32)
        dq_groups, dgate_groups, dk_heads, dv_heads = [], [], [], []
        for g, (kd, vd, qg, ps, p_sinks, pb, o) in enumerate(groups):
            cs = slice(g * gw, (g + 1) * gw)
            gate_g, d_ag_g = gate[:, cs], d_ag_v[:, cs]
            dgate_groups.append(d_ag_g * o * _dsilu(gate_g))
            do = (d_ag_g * _silu(gate_g)).astype(BF16)
            dp_all = _dot(do, vd, NT)
            dss = []
            for r in range(per_kv):
                p, dp = ps[r], dp_all[:, r * keys:(r + 1) * keys]
                delta = jnp.sum(p * dp, axis=-1, keepdims=True)
                dss.append(p * (dp - delta) * scale)
                gs_h = jnp.sum(-p_sinks[r] * delta, axis=0, keepdims=True)
                gsink = gsink + jnp.where((lane == g * per_kv + r) & (sub == 0), gs_h, 0.0)
            ds = jnp.concatenate(dss, axis=1).astype(BF16)
            dq_groups.append(_dot(ds, kd, NN))
            dk_heads.append(_fold_heads(_dot(ds, qg, TN), per_kv))
            dv_heads.append(_fold_heads(_dot(pb, do, TN), per_kv))
        dqh = jnp.concatenate(dq_groups, axis=1)
        gqw = jnp.sum(dqh * qn, axis=0, keepdims=True)
        dqn = dqh * qw
        dq = q_rstd * (dqn - qn * _head_mean(dqn * qn, gmat_v))
        dkh = _join_heads(dk_heads)
        gkw = jnp.sum(dkh * kn, axis=0, keepdims=True)
        dkn = dkh * kw
        dk = k_rstd * (dkn - kn * _head_mean(dkn * kn, gmat_v))
        dp_ref[:, 0:attn_w] = dq.astype(BF16)
        dp_ref[:, attn_w:attn_w + 2 * kv_w] = jnp.zeros((BLOCK, 2 * kv_w), BF16)
        dp_ref[:, attn_w + 2 * kv_w:w_out] = jnp.concatenate(dgate_groups, axis=1).astype(BF16)
        dkv_ref[0] = jnp.concatenate([dk, _join_heads(dv_heads)], axis=1)

        @pl.when(i == 0)
        def _():
            gqw_ref[...] = gqw
            gkw_ref[...] = gkw
            gs_ref[...] = gsink

        @pl.when(i > 0)
        def _():
            gqw_ref[...] += gqw
            gkw_ref[...] += gkw
            gs_ref[...] += gsink

    const = lambda a: pl.BlockSpec(a.shape, lambda i: (0, 0))
    res, landed = _call(
        body, [proj, proj, proj, proj, proj, proj, proj, d_ag, qw_row, kw_row, gmat, sinks, dproj], name=name,
        out_shape=[jax.ShapeDtypeStruct(dproj.shape, BF16),
                   jax.ShapeDtypeStruct((nb, 2 * BLOCK, 2 * kv_w), F32),
                   jax.ShapeDtypeStruct(qw_row.shape, F32), jax.ShapeDtypeStruct(kw_row.shape, F32),
                   jax.ShapeDtypeStruct((SUBLANES, LANES), F32)],
        grid=(nb,),
        in_specs=_attn_specs(attn_w, kv_w) + [pl.BlockSpec((BLOCK, attn_w), lambda i: (i, 0)), const(qw_row),
                                              const(kw_row), const(gmat), pl.BlockSpec(memory_space=pltpu.SMEM),
                                              _any_spec()],
        out_specs=[pl.BlockSpec((BLOCK, w_out), lambda i: (i, 0)),
                   pl.BlockSpec((1, 2 * BLOCK, 2 * kv_w), lambda i: (i, 0, 0)),
                   const(qw_row), const(kw_row), pl.BlockSpec((SUBLANES, LANES), lambda i: (0, 0))],
        aliases={12: 0}, semantics=("arbitrary",), vmem=48 * 1024 * 1024, rider=rider)
    return res if rider is None else (res, landed)


def _attention_dkv(dproj, dkv, *, attn_w, kv_w, name):
    rows = dproj.shape[0]
    nb = rows // BLOCK
    col = attn_w // (2 * kv_w)

    def body(cur_ref, nxt_ref, _, o_ref):
        i = pl.program_id(0)
        nxt = jnp.where(i < nb - 1, nxt_ref[0, 0:BLOCK, :], 0.0)
        o_ref[...] = (cur_ref[0, BLOCK:2 * BLOCK, :] + nxt).astype(BF16)

    blk = lambda f: pl.BlockSpec((1, 2 * BLOCK, 2 * kv_w), f)
    return _pallas(
        body, name=name, out_shape=jax.ShapeDtypeStruct(dproj.shape, BF16), grid=(nb,),
        in_specs=[blk(lambda i: (i, 0, 0)), blk(lambda i: (jnp.minimum(i + 1, nb - 1), 0, 0)), _any_spec()],
        out_specs=pl.BlockSpec((BLOCK, 2 * kv_w), lambda i: (i, col)),
        input_output_aliases={2: 0},
        compiler_params=_params(("parallel",)),
    )(dkv, dkv, dproj)


def _cmul(ar, ai, br, bi):
    return ar * br - ai * bi, ar * bi + ai * br


def _ssm_prep(a_re, a_im, log_dt_col, steps, name):
    def body(are_ref, aim_ref, ldt_ref, abr_ref, abi_ref, cfr_ref, cfi_ref, apr_ref, api_ref, pwr_ref, pwi_ref):
        are, aim = are_ref[...], aim_ref[...]
        dt = jnp.exp(ldt_ref[...])
        mag = jnp.exp(dt * are)
        abr = mag * jnp.cos(dt * aim)
        abi = mag * jnp.sin(dt * aim)
        num_re, num_im = abr - 1.0, abi
        den = are * are + aim * aim
        abr_ref[...] = abr
        abi_ref[...] = abi
        cfr_ref[...] = (num_re * are + num_im * aim) / den
        cfi_ref[...] = (num_im * are - num_re * aim) / den
        pr, pi = jnp.ones_like(abr), jnp.zeros_like(abr)
        for k in range(steps):
            pwr_ref[k] = pr
            pwi_ref[k] = pi
            pr, pi = _cmul(pr, pi, abr, abi)
        apr_ref[...] = pr
        api_ref[...] = pi

    shp = jax.ShapeDtypeStruct(a_re.shape, F32)
    pows = jax.ShapeDtypeStruct((steps,) + a_re.shape, F32)
    return _pallas(body, name=name, out_shape=[shp] * 6 + [pows] * 2)(a_re, a_im, log_dt_col)


def _ssm_param_bwd(a_re, a_im, log_dt_col, d_ab_re, d_ab_im, b_re, b_im, dbt_re, dbt_im, name):
    def body(are_ref, aim_ref, ldt_ref, gabr_ref, gabi_ref, br_ref, bi_ref, tr_ref, ti_ref,
             dar_ref, dai_ref, dldt_ref, dbr_ref, dbi_ref):
        are, aim = are_ref[...], aim_ref[...]
        dt = jnp.exp(ldt_ref[...])
        mag = jnp.exp(dt * are)
        abr = mag * jnp.cos(dt * aim)
        abi = mag * jnp.sin(dt * aim)
        den = are * are + aim * aim
        cfr = ((abr - 1.0) * are + abi * aim) / den
        cfi = (abi * are - (abr - 1.0) * aim) / den
        gabr, gabi = jnp.sum(gabr_ref[...], axis=0), jnp.sum(gabi_ref[...], axis=0)
        t_re, t_im = tr_ref[...], ti_ref[...]
        g_r, g_i = _cmul(br_ref[...], -bi_ref[...], t_re, t_im)
        gcfr, gcfi = jnp.sum(g_r, axis=1), jnp.sum(g_i, axis=1)
        dbr, dbi = _cmul(cfr[:, None, :], -cfi[:, None, :], t_re, t_im)
        dbr_ref[...] = dbr
        dbi_ref[...] = dbi
        inv_r, inv_i = are / den, -aim / den
        t_r, t_i = _cmul(inv_r, -inv_i, gcfr, gcfi)
        gabr, gabi = gabr + t_r, gabi + t_i
        q_r, q_i = _cmul(cfr, cfi, inv_r, inv_i)
        da_r, da_i = _cmul(-q_r, q_i, gcfr, gcfi)
        gz_r, gz_i = _cmul(abr, -abi, gabr, gabi)
        dar_ref[...] = da_r + dt * gz_r
        dai_ref[...] = da_i + dt * gz_i
        dldt_ref[...] = dt * jnp.sum(are * gz_r + aim * gz_i, axis=-1, keepdims=True)

    shp = jax.ShapeDtypeStruct(a_re.shape, F32)
    bshp = jax.ShapeDtypeStruct(b_re.shape, F32)
    return _pallas(body, name=name,
                   out_shape=[shp, shp, jax.ShapeDtypeStruct(log_dt_col.shape, F32), bshp, bshp])(
        a_re, a_im, log_dt_col, d_ab_re, d_ab_im, b_re, b_im, dbt_re, dbt_im)


SCAN_LANES = 512
SSM_CHUNK = 256
W_IN_PARTS = 2


def _scan_segments(xr_ref, xi_ref, a_re, a_im, ap_re, ap_im, pw_re, pw_im, carry_re, carry_im, cm_re, cm_im, steps,
                   reverse, base):
    n = xr_ref.shape[1]
    seg_order = range(SUBLANES - 1, -1, -1) if reverse else range(SUBLANES)
    sign = -1.0 if reverse else 1.0
    for c0 in range(0, n, SCAN_LANES):
        ls = slice(c0, c0 + SCAN_LANES)
        gs = slice(base + c0, base + c0 + SCAN_LANES)
        ar = jnp.broadcast_to(a_re[:, gs], (SUBLANES, SCAN_LANES))
        ai = jnp.broadcast_to(a_im[:, gs], (SUBLANES, SCAN_LANES))
        end_r = jnp.zeros((SUBLANES, SCAN_LANES), F32)
        end_i = jnp.zeros((SUBLANES, SCAN_LANES), F32)
        for j in range(steps):
            k = j if reverse else steps - 1 - j
            rws = slice(j * SUBLANES, (j + 1) * SUBLANES)
            tr, ti = _cmul(pw_re[k:k + 1, gs], sign * pw_im[k:k + 1, gs], xr_ref[rws, ls], xi_ref[rws, ls])
            end_r, end_i = end_r + tr, end_i + ti
        cr, ci = carry_re[:, gs], carry_im[:, gs]
        apr, api = ap_re[:, gs], ap_im[:, gs]
        for r in seg_order:
            cm_re[r:r + 1, gs] = cr
            cm_im[r:r + 1, gs] = ci
            tr, ti = _cmul(apr, api, cr, ci)
            cr, ci = end_r[r:r + 1, :] + tr, end_i[r:r + 1, :] + ti
        carry_re[:, gs] = cr
        carry_im[:, gs] = ci

        def run(t, s, ar=ar, ai=ai, ls=ls):
            j = steps - 1 - t if reverse else t
            r0 = pl.multiple_of(j * SUBLANES, SUBLANES)
            sr, si = _cmul(ar, ai, s[0], s[1])
            sr = sr + xr_ref[pl.ds(r0, SUBLANES), ls]
            si = si + xi_ref[pl.ds(r0, SUBLANES), ls]
            xr_ref[pl.ds(r0, SUBLANES), ls] = sr
            xi_ref[pl.ds(r0, SUBLANES), ls] = si
            return sr, si

        lax.fori_loop(0, steps, run, (cm_re[:, gs], cm_im[:, gs]))


SB_GROUPS = MXU_DIM // GROUP
SB_STATE = SB_GROUPS * STATE


def _ssm_rows(m):
    flat = m.reshape(-1, STATE).astype(F32)
    return jnp.concatenate([flat, flat], axis=1)


def _from_ssm_rows(rows):
    return rows[:, :STATE].reshape(-1, GROUP, STATE)


def _own_group(shape):
    row_g = lax.broadcasted_iota(jnp.int32, shape, 0) // GROUP
    col_g = lax.broadcasted_iota(jnp.int32, shape, 1) // STATE
    return row_g == col_g


def _block_diagonal(rows):
    tiled = jnp.concatenate([rows] * (SB_STATE // LANES), axis=1)
    return jnp.where(_own_group(tiled.shape), tiled, 0.0).astype(BF16)


def _block_rows(acc):
    x = jnp.where(_own_group(acc.shape), acc, 0.0)
    while x.shape[1] > LANES:
        half = x.shape[1] // 2
        x = x[:, :half] + x[:, half:]
    return x + pltpu.roll(x, STATE, axis=1)


def _rows_to_segments(dst, srcs, steps, stage):
    for ref, off in srcs:
        for k in range(ref.shape[1] // LANES):
            stage[off // LANES + k] = ref[:, k * LANES:(k + 1) * LANES].astype(F32)
    for k in range(dst.shape[1] // LANES):
        for j in range(steps):
            dst[j * SUBLANES:(j + 1) * SUBLANES, k * LANES:(k + 1) * LANES] = (
                stage[k, pl.ds(j, SUBLANES, stride=steps), :])


def _segments_to_rows(dst, src, steps, stage):
    for k in range(src.shape[1] // LANES):
        for j in range(steps):
            stage[k, pl.ds(j, SUBLANES, stride=steps), :] = (
                src[j * SUBLANES:(j + 1) * SUBLANES, k * LANES:(k + 1) * LANES])
    for k in range(src.shape[1] // LANES):
        dst[:, k * LANES:(k + 1) * LANES] = stage[k]


def _u_specs(w, o_u, chunk, index):
    half = w // 2
    assert o_u % half == 0
    return [pl.BlockSpec((chunk, half), lambda c, k=k: (index(c), o_u // half + k)) for k in range(2)]


def _ssm_fwd(proj, o_u, bc_rows, rows_p, d_row, *, chunk, name, rider=None):
    rows = proj.shape[0]
    w = d_row.shape[1]
    nc = rows // chunk
    steps = chunk // SUBLANES
    nsb = w // MXU_DIM
    n_state = nsb * SB_STATE

    def body(ulo_ref, uhi_ref, b2r_ref, b2i_ref, c2r_ref, c2i_ref, abr_ref, abi_ref, cfr_ref, cfi_ref, apr_ref,
             api_ref, pwr_ref, pwi_ref, d_ref, y_ref, str_ref, sti_ref, yg_ref, bre_ref, bim_ref, cre_ref, cim_ref,
             useg, yseg, stage, sr, si,
             carry_r, carry_i, cm_r, cm_i):
        @pl.when(pl.program_id(0) == 0)
        def _():
            for src, dst in ((b2r_ref, bre_ref), (b2i_ref, bim_ref), (c2r_ref, cre_ref), (c2i_ref, cim_ref)):
                for sb in range(nsb):
                    dst[sb] = _block_diagonal(src[sb * MXU_DIM:(sb + 1) * MXU_DIM, :])
            carry_r[...] = jnp.zeros_like(carry_r)
            carry_i[...] = jnp.zeros_like(carry_i)

        str_ref[0] = carry_r[...]
        sti_ref[0] = carry_i[...]
        _rows_to_segments(useg, [(ulo_ref, 0), (uhi_ref, w // 2)], steps, stage)
        for sb in range(nsb):
            us = slice(sb * MXU_DIM, (sb + 1) * MXU_DIM)
            ss = slice(sb * SB_STATE, (sb + 1) * SB_STATE)
            ub = useg[:, us].astype(BF16)
            bur = _dot(ub, bre_ref[sb], NN)
            bui = _dot(ub, bim_ref[sb], NN)
            xr, xi = _cmul(cfr_ref[:, ss], cfi_ref[:, ss], bur, bui)
            sr[...] = xr
            si[...] = xi
            _scan_segments(sr, si, abr_ref[...], abi_ref[...], apr_ref[...], api_ref[...], pwr_ref, pwi_ref,
                           carry_r, carry_i, cm_r, cm_i, steps, False, sb * SB_STATE)
            y = _dot(sr[...].astype(BF16), cre_ref[sb], NT) - _dot(si[...].astype(BF16), cim_ref[sb], NT)
            yseg[:, us] = y + d_ref[:, us] * useg[:, us]
        _segments_to_rows(y_ref, yseg, steps, stage)
        yg_ref[...] = _gelu(y_ref[...]).astype(BF16)

    const = lambda a: pl.BlockSpec(a.shape, lambda c: (0,) * a.ndim)
    row_n = pl.BlockSpec((1, n_state), lambda c: (0, 0))
    st = pl.BlockSpec((1, 1, n_state), lambda c: (c, 0, 0))
    held = [pltpu.VMEM((nsb, MXU_DIM, SB_STATE), BF16)] * 4
    vmem = (4 * _nbytes((nsb, MXU_DIM, SB_STATE), BF16) + 4 * _nbytes((chunk, SB_STATE), F32)
            + 12 * _nbytes((chunk, w), F32) + 8 * _nbytes(bc_rows[0].shape, F32))
    res, landed = _call(
        body, [proj, proj, *bc_rows, *rows_p, d_row], name=name,
        out_shape=[jax.ShapeDtypeStruct((rows, w), F32), jax.ShapeDtypeStruct((nc, 1, n_state), F32),
                   jax.ShapeDtypeStruct((nc, 1, n_state), F32), jax.ShapeDtypeStruct((rows, w), BF16)],
        grid=(nc,),
        in_specs=_u_specs(w, o_u, chunk, lambda c: c) + [const(b) for b in bc_rows]
        + [row_n] * 6 + [pl.BlockSpec((steps, n_state), lambda c: (0, 0))] * 2 + [pl.BlockSpec((1, w), lambda c: (0, 0))],
        out_specs=[pl.BlockSpec((chunk, w), lambda c: (c, 0)), st, st, pl.BlockSpec((chunk, w), lambda c: (c, 0))],
        scratch_shapes=held + [pltpu.VMEM((chunk, w), F32), pltpu.VMEM((chunk, w), F32),
                               pltpu.VMEM((w // LANES, chunk, LANES), F32),
                               pltpu.VMEM((chunk, SB_STATE), F32), pltpu.VMEM((chunk, SB_STATE), F32),
                               pltpu.VMEM((1, n_state), F32), pltpu.VMEM((1, n_state), F32),
                               pltpu.VMEM((SUBLANES, n_state), F32), pltpu.VMEM((SUBLANES, n_state), F32)],
        semantics=("arbitrary",), vmem=vmem, rider=rider)
    return res if rider is None else (res, landed)


def _ssm_bwd(proj, o_u, y, dyg, st_re, st_im, bc_rows, rows_p, d_row, *, chunk, name, rider=None):
    rows = proj.shape[0]
    w = d_row.shape[1]
    nc = rows // chunk
    steps = chunk // SUBLANES
    nsb = w // MXU_DIM
    n_state = nsb * SB_STATE

    def body(ulo_ref, uhi_ref, y_ref, dyg_ref, str_ref, sti_ref, b2r_ref, b2i_ref, c2r_ref, c2i_ref, t2r_ref,
             t2i_ref, abr_ref, abi_ref, cfr_ref, cfi_ref, apr_ref, api_ref, pwr_ref, pwi_ref, d_ref,
             du_ref, gb2r_ref, gb2i_ref, gc2r_ref, gc2i_ref, gabr_ref, gabi_ref, dd_ref,
             bre_ref, bim_ref, cre_ref, cim_ref, btr_ref, bti_ref, dbre_ref, dbim_ref, dcre_ref, dcim_ref,
             useg, dyseg, dynat, stage, sr, si, lr, li, carry_r, carry_i, lam_r, lam_i, cm_r, cm_i, cl_r, cl_i):
        first = pl.program_id(0) == 0

        @pl.when(first)
        def _():
            for src, dst in ((b2r_ref, bre_ref), (b2i_ref, bim_ref), (c2r_ref, cre_ref), (c2i_ref, cim_ref),
                             (t2r_ref, btr_ref), (t2i_ref, bti_ref)):
                for sb in range(nsb):
                    dst[sb] = _block_diagonal(src[sb * MXU_DIM:(sb + 1) * MXU_DIM, :])
            lam_r[...] = jnp.zeros_like(lam_r)
            lam_i[...] = jnp.zeros_like(lam_i)
            for ref in (dbre_ref, dbim_ref, dcre_ref, dcim_ref, gabr_ref, gabi_ref, dd_ref):
                ref[...] = jnp.zeros_like(ref)

        dynat[...] = dyg_ref[...].astype(F32) * _dgelu(y_ref[...])
        half = w // 2
        dd_ref[:, :half] += jnp.sum(dynat[:, :half] * ulo_ref[...].astype(F32), axis=0, keepdims=True)
        dd_ref[:, half:] += jnp.sum(dynat[:, half:] * uhi_ref[...].astype(F32), axis=0, keepdims=True)
        _rows_to_segments(useg, [(ulo_ref, 0), (uhi_ref, half)], steps, stage)
        _rows_to_segments(dyseg, [(dynat, 0)], steps, stage)
        dy = dyseg[...]
        dyb = dy.astype(BF16)
        ub = useg[...].astype(BF16)
        carry_r[...] = str_ref[0]
        carry_i[...] = sti_ref[0]
        abr, abi = abr_ref[...], abi_ref[...]
        apr, api = apr_ref[...], api_ref[...]
        for sb in range(nsb):
            us = slice(sb * MXU_DIM, (sb + 1) * MXU_DIM)
            ss = slice(sb * SB_STATE, (sb + 1) * SB_STATE)
            base = sb * SB_STATE
            br = _dot(ub[:, us], bre_ref[sb], NN)
            bi = _dot(ub[:, us], bim_ref[sb], NN)
            xr, xi = _cmul(cfr_ref[:, ss], cfi_ref[:, ss], br, bi)
            sr[...] = xr
            si[...] = xi
            lr[...] = _dot(dyb[:, us], cre_ref[sb], NN)
            li[...] = -_dot(dyb[:, us], cim_ref[sb], NN)
            _scan_segments(sr, si, abr, abi, apr, api, pwr_ref, pwi_ref, carry_r, carry_i, cm_r, cm_i, steps, False,
                           base)
            dcre_ref[sb] += _dot(dyb[:, us], sr[...].astype(BF16), TN)
            dcim_ref[sb] -= _dot(dyb[:, us], si[...].astype(BF16), TN)
            _scan_segments(lr, li, abr, -abi, apr, -api, pwr_ref, pwi_ref, lam_r, lam_i, cl_r, cl_i, steps, True,
                           base)
            for c0 in range(0, SB_STATE, SCAN_LANES):
                ls = slice(c0, c0 + SCAN_LANES)
                gs = slice(base + c0, base + c0 + SCAN_LANES)

                def step(j, acc, ls=ls):
                    gar, gai, pr, pi = acc
                    r0 = pl.multiple_of(j * SUBLANES, SUBLANES)
                    rws = pl.ds(r0, SUBLANES)
                    t_r, t_i = _cmul(pr, -pi, lr[rws, ls], li[rws, ls])
                    return gar + t_r, gai + t_i, sr[rws, ls], si[rws, ls]

                zero = jnp.zeros((SUBLANES, SCAN_LANES), F32)
                gar, gai, _, _ = lax.fori_loop(0, steps, step, (zero, zero, cm_r[:, gs], cm_i[:, gs]))
                gabr_ref[:, gs] += gar
                gabi_ref[:, gs] += gai
            xr, xi = lr[...].astype(BF16), li[...].astype(BF16)
            du = _dot(xr, btr_ref[sb], NT) + _dot(xi, bti_ref[sb], NT)
            useg[:, us] = du + d_ref[:, us] * dy[:, us]
            dbre_ref[sb] += _dot(ub[:, us], xr, TN)
            dbim_ref[sb] += _dot(ub[:, us], xi, TN)
        _segments_to_rows(du_ref, useg, steps, stage)

        @pl.when(pl.program_id(0) == nc - 1)
        def _():
            for src, dst in ((dbre_ref, gb2r_ref), (dbim_ref, gb2i_ref), (dcre_ref, gc2r_ref), (dcim_ref, gc2i_ref)):
                for sb in range(nsb):
                    dst[sb * MXU_DIM:(sb + 1) * MXU_DIM, :] = _block_rows(src[sb])

    rev = lambda c: nc - 1 - c
    const = lambda a: pl.BlockSpec(a.shape, lambda c: (0,) * a.ndim)
    tile = pl.BlockSpec((chunk, w), lambda c: (rev(c), 0))
    row_n = pl.BlockSpec((1, n_state), lambda c: (0, 0))
    row_w = pl.BlockSpec((1, w), lambda c: (0, 0))
    st = pl.BlockSpec((1, 1, n_state), lambda c: (rev(c), 0, 0))
    acc8 = pl.BlockSpec((SUBLANES, n_state), lambda c: (0, 0))
    big = pltpu.VMEM((chunk, SB_STATE), F32)
    small = pltpu.VMEM((chunk, w), F32)
    row = pltpu.VMEM((1, n_state), F32)
    eight = pltpu.VMEM((SUBLANES, n_state), F32)
    blk = (nsb, MXU_DIM, SB_STATE)
    held = [pltpu.VMEM(blk, BF16)] * 6 + [pltpu.VMEM(blk, F32)] * 4
    vmem = (6 * _nbytes(blk, BF16) + 4 * _nbytes(blk, F32) + 5 * _nbytes((chunk, SB_STATE), F32)
            + 12 * _nbytes((chunk, w), F32) + 20 * _nbytes(bc_rows[0].shape, F32))
    res, landed = _call(
        body, [proj, proj, y, dyg, st_re, st_im, *bc_rows, *rows_p, d_row], name=name,
        out_shape=[jax.ShapeDtypeStruct((rows, w), F32)] + [jax.ShapeDtypeStruct(b.shape, F32) for b in bc_rows[:4]]
        + [jax.ShapeDtypeStruct((SUBLANES, n_state), F32)] * 2 + [jax.ShapeDtypeStruct((1, w), F32)],
        grid=(nc,),
        in_specs=_u_specs(w, o_u, chunk, rev) + [tile, tile, st, st] + [const(b) for b in bc_rows]
        + [row_n] * 6 + [pl.BlockSpec((steps, n_state), lambda c: (0, 0))] * 2 + [row_w],
        out_specs=[tile] + [const(b) for b in bc_rows[:4]] + [acc8] * 2 + [row_w],
        scratch_shapes=held + [small] * 3 + [pltpu.VMEM((w // LANES, chunk, LANES), F32)] + [big] * 4 + [row] * 4
        + [eight] * 4,
        semantics=("arbitrary",), vmem=vmem, rider=rider)
    return res if rider is None else (res, landed)


def _out_proj_loss(merged, w_o, x, target, name):
    rows, d = x.shape
    tm, tn = _tile(rows, 1024, SUBLANES), _tile(d, 1024)

    def body(a_ref, b_ref, x_ref, t_ref, g_ref, gb_ref, l_ref):
        err = x_ref[...] + _dot(a_ref[...], b_ref[...], NN) - t_ref[...]
        g = err * (1.0 / d)
        g_ref[...] = g
        gb_ref[...] = g.astype(BF16)
        part = jnp.sum(0.5 * err * g, axis=0, keepdims=True)
        first = pl.program_id(1) == 0

        @pl.when(first)
        def _():
            l_ref[...] = part

        @pl.when(jnp.logical_not(first))
        def _():
            l_ref[...] += part

    tile = pl.BlockSpec((tm, tn), lambda j, i: (i, j))
    vmem = 2 * (_nbytes((tm, d), BF16) + _nbytes((d, tn), BF16)) + 12 * _nbytes((tm, tn), F32)
    return _pallas(
        body, name=name,
        out_shape=[jax.ShapeDtypeStruct((rows, d), F32), jax.ShapeDtypeStruct((rows, d), BF16),
                   jax.ShapeDtypeStruct((1, d), F32)],
        grid=(d // tn, rows // tm),
        in_specs=[pl.BlockSpec((tm, d), lambda j, i: (i, 0)), pl.BlockSpec((d, tn), lambda j, i: (0, j)), tile, tile],
        out_specs=[tile, tile, pl.BlockSpec((1, tn), lambda j, i: (0, j))],
        compiler_params=_params(("parallel", "arbitrary"), vmem),
    )(merged, w_o, x, target)


def _pair_sum(grad, recv, name):
    r4, cdim = recv.shape
    r = r4 // N_CHIPS
    tr = _tile(r, 544, 16)
    g4 = grad.reshape(N_CHIPS, 2, r, cdim)
    r3 = recv.reshape(N_CHIPS, r, cdim)
    core = jnp.reshape(lax.axis_index("c"), (1,)).astype(jnp.int32)

    def body(c_ref, g_ref, r_ref, o_ref):
        o_ref[...] = (g_ref[0] + r_ref[...]).astype(BF16)

    out = _pallas(
        body, name=name, out_shape=jax.ShapeDtypeStruct((N_CHIPS, r, cdim), BF16),
        grid_spec=pltpu.PrefetchScalarGridSpec(
            num_scalar_prefetch=1, grid=(N_CHIPS, r // tr),
            in_specs=[pl.BlockSpec((1, 1, tr, cdim), lambda j, i, c: (j, c[0], i, 0)),
                      pl.BlockSpec((1, tr, cdim), lambda j, i, c: (j, i, 0))],
            out_specs=pl.BlockSpec((1, tr, cdim), lambda j, i, c: (j, i, 0))),
        compiler_params=_params(("parallel", "parallel"), 6 * _nbytes((tr, cdim), F32)),
    )(core, g4, r3)
    return out.reshape(r4, cdim)


def _chip_sum(recv, name):
    r4, cdim = recv.shape
    r = r4 // N_CHIPS
    tr = _tile(r, 544, 16)
    r3 = recv.reshape(N_CHIPS, r, cdim)

    def body(r_ref, o_ref):
        acc = r_ref[0].astype(F32)
        for j in range(1, N_CHIPS):
            acc = acc + r_ref[j].astype(F32)
        o_ref[...] = acc

    return _pallas(
        body, name=name, out_shape=jax.ShapeDtypeStruct((r, cdim), F32), grid=(r // tr,),
        in_specs=[pl.BlockSpec((N_CHIPS, tr, cdim), lambda i: (0, i, 0))],
        out_specs=pl.BlockSpec((tr, cdim), lambda i: (i, 0)),
        compiler_params=_params(("parallel",), 8 * _nbytes((tr, cdim), F32)),
    )(r3)


def _adamw_math(w, g, m, v):
    m = ADAM_B1 * m + (1.0 - ADAM_B1) * g
    v = ADAM_B2 * v + (1.0 - ADAM_B2) * (g * g)
    m_hat = m / (1.0 - ADAM_B1 ** ADAM_STEP)
    v_hat = v / (1.0 - ADAM_B2 ** ADAM_STEP)
    delta = -ADAM_LR * (m_hat / (jnp.sqrt(v_hat) + ADAM_EPS) + ADAM_WD * w)
    return delta, m, v


def _adamw(w, g, m, v, name):
    rows, cols = w.shape
    tr = _tile(rows, 256, SUBLANES)

    def body(w_ref, g_ref, m_ref, v_ref, d_ref, nm_ref, nv_ref):
        d, nm, nv = _adamw_math(w_ref[...], g_ref[...], m_ref[...], v_ref[...])
        d_ref[...] = d
        nm_ref[...] = nm
        nv_ref[...] = nv

    spec = pl.BlockSpec((tr, cols), lambda i: (i, 0))
    shp = jax.ShapeDtypeStruct((rows, cols), F32)
    return _pallas(
        body, name=name, out_shape=[shp] * 3, grid=(rows // tr,), in_specs=[spec] * 4, out_specs=[spec] * 3,
        compiler_params=_params(("parallel",)),
    )(w, g, m, v)


def _adamw_chips(w, parts, m, v, name):
    rows, cols = w.shape
    assert sum(p.shape[1] for p in parts) == cols
    tr = _tile(rows, 64, 16)
    n = len(parts)

    def body(*refs):
        w_ref, m_ref, v_ref = refs[0], refs[1 + n], refs[2 + n]
        g_ref, d_ref, nm_ref, nv_ref = refs[3 + n:]
        cols_g = []
        for p_ref in refs[1:1 + n]:
            acc = p_ref[0].astype(F32)
            for j in range(1, N_CHIPS):
                acc = acc + p_ref[j].astype(F32)
            cols_g.append(acc)
        g = cols_g[0] if n == 1 else jnp.concatenate(cols_g, axis=1)
        d, nm, nv = _adamw_math(w_ref[...], g, m_ref[...], v_ref[...])
        g_ref[...] = g
        d_ref[...] = d
        nm_ref[...] = nm
        nv_ref[...] = nv

    spec = pl.BlockSpec((tr, cols), lambda i: (i, 0))
    part_specs = [pl.BlockSpec((N_CHIPS, tr, p.shape[1]), lambda i: (0, i, 0)) for p in parts]
    shp = jax.ShapeDtypeStruct((rows, cols), F32)
    return _pallas(
        body, name=name, out_shape=[shp] * 4, grid=(rows // tr,),
        in_specs=[spec] + part_specs + [spec, spec], out_specs=[spec] * 4,
        compiler_params=_params(("parallel",)),
    )(w, *[p.reshape(N_CHIPS, rows, p.shape[1]) for p in parts], m, v)


def _adamw_small(w, parts, m, v, name):
    rows, cols = w.shape
    p3 = parts.reshape(N_DEV, rows, cols)

    def body(w_ref, p_ref, m_ref, v_ref, g_ref, d_ref, nm_ref, nv_ref):
        g = p_ref[0]
        for k in range(1, N_DEV):
            g = g + p_ref[k]
        d, nm, nv = _adamw_math(w_ref[...], g, m_ref[...], v_ref[...])
        g_ref[...] = g
        d_ref[...] = d
        nm_ref[...] = nm
        nv_ref[...] = nv

    shp = jax.ShapeDtypeStruct((rows, cols), F32)
    return _pallas(body, name=name, out_shape=[shp] * 4)(w, p3, m, v)


SMALL = ("norm_w", "q_norm_w", "k_norm_w", "sinks", "A_re", "A_im", "log_dt", "B_re", "B_im", "C_re", "C_im",
         "D_skip", "b_glu")
LARGE = ("w_in", "w_attn_proj", "w_glu", "w_ssm_proj", "w_out")
ORDER = ("norm_w", "w_in", "q_norm_w", "k_norm_w", "sinks", "w_attn_proj", "A_re", "A_im", "log_dt", "B_re", "B_im",
         "C_re", "C_im", "D_skip", "w_glu", "b_glu", "w_ssm_proj", "w_out")


SMALL_REST = ("loss",) + SMALL[1:]


def _pack(named, keys):
    flat = jnp.concatenate([named[k].reshape(-1).astype(F32) for k in keys])
    n = flat.shape[0]
    rows = -(-n // (LANES * SUBLANES)) * SUBLANES
    return jnp.pad(flat, (0, rows * LANES - n)).reshape(rows, LANES)


def _unpack(packed, like, keys):
    flat = packed.reshape(-1)
    out, o = {}, 0
    for k in keys:
        n = like[k].size
        out[k] = flat[o:o + n].reshape(like[k].shape)
        o += n
    return out


def _step(xs, target, p, shards):
    s_in, s_ap, s_glu, s_sp, s_o = shards
    seq, d = xs.shape
    attn_w = (d // 128) * HEAD_DIM
    n_q = attn_w // HEAD_DIM
    kv_w = N_KV_HEADS * HEAD_DIM
    ssm_w = d // 2
    n_groups = ssm_w // GROUP
    n_state = n_groups * STATE
    in_w = N_DEV * s_in.shape[0]
    assert in_w == 2 * attn_w + 2 * kv_w + 2 * ssm_w + 2 * d
    o_u = 2 * attn_w + 2 * kv_w
    o_z = o_u + ssm_w
    o_ga = o_z + ssm_w
    chunk = min(SSM_CHUNK, seq)
    cw = d // 4

    norm_row = p["norm_w"].reshape(1, d)
    h = _rmsnorm_fwd(xs, norm_row, "rmsnorm_fwd")
    half = d // W_IN_PARTS
    assert W_IN_PARTS == 2
    s_in_parts = [s_in[:, :half], s_in[:, half:]]
    (w_lo,) = _exchange(_all_gather(s_in_parts[:1]), "gather_w_in_0")
    part, (w_hi,) = _matmul(Cols(h, 0, half), w_lo, mode="nt", name="in_proj_0", tn=2176, out_dtype=BF16,
                            rider=_all_gather(s_in_parts[1:]))
    proj = _matmul(Cols(h, half, half), w_hi, mode="nt", name="in_proj_1", tn=2176, out_dtype=BF16, add=part)
    w_in_parts = [w_lo, w_hi]
    qw_row = jnp.tile(p["q_norm_w"], n_q).reshape(1, attn_w)
    kw_row = jnp.tile(p["k_norm_w"], N_KV_HEADS).reshape(1, kv_w)
    gmat = _head_mean_matrix()
    ag = _attention_fwd(proj, qw_row, kw_row, gmat, p["sinks"], attn_w=attn_w, kv_w=kv_w, name="attention_fwd")

    log_dt_col = p["log_dt"].reshape(n_groups, 1)
    prep = _ssm_prep(p["A_re"], p["A_im"], log_dt_col, chunk // SUBLANES, "ssm_prep")
    rows_p = [v.reshape(1, n_state) for v in prep[:6]] + [v.reshape(-1, n_state) for v in prep[6:]]
    bt_re, bt_im = p["B_re"].transpose(0, 2, 1), p["B_im"].transpose(0, 2, 1)
    cf_re, cf_im = prep[2][:, None, :], prep[3][:, None, :]
    bc_rows = [_ssm_rows(m) for m in (bt_re, bt_im, p["C_re"], p["C_im"],
                                      cf_re * bt_re - cf_im * bt_im, cf_re * bt_im + cf_im * bt_re)]
    d_row = p["D_skip"].reshape(1, ssm_w)
    (y_ssm, st_re, st_im, yg), (w_ap_t, w_glu_t, w_sp_t, w_o) = _ssm_fwd(
        proj, o_u, bc_rows[:4], rows_p, d_row, chunk=chunk, name="ssm_fwd",
        rider=_all_gather([s_ap, s_glu, s_sp, s_o]))
    glu = _matmul(yg, w_glu_t, mode="nt", name="glu_proj", out_dtype=BF16, bias=p["b_glu"].reshape(1, 2 * ssm_w))
    (ts,) = _ew(lambda ga, gb, z: ga * _sigmoid(gb) * _silu(z), name="glu_gate", rows=seq, width=ssm_w,
                tiles=[(glu, 0), (glu, ssm_w), (proj, o_z)], outs=[(BF16, ssm_w, 0)], cw=cw)
    yy = _matmul(ag, w_ap_t, mode="nt", name="attn_proj", out_dtype=BF16, out_cols=(2 * d, 0))
    yy = _matmul(ts, w_sp_t, mode="nt", name="ssm_proj", out_dtype=BF16, out_cols=(2 * d, d), into=yy)
    (merged,) = _ew(lambda ya, ys, ga, gs: _sigmoid(ga) * ya + _sigmoid(gs) * ys, name="merge", rows=seq, width=d,
                    tiles=[(yy, 0), (yy, d), (proj, o_ga), (proj, o_ga + d)], outs=[(BF16, d, 0)], cw=cw)
    dout, dout_b, loss_cols = _out_proj_loss(merged, w_o, xs, target, "out_proj_loss")
    loss_local = jnp.sum(loss_cols)

    g_w_o = _matmul(merged, dout_b, mode="tn", name="grad_w_out", tm=512, tk=4096)
    dmerged, (sib_o,) = _matmul(dout_b, w_o, mode="nt", name="d_merged", out_dtype=BF16,
                                rider=_sibling_exchange([g_w_o]))
    pair_o = _pair_sum(g_w_o, sib_o, "pair_sum_w_out")

    def merge_bwd(dm, y, g):
        s = _sigmoid(g)
        return dm * s, dm * y * s * (1.0 - s)

    dyy, dproj = _ew(merge_bwd, name="merge_bwd", rows=seq, width=2 * d,
                     tiles=[(dmerged, 0, d), (yy, 0), (proj, o_ga)],
                     outs=[(BF16, 2 * d, 0), (BF16, in_w, o_ga)], cw=cw)
    dy_a, dy_s = Cols(dyy, 0, d), Cols(dyy, d, d)
    g_w_ap_t = _matmul(dy_a, ag, mode="tn", name="grad_w_attn_proj", tm=512, tk=4096)
    g_w_sp_t = _matmul(dy_s, ts, mode="tn", name="grad_w_ssm_proj", tm=512, tk=4096)
    d_ag = _matmul(dy_a, w_ap_t, mode="nn", name="d_attn_gated", out_dtype=BF16)
    d_ts = _matmul(dy_s, w_sp_t, mode="nn", name="d_ssm_gated", out_dtype=BF16)

    (dproj, dkv, g_qw, g_kw, g_sinks), (chips_o, sib_ap, sib_sp) = _attention_bwd(
        proj, d_ag, dproj, qw_row, kw_row, gmat, p["sinks"], attn_w=attn_w, kv_w=kv_w, name="attention_bwd",
        rider=_join(_chip_exchange([pair_o]), _sibling_exchange([g_w_ap_t, g_w_sp_t])))
    pair_ap = _pair_sum(g_w_ap_t, sib_ap, "pair_sum_w_attn_proj")
    pair_sp = _pair_sum(g_w_sp_t, sib_sp, "pair_sum_w_ssm_proj")
    dproj = _attention_dkv(dproj, dkv, attn_w=attn_w, kv_w=kv_w, name="attention_dkv")

    n_half = ssm_w // _tile(2 * ssm_w, cw)

    def glu_bwd(j, dt, ga, gb, z):
        sb, sz = _sigmoid(gb), _silu(z)
        dg = jnp.where(j < n_half, dt * sb * sz, dt * ga * sb * (1.0 - sb) * sz)
        return dg, dg

    glu_ops = [(d_ts, 0, ssm_w), (glu, 0, ssm_w), (glu, ssm_w, ssm_w), (proj, o_z, ssm_w)]
    dglu, g_bglu = _ew(glu_bwd, name="glu_bwd", rows=seq, width=2 * ssm_w, tiles=glu_ops,
                       outs=[(BF16, 2 * ssm_w, 0)], accs=1, cw=cw, with_col=True)
    (dproj,) = _ew(lambda dt, ga, gb, z: dt * ga * _sigmoid(gb) * _dsilu(z), name="glu_bwd_z", rows=seq,
                   width=ssm_w, tiles=glu_ops, outs=[(BF16, in_w, o_z)], into=[dproj], cw=cw)
    g_w_glu_t = _matmul(dglu, yg, mode="tn", name="grad_w_glu", tm=512, tk=4096)
    d_yg = _matmul(dglu, w_glu_t, mode="nn", name="d_gelu", out_dtype=BF16)
    ((du, dbt_re, dbt_im, dc_re, dc_im, gabr, gabi, g_d), (chips_ap, chips_sp, sib_glu)) = _ssm_bwd(
        proj, o_u, y_ssm, d_yg, st_re, st_im, bc_rows, rows_p, d_row, chunk=chunk, name="ssm_bwd",
        rider=_join(_chip_exchange([pair_ap, pair_sp]), _sibling_exchange([g_w_glu_t])))
    pair_glu = _pair_sum(g_w_glu_t, sib_glu, "pair_sum_w_glu")
    (dproj,) = _ew(lambda v: v, name="du_store", rows=seq, width=ssm_w, tiles=[(du, 0)],
                   outs=[(BF16, in_w, o_u)], into=[dproj], cw=cw)
    g_a_re, g_a_im, g_log_dt, g_bt_re, g_bt_im = _ssm_param_bwd(
        p["A_re"], p["A_im"], log_dt_col, *[g.reshape(SUBLANES, n_groups, STATE) for g in (gabr, gabi)],
        bt_re, bt_im, _from_ssm_rows(dbt_re), _from_ssm_rows(dbt_im), "ssm_param_bwd")
    small_grads = dict(
        loss=loss_local, q_norm_w=g_qw.reshape(n_q, HEAD_DIM).sum(0), k_norm_w=g_kw.reshape(N_KV_HEADS, HEAD_DIM).sum(0),
        sinks=g_sinks[0, :n_q], A_re=g_a_re, A_im=g_a_im, log_dt=g_log_dt.reshape(n_groups),
        B_re=g_bt_re.transpose(0, 2, 1), B_im=g_bt_im.transpose(0, 2, 1),
        C_re=_from_ssm_rows(dc_re), C_im=_from_ssm_rows(dc_im),
        D_skip=g_d.reshape(n_groups, GROUP), b_glu=g_bglu.reshape(2 * ssm_w))

    n_parts = W_IN_PARTS
    wq = d // n_parts
    g_parts, pair_parts, chip_parts = [], [], []
    extra = [_chip_exchange([pair_glu]), _all_gather([_pack(small_grads, SMALL_REST)])]
    chips_glu = small_parts = dh = None
    for step in range(n_parts + 2):
        riders = list(extra) if step == 0 else []
        if 0 <= step - 2 < n_parts:
            riders.append(_chip_exchange([pair_parts[step - 2]]))
        if 0 <= step - 1 < n_parts:
            riders.append(_sibling_exchange([g_parts[step - 1]]))
        rider = _join(*riders) if riders else None
        if step < n_parts:
            res = _matmul(dproj, Cols(h, step * wq, wq), mode="tn", name="grad_w_in_%d" % step, tk=4096, rider=rider)
            out, landed = res if rider is not None else (res, [])
            g_parts.append(out)
        else:
            q = step - n_parts
            dh, landed = _matmul(dproj, w_in_parts[q], mode="nn", name="d_normed_%d" % q, tk=2176,
                                 out_cols=(d, q * wq), into=dh, rider=rider)
        landed = list(landed)
        if step == 0:
            chips_glu, small_parts = landed[:2]
            landed = landed[2:]
        if 0 <= step - 2 < n_parts:
            chip_parts.append(landed.pop(0))
        if 0 <= step - 1 < n_parts:
            pair_parts.append(_pair_sum(g_parts[step - 1], landed.pop(0), "pair_sum_w_in_%d" % (step - 1)))
    grad_x, g_norm = _rmsnorm_bwd(xs, norm_row, dh, dout, "rmsnorm_bwd")
    (norm_parts,) = _exchange(_all_gather([_pack(dict(norm_w=g_norm), ("norm_w",))]), "gather_norm_grad")
    from_chips = dict(zip(LARGE, (chip_parts, [chips_ap], [chips_glu], [chips_sp], [chips_o])))
    return grad_x, from_chips, small_parts, norm_parts


def kernel(x, norm_w, w_in, q_norm_w, k_norm_w, sinks, w_attn_proj, A_re, A_im, log_dt, B_re, B_im, C_re, C_im, D_skip, w_glu, b_glu, w_ssm_proj, w_out, loss_target, m_norm_w, m_w_in, m_q_norm_w, m_k_norm_w, m_sinks, m_w_attn_proj, m_A_re, m_A_im, m_log_dt, m_B_re, m_B_im, m_C_re, m_C_im, m_D_skip, m_w_glu, m_b_glu, m_w_ssm_proj, m_w_out, v_norm_w, v_w_in, v_q_norm_w, v_k_norm_w, v_sinks, v_w_attn_proj, v_A_re, v_A_im, v_log_dt, v_B_re, v_B_im, v_C_re, v_C_im, v_D_skip, v_w_glu, v_b_glu, v_w_ssm_proj, v_w_out):
    weights = dict(norm_w=norm_w, w_in=w_in, q_norm_w=q_norm_w, k_norm_w=k_norm_w, sinks=sinks,
                   w_attn_proj=w_attn_proj, A_re=A_re, A_im=A_im, log_dt=log_dt, B_re=B_re, B_im=B_im, C_re=C_re,
                   C_im=C_im, D_skip=D_skip, w_glu=w_glu, b_glu=b_glu, w_ssm_proj=w_ssm_proj, w_out=w_out)
    m_in = dict(norm_w=m_norm_w, w_in=m_w_in, q_norm_w=m_q_norm_w, k_norm_w=m_k_norm_w, sinks=m_sinks,
                w_attn_proj=m_w_attn_proj, A_re=m_A_re, A_im=m_A_im, log_dt=m_log_dt, B_re=m_B_re, B_im=m_B_im,
                C_re=m_C_re, C_im=m_C_im, D_skip=m_D_skip, w_glu=m_w_glu, b_glu=m_b_glu, w_ssm_proj=m_w_ssm_proj,
                w_out=m_w_out)
    v_in = dict(norm_w=v_norm_w, w_in=v_w_in, q_norm_w=v_q_norm_w, k_norm_w=v_k_norm_w, sinks=v_sinks,
                w_attn_proj=v_w_attn_proj, A_re=v_A_re, A_im=v_A_im, log_dt=v_log_dt, B_re=v_B_re, B_im=v_B_im,
                C_re=v_C_re, C_im=v_C_im, D_skip=v_D_skip, w_glu=v_w_glu, b_glu=v_b_glu, w_ssm_proj=v_w_ssm_proj,
                w_out=v_w_out)

    _, seq, d = x.shape
    column_sharded = LARGE[:4]
    as_rows = lambda k, a: a.T if k in column_sharded else a
    shards = [as_rows(k, weights[k]).astype(BF16) for k in LARGE]
    small = {k: weights[k] for k in SMALL}
    grad_x, from_chips, small_parts, norm_parts = _step(x.reshape(seq, d), loss_target.reshape(seq, d), small,
                                                        shards)

    grads, delta, new_m, new_v = {}, {}, {}, {}
    for k in LARGE:
        if k == "w_in":
            res = _adamw_chips(weights[k].T, from_chips[k], m_in[k].T, v_in[k].T, "adamw_" + k)
            grads[k], delta[k], new_m[k], new_v[k] = [a.T for a in res]
        elif k == "w_out":
            grads[k], delta[k], new_m[k], new_v[k] = _adamw_chips(weights[k], from_chips[k], m_in[k], v_in[k],
                                                                  "adamw_" + k)
        else:
            grads[k] = _chip_sum(from_chips[k][0], "chip_sum_" + k).T
            delta[k], new_m[k], new_v[k] = _adamw(weights[k], grads[k], m_in[k], v_in[k], "adamw_" + k)

    zero = jnp.zeros((), F32)
    for keys, parts in ((SMALL_REST, small_parts), (("norm_w",), norm_parts)):
        like = dict(small, loss=zero)
        packs = [_pack(dict(src, loss=zero), keys) for src in (weights, m_in, v_in)]
        res = _adamw_small(packs[0], parts, packs[1], packs[2], "adamw_small_%d" % len(keys))
        for dst, r in zip((grads, delta, new_m, new_v), res):
            dst.update(_unpack(r, like, keys))
    loss = grads["loss"]

    return (loss, grad_x.reshape(x.shape), *[grads[k] for k in ORDER], *[delta[k] for k in ORDER],
            *[new_m[k] for k in ORDER], *[new_v[k] for k in ORDER])
```

```python
import math
from typing import Callable, NamedTuple

import jax
import jax.numpy as jnp
import numpy as np
from jax import lax
from jax.experimental import pallas as pl
from jax.experimental.pallas import tpu as pltpu

F32 = jnp.float32
BF16 = jnp.bfloat16
MESH = pl.DeviceIdType.MESH

HEAD_DIM = 64
N_KV_HEADS = 4
GROUP = 16
STATE = 64
BLOCK = 128
NORM_EPS = 1e-6
N_DEV = 8
N_CHIPS = 4
LANES = 128
SUBLANES = 8
MXU_DIM = 256
VMEM_BYTES = 64 * 1024 * 1024
VMEM_CAP = VMEM_BYTES - 8 * 1024 * 1024

ADAM_LR = 0.001
ADAM_B1 = 0.9
ADAM_B2 = 0.999
ADAM_EPS = 1e-08
ADAM_WD = 0.01
ADAM_STEP = 10

GELU_C = math.sqrt(2.0 / math.pi)
GELU_K = 0.044715


def _tile(dim, pref, mult=LANES):
    if dim <= pref:
        return dim
    best = None
    for d in range(mult, pref + 1, mult):
        if dim % d == 0:
            best = d
    assert best is not None, (dim, pref, mult)
    return best


def _params(semantics=None, vmem=None):
    kw = {}
    if semantics is not None:
        kw["dimension_semantics"] = semantics
    if vmem is not None:
        kw["vmem_limit_bytes"] = int(min(VMEM_CAP, max(vmem, 32 * 1024 * 1024)))
    return pltpu.CompilerParams(**kw)


def _nbytes(shape, dtype):
    return math.prod(shape) * jnp.dtype(dtype).itemsize


def _sigmoid(x):
    return 1.0 / (1.0 + jnp.exp(-x))


def _silu(x):
    return x * _sigmoid(x)


def _dsilu(x):
    s = _sigmoid(x)
    return s * (1.0 + x * (1.0 - s))


def _gelu(x):
    return 0.5 * x * (1.0 + jnp.tanh(GELU_C * (x + GELU_K * x * x * x)))


def _dgelu(x):
    t = jnp.tanh(GELU_C * (x + GELU_K * x * x * x))
    return 0.5 * (1.0 + t) + 0.5 * x * (1.0 - t * t) * GELU_C * (1.0 + 3.0 * GELU_K * x * x)


def _dot(a, b, dims):
    return lax.dot_general(a, b, (dims, ((), ())), preferred_element_type=F32)


NN = ((1,), (0,))
NT = ((1,), (1,))
TN = ((0,), (0,))


def _any_spec():
    return pl.BlockSpec(memory_space=pl.ANY)


def _pallas(body, **kw):
    pin = lambda s: pltpu.HBM(s.shape, s.dtype) if isinstance(s, jax.ShapeDtypeStruct) else s
    out_shape = kw.pop("out_shape")
    out_shape = [pin(s) for s in out_shape] if isinstance(out_shape, (list, tuple)) else pin(out_shape)
    call = pl.pallas_call(body, out_shape=out_shape, **kw)

    def run(*operands):
        pinned = [pltpu.with_memory_space_constraint(o, pltpu.HBM) if jnp.issubdtype(o.dtype, jnp.floating) else o
                  for o in operands]
        return call(*pinned)

    return run


class Rider(NamedTuple):
    operands: tuple
    out_shapes: tuple
    sems: tuple
    start: Callable
    finish: Callable


def _all_gather(shards):
    n = len(shards)

    def copies(ins, outs, sems):
        send_sems, recv_sems, local_sems = sems
        x, y, c = lax.axis_index("x"), lax.axis_index("y"), lax.axis_index("c")
        me, sibling = (x, y, c), (x, y, 1 - c)
        chips = [(1 - x, y), (x, 1 - y), (1 - x, 1 - y)]

        def rows(k, px, py, pc):
            r = shards[k].shape[0]
            return outs[k].at[pl.ds((4 * px + 2 * py + pc) * r, r), :]

        def copy(k, s, block, to, src=None):
            return pltpu.make_async_remote_copy(
                src_ref=rows(k, *block) if src is None else src, dst_ref=rows(k, *block),
                send_sem=send_sems.at[7 * k + s], recv_sem=recv_sems.at[7 * k + s],
                device_id=to, device_id_type=MESH)

        mine = [pltpu.make_async_copy(ins[k], rows(k, *me), local_sems.at[k]) for k in range(n)]
        first = []
        for k in range(n):
            first.append(copy(k, 0, me, sibling, src=ins[k]))
            first += [copy(k, 1 + j, me, (*chip, c), src=ins[k]) for j, chip in enumerate(chips)]
        return me, sibling, chips, c, copy, mine, first

    def start(ins, outs, sems):
        *_, mine, first = copies(ins, outs, sems)
        for cp in mine + first:
            cp.start()

    def finish(ins, outs, sems):
        me, sibling, chips, c, copy, mine, first = copies(ins, outs, sems)
        passed = []
        for j, chip in enumerate(chips):
            for k in range(n):
                copy(k, 1 + j, (*chip, c), me).wait_recv()
                fwd = copy(k, 4 + j, (*chip, c), sibling)
                fwd.start()
                passed.append(fwd)
        for k in range(n):
            copy(k, 0, sibling, me).wait_recv()
            for j, chip in enumerate(chips):
                copy(k, 4 + j, (*chip, 1 - c), me).wait_recv()
        for cp in first + passed:
            cp.wait_send()
        for cp in mine:
            cp.wait()

    return Rider(
        tuple(shards),
        tuple(jax.ShapeDtypeStruct((N_DEV * s.shape[0], s.shape[1]), s.dtype) for s in shards),
        (pltpu.SemaphoreType.DMA((7 * n,)), pltpu.SemaphoreType.DMA((7 * n,)), pltpu.SemaphoreType.DMA((n,))),
        start, finish)


def _sibling_exchange(grads):
    n = len(grads)

    def copies(ins, outs, sems):
        send_sems, recv_sems = sems
        x, y, c = lax.axis_index("x"), lax.axis_index("y"), lax.axis_index("c")
        out = []
        for k in range(n):
            r = grads[k].shape[0] // N_DEV
            for j in range(N_CHIPS):
                out.append(pltpu.make_async_remote_copy(
                    src_ref=ins[k].at[pl.ds((2 * j + 1 - c) * r, r), :],
                    dst_ref=outs[k].at[pl.ds(j * r, r), :],
                    send_sem=send_sems.at[N_CHIPS * k + j], recv_sem=recv_sems.at[N_CHIPS * k + j],
                    device_id=(x, y, 1 - c), device_id_type=MESH))
        return out

    def start(ins, outs, sems):
        for cp in copies(ins, outs, sems):
            cp.start()

    def finish(ins, outs, sems):
        for cp in copies(ins, outs, sems):
            cp.wait()

    return Rider(
        tuple(grads), tuple(jax.ShapeDtypeStruct((g.shape[0] // 2, g.shape[1]), g.dtype) for g in grads),
        (pltpu.SemaphoreType.DMA((N_CHIPS * n,)), pltpu.SemaphoreType.DMA((N_CHIPS * n,))), start, finish)


def _chip_exchange(parts):
    n = len(parts)

    def copies(ins, outs, sems):
        send_sems, recv_sems, local_sems = sems
        x, y, c = lax.axis_index("x"), lax.axis_index("y"), lax.axis_index("c")
        my_chip = 2 * x + y
        chips = [(1 - x, y), (x, 1 - y), (1 - x, 1 - y)]
        local, sent = [], []
        for k in range(n):
            r = parts[k].shape[0] // N_CHIPS
            mine = pl.ds(my_chip * r, r)
            local.append(pltpu.make_async_copy(ins[k].at[mine, :], outs[k].at[mine, :], local_sems.at[k]))
            for s, (px, py) in enumerate(chips):
                sent.append(pltpu.make_async_remote_copy(
                    src_ref=ins[k].at[pl.ds((2 * px + py) * r, r), :], dst_ref=outs[k].at[mine, :],
                    send_sem=send_sems.at[3 * k + s], recv_sem=recv_sems.at[3 * k + s],
                    device_id=(px, py, c), device_id_type=MESH))
        return local, sent

    def start(ins, outs, sems):
        local, sent = copies(ins, outs, sems)
        for cp in local + sent:
            cp.start()

    def finish(ins, outs, sems):
        local, sent = copies(ins, outs, sems)
        for cp in sent + local:
            cp.wait()

    return Rider(
        tuple(parts), tuple(jax.ShapeDtypeStruct(p.shape, p.dtype) for p in parts),
        (pltpu.SemaphoreType.DMA((3 * n,)), pltpu.SemaphoreType.DMA((3 * n,)), pltpu.SemaphoreType.DMA((n,))),
        start, finish)


def _join(*riders):
    cuts_in, cuts_out, cuts_sem = [0], [0], [0]
    for r in riders:
        cuts_in.append(cuts_in[-1] + len(r.operands))
        cuts_out.append(cuts_out[-1] + len(r.out_shapes))
        cuts_sem.append(cuts_sem[-1] + len(r.sems))

    def each(which):
        def run(ins, outs, sems):
            for i, r in enumerate(riders):
                getattr(r, which)(ins[cuts_in[i]:cuts_in[i + 1]], outs[cuts_out[i]:cuts_out[i + 1]],
                                  sems[cuts_sem[i]:cuts_sem[i + 1]])
        return run

    return Rider(sum((r.operands for r in riders), ()), sum((r.out_shapes for r in riders), ()),
                 sum((r.sems for r in riders), ()), each("start"), each("finish"))


def _call(body, operands, *, name, out_shape, grid, in_specs, out_specs, scratch_shapes=(), aliases=None,
          semantics=None, vmem=None, rider=None):
    operands, out_shape, scratch_shapes = list(operands), list(out_shape), list(scratch_shapes)
    in_specs, out_specs = list(in_specs), list(out_specs)
    if rider is None:
        res = _pallas(
            body, name=name, out_shape=out_shape, grid=grid, in_specs=in_specs, out_specs=out_specs,
            scratch_shapes=scratch_shapes, input_output_aliases=aliases or {},
            compiler_params=_params(semantics, vmem))(*operands)
        return list(res), []
    n_in, n_out, n_scr = len(operands), len(out_shape), len(scratch_shapes)
    ri, ro = len(rider.operands), len(rider.out_shapes)

    def carried(*refs):
        a, b = n_in, n_in + ri
        c, d = b + n_out, b + n_out + ro
        e = d + n_scr
        ids = [pl.program_id(k) for k in range(len(grid))]
        first = ids[0] == 0
        last = ids[0] == grid[0] - 1
        for k in range(1, len(grid)):
            first = jnp.logical_and(first, ids[k] == 0)
            last = jnp.logical_and(last, ids[k] == grid[k] - 1)

        @pl.when(first)
        def _():
            rider.start(refs[a:b], refs[c:d], refs[e:])

        body(*refs[:a], *refs[b:c], *refs[d:e])

        @pl.when(last)
        def _():
            rider.finish(refs[a:b], refs[c:d], refs[e:])

    res = _pallas(
        carried, name=name, out_shape=out_shape + list(rider.out_shapes), grid=grid,
        in_specs=in_specs + [_any_spec()] * ri, out_specs=out_specs + [_any_spec()] * ro,
        scratch_shapes=scratch_shapes + list(rider.sems), input_output_aliases=aliases or {},
        compiler_params=_params(("arbitrary",) * len(grid), vmem))(*operands, *rider.operands)
    return list(res[:n_out]), list(res[n_out:])


def _exchange(rider, name):
    ri, ro = len(rider.operands), len(rider.out_shapes)

    def body(*refs):
        rider.start(refs[:ri], refs[ri:ri + ro], refs[ri + ro:])
        rider.finish(refs[:ri], refs[ri:ri + ro], refs[ri + ro:])

    return _pallas(
        body, name=name, out_shape=list(rider.out_shapes), in_specs=[_any_spec()] * ri,
        out_specs=[_any_spec()] * ro, scratch_shapes=list(rider.sems))(*rider.operands)


class Cols(NamedTuple):
    arr: jax.Array
    off: int
    width: int


def _cols(a):
    return a if isinstance(a, Cols) else Cols(a, 0, a.shape[1])


def _matmul(a, b, *, mode, name, out_dtype=F32, tm=1024, tn=1024, tk=2048, bias=None, add=None, out_cols=None,
            into=None, rider=None):
    a, b = _cols(a), _cols(b)
    if mode == "nn":
        (m, k), (k2, n) = (a.arr.shape[0], a.width), (b.arr.shape[0], b.width)
    elif mode == "nt":
        (m, k), (n, k2) = (a.arr.shape[0], a.width), (b.arr.shape[0], b.width)
    else:
        (k, m), (k2, n) = (a.arr.shape[0], a.width), (b.arr.shape[0], b.width)
    assert k == k2, (a.arr.shape, b.arr.shape, mode)
    tm, tn, tk = _tile(m, tm), _tile(n, tn), _tile(k, tk)
    nk = k // tk
    dims = {"nn": NN, "nt": NT, "tn": TN}[mode]
    if mode == "tn":
        assert a.off % tm == 0
        a_spec = pl.BlockSpec((tk, tm), lambda i, j, kk, o=a.off // tm: (kk, i + o))
    else:
        assert a.off % tk == 0
        a_spec = pl.BlockSpec((tm, tk), lambda i, j, kk, o=a.off // tk: (i, kk + o))
    if mode == "nt":
        assert b.off % tk == 0
        b_spec = pl.BlockSpec((tn, tk), lambda i, j, kk, o=b.off // tk: (j, kk + o))
    else:
        assert b.off % tn == 0
        b_spec = pl.BlockSpec((tk, tn), lambda i, j, kk, o=b.off // tn: (kk, j + o))
    in_specs, operands = [a_spec, b_spec], [a.arr, b.arr]
    assert bias is None or add is None
    if bias is not None:
        in_specs.append(pl.BlockSpec((1, tn), lambda i, j, kk: (0, j)))
        operands.append(bias)
    if add is not None:
        assert add.shape == (m, n)
        in_specs.append(pl.BlockSpec((tm, tn), lambda i, j, kk: (i, j)))
        operands.append(add)
    total_w, o_off = out_cols if out_cols is not None else (n, 0)
    assert o_off % tn == 0
    aliases = {}
    if into is not None:
        assert into.shape == (m, total_w) and into.dtype == out_dtype
        in_specs.append(_any_spec())
        operands.append(into)
        aliases = {len(operands) - 1: 0}
    n_in = len(operands)

    def body(*refs):
        a_ref, b_ref = refs[0], refs[1]
        bias_ref = refs[2] if bias is not None or add is not None else None
        o_ref = refs[n_in]
        acc_ref = refs[-1] if nk > 1 else None
        part = _dot(a_ref[...].astype(BF16), b_ref[...].astype(BF16), dims)

        def finish(acc):
            if bias_ref is not None:
                acc = acc + bias_ref[...]
            o_ref[...] = acc.astype(out_dtype)

        if nk == 1:
            finish(part)
        else:
            kk = pl.program_id(2)

            @pl.when(kk == 0)
            def _():
                acc_ref[...] = part

            @pl.when(kk > 0)
            def _():
                acc_ref[...] += part

            @pl.when(kk == nk - 1)
            def _():
                finish(acc_ref[...])

    vmem = 2 * (_nbytes((tm, tk), a.arr.dtype) + _nbytes((tk, tn), b.arr.dtype) + _nbytes((tm, tn), out_dtype))
    vmem += 3 * _nbytes((tm, tn), F32)
    (out,), landed = _call(
        body, operands, name=name, out_shape=[jax.ShapeDtypeStruct((m, total_w), out_dtype)],
        grid=(m // tm, n // tn, nk), in_specs=in_specs,
        out_specs=[pl.BlockSpec((tm, tn), lambda i, j, kk, o=o_off // tn: (i, j + o))],
        scratch_shapes=[pltpu.VMEM((tm, tn), F32)] if nk > 1 else [], aliases=aliases,
        semantics=("parallel", "parallel", "arbitrary"), vmem=vmem, rider=rider)
    return out if rider is None else (out, landed)


def _ew(fn, *, name, rows, width, tiles, vecs=(), outs, accs=0, tl=1024, cw=512, into=None, with_col=False):
    tl, cw = _tile(rows, tl, SUBLANES), _tile(width, cw)
    ncol = width // cw
    nt_, nv = len(tiles), len(vecs)
    into = list(into) if into is not None else [None] * len(outs)
    aliased = [t for t in into if t is not None]

    def off(o):
        assert o % cw == 0, (name, o, cw)
        return o // cw

    in_specs, vmem = [], 0
    for t in tiles:
        arr, o = t[0], off(t[1])
        wrap = t[2] // cw if len(t) > 2 else ncol
        in_specs.append(pl.BlockSpec((tl, cw), lambda j, i, o=o, wrap=wrap: (i, o + j % wrap)))
        vmem += _nbytes((tl, cw), arr.dtype)
    in_specs += [pl.BlockSpec((1, cw), lambda j, i, o=off(o): (0, j + o)) for _, o in vecs]
    in_specs += [_any_spec() for _ in aliased]
    out_shape, out_specs, aliases = [], [], {}
    n_in = nt_ + nv
    for idx, ((dt, tw, o), tgt) in enumerate(zip(outs, into)):
        out_shape.append(jax.ShapeDtypeStruct((rows, tw), dt))
        out_specs.append(pl.BlockSpec((tl, cw), lambda j, i, o=off(o): (i, j + o)))
        vmem += _nbytes((tl, cw), dt)
        if tgt is not None:
            assert tgt.shape == (rows, tw) and tgt.dtype == dt, (name, tgt.shape, tgt.dtype)
            aliases[n_in + len(aliases)] = idx
    for _ in range(accs):
        out_shape.append(jax.ShapeDtypeStruct((1, width), F32))
        out_specs.append(pl.BlockSpec((1, cw), lambda j, i: (0, j)))
    n_out = len(outs)

    def body(*refs):
        vals = [r[...].astype(F32) for r in refs[:n_in]]
        out_refs = refs[n_in + len(aliased):]
        res = fn(pl.program_id(0), *vals) if with_col else fn(*vals)
        res = res if isinstance(res, (tuple, list)) else (res,)
        assert len(res) == n_out + accs, (name, len(res))
        for r, v in zip(out_refs[:n_out], res[:n_out]):
            r[...] = v.astype(r.dtype)
        first = pl.program_id(1) == 0
        for r, v in zip(out_refs[n_out:], res[n_out:]):
            s = jnp.sum(v, axis=0, keepdims=True)

            @pl.when(first)
            def _(r=r, s=s):
                r[...] = s

            @pl.when(jnp.logical_not(first))
            def _(r=r, s=s):
                r[...] += s

    return _pallas(
        body, name=name, out_shape=out_shape, grid=(ncol, rows // tl),
        in_specs=in_specs, out_specs=out_specs, input_output_aliases=aliases,
        compiler_params=_params(("parallel", "arbitrary"), 3 * vmem),
    )(*[t[0] for t in tiles], *[v for v, _ in vecs], *aliased)


def _rmsnorm_fwd(x, w_row, name, rider=None):
    rows, d = x.shape
    tl = _tile(rows, 512, SUBLANES)

    def body(x_ref, w_ref, h_ref):
        xv = x_ref[...]
        rstd = lax.rsqrt(jnp.mean(xv * xv, axis=-1, keepdims=True) + NORM_EPS)
        h_ref[...] = (xv * rstd * w_ref[...]).astype(BF16)

    (h,), landed = _call(
        body, [x, w_row], name=name, out_shape=[jax.ShapeDtypeStruct((rows, d), BF16)], grid=(rows // tl,),
        in_specs=[pl.BlockSpec((tl, d), lambda i: (i, 0)), pl.BlockSpec((1, d), lambda i: (0, 0))],
        out_specs=[pl.BlockSpec((tl, d), lambda i: (i, 0))], semantics=("parallel",), rider=rider)
    return h if rider is None else (h, landed)


def _rmsnorm_bwd(x, w_row, dh, dout, name, rider=None):
    rows, d = x.shape
    tl = _tile(rows, 256, SUBLANES)

    def body(x_ref, w_ref, dh_ref, dout_ref, gx_ref, gw_ref):
        xv = x_ref[...]
        rstd = lax.rsqrt(jnp.mean(xv * xv, axis=-1, keepdims=True) + NORM_EPS)
        xn = xv * rstd
        dhv = dh_ref[...]
        dxn = dhv * w_ref[...]
        dx = rstd * (dxn - xn * jnp.mean(dxn * xn, axis=-1, keepdims=True))
        gx_ref[...] = dout_ref[...] + dx
        gw = jnp.sum(dhv * xn, axis=0, keepdims=True)

        @pl.when(pl.program_id(0) == 0)
        def _():
            gw_ref[...] = gw

        @pl.when(pl.program_id(0) > 0)
        def _():
            gw_ref[...] += gw

    tile = pl.BlockSpec((tl, d), lambda i: (i, 0))
    row = pl.BlockSpec((1, d), lambda i: (0, 0))
    res, landed = _call(
        body, [x, w_row, dh, dout], name=name,
        out_shape=[jax.ShapeDtypeStruct((rows, d), F32), jax.ShapeDtypeStruct((1, d), F32)],
        grid=(rows // tl,), in_specs=[tile, row, tile, tile], out_specs=[tile, row],
        semantics=("arbitrary",), rider=rider)
    return res if rider is None else (res, landed)


def _head_mean(x, gmat):
    hi = x.astype(BF16)
    lo = (x - hi.astype(F32)).astype(BF16)
    out = []
    for s in range(x.shape[1] // MXU_DIM):
        sl = slice(s * MXU_DIM, (s + 1) * MXU_DIM)
        out.append(_dot(hi[:, sl], gmat, NN) + _dot(lo[:, sl], gmat, NN))
    return out[0] if len(out) == 1 else jnp.concatenate(out, axis=1)


def _head_mean_matrix():
    blk = jnp.arange(MXU_DIM) // HEAD_DIM
    return jnp.where(blk[:, None] == blk[None, :], 1.0 / HEAD_DIM, 0.0).astype(BF16)


def _spread_head(x, g, width):
    col = x[:, (g // 2) * LANES:(g // 2 + 1) * LANES]
    other = pltpu.roll(col, HEAD_DIM, axis=1)
    low = lax.broadcasted_iota(jnp.int32, col.shape, 1) < HEAD_DIM
    both = jnp.where(low, col, other) if g % 2 == 0 else jnp.where(low, other, col)
    return both if width == LANES else jnp.concatenate([both] * (width // LANES), axis=1)


def _head_diagonal(t, per_kv):
    head = lax.broadcasted_iota(jnp.int32, t.shape, 1) // HEAD_DIM
    zero = jnp.zeros_like(t)
    return jnp.concatenate([jnp.where(head == r, t, zero) for r in range(per_kv)], axis=0)


def _fold_heads(x, per_kv):
    rows = x.shape[0] // per_kv
    head = lax.broadcasted_iota(jnp.int32, (rows, x.shape[1]), 1) // HEAD_DIM
    acc = jnp.where(head == 0, x[0:rows], 0.0)
    for r in range(1, per_kv):
        acc = acc + jnp.where(head == r, x[r * rows:(r + 1) * rows], 0.0)
    while acc.shape[1] > LANES:
        half = acc.shape[1] // 2
        acc = acc[:, :half] + acc[:, half:]
    return acc + pltpu.roll(acc, HEAD_DIM, axis=1)


def _join_heads(parts):
    low = lax.broadcasted_iota(jnp.int32, parts[0].shape, 1) < HEAD_DIM
    cols = [jnp.where(low, parts[2 * j], parts[2 * j + 1]) for j in range(len(parts) // 2)]
    return cols[0] if len(cols) == 1 else jnp.concatenate(cols, axis=1)


def _attn_specs(attn_w, kv_w):
    half = attn_w // 2
    kcol, vcol = attn_w // kv_w, attn_w // kv_w + 1
    gcol = (attn_w + 2 * kv_w) // half
    prev = lambda i: jnp.maximum(i - 1, 0)
    return [
        pl.BlockSpec((BLOCK, attn_w), lambda i: (i, 0)),
        pl.BlockSpec((BLOCK, kv_w), lambda i: (prev(i), kcol)),
        pl.BlockSpec((BLOCK, kv_w), lambda i: (i, kcol)),
        pl.BlockSpec((BLOCK, kv_w), lambda i: (prev(i), vcol)),
        pl.BlockSpec((BLOCK, kv_w), lambda i: (i, vcol)),
        pl.BlockSpec((BLOCK, half), lambda i: (i, gcol)),
        pl.BlockSpec((BLOCK, half), lambda i: (i, gcol + 1)),
    ]


def _band_mask(i):
    q_loc = lax.broadcasted_iota(jnp.int32, (BLOCK, 2 * BLOCK), 0) + BLOCK
    k_loc = lax.broadcasted_iota(jnp.int32, (BLOCK, 2 * BLOCK), 1)
    diff = q_loc - k_loc
    first_key = jnp.where(i == 0, BLOCK, 0)
    return (diff >= 0) & (diff < BLOCK) & (k_loc >= first_key)


def _softmax_with_sink(s, sink):
    m = jnp.maximum(jnp.max(s, axis=-1, keepdims=True), sink)
    p = jnp.exp(s - m)
    e_sink = jnp.exp(sink - m)
    den = jnp.sum(p, axis=-1, keepdims=True) + e_sink
    inv = 1.0 / den
    return p * inv, e_sink * inv


def _attn_block(i, q, kk, vv, qw, kw, gmat, sink_ref, per_kv):
    scale = 1.0 / math.sqrt(HEAD_DIM)
    keys = 2 * BLOCK
    valid = _band_mask(i)
    q_rstd = lax.rsqrt(_head_mean(q * q, gmat) + NORM_EPS)
    qn = q * q_rstd
    qh = (qn * qw).astype(BF16)
    k_rstd = lax.rsqrt(_head_mean(kk * kk, gmat) + NORM_EPS)
    kn = kk * k_rstd
    kh = kn * kw
    gw = per_kv * HEAD_DIM
    groups = []
    for g in range(N_KV_HEADS):
        kd = _head_diagonal(_spread_head(kh, g, gw).astype(BF16), per_kv)
        vd = _head_diagonal(_spread_head(vv, g, gw).astype(BF16), per_kv)
        qg = qh[:, g * gw:(g + 1) * gw]
        s_all = _dot(qg, kd, NT) * scale
        ps, p_sinks = [], []
        for r in range(per_kv):
            s = jnp.where(valid, s_all[:, r * keys:(r + 1) * keys], -1e30)
            p, p_sink = _softmax_with_sink(s, sink_ref[g * per_kv + r])
            ps.append(p)
            p_sinks.append(p_sink)
        pb = jnp.concatenate(ps, axis=1).astype(BF16)
        groups.append((kd, vd, qg, ps, p_sinks, pb, _dot(pb, vd, NN)))
    return qn, q_rstd, kn, k_rstd, groups


def _attention_fwd(proj, qw_row, kw_row, gmat, sinks, *, attn_w, kv_w, name):
    rows = proj.shape[0]
    per_kv = attn_w // HEAD_DIM // N_KV_HEADS

    def body(q_ref, kp_ref, kc_ref, vp_ref, vc_ref, glo_ref, ghi_ref, qw_ref, kw_ref, gm_ref, sink_ref, o_ref):
        kk = jnp.concatenate([kp_ref[...], kc_ref[...]], axis=0).astype(F32)
        vv = jnp.concatenate([vp_ref[...], vc_ref[...]], axis=0).astype(F32)
        gate = jnp.concatenate([glo_ref[...], ghi_ref[...]], axis=1).astype(F32)
        *_, groups = _attn_block(pl.program_id(0), q_ref[...].astype(F32), kk, vv, qw_ref[...], kw_ref[...], gm_ref[...],
                                 sink_ref, per_kv)
        attn = jnp.concatenate([grp[-1] for grp in groups], axis=1)
        o_ref[...] = (attn * _silu(gate)).astype(BF16)

    const = lambda a: pl.BlockSpec(a.shape, lambda i: (0, 0))
    return _pallas(
        body, name=name, out_shape=jax.ShapeDtypeStruct((rows, attn_w), BF16), grid=(rows // BLOCK,),
        in_specs=_attn_specs(attn_w, kv_w) + [const(qw_row), const(kw_row), const(gmat),
                                              pl.BlockSpec(memory_space=pltpu.SMEM)],
        out_specs=pl.BlockSpec((BLOCK, attn_w), lambda i: (i, 0)),
        compiler_params=_params(("parallel",), 40 * 1024 * 1024),
    )(proj, proj, proj, proj, proj, proj, proj, qw_row, kw_row, gmat, sinks)


def _attention_bwd(proj, d_ag, dproj, qw_row, kw_row, gmat, sinks, *, attn_w, kv_w, name, rider=None):
    rows = proj.shape[0]
    nb = rows // BLOCK
    per_kv = attn_w // HEAD_DIM // N_KV_HEADS
    gw = per_kv * HEAD_DIM
    keys = 2 * BLOCK
    scale = 1.0 / math.sqrt(HEAD_DIM)
    w_out = 2 * attn_w + 2 * kv_w

    def body(q_ref, kp_ref, kc_ref, vp_ref, vc_ref, glo_ref, ghi_ref, dag_ref, qw_ref, kw_ref, gm_ref, sink_ref, _,
             dp_ref, dkv_ref, gqw_ref, gkw_ref, gs_ref):
        i = pl.program_id(0)
        kk = jnp.concatenate([kp_ref[...], kc_ref[...]], axis=0).astype(F32)
        vv = jnp.concatenate([vp_ref[...], vc_ref[...]], axis=0).astype(F32)
        gate = jnp.concatenate([glo_ref[...], ghi_ref[...]], axis=1).astype(F32)
        d_ag_v = dag_ref[...].astype(F32)
        qw, kw, gmat_v = qw_ref[...], kw_ref[...], gm_ref[...]
        qn, q_rstd, kn, k_rstd, groups = _attn_block(i, q_ref[...].astype(F32), kk, vv, qw, kw, gmat_v, sink_ref,
                                                     per_kv)
        lane = lax.broadcasted_iota(jnp.int32, (SUBLANES, LANES), 1)
        sub = lax.broadcasted_iota(jnp.int32, (SUBLANES, LANES), 0)
        gsink = jnp.zeros((SUBLANES, LANES), F32)
        dq_groups, dgate_groups, dk_heads, dv_heads = [], [], [], []
        for g, (kd, vd, qg, ps, p_sinks, pb, o) in enumerate(groups):
            cs = slice(g * gw, (g + 1) * gw)
            gate_g, d_ag_g = gate[:, cs], d_ag_v[:, cs]
            dgate_groups.append(d_ag_g * o * _dsilu(gate_g))
            do = (d_ag_g * _silu(gate_g)).astype(BF16)
            dp_all = _dot(do, vd, NT)
            dss = []
            for r in range(per_kv):
                p, dp = ps[r], dp_all[:, r * keys:(r + 1) * keys]
                delta = jnp.sum(p * dp, axis=-1, keepdims=True)
                dss.append(p * (dp - delta) * scale)
                gs_h = jnp.sum(-p_sinks[r] * delta, axis=0, keepdims=True)
                gsink = gsink + jnp.where((lane == g * per_kv + r) & (sub == 0), gs_h, 0.0)
            ds = jnp.concatenate(dss, axis=1).astype(BF16)
            dq_groups.append(_dot(ds, kd, NN))
            dk_heads.append(_fold_heads(_dot(ds, qg, TN), per_kv))
            dv_heads.append(_fold_heads(_dot(pb, do, TN), per_kv))
        dqh = jnp.concatenate(dq_groups, axis=1)
        gqw = jnp.sum(dqh * qn, axis=0, keepdims=True)
        dqn = dqh * qw
        dq = q_rstd * (dqn - qn * _head_mean(dqn * qn, gmat_v))
        dkh = _join_heads(dk_heads)
        gkw = jnp.sum(dkh * kn, axis=0, keepdims=True)
        dkn = dkh * kw
        dk = k_rstd * (dkn - kn * _head_mean(dkn * kn, gmat_v))
        dp_ref[:, 0:attn_w] = dq.astype(BF16)
        dp_ref[:, attn_w:attn_w + 2 * kv_w] = jnp.zeros((BLOCK, 2 * kv_w), BF16)
        dp_ref[:, attn_w + 2 * kv_w:w_out] = jnp.concatenate(dgate_groups, axis=1).astype(BF16)
        dkv_ref[0] = jnp.concatenate([dk, _join_heads(dv_heads)], axis=1)

        @pl.when(i == 0)
        def _():
            gqw_ref[...] = gqw
            gkw_ref[...] = gkw
            gs_ref[...] = gsink

        @pl.when(i > 0)
        def _():
            gqw_ref[...] += gqw
            gkw_ref[...] += gkw
            gs_ref[...] += gsink

    const = lambda a: pl.BlockSpec(a.shape, lambda i: (0, 0))
    res, landed = _call(
        body, [proj, proj, proj, proj, proj, proj, proj, d_ag, qw_row, kw_row, gmat, sinks, dproj], name=name,
        out_shape=[jax.ShapeDtypeStruct(dproj.shape, BF16),
                   jax.ShapeDtypeStruct((nb, 2 * BLOCK, 2 * kv_w), F32),
                   jax.ShapeDtypeStruct(qw_row.shape, F32), jax.ShapeDtypeStruct(kw_row.shape, F32),
                   jax.ShapeDtypeStruct((SUBLANES, LANES), F32)],
        grid=(nb,),
        in_specs=_attn_specs(attn_w, kv_w) + [pl.BlockSpec((BLOCK, attn_w), lambda i: (i, 0)), const(qw_row),
                                              const(kw_row), const(gmat), pl.BlockSpec(memory_space=pltpu.SMEM),
                                              _any_spec()],
        out_specs=[pl.BlockSpec((BLOCK, w_out), lambda i: (i, 0)),
                   pl.BlockSpec((1, 2 * BLOCK, 2 * kv_w), lambda i: (i, 0, 0)),
                   const(qw_row), const(kw_row), pl.BlockSpec((SUBLANES, LANES), lambda i: (0, 0))],
        aliases={12: 0}, semantics=("arbitrary",), vmem=48 * 1024 * 1024, rider=rider)
    return res if rider is None else (res, landed)


def _attention_dkv(dproj, dkv, *, attn_w, kv_w, name):
    rows = dproj.shape[0]
    nb = rows // BLOCK
    col = attn_w // (2 * kv_w)

    def body(cur_ref, nxt_ref, _, o_ref):
        i = pl.program_id(0)
        nxt = jnp.where(i < nb - 1, nxt_ref[0, 0:BLOCK, :], 0.0)
        o_ref[...] = (cur_ref[0, BLOCK:2 * BLOCK, :] + nxt).astype(BF16)

    blk = lambda f: pl.BlockSpec((1, 2 * BLOCK, 2 * kv_w), f)
    return _pallas(
        body, name=name, out_shape=jax.ShapeDtypeStruct(dproj.shape, BF16), grid=(nb,),
        in_specs=[blk(lambda i: (i, 0, 0)), blk(lambda i: (jnp.minimum(i + 1, nb - 1), 0, 0)), _any_spec()],
        out_specs=pl.BlockSpec((BLOCK, 2 * kv_w), lambda i: (i, col)),
        input_output_aliases={2: 0},
        compiler_params=_params(("parallel",)),
    )(dkv, dkv, dproj)


def _cmul(ar, ai, br, bi):
    return ar * br - ai * bi, ar * bi + ai * br


def _ssm_prep(a_re, a_im, log_dt_col, steps, name):
    def body(are_ref, aim_ref, ldt_ref, abr_ref, abi_ref, cfr_ref, cfi_ref, apr_ref, api_ref, pwr_ref, pwi_ref):
        are, aim = are_ref[...], aim_ref[...]
        dt = jnp.exp(ldt_ref[...])
        mag = jnp.exp(dt * are)
        abr = mag * jnp.cos(dt * aim)
        abi = mag * jnp.sin(dt * aim)
        num_re, num_im = abr - 1.0, abi
        den = are * are + aim * aim
        abr_ref[...] = abr
        abi_ref[...] = abi
        cfr_ref[...] = (num_re * are + num_im * aim) / den
        cfi_ref[...] = (num_im * are - num_re * aim) / den
        pr, pi = jnp.ones_like(abr), jnp.zeros_like(abr)
        for k in range(steps):
            pwr_ref[k] = pr
            pwi_ref[k] = pi
            pr, pi = _cmul(pr, pi, abr, abi)
        apr_ref[...] = pr
        api_ref[...] = pi

    shp = jax.ShapeDtypeStruct(a_re.shape, F32)
    pows = jax.ShapeDtypeStruct((steps,) + a_re.shape, F32)
    return _pallas(body, name=name, out_shape=[shp] * 6 + [pows] * 2)(a_re, a_im, log_dt_col)


def _ssm_param_bwd(a_re, a_im, log_dt_col, d_ab_re, d_ab_im, b_re, b_im, dbt_re, dbt_im, name):
    def body(are_ref, aim_ref, ldt_ref, gabr_ref, gabi_ref, br_ref, bi_ref, tr_ref, ti_ref,
             dar_ref, dai_ref, dldt_ref, dbr_ref, dbi_ref):
        are, aim = are_ref[...], aim_ref[...]
        dt = jnp.exp(ldt_ref[...])
        mag = jnp.exp(dt * are)
        abr = mag * jnp.cos(dt * aim)
        abi = mag * jnp.sin(dt * aim)
        den = are * are + aim * aim
        cfr = ((abr - 1.0) * are + abi * aim) / den
        cfi = (abi * are - (abr - 1.0) * aim) / den
        gabr, gabi = jnp.sum(gabr_ref[...], axis=0), jnp.sum(gabi_ref[...], axis=0)
        t_re, t_im = tr_ref[...], ti_ref[...]
        g_r, g_i = _cmul(br_ref[...], -bi_ref[...], t_re, t_im)
        gcfr, gcfi = jnp.sum(g_r, axis=1), jnp.sum(g_i, axis=1)
        dbr, dbi = _cmul(cfr[:, None, :], -cfi[:, None, :], t_re, t_im)
        dbr_ref[...] = dbr
        dbi_ref[...] = dbi
        inv_r, inv_i = are / den, -aim / den
        t_r, t_i = _cmul(inv_r, -inv_i, gcfr, gcfi)
        gabr, gabi = gabr + t_r, gabi + t_i
        q_r, q_i = _cmul(cfr, cfi, inv_r, inv_i)
        da_r, da_i = _cmul(-q_r, q_i, gcfr, gcfi)
        gz_r, gz_i = _cmul(abr, -abi, gabr, gabi)
        dar_ref[...] = da_r + dt * gz_r
        dai_ref[...] = da_i + dt * gz_i
        dldt_ref[...] = dt * jnp.sum(are * gz_r + aim * gz_i, axis=-1, keepdims=True)

    shp = jax.ShapeDtypeStruct(a_re.shape, F32)
    bshp = jax.ShapeDtypeStruct(b_re.shape, F32)
    return _pallas(body, name=name,
                   out_shape=[shp, shp, jax.ShapeDtypeStruct(log_dt_col.shape, F32), bshp, bshp])(
        a_re, a_im, log_dt_col, d_ab_re, d_ab_im, b_re, b_im, dbt_re, dbt_im)


SCAN_LANES = 512
SSM_CHUNK = 256
W_IN_PARTS = 2


def _scan_segments(xr_ref, xi_ref, a_re, a_im, ap_re, ap_im, pw_re, pw_im, carry_re, carry_im, cm_re, cm_im, steps,
                   reverse, base):
    n = xr_ref.shape[1]
    seg_order = range(SUBLANES - 1, -1, -1) if reverse else range(SUBLANES)
    sign = -1.0 if reverse else 1.0
    for c0 in range(0, n, SCAN_LANES):
        ls = slice(c0, c0 + SCAN_LANES)
        gs = slice(base + c0, base + c0 + SCAN_LANES)
        ar = jnp.broadcast_to(a_re[:, gs], (SUBLANES, SCAN_LANES))
        ai = jnp.broadcast_to(a_im[:, gs], (SUBLANES, SCAN_LANES))
        end_r = jnp.zeros((SUBLANES, SCAN_LANES), F32)
        end_i = jnp.zeros((SUBLANES, SCAN_LANES), F32)
        for j in range(steps):
            k = j if reverse else steps - 1 - j
            rws = slice(j * SUBLANES, (j + 1) * SUBLANES)
            tr, ti = _cmul(pw_re[k:k + 1, gs], sign * pw_im[k:k + 1, gs], xr_ref[rws, ls], xi_ref[rws, ls])
            end_r, end_i = end_r + tr, end_i + ti
        cr, ci = carry_re[:, gs], carry_im[:, gs]
        apr, api = ap_re[:, gs], ap_im[:, gs]
        for r in seg_order:
            cm_re[r:r + 1, gs] = cr
            cm_im[r:r + 1, gs] = ci
            tr, ti = _cmul(apr, api, cr, ci)
            cr, ci = end_r[r:r + 1, :] + tr, end_i[r:r + 1, :] + ti
        carry_re[:, gs] = cr
        carry_im[:, gs] = ci

        def run(t, s, ar=ar, ai=ai, ls=ls):
            j = steps - 1 - t if reverse else t
            r0 = pl.multiple_of(j * SUBLANES, SUBLANES)
            sr, si = _cmul(ar, ai, s[0], s[1])
            sr = sr + xr_ref[pl.ds(r0, SUBLANES), ls]
            si = si + xi_ref[pl.ds(r0, SUBLANES), ls]
            xr_ref[pl.ds(r0, SUBLANES), ls] = sr
            xi_ref[pl.ds(r0, SUBLANES), ls] = si
            return sr, si

        lax.fori_loop(0, steps, run, (cm_re[:, gs], cm_im[:, gs]))


SB_GROUPS = MXU_DIM // GROUP
SB_STATE = SB_GROUPS * STATE


def _ssm_rows(m):
    flat = m.reshape(-1, STATE).astype(F32)
    return jnp.concatenate([flat, flat], axis=1)


def _from_ssm_rows(rows):
    return rows[:, :STATE].reshape(-1, GROUP, STATE)


def _own_group(shape):
    row_g = lax.broadcasted_iota(jnp.int32, shape, 0) // GROUP
    col_g = lax.broadcasted_iota(jnp.int32, shape, 1) // STATE
    return row_g == col_g


def _block_diagonal(rows):
    tiled = jnp.concatenate([rows] * (SB_STATE // LANES), axis=1)
    return jnp.where(_own_group(tiled.shape), tiled, 0.0).astype(BF16)


def _block_rows(acc):
    x = jnp.where(_own_group(acc.shape), acc, 0.0)
    while x.shape[1] > LANES:
        half = x.shape[1] // 2
        x = x[:, :half] + x[:, half:]
    return x + pltpu.roll(x, STATE, axis=1)


def _rows_to_segments(dst, srcs, steps, stage):
    for ref, off in srcs:
        for k in range(ref.shape[1] // LANES):
            stage[off // LANES + k] = ref[:, k * LANES:(k + 1) * LANES].astype(F32)
    for k in range(dst.shape[1] // LANES):
        for j in range(steps):
            dst[j * SUBLANES:(j + 1) * SUBLANES, k * LANES:(k + 1) * LANES] = (
                stage[k, pl.ds(j, SUBLANES, stride=steps), :])


def _segments_to_rows(dst, src, steps, stage):
    for k in range(src.shape[1] // LANES):
        for j in range(steps):
            stage[k, pl.ds(j, SUBLANES, stride=steps), :] = (
                src[j * SUBLANES:(j + 1) * SUBLANES, k * LANES:(k + 1) * LANES])
    for k in range(src.shape[1] // LANES):
        dst[:, k * LANES:(k + 1) * LANES] = stage[k]


def _u_specs(w, o_u, chunk, index):
    half = w // 2
    assert o_u % half == 0
    return [pl.BlockSpec((chunk, half), lambda c, k=k: (index(c), o_u // half + k)) for k in range(2)]


def _ssm_fwd(proj, o_u, bc_rows, rows_p, d_row, *, chunk, name, rider=None):
    rows = proj.shape[0]
    w = d_row.shape[1]
    nc = rows // chunk
    steps = chunk // SUBLANES
    nsb = w // MXU_DIM
    n_state = nsb * SB_STATE

    def body(ulo_ref, uhi_ref, b2r_ref, b2i_ref, c2r_ref, c2i_ref, abr_ref, abi_ref, cfr_ref, cfi_ref, apr_ref,
             api_ref, pwr_ref, pwi_ref, d_ref, y_ref, str_ref, sti_ref, yg_ref, bre_ref, bim_ref, cre_ref, cim_ref,
             useg, yseg, stage, sr, si,
             carry_r, carry_i, cm_r, cm_i):
        @pl.when(pl.program_id(0) == 0)
        def _():
            for src, dst in ((b2r_ref, bre_ref), (b2i_ref, bim_ref), (c2r_ref, cre_ref), (c2i_ref, cim_ref)):
                for sb in range(nsb):
                    dst[sb] = _block_diagonal(src[sb * MXU_DIM:(sb + 1) * MXU_DIM, :])
            carry_r[...] = jnp.zeros_like(carry_r)
            carry_i[...] = jnp.zeros_like(carry_i)

        str_ref[0] = carry_r[...]
        sti_ref[0] = carry_i[...]
        _rows_to_segments(useg, [(ulo_ref, 0), (uhi_ref, w // 2)], steps, stage)
        for sb in range(nsb):
            us = slice(sb * MXU_DIM, (sb + 1) * MXU_DIM)
            ss = slice(sb * SB_STATE, (sb + 1) * SB_STATE)
            ub = useg[:, us].astype(BF16)
            bur = _dot(ub, bre_ref[sb], NN)
            bui = _dot(ub, bim_ref[sb], NN)
            xr, xi = _cmul(cfr_ref[:, ss], cfi_ref[:, ss], bur, bui)
            sr[...] = xr
            si[...] = xi
            _scan_segments(sr, si, abr_ref[...], abi_ref[...], apr_ref[...], api_ref[...], pwr_ref, pwi_ref,
                           carry_r, carry_i, cm_r, cm_i, steps, False, sb * SB_STATE)
            y = _dot(sr[...].astype(BF16), cre_ref[sb], NT) - _dot(si[...].astype(BF16), cim_ref[sb], NT)
            yseg[:, us] = y + d_ref[:, us] * useg[:, us]
        _segments_to_rows(y_ref, yseg, steps, stage)
        yg_ref[...] = _gelu(y_ref[...]).astype(BF16)

    const = lambda a: pl.BlockSpec(a.shape, lambda c: (0,) * a.ndim)
    row_n = pl.BlockSpec((1, n_state), lambda c: (0, 0))
    st = pl.BlockSpec((1, 1, n_state), lambda c: (c, 0, 0))
    held = [pltpu.VMEM((nsb, MXU_DIM, SB_STATE), BF16)] * 4
    vmem = (4 * _nbytes((nsb, MXU_DIM, SB_STATE), BF16) + 4 * _nbytes((chunk, SB_STATE), F32)
            + 12 * _nbytes((chunk, w), F32) + 8 * _nbytes(bc_rows[0].shape, F32))
    res, landed = _call(
        body, [proj, proj, *bc_rows, *rows_p, d_row], name=name,
        out_shape=[jax.ShapeDtypeStruct((rows, w), F32), jax.ShapeDtypeStruct((nc, 1, n_state), F32),
                   jax.ShapeDtypeStruct((nc, 1, n_state), F32), jax.ShapeDtypeStruct((rows, w), BF16)],
        grid=(nc,),
        in_specs=_u_specs(w, o_u, chunk, lambda c: c) + [const(b) for b in bc_rows]
        + [row_n] * 6 + [pl.BlockSpec((steps, n_state), lambda c: (0, 0))] * 2 + [pl.BlockSpec((1, w), lambda c: (0, 0))],
        out_specs=[pl.BlockSpec((chunk, w), lambda c: (c, 0)), st, st, pl.BlockSpec((chunk, w), lambda c: (c, 0))],
        scratch_shapes=held + [pltpu.VMEM((chunk, w), F32), pltpu.VMEM((chunk, w), F32),
                               pltpu.VMEM((w // LANES, chunk, LANES), F32),
                               pltpu.VMEM((chunk, SB_STATE), F32), pltpu.VMEM((chunk, SB_STATE), F32),
                               pltpu.VMEM((1, n_state), F32), pltpu.VMEM((1, n_state), F32),
                               pltpu.VMEM((SUBLANES, n_state), F32), pltpu.VMEM((SUBLANES, n_state), F32)],
        semantics=("arbitrary",), vmem=vmem, rider=rider)
    return res if rider is None else (res, landed)


def _ssm_bwd(proj, o_u, y, dyg, st_re, st_im, bc_rows, rows_p, d_row, *, chunk, name, rider=None):
    rows = proj.shape[0]
    w = d_row.shape[1]
    nc = rows // chunk
    steps = chunk // SUBLANES
    nsb = w // MXU_DIM
    n_state = nsb * SB_STATE

    def body(ulo_ref, uhi_ref, y_ref, dyg_ref, str_ref, sti_ref, b2r_ref, b2i_ref, c2r_ref, c2i_ref, t2r_ref,
             t2i_ref, abr_ref, abi_ref, cfr_ref, cfi_ref, apr_ref, api_ref, pwr_ref, pwi_ref, d_ref,
             du_ref, gb2r_ref, gb2i_ref, gc2r_ref, gc2i_ref, gabr_ref, gabi_ref, dd_ref,
             bre_ref, bim_ref, cre_ref, cim_ref, btr_ref, bti_ref, dbre_ref, dbim_ref, dcre_ref, dcim_ref,
             useg, dyseg, dynat, stage, sr, si, lr, li, carry_r, carry_i, lam_r, lam_i, cm_r, cm_i, cl_r, cl_i):
        first = pl.program_id(0) == 0

        @pl.when(first)
        def _():
            for src, dst in ((b2r_ref, bre_ref), (b2i_ref, bim_ref), (c2r_ref, cre_ref), (c2i_ref, cim_ref),
                             (t2r_ref, btr_ref), (t2i_ref, bti_ref)):
                for sb in range(nsb):
                    dst[sb] = _block_diagonal(src[sb * MXU_DIM:(sb + 1) * MXU_DIM, :])
            lam_r[...] = jnp.zeros_like(lam_r)
            lam_i[...] = jnp.zeros_like(lam_i)
            for ref in (dbre_ref, dbim_ref, dcre_ref, dcim_ref, gabr_ref, gabi_ref, dd_ref):
                ref[...] = jnp.zeros_like(ref)

        dynat[...] = dyg_ref[...].astype(F32) * _dgelu(y_ref[...])
        half = w // 2
        dd_ref[:, :half] += jnp.sum(dynat[:, :half] * ulo_ref[...].astype(F32), axis=0, keepdims=True)
        dd_ref[:, half:] += jnp.sum(dynat[:, half:] * uhi_ref[...].astype(F32), axis=0, keepdims=True)
        _rows_to_segments(useg, [(ulo_ref, 0), (uhi_ref, half)], steps, stage)
        _rows_to_segments(dyseg, [(dynat, 0)], steps, stage)
        dy = dyseg[...]
        dyb = dy.astype(BF16)
        ub = useg[...].astype(BF16)
        carry_r[...] = str_ref[0]
        carry_i[...] = sti_ref[0]
        abr, abi = abr_ref[...], abi_ref[...]
        apr, api = apr_ref[...], api_ref[...]
        for sb in range(nsb):
            us = slice(sb * MXU_DIM, (sb + 1) * MXU_DIM)
            ss = slice(sb * SB_STATE, (sb + 1) * SB_STATE)
            base = sb * SB_STATE
            br = _dot(ub[:, us], bre_ref[sb], NN)
            bi = _dot(ub[:, us], bim_ref[sb], NN)
            xr, xi = _cmul(cfr_ref[:, ss], cfi_ref[:, ss], br, bi)
            sr[...] = xr
            si[...] = xi
            lr[...] = _dot(dyb[:, us], cre_ref[sb], NN)
            li[...] = -_dot(dyb[:, us], cim_ref[sb], NN)
            _scan_segments(sr, si, abr, abi, apr, api, pwr_ref, pwi_ref, carry_r, carry_i, cm_r, cm_i, steps, False,
                           base)
            dcre_ref[sb] += _dot(dyb[:, us], sr[...].astype(BF16), TN)
            dcim_ref[sb] -= _dot(dyb[:, us], si[...].astype(BF16), TN)
            _scan_segments(lr, li, abr, -abi, apr, -api, pwr_ref, pwi_ref, lam_r, lam_i, cl_r, cl_i, steps, True,
                           base)
            for c0 in range(0, SB_STATE, SCAN_LANES):
                ls = slice(c0, c0 + SCAN_LANES)
                gs = slice(base + c0, base + c0 + SCAN_LANES)

                def step(j, acc, ls=ls):
                    gar, gai, pr, pi = acc
                    r0 = pl.multiple_of(j * SUBLANES, SUBLANES)
                    rws = pl.ds(r0, SUBLANES)
                    t_r, t_i = _cmul(pr, -pi, lr[rws, ls], li[rws, ls])
                    return gar + t_r, gai + t_i, sr[rws, ls], si[rws, ls]

                zero = jnp.zeros((SUBLANES, SCAN_LANES), F32)
                gar, gai, _, _ = lax.fori_loop(0, steps, step, (zero, zero, cm_r[:, gs], cm_i[:, gs]))
                gabr_ref[:, gs] += gar
                gabi_ref[:, gs] += gai
            xr, xi = lr[...].astype(BF16), li[...].astype(BF16)
            du = _dot(xr, btr_ref[sb], NT) + _dot(xi, bti_ref[sb], NT)
            useg[:, us] = du + d_ref[:, us] * dy[:, us]
            dbre_ref[sb] += _dot(ub[:, us], xr, TN)
            dbim_ref[sb] += _dot(ub[:, us], xi, TN)
        _segments_to_rows(du_ref, useg, steps, stage)

        @pl.when(pl.program_id(0) == nc - 1)
        def _():
            for src, dst in ((dbre_ref, gb2r_ref), (dbim_ref, gb2i_ref), (dcre_ref, gc2r_ref), (dcim_ref, gc2i_ref)):
                for sb in range(nsb):
                    dst[sb * MXU_DIM:(sb + 1) * MXU_DIM, :] = _block_rows(src[sb])

    rev = lambda c: nc - 1 - c
    const = lambda a: pl.BlockSpec(a.shape, lambda c: (0,) * a.ndim)
    tile = pl.BlockSpec((chunk, w), lambda c: (rev(c), 0))
    row_n = pl.BlockSpec((1, n_state), lambda c: (0, 0))
    row_w = pl.BlockSpec((1, w), lambda c: (0, 0))
    st = pl.BlockSpec((1, 1, n_state), lambda c: (rev(c), 0, 0))
    acc8 = pl.BlockSpec((SUBLANES, n_state), lambda c: (0, 0))
    big = pltpu.VMEM((chunk, SB_STATE), F32)
    small = pltpu.VMEM((chunk, w), F32)
    row = pltpu.VMEM((1, n_state), F32)
    eight = pltpu.VMEM((SUBLANES, n_state), F32)
    blk = (nsb, MXU_DIM, SB_STATE)
    held = [pltpu.VMEM(blk, BF16)] * 6 + [pltpu.VMEM(blk, F32)] * 4
    vmem = (6 * _nbytes(blk, BF16) + 4 * _nbytes(blk, F32) + 5 * _nbytes((chunk, SB_STATE), F32)
            + 12 * _nbytes((chunk, w), F32) + 20 * _nbytes(bc_rows[0].shape, F32))
    res, landed = _call(
        body, [proj, proj, y, dyg, st_re, st_im, *bc_rows, *rows_p, d_row], name=name,
        out_shape=[jax.ShapeDtypeStruct((rows, w), F32)] + [jax.ShapeDtypeStruct(b.shape, F32) for b in bc_rows[:4]]
        + [jax.ShapeDtypeStruct((SUBLANES, n_state), F32)] * 2 + [jax.ShapeDtypeStruct((1, w), F32)],
        grid=(nc,),
        in_specs=_u_specs(w, o_u, chunk, rev) + [tile, tile, st, st] + [const(b) for b in bc_rows]
        + [row_n] * 6 + [pl.BlockSpec((steps, n_state), lambda c: (0, 0))] * 2 + [row_w],
        out_specs=[tile] + [const(b) for b in bc_rows[:4]] + [acc8] * 2 + [row_w],
        scratch_shapes=held + [small] * 3 + [pltpu.VMEM((w // LANES, chunk, LANES), F32)] + [big] * 4 + [row] * 4
        + [eight] * 4,
        semantics=("arbitrary",), vmem=vmem, rider=rider)
    return res if rider is None else (res, landed)


def _out_proj_loss(merged, w_o, x, target, name):
    rows, d = x.shape
    tm, tn = _tile(rows, 1024, SUBLANES), _tile(d, 1024)

    def body(a_ref, b_ref, x_ref, t_ref, g_ref, gb_ref, l_ref):
        err = x_ref[...] + _dot(a_ref[...], b_ref[...], NN) - t_ref[...]
        g = err * (1.0 / d)
        g_ref[...] = g
        gb_ref[...] = g.astype(BF16)
        part = jnp.sum(0.5 * err * g, axis=0, keepdims=True)
        first = pl.program_id(1) == 0

        @pl.when(first)
        def _():
            l_ref[...] = part

        @pl.when(jnp.logical_not(first))
        def _():
            l_ref[...] += part

    tile = pl.BlockSpec((tm, tn), lambda j, i: (i, j))
    vmem = 2 * (_nbytes((tm, d), BF16) + _nbytes((d, tn), BF16)) + 12 * _nbytes((tm, tn), F32)
    return _pallas(
        body, name=name,
        out_shape=[jax.ShapeDtypeStruct((rows, d), F32), jax.ShapeDtypeStruct((rows, d), BF16),
                   jax.ShapeDtypeStruct((1, d), F32)],
        grid=(d // tn, rows // tm),
        in_specs=[pl.BlockSpec((tm, d), lambda j, i: (i, 0)), pl.BlockSpec((d, tn), lambda j, i: (0, j)), tile, tile],
        out_specs=[tile, tile, pl.BlockSpec((1, tn), lambda j, i: (0, j))],
        compiler_params=_params(("parallel", "arbitrary"), vmem),
    )(merged, w_o, x, target)


def _pair_sum(grad, recv, name):
    r4, cdim = recv.shape
    r = r4 // N_CHIPS
    tr = _tile(r, 544, 16)
    g4 = grad.reshape(N_CHIPS, 2, r, cdim)
    r3 = recv.reshape(N_CHIPS, r, cdim)
    core = jnp.reshape(lax.axis_index("c"), (1,)).astype(jnp.int32)

    def body(c_ref, g_ref, r_ref, o_ref):
        o_ref[...] = (g_ref[0] + r_ref[...]).astype(BF16)

    out = _pallas(
        body, name=name, out_shape=jax.ShapeDtypeStruct((N_CHIPS, r, cdim), BF16),
        grid_spec=pltpu.PrefetchScalarGridSpec(
            num_scalar_prefetch=1, grid=(N_CHIPS, r // tr),
            in_specs=[pl.BlockSpec((1, 1, tr, cdim), lambda j, i, c: (j, c[0], i, 0)),
                      pl.BlockSpec((1, tr, cdim), lambda j, i, c: (j, i, 0))],
            out_specs=pl.BlockSpec((1, tr, cdim), lambda j, i, c: (j, i, 0))),
        compiler_params=_params(("parallel", "parallel"), 6 * _nbytes((tr, cdim), F32)),
    )(core, g4, r3)
    return out.reshape(r4, cdim)


def _chip_sum(recv, name):
    r4, cdim = recv.shape
    r = r4 // N_CHIPS
    tr = _tile(r, 544, 16)
    r3 = recv.reshape(N_CHIPS, r, cdim)

    def body(r_ref, o_ref):
        acc = r_ref[0].astype(F32)
        for j in range(1, N_CHIPS):
            acc = acc + r_ref[j].astype(F32)
        o_ref[...] = acc

    return _pallas(
        body, name=name, out_shape=jax.ShapeDtypeStruct((r, cdim), F32), grid=(r // tr,),
        in_specs=[pl.BlockSpec((N_CHIPS, tr, cdim), lambda i: (0, i, 0))],
        out_specs=pl.BlockSpec((tr, cdim), lambda i: (i, 0)),
        compiler_params=_params(("parallel",), 8 * _nbytes((tr, cdim), F32)),
    )(r3)


def _adamw_math(w, g, m, v):
    m = ADAM_B1 * m + (1.0 - ADAM_B1) * g
    v = ADAM_B2 * v + (1.0 - ADAM_B2) * (g * g)
    m_hat = m / (1.0 - ADAM_B1 ** ADAM_STEP)
    v_hat = v / (1.0 - ADAM_B2 ** ADAM_STEP)
    delta = -ADAM_LR * (m_hat / (jnp.sqrt(v_hat) + ADAM_EPS) + ADAM_WD * w)
    return delta, m, v


def _adamw(w, g, m, v, name):
    rows, cols = w.shape
    tr = _tile(rows, 256, SUBLANES)

    def body(w_ref, g_ref, m_ref, v_ref, d_ref, nm_ref, nv_ref):
        d, nm, nv = _adamw_math(w_ref[...], g_ref[...], m_ref[...], v_ref[...])
        d_ref[...] = d
        nm_ref[...] = nm
        nv_ref[...] = nv

    spec = pl.BlockSpec((tr, cols), lambda i: (i, 0))
    shp = jax.ShapeDtypeStruct((rows, cols), F32)
    return _pallas(
        body, name=name, out_shape=[shp] * 3, grid=(rows // tr,), in_specs=[spec] * 4, out_specs=[spec] * 3,
        compiler_params=_params(("parallel",)),
    )(w, g, m, v)


def _adamw_chips(w, parts, m, v, name):
    rows, cols = w.shape
    assert sum(p.shape[1] for p in parts) == cols
    tr = _tile(rows, 64, 16)
    n = len(parts)

    def body(*refs):
        w_ref, m_ref, v_ref = refs[0], refs[1 + n], refs[2 + n]
        g_ref, d_ref, nm_ref, nv_ref = refs[3 + n:]
        cols_g = []
        for p_ref in refs[1:1 + n]:
            acc = p_ref[0].astype(F32)
            for j in range(1, N_CHIPS):
                acc = acc + p_ref[j].astype(F32)
            cols_g.append(acc)
        g = cols_g[0] if n == 1 else jnp.concatenate(cols_g, axis=1)
        d, nm, nv = _adamw_math(w_ref[...], g, m_ref[...], v_ref[...])
        g_ref[...] = g
        d_ref[...] = d
        nm_ref[...] = nm
        nv_ref[...] = nv

    spec = pl.BlockSpec((tr, cols), lambda i: (i, 0))
    part_specs = [pl.BlockSpec((N_CHIPS, tr, p.shape[1]), lambda i: (0, i, 0)) for p in parts]
    shp = jax.ShapeDtypeStruct((rows, cols), F32)
    return _pallas(
        body, name=name, out_shape=[shp] * 4, grid=(rows // tr,),
        in_specs=[spec] + part_specs + [spec, spec], out_specs=[spec] * 4,
        compiler_params=_params(("parallel",)),
    )(w, *[p.reshape(N_CHIPS, rows, p.shape[1]) for p in parts], m, v)


def _adamw_small(w, parts, m, v, name):
    rows, cols = w.shape
    p3 = parts.reshape(N_DEV, rows, cols)

    def body(w_ref, p_ref, m_ref, v_ref, g_ref, d_ref, nm_ref, nv_ref):
        g = p_ref[0]
        for k in range(1, N_DEV):
            g = g + p_ref[k]
        d, nm, nv = _adamw_math(w_ref[...], g, m_ref[...], v_ref[...])
        g_ref[...] = g
        d_ref[...] = d
        nm_ref[...] = nm
        nv_ref[...] = nv

    shp = jax.ShapeDtypeStruct((rows, cols), F32)
    return _pallas(body, name=name, out_shape=[shp] * 4)(w, p3, m, v)


SMALL = ("norm_w", "q_norm_w", "k_norm_w", "sinks", "A_re", "A_im", "log_dt", "B_re", "B_im", "C_re", "C_im",
         "D_skip", "b_glu")
LARGE = ("w_in", "w_attn_proj", "w_glu", "w_ssm_proj", "w_out")
ORDER = ("norm_w", "w_in", "q_norm_w", "k_norm_w", "sinks", "w_attn_proj", "A_re", "A_im", "log_dt", "B_re", "B_im",
         "C_re", "C_im", "D_skip", "w_glu", "b_glu", "w_ssm_proj", "w_out")


SMALL_REST = ("loss",) + SMALL[1:]


def _pack(named, keys):
    flat = jnp.concatenate([named[k].reshape(-1).astype(F32) for k in keys])
    n = flat.shape[0]
    rows = -(-n // (LANES * SUBLANES)) * SUBLANES
    return jnp.pad(flat, (0, rows * LANES - n)).reshape(rows, LANES)


def _unpack(packed, like, keys):
    flat = packed.reshape(-1)
    out, o = {}, 0
    for k in keys:
        n = like[k].size
        out[k] = flat[o:o + n].reshape(like[k].shape)
        o += n
    return out


def _step(xs, target, p, shards):
    s_in, s_ap, s_glu, s_sp, s_o = shards
    seq, d = xs.shape
    attn_w = (d // 128) * HEAD_DIM
    n_q = attn_w // HEAD_DIM
    kv_w = N_KV_HEADS * HEAD_DIM
    ssm_w = d // 2
    n_groups = ssm_w // GROUP
    n_state = n_groups * STATE
    in_w = N_DEV * s_in.shape[0]
    assert in_w == 2 * attn_w + 2 * kv_w + 2 * ssm_w + 2 * d
    o_u = 2 * attn_w + 2 * kv_w
    o_z = o_u + ssm_w
    o_ga = o_z + ssm_w
    chunk = min(SSM_CHUNK, seq)
    cw = d // 4

    norm_row = p["norm_w"].reshape(1, d)
    half = d // W_IN_PARTS
    assert W_IN_PARTS == 2
    s_in_parts = [s_in[:, :half], s_in[:, half:]]
    h, (w_lo,) = _rmsnorm_fwd(xs, norm_row, "rmsnorm_fwd", rider=_all_gather(s_in_parts[:1]))
    part, (w_hi,) = _matmul(Cols(h, 0, half), w_lo, mode="nt", name="in_proj_0", tn=2176, out_dtype=BF16,
                            rider=_all_gather(s_in_parts[1:]))
    proj = _matmul(Cols(h, half, half), w_hi, mode="nt", name="in_proj_1", tn=2176, out_dtype=BF16, add=part)
    w_in_parts = [w_lo, w_hi]
    qw_row = jnp.tile(p["q_norm_w"], n_q).reshape(1, attn_w)
    kw_row = jnp.tile(p["k_norm_w"], N_KV_HEADS).reshape(1, kv_w)
    gmat = _head_mean_matrix()
    ag = _attention_fwd(proj, qw_row, kw_row, gmat, p["sinks"], attn_w=attn_w, kv_w=kv_w, name="attention_fwd")

    log_dt_col = p["log_dt"].reshape(n_groups, 1)
    prep = _ssm_prep(p["A_re"], p["A_im"], log_dt_col, chunk // SUBLANES, "ssm_prep")
    rows_p = [v.reshape(1, n_state) for v in prep[:6]] + [v.reshape(-1, n_state) for v in prep[6:]]
    bt_re, bt_im = p["B_re"].transpose(0, 2, 1), p["B_im"].transpose(0, 2, 1)
    cf_re, cf_im = prep[2][:, None, :], prep[3][:, None, :]
    bc_rows = [_ssm_rows(m) for m in (bt_re, bt_im, p["C_re"], p["C_im"],
                                      cf_re * bt_re - cf_im * bt_im, cf_re * bt_im + cf_im * bt_re)]
    d_row = p["D_skip"].reshape(1, ssm_w)
    (y_ssm, st_re, st_im, yg), (w_ap_t, w_glu_t, w_sp_t, w_o) = _ssm_fwd(
        proj, o_u, bc_rows[:4], rows_p, d_row, chunk=chunk, name="ssm_fwd",
        rider=_all_gather([s_ap, s_glu, s_sp, s_o]))
    glu = _matmul(yg, w_glu_t, mode="nt", name="glu_proj", out_dtype=BF16, bias=p["b_glu"].reshape(1, 2 * ssm_w))
    (ts,) = _ew(lambda ga, gb, z: ga * _sigmoid(gb) * _silu(z), name="glu_gate", rows=seq, width=ssm_w,
                tiles=[(glu, 0), (glu, ssm_w), (proj, o_z)], outs=[(BF16, ssm_w, 0)], cw=cw)
    yy = _matmul(ag, w_ap_t, mode="nt", name="attn_proj", out_dtype=BF16, out_cols=(2 * d, 0))
    yy = _matmul(ts, w_sp_t, mode="nt", name="ssm_proj", out_dtype=BF16, out_cols=(2 * d, d), into=yy)
    (merged,) = _ew(lambda ya, ys, ga, gs: _sigmoid(ga) * ya + _sigmoid(gs) * ys, name="merge", rows=seq, width=d,
                    tiles=[(yy, 0), (yy, d), (proj, o_ga), (proj, o_ga + d)], outs=[(BF16, d, 0)], cw=cw)
    dout, dout_b, loss_cols = _out_proj_loss(merged, w_o, xs, target, "out_proj_loss")
    loss_local = jnp.sum(loss_cols)

    g_w_o = _matmul(merged, dout_b, mode="tn", name="grad_w_out", tm=512, tk=4096)
    dmerged, (sib_o,) = _matmul(dout_b, w_o, mode="nt", name="d_merged", out_dtype=BF16,
                                rider=_sibling_exchange([g_w_o]))
    pair_o = _pair_sum(g_w_o, sib_o, "pair_sum_w_out")

    def merge_bwd(dm, y, g):
        s = _sigmoid(g)
        return dm * s, dm * y * s * (1.0 - s)

    dyy, dproj = _ew(merge_bwd, name="merge_bwd", rows=seq, width=2 * d,
                     tiles=[(dmerged, 0, d), (yy, 0), (proj, o_ga)],
                     outs=[(BF16, 2 * d, 0), (BF16, in_w, o_ga)], cw=cw)
    dy_a, dy_s = Cols(dyy, 0, d), Cols(dyy, d, d)
    g_w_ap_t = _matmul(dy_a, ag, mode="tn", name="grad_w_attn_proj", tm=512, tk=4096)
    g_w_sp_t = _matmul(dy_s, ts, mode="tn", name="grad_w_ssm_proj", tm=512, tk=4096)
    d_ag = _matmul(dy_a, w_ap_t, mode="nn", name="d_attn_gated", out_dtype=BF16)
    d_ts = _matmul(dy_s, w_sp_t, mode="nn", name="d_ssm_gated", out_dtype=BF16)

    (dproj, dkv, g_qw, g_kw, g_sinks), (chips_o, sib_ap, sib_sp) = _attention_bwd(
        proj, d_ag, dproj, qw_row, kw_row, gmat, p["sinks"], attn_w=attn_w, kv_w=kv_w, name="attention_bwd",
        rider=_join(_chip_exchange([pair_o]), _sibling_exchange([g_w_ap_t, g_w_sp_t])))
    pair_ap = _pair_sum(g_w_ap_t, sib_ap, "pair_sum_w_attn_proj")
    pair_sp = _pair_sum(g_w_sp_t, sib_sp, "pair_sum_w_ssm_proj")
    dproj = _attention_dkv(dproj, dkv, attn_w=attn_w, kv_w=kv_w, name="attention_dkv")

    n_half = ssm_w // _tile(2 * ssm_w, cw)

    def glu_bwd(j, dt, ga, gb, z):
        sb, sz = _sigmoid(gb), _silu(z)
        dg = jnp.where(j < n_half, dt * sb * sz, dt * ga * sb * (1.0 - sb) * sz)
        return dg, dg

    glu_ops = [(d_ts, 0, ssm_w), (glu, 0, ssm_w), (glu, ssm_w, ssm_w), (proj, o_z, ssm_w)]
    dglu, g_bglu = _ew(glu_bwd, name="glu_bwd", rows=seq, width=2 * ssm_w, tiles=glu_ops,
                       outs=[(BF16, 2 * ssm_w, 0)], accs=1, cw=cw, with_col=True)
    (dproj,) = _ew(lambda dt, ga, gb, z: dt * ga * _sigmoid(gb) * _dsilu(z), name="glu_bwd_z", rows=seq,
                   width=ssm_w, tiles=glu_ops, outs=[(BF16, in_w, o_z)], into=[dproj], cw=cw)
    g_w_glu_t = _matmul(dglu, yg, mode="tn", name="grad_w_glu", tm=512, tk=4096)
    d_yg = _matmul(dglu, w_glu_t, mode="nn", name="d_gelu", out_dtype=BF16)
    ((du, dbt_re, dbt_im, dc_re, dc_im, gabr, gabi, g_d), (chips_ap, chips_sp, sib_glu)) = _ssm_bwd(
        proj, o_u, y_ssm, d_yg, st_re, st_im, bc_rows, rows_p, d_row, chunk=chunk, name="ssm_bwd",
        rider=_join(_chip_exchange([pair_ap, pair_sp]), _sibling_exchange([g_w_glu_t])))
    pair_glu = _pair_sum(g_w_glu_t, sib_glu, "pair_sum_w_glu")
    (dproj,) = _ew(lambda v: v, name="du_store", rows=seq, width=ssm_w, tiles=[(du, 0)],
                   outs=[(BF16, in_w, o_u)], into=[dproj], cw=cw)
    g_a_re, g_a_im, g_log_dt, g_bt_re, g_bt_im = _ssm_param_bwd(
        p["A_re"], p["A_im"], log_dt_col, *[g.reshape(SUBLANES, n_groups, STATE) for g in (gabr, gabi)],
        bt_re, bt_im, _from_ssm_rows(dbt_re), _from_ssm_rows(dbt_im), "ssm_param_bwd")
    small_grads = dict(
        loss=loss_local, q_norm_w=g_qw.reshape(n_q, HEAD_DIM).sum(0), k_norm_w=g_kw.reshape(N_KV_HEADS, HEAD_DIM).sum(0),
        sinks=g_sinks[0, :n_q], A_re=g_a_re, A_im=g_a_im, log_dt=g_log_dt.reshape(n_groups),
        B_re=g_bt_re.transpose(0, 2, 1), B_im=g_bt_im.transpose(0, 2, 1),
        C_re=_from_ssm_rows(dc_re), C_im=_from_ssm_rows(dc_im),
        D_skip=g_d.reshape(n_groups, GROUP), b_glu=g_bglu.reshape(2 * ssm_w))

    n_parts = W_IN_PARTS
    wq = d // n_parts
    g_parts, pair_parts, chip_parts = [], [], []
    extra = [_chip_exchange([pair_glu]), _all_gather([_pack(small_grads, SMALL_REST)])]
    chips_glu = small_parts = dh = None
    for step in range(n_parts + 2):
        riders = list(extra) if step == 0 else []
        if 0 <= step - 2 < n_parts:
            riders.append(_chip_exchange([pair_parts[step - 2]]))
        if 0 <= step - 1 < n_parts:
            riders.append(_sibling_exchange([g_parts[step - 1]]))
        rider = _join(*riders) if riders else None
        if step < n_parts:
            res = _matmul(dproj, Cols(h, step * wq, wq), mode="tn", name="grad_w_in_%d" % step, tk=4096, rider=rider)
            out, landed = res if rider is not None else (res, [])
            g_parts.append(out)
        else:
            q = step - n_parts
            dh, landed = _matmul(dproj, w_in_parts[q], mode="nn", name="d_normed_%d" % q, tk=2176,
                                 out_cols=(d, q * wq), into=dh, rider=rider)
        landed = list(landed)
        if step == 0:
            chips_glu, small_parts = landed[:2]
            landed = landed[2:]
        if 0 <= step - 2 < n_parts:
            chip_parts.append(landed.pop(0))
        if 0 <= step - 1 < n_parts:
            pair_parts.append(_pair_sum(g_parts[step - 1], landed.pop(0), "pair_sum_w_in_%d" % (step - 1)))
    grad_x, g_norm = _rmsnorm_bwd(xs, norm_row, dh, dout, "rmsnorm_bwd")
    (norm_parts,) = _exchange(_all_gather([_pack(dict(norm_w=g_norm), ("norm_w",))]), "gather_norm_grad")
    from_chips = dict(zip(LARGE, (chip_parts, [chips_ap], [chips_glu], [chips_sp], [chips_o])))
    return grad_x, from_chips, small_parts, norm_parts


def kernel(x, norm_w, w_in, q_norm_w, k_norm_w, sinks, w_attn_proj, A_re, A_im, log_dt, B_re, B_im, C_re, C_im, D_skip, w_glu, b_glu, w_ssm_proj, w_out, loss_target, m_norm_w, m_w_in, m_q_norm_w, m_k_norm_w, m_sinks, m_w_attn_proj, m_A_re, m_A_im, m_log_dt, m_B_re, m_B_im, m_C_re, m_C_im, m_D_skip, m_w_glu, m_b_glu, m_w_ssm_proj, m_w_out, v_norm_w, v_w_in, v_q_norm_w, v_k_norm_w, v_sinks, v_w_attn_proj, v_A_re, v_A_im, v_log_dt, v_B_re, v_B_im, v_C_re, v_C_im, v_D_skip, v_w_glu, v_b_glu, v_w_ssm_proj, v_w_out):
    weights = dict(norm_w=norm_w, w_in=w_in, q_norm_w=q_norm_w, k_norm_w=k_norm_w, sinks=sinks,
                   w_attn_proj=w_attn_proj, A_re=A_re, A_im=A_im, log_dt=log_dt, B_re=B_re, B_im=B_im, C_re=C_re,
                   C_im=C_im, D_skip=D_skip, w_glu=w_glu, b_glu=b_glu, w_ssm_proj=w_ssm_proj, w_out=w_out)
    m_in = dict(norm_w=m_norm_w, w_in=m_w_in, q_norm_w=m_q_norm_w, k_norm_w=m_k_norm_w, sinks=m_sinks,
                w_attn_proj=m_w_attn_proj, A_re=m_A_re, A_im=m_A_im, log_dt=m_log_dt, B_re=m_B_re, B_im=m_B_im,
                C_re=m_C_re, C_im=m_C_im, D_skip=m_D_skip, w_glu=m_w_glu, b_glu=m_b_glu, w_ssm_proj=m_w_ssm_proj,
                w_out=m_w_out)
    v_in = dict(norm_w=v_norm_w, w_in=v_w_in, q_norm_w=v_q_norm_w, k_norm_w=v_k_norm_w, sinks=v_sinks,
                w_attn_proj=v_w_attn_proj, A_re=v_A_re, A_im=v_A_im, log_dt=v_log_dt, B_re=v_B_re, B_im=v_B_im,
                C_re=v_C_re, C_im=v_C_im, D_skip=v_D_skip, w_glu=v_w_glu, b_glu=v_b_glu, w_ssm_proj=v_w_ssm_proj,
                w_out=v_w_out)

    _, seq, d = x.shape
    column_sharded = LARGE[:4]
    as_rows = lambda k, a: a.T if k in column_sharded else a
    shards = [as_rows(k, weights[k]).astype(BF16) for k in LARGE]
    small = {k: weights[k] for k in SMALL}
    grad_x, from_chips, small_parts, norm_parts = _step(x.reshape(seq, d), loss_target.reshape(seq, d), small,
                                                        shards)

    grads, delta, new_m, new_v = {}, {}, {}, {}
    for k in LARGE:
        if k == "w_in":
            res = _adamw_chips(weights[k].T, from_chips[k], m_in[k].T, v_in[k].T, "adamw_" + k)
            grads[k], delta[k], new_m[k], new_v[k] = [a.T for a in res]
        elif k == "w_out":
            grads[k], delta[k], new_m[k], new_v[k] = _adamw_chips(weights[k], from_chips[k], m_in[k], v_in[k],
                                                                  "adamw_" + k)
        else:
            grads[k] = _chip_sum(from_chips[k][0], "chip_sum_" + k).T
            delta[k], new_m[k], new_v[k] = _adamw(weights[k], grads[k], m_in[k], v_in[k], "adamw_" + k)

    zero = jnp.zeros((), F32)
    for keys, parts in ((SMALL_REST, small_parts), (("norm_w",), norm_parts)):
        like = dict(small, loss=zero)
        packs = [_pack(dict(src, loss=zero), keys) for src in (weights, m_in, v_in)]
        res = _adamw_small(packs[0], parts, packs[1], packs[2], "adamw_small_%d" % len(keys))
        for dst, r in zip((grads, delta, new_m, new_v), res):
            dst.update(_unpack(r, like, keys))
    loss = grads["loss"]

    return (loss, grad_x.reshape(x.shape), *[grads[k] for k in ORDER], *[delta[k] for k in ORDER],
            *[new_m[k] for k in ORDER], *[new_v[k] for k in ORDER])
```

```python
import math
from typing import Callable, NamedTuple

import jax
import jax.numpy as jnp
import numpy as np
from jax import lax
from jax.experimental import pallas as pl
from jax.experimental.pallas import tpu as pltpu

F32 = jnp.float32
BF16 = jnp.bfloat16
MESH = pl.DeviceIdType.MESH

HEAD_DIM = 64
N_KV_HEADS = 4
GROUP = 16
STATE = 64
BLOCK = 128
NORM_EPS = 1e-6
N_DEV = 8
N_CHIPS = 4
LANES = 128
SUBLANES = 8
MXU_DIM = 256
VMEM_BYTES = 64 * 1024 * 1024
VMEM_CAP = VMEM_BYTES - 8 * 1024 * 1024

ADAM_LR = 0.001
ADAM_B1 = 0.9
ADAM_B2 = 0.999
ADAM_EPS = 1e-08
ADAM_WD = 0.01
ADAM_STEP = 10

GELU_C = math.sqrt(2.0 / math.pi)
GELU_K = 0.044715


def _tile(dim, pref, mult=LANES):
    if dim <= pref:
        return dim
    best = None
    for d in range(mult, pref + 1, mult):
        if dim % d == 0:
            best = d
    assert best is not None, (dim, pref, mult)
    return best


def _params(semantics=None, vmem=None):
    kw = {}
    if semantics is not None:
        kw["dimension_semantics"] = semantics
    if vmem is not None:
        kw["vmem_limit_bytes"] = int(min(VMEM_CAP, max(vmem, 32 * 1024 * 1024)))
    return pltpu.CompilerParams(**kw)


def _nbytes(shape, dtype):
    return math.prod(shape) * jnp.dtype(dtype).itemsize


def _sigmoid(x):
    return 1.0 / (1.0 + jnp.exp(-x))


def _silu(x):
    return x * _sigmoid(x)


def _dsilu(x):
    s = _sigmoid(x)
    return s * (1.0 + x * (1.0 - s))


def _gelu(x):
    return 0.5 * x * (1.0 + jnp.tanh(GELU_C * (x + GELU_K * x * x * x)))


def _dgelu(x):
    t = jnp.tanh(GELU_C * (x + GELU_K * x * x * x))
    return 0.5 * (1.0 + t) + 0.5 * x * (1.0 - t * t) * GELU_C * (1.0 + 3.0 * GELU_K * x * x)


def _dot(a, b, dims):
    return lax.dot_general(a, b, (dims, ((), ())), preferred_element_type=F32)


NN = ((1,), (0,))
NT = ((1,), (1,))
TN = ((0,), (0,))


def _any_spec():
    return pl.BlockSpec(memory_space=pl.ANY)


def _pallas(body, **kw):
    pin = lambda s: pltpu.HBM(s.shape, s.dtype) if isinstance(s, jax.ShapeDtypeStruct) else s
    out_shape = kw.pop("out_shape")
    out_shape = [pin(s) for s in out_shape] if isinstance(out_shape, (list, tuple)) else pin(out_shape)
    call = pl.pallas_call(body, out_shape=out_shape, **kw)

    def run(*operands):
        pinned = [pltpu.with_memory_space_constraint(o, pltpu.HBM) if jnp.issubdtype(o.dtype, jnp.floating) else o
                  for o in operands]
        return call(*pinned)

    return run


class Rider(NamedTuple):
    operands: tuple
    out_shapes: tuple
    sems: tuple
    start: Callable
    finish: Callable


def _all_gather(shards):
    n = len(shards)

    def copies(ins, outs, sems):
        send_sems, recv_sems, local_sems = sems
        x, y, c = lax.axis_index("x"), lax.axis_index("y"), lax.axis_index("c")
        me, sibling = (x, y, c), (x, y, 1 - c)
        chips = [(1 - x, y), (x, 1 - y), (1 - x, 1 - y)]
        relayed = (c * (1 - x) + (1 - c) * x, c * y + (1 - c) * (1 - y))
        relay_to = (c * x + (1 - c) * (1 - x), c * (1 - y) + (1 - c) * y)

        def rows(k, px, py, pc):
            r = shards[k].shape[0]
            return outs[k].at[pl.ds((4 * px + 2 * py + pc) * r, r), :]

        def copy(k, s, block, to, src=None):
            return pltpu.make_async_remote_copy(
                src_ref=rows(k, *block) if src is None else src, dst_ref=rows(k, *block),
                send_sem=send_sems.at[7 * k + s], recv_sem=recv_sems.at[7 * k + s],
                device_id=to, device_id_type=MESH)

        mine = [pltpu.make_async_copy(ins[k], rows(k, *me), local_sems.at[k]) for k in range(n)]
        first = []
        for k in range(n):
            first.append(copy(k, 0, me, sibling, src=ins[k]))
            first += [copy(k, 1 + j, me, (*chips[j], c), src=ins[k]) for j in range(2)]
        return me, sibling, chips, c, copy, mine, first, (*relayed, c), (*relay_to, c)

    def start(ins, outs, sems):
        *_, mine, first, _, _ = copies(ins, outs, sems)
        for cp in mine + first:
            cp.start()

    def finish(ins, outs, sems):
        me, sibling, chips, c, copy, mine, first, relayed, relay_to = copies(ins, outs, sems)
        relays = [copy(k, 3, relayed, relay_to) for k in range(n)]
        passed = []

        def land(j):
            for k in range(n):
                copy(k, 1 + j, (*chips[j], c), me).wait_recv()
                fwd = copy(k, 4 + j, (*chips[j], c), sibling)
                fwd.start()
                passed.append(fwd)

        land(0)
        land(1)
        for cp in relays:
            cp.start()
        land(2)
        for k in range(n):
            copy(k, 0, sibling, me).wait_recv()
            for j, chip in enumerate(chips):
                copy(k, 4 + j, (*chip, 1 - c), me).wait_recv()
        for cp in first + relays + passed:
            cp.wait_send()
        for cp in mine:
            cp.wait()

    return Rider(
        tuple(shards),
        tuple(jax.ShapeDtypeStruct((N_DEV * s.shape[0], s.shape[1]), s.dtype) for s in shards),
        (pltpu.SemaphoreType.DMA((7 * n,)), pltpu.SemaphoreType.DMA((7 * n,)), pltpu.SemaphoreType.DMA((n,))),
        start, finish)


def _sibling_exchange(grads):
    n = len(grads)

    def copies(ins, outs, sems):
        send_sems, recv_sems = sems
        x, y, c = lax.axis_index("x"), lax.axis_index("y"), lax.axis_index("c")
        out = []
        for k in range(n):
            r = grads[k].shape[0] // N_DEV
            for j in range(N_CHIPS):
                out.append(pltpu.make_async_remote_copy(
                    src_ref=ins[k].at[pl.ds((2 * j + 1 - c) * r, r), :],
                    dst_ref=outs[k].at[pl.ds(j * r, r), :],
                    send_sem=send_sems.at[N_CHIPS * k + j], recv_sem=recv_sems.at[N_CHIPS * k + j],
                    device_id=(x, y, 1 - c), device_id_type=MESH))
        return out

    def start(ins, outs, sems):
        for cp in copies(ins, outs, sems):
            cp.start()

    def finish(ins, outs, sems):
        for cp in copies(ins, outs, sems):
            cp.wait()

    return Rider(
        tuple(grads), tuple(jax.ShapeDtypeStruct((g.shape[0] // 2, g.shape[1]), g.dtype) for g in grads),
        (pltpu.SemaphoreType.DMA((N_CHIPS * n,)), pltpu.SemaphoreType.DMA((N_CHIPS * n,))), start, finish)


def _chip_exchange(parts):
    n = len(parts)

    def copies(ins, outs, sems):
        send_sems, recv_sems, local_sems = sems
        x, y, c = lax.axis_index("x"), lax.axis_index("y"), lax.axis_index("c")
        my_chip = 2 * x + y
        chips = [(1 - x, y), (x, 1 - y), (1 - x, 1 - y)]
        local, sent = [], []
        for k in range(n):
            r = parts[k].shape[0] // N_CHIPS
            mine = pl.ds(my_chip * r, r)
            local.append(pltpu.make_async_copy(ins[k].at[mine, :], outs[k].at[mine, :], local_sems.at[k]))
            for s, (px, py) in enumerate(chips):
                sent.append(pltpu.make_async_remote_copy(
                    src_ref=ins[k].at[pl.ds((2 * px + py) * r, r), :], dst_ref=outs[k].at[mine, :],
                    send_sem=send_sems.at[3 * k + s], recv_sem=recv_sems.at[3 * k + s],
                    device_id=(px, py, c), device_id_type=MESH))
        return local, sent

    def start(ins, outs, sems):
        local, sent = copies(ins, outs, sems)
        for cp in local + sent:
            cp.start()

    def finish(ins, outs, sems):
        local, sent = copies(ins, outs, sems)
        for cp in sent + local:
            cp.wait()

    return Rider(
        tuple(parts), tuple(jax.ShapeDtypeStruct(p.shape, p.dtype) for p in parts),
        (pltpu.SemaphoreType.DMA((3 * n,)), pltpu.SemaphoreType.DMA((3 * n,)), pltpu.SemaphoreType.DMA((n,))),
        start, finish)


def _join(*riders):
    cuts_in, cuts_out, cuts_sem = [0], [0], [0]
    for r in riders:
        cuts_in.append(cuts_in[-1] + len(r.operands))
        cuts_out.append(cuts_out[-1] + len(r.out_shapes))
        cuts_sem.append(cuts_sem[-1] + len(r.sems))

    def each(which):
        def run(ins, outs, sems):
            for i, r in enumerate(riders):
                getattr(r, which)(ins[cuts_in[i]:cuts_in[i + 1]], outs[cuts_out[i]:cuts_out[i + 1]],
                                  sems[cuts_sem[i]:cuts_sem[i + 1]])
        return run

    return Rider(sum((r.operands for r in riders), ()), sum((r.out_shapes for r in riders), ()),
                 sum((r.sems for r in riders), ()), each("start"), each("finish"))


def _call(body, operands, *, name, out_shape, grid, in_specs, out_specs, scratch_shapes=(), aliases=None,
          semantics=None, vmem=None, rider=None):
    operands, out_shape, scratch_shapes = list(operands), list(out_shape), list(scratch_shapes)
    in_specs, out_specs = list(in_specs), list(out_specs)
    if rider is None:
        res = _pallas(
            body, name=name, out_shape=out_shape, grid=grid, in_specs=in_specs, out_specs=out_specs,
            scratch_shapes=scratch_shapes, input_output_aliases=aliases or {},
            compiler_params=_params(semantics, vmem))(*operands)
        return list(res), []
    n_in, n_out, n_scr = len(operands), len(out_shape), len(scratch_shapes)
    ri, ro = len(rider.operands), len(rider.out_shapes)

    def carried(*refs):
        a, b = n_in, n_in + ri
        c, d = b + n_out, b + n_out + ro
        e = d + n_scr
        ids = [pl.program_id(k) for k in range(len(grid))]
        first = ids[0] == 0
        last = ids[0] == grid[0] - 1
        for k in range(1, len(grid)):
            first = jnp.logical_and(first, ids[k] == 0)
            last = jnp.logical_and(last, ids[k] == grid[k] - 1)

        @pl.when(first)
        def _():
            rider.start(refs[a:b], refs[c:d], refs[e:])

        body(*refs[:a], *refs[b:c], *refs[d:e])

        @pl.when(last)
        def _():
            rider.finish(refs[a:b], refs[c:d], refs[e:])

    res = _pallas(
        carried, name=name, out_shape=out_shape + list(rider.out_shapes), grid=grid,
        in_specs=in_specs + [_any_spec()] * ri, out_specs=out_specs + [_any_spec()] * ro,
        scratch_shapes=scratch_shapes + list(rider.sems), input_output_aliases=aliases or {},
        compiler_params=_params(("arbitrary",) * len(grid), vmem))(*operands, *rider.operands)
    return list(res[:n_out]), list(res[n_out:])


def _exchange(rider, name):
    ri, ro = len(rider.operands), len(rider.out_shapes)

    def body(*refs):
        rider.start(refs[:ri], refs[ri:ri + ro], refs[ri + ro:])
        rider.finish(refs[:ri], refs[ri:ri + ro], refs[ri + ro:])

    return _pallas(
        body, name=name, out_shape=list(rider.out_shapes), in_specs=[_any_spec()] * ri,
        out_specs=[_any_spec()] * ro, scratch_shapes=list(rider.sems))(*rider.operands)


class Cols(NamedTuple):
    arr: jax.Array
    off: int
    width: int


def _cols(a):
    return a if isinstance(a, Cols) else Cols(a, 0, a.shape[1])


def _matmul(a, b, *, mode, name, out_dtype=F32, tm=1024, tn=1024, tk=2048, bias=None, add=None, out_cols=None,
            into=None, rider=None):
    a, b = _cols(a), _cols(b)
    if mode == "nn":
        (m, k), (k2, n) = (a.arr.shape[0], a.width), (b.arr.shape[0], b.width)
    elif mode == "nt":
        (m, k), (n, k2) = (a.arr.shape[0], a.width), (b.arr.shape[0], b.width)
    else:
        (k, m), (k2, n) = (a.arr.shape[0], a.width), (b.arr.shape[0], b.width)
    assert k == k2, (a.arr.shape, b.arr.shape, mode)
    tm, tn, tk = _tile(m, tm), _tile(n, tn), _tile(k, tk)
    nk = k // tk
    dims = {"nn": NN, "nt": NT, "tn": TN}[mode]
    if mode == "tn":
        assert a.off % tm == 0
        a_spec = pl.BlockSpec((tk, tm), lambda i, j, kk, o=a.off // tm: (kk, i + o))
    else:
        assert a.off % tk == 0
        a_spec = pl.BlockSpec((tm, tk), lambda i, j, kk, o=a.off // tk: (i, kk + o))
    if mode == "nt":
        assert b.off % tk == 0
        b_spec = pl.BlockSpec((tn, tk), lambda i, j, kk, o=b.off // tk: (j, kk + o))
    else:
        assert b.off % tn == 0
        b_spec = pl.BlockSpec((tk, tn), lambda i, j, kk, o=b.off // tn: (kk, j + o))
    in_specs, operands = [a_spec, b_spec], [a.arr, b.arr]
    assert bias is None or add is None
    if bias is not None:
        in_specs.append(pl.BlockSpec((1, tn), lambda i, j, kk: (0, j)))
        operands.append(bias)
    if add is not None:
        assert add.shape == (m, n)
        in_specs.append(pl.BlockSpec((tm, tn), lambda i, j, kk: (i, j)))
        operands.append(add)
    total_w, o_off = out_cols if out_cols is not None else (n, 0)
    assert o_off % tn == 0
    aliases = {}
    if into is not None:
        assert into.shape == (m, total_w) and into.dtype == out_dtype
        in_specs.append(_any_spec())
        operands.append(into)
        aliases = {len(operands) - 1: 0}
    n_in = len(operands)

    def body(*refs):
        a_ref, b_ref = refs[0], refs[1]
        bias_ref = refs[2] if bias is not None or add is not None else None
        o_ref = refs[n_in]
        acc_ref = refs[-1] if nk > 1 else None
        part = _dot(a_ref[...].astype(BF16), b_ref[...].astype(BF16), dims)

        def finish(acc):
            if bias_ref is not None:
                acc = acc + bias_ref[...]
            o_ref[...] = acc.astype(out_dtype)

        if nk == 1:
            finish(part)
        else:
            kk = pl.program_id(2)

            @pl.when(kk == 0)
            def _():
                acc_ref[...] = part

            @pl.when(kk > 0)
            def _():
                acc_ref[...] += part

            @pl.when(kk == nk - 1)
            def _():
                finish(acc_ref[...])

    vmem = 2 * (_nbytes((tm, tk), a.arr.dtype) + _nbytes((tk, tn), b.arr.dtype) + _nbytes((tm, tn), out_dtype))
    vmem += 3 * _nbytes((tm, tn), F32)
    (out,), landed = _call(
        body, operands, name=name, out_shape=[jax.ShapeDtypeStruct((m, total_w), out_dtype)],
        grid=(m // tm, n // tn, nk), in_specs=in_specs,
        out_specs=[pl.BlockSpec((tm, tn), lambda i, j, kk, o=o_off // tn: (i, j + o))],
        scratch_shapes=[pltpu.VMEM((tm, tn), F32)] if nk > 1 else [], aliases=aliases,
        semantics=("parallel", "parallel", "arbitrary"), vmem=vmem, rider=rider)
    return out if rider is None else (out, landed)


def _ew(fn, *, name, rows, width, tiles, vecs=(), outs, accs=0, tl=1024, cw=512, into=None, with_col=False):
    tl, cw = _tile(rows, tl, SUBLANES), _tile(width, cw)
    ncol = width // cw
    nt_, nv = len(tiles), len(vecs)
    into = list(into) if into is not None else [None] * len(outs)
    aliased = [t for t in into if t is not None]

    def off(o):
        assert o % cw == 0, (name, o, cw)
        return o // cw

    in_specs, vmem = [], 0
    for t in tiles:
        arr, o = t[0], off(t[1])
        wrap = t[2] // cw if len(t) > 2 else ncol
        in_specs.append(pl.BlockSpec((tl, cw), lambda j, i, o=o, wrap=wrap: (i, o + j % wrap)))
        vmem += _nbytes((tl, cw), arr.dtype)
    in_specs += [pl.BlockSpec((1, cw), lambda j, i, o=off(o): (0, j + o)) for _, o in vecs]
    in_specs += [_any_spec() for _ in aliased]
    out_shape, out_specs, aliases = [], [], {}
    n_in = nt_ + nv
    for idx, ((dt, tw, o), tgt) in enumerate(zip(outs, into)):
        out_shape.append(jax.ShapeDtypeStruct((rows, tw), dt))
        out_specs.append(pl.BlockSpec((tl, cw), lambda j, i, o=off(o): (i, j + o)))
        vmem += _nbytes((tl, cw), dt)
        if tgt is not None:
            assert tgt.shape == (rows, tw) and tgt.dtype == dt, (name, tgt.shape, tgt.dtype)
            aliases[n_in + len(aliases)] = idx
    for _ in range(accs):
        out_shape.append(jax.ShapeDtypeStruct((1, width), F32))
        out_specs.append(pl.BlockSpec((1, cw), lambda j, i: (0, j)))
    n_out = len(outs)

    def body(*refs):
        vals = [r[...].astype(F32) for r in refs[:n_in]]
        out_refs = refs[n_in + len(aliased):]
        res = fn(pl.program_id(0), *vals) if with_col else fn(*vals)
        res = res if isinstance(res, (tuple, list)) else (res,)
        assert len(res) == n_out + accs, (name, len(res))
        for r, v in zip(out_refs[:n_out], res[:n_out]):
            r[...] = v.astype(r.dtype)
        first = pl.program_id(1) == 0
        for r, v in zip(out_refs[n_out:], res[n_out:]):
            s = jnp.sum(v, axis=0, keepdims=True)

            @pl.when(first)
            def _(r=r, s=s):
                r[...] = s

            @pl.when(jnp.logical_not(first))
            def _(r=r, s=s):
                r[...] += s

    return _pallas(
        body, name=name, out_shape=out_shape, grid=(ncol, rows // tl),
        in_specs=in_specs, out_specs=out_specs, input_output_aliases=aliases,
        compiler_params=_params(("parallel", "arbitrary"), 3 * vmem),
    )(*[t[0] for t in tiles], *[v for v, _ in vecs], *aliased)


def _rmsnorm_fwd(x, w_row, name, rider=None):
    rows, d = x.shape
    tl = _tile(rows, 512, SUBLANES)

    def body(x_ref, w_ref, h_ref):
        xv = x_ref[...]
        rstd = lax.rsqrt(jnp.mean(xv * xv, axis=-1, keepdims=True) + NORM_EPS)
        h_ref[...] = (xv * rstd * w_ref[...]).astype(BF16)

    (h,), landed = _call(
        body, [x, w_row], name=name, out_shape=[jax.ShapeDtypeStruct((rows, d), BF16)], grid=(rows // tl,),
        in_specs=[pl.BlockSpec((tl, d), lambda i: (i, 0)), pl.BlockSpec((1, d), lambda i: (0, 0))],
        out_specs=[pl.BlockSpec((tl, d), lambda i: (i, 0))], semantics=("parallel",), rider=rider)
    return h if rider is None else (h, landed)


def _rmsnorm_bwd(x, w_row, dh, dout, name, rider=None):
    rows, d = x.shape
    tl = _tile(rows, 256, SUBLANES)

    def body(x_ref, w_ref, dh_ref, dout_ref, gx_ref, gw_ref):
        xv = x_ref[...]
        rstd = lax.rsqrt(jnp.mean(xv * xv, axis=-1, keepdims=True) + NORM_EPS)
        xn = xv * rstd
        dhv = dh_ref[...]
        dxn = dhv * w_ref[...]
        dx = rstd * (dxn - xn * jnp.mean(dxn * xn, axis=-1, keepdims=True))
        gx_ref[...] = dout_ref[...] + dx
        gw = jnp.sum(dhv * xn, axis=0, keepdims=True)

        @pl.when(pl.program_id(0) == 0)
        def _():
            gw_ref[...] = gw

        @pl.when(pl.program_id(0) > 0)
        def _():
            gw_ref[...] += gw

    tile = pl.BlockSpec((tl, d), lambda i: (i, 0))
    row = pl.BlockSpec((1, d), lambda i: (0, 0))
    res, landed = _call(
        body, [x, w_row, dh, dout], name=name,
        out_shape=[jax.ShapeDtypeStruct((rows, d), F32), jax.ShapeDtypeStruct((1, d), F32)],
        grid=(rows // tl,), in_specs=[tile, row, tile, tile], out_specs=[tile, row],
        semantics=("arbitrary",), rider=rider)
    return res if rider is None else (res, landed)


def _head_mean(x, gmat):
    hi = x.astype(BF16)
    lo = (x - hi.astype(F32)).astype(BF16)
    out = []
    for s in range(x.shape[1] // MXU_DIM):
        sl = slice(s * MXU_DIM, (s + 1) * MXU_DIM)
        out.append(_dot(hi[:, sl], gmat, NN) + _dot(lo[:, sl], gmat, NN))
    return out[0] if len(out) == 1 else jnp.concatenate(out, axis=1)


def _head_mean_matrix():
    blk = jnp.arange(MXU_DIM) // HEAD_DIM
    return jnp.where(blk[:, None] == blk[None, :], 1.0 / HEAD_DIM, 0.0).astype(BF16)


def _spread_head(x, g, width):
    col = x[:, (g // 2) * LANES:(g // 2 + 1) * LANES]
    other = pltpu.roll(col, HEAD_DIM, axis=1)
    low = lax.broadcasted_iota(jnp.int32, col.shape, 1) < HEAD_DIM
    both = jnp.where(low, col, other) if g % 2 == 0 else jnp.where(low, other, col)
    return both if width == LANES else jnp.concatenate([both] * (width // LANES), axis=1)


def _head_diagonal(t, per_kv):
    head = lax.broadcasted_iota(jnp.int32, t.shape, 1) // HEAD_DIM
    zero = jnp.zeros_like(t)
    return jnp.concatenate([jnp.where(head == r, t, zero) for r in range(per_kv)], axis=0)


def _fold_heads(x, per_kv):
    rows = x.shape[0] // per_kv
    head = lax.broadcasted_iota(jnp.int32, (rows, x.shape[1]), 1) // HEAD_DIM
    acc = jnp.where(head == 0, x[0:rows], 0.0)
    for r in range(1, per_kv):
        acc = acc + jnp.where(head == r, x[r * rows:(r + 1) * rows], 0.0)
    while acc.shape[1] > LANES:
        half = acc.shape[1] // 2
        acc = acc[:, :half] + acc[:, half:]
    return acc + pltpu.roll(acc, HEAD_DIM, axis=1)


def _join_heads(parts):
    low = lax.broadcasted_iota(jnp.int32, parts[0].shape, 1) < HEAD_DIM
    cols = [jnp.where(low, parts[2 * j], parts[2 * j + 1]) for j in range(len(parts) // 2)]
    return cols[0] if len(cols) == 1 else jnp.concatenate(cols, axis=1)


def _attn_specs(attn_w, kv_w):
    half = attn_w // 2
    kcol, vcol = attn_w // kv_w, attn_w // kv_w + 1
    gcol = (attn_w + 2 * kv_w) // half
    prev = lambda i: jnp.maximum(i - 1, 0)
    return [
        pl.BlockSpec((BLOCK, attn_w), lambda i: (i, 0)),
        pl.BlockSpec((BLOCK, kv_w), lambda i: (prev(i), kcol)),
        pl.BlockSpec((BLOCK, kv_w), lambda i: (i, kcol)),
        pl.BlockSpec((BLOCK, kv_w), lambda i: (prev(i), vcol)),
        pl.BlockSpec((BLOCK, kv_w), lambda i: (i, vcol)),
        pl.BlockSpec((BLOCK, half), lambda i: (i, gcol)),
        pl.BlockSpec((BLOCK, half), lambda i: (i, gcol + 1)),
    ]


def _band_mask(i):
    q_loc = lax.broadcasted_iota(jnp.int32, (BLOCK, 2 * BLOCK), 0) + BLOCK
    k_loc = lax.broadcasted_iota(jnp.int32, (BLOCK, 2 * BLOCK), 1)
    diff = q_loc - k_loc
    first_key = jnp.where(i == 0, BLOCK, 0)
    return (diff >= 0) & (diff < BLOCK) & (k_loc >= first_key)


def _softmax_with_sink(s, sink):
    m = jnp.maximum(jnp.max(s, axis=-1, keepdims=True), sink)
    p = jnp.exp(s - m)
    e_sink = jnp.exp(sink - m)
    den = jnp.sum(p, axis=-1, keepdims=True) + e_sink
    inv = 1.0 / den
    return p * inv, e_sink * inv


def _attn_block(i, q, kk, vv, qw, kw, gmat, sink_ref, per_kv):
    scale = 1.0 / math.sqrt(HEAD_DIM)
    keys = 2 * BLOCK
    valid = _band_mask(i)
    q_rstd = lax.rsqrt(_head_mean(q * q, gmat) + NORM_EPS)
    qn = q * q_rstd
    qh = (qn * qw).astype(BF16)
    k_rstd = lax.rsqrt(_head_mean(kk * kk, gmat) + NORM_EPS)
    kn = kk * k_rstd
    kh = kn * kw
    gw = per_kv * HEAD_DIM
    groups = []
    for g in range(N_KV_HEADS):
        kd = _head_diagonal(_spread_head(kh, g, gw).astype(BF16), per_kv)
        vd = _head_diagonal(_spread_head(vv, g, gw).astype(BF16), per_kv)
        qg = qh[:, g * gw:(g + 1) * gw]
        s_all = _dot(qg, kd, NT) * scale
        ps, p_sinks = [], []
        for r in range(per_kv):
            s = jnp.where(valid, s_all[:, r * keys:(r + 1) * keys], -1e30)
            p, p_sink = _softmax_with_sink(s, sink_ref[g * per_kv + r])
            ps.append(p)
            p_sinks.append(p_sink)
        pb = jnp.concatenate(ps, axis=1).astype(BF16)
        groups.append((kd, vd, qg, ps, p_sinks, pb, _dot(pb, vd, NN)))
    return qn, q_rstd, kn, k_rstd, groups


def _attention_fwd(proj, qw_row, kw_row, gmat, sinks, *, attn_w, kv_w, name):
    rows = proj.shape[0]
    per_kv = attn_w // HEAD_DIM // N_KV_HEADS

    def body(q_ref, kp_ref, kc_ref, vp_ref, vc_ref, glo_ref, ghi_ref, qw_ref, kw_ref, gm_ref, sink_ref, o_ref):
        kk = jnp.concatenate([kp_ref[...], kc_ref[...]], axis=0).astype(F32)
        vv = jnp.concatenate([vp_ref[...], vc_ref[...]], axis=0).astype(F32)
        gate = jnp.concatenate([glo_ref[...], ghi_ref[...]], axis=1).astype(F32)
        *_, groups = _attn_block(pl.program_id(0), q_ref[...].astype(F32), kk, vv, qw_ref[...], kw_ref[...], gm_ref[...],
                                 sink_ref, per_kv)
        attn = jnp.concatenate([grp[-1] for grp in groups], axis=1)
        o_ref[...] = (attn * _silu(gate)).astype(BF16)

    const = lambda a: pl.BlockSpec(a.shape, lambda i: (0, 0))
    return _pallas(
        body, name=name, out_shape=jax.ShapeDtypeStruct((rows, attn_w), BF16), grid=(rows // BLOCK,),
        in_specs=_attn_specs(attn_w, kv_w) + [const(qw_row), const(kw_row), const(gmat),
                                              pl.BlockSpec(memory_space=pltpu.SMEM)],
        out_specs=pl.BlockSpec((BLOCK, attn_w), lambda i: (i, 0)),
        compiler_params=_params(("parallel",), 40 * 1024 * 1024),
    )(proj, proj, proj, proj, proj, proj, proj, qw_row, kw_row, gmat, sinks)


def _attention_bwd(proj, d_ag, dproj, qw_row, kw_row, gmat, sinks, *, attn_w, kv_w, name, rider=None):
    rows = proj.shape[0]
    nb = rows // BLOCK
    per_kv = attn_w // HEAD_DIM // N_KV_HEADS
    gw = per_kv * HEAD_DIM
    keys = 2 * BLOCK
    scale = 1.0 / math.sqrt(HEAD_DIM)
    w_out = 2 * attn_w + 2 * kv_w

    def body(q_ref, kp_ref, kc_ref, vp_ref, vc_ref, glo_ref, ghi_ref, dag_ref, qw_ref, kw_ref, gm_ref, sink_ref, _,
             dp_ref, dkv_ref, gqw_ref, gkw_ref, gs_ref):
        i = pl.program_id(0)
        kk = jnp.concatenate([kp_ref[...], kc_ref[...]], axis=0).astype(F32)
        vv = jnp.concatenate([vp_ref[...], vc_ref[...]], axis=0).astype(F32)
        gate = jnp.concatenate([glo_ref[...], ghi_ref[...]], axis=1).astype(F32)
        d_ag_v = dag_ref[...].astype(F32)
        qw, kw, gmat_v = qw_ref[...], kw_ref[...], gm_ref[...]
        qn, q_rstd, kn, k_rstd, groups = _attn_block(i, q_ref[...].astype(F32), kk, vv, qw, kw, gmat_v, sink_ref,
                                                     per_kv)
        lane = lax.broadcasted_iota(jnp.int32, (SUBLANES, LANES), 1)
        sub = lax.broadcasted_iota(jnp.int32, (SUBLANES, LANES), 0)
        gsink = jnp.zeros((SUBLANES, LANES), F32)
        dq_groups, dgate_groups, dk_heads, dv_heads = [], [], [], []
        for g, (kd, vd, qg, ps, p_sinks, pb, o) in enumerate(groups):
            cs = slice(g * gw, (g + 1) * gw)
            gate_g, d_ag_g = gate[:, cs], d_ag_v[:, cs]
            dgate_groups.append(d_ag_g * o * _dsilu(gate_g))
            do = (d_ag_g * _silu(gate_g)).astype(BF16)
            dp_all = _dot(do, vd, NT)
            dss = []
            for r in range(per_kv):
                p, dp = ps[r], dp_all[:, r * keys:(r + 1) * keys]
                delta = jnp.sum(p * dp, axis=-1, keepdims=True)
                dss.append(p * (dp - delta) * scale)
                gs_h = jnp.sum(-p_sinks[r] * delta, axis=0, keepdims=True)
                gsink = gsink + jnp.where((lane == g * per_kv + r) & (sub == 0), gs_h, 0.0)
            ds = jnp.concatenate(dss, axis=1).astype(BF16)
            dq_groups.append(_dot(ds, kd, NN))
            dk_heads.append(_fold_heads(_dot(ds, qg, TN), per_kv))
            dv_heads.append(_fold_heads(_dot(pb, do, TN), per_kv))
        dqh = jnp.concatenate(dq_groups, axis=1)
        gqw = jnp.sum(dqh * qn, axis=0, keepdims=True)
        dqn = dqh * qw
        dq = q_rstd * (dqn - qn * _head_mean(dqn * qn, gmat_v))
        dkh = _join_heads(dk_heads)
        gkw = jnp.sum(dkh * kn, axis=0, keepdims=True)
        dkn = dkh * kw
        dk = k_rstd * (dkn - kn * _head_mean(dkn * kn, gmat_v))
        dp_ref[:, 0:attn_w] = dq.astype(BF16)
        dp_ref[:, attn_w:attn_w + 2 * kv_w] = jnp.zeros((BLOCK, 2 * kv_w), BF16)
        dp_ref[:, attn_w + 2 * kv_w:w_out] = jnp.concatenate(dgate_groups, axis=1).astype(BF16)
        dkv_ref[0] = jnp.concatenate([dk, _join_heads(dv_heads)], axis=1)

        @pl.when(i == 0)
        def _():
            gqw_ref[...] = gqw
            gkw_ref[...] = gkw
            gs_ref[...] = gsink

        @pl.when(i > 0)
        def _():
            gqw_ref[...] += gqw
            gkw_ref[...] += gkw
            gs_ref[...] += gsink

    const = lambda a: pl.BlockSpec(a.shape, lambda i: (0, 0))
    res, landed = _call(
        body, [proj, proj, proj, proj, proj, proj, proj, d_ag, qw_row, kw_row, gmat, sinks, dproj], name=name,
        out_shape=[jax.ShapeDtypeStruct(dproj.shape, BF16),
                   jax.ShapeDtypeStruct((nb, 2 * BLOCK, 2 * kv_w), F32),
                   jax.ShapeDtypeStruct(qw_row.shape, F32), jax.ShapeDtypeStruct(kw_row.shape, F32),
                   jax.ShapeDtypeStruct((SUBLANES, LANES), F32)],
        grid=(nb,),
        in_specs=_attn_specs(attn_w, kv_w) + [pl.BlockSpec((BLOCK, attn_w), lambda i: (i, 0)), const(qw_row),
                                              const(kw_row), const(gmat), pl.BlockSpec(memory_space=pltpu.SMEM),
                                              _any_spec()],
        out_specs=[pl.BlockSpec((BLOCK, w_out), lambda i: (i, 0)),
                   pl.BlockSpec((1, 2 * BLOCK, 2 * kv_w), lambda i: (i, 0, 0)),
                   const(qw_row), const(kw_row), pl.BlockSpec((SUBLANES, LANES), lambda i: (0, 0))],
        aliases={12: 0}, semantics=("arbitrary",), vmem=48 * 1024 * 1024, rider=rider)
    return res if rider is None else (res, landed)


def _attention_dkv(dproj, dkv, *, attn_w, kv_w, name):
    rows = dproj.shape[0]
    nb = rows // BLOCK
    col = attn_w // (2 * kv_w)

    def body(cur_ref, nxt_ref, _, o_ref):
        i = pl.program_id(0)
        nxt = jnp.where(i < nb - 1, nxt_ref[0, 0:BLOCK, :], 0.0)
        o_ref[...] = (cur_ref[0, BLOCK:2 * BLOCK, :] + nxt).astype(BF16)

    blk = lambda f: pl.BlockSpec((1, 2 * BLOCK, 2 * kv_w), f)
    return _pallas(
        body, name=name, out_shape=jax.ShapeDtypeStruct(dproj.shape, BF16), grid=(nb,),
        in_specs=[blk(lambda i: (i, 0, 0)), blk(lambda i: (jnp.minimum(i + 1, nb - 1), 0, 0)), _any_spec()],
        out_specs=pl.BlockSpec((BLOCK, 2 * kv_w), lambda i: (i, col)),
        input_output_aliases={2: 0},
        compiler_params=_params(("parallel",)),
    )(dkv, dkv, dproj)


def _cmul(ar, ai, br, bi):
    return ar * br - ai * bi, ar * bi + ai * br


def _ssm_prep(a_re, a_im, log_dt_col, steps, name):
    def body(are_ref, aim_ref, ldt_ref, abr_ref, abi_ref, cfr_ref, cfi_ref, apr_ref, api_ref, pwr_ref, pwi_ref):
        are, aim = are_ref[...], aim_ref[...]
        dt = jnp.exp(ldt_ref[...])
        mag = jnp.exp(dt * are)
        abr = mag * jnp.cos(dt * aim)
        abi = mag * jnp.sin(dt * aim)
        num_re, num_im = abr - 1.0, abi
        den = are * are + aim * aim
        abr_ref[...] = abr
        abi_ref[...] = abi
        cfr_ref[...] = (num_re * are + num_im * aim) / den
        cfi_ref[...] = (num_im * are - num_re * aim) / den
        pr, pi = jnp.ones_like(abr), jnp.zeros_like(abr)
        for k in range(steps):
            pwr_ref[k] = pr
            pwi_ref[k] = pi
            pr, pi = _cmul(pr, pi, abr, abi)
        apr_ref[...] = pr
        api_ref[...] = pi

    shp = jax.ShapeDtypeStruct(a_re.shape, F32)
    pows = jax.ShapeDtypeStruct((steps,) + a_re.shape, F32)
    return _pallas(body, name=name, out_shape=[shp] * 6 + [pows] * 2)(a_re, a_im, log_dt_col)


def _ssm_param_bwd(a_re, a_im, log_dt_col, d_ab_re, d_ab_im, b_re, b_im, dbt_re, dbt_im, name):
    def body(are_ref, aim_ref, ldt_ref, gabr_ref, gabi_ref, br_ref, bi_ref, tr_ref, ti_ref,
             dar_ref, dai_ref, dldt_ref, dbr_ref, dbi_ref):
        are, aim = are_ref[...], aim_ref[...]
        dt = jnp.exp(ldt_ref[...])
        mag = jnp.exp(dt * are)
        abr = mag * jnp.cos(dt * aim)
        abi = mag * jnp.sin(dt * aim)
        den = are * are + aim * aim
        cfr = ((abr - 1.0) * are + abi * aim) / den
        cfi = (abi * are - (abr - 1.0) * aim) / den
        gabr, gabi = jnp.sum(gabr_ref[...], axis=0), jnp.sum(gabi_ref[...], axis=0)
        t_re, t_im = tr_ref[...], ti_ref[...]
        g_r, g_i = _cmul(br_ref[...], -bi_ref[...], t_re, t_im)
        gcfr, gcfi = jnp.sum(g_r, axis=1), jnp.sum(g_i, axis=1)
        dbr, dbi = _cmul(cfr[:, None, :], -cfi[:, None, :], t_re, t_im)
        dbr_ref[...] = dbr
        dbi_ref[...] = dbi
        inv_r, inv_i = are / den, -aim / den
        t_r, t_i = _cmul(inv_r, -inv_i, gcfr, gcfi)
        gabr, gabi = gabr + t_r, gabi + t_i
        q_r, q_i = _cmul(cfr, cfi, inv_r, inv_i)
        da_r, da_i = _cmul(-q_r, q_i, gcfr, gcfi)
        gz_r, gz_i = _cmul(abr, -abi, gabr, gabi)
        dar_ref[...] = da_r + dt * gz_r
        dai_ref[...] = da_i + dt * gz_i
        dldt_ref[...] = dt * jnp.sum(are * gz_r + aim * gz_i, axis=-1, keepdims=True)

    shp = jax.ShapeDtypeStruct(a_re.shape, F32)
    bshp = jax.ShapeDtypeStruct(b_re.shape, F32)
    return _pallas(body, name=name,
                   out_shape=[shp, shp, jax.ShapeDtypeStruct(log_dt_col.shape, F32), bshp, bshp])(
        a_re, a_im, log_dt_col, d_ab_re, d_ab_im, b_re, b_im, dbt_re, dbt_im)


SCAN_LANES = 512
SSM_CHUNK = 256
W_IN_PARTS = 2


def _scan_segments(xr_ref, xi_ref, a_re, a_im, ap_re, ap_im, pw_re, pw_im, carry_re, carry_im, cm_re, cm_im, steps,
                   reverse, base):
    n = xr_ref.shape[1]
    seg_order = range(SUBLANES - 1, -1, -1) if reverse else range(SUBLANES)
    sign = -1.0 if reverse else 1.0
    for c0 in range(0, n, SCAN_LANES):
        ls = slice(c0, c0 + SCAN_LANES)
        gs = slice(base + c0, base + c0 + SCAN_LANES)
        ar = jnp.broadcast_to(a_re[:, gs], (SUBLANES, SCAN_LANES))
        ai = jnp.broadcast_to(a_im[:, gs], (SUBLANES, SCAN_LANES))
        end_r = jnp.zeros((SUBLANES, SCAN_LANES), F32)
        end_i = jnp.zeros((SUBLANES, SCAN_LANES), F32)
        for j in range(steps):
            k = j if reverse else steps - 1 - j
            rws = slice(j * SUBLANES, (j + 1) * SUBLANES)
            tr, ti = _cmul(pw_re[k:k + 1, gs], sign * pw_im[k:k + 1, gs], xr_ref[rws, ls], xi_ref[rws, ls])
            end_r, end_i = end_r + tr, end_i + ti
        cr, ci = carry_re[:, gs], carry_im[:, gs]
        apr, api = ap_re[:, gs], ap_im[:, gs]
        for r in seg_order:
            cm_re[r:r + 1, gs] = cr
            cm_im[r:r + 1, gs] = ci
            tr, ti = _cmul(apr, api, cr, ci)
            cr, ci = end_r[r:r + 1, :] + tr, end_i[r:r + 1, :] + ti
        carry_re[:, gs] = cr
        carry_im[:, gs] = ci

        def run(t, s, ar=ar, ai=ai, ls=ls):
            j = steps - 1 - t if reverse else t
            r0 = pl.multiple_of(j * SUBLANES, SUBLANES)
            sr, si = _cmul(ar, ai, s[0], s[1])
            sr = sr + xr_ref[pl.ds(r0, SUBLANES), ls]
            si = si + xi_ref[pl.ds(r0, SUBLANES), ls]
            xr_ref[pl.ds(r0, SUBLANES), ls] = sr
            xi_ref[pl.ds(r0, SUBLANES), ls] = si
            return sr, si

        lax.fori_loop(0, steps, run, (cm_re[:, gs], cm_im[:, gs]))


SB_GROUPS = MXU_DIM // GROUP
SB_STATE = SB_GROUPS * STATE


def _ssm_rows(m):
    flat = m.reshape(-1, STATE).astype(F32)
    return jnp.concatenate([flat, flat], axis=1)


def _from_ssm_rows(rows):
    return rows[:, :STATE].reshape(-1, GROUP, STATE)


def _own_group(shape):
    row_g = lax.broadcasted_iota(jnp.int32, shape, 0) // GROUP
    col_g = lax.broadcasted_iota(jnp.int32, shape, 1) // STATE
    return row_g == col_g


def _block_diagonal(rows):
    tiled = jnp.concatenate([rows] * (SB_STATE // LANES), axis=1)
    return jnp.where(_own_group(tiled.shape), tiled, 0.0).astype(BF16)


def _block_rows(acc):
    x = jnp.where(_own_group(acc.shape), acc, 0.0)
    while x.shape[1] > LANES:
        half = x.shape[1] // 2
        x = x[:, :half] + x[:, half:]
    return x + pltpu.roll(x, STATE, axis=1)


def _rows_to_segments(dst, srcs, steps, stage):
    for ref, off in srcs:
        for k in range(ref.shape[1] // LANES):
            stage[off // LANES + k] = ref[:, k * LANES:(k + 1) * LANES].astype(F32)
    for k in range(dst.shape[1] // LANES):
        for j in range(steps):
            dst[j * SUBLANES:(j + 1) * SUBLANES, k * LANES:(k + 1) * LANES] = (
                stage[k, pl.ds(j, SUBLANES, stride=steps), :])


def _segments_to_rows(dst, src, steps, stage):
    for k in range(src.shape[1] // LANES):
        for j in range(steps):
            stage[k, pl.ds(j, SUBLANES, stride=steps), :] = (
                src[j * SUBLANES:(j + 1) * SUBLANES, k * LANES:(k + 1) * LANES])
    for k in range(src.shape[1] // LANES):
        dst[:, k * LANES:(k + 1) * LANES] = stage[k]


def _u_specs(w, o_u, chunk, index):
    half = w // 2
    assert o_u % half == 0
    return [pl.BlockSpec((chunk, half), lambda c, k=k: (index(c), o_u // half + k)) for k in range(2)]


def _ssm_fwd(proj, o_u, bc_rows, rows_p, d_row, *, chunk, name, rider=None):
    rows = proj.shape[0]
    w = d_row.shape[1]
    nc = rows // chunk
    steps = chunk // SUBLANES
    nsb = w // MXU_DIM
    n_state = nsb * SB_STATE

    def body(ulo_ref, uhi_ref, b2r_ref, b2i_ref, c2r_ref, c2i_ref, abr_ref, abi_ref, cfr_ref, cfi_ref, apr_ref,
             api_ref, pwr_ref, pwi_ref, d_ref, y_ref, str_ref, sti_ref, yg_ref, bre_ref, bim_ref, cre_ref, cim_ref,
             useg, yseg, stage, sr, si,
             carry_r, carry_i, cm_r, cm_i):
        @pl.when(pl.program_id(0) == 0)
        def _():
            for src, dst in ((b2r_ref, bre_ref), (b2i_ref, bim_ref), (c2r_ref, cre_ref), (c2i_ref, cim_ref)):
                for sb in range(nsb):
                    dst[sb] = _block_diagonal(src[sb * MXU_DIM:(sb + 1) * MXU_DIM, :])
            carry_r[...] = jnp.zeros_like(carry_r)
            carry_i[...] = jnp.zeros_like(carry_i)

        str_ref[0] = carry_r[...]
        sti_ref[0] = carry_i[...]
        _rows_to_segments(useg, [(ulo_ref, 0), (uhi_ref, w // 2)], steps, stage)
        for sb in range(nsb):
            us = slice(sb * MXU_DIM, (sb + 1) * MXU_DIM)
            ss = slice(sb * SB_STATE, (sb + 1) * SB_STATE)
            ub = useg[:, us].astype(BF16)
            bur = _dot(ub, bre_ref[sb], NN)
            bui = _dot(ub, bim_ref[sb], NN)
            xr, xi = _cmul(cfr_ref[:, ss], cfi_ref[:, ss], bur, bui)
            sr[...] = xr
            si[...] = xi
            _scan_segments(sr, si, abr_ref[...], abi_ref[...], apr_ref[...], api_ref[...], pwr_ref, pwi_ref,
                           carry_r, carry_i, cm_r, cm_i, steps, False, sb * SB_STATE)
            y = _dot(sr[...].astype(BF16), cre_ref[sb], NT) - _dot(si[...].astype(BF16), cim_ref[sb], NT)
            yseg[:, us] = y + d_ref[:, us] * useg[:, us]
        _segments_to_rows(y_ref, yseg, steps, stage)
        yg_ref[...] = _gelu(y_ref[...]).astype(BF16)

    const = lambda a: pl.BlockSpec(a.shape, lambda c: (0,) * a.ndim)
    row_n = pl.BlockSpec((1, n_state), lambda c: (0, 0))
    st = pl.BlockSpec((1, 1, n_state), lambda c: (c, 0, 0))
    held = [pltpu.VMEM((nsb, MXU_DIM, SB_STATE), BF16)] * 4
    vmem = (4 * _nbytes((nsb, MXU_DIM, SB_STATE), BF16) + 4 * _nbytes((chunk, SB_STATE), F32)
            + 12 * _nbytes((chunk, w), F32) + 8 * _nbytes(bc_rows[0].shape, F32))
    res, landed = _call(
        body, [proj, proj, *bc_rows, *rows_p, d_row], name=name,
        out_shape=[jax.ShapeDtypeStruct((rows, w), F32), jax.ShapeDtypeStruct((nc, 1, n_state), F32),
                   jax.ShapeDtypeStruct((nc, 1, n_state), F32), jax.ShapeDtypeStruct((rows, w), BF16)],
        grid=(nc,),
        in_specs=_u_specs(w, o_u, chunk, lambda c: c) + [const(b) for b in bc_rows]
        + [row_n] * 6 + [pl.BlockSpec((steps, n_state), lambda c: (0, 0))] * 2 + [pl.BlockSpec((1, w), lambda c: (0, 0))],
        out_specs=[pl.BlockSpec((chunk, w), lambda c: (c, 0)), st, st, pl.BlockSpec((chunk, w), lambda c: (c, 0))],
        scratch_shapes=held + [pltpu.VMEM((chunk, w), F32), pltpu.VMEM((chunk, w), F32),
                               pltpu.VMEM((w // LANES, chunk, LANES), F32),
                               pltpu.VMEM((chunk, SB_STATE), F32), pltpu.VMEM((chunk, SB_STATE), F32),
                               pltpu.VMEM((1, n_state), F32), pltpu.VMEM((1, n_state), F32),
                               pltpu.VMEM((SUBLANES, n_state), F32), pltpu.VMEM((SUBLANES, n_state), F32)],
        semantics=("arbitrary",), vmem=vmem, rider=rider)
    return res if rider is None else (res, landed)


def _ssm_bwd(proj, o_u, y, dyg, st_re, st_im, bc_rows, rows_p, d_row, *, chunk, name, rider=None):
    rows = proj.shape[0]
    w = d_row.shape[1]
    nc = rows // chunk
    steps = chunk // SUBLANES
    nsb = w // MXU_DIM
    n_state = nsb * SB_STATE

    def body(ulo_ref, uhi_ref, y_ref, dyg_ref, str_ref, sti_ref, b2r_ref, b2i_ref, c2r_ref, c2i_ref, t2r_ref,
             t2i_ref, abr_ref, abi_ref, cfr_ref, cfi_ref, apr_ref, api_ref, pwr_ref, pwi_ref, d_ref,
             du_ref, gb2r_ref, gb2i_ref, gc2r_ref, gc2i_ref, gabr_ref, gabi_ref, dd_ref,
             bre_ref, bim_ref, cre_ref, cim_ref, btr_ref, bti_ref, dbre_ref, dbim_ref, dcre_ref, dcim_ref,
             useg, dyseg, dynat, stage, sr, si, lr, li, carry_r, carry_i, lam_r, lam_i, cm_r, cm_i, cl_r, cl_i):
        first = pl.program_id(0) == 0

        @pl.when(first)
        def _():
            for src, dst in ((b2r_ref, bre_ref), (b2i_ref, bim_ref), (c2r_ref, cre_ref), (c2i_ref, cim_ref),
                             (t2r_ref, btr_ref), (t2i_ref, bti_ref)):
                for sb in range(nsb):
                    dst[sb] = _block_diagonal(src[sb * MXU_DIM:(sb + 1) * MXU_DIM, :])
            lam_r[...] = jnp.zeros_like(lam_r)
            lam_i[...] = jnp.zeros_like(lam_i)
            for ref in (dbre_ref, dbim_ref, dcre_ref, dcim_ref, gabr_ref, gabi_ref, dd_ref):
                ref[...] = jnp.zeros_like(ref)

        dynat[...] = dyg_ref[...].astype(F32) * _dgelu(y_ref[...])
        half = w // 2
        dd_ref[:, :half] += jnp.sum(dynat[:, :half] * ulo_ref[...].astype(F32), axis=0, keepdims=True)
        dd_ref[:, half:] += jnp.sum(dynat[:, half:] * uhi_ref[...].astype(F32), axis=0, keepdims=True)
        _rows_to_segments(useg, [(ulo_ref, 0), (uhi_ref, half)], steps, stage)
        _rows_to_segments(dyseg, [(dynat, 0)], steps, stage)
        dy = dyseg[...]
        dyb = dy.astype(BF16)
        ub = useg[...].astype(BF16)
        carry_r[...] = str_ref[0]
        carry_i[...] = sti_ref[0]
        abr, abi = abr_ref[...], abi_ref[...]
        apr, api = apr_ref[...], api_ref[...]
        for sb in range(nsb):
            us = slice(sb * MXU_DIM, (sb + 1) * MXU_DIM)
            ss = slice(sb * SB_STATE, (sb + 1) * SB_STATE)
            base = sb * SB_STATE
            br = _dot(ub[:, us], bre_ref[sb], NN)
            bi = _dot(ub[:, us], bim_ref[sb], NN)
            xr, xi = _cmul(cfr_ref[:, ss], cfi_ref[:, ss], br, bi)
            sr[...] = xr
            si[...] = xi
            lr[...] = _dot(dyb[:, us], cre_ref[sb], NN)
            li[...] = -_dot(dyb[:, us], cim_ref[sb], NN)
            _scan_segments(sr, si, abr, abi, apr, api, pwr_ref, pwi_ref, carry_r, carry_i, cm_r, cm_i, steps, False,
                           base)
            dcre_ref[sb] += _dot(dyb[:, us], sr[...].astype(BF16), TN)
            dcim_ref[sb] -= _dot(dyb[:, us], si[...].astype(BF16), TN)
            _scan_segments(lr, li, abr, -abi, apr, -api, pwr_ref, pwi_ref, lam_r, lam_i, cl_r, cl_i, steps, True,
                           base)
            for c0 in range(0, SB_STATE, SCAN_LANES):
                ls = slice(c0, c0 + SCAN_LANES)
                gs = slice(base + c0, base + c0 + SCAN_LANES)

                def step(j, acc, ls=ls):
                    gar, gai, pr, pi = acc
                    r0 = pl.multiple_of(j * SUBLANES, SUBLANES)
                    rws = pl.ds(r0, SUBLANES)
                    t_r, t_i = _cmul(pr, -pi, lr[rws, ls], li[rws, ls])
                    return gar + t_r, gai + t_i, sr[rws, ls], si[rws, ls]

                zero = jnp.zeros((SUBLANES, SCAN_LANES), F32)
                gar, gai, _, _ = lax.fori_loop(0, steps, step, (zero, zero, cm_r[:, gs], cm_i[:, gs]))
                gabr_ref[:, gs] += gar
                gabi_ref[:, gs] += gai
            xr, xi = lr[...].astype(BF16), li[...].astype(BF16)
            du = _dot(xr, btr_ref[sb], NT) + _dot(xi, bti_ref[sb], NT)
            useg[:, us] = du + d_ref[:, us] * dy[:, us]
            dbre_ref[sb] += _dot(ub[:, us], xr, TN)
            dbim_ref[sb] += _dot(ub[:, us], xi, TN)
        _segments_to_rows(du_ref, useg, steps, stage)

        @pl.when(pl.program_id(0) == nc - 1)
        def _():
            for src, dst in ((dbre_ref, gb2r_ref), (dbim_ref, gb2i_ref), (dcre_ref, gc2r_ref), (dcim_ref, gc2i_ref)):
                for sb in range(nsb):
                    dst[sb * MXU_DIM:(sb + 1) * MXU_DIM, :] = _block_rows(src[sb])

    rev = lambda c: nc - 1 - c
    const = lambda a: pl.BlockSpec(a.shape, lambda c: (0,) * a.ndim)
    tile = pl.BlockSpec((chunk, w), lambda c: (rev(c), 0))
    row_n = pl.BlockSpec((1, n_state), lambda c: (0, 0))
    row_w = pl.BlockSpec((1, w), lambda c: (0, 0))
    st = pl.BlockSpec((1, 1, n_state), lambda c: (rev(c), 0, 0))
    acc8 = pl.BlockSpec((SUBLANES, n_state), lambda c: (0, 0))
    big = pltpu.VMEM((chunk, SB_STATE), F32)
    small = pltpu.VMEM((chunk, w), F32)
    row = pltpu.VMEM((1, n_state), F32)
    eight = pltpu.VMEM((SUBLANES, n_state), F32)
    blk = (nsb, MXU_DIM, SB_STATE)
    held = [pltpu.VMEM(blk, BF16)] * 6 + [pltpu.VMEM(blk, F32)] * 4
    vmem = (6 * _nbytes(blk, BF16) + 4 * _nbytes(blk, F32) + 5 * _nbytes((chunk, SB_STATE), F32)
            + 12 * _nbytes((chunk, w), F32) + 20 * _nbytes(bc_rows[0].shape, F32))
    res, landed = _call(
        body, [proj, proj, y, dyg, st_re, st_im, *bc_rows, *rows_p, d_row], name=name,
        out_shape=[jax.ShapeDtypeStruct((rows, w), F32)] + [jax.ShapeDtypeStruct(b.shape, F32) for b in bc_rows[:4]]
        + [jax.ShapeDtypeStruct((SUBLANES, n_state), F32)] * 2 + [jax.ShapeDtypeStruct((1, w), F32)],
        grid=(nc,),
        in_specs=_u_specs(w, o_u, chunk, rev) + [tile, tile, st, st] + [const(b) for b in bc_rows]
        + [row_n] * 6 + [pl.BlockSpec((steps, n_state), lambda c: (0, 0))] * 2 + [row_w],
        out_specs=[tile] + [const(b) for b in bc_rows[:4]] + [acc8] * 2 + [row_w],
        scratch_shapes=held + [small] * 3 + [pltpu.VMEM((w // LANES, chunk, LANES), F32)] + [big] * 4 + [row] * 4
        + [eight] * 4,
        semantics=("arbitrary",), vmem=vmem, rider=rider)
    return res if rider is None else (res, landed)


def _out_proj_loss(merged, w_o, x, target, name):
    rows, d = x.shape
    tm, tn = _tile(rows, 1024, SUBLANES), _tile(d, 1024)

    def body(a_ref, b_ref, x_ref, t_ref, g_ref, gb_ref, l_ref):
        err = x_ref[...] + _dot(a_ref[...], b_ref[...], NN) - t_ref[...]
        g = err * (1.0 / d)
        g_ref[...] = g
        gb_ref[...] = g.astype(BF16)
        part = jnp.sum(0.5 * err * g, axis=0, keepdims=True)
        first = pl.program_id(1) == 0

        @pl.when(first)
        def _():
            l_ref[...] = part

        @pl.when(jnp.logical_not(first))
        def _():
            l_ref[...] += part

    tile = pl.BlockSpec((tm, tn), lambda j, i: (i, j))
    vmem = 2 * (_nbytes((tm, d), BF16) + _nbytes((d, tn), BF16)) + 12 * _nbytes((tm, tn), F32)
    return _pallas(
        body, name=name,
        out_shape=[jax.ShapeDtypeStruct((rows, d), F32), jax.ShapeDtypeStruct((rows, d), BF16),
                   jax.ShapeDtypeStruct((1, d), F32)],
        grid=(d // tn, rows // tm),
        in_specs=[pl.BlockSpec((tm, d), lambda j, i: (i, 0)), pl.BlockSpec((d, tn), lambda j, i: (0, j)), tile, tile],
        out_specs=[tile, tile, pl.BlockSpec((1, tn), lambda j, i: (0, j))],
        compiler_params=_params(("parallel", "arbitrary"), vmem),
    )(merged, w_o, x, target)


def _pair_sum(grad, recv, name):
    r4, cdim = recv.shape
    r = r4 // N_CHIPS
    tr = _tile(r, 544, 16)
    g4 = grad.reshape(N_CHIPS, 2, r, cdim)
    r3 = recv.reshape(N_CHIPS, r, cdim)
    core = jnp.reshape(lax.axis_index("c"), (1,)).astype(jnp.int32)

    def body(c_ref, g_ref, r_ref, o_ref):
        o_ref[...] = (g_ref[0] + r_ref[...]).astype(BF16)

    out = _pallas(
        body, name=name, out_shape=jax.ShapeDtypeStruct((N_CHIPS, r, cdim), BF16),
        grid_spec=pltpu.PrefetchScalarGridSpec(
            num_scalar_prefetch=1, grid=(N_CHIPS, r // tr),
            in_specs=[pl.BlockSpec((1, 1, tr, cdim), lambda j, i, c: (j, c[0], i, 0)),
                      pl.BlockSpec((1, tr, cdim), lambda j, i, c: (j, i, 0))],
            out_specs=pl.BlockSpec((1, tr, cdim), lambda j, i, c: (j, i, 0))),
        compiler_params=_params(("parallel", "parallel"), 6 * _nbytes((tr, cdim), F32)),
    )(core, g4, r3)
    return out.reshape(r4, cdim)


def _chip_sum(recv, name):
    r4, cdim = recv.shape
    r = r4 // N_CHIPS
    tr = _tile(r, 544, 16)
    r3 = recv.reshape(N_CHIPS, r, cdim)

    def body(r_ref, o_ref):
        acc = r_ref[0].astype(F32)
        for j in range(1, N_CHIPS):
            acc = acc + r_ref[j].astype(F32)
        o_ref[...] = acc

    return _pallas(
        body, name=name, out_shape=jax.ShapeDtypeStruct((r, cdim), F32), grid=(r // tr,),
        in_specs=[pl.BlockSpec((N_CHIPS, tr, cdim), lambda i: (0, i, 0))],
        out_specs=pl.BlockSpec((tr, cdim), lambda i: (i, 0)),
        compiler_params=_params(("parallel",), 8 * _nbytes((tr, cdim), F32)),
    )(r3)


def _adamw_math(w, g, m, v):
    m = ADAM_B1 * m + (1.0 - ADAM_B1) * g
    v = ADAM_B2 * v + (1.0 - ADAM_B2) * (g * g)
    m_hat = m / (1.0 - ADAM_B1 ** ADAM_STEP)
    v_hat = v / (1.0 - ADAM_B2 ** ADAM_STEP)
    delta = -ADAM_LR * (m_hat / (jnp.sqrt(v_hat) + ADAM_EPS) + ADAM_WD * w)
    return delta, m, v


def _adamw(w, g, m, v, name):
    rows, cols = w.shape
    tr = _tile(rows, 256, SUBLANES)

    def body(w_ref, g_ref, m_ref, v_ref, d_ref, nm_ref, nv_ref):
        d, nm, nv = _adamw_math(w_ref[...], g_ref[...], m_ref[...], v_ref[...])
        d_ref[...] = d
        nm_ref[...] = nm
        nv_ref[...] = nv

    spec = pl.BlockSpec((tr, cols), lambda i: (i, 0))
    shp = jax.ShapeDtypeStruct((rows, cols), F32)
    return _pallas(
        body, name=name, out_shape=[shp] * 3, grid=(rows // tr,), in_specs=[spec] * 4, out_specs=[spec] * 3,
        compiler_params=_params(("parallel",)),
    )(w, g, m, v)


def _adamw_chips(w, parts, m, v, name):
    rows, cols = w.shape
    assert sum(p.shape[1] for p in parts) == cols
    tr = _tile(rows, 64, 16)
    n = len(parts)

    def body(*refs):
        w_ref, m_ref, v_ref = refs[0], refs[1 + n], refs[2 + n]
        g_ref, d_ref, nm_ref, nv_ref = refs[3 + n:]
        cols_g = []
        for p_ref in refs[1:1 + n]:
            acc = p_ref[0].astype(F32)
            for j in range(1, N_CHIPS):
                acc = acc + p_ref[j].astype(F32)
            cols_g.append(acc)
        g = cols_g[0] if n == 1 else jnp.concatenate(cols_g, axis=1)
        d, nm, nv = _adamw_math(w_ref[...], g, m_ref[...], v_ref[...])
        g_ref[...] = g
        d_ref[...] = d
        nm_ref[...] = nm
        nv_ref[...] = nv

    spec = pl.BlockSpec((tr, cols), lambda i: (i, 0))
    part_specs = [pl.BlockSpec((N_CHIPS, tr, p.shape[1]), lambda i: (0, i, 0)) for p in parts]
    shp = jax.ShapeDtypeStruct((rows, cols), F32)
    return _pallas(
        body, name=name, out_shape=[shp] * 4, grid=(rows // tr,),
        in_specs=[spec] + part_specs + [spec, spec], out_specs=[spec] * 4,
        compiler_params=_params(("parallel",)),
    )(w, *[p.reshape(N_CHIPS, rows, p.shape[1]) for p in parts], m, v)


def _adamw_small(w, parts, m, v, name):
    rows, cols = w.shape
    p3 = parts.reshape(N_DEV, rows, cols)

    def body(w_ref, p_ref, m_ref, v_ref, g_ref, d_ref, nm_ref, nv_ref):
        g = p_ref[0]
        for k in range(1, N_DEV):
            g = g + p_ref[k]
        d, nm, nv = _adamw_math(w_ref[...], g, m_ref[...], v_ref[...])
        g_ref[...] = g
        d_ref[...] = d
        nm_ref[...] = nm
        nv_ref[...] = nv

    shp = jax.ShapeDtypeStruct((rows, cols), F32)
    return _pallas(body, name=name, out_shape=[shp] * 4)(w, p3, m, v)


SMALL = ("norm_w", "q_norm_w", "k_norm_w", "sinks", "A_re", "A_im", "log_dt", "B_re", "B_im", "C_re", "C_im",
         "D_skip", "b_glu")
LARGE = ("w_in", "w_attn_proj", "w_glu", "w_ssm_proj", "w_out")
ORDER = ("norm_w", "w_in", "q_norm_w", "k_norm_w", "sinks", "w_attn_proj", "A_re", "A_im", "log_dt", "B_re", "B_im",
         "C_re", "C_im", "D_skip", "w_glu", "b_glu", "w_ssm_proj", "w_out")


SMALL_REST = ("loss",) + SMALL[1:]


def _pack(named, keys):
    flat = jnp.concatenate([named[k].reshape(-1).astype(F32) for k in keys])
    n = flat.shape[0]
    rows = -(-n // (LANES * SUBLANES)) * SUBLANES
    return jnp.pad(flat, (0, rows * LANES - n)).reshape(rows, LANES)


def _unpack(packed, like, keys):
    flat = packed.reshape(-1)
    out, o = {}, 0
    for k in keys:
        n = like[k].size
        out[k] = flat[o:o + n].reshape(like[k].shape)
        o += n
    return out


def _step(xs, target, p, shards):
    s_in, s_ap, s_glu, s_sp, s_o = shards
    seq, d = xs.shape
    attn_w = (d // 128) * HEAD_DIM
    n_q = attn_w // HEAD_DIM
    kv_w = N_KV_HEADS * HEAD_DIM
    ssm_w = d // 2
    n_groups = ssm_w // GROUP
    n_state = n_groups * STATE
    in_w = N_DEV * s_in.shape[0]
    assert in_w == 2 * attn_w + 2 * kv_w + 2 * ssm_w + 2 * d
    o_u = 2 * attn_w + 2 * kv_w
    o_z = o_u + ssm_w
    o_ga = o_z + ssm_w
    chunk = min(SSM_CHUNK, seq)
    cw = d // 4

    norm_row = p["norm_w"].reshape(1, d)
    half = d // W_IN_PARTS
    assert W_IN_PARTS == 2
    s_in_parts = [s_in[:, :half], s_in[:, half:]]
    h, (w_lo,) = _rmsnorm_fwd(xs, norm_row, "rmsnorm_fwd", rider=_all_gather(s_in_parts[:1]))
    part, (w_hi,) = _matmul(Cols(h, 0, half), w_lo, mode="nt", name="in_proj_0", tn=2176, out_dtype=BF16,
                            rider=_all_gather(s_in_parts[1:]))
    proj = _matmul(Cols(h, half, half), w_hi, mode="nt", name="in_proj_1", tn=2176, out_dtype=BF16, add=part)
    w_in_parts = [w_lo, w_hi]
    qw_row = jnp.tile(p["q_norm_w"], n_q).reshape(1, attn_w)
    kw_row = jnp.tile(p["k_norm_w"], N_KV_HEADS).reshape(1, kv_w)
    gmat = _head_mean_matrix()
    ag = _attention_fwd(proj, qw_row, kw_row, gmat, p["sinks"], attn_w=attn_w, kv_w=kv_w, name="attention_fwd")

    log_dt_col = p["log_dt"].reshape(n_groups, 1)
    prep = _ssm_prep(p["A_re"], p["A_im"], log_dt_col, chunk // SUBLANES, "ssm_prep")
    rows_p = [v.reshape(1, n_state) for v in prep[:6]] + [v.reshape(-1, n_state) for v in prep[6:]]
    bt_re, bt_im = p["B_re"].transpose(0, 2, 1), p["B_im"].transpose(0, 2, 1)
    cf_re, cf_im = prep[2][:, None, :], prep[3][:, None, :]
    bc_rows = [_ssm_rows(m) for m in (bt_re, bt_im, p["C_re"], p["C_im"],
                                      cf_re * bt_re - cf_im * bt_im, cf_re * bt_im + cf_im * bt_re)]
    d_row = p["D_skip"].reshape(1, ssm_w)
    (y_ssm, st_re, st_im, yg), (w_ap_t, w_glu_t, w_sp_t, w_o) = _ssm_fwd(
        proj, o_u, bc_rows[:4], rows_p, d_row, chunk=chunk, name="ssm_fwd",
        rider=_all_gather([s_ap, s_glu, s_sp, s_o]))
    glu = _matmul(yg, w_glu_t, mode="nt", name="glu_proj", out_dtype=BF16, bias=p["b_glu"].reshape(1, 2 * ssm_w))
    (ts,) = _ew(lambda ga, gb, z: ga * _sigmoid(gb) * _silu(z), name="glu_gate", rows=seq, width=ssm_w,
                tiles=[(glu, 0), (glu, ssm_w), (proj, o_z)], outs=[(BF16, ssm_w, 0)], cw=cw)
    yy = _matmul(ag, w_ap_t, mode="nt", name="attn_proj", out_dtype=BF16, out_cols=(2 * d, 0))
    yy = _matmul(ts, w_sp_t, mode="nt", name="ssm_proj", out_dtype=BF16, out_cols=(2 * d, d), into=yy)
    (merged,) = _ew(lambda ya, ys, ga, gs: _sigmoid(ga) * ya + _sigmoid(gs) * ys, name="merge", rows=seq, width=d,
                    tiles=[(yy, 0), (yy, d), (proj, o_ga), (proj, o_ga + d)], outs=[(BF16, d, 0)], cw=cw)
    dout, dout_b, loss_cols = _out_proj_loss(merged, w_o, xs, target, "out_proj_loss")
    loss_local = jnp.sum(loss_cols)

    g_w_o = _matmul(merged, dout_b, mode="tn", name="grad_w_out", tm=512, tk=4096)
    dmerged, (sib_o,) = _matmul(dout_b, w_o, mode="nt", name="d_merged", out_dtype=BF16,
                                rider=_sibling_exchange([g_w_o]))
    pair_o = _pair_sum(g_w_o, sib_o, "pair_sum_w_out")

    def merge_bwd(dm, y, g):
        s = _sigmoid(g)
        return dm * s, dm * y * s * (1.0 - s)

    dyy, dproj = _ew(merge_bwd, name="merge_bwd", rows=seq, width=2 * d,
                     tiles=[(dmerged, 0, d), (yy, 0), (proj, o_ga)],
                     outs=[(BF16, 2 * d, 0), (BF16, in_w, o_ga)], cw=cw)
    dy_a, dy_s = Cols(dyy, 0, d), Cols(dyy, d, d)
    g_w_ap_t = _matmul(dy_a, ag, mode="tn", name="grad_w_attn_proj", tm=512, tk=4096)
    g_w_sp_t = _matmul(dy_s, ts, mode="tn", name="grad_w_ssm_proj", tm=512, tk=4096)
    d_ag = _matmul(dy_a, w_ap_t, mode="nn", name="d_attn_gated", out_dtype=BF16)
    d_ts = _matmul(dy_s, w_sp_t, mode="nn", name="d_ssm_gated", out_dtype=BF16)

    (dproj, dkv, g_qw, g_kw, g_sinks), (chips_o, sib_ap, sib_sp) = _attention_bwd(
        proj, d_ag, dproj, qw_row, kw_row, gmat, p["sinks"], attn_w=attn_w, kv_w=kv_w, name="attention_bwd",
        rider=_join(_chip_exchange([pair_o]), _sibling_exchange([g_w_ap_t, g_w_sp_t])))
    pair_ap = _pair_sum(g_w_ap_t, sib_ap, "pair_sum_w_attn_proj")
    pair_sp = _pair_sum(g_w_sp_t, sib_sp, "pair_sum_w_ssm_proj")
    dproj = _attention_dkv(dproj, dkv, attn_w=attn_w, kv_w=kv_w, name="attention_dkv")

    n_half = ssm_w // _tile(2 * ssm_w, cw)

    def glu_bwd(j, dt, ga, gb, z):
        sb, sz = _sigmoid(gb), _silu(z)
        dg = jnp.where(j < n_half, dt * sb * sz, dt * ga * sb * (1.0 - sb) * sz)
        return dg, dg

    glu_ops = [(d_ts, 0, ssm_w), (glu, 0, ssm_w), (glu, ssm_w, ssm_w), (proj, o_z, ssm_w)]
    dglu, g_bglu = _ew(glu_bwd, name="glu_bwd", rows=seq, width=2 * ssm_w, tiles=glu_ops,
                       outs=[(BF16, 2 * ssm_w, 0)], accs=1, cw=cw, with_col=True)
    (dproj,) = _ew(lambda dt, ga, gb, z: dt * ga * _sigmoid(gb) * _dsilu(z), name="glu_bwd_z", rows=seq,
                   width=ssm_w, tiles=glu_ops, outs=[(BF16, in_w, o_z)], into=[dproj], cw=cw)
    g_w_glu_t = _matmul(dglu, yg, mode="tn", name="grad_w_glu", tm=512, tk=4096)
    d_yg = _matmul(dglu, w_glu_t, mode="nn", name="d_gelu", out_dtype=BF16)
    ((du, dbt_re, dbt_im, dc_re, dc_im, gabr, gabi, g_d), (chips_ap, chips_sp, sib_glu)) = _ssm_bwd(
        proj, o_u, y_ssm, d_yg, st_re, st_im, bc_rows, rows_p, d_row, chunk=chunk, name="ssm_bwd",
        rider=_join(_chip_exchange([pair_ap, pair_sp]), _sibling_exchange([g_w_glu_t])))
    pair_glu = _pair_sum(g_w_glu_t, sib_glu, "pair_sum_w_glu")
    (dproj,) = _ew(lambda v: v, name="du_store", rows=seq, width=ssm_w, tiles=[(du, 0)],
                   outs=[(BF16, in_w, o_u)], into=[dproj], cw=cw)
    g_a_re, g_a_im, g_log_dt, g_bt_re, g_bt_im = _ssm_param_bwd(
        p["A_re"], p["A_im"], log_dt_col, *[g.reshape(SUBLANES, n_groups, STATE) for g in (gabr, gabi)],
        bt_re, bt_im, _from_ssm_rows(dbt_re), _from_ssm_rows(dbt_im), "ssm_param_bwd")
    small_grads = dict(
        loss=loss_local, q_norm_w=g_qw.reshape(n_q, HEAD_DIM).sum(0), k_norm_w=g_kw.reshape(N_KV_HEADS, HEAD_DIM).sum(0),
        sinks=g_sinks[0, :n_q], A_re=g_a_re, A_im=g_a_im, log_dt=g_log_dt.reshape(n_groups),
        B_re=g_bt_re.transpose(0, 2, 1), B_im=g_bt_im.transpose(0, 2, 1),
        C_re=_from_ssm_rows(dc_re), C_im=_from_ssm_rows(dc_im),
        D_skip=g_d.reshape(n_groups, GROUP), b_glu=g_bglu.reshape(2 * ssm_w))

    n_parts = W_IN_PARTS
    wq = d // n_parts
    g_parts, pair_parts, chip_parts = [], [], []
    extra = [_chip_exchange([pair_glu]), _all_gather([_pack(small_grads, SMALL_REST)])]
    chips_glu = small_parts = dh = None
    for step in range(n_parts + 2):
        riders = list(extra) if step == 0 else []
        if 0 <= step - 2 < n_parts:
            riders.append(_chip_exchange([pair_parts[step - 2]]))
        if 0 <= step - 1 < n_parts:
            riders.append(_sibling_exchange([g_parts[step - 1]]))
        rider = _join(*riders) if riders else None
        if step < n_parts:
            res = _matmul(dproj, Cols(h, step * wq, wq), mode="tn", name="grad_w_in_%d" % step, tk=4096, rider=rider)
            out, landed = res if rider is not None else (res, [])
            g_parts.append(out)
        else:
            q = step - n_parts
            dh, landed = _matmul(dproj, w_in_parts[q], mode="nn", name="d_normed_%d" % q, tk=2176,
                                 out_cols=(d, q * wq), into=dh, rider=rider)
        landed = list(landed)
        if step == 0:
            chips_glu, small_parts = landed[:2]
            landed = landed[2:]
        if 0 <= step - 2 < n_parts:
            chip_parts.append(landed.pop(0))
        if 0 <= step - 1 < n_parts:
            pair_parts.append(_pair_sum(g_parts[step - 1], landed.pop(0), "pair_sum_w_in_%d" % (step - 1)))
    grad_x, g_norm = _rmsnorm_bwd(xs, norm_row, dh, dout, "rmsnorm_bwd")
    (norm_parts,) = _exchange(_all_gather([_pack(dict(norm_w=g_norm), ("norm_w",))]), "gather_norm_grad")
    from_chips = dict(zip(LARGE, (chip_parts, [chips_ap], [chips_glu], [chips_sp], [chips_o])))
    return grad_x, from_chips, small_parts, norm_parts


def kernel(x, norm_w, w_in, q_norm_w, k_norm_w, sinks, w_attn_proj, A_re, A_im, log_dt, B_re, B_im, C_re, C_im, D_skip, w_glu, b_glu, w_ssm_proj, w_out, loss_target, m_norm_w, m_w_in, m_q_norm_w, m_k_norm_w, m_sinks, m_w_attn_proj, m_A_re, m_A_im, m_log_dt, m_B_re, m_B_im, m_C_re, m_C_im, m_D_skip, m_w_glu, m_b_glu, m_w_ssm_proj, m_w_out, v_norm_w, v_w_in, v_q_norm_w, v_k_norm_w, v_sinks, v_w_attn_proj, v_A_re, v_A_im, v_log_dt, v_B_re, v_B_im, v_C_re, v_C_im, v_D_skip, v_w_glu, v_b_glu, v_w_ssm_proj, v_w_out):
    weights = dict(norm_w=norm_w, w_in=w_in, q_norm_w=q_norm_w, k_norm_w=k_norm_w, sinks=sinks,
                   w_attn_proj=w_attn_proj, A_re=A_re, A_im=A_im, log_dt=log_dt, B_re=B_re, B_im=B_im, C_re=C_re,
                   C_im=C_im, D_skip=D_skip, w_glu=w_glu, b_glu=b_glu, w_ssm_proj=w_ssm_proj, w_out=w_out)
    m_in = dict(norm_w=m_norm_w, w_in=m_w_in, q_norm_w=m_q_norm_w, k_norm_w=m_k_norm_w, sinks=m_sinks,
                w_attn_proj=m_w_attn_proj, A_re=m_A_re, A_im=m_A_im, log_dt=m_log_dt, B_re=m_B_re, B_im=m_B_im,
                C_re=m_C_re, C_im=m_C_im, D_skip=m_D_skip, w_glu=m_w_glu, b_glu=m_b_glu, w_ssm_proj=m_w_ssm_proj,
                w_out=m_w_out)
    v_in = dict(norm_w=v_norm_w, w_in=v_w_in, q_norm_w=v_q_norm_w, k_norm_w=v_k_norm_w, sinks=v_sinks,
                w_attn_proj=v_w_attn_proj, A_re=v_A_re, A_im=v_A_im, log_dt=v_log_dt, B_re=v_B_re, B_im=v_B_im,
                C_re=v_C_re, C_im=v_C_im, D_skip=v_D_skip, w_glu=v_w_glu, b_glu=v_b_glu, w_ssm_proj=v_w_ssm_proj,
                w_out=v_w_out)

    _, seq, d = x.shape
    column_sharded = LARGE[:4]
    as_rows = lambda k, a: a.T if k in column_sharded else a
    shards = [as_rows(k, weights[k]).astype(BF16) for k in LARGE]
    small = {k: weights[k] for k in SMALL}
    grad_x, from_chips, small_parts, norm_parts = _step(x.reshape(seq, d), loss_target.reshape(seq, d), small,
                                                        shards)

    grads, delta, new_m, new_v = {}, {}, {}, {}
    for k in LARGE:
        if k == "w_in":
            res = _adamw_chips(weights[k].T, from_chips[k], m_in[k].T, v_in[k].T, "adamw_" + k)
            grads[k], delta[k], new_m[k], new_v[k] = [a.T for a in res]
        elif k == "w_out":
            grads[k], delta[k], new_m[k], new_v[k] = _adamw_chips(weights[k], from_chips[k], m_in[k], v_in[k],
                                                                  "adamw_" + k)
        else:
            grads[k] = _chip_sum(from_chips[k][0], "chip_sum_" + k).T
            delta[k], new_m[k], new_v[k] = _adamw(weights[k], grads[k], m_in[k], v_in[k], "adamw_" + k)

    zero = jnp.zeros((), F32)
    for keys, parts in ((SMALL_REST, small_parts), (("norm_w",), norm_parts)):
        like = dict(small, loss=zero)
        packs = [_pack(dict(src, loss=zero), keys) for src in (weights, m_in, v_in)]
        res = _adamw_small(packs[0], parts, packs[1], packs[2], "adamw_small_%d" % len(keys))
        for dst, r in zip((grads, delta, new_m, new_v), res):
            dst.update(_unpack(r, like, keys))
    loss = grads["loss"]

    return (loss, grad_x.reshape(x.shape), *[grads[k] for k in ORDER], *[delta[k] for k in ORDER],
            *[new_m[k] for k in ORDER], *[new_v[k] for k in ORDER])
```

```python
import math
from typing import Callable, NamedTuple

import jax
import jax.numpy as jnp
import numpy as np
from jax import lax
from jax.experimental import pallas as pl
from jax.experimental.pallas import tpu as pltpu

F32 = jnp.float32
BF16 = jnp.bfloat16
MESH = pl.DeviceIdType.MESH

HEAD_DIM = 64
N_KV_HEADS = 4
GROUP = 16
STATE = 64
BLOCK = 128
NORM_EPS = 1e-6
N_DEV = 8
N_CHIPS = 4
LANES = 128
SUBLANES = 8
MXU_DIM = 256
VMEM_BYTES = 64 * 1024 * 1024
VMEM_CAP = VMEM_BYTES - 8 * 1024 * 1024

ADAM_LR = 0.001
ADAM_B1 = 0.9
ADAM_B2 = 0.999
ADAM_EPS = 1e-08
ADAM_WD = 0.01
ADAM_STEP = 10

GELU_C = math.sqrt(2.0 / math.pi)
GELU_K = 0.044715


def _tile(dim, pref, mult=LANES):
    if dim <= pref:
        return dim
    best = None
    for d in range(mult, pref + 1, mult):
        if dim % d == 0:
            best = d
    assert best is not None, (dim, pref, mult)
    return best


def _params(semantics=None, vmem=None):
    kw = {}
    if semantics is not None:
        kw["dimension_semantics"] = semantics
    if vmem is not None:
        kw["vmem_limit_bytes"] = int(min(VMEM_CAP, max(vmem, 32 * 1024 * 1024)))
    return pltpu.CompilerParams(**kw)


def _nbytes(shape, dtype):
    return math.prod(shape) * jnp.dtype(dtype).itemsize


def _sigmoid(x):
    return 1.0 / (1.0 + jnp.exp(-x))


def _silu(x):
    return x * _sigmoid(x)


def _dsilu(x):
    s = _sigmoid(x)
    return s * (1.0 + x * (1.0 - s))


def _gelu(x):
    return 0.5 * x * (1.0 + jnp.tanh(GELU_C * (x + GELU_K * x * x * x)))


def _dgelu(x):
    t = jnp.tanh(GELU_C * (x + GELU_K * x * x * x))
    return 0.5 * (1.0 + t) + 0.5 * x * (1.0 - t * t) * GELU_C * (1.0 + 3.0 * GELU_K * x * x)


def _dot(a, b, dims):
    return lax.dot_general(a, b, (dims, ((), ())), preferred_element_type=F32)


NN = ((1,), (0,))
NT = ((1,), (1,))
TN = ((0,), (0,))


def _any_spec():
    return pl.BlockSpec(memory_space=pl.ANY)


def _pallas(body, **kw):
    pin = lambda s: pltpu.HBM(s.shape, s.dtype) if isinstance(s, jax.ShapeDtypeStruct) else s
    out_shape = kw.pop("out_shape")
    out_shape = [pin(s) for s in out_shape] if isinstance(out_shape, (list, tuple)) else pin(out_shape)
    call = pl.pallas_call(body, out_shape=out_shape, **kw)

    def run(*operands):
        pinned = [pltpu.with_memory_space_constraint(o, pltpu.HBM) if jnp.issubdtype(o.dtype, jnp.floating) else o
                  for o in operands]
        return call(*pinned)

    return run


class Rider(NamedTuple):
    operands: tuple
    out_shapes: tuple
    sems: tuple
    start: Callable
    finish: Callable


def _all_gather(shards):
    n = len(shards)

    def copies(ins, outs, sems):
        send_sems, recv_sems, local_sems = sems
        x, y, c = lax.axis_index("x"), lax.axis_index("y"), lax.axis_index("c")
        me, sibling = (x, y, c), (x, y, 1 - c)
        chips = [(1 - x, y), (x, 1 - y), (1 - x, 1 - y)]

        def rows(k, px, py, pc):
            r = shards[k].shape[0]
            return outs[k].at[pl.ds((4 * px + 2 * py + pc) * r, r), :]

        def copy(k, s, block, to, src=None):
            return pltpu.make_async_remote_copy(
                src_ref=rows(k, *block) if src is None else src, dst_ref=rows(k, *block),
                send_sem=send_sems.at[7 * k + s], recv_sem=recv_sems.at[7 * k + s],
                device_id=to, device_id_type=MESH)

        mine = [pltpu.make_async_copy(ins[k], rows(k, *me), local_sems.at[k]) for k in range(n)]
        first = []
        for k in range(n):
            first.append(copy(k, 0, me, sibling, src=ins[k]))
            first += [copy(k, 1 + j, me, (*chip, c), src=ins[k]) for j, chip in enumerate(chips)]
        return me, sibling, chips, c, copy, mine, first

    def start(ins, outs, sems):
        *_, mine, first = copies(ins, outs, sems)
        for cp in mine + first:
            cp.start()

    def finish(ins, outs, sems):
        me, sibling, chips, c, copy, mine, first = copies(ins, outs, sems)
        passed = []
        for j, chip in enumerate(chips):
            for k in range(n):
                copy(k, 1 + j, (*chip, c), me).wait_recv()
                fwd = copy(k, 4 + j, (*chip, c), sibling)
                fwd.start()
                passed.append(fwd)
        for k in range(n):
            copy(k, 0, sibling, me).wait_recv()
            for j, chip in enumerate(chips):
                copy(k, 4 + j, (*chip, 1 - c), me).wait_recv()
        for cp in first + passed:
            cp.wait_send()
        for cp in mine:
            cp.wait()

    return Rider(
        tuple(shards),
        tuple(jax.ShapeDtypeStruct((N_DEV * s.shape[0], s.shape[1]), s.dtype) for s in shards),
        (pltpu.SemaphoreType.DMA((7 * n,)), pltpu.SemaphoreType.DMA((7 * n,)), pltpu.SemaphoreType.DMA((n,))),
        start, finish)


def _sibling_exchange(grads):
    n = len(grads)

    def copies(ins, outs, sems):
        send_sems, recv_sems = sems
        x, y, c = lax.axis_index("x"), lax.axis_index("y"), lax.axis_index("c")
        out = []
        for k in range(n):
            r = grads[k].shape[0] // N_DEV
            for j in range(N_CHIPS):
                out.append(pltpu.make_async_remote_copy(
                    src_ref=ins[k].at[pl.ds((2 * j + 1 - c) * r, r), :],
                    dst_ref=outs[k].at[pl.ds(j * r, r), :],
                    send_sem=send_sems.at[N_CHIPS * k + j], recv_sem=recv_sems.at[N_CHIPS * k + j],
                    device_id=(x, y, 1 - c), device_id_type=MESH))
        return out

    def start(ins, outs, sems):
        for cp in copies(ins, outs, sems):
            cp.start()

    def finish(ins, outs, sems):
        for cp in copies(ins, outs, sems):
            cp.wait()

    return Rider(
        tuple(grads), tuple(jax.ShapeDtypeStruct((g.shape[0] // 2, g.shape[1]), g.dtype) for g in grads),
        (pltpu.SemaphoreType.DMA((N_CHIPS * n,)), pltpu.SemaphoreType.DMA((N_CHIPS * n,))), start, finish)


def _chip_exchange(parts):
    n = len(parts)

    def copies(ins, outs, sems):
        send_sems, recv_sems, local_sems = sems
        x, y, c = lax.axis_index("x"), lax.axis_index("y"), lax.axis_index("c")
        my_chip = 2 * x + y
        chips = [(1 - x, y), (x, 1 - y), (1 - x, 1 - y)]
        local, sent = [], []
        for k in range(n):
            r = parts[k].shape[0] // N_CHIPS
            mine = pl.ds(my_chip * r, r)
            local.append(pltpu.make_async_copy(ins[k].at[mine, :], outs[k].at[mine, :], local_sems.at[k]))
            for s, (px, py) in enumerate(chips):
                sent.append(pltpu.make_async_remote_copy(
                    src_ref=ins[k].at[pl.ds((2 * px + py) * r, r), :], dst_ref=outs[k].at[mine, :],
                    send_sem=send_sems.at[3 * k + s], recv_sem=recv_sems.at[3 * k + s],
                    device_id=(px, py, c), device_id_type=MESH))
        return local, sent

    def start(ins, outs, sems):
        local, sent = copies(ins, outs, sems)
        for cp in local + sent:
            cp.start()

    def finish(ins, outs, sems):
        local, sent = copies(ins, outs, sems)
        for cp in sent + local:
            cp.wait()

    return Rider(
        tuple(parts), tuple(jax.ShapeDtypeStruct(p.shape, p.dtype) for p in parts),
        (pltpu.SemaphoreType.DMA((3 * n,)), pltpu.SemaphoreType.DMA((3 * n,)), pltpu.SemaphoreType.DMA((n,))),
        start, finish)


def _join(*riders):
    cuts_in, cuts_out, cuts_sem = [0], [0], [0]
    for r in riders:
        cuts_in.append(cuts_in[-1] + len(r.operands))
        cuts_out.append(cuts_out[-1] + len(r.out_shapes))
        cuts_sem.append(cuts_sem[-1] + len(r.sems))

    def each(which):
        def run(ins, outs, sems):
            for i, r in enumerate(riders):
                getattr(r, which)(ins[cuts_in[i]:cuts_in[i + 1]], outs[cuts_out[i]:cuts_out[i + 1]],
                                  sems[cuts_sem[i]:cuts_sem[i + 1]])
        return run

    return Rider(sum((r.operands for r in riders), ()), sum((r.out_shapes for r in riders), ()),
                 sum((r.sems for r in riders), ()), each("start"), each("finish"))


def _call(body, operands, *, name, out_shape, grid, in_specs, out_specs, scratch_shapes=(), aliases=None,
          semantics=None, vmem=None, rider=None):
    operands, out_shape, scratch_shapes = list(operands), list(out_shape), list(scratch_shapes)
    in_specs, out_specs = list(in_specs), list(out_specs)
    if rider is None:
        res = _pallas(
            body, name=name, out_shape=out_shape, grid=grid, in_specs=in_specs, out_specs=out_specs,
            scratch_shapes=scratch_shapes, input_output_aliases=aliases or {},
            compiler_params=_params(semantics, vmem))(*operands)
        return list(res), []
    n_in, n_out, n_scr = len(operands), len(out_shape), len(scratch_shapes)
    ri, ro = len(rider.operands), len(rider.out_shapes)

    def carried(*refs):
        a, b = n_in, n_in + ri
        c, d = b + n_out, b + n_out + ro
        e = d + n_scr
        ids = [pl.program_id(k) for k in range(len(grid))]
        first = ids[0] == 0
        last = ids[0] == grid[0] - 1
        for k in range(1, len(grid)):
            first = jnp.logical_and(first, ids[k] == 0)
            last = jnp.logical_and(last, ids[k] == grid[k] - 1)

        @pl.when(first)
        def _():
            rider.start(refs[a:b], refs[c:d], refs[e:])

        body(*refs[:a], *refs[b:c], *refs[d:e])

        @pl.when(last)
        def _():
            rider.finish(refs[a:b], refs[c:d], refs[e:])

    res = _pallas(
        carried, name=name, out_shape=out_shape + list(rider.out_shapes), grid=grid,
        in_specs=in_specs + [_any_spec()] * ri, out_specs=out_specs + [_any_spec()] * ro,
        scratch_shapes=scratch_shapes + list(rider.sems), input_output_aliases=aliases or {},
        compiler_params=_params(("arbitrary",) * len(grid), vmem))(*operands, *rider.operands)
    return list(res[:n_out]), list(res[n_out:])


def _exchange(rider, name):
    ri, ro = len(rider.operands), len(rider.out_shapes)

    def body(*refs):
        rider.start(refs[:ri], refs[ri:ri + ro], refs[ri + ro:])
        rider.finish(refs[:ri], refs[ri:ri + ro], refs[ri + ro:])

    return _pallas(
        body, name=name, out_shape=list(rider.out_shapes), in_specs=[_any_spec()] * ri,
        out_specs=[_any_spec()] * ro, scratch_shapes=list(rider.sems))(*rider.operands)


class Cols(NamedTuple):
    arr: jax.Array
    off: int
    width: int


def _cols(a):
    return a if isinstance(a, Cols) else Cols(a, 0, a.shape[1])


def _matmul(a, b, *, mode, name, out_dtype=F32, tm=1024, tn=1024, tk=2048, bias=None, add=None, out_cols=None,
            into=None, rider=None):
    a, b = _cols(a), _cols(b)
    if mode == "nn":
        (m, k), (k2, n) = (a.arr.shape[0], a.width), (b.arr.shape[0], b.width)
    elif mode == "nt":
        (m, k), (n, k2) = (a.arr.shape[0], a.width), (b.arr.shape[0], b.width)
    else:
        (k, m), (k2, n) = (a.arr.shape[0], a.width), (b.arr.shape[0], b.width)
    assert k == k2, (a.arr.shape, b.arr.shape, mode)
    tm, tn, tk = _tile(m, tm), _tile(n, tn), _tile(k, tk)
    nk = k // tk
    dims = {"nn": NN, "nt": NT, "tn": TN}[mode]
    if mode == "tn":
        assert a.off % tm == 0
        a_spec = pl.BlockSpec((tk, tm), lambda i, j, kk, o=a.off // tm: (kk, i + o))
    else:
        assert a.off % tk == 0
        a_spec = pl.BlockSpec((tm, tk), lambda i, j, kk, o=a.off // tk: (i, kk + o))
    if mode == "nt":
        assert b.off % tk == 0
        b_spec = pl.BlockSpec((tn, tk), lambda i, j, kk, o=b.off // tk: (j, kk + o))
    else:
        assert b.off % tn == 0
        b_spec = pl.BlockSpec((tk, tn), lambda i, j, kk, o=b.off // tn: (kk, j + o))
    in_specs, operands = [a_spec, b_spec], [a.arr, b.arr]
    assert bias is None or add is None
    if bias is not None:
        in_specs.append(pl.BlockSpec((1, tn), lambda i, j, kk: (0, j)))
        operands.append(bias)
    if add is not None:
        assert add.shape == (m, n)
        in_specs.append(pl.BlockSpec((tm, tn), lambda i, j, kk: (i, j)))
        operands.append(add)
    total_w, o_off = out_cols if out_cols is not None else (n, 0)
    assert o_off % tn == 0
    aliases = {}
    if into is not None:
        assert into.shape == (m, total_w) and into.dtype == out_dtype
        in_specs.append(_any_spec())
        operands.append(into)
        aliases = {len(operands) - 1: 0}
    n_in = len(operands)

    def body(*refs):
        a_ref, b_ref = refs[0], refs[1]
        bias_ref = refs[2] if bias is not None or add is not None else None
        o_ref = refs[n_in]
        acc_ref = refs[-1] if nk > 1 else None
        part = _dot(a_ref[...].astype(BF16), b_ref[...].astype(BF16), dims)

        def finish(acc):
            if bias_ref is not None:
                acc = acc + bias_ref[...]
            o_ref[...] = acc.astype(out_dtype)

        if nk == 1:
            finish(part)
        else:
            kk = pl.program_id(2)

            @pl.when(kk == 0)
            def _():
                acc_ref[...] = part

            @pl.when(kk > 0)
            def _():
                acc_ref[...] += part

            @pl.when(kk == nk - 1)
            def _():
                finish(acc_ref[...])

    vmem = 2 * (_nbytes((tm, tk), a.arr.dtype) + _nbytes((tk, tn), b.arr.dtype) + _nbytes((tm, tn), out_dtype))
    vmem += 3 * _nbytes((tm, tn), F32)
    (out,), landed = _call(
        body, operands, name=name, out_shape=[jax.ShapeDtypeStruct((m, total_w), out_dtype)],
        grid=(m // tm, n // tn, nk), in_specs=in_specs,
        out_specs=[pl.BlockSpec((tm, tn), lambda i, j, kk, o=o_off // tn: (i, j + o))],
        scratch_shapes=[pltpu.VMEM((tm, tn), F32)] if nk > 1 else [], aliases=aliases,
        semantics=("parallel", "parallel", "arbitrary"), vmem=vmem, rider=rider)
    return out if rider is None else (out, landed)


def _ew(fn, *, name, rows, width, tiles, vecs=(), outs, accs=0, tl=1024, cw=512, into=None, with_col=False):
    tl, cw = _tile(rows, tl, SUBLANES), _tile(width, cw)
    ncol = width // cw
    nt_, nv = len(tiles), len(vecs)
    into = list(into) if into is not None else [None] * len(outs)
    aliased = [t for t in into if t is not None]

    def off(o):
        assert o % cw == 0, (name, o, cw)
        return o // cw

    in_specs, vmem = [], 0
    for t in tiles:
        arr, o = t[0], off(t[1])
        wrap = t[2] // cw if len(t) > 2 else ncol
        in_specs.append(pl.BlockSpec((tl, cw), lambda j, i, o=o, wrap=wrap: (i, o + j % wrap)))
        vmem += _nbytes((tl, cw), arr.dtype)
    in_specs += [pl.BlockSpec((1, cw), lambda j, i, o=off(o): (0, j + o)) for _, o in vecs]
    in_specs += [_any_spec() for _ in aliased]
    out_shape, out_specs, aliases = [], [], {}
    n_in = nt_ + nv
    for idx, ((dt, tw, o), tgt) in enumerate(zip(outs, into)):
        out_shape.append(jax.ShapeDtypeStruct((rows, tw), dt))
        out_specs.append(pl.BlockSpec((tl, cw), lambda j, i, o=off(o): (i, j + o)))
        vmem += _nbytes((tl, cw), dt)
        if tgt is not None:
            assert tgt.shape == (rows, tw) and tgt.dtype == dt, (name, tgt.shape, tgt.dtype)
            aliases[n_in + len(aliases)] = idx
    for _ in range(accs):
        out_shape.append(jax.ShapeDtypeStruct((1, width), F32))
        out_specs.append(pl.BlockSpec((1, cw), lambda j, i: (0, j)))
    n_out = len(outs)

    def body(*refs):
        vals = [r[...].astype(F32) for r in refs[:n_in]]
        out_refs = refs[n_in + len(aliased):]
        res = fn(pl.program_id(0), *vals) if with_col else fn(*vals)
        res = res if isinstance(res, (tuple, list)) else (res,)
        assert len(res) == n_out + accs, (name, len(res))
        for r, v in zip(out_refs[:n_out], res[:n_out]):
            r[...] = v.astype(r.dtype)
        first = pl.program_id(1) == 0
        for r, v in zip(out_refs[n_out:], res[n_out:]):
            s = jnp.sum(v, axis=0, keepdims=True)

            @pl.when(first)
            def _(r=r, s=s):
                r[...] = s

            @pl.when(jnp.logical_not(first))
            def _(r=r, s=s):
                r[...] += s

    return _pallas(
        body, name=name, out_shape=out_shape, grid=(ncol, rows // tl),
        in_specs=in_specs, out_specs=out_specs, input_output_aliases=aliases,
        compiler_params=_params(("parallel", "arbitrary"), 3 * vmem),
    )(*[t[0] for t in tiles], *[v for v, _ in vecs], *aliased)


def _rmsnorm_fwd(x, w_row, name, rider=None):
    rows, d = x.shape
    tl = _tile(rows, 512, SUBLANES)

    def body(x_ref, w_ref, h_ref):
        xv = x_ref[...]
        rstd = lax.rsqrt(jnp.mean(xv * xv, axis=-1, keepdims=True) + NORM_EPS)
        h_ref[...] = (xv * rstd * w_ref[...]).astype(BF16)

    (h,), landed = _call(
        body, [x, w_row], name=name, out_shape=[jax.ShapeDtypeStruct((rows, d), BF16)], grid=(rows // tl,),
        in_specs=[pl.BlockSpec((tl, d), lambda i: (i, 0)), pl.BlockSpec((1, d), lambda i: (0, 0))],
        out_specs=[pl.BlockSpec((tl, d), lambda i: (i, 0))], semantics=("parallel",), rider=rider)
    return h if rider is None else (h, landed)


def _rmsnorm_bwd(x, w_row, dh, dout, name, rider=None):
    rows, d = x.shape
    tl = _tile(rows, 256, SUBLANES)

    def body(x_ref, w_ref, dh_ref, dout_ref, gx_ref, gw_ref):
        xv = x_ref[...]
        rstd = lax.rsqrt(jnp.mean(xv * xv, axis=-1, keepdims=True) + NORM_EPS)
        xn = xv * rstd
        dhv = dh_ref[...]
        dxn = dhv * w_ref[...]
        dx = rstd * (dxn - xn * jnp.mean(dxn * xn, axis=-1, keepdims=True))
        gx_ref[...] = dout_ref[...] + dx
        gw = jnp.sum(dhv * xn, axis=0, keepdims=True)

        @pl.when(pl.program_id(0) == 0)
        def _():
            gw_ref[...] = gw

        @pl.when(pl.program_id(0) > 0)
        def _():
            gw_ref[...] += gw

    tile = pl.BlockSpec((tl, d), lambda i: (i, 0))
    row = pl.BlockSpec((1, d), lambda i: (0, 0))
    res, landed = _call(
        body, [x, w_row, dh, dout], name=name,
        out_shape=[jax.ShapeDtypeStruct((rows, d), F32), jax.ShapeDtypeStruct((1, d), F32)],
        grid=(rows // tl,), in_specs=[tile, row, tile, tile], out_specs=[tile, row],
        semantics=("arbitrary",), rider=rider)
    return res if rider is None else (res, landed)


def _head_mean(x, gmat):
    hi = x.astype(BF16)
    lo = (x - hi.astype(F32)).astype(BF16)
    out = []
    for s in range(x.shape[1] // MXU_DIM):
        sl = slice(s * MXU_DIM, (s + 1) * MXU_DIM)
        out.append(_dot(hi[:, sl], gmat, NN) + _dot(lo[:, sl], gmat, NN))
    return out[0] if len(out) == 1 else jnp.concatenate(out, axis=1)


def _head_mean_matrix():
    blk = jnp.arange(MXU_DIM) // HEAD_DIM
    return jnp.where(blk[:, None] == blk[None, :], 1.0 / HEAD_DIM, 0.0).astype(BF16)


def _spread_head(x, g, width):
    col = x[:, (g // 2) * LANES:(g // 2 + 1) * LANES]
    other = pltpu.roll(col, HEAD_DIM, axis=1)
    low = lax.broadcasted_iota(jnp.int32, col.shape, 1) < HEAD_DIM
    both = jnp.where(low, col, other) if g % 2 == 0 else jnp.where(low, other, col)
    return both if width == LANES else jnp.concatenate([both] * (width // LANES), axis=1)


def _head_diagonal(t, per_kv):
    head = lax.broadcasted_iota(jnp.int32, t.shape, 1) // HEAD_DIM
    zero = jnp.zeros_like(t)
    return jnp.concatenate([jnp.where(head == r, t, zero) for r in range(per_kv)], axis=0)


def _fold_heads(x, per_kv):
    rows = x.shape[0] // per_kv
    head = lax.broadcasted_iota(jnp.int32, (rows, x.shape[1]), 1) // HEAD_DIM
    acc = jnp.where(head == 0, x[0:rows], 0.0)
    for r in range(1, per_kv):
        acc = acc + jnp.where(head == r, x[r * rows:(r + 1) * rows], 0.0)
    while acc.shape[1] > LANES:
        half = acc.shape[1] // 2
        acc = acc[:, :half] + acc[:, half:]
    return acc + pltpu.roll(acc, HEAD_DIM, axis=1)


def _join_heads(parts):
    low = lax.broadcasted_iota(jnp.int32, parts[0].shape, 1) < HEAD_DIM
    cols = [jnp.where(low, parts[2 * j], parts[2 * j + 1]) for j in range(len(parts) // 2)]
    return cols[0] if len(cols) == 1 else jnp.concatenate(cols, axis=1)


def _attn_specs(attn_w, kv_w):
    half = attn_w // 2
    kcol, vcol = attn_w // kv_w, attn_w // kv_w + 1
    gcol = (attn_w + 2 * kv_w) // half
    prev = lambda i: jnp.maximum(i - 1, 0)
    return [
        pl.BlockSpec((BLOCK, attn_w), lambda i: (i, 0)),
        pl.BlockSpec((BLOCK, kv_w), lambda i: (prev(i), kcol)),
        pl.BlockSpec((BLOCK, kv_w), lambda i: (i, kcol)),
        pl.BlockSpec((BLOCK, kv_w), lambda i: (prev(i), vcol)),
        pl.BlockSpec((BLOCK, kv_w), lambda i: (i, vcol)),
        pl.BlockSpec((BLOCK, half), lambda i: (i, gcol)),
        pl.BlockSpec((BLOCK, half), lambda i: (i, gcol + 1)),
    ]


def _band_mask(i):
    q_loc = lax.broadcasted_iota(jnp.int32, (BLOCK, 2 * BLOCK), 0) + BLOCK
    k_loc = lax.broadcasted_iota(jnp.int32, (BLOCK, 2 * BLOCK), 1)
    diff = q_loc - k_loc
    first_key = jnp.where(i == 0, BLOCK, 0)
    return (diff >= 0) & (diff < BLOCK) & (k_loc >= first_key)


def _softmax_with_sink(s, sink):
    m = jnp.maximum(jnp.max(s, axis=-1, keepdims=True), sink)
    p = jnp.exp(s - m)
    e_sink = jnp.exp(sink - m)
    den = jnp.sum(p, axis=-1, keepdims=True) + e_sink
    inv = 1.0 / den
    return p * inv, e_sink * inv


def _attn_block(i, q, kk, vv, qw, kw, gmat, sink_ref, per_kv):
    scale = 1.0 / math.sqrt(HEAD_DIM)
    keys = 2 * BLOCK
    valid = _band_mask(i)
    q_rstd = lax.rsqrt(_head_mean(q * q, gmat) + NORM_EPS)
    qn = q * q_rstd
    qh = (qn * qw).astype(BF16)
    k_rstd = lax.rsqrt(_head_mean(kk * kk, gmat) + NORM_EPS)
    kn = kk * k_rstd
    kh = kn * kw
    gw = per_kv * HEAD_DIM
    groups = []
    for g in range(N_KV_HEADS):
        kd = _head_diagonal(_spread_head(kh, g, gw).astype(BF16), per_kv)
        vd = _head_diagonal(_spread_head(vv, g, gw).astype(BF16), per_kv)
        qg = qh[:, g * gw:(g + 1) * gw]
        s_all = _dot(qg, kd, NT) * scale
        ps, p_sinks = [], []
        for r in range(per_kv):
            s = jnp.where(valid, s_all[:, r * keys:(r + 1) * keys], -1e30)
            p, p_sink = _softmax_with_sink(s, sink_ref[g * per_kv + r])
            ps.append(p)
            p_sinks.append(p_sink)
        pb = jnp.concatenate(ps, axis=1).astype(BF16)
        groups.append((kd, vd, qg, ps, p_sinks, pb, _dot(pb, vd, NN)))
    return qn, q_rstd, kn, k_rstd, groups


def _attention_fwd(proj, qw_row, kw_row, gmat, sinks, *, attn_w, kv_w, name):
    rows = proj.shape[0]
    per_kv = attn_w // HEAD_DIM // N_KV_HEADS

    def body(q_ref, kp_ref, kc_ref, vp_ref, vc_ref, glo_ref, ghi_ref, qw_ref, kw_ref, gm_ref, sink_ref, o_ref):
        kk = jnp.concatenate([kp_ref[...], kc_ref[...]], axis=0).astype(F32)
        vv = jnp.concatenate([vp_ref[...], vc_ref[...]], axis=0).astype(F32)
        gate = jnp.concatenate([glo_ref[...], ghi_ref[...]], axis=1).astype(F32)
        *_, groups = _attn_block(pl.program_id(0), q_ref[...].astype(F32), kk, vv, qw_ref[...], kw_ref[...], gm_ref[...],
                                 sink_ref, per_kv)
        attn = jnp.concatenate([grp[-1] for grp in groups], axis=1)
        o_ref[...] = (attn * _silu(gate)).astype(BF16)

    const = lambda a: pl.BlockSpec(a.shape, lambda i: (0, 0))
    return _pallas(
        body, name=name, out_shape=jax.ShapeDtypeStruct((rows, attn_w), BF16), grid=(rows // BLOCK,),
        in_specs=_attn_specs(attn_w, kv_w) + [const(qw_row), const(kw_row), const(gmat),
                                              pl.BlockSpec(memory_space=pltpu.SMEM)],
        out_specs=pl.BlockSpec((BLOCK, attn_w), lambda i: (i, 0)),
        compiler_params=_params(("parallel",), 40 * 1024 * 1024),
    )(proj, proj, proj, proj, proj, proj, proj, qw_row, kw_row, gmat, sinks)


def _attention_bwd(proj, d_ag, dproj, qw_row, kw_row, gmat, sinks, *, attn_w, kv_w, name, rider=None):
    rows = proj.shape[0]
    nb = rows // BLOCK
    per_kv = attn_w // HEAD_DIM // N_KV_HEADS
    gw = per_kv * HEAD_DIM
    keys = 2 * BLOCK
    scale = 1.0 / math.sqrt(HEAD_DIM)
    w_out = 2 * attn_w + 2 * kv_w

    def body(q_ref, kp_ref, kc_ref, vp_ref, vc_ref, glo_ref, ghi_ref, dag_ref, qw_ref, kw_ref, gm_ref, sink_ref, _,
             dp_ref, dkv_ref, gqw_ref, gkw_ref, gs_ref):
        i = pl.program_id(0)
        kk = jnp.concatenate([kp_ref[...], kc_ref[...]], axis=0).astype(F32)
        vv = jnp.concatenate([vp_ref[...], vc_ref[...]], axis=0).astype(F32)
        gate = jnp.concatenate([glo_ref[...], ghi_ref[...]], axis=1).astype(F32)
        d_ag_v = dag_ref[...].astype(F32)
        qw, kw, gmat_v = qw_ref[...], kw_ref[...], gm_ref[...]
        qn, q_rstd, kn, k_rstd, groups = _attn_block(i, q_ref[...].astype(F32), kk, vv, qw, kw, gmat_v, sink_ref,
                                                     per_kv)
        lane = lax.broadcasted_iota(jnp.int32, (SUBLANES, LANES), 1)
        sub = lax.broadcasted_iota(jnp.int32, (SUBLANES, LANES), 0)
        gsink = jnp.zeros((SUBLANES, LANES), F32)
        dq_groups, dgate_groups, dk_heads, dv_heads = [], [], [], []
        for g, (kd, vd, qg, ps, p_sinks, pb, o) in enumerate(groups):
            cs = slice(g * gw, (g + 1) * gw)
            gate_g, d_ag_g = gate[:, cs], d_ag_v[:, cs]
            dgate_groups.append(d_ag_g * o * _dsilu(gate_g))
            do = (d_ag_g * _silu(gate_g)).astype(BF16)
            dp_all = _dot(do, vd, NT)
            dss = []
            for r in range(per_kv):
                p, dp = ps[r], dp_all[:, r * keys:(r + 1) * keys]
                delta = jnp.sum(p * dp, axis=-1, keepdims=True)
                dss.append(p * (dp - delta) * scale)
                gs_h = jnp.sum(-p_sinks[r] * delta, axis=0, keepdims=True)
                gsink = gsink + jnp.where((lane == g * per_kv + r) & (sub == 0), gs_h, 0.0)
            ds = jnp.concatenate(dss, axis=1).astype(BF16)
            dq_groups.append(_dot(ds, kd, NN))
            dk_heads.append(_fold_heads(_dot(ds, qg, TN), per_kv))
            dv_heads.append(_fold_heads(_dot(pb, do, TN), per_kv))
        dqh = jnp.concatenate(dq_groups, axis=1)
        gqw = jnp.sum(dqh * qn, axis=0, keepdims=True)
        dqn = dqh * qw
        dq = q_rstd * (dqn - qn * _head_mean(dqn * qn, gmat_v))
        dkh = _join_heads(dk_heads)
        gkw = jnp.sum(dkh * kn, axis=0, keepdims=True)
        dkn = dkh * kw
        dk = k_rstd * (dkn - kn * _head_mean(dkn * kn, gmat_v))
        dp_ref[:, 0:attn_w] = dq.astype(BF16)
        dp_ref[:, attn_w:attn_w + 2 * kv_w] = jnp.zeros((BLOCK, 2 * kv_w), BF16)
        dp_ref[:, attn_w + 2 * kv_w:w_out] = jnp.concatenate(dgate_groups, axis=1).astype(BF16)
        dkv_ref[0] = jnp.concatenate([dk, _join_heads(dv_heads)], axis=1)

        @pl.when(i == 0)
        def _():
            gqw_ref[...] = gqw
            gkw_ref[...] = gkw
            gs_ref[...] = gsink

        @pl.when(i > 0)
        def _():
            gqw_ref[...] += gqw
            gkw_ref[...] += gkw
            gs_ref[...] += gsink

    const = lambda a: pl.BlockSpec(a.shape, lambda i: (0, 0))
    res, landed = _call(
        body, [proj, proj, proj, proj, proj, proj, proj, d_ag, qw_row, kw_row, gmat, sinks, dproj], name=name,
        out_shape=[jax.ShapeDtypeStruct(dproj.shape, BF16),
                   jax.ShapeDtypeStruct((nb, 2 * BLOCK, 2 * kv_w), F32),
                   jax.ShapeDtypeStruct(qw_row.shape, F32), jax.ShapeDtypeStruct(kw_row.shape, F32),
                   jax.ShapeDtypeStruct((SUBLANES, LANES), F32)],
        grid=(nb,),
        in_specs=_attn_specs(attn_w, kv_w) + [pl.BlockSpec((BLOCK, attn_w), lambda i: (i, 0)), const(qw_row),
                                              const(kw_row), const(gmat), pl.BlockSpec(memory_space=pltpu.SMEM),
                                              _any_spec()],
        out_specs=[pl.BlockSpec((BLOCK, w_out), lambda i: (i, 0)),
                   pl.BlockSpec((1, 2 * BLOCK, 2 * kv_w), lambda i: (i, 0, 0)),
                   const(qw_row), const(kw_row), pl.BlockSpec((SUBLANES, LANES), lambda i: (0, 0))],
        aliases={12: 0}, semantics=("arbitrary",), vmem=48 * 1024 * 1024, rider=rider)
    return res if rider is None else (res, landed)


def _attention_dkv(dproj, dkv, *, attn_w, kv_w, name):
    rows = dproj.shape[0]
    nb = rows // BLOCK
    col = attn_w // (2 * kv_w)

    def body(cur_ref, nxt_ref, _, o_ref):
        i = pl.program_id(0)
        nxt = jnp.where(i < nb - 1, nxt_ref[0, 0:BLOCK, :], 0.0)
        o_ref[...] = (cur_ref[0, BLOCK:2 * BLOCK, :] + nxt).astype(BF16)

    blk = lambda f: pl.BlockSpec((1, 2 * BLOCK, 2 * kv_w), f)
    return _pallas(
        body, name=name, out_shape=jax.ShapeDtypeStruct(dproj.shape, BF16), grid=(nb,),
        in_specs=[blk(lambda i: (i, 0, 0)), blk(lambda i: (jnp.minimum(i + 1, nb - 1), 0, 0)), _any_spec()],
        out_specs=pl.BlockSpec((BLOCK, 2 * kv_w), lambda i: (i, col)),
        input_output_aliases={2: 0},
        compiler_params=_params(("parallel",)),
    )(dkv, dkv, dproj)


def _cmul(ar, ai, br, bi):
    return ar * br - ai * bi, ar * bi + ai * br


def _ssm_prep(a_re, a_im, log_dt_col, steps, name):
    def body(are_ref, aim_ref, ldt_ref, abr_ref, abi_ref, cfr_ref, cfi_ref, apr_ref, api_ref, pwr_ref, pwi_ref):
        are, aim = are_ref[...], aim_ref[...]
        dt = jnp.exp(ldt_ref[...])
        mag = jnp.exp(dt * are)
        abr = mag * jnp.cos(dt * aim)
        abi = mag * jnp.sin(dt * aim)
        num_re, num_im = abr - 1.0, abi
        den = are * are + aim * aim
        abr_ref[...] = abr
        abi_ref[...] = abi
        cfr_ref[...] = (num_re * are + num_im * aim) / den
        cfi_ref[...] = (num_im * are - num_re * aim) / den
        pr, pi = jnp.ones_like(abr), jnp.zeros_like(abr)
        for k in range(steps):
            pwr_ref[k] = pr
            pwi_ref[k] = pi
            pr, pi = _cmul(pr, pi, abr, abi)
        apr_ref[...] = pr
        api_ref[...] = pi

    shp = jax.ShapeDtypeStruct(a_re.shape, F32)
    pows = jax.ShapeDtypeStruct((steps,) + a_re.shape, F32)
    return _pallas(body, name=name, out_shape=[shp] * 6 + [pows] * 2)(a_re, a_im, log_dt_col)


def _ssm_param_bwd(a_re, a_im, log_dt_col, d_ab_re, d_ab_im, b_re, b_im, dbt_re, dbt_im, name):
    def body(are_ref, aim_ref, ldt_ref, gabr_ref, gabi_ref, br_ref, bi_ref, tr_ref, ti_ref,
             dar_ref, dai_ref, dldt_ref, dbr_ref, dbi_ref):
        are, aim = are_ref[...], aim_ref[...]
        dt = jnp.exp(ldt_ref[...])
        mag = jnp.exp(dt * are)
        abr = mag * jnp.cos(dt * aim)
        abi = mag * jnp.sin(dt * aim)
        den = are * are + aim * aim
        cfr = ((abr - 1.0) * are + abi * aim) / den
        cfi = (abi * are - (abr - 1.0) * aim) / den
        gabr, gabi = jnp.sum(gabr_ref[...], axis=0), jnp.sum(gabi_ref[...], axis=0)
        t_re, t_im = tr_ref[...], ti_ref[...]
        g_r, g_i = _cmul(br_ref[...], -bi_ref[...], t_re, t_im)
        gcfr, gcfi = jnp.sum(g_r, axis=1), jnp.sum(g_i, axis=1)
        dbr, dbi = _cmul(cfr[:, None, :], -cfi[:, None, :], t_re, t_im)
        dbr_ref[...] = dbr
        dbi_ref[...] = dbi
        inv_r, inv_i = are / den, -aim / den
        t_r, t_i = _cmul(inv_r, -inv_i, gcfr, gcfi)
        gabr, gabi = gabr + t_r, gabi + t_i
        q_r, q_i = _cmul(cfr, cfi, inv_r, inv_i)
        da_r, da_i = _cmul(-q_r, q_i, gcfr, gcfi)
        gz_r, gz_i = _cmul(abr, -abi, gabr, gabi)
        dar_ref[...] = da_r + dt * gz_r
        dai_ref[...] = da_i + dt * gz_i
        dldt_ref[...] = dt * jnp.sum(are * gz_r + aim * gz_i, axis=-1, keepdims=True)

    shp = jax.ShapeDtypeStruct(a_re.shape, F32)
    bshp = jax.ShapeDtypeStruct(b_re.shape, F32)
    return _pallas(body, name=name,
                   out_shape=[shp, shp, jax.ShapeDtypeStruct(log_dt_col.shape, F32), bshp, bshp])(
        a_re, a_im, log_dt_col, d_ab_re, d_ab_im, b_re, b_im, dbt_re, dbt_im)


SCAN_LANES = 512
SSM_CHUNK = 256
W_IN_PARTS = 2


def _scan_segments(xr_ref, xi_ref, a_re, a_im, ap_re, ap_im, pw_re, pw_im, carry_re, carry_im, cm_re, cm_im, steps,
                   reverse, base):
    n = xr_ref.shape[1]
    seg_order = range(SUBLANES - 1, -1, -1) if reverse else range(SUBLANES)
    sign = -1.0 if reverse else 1.0
    for c0 in range(0, n, SCAN_LANES):
        ls = slice(c0, c0 + SCAN_LANES)
        gs = slice(base + c0, base + c0 + SCAN_LANES)
        ar = jnp.broadcast_to(a_re[:, gs], (SUBLANES, SCAN_LANES))
        ai = jnp.broadcast_to(a_im[:, gs], (SUBLANES, SCAN_LANES))
        end_r = jnp.zeros((SUBLANES, SCAN_LANES), F32)
        end_i = jnp.zeros((SUBLANES, SCAN_LANES), F32)
        for j in range(steps):
            k = j if reverse else steps - 1 - j
            rws = slice(j * SUBLANES, (j + 1) * SUBLANES)
            tr, ti = _cmul(pw_re[k:k + 1, gs], sign * pw_im[k:k + 1, gs], xr_ref[rws, ls], xi_ref[rws, ls])
            end_r, end_i = end_r + tr, end_i + ti
        cr, ci = carry_re[:, gs], carry_im[:, gs]
        apr, api = ap_re[:, gs], ap_im[:, gs]
        for r in seg_order:
            cm_re[r:r + 1, gs] = cr
            cm_im[r:r + 1, gs] = ci
            tr, ti = _cmul(apr, api, cr, ci)
            cr, ci = end_r[r:r + 1, :] + tr, end_i[r:r + 1, :] + ti
        carry_re[:, gs] = cr
        carry_im[:, gs] = ci

        def run(t, s, ar=ar, ai=ai, ls=ls):
            j = steps - 1 - t if reverse else t
            r0 = pl.multiple_of(j * SUBLANES, SUBLANES)
            sr, si = _cmul(ar, ai, s[0], s[1])
            sr = sr + xr_ref[pl.ds(r0, SUBLANES), ls]
            si = si + xi_ref[pl.ds(r0, SUBLANES), ls]
            xr_ref[pl.ds(r0, SUBLANES), ls] = sr
            xi_ref[pl.ds(r0, SUBLANES), ls] = si
            return sr, si

        lax.fori_loop(0, steps, run, (cm_re[:, gs], cm_im[:, gs]))


SB_GROUPS = MXU_DIM // GROUP
SB_STATE = SB_GROUPS * STATE


def _ssm_rows(m):
    flat = m.reshape(-1, STATE).astype(F32)
    return jnp.concatenate([flat, flat], axis=1)


def _from_ssm_rows(rows):
    return rows[:, :STATE].reshape(-1, GROUP, STATE)


def _own_group(shape):
    row_g = lax.broadcasted_iota(jnp.int32, shape, 0) // GROUP
    col_g = lax.broadcasted_iota(jnp.int32, shape, 1) // STATE
    return row_g == col_g


def _block_diagonal(rows):
    tiled = jnp.concatenate([rows] * (SB_STATE // LANES), axis=1)
    return jnp.where(_own_group(tiled.shape), tiled, 0.0).astype(BF16)


def _block_rows(acc):
    x = jnp.where(_own_group(acc.shape), acc, 0.0)
    while x.shape[1] > LANES:
        half = x.shape[1] // 2
        x = x[:, :half] + x[:, half:]
    return x + pltpu.roll(x, STATE, axis=1)


def _rows_to_segments(dst, srcs, steps, stage):
    for ref, off in srcs:
        for k in range(ref.shape[1] // LANES):
            stage[off // LANES + k] = ref[:, k * LANES:(k + 1) * LANES].astype(F32)
    for k in range(dst.shape[1] // LANES):
        for j in range(steps):
            dst[j * SUBLANES:(j + 1) * SUBLANES, k * LANES:(k + 1) * LANES] = (
                stage[k, pl.ds(j, SUBLANES, stride=steps), :])


def _segments_to_rows(dst, src, steps, stage):
    for k in range(src.shape[1] // LANES):
        for j in range(steps):
            stage[k, pl.ds(j, SUBLANES, stride=steps), :] = (
                src[j * SUBLANES:(j + 1) * SUBLANES, k * LANES:(k + 1) * LANES])
    for k in range(src.shape[1] // LANES):
        dst[:, k * LANES:(k + 1) * LANES] = stage[k]


def _u_specs(w, o_u, chunk, index):
    half = w // 2
    assert o_u % half == 0
    return [pl.BlockSpec((chunk, half), lambda c, k=k: (index(c), o_u // half + k)) for k in range(2)]


def _ssm_fwd(proj, o_u, bc_rows, rows_p, d_row, *, chunk, name, rider=None):
    rows = proj.shape[0]
    w = d_row.shape[1]
    nc = rows // chunk
    steps = chunk // SUBLANES
    nsb = w // MXU_DIM
    n_state = nsb * SB_STATE

    def body(ulo_ref, uhi_ref, b2r_ref, b2i_ref, c2r_ref, c2i_ref, abr_ref, abi_ref, cfr_ref, cfi_ref, apr_ref,
             api_ref, pwr_ref, pwi_ref, d_ref, y_ref, str_ref, sti_ref, yg_ref, bre_ref, bim_ref, cre_ref, cim_ref,
             useg, yseg, stage, sr, si,
             carry_r, carry_i, cm_r, cm_i):
        @pl.when(pl.program_id(0) == 0)
        def _():
            for src, dst in ((b2r_ref, bre_ref), (b2i_ref, bim_ref), (c2r_ref, cre_ref), (c2i_ref, cim_ref)):
                for sb in range(nsb):
                    dst[sb] = _block_diagonal(src[sb * MXU_DIM:(sb + 1) * MXU_DIM, :])
            carry_r[...] = jnp.zeros_like(carry_r)
            carry_i[...] = jnp.zeros_like(carry_i)

        str_ref[0] = carry_r[...]
        sti_ref[0] = carry_i[...]
        _rows_to_segments(useg, [(ulo_ref, 0), (uhi_ref, w // 2)], steps, stage)
        for sb in range(nsb):
            us = slice(sb * MXU_DIM, (sb + 1) * MXU_DIM)
            ss = slice(sb * SB_STATE, (sb + 1) * SB_STATE)
            ub = useg[:, us].astype(BF16)
            bur = _dot(ub, bre_ref[sb], NN)
            bui = _dot(ub, bim_ref[sb], NN)
            xr, xi = _cmul(cfr_ref[:, ss], cfi_ref[:, ss], bur, bui)
            sr[...] = xr
            si[...] = xi
            _scan_segments(sr, si, abr_ref[...], abi_ref[...], apr_ref[...], api_ref[...], pwr_ref, pwi_ref,
                           carry_r, carry_i, cm_r, cm_i, steps, False, sb * SB_STATE)
            y = _dot(sr[...].astype(BF16), cre_ref[sb], NT) - _dot(si[...].astype(BF16), cim_ref[sb], NT)
            yseg[:, us] = y + d_ref[:, us] * useg[:, us]
        _segments_to_rows(y_ref, yseg, steps, stage)
        yg_ref[...] = _gelu(y_ref[...]).astype(BF16)

    const = lambda a: pl.BlockSpec(a.shape, lambda c: (0,) * a.ndim)
    row_n = pl.BlockSpec((1, n_state), lambda c: (0, 0))
    st = pl.BlockSpec((1, 1, n_state), lambda c: (c, 0, 0))
    held = [pltpu.VMEM((nsb, MXU_DIM, SB_STATE), BF16)] * 4
    vmem = (4 * _nbytes((nsb, MXU_DIM, SB_STATE), BF16) + 4 * _nbytes((chunk, SB_STATE), F32)
            + 12 * _nbytes((chunk, w), F32) + 8 * _nbytes(bc_rows[0].shape, F32))
    res, landed = _call(
        body, [proj, proj, *bc_rows, *rows_p, d_row], name=name,
        out_shape=[jax.ShapeDtypeStruct((rows, w), F32), jax.ShapeDtypeStruct((nc, 1, n_state), F32),
                   jax.ShapeDtypeStruct((nc, 1, n_state), F32), jax.ShapeDtypeStruct((rows, w), BF16)],
        grid=(nc,),
        in_specs=_u_specs(w, o_u, chunk, lambda c: c) + [const(b) for b in bc_rows]
        + [row_n] * 6 + [pl.BlockSpec((steps, n_state), lambda c: (0, 0))] * 2 + [pl.BlockSpec((1, w), lambda c: (0, 0))],
        out_specs=[pl.BlockSpec((chunk, w), lambda c: (c, 0)), st, st, pl.BlockSpec((chunk, w), lambda c: (c, 0))],
        scratch_shapes=held + [pltpu.VMEM((chunk, w), F32), pltpu.VMEM((chunk, w), F32),
                               pltpu.VMEM((w // LANES, chunk, LANES), F32),
                               pltpu.VMEM((chunk, SB_STATE), F32), pltpu.VMEM((chunk, SB_STATE), F32),
                               pltpu.VMEM((1, n_state), F32), pltpu.VMEM((1, n_state), F32),
                               pltpu.VMEM((SUBLANES, n_state), F32), pltpu.VMEM((SUBLANES, n_state), F32)],
        semantics=("arbitrary",), vmem=vmem, rider=rider)
    return res if rider is None else (res, landed)


def _ssm_bwd(proj, o_u, y, dyg, st_re, st_im, bc_rows, rows_p, d_row, *, chunk, name, rider=None):
    rows = proj.shape[0]
    w = d_row.shape[1]
    nc = rows // chunk
    steps = chunk // SUBLANES
    nsb = w // MXU_DIM
    n_state = nsb * SB_STATE

    def body(ulo_ref, uhi_ref, y_ref, dyg_ref, str_ref, sti_ref, b2r_ref, b2i_ref, c2r_ref, c2i_ref, t2r_ref,
             t2i_ref, abr_ref, abi_ref, cfr_ref, cfi_ref, apr_ref, api_ref, pwr_ref, pwi_ref, d_ref,
             du_ref, gb2r_ref, gb2i_ref, gc2r_ref, gc2i_ref, gabr_ref, gabi_ref, dd_ref,
             bre_ref, bim_ref, cre_ref, cim_ref, btr_ref, bti_ref, dbre_ref, dbim_ref, dcre_ref, dcim_ref,
             useg, dyseg, dynat, stage, sr, si, lr, li, carry_r, carry_i, lam_r, lam_i, cm_r, cm_i, cl_r, cl_i):
        first = pl.program_id(0) == 0

        @pl.when(first)
        def _():
            for src, dst in ((b2r_ref, bre_ref), (b2i_ref, bim_ref), (c2r_ref, cre_ref), (c2i_ref, cim_ref),
                             (t2r_ref, btr_ref), (t2i_ref, bti_ref)):
                for sb in range(nsb):
                    dst[sb] = _block_diagonal(src[sb * MXU_DIM:(sb + 1) * MXU_DIM, :])
            lam_r[...] = jnp.zeros_like(lam_r)
            lam_i[...] = jnp.zeros_like(lam_i)
            for ref in (dbre_ref, dbim_ref, dcre_ref, dcim_ref, gabr_ref, gabi_ref, dd_ref):
                ref[...] = jnp.zeros_like(ref)

        dynat[...] = dyg_ref[...].astype(F32) * _dgelu(y_ref[...])
        half = w // 2
        dd_ref[:, :half] += jnp.sum(dynat[:, :half] * ulo_ref[...].astype(F32), axis=0, keepdims=True)
        dd_ref[:, half:] += jnp.sum(dynat[:, half:] * uhi_ref[...].astype(F32), axis=0, keepdims=True)
        _rows_to_segments(useg, [(ulo_ref, 0), (uhi_ref, half)], steps, stage)
        _rows_to_segments(dyseg, [(dynat, 0)], steps, stage)
        dy = dyseg[...]
        dyb = dy.astype(BF16)
        ub = useg[...].astype(BF16)
        carry_r[...] = str_ref[0]
        carry_i[...] = sti_ref[0]
        abr, abi = abr_ref[...], abi_ref[...]
        apr, api = apr_ref[...], api_ref[...]
        for sb in range(nsb):
            us = slice(sb * MXU_DIM, (sb + 1) * MXU_DIM)
            ss = slice(sb * SB_STATE, (sb + 1) * SB_STATE)
            base = sb * SB_STATE
            br = _dot(ub[:, us], bre_ref[sb], NN)
            bi = _dot(ub[:, us], bim_ref[sb], NN)
            xr, xi = _cmul(cfr_ref[:, ss], cfi_ref[:, ss], br, bi)
            sr[...] = xr
            si[...] = xi
            lr[...] = _dot(dyb[:, us], cre_ref[sb], NN)
            li[...] = -_dot(dyb[:, us], cim_ref[sb], NN)
            _scan_segments(sr, si, abr, abi, apr, api, pwr_ref, pwi_ref, carry_r, carry_i, cm_r, cm_i, steps, False,
                           base)
            dcre_ref[sb] += _dot(dyb[:, us], sr[...].astype(BF16), TN)
            dcim_ref[sb] -= _dot(dyb[:, us], si[...].astype(BF16), TN)
            _scan_segments(lr, li, abr, -abi, apr, -api, pwr_ref, pwi_ref, lam_r, lam_i, cl_r, cl_i, steps, True,
                           base)
            for c0 in range(0, SB_STATE, SCAN_LANES):
                ls = slice(c0, c0 + SCAN_LANES)
                gs = slice(base + c0, base + c0 + SCAN_LANES)

                def step(j, acc, ls=ls):
                    gar, gai, pr, pi = acc
                    r0 = pl.multiple_of(j * SUBLANES, SUBLANES)
                    rws = pl.ds(r0, SUBLANES)
                    t_r, t_i = _cmul(pr, -pi, lr[rws, ls], li[rws, ls])
                    return gar + t_r, gai + t_i, sr[rws, ls], si[rws, ls]

                zero = jnp.zeros((SUBLANES, SCAN_LANES), F32)
                gar, gai, _, _ = lax.fori_loop(0, steps, step, (zero, zero, cm_r[:, gs], cm_i[:, gs]))
                gabr_ref[:, gs] += gar
                gabi_ref[:, gs] += gai
            xr, xi = lr[...].astype(BF16), li[...].astype(BF16)
            du = _dot(xr, btr_ref[sb], NT) + _dot(xi, bti_ref[sb], NT)
            useg[:, us] = du + d_ref[:, us] * dy[:, us]
            dbre_ref[sb] += _dot(ub[:, us], xr, TN)
            dbim_ref[sb] += _dot(ub[:, us], xi, TN)
        _segments_to_rows(du_ref, useg, steps, stage)

        @pl.when(pl.program_id(0) == nc - 1)
        def _():
            for src, dst in ((dbre_ref, gb2r_ref), (dbim_ref, gb2i_ref), (dcre_ref, gc2r_ref), (dcim_ref, gc2i_ref)):
                for sb in range(nsb):
                    dst[sb * MXU_DIM:(sb + 1) * MXU_DIM, :] = _block_rows(src[sb])

    rev = lambda c: nc - 1 - c
    const = lambda a: pl.BlockSpec(a.shape, lambda c: (0,) * a.ndim)
    tile = pl.BlockSpec((chunk, w), lambda c: (rev(c), 0))
    row_n = pl.BlockSpec((1, n_state), lambda c: (0, 0))
    row_w = pl.BlockSpec((1, w), lambda c: (0, 0))
    st = pl.BlockSpec((1, 1, n_state), lambda c: (rev(c), 0, 0))
    acc8 = pl.BlockSpec((SUBLANES, n_state), lambda c: (0, 0))
    big = pltpu.VMEM((chunk, SB_STATE), F32)
    small = pltpu.VMEM((chunk, w), F32)
    row = pltpu.VMEM((1, n_state), F32)
    eight = pltpu.VMEM((SUBLANES, n_state), F32)
    blk = (nsb, MXU_DIM, SB_STATE)
    held = [pltpu.VMEM(blk, BF16)] * 6 + [pltpu.VMEM(blk, F32)] * 4
    vmem = (6 * _nbytes(blk, BF16) + 4 * _nbytes(blk, F32) + 5 * _nbytes((chunk, SB_STATE), F32)
            + 12 * _nbytes((chunk, w), F32) + 20 * _nbytes(bc_rows[0].shape, F32))
    res, landed = _call(
        body, [proj, proj, y, dyg, st_re, st_im, *bc_rows, *rows_p, d_row], name=name,
        out_shape=[jax.ShapeDtypeStruct((rows, w), F32)] + [jax.ShapeDtypeStruct(b.shape, F32) for b in bc_rows[:4]]
        + [jax.ShapeDtypeStruct((SUBLANES, n_state), F32)] * 2 + [jax.ShapeDtypeStruct((1, w), F32)],
        grid=(nc,),
        in_specs=_u_specs(w, o_u, chunk, rev) + [tile, tile, st, st] + [const(b) for b in bc_rows]
        + [row_n] * 6 + [pl.BlockSpec((steps, n_state), lambda c: (0, 0))] * 2 + [row_w],
        out_specs=[tile] + [const(b) for b in bc_rows[:4]] + [acc8] * 2 + [row_w],
        scratch_shapes=held + [small] * 3 + [pltpu.VMEM((w // LANES, chunk, LANES), F32)] + [big] * 4 + [row] * 4
        + [eight] * 4,
        semantics=("arbitrary",), vmem=vmem, rider=rider)
    return res if rider is None else (res, landed)


def _out_proj_loss(merged, w_o, x, target, name):
    rows, d = x.shape
    tm, tn = _tile(rows, 1024, SUBLANES), _tile(d, 1024)

    def body(a_ref, b_ref, x_ref, t_ref, g_ref, gb_ref, l_ref):
        err = x_ref[...] + _dot(a_ref[...], b_ref[...], NN) - t_ref[...]
        g = err * (1.0 / d)
        g_ref[...] = g
        gb_ref[...] = g.astype(BF16)
        part = jnp.sum(0.5 * err * g, axis=0, keepdims=True)
        first = pl.program_id(1) == 0

        @pl.when(first)
        def _():
            l_ref[...] = part

        @pl.when(jnp.logical_not(first))
        def _():
            l_ref[...] += part

    tile = pl.BlockSpec((tm, tn), lambda j, i: (i, j))
    vmem = 2 * (_nbytes((tm, d), BF16) + _nbytes((d, tn), BF16)) + 12 * _nbytes((tm, tn), F32)
    return _pallas(
        body, name=name,
        out_shape=[jax.ShapeDtypeStruct((rows, d), F32), jax.ShapeDtypeStruct((rows, d), BF16),
                   jax.ShapeDtypeStruct((1, d), F32)],
        grid=(d // tn, rows // tm),
        in_specs=[pl.BlockSpec((tm, d), lambda j, i: (i, 0)), pl.BlockSpec((d, tn), lambda j, i: (0, j)), tile, tile],
        out_specs=[tile, tile, pl.BlockSpec((1, tn), lambda j, i: (0, j))],
        compiler_params=_params(("parallel", "arbitrary"), vmem),
    )(merged, w_o, x, target)


def _pair_sum(grad, recv, name):
    r4, cdim = recv.shape
    r = r4 // N_CHIPS
    tr = _tile(r, 544, 16)
    g4 = grad.reshape(N_CHIPS, 2, r, cdim)
    r3 = recv.reshape(N_CHIPS, r, cdim)
    core = jnp.reshape(lax.axis_index("c"), (1,)).astype(jnp.int32)

    def body(c_ref, g_ref, r_ref, o_ref):
        o_ref[...] = (g_ref[0] + r_ref[...]).astype(BF16)

    out = _pallas(
        body, name=name, out_shape=jax.ShapeDtypeStruct((N_CHIPS, r, cdim), BF16),
        grid_spec=pltpu.PrefetchScalarGridSpec(
            num_scalar_prefetch=1, grid=(N_CHIPS, r // tr),
            in_specs=[pl.BlockSpec((1, 1, tr, cdim), lambda j, i, c: (j, c[0], i, 0)),
                      pl.BlockSpec((1, tr, cdim), lambda j, i, c: (j, i, 0))],
            out_specs=pl.BlockSpec((1, tr, cdim), lambda j, i, c: (j, i, 0))),
        compiler_params=_params(("parallel", "parallel"), 6 * _nbytes((tr, cdim), F32)),
    )(core, g4, r3)
    return out.reshape(r4, cdim)


def _chip_sum(recv, name):
    r4, cdim = recv.shape
    r = r4 // N_CHIPS
    tr = _tile(r, 544, 16)
    r3 = recv.reshape(N_CHIPS, r, cdim)

    def body(r_ref, o_ref):
        acc = r_ref[0].astype(F32)
        for j in range(1, N_CHIPS):
            acc = acc + r_ref[j].astype(F32)
        o_ref[...] = acc

    return _pallas(
        body, name=name, out_shape=jax.ShapeDtypeStruct((r, cdim), F32), grid=(r // tr,),
        in_specs=[pl.BlockSpec((N_CHIPS, tr, cdim), lambda i: (0, i, 0))],
        out_specs=pl.BlockSpec((tr, cdim), lambda i: (i, 0)),
        compiler_params=_params(("parallel",), 8 * _nbytes((tr, cdim), F32)),
    )(r3)


def _adamw_math(w, g, m, v):
    m = ADAM_B1 * m + (1.0 - ADAM_B1) * g
    v = ADAM_B2 * v + (1.0 - ADAM_B2) * (g * g)
    m_hat = m / (1.0 - ADAM_B1 ** ADAM_STEP)
    v_hat = v / (1.0 - ADAM_B2 ** ADAM_STEP)
    delta = -ADAM_LR * (m_hat / (jnp.sqrt(v_hat) + ADAM_EPS) + ADAM_WD * w)
    return delta, m, v


def _adamw(w, g, m, v, name):
    rows, cols = w.shape
    tr = _tile(rows, 256, SUBLANES)

    def body(w_ref, g_ref, m_ref, v_ref, d_ref, nm_ref, nv_ref):
        d, nm, nv = _adamw_math(w_ref[...], g_ref[...], m_ref[...], v_ref[...])
        d_ref[...] = d
        nm_ref[...] = nm
        nv_ref[...] = nv

    spec = pl.BlockSpec((tr, cols), lambda i: (i, 0))
    shp = jax.ShapeDtypeStruct((rows, cols), F32)
    return _pallas(
        body, name=name, out_shape=[shp] * 3, grid=(rows // tr,), in_specs=[spec] * 4, out_specs=[spec] * 3,
        compiler_params=_params(("parallel",)),
    )(w, g, m, v)


def _adamw_chips(w, parts, m, v, name):
    rows, cols = w.shape
    assert sum(p.shape[1] for p in parts) == cols
    tr = _tile(rows, 64, 16)
    n = len(parts)

    def body(*refs):
        w_ref, m_ref, v_ref = refs[0], refs[1 + n], refs[2 + n]
        g_ref, d_ref, nm_ref, nv_ref = refs[3 + n:]
        cols_g = []
        for p_ref in refs[1:1 + n]:
            acc = p_ref[0].astype(F32)
            for j in range(1, N_CHIPS):
                acc = acc + p_ref[j].astype(F32)
            cols_g.append(acc)
        g = cols_g[0] if n == 1 else jnp.concatenate(cols_g, axis=1)
        d, nm, nv = _adamw_math(w_ref[...], g, m_ref[...], v_ref[...])
        g_ref[...] = g
        d_ref[...] = d
        nm_ref[...] = nm
        nv_ref[...] = nv

    spec = pl.BlockSpec((tr, cols), lambda i: (i, 0))
    part_specs = [pl.BlockSpec((N_CHIPS, tr, p.shape[1]), lambda i: (0, i, 0)) for p in parts]
    shp = jax.ShapeDtypeStruct((rows, cols), F32)
    return _pallas(
        body, name=name, out_shape=[shp] * 4, grid=(rows // tr,),
        in_specs=[spec] + part_specs + [spec, spec], out_specs=[spec] * 4,
        compiler_params=_params(("parallel",)),
    )(w, *[p.reshape(N_CHIPS, rows, p.shape[1]) for p in parts], m, v)


def _adamw_small(w, parts, m, v, name):
    rows, cols = w.shape
    p3 = parts.reshape(N_DEV, rows, cols)

    def body(w_ref, p_ref, m_ref, v_ref, g_ref, d_ref, nm_ref, nv_ref):
        g = p_ref[0]
        for k in range(1, N_DEV):
            g = g + p_ref[k]
        d, nm, nv = _adamw_math(w_ref[...], g, m_ref[...], v_ref[...])
        g_ref[...] = g
        d_ref[...] = d
        nm_ref[...] = nm
        nv_ref[...] = nv

    shp = jax.ShapeDtypeStruct((rows, cols), F32)
    return _pallas(body, name=name, out_shape=[shp] * 4)(w, p3, m, v)


SMALL = ("norm_w", "q_norm_w", "k_norm_w", "sinks", "A_re", "A_im", "log_dt", "B_re", "B_im", "C_re", "C_im",
         "D_skip", "b_glu")
LARGE = ("w_in", "w_attn_proj", "w_glu", "w_ssm_proj", "w_out")
ORDER = ("norm_w", "w_in", "q_norm_w", "k_norm_w", "sinks", "w_attn_proj", "A_re", "A_im", "log_dt", "B_re", "B_im",
         "C_re", "C_im", "D_skip", "w_glu", "b_glu", "w_ssm_proj", "w_out")


SMALL_REST = ("loss",) + SMALL[1:]


def _pack(named, keys):
    flat = jnp.concatenate([named[k].reshape(-1).astype(F32) for k in keys])
    n = flat.shape[0]
    rows = -(-n // (LANES * SUBLANES)) * SUBLANES
    return jnp.pad(flat, (0, rows * LANES - n)).reshape(rows, LANES)


def _unpack(packed, like, keys):
    flat = packed.reshape(-1)
    out, o = {}, 0
    for k in keys:
        n = like[k].size
        out[k] = flat[o:o + n].reshape(like[k].shape)
        o += n
    return out


def _step(xs, target, p, shards):
    s_in, s_ap, s_glu, s_sp, s_o = shards
    seq, d = xs.shape
    attn_w = (d // 128) * HEAD_DIM
    n_q = attn_w // HEAD_DIM
    kv_w = N_KV_HEADS * HEAD_DIM
    ssm_w = d // 2
    n_groups = ssm_w // GROUP
    n_state = n_groups * STATE
    in_w = N_DEV * s_in.shape[0]
    assert in_w == 2 * attn_w + 2 * kv_w + 2 * ssm_w + 2 * d
    o_u = 2 * attn_w + 2 * kv_w
    o_z = o_u + ssm_w
    o_ga = o_z + ssm_w
    chunk = min(SSM_CHUNK, seq)
    cw = d // 4

    norm_row = p["norm_w"].reshape(1, d)
    half = d // W_IN_PARTS
    assert W_IN_PARTS == 2
    s_in_parts = [s_in[:, :half], s_in[:, half:]]
    h, (w_lo,) = _rmsnorm_fwd(xs, norm_row, "rmsnorm_fwd", rider=_all_gather(s_in_parts[:1]))
    part, (w_hi,) = _matmul(Cols(h, 0, half), w_lo, mode="nt", name="in_proj_0", tn=2176, out_dtype=BF16,
                            rider=_all_gather(s_in_parts[1:]))
    proj = _matmul(Cols(h, half, half), w_hi, mode="nt", name="in_proj_1", tn=2176, out_dtype=BF16, add=part)
    w_in_parts = [w_lo, w_hi]
    qw_row = jnp.tile(p["q_norm_w"], n_q).reshape(1, attn_w)
    kw_row = jnp.tile(p["k_norm_w"], N_KV_HEADS).reshape(1, kv_w)
    gmat = _head_mean_matrix()
    ag = _attention_fwd(proj, qw_row, kw_row, gmat, p["sinks"], attn_w=attn_w, kv_w=kv_w, name="attention_fwd")

    log_dt_col = p["log_dt"].reshape(n_groups, 1)
    prep = _ssm_prep(p["A_re"], p["A_im"], log_dt_col, chunk // SUBLANES, "ssm_prep")
    rows_p = [v.reshape(1, n_state) for v in prep[:6]] + [v.reshape(-1, n_state) for v in prep[6:]]
    bt_re, bt_im = p["B_re"].transpose(0, 2, 1), p["B_im"].transpose(0, 2, 1)
    cf_re, cf_im = prep[2][:, None, :], prep[3][:, None, :]
    bc_rows = [_ssm_rows(m) for m in (bt_re, bt_im, p["C_re"], p["C_im"],
                                      cf_re * bt_re - cf_im * bt_im, cf_re * bt_im + cf_im * bt_re)]
    d_row = p["D_skip"].reshape(1, ssm_w)
    (y_ssm, st_re, st_im, yg), (w_ap_t, w_glu_t, w_sp_t, w_o) = _ssm_fwd(
        proj, o_u, bc_rows[:4], rows_p, d_row, chunk=chunk, name="ssm_fwd",
        rider=_all_gather([s_ap, s_glu, s_sp, s_o]))
    glu = _matmul(yg, w_glu_t, mode="nt", name="glu_proj", out_dtype=BF16, bias=p["b_glu"].reshape(1, 2 * ssm_w))
    (ts,) = _ew(lambda ga, gb, z: ga * _sigmoid(gb) * _silu(z), name="glu_gate", rows=seq, width=ssm_w,
                tiles=[(glu, 0), (glu, ssm_w), (proj, o_z)], outs=[(BF16, ssm_w, 0)], cw=cw)
    yy = _matmul(ag, w_ap_t, mode="nt", name="attn_proj", out_dtype=BF16, out_cols=(2 * d, 0))
    yy = _matmul(ts, w_sp_t, mode="nt", name="ssm_proj", out_dtype=BF16, out_cols=(2 * d, d), into=yy)
    (merged,) = _ew(lambda ya, ys, ga, gs: _sigmoid(ga) * ya + _sigmoid(gs) * ys, name="merge", rows=seq, width=d,
                    tiles=[(yy, 0), (yy, d), (proj, o_ga), (proj, o_ga + d)], outs=[(BF16, d, 0)], cw=cw)
    dout, dout_b, loss_cols = _out_proj_loss(merged, w_o, xs, target, "out_proj_loss")
    loss_local = jnp.sum(loss_cols)

    g_w_o = _matmul(merged, dout_b, mode="tn", name="grad_w_out", tm=512, tk=4096)
    dmerged = _matmul(dout_b, w_o, mode="nt", name="d_merged", out_dtype=BF16)

    def merge_bwd(dm, y, g):
        s = _sigmoid(g)
        return dm * s, dm * y * s * (1.0 - s)

    dyy, dproj = _ew(merge_bwd, name="merge_bwd", rows=seq, width=2 * d,
                     tiles=[(dmerged, 0, d), (yy, 0), (proj, o_ga)],
                     outs=[(BF16, 2 * d, 0), (BF16, in_w, o_ga)], cw=cw)
    dy_a, dy_s = Cols(dyy, 0, d), Cols(dyy, d, d)
    g_w_ap_t = _matmul(dy_a, ag, mode="tn", name="grad_w_attn_proj", tm=512, tk=4096)
    g_w_sp_t = _matmul(dy_s, ts, mode="tn", name="grad_w_ssm_proj", tm=512, tk=4096)
    d_ag = _matmul(dy_a, w_ap_t, mode="nn", name="d_attn_gated", out_dtype=BF16)
    d_ts = _matmul(dy_s, w_sp_t, mode="nn", name="d_ssm_gated", out_dtype=BF16)

    (dproj, dkv, g_qw, g_kw, g_sinks), (sib_o, sib_ap, sib_sp) = _attention_bwd(
        proj, d_ag, dproj, qw_row, kw_row, gmat, p["sinks"], attn_w=attn_w, kv_w=kv_w, name="attention_bwd",
        rider=_sibling_exchange([g_w_o, g_w_ap_t, g_w_sp_t]))
    pair_o = _pair_sum(g_w_o, sib_o, "pair_sum_w_out")
    pair_ap = _pair_sum(g_w_ap_t, sib_ap, "pair_sum_w_attn_proj")
    pair_sp = _pair_sum(g_w_sp_t, sib_sp, "pair_sum_w_ssm_proj")
    dproj = _attention_dkv(dproj, dkv, attn_w=attn_w, kv_w=kv_w, name="attention_dkv")

    n_half = ssm_w // _tile(2 * ssm_w, cw)

    def glu_bwd(j, dt, ga, gb, z):
        sb, sz = _sigmoid(gb), _silu(z)
        dg = jnp.where(j < n_half, dt * sb * sz, dt * ga * sb * (1.0 - sb) * sz)
        return dg, dg

    glu_ops = [(d_ts, 0, ssm_w), (glu, 0, ssm_w), (glu, ssm_w, ssm_w), (proj, o_z, ssm_w)]
    dglu, g_bglu = _ew(glu_bwd, name="glu_bwd", rows=seq, width=2 * ssm_w, tiles=glu_ops,
                       outs=[(BF16, 2 * ssm_w, 0)], accs=1, cw=cw, with_col=True)
    (dproj,) = _ew(lambda dt, ga, gb, z: dt * ga * _sigmoid(gb) * _dsilu(z), name="glu_bwd_z", rows=seq,
                   width=ssm_w, tiles=glu_ops, outs=[(BF16, in_w, o_z)], into=[dproj], cw=cw)
    g_w_glu_t = _matmul(dglu, yg, mode="tn", name="grad_w_glu", tm=512, tk=4096)
    d_yg = _matmul(dglu, w_glu_t, mode="nn", name="d_gelu", out_dtype=BF16)
    ((du, dbt_re, dbt_im, dc_re, dc_im, gabr, gabi, g_d), (chips_o, chips_ap, chips_sp, sib_glu)) = _ssm_bwd(
        proj, o_u, y_ssm, d_yg, st_re, st_im, bc_rows, rows_p, d_row, chunk=chunk, name="ssm_bwd",
        rider=_join(_chip_exchange([pair_o, pair_ap, pair_sp]), _sibling_exchange([g_w_glu_t])))
    pair_glu = _pair_sum(g_w_glu_t, sib_glu, "pair_sum_w_glu")
    (dproj,) = _ew(lambda v: v, name="du_store", rows=seq, width=ssm_w, tiles=[(du, 0)],
                   outs=[(BF16, in_w, o_u)], into=[dproj], cw=cw)
    g_a_re, g_a_im, g_log_dt, g_bt_re, g_bt_im = _ssm_param_bwd(
        p["A_re"], p["A_im"], log_dt_col, *[g.reshape(SUBLANES, n_groups, STATE) for g in (gabr, gabi)],
        bt_re, bt_im, _from_ssm_rows(dbt_re), _from_ssm_rows(dbt_im), "ssm_param_bwd")
    small_grads = dict(
        loss=loss_local, q_norm_w=g_qw.reshape(n_q, HEAD_DIM).sum(0), k_norm_w=g_kw.reshape(N_KV_HEADS, HEAD_DIM).sum(0),
        sinks=g_sinks[0, :n_q], A_re=g_a_re, A_im=g_a_im, log_dt=g_log_dt.reshape(n_groups),
        B_re=g_bt_re.transpose(0, 2, 1), B_im=g_bt_im.transpose(0, 2, 1),
        C_re=_from_ssm_rows(dc_re), C_im=_from_ssm_rows(dc_im),
        D_skip=g_d.reshape(n_groups, GROUP), b_glu=g_bglu.reshape(2 * ssm_w))

    n_parts = W_IN_PARTS
    wq = d // n_parts
    g_parts, pair_parts, chip_parts = [], [], []
    extra = [_chip_exchange([pair_glu]), _all_gather([_pack(small_grads, SMALL_REST)])]
    chips_glu = small_parts = dh = None
    for step in range(n_parts + 2):
        riders = list(extra) if step == 0 else []
        if 0 <= step - 2 < n_parts:
            riders.append(_chip_exchange([pair_parts[step - 2]]))
        if 0 <= step - 1 < n_parts:
            riders.append(_sibling_exchange([g_parts[step - 1]]))
        rider = _join(*riders) if riders else None
        if step < n_parts:
            res = _matmul(dproj, Cols(h, step * wq, wq), mode="tn", name="grad_w_in_%d" % step, tk=4096, rider=rider)
            out, landed = res if rider is not None else (res, [])
            g_parts.append(out)
        else:
            q = step - n_parts
            dh, landed = _matmul(dproj, w_in_parts[q], mode="nn", name="d_normed_%d" % q, tk=2176,
                                 out_cols=(d, q * wq), into=dh, rider=rider)
        landed = list(landed)
        if step == 0:
            chips_glu, small_parts = landed[:2]
            landed = landed[2:]
        if 0 <= step - 2 < n_parts:
            chip_parts.append(landed.pop(0))
        if 0 <= step - 1 < n_parts:
            pair_parts.append(_pair_sum(g_parts[step - 1], landed.pop(0), "pair_sum_w_in_%d" % (step - 1)))
    grad_x, g_norm = _rmsnorm_bwd(xs, norm_row, dh, dout, "rmsnorm_bwd")
    (norm_parts,) = _exchange(_all_gather([_pack(dict(norm_w=g_norm), ("norm_w",))]), "gather_norm_grad")
    from_chips = dict(zip(LARGE, (chip_parts, [chips_ap], [chips_glu], [chips_sp], [chips_o])))
    return grad_x, from_chips, small_parts, norm_parts


def kernel(x, norm_w, w_in, q_norm_w, k_norm_w, sinks, w_attn_proj, A_re, A_im, log_dt, B_re, B_im, C_re, C_im, D_skip, w_glu, b_glu, w_ssm_proj, w_out, loss_target, m_norm_w, m_w_in, m_q_norm_w, m_k_norm_w, m_sinks, m_w_attn_proj, m_A_re, m_A_im, m_log_dt, m_B_re, m_B_im, m_C_re, m_C_im, m_D_skip, m_w_glu, m_b_glu, m_w_ssm_proj, m_w_out, v_norm_w, v_w_in, v_q_norm_w, v_k_norm_w, v_sinks, v_w_attn_proj, v_A_re, v_A_im, v_log_dt, v_B_re, v_B_im, v_C_re, v_C_im, v_D_skip, v_w_glu, v_b_glu, v_w_ssm_proj, v_w_out):
    weights = dict(norm_w=norm_w, w_in=w_in, q_norm_w=q_norm_w, k_norm_w=k_norm_w, sinks=sinks,
                   w_attn_proj=w_attn_proj, A_re=A_re, A_im=A_im, log_dt=log_dt, B_re=B_re, B_im=B_im, C_re=C_re,
                   C_im=C_im, D_skip=D_skip, w_glu=w_glu, b_glu=b_glu, w_ssm_proj=w_ssm_proj, w_out=w_out)
    m_in = dict(norm_w=m_norm_w, w_in=m_w_in, q_norm_w=m_q_norm_w, k_norm_w=m_k_norm_w, sinks=m_sinks,
                w_attn_proj=m_w_attn_proj, A_re=m_A_re, A_im=m_A_im, log_dt=m_log_dt, B_re=m_B_re, B_im=m_B_im,
                C_re=m_C_re, C_im=m_C_im, D_skip=m_D_skip, w_glu=m_w_glu, b_glu=m_b_glu, w_ssm_proj=m_w_ssm_proj,
                w_out=m_w_out)
    v_in = dict(norm_w=v_norm_w, w_in=v_w_in, q_norm_w=v_q_norm_w, k_norm_w=v_k_norm_w, sinks=v_sinks,
                w_attn_proj=v_w_attn_proj, A_re=v_A_re, A_im=v_A_im, log_dt=v_log_dt, B_re=v_B_re, B_im=v_B_im,
                C_re=v_C_re, C_im=v_C_im, D_skip=v_D_skip, w_glu=v_w_glu, b_glu=v_b_glu, w_ssm_proj=v_w_ssm_proj,
                w_out=v_w_out)

    _, seq, d = x.shape
    column_sharded = LARGE[:4]
    as_rows = lambda k, a: a.T if k in column_sharded else a
    shards = [as_rows(k, weights[k]).astype(BF16) for k in LARGE]
    small = {k: weights[k] for k in SMALL}
    grad_x, from_chips, small_parts, norm_parts = _step(x.reshape(seq, d), loss_target.reshape(seq, d), small,
                                                        shards)

    grads, delta, new_m, new_v = {}, {}, {}, {}
    for k in LARGE:
        if k == "w_in":
            res = _adamw_chips(weights[k].T, from_chips[k], m_in[k].T, v_in[k].T, "adamw_" + k)
            grads[k], delta[k], new_m[k], new_v[k] = [a.T for a in res]
        elif k == "w_out":
            grads[k], delta[k], new_m[k], new_v[k] = _adamw_chips(weights[k], from_chips[k], m_in[k], v_in[k],
                                                                  "adamw_" + k)
        else:
            grads[k] = _chip_sum(from_chips[k][0], "chip_sum_" + k).T
            delta[k], new_m[k], new_v[k] = _adamw(weights[k], grads[k], m_in[k], v_in[k], "adamw_" + k)

    zero = jnp.zeros((), F32)
    for keys, parts in ((SMALL_REST, small_parts), (("norm_w",), norm_parts)):
        like = dict(small, loss=zero)
        packs = [_pack(dict(src, loss=zero), keys) for src in (weights, m_in, v_in)]
        res = _adamw_small(packs[0], parts, packs[1], packs[2], "adamw_small_%d" % len(keys))
        for dst, r in zip((grads, delta, new_m, new_v), res):
            dst.update(_unpack(r, like, keys))
    loss = grads["loss"]

    return (loss, grad_x.reshape(x.shape), *[grads[k] for k in ORDER], *[delta[k] for k in ORDER],
            *[new_m[k] for k in ORDER], *[new_v[k] for k in ORDER])
```

```python
import math
from typing import Callable, NamedTuple

import jax
import jax.numpy as jnp
import numpy as np
from jax import lax
from jax.experimental import pallas as pl
from jax.experimental.pallas import tpu as pltpu

F32 = jnp.float32
BF16 = jnp.bfloat16
MESH = pl.DeviceIdType.MESH

HEAD_DIM = 64
N_KV_HEADS = 4
GROUP = 16
STATE = 64
BLOCK = 128
NORM_EPS = 1e-6
N_DEV = 8
N_CHIPS = 4
LANES = 128
SUBLANES = 8
MXU_DIM = 256
VMEM_BYTES = 64 * 1024 * 1024
VMEM_CAP = VMEM_BYTES - 8 * 1024 * 1024

ADAM_LR = 0.001
ADAM_B1 = 0.9
ADAM_B2 = 0.999
ADAM_EPS = 1e-08
ADAM_WD = 0.01
ADAM_STEP = 10

GELU_C = math.sqrt(2.0 / math.pi)
GELU_K = 0.044715


def _tile(dim, pref, mult=LANES):
    if dim <= pref:
        return dim
    best = None
    for d in range(mult, pref + 1, mult):
        if dim % d == 0:
            best = d
    assert best is not None, (dim, pref, mult)
    return best


def _params(semantics=None, vmem=None):
    kw = {}
    if semantics is not None:
        kw["dimension_semantics"] = semantics
    if vmem is not None:
        kw["vmem_limit_bytes"] = int(min(VMEM_CAP, max(vmem, 32 * 1024 * 1024)))
    return pltpu.CompilerParams(**kw)


def _nbytes(shape, dtype):
    return math.prod(shape) * jnp.dtype(dtype).itemsize


def _sigmoid(x):
    return 1.0 / (1.0 + jnp.exp(-x))


def _silu(x):
    return x * _sigmoid(x)


def _dsilu(x):
    s = _sigmoid(x)
    return s * (1.0 + x * (1.0 - s))


def _gelu(x):
    return 0.5 * x * (1.0 + jnp.tanh(GELU_C * (x + GELU_K * x * x * x)))


def _dgelu(x):
    t = jnp.tanh(GELU_C * (x + GELU_K * x * x * x))
    return 0.5 * (1.0 + t) + 0.5 * x * (1.0 - t * t) * GELU_C * (1.0 + 3.0 * GELU_K * x * x)


def _dot(a, b, dims):
    return lax.dot_general(a, b, (dims, ((), ())), preferred_element_type=F32)


NN = ((1,), (0,))
NT = ((1,), (1,))
TN = ((0,), (0,))


def _any_spec():
    return pl.BlockSpec(memory_space=pl.ANY)


def _pallas(body, **kw):
    pin = lambda s: pltpu.HBM(s.shape, s.dtype) if isinstance(s, jax.ShapeDtypeStruct) else s
    out_shape = kw.pop("out_shape")
    out_shape = [pin(s) for s in out_shape] if isinstance(out_shape, (list, tuple)) else pin(out_shape)
    call = pl.pallas_call(body, out_shape=out_shape, **kw)

    def run(*operands):
        pinned = [pltpu.with_memory_space_constraint(o, pltpu.HBM) if jnp.issubdtype(o.dtype, jnp.floating) else o
                  for o in operands]
        return call(*pinned)

    return run


class Rider(NamedTuple):
    operands: tuple
    out_shapes: tuple
    sems: tuple
    start: Callable
    finish: Callable


def _all_gather(shards):
    n = len(shards)

    def copies(ins, outs, sems):
        send_sems, recv_sems, local_sems = sems
        x, y, c = lax.axis_index("x"), lax.axis_index("y"), lax.axis_index("c")
        me, sibling = (x, y, c), (x, y, 1 - c)
        chips = [(1 - x, y), (x, 1 - y), (1 - x, 1 - y)]

        def rows(k, px, py, pc):
            r = shards[k].shape[0]
            return outs[k].at[pl.ds((4 * px + 2 * py + pc) * r, r), :]

        def copy(k, s, block, to, src=None):
            return pltpu.make_async_remote_copy(
                src_ref=rows(k, *block) if src is None else src, dst_ref=rows(k, *block),
                send_sem=send_sems.at[7 * k + s], recv_sem=recv_sems.at[7 * k + s],
                device_id=to, device_id_type=MESH)

        mine = [pltpu.make_async_copy(ins[k], rows(k, *me), local_sems.at[k]) for k in range(n)]
        first = []
        for k in range(n):
            first.append(copy(k, 0, me, sibling, src=ins[k]))
            first += [copy(k, 1 + j, me, (*chip, c), src=ins[k]) for j, chip in enumerate(chips)]
        return me, sibling, chips, c, copy, mine, first

    def start(ins, outs, sems):
        *_, mine, first = copies(ins, outs, sems)
        for cp in mine + first:
            cp.start()

    def finish(ins, outs, sems):
        me, sibling, chips, c, copy, mine, first = copies(ins, outs, sems)
        passed = []
        for j, chip in enumerate(chips):
            for k in range(n):
                copy(k, 1 + j, (*chip, c), me).wait_recv()
                fwd = copy(k, 4 + j, (*chip, c), sibling)
                fwd.start()
                passed.append(fwd)
        for k in range(n):
            copy(k, 0, sibling, me).wait_recv()
            for j, chip in enumerate(chips):
                copy(k, 4 + j, (*chip, 1 - c), me).wait_recv()
        for cp in first + passed:
            cp.wait_send()
        for cp in mine:
            cp.wait()

    return Rider(
        tuple(shards),
        tuple(jax.ShapeDtypeStruct((N_DEV * s.shape[0], s.shape[1]), s.dtype) for s in shards),
        (pltpu.SemaphoreType.DMA((7 * n,)), pltpu.SemaphoreType.DMA((7 * n,)), pltpu.SemaphoreType.DMA((n,))),
        start, finish)


def _sibling_exchange(grads):
    n = len(grads)

    def copies(ins, outs, sems):
        send_sems, recv_sems = sems
        x, y, c = lax.axis_index("x"), lax.axis_index("y"), lax.axis_index("c")
        out = []
        for k in range(n):
            r = grads[k].shape[0] // N_DEV
            for j in range(N_CHIPS):
                out.append(pltpu.make_async_remote_copy(
                    src_ref=ins[k].at[pl.ds((2 * j + 1 - c) * r, r), :],
                    dst_ref=outs[k].at[pl.ds(j * r, r), :],
                    send_sem=send_sems.at[N_CHIPS * k + j], recv_sem=recv_sems.at[N_CHIPS * k + j],
                    device_id=(x, y, 1 - c), device_id_type=MESH))
        return out

    def start(ins, outs, sems):
        for cp in copies(ins, outs, sems):
            cp.start()

    def finish(ins, outs, sems):
        for cp in copies(ins, outs, sems):
            cp.wait()

    return Rider(
        tuple(grads), tuple(jax.ShapeDtypeStruct((g.shape[0] // 2, g.shape[1]), g.dtype) for g in grads),
        (pltpu.SemaphoreType.DMA((N_CHIPS * n,)), pltpu.SemaphoreType.DMA((N_CHIPS * n,))), start, finish)


def _chip_exchange(parts):
    n = len(parts)

    def copies(ins, outs, sems):
        send_sems, recv_sems, local_sems = sems
        x, y, c = lax.axis_index("x"), lax.axis_index("y"), lax.axis_index("c")
        my_chip = 2 * x + y
        chips = [(1 - x, y), (x, 1 - y), (1 - x, 1 - y)]
        local, sent = [], []
        for k in range(n):
            r = parts[k].shape[0] // N_CHIPS
            mine = pl.ds(my_chip * r, r)
            local.append(pltpu.make_async_copy(ins[k].at[mine, :], outs[k].at[mine, :], local_sems.at[k]))
            for s, (px, py) in enumerate(chips):
                sent.append(pltpu.make_async_remote_copy(
                    src_ref=ins[k].at[pl.ds((2 * px + py) * r, r), :], dst_ref=outs[k].at[mine, :],
                    send_sem=send_sems.at[3 * k + s], recv_sem=recv_sems.at[3 * k + s],
                    device_id=(px, py, c), device_id_type=MESH))
        return local, sent

    def start(ins, outs, sems):
        local, sent = copies(ins, outs, sems)
        for cp in local + sent:
            cp.start()

    def finish(ins, outs, sems):
        local, sent = copies(ins, outs, sems)
        for cp in sent + local:
            cp.wait()

    return Rider(
        tuple(parts), tuple(jax.ShapeDtypeStruct(p.shape, p.dtype) for p in parts),
        (pltpu.SemaphoreType.DMA((3 * n,)), pltpu.SemaphoreType.DMA((3 * n,)), pltpu.SemaphoreType.DMA((n,))),
        start, finish)


def _join(*riders):
    cuts_in, cuts_out, cuts_sem = [0], [0], [0]
    for r in riders:
        cuts_in.append(cuts_in[-1] + len(r.operands))
        cuts_out.append(cuts_out[-1] + len(r.out_shapes))
        cuts_sem.append(cuts_sem[-1] + len(r.sems))

    def each(which):
        def run(ins, outs, sems):
            for i, r in enumerate(riders):
                getattr(r, which)(ins[cuts_in[i]:cuts_in[i + 1]], outs[cuts_out[i]:cuts_out[i + 1]],
                                  sems[cuts_sem[i]:cuts_sem[i + 1]])
        return run

    return Rider(sum((r.operands for r in riders), ()), sum((r.out_shapes for r in riders), ()),
                 sum((r.sems for r in riders), ()), each("start"), each("finish"))


def _call(body, operands, *, name, out_shape, grid, in_specs, out_specs, scratch_shapes=(), aliases=None,
          semantics=None, vmem=None, rider=None):
    operands, out_shape, scratch_shapes = list(operands), list(out_shape), list(scratch_shapes)
    in_specs, out_specs = list(in_specs), list(out_specs)
    if rider is None:
        res = _pallas(
            body, name=name, out_shape=out_shape, grid=grid, in_specs=in_specs, out_specs=out_specs,
            scratch_shapes=scratch_shapes, input_output_aliases=aliases or {},
            compiler_params=_params(semantics, vmem))(*operands)
        return list(res), []
    n_in, n_out, n_scr = len(operands), len(out_shape), len(scratch_shapes)
    ri, ro = len(rider.operands), len(rider.out_shapes)

    def carried(*refs):
        a, b = n_in, n_in + ri
        c, d = b + n_out, b + n_out + ro
        e = d + n_scr
        ids = [pl.program_id(k) for k in range(len(grid))]
        first = ids[0] == 0
        last = ids[0] == grid[0] - 1
        for k in range(1, len(grid)):
            first = jnp.logical_and(first, ids[k] == 0)
            last = jnp.logical_and(last, ids[k] == grid[k] - 1)

        @pl.when(first)
        def _():
            rider.start(refs[a:b], refs[c:d], refs[e:])

        body(*refs[:a], *refs[b:c], *refs[d:e])

        @pl.when(last)
        def _():
            rider.finish(refs[a:b], refs[c:d], refs[e:])

    res = _pallas(
        carried, name=name, out_shape=out_shape + list(rider.out_shapes), grid=grid,
        in_specs=in_specs + [_any_spec()] * ri, out_specs=out_specs + [_any_spec()] * ro,
        scratch_shapes=scratch_shapes + list(rider.sems), input_output_aliases=aliases or {},
        compiler_params=_params(("arbitrary",) * len(grid), vmem))(*operands, *rider.operands)
    return list(res[:n_out]), list(res[n_out:])


def _exchange(rider, name):
    ri, ro = len(rider.operands), len(rider.out_shapes)

    def body(*refs):
        rider.start(refs[:ri], refs[ri:ri + ro], refs[ri + ro:])
        rider.finish(refs[:ri], refs[ri:ri + ro], refs[ri + ro:])

    return _pallas(
        body, name=name, out_shape=list(rider.out_shapes), in_specs=[_any_spec()] * ri,
        out_specs=[_any_spec()] * ro, scratch_shapes=list(rider.sems))(*rider.operands)


class Cols(NamedTuple):
    arr: jax.Array
    off: int
    width: int


def _cols(a):
    return a if isinstance(a, Cols) else Cols(a, 0, a.shape[1])


def _matmul(a, b, *, mode, name, out_dtype=F32, tm=1024, tn=1024, tk=2048, bias=None, add=None, out_cols=None,
            into=None, rider=None):
    a, b = _cols(a), _cols(b)
    if mode == "nn":
        (m, k), (k2, n) = (a.arr.shape[0], a.width), (b.arr.shape[0], b.width)
    elif mode == "nt":
        (m, k), (n, k2) = (a.arr.shape[0], a.width), (b.arr.shape[0], b.width)
    else:
        (k, m), (k2, n) = (a.arr.shape[0], a.width), (b.arr.shape[0], b.width)
    assert k == k2, (a.arr.shape, b.arr.shape, mode)
    tm, tn, tk = _tile(m, tm), _tile(n, tn), _tile(k, tk)
    nk = k // tk
    dims = {"nn": NN, "nt": NT, "tn": TN}[mode]
    if mode == "tn":
        assert a.off % tm == 0
        a_spec = pl.BlockSpec((tk, tm), lambda i, j, kk, o=a.off // tm: (kk, i + o))
    else:
        assert a.off % tk == 0
        a_spec = pl.BlockSpec((tm, tk), lambda i, j, kk, o=a.off // tk: (i, kk + o))
    if mode == "nt":
        assert b.off % tk == 0
        b_spec = pl.BlockSpec((tn, tk), lambda i, j, kk, o=b.off // tk: (j, kk + o))
    else:
        assert b.off % tn == 0
        b_spec = pl.BlockSpec((tk, tn), lambda i, j, kk, o=b.off // tn: (kk, j + o))
    in_specs, operands = [a_spec, b_spec], [a.arr, b.arr]
    assert bias is None or add is None
    if bias is not None:
        in_specs.append(pl.BlockSpec((1, tn), lambda i, j, kk: (0, j)))
        operands.append(bias)
    if add is not None:
        assert add.shape == (m, n)
        in_specs.append(pl.BlockSpec((tm, tn), lambda i, j, kk: (i, j)))
        operands.append(add)
    total_w, o_off = out_cols if out_cols is not None else (n, 0)
    assert o_off % tn == 0
    aliases = {}
    if into is not None:
        assert into.shape == (m, total_w) and into.dtype == out_dtype
        in_specs.append(_any_spec())
        operands.append(into)
        aliases = {len(operands) - 1: 0}
    n_in = len(operands)

    def body(*refs):
        a_ref, b_ref = refs[0], refs[1]
        bias_ref = refs[2] if bias is not None or add is not None else None
        o_ref = refs[n_in]
        acc_ref = refs[-1] if nk > 1 else None
        part = _dot(a_ref[...].astype(BF16), b_ref[...].astype(BF16), dims)

        def finish(acc):
            if bias_ref is not None:
                acc = acc + bias_ref[...]
            o_ref[...] = acc.astype(out_dtype)

        if nk == 1:
            finish(part)
        else:
            kk = pl.program_id(2)

            @pl.when(kk == 0)
            def _():
                acc_ref[...] = part

            @pl.when(kk > 0)
            def _():
                acc_ref[...] += part

            @pl.when(kk == nk - 1)
            def _():
                finish(acc_ref[...])

    vmem = 2 * (_nbytes((tm, tk), a.arr.dtype) + _nbytes((tk, tn), b.arr.dtype) + _nbytes((tm, tn), out_dtype))
    vmem += 3 * _nbytes((tm, tn), F32)
    (out,), landed = _call(
        body, operands, name=name, out_shape=[jax.ShapeDtypeStruct((m, total_w), out_dtype)],
        grid=(m // tm, n // tn, nk), in_specs=in_specs,
        out_specs=[pl.BlockSpec((tm, tn), lambda i, j, kk, o=o_off // tn: (i, j + o))],
        scratch_shapes=[pltpu.VMEM((tm, tn), F32)] if nk > 1 else [], aliases=aliases,
        semantics=("parallel", "parallel", "arbitrary"), vmem=vmem, rider=rider)
    return out if rider is None else (out, landed)


def _ew(fn, *, name, rows, width, tiles, vecs=(), outs, accs=0, tl=1024, cw=512, into=None, with_col=False):
    tl, cw = _tile(rows, tl, SUBLANES), _tile(width, cw)
    ncol = width // cw
    nt_, nv = len(tiles), len(vecs)
    into = list(into) if into is not None else [None] * len(outs)
    aliased = [t for t in into if t is not None]

    def off(o):
        assert o % cw == 0, (name, o, cw)
        return o // cw

    in_specs, vmem = [], 0
    for t in tiles:
        arr, o = t[0], off(t[1])
        wrap = t[2] // cw if len(t) > 2 else ncol
        in_specs.append(pl.BlockSpec((tl, cw), lambda j, i, o=o, wrap=wrap: (i, o + j % wrap)))
        vmem += _nbytes((tl, cw), arr.dtype)
    in_specs += [pl.BlockSpec((1, cw), lambda j, i, o=off(o): (0, j + o)) for _, o in vecs]
    in_specs += [_any_spec() for _ in aliased]
    out_shape, out_specs, aliases = [], [], {}
    n_in = nt_ + nv
    for idx, ((dt, tw, o), tgt) in enumerate(zip(outs, into)):
        out_shape.append(jax.ShapeDtypeStruct((rows, tw), dt))
        out_specs.append(pl.BlockSpec((tl, cw), lambda j, i, o=off(o): (i, j + o)))
        vmem += _nbytes((tl, cw), dt)
        if tgt is not None:
            assert tgt.shape == (rows, tw) and tgt.dtype == dt, (name, tgt.shape, tgt.dtype)
            aliases[n_in + len(aliases)] = idx
    for _ in range(accs):
        out_shape.append(jax.ShapeDtypeStruct((1, width), F32))
        out_specs.append(pl.BlockSpec((1, cw), lambda j, i: (0, j)))
    n_out = len(outs)

    def body(*refs):
        vals = [r[...].astype(F32) for r in refs[:n_in]]
        out_refs = refs[n_in + len(aliased):]
        res = fn(pl.program_id(0), *vals) if with_col else fn(*vals)
        res = res if isinstance(res, (tuple, list)) else (res,)
        assert len(res) == n_out + accs, (name, len(res))
        for r, v in zip(out_refs[:n_out], res[:n_out]):
            r[...] = v.astype(r.dtype)
        first = pl.program_id(1) == 0
        for r, v in zip(out_refs[n_out:], res[n_out:]):
            s = jnp.sum(v, axis=0, keepdims=True)

            @pl.when(first)
            def _(r=r, s=s):
                r[...] = s

            @pl.when(jnp.logical_not(first))
            def _(r=r, s=s):
                r[...] += s

    return _pallas(
        body, name=name, out_shape=out_shape, grid=(ncol, rows // tl),
        in_specs=in_specs, out_specs=out_specs, input_output_aliases=aliases,
        compiler_params=_params(("parallel", "arbitrary"), 3 * vmem),
    )(*[t[0] for t in tiles], *[v for v, _ in vecs], *aliased)


def _rmsnorm_fwd(x, w_row, name, rider=None):
    rows, d = x.shape
    tl = _tile(rows, 512, SUBLANES)

    def body(x_ref, w_ref, h_ref):
        xv = x_ref[...]
        rstd = lax.rsqrt(jnp.mean(xv * xv, axis=-1, keepdims=True) + NORM_EPS)
        h_ref[...] = (xv * rstd * w_ref[...]).astype(BF16)

    (h,), landed = _call(
        body, [x, w_row], name=name, out_shape=[jax.ShapeDtypeStruct((rows, d), BF16)], grid=(rows // tl,),
        in_specs=[pl.BlockSpec((tl, d), lambda i: (i, 0)), pl.BlockSpec((1, d), lambda i: (0, 0))],
        out_specs=[pl.BlockSpec((tl, d), lambda i: (i, 0))], semantics=("parallel",), rider=rider)
    return h if rider is None else (h, landed)


def _rmsnorm_bwd(x, w_row, dh, dout, name, rider=None):
    rows, d = x.shape
    tl = _tile(rows, 256, SUBLANES)

    def body(x_ref, w_ref, dh_ref, dout_ref, gx_ref, gw_ref):
        xv = x_ref[...]
        rstd = lax.rsqrt(jnp.mean(xv * xv, axis=-1, keepdims=True) + NORM_EPS)
        xn = xv * rstd
        dhv = dh_ref[...]
        dxn = dhv * w_ref[...]
        dx = rstd * (dxn - xn * jnp.mean(dxn * xn, axis=-1, keepdims=True))
        gx_ref[...] = dout_ref[...] + dx
        gw = jnp.sum(dhv * xn, axis=0, keepdims=True)

        @pl.when(pl.program_id(0) == 0)
        def _():
            gw_ref[...] = gw

        @pl.when(pl.program_id(0) > 0)
        def _():
            gw_ref[...] += gw

    tile = pl.BlockSpec((tl, d), lambda i: (i, 0))
    row = pl.BlockSpec((1, d), lambda i: (0, 0))
    res, landed = _call(
        body, [x, w_row, dh, dout], name=name,
        out_shape=[jax.ShapeDtypeStruct((rows, d), F32), jax.ShapeDtypeStruct((1, d), F32)],
        grid=(rows // tl,), in_specs=[tile, row, tile, tile], out_specs=[tile, row],
        semantics=("arbitrary",), rider=rider)
    return res if rider is None else (res, landed)


def _head_mean(x, gmat):
    hi = x.astype(BF16)
    lo = (x - hi.astype(F32)).astype(BF16)
    out = []
    for s in range(x.shape[1] // MXU_DIM):
        sl = slice(s * MXU_DIM, (s + 1) * MXU_DIM)
        out.append(_dot(hi[:, sl], gmat, NN) + _dot(lo[:, sl], gmat, NN))
    return out[0] if len(out) == 1 else jnp.concatenate(out, axis=1)


def _head_mean_matrix():
    blk = jnp.arange(MXU_DIM) // HEAD_DIM
    return jnp.where(blk[:, None] == blk[None, :], 1.0 / HEAD_DIM, 0.0).astype(BF16)


def _spread_head(x, g, width):
    col = x[:, (g // 2) * LANES:(g // 2 + 1) * LANES]
    other = pltpu.roll(col, HEAD_DIM, axis=1)
    low = lax.broadcasted_iota(jnp.int32, col.shape, 1) < HEAD_DIM
    both = jnp.where(low, col, other) if g % 2 == 0 else jnp.where(low, other, col)
    return both if width == LANES else jnp.concatenate([both] * (width // LANES), axis=1)


def _head_diagonal(t, per_kv):
    head = lax.broadcasted_iota(jnp.int32, t.shape, 1) // HEAD_DIM
    zero = jnp.zeros_like(t)
    return jnp.concatenate([jnp.where(head == r, t, zero) for r in range(per_kv)], axis=0)


def _fold_heads(x, per_kv):
    rows = x.shape[0] // per_kv
    head = lax.broadcasted_iota(jnp.int32, (rows, x.shape[1]), 1) // HEAD_DIM
    acc = jnp.where(head == 0, x[0:rows], 0.0)
    for r in range(1, per_kv):
        acc = acc + jnp.where(head == r, x[r * rows:(r + 1) * rows], 0.0)
    while acc.shape[1] > LANES:
        half = acc.shape[1] // 2
        acc = acc[:, :half] + acc[:, half:]
    return acc + pltpu.roll(acc, HEAD_DIM, axis=1)


def _join_heads(parts):
    low = lax.broadcasted_iota(jnp.int32, parts[0].shape, 1) < HEAD_DIM
    cols = [jnp.where(low, parts[2 * j], parts[2 * j + 1]) for j in range(len(parts) // 2)]
    return cols[0] if len(cols) == 1 else jnp.concatenate(cols, axis=1)


def _attn_specs(attn_w, kv_w, block=lambda s: s):
    half = attn_w // 2
    kcol, vcol = attn_w // kv_w, attn_w // kv_w + 1
    gcol = (attn_w + 2 * kv_w) // half
    prev = lambda s: jnp.maximum(block(s) - 1, 0)
    return [
        pl.BlockSpec((BLOCK, attn_w), lambda s: (block(s), 0)),
        pl.BlockSpec((BLOCK, kv_w), lambda s: (prev(s), kcol)),
        pl.BlockSpec((BLOCK, kv_w), lambda s: (block(s), kcol)),
        pl.BlockSpec((BLOCK, kv_w), lambda s: (prev(s), vcol)),
        pl.BlockSpec((BLOCK, kv_w), lambda s: (block(s), vcol)),
        pl.BlockSpec((BLOCK, half), lambda s: (block(s), gcol)),
        pl.BlockSpec((BLOCK, half), lambda s: (block(s), gcol + 1)),
    ]


def _band_mask(i):
    q_loc = lax.broadcasted_iota(jnp.int32, (BLOCK, 2 * BLOCK), 0) + BLOCK
    k_loc = lax.broadcasted_iota(jnp.int32, (BLOCK, 2 * BLOCK), 1)
    diff = q_loc - k_loc
    first_key = jnp.where(i == 0, BLOCK, 0)
    return (diff >= 0) & (diff < BLOCK) & (k_loc >= first_key)


def _softmax_with_sink(s, sink):
    m = jnp.maximum(jnp.max(s, axis=-1, keepdims=True), sink)
    p = jnp.exp(s - m)
    e_sink = jnp.exp(sink - m)
    den = jnp.sum(p, axis=-1, keepdims=True) + e_sink
    inv = 1.0 / den
    return p * inv, e_sink * inv


def _attn_block(i, q, kk, vv, qw, kw, gmat, sink_ref, per_kv):
    scale = 1.0 / math.sqrt(HEAD_DIM)
    keys = 2 * BLOCK
    valid = _band_mask(i)
    q_rstd = lax.rsqrt(_head_mean(q * q, gmat) + NORM_EPS)
    qn = q * q_rstd
    qh = (qn * qw).astype(BF16)
    k_rstd = lax.rsqrt(_head_mean(kk * kk, gmat) + NORM_EPS)
    kn = kk * k_rstd
    kh = kn * kw
    gw = per_kv * HEAD_DIM
    groups = []
    for g in range(N_KV_HEADS):
        kd = _head_diagonal(_spread_head(kh, g, gw).astype(BF16), per_kv)
        vd = _head_diagonal(_spread_head(vv, g, gw).astype(BF16), per_kv)
        qg = qh[:, g * gw:(g + 1) * gw]
        s_all = _dot(qg, kd, NT) * scale
        ps, p_sinks = [], []
        for r in range(per_kv):
            s = jnp.where(valid, s_all[:, r * keys:(r + 1) * keys], -1e30)
            p, p_sink = _softmax_with_sink(s, sink_ref[g * per_kv + r])
            ps.append(p)
            p_sinks.append(p_sink)
        pb = jnp.concatenate(ps, axis=1).astype(BF16)
        groups.append((kd, vd, qg, ps, p_sinks, pb, _dot(pb, vd, NN)))
    return qn, q_rstd, kn, k_rstd, groups


def _attention_fwd(proj, qw_row, kw_row, gmat, sinks, *, attn_w, kv_w, name):
    rows = proj.shape[0]
    per_kv = attn_w // HEAD_DIM // N_KV_HEADS

    def body(q_ref, kp_ref, kc_ref, vp_ref, vc_ref, glo_ref, ghi_ref, qw_ref, kw_ref, gm_ref, sink_ref, o_ref):
        kk = jnp.concatenate([kp_ref[...], kc_ref[...]], axis=0).astype(F32)
        vv = jnp.concatenate([vp_ref[...], vc_ref[...]], axis=0).astype(F32)
        gate = jnp.concatenate([glo_ref[...], ghi_ref[...]], axis=1).astype(F32)
        *_, groups = _attn_block(pl.program_id(0), q_ref[...].astype(F32), kk, vv, qw_ref[...], kw_ref[...], gm_ref[...],
                                 sink_ref, per_kv)
        attn = jnp.concatenate([grp[-1] for grp in groups], axis=1)
        o_ref[...] = (attn * _silu(gate)).astype(BF16)

    const = lambda a: pl.BlockSpec(a.shape, lambda i: (0, 0))
    return _pallas(
        body, name=name, out_shape=jax.ShapeDtypeStruct((rows, attn_w), BF16), grid=(rows // BLOCK,),
        in_specs=_attn_specs(attn_w, kv_w) + [const(qw_row), const(kw_row), const(gmat),
                                              pl.BlockSpec(memory_space=pltpu.SMEM)],
        out_specs=pl.BlockSpec((BLOCK, attn_w), lambda i: (i, 0)),
        compiler_params=_params(("parallel",), 40 * 1024 * 1024),
    )(proj, proj, proj, proj, proj, proj, proj, qw_row, kw_row, gmat, sinks)


def _attention_bwd(proj, d_ag, dproj, qw_row, kw_row, gmat, sinks, *, attn_w, kv_w, name, rider=None):
    rows = proj.shape[0]
    nb = rows // BLOCK
    per_kv = attn_w // HEAD_DIM // N_KV_HEADS
    gw = per_kv * HEAD_DIM
    keys = 2 * BLOCK
    scale = 1.0 / math.sqrt(HEAD_DIM)
    w_out = 2 * attn_w + 2 * kv_w
    rev = lambda s: nb - 1 - s

    def body(q_ref, kp_ref, kc_ref, vp_ref, vc_ref, glo_ref, ghi_ref, dag_ref, qw_ref, kw_ref, gm_ref, sink_ref, _,
             dp_ref, gqw_ref, gkw_ref, gs_ref, carry_ref):
        step = pl.program_id(0)
        i = rev(step)
        kk = jnp.concatenate([kp_ref[...], kc_ref[...]], axis=0).astype(F32)
        vv = jnp.concatenate([vp_ref[...], vc_ref[...]], axis=0).astype(F32)
        gate = jnp.concatenate([glo_ref[...], ghi_ref[...]], axis=1).astype(F32)
        d_ag_v = dag_ref[...].astype(F32)
        qw, kw, gmat_v = qw_ref[...], kw_ref[...], gm_ref[...]
        qn, q_rstd, kn, k_rstd, groups = _attn_block(i, q_ref[...].astype(F32), kk, vv, qw, kw, gmat_v, sink_ref,
                                                     per_kv)
        lane = lax.broadcasted_iota(jnp.int32, (SUBLANES, LANES), 1)
        sub = lax.broadcasted_iota(jnp.int32, (SUBLANES, LANES), 0)
        gsink = jnp.zeros((SUBLANES, LANES), F32)
        dq_groups, dgate_groups, dk_heads, dv_heads = [], [], [], []
        for g, (kd, vd, qg, ps, p_sinks, pb, o) in enumerate(groups):
            cs = slice(g * gw, (g + 1) * gw)
            gate_g, d_ag_g = gate[:, cs], d_ag_v[:, cs]
            dgate_groups.append(d_ag_g * o * _dsilu(gate_g))
            do = (d_ag_g * _silu(gate_g)).astype(BF16)
            dp_all = _dot(do, vd, NT)
            dss = []
            for r in range(per_kv):
                p, dp = ps[r], dp_all[:, r * keys:(r + 1) * keys]
                delta = jnp.sum(p * dp, axis=-1, keepdims=True)
                dss.append(p * (dp - delta) * scale)
                gs_h = jnp.sum(-p_sinks[r] * delta, axis=0, keepdims=True)
                gsink = gsink + jnp.where((lane == g * per_kv + r) & (sub == 0), gs_h, 0.0)
            ds = jnp.concatenate(dss, axis=1).astype(BF16)
            dq_groups.append(_dot(ds, kd, NN))
            dk_heads.append(_fold_heads(_dot(ds, qg, TN), per_kv))
            dv_heads.append(_fold_heads(_dot(pb, do, TN), per_kv))
        dqh = jnp.concatenate(dq_groups, axis=1)
        gqw = jnp.sum(dqh * qn, axis=0, keepdims=True)
        dqn = dqh * qw
        dq = q_rstd * (dqn - qn * _head_mean(dqn * qn, gmat_v))
        dkh = _join_heads(dk_heads)
        gkw = jnp.sum(dkh * kn, axis=0, keepdims=True)
        dkn = dkh * kw
        dk = k_rstd * (dkn - kn * _head_mean(dkn * kn, gmat_v))
        dkv = jnp.concatenate([dk, _join_heads(dv_heads)], axis=1)

        @pl.when(step == 0)
        def _():
            carry_ref[...] = jnp.zeros_like(carry_ref)
            gqw_ref[...] = gqw
            gkw_ref[...] = gkw
            gs_ref[...] = gsink

        @pl.when(step > 0)
        def _():
            gqw_ref[...] += gqw
            gkw_ref[...] += gkw
            gs_ref[...] += gsink

        dp_ref[:, 0:attn_w] = dq.astype(BF16)
        dp_ref[:, attn_w:attn_w + 2 * kv_w] = (dkv[BLOCK:2 * BLOCK, :] + carry_ref[...]).astype(BF16)
        dp_ref[:, attn_w + 2 * kv_w:w_out] = jnp.concatenate(dgate_groups, axis=1).astype(BF16)
        carry_ref[...] = dkv[0:BLOCK, :]

    const = lambda a: pl.BlockSpec(a.shape, lambda s: (0, 0))
    res, landed = _call(
        body, [proj, proj, proj, proj, proj, proj, proj, d_ag, qw_row, kw_row, gmat, sinks, dproj], name=name,
        out_shape=[jax.ShapeDtypeStruct(dproj.shape, BF16),
                   jax.ShapeDtypeStruct(qw_row.shape, F32), jax.ShapeDtypeStruct(kw_row.shape, F32),
                   jax.ShapeDtypeStruct((SUBLANES, LANES), F32)],
        grid=(nb,),
        in_specs=_attn_specs(attn_w, kv_w, rev) + [pl.BlockSpec((BLOCK, attn_w), lambda s: (rev(s), 0)),
                                                   const(qw_row), const(kw_row), const(gmat),
                                                   pl.BlockSpec(memory_space=pltpu.SMEM), _any_spec()],
        out_specs=[pl.BlockSpec((BLOCK, w_out), lambda s: (rev(s), 0)),
                   const(qw_row), const(kw_row), pl.BlockSpec((SUBLANES, LANES), lambda s: (0, 0))],
        scratch_shapes=[pltpu.VMEM((BLOCK, 2 * kv_w), F32)],
        aliases={12: 0}, semantics=("arbitrary",), vmem=48 * 1024 * 1024, rider=rider)
    return res if rider is None else (res, landed)


def _cmul(ar, ai, br, bi):
    return ar * br - ai * bi, ar * bi + ai * br


def _ssm_prep(a_re, a_im, log_dt_col, steps, name):
    def body(are_ref, aim_ref, ldt_ref, abr_ref, abi_ref, cfr_ref, cfi_ref, apr_ref, api_ref, pwr_ref, pwi_ref):
        are, aim = are_ref[...], aim_ref[...]
        dt = jnp.exp(ldt_ref[...])
        mag = jnp.exp(dt * are)
        abr = mag * jnp.cos(dt * aim)
        abi = mag * jnp.sin(dt * aim)
        num_re, num_im = abr - 1.0, abi
        den = are * are + aim * aim
        abr_ref[...] = abr
        abi_ref[...] = abi
        cfr_ref[...] = (num_re * are + num_im * aim) / den
        cfi_ref[...] = (num_im * are - num_re * aim) / den
        pr, pi = jnp.ones_like(abr), jnp.zeros_like(abr)
        for k in range(steps):
            pwr_ref[k] = pr
            pwi_ref[k] = pi
            pr, pi = _cmul(pr, pi, abr, abi)
        apr_ref[...] = pr
        api_ref[...] = pi

    shp = jax.ShapeDtypeStruct(a_re.shape, F32)
    pows = jax.ShapeDtypeStruct((steps,) + a_re.shape, F32)
    return _pallas(body, name=name, out_shape=[shp] * 6 + [pows] * 2)(a_re, a_im, log_dt_col)


def _ssm_param_bwd(a_re, a_im, log_dt_col, d_ab_re, d_ab_im, b_re, b_im, dbt_re, dbt_im, name):
    def body(are_ref, aim_ref, ldt_ref, gabr_ref, gabi_ref, br_ref, bi_ref, tr_ref, ti_ref,
             dar_ref, dai_ref, dldt_ref, dbr_ref, dbi_ref):
        are, aim = are_ref[...], aim_ref[...]
        dt = jnp.exp(ldt_ref[...])
        mag = jnp.exp(dt * are)
        abr = mag * jnp.cos(dt * aim)
        abi = mag * jnp.sin(dt * aim)
        den = are * are + aim * aim
        cfr = ((abr - 1.0) * are + abi * aim) / den
        cfi = (abi * are - (abr - 1.0) * aim) / den
        gabr, gabi = jnp.sum(gabr_ref[...], axis=0), jnp.sum(gabi_ref[...], axis=0)
        t_re, t_im = tr_ref[...], ti_ref[...]
        g_r, g_i = _cmul(br_ref[...], -bi_ref[...], t_re, t_im)
        gcfr, gcfi = jnp.sum(g_r, axis=1), jnp.sum(g_i, axis=1)
        dbr, dbi = _cmul(cfr[:, None, :], -cfi[:, None, :], t_re, t_im)
        dbr_ref[...] = dbr
        dbi_ref[...] = dbi
        inv_r, inv_i = are / den, -aim / den
        t_r, t_i = _cmul(inv_r, -inv_i, gcfr, gcfi)
        gabr, gabi = gabr + t_r, gabi + t_i
        q_r, q_i = _cmul(cfr, cfi, inv_r, inv_i)
        da_r, da_i = _cmul(-q_r, q_i, gcfr, gcfi)
        gz_r, gz_i = _cmul(abr, -abi, gabr, gabi)
        dar_ref[...] = da_r + dt * gz_r
        dai_ref[...] = da_i + dt * gz_i
        dldt_ref[...] = dt * jnp.sum(are * gz_r + aim * gz_i, axis=-1, keepdims=True)

    shp = jax.ShapeDtypeStruct(a_re.shape, F32)
    bshp = jax.ShapeDtypeStruct(b_re.shape, F32)
    return _pallas(body, name=name,
                   out_shape=[shp, shp, jax.ShapeDtypeStruct(log_dt_col.shape, F32), bshp, bshp])(
        a_re, a_im, log_dt_col, d_ab_re, d_ab_im, b_re, b_im, dbt_re, dbt_im)


SCAN_LANES = 512
SSM_CHUNK = 256
W_IN_PARTS = 2


def _scan_segments(xr_ref, xi_ref, a_re, a_im, ap_re, ap_im, pw_re, pw_im, carry_re, carry_im, cm_re, cm_im, steps,
                   reverse, base):
    n = xr_ref.shape[1]
    seg_order = range(SUBLANES - 1, -1, -1) if reverse else range(SUBLANES)
    sign = -1.0 if reverse else 1.0
    for c0 in range(0, n, SCAN_LANES):
        ls = slice(c0, c0 + SCAN_LANES)
        gs = slice(base + c0, base + c0 + SCAN_LANES)
        ar = jnp.broadcast_to(a_re[:, gs], (SUBLANES, SCAN_LANES))
        ai = jnp.broadcast_to(a_im[:, gs], (SUBLANES, SCAN_LANES))
        end_r = jnp.zeros((SUBLANES, SCAN_LANES), F32)
        end_i = jnp.zeros((SUBLANES, SCAN_LANES), F32)
        for j in range(steps):
            k = j if reverse else steps - 1 - j
            rws = slice(j * SUBLANES, (j + 1) * SUBLANES)
            tr, ti = _cmul(pw_re[k:k + 1, gs], sign * pw_im[k:k + 1, gs], xr_ref[rws, ls], xi_ref[rws, ls])
            end_r, end_i = end_r + tr, end_i + ti
        cr, ci = carry_re[:, gs], carry_im[:, gs]
        apr, api = ap_re[:, gs], ap_im[:, gs]
        for r in seg_order:
            cm_re[r:r + 1, gs] = cr
            cm_im[r:r + 1, gs] = ci
            tr, ti = _cmul(apr, api, cr, ci)
            cr, ci = end_r[r:r + 1, :] + tr, end_i[r:r + 1, :] + ti
        carry_re[:, gs] = cr
        carry_im[:, gs] = ci

        def run(t, s, ar=ar, ai=ai, ls=ls):
            j = steps - 1 - t if reverse else t
            r0 = pl.multiple_of(j * SUBLANES, SUBLANES)
            sr, si = _cmul(ar, ai, s[0], s[1])
            sr = sr + xr_ref[pl.ds(r0, SUBLANES), ls]
            si = si + xi_ref[pl.ds(r0, SUBLANES), ls]
            xr_ref[pl.ds(r0, SUBLANES), ls] = sr
            xi_ref[pl.ds(r0, SUBLANES), ls] = si
            return sr, si

        lax.fori_loop(0, steps, run, (cm_re[:, gs], cm_im[:, gs]))


SB_GROUPS = MXU_DIM // GROUP
SB_STATE = SB_GROUPS * STATE


def _ssm_rows(m):
    flat = m.reshape(-1, STATE).astype(F32)
    return jnp.concatenate([flat, flat], axis=1)


def _from_ssm_rows(rows):
    return rows[:, :STATE].reshape(-1, GROUP, STATE)


def _own_group(shape):
    row_g = lax.broadcasted_iota(jnp.int32, shape, 0) // GROUP
    col_g = lax.broadcasted_iota(jnp.int32, shape, 1) // STATE
    return row_g == col_g


def _block_diagonal(rows):
    tiled = jnp.concatenate([rows] * (SB_STATE // LANES), axis=1)
    return jnp.where(_own_group(tiled.shape), tiled, 0.0).astype(BF16)


def _block_rows(acc):
    x = jnp.where(_own_group(acc.shape), acc, 0.0)
    while x.shape[1] > LANES:
        half = x.shape[1] // 2
        x = x[:, :half] + x[:, half:]
    return x + pltpu.roll(x, STATE, axis=1)


def _rows_to_segments(dst, srcs, steps, stage):
    for ref, off in srcs:
        for k in range(ref.shape[1] // LANES):
            stage[off // LANES + k] = ref[:, k * LANES:(k + 1) * LANES].astype(F32)
    for k in range(dst.shape[1] // LANES):
        for j in range(steps):
            dst[j * SUBLANES:(j + 1) * SUBLANES, k * LANES:(k + 1) * LANES] = (
                stage[k, pl.ds(j, SUBLANES, stride=steps), :])


def _segments_to_rows(dst, src, steps, stage):
    for k in range(src.shape[1] // LANES):
        for j in range(steps):
            stage[k, pl.ds(j, SUBLANES, stride=steps), :] = (
                src[j * SUBLANES:(j + 1) * SUBLANES, k * LANES:(k + 1) * LANES])
    for k in range(src.shape[1] // LANES):
        dst[:, k * LANES:(k + 1) * LANES] = stage[k]


def _u_specs(w, o_u, chunk, index):
    half = w // 2
    assert o_u % half == 0
    return [pl.BlockSpec((chunk, half), lambda c, k=k: (index(c), o_u // half + k)) for k in range(2)]


def _ssm_fwd(proj, o_u, bc_rows, rows_p, d_row, *, chunk, name, rider=None):
    rows = proj.shape[0]
    w = d_row.shape[1]
    nc = rows // chunk
    steps = chunk // SUBLANES
    nsb = w // MXU_DIM
    n_state = nsb * SB_STATE

    def body(ulo_ref, uhi_ref, b2r_ref, b2i_ref, c2r_ref, c2i_ref, abr_ref, abi_ref, cfr_ref, cfi_ref, apr_ref,
             api_ref, pwr_ref, pwi_ref, d_ref, y_ref, str_ref, sti_ref, yg_ref, bre_ref, bim_ref, cre_ref, cim_ref,
             useg, yseg, stage, sr, si,
             carry_r, carry_i, cm_r, cm_i):
        @pl.when(pl.program_id(0) == 0)
        def _():
            for src, dst in ((b2r_ref, bre_ref), (b2i_ref, bim_ref), (c2r_ref, cre_ref), (c2i_ref, cim_ref)):
                for sb in range(nsb):
                    dst[sb] = _block_diagonal(src[sb * MXU_DIM:(sb + 1) * MXU_DIM, :])
            carry_r[...] = jnp.zeros_like(carry_r)
            carry_i[...] = jnp.zeros_like(carry_i)

        str_ref[0] = carry_r[...]
        sti_ref[0] = carry_i[...]
        _rows_to_segments(useg, [(ulo_ref, 0), (uhi_ref, w // 2)], steps, stage)
        for sb in range(nsb):
            us = slice(sb * MXU_DIM, (sb + 1) * MXU_DIM)
            ss = slice(sb * SB_STATE, (sb + 1) * SB_STATE)
            ub = useg[:, us].astype(BF16)
            bur = _dot(ub, bre_ref[sb], NN)
            bui = _dot(ub, bim_ref[sb], NN)
            xr, xi = _cmul(cfr_ref[:, ss], cfi_ref[:, ss], bur, bui)
            sr[...] = xr
            si[...] = xi
            _scan_segments(sr, si, abr_ref[...], abi_ref[...], apr_ref[...], api_ref[...], pwr_ref, pwi_ref,
                           carry_r, carry_i, cm_r, cm_i, steps, False, sb * SB_STATE)
            y = _dot(sr[...].astype(BF16), cre_ref[sb], NT) - _dot(si[...].astype(BF16), cim_ref[sb], NT)
            yseg[:, us] = y + d_ref[:, us] * useg[:, us]
        _segments_to_rows(y_ref, yseg, steps, stage)
        yg_ref[...] = _gelu(y_ref[...]).astype(BF16)

    const = lambda a: pl.BlockSpec(a.shape, lambda c: (0,) * a.ndim)
    row_n = pl.BlockSpec((1, n_state), lambda c: (0, 0))
    st = pl.BlockSpec((1, 1, n_state), lambda c: (c, 0, 0))
    held = [pltpu.VMEM((nsb, MXU_DIM, SB_STATE), BF16)] * 4
    vmem = (4 * _nbytes((nsb, MXU_DIM, SB_STATE), BF16) + 4 * _nbytes((chunk, SB_STATE), F32)
            + 12 * _nbytes((chunk, w), F32) + 8 * _nbytes(bc_rows[0].shape, F32))
    res, landed = _call(
        body, [proj, proj, *bc_rows, *rows_p, d_row], name=name,
        out_shape=[jax.ShapeDtypeStruct((rows, w), F32), jax.ShapeDtypeStruct((nc, 1, n_state), F32),
                   jax.ShapeDtypeStruct((nc, 1, n_state), F32), jax.ShapeDtypeStruct((rows, w), BF16)],
        grid=(nc,),
        in_specs=_u_specs(w, o_u, chunk, lambda c: c) + [const(b) for b in bc_rows]
        + [row_n] * 6 + [pl.BlockSpec((steps, n_state), lambda c: (0, 0))] * 2 + [pl.BlockSpec((1, w), lambda c: (0, 0))],
        out_specs=[pl.BlockSpec((chunk, w), lambda c: (c, 0)), st, st, pl.BlockSpec((chunk, w), lambda c: (c, 0))],
        scratch_shapes=held + [pltpu.VMEM((chunk, w), F32), pltpu.VMEM((chunk, w), F32),
                               pltpu.VMEM((w // LANES, chunk, LANES), F32),
                               pltpu.VMEM((chunk, SB_STATE), F32), pltpu.VMEM((chunk, SB_STATE), F32),
                               pltpu.VMEM((1, n_state), F32), pltpu.VMEM((1, n_state), F32),
                               pltpu.VMEM((SUBLANES, n_state), F32), pltpu.VMEM((SUBLANES, n_state), F32)],
        semantics=("arbitrary",), vmem=vmem, rider=rider)
    return res if rider is None else (res, landed)


def _ssm_bwd(proj, o_u, y, dyg, st_re, st_im, bc_rows, rows_p, d_row, *, chunk, name, rider=None):
    rows = proj.shape[0]
    w = d_row.shape[1]
    nc = rows // chunk
    steps = chunk // SUBLANES
    nsb = w // MXU_DIM
    n_state = nsb * SB_STATE

    def body(ulo_ref, uhi_ref, y_ref, dyg_ref, str_ref, sti_ref, b2r_ref, b2i_ref, c2r_ref, c2i_ref, t2r_ref,
             t2i_ref, abr_ref, abi_ref, cfr_ref, cfi_ref, apr_ref, api_ref, pwr_ref, pwi_ref, d_ref,
             du_ref, gb2r_ref, gb2i_ref, gc2r_ref, gc2i_ref, gabr_ref, gabi_ref, dd_ref,
             bre_ref, bim_ref, cre_ref, cim_ref, btr_ref, bti_ref, dbre_ref, dbim_ref, dcre_ref, dcim_ref,
             useg, dyseg, dynat, stage, sr, si, lr, li, carry_r, carry_i, lam_r, lam_i, cm_r, cm_i, cl_r, cl_i):
        first = pl.program_id(0) == 0

        @pl.when(first)
        def _():
            for src, dst in ((b2r_ref, bre_ref), (b2i_ref, bim_ref), (c2r_ref, cre_ref), (c2i_ref, cim_ref),
                             (t2r_ref, btr_ref), (t2i_ref, bti_ref)):
                for sb in range(nsb):
                    dst[sb] = _block_diagonal(src[sb * MXU_DIM:(sb + 1) * MXU_DIM, :])
            lam_r[...] = jnp.zeros_like(lam_r)
            lam_i[...] = jnp.zeros_like(lam_i)
            for ref in (dbre_ref, dbim_ref, dcre_ref, dcim_ref, gabr_ref, gabi_ref, dd_ref):
                ref[...] = jnp.zeros_like(ref)

        dynat[...] = dyg_ref[...].astype(F32) * _dgelu(y_ref[...])
        half = w // 2
        dd_ref[:, :half] += jnp.sum(dynat[:, :half] * ulo_ref[...].astype(F32), axis=0, keepdims=True)
        dd_ref[:, half:] += jnp.sum(dynat[:, half:] * uhi_ref[...].astype(F32), axis=0, keepdims=True)
        _rows_to_segments(useg, [(ulo_ref, 0), (uhi_ref, half)], steps, stage)
        _rows_to_segments(dyseg, [(dynat, 0)], steps, stage)
        dy = dyseg[...]
        dyb = dy.astype(BF16)
        ub = useg[...].astype(BF16)
        carry_r[...] = str_ref[0]
        carry_i[...] = sti_ref[0]
        abr, abi = abr_ref[...], abi_ref[...]
        apr, api = apr_ref[...], api_ref[...]
        for sb in range(nsb):
            us = slice(sb * MXU_DIM, (sb + 1) * MXU_DIM)
            ss = slice(sb * SB_STATE, (sb + 1) * SB_STATE)
            base = sb * SB_STATE
            br = _dot(ub[:, us], bre_ref[sb], NN)
            bi = _dot(ub[:, us], bim_ref[sb], NN)
            xr, xi = _cmul(cfr_ref[:, ss], cfi_ref[:, ss], br, bi)
            sr[...] = xr
            si[...] = xi
            lr[...] = _dot(dyb[:, us], cre_ref[sb], NN)
            li[...] = -_dot(dyb[:, us], cim_ref[sb], NN)
            _scan_segments(sr, si, abr, abi, apr, api, pwr_ref, pwi_ref, carry_r, carry_i, cm_r, cm_i, steps, False,
                           base)
            dcre_ref[sb] += _dot(dyb[:, us], sr[...].astype(BF16), TN)
            dcim_ref[sb] -= _dot(dyb[:, us], si[...].astype(BF16), TN)
            _scan_segments(lr, li, abr, -abi, apr, -api, pwr_ref, pwi_ref, lam_r, lam_i, cl_r, cl_i, steps, True,
                           base)
            for c0 in range(0, SB_STATE, SCAN_LANES):
                ls = slice(c0, c0 + SCAN_LANES)
                gs = slice(base + c0, base + c0 + SCAN_LANES)

                def step(j, acc, ls=ls):
                    gar, gai, pr, pi = acc
                    r0 = pl.multiple_of(j * SUBLANES, SUBLANES)
                    rws = pl.ds(r0, SUBLANES)
                    t_r, t_i = _cmul(pr, -pi, lr[rws, ls], li[rws, ls])
                    return gar + t_r, gai + t_i, sr[rws, ls], si[rws, ls]

                zero = jnp.zeros((SUBLANES, SCAN_LANES), F32)
                gar, gai, _, _ = lax.fori_loop(0, steps, step, (zero, zero, cm_r[:, gs], cm_i[:, gs]))
                gabr_ref[:, gs] += gar
                gabi_ref[:, gs] += gai
            xr, xi = lr[...].astype(BF16), li[...].astype(BF16)
            du = _dot(xr, btr_ref[sb], NT) + _dot(xi, bti_ref[sb], NT)
            useg[:, us] = du + d_ref[:, us] * dy[:, us]
            dbre_ref[sb] += _dot(ub[:, us], xr, TN)
            dbim_ref[sb] += _dot(ub[:, us], xi, TN)
        _segments_to_rows(du_ref, useg, steps, stage)

        @pl.when(pl.program_id(0) == nc - 1)
        def _():
            for src, dst in ((dbre_ref, gb2r_ref), (dbim_ref, gb2i_ref), (dcre_ref, gc2r_ref), (dcim_ref, gc2i_ref)):
                for sb in range(nsb):
                    dst[sb * MXU_DIM:(sb + 1) * MXU_DIM, :] = _block_rows(src[sb])

    rev = lambda c: nc - 1 - c
    const = lambda a: pl.BlockSpec(a.shape, lambda c: (0,) * a.ndim)
    tile = pl.BlockSpec((chunk, w), lambda c: (rev(c), 0))
    row_n = pl.BlockSpec((1, n_state), lambda c: (0, 0))
    row_w = pl.BlockSpec((1, w), lambda c: (0, 0))
    st = pl.BlockSpec((1, 1, n_state), lambda c: (rev(c), 0, 0))
    acc8 = pl.BlockSpec((SUBLANES, n_state), lambda c: (0, 0))
    big = pltpu.VMEM((chunk, SB_STATE), F32)
    small = pltpu.VMEM((chunk, w), F32)
    row = pltpu.VMEM((1, n_state), F32)
    eight = pltpu.VMEM((SUBLANES, n_state), F32)
    blk = (nsb, MXU_DIM, SB_STATE)
    held = [pltpu.VMEM(blk, BF16)] * 6 + [pltpu.VMEM(blk, F32)] * 4
    vmem = (6 * _nbytes(blk, BF16) + 4 * _nbytes(blk, F32) + 5 * _nbytes((chunk, SB_STATE), F32)
            + 12 * _nbytes((chunk, w), F32) + 20 * _nbytes(bc_rows[0].shape, F32))
    res, landed = _call(
        body, [proj, proj, y, dyg, st_re, st_im, *bc_rows, *rows_p, d_row], name=name,
        out_shape=[jax.ShapeDtypeStruct((rows, w), F32)] + [jax.ShapeDtypeStruct(b.shape, F32) for b in bc_rows[:4]]
        + [jax.ShapeDtypeStruct((SUBLANES, n_state), F32)] * 2 + [jax.ShapeDtypeStruct((1, w), F32)],
        grid=(nc,),
        in_specs=_u_specs(w, o_u, chunk, rev) + [tile, tile, st, st] + [const(b) for b in bc_rows]
        + [row_n] * 6 + [pl.BlockSpec((steps, n_state), lambda c: (0, 0))] * 2 + [row_w],
        out_specs=[tile] + [const(b) for b in bc_rows[:4]] + [acc8] * 2 + [row_w],
        scratch_shapes=held + [small] * 3 + [pltpu.VMEM((w // LANES, chunk, LANES), F32)] + [big] * 4 + [row] * 4
        + [eight] * 4,
        semantics=("arbitrary",), vmem=vmem, rider=rider)
    return res if rider is None else (res, landed)


def _out_proj_loss(merged, w_o, x, target, name):
    rows, d = x.shape
    tm, tn = _tile(rows, 1024, SUBLANES), _tile(d, 1024)

    def body(a_ref, b_ref, x_ref, t_ref, g_ref, gb_ref, l_ref):
        err = x_ref[...] + _dot(a_ref[...], b_ref[...], NN) - t_ref[...]
        g = err * (1.0 / d)
        g_ref[...] = g
        gb_ref[...] = g.astype(BF16)
        part = jnp.sum(0.5 * err * g, axis=0, keepdims=True)
        first = pl.program_id(1) == 0

        @pl.when(first)
        def _():
            l_ref[...] = part

        @pl.when(jnp.logical_not(first))
        def _():
            l_ref[...] += part

    tile = pl.BlockSpec((tm, tn), lambda j, i: (i, j))
    vmem = 2 * (_nbytes((tm, d), BF16) + _nbytes((d, tn), BF16)) + 12 * _nbytes((tm, tn), F32)
    return _pallas(
        body, name=name,
        out_shape=[jax.ShapeDtypeStruct((rows, d), F32), jax.ShapeDtypeStruct((rows, d), BF16),
                   jax.ShapeDtypeStruct((1, d), F32)],
        grid=(d // tn, rows // tm),
        in_specs=[pl.BlockSpec((tm, d), lambda j, i: (i, 0)), pl.BlockSpec((d, tn), lambda j, i: (0, j)), tile, tile],
        out_specs=[tile, tile, pl.BlockSpec((1, tn), lambda j, i: (0, j))],
        compiler_params=_params(("parallel", "arbitrary"), vmem),
    )(merged, w_o, x, target)


def _pair_sum(grad, recv, name):
    r4, cdim = recv.shape
    r = r4 // N_CHIPS
    tr = _tile(r, 544, 16)
    g4 = grad.reshape(N_CHIPS, 2, r, cdim)
    r3 = recv.reshape(N_CHIPS, r, cdim)
    core = jnp.reshape(lax.axis_index("c"), (1,)).astype(jnp.int32)

    def body(c_ref, g_ref, r_ref, o_ref):
        o_ref[...] = (g_ref[0] + r_ref[...]).astype(BF16)

    out = _pallas(
        body, name=name, out_shape=jax.ShapeDtypeStruct((N_CHIPS, r, cdim), BF16),
        grid_spec=pltpu.PrefetchScalarGridSpec(
            num_scalar_prefetch=1, grid=(N_CHIPS, r // tr),
            in_specs=[pl.BlockSpec((1, 1, tr, cdim), lambda j, i, c: (j, c[0], i, 0)),
                      pl.BlockSpec((1, tr, cdim), lambda j, i, c: (j, i, 0))],
            out_specs=pl.BlockSpec((1, tr, cdim), lambda j, i, c: (j, i, 0))),
        compiler_params=_params(("parallel", "parallel"), 6 * _nbytes((tr, cdim), F32)),
    )(core, g4, r3)
    return out.reshape(r4, cdim)


def _chip_sum(recv, name):
    r4, cdim = recv.shape
    r = r4 // N_CHIPS
    tr = _tile(r, 544, 16)
    r3 = recv.reshape(N_CHIPS, r, cdim)

    def body(r_ref, o_ref):
        acc = r_ref[0].astype(F32)
        for j in range(1, N_CHIPS):
            acc = acc + r_ref[j].astype(F32)
        o_ref[...] = acc

    return _pallas(
        body, name=name, out_shape=jax.ShapeDtypeStruct((r, cdim), F32), grid=(r // tr,),
        in_specs=[pl.BlockSpec((N_CHIPS, tr, cdim), lambda i: (0, i, 0))],
        out_specs=pl.BlockSpec((tr, cdim), lambda i: (i, 0)),
        compiler_params=_params(("parallel",), 8 * _nbytes((tr, cdim), F32)),
    )(r3)


def _adamw_math(w, g, m, v):
    m = ADAM_B1 * m + (1.0 - ADAM_B1) * g
    v = ADAM_B2 * v + (1.0 - ADAM_B2) * (g * g)
    m_hat = m / (1.0 - ADAM_B1 ** ADAM_STEP)
    v_hat = v / (1.0 - ADAM_B2 ** ADAM_STEP)
    delta = -ADAM_LR * (m_hat / (jnp.sqrt(v_hat) + ADAM_EPS) + ADAM_WD * w)
    return delta, m, v


def _adamw(w, g, m, v, name):
    rows, cols = w.shape
    tr = _tile(rows, 256, SUBLANES)

    def body(w_ref, g_ref, m_ref, v_ref, d_ref, nm_ref, nv_ref):
        d, nm, nv = _adamw_math(w_ref[...], g_ref[...], m_ref[...], v_ref[...])
        d_ref[...] = d
        nm_ref[...] = nm
        nv_ref[...] = nv

    spec = pl.BlockSpec((tr, cols), lambda i: (i, 0))
    shp = jax.ShapeDtypeStruct((rows, cols), F32)
    return _pallas(
        body, name=name, out_shape=[shp] * 3, grid=(rows // tr,), in_specs=[spec] * 4, out_specs=[spec] * 3,
        compiler_params=_params(("parallel",)),
    )(w, g, m, v)


def _adamw_chips(w, parts, m, v, name):
    rows, cols = w.shape
    assert sum(p.shape[1] for p in parts) == cols
    tr = _tile(rows, 64, 16)
    n = len(parts)

    def body(*refs):
        w_ref, m_ref, v_ref = refs[0], refs[1 + n], refs[2 + n]
        g_ref, d_ref, nm_ref, nv_ref = refs[3 + n:]
        cols_g = []
        for p_ref in refs[1:1 + n]:
            acc = p_ref[0].astype(F32)
            for j in range(1, N_CHIPS):
                acc = acc + p_ref[j].astype(F32)
            cols_g.append(acc)
        g = cols_g[0] if n == 1 else jnp.concatenate(cols_g, axis=1)
        d, nm, nv = _adamw_math(w_ref[...], g, m_ref[...], v_ref[...])
        g_ref[...] = g
        d_ref[...] = d
        nm_ref[...] = nm
        nv_ref[...] = nv

    spec = pl.BlockSpec((tr, cols), lambda i: (i, 0))
    part_specs = [pl.BlockSpec((N_CHIPS, tr, p.shape[1]), lambda i: (0, i, 0)) for p in parts]
    shp = jax.ShapeDtypeStruct((rows, cols), F32)
    return _pallas(
        body, name=name, out_shape=[shp] * 4, grid=(rows // tr,),
        in_specs=[spec] + part_specs + [spec, spec], out_specs=[spec] * 4,
        compiler_params=_params(("parallel",)),
    )(w, *[p.reshape(N_CHIPS, rows, p.shape[1]) for p in parts], m, v)


def _adamw_small(wmv, parts, name):
    _, rows, cols = wmv.shape
    p3 = parts.reshape(N_DEV, rows, cols)

    def body(wmv_ref, p_ref, g_ref, d_ref, nm_ref, nv_ref):
        g = p_ref[0]
        for k in range(1, N_DEV):
            g = g + p_ref[k]
        d, nm, nv = _adamw_math(wmv_ref[0], g, wmv_ref[1], wmv_ref[2])
        g_ref[...] = g
        d_ref[...] = d
        nm_ref[...] = nm
        nv_ref[...] = nv

    shp = jax.ShapeDtypeStruct((rows, cols), F32)
    return _pallas(body, name=name, out_shape=[shp] * 4)(wmv, p3)


SMALL = ("norm_w", "q_norm_w", "k_norm_w", "sinks", "A_re", "A_im", "log_dt", "B_re", "B_im", "C_re", "C_im",
         "D_skip", "b_glu")
LARGE = ("w_in", "w_attn_proj", "w_glu", "w_ssm_proj", "w_out")
ORDER = ("norm_w", "w_in", "q_norm_w", "k_norm_w", "sinks", "w_attn_proj", "A_re", "A_im", "log_dt", "B_re", "B_im",
         "C_re", "C_im", "D_skip", "w_glu", "b_glu", "w_ssm_proj", "w_out")


SMALL_REST = ("loss",) + SMALL[1:]


def _pack_all(sources, keys):
    n = sum(sources[0][k].size for k in keys)
    rows = -(-n // (LANES * SUBLANES)) * SUBLANES
    pad = jnp.zeros((rows * LANES - n,), F32)
    pieces = []
    for src in sources:
        pieces += [src[k].reshape(-1).astype(F32) for k in keys] + [pad]
    return jnp.concatenate(pieces).reshape(len(sources), rows, LANES)


def _pack(named, keys):
    return _pack_all([named], keys)[0]


def _unpack(packed, like, keys):
    flat = packed.reshape(-1)
    out, o = {}, 0
    for k in keys:
        n = like[k].size
        out[k] = flat[o:o + n].reshape(like[k].shape)
        o += n
    return out


def _step(xs, target, p, shards):
    s_in, s_ap, s_glu, s_sp, s_o = shards
    seq, d = xs.shape
    attn_w = (d // 128) * HEAD_DIM
    n_q = attn_w // HEAD_DIM
    kv_w = N_KV_HEADS * HEAD_DIM
    ssm_w = d // 2
    n_groups = ssm_w // GROUP
    n_state = n_groups * STATE
    in_w = N_DEV * s_in.shape[0]
    assert in_w == 2 * attn_w + 2 * kv_w + 2 * ssm_w + 2 * d
    o_u = 2 * attn_w + 2 * kv_w
    o_z = o_u + ssm_w
    o_ga = o_z + ssm_w
    chunk = min(SSM_CHUNK, seq)
    cw = d // 4

    norm_row = p["norm_w"].reshape(1, d)
    half = d // W_IN_PARTS
    assert W_IN_PARTS == 2
    s_in_parts = [s_in[:, :half], s_in[:, half:]]
    h, (w_lo,) = _rmsnorm_fwd(xs, norm_row, "rmsnorm_fwd", rider=_all_gather(s_in_parts[:1]))
    part, (w_hi,) = _matmul(Cols(h, 0, half), w_lo, mode="nt", name="in_proj_0", tn=2176, out_dtype=BF16,
                            rider=_all_gather(s_in_parts[1:]))
    proj = _matmul(Cols(h, half, half), w_hi, mode="nt", name="in_proj_1", tn=2176, out_dtype=BF16, add=part)
    w_in_parts = [w_lo, w_hi]
    qw_row = jnp.tile(p["q_norm_w"], n_q).reshape(1, attn_w)
    kw_row = jnp.tile(p["k_norm_w"], N_KV_HEADS).reshape(1, kv_w)
    gmat = _head_mean_matrix()
    ag = _attention_fwd(proj, qw_row, kw_row, gmat, p["sinks"], attn_w=attn_w, kv_w=kv_w, name="attention_fwd")

    log_dt_col = p["log_dt"].reshape(n_groups, 1)
    prep = _ssm_prep(p["A_re"], p["A_im"], log_dt_col, chunk // SUBLANES, "ssm_prep")
    rows_p = [v.reshape(1, n_state) for v in prep[:6]] + [v.reshape(-1, n_state) for v in prep[6:]]
    bt_re, bt_im = p["B_re"].transpose(0, 2, 1), p["B_im"].transpose(0, 2, 1)
    cf_re, cf_im = prep[2][:, None, :], prep[3][:, None, :]
    bc_rows = [_ssm_rows(m) for m in (bt_re, bt_im, p["C_re"], p["C_im"],
                                      cf_re * bt_re - cf_im * bt_im, cf_re * bt_im + cf_im * bt_re)]
    d_row = p["D_skip"].reshape(1, ssm_w)
    (y_ssm, st_re, st_im, yg), (w_ap_t, w_glu_t, w_sp_t, w_o) = _ssm_fwd(
        proj, o_u, bc_rows[:4], rows_p, d_row, chunk=chunk, name="ssm_fwd",
        rider=_all_gather([s_ap, s_glu, s_sp, s_o]))
    glu = _matmul(yg, w_glu_t, mode="nt", name="glu_proj", out_dtype=BF16, bias=p["b_glu"].reshape(1, 2 * ssm_w))
    (ts,) = _ew(lambda ga, gb, z: ga * _sigmoid(gb) * _silu(z), name="glu_gate", rows=seq, width=ssm_w,
                tiles=[(glu, 0), (glu, ssm_w), (proj, o_z)], outs=[(BF16, ssm_w, 0)], cw=cw)
    yy = _matmul(ag, w_ap_t, mode="nt", name="attn_proj", out_dtype=BF16, out_cols=(2 * d, 0))
    yy = _matmul(ts, w_sp_t, mode="nt", name="ssm_proj", out_dtype=BF16, out_cols=(2 * d, d), into=yy)
    (merged,) = _ew(lambda ya, ys, ga, gs: _sigmoid(ga) * ya + _sigmoid(gs) * ys, name="merge", rows=seq, width=d,
                    tiles=[(yy, 0), (yy, d), (proj, o_ga), (proj, o_ga + d)], outs=[(BF16, d, 0)], cw=cw)
    dout, dout_b, loss_cols = _out_proj_loss(merged, w_o, xs, target, "out_proj_loss")
    loss_local = jnp.sum(loss_cols)

    g_w_o = _matmul(merged, dout_b, mode="tn", name="grad_w_out", tm=512, tk=4096)
    dmerged = _matmul(dout_b, w_o, mode="nt", name="d_merged", out_dtype=BF16)

    def merge_bwd(dm, y, g):
        s = _sigmoid(g)
        return dm * s, dm * y * s * (1.0 - s)

    dyy, dproj = _ew(merge_bwd, name="merge_bwd", rows=seq, width=2 * d,
                     tiles=[(dmerged, 0, d), (yy, 0), (proj, o_ga)],
                     outs=[(BF16, 2 * d, 0), (BF16, in_w, o_ga)], cw=cw)
    dy_a, dy_s = Cols(dyy, 0, d), Cols(dyy, d, d)
    g_w_ap_t = _matmul(dy_a, ag, mode="tn", name="grad_w_attn_proj", tm=512, tk=4096)
    g_w_sp_t = _matmul(dy_s, ts, mode="tn", name="grad_w_ssm_proj", tm=512, tk=4096)
    d_ag = _matmul(dy_a, w_ap_t, mode="nn", name="d_attn_gated", out_dtype=BF16)
    d_ts = _matmul(dy_s, w_sp_t, mode="nn", name="d_ssm_gated", out_dtype=BF16)

    (dproj, g_qw, g_kw, g_sinks), (sib_o, sib_ap, sib_sp) = _attention_bwd(
        proj, d_ag, dproj, qw_row, kw_row, gmat, p["sinks"], attn_w=attn_w, kv_w=kv_w, name="attention_bwd",
        rider=_sibling_exchange([g_w_o, g_w_ap_t, g_w_sp_t]))
    pair_o = _pair_sum(g_w_o, sib_o, "pair_sum_w_out")
    pair_ap = _pair_sum(g_w_ap_t, sib_ap, "pair_sum_w_attn_proj")
    pair_sp = _pair_sum(g_w_sp_t, sib_sp, "pair_sum_w_ssm_proj")

    n_half = ssm_w // _tile(2 * ssm_w, cw)

    def glu_bwd(j, dt, ga, gb, z):
        sb, sz = _sigmoid(gb), _silu(z)
        dg = jnp.where(j < n_half, dt * sb * sz, dt * ga * sb * (1.0 - sb) * sz)
        return dg, dg

    glu_ops = [(d_ts, 0, ssm_w), (glu, 0, ssm_w), (glu, ssm_w, ssm_w), (proj, o_z, ssm_w)]
    dglu, g_bglu = _ew(glu_bwd, name="glu_bwd", rows=seq, width=2 * ssm_w, tiles=glu_ops,
                       outs=[(BF16, 2 * ssm_w, 0)], accs=1, cw=cw, with_col=True)
    (dproj,) = _ew(lambda dt, ga, gb, z: dt * ga * _sigmoid(gb) * _dsilu(z), name="glu_bwd_z", rows=seq,
                   width=ssm_w, tiles=glu_ops, outs=[(BF16, in_w, o_z)], into=[dproj], cw=cw)
    g_w_glu_t = _matmul(dglu, yg, mode="tn", name="grad_w_glu", tm=512, tk=4096)
    d_yg = _matmul(dglu, w_glu_t, mode="nn", name="d_gelu", out_dtype=BF16)
    ((du, dbt_re, dbt_im, dc_re, dc_im, gabr, gabi, g_d), (chips_o, chips_ap, chips_sp, sib_glu)) = _ssm_bwd(
        proj, o_u, y_ssm, d_yg, st_re, st_im, bc_rows, rows_p, d_row, chunk=chunk, name="ssm_bwd",
        rider=_join(_chip_exchange([pair_o, pair_ap, pair_sp]), _sibling_exchange([g_w_glu_t])))
    pair_glu = _pair_sum(g_w_glu_t, sib_glu, "pair_sum_w_glu")
    (dproj,) = _ew(lambda v: v, name="du_store", rows=seq, width=ssm_w, tiles=[(du, 0)],
                   outs=[(BF16, in_w, o_u)], into=[dproj], cw=cw)
    g_a_re, g_a_im, g_log_dt, g_bt_re, g_bt_im = _ssm_param_bwd(
        p["A_re"], p["A_im"], log_dt_col, *[g.reshape(SUBLANES, n_groups, STATE) for g in (gabr, gabi)],
        bt_re, bt_im, _from_ssm_rows(dbt_re), _from_ssm_rows(dbt_im), "ssm_param_bwd")
    small_grads = dict(
        loss=loss_local, q_norm_w=g_qw.reshape(n_q, HEAD_DIM).sum(0), k_norm_w=g_kw.reshape(N_KV_HEADS, HEAD_DIM).sum(0),
        sinks=g_sinks[0, :n_q], A_re=g_a_re, A_im=g_a_im, log_dt=g_log_dt.reshape(n_groups),
        B_re=g_bt_re.transpose(0, 2, 1), B_im=g_bt_im.transpose(0, 2, 1),
        C_re=_from_ssm_rows(dc_re), C_im=_from_ssm_rows(dc_im),
        D_skip=g_d.reshape(n_groups, GROUP), b_glu=g_bglu.reshape(2 * ssm_w))

    n_parts = W_IN_PARTS
    wq = d // n_parts
    g_parts, pair_parts, chip_parts = [], [], []
    extra = [_chip_exchange([pair_glu]), _all_gather([_pack(small_grads, SMALL_REST)])]
    chips_glu = small_parts = dh = None
    for step in range(n_parts + 2):
        riders = list(extra) if step == 0 else []
        if 0 <= step - 2 < n_parts:
            riders.append(_chip_exchange([pair_parts[step - 2]]))
        if 0 <= step - 1 < n_parts:
            riders.append(_sibling_exchange([g_parts[step - 1]]))
        rider = _join(*riders) if riders else None
        if step < n_parts:
            res = _matmul(dproj, Cols(h, step * wq, wq), mode="tn", name="grad_w_in_%d" % step, tk=4096, rider=rider)
            out, landed = res if rider is not None else (res, [])
            g_parts.append(out)
        else:
            q = step - n_parts
            dh, landed = _matmul(dproj, w_in_parts[q], mode="nn", name="d_normed_%d" % q, tk=2176,
                                 out_cols=(d, q * wq), into=dh, rider=rider)
        landed = list(landed)
        if step == 0:
            chips_glu, small_parts = landed[:2]
            landed = landed[2:]
        if 0 <= step - 2 < n_parts:
            chip_parts.append(landed.pop(0))
        if 0 <= step - 1 < n_parts:
            pair_parts.append(_pair_sum(g_parts[step - 1], landed.pop(0), "pair_sum_w_in_%d" % (step - 1)))
    grad_x, g_norm = _rmsnorm_bwd(xs, norm_row, dh, dout, "rmsnorm_bwd")
    (norm_parts,) = _exchange(_all_gather([_pack(dict(norm_w=g_norm), ("norm_w",))]), "gather_norm_grad")
    from_chips = dict(zip(LARGE, (chip_parts, [chips_ap], [chips_glu], [chips_sp], [chips_o])))
    return grad_x, from_chips, small_parts, norm_parts


def kernel(x, norm_w, w_in, q_norm_w, k_norm_w, sinks, w_attn_proj, A_re, A_im, log_dt, B_re, B_im, C_re, C_im, D_skip, w_glu, b_glu, w_ssm_proj, w_out, loss_target, m_norm_w, m_w_in, m_q_norm_w, m_k_norm_w, m_sinks, m_w_attn_proj, m_A_re, m_A_im, m_log_dt, m_B_re, m_B_im, m_C_re, m_C_im, m_D_skip, m_w_glu, m_b_glu, m_w_ssm_proj, m_w_out, v_norm_w, v_w_in, v_q_norm_w, v_k_norm_w, v_sinks, v_w_attn_proj, v_A_re, v_A_im, v_log_dt, v_B_re, v_B_im, v_C_re, v_C_im, v_D_skip, v_w_glu, v_b_glu, v_w_ssm_proj, v_w_out):
    weights = dict(norm_w=norm_w, w_in=w_in, q_norm_w=q_norm_w, k_norm_w=k_norm_w, sinks=sinks,
                   w_attn_proj=w_attn_proj, A_re=A_re, A_im=A_im, log_dt=log_dt, B_re=B_re, B_im=B_im, C_re=C_re,
                   C_im=C_im, D_skip=D_skip, w_glu=w_glu, b_glu=b_glu, w_ssm_proj=w_ssm_proj, w_out=w_out)
    m_in = dict(norm_w=m_norm_w, w_in=m_w_in, q_norm_w=m_q_norm_w, k_norm_w=m_k_norm_w, sinks=m_sinks,
                w_attn_proj=m_w_attn_proj, A_re=m_A_re, A_im=m_A_im, log_dt=m_log_dt, B_re=m_B_re, B_im=m_B_im,
                C_re=m_C_re, C_im=m_C_im, D_skip=m_D_skip, w_glu=m_w_glu, b_glu=m_b_glu, w_ssm_proj=m_w_ssm_proj,
                w_out=m_w_out)
    v_in = dict(norm_w=v_norm_w, w_in=v_w_in, q_norm_w=v_q_norm_w, k_norm_w=v_k_norm_w, sinks=v_sinks,
                w_attn_proj=v_w_attn_proj, A_re=v_A_re, A_im=v_A_im, log_dt=v_log_dt, B_re=v_B_re, B_im=v_B_im,
                C_re=v_C_re, C_im=v_C_im, D_skip=v_D_skip, w_glu=v_w_glu, b_glu=v_b_glu, w_ssm_proj=v_w_ssm_proj,
                w_out=v_w_out)

    _, seq, d = x.shape
    column_sharded = LARGE[:4]
    as_rows = lambda k, a: a.T if k in column_sharded else a
    shards = [as_rows(k, weights[k]).astype(BF16) for k in LARGE]
    small = {k: weights[k] for k in SMALL}
    grad_x, from_chips, small_parts, norm_parts = _step(x.reshape(seq, d), loss_target.reshape(seq, d), small,
                                                        shards)

    grads, delta, new_m, new_v = {}, {}, {}, {}
    for k in LARGE:
        if k == "w_in":
            res = _adamw_chips(weights[k].T, from_chips[k], m_in[k].T, v_in[k].T, "adamw_" + k)
            grads[k], delta[k], new_m[k], new_v[k] = [a.T for a in res]
        elif k == "w_out":
            grads[k], delta[k], new_m[k], new_v[k] = _adamw_chips(weights[k], from_chips[k], m_in[k], v_in[k],
                                                                  "adamw_" + k)
        else:
            grads[k] = _chip_sum(from_chips[k][0], "chip_sum_" + k).T
            delta[k], new_m[k], new_v[k] = _adamw(weights[k], grads[k], m_in[k], v_in[k], "adamw_" + k)

    zero = jnp.zeros((), F32)
    for keys, parts in ((SMALL_REST, small_parts), (("norm_w",), norm_parts)):
        like = dict(small, loss=zero)
        wmv = _pack_all([dict(src, loss=zero) for src in (weights, m_in, v_in)], keys)
        res = _adamw_small(wmv, parts, "adamw_small_%d" % len(keys))
        for dst, r in zip((grads, delta, new_m, new_v), res):
            dst.update(_unpack(r, like, keys))
    loss = grads["loss"]

    return (loss, grad_x.reshape(x.shape), *[grads[k] for k in ORDER], *[delta[k] for k in ORDER],
            *[new_m[k] for k in ORDER], *[new_v[k] for k in ORDER])
```

```python
import math
from typing import Callable, NamedTuple

import jax
import jax.numpy as jnp
import numpy as np
from jax import lax
from jax.experimental import pallas as pl
from jax.experimental.pallas import tpu as pltpu

F32 = jnp.float32
BF16 = jnp.bfloat16
MESH = pl.DeviceIdType.MESH

HEAD_DIM = 64
N_KV_HEADS = 4
GROUP = 16
STATE = 64
BLOCK = 128
NORM_EPS = 1e-6
N_DEV = 8
N_CHIPS = 4
LANES = 128
SUBLANES = 8
MXU_DIM = 256
VMEM_BYTES = 64 * 1024 * 1024
VMEM_CAP = VMEM_BYTES - 8 * 1024 * 1024

ADAM_LR = 0.001
ADAM_B1 = 0.9
ADAM_B2 = 0.999
ADAM_EPS = 1e-08
ADAM_WD = 0.01
ADAM_STEP = 10

GELU_C = math.sqrt(2.0 / math.pi)
GELU_K = 0.044715


def _tile(dim, pref, mult=LANES):
    if dim <= pref:
        return dim
    best = None
    for d in range(mult, pref + 1, mult):
        if dim % d == 0:
            best = d
    assert best is not None, (dim, pref, mult)
    return best


def _params(semantics=None, vmem=None):
    kw = {}
    if semantics is not None:
        kw["dimension_semantics"] = semantics
    if vmem is not None:
        kw["vmem_limit_bytes"] = int(min(VMEM_CAP, max(vmem, 32 * 1024 * 1024)))
    return pltpu.CompilerParams(**kw)


def _nbytes(shape, dtype):
    return math.prod(shape) * jnp.dtype(dtype).itemsize


def _sigmoid(x):
    return 1.0 / (1.0 + jnp.exp(-x))


def _silu(x):
    return x * _sigmoid(x)


def _dsilu(x):
    s = _sigmoid(x)
    return s * (1.0 + x * (1.0 - s))


def _gelu(x):
    return 0.5 * x * (1.0 + jnp.tanh(GELU_C * (x + GELU_K * x * x * x)))


def _dgelu(x):
    t = jnp.tanh(GELU_C * (x + GELU_K * x * x * x))
    return 0.5 * (1.0 + t) + 0.5 * x * (1.0 - t * t) * GELU_C * (1.0 + 3.0 * GELU_K * x * x)


def _dot(a, b, dims):
    return lax.dot_general(a, b, (dims, ((), ())), preferred_element_type=F32)


NN = ((1,), (0,))
NT = ((1,), (1,))
TN = ((0,), (0,))


def _any_spec():
    return pl.BlockSpec(memory_space=pl.ANY)


def _pallas(body, **kw):
    pin = lambda s: pltpu.HBM(s.shape, s.dtype) if isinstance(s, jax.ShapeDtypeStruct) else s
    out_shape = kw.pop("out_shape")
    out_shape = [pin(s) for s in out_shape] if isinstance(out_shape, (list, tuple)) else pin(out_shape)
    call = pl.pallas_call(body, out_shape=out_shape, **kw)

    def run(*operands):
        pinned = [pltpu.with_memory_space_constraint(o, pltpu.HBM) if jnp.issubdtype(o.dtype, jnp.floating) else o
                  for o in operands]
        return call(*pinned)

    return run


class Rider(NamedTuple):
    operands: tuple
    out_shapes: tuple
    sems: tuple
    start: Callable
    finish: Callable


def _all_gather(shards):
    n = len(shards)

    def copies(ins, outs, sems):
        send_sems, recv_sems, local_sems = sems
        x, y, c = lax.axis_index("x"), lax.axis_index("y"), lax.axis_index("c")
        me, sibling = (x, y, c), (x, y, 1 - c)
        chips = [(1 - x, y), (x, 1 - y), (1 - x, 1 - y)]

        def rows(k, px, py, pc):
            r = shards[k].shape[0]
            return outs[k].at[pl.ds((4 * px + 2 * py + pc) * r, r), :]

        def copy(k, s, block, to, src=None):
            return pltpu.make_async_remote_copy(
                src_ref=rows(k, *block) if src is None else src, dst_ref=rows(k, *block),
                send_sem=send_sems.at[7 * k + s], recv_sem=recv_sems.at[7 * k + s],
                device_id=to, device_id_type=MESH)

        mine = [pltpu.make_async_copy(ins[k], rows(k, *me), local_sems.at[k]) for k in range(n)]
        first = []
        for k in range(n):
            first.append(copy(k, 0, me, sibling, src=ins[k]))
            first += [copy(k, 1 + j, me, (*chip, c), src=ins[k]) for j, chip in enumerate(chips)]
        return me, sibling, chips, c, copy, mine, first

    def start(ins, outs, sems):
        *_, mine, first = copies(ins, outs, sems)
        for cp in mine + first:
            cp.start()

    def finish(ins, outs, sems):
        me, sibling, chips, c, copy, mine, first = copies(ins, outs, sems)
        passed = []
        for j, chip in enumerate(chips):
            for k in range(n):
                copy(k, 1 + j, (*chip, c), me).wait_recv()
                fwd = copy(k, 4 + j, (*chip, c), sibling)
                fwd.start()
                passed.append(fwd)
        for k in range(n):
            copy(k, 0, sibling, me).wait_recv()
            for j, chip in enumerate(chips):
                copy(k, 4 + j, (*chip, 1 - c), me).wait_recv()
        for cp in first + passed:
            cp.wait_send()
        for cp in mine:
            cp.wait()

    return Rider(
        tuple(shards),
        tuple(jax.ShapeDtypeStruct((N_DEV * s.shape[0], s.shape[1]), s.dtype) for s in shards),
        (pltpu.SemaphoreType.DMA((7 * n,)), pltpu.SemaphoreType.DMA((7 * n,)), pltpu.SemaphoreType.DMA((n,))),
        start, finish)


def _sibling_exchange(grads):
    n = len(grads)

    def copies(ins, outs, sems):
        send_sems, recv_sems = sems
        x, y, c = lax.axis_index("x"), lax.axis_index("y"), lax.axis_index("c")
        out = []
        for k in range(n):
            r = grads[k].shape[0] // N_DEV
            for j in range(N_CHIPS):
                out.append(pltpu.make_async_remote_copy(
                    src_ref=ins[k].at[pl.ds((2 * j + 1 - c) * r, r), :],
                    dst_ref=outs[k].at[pl.ds(j * r, r), :],
                    send_sem=send_sems.at[N_CHIPS * k + j], recv_sem=recv_sems.at[N_CHIPS * k + j],
                    device_id=(x, y, 1 - c), device_id_type=MESH))
        return out

    def start(ins, outs, sems):
        for cp in copies(ins, outs, sems):
            cp.start()

    def finish(ins, outs, sems):
        for cp in copies(ins, outs, sems):
            cp.wait()

    return Rider(
        tuple(grads), tuple(jax.ShapeDtypeStruct((g.shape[0] // 2, g.shape[1]), g.dtype) for g in grads),
        (pltpu.SemaphoreType.DMA((N_CHIPS * n,)), pltpu.SemaphoreType.DMA((N_CHIPS * n,))), start, finish)


def _chip_exchange(parts):
    n = len(parts)

    def copies(ins, outs, sems):
        send_sems, recv_sems, local_sems = sems
        x, y, c = lax.axis_index("x"), lax.axis_index("y"), lax.axis_index("c")
        my_chip = 2 * x + y
        chips = [(1 - x, y), (x, 1 - y), (1 - x, 1 - y)]
        local, sent = [], []
        for k in range(n):
            r = parts[k].shape[0] // N_CHIPS
            mine = pl.ds(my_chip * r, r)
            local.append(pltpu.make_async_copy(ins[k].at[mine, :], outs[k].at[mine, :], local_sems.at[k]))
            for s, (px, py) in enumerate(chips):
                sent.append(pltpu.make_async_remote_copy(
                    src_ref=ins[k].at[pl.ds((2 * px + py) * r, r), :], dst_ref=outs[k].at[mine, :],
                    send_sem=send_sems.at[3 * k + s], recv_sem=recv_sems.at[3 * k + s],
                    device_id=(px, py, c), device_id_type=MESH))
        return local, sent

    def start(ins, outs, sems):
        local, sent = copies(ins, outs, sems)
        for cp in local + sent:
            cp.start()

    def finish(ins, outs, sems):
        local, sent = copies(ins, outs, sems)
        for cp in sent + local:
            cp.wait()

    return Rider(
        tuple(parts), tuple(jax.ShapeDtypeStruct(p.shape, p.dtype) for p in parts),
        (pltpu.SemaphoreType.DMA((3 * n,)), pltpu.SemaphoreType.DMA((3 * n,)), pltpu.SemaphoreType.DMA((n,))),
        start, finish)


def _join(*riders):
    cuts_in, cuts_out, cuts_sem = [0], [0], [0]
    for r in riders:
        cuts_in.append(cuts_in[-1] + len(r.operands))
        cuts_out.append(cuts_out[-1] + len(r.out_shapes))
        cuts_sem.append(cuts_sem[-1] + len(r.sems))

    def each(which):
        def run(ins, outs, sems):
            for i, r in enumerate(riders):
                getattr(r, which)(ins[cuts_in[i]:cuts_in[i + 1]], outs[cuts_out[i]:cuts_out[i + 1]],
                                  sems[cuts_sem[i]:cuts_sem[i + 1]])
        return run

    return Rider(sum((r.operands for r in riders), ()), sum((r.out_shapes for r in riders), ()),
                 sum((r.sems for r in riders), ()), each("start"), each("finish"))


def _call(body, operands, *, name, out_shape, grid, in_specs, out_specs, scratch_shapes=(), aliases=None,
          semantics=None, vmem=None, rider=None):
    operands, out_shape, scratch_shapes = list(operands), list(out_shape), list(scratch_shapes)
    in_specs, out_specs = list(in_specs), list(out_specs)
    if rider is None:
        res = _pallas(
            body, name=name, out_shape=out_shape, grid=grid, in_specs=in_specs, out_specs=out_specs,
            scratch_shapes=scratch_shapes, input_output_aliases=aliases or {},
            compiler_params=_params(semantics, vmem))(*operands)
        return list(res), []
    n_in, n_out, n_scr = len(operands), len(out_shape), len(scratch_shapes)
    ri, ro = len(rider.operands), len(rider.out_shapes)

    def carried(*refs):
        a, b = n_in, n_in + ri
        c, d = b + n_out, b + n_out + ro
        e = d + n_scr
        ids = [pl.program_id(k) for k in range(len(grid))]
        first = ids[0] == 0
        last = ids[0] == grid[0] - 1
        for k in range(1, len(grid)):
            first = jnp.logical_and(first, ids[k] == 0)
            last = jnp.logical_and(last, ids[k] == grid[k] - 1)

        @pl.when(first)
        def _():
            rider.start(refs[a:b], refs[c:d], refs[e:])

        body(*refs[:a], *refs[b:c], *refs[d:e])

        @pl.when(last)
        def _():
            rider.finish(refs[a:b], refs[c:d], refs[e:])

    res = _pallas(
        carried, name=name, out_shape=out_shape + list(rider.out_shapes), grid=grid,
        in_specs=in_specs + [_any_spec()] * ri, out_specs=out_specs + [_any_spec()] * ro,
        scratch_shapes=scratch_shapes + list(rider.sems), input_output_aliases=aliases or {},
        compiler_params=_params(("arbitrary",) * len(grid), vmem))(*operands, *rider.operands)
    return list(res[:n_out]), list(res[n_out:])


def _exchange(rider, name):
    ri, ro = len(rider.operands), len(rider.out_shapes)

    def body(*refs):
        rider.start(refs[:ri], refs[ri:ri + ro], refs[ri + ro:])
        rider.finish(refs[:ri], refs[ri:ri + ro], refs[ri + ro:])

    return _pallas(
        body, name=name, out_shape=list(rider.out_shapes), in_specs=[_any_spec()] * ri,
        out_specs=[_any_spec()] * ro, scratch_shapes=list(rider.sems))(*rider.operands)


class Cols(NamedTuple):
    arr: jax.Array
    off: int
    width: int


def _cols(a):
    return a if isinstance(a, Cols) else Cols(a, 0, a.shape[1])


def _matmul(a, b, *, mode, name, out_dtype=F32, tm=1024, tn=1024, tk=2048, bias=None, add=None, out_cols=None,
            into=None, rider=None):
    a, b = _cols(a), _cols(b)
    if mode == "nn":
        (m, k), (k2, n) = (a.arr.shape[0], a.width), (b.arr.shape[0], b.width)
    elif mode == "nt":
        (m, k), (n, k2) = (a.arr.shape[0], a.width), (b.arr.shape[0], b.width)
    else:
        (k, m), (k2, n) = (a.arr.shape[0], a.width), (b.arr.shape[0], b.width)
    assert k == k2, (a.arr.shape, b.arr.shape, mode)
    tm, tn, tk = _tile(m, tm), _tile(n, tn), _tile(k, tk)
    nk = k // tk
    dims = {"nn": NN, "nt": NT, "tn": TN}[mode]
    if mode == "tn":
        assert a.off % tm == 0
        a_spec = pl.BlockSpec((tk, tm), lambda i, j, kk, o=a.off // tm: (kk, i + o))
    else:
        assert a.off % tk == 0
        a_spec = pl.BlockSpec((tm, tk), lambda i, j, kk, o=a.off // tk: (i, kk + o))
    if mode == "nt":
        assert b.off % tk == 0
        b_spec = pl.BlockSpec((tn, tk), lambda i, j, kk, o=b.off // tk: (j, kk + o))
    else:
        assert b.off % tn == 0
        b_spec = pl.BlockSpec((tk, tn), lambda i, j, kk, o=b.off // tn: (kk, j + o))
    in_specs, operands = [a_spec, b_spec], [a.arr, b.arr]
    assert bias is None or add is None
    if bias is not None:
        in_specs.append(pl.BlockSpec((1, tn), lambda i, j, kk: (0, j)))
        operands.append(bias)
    if add is not None:
        assert add.shape == (m, n)
        in_specs.append(pl.BlockSpec((tm, tn), lambda i, j, kk: (i, j)))
        operands.append(add)
    total_w, o_off = out_cols if out_cols is not None else (n, 0)
    assert o_off % tn == 0
    aliases = {}
    if into is not None:
        assert into.shape == (m, total_w) and into.dtype == out_dtype
        in_specs.append(_any_spec())
        operands.append(into)
        aliases = {len(operands) - 1: 0}
    n_in = len(operands)

    def body(*refs):
        a_ref, b_ref = refs[0], refs[1]
        bias_ref = refs[2] if bias is not None or add is not None else None
        o_ref = refs[n_in]
        acc_ref = refs[-1] if nk > 1 else None
        part = _dot(a_ref[...].astype(BF16), b_ref[...].astype(BF16), dims)

        def finish(acc):
            if bias_ref is not None:
                acc = acc + bias_ref[...]
            o_ref[...] = acc.astype(out_dtype)

        if nk == 1:
            finish(part)
        else:
            kk = pl.program_id(2)

            @pl.when(kk == 0)
            def _():
                acc_ref[...] = part

            @pl.when(kk > 0)
            def _():
                acc_ref[...] += part

            @pl.when(kk == nk - 1)
            def _():
                finish(acc_ref[...])

    vmem = 2 * (_nbytes((tm, tk), a.arr.dtype) + _nbytes((tk, tn), b.arr.dtype) + _nbytes((tm, tn), out_dtype))
    vmem += 3 * _nbytes((tm, tn), F32)
    (out,), landed = _call(
        body, operands, name=name, out_shape=[jax.ShapeDtypeStruct((m, total_w), out_dtype)],
        grid=(m // tm, n // tn, nk), in_specs=in_specs,
        out_specs=[pl.BlockSpec((tm, tn), lambda i, j, kk, o=o_off // tn: (i, j + o))],
        scratch_shapes=[pltpu.VMEM((tm, tn), F32)] if nk > 1 else [], aliases=aliases,
        semantics=("parallel", "parallel", "arbitrary"), vmem=vmem, rider=rider)
    return out if rider is None else (out, landed)


def _ew(fn, *, name, rows, width, tiles, vecs=(), outs, accs=0, tl=1024, cw=512, into=None, with_col=False):
    tl, cw = _tile(rows, tl, SUBLANES), _tile(width, cw)
    ncol = width // cw
    nt_, nv = len(tiles), len(vecs)
    into = list(into) if into is not None else [None] * len(outs)
    aliased = [t for t in into if t is not None]

    def off(o):
        assert o % cw == 0, (name, o, cw)
        return o // cw

    in_specs, vmem = [], 0
    for t in tiles:
        arr, o = t[0], off(t[1])
        wrap = t[2] // cw if len(t) > 2 else ncol
        in_specs.append(pl.BlockSpec((tl, cw), lambda j, i, o=o, wrap=wrap: (i, o + j % wrap)))
        vmem += _nbytes((tl, cw), arr.dtype)
    in_specs += [pl.BlockSpec((1, cw), lambda j, i, o=off(o): (0, j + o)) for _, o in vecs]
    in_specs += [_any_spec() for _ in aliased]
    out_shape, out_specs, aliases = [], [], {}
    n_in = nt_ + nv
    for idx, ((dt, tw, o), tgt) in enumerate(zip(outs, into)):
        out_shape.append(jax.ShapeDtypeStruct((rows, tw), dt))
        out_specs.append(pl.BlockSpec((tl, cw), lambda j, i, o=off(o): (i, j + o)))
        vmem += _nbytes((tl, cw), dt)
        if tgt is not None:
            assert tgt.shape == (rows, tw) and tgt.dtype == dt, (name, tgt.shape, tgt.dtype)
            aliases[n_in + len(aliases)] = idx
    for _ in range(accs):
        out_shape.append(jax.ShapeDtypeStruct((1, width), F32))
        out_specs.append(pl.BlockSpec((1, cw), lambda j, i: (0, j)))
    n_out = len(outs)

    def body(*refs):
        vals = [r[...].astype(F32) for r in refs[:n_in]]
        out_refs = refs[n_in + len(aliased):]
        res = fn(pl.program_id(0), *vals) if with_col else fn(*vals)
        res = res if isinstance(res, (tuple, list)) else (res,)
        assert len(res) == n_out + accs, (name, len(res))
        for r, v in zip(out_refs[:n_out], res[:n_out]):
            r[...] = v.astype(r.dtype)
        first = pl.program_id(1) == 0
        for r, v in zip(out_refs[n_out:], res[n_out:]):
            s = jnp.sum(v, axis=0, keepdims=True)

            @pl.when(first)
            def _(r=r, s=s):
                r[...] = s

            @pl.when(jnp.logical_not(first))
            def _(r=r, s=s):
                r[...] += s

    return _pallas(
        body, name=name, out_shape=out_shape, grid=(ncol, rows // tl),
        in_specs=in_specs, out_specs=out_specs, input_output_aliases=aliases,
        compiler_params=_params(("parallel", "arbitrary"), 3 * vmem),
    )(*[t[0] for t in tiles], *[v for v, _ in vecs], *aliased)


def _rmsnorm_fwd(x, w_row, name, rider=None):
    rows, d = x.shape
    tl = _tile(rows, 512, SUBLANES)

    def body(x_ref, w_ref, h_ref):
        xv = x_ref[...]
        rstd = lax.rsqrt(jnp.mean(xv * xv, axis=-1, keepdims=True) + NORM_EPS)
        h_ref[...] = (xv * rstd * w_ref[...]).astype(BF16)

    (h,), landed = _call(
        body, [x, w_row], name=name, out_shape=[jax.ShapeDtypeStruct((rows, d), BF16)], grid=(rows // tl,),
        in_specs=[pl.BlockSpec((tl, d), lambda i: (i, 0)), pl.BlockSpec((1, d), lambda i: (0, 0))],
        out_specs=[pl.BlockSpec((tl, d), lambda i: (i, 0))], semantics=("parallel",), rider=rider)
    return h if rider is None else (h, landed)


def _rmsnorm_bwd(x, w_row, dh, dout, name, rider=None):
    rows, d = x.shape
    tl = _tile(rows, 256, SUBLANES)

    def body(x_ref, w_ref, dh_ref, dout_ref, gx_ref, gw_ref):
        xv = x_ref[...]
        rstd = lax.rsqrt(jnp.mean(xv * xv, axis=-1, keepdims=True) + NORM_EPS)
        xn = xv * rstd
        dhv = dh_ref[...]
        dxn = dhv * w_ref[...]
        dx = rstd * (dxn - xn * jnp.mean(dxn * xn, axis=-1, keepdims=True))
        gx_ref[...] = dout_ref[...] + dx
        gw = jnp.sum(dhv * xn, axis=0, keepdims=True)

        @pl.when(pl.program_id(0) == 0)
        def _():
            gw_ref[...] = gw

        @pl.when(pl.program_id(0) > 0)
        def _():
            gw_ref[...] += gw

    tile = pl.BlockSpec((tl, d), lambda i: (i, 0))
    row = pl.BlockSpec((1, d), lambda i: (0, 0))
    res, landed = _call(
        body, [x, w_row, dh, dout], name=name,
        out_shape=[jax.ShapeDtypeStruct((rows, d), F32), jax.ShapeDtypeStruct((1, d), F32)],
        grid=(rows // tl,), in_specs=[tile, row, tile, tile], out_specs=[tile, row],
        semantics=("arbitrary",), rider=rider)
    return res if rider is None else (res, landed)


def _head_mean(x, gmat):
    hi = x.astype(BF16)
    lo = (x - hi.astype(F32)).astype(BF16)
    out = []
    for s in range(x.shape[1] // MXU_DIM):
        sl = slice(s * MXU_DIM, (s + 1) * MXU_DIM)
        out.append(_dot(hi[:, sl], gmat, NN) + _dot(lo[:, sl], gmat, NN))
    return out[0] if len(out) == 1 else jnp.concatenate(out, axis=1)


def _head_mean_matrix():
    blk = jnp.arange(MXU_DIM) // HEAD_DIM
    return jnp.where(blk[:, None] == blk[None, :], 1.0 / HEAD_DIM, 0.0).astype(BF16)


def _spread_head(x, g, width):
    col = x[:, (g // 2) * LANES:(g // 2 + 1) * LANES]
    other = pltpu.roll(col, HEAD_DIM, axis=1)
    low = lax.broadcasted_iota(jnp.int32, col.shape, 1) < HEAD_DIM
    both = jnp.where(low, col, other) if g % 2 == 0 else jnp.where(low, other, col)
    return both if width == LANES else jnp.concatenate([both] * (width // LANES), axis=1)


def _head_diagonal(t, per_kv):
    head = lax.broadcasted_iota(jnp.int32, t.shape, 1) // HEAD_DIM
    zero = jnp.zeros_like(t)
    return jnp.concatenate([jnp.where(head == r, t, zero) for r in range(per_kv)], axis=0)


def _fold_heads(x, per_kv):
    rows = x.shape[0] // per_kv
    head = lax.broadcasted_iota(jnp.int32, (rows, x.shape[1]), 1) // HEAD_DIM
    acc = jnp.where(head == 0, x[0:rows], 0.0)
    for r in range(1, per_kv):
        acc = acc + jnp.where(head == r, x[r * rows:(r + 1) * rows], 0.0)
    while acc.shape[1] > LANES:
        half = acc.shape[1] // 2
        acc = acc[:, :half] + acc[:, half:]
    return acc + pltpu.roll(acc, HEAD_DIM, axis=1)


def _join_heads(parts):
    low = lax.broadcasted_iota(jnp.int32, parts[0].shape, 1) < HEAD_DIM
    cols = [jnp.where(low, parts[2 * j], parts[2 * j + 1]) for j in range(len(parts) // 2)]
    return cols[0] if len(cols) == 1 else jnp.concatenate(cols, axis=1)


def _attn_specs(attn_w, kv_w, block=lambda s: s):
    half = attn_w // 2
    kcol, vcol = attn_w // kv_w, attn_w // kv_w + 1
    gcol = (attn_w + 2 * kv_w) // half
    prev = lambda s: jnp.maximum(block(s) - 1, 0)
    return [
        pl.BlockSpec((BLOCK, attn_w), lambda s: (block(s), 0)),
        pl.BlockSpec((BLOCK, kv_w), lambda s: (prev(s), kcol)),
        pl.BlockSpec((BLOCK, kv_w), lambda s: (block(s), kcol)),
        pl.BlockSpec((BLOCK, kv_w), lambda s: (prev(s), vcol)),
        pl.BlockSpec((BLOCK, kv_w), lambda s: (block(s), vcol)),
        pl.BlockSpec((BLOCK, half), lambda s: (block(s), gcol)),
        pl.BlockSpec((BLOCK, half), lambda s: (block(s), gcol + 1)),
    ]


def _band_mask(i):
    q_loc = lax.broadcasted_iota(jnp.int32, (BLOCK, 2 * BLOCK), 0) + BLOCK
    k_loc = lax.broadcasted_iota(jnp.int32, (BLOCK, 2 * BLOCK), 1)
    diff = q_loc - k_loc
    first_key = jnp.where(i == 0, BLOCK, 0)
    return (diff >= 0) & (diff < BLOCK) & (k_loc >= first_key)


def _softmax_with_sink(s, sink):
    m = jnp.maximum(jnp.max(s, axis=-1, keepdims=True), sink)
    p = jnp.exp(s - m)
    e_sink = jnp.exp(sink - m)
    den = jnp.sum(p, axis=-1, keepdims=True) + e_sink
    inv = 1.0 / den
    return p * inv, e_sink * inv


def _attn_block(i, q, kk, vv, qw, kw, gmat, sink_ref, per_kv):
    scale = 1.0 / math.sqrt(HEAD_DIM)
    keys = 2 * BLOCK
    valid = _band_mask(i)
    q_rstd = lax.rsqrt(_head_mean(q * q, gmat) + NORM_EPS)
    qn = q * q_rstd
    qh = (qn * qw).astype(BF16)
    k_rstd = lax.rsqrt(_head_mean(kk * kk, gmat) + NORM_EPS)
    kn = kk * k_rstd
    kh = kn * kw
    gw = per_kv * HEAD_DIM
    groups = []
    for g in range(N_KV_HEADS):
        kd = _head_diagonal(_spread_head(kh, g, gw).astype(BF16), per_kv)
        vd = _head_diagonal(_spread_head(vv, g, gw).astype(BF16), per_kv)
        qg = qh[:, g * gw:(g + 1) * gw]
        s_all = _dot(qg, kd, NT) * scale
        ps, p_sinks = [], []
        for r in range(per_kv):
            s = jnp.where(valid, s_all[:, r * keys:(r + 1) * keys], -1e30)
            p, p_sink = _softmax_with_sink(s, sink_ref[g * per_kv + r])
            ps.append(p)
            p_sinks.append(p_sink)
        pb = jnp.concatenate(ps, axis=1).astype(BF16)
        groups.append((kd, vd, qg, ps, p_sinks, pb, _dot(pb, vd, NN)))
    return qn, q_rstd, kn, k_rstd, groups


def _attention_fwd(proj, qw_row, kw_row, gmat, sinks, *, attn_w, kv_w, name):
    rows = proj.shape[0]
    per_kv = attn_w // HEAD_DIM // N_KV_HEADS

    def body(q_ref, kp_ref, kc_ref, vp_ref, vc_ref, glo_ref, ghi_ref, qw_ref, kw_ref, gm_ref, sink_ref, o_ref):
        kk = jnp.concatenate([kp_ref[...], kc_ref[...]], axis=0).astype(F32)
        vv = jnp.concatenate([vp_ref[...], vc_ref[...]], axis=0).astype(F32)
        gate = jnp.concatenate([glo_ref[...], ghi_ref[...]], axis=1).astype(F32)
        *_, groups = _attn_block(pl.program_id(0), q_ref[...].astype(F32), kk, vv, qw_ref[...], kw_ref[...], gm_ref[...],
                                 sink_ref, per_kv)
        attn = jnp.concatenate([grp[-1] for grp in groups], axis=1)
        o_ref[...] = (attn * _silu(gate)).astype(BF16)

    const = lambda a: pl.BlockSpec(a.shape, lambda i: (0, 0))
    return _pallas(
        body, name=name, out_shape=jax.ShapeDtypeStruct((rows, attn_w), BF16), grid=(rows // BLOCK,),
        in_specs=_attn_specs(attn_w, kv_w) + [const(qw_row), const(kw_row), const(gmat),
                                              pl.BlockSpec(memory_space=pltpu.SMEM)],
        out_specs=pl.BlockSpec((BLOCK, attn_w), lambda i: (i, 0)),
        compiler_params=_params(("parallel",), 40 * 1024 * 1024),
    )(proj, proj, proj, proj, proj, proj, proj, qw_row, kw_row, gmat, sinks)


def _attention_bwd(proj, d_ag, dproj, qw_row, kw_row, gmat, sinks, *, attn_w, kv_w, name, rider=None):
    rows = proj.shape[0]
    nb = rows // BLOCK
    per_kv = attn_w // HEAD_DIM // N_KV_HEADS
    gw = per_kv * HEAD_DIM
    keys = 2 * BLOCK
    scale = 1.0 / math.sqrt(HEAD_DIM)
    w_out = 2 * attn_w + 2 * kv_w
    rev = lambda s: nb - 1 - s

    def body(q_ref, kp_ref, kc_ref, vp_ref, vc_ref, glo_ref, ghi_ref, dag_ref, qw_ref, kw_ref, gm_ref, sink_ref, _,
             dp_ref, gqw_ref, gkw_ref, gs_ref, carry_ref):
        step = pl.program_id(0)
        i = rev(step)
        kk = jnp.concatenate([kp_ref[...], kc_ref[...]], axis=0).astype(F32)
        vv = jnp.concatenate([vp_ref[...], vc_ref[...]], axis=0).astype(F32)
        gate = jnp.concatenate([glo_ref[...], ghi_ref[...]], axis=1).astype(F32)
        d_ag_v = dag_ref[...].astype(F32)
        qw, kw, gmat_v = qw_ref[...], kw_ref[...], gm_ref[...]
        qn, q_rstd, kn, k_rstd, groups = _attn_block(i, q_ref[...].astype(F32), kk, vv, qw, kw, gmat_v, sink_ref,
                                                     per_kv)
        lane = lax.broadcasted_iota(jnp.int32, (SUBLANES, LANES), 1)
        sub = lax.broadcasted_iota(jnp.int32, (SUBLANES, LANES), 0)
        gsink = jnp.zeros((SUBLANES, LANES), F32)
        dq_groups, dgate_groups, dk_heads, dv_heads = [], [], [], []
        for g, (kd, vd, qg, ps, p_sinks, pb, o) in enumerate(groups):
            cs = slice(g * gw, (g + 1) * gw)
            gate_g, d_ag_g = gate[:, cs], d_ag_v[:, cs]
            dgate_groups.append(d_ag_g * o * _dsilu(gate_g))
            do = (d_ag_g * _silu(gate_g)).astype(BF16)
            dp_all = _dot(do, vd, NT)
            dss = []
            for r in range(per_kv):
                p, dp = ps[r], dp_all[:, r * keys:(r + 1) * keys]
                delta = jnp.sum(p * dp, axis=-1, keepdims=True)
                dss.append(p * (dp - delta) * scale)
                gs_h = jnp.sum(-p_sinks[r] * delta, axis=0, keepdims=True)
                gsink = gsink + jnp.where((lane == g * per_kv + r) & (sub == 0), gs_h, 0.0)
            ds = jnp.concatenate(dss, axis=1).astype(BF16)
            dq_groups.append(_dot(ds, kd, NN))
            dk_heads.append(_fold_heads(_dot(ds, qg, TN), per_kv))
            dv_heads.append(_fold_heads(_dot(pb, do, TN), per_kv))
        dqh = jnp.concatenate(dq_groups, axis=1)
        gqw = jnp.sum(dqh * qn, axis=0, keepdims=True)
        dqn = dqh * qw
        dq = q_rstd * (dqn - qn * _head_mean(dqn * qn, gmat_v))
        dkh = _join_heads(dk_heads)
        gkw = jnp.sum(dkh * kn, axis=0, keepdims=True)
        dkn = dkh * kw
        dk = k_rstd * (dkn - kn * _head_mean(dkn * kn, gmat_v))
        dkv = jnp.concatenate([dk, _join_heads(dv_heads)], axis=1)

        @pl.when(step == 0)
        def _():
            carry_ref[...] = jnp.zeros_like(carry_ref)
            gqw_ref[...] = gqw
            gkw_ref[...] = gkw
            gs_ref[...] = gsink

        @pl.when(step > 0)
        def _():
            gqw_ref[...] += gqw
            gkw_ref[...] += gkw
            gs_ref[...] += gsink

        dp_ref[:, 0:attn_w] = dq.astype(BF16)
        dp_ref[:, attn_w:attn_w + 2 * kv_w] = (dkv[BLOCK:2 * BLOCK, :] + carry_ref[...]).astype(BF16)
        dp_ref[:, attn_w + 2 * kv_w:w_out] = jnp.concatenate(dgate_groups, axis=1).astype(BF16)
        carry_ref[...] = dkv[0:BLOCK, :]

    const = lambda a: pl.BlockSpec(a.shape, lambda s: (0, 0))
    res, landed = _call(
        body, [proj, proj, proj, proj, proj, proj, proj, d_ag, qw_row, kw_row, gmat, sinks, dproj], name=name,
        out_shape=[jax.ShapeDtypeStruct(dproj.shape, BF16),
                   jax.ShapeDtypeStruct(qw_row.shape, F32), jax.ShapeDtypeStruct(kw_row.shape, F32),
                   jax.ShapeDtypeStruct((SUBLANES, LANES), F32)],
        grid=(nb,),
        in_specs=_attn_specs(attn_w, kv_w, rev) + [pl.BlockSpec((BLOCK, attn_w), lambda s: (rev(s), 0)),
                                                   const(qw_row), const(kw_row), const(gmat),
                                                   pl.BlockSpec(memory_space=pltpu.SMEM), _any_spec()],
        out_specs=[pl.BlockSpec((BLOCK, w_out), lambda s: (rev(s), 0)),
                   const(qw_row), const(kw_row), pl.BlockSpec((SUBLANES, LANES), lambda s: (0, 0))],
        scratch_shapes=[pltpu.VMEM((BLOCK, 2 * kv_w), F32)],
        aliases={12: 0}, semantics=("arbitrary",), vmem=48 * 1024 * 1024, rider=rider)
    return res if rider is None else (res, landed)


def _cmul(ar, ai, br, bi):
    return ar * br - ai * bi, ar * bi + ai * br


def _ssm_prep(a_re, a_im, log_dt_col, steps, name):
    def body(are_ref, aim_ref, ldt_ref, abr_ref, abi_ref, cfr_ref, cfi_ref, apr_ref, api_ref, pwr_ref, pwi_ref):
        are, aim = are_ref[...], aim_ref[...]
        dt = jnp.exp(ldt_ref[...])
        mag = jnp.exp(dt * are)
        abr = mag * jnp.cos(dt * aim)
        abi = mag * jnp.sin(dt * aim)
        num_re, num_im = abr - 1.0, abi
        den = are * are + aim * aim
        abr_ref[...] = abr
        abi_ref[...] = abi
        cfr_ref[...] = (num_re * are + num_im * aim) / den
        cfi_ref[...] = (num_im * are - num_re * aim) / den
        pr, pi = jnp.ones_like(abr), jnp.zeros_like(abr)
        for k in range(steps):
            pwr_ref[k] = pr
            pwi_ref[k] = pi
            pr, pi = _cmul(pr, pi, abr, abi)
        apr_ref[...] = pr
        api_ref[...] = pi

    shp = jax.ShapeDtypeStruct(a_re.shape, F32)
    pows = jax.ShapeDtypeStruct((steps,) + a_re.shape, F32)
    return _pallas(body, name=name, out_shape=[shp] * 6 + [pows] * 2)(a_re, a_im, log_dt_col)


def _ssm_param_bwd(a_re, a_im, log_dt_col, d_ab_re, d_ab_im, b_re, b_im, dbt_re, dbt_im, name):
    def body(are_ref, aim_ref, ldt_ref, gabr_ref, gabi_ref, br_ref, bi_ref, tr_ref, ti_ref,
             dar_ref, dai_ref, dldt_ref, dbr_ref, dbi_ref):
        are, aim = are_ref[...], aim_ref[...]
        dt = jnp.exp(ldt_ref[...])
        mag = jnp.exp(dt * are)
        abr = mag * jnp.cos(dt * aim)
        abi = mag * jnp.sin(dt * aim)
        den = are * are + aim * aim
        cfr = ((abr - 1.0) * are + abi * aim) / den
        cfi = (abi * are - (abr - 1.0) * aim) / den
        gabr, gabi = jnp.sum(gabr_ref[...], axis=0), jnp.sum(gabi_ref[...], axis=0)
        t_re, t_im = tr_ref[...], ti_ref[...]
        g_r, g_i = _cmul(br_ref[...], -bi_ref[...], t_re, t_im)
        gcfr, gcfi = jnp.sum(g_r, axis=1), jnp.sum(g_i, axis=1)
        dbr, dbi = _cmul(cfr[:, None, :], -cfi[:, None, :], t_re, t_im)
        dbr_ref[...] = dbr
        dbi_ref[...] = dbi
        inv_r, inv_i = are / den, -aim / den
        t_r, t_i = _cmul(inv_r, -inv_i, gcfr, gcfi)
        gabr, gabi = gabr + t_r, gabi + t_i
        q_r, q_i = _cmul(cfr, cfi, inv_r, inv_i)
        da_r, da_i = _cmul(-q_r, q_i, gcfr, gcfi)
        gz_r, gz_i = _cmul(abr, -abi, gabr, gabi)
        dar_ref[...] = da_r + dt * gz_r
        dai_ref[...] = da_i + dt * gz_i
        dldt_ref[...] = dt * jnp.sum(are * gz_r + aim * gz_i, axis=-1, keepdims=True)

    shp = jax.ShapeDtypeStruct(a_re.shape, F32)
    bshp = jax.ShapeDtypeStruct(b_re.shape, F32)
    return _pallas(body, name=name,
                   out_shape=[shp, shp, jax.ShapeDtypeStruct(log_dt_col.shape, F32), bshp, bshp])(
        a_re, a_im, log_dt_col, d_ab_re, d_ab_im, b_re, b_im, dbt_re, dbt_im)


SCAN_LANES = 512
SSM_CHUNK = 256
W_IN_PARTS = 2


def _scan_segments(xr_ref, xi_ref, a_re, a_im, ap_re, ap_im, pw_re, pw_im, carry_re, carry_im, cm_re, cm_im, steps,
                   reverse, base):
    n = xr_ref.shape[1]
    seg_order = range(SUBLANES - 1, -1, -1) if reverse else range(SUBLANES)
    sign = -1.0 if reverse else 1.0
    for c0 in range(0, n, SCAN_LANES):
        ls = slice(c0, c0 + SCAN_LANES)
        gs = slice(base + c0, base + c0 + SCAN_LANES)
        ar = jnp.broadcast_to(a_re[:, gs], (SUBLANES, SCAN_LANES))
        ai = jnp.broadcast_to(a_im[:, gs], (SUBLANES, SCAN_LANES))
        end_r = jnp.zeros((SUBLANES, SCAN_LANES), F32)
        end_i = jnp.zeros((SUBLANES, SCAN_LANES), F32)
        for j in range(steps):
            k = j if reverse else steps - 1 - j
            rws = slice(j * SUBLANES, (j + 1) * SUBLANES)
            tr, ti = _cmul(pw_re[k:k + 1, gs], sign * pw_im[k:k + 1, gs], xr_ref[rws, ls], xi_ref[rws, ls])
            end_r, end_i = end_r + tr, end_i + ti
        cr, ci = carry_re[:, gs], carry_im[:, gs]
        apr, api = ap_re[:, gs], ap_im[:, gs]
        for r in seg_order:
            cm_re[r:r + 1, gs] = cr
            cm_im[r:r + 1, gs] = ci
            tr, ti = _cmul(apr, api, cr, ci)
            cr, ci = end_r[r:r + 1, :] + tr, end_i[r:r + 1, :] + ti
        carry_re[:, gs] = cr
        carry_im[:, gs] = ci

        def run(t, s, ar=ar, ai=ai, ls=ls):
            j = steps - 1 - t if reverse else t
            r0 = pl.multiple_of(j * SUBLANES, SUBLANES)
            sr, si = _cmul(ar, ai, s[0], s[1])
            sr = sr + xr_ref[pl.ds(r0, SUBLANES), ls]
            si = si + xi_ref[pl.ds(r0, SUBLANES), ls]
            xr_ref[pl.ds(r0, SUBLANES), ls] = sr
            xi_ref[pl.ds(r0, SUBLANES), ls] = si
            return sr, si

        lax.fori_loop(0, steps, run, (cm_re[:, gs], cm_im[:, gs]))


SB_GROUPS = MXU_DIM // GROUP
SB_STATE = SB_GROUPS * STATE


def _ssm_rows(m):
    flat = m.reshape(-1, STATE).astype(F32)
    return jnp.concatenate([flat, flat], axis=1)


def _from_ssm_rows(rows):
    return rows[:, :STATE].reshape(-1, GROUP, STATE)


def _own_group(shape):
    row_g = lax.broadcasted_iota(jnp.int32, shape, 0) // GROUP
    col_g = lax.broadcasted_iota(jnp.int32, shape, 1) // STATE
    return row_g == col_g


def _block_diagonal(rows):
    tiled = jnp.concatenate([rows] * (SB_STATE // LANES), axis=1)
    return jnp.where(_own_group(tiled.shape), tiled, 0.0).astype(BF16)


def _block_rows(acc):
    x = jnp.where(_own_group(acc.shape), acc, 0.0)
    while x.shape[1] > LANES:
        half = x.shape[1] // 2
        x = x[:, :half] + x[:, half:]
    return x + pltpu.roll(x, STATE, axis=1)


def _rows_to_segments(dst, srcs, steps, stage):
    for ref, off in srcs:
        for k in range(ref.shape[1] // LANES):
            stage[off // LANES + k] = ref[:, k * LANES:(k + 1) * LANES].astype(F32)
    for k in range(dst.shape[1] // LANES):
        for j in range(steps):
            dst[j * SUBLANES:(j + 1) * SUBLANES, k * LANES:(k + 1) * LANES] = (
                stage[k, pl.ds(j, SUBLANES, stride=steps), :])


def _segments_to_rows(dst, src, steps, stage):
    for k in range(src.shape[1] // LANES):
        for j in range(steps):
            stage[k, pl.ds(j, SUBLANES, stride=steps), :] = (
                src[j * SUBLANES:(j + 1) * SUBLANES, k * LANES:(k + 1) * LANES])
    for k in range(src.shape[1] // LANES):
        dst[:, k * LANES:(k + 1) * LANES] = stage[k]


def _u_specs(w, o_u, chunk, index):
    half = w // 2
    assert o_u % half == 0
    return [pl.BlockSpec((chunk, half), lambda c, k=k: (index(c), o_u // half + k)) for k in range(2)]


def _ssm_fwd(proj, o_u, bc_rows, rows_p, d_row, *, chunk, name, rider=None):
    rows = proj.shape[0]
    w = d_row.shape[1]
    nc = rows // chunk
    steps = chunk // SUBLANES
    nsb = w // MXU_DIM
    n_state = nsb * SB_STATE

    def body(ulo_ref, uhi_ref, b2r_ref, b2i_ref, c2r_ref, c2i_ref, abr_ref, abi_ref, cfr_ref, cfi_ref, apr_ref,
             api_ref, pwr_ref, pwi_ref, d_ref, y_ref, str_ref, sti_ref, yg_ref, bre_ref, bim_ref, cre_ref, cim_ref,
             useg, yseg, stage, sr, si,
             carry_r, carry_i, cm_r, cm_i):
        @pl.when(pl.program_id(0) == 0)
        def _():
            for src, dst in ((b2r_ref, bre_ref), (b2i_ref, bim_ref), (c2r_ref, cre_ref), (c2i_ref, cim_ref)):
                for sb in range(nsb):
                    dst[sb] = _block_diagonal(src[sb * MXU_DIM:(sb + 1) * MXU_DIM, :])
            carry_r[...] = jnp.zeros_like(carry_r)
            carry_i[...] = jnp.zeros_like(carry_i)

        str_ref[0] = carry_r[...]
        sti_ref[0] = carry_i[...]
        _rows_to_segments(useg, [(ulo_ref, 0), (uhi_ref, w // 2)], steps, stage)
        for sb in range(nsb):
            us = slice(sb * MXU_DIM, (sb + 1) * MXU_DIM)
            ss = slice(sb * SB_STATE, (sb + 1) * SB_STATE)
            ub = useg[:, us].astype(BF16)
            bur = _dot(ub, bre_ref[sb], NN)
            bui = _dot(ub, bim_ref[sb], NN)
            xr, xi = _cmul(cfr_ref[:, ss], cfi_ref[:, ss], bur, bui)
            sr[...] = xr
            si[...] = xi
            _scan_segments(sr, si, abr_ref[...], abi_ref[...], apr_ref[...], api_ref[...], pwr_ref, pwi_ref,
                           carry_r, carry_i, cm_r, cm_i, steps, False, sb * SB_STATE)
            y = _dot(sr[...].astype(BF16), cre_ref[sb], NT) - _dot(si[...].astype(BF16), cim_ref[sb], NT)
            yseg[:, us] = y + d_ref[:, us] * useg[:, us]
        _segments_to_rows(y_ref, yseg, steps, stage)
        yg_ref[...] = _gelu(y_ref[...]).astype(BF16)

    const = lambda a: pl.BlockSpec(a.shape, lambda c: (0,) * a.ndim)
    row_n = pl.BlockSpec((1, n_state), lambda c: (0, 0))
    st = pl.BlockSpec((1, 1, n_state), lambda c: (c, 0, 0))
    held = [pltpu.VMEM((nsb, MXU_DIM, SB_STATE), BF16)] * 4
    vmem = (4 * _nbytes((nsb, MXU_DIM, SB_STATE), BF16) + 4 * _nbytes((chunk, SB_STATE), F32)
            + 12 * _nbytes((chunk, w), F32) + 8 * _nbytes(bc_rows[0].shape, F32))
    res, landed = _call(
        body, [proj, proj, *bc_rows, *rows_p, d_row], name=name,
        out_shape=[jax.ShapeDtypeStruct((rows, w), F32), jax.ShapeDtypeStruct((nc, 1, n_state), F32),
                   jax.ShapeDtypeStruct((nc, 1, n_state), F32), jax.ShapeDtypeStruct((rows, w), BF16)],
        grid=(nc,),
        in_specs=_u_specs(w, o_u, chunk, lambda c: c) + [const(b) for b in bc_rows]
        + [row_n] * 6 + [pl.BlockSpec((steps, n_state), lambda c: (0, 0))] * 2 + [pl.BlockSpec((1, w), lambda c: (0, 0))],
        out_specs=[pl.BlockSpec((chunk, w), lambda c: (c, 0)), st, st, pl.BlockSpec((chunk, w), lambda c: (c, 0))],
        scratch_shapes=held + [pltpu.VMEM((chunk, w), F32), pltpu.VMEM((chunk, w), F32),
                               pltpu.VMEM((w // LANES, chunk, LANES), F32),
                               pltpu.VMEM((chunk, SB_STATE), F32), pltpu.VMEM((chunk, SB_STATE), F32),
                               pltpu.VMEM((1, n_state), F32), pltpu.VMEM((1, n_state), F32),
                               pltpu.VMEM((SUBLANES, n_state), F32), pltpu.VMEM((SUBLANES, n_state), F32)],
        semantics=("arbitrary",), vmem=vmem, rider=rider)
    return res if rider is None else (res, landed)


def _ssm_bwd(proj, o_u, y, dyg, st_re, st_im, bc_rows, rows_p, d_row, *, chunk, name, rider=None):
    rows = proj.shape[0]
    w = d_row.shape[1]
    nc = rows // chunk
    steps = chunk // SUBLANES
    nsb = w // MXU_DIM
    n_state = nsb * SB_STATE

    def body(ulo_ref, uhi_ref, y_ref, dyg_ref, str_ref, sti_ref, b2r_ref, b2i_ref, c2r_ref, c2i_ref, t2r_ref,
             t2i_ref, abr_ref, abi_ref, cfr_ref, cfi_ref, apr_ref, api_ref, pwr_ref, pwi_ref, d_ref,
             du_ref, gb2r_ref, gb2i_ref, gc2r_ref, gc2i_ref, gabr_ref, gabi_ref, dd_ref,
             bre_ref, bim_ref, cre_ref, cim_ref, btr_ref, bti_ref, dbre_ref, dbim_ref, dcre_ref, dcim_ref,
             useg, dyseg, dynat, stage, sr, si, lr, li, carry_r, carry_i, lam_r, lam_i, cm_r, cm_i, cl_r, cl_i):
        first = pl.program_id(0) == 0

        @pl.when(first)
        def _():
            for src, dst in ((b2r_ref, bre_ref), (b2i_ref, bim_ref), (c2r_ref, cre_ref), (c2i_ref, cim_ref),
                             (t2r_ref, btr_ref), (t2i_ref, bti_ref)):
                for sb in range(nsb):
                    dst[sb] = _block_diagonal(src[sb * MXU_DIM:(sb + 1) * MXU_DIM, :])
            lam_r[...] = jnp.zeros_like(lam_r)
            lam_i[...] = jnp.zeros_like(lam_i)
            for ref in (dbre_ref, dbim_ref, dcre_ref, dcim_ref, gabr_ref, gabi_ref, dd_ref):
                ref[...] = jnp.zeros_like(ref)

        dynat[...] = dyg_ref[...].astype(F32) * _dgelu(y_ref[...])
        half = w // 2
        dd_ref[:, :half] += jnp.sum(dynat[:, :half] * ulo_ref[...].astype(F32), axis=0, keepdims=True)
        dd_ref[:, half:] += jnp.sum(dynat[:, half:] * uhi_ref[...].astype(F32), axis=0, keepdims=True)
        _rows_to_segments(useg, [(ulo_ref, 0), (uhi_ref, half)], steps, stage)
        _rows_to_segments(dyseg, [(dynat, 0)], steps, stage)
        dy = dyseg[...]
        dyb = dy.astype(BF16)
        ub = useg[...].astype(BF16)
        carry_r[...] = str_ref[0]
        carry_i[...] = sti_ref[0]
        abr, abi = abr_ref[...], abi_ref[...]
        apr, api = apr_ref[...], api_ref[...]
        for sb in range(nsb):
            us = slice(sb * MXU_DIM, (sb + 1) * MXU_DIM)
            ss = slice(sb * SB_STATE, (sb + 1) * SB_STATE)
            base = sb * SB_STATE
            br = _dot(ub[:, us], bre_ref[sb], NN)
            bi = _dot(ub[:, us], bim_ref[sb], NN)
            xr, xi = _cmul(cfr_ref[:, ss], cfi_ref[:, ss], br, bi)
            sr[...] = xr
            si[...] = xi
            lr[...] = _dot(dyb[:, us], cre_ref[sb], NN)
            li[...] = -_dot(dyb[:, us], cim_ref[sb], NN)
            _scan_segments(sr, si, abr, abi, apr, api, pwr_ref, pwi_ref, carry_r, carry_i, cm_r, cm_i, steps, False,
                           base)
            dcre_ref[sb] += _dot(dyb[:, us], sr[...].astype(BF16), TN)
            dcim_ref[sb] -= _dot(dyb[:, us], si[...].astype(BF16), TN)
            _scan_segments(lr, li, abr, -abi, apr, -api, pwr_ref, pwi_ref, lam_r, lam_i, cl_r, cl_i, steps, True,
                           base)
            for c0 in range(0, SB_STATE, SCAN_LANES):
                ls = slice(c0, c0 + SCAN_LANES)
                gs = slice(base + c0, base + c0 + SCAN_LANES)

                def step(j, acc, ls=ls):
                    gar, gai, pr, pi = acc
                    r0 = pl.multiple_of(j * SUBLANES, SUBLANES)
                    rws = pl.ds(r0, SUBLANES)
                    t_r, t_i = _cmul(pr, -pi, lr[rws, ls], li[rws, ls])
                    return gar + t_r, gai + t_i, sr[rws, ls], si[rws, ls]

                zero = jnp.zeros((SUBLANES, SCAN_LANES), F32)
                gar, gai, _, _ = lax.fori_loop(0, steps, step, (zero, zero, cm_r[:, gs], cm_i[:, gs]))
                gabr_ref[:, gs] += gar
                gabi_ref[:, gs] += gai
            xr, xi = lr[...].astype(BF16), li[...].astype(BF16)
            du = _dot(xr, btr_ref[sb], NT) + _dot(xi, bti_ref[sb], NT)
            useg[:, us] = du + d_ref[:, us] * dy[:, us]
            dbre_ref[sb] += _dot(ub[:, us], xr, TN)
            dbim_ref[sb] += _dot(ub[:, us], xi, TN)
        _segments_to_rows(du_ref, useg, steps, stage)

        @pl.when(pl.program_id(0) == nc - 1)
        def _():
            for src, dst in ((dbre_ref, gb2r_ref), (dbim_ref, gb2i_ref), (dcre_ref, gc2r_ref), (dcim_ref, gc2i_ref)):
                for sb in range(nsb):
                    dst[sb * MXU_DIM:(sb + 1) * MXU_DIM, :] = _block_rows(src[sb])

    rev = lambda c: nc - 1 - c
    const = lambda a: pl.BlockSpec(a.shape, lambda c: (0,) * a.ndim)
    tile = pl.BlockSpec((chunk, w), lambda c: (rev(c), 0))
    row_n = pl.BlockSpec((1, n_state), lambda c: (0, 0))
    row_w = pl.BlockSpec((1, w), lambda c: (0, 0))
    st = pl.BlockSpec((1, 1, n_state), lambda c: (rev(c), 0, 0))
    acc8 = pl.BlockSpec((SUBLANES, n_state), lambda c: (0, 0))
    big = pltpu.VMEM((chunk, SB_STATE), F32)
    small = pltpu.VMEM((chunk, w), F32)
    row = pltpu.VMEM((1, n_state), F32)
    eight = pltpu.VMEM((SUBLANES, n_state), F32)
    blk = (nsb, MXU_DIM, SB_STATE)
    held = [pltpu.VMEM(blk, BF16)] * 6 + [pltpu.VMEM(blk, F32)] * 4
    vmem = (6 * _nbytes(blk, BF16) + 4 * _nbytes(blk, F32) + 5 * _nbytes((chunk, SB_STATE), F32)
            + 12 * _nbytes((chunk, w), F32) + 20 * _nbytes(bc_rows[0].shape, F32))
    res, landed = _call(
        body, [proj, proj, y, dyg, st_re, st_im, *bc_rows, *rows_p, d_row], name=name,
        out_shape=[jax.ShapeDtypeStruct((rows, w), F32)] + [jax.ShapeDtypeStruct(b.shape, F32) for b in bc_rows[:4]]
        + [jax.ShapeDtypeStruct((SUBLANES, n_state), F32)] * 2 + [jax.ShapeDtypeStruct((1, w), F32)],
        grid=(nc,),
        in_specs=_u_specs(w, o_u, chunk, rev) + [tile, tile, st, st] + [const(b) for b in bc_rows]
        + [row_n] * 6 + [pl.BlockSpec((steps, n_state), lambda c: (0, 0))] * 2 + [row_w],
        out_specs=[tile] + [const(b) for b in bc_rows[:4]] + [acc8] * 2 + [row_w],
        scratch_shapes=held + [small] * 3 + [pltpu.VMEM((w // LANES, chunk, LANES), F32)] + [big] * 4 + [row] * 4
        + [eight] * 4,
        semantics=("arbitrary",), vmem=vmem, rider=rider)
    return res if rider is None else (res, landed)


def _out_proj_loss(merged, w_o, x, target, name):
    rows, d = x.shape
    tm, tn = _tile(rows, 1024, SUBLANES), _tile(d, 1024)

    def body(a_ref, b_ref, x_ref, t_ref, g_ref, gb_ref, l_ref):
        err = x_ref[...] + _dot(a_ref[...], b_ref[...], NN) - t_ref[...]
        g = err * (1.0 / d)
        g_ref[...] = g
        gb_ref[...] = g.astype(BF16)
        part = jnp.sum(0.5 * err * g, axis=0, keepdims=True)
        first = pl.program_id(1) == 0

        @pl.when(first)
        def _():
            l_ref[...] = part

        @pl.when(jnp.logical_not(first))
        def _():
            l_ref[...] += part

    tile = pl.BlockSpec((tm, tn), lambda j, i: (i, j))
    vmem = 2 * (_nbytes((tm, d), BF16) + _nbytes((d, tn), BF16)) + 12 * _nbytes((tm, tn), F32)
    return _pallas(
        body, name=name,
        out_shape=[jax.ShapeDtypeStruct((rows, d), F32), jax.ShapeDtypeStruct((rows, d), BF16),
                   jax.ShapeDtypeStruct((1, d), F32)],
        grid=(d // tn, rows // tm),
        in_specs=[pl.BlockSpec((tm, d), lambda j, i: (i, 0)), pl.BlockSpec((d, tn), lambda j, i: (0, j)), tile, tile],
        out_specs=[tile, tile, pl.BlockSpec((1, tn), lambda j, i: (0, j))],
        compiler_params=_params(("parallel", "arbitrary"), vmem),
    )(merged, w_o, x, target)


def _pair_sum(grad, recv, name):
    r4, cdim = recv.shape
    r = r4 // N_CHIPS
    tr = _tile(r, 544, 16)
    g4 = grad.reshape(N_CHIPS, 2, r, cdim)
    r3 = recv.reshape(N_CHIPS, r, cdim)
    core = jnp.reshape(lax.axis_index("c"), (1,)).astype(jnp.int32)

    def body(c_ref, g_ref, r_ref, o_ref):
        o_ref[...] = (g_ref[0] + r_ref[...]).astype(BF16)

    out = _pallas(
        body, name=name, out_shape=jax.ShapeDtypeStruct((N_CHIPS, r, cdim), BF16),
        grid_spec=pltpu.PrefetchScalarGridSpec(
            num_scalar_prefetch=1, grid=(N_CHIPS, r // tr),
            in_specs=[pl.BlockSpec((1, 1, tr, cdim), lambda j, i, c: (j, c[0], i, 0)),
                      pl.BlockSpec((1, tr, cdim), lambda j, i, c: (j, i, 0))],
            out_specs=pl.BlockSpec((1, tr, cdim), lambda j, i, c: (j, i, 0))),
        compiler_params=_params(("parallel", "parallel"), 6 * _nbytes((tr, cdim), F32)),
    )(core, g4, r3)
    return out.reshape(r4, cdim)


def _chip_sum(recv, name):
    r4, cdim = recv.shape
    r = r4 // N_CHIPS
    tr = _tile(r, 544, 16)
    r3 = recv.reshape(N_CHIPS, r, cdim)

    def body(r_ref, o_ref):
        acc = r_ref[0].astype(F32)
        for j in range(1, N_CHIPS):
            acc = acc + r_ref[j].astype(F32)
        o_ref[...] = acc

    return _pallas(
        body, name=name, out_shape=jax.ShapeDtypeStruct((r, cdim), F32), grid=(r // tr,),
        in_specs=[pl.BlockSpec((N_CHIPS, tr, cdim), lambda i: (0, i, 0))],
        out_specs=pl.BlockSpec((tr, cdim), lambda i: (i, 0)),
        compiler_params=_params(("parallel",), 8 * _nbytes((tr, cdim), F32)),
    )(r3)


def _adamw_math(w, g, m, v):
    m = ADAM_B1 * m + (1.0 - ADAM_B1) * g
    v = ADAM_B2 * v + (1.0 - ADAM_B2) * (g * g)
    m_hat = m / (1.0 - ADAM_B1 ** ADAM_STEP)
    v_hat = v / (1.0 - ADAM_B2 ** ADAM_STEP)
    delta = -ADAM_LR * (m_hat / (jnp.sqrt(v_hat) + ADAM_EPS) + ADAM_WD * w)
    return delta, m, v


def _adamw(w, g, m, v, name):
    rows, cols = w.shape
    tr = _tile(rows, 256, SUBLANES)

    def body(w_ref, g_ref, m_ref, v_ref, d_ref, nm_ref, nv_ref):
        d, nm, nv = _adamw_math(w_ref[...], g_ref[...], m_ref[...], v_ref[...])
        d_ref[...] = d
        nm_ref[...] = nm
        nv_ref[...] = nv

    spec = pl.BlockSpec((tr, cols), lambda i: (i, 0))
    shp = jax.ShapeDtypeStruct((rows, cols), F32)
    return _pallas(
        body, name=name, out_shape=[shp] * 3, grid=(rows // tr,), in_specs=[spec] * 4, out_specs=[spec] * 3,
        compiler_params=_params(("parallel",)),
    )(w, g, m, v)


def _adamw_chips(w, parts, m, v, name):
    rows, cols = w.shape
    assert sum(p.shape[1] for p in parts) == cols
    tr = _tile(rows, 64, 16)
    n = len(parts)

    def body(*refs):
        w_ref, m_ref, v_ref = refs[0], refs[1 + n], refs[2 + n]
        g_ref, d_ref, nm_ref, nv_ref = refs[3 + n:]
        cols_g = []
        for p_ref in refs[1:1 + n]:
            acc = p_ref[0].astype(F32)
            for j in range(1, N_CHIPS):
                acc = acc + p_ref[j].astype(F32)
            cols_g.append(acc)
        g = cols_g[0] if n == 1 else jnp.concatenate(cols_g, axis=1)
        d, nm, nv = _adamw_math(w_ref[...], g, m_ref[...], v_ref[...])
        g_ref[...] = g
        d_ref[...] = d
        nm_ref[...] = nm
        nv_ref[...] = nv

    spec = pl.BlockSpec((tr, cols), lambda i: (i, 0))
    part_specs = [pl.BlockSpec((N_CHIPS, tr, p.shape[1]), lambda i: (0, i, 0)) for p in parts]
    shp = jax.ShapeDtypeStruct((rows, cols), F32)
    return _pallas(
        body, name=name, out_shape=[shp] * 4, grid=(rows // tr,),
        in_specs=[spec] + part_specs + [spec, spec], out_specs=[spec] * 4,
        compiler_params=_params(("parallel",)),
    )(w, *[p.reshape(N_CHIPS, rows, p.shape[1]) for p in parts], m, v)


def _adamw_small(w, parts, m, v, name):
    rows, cols = w.shape
    p3 = parts.reshape(N_DEV, rows, cols)

    def body(w_ref, p_ref, m_ref, v_ref, g_ref, d_ref, nm_ref, nv_ref):
        g = p_ref[0]
        for k in range(1, N_DEV):
            g = g + p_ref[k]
        d, nm, nv = _adamw_math(w_ref[...], g, m_ref[...], v_ref[...])
        g_ref[...] = g
        d_ref[...] = d
        nm_ref[...] = nm
        nv_ref[...] = nv

    shp = jax.ShapeDtypeStruct((rows, cols), F32)
    return _pallas(body, name=name, out_shape=[shp] * 4)(w, p3, m, v)


SMALL = ("norm_w", "q_norm_w", "k_norm_w", "sinks", "A_re", "A_im", "log_dt", "B_re", "B_im", "C_re", "C_im",
         "D_skip", "b_glu")
LARGE = ("w_in", "w_attn_proj", "w_glu", "w_ssm_proj", "w_out")
ORDER = ("norm_w", "w_in", "q_norm_w", "k_norm_w", "sinks", "w_attn_proj", "A_re", "A_im", "log_dt", "B_re", "B_im",
         "C_re", "C_im", "D_skip", "w_glu", "b_glu", "w_ssm_proj", "w_out")


SMALL_REST = ("loss",) + SMALL[1:]


def _pack(named, keys):
    flat = jnp.concatenate([named[k].reshape(-1).astype(F32) for k in keys])
    n = flat.shape[0]
    rows = -(-n // (LANES * SUBLANES)) * SUBLANES
    return jnp.pad(flat, (0, rows * LANES - n)).reshape(rows, LANES)


def _unpack(packed, like, keys):
    flat = packed.reshape(-1)
    out, o = {}, 0
    for k in keys:
        n = like[k].size
        out[k] = flat[o:o + n].reshape(like[k].shape)
        o += n
    return out


def _step(xs, target, p, shards):
    s_in, s_ap, s_glu, s_sp, s_o = shards
    seq, d = xs.shape
    attn_w = (d // 128) * HEAD_DIM
    n_q = attn_w // HEAD_DIM
    kv_w = N_KV_HEADS * HEAD_DIM
    ssm_w = d // 2
    n_groups = ssm_w // GROUP
    n_state = n_groups * STATE
    in_w = N_DEV * s_in.shape[0]
    assert in_w == 2 * attn_w + 2 * kv_w + 2 * ssm_w + 2 * d
    o_u = 2 * attn_w + 2 * kv_w
    o_z = o_u + ssm_w
    o_ga = o_z + ssm_w
    chunk = min(SSM_CHUNK, seq)
    cw = d // 4

    norm_row = p["norm_w"].reshape(1, d)
    half = d // W_IN_PARTS
    assert W_IN_PARTS == 2
    s_in_parts = [s_in[:, :half], s_in[:, half:]]
    h, (w_lo,) = _rmsnorm_fwd(xs, norm_row, "rmsnorm_fwd", rider=_all_gather(s_in_parts[:1]))
    part, (w_hi,) = _matmul(Cols(h, 0, half), w_lo, mode="nt", name="in_proj_0", tn=2176, out_dtype=BF16,
                            rider=_all_gather(s_in_parts[1:]))
    proj = _matmul(Cols(h, half, half), w_hi, mode="nt", name="in_proj_1", tn=2176, out_dtype=BF16, add=part)
    w_in_parts = [w_lo, w_hi]
    qw_row = jnp.tile(p["q_norm_w"], n_q).reshape(1, attn_w)
    kw_row = jnp.tile(p["k_norm_w"], N_KV_HEADS).reshape(1, kv_w)
    gmat = _head_mean_matrix()
    ag = _attention_fwd(proj, qw_row, kw_row, gmat, p["sinks"], attn_w=attn_w, kv_w=kv_w, name="attention_fwd")

    log_dt_col = p["log_dt"].reshape(n_groups, 1)
    prep = _ssm_prep(p["A_re"], p["A_im"], log_dt_col, chunk // SUBLANES, "ssm_prep")
    rows_p = [v.reshape(1, n_state) for v in prep[:6]] + [v.reshape(-1, n_state) for v in prep[6:]]
    bt_re, bt_im = p["B_re"].transpose(0, 2, 1), p["B_im"].transpose(0, 2, 1)
    cf_re, cf_im = prep[2][:, None, :], prep[3][:, None, :]
    bc_rows = [_ssm_rows(m) for m in (bt_re, bt_im, p["C_re"], p["C_im"],
                                      cf_re * bt_re - cf_im * bt_im, cf_re * bt_im + cf_im * bt_re)]
    d_row = p["D_skip"].reshape(1, ssm_w)
    (y_ssm, st_re, st_im, yg), (w_ap_t, w_glu_t, w_sp_t, w_o) = _ssm_fwd(
        proj, o_u, bc_rows[:4], rows_p, d_row, chunk=chunk, name="ssm_fwd",
        rider=_all_gather([s_ap, s_glu, s_sp, s_o]))
    glu = _matmul(yg, w_glu_t, mode="nt", name="glu_proj", out_dtype=BF16, bias=p["b_glu"].reshape(1, 2 * ssm_w))
    (ts,) = _ew(lambda ga, gb, z: ga * _sigmoid(gb) * _silu(z), name="glu_gate", rows=seq, width=ssm_w,
                tiles=[(glu, 0), (glu, ssm_w), (proj, o_z)], outs=[(BF16, ssm_w, 0)], cw=cw)
    yy = _matmul(ag, w_ap_t, mode="nt", name="attn_proj", out_dtype=BF16, out_cols=(2 * d, 0))
    yy = _matmul(ts, w_sp_t, mode="nt", name="ssm_proj", out_dtype=BF16, out_cols=(2 * d, d), into=yy)
    (merged,) = _ew(lambda ya, ys, ga, gs: _sigmoid(ga) * ya + _sigmoid(gs) * ys, name="merge", rows=seq, width=d,
                    tiles=[(yy, 0), (yy, d), (proj, o_ga), (proj, o_ga + d)], outs=[(BF16, d, 0)], cw=cw)
    dout, dout_b, loss_cols = _out_proj_loss(merged, w_o, xs, target, "out_proj_loss")
    loss_local = jnp.sum(loss_cols)

    g_w_o = _matmul(merged, dout_b, mode="tn", name="grad_w_out", tm=512, tk=4096)
    dmerged = _matmul(dout_b, w_o, mode="nt", name="d_merged", out_dtype=BF16)

    def merge_bwd(dm, y, g):
        s = _sigmoid(g)
        return dm * s, dm * y * s * (1.0 - s)

    dyy, dproj = _ew(merge_bwd, name="merge_bwd", rows=seq, width=2 * d,
                     tiles=[(dmerged, 0, d), (yy, 0), (proj, o_ga)],
                     outs=[(BF16, 2 * d, 0), (BF16, in_w, o_ga)], cw=cw)
    dy_a, dy_s = Cols(dyy, 0, d), Cols(dyy, d, d)
    g_w_ap_t = _matmul(dy_a, ag, mode="tn", name="grad_w_attn_proj", tm=512, tk=4096)
    g_w_sp_t = _matmul(dy_s, ts, mode="tn", name="grad_w_ssm_proj", tm=512, tk=4096)
    d_ag = _matmul(dy_a, w_ap_t, mode="nn", name="d_attn_gated", out_dtype=BF16)
    d_ts = _matmul(dy_s, w_sp_t, mode="nn", name="d_ssm_gated", out_dtype=BF16)

    (dproj, g_qw, g_kw, g_sinks), (sib_o, sib_ap, sib_sp) = _attention_bwd(
        proj, d_ag, dproj, qw_row, kw_row, gmat, p["sinks"], attn_w=attn_w, kv_w=kv_w, name="attention_bwd",
        rider=_sibling_exchange([g_w_o, g_w_ap_t, g_w_sp_t]))
    pair_o = _pair_sum(g_w_o, sib_o, "pair_sum_w_out")
    pair_ap = _pair_sum(g_w_ap_t, sib_ap, "pair_sum_w_attn_proj")
    pair_sp = _pair_sum(g_w_sp_t, sib_sp, "pair_sum_w_ssm_proj")

    n_half = ssm_w // _tile(2 * ssm_w, cw)

    def glu_bwd(j, dt, ga, gb, z):
        sb, sz = _sigmoid(gb), _silu(z)
        dg = jnp.where(j < n_half, dt * sb * sz, dt * ga * sb * (1.0 - sb) * sz)
        return dg, dg

    glu_ops = [(d_ts, 0, ssm_w), (glu, 0, ssm_w), (glu, ssm_w, ssm_w), (proj, o_z, ssm_w)]
    dglu, g_bglu = _ew(glu_bwd, name="glu_bwd", rows=seq, width=2 * ssm_w, tiles=glu_ops,
                       outs=[(BF16, 2 * ssm_w, 0)], accs=1, cw=cw, with_col=True)
    (dproj,) = _ew(lambda dt, ga, gb, z: dt * ga * _sigmoid(gb) * _dsilu(z), name="glu_bwd_z", rows=seq,
                   width=ssm_w, tiles=glu_ops, outs=[(BF16, in_w, o_z)], into=[dproj], cw=cw)
    g_w_glu_t = _matmul(dglu, yg, mode="tn", name="grad_w_glu", tm=512, tk=4096)
    d_yg = _matmul(dglu, w_glu_t, mode="nn", name="d_gelu", out_dtype=BF16)
    ((du, dbt_re, dbt_im, dc_re, dc_im, gabr, gabi, g_d), (chips_o, chips_ap, chips_sp, sib_glu)) = _ssm_bwd(
        proj, o_u, y_ssm, d_yg, st_re, st_im, bc_rows, rows_p, d_row, chunk=chunk, name="ssm_bwd",
        rider=_join(_chip_exchange([pair_o, pair_ap, pair_sp]), _sibling_exchange([g_w_glu_t])))
    pair_glu = _pair_sum(g_w_glu_t, sib_glu, "pair_sum_w_glu")
    (dproj,) = _ew(lambda v: v, name="du_store", rows=seq, width=ssm_w, tiles=[(du, 0)],
                   outs=[(BF16, in_w, o_u)], into=[dproj], cw=cw)
    g_a_re, g_a_im, g_log_dt, g_bt_re, g_bt_im = _ssm_param_bwd(
        p["A_re"], p["A_im"], log_dt_col, *[g.reshape(SUBLANES, n_groups, STATE) for g in (gabr, gabi)],
        bt_re, bt_im, _from_ssm_rows(dbt_re), _from_ssm_rows(dbt_im), "ssm_param_bwd")
    small_grads = dict(
        loss=loss_local, q_norm_w=g_qw.reshape(n_q, HEAD_DIM).sum(0), k_norm_w=g_kw.reshape(N_KV_HEADS, HEAD_DIM).sum(0),
        sinks=g_sinks[0, :n_q], A_re=g_a_re, A_im=g_a_im, log_dt=g_log_dt.reshape(n_groups),
        B_re=g_bt_re.transpose(0, 2, 1), B_im=g_bt_im.transpose(0, 2, 1),
        C_re=_from_ssm_rows(dc_re), C_im=_from_ssm_rows(dc_im),
        D_skip=g_d.reshape(n_groups, GROUP), b_glu=g_bglu.reshape(2 * ssm_w))

    n_parts = W_IN_PARTS
    wq = d // n_parts
    g_parts, pair_parts, chip_parts = [], [], []
    extra = [_chip_exchange([pair_glu]), _all_gather([_pack(small_grads, SMALL_REST)])]
    chips_glu = small_parts = dh = None
    for step in range(n_parts + 2):
        riders = list(extra) if step == 0 else []
        if 0 <= step - 2 < n_parts:
            riders.append(_chip_exchange([pair_parts[step - 2]]))
        if 0 <= step - 1 < n_parts:
            riders.append(_sibling_exchange([g_parts[step - 1]]))
        rider = _join(*riders) if riders else None
        if step < n_parts:
            res = _matmul(dproj, Cols(h, step * wq, wq), mode="tn", name="grad_w_in_%d" % step, tk=4096, rider=rider)
            out, landed = res if rider is not None else (res, [])
            g_parts.append(out)
        else:
            q = step - n_parts
            dh, landed = _matmul(dproj, w_in_parts[q], mode="nn", name="d_normed_%d" % q, tk=2176,
                                 out_cols=(d, q * wq), into=dh, rider=rider)
        landed = list(landed)
        if step == 0:
            chips_glu, small_parts = landed[:2]
            landed = landed[2:]
        if 0 <= step - 2 < n_parts:
            chip_parts.append(landed.pop(0))
        if 0 <= step - 1 < n_parts:
            pair_parts.append(_pair_sum(g_parts[step - 1], landed.pop(0), "pair_sum_w_in_%d" % (step - 1)))
    grad_x, g_norm = _rmsnorm_bwd(xs, norm_row, dh, dout, "rmsnorm_bwd")
    (norm_parts,) = _exchange(_all_gather([_pack(dict(norm_w=g_norm), ("norm_w",))]), "gather_norm_grad")
    from_chips = dict(zip(LARGE, (chip_parts, [chips_ap], [chips_glu], [chips_sp], [chips_o])))
    return grad_x, from_chips, small_parts, norm_parts


def kernel(x, norm_w, w_in, q_norm_w, k_norm_w, sinks, w_attn_proj, A_re, A_im, log_dt, B_re, B_im, C_re, C_im, D_skip, w_glu, b_glu, w_ssm_proj, w_out, loss_target, m_norm_w, m_w_in, m_q_norm_w, m_k_norm_w, m_sinks, m_w_attn_proj, m_A_re, m_A_im, m_log_dt, m_B_re, m_B_im, m_C_re, m_C_im, m_D_skip, m_w_glu, m_b_glu, m_w_ssm_proj, m_w_out, v_norm_w, v_w_in, v_q_norm_w, v_k_norm_w, v_sinks, v_w_attn_proj, v_A_re, v_A_im, v_log_dt, v_B_re, v_B_im, v_C_re, v_C_im, v_D_skip, v_w_glu, v_b_glu, v_w_ssm_proj, v_w_out):
    weights = dict(norm_w=norm_w, w_in=w_in, q_norm_w=q_norm_w, k_norm_w=k_norm_w, sinks=sinks,
                   w_attn_proj=w_attn_proj, A_re=A_re, A_im=A_im, log_dt=log_dt, B_re=B_re, B_im=B_im, C_re=C_re,
                   C_im=C_im, D_skip=D_skip, w_glu=w_glu, b_glu=b_glu, w_ssm_proj=w_ssm_proj, w_out=w_out)
    m_in = dict(norm_w=m_norm_w, w_in=m_w_in, q_norm_w=m_q_norm_w, k_norm_w=m_k_norm_w, sinks=m_sinks,
                w_attn_proj=m_w_attn_proj, A_re=m_A_re, A_im=m_A_im, log_dt=m_log_dt, B_re=m_B_re, B_im=m_B_im,
                C_re=m_C_re, C_im=m_C_im, D_skip=m_D_skip, w_glu=m_w_glu, b_glu=m_b_glu, w_ssm_proj=m_w_ssm_proj,
                w_out=m_w_out)
    v_in = dict(norm_w=v_norm_w, w_in=v_w_in, q_norm_w=v_q_norm_w, k_norm_w=v_k_norm_w, sinks=v_sinks,
                w_attn_proj=v_w_attn_proj, A_re=v_A_re, A_im=v_A_im, log_dt=v_log_dt, B_re=v_B_re, B_im=v_B_im,
                C_re=v_C_re, C_im=v_C_im, D_skip=v_D_skip, w_glu=v_w_glu, b_glu=v_b_glu, w_ssm_proj=v_w_ssm_proj,
                w_out=v_w_out)

    _, seq, d = x.shape
    column_sharded = LARGE[:4]
    as_rows = lambda k, a: a.T if k in column_sharded else a
    shards = [as_rows(k, weights[k]).astype(BF16) for k in LARGE]
    small = {k: weights[k] for k in SMALL}
    grad_x, from_chips, small_parts, norm_parts = _step(x.reshape(seq, d), loss_target.reshape(seq, d), small,
                                                        shards)

    grads, delta, new_m, new_v = {}, {}, {}, {}
    for k in LARGE:
        if k == "w_in":
            res = _adamw_chips(weights[k].T, from_chips[k], m_in[k].T, v_in[k].T, "adamw_" + k)
            grads[k], delta[k], new_m[k], new_v[k] = [a.T for a in res]
        elif k == "w_out":
            grads[k], delta[k], new_m[k], new_v[k] = _adamw_chips(weights[k], from_chips[k], m_in[k], v_in[k],
                                                                  "adamw_" + k)
        else:
            grads[k] = _chip_sum(from_chips[k][0], "chip_sum_" + k).T
            delta[k], new_m[k], new_v[k] = _adamw(weights[k], grads[k], m_in[k], v_in[k], "adamw_" + k)

    zero = jnp.zeros((), F32)
    for keys, parts in ((SMALL_REST, small_parts), (("norm_w",), norm_parts)):
        like = dict(small, loss=zero)
        packs = [_pack(dict(src, loss=zero), keys) for src in (weights, m_in, v_in)]
        res = _adamw_small(packs[0], parts, packs[1], packs[2], "adamw_small_%d" % len(keys))
        for dst, r in zip((grads, delta, new_m, new_v), res):
            dst.update(_unpack(r, like, keys))
    loss = grads["loss"]

    return (loss, grad_x.reshape(x.shape), *[grads[k] for k in ORDER], *[delta[k] for k in ORDER],
            *[new_m[k] for k in ORDER], *[new_v[k] for k in ORDER])
```

```python
import math
from typing import Callable, NamedTuple

import jax
import jax.numpy as jnp
import numpy as np
from jax import lax
from jax.experimental import pallas as pl
from jax.experimental.pallas import tpu as pltpu

F32 = jnp.float32
BF16 = jnp.bfloat16
MESH = pl.DeviceIdType.MESH

HEAD_DIM = 64
N_KV_HEADS = 4
GROUP = 16
STATE = 64
BLOCK = 128
NORM_EPS = 1e-6
N_DEV = 8
N_CHIPS = 4
LANES = 128
SUBLANES = 8
MXU_DIM = 256
VMEM_BYTES = 64 * 1024 * 1024
VMEM_CAP = VMEM_BYTES - 8 * 1024 * 1024

ADAM_LR = 0.001
ADAM_B1 = 0.9
ADAM_B2 = 0.999
ADAM_EPS = 1e-08
ADAM_WD = 0.01
ADAM_STEP = 10

GELU_C = math.sqrt(2.0 / math.pi)
GELU_K = 0.044715


def _tile(dim, pref, mult=LANES):
    if dim <= pref:
        return dim
    best = None
    for d in range(mult, pref + 1, mult):
        if dim % d == 0:
            best = d
    assert best is not None, (dim, pref, mult)
    return best


def _params(semantics=None, vmem=None):
    kw = {}
    if semantics is not None:
        kw["dimension_semantics"] = semantics
    if vmem is not None:
        kw["vmem_limit_bytes"] = int(min(VMEM_CAP, max(vmem, 32 * 1024 * 1024)))
    return pltpu.CompilerParams(**kw)


def _nbytes(shape, dtype):
    return math.prod(shape) * jnp.dtype(dtype).itemsize


def _sigmoid(x):
    return 1.0 / (1.0 + jnp.exp(-x))


def _silu(x):
    return x * _sigmoid(x)


def _dsilu(x):
    s = _sigmoid(x)
    return s * (1.0 + x * (1.0 - s))


def _gelu(x):
    return 0.5 * x * (1.0 + jnp.tanh(GELU_C * (x + GELU_K * x * x * x)))


def _dgelu(x):
    t = jnp.tanh(GELU_C * (x + GELU_K * x * x * x))
    return 0.5 * (1.0 + t) + 0.5 * x * (1.0 - t * t) * GELU_C * (1.0 + 3.0 * GELU_K * x * x)


def _dot(a, b, dims):
    return lax.dot_general(a, b, (dims, ((), ())), preferred_element_type=F32)


NN = ((1,), (0,))
NT = ((1,), (1,))
TN = ((0,), (0,))


def _any_spec():
    return pl.BlockSpec(memory_space=pl.ANY)


def _pallas(body, **kw):
    pin = lambda s: pltpu.HBM(s.shape, s.dtype) if isinstance(s, jax.ShapeDtypeStruct) else s
    out_shape = kw.pop("out_shape")
    out_shape = [pin(s) for s in out_shape] if isinstance(out_shape, (list, tuple)) else pin(out_shape)
    call = pl.pallas_call(body, out_shape=out_shape, **kw)

    def run(*operands):
        pinned = [pltpu.with_memory_space_constraint(o, pltpu.HBM) if jnp.issubdtype(o.dtype, jnp.floating) else o
                  for o in operands]
        return call(*pinned)

    return run


class Rider(NamedTuple):
    operands: tuple
    out_shapes: tuple
    sems: tuple
    start: Callable
    finish: Callable


def _all_gather(shards):
    n = len(shards)

    def copies(ins, outs, sems):
        send_sems, recv_sems, local_sems = sems
        x, y, c = lax.axis_index("x"), lax.axis_index("y"), lax.axis_index("c")
        me, sibling = (x, y, c), (x, y, 1 - c)
        chips = [(1 - x, y), (x, 1 - y), (1 - x, 1 - y)]

        def rows(k, px, py, pc):
            r = shards[k].shape[0]
            return outs[k].at[pl.ds((4 * px + 2 * py + pc) * r, r), :]

        def copy(k, s, block, to, src=None):
            return pltpu.make_async_remote_copy(
                src_ref=rows(k, *block) if src is None else src, dst_ref=rows(k, *block),
                send_sem=send_sems.at[7 * k + s], recv_sem=recv_sems.at[7 * k + s],
                device_id=to, device_id_type=MESH)

        mine = [pltpu.make_async_copy(ins[k], rows(k, *me), local_sems.at[k]) for k in range(n)]
        first = []
        for k in range(n):
            first.append(copy(k, 0, me, sibling, src=ins[k]))
            first += [copy(k, 1 + j, me, (*chip, c), src=ins[k]) for j, chip in enumerate(chips)]
        return me, sibling, chips, c, copy, mine, first

    def start(ins, outs, sems):
        *_, mine, first = copies(ins, outs, sems)
        for cp in mine + first:
            cp.start()

    def finish(ins, outs, sems):
        me, sibling, chips, c, copy, mine, first = copies(ins, outs, sems)
        passed = []
        for j, chip in enumerate(chips):
            for k in range(n):
                copy(k, 1 + j, (*chip, c), me).wait_recv()
                fwd = copy(k, 4 + j, (*chip, c), sibling)
                fwd.start()
                passed.append(fwd)
        for k in range(n):
            copy(k, 0, sibling, me).wait_recv()
            for j, chip in enumerate(chips):
                copy(k, 4 + j, (*chip, 1 - c), me).wait_recv()
        for cp in first + passed:
            cp.wait_send()
        for cp in mine:
            cp.wait()

    return Rider(
        tuple(shards),
        tuple(jax.ShapeDtypeStruct((N_DEV * s.shape[0], s.shape[1]), s.dtype) for s in shards),
        (pltpu.SemaphoreType.DMA((7 * n,)), pltpu.SemaphoreType.DMA((7 * n,)), pltpu.SemaphoreType.DMA((n,))),
        start, finish)


def _sibling_exchange(grads):
    n = len(grads)

    def copies(ins, outs, sems):
        send_sems, recv_sems = sems
        x, y, c = lax.axis_index("x"), lax.axis_index("y"), lax.axis_index("c")
        out = []
        for k in range(n):
            r = grads[k].shape[0] // N_DEV
            for j in range(N_CHIPS):
                out.append(pltpu.make_async_remote_copy(
                    src_ref=ins[k].at[pl.ds((2 * j + 1 - c) * r, r), :],
                    dst_ref=outs[k].at[pl.ds(j * r, r), :],
                    send_sem=send_sems.at[N_CHIPS * k + j], recv_sem=recv_sems.at[N_CHIPS * k + j],
                    device_id=(x, y, 1 - c), device_id_type=MESH))
        return out

    def start(ins, outs, sems):
        for cp in copies(ins, outs, sems):
            cp.start()

    def finish(ins, outs, sems):
        for cp in copies(ins, outs, sems):
            cp.wait()

    return Rider(
        tuple(grads), tuple(jax.ShapeDtypeStruct((g.shape[0] // 2, g.shape[1]), g.dtype) for g in grads),
        (pltpu.SemaphoreType.DMA((N_CHIPS * n,)), pltpu.SemaphoreType.DMA((N_CHIPS * n,))), start, finish)


def _chip_exchange(parts):
    n = len(parts)

    def copies(ins, outs, sems):
        send_sems, recv_sems, local_sems = sems
        x, y, c = lax.axis_index("x"), lax.axis_index("y"), lax.axis_index("c")
        my_chip = 2 * x + y
        chips = [(1 - x, y), (x, 1 - y), (1 - x, 1 - y)]
        local, sent = [], []
        for k in range(n):
            r = parts[k].shape[0] // N_CHIPS
            mine = pl.ds(my_chip * r, r)
            local.append(pltpu.make_async_copy(ins[k].at[mine, :], outs[k].at[mine, :], local_sems.at[k]))
            for s, (px, py) in enumerate(chips):
                sent.append(pltpu.make_async_remote_copy(
                    src_ref=ins[k].at[pl.ds((2 * px + py) * r, r), :], dst_ref=outs[k].at[mine, :],
                    send_sem=send_sems.at[3 * k + s], recv_sem=recv_sems.at[3 * k + s],
                    device_id=(px, py, c), device_id_type=MESH))
        return local, sent

    def start(ins, outs, sems):
        local, sent = copies(ins, outs, sems)
        for cp in local + sent:
            cp.start()

    def finish(ins, outs, sems):
        local, sent = copies(ins, outs, sems)
        for cp in sent + local:
            cp.wait()

    return Rider(
        tuple(parts), tuple(jax.ShapeDtypeStruct(p.shape, p.dtype) for p in parts),
        (pltpu.SemaphoreType.DMA((3 * n,)), pltpu.SemaphoreType.DMA((3 * n,)), pltpu.SemaphoreType.DMA((n,))),
        start, finish)


def _join(*riders):
    cuts_in, cuts_out, cuts_sem = [0], [0], [0]
    for r in riders:
        cuts_in.append(cuts_in[-1] + len(r.operands))
        cuts_out.append(cuts_out[-1] + len(r.out_shapes))
        cuts_sem.append(cuts_sem[-1] + len(r.sems))

    def each(which):
        def run(ins, outs, sems):
            for i, r in enumerate(riders):
                getattr(r, which)(ins[cuts_in[i]:cuts_in[i + 1]], outs[cuts_out[i]:cuts_out[i + 1]],
                                  sems[cuts_sem[i]:cuts_sem[i + 1]])
        return run

    return Rider(sum((r.operands for r in riders), ()), sum((r.out_shapes for r in riders), ()),
                 sum((r.sems for r in riders), ()), each("start"), each("finish"))


def _call(body, operands, *, name, out_shape, grid, in_specs, out_specs, scratch_shapes=(), aliases=None,
          semantics=None, vmem=None, rider=None):
    operands, out_shape, scratch_shapes = list(operands), list(out_shape), list(scratch_shapes)
    in_specs, out_specs = list(in_specs), list(out_specs)
    if rider is None:
        res = _pallas(
            body, name=name, out_shape=out_shape, grid=grid, in_specs=in_specs, out_specs=out_specs,
            scratch_shapes=scratch_shapes, input_output_aliases=aliases or {},
            compiler_params=_params(semantics, vmem))(*operands)
        return list(res), []
    n_in, n_out, n_scr = len(operands), len(out_shape), len(scratch_shapes)
    ri, ro = len(rider.operands), len(rider.out_shapes)

    def carried(*refs):
        a, b = n_in, n_in + ri
        c, d = b + n_out, b + n_out + ro
        e = d + n_scr
        ids = [pl.program_id(k) for k in range(len(grid))]
        first = ids[0] == 0
        last = ids[0] == grid[0] - 1
        for k in range(1, len(grid)):
            first = jnp.logical_and(first, ids[k] == 0)
            last = jnp.logical_and(last, ids[k] == grid[k] - 1)

        @pl.when(first)
        def _():
            rider.start(refs[a:b], refs[c:d], refs[e:])

        body(*refs[:a], *refs[b:c], *refs[d:e])

        @pl.when(last)
        def _():
            rider.finish(refs[a:b], refs[c:d], refs[e:])

    res = _pallas(
        carried, name=name, out_shape=out_shape + list(rider.out_shapes), grid=grid,
        in_specs=in_specs + [_any_spec()] * ri, out_specs=out_specs + [_any_spec()] * ro,
        scratch_shapes=scratch_shapes + list(rider.sems), input_output_aliases=aliases or {},
        compiler_params=_params(("arbitrary",) * len(grid), vmem))(*operands, *rider.operands)
    return list(res[:n_out]), list(res[n_out:])


def _exchange(rider, name):
    ri, ro = len(rider.operands), len(rider.out_shapes)

    def body(*refs):
        rider.start(refs[:ri], refs[ri:ri + ro], refs[ri + ro:])
        rider.finish(refs[:ri], refs[ri:ri + ro], refs[ri + ro:])

    return _pallas(
        body, name=name, out_shape=list(rider.out_shapes), in_specs=[_any_spec()] * ri,
        out_specs=[_any_spec()] * ro, scratch_shapes=list(rider.sems))(*rider.operands)


class Cols(NamedTuple):
    arr: jax.Array
    off: int
    width: int


def _cols(a):
    return a if isinstance(a, Cols) else Cols(a, 0, a.shape[1])


def _matmul(a, b, *, mode, name, out_dtype=F32, tm=1024, tn=1024, tk=2048, bias=None, add=None, out_cols=None,
            into=None, rider=None):
    a, b = _cols(a), _cols(b)
    if mode == "nn":
        (m, k), (k2, n) = (a.arr.shape[0], a.width), (b.arr.shape[0], b.width)
    elif mode == "nt":
        (m, k), (n, k2) = (a.arr.shape[0], a.width), (b.arr.shape[0], b.width)
    else:
        (k, m), (k2, n) = (a.arr.shape[0], a.width), (b.arr.shape[0], b.width)
    assert k == k2, (a.arr.shape, b.arr.shape, mode)
    tm, tn, tk = _tile(m, tm), _tile(n, tn), _tile(k, tk)
    nk = k // tk
    dims = {"nn": NN, "nt": NT, "tn": TN}[mode]
    if mode == "tn":
        assert a.off % tm == 0
        a_spec = pl.BlockSpec((tk, tm), lambda i, j, kk, o=a.off // tm: (kk, i + o))
    else:
        assert a.off % tk == 0
        a_spec = pl.BlockSpec((tm, tk), lambda i, j, kk, o=a.off // tk: (i, kk + o))
    if mode == "nt":
        assert b.off % tk == 0
        b_spec = pl.BlockSpec((tn, tk), lambda i, j, kk, o=b.off // tk: (j, kk + o))
    else:
        assert b.off % tn == 0
        b_spec = pl.BlockSpec((tk, tn), lambda i, j, kk, o=b.off // tn: (kk, j + o))
    in_specs, operands = [a_spec, b_spec], [a.arr, b.arr]
    assert bias is None or add is None
    if bias is not None:
        in_specs.append(pl.BlockSpec((1, tn), lambda i, j, kk: (0, j)))
        operands.append(bias)
    if add is not None:
        assert add.shape == (m, n)
        in_specs.append(pl.BlockSpec((tm, tn), lambda i, j, kk: (i, j)))
        operands.append(add)
    total_w, o_off = out_cols if out_cols is not None else (n, 0)
    assert o_off % tn == 0
    aliases = {}
    if into is not None:
        assert into.shape == (m, total_w) and into.dtype == out_dtype
        in_specs.append(_any_spec())
        operands.append(into)
        aliases = {len(operands) - 1: 0}
    n_in = len(operands)

    def body(*refs):
        a_ref, b_ref = refs[0], refs[1]
        bias_ref = refs[2] if bias is not None or add is not None else None
        o_ref = refs[n_in]
        acc_ref = refs[-1] if nk > 1 else None
        part = _dot(a_ref[...].astype(BF16), b_ref[...].astype(BF16), dims)

        def finish(acc):
            if bias_ref is not None:
                acc = acc + bias_ref[...]
            o_ref[...] = acc.astype(out_dtype)

        if nk == 1:
            finish(part)
        else:
            kk = pl.program_id(2)

            @pl.when(kk == 0)
            def _():
                acc_ref[...] = part

            @pl.when(kk > 0)
            def _():
                acc_ref[...] += part

            @pl.when(kk == nk - 1)
            def _():
                finish(acc_ref[...])

    vmem = 2 * (_nbytes((tm, tk), a.arr.dtype) + _nbytes((tk, tn), b.arr.dtype) + _nbytes((tm, tn), out_dtype))
    vmem += 3 * _nbytes((tm, tn), F32)
    (out,), landed = _call(
        body, operands, name=name, out_shape=[jax.ShapeDtypeStruct((m, total_w), out_dtype)],
        grid=(m // tm, n // tn, nk), in_specs=in_specs,
        out_specs=[pl.BlockSpec((tm, tn), lambda i, j, kk, o=o_off // tn: (i, j + o))],
        scratch_shapes=[pltpu.VMEM((tm, tn), F32)] if nk > 1 else [], aliases=aliases,
        semantics=("parallel", "parallel", "arbitrary"), vmem=vmem, rider=rider)
    return out if rider is None else (out, landed)


def _ew(fn, *, name, rows, width, tiles, vecs=(), outs, accs=0, tl=1024, cw=512, into=None, with_col=False):
    tl, cw = _tile(rows, tl, SUBLANES), _tile(width, cw)
    ncol = width // cw
    nt_, nv = len(tiles), len(vecs)
    into = list(into) if into is not None else [None] * len(outs)
    aliased = [t for t in into if t is not None]

    def off(o):
        assert o % cw == 0, (name, o, cw)
        return o // cw

    in_specs, vmem = [], 0
    for t in tiles:
        arr, o = t[0], off(t[1])
        wrap = t[2] // cw if len(t) > 2 else ncol
        in_specs.append(pl.BlockSpec((tl, cw), lambda j, i, o=o, wrap=wrap: (i, o + j % wrap)))
        vmem += _nbytes((tl, cw), arr.dtype)
    in_specs += [pl.BlockSpec((1, cw), lambda j, i, o=off(o): (0, j + o)) for _, o in vecs]
    in_specs += [_any_spec() for _ in aliased]
    out_shape, out_specs, aliases = [], [], {}
    n_in = nt_ + nv
    for idx, ((dt, tw, o), tgt) in enumerate(zip(outs, into)):
        out_shape.append(jax.ShapeDtypeStruct((rows, tw), dt))
        out_specs.append(pl.BlockSpec((tl, cw), lambda j, i, o=off(o): (i, j + o)))
        vmem += _nbytes((tl, cw), dt)
        if tgt is not None:
            assert tgt.shape == (rows, tw) and tgt.dtype == dt, (name, tgt.shape, tgt.dtype)
            aliases[n_in + len(aliases)] = idx
    for _ in range(accs):
        out_shape.append(jax.ShapeDtypeStruct((1, width), F32))
        out_specs.append(pl.BlockSpec((1, cw), lambda j, i: (0, j)))
    n_out = len(outs)

    def body(*refs):
        vals = [r[...].astype(F32) for r in refs[:n_in]]
        out_refs = refs[n_in + len(aliased):]
        res = fn(pl.program_id(0), *vals) if with_col else fn(*vals)
        res = res if isinstance(res, (tuple, list)) else (res,)
        assert len(res) == n_out + accs, (name, len(res))
        for r, v in zip(out_refs[:n_out], res[:n_out]):
            r[...] = v.astype(r.dtype)
        first = pl.program_id(1) == 0
        for r, v in zip(out_refs[n_out:], res[n_out:]):
            s = jnp.sum(v, axis=0, keepdims=True)

            @pl.when(first)
            def _(r=r, s=s):
                r[...] = s

            @pl.when(jnp.logical_not(first))
            def _(r=r, s=s):
                r[...] += s

    return _pallas(
        body, name=name, out_shape=out_shape, grid=(ncol, rows // tl),
        in_specs=in_specs, out_specs=out_specs, input_output_aliases=aliases,
        compiler_params=_params(("parallel", "arbitrary"), 3 * vmem),
    )(*[t[0] for t in tiles], *[v for v, _ in vecs], *aliased)


def _rmsnorm_fwd(x, w_row, name, rider=None):
    rows, d = x.shape
    tl = _tile(rows, 512, SUBLANES)

    def body(x_ref, w_ref, h_ref):
        xv = x_ref[...]
        rstd = lax.rsqrt(jnp.mean(xv * xv, axis=-1, keepdims=True) + NORM_EPS)
        h_ref[...] = (xv * rstd * w_ref[...]).astype(BF16)

    (h,), landed = _call(
        body, [x, w_row], name=name, out_shape=[jax.ShapeDtypeStruct((rows, d), BF16)], grid=(rows // tl,),
        in_specs=[pl.BlockSpec((tl, d), lambda i: (i, 0)), pl.BlockSpec((1, d), lambda i: (0, 0))],
        out_specs=[pl.BlockSpec((tl, d), lambda i: (i, 0))], semantics=("parallel",), rider=rider)
    return h if rider is None else (h, landed)


def _rmsnorm_bwd(x, w_row, dh, dout, name, rider=None):
    rows, d = x.shape
    tl = _tile(rows, 256, SUBLANES)

    def body(x_ref, w_ref, dh_ref, dout_ref, gx_ref, gw_ref):
        xv = x_ref[...]
        rstd = lax.rsqrt(jnp.mean(xv * xv, axis=-1, keepdims=True) + NORM_EPS)
        xn = xv * rstd
        dhv = dh_ref[...]
        dxn = dhv * w_ref[...]
        dx = rstd * (dxn - xn * jnp.mean(dxn * xn, axis=-1, keepdims=True))
        gx_ref[...] = dout_ref[...] + dx
        gw = jnp.sum(dhv * xn, axis=0, keepdims=True)

        @pl.when(pl.program_id(0) == 0)
        def _():
            gw_ref[...] = gw

        @pl.when(pl.program_id(0) > 0)
        def _():
            gw_ref[...] += gw

    tile = pl.BlockSpec((tl, d), lambda i: (i, 0))
    row = pl.BlockSpec((1, d), lambda i: (0, 0))
    res, landed = _call(
        body, [x, w_row, dh, dout], name=name,
        out_shape=[jax.ShapeDtypeStruct((rows, d), F32), jax.ShapeDtypeStruct((1, d), F32)],
        grid=(rows // tl,), in_specs=[tile, row, tile, tile], out_specs=[tile, row],
        semantics=("arbitrary",), rider=rider)
    return res if rider is None else (res, landed)


def _head_mean(x, gmat):
    hi = x.astype(BF16)
    lo = (x - hi.astype(F32)).astype(BF16)
    out = []
    for s in range(x.shape[1] // MXU_DIM):
        sl = slice(s * MXU_DIM, (s + 1) * MXU_DIM)
        out.append(_dot(hi[:, sl], gmat, NN) + _dot(lo[:, sl], gmat, NN))
    return out[0] if len(out) == 1 else jnp.concatenate(out, axis=1)


def _head_mean_matrix():
    blk = jnp.arange(MXU_DIM) // HEAD_DIM
    return jnp.where(blk[:, None] == blk[None, :], 1.0 / HEAD_DIM, 0.0).astype(BF16)


def _spread_head(x, g, width):
    col = x[:, (g // 2) * LANES:(g // 2 + 1) * LANES]
    other = pltpu.roll(col, HEAD_DIM, axis=1)
    low = lax.broadcasted_iota(jnp.int32, col.shape, 1) < HEAD_DIM
    both = jnp.where(low, col, other) if g % 2 == 0 else jnp.where(low, other, col)
    return both if width == LANES else jnp.concatenate([both] * (width // LANES), axis=1)


def _head_diagonal(t, per_kv):
    head = lax.broadcasted_iota(jnp.int32, t.shape, 1) // HEAD_DIM
    zero = jnp.zeros_like(t)
    return jnp.concatenate([jnp.where(head == r, t, zero) for r in range(per_kv)], axis=0)


def _fold_heads(x, per_kv):
    rows = x.shape[0] // per_kv
    head = lax.broadcasted_iota(jnp.int32, (rows, x.shape[1]), 1) // HEAD_DIM
    acc = jnp.where(head == 0, x[0:rows], 0.0)
    for r in range(1, per_kv):
        acc = acc + jnp.where(head == r, x[r * rows:(r + 1) * rows], 0.0)
    while acc.shape[1] > LANES:
        half = acc.shape[1] // 2
        acc = acc[:, :half] + acc[:, half:]
    return acc + pltpu.roll(acc, HEAD_DIM, axis=1)


def _join_heads(parts):
    low = lax.broadcasted_iota(jnp.int32, parts[0].shape, 1) < HEAD_DIM
    cols = [jnp.where(low, parts[2 * j], parts[2 * j + 1]) for j in range(len(parts) // 2)]
    return cols[0] if len(cols) == 1 else jnp.concatenate(cols, axis=1)


def _attn_specs(attn_w, kv_w, block=lambda s: s):
    half = attn_w // 2
    kcol, vcol = attn_w // kv_w, attn_w // kv_w + 1
    gcol = (attn_w + 2 * kv_w) // half
    prev = lambda s: jnp.maximum(block(s) - 1, 0)
    return [
        pl.BlockSpec((BLOCK, attn_w), lambda s: (block(s), 0)),
        pl.BlockSpec((BLOCK, kv_w), lambda s: (prev(s), kcol)),
        pl.BlockSpec((BLOCK, kv_w), lambda s: (block(s), kcol)),
        pl.BlockSpec((BLOCK, kv_w), lambda s: (prev(s), vcol)),
        pl.BlockSpec((BLOCK, kv_w), lambda s: (block(s), vcol)),
        pl.BlockSpec((BLOCK, half), lambda s: (block(s), gcol)),
        pl.BlockSpec((BLOCK, half), lambda s: (block(s), gcol + 1)),
    ]


def _band_mask(i):
    q_loc = lax.broadcasted_iota(jnp.int32, (BLOCK, 2 * BLOCK), 0) + BLOCK
    k_loc = lax.broadcasted_iota(jnp.int32, (BLOCK, 2 * BLOCK), 1)
    diff = q_loc - k_loc
    first_key = jnp.where(i == 0, BLOCK, 0)
    return (diff >= 0) & (diff < BLOCK) & (k_loc >= first_key)


def _softmax_with_sink(s, sink):
    m = jnp.maximum(jnp.max(s, axis=-1, keepdims=True), sink)
    p = jnp.exp(s - m)
    e_sink = jnp.exp(sink - m)
    den = jnp.sum(p, axis=-1, keepdims=True) + e_sink
    inv = 1.0 / den
    return p * inv, e_sink * inv


def _attn_block(i, q, kk, vv, qw, kw, gmat, sink_ref, per_kv):
    scale = 1.0 / math.sqrt(HEAD_DIM)
    keys = 2 * BLOCK
    valid = _band_mask(i)
    q_rstd = lax.rsqrt(_head_mean(q * q, gmat) + NORM_EPS)
    qn = q * q_rstd
    qh = (qn * qw).astype(BF16)
    k_rstd = lax.rsqrt(_head_mean(kk * kk, gmat) + NORM_EPS)
    kn = kk * k_rstd
    kh = kn * kw
    gw = per_kv * HEAD_DIM
    groups = []
    for g in range(N_KV_HEADS):
        kd = _head_diagonal(_spread_head(kh, g, gw).astype(BF16), per_kv)
        vd = _head_diagonal(_spread_head(vv, g, gw).astype(BF16), per_kv)
        qg = qh[:, g * gw:(g + 1) * gw]
        s_all = _dot(qg, kd, NT) * scale
        ps, p_sinks = [], []
        for r in range(per_kv):
            s = jnp.where(valid, s_all[:, r * keys:(r + 1) * keys], -1e30)
            p, p_sink = _softmax_with_sink(s, sink_ref[g * per_kv + r])
            ps.append(p)
            p_sinks.append(p_sink)
        pb = jnp.concatenate(ps, axis=1).astype(BF16)
        groups.append((kd, vd, qg, ps, p_sinks, pb, _dot(pb, vd, NN)))
    return qn, q_rstd, kn, k_rstd, groups


def _attention_fwd(proj, qw_row, kw_row, gmat, sinks, *, attn_w, kv_w, name):
    rows = proj.shape[0]
    per_kv = attn_w // HEAD_DIM // N_KV_HEADS

    def body(q_ref, kp_ref, kc_ref, vp_ref, vc_ref, glo_ref, ghi_ref, qw_ref, kw_ref, gm_ref, sink_ref, o_ref):
        kk = jnp.concatenate([kp_ref[...], kc_ref[...]], axis=0).astype(F32)
        vv = jnp.concatenate([vp_ref[...], vc_ref[...]], axis=0).astype(F32)
        gate = jnp.concatenate([glo_ref[...], ghi_ref[...]], axis=1).astype(F32)
        *_, groups = _attn_block(pl.program_id(0), q_ref[...].astype(F32), kk, vv, qw_ref[...], kw_ref[...], gm_ref[...],
                                 sink_ref, per_kv)
        attn = jnp.concatenate([grp[-1] for grp in groups], axis=1)
        o_ref[...] = (attn * _silu(gate)).astype(BF16)

    const = lambda a: pl.BlockSpec(a.shape, lambda i: (0, 0))
    return _pallas(
        body, name=name, out_shape=jax.ShapeDtypeStruct((rows, attn_w), BF16), grid=(rows // BLOCK,),
        in_specs=_attn_specs(attn_w, kv_w) + [const(qw_row), const(kw_row), const(gmat),
                                              pl.BlockSpec(memory_space=pltpu.SMEM)],
        out_specs=pl.BlockSpec((BLOCK, attn_w), lambda i: (i, 0)),
        compiler_params=_params(("parallel",), 40 * 1024 * 1024),
    )(proj, proj, proj, proj, proj, proj, proj, qw_row, kw_row, gmat, sinks)


def _attention_bwd(proj, d_ag, dproj, qw_row, kw_row, gmat, sinks, *, attn_w, kv_w, name, rider=None):
    rows = proj.shape[0]
    nb = rows // BLOCK
    per_kv = attn_w // HEAD_DIM // N_KV_HEADS
    gw = per_kv * HEAD_DIM
    keys = 2 * BLOCK
    scale = 1.0 / math.sqrt(HEAD_DIM)
    w_out = 2 * attn_w + 2 * kv_w
    rev = lambda s: nb - 1 - s

    def body(q_ref, kp_ref, kc_ref, vp_ref, vc_ref, glo_ref, ghi_ref, dag_ref, qw_ref, kw_ref, gm_ref, sink_ref, _,
             dp_ref, gqw_ref, gkw_ref, gs_ref, carry_ref):
        step = pl.program_id(0)
        i = rev(step)
        kk = jnp.concatenate([kp_ref[...], kc_ref[...]], axis=0).astype(F32)
        vv = jnp.concatenate([vp_ref[...], vc_ref[...]], axis=0).astype(F32)
        gate = jnp.concatenate([glo_ref[...], ghi_ref[...]], axis=1).astype(F32)
        d_ag_v = dag_ref[...].astype(F32)
        qw, kw, gmat_v = qw_ref[...], kw_ref[...], gm_ref[...]
        qn, q_rstd, kn, k_rstd, groups = _attn_block(i, q_ref[...].astype(F32), kk, vv, qw, kw, gmat_v, sink_ref,
                                                     per_kv)
        lane = lax.broadcasted_iota(jnp.int32, (SUBLANES, LANES), 1)
        sub = lax.broadcasted_iota(jnp.int32, (SUBLANES, LANES), 0)
        gsink = jnp.zeros((SUBLANES, LANES), F32)
        dq_groups, dgate_groups, dk_heads, dv_heads = [], [], [], []
        for g, (kd, vd, qg, ps, p_sinks, pb, o) in enumerate(groups):
            cs = slice(g * gw, (g + 1) * gw)
            gate_g, d_ag_g = gate[:, cs], d_ag_v[:, cs]
            dgate_groups.append(d_ag_g * o * _dsilu(gate_g))
            do = (d_ag_g * _silu(gate_g)).astype(BF16)
            dp_all = _dot(do, vd, NT)
            dss = []
            for r in range(per_kv):
                p, dp = ps[r], dp_all[:, r * keys:(r + 1) * keys]
                delta = jnp.sum(p * dp, axis=-1, keepdims=True)
                dss.append(p * (dp - delta) * scale)
                gs_h = jnp.sum(-p_sinks[r] * delta, axis=0, keepdims=True)
                gsink = gsink + jnp.where((lane == g * per_kv + r) & (sub == 0), gs_h, 0.0)
            ds = jnp.concatenate(dss, axis=1).astype(BF16)
            dq_groups.append(_dot(ds, kd, NN))
            dk_heads.append(_fold_heads(_dot(ds, qg, TN), per_kv))
            dv_heads.append(_fold_heads(_dot(pb, do, TN), per_kv))
        dqh = jnp.concatenate(dq_groups, axis=1)
        gqw = jnp.sum(dqh * qn, axis=0, keepdims=True)
        dqn = dqh * qw
        dq = q_rstd * (dqn - qn * _head_mean(dqn * qn, gmat_v))
        dkh = _join_heads(dk_heads)
        gkw = jnp.sum(dkh * kn, axis=0, keepdims=True)
        dkn = dkh * kw
        dk = k_rstd * (dkn - kn * _head_mean(dkn * kn, gmat_v))
        dkv = jnp.concatenate([dk, _join_heads(dv_heads)], axis=1)

        @pl.when(step == 0)
        def _():
            carry_ref[...] = jnp.zeros_like(carry_ref)
            gqw_ref[...] = gqw
            gkw_ref[...] = gkw
            gs_ref[...] = gsink

        @pl.when(step > 0)
        def _():
            gqw_ref[...] += gqw
            gkw_ref[...] += gkw
            gs_ref[...] += gsink

        dp_ref[:, 0:attn_w] = dq.astype(BF16)
        dp_ref[:, attn_w:attn_w + 2 * kv_w] = (dkv[BLOCK:2 * BLOCK, :] + carry_ref[...]).astype(BF16)
        dp_ref[:, attn_w + 2 * kv_w:w_out] = jnp.concatenate(dgate_groups, axis=1).astype(BF16)
        carry_ref[...] = dkv[0:BLOCK, :]

    const = lambda a: pl.BlockSpec(a.shape, lambda s: (0, 0))
    res, landed = _call(
        body, [proj, proj, proj, proj, proj, proj, proj, d_ag, qw_row, kw_row, gmat, sinks, dproj], name=name,
        out_shape=[jax.ShapeDtypeStruct(dproj.shape, BF16),
                   jax.ShapeDtypeStruct(qw_row.shape, F32), jax.ShapeDtypeStruct(kw_row.shape, F32),
                   jax.ShapeDtypeStruct((SUBLANES, LANES), F32)],
        grid=(nb,),
        in_specs=_attn_specs(attn_w, kv_w, rev) + [pl.BlockSpec((BLOCK, attn_w), lambda s: (rev(s), 0)),
                                                   const(qw_row), const(kw_row), const(gmat),
                                                   pl.BlockSpec(memory_space=pltpu.SMEM), _any_spec()],
        out_specs=[pl.BlockSpec((BLOCK, w_out), lambda s: (rev(s), 0)),
                   const(qw_row), const(kw_row), pl.BlockSpec((SUBLANES, LANES), lambda s: (0, 0))],
        scratch_shapes=[pltpu.VMEM((BLOCK, 2 * kv_w), F32)],
        aliases={12: 0}, semantics=("arbitrary",), vmem=48 * 1024 * 1024, rider=rider)
    return res if rider is None else (res, landed)


def _cmul(ar, ai, br, bi):
    return ar * br - ai * bi, ar * bi + ai * br


def _ssm_prep(a_re, a_im, log_dt_col, steps, name):
    def body(are_ref, aim_ref, ldt_ref, abr_ref, abi_ref, cfr_ref, cfi_ref, apr_ref, api_ref, pwr_ref, pwi_ref):
        are, aim = are_ref[...], aim_ref[...]
        dt = jnp.exp(ldt_ref[...])
        mag = jnp.exp(dt * are)
        abr = mag * jnp.cos(dt * aim)
        abi = mag * jnp.sin(dt * aim)
        num_re, num_im = abr - 1.0, abi
        den = are * are + aim * aim
        abr_ref[...] = abr
        abi_ref[...] = abi
        cfr_ref[...] = (num_re * are + num_im * aim) / den
        cfi_ref[...] = (num_im * are - num_re * aim) / den
        pr, pi = jnp.ones_like(abr), jnp.zeros_like(abr)
        for k in range(steps):
            pwr_ref[k] = pr
            pwi_ref[k] = pi
            pr, pi = _cmul(pr, pi, abr, abi)
        apr_ref[...] = pr
        api_ref[...] = pi

    shp = jax.ShapeDtypeStruct(a_re.shape, F32)
    pows = jax.ShapeDtypeStruct((steps,) + a_re.shape, F32)
    return _pallas(body, name=name, out_shape=[shp] * 6 + [pows] * 2)(a_re, a_im, log_dt_col)


def _ssm_param_bwd(a_re, a_im, log_dt_col, d_ab_re, d_ab_im, b_re, b_im, dbt_re, dbt_im, name):
    def body(are_ref, aim_ref, ldt_ref, gabr_ref, gabi_ref, br_ref, bi_ref, tr_ref, ti_ref,
             dar_ref, dai_ref, dldt_ref, dbr_ref, dbi_ref):
        are, aim = are_ref[...], aim_ref[...]
        dt = jnp.exp(ldt_ref[...])
        mag = jnp.exp(dt * are)
        abr = mag * jnp.cos(dt * aim)
        abi = mag * jnp.sin(dt * aim)
        den = are * are + aim * aim
        cfr = ((abr - 1.0) * are + abi * aim) / den
        cfi = (abi * are - (abr - 1.0) * aim) / den
        gabr, gabi = jnp.sum(gabr_ref[...], axis=0), jnp.sum(gabi_ref[...], axis=0)
        t_re, t_im = tr_ref[...], ti_ref[...]
        g_r, g_i = _cmul(br_ref[...], -bi_ref[...], t_re, t_im)
        gcfr, gcfi = jnp.sum(g_r, axis=1), jnp.sum(g_i, axis=1)
        dbr, dbi = _cmul(cfr[:, None, :], -cfi[:, None, :], t_re, t_im)
        dbr_ref[...] = dbr
        dbi_ref[...] = dbi
        inv_r, inv_i = are / den, -aim / den
        t_r, t_i = _cmul(inv_r, -inv_i, gcfr, gcfi)
        gabr, gabi = gabr + t_r, gabi + t_i
        q_r, q_i = _cmul(cfr, cfi, inv_r, inv_i)
        da_r, da_i = _cmul(-q_r, q_i, gcfr, gcfi)
        gz_r, gz_i = _cmul(abr, -abi, gabr, gabi)
        dar_ref[...] = da_r + dt * gz_r
        dai_ref[...] = da_i + dt * gz_i
        dldt_ref[...] = dt * jnp.sum(are * gz_r + aim * gz_i, axis=-1, keepdims=True)

    shp = jax.ShapeDtypeStruct(a_re.shape, F32)
    bshp = jax.ShapeDtypeStruct(b_re.shape, F32)
    return _pallas(body, name=name,
                   out_shape=[shp, shp, jax.ShapeDtypeStruct(log_dt_col.shape, F32), bshp, bshp])(
        a_re, a_im, log_dt_col, d_ab_re, d_ab_im, b_re, b_im, dbt_re, dbt_im)


SCAN_LANES = 1024
SSM_CHUNK = 256
W_IN_PARTS = 2


def _scan_segments(xr_ref, xi_ref, a_re, a_im, ap_re, ap_im, pw_re, pw_im, carry_re, carry_im, cm_re, cm_im, steps,
                   reverse, base):
    n = xr_ref.shape[1]
    seg_order = range(SUBLANES - 1, -1, -1) if reverse else range(SUBLANES)
    sign = -1.0 if reverse else 1.0
    for c0 in range(0, n, SCAN_LANES):
        ls = slice(c0, c0 + SCAN_LANES)
        gs = slice(base + c0, base + c0 + SCAN_LANES)
        ar = jnp.broadcast_to(a_re[:, gs], (SUBLANES, SCAN_LANES))
        ai = jnp.broadcast_to(a_im[:, gs], (SUBLANES, SCAN_LANES))
        end_r = jnp.zeros((SUBLANES, SCAN_LANES), F32)
        end_i = jnp.zeros((SUBLANES, SCAN_LANES), F32)
        for j in range(steps):
            k = j if reverse else steps - 1 - j
            rws = slice(j * SUBLANES, (j + 1) * SUBLANES)
            tr, ti = _cmul(pw_re[k:k + 1, gs], sign * pw_im[k:k + 1, gs], xr_ref[rws, ls], xi_ref[rws, ls])
            end_r, end_i = end_r + tr, end_i + ti
        cr, ci = carry_re[:, gs], carry_im[:, gs]
        apr, api = ap_re[:, gs], ap_im[:, gs]
        for r in seg_order:
            cm_re[r:r + 1, gs] = cr
            cm_im[r:r + 1, gs] = ci
            tr, ti = _cmul(apr, api, cr, ci)
            cr, ci = end_r[r:r + 1, :] + tr, end_i[r:r + 1, :] + ti
        carry_re[:, gs] = cr
        carry_im[:, gs] = ci

        def run(t, s, ar=ar, ai=ai, ls=ls):
            j = steps - 1 - t if reverse else t
            r0 = pl.multiple_of(j * SUBLANES, SUBLANES)
            sr, si = _cmul(ar, ai, s[0], s[1])
            sr = sr + xr_ref[pl.ds(r0, SUBLANES), ls]
            si = si + xi_ref[pl.ds(r0, SUBLANES), ls]
            xr_ref[pl.ds(r0, SUBLANES), ls] = sr
            xi_ref[pl.ds(r0, SUBLANES), ls] = si
            return sr, si

        lax.fori_loop(0, steps, run, (cm_re[:, gs], cm_im[:, gs]))


SB_GROUPS = MXU_DIM // GROUP
SB_STATE = SB_GROUPS * STATE


def _ssm_rows(m):
    flat = m.reshape(-1, STATE).astype(F32)
    return jnp.concatenate([flat, flat], axis=1)


def _from_ssm_rows(rows):
    return rows[:, :STATE].reshape(-1, GROUP, STATE)


def _own_group(shape):
    row_g = lax.broadcasted_iota(jnp.int32, shape, 0) // GROUP
    col_g = lax.broadcasted_iota(jnp.int32, shape, 1) // STATE
    return row_g == col_g


def _block_diagonal(rows):
    tiled = jnp.concatenate([rows] * (SB_STATE // LANES), axis=1)
    return jnp.where(_own_group(tiled.shape), tiled, 0.0).astype(BF16)


def _block_rows(acc):
    x = jnp.where(_own_group(acc.shape), acc, 0.0)
    while x.shape[1] > LANES:
        half = x.shape[1] // 2
        x = x[:, :half] + x[:, half:]
    return x + pltpu.roll(x, STATE, axis=1)


def _rows_to_segments(dst, srcs, steps, stage):
    for ref, off in srcs:
        for k in range(ref.shape[1] // LANES):
            stage[off // LANES + k] = ref[:, k * LANES:(k + 1) * LANES].astype(F32)
    for k in range(dst.shape[1] // LANES):
        for j in range(steps):
            dst[j * SUBLANES:(j + 1) * SUBLANES, k * LANES:(k + 1) * LANES] = (
                stage[k, pl.ds(j, SUBLANES, stride=steps), :])


def _segments_to_rows(dst, src, steps, stage):
    for k in range(src.shape[1] // LANES):
        for j in range(steps):
            stage[k, pl.ds(j, SUBLANES, stride=steps), :] = (
                src[j * SUBLANES:(j + 1) * SUBLANES, k * LANES:(k + 1) * LANES])
    for k in range(src.shape[1] // LANES):
        dst[:, k * LANES:(k + 1) * LANES] = stage[k]


def _u_specs(w, o_u, chunk, index):
    half = w // 2
    assert o_u % half == 0
    return [pl.BlockSpec((chunk, half), lambda c, k=k: (index(c), o_u // half + k)) for k in range(2)]


def _ssm_fwd(proj, o_u, bc_rows, rows_p, d_row, *, chunk, name, rider=None):
    rows = proj.shape[0]
    w = d_row.shape[1]
    nc = rows // chunk
    steps = chunk // SUBLANES
    nsb = w // MXU_DIM
    n_state = nsb * SB_STATE

    def body(ulo_ref, uhi_ref, b2r_ref, b2i_ref, c2r_ref, c2i_ref, abr_ref, abi_ref, cfr_ref, cfi_ref, apr_ref,
             api_ref, pwr_ref, pwi_ref, d_ref, y_ref, str_ref, sti_ref, yg_ref, bre_ref, bim_ref, cre_ref, cim_ref,
             useg, yseg, stage, sr, si,
             carry_r, carry_i, cm_r, cm_i):
        @pl.when(pl.program_id(0) == 0)
        def _():
            for src, dst in ((b2r_ref, bre_ref), (b2i_ref, bim_ref), (c2r_ref, cre_ref), (c2i_ref, cim_ref)):
                for sb in range(nsb):
                    dst[sb] = _block_diagonal(src[sb * MXU_DIM:(sb + 1) * MXU_DIM, :])
            carry_r[...] = jnp.zeros_like(carry_r)
            carry_i[...] = jnp.zeros_like(carry_i)

        str_ref[0] = carry_r[...]
        sti_ref[0] = carry_i[...]
        _rows_to_segments(useg, [(ulo_ref, 0), (uhi_ref, w // 2)], steps, stage)
        for sb in range(nsb):
            us = slice(sb * MXU_DIM, (sb + 1) * MXU_DIM)
            ss = slice(sb * SB_STATE, (sb + 1) * SB_STATE)
            ub = useg[:, us].astype(BF16)
            bur = _dot(ub, bre_ref[sb], NN)
            bui = _dot(ub, bim_ref[sb], NN)
            xr, xi = _cmul(cfr_ref[:, ss], cfi_ref[:, ss], bur, bui)
            sr[...] = xr
            si[...] = xi
            _scan_segments(sr, si, abr_ref[...], abi_ref[...], apr_ref[...], api_ref[...], pwr_ref, pwi_ref,
                           carry_r, carry_i, cm_r, cm_i, steps, False, sb * SB_STATE)
            y = _dot(sr[...].astype(BF16), cre_ref[sb], NT) - _dot(si[...].astype(BF16), cim_ref[sb], NT)
            yseg[:, us] = y + d_ref[:, us] * useg[:, us]
        _segments_to_rows(y_ref, yseg, steps, stage)
        yg_ref[...] = _gelu(y_ref[...]).astype(BF16)

    const = lambda a: pl.BlockSpec(a.shape, lambda c: (0,) * a.ndim)
    row_n = pl.BlockSpec((1, n_state), lambda c: (0, 0))
    st = pl.BlockSpec((1, 1, n_state), lambda c: (c, 0, 0))
    held = [pltpu.VMEM((nsb, MXU_DIM, SB_STATE), BF16)] * 4
    vmem = (4 * _nbytes((nsb, MXU_DIM, SB_STATE), BF16) + 4 * _nbytes((chunk, SB_STATE), F32)
            + 12 * _nbytes((chunk, w), F32) + 8 * _nbytes(bc_rows[0].shape, F32))
    res, landed = _call(
        body, [proj, proj, *bc_rows, *rows_p, d_row], name=name,
        out_shape=[jax.ShapeDtypeStruct((rows, w), F32), jax.ShapeDtypeStruct((nc, 1, n_state), F32),
                   jax.ShapeDtypeStruct((nc, 1, n_state), F32), jax.ShapeDtypeStruct((rows, w), BF16)],
        grid=(nc,),
        in_specs=_u_specs(w, o_u, chunk, lambda c: c) + [const(b) for b in bc_rows]
        + [row_n] * 6 + [pl.BlockSpec((steps, n_state), lambda c: (0, 0))] * 2 + [pl.BlockSpec((1, w), lambda c: (0, 0))],
        out_specs=[pl.BlockSpec((chunk, w), lambda c: (c, 0)), st, st, pl.BlockSpec((chunk, w), lambda c: (c, 0))],
        scratch_shapes=held + [pltpu.VMEM((chunk, w), F32), pltpu.VMEM((chunk, w), F32),
                               pltpu.VMEM((w // LANES, chunk, LANES), F32),
                               pltpu.VMEM((chunk, SB_STATE), F32), pltpu.VMEM((chunk, SB_STATE), F32),
                               pltpu.VMEM((1, n_state), F32), pltpu.VMEM((1, n_state), F32),
                               pltpu.VMEM((SUBLANES, n_state), F32), pltpu.VMEM((SUBLANES, n_state), F32)],
        semantics=("arbitrary",), vmem=vmem, rider=rider)
    return res if rider is None else (res, landed)


def _ssm_bwd(proj, o_u, y, dyg, st_re, st_im, bc_rows, rows_p, d_row, *, chunk, name, rider=None):
    rows = proj.shape[0]
    w = d_row.shape[1]
    nc = rows // chunk
    steps = chunk // SUBLANES
    nsb = w // MXU_DIM
    n_state = nsb * SB_STATE

    def body(ulo_ref, uhi_ref, y_ref, dyg_ref, str_ref, sti_ref, b2r_ref, b2i_ref, c2r_ref, c2i_ref, t2r_ref,
             t2i_ref, abr_ref, abi_ref, cfr_ref, cfi_ref, apr_ref, api_ref, pwr_ref, pwi_ref, d_ref,
             du_ref, gb2r_ref, gb2i_ref, gc2r_ref, gc2i_ref, gabr_ref, gabi_ref, dd_ref,
             bre_ref, bim_ref, cre_ref, cim_ref, btr_ref, bti_ref, dbre_ref, dbim_ref, dcre_ref, dcim_ref,
             useg, dyseg, dynat, stage, sr, si, lr, li, carry_r, carry_i, lam_r, lam_i, cm_r, cm_i, cl_r, cl_i):
        first = pl.program_id(0) == 0

        @pl.when(first)
        def _():
            for src, dst in ((b2r_ref, bre_ref), (b2i_ref, bim_ref), (c2r_ref, cre_ref), (c2i_ref, cim_ref),
                             (t2r_ref, btr_ref), (t2i_ref, bti_ref)):
                for sb in range(nsb):
                    dst[sb] = _block_diagonal(src[sb * MXU_DIM:(sb + 1) * MXU_DIM, :])
            lam_r[...] = jnp.zeros_like(lam_r)
            lam_i[...] = jnp.zeros_like(lam_i)
            for ref in (dbre_ref, dbim_ref, dcre_ref, dcim_ref, gabr_ref, gabi_ref, dd_ref):
                ref[...] = jnp.zeros_like(ref)

        dynat[...] = dyg_ref[...].astype(F32) * _dgelu(y_ref[...])
        half = w // 2
        dd_ref[:, :half] += jnp.sum(dynat[:, :half] * ulo_ref[...].astype(F32), axis=0, keepdims=True)
        dd_ref[:, half:] += jnp.sum(dynat[:, half:] * uhi_ref[...].astype(F32), axis=0, keepdims=True)
        _rows_to_segments(useg, [(ulo_ref, 0), (uhi_ref, half)], steps, stage)
        _rows_to_segments(dyseg, [(dynat, 0)], steps, stage)
        dy = dyseg[...]
        dyb = dy.astype(BF16)
        ub = useg[...].astype(BF16)
        carry_r[...] = str_ref[0]
        carry_i[...] = sti_ref[0]
        abr, abi = abr_ref[...], abi_ref[...]
        apr, api = apr_ref[...], api_ref[...]
        for sb in range(nsb):
            us = slice(sb * MXU_DIM, (sb + 1) * MXU_DIM)
            ss = slice(sb * SB_STATE, (sb + 1) * SB_STATE)
            base = sb * SB_STATE
            br = _dot(ub[:, us], bre_ref[sb], NN)
            bi = _dot(ub[:, us], bim_ref[sb], NN)
            xr, xi = _cmul(cfr_ref[:, ss], cfi_ref[:, ss], br, bi)
            sr[...] = xr
            si[...] = xi
            lr[...] = _dot(dyb[:, us], cre_ref[sb], NN)
            li[...] = -_dot(dyb[:, us], cim_ref[sb], NN)
            _scan_segments(sr, si, abr, abi, apr, api, pwr_ref, pwi_ref, carry_r, carry_i, cm_r, cm_i, steps, False,
                           base)
            dcre_ref[sb] += _dot(dyb[:, us], sr[...].astype(BF16), TN)
            dcim_ref[sb] -= _dot(dyb[:, us], si[...].astype(BF16), TN)
            _scan_segments(lr, li, abr, -abi, apr, -api, pwr_ref, pwi_ref, lam_r, lam_i, cl_r, cl_i, steps, True,
                           base)
            for c0 in range(0, SB_STATE, SCAN_LANES):
                ls = slice(c0, c0 + SCAN_LANES)
                gs = slice(base + c0, base + c0 + SCAN_LANES)

                def step(j, acc, ls=ls):
                    gar, gai, pr, pi = acc
                    r0 = pl.multiple_of(j * SUBLANES, SUBLANES)
                    rws = pl.ds(r0, SUBLANES)
                    t_r, t_i = _cmul(pr, -pi, lr[rws, ls], li[rws, ls])
                    return gar + t_r, gai + t_i, sr[rws, ls], si[rws, ls]

                zero = jnp.zeros((SUBLANES, SCAN_LANES), F32)
                gar, gai, _, _ = lax.fori_loop(0, steps, step, (zero, zero, cm_r[:, gs], cm_i[:, gs]))
                gabr_ref[:, gs] += gar
                gabi_ref[:, gs] += gai
            xr, xi = lr[...].astype(BF16), li[...].astype(BF16)
            du = _dot(xr, btr_ref[sb], NT) + _dot(xi, bti_ref[sb], NT)
            useg[:, us] = du + d_ref[:, us] * dy[:, us]
            dbre_ref[sb] += _dot(ub[:, us], xr, TN)
            dbim_ref[sb] += _dot(ub[:, us], xi, TN)
        _segments_to_rows(du_ref, useg, steps, stage)

        @pl.when(pl.program_id(0) == nc - 1)
        def _():
            for src, dst in ((dbre_ref, gb2r_ref), (dbim_ref, gb2i_ref), (dcre_ref, gc2r_ref), (dcim_ref, gc2i_ref)):
                for sb in range(nsb):
                    dst[sb * MXU_DIM:(sb + 1) * MXU_DIM, :] = _block_rows(src[sb])

    rev = lambda c: nc - 1 - c
    const = lambda a: pl.BlockSpec(a.shape, lambda c: (0,) * a.ndim)
    tile = pl.BlockSpec((chunk, w), lambda c: (rev(c), 0))
    row_n = pl.BlockSpec((1, n_state), lambda c: (0, 0))
    row_w = pl.BlockSpec((1, w), lambda c: (0, 0))
    st = pl.BlockSpec((1, 1, n_state), lambda c: (rev(c), 0, 0))
    acc8 = pl.BlockSpec((SUBLANES, n_state), lambda c: (0, 0))
    big = pltpu.VMEM((chunk, SB_STATE), F32)
    small = pltpu.VMEM((chunk, w), F32)
    row = pltpu.VMEM((1, n_state), F32)
    eight = pltpu.VMEM((SUBLANES, n_state), F32)
    blk = (nsb, MXU_DIM, SB_STATE)
    held = [pltpu.VMEM(blk, BF16)] * 6 + [pltpu.VMEM(blk, F32)] * 4
    vmem = (6 * _nbytes(blk, BF16) + 4 * _nbytes(blk, F32) + 5 * _nbytes((chunk, SB_STATE), F32)
            + 12 * _nbytes((chunk, w), F32) + 20 * _nbytes(bc_rows[0].shape, F32))
    res, landed = _call(
        body, [proj, proj, y, dyg, st_re, st_im, *bc_rows, *rows_p, d_row], name=name,
        out_shape=[jax.ShapeDtypeStruct((rows, w), F32)] + [jax.ShapeDtypeStruct(b.shape, F32) for b in bc_rows[:4]]
        + [jax.ShapeDtypeStruct((SUBLANES, n_state), F32)] * 2 + [jax.ShapeDtypeStruct((1, w), F32)],
        grid=(nc,),
        in_specs=_u_specs(w, o_u, chunk, rev) + [tile, tile, st, st] + [const(b) for b in bc_rows]
        + [row_n] * 6 + [pl.BlockSpec((steps, n_state), lambda c: (0, 0))] * 2 + [row_w],
        out_specs=[tile] + [const(b) for b in bc_rows[:4]] + [acc8] * 2 + [row_w],
        scratch_shapes=held + [small] * 3 + [pltpu.VMEM((w // LANES, chunk, LANES), F32)] + [big] * 4 + [row] * 4
        + [eight] * 4,
        semantics=("arbitrary",), vmem=vmem, rider=rider)
    return res if rider is None else (res, landed)


def _out_proj_loss(merged, w_o, x, target, name):
    rows, d = x.shape
    tm, tn = _tile(rows, 1024, SUBLANES), _tile(d, 1024)

    def body(a_ref, b_ref, x_ref, t_ref, g_ref, gb_ref, l_ref):
        err = x_ref[...] + _dot(a_ref[...], b_ref[...], NN) - t_ref[...]
        g = err * (1.0 / d)
        g_ref[...] = g
        gb_ref[...] = g.astype(BF16)
        part = jnp.sum(0.5 * err * g, axis=0, keepdims=True)
        first = pl.program_id(1) == 0

        @pl.when(first)
        def _():
            l_ref[...] = part

        @pl.when(jnp.logical_not(first))
        def _():
            l_ref[...] += part

    tile = pl.BlockSpec((tm, tn), lambda j, i: (i, j))
    vmem = 2 * (_nbytes((tm, d), BF16) + _nbytes((d, tn), BF16)) + 12 * _nbytes((tm, tn), F32)
    return _pallas(
        body, name=name,
        out_shape=[jax.ShapeDtypeStruct((rows, d), F32), jax.ShapeDtypeStruct((rows, d), BF16),
                   jax.ShapeDtypeStruct((1, d), F32)],
        grid=(d // tn, rows // tm),
        in_specs=[pl.BlockSpec((tm, d), lambda j, i: (i, 0)), pl.BlockSpec((d, tn), lambda j, i: (0, j)), tile, tile],
        out_specs=[tile, tile, pl.BlockSpec((1, tn), lambda j, i: (0, j))],
        compiler_params=_params(("parallel", "arbitrary"), vmem),
    )(merged, w_o, x, target)


def _pair_sum(grad, recv, name):
    r4, cdim = recv.shape
    r = r4 // N_CHIPS
    tr = _tile(r, 544, 16)
    g4 = grad.reshape(N_CHIPS, 2, r, cdim)
    r3 = recv.reshape(N_CHIPS, r, cdim)
    core = jnp.reshape(lax.axis_index("c"), (1,)).astype(jnp.int32)

    def body(c_ref, g_ref, r_ref, o_ref):
        o_ref[...] = (g_ref[0] + r_ref[...]).astype(BF16)

    out = _pallas(
        body, name=name, out_shape=jax.ShapeDtypeStruct((N_CHIPS, r, cdim), BF16),
        grid_spec=pltpu.PrefetchScalarGridSpec(
            num_scalar_prefetch=1, grid=(N_CHIPS, r // tr),
            in_specs=[pl.BlockSpec((1, 1, tr, cdim), lambda j, i, c: (j, c[0], i, 0)),
                      pl.BlockSpec((1, tr, cdim), lambda j, i, c: (j, i, 0))],
            out_specs=pl.BlockSpec((1, tr, cdim), lambda j, i, c: (j, i, 0))),
        compiler_params=_params(("parallel", "parallel"), 6 * _nbytes((tr, cdim), F32)),
    )(core, g4, r3)
    return out.reshape(r4, cdim)


def _chip_sum(recv, name):
    r4, cdim = recv.shape
    r = r4 // N_CHIPS
    tr = _tile(r, 544, 16)
    r3 = recv.reshape(N_CHIPS, r, cdim)

    def body(r_ref, o_ref):
        acc = r_ref[0].astype(F32)
        for j in range(1, N_CHIPS):
            acc = acc + r_ref[j].astype(F32)
        o_ref[...] = acc

    return _pallas(
        body, name=name, out_shape=jax.ShapeDtypeStruct((r, cdim), F32), grid=(r // tr,),
        in_specs=[pl.BlockSpec((N_CHIPS, tr, cdim), lambda i: (0, i, 0))],
        out_specs=pl.BlockSpec((tr, cdim), lambda i: (i, 0)),
        compiler_params=_params(("parallel",), 8 * _nbytes((tr, cdim), F32)),
    )(r3)


def _adamw_math(w, g, m, v):
    m = ADAM_B1 * m + (1.0 - ADAM_B1) * g
    v = ADAM_B2 * v + (1.0 - ADAM_B2) * (g * g)
    m_hat = m / (1.0 - ADAM_B1 ** ADAM_STEP)
    v_hat = v / (1.0 - ADAM_B2 ** ADAM_STEP)
    delta = -ADAM_LR * (m_hat / (jnp.sqrt(v_hat) + ADAM_EPS) + ADAM_WD * w)
    return delta, m, v


def _adamw(w, g, m, v, name):
    rows, cols = w.shape
    tr = _tile(rows, 256, SUBLANES)

    def body(w_ref, g_ref, m_ref, v_ref, d_ref, nm_ref, nv_ref):
        d, nm, nv = _adamw_math(w_ref[...], g_ref[...], m_ref[...], v_ref[...])
        d_ref[...] = d
        nm_ref[...] = nm
        nv_ref[...] = nv

    spec = pl.BlockSpec((tr, cols), lambda i: (i, 0))
    shp = jax.ShapeDtypeStruct((rows, cols), F32)
    return _pallas(
        body, name=name, out_shape=[shp] * 3, grid=(rows // tr,), in_specs=[spec] * 4, out_specs=[spec] * 3,
        compiler_params=_params(("parallel",)),
    )(w, g, m, v)


def _adamw_chips(w, parts, m, v, name):
    rows, cols = w.shape
    assert sum(p.shape[1] for p in parts) == cols
    tr = _tile(rows, 64, 16)
    n = len(parts)

    def body(*refs):
        w_ref, m_ref, v_ref = refs[0], refs[1 + n], refs[2 + n]
        g_ref, d_ref, nm_ref, nv_ref = refs[3 + n:]
        cols_g = []
        for p_ref in refs[1:1 + n]:
            acc = p_ref[0].astype(F32)
            for j in range(1, N_CHIPS):
                acc = acc + p_ref[j].astype(F32)
            cols_g.append(acc)
        g = cols_g[0] if n == 1 else jnp.concatenate(cols_g, axis=1)
        d, nm, nv = _adamw_math(w_ref[...], g, m_ref[...], v_ref[...])
        g_ref[...] = g
        d_ref[...] = d
        nm_ref[...] = nm
        nv_ref[...] = nv

    spec = pl.BlockSpec((tr, cols), lambda i: (i, 0))
    part_specs = [pl.BlockSpec((N_CHIPS, tr, p.shape[1]), lambda i: (0, i, 0)) for p in parts]
    shp = jax.ShapeDtypeStruct((rows, cols), F32)
    return _pallas(
        body, name=name, out_shape=[shp] * 4, grid=(rows // tr,),
        in_specs=[spec] + part_specs + [spec, spec], out_specs=[spec] * 4,
        compiler_params=_params(("parallel",)),
    )(w, *[p.reshape(N_CHIPS, rows, p.shape[1]) for p in parts], m, v)


def _adamw_small(w, parts, m, v, name):
    rows, cols = w.shape
    p3 = parts.reshape(N_DEV, rows, cols)

    def body(w_ref, p_ref, m_ref, v_ref, g_ref, d_ref, nm_ref, nv_ref):
        g = p_ref[0]
        for k in range(1, N_DEV):
            g = g + p_ref[k]
        d, nm, nv = _adamw_math(w_ref[...], g, m_ref[...], v_ref[...])
        g_ref[...] = g
        d_ref[...] = d
        nm_ref[...] = nm
        nv_ref[...] = nv

    shp = jax.ShapeDtypeStruct((rows, cols), F32)
    return _pallas(body, name=name, out_shape=[shp] * 4)(w, p3, m, v)


SMALL = ("norm_w", "q_norm_w", "k_norm_w", "sinks", "A_re", "A_im", "log_dt", "B_re", "B_im", "C_re", "C_im",
         "D_skip", "b_glu")
LARGE = ("w_in", "w_attn_proj", "w_glu", "w_ssm_proj", "w_out")
ORDER = ("norm_w", "w_in", "q_norm_w", "k_norm_w", "sinks", "w_attn_proj", "A_re", "A_im", "log_dt", "B_re", "B_im",
         "C_re", "C_im", "D_skip", "w_glu", "b_glu", "w_ssm_proj", "w_out")


SMALL_REST = ("loss",) + SMALL[1:]


def _pack(named, keys):
    flat = jnp.concatenate([named[k].reshape(-1).astype(F32) for k in keys])
    n = flat.shape[0]
    rows = -(-n // (LANES * SUBLANES)) * SUBLANES
    return jnp.pad(flat, (0, rows * LANES - n)).reshape(rows, LANES)


def _unpack(packed, like, keys):
    flat = packed.reshape(-1)
    out, o = {}, 0
    for k in keys:
        n = like[k].size
        out[k] = flat[o:o + n].reshape(like[k].shape)
        o += n
    return out


def _step(xs, target, p, shards):
    s_in, s_ap, s_glu, s_sp, s_o = shards
    seq, d = xs.shape
    attn_w = (d // 128) * HEAD_DIM
    n_q = attn_w // HEAD_DIM
    kv_w = N_KV_HEADS * HEAD_DIM
    ssm_w = d // 2
    n_groups = ssm_w // GROUP
    n_state = n_groups * STATE
    in_w = N_DEV * s_in.shape[0]
    assert in_w == 2 * attn_w + 2 * kv_w + 2 * ssm_w + 2 * d
    o_u = 2 * attn_w + 2 * kv_w
    o_z = o_u + ssm_w
    o_ga = o_z + ssm_w
    chunk = min(SSM_CHUNK, seq)
    cw = d // 4

    norm_row = p["norm_w"].reshape(1, d)
    half = d // W_IN_PARTS
    assert W_IN_PARTS == 2
    s_in_parts = [s_in[:, :half], s_in[:, half:]]
    h, (w_lo,) = _rmsnorm_fwd(xs, norm_row, "rmsnorm_fwd", rider=_all_gather(s_in_parts[:1]))
    part, (w_hi,) = _matmul(Cols(h, 0, half), w_lo, mode="nt", name="in_proj_0", tn=2176, out_dtype=BF16,
                            rider=_all_gather(s_in_parts[1:]))
    proj = _matmul(Cols(h, half, half), w_hi, mode="nt", name="in_proj_1", tn=2176, out_dtype=BF16, add=part)
    w_in_parts = [w_lo, w_hi]
    qw_row = jnp.tile(p["q_norm_w"], n_q).reshape(1, attn_w)
    kw_row = jnp.tile(p["k_norm_w"], N_KV_HEADS).reshape(1, kv_w)
    gmat = _head_mean_matrix()
    ag = _attention_fwd(proj, qw_row, kw_row, gmat, p["sinks"], attn_w=attn_w, kv_w=kv_w, name="attention_fwd")

    log_dt_col = p["log_dt"].reshape(n_groups, 1)
    prep = _ssm_prep(p["A_re"], p["A_im"], log_dt_col, chunk // SUBLANES, "ssm_prep")
    rows_p = [v.reshape(1, n_state) for v in prep[:6]] + [v.reshape(-1, n_state) for v in prep[6:]]
    bt_re, bt_im = p["B_re"].transpose(0, 2, 1), p["B_im"].transpose(0, 2, 1)
    cf_re, cf_im = prep[2][:, None, :], prep[3][:, None, :]
    bc_rows = [_ssm_rows(m) for m in (bt_re, bt_im, p["C_re"], p["C_im"],
                                      cf_re * bt_re - cf_im * bt_im, cf_re * bt_im + cf_im * bt_re)]
    d_row = p["D_skip"].reshape(1, ssm_w)
    (y_ssm, st_re, st_im, yg), (w_ap_t, w_glu_t, w_sp_t, w_o) = _ssm_fwd(
        proj, o_u, bc_rows[:4], rows_p, d_row, chunk=chunk, name="ssm_fwd",
        rider=_all_gather([s_ap, s_glu, s_sp, s_o]))
    glu = _matmul(yg, w_glu_t, mode="nt", name="glu_proj", out_dtype=BF16, bias=p["b_glu"].reshape(1, 2 * ssm_w))
    (ts,) = _ew(lambda ga, gb, z: ga * _sigmoid(gb) * _silu(z), name="glu_gate", rows=seq, width=ssm_w,
                tiles=[(glu, 0), (glu, ssm_w), (proj, o_z)], outs=[(BF16, ssm_w, 0)], cw=cw)
    yy = _matmul(ag, w_ap_t, mode="nt", name="attn_proj", out_dtype=BF16, out_cols=(2 * d, 0))
    yy = _matmul(ts, w_sp_t, mode="nt", name="ssm_proj", out_dtype=BF16, out_cols=(2 * d, d), into=yy)
    (merged,) = _ew(lambda ya, ys, ga, gs: _sigmoid(ga) * ya + _sigmoid(gs) * ys, name="merge", rows=seq, width=d,
                    tiles=[(yy, 0), (yy, d), (proj, o_ga), (proj, o_ga + d)], outs=[(BF16, d, 0)], cw=cw)
    dout, dout_b, loss_cols = _out_proj_loss(merged, w_o, xs, target, "out_proj_loss")
    loss_local = jnp.sum(loss_cols)

    g_w_o = _matmul(merged, dout_b, mode="tn", name="grad_w_out", tm=512, tk=4096)
    dmerged = _matmul(dout_b, w_o, mode="nt", name="d_merged", out_dtype=BF16)

    def merge_bwd(dm, y, g):
        s = _sigmoid(g)
        return dm * s, dm * y * s * (1.0 - s)

    dyy, dproj = _ew(merge_bwd, name="merge_bwd", rows=seq, width=2 * d,
                     tiles=[(dmerged, 0, d), (yy, 0), (proj, o_ga)],
                     outs=[(BF16, 2 * d, 0), (BF16, in_w, o_ga)], cw=cw)
    dy_a, dy_s = Cols(dyy, 0, d), Cols(dyy, d, d)
    g_w_ap_t = _matmul(dy_a, ag, mode="tn", name="grad_w_attn_proj", tm=512, tk=4096)
    g_w_sp_t = _matmul(dy_s, ts, mode="tn", name="grad_w_ssm_proj", tm=512, tk=4096)
    d_ag = _matmul(dy_a, w_ap_t, mode="nn", name="d_attn_gated", out_dtype=BF16)
    d_ts = _matmul(dy_s, w_sp_t, mode="nn", name="d_ssm_gated", out_dtype=BF16)

    (dproj, g_qw, g_kw, g_sinks), (sib_o, sib_ap, sib_sp) = _attention_bwd(
        proj, d_ag, dproj, qw_row, kw_row, gmat, p["sinks"], attn_w=attn_w, kv_w=kv_w, name="attention_bwd",
        rider=_sibling_exchange([g_w_o, g_w_ap_t, g_w_sp_t]))
    pair_o = _pair_sum(g_w_o, sib_o, "pair_sum_w_out")
    pair_ap = _pair_sum(g_w_ap_t, sib_ap, "pair_sum_w_attn_proj")
    pair_sp = _pair_sum(g_w_sp_t, sib_sp, "pair_sum_w_ssm_proj")

    n_half = ssm_w // _tile(2 * ssm_w, cw)

    def glu_bwd(j, dt, ga, gb, z):
        sb, sz = _sigmoid(gb), _silu(z)
        dg = jnp.where(j < n_half, dt * sb * sz, dt * ga * sb * (1.0 - sb) * sz)
        return dg, dg

    glu_ops = [(d_ts, 0, ssm_w), (glu, 0, ssm_w), (glu, ssm_w, ssm_w), (proj, o_z, ssm_w)]
    dglu, g_bglu = _ew(glu_bwd, name="glu_bwd", rows=seq, width=2 * ssm_w, tiles=glu_ops,
                       outs=[(BF16, 2 * ssm_w, 0)], accs=1, cw=cw, with_col=True)
    (dproj,) = _ew(lambda dt, ga, gb, z: dt * ga * _sigmoid(gb) * _dsilu(z), name="glu_bwd_z", rows=seq,
                   width=ssm_w, tiles=glu_ops, outs=[(BF16, in_w, o_z)], into=[dproj], cw=cw)
    g_w_glu_t = _matmul(dglu, yg, mode="tn", name="grad_w_glu", tm=512, tk=4096)
    d_yg = _matmul(dglu, w_glu_t, mode="nn", name="d_gelu", out_dtype=BF16)
    ((du, dbt_re, dbt_im, dc_re, dc_im, gabr, gabi, g_d), (chips_o, chips_ap, chips_sp, sib_glu)) = _ssm_bwd(
        proj, o_u, y_ssm, d_yg, st_re, st_im, bc_rows, rows_p, d_row, chunk=chunk, name="ssm_bwd",
        rider=_join(_chip_exchange([pair_o, pair_ap, pair_sp]), _sibling_exchange([g_w_glu_t])))
    pair_glu = _pair_sum(g_w_glu_t, sib_glu, "pair_sum_w_glu")
    (dproj,) = _ew(lambda v: v, name="du_store", rows=seq, width=ssm_w, tiles=[(du, 0)],
                   outs=[(BF16, in_w, o_u)], into=[dproj], cw=cw)
    g_a_re, g_a_im, g_log_dt, g_bt_re, g_bt_im = _ssm_param_bwd(
        p["A_re"], p["A_im"], log_dt_col, *[g.reshape(SUBLANES, n_groups, STATE) for g in (gabr, gabi)],
        bt_re, bt_im, _from_ssm_rows(dbt_re), _from_ssm_rows(dbt_im), "ssm_param_bwd")
    small_grads = dict(
        loss=loss_local, q_norm_w=g_qw.reshape(n_q, HEAD_DIM).sum(0), k_norm_w=g_kw.reshape(N_KV_HEADS, HEAD_DIM).sum(0),
        sinks=g_sinks[0, :n_q], A_re=g_a_re, A_im=g_a_im, log_dt=g_log_dt.reshape(n_groups),
        B_re=g_bt_re.transpose(0, 2, 1), B_im=g_bt_im.transpose(0, 2, 1),
        C_re=_from_ssm_rows(dc_re), C_im=_from_ssm_rows(dc_im),
        D_skip=g_d.reshape(n_groups, GROUP), b_glu=g_bglu.reshape(2 * ssm_w))

    n_parts = W_IN_PARTS
    wq = d // n_parts
    g_parts, pair_parts, chip_parts = [], [], []
    extra = [_chip_exchange([pair_glu]), _all_gather([_pack(small_grads, SMALL_REST)])]
    chips_glu = small_parts = dh = None
    for step in range(n_parts + 2):
        riders = list(extra) if step == 0 else []
        if 0 <= step - 2 < n_parts:
            riders.append(_chip_exchange([pair_parts[step - 2]]))
        if 0 <= step - 1 < n_parts:
            riders.append(_sibling_exchange([g_parts[step - 1]]))
        rider = _join(*riders) if riders else None
        if step < n_parts:
            res = _matmul(dproj, Cols(h, step * wq, wq), mode="tn", name="grad_w_in_%d" % step, tk=4096, rider=rider)
            out, landed = res if rider is not None else (res, [])
            g_parts.append(out)
        else:
            q = step - n_parts
            dh, landed = _matmul(dproj, w_in_parts[q], mode="nn", name="d_normed_%d" % q, tk=2176,
                                 out_cols=(d, q * wq), into=dh, rider=rider)
        landed = list(landed)
        if step == 0:
            chips_glu, small_parts = landed[:2]
            landed = landed[2:]
        if 0 <= step - 2 < n_parts:
            chip_parts.append(landed.pop(0))
        if 0 <= step - 1 < n_parts:
            pair_parts.append(_pair_sum(g_parts[step - 1], landed.pop(0), "pair_sum_w_in_%d" % (step - 1)))
    grad_x, g_norm = _rmsnorm_bwd(xs, norm_row, dh, dout, "rmsnorm_bwd")
    (norm_parts,) = _exchange(_all_gather([_pack(dict(norm_w=g_norm), ("norm_w",))]), "gather_norm_grad")
    from_chips = dict(zip(LARGE, (chip_parts, [chips_ap], [chips_glu], [chips_sp], [chips_o])))
    return grad_x, from_chips, small_parts, norm_parts


def kernel(x, norm_w, w_in, q_norm_w, k_norm_w, sinks, w_attn_proj, A_re, A_im, log_dt, B_re, B_im, C_re, C_im, D_skip, w_glu, b_glu, w_ssm_proj, w_out, loss_target, m_norm_w, m_w_in, m_q_norm_w, m_k_norm_w, m_sinks, m_w_attn_proj, m_A_re, m_A_im, m_log_dt, m_B_re, m_B_im, m_C_re, m_C_im, m_D_skip, m_w_glu, m_b_glu, m_w_ssm_proj, m_w_out, v_norm_w, v_w_in, v_q_norm_w, v_k_norm_w, v_sinks, v_w_attn_proj, v_A_re, v_A_im, v_log_dt, v_B_re, v_B_im, v_C_re, v_C_im, v_D_skip, v_w_glu, v_b_glu, v_w_ssm_proj, v_w_out):
    weights = dict(norm_w=norm_w, w_in=w_in, q_norm_w=q_norm_w, k_norm_w=k_norm_w, sinks=sinks,
                   w_attn_proj=w_attn_proj, A_re=A_re, A_im=A_im, log_dt=log_dt, B_re=B_re, B_im=B_im, C_re=C_re,
                   C_im=C_im, D_skip=D_skip, w_glu=w_glu, b_glu=b_glu, w_ssm_proj=w_ssm_proj, w_out=w_out)
    m_in = dict(norm_w=m_norm_w, w_in=m_w_in, q_norm_w=m_q_norm_w, k_norm_w=m_k_norm_w, sinks=m_sinks,
                w_attn_proj=m_w_attn_proj, A_re=m_A_re, A_im=m_A_im, log_dt=m_log_dt, B_re=m_B_re, B_im=m_B_im,
                C_re=m_C_re, C_im=m_C_im, D_skip=m_D_skip, w_glu=m_w_glu, b_glu=m_b_glu, w_ssm_proj=m_w_ssm_proj,
                w_out=m_w_out)
    v_in = dict(norm_w=v_norm_w, w_in=v_w_in, q_norm_w=v_q_norm_w, k_norm_w=v_k_norm_w, sinks=v_sinks,
                w_attn_proj=v_w_attn_proj, A_re=v_A_re, A_im=v_A_im, log_dt=v_log_dt, B_re=v_B_re, B_im=v_B_im,
                C_re=v_C_re, C_im=v_C_im, D_skip=v_D_skip, w_glu=v_w_glu, b_glu=v_b_glu, w_ssm_proj=v_w_ssm_proj,
                w_out=v_w_out)

    _, seq, d = x.shape
    column_sharded = LARGE[:4]
    as_rows = lambda k, a: a.T if k in column_sharded else a
    shards = [as_rows(k, weights[k]).astype(BF16) for k in LARGE]
    small = {k: weights[k] for k in SMALL}
    grad_x, from_chips, small_parts, norm_parts = _step(x.reshape(seq, d), loss_target.reshape(seq, d), small,
                                                        shards)

    grads, delta, new_m, new_v = {}, {}, {}, {}
    for k in LARGE:
        if k == "w_in":
            res = _adamw_chips(weights[k].T, from_chips[k], m_in[k].T, v_in[k].T, "adamw_" + k)
            grads[k], delta[k], new_m[k], new_v[k] = [a.T for a in res]
        elif k == "w_out":
            grads[k], delta[k], new_m[k], new_v[k] = _adamw_chips(weights[k], from_chips[k], m_in[k], v_in[k],
                                                                  "adamw_" + k)
        else:
            grads[k] = _chip_sum(from_chips[k][0], "chip_sum_" + k).T
            delta[k], new_m[k], new_v[k] = _adamw(weights[k], grads[k], m_in[k], v_in[k], "adamw_" + k)

    zero = jnp.zeros((), F32)
    for keys, parts in ((SMALL_REST, small_parts), (("norm_w",), norm_parts)):
        like = dict(small, loss=zero)
        packs = [_pack(dict(src, loss=zero), keys) for src in (weights, m_in, v_in)]
        res = _adamw_small(packs[0], parts, packs[1], packs[2], "adamw_small_%d" % len(keys))
        for dst, r in zip((grads, delta, new_m, new_v), res):
            dst.update(_unpack(r, like, keys))
    loss = grads["loss"]

    return (loss, grad_x.reshape(x.shape), *[grads[k] for k in ORDER], *[delta[k] for k in ORDER],
            *[new_m[k] for k in ORDER], *[new_v[k] for k in ORDER])
```

```python
import math
from typing import Callable, NamedTuple

import jax
import jax.numpy as jnp
import numpy as np
from jax import lax
from jax.experimental import pallas as pl
from jax.experimental.pallas import tpu as pltpu

F32 = jnp.float32
BF16 = jnp.bfloat16
MESH = pl.DeviceIdType.MESH

HEAD_DIM = 64
N_KV_HEADS = 4
GROUP = 16
STATE = 64
BLOCK = 128
NORM_EPS = 1e-6
N_DEV = 8
N_CHIPS = 4
LANES = 128
SUBLANES = 8
MXU_DIM = 256
VMEM_BYTES = 64 * 1024 * 1024
VMEM_CAP = VMEM_BYTES - 8 * 1024 * 1024

ADAM_LR = 0.001
ADAM_B1 = 0.9
ADAM_B2 = 0.999
ADAM_EPS = 1e-08
ADAM_WD = 0.01
ADAM_STEP = 10

GELU_C = math.sqrt(2.0 / math.pi)
GELU_K = 0.044715


def _tile(dim, pref, mult=LANES):
    if dim <= pref:
        return dim
    best = None
    for d in range(mult, pref + 1, mult):
        if dim % d == 0:
            best = d
    assert best is not None, (dim, pref, mult)
    return best


def _params(semantics=None, vmem=None):
    kw = {}
    if semantics is not None:
        kw["dimension_semantics"] = semantics
    if vmem is not None:
        kw["vmem_limit_bytes"] = int(min(VMEM_CAP, max(vmem, 32 * 1024 * 1024)))
    return pltpu.CompilerParams(**kw)


def _nbytes(shape, dtype):
    return math.prod(shape) * jnp.dtype(dtype).itemsize


def _sigmoid(x):
    return 1.0 / (1.0 + jnp.exp(-x))


def _silu(x):
    return x * _sigmoid(x)


def _dsilu(x):
    s = _sigmoid(x)
    return s * (1.0 + x * (1.0 - s))


def _gelu(x):
    return 0.5 * x * (1.0 + jnp.tanh(GELU_C * (x + GELU_K * x * x * x)))


def _dgelu(x):
    t = jnp.tanh(GELU_C * (x + GELU_K * x * x * x))
    return 0.5 * (1.0 + t) + 0.5 * x * (1.0 - t * t) * GELU_C * (1.0 + 3.0 * GELU_K * x * x)


def _dot(a, b, dims):
    return lax.dot_general(a, b, (dims, ((), ())), preferred_element_type=F32)


NN = ((1,), (0,))
NT = ((1,), (1,))
TN = ((0,), (0,))


def _any_spec():
    return pl.BlockSpec(memory_space=pl.ANY)


def _pallas(body, **kw):
    pin = lambda s: pltpu.HBM(s.shape, s.dtype) if isinstance(s, jax.ShapeDtypeStruct) else s
    out_shape = kw.pop("out_shape")
    out_shape = [pin(s) for s in out_shape] if isinstance(out_shape, (list, tuple)) else pin(out_shape)
    call = pl.pallas_call(body, out_shape=out_shape, **kw)

    def run(*operands):
        pinned = [pltpu.with_memory_space_constraint(o, pltpu.HBM) if jnp.issubdtype(o.dtype, jnp.floating) else o
                  for o in operands]
        return call(*pinned)

    return run


class Rider(NamedTuple):
    operands: tuple
    out_shapes: tuple
    sems: tuple
    start: Callable
    finish: Callable


def _all_gather(shards):
    n = len(shards)

    def copies(ins, outs, sems):
        send_sems, recv_sems, local_sems = sems
        x, y, c = lax.axis_index("x"), lax.axis_index("y"), lax.axis_index("c")
        me, sibling = (x, y, c), (x, y, 1 - c)
        chips = [(1 - x, y), (x, 1 - y), (1 - x, 1 - y)]

        def rows(k, px, py, pc):
            r = shards[k].shape[0]
            return outs[k].at[pl.ds((4 * px + 2 * py + pc) * r, r), :]

        def copy(k, s, block, to, src=None):
            return pltpu.make_async_remote_copy(
                src_ref=rows(k, *block) if src is None else src, dst_ref=rows(k, *block),
                send_sem=send_sems.at[7 * k + s], recv_sem=recv_sems.at[7 * k + s],
                device_id=to, device_id_type=MESH)

        mine = [pltpu.make_async_copy(ins[k], rows(k, *me), local_sems.at[k]) for k in range(n)]
        first = []
        for k in range(n):
            first.append(copy(k, 0, me, sibling, src=ins[k]))
            first += [copy(k, 1 + j, me, (*chip, c), src=ins[k]) for j, chip in enumerate(chips)]
        return me, sibling, chips, c, copy, mine, first

    def start(ins, outs, sems):
        *_, mine, first = copies(ins, outs, sems)
        for cp in mine + first:
            cp.start()

    def finish(ins, outs, sems):
        me, sibling, chips, c, copy, mine, first = copies(ins, outs, sems)
        passed = []
        for j, chip in enumerate(chips):
            for k in range(n):
                copy(k, 1 + j, (*chip, c), me).wait_recv()
                fwd = copy(k, 4 + j, (*chip, c), sibling)
                fwd.start()
                passed.append(fwd)
        for k in range(n):
            copy(k, 0, sibling, me).wait_recv()
            for j, chip in enumerate(chips):
                copy(k, 4 + j, (*chip, 1 - c), me).wait_recv()
        for cp in first + passed:
            cp.wait_send()
        for cp in mine:
            cp.wait()

    return Rider(
        tuple(shards),
        tuple(jax.ShapeDtypeStruct((N_DEV * s.shape[0], s.shape[1]), s.dtype) for s in shards),
        (pltpu.SemaphoreType.DMA((7 * n,)), pltpu.SemaphoreType.DMA((7 * n,)), pltpu.SemaphoreType.DMA((n,))),
        start, finish)


def _sibling_exchange(grads):
    n = len(grads)

    def copies(ins, outs, sems):
        send_sems, recv_sems = sems
        x, y, c = lax.axis_index("x"), lax.axis_index("y"), lax.axis_index("c")
        out = []
        for k in range(n):
            r = grads[k].shape[0] // N_DEV
            for j in range(N_CHIPS):
                out.append(pltpu.make_async_remote_copy(
                    src_ref=ins[k].at[pl.ds((2 * j + 1 - c) * r, r), :],
                    dst_ref=outs[k].at[pl.ds(j * r, r), :],
                    send_sem=send_sems.at[N_CHIPS * k + j], recv_sem=recv_sems.at[N_CHIPS * k + j],
                    device_id=(x, y, 1 - c), device_id_type=MESH))
        return out

    def start(ins, outs, sems):
        for cp in copies(ins, outs, sems):
            cp.start()

    def finish(ins, outs, sems):
        for cp in copies(ins, outs, sems):
            cp.wait()

    return Rider(
        tuple(grads), tuple(jax.ShapeDtypeStruct((g.shape[0] // 2, g.shape[1]), g.dtype) for g in grads),
        (pltpu.SemaphoreType.DMA((N_CHIPS * n,)), pltpu.SemaphoreType.DMA((N_CHIPS * n,))), start, finish)


def _chip_exchange(parts):
    n = len(parts)

    def copies(ins, outs, sems):
        send_sems, recv_sems, local_sems = sems
        x, y, c = lax.axis_index("x"), lax.axis_index("y"), lax.axis_index("c")
        my_chip = 2 * x + y
        chips = [(1 - x, y), (x, 1 - y), (1 - x, 1 - y)]
        local, sent = [], []
        for k in range(n):
            r = parts[k].shape[0] // N_CHIPS
            mine = pl.ds(my_chip * r, r)
            local.append(pltpu.make_async_copy(ins[k].at[mine, :], outs[k].at[mine, :], local_sems.at[k]))
            for s, (px, py) in enumerate(chips):
                sent.append(pltpu.make_async_remote_copy(
                    src_ref=ins[k].at[pl.ds((2 * px + py) * r, r), :], dst_ref=outs[k].at[mine, :],
                    send_sem=send_sems.at[3 * k + s], recv_sem=recv_sems.at[3 * k + s],
                    device_id=(px, py, c), device_id_type=MESH))
        return local, sent

    def start(ins, outs, sems):
        local, sent = copies(ins, outs, sems)
        for cp in local + sent:
            cp.start()

    def finish(ins, outs, sems):
        local, sent = copies(ins, outs, sems)
        for cp in sent + local:
            cp.wait()

    return Rider(
        tuple(parts), tuple(jax.ShapeDtypeStruct(p.shape, p.dtype) for p in parts),
        (pltpu.SemaphoreType.DMA((3 * n,)), pltpu.SemaphoreType.DMA((3 * n,)), pltpu.SemaphoreType.DMA((n,))),
        start, finish)


def _join(*riders):
    cuts_in, cuts_out, cuts_sem = [0], [0], [0]
    for r in riders:
        cuts_in.append(cuts_in[-1] + len(r.operands))
        cuts_out.append(cuts_out[-1] + len(r.out_shapes))
        cuts_sem.append(cuts_sem[-1] + len(r.sems))

    def each(which):
        def run(ins, outs, sems):
            for i, r in enumerate(riders):
                getattr(r, which)(ins[cuts_in[i]:cuts_in[i + 1]], outs[cuts_out[i]:cuts_out[i + 1]],
                                  sems[cuts_sem[i]:cuts_sem[i + 1]])
        return run

    return Rider(sum((r.operands for r in riders), ()), sum((r.out_shapes for r in riders), ()),
                 sum((r.sems for r in riders), ()), each("start"), each("finish"))


def _call(body, operands, *, name, out_shape, grid, in_specs, out_specs, scratch_shapes=(), aliases=None,
          semantics=None, vmem=None, rider=None):
    operands, out_shape, scratch_shapes = list(operands), list(out_shape), list(scratch_shapes)
    in_specs, out_specs = list(in_specs), list(out_specs)
    if rider is None:
        res = _pallas(
            body, name=name, out_shape=out_shape, grid=grid, in_specs=in_specs, out_specs=out_specs,
            scratch_shapes=scratch_shapes, input_output_aliases=aliases or {},
            compiler_params=_params(semantics, vmem))(*operands)
        return list(res), []
    n_in, n_out, n_scr = len(operands), len(out_shape), len(scratch_shapes)
    ri, ro = len(rider.operands), len(rider.out_shapes)

    def carried(*refs):
        a, b = n_in, n_in + ri
        c, d = b + n_out, b + n_out + ro
        e = d + n_scr
        ids = [pl.program_id(k) for k in range(len(grid))]
        first = ids[0] == 0
        last = ids[0] == grid[0] - 1
        for k in range(1, len(grid)):
            first = jnp.logical_and(first, ids[k] == 0)
            last = jnp.logical_and(last, ids[k] == grid[k] - 1)

        @pl.when(first)
        def _():
            rider.start(refs[a:b], refs[c:d], refs[e:])

        body(*refs[:a], *refs[b:c], *refs[d:e])

        @pl.when(last)
        def _():
            rider.finish(refs[a:b], refs[c:d], refs[e:])

    res = _pallas(
        carried, name=name, out_shape=out_shape + list(rider.out_shapes), grid=grid,
        in_specs=in_specs + [_any_spec()] * ri, out_specs=out_specs + [_any_spec()] * ro,
        scratch_shapes=scratch_shapes + list(rider.sems), input_output_aliases=aliases or {},
        compiler_params=_params(("arbitrary",) * len(grid), vmem))(*operands, *rider.operands)
    return list(res[:n_out]), list(res[n_out:])


def _exchange(rider, name):
    ri, ro = len(rider.operands), len(rider.out_shapes)

    def body(*refs):
        rider.start(refs[:ri], refs[ri:ri + ro], refs[ri + ro:])
        rider.finish(refs[:ri], refs[ri:ri + ro], refs[ri + ro:])

    return _pallas(
        body, name=name, out_shape=list(rider.out_shapes), in_specs=[_any_spec()] * ri,
        out_specs=[_any_spec()] * ro, scratch_shapes=list(rider.sems))(*rider.operands)


class Cols(NamedTuple):
    arr: jax.Array
    off: int
    width: int


def _cols(a):
    return a if isinstance(a, Cols) else Cols(a, 0, a.shape[1])


def _matmul(a, b, *, mode, name, out_dtype=F32, tm=1024, tn=1024, tk=2048, bias=None, add=None, out_cols=None,
            into=None, rider=None):
    a, b = _cols(a), _cols(b)
    if mode == "nn":
        (m, k), (k2, n) = (a.arr.shape[0], a.width), (b.arr.shape[0], b.width)
    elif mode == "nt":
        (m, k), (n, k2) = (a.arr.shape[0], a.width), (b.arr.shape[0], b.width)
    else:
        (k, m), (k2, n) = (a.arr.shape[0], a.width), (b.arr.shape[0], b.width)
    assert k == k2, (a.arr.shape, b.arr.shape, mode)
    tm, tn, tk = _tile(m, tm), _tile(n, tn), _tile(k, tk)
    nk = k // tk
    dims = {"nn": NN, "nt": NT, "tn": TN}[mode]
    if mode == "tn":
        assert a.off % tm == 0
        a_spec = pl.BlockSpec((tk, tm), lambda i, j, kk, o=a.off // tm: (kk, i + o))
    else:
        assert a.off % tk == 0
        a_spec = pl.BlockSpec((tm, tk), lambda i, j, kk, o=a.off // tk: (i, kk + o))
    if mode == "nt":
        assert b.off % tk == 0
        b_spec = pl.BlockSpec((tn, tk), lambda i, j, kk, o=b.off // tk: (j, kk + o))
    else:
        assert b.off % tn == 0
        b_spec = pl.BlockSpec((tk, tn), lambda i, j, kk, o=b.off // tn: (kk, j + o))
    in_specs, operands = [a_spec, b_spec], [a.arr, b.arr]
    assert bias is None or add is None
    if bias is not None:
        in_specs.append(pl.BlockSpec((1, tn), lambda i, j, kk: (0, j)))
        operands.append(bias)
    if add is not None:
        assert add.shape == (m, n)
        in_specs.append(pl.BlockSpec((tm, tn), lambda i, j, kk: (i, j)))
        operands.append(add)
    total_w, o_off = out_cols if out_cols is not None else (n, 0)
    assert o_off % tn == 0
    aliases = {}
    if into is not None:
        assert into.shape == (m, total_w) and into.dtype == out_dtype
        in_specs.append(_any_spec())
        operands.append(into)
        aliases = {len(operands) - 1: 0}
    n_in = len(operands)

    def body(*refs):
        a_ref, b_ref = refs[0], refs[1]
        bias_ref = refs[2] if bias is not None or add is not None else None
        o_ref = refs[n_in]
        acc_ref = refs[-1] if nk > 1 else None
        part = _dot(a_ref[...].astype(BF16), b_ref[...].astype(BF16), dims)

        def finish(acc):
            if bias_ref is not None:
                acc = acc + bias_ref[...]
            o_ref[...] = acc.astype(out_dtype)

        if nk == 1:
            finish(part)
        else:
            kk = pl.program_id(2)

            @pl.when(kk == 0)
            def _():
                acc_ref[...] = part

            @pl.when(kk > 0)
            def _():
                acc_ref[...] += part

            @pl.when(kk == nk - 1)
            def _():
                finish(acc_ref[...])

    vmem = 2 * (_nbytes((tm, tk), a.arr.dtype) + _nbytes((tk, tn), b.arr.dtype) + _nbytes((tm, tn), out_dtype))
    vmem += 3 * _nbytes((tm, tn), F32)
    (out,), landed = _call(
        body, operands, name=name, out_shape=[jax.ShapeDtypeStruct((m, total_w), out_dtype)],
        grid=(m // tm, n // tn, nk), in_specs=in_specs,
        out_specs=[pl.BlockSpec((tm, tn), lambda i, j, kk, o=o_off // tn: (i, j + o))],
        scratch_shapes=[pltpu.VMEM((tm, tn), F32)] if nk > 1 else [], aliases=aliases,
        semantics=("parallel", "parallel", "arbitrary"), vmem=vmem, rider=rider)
    return out if rider is None else (out, landed)


def _ew(fn, *, name, rows, width, tiles, vecs=(), outs, accs=0, tl=1024, cw=512, into=None, with_col=False):
    tl, cw = _tile(rows, tl, SUBLANES), _tile(width, cw)
    ncol = width // cw
    nt_, nv = len(tiles), len(vecs)
    into = list(into) if into is not None else [None] * len(outs)
    aliased = [t for t in into if t is not None]

    def off(o):
        assert o % cw == 0, (name, o, cw)
        return o // cw

    in_specs, vmem = [], 0
    for t in tiles:
        arr, o = t[0], off(t[1])
        wrap = t[2] // cw if len(t) > 2 else ncol
        in_specs.append(pl.BlockSpec((tl, cw), lambda j, i, o=o, wrap=wrap: (i, o + j % wrap)))
        vmem += _nbytes((tl, cw), arr.dtype)
    in_specs += [pl.BlockSpec((1, cw), lambda j, i, o=off(o): (0, j + o)) for _, o in vecs]
    in_specs += [_any_spec() for _ in aliased]
    out_shape, out_specs, aliases = [], [], {}
    n_in = nt_ + nv
    for idx, ((dt, tw, o), tgt) in enumerate(zip(outs, into)):
        out_shape.append(jax.ShapeDtypeStruct((rows, tw), dt))
        out_specs.append(pl.BlockSpec((tl, cw), lambda j, i, o=off(o): (i, j + o)))
        vmem += _nbytes((tl, cw), dt)
        if tgt is not None:
            assert tgt.shape == (rows, tw) and tgt.dtype == dt, (name, tgt.shape, tgt.dtype)
            aliases[n_in + len(aliases)] = idx
    for _ in range(accs):
        out_shape.append(jax.ShapeDtypeStruct((1, width), F32))
        out_specs.append(pl.BlockSpec((1, cw), lambda j, i: (0, j)))
    n_out = len(outs)

    def body(*refs):
        vals = [r[...].astype(F32) for r in refs[:n_in]]
        out_refs = refs[n_in + len(aliased):]
        res = fn(pl.program_id(0), *vals) if with_col else fn(*vals)
        res = res if isinstance(res, (tuple, list)) else (res,)
        assert len(res) == n_out + accs, (name, len(res))
        for r, v in zip(out_refs[:n_out], res[:n_out]):
            r[...] = v.astype(r.dtype)
        first = pl.program_id(1) == 0
        for r, v in zip(out_refs[n_out:], res[n_out:]):
            s = jnp.sum(v, axis=0, keepdims=True)

            @pl.when(first)
            def _(r=r, s=s):
                r[...] = s

            @pl.when(jnp.logical_not(first))
            def _(r=r, s=s):
                r[...] += s

    return _pallas(
        body, name=name, out_shape=out_shape, grid=(ncol, rows // tl),
        in_specs=in_specs, out_specs=out_specs, input_output_aliases=aliases,
        compiler_params=_params(("parallel", "arbitrary"), 3 * vmem),
    )(*[t[0] for t in tiles], *[v for v, _ in vecs], *aliased)


def _rmsnorm_fwd(x, w_row, name, rider=None):
    rows, d = x.shape
    tl = _tile(rows, 512, SUBLANES)

    def body(x_ref, w_ref, h_ref):
        xv = x_ref[...]
        rstd = lax.rsqrt(jnp.mean(xv * xv, axis=-1, keepdims=True) + NORM_EPS)
        h_ref[...] = (xv * rstd * w_ref[...]).astype(BF16)

    (h,), landed = _call(
        body, [x, w_row], name=name, out_shape=[jax.ShapeDtypeStruct((rows, d), BF16)], grid=(rows // tl,),
        in_specs=[pl.BlockSpec((tl, d), lambda i: (i, 0)), pl.BlockSpec((1, d), lambda i: (0, 0))],
        out_specs=[pl.BlockSpec((tl, d), lambda i: (i, 0))], semantics=("parallel",), rider=rider)
    return h if rider is None else (h, landed)


def _rmsnorm_bwd(x, w_row, dh, dout, name, rider=None):
    rows, d = x.shape
    tl = _tile(rows, 256, SUBLANES)

    def body(x_ref, w_ref, dh_ref, dout_ref, gx_ref, gw_ref):
        xv = x_ref[...]
        rstd = lax.rsqrt(jnp.mean(xv * xv, axis=-1, keepdims=True) + NORM_EPS)
        xn = xv * rstd
        dhv = dh_ref[...]
        dxn = dhv * w_ref[...]
        dx = rstd * (dxn - xn * jnp.mean(dxn * xn, axis=-1, keepdims=True))
        gx_ref[...] = dout_ref[...] + dx
        gw = jnp.sum(dhv * xn, axis=0, keepdims=True)

        @pl.when(pl.program_id(0) == 0)
        def _():
            gw_ref[...] = gw

        @pl.when(pl.program_id(0) > 0)
        def _():
            gw_ref[...] += gw

    tile = pl.BlockSpec((tl, d), lambda i: (i, 0))
    row = pl.BlockSpec((1, d), lambda i: (0, 0))
    res, landed = _call(
        body, [x, w_row, dh, dout], name=name,
        out_shape=[jax.ShapeDtypeStruct((rows, d), F32), jax.ShapeDtypeStruct((1, d), F32)],
        grid=(rows // tl,), in_specs=[tile, row, tile, tile], out_specs=[tile, row],
        semantics=("arbitrary",), rider=rider)
    return res if rider is None else (res, landed)


def _head_mean(x, gmat):
    hi = x.astype(BF16)
    lo = (x - hi.astype(F32)).astype(BF16)
    out = []
    for s in range(x.shape[1] // MXU_DIM):
        sl = slice(s * MXU_DIM, (s + 1) * MXU_DIM)
        out.append(_dot(hi[:, sl], gmat, NN) + _dot(lo[:, sl], gmat, NN))
    return out[0] if len(out) == 1 else jnp.concatenate(out, axis=1)


def _head_mean_matrix():
    blk = jnp.arange(MXU_DIM) // HEAD_DIM
    return jnp.where(blk[:, None] == blk[None, :], 1.0 / HEAD_DIM, 0.0).astype(BF16)


def _spread_head(x, g, width):
    col = x[:, (g // 2) * LANES:(g // 2 + 1) * LANES]
    other = pltpu.roll(col, HEAD_DIM, axis=1)
    low = lax.broadcasted_iota(jnp.int32, col.shape, 1) < HEAD_DIM
    both = jnp.where(low, col, other) if g % 2 == 0 else jnp.where(low, other, col)
    return both if width == LANES else jnp.concatenate([both] * (width // LANES), axis=1)


def _head_diagonal(t, per_kv):
    head = lax.broadcasted_iota(jnp.int32, t.shape, 1) // HEAD_DIM
    zero = jnp.zeros_like(t)
    return jnp.concatenate([jnp.where(head == r, t, zero) for r in range(per_kv)], axis=0)


def _fold_heads(x, per_kv):
    rows = x.shape[0] // per_kv
    head = lax.broadcasted_iota(jnp.int32, (rows, x.shape[1]), 1) // HEAD_DIM
    acc = jnp.where(head == 0, x[0:rows], 0.0)
    for r in range(1, per_kv):
        acc = acc + jnp.where(head == r, x[r * rows:(r + 1) * rows], 0.0)
    while acc.shape[1] > LANES:
        half = acc.shape[1] // 2
        acc = acc[:, :half] + acc[:, half:]
    return acc + pltpu.roll(acc, HEAD_DIM, axis=1)


def _join_heads(parts):
    low = lax.broadcasted_iota(jnp.int32, parts[0].shape, 1) < HEAD_DIM
    cols = [jnp.where(low, parts[2 * j], parts[2 * j + 1]) for j in range(len(parts) // 2)]
    return cols[0] if len(cols) == 1 else jnp.concatenate(cols, axis=1)


def _attn_specs(attn_w, kv_w, block=lambda s: s):
    half = attn_w // 2
    kcol, vcol = attn_w // kv_w, attn_w // kv_w + 1
    gcol = (attn_w + 2 * kv_w) // half
    prev = lambda s: jnp.maximum(block(s) - 1, 0)
    return [
        pl.BlockSpec((BLOCK, attn_w), lambda s: (block(s), 0)),
        pl.BlockSpec((BLOCK, kv_w), lambda s: (prev(s), kcol)),
        pl.BlockSpec((BLOCK, kv_w), lambda s: (block(s), kcol)),
        pl.BlockSpec((BLOCK, kv_w), lambda s: (prev(s), vcol)),
        pl.BlockSpec((BLOCK, kv_w), lambda s: (block(s), vcol)),
        pl.BlockSpec((BLOCK, half), lambda s: (block(s), gcol)),
        pl.BlockSpec((BLOCK, half), lambda s: (block(s), gcol + 1)),
    ]


def _band_mask(i):
    q_loc = lax.broadcasted_iota(jnp.int32, (BLOCK, 2 * BLOCK), 0) + BLOCK
    k_loc = lax.broadcasted_iota(jnp.int32, (BLOCK, 2 * BLOCK), 1)
    diff = q_loc - k_loc
    first_key = jnp.where(i == 0, BLOCK, 0)
    return (diff >= 0) & (diff < BLOCK) & (k_loc >= first_key)


def _softmax_with_sink(s, sink):
    m = jnp.maximum(jnp.max(s, axis=-1, keepdims=True), sink)
    p = jnp.exp(s - m)
    e_sink = jnp.exp(sink - m)
    den = jnp.sum(p, axis=-1, keepdims=True) + e_sink
    inv = 1.0 / den
    return p * inv, e_sink * inv


def _attn_block(i, q, kk, vv, qw, kw, gmat, sink_ref, per_kv):
    scale = 1.0 / math.sqrt(HEAD_DIM)
    keys = 2 * BLOCK
    valid = _band_mask(i)
    q_rstd = lax.rsqrt(_head_mean(q * q, gmat) + NORM_EPS)
    qn = q * q_rstd
    qh = (qn * qw).astype(BF16)
    k_rstd = lax.rsqrt(_head_mean(kk * kk, gmat) + NORM_EPS)
    kn = kk * k_rstd
    kh = kn * kw
    gw = per_kv * HEAD_DIM
    groups = []
    for g in range(N_KV_HEADS):
        kd = _head_diagonal(_spread_head(kh, g, gw).astype(BF16), per_kv)
        vd = _head_diagonal(_spread_head(vv, g, gw).astype(BF16), per_kv)
        qg = qh[:, g * gw:(g + 1) * gw]
        s_all = _dot(qg, kd, NT) * scale
        ps, p_sinks = [], []
        for r in range(per_kv):
            s = jnp.where(valid, s_all[:, r * keys:(r + 1) * keys], -1e30)
            p, p_sink = _softmax_with_sink(s, sink_ref[g * per_kv + r])
            ps.append(p)
            p_sinks.append(p_sink)
        pb = jnp.concatenate(ps, axis=1).astype(BF16)
        groups.append((kd, vd, qg, ps, p_sinks, pb, _dot(pb, vd, NN)))
    return qn, q_rstd, kn, k_rstd, groups


def _attention_fwd(proj, qw_row, kw_row, gmat, sinks, *, attn_w, kv_w, name):
    rows = proj.shape[0]
    per_kv = attn_w // HEAD_DIM // N_KV_HEADS

    def body(q_ref, kp_ref, kc_ref, vp_ref, vc_ref, glo_ref, ghi_ref, qw_ref, kw_ref, gm_ref, sink_ref, o_ref):
        kk = jnp.concatenate([kp_ref[...], kc_ref[...]], axis=0).astype(F32)
        vv = jnp.concatenate([vp_ref[...], vc_ref[...]], axis=0).astype(F32)
        gate = jnp.concatenate([glo_ref[...], ghi_ref[...]], axis=1).astype(F32)
        *_, groups = _attn_block(pl.program_id(0), q_ref[...].astype(F32), kk, vv, qw_ref[...], kw_ref[...], gm_ref[...],
                                 sink_ref, per_kv)
        attn = jnp.concatenate([grp[-1] for grp in groups], axis=1)
        o_ref[...] = (attn * _silu(gate)).astype(BF16)

    const = lambda a: pl.BlockSpec(a.shape, lambda i: (0, 0))
    return _pallas(
        body, name=name, out_shape=jax.ShapeDtypeStruct((rows, attn_w), BF16), grid=(rows // BLOCK,),
        in_specs=_attn_specs(attn_w, kv_w) + [const(qw_row), const(kw_row), const(gmat),
                                              pl.BlockSpec(memory_space=pltpu.SMEM)],
        out_specs=pl.BlockSpec((BLOCK, attn_w), lambda i: (i, 0)),
        compiler_params=_params(("parallel",), 40 * 1024 * 1024),
    )(proj, proj, proj, proj, proj, proj, proj, qw_row, kw_row, gmat, sinks)


def _attention_bwd(proj, d_ag, dproj, qw_row, kw_row, gmat, sinks, *, attn_w, kv_w, name, rider=None):
    rows = proj.shape[0]
    nb = rows // BLOCK
    per_kv = attn_w // HEAD_DIM // N_KV_HEADS
    gw = per_kv * HEAD_DIM
    keys = 2 * BLOCK
    scale = 1.0 / math.sqrt(HEAD_DIM)
    w_out = 2 * attn_w + 2 * kv_w
    rev = lambda s: nb - 1 - s

    def body(q_ref, kp_ref, kc_ref, vp_ref, vc_ref, glo_ref, ghi_ref, dag_ref, qw_ref, kw_ref, gm_ref, sink_ref, _,
             dp_ref, gqw_ref, gkw_ref, gs_ref, carry_ref):
        step = pl.program_id(0)
        i = rev(step)
        kk = jnp.concatenate([kp_ref[...], kc_ref[...]], axis=0).astype(F32)
        vv = jnp.concatenate([vp_ref[...], vc_ref[...]], axis=0).astype(F32)
        gate = jnp.concatenate([glo_ref[...], ghi_ref[...]], axis=1).astype(F32)
        d_ag_v = dag_ref[...].astype(F32)
        qw, kw, gmat_v = qw_ref[...], kw_ref[...], gm_ref[...]
        qn, q_rstd, kn, k_rstd, groups = _attn_block(i, q_ref[...].astype(F32), kk, vv, qw, kw, gmat_v, sink_ref,
                                                     per_kv)
        lane = lax.broadcasted_iota(jnp.int32, (SUBLANES, LANES), 1)
        sub = lax.broadcasted_iota(jnp.int32, (SUBLANES, LANES), 0)
        gsink = jnp.zeros((SUBLANES, LANES), F32)
        dq_groups, dgate_groups, dk_heads, dv_heads = [], [], [], []
        for g, (kd, vd, qg, ps, p_sinks, pb, o) in enumerate(groups):
            cs = slice(g * gw, (g + 1) * gw)
            gate_g, d_ag_g = gate[:, cs], d_ag_v[:, cs]
            dgate_groups.append(d_ag_g * o * _dsilu(gate_g))
            do = (d_ag_g * _silu(gate_g)).astype(BF16)
            dp_all = _dot(do, vd, NT)
            dss = []
            for r in range(per_kv):
                p, dp = ps[r], dp_all[:, r * keys:(r + 1) * keys]
                delta = jnp.sum(p * dp, axis=-1, keepdims=True)
                dss.append(p * (dp - delta) * scale)
                gs_h = jnp.sum(-p_sinks[r] * delta, axis=0, keepdims=True)
                gsink = gsink + jnp.where((lane == g * per_kv + r) & (sub == 0), gs_h, 0.0)
            ds = jnp.concatenate(dss, axis=1).astype(BF16)
            dq_groups.append(_dot(ds, kd, NN))
            dk_heads.append(_fold_heads(_dot(ds, qg, TN), per_kv))
            dv_heads.append(_fold_heads(_dot(pb, do, TN), per_kv))
        dqh = jnp.concatenate(dq_groups, axis=1)
        gqw = jnp.sum(dqh * qn, axis=0, keepdims=True)
        dqn = dqh * qw
        dq = q_rstd * (dqn - qn * _head_mean(dqn * qn, gmat_v))
        dkh = _join_heads(dk_heads)
        gkw = jnp.sum(dkh * kn, axis=0, keepdims=True)
        dkn = dkh * kw
        dk = k_rstd * (dkn - kn * _head_mean(dkn * kn, gmat_v))
        dkv = jnp.concatenate([dk, _join_heads(dv_heads)], axis=1)

        @pl.when(step == 0)
        def _():
            carry_ref[...] = jnp.zeros_like(carry_ref)
            gqw_ref[...] = gqw
            gkw_ref[...] = gkw
            gs_ref[...] = gsink

        @pl.when(step > 0)
        def _():
            gqw_ref[...] += gqw
            gkw_ref[...] += gkw
            gs_ref[...] += gsink

        dp_ref[:, 0:attn_w] = dq.astype(BF16)
        dp_ref[:, attn_w:attn_w + 2 * kv_w] = (dkv[BLOCK:2 * BLOCK, :] + carry_ref[...]).astype(BF16)
        dp_ref[:, attn_w + 2 * kv_w:w_out] = jnp.concatenate(dgate_groups, axis=1).astype(BF16)
        carry_ref[...] = dkv[0:BLOCK, :]

    const = lambda a: pl.BlockSpec(a.shape, lambda s: (0, 0))
    res, landed = _call(
        body, [proj, proj, proj, proj, proj, proj, proj, d_ag, qw_row, kw_row, gmat, sinks, dproj], name=name,
        out_shape=[jax.ShapeDtypeStruct(dproj.shape, BF16),
                   jax.ShapeDtypeStruct(qw_row.shape, F32), jax.ShapeDtypeStruct(kw_row.shape, F32),
                   jax.ShapeDtypeStruct((SUBLANES, LANES), F32)],
        grid=(nb,),
        in_specs=_attn_specs(attn_w, kv_w, rev) + [pl.BlockSpec((BLOCK, attn_w), lambda s: (rev(s), 0)),
                                                   const(qw_row), const(kw_row), const(gmat),
                                                   pl.BlockSpec(memory_space=pltpu.SMEM), _any_spec()],
        out_specs=[pl.BlockSpec((BLOCK, w_out), lambda s: (rev(s), 0)),
                   const(qw_row), const(kw_row), pl.BlockSpec((SUBLANES, LANES), lambda s: (0, 0))],
        scratch_shapes=[pltpu.VMEM((BLOCK, 2 * kv_w), F32)],
        aliases={12: 0}, semantics=("arbitrary",), vmem=48 * 1024 * 1024, rider=rider)
    return res if rider is None else (res, landed)


def _cmul(ar, ai, br, bi):
    return ar * br - ai * bi, ar * bi + ai * br


def _ssm_prep(a_re, a_im, log_dt_col, steps, name):
    def body(are_ref, aim_ref, ldt_ref, abr_ref, abi_ref, cfr_ref, cfi_ref, apr_ref, api_ref, pwr_ref, pwi_ref):
        are, aim = are_ref[...], aim_ref[...]
        dt = jnp.exp(ldt_ref[...])
        mag = jnp.exp(dt * are)
        abr = mag * jnp.cos(dt * aim)
        abi = mag * jnp.sin(dt * aim)
        num_re, num_im = abr - 1.0, abi
        den = are * are + aim * aim
        abr_ref[...] = abr
        abi_ref[...] = abi
        cfr_ref[...] = (num_re * are + num_im * aim) / den
        cfi_ref[...] = (num_im * are - num_re * aim) / den
        pr, pi = jnp.ones_like(abr), jnp.zeros_like(abr)
        for k in range(steps):
            pwr_ref[k] = pr
            pwi_ref[k] = pi
            pr, pi = _cmul(pr, pi, abr, abi)
        apr_ref[...] = pr
        api_ref[...] = pi

    shp = jax.ShapeDtypeStruct(a_re.shape, F32)
    pows = jax.ShapeDtypeStruct((steps,) + a_re.shape, F32)
    return _pallas(body, name=name, out_shape=[shp] * 6 + [pows] * 2)(a_re, a_im, log_dt_col)


def _ssm_param_bwd(a_re, a_im, log_dt_col, d_ab_re, d_ab_im, b_re, b_im, dbt_re, dbt_im, name):
    def body(are_ref, aim_ref, ldt_ref, gabr_ref, gabi_ref, br_ref, bi_ref, tr_ref, ti_ref,
             dar_ref, dai_ref, dldt_ref, dbr_ref, dbi_ref):
        are, aim = are_ref[...], aim_ref[...]
        dt = jnp.exp(ldt_ref[...])
        mag = jnp.exp(dt * are)
        abr = mag * jnp.cos(dt * aim)
        abi = mag * jnp.sin(dt * aim)
        den = are * are + aim * aim
        cfr = ((abr - 1.0) * are + abi * aim) / den
        cfi = (abi * are - (abr - 1.0) * aim) / den
        gabr, gabi = jnp.sum(gabr_ref[...], axis=0), jnp.sum(gabi_ref[...], axis=0)
        t_re, t_im = tr_ref[...], ti_ref[...]
        g_r, g_i = _cmul(br_ref[...], -bi_ref[...], t_re, t_im)
        gcfr, gcfi = jnp.sum(g_r, axis=1), jnp.sum(g_i, axis=1)
        dbr, dbi = _cmul(cfr[:, None, :], -cfi[:, None, :], t_re, t_im)
        dbr_ref[...] = dbr
        dbi_ref[...] = dbi
        inv_r, inv_i = are / den, -aim / den
        t_r, t_i = _cmul(inv_r, -inv_i, gcfr, gcfi)
        gabr, gabi = gabr + t_r, gabi + t_i
        q_r, q_i = _cmul(cfr, cfi, inv_r, inv_i)
        da_r, da_i = _cmul(-q_r, q_i, gcfr, gcfi)
        gz_r, gz_i = _cmul(abr, -abi, gabr, gabi)
        dar_ref[...] = da_r + dt * gz_r
        dai_ref[...] = da_i + dt * gz_i
        dldt_ref[...] = dt * jnp.sum(are * gz_r + aim * gz_i, axis=-1, keepdims=True)

    shp = jax.ShapeDtypeStruct(a_re.shape, F32)
    bshp = jax.ShapeDtypeStruct(b_re.shape, F32)
    return _pallas(body, name=name,
                   out_shape=[shp, shp, jax.ShapeDtypeStruct(log_dt_col.shape, F32), bshp, bshp])(
        a_re, a_im, log_dt_col, d_ab_re, d_ab_im, b_re, b_im, dbt_re, dbt_im)


SCAN_LANES = 1024
SSM_CHUNK = 256
SCAN_UNROLL = 8
W_IN_PARTS = 2


def _scan_segments(xr_ref, xi_ref, a_re, a_im, ap_re, ap_im, pw_re, pw_im, carry_re, carry_im, cm_re, cm_im, steps,
                   reverse, base):
    n = xr_ref.shape[1]
    seg_order = range(SUBLANES - 1, -1, -1) if reverse else range(SUBLANES)
    sign = -1.0 if reverse else 1.0
    for c0 in range(0, n, SCAN_LANES):
        ls = slice(c0, c0 + SCAN_LANES)
        gs = slice(base + c0, base + c0 + SCAN_LANES)
        ar = jnp.broadcast_to(a_re[:, gs], (SUBLANES, SCAN_LANES))
        ai = jnp.broadcast_to(a_im[:, gs], (SUBLANES, SCAN_LANES))
        end_r = jnp.zeros((SUBLANES, SCAN_LANES), F32)
        end_i = jnp.zeros((SUBLANES, SCAN_LANES), F32)
        for j in range(steps):
            k = j if reverse else steps - 1 - j
            rws = slice(j * SUBLANES, (j + 1) * SUBLANES)
            tr, ti = _cmul(pw_re[k:k + 1, gs], sign * pw_im[k:k + 1, gs], xr_ref[rws, ls], xi_ref[rws, ls])
            end_r, end_i = end_r + tr, end_i + ti
        cr, ci = carry_re[:, gs], carry_im[:, gs]
        apr, api = ap_re[:, gs], ap_im[:, gs]
        for r in seg_order:
            cm_re[r:r + 1, gs] = cr
            cm_im[r:r + 1, gs] = ci
            tr, ti = _cmul(apr, api, cr, ci)
            cr, ci = end_r[r:r + 1, :] + tr, end_i[r:r + 1, :] + ti
        carry_re[:, gs] = cr
        carry_im[:, gs] = ci

        def run(t, s, ar=ar, ai=ai, ls=ls):
            sr, si = s
            for k in range(SCAN_UNROLL):
                j = steps - 1 - (t * SCAN_UNROLL + k) if reverse else t * SCAN_UNROLL + k
                r0 = pl.multiple_of(j * SUBLANES, SUBLANES)
                sr, si = _cmul(ar, ai, sr, si)
                sr = sr + xr_ref[pl.ds(r0, SUBLANES), ls]
                si = si + xi_ref[pl.ds(r0, SUBLANES), ls]
                xr_ref[pl.ds(r0, SUBLANES), ls] = sr
                xi_ref[pl.ds(r0, SUBLANES), ls] = si
            return sr, si

        assert steps % SCAN_UNROLL == 0
        lax.fori_loop(0, steps // SCAN_UNROLL, run, (cm_re[:, gs], cm_im[:, gs]))


SB_GROUPS = MXU_DIM // GROUP
SB_STATE = SB_GROUPS * STATE


def _ssm_rows(m):
    flat = m.reshape(-1, STATE).astype(F32)
    return jnp.concatenate([flat, flat], axis=1)


def _from_ssm_rows(rows):
    return rows[:, :STATE].reshape(-1, GROUP, STATE)


def _own_group(shape):
    row_g = lax.broadcasted_iota(jnp.int32, shape, 0) // GROUP
    col_g = lax.broadcasted_iota(jnp.int32, shape, 1) // STATE
    return row_g == col_g


def _block_diagonal(rows):
    tiled = jnp.concatenate([rows] * (SB_STATE // LANES), axis=1)
    return jnp.where(_own_group(tiled.shape), tiled, 0.0).astype(BF16)


def _block_rows(acc):
    x = jnp.where(_own_group(acc.shape), acc, 0.0)
    while x.shape[1] > LANES:
        half = x.shape[1] // 2
        x = x[:, :half] + x[:, half:]
    return x + pltpu.roll(x, STATE, axis=1)


def _rows_to_segments(dst, srcs, steps, stage):
    for ref, off in srcs:
        for k in range(ref.shape[1] // LANES):
            stage[off // LANES + k] = ref[:, k * LANES:(k + 1) * LANES].astype(F32)
    for k in range(dst.shape[1] // LANES):
        for j in range(steps):
            dst[j * SUBLANES:(j + 1) * SUBLANES, k * LANES:(k + 1) * LANES] = (
                stage[k, pl.ds(j, SUBLANES, stride=steps), :])


def _segments_to_rows(dst, src, steps, stage):
    for k in range(src.shape[1] // LANES):
        for j in range(steps):
            stage[k, pl.ds(j, SUBLANES, stride=steps), :] = (
                src[j * SUBLANES:(j + 1) * SUBLANES, k * LANES:(k + 1) * LANES])
    for k in range(src.shape[1] // LANES):
        dst[:, k * LANES:(k + 1) * LANES] = stage[k]


def _u_specs(w, o_u, chunk, index):
    half = w // 2
    assert o_u % half == 0
    return [pl.BlockSpec((chunk, half), lambda c, k=k: (index(c), o_u // half + k)) for k in range(2)]


def _ssm_fwd(proj, o_u, bc_rows, rows_p, d_row, *, chunk, name, rider=None):
    rows = proj.shape[0]
    w = d_row.shape[1]
    nc = rows // chunk
    steps = chunk // SUBLANES
    nsb = w // MXU_DIM
    n_state = nsb * SB_STATE

    def body(ulo_ref, uhi_ref, b2r_ref, b2i_ref, c2r_ref, c2i_ref, abr_ref, abi_ref, cfr_ref, cfi_ref, apr_ref,
             api_ref, pwr_ref, pwi_ref, d_ref, y_ref, str_ref, sti_ref, yg_ref, bre_ref, bim_ref, cre_ref, cim_ref,
             useg, yseg, stage, sr, si,
             carry_r, carry_i, cm_r, cm_i):
        @pl.when(pl.program_id(0) == 0)
        def _():
            for src, dst in ((b2r_ref, bre_ref), (b2i_ref, bim_ref), (c2r_ref, cre_ref), (c2i_ref, cim_ref)):
                for sb in range(nsb):
                    dst[sb] = _block_diagonal(src[sb * MXU_DIM:(sb + 1) * MXU_DIM, :])
            carry_r[...] = jnp.zeros_like(carry_r)
            carry_i[...] = jnp.zeros_like(carry_i)

        str_ref[0] = carry_r[...]
        sti_ref[0] = carry_i[...]
        _rows_to_segments(useg, [(ulo_ref, 0), (uhi_ref, w // 2)], steps, stage)
        for sb in range(nsb):
            us = slice(sb * MXU_DIM, (sb + 1) * MXU_DIM)
            ss = slice(sb * SB_STATE, (sb + 1) * SB_STATE)
            ub = useg[:, us].astype(BF16)
            bur = _dot(ub, bre_ref[sb], NN)
            bui = _dot(ub, bim_ref[sb], NN)
            xr, xi = _cmul(cfr_ref[:, ss], cfi_ref[:, ss], bur, bui)
            sr[...] = xr
            si[...] = xi
            _scan_segments(sr, si, abr_ref[...], abi_ref[...], apr_ref[...], api_ref[...], pwr_ref, pwi_ref,
                           carry_r, carry_i, cm_r, cm_i, steps, False, sb * SB_STATE)
            y = _dot(sr[...].astype(BF16), cre_ref[sb], NT) - _dot(si[...].astype(BF16), cim_ref[sb], NT)
            yseg[:, us] = y + d_ref[:, us] * useg[:, us]
        _segments_to_rows(y_ref, yseg, steps, stage)
        yg_ref[...] = _gelu(y_ref[...]).astype(BF16)

    const = lambda a: pl.BlockSpec(a.shape, lambda c: (0,) * a.ndim)
    row_n = pl.BlockSpec((1, n_state), lambda c: (0, 0))
    st = pl.BlockSpec((1, 1, n_state), lambda c: (c, 0, 0))
    held = [pltpu.VMEM((nsb, MXU_DIM, SB_STATE), BF16)] * 4
    vmem = (4 * _nbytes((nsb, MXU_DIM, SB_STATE), BF16) + 4 * _nbytes((chunk, SB_STATE), F32)
            + 12 * _nbytes((chunk, w), F32) + 8 * _nbytes(bc_rows[0].shape, F32))
    res, landed = _call(
        body, [proj, proj, *bc_rows, *rows_p, d_row], name=name,
        out_shape=[jax.ShapeDtypeStruct((rows, w), F32), jax.ShapeDtypeStruct((nc, 1, n_state), F32),
                   jax.ShapeDtypeStruct((nc, 1, n_state), F32), jax.ShapeDtypeStruct((rows, w), BF16)],
        grid=(nc,),
        in_specs=_u_specs(w, o_u, chunk, lambda c: c) + [const(b) for b in bc_rows]
        + [row_n] * 6 + [pl.BlockSpec((steps, n_state), lambda c: (0, 0))] * 2 + [pl.BlockSpec((1, w), lambda c: (0, 0))],
        out_specs=[pl.BlockSpec((chunk, w), lambda c: (c, 0)), st, st, pl.BlockSpec((chunk, w), lambda c: (c, 0))],
        scratch_shapes=held + [pltpu.VMEM((chunk, w), F32), pltpu.VMEM((chunk, w), F32),
                               pltpu.VMEM((w // LANES, chunk, LANES), F32),
                               pltpu.VMEM((chunk, SB_STATE), F32), pltpu.VMEM((chunk, SB_STATE), F32),
                               pltpu.VMEM((1, n_state), F32), pltpu.VMEM((1, n_state), F32),
                               pltpu.VMEM((SUBLANES, n_state), F32), pltpu.VMEM((SUBLANES, n_state), F32)],
        semantics=("arbitrary",), vmem=vmem, rider=rider)
    return res if rider is None else (res, landed)


def _ssm_bwd(proj, o_u, y, dyg, st_re, st_im, bc_rows, rows_p, d_row, *, chunk, name, rider=None):
    rows = proj.shape[0]
    w = d_row.shape[1]
    nc = rows // chunk
    steps = chunk // SUBLANES
    nsb = w // MXU_DIM
    n_state = nsb * SB_STATE

    def body(ulo_ref, uhi_ref, y_ref, dyg_ref, str_ref, sti_ref, b2r_ref, b2i_ref, c2r_ref, c2i_ref, t2r_ref,
             t2i_ref, abr_ref, abi_ref, cfr_ref, cfi_ref, apr_ref, api_ref, pwr_ref, pwi_ref, d_ref,
             du_ref, gb2r_ref, gb2i_ref, gc2r_ref, gc2i_ref, gabr_ref, gabi_ref, dd_ref,
             bre_ref, bim_ref, cre_ref, cim_ref, btr_ref, bti_ref, dbre_ref, dbim_ref, dcre_ref, dcim_ref,
             useg, dyseg, dynat, stage, sr, si, lr, li, carry_r, carry_i, lam_r, lam_i, cm_r, cm_i, cl_r, cl_i):
        first = pl.program_id(0) == 0

        @pl.when(first)
        def _():
            for src, dst in ((b2r_ref, bre_ref), (b2i_ref, bim_ref), (c2r_ref, cre_ref), (c2i_ref, cim_ref),
                             (t2r_ref, btr_ref), (t2i_ref, bti_ref)):
                for sb in range(nsb):
                    dst[sb] = _block_diagonal(src[sb * MXU_DIM:(sb + 1) * MXU_DIM, :])
            lam_r[...] = jnp.zeros_like(lam_r)
            lam_i[...] = jnp.zeros_like(lam_i)
            for ref in (dbre_ref, dbim_ref, dcre_ref, dcim_ref, gabr_ref, gabi_ref, dd_ref):
                ref[...] = jnp.zeros_like(ref)

        dynat[...] = dyg_ref[...].astype(F32) * _dgelu(y_ref[...])
        half = w // 2
        dd_ref[:, :half] += jnp.sum(dynat[:, :half] * ulo_ref[...].astype(F32), axis=0, keepdims=True)
        dd_ref[:, half:] += jnp.sum(dynat[:, half:] * uhi_ref[...].astype(F32), axis=0, keepdims=True)
        _rows_to_segments(useg, [(ulo_ref, 0), (uhi_ref, half)], steps, stage)
        _rows_to_segments(dyseg, [(dynat, 0)], steps, stage)
        dy = dyseg[...]
        dyb = dy.astype(BF16)
        ub = useg[...].astype(BF16)
        carry_r[...] = str_ref[0]
        carry_i[...] = sti_ref[0]
        abr, abi = abr_ref[...], abi_ref[...]
        apr, api = apr_ref[...], api_ref[...]
        for sb in range(nsb):
            us = slice(sb * MXU_DIM, (sb + 1) * MXU_DIM)
            ss = slice(sb * SB_STATE, (sb + 1) * SB_STATE)
            base = sb * SB_STATE
            br = _dot(ub[:, us], bre_ref[sb], NN)
            bi = _dot(ub[:, us], bim_ref[sb], NN)
            xr, xi = _cmul(cfr_ref[:, ss], cfi_ref[:, ss], br, bi)
            sr[...] = xr
            si[...] = xi
            lr[...] = _dot(dyb[:, us], cre_ref[sb], NN)
            li[...] = -_dot(dyb[:, us], cim_ref[sb], NN)
            _scan_segments(sr, si, abr, abi, apr, api, pwr_ref, pwi_ref, carry_r, carry_i, cm_r, cm_i, steps, False,
                           base)
            dcre_ref[sb] += _dot(dyb[:, us], sr[...].astype(BF16), TN)
            dcim_ref[sb] -= _dot(dyb[:, us], si[...].astype(BF16), TN)
            _scan_segments(lr, li, abr, -abi, apr, -api, pwr_ref, pwi_ref, lam_r, lam_i, cl_r, cl_i, steps, True,
                           base)
            for c0 in range(0, SB_STATE, SCAN_LANES):
                ls = slice(c0, c0 + SCAN_LANES)
                gs = slice(base + c0, base + c0 + SCAN_LANES)

                def step(t, acc, ls=ls):
                    gar, gai, pr, pi = acc
                    for k in range(SCAN_UNROLL):
                        r0 = pl.multiple_of((t * SCAN_UNROLL + k) * SUBLANES, SUBLANES)
                        rws = pl.ds(r0, SUBLANES)
                        t_r, t_i = _cmul(pr, -pi, lr[rws, ls], li[rws, ls])
                        gar, gai, pr, pi = gar + t_r, gai + t_i, sr[rws, ls], si[rws, ls]
                    return gar, gai, pr, pi

                zero = jnp.zeros((SUBLANES, SCAN_LANES), F32)
                gar, gai, _, _ = lax.fori_loop(0, steps // SCAN_UNROLL, step,
                                               (zero, zero, cm_r[:, gs], cm_i[:, gs]))
                gabr_ref[:, gs] += gar
                gabi_ref[:, gs] += gai
            xr, xi = lr[...].astype(BF16), li[...].astype(BF16)
            du = _dot(xr, btr_ref[sb], NT) + _dot(xi, bti_ref[sb], NT)
            useg[:, us] = du + d_ref[:, us] * dy[:, us]
            dbre_ref[sb] += _dot(ub[:, us], xr, TN)
            dbim_ref[sb] += _dot(ub[:, us], xi, TN)
        _segments_to_rows(du_ref, useg, steps, stage)

        @pl.when(pl.program_id(0) == nc - 1)
        def _():
            for src, dst in ((dbre_ref, gb2r_ref), (dbim_ref, gb2i_ref), (dcre_ref, gc2r_ref), (dcim_ref, gc2i_ref)):
                for sb in range(nsb):
                    dst[sb * MXU_DIM:(sb + 1) * MXU_DIM, :] = _block_rows(src[sb])

    rev = lambda c: nc - 1 - c
    const = lambda a: pl.BlockSpec(a.shape, lambda c: (0,) * a.ndim)
    tile = pl.BlockSpec((chunk, w), lambda c: (rev(c), 0))
    row_n = pl.BlockSpec((1, n_state), lambda c: (0, 0))
    row_w = pl.BlockSpec((1, w), lambda c: (0, 0))
    st = pl.BlockSpec((1, 1, n_state), lambda c: (rev(c), 0, 0))
    acc8 = pl.BlockSpec((SUBLANES, n_state), lambda c: (0, 0))
    big = pltpu.VMEM((chunk, SB_STATE), F32)
    small = pltpu.VMEM((chunk, w), F32)
    row = pltpu.VMEM((1, n_state), F32)
    eight = pltpu.VMEM((SUBLANES, n_state), F32)
    blk = (nsb, MXU_DIM, SB_STATE)
    held = [pltpu.VMEM(blk, BF16)] * 6 + [pltpu.VMEM(blk, F32)] * 4
    vmem = (6 * _nbytes(blk, BF16) + 4 * _nbytes(blk, F32) + 5 * _nbytes((chunk, SB_STATE), F32)
            + 12 * _nbytes((chunk, w), F32) + 20 * _nbytes(bc_rows[0].shape, F32))
    res, landed = _call(
        body, [proj, proj, y, dyg, st_re, st_im, *bc_rows, *rows_p, d_row], name=name,
        out_shape=[jax.ShapeDtypeStruct((rows, w), F32)] + [jax.ShapeDtypeStruct(b.shape, F32) for b in bc_rows[:4]]
        + [jax.ShapeDtypeStruct((SUBLANES, n_state), F32)] * 2 + [jax.ShapeDtypeStruct((1, w), F32)],
        grid=(nc,),
        in_specs=_u_specs(w, o_u, chunk, rev) + [tile, tile, st, st] + [const(b) for b in bc_rows]
        + [row_n] * 6 + [pl.BlockSpec((steps, n_state), lambda c: (0, 0))] * 2 + [row_w],
        out_specs=[tile] + [const(b) for b in bc_rows[:4]] + [acc8] * 2 + [row_w],
        scratch_shapes=held + [small] * 3 + [pltpu.VMEM((w // LANES, chunk, LANES), F32)] + [big] * 4 + [row] * 4
        + [eight] * 4,
        semantics=("arbitrary",), vmem=vmem, rider=rider)
    return res if rider is None else (res, landed)


def _out_proj_loss(merged, w_o, x, target, name):
    rows, d = x.shape
    tm, tn = _tile(rows, 1024, SUBLANES), _tile(d, 1024)

    def body(a_ref, b_ref, x_ref, t_ref, g_ref, gb_ref, l_ref):
        err = x_ref[...] + _dot(a_ref[...], b_ref[...], NN) - t_ref[...]
        g = err * (1.0 / d)
        g_ref[...] = g
        gb_ref[...] = g.astype(BF16)
        part = jnp.sum(0.5 * err * g, axis=0, keepdims=True)
        first = pl.program_id(1) == 0

        @pl.when(first)
        def _():
            l_ref[...] = part

        @pl.when(jnp.logical_not(first))
        def _():
            l_ref[...] += part

    tile = pl.BlockSpec((tm, tn), lambda j, i: (i, j))
    vmem = 2 * (_nbytes((tm, d), BF16) + _nbytes((d, tn), BF16)) + 12 * _nbytes((tm, tn), F32)
    return _pallas(
        body, name=name,
        out_shape=[jax.ShapeDtypeStruct((rows, d), F32), jax.ShapeDtypeStruct((rows, d), BF16),
                   jax.ShapeDtypeStruct((1, d), F32)],
        grid=(d // tn, rows // tm),
        in_specs=[pl.BlockSpec((tm, d), lambda j, i: (i, 0)), pl.BlockSpec((d, tn), lambda j, i: (0, j)), tile, tile],
        out_specs=[tile, tile, pl.BlockSpec((1, tn), lambda j, i: (0, j))],
        compiler_params=_params(("parallel", "arbitrary"), vmem),
    )(merged, w_o, x, target)


def _pair_sum(grad, recv, name):
    r4, cdim = recv.shape
    r = r4 // N_CHIPS
    tr = _tile(r, 544, 16)
    g4 = grad.reshape(N_CHIPS, 2, r, cdim)
    r3 = recv.reshape(N_CHIPS, r, cdim)
    core = jnp.reshape(lax.axis_index("c"), (1,)).astype(jnp.int32)

    def body(c_ref, g_ref, r_ref, o_ref):
        o_ref[...] = (g_ref[0] + r_ref[...]).astype(BF16)

    out = _pallas(
        body, name=name, out_shape=jax.ShapeDtypeStruct((N_CHIPS, r, cdim), BF16),
        grid_spec=pltpu.PrefetchScalarGridSpec(
            num_scalar_prefetch=1, grid=(N_CHIPS, r // tr),
            in_specs=[pl.BlockSpec((1, 1, tr, cdim), lambda j, i, c: (j, c[0], i, 0)),
                      pl.BlockSpec((1, tr, cdim), lambda j, i, c: (j, i, 0))],
            out_specs=pl.BlockSpec((1, tr, cdim), lambda j, i, c: (j, i, 0))),
        compiler_params=_params(("parallel", "parallel"), 6 * _nbytes((tr, cdim), F32)),
    )(core, g4, r3)
    return out.reshape(r4, cdim)


def _chip_sum(recv, name):
    r4, cdim = recv.shape
    r = r4 // N_CHIPS
    tr = _tile(r, 544, 16)
    r3 = recv.reshape(N_CHIPS, r, cdim)

    def body(r_ref, o_ref):
        acc = r_ref[0].astype(F32)
        for j in range(1, N_CHIPS):
            acc = acc + r_ref[j].astype(F32)
        o_ref[...] = acc

    return _pallas(
        body, name=name, out_shape=jax.ShapeDtypeStruct((r, cdim), F32), grid=(r // tr,),
        in_specs=[pl.BlockSpec((N_CHIPS, tr, cdim), lambda i: (0, i, 0))],
        out_specs=pl.BlockSpec((tr, cdim), lambda i: (i, 0)),
        compiler_params=_params(("parallel",), 8 * _nbytes((tr, cdim), F32)),
    )(r3)


def _adamw_math(w, g, m, v):
    m = ADAM_B1 * m + (1.0 - ADAM_B1) * g
    v = ADAM_B2 * v + (1.0 - ADAM_B2) * (g * g)
    m_hat = m / (1.0 - ADAM_B1 ** ADAM_STEP)
    v_hat = v / (1.0 - ADAM_B2 ** ADAM_STEP)
    delta = -ADAM_LR * (m_hat / (jnp.sqrt(v_hat) + ADAM_EPS) + ADAM_WD * w)
    return delta, m, v


def _adamw(w, g, m, v, name):
    rows, cols = w.shape
    tr = _tile(rows, 256, SUBLANES)

    def body(w_ref, g_ref, m_ref, v_ref, d_ref, nm_ref, nv_ref):
        d, nm, nv = _adamw_math(w_ref[...], g_ref[...], m_ref[...], v_ref[...])
        d_ref[...] = d
        nm_ref[...] = nm
        nv_ref[...] = nv

    spec = pl.BlockSpec((tr, cols), lambda i: (i, 0))
    shp = jax.ShapeDtypeStruct((rows, cols), F32)
    return _pallas(
        body, name=name, out_shape=[shp] * 3, grid=(rows // tr,), in_specs=[spec] * 4, out_specs=[spec] * 3,
        compiler_params=_params(("parallel",)),
    )(w, g, m, v)


def _adamw_chips(w, parts, m, v, name):
    rows, cols = w.shape
    assert sum(p.shape[1] for p in parts) == cols
    tr = _tile(rows, 64, 16)
    n = len(parts)

    def body(*refs):
        w_ref, m_ref, v_ref = refs[0], refs[1 + n], refs[2 + n]
        g_ref, d_ref, nm_ref, nv_ref = refs[3 + n:]
        cols_g = []
        for p_ref in refs[1:1 + n]:
            acc = p_ref[0].astype(F32)
            for j in range(1, N_CHIPS):
                acc = acc + p_ref[j].astype(F32)
            cols_g.append(acc)
        g = cols_g[0] if n == 1 else jnp.concatenate(cols_g, axis=1)
        d, nm, nv = _adamw_math(w_ref[...], g, m_ref[...], v_ref[...])
        g_ref[...] = g
        d_ref[...] = d
        nm_ref[...] = nm
        nv_ref[...] = nv

    spec = pl.BlockSpec((tr, cols), lambda i: (i, 0))
    part_specs = [pl.BlockSpec((N_CHIPS, tr, p.shape[1]), lambda i: (0, i, 0)) for p in parts]
    shp = jax.ShapeDtypeStruct((rows, cols), F32)
    return _pallas(
        body, name=name, out_shape=[shp] * 4, grid=(rows // tr,),
        in_specs=[spec] + part_specs + [spec, spec], out_specs=[spec] * 4,
        compiler_params=_params(("parallel",)),
    )(w, *[p.reshape(N_CHIPS, rows, p.shape[1]) for p in parts], m, v)


def _adamw_small(w, parts, m, v, name):
    rows, cols = w.shape
    p3 = parts.reshape(N_DEV, rows, cols)

    def body(w_ref, p_ref, m_ref, v_ref, g_ref, d_ref, nm_ref, nv_ref):
        g = p_ref[0]
        for k in range(1, N_DEV):
            g = g + p_ref[k]
        d, nm, nv = _adamw_math(w_ref[...], g, m_ref[...], v_ref[...])
        g_ref[...] = g
        d_ref[...] = d
        nm_ref[...] = nm
        nv_ref[...] = nv

    shp = jax.ShapeDtypeStruct((rows, cols), F32)
    return _pallas(body, name=name, out_shape=[shp] * 4)(w, p3, m, v)


SMALL = ("norm_w", "q_norm_w", "k_norm_w", "sinks", "A_re", "A_im", "log_dt", "B_re", "B_im", "C_re", "C_im",
         "D_skip", "b_glu")
LARGE = ("w_in", "w_attn_proj", "w_glu", "w_ssm_proj", "w_out")
ORDER = ("norm_w", "w_in", "q_norm_w", "k_norm_w", "sinks", "w_attn_proj", "A_re", "A_im", "log_dt", "B_re", "B_im",
         "C_re", "C_im", "D_skip", "w_glu", "b_glu", "w_ssm_proj", "w_out")


SMALL_REST = ("loss",) + SMALL[1:]


def _pack(named, keys):
    flat = jnp.concatenate([named[k].reshape(-1).astype(F32) for k in keys])
    n = flat.shape[0]
    rows = -(-n // (LANES * SUBLANES)) * SUBLANES
    return jnp.pad(flat, (0, rows * LANES - n)).reshape(rows, LANES)


def _unpack(packed, like, keys):
    flat = packed.reshape(-1)
    out, o = {}, 0
    for k in keys:
        n = like[k].size
        out[k] = flat[o:o + n].reshape(like[k].shape)
        o += n
    return out


def _step(xs, target, p, shards):
    s_in, s_ap, s_glu, s_sp, s_o = shards
    seq, d = xs.shape
    attn_w = (d // 128) * HEAD_DIM
    n_q = attn_w // HEAD_DIM
    kv_w = N_KV_HEADS * HEAD_DIM
    ssm_w = d // 2
    n_groups = ssm_w // GROUP
    n_state = n_groups * STATE
    in_w = N_DEV * s_in.shape[0]
    assert in_w == 2 * attn_w + 2 * kv_w + 2 * ssm_w + 2 * d
    o_u = 2 * attn_w + 2 * kv_w
    o_z = o_u + ssm_w
    o_ga = o_z + ssm_w
    chunk = min(SSM_CHUNK, seq)
    cw = d // 4

    norm_row = p["norm_w"].reshape(1, d)
    half = d // W_IN_PARTS
    assert W_IN_PARTS == 2
    s_in_parts = [s_in[:, :half], s_in[:, half:]]
    h, (w_lo,) = _rmsnorm_fwd(xs, norm_row, "rmsnorm_fwd", rider=_all_gather(s_in_parts[:1]))
    part, (w_hi,) = _matmul(Cols(h, 0, half), w_lo, mode="nt", name="in_proj_0", tn=2176, out_dtype=BF16,
                            rider=_all_gather(s_in_parts[1:]))
    proj = _matmul(Cols(h, half, half), w_hi, mode="nt", name="in_proj_1", tn=2176, out_dtype=BF16, add=part)
    w_in_parts = [w_lo, w_hi]
    qw_row = jnp.tile(p["q_norm_w"], n_q).reshape(1, attn_w)
    kw_row = jnp.tile(p["k_norm_w"], N_KV_HEADS).reshape(1, kv_w)
    gmat = _head_mean_matrix()
    ag = _attention_fwd(proj, qw_row, kw_row, gmat, p["sinks"], attn_w=attn_w, kv_w=kv_w, name="attention_fwd")

    log_dt_col = p["log_dt"].reshape(n_groups, 1)
    prep = _ssm_prep(p["A_re"], p["A_im"], log_dt_col, chunk // SUBLANES, "ssm_prep")
    rows_p = [v.reshape(1, n_state) for v in prep[:6]] + [v.reshape(-1, n_state) for v in prep[6:]]
    bt_re, bt_im = p["B_re"].transpose(0, 2, 1), p["B_im"].transpose(0, 2, 1)
    cf_re, cf_im = prep[2][:, None, :], prep[3][:, None, :]
    bc_rows = [_ssm_rows(m) for m in (bt_re, bt_im, p["C_re"], p["C_im"],
                                      cf_re * bt_re - cf_im * bt_im, cf_re * bt_im + cf_im * bt_re)]
    d_row = p["D_skip"].reshape(1, ssm_w)
    (y_ssm, st_re, st_im, yg), (w_ap_t, w_glu_t, w_sp_t, w_o) = _ssm_fwd(
        proj, o_u, bc_rows[:4], rows_p, d_row, chunk=chunk, name="ssm_fwd",
        rider=_all_gather([s_ap, s_glu, s_sp, s_o]))
    glu = _matmul(yg, w_glu_t, mode="nt", name="glu_proj", out_dtype=BF16, bias=p["b_glu"].reshape(1, 2 * ssm_w))
    (ts,) = _ew(lambda ga, gb, z: ga * _sigmoid(gb) * _silu(z), name="glu_gate", rows=seq, width=ssm_w,
                tiles=[(glu, 0), (glu, ssm_w), (proj, o_z)], outs=[(BF16, ssm_w, 0)], cw=cw)
    yy = _matmul(ag, w_ap_t, mode="nt", name="attn_proj", out_dtype=BF16, out_cols=(2 * d, 0))
    yy = _matmul(ts, w_sp_t, mode="nt", name="ssm_proj", out_dtype=BF16, out_cols=(2 * d, d), into=yy)
    (merged,) = _ew(lambda ya, ys, ga, gs: _sigmoid(ga) * ya + _sigmoid(gs) * ys, name="merge", rows=seq, width=d,
                    tiles=[(yy, 0), (yy, d), (proj, o_ga), (proj, o_ga + d)], outs=[(BF16, d, 0)], cw=cw)
    dout, dout_b, loss_cols = _out_proj_loss(merged, w_o, xs, target, "out_proj_loss")
    loss_local = jnp.sum(loss_cols)

    g_w_o = _matmul(merged, dout_b, mode="tn", name="grad_w_out", tm=512, tk=4096)
    dmerged = _matmul(dout_b, w_o, mode="nt", name="d_merged", out_dtype=BF16)

    def merge_bwd(dm, y, g):
        s = _sigmoid(g)
        return dm * s, dm * y * s * (1.0 - s)

    dyy, dproj = _ew(merge_bwd, name="merge_bwd", rows=seq, width=2 * d,
                     tiles=[(dmerged, 0, d), (yy, 0), (proj, o_ga)],
                     outs=[(BF16, 2 * d, 0), (BF16, in_w, o_ga)], cw=cw)
    dy_a, dy_s = Cols(dyy, 0, d), Cols(dyy, d, d)
    g_w_ap_t = _matmul(dy_a, ag, mode="tn", name="grad_w_attn_proj", tm=512, tk=4096)
    g_w_sp_t = _matmul(dy_s, ts, mode="tn", name="grad_w_ssm_proj", tm=512, tk=4096)
    d_ag = _matmul(dy_a, w_ap_t, mode="nn", name="d_attn_gated", out_dtype=BF16)
    d_ts = _matmul(dy_s, w_sp_t, mode="nn", name="d_ssm_gated", out_dtype=BF16)

    (dproj, g_qw, g_kw, g_sinks), (sib_o, sib_ap, sib_sp) = _attention_bwd(
        proj, d_ag, dproj, qw_row, kw_row, gmat, p["sinks"], attn_w=attn_w, kv_w=kv_w, name="attention_bwd",
        rider=_sibling_exchange([g_w_o, g_w_ap_t, g_w_sp_t]))
    pair_o = _pair_sum(g_w_o, sib_o, "pair_sum_w_out")
    pair_ap = _pair_sum(g_w_ap_t, sib_ap, "pair_sum_w_attn_proj")
    pair_sp = _pair_sum(g_w_sp_t, sib_sp, "pair_sum_w_ssm_proj")

    n_half = ssm_w // _tile(2 * ssm_w, cw)

    def glu_bwd(j, dt, ga, gb, z):
        sb, sz = _sigmoid(gb), _silu(z)
        dg = jnp.where(j < n_half, dt * sb * sz, dt * ga * sb * (1.0 - sb) * sz)
        return dg, dg

    glu_ops = [(d_ts, 0, ssm_w), (glu, 0, ssm_w), (glu, ssm_w, ssm_w), (proj, o_z, ssm_w)]
    dglu, g_bglu = _ew(glu_bwd, name="glu_bwd", rows=seq, width=2 * ssm_w, tiles=glu_ops,
                       outs=[(BF16, 2 * ssm_w, 0)], accs=1, cw=cw, with_col=True)
    (dproj,) = _ew(lambda dt, ga, gb, z: dt * ga * _sigmoid(gb) * _dsilu(z), name="glu_bwd_z", rows=seq,
                   width=ssm_w, tiles=glu_ops, outs=[(BF16, in_w, o_z)], into=[dproj], cw=cw)
    g_w_glu_t = _matmul(dglu, yg, mode="tn", name="grad_w_glu", tm=512, tk=4096)
    d_yg = _matmul(dglu, w_glu_t, mode="nn", name="d_gelu", out_dtype=BF16)
    ((du, dbt_re, dbt_im, dc_re, dc_im, gabr, gabi, g_d), (chips_o, chips_ap, chips_sp, sib_glu)) = _ssm_bwd(
        proj, o_u, y_ssm, d_yg, st_re, st_im, bc_rows, rows_p, d_row, chunk=chunk, name="ssm_bwd",
        rider=_join(_chip_exchange([pair_o, pair_ap, pair_sp]), _sibling_exchange([g_w_glu_t])))
    pair_glu = _pair_sum(g_w_glu_t, sib_glu, "pair_sum_w_glu")
    (dproj,) = _ew(lambda v: v, name="du_store", rows=seq, width=ssm_w, tiles=[(du, 0)],
                   outs=[(BF16, in_w, o_u)], into=[dproj], cw=cw)
    g_a_re, g_a_im, g_log_dt, g_bt_re, g_bt_im = _ssm_param_bwd(
        p["A_re"], p["A_im"], log_dt_col, *[g.reshape(SUBLANES, n_groups, STATE) for g in (gabr, gabi)],
        bt_re, bt_im, _from_ssm_rows(dbt_re), _from_ssm_rows(dbt_im), "ssm_param_bwd")
    small_grads = dict(
        loss=loss_local, q_norm_w=g_qw.reshape(n_q, HEAD_DIM).sum(0), k_norm_w=g_kw.reshape(N_KV_HEADS, HEAD_DIM).sum(0),
        sinks=g_sinks[0, :n_q], A_re=g_a_re, A_im=g_a_im, log_dt=g_log_dt.reshape(n_groups),
        B_re=g_bt_re.transpose(0, 2, 1), B_im=g_bt_im.transpose(0, 2, 1),
        C_re=_from_ssm_rows(dc_re), C_im=_from_ssm_rows(dc_im),
        D_skip=g_d.reshape(n_groups, GROUP), b_glu=g_bglu.reshape(2 * ssm_w))

    n_parts = W_IN_PARTS
    wq = d // n_parts
    g_parts, pair_parts, chip_parts = [], [], []
    extra = [_chip_exchange([pair_glu]), _all_gather([_pack(small_grads, SMALL_REST)])]
    chips_glu = small_parts = dh = None
    for step in range(n_parts + 2):
        riders = list(extra) if step == 0 else []
        if 0 <= step - 2 < n_parts:
            riders.append(_chip_exchange([pair_parts[step - 2]]))
        if 0 <= step - 1 < n_parts:
            riders.append(_sibling_exchange([g_parts[step - 1]]))
        rider = _join(*riders) if riders else None
        if step < n_parts:
            res = _matmul(dproj, Cols(h, step * wq, wq), mode="tn", name="grad_w_in_%d" % step, tk=4096, rider=rider)
            out, landed = res if rider is not None else (res, [])
            g_parts.append(out)
        else:
            q = step - n_parts
            dh, landed = _matmul(dproj, w_in_parts[q], mode="nn", name="d_normed_%d" % q, tk=2176,
                                 out_cols=(d, q * wq), into=dh, rider=rider)
        landed = list(landed)
        if step == 0:
            chips_glu, small_parts = landed[:2]
            landed = landed[2:]
        if 0 <= step - 2 < n_parts:
            chip_parts.append(landed.pop(0))
        if 0 <= step - 1 < n_parts:
            pair_parts.append(_pair_sum(g_parts[step - 1], landed.pop(0), "pair_sum_w_in_%d" % (step - 1)))
    grad_x, g_norm = _rmsnorm_bwd(xs, norm_row, dh, dout, "rmsnorm_bwd")
    (norm_parts,) = _exchange(_all_gather([_pack(dict(norm_w=g_norm), ("norm_w",))]), "gather_norm_grad")
    from_chips = dict(zip(LARGE, (chip_parts, [chips_ap], [chips_glu], [chips_sp], [chips_o])))
    return grad_x, from_chips, small_parts, norm_parts


def kernel(x, norm_w, w_in, q_norm_w, k_norm_w, sinks, w_attn_proj, A_re, A_im, log_dt, B_re, B_im, C_re, C_im, D_skip, w_glu, b_glu, w_ssm_proj, w_out, loss_target, m_norm_w, m_w_in, m_q_norm_w, m_k_norm_w, m_sinks, m_w_attn_proj, m_A_re, m_A_im, m_log_dt, m_B_re, m_B_im, m_C_re, m_C_im, m_D_skip, m_w_glu, m_b_glu, m_w_ssm_proj, m_w_out, v_norm_w, v_w_in, v_q_norm_w, v_k_norm_w, v_sinks, v_w_attn_proj, v_A_re, v_A_im, v_log_dt, v_B_re, v_B_im, v_C_re, v_C_im, v_D_skip, v_w_glu, v_b_glu, v_w_ssm_proj, v_w_out):
    weights = dict(norm_w=norm_w, w_in=w_in, q_norm_w=q_norm_w, k_norm_w=k_norm_w, sinks=sinks,
                   w_attn_proj=w_attn_proj, A_re=A_re, A_im=A_im, log_dt=log_dt, B_re=B_re, B_im=B_im, C_re=C_re,
                   C_im=C_im, D_skip=D_skip, w_glu=w_glu, b_glu=b_glu, w_ssm_proj=w_ssm_proj, w_out=w_out)
    m_in = dict(norm_w=m_norm_w, w_in=m_w_in, q_norm_w=m_q_norm_w, k_norm_w=m_k_norm_w, sinks=m_sinks,
                w_attn_proj=m_w_attn_proj, A_re=m_A_re, A_im=m_A_im, log_dt=m_log_dt, B_re=m_B_re, B_im=m_B_im,
                C_re=m_C_re, C_im=m_C_im, D_skip=m_D_skip, w_glu=m_w_glu, b_glu=m_b_glu, w_ssm_proj=m_w_ssm_proj,
                w_out=m_w_out)
    v_in = dict(norm_w=v_norm_w, w_in=v_w_in, q_norm_w=v_q_norm_w, k_norm_w=v_k_norm_w, sinks=v_sinks,
                w_attn_proj=v_w_attn_proj, A_re=v_A_re, A_im=v_A_im, log_dt=v_log_dt, B_re=v_B_re, B_im=v_B_im,
                C_re=v_C_re, C_im=v_C_im, D_skip=v_D_skip, w_glu=v_w_glu, b_glu=v_b_glu, w_ssm_proj=v_w_ssm_proj,
                w_out=v_w_out)

    _, seq, d = x.shape
    column_sharded = LARGE[:4]
    as_rows = lambda k, a: a.T if k in column_sharded else a
    shards = [as_rows(k, weights[k]).astype(BF16) for k in LARGE]
    small = {k: weights[k] for k in SMALL}
    grad_x, from_chips, small_parts, norm_parts = _step(x.reshape(seq, d), loss_target.reshape(seq, d), small,
                                                        shards)

    grads, delta, new_m, new_v = {}, {}, {}, {}
    for k in LARGE:
        if k == "w_in":
            res = _adamw_chips(weights[k].T, from_chips[k], m_in[k].T, v_in[k].T, "adamw_" + k)
            grads[k], delta[k], new_m[k], new_v[k] = [a.T for a in res]
        elif k == "w_out":
            grads[k], delta[k], new_m[k], new_v[k] = _adamw_chips(weights[k], from_chips[k], m_in[k], v_in[k],
                                                                  "adamw_" + k)
        else:
            grads[k] = _chip_sum(from_chips[k][0], "chip_sum_" + k).T
            delta[k], new_m[k], new_v[k] = _adamw(weights[k], grads[k], m_in[k], v_in[k], "adamw_" + k)

    zero = jnp.zeros((), F32)
    for keys, parts in ((SMALL_REST, small_parts), (("norm_w",), norm_parts)):
        like = dict(small, loss=zero)
        packs = [_pack(dict(src, loss=zero), keys) for src in (weights, m_in, v_in)]
        res = _adamw_small(packs[0], parts, packs[1], packs[2], "adamw_small_%d" % len(keys))
        for dst, r in zip((grads, delta, new_m, new_v), res):
            dst.update(_unpack(r, like, keys))
    loss = grads["loss"]

    return (loss, grad_x.reshape(x.shape), *[grads[k] for k in ORDER], *[delta[k] for k in ORDER],
            *[new_m[k] for k in ORDER], *[new_v[k] for k in ORDER])
```

```python
import math
from typing import Callable, NamedTuple

import jax
import jax.numpy as jnp
import numpy as np
from jax import lax
from jax.experimental import pallas as pl
from jax.experimental.pallas import tpu as pltpu

F32 = jnp.float32
BF16 = jnp.bfloat16
MESH = pl.DeviceIdType.MESH

HEAD_DIM = 64
N_KV_HEADS = 4
GROUP = 16
STATE = 64
BLOCK = 128
NORM_EPS = 1e-6
N_DEV = 8
N_CHIPS = 4
LANES = 128
SUBLANES = 8
MXU_DIM = 256
VMEM_BYTES = 64 * 1024 * 1024
VMEM_CAP = VMEM_BYTES - 8 * 1024 * 1024

ADAM_LR = 0.001
ADAM_B1 = 0.9
ADAM_B2 = 0.999
ADAM_EPS = 1e-08
ADAM_WD = 0.01
ADAM_STEP = 10

GELU_C = math.sqrt(2.0 / math.pi)
GELU_K = 0.044715


def _tile(dim, pref, mult=LANES):
    if dim <= pref:
        return dim
    best = None
    for d in range(mult, pref + 1, mult):
        if dim % d == 0:
            best = d
    assert best is not None, (dim, pref, mult)
    return best


def _params(semantics=None, vmem=None):
    kw = {}
    if semantics is not None:
        kw["dimension_semantics"] = semantics
    if vmem is not None:
        kw["vmem_limit_bytes"] = int(min(VMEM_CAP, max(vmem, 32 * 1024 * 1024)))
    return pltpu.CompilerParams(**kw)


def _nbytes(shape, dtype):
    return math.prod(shape) * jnp.dtype(dtype).itemsize


def _sigmoid(x):
    return 1.0 / (1.0 + jnp.exp(-x))


def _silu(x):
    return x * _sigmoid(x)


def _dsilu(x):
    s = _sigmoid(x)
    return s * (1.0 + x * (1.0 - s))


def _gelu(x):
    return 0.5 * x * (1.0 + jnp.tanh(GELU_C * (x + GELU_K * x * x * x)))


def _dgelu(x):
    t = jnp.tanh(GELU_C * (x + GELU_K * x * x * x))
    return 0.5 * (1.0 + t) + 0.5 * x * (1.0 - t * t) * GELU_C * (1.0 + 3.0 * GELU_K * x * x)


def _dot(a, b, dims):
    return lax.dot_general(a, b, (dims, ((), ())), preferred_element_type=F32)


NN = ((1,), (0,))
NT = ((1,), (1,))
TN = ((0,), (0,))


def _any_spec():
    return pl.BlockSpec(memory_space=pl.ANY)


def _pallas(body, **kw):
    pin = lambda s: pltpu.HBM(s.shape, s.dtype) if isinstance(s, jax.ShapeDtypeStruct) else s
    out_shape = kw.pop("out_shape")
    out_shape = [pin(s) for s in out_shape] if isinstance(out_shape, (list, tuple)) else pin(out_shape)
    call = pl.pallas_call(body, out_shape=out_shape, **kw)

    def run(*operands):
        pinned = [pltpu.with_memory_space_constraint(o, pltpu.HBM) if jnp.issubdtype(o.dtype, jnp.floating) else o
                  for o in operands]
        return call(*pinned)

    return run


class Rider(NamedTuple):
    operands: tuple
    out_shapes: tuple
    sems: tuple
    start: Callable
    finish: Callable


def _all_gather(shards):
    n = len(shards)

    def copies(ins, outs, sems):
        send_sems, recv_sems, local_sems = sems
        x, y, c = lax.axis_index("x"), lax.axis_index("y"), lax.axis_index("c")
        me, sibling = (x, y, c), (x, y, 1 - c)
        chips = [(1 - x, y), (x, 1 - y), (1 - x, 1 - y)]

        def rows(k, px, py, pc):
            r = shards[k].shape[0]
            return outs[k].at[pl.ds((4 * px + 2 * py + pc) * r, r), :]

        def copy(k, s, block, to, src=None):
            return pltpu.make_async_remote_copy(
                src_ref=rows(k, *block) if src is None else src, dst_ref=rows(k, *block),
                send_sem=send_sems.at[7 * k + s], recv_sem=recv_sems.at[7 * k + s],
                device_id=to, device_id_type=MESH)

        mine = [pltpu.make_async_copy(ins[k], rows(k, *me), local_sems.at[k]) for k in range(n)]
        first = []
        for k in range(n):
            first.append(copy(k, 0, me, sibling, src=ins[k]))
            first += [copy(k, 1 + j, me, (*chip, c), src=ins[k]) for j, chip in enumerate(chips)]
        return me, sibling, chips, c, copy, mine, first

    def start(ins, outs, sems):
        *_, mine, first = copies(ins, outs, sems)
        for cp in mine + first:
            cp.start()

    def finish(ins, outs, sems):
        me, sibling, chips, c, copy, mine, first = copies(ins, outs, sems)
        passed = []
        for j, chip in enumerate(chips):
            for k in range(n):
                copy(k, 1 + j, (*chip, c), me).wait_recv()
                fwd = copy(k, 4 + j, (*chip, c), sibling)
                fwd.start()
                passed.append(fwd)
        for k in range(n):
            copy(k, 0, sibling, me).wait_recv()
            for j, chip in enumerate(chips):
                copy(k, 4 + j, (*chip, 1 - c), me).wait_recv()
        for cp in first + passed:
            cp.wait_send()
        for cp in mine:
            cp.wait()

    return Rider(
        tuple(shards),
        tuple(jax.ShapeDtypeStruct((N_DEV * s.shape[0], s.shape[1]), s.dtype) for s in shards),
        (pltpu.SemaphoreType.DMA((7 * n,)), pltpu.SemaphoreType.DMA((7 * n,)), pltpu.SemaphoreType.DMA((n,))),
        start, finish)


def _sibling_exchange(grads):
    n = len(grads)

    def copies(ins, outs, sems):
        send_sems, recv_sems = sems
        x, y, c = lax.axis_index("x"), lax.axis_index("y"), lax.axis_index("c")
        out = []
        for k in range(n):
            r = grads[k].shape[0] // N_DEV
            for j in range(N_CHIPS):
                out.append(pltpu.make_async_remote_copy(
                    src_ref=ins[k].at[pl.ds((2 * j + 1 - c) * r, r), :],
                    dst_ref=outs[k].at[pl.ds(j * r, r), :],
                    send_sem=send_sems.at[N_CHIPS * k + j], recv_sem=recv_sems.at[N_CHIPS * k + j],
                    device_id=(x, y, 1 - c), device_id_type=MESH))
        return out

    def start(ins, outs, sems):
        for cp in copies(ins, outs, sems):
            cp.start()

    def finish(ins, outs, sems):
        for cp in copies(ins, outs, sems):
            cp.wait()

    return Rider(
        tuple(grads), tuple(jax.ShapeDtypeStruct((g.shape[0] // 2, g.shape[1]), g.dtype) for g in grads),
        (pltpu.SemaphoreType.DMA((N_CHIPS * n,)), pltpu.SemaphoreType.DMA((N_CHIPS * n,))), start, finish)


def _chip_exchange(parts):
    n = len(parts)

    def copies(ins, outs, sems):
        send_sems, recv_sems, local_sems = sems
        x, y, c = lax.axis_index("x"), lax.axis_index("y"), lax.axis_index("c")
        my_chip = 2 * x + y
        chips = [(1 - x, y), (x, 1 - y), (1 - x, 1 - y)]
        local, sent = [], []
        for k in range(n):
            r = parts[k].shape[0] // N_CHIPS
            mine = pl.ds(my_chip * r, r)
            local.append(pltpu.make_async_copy(ins[k].at[mine, :], outs[k].at[mine, :], local_sems.at[k]))
            for s, (px, py) in enumerate(chips):
                sent.append(pltpu.make_async_remote_copy(
                    src_ref=ins[k].at[pl.ds((2 * px + py) * r, r), :], dst_ref=outs[k].at[mine, :],
                    send_sem=send_sems.at[3 * k + s], recv_sem=recv_sems.at[3 * k + s],
                    device_id=(px, py, c), device_id_type=MESH))
        return local, sent

    def start(ins, outs, sems):
        local, sent = copies(ins, outs, sems)
        for cp in local + sent:
            cp.start()

    def finish(ins, outs, sems):
        local, sent = copies(ins, outs, sems)
        for cp in sent + local:
            cp.wait()

    return Rider(
        tuple(parts), tuple(jax.ShapeDtypeStruct(p.shape, p.dtype) for p in parts),
        (pltpu.SemaphoreType.DMA((3 * n,)), pltpu.SemaphoreType.DMA((3 * n,)), pltpu.SemaphoreType.DMA((n,))),
        start, finish)


def _join(*riders):
    cuts_in, cuts_out, cuts_sem = [0], [0], [0]
    for r in riders:
        cuts_in.append(cuts_in[-1] + len(r.operands))
        cuts_out.append(cuts_out[-1] + len(r.out_shapes))
        cuts_sem.append(cuts_sem[-1] + len(r.sems))

    def each(which):
        def run(ins, outs, sems):
            for i, r in enumerate(riders):
                getattr(r, which)(ins[cuts_in[i]:cuts_in[i + 1]], outs[cuts_out[i]:cuts_out[i + 1]],
                                  sems[cuts_sem[i]:cuts_sem[i + 1]])
        return run

    return Rider(sum((r.operands for r in riders), ()), sum((r.out_shapes for r in riders), ()),
                 sum((r.sems for r in riders), ()), each("start"), each("finish"))


def _call(body, operands, *, name, out_shape, grid, in_specs, out_specs, scratch_shapes=(), aliases=None,
          semantics=None, vmem=None, rider=None):
    operands, out_shape, scratch_shapes = list(operands), list(out_shape), list(scratch_shapes)
    in_specs, out_specs = list(in_specs), list(out_specs)
    if rider is None:
        res = _pallas(
            body, name=name, out_shape=out_shape, grid=grid, in_specs=in_specs, out_specs=out_specs,
            scratch_shapes=scratch_shapes, input_output_aliases=aliases or {},
            compiler_params=_params(semantics, vmem))(*operands)
        return list(res), []
    n_in, n_out, n_scr = len(operands), len(out_shape), len(scratch_shapes)
    ri, ro = len(rider.operands), len(rider.out_shapes)

    def carried(*refs):
        a, b = n_in, n_in + ri
        c, d = b + n_out, b + n_out + ro
        e = d + n_scr
        ids = [pl.program_id(k) for k in range(len(grid))]
        first = ids[0] == 0
        last = ids[0] == grid[0] - 1
        for k in range(1, len(grid)):
            first = jnp.logical_and(first, ids[k] == 0)
            last = jnp.logical_and(last, ids[k] == grid[k] - 1)

        @pl.when(first)
        def _():
            rider.start(refs[a:b], refs[c:d], refs[e:])

        body(*refs[:a], *refs[b:c], *refs[d:e])

        @pl.when(last)
        def _():
            rider.finish(refs[a:b], refs[c:d], refs[e:])

    res = _pallas(
        carried, name=name, out_shape=out_shape + list(rider.out_shapes), grid=grid,
        in_specs=in_specs + [_any_spec()] * ri, out_specs=out_specs + [_any_spec()] * ro,
        scratch_shapes=scratch_shapes + list(rider.sems), input_output_aliases=aliases or {},
        compiler_params=_params(("arbitrary",) * len(grid), vmem))(*operands, *rider.operands)
    return list(res[:n_out]), list(res[n_out:])


def _exchange(rider, name):
    ri, ro = len(rider.operands), len(rider.out_shapes)

    def body(*refs):
        rider.start(refs[:ri], refs[ri:ri + ro], refs[ri + ro:])
        rider.finish(refs[:ri], refs[ri:ri + ro], refs[ri + ro:])

    return _pallas(
        body, name=name, out_shape=list(rider.out_shapes), in_specs=[_any_spec()] * ri,
        out_specs=[_any_spec()] * ro, scratch_shapes=list(rider.sems))(*rider.operands)


class Cols(NamedTuple):
    arr: jax.Array
    off: int
    width: int


def _cols(a):
    return a if isinstance(a, Cols) else Cols(a, 0, a.shape[1])


def _matmul(a, b, *, mode, name, out_dtype=F32, tm=1024, tn=1024, tk=2048, bias=None, add=None, out_cols=None,
            into=None, rider=None):
    a, b = _cols(a), _cols(b)
    if mode == "nn":
        (m, k), (k2, n) = (a.arr.shape[0], a.width), (b.arr.shape[0], b.width)
    elif mode == "nt":
        (m, k), (n, k2) = (a.arr.shape[0], a.width), (b.arr.shape[0], b.width)
    else:
        (k, m), (k2, n) = (a.arr.shape[0], a.width), (b.arr.shape[0], b.width)
    assert k == k2, (a.arr.shape, b.arr.shape, mode)
    tm, tn, tk = _tile(m, tm), _tile(n, tn), _tile(k, tk)
    nk = k // tk
    dims = {"nn": NN, "nt": NT, "tn": TN}[mode]
    if mode == "tn":
        assert a.off % tm == 0
        a_spec = pl.BlockSpec((tk, tm), lambda i, j, kk, o=a.off // tm: (kk, i + o))
    else:
        assert a.off % tk == 0
        a_spec = pl.BlockSpec((tm, tk), lambda i, j, kk, o=a.off // tk: (i, kk + o))
    if mode == "nt":
        assert b.off % tk == 0
        b_spec = pl.BlockSpec((tn, tk), lambda i, j, kk, o=b.off // tk: (j, kk + o))
    else:
        assert b.off % tn == 0
        b_spec = pl.BlockSpec((tk, tn), lambda i, j, kk, o=b.off // tn: (kk, j + o))
    in_specs, operands = [a_spec, b_spec], [a.arr, b.arr]
    assert bias is None or add is None
    if bias is not None:
        in_specs.append(pl.BlockSpec((1, tn), lambda i, j, kk: (0, j)))
        operands.append(bias)
    if add is not None:
        assert add.shape == (m, n)
        in_specs.append(pl.BlockSpec((tm, tn), lambda i, j, kk: (i, j)))
        operands.append(add)
    total_w, o_off = out_cols if out_cols is not None else (n, 0)
    assert o_off % tn == 0
    aliases = {}
    if into is not None:
        assert into.shape == (m, total_w) and into.dtype == out_dtype
        in_specs.append(_any_spec())
        operands.append(into)
        aliases = {len(operands) - 1: 0}
    n_in = len(operands)

    def body(*refs):
        a_ref, b_ref = refs[0], refs[1]
        bias_ref = refs[2] if bias is not None or add is not None else None
        o_ref = refs[n_in]
        acc_ref = refs[-1] if nk > 1 else None
        part = _dot(a_ref[...].astype(BF16), b_ref[...].astype(BF16), dims)

        def finish(acc):
            if bias_ref is not None:
                acc = acc + bias_ref[...]
            o_ref[...] = acc.astype(out_dtype)

        if nk == 1:
            finish(part)
        else:
            kk = pl.program_id(2)

            @pl.when(kk == 0)
            def _():
                acc_ref[...] = part

            @pl.when(kk > 0)
            def _():
                acc_ref[...] += part

            @pl.when(kk == nk - 1)
            def _():
                finish(acc_ref[...])

    vmem = 2 * (_nbytes((tm, tk), a.arr.dtype) + _nbytes((tk, tn), b.arr.dtype) + _nbytes((tm, tn), out_dtype))
    vmem += 3 * _nbytes((tm, tn), F32)
    (out,), landed = _call(
        body, operands, name=name, out_shape=[jax.ShapeDtypeStruct((m, total_w), out_dtype)],
        grid=(m // tm, n // tn, nk), in_specs=in_specs,
        out_specs=[pl.BlockSpec((tm, tn), lambda i, j, kk, o=o_off // tn: (i, j + o))],
        scratch_shapes=[pltpu.VMEM((tm, tn), F32)] if nk > 1 else [], aliases=aliases,
        semantics=("parallel", "parallel", "arbitrary"), vmem=vmem, rider=rider)
    return out if rider is None else (out, landed)


def _ew(fn, *, name, rows, width, tiles, vecs=(), outs, accs=0, tl=1024, cw=512, into=None, with_col=False):
    tl, cw = _tile(rows, tl, SUBLANES), _tile(width, cw)
    ncol = width // cw
    nt_, nv = len(tiles), len(vecs)
    into = list(into) if into is not None else [None] * len(outs)
    aliased = [t for t in into if t is not None]

    def off(o):
        assert o % cw == 0, (name, o, cw)
        return o // cw

    in_specs, vmem = [], 0
    for t in tiles:
        arr, o = t[0], off(t[1])
        wrap = t[2] // cw if len(t) > 2 else ncol
        in_specs.append(pl.BlockSpec((tl, cw), lambda j, i, o=o, wrap=wrap: (i, o + j % wrap)))
        vmem += _nbytes((tl, cw), arr.dtype)
    in_specs += [pl.BlockSpec((1, cw), lambda j, i, o=off(o): (0, j + o)) for _, o in vecs]
    in_specs += [_any_spec() for _ in aliased]
    out_shape, out_specs, aliases = [], [], {}
    n_in = nt_ + nv
    for idx, ((dt, tw, o), tgt) in enumerate(zip(outs, into)):
        out_shape.append(jax.ShapeDtypeStruct((rows, tw), dt))
        out_specs.append(pl.BlockSpec((tl, cw), lambda j, i, o=off(o): (i, j + o)))
        vmem += _nbytes((tl, cw), dt)
        if tgt is not None:
            assert tgt.shape == (rows, tw) and tgt.dtype == dt, (name, tgt.shape, tgt.dtype)
            aliases[n_in + len(aliases)] = idx
    for _ in range(accs):
        out_shape.append(jax.ShapeDtypeStruct((1, width), F32))
        out_specs.append(pl.BlockSpec((1, cw), lambda j, i: (0, j)))
    n_out = len(outs)

    def body(*refs):
        vals = [r[...].astype(F32) for r in refs[:n_in]]
        out_refs = refs[n_in + len(aliased):]
        res = fn(pl.program_id(0), *vals) if with_col else fn(*vals)
        res = res if isinstance(res, (tuple, list)) else (res,)
        assert len(res) == n_out + accs, (name, len(res))
        for r, v in zip(out_refs[:n_out], res[:n_out]):
            r[...] = v.astype(r.dtype)
        first = pl.program_id(1) == 0
        for r, v in zip(out_refs[n_out:], res[n_out:]):
            s = jnp.sum(v, axis=0, keepdims=True)

            @pl.when(first)
            def _(r=r, s=s):
                r[...] = s

            @pl.when(jnp.logical_not(first))
            def _(r=r, s=s):
                r[...] += s

    return _pallas(
        body, name=name, out_shape=out_shape, grid=(ncol, rows // tl),
        in_specs=in_specs, out_specs=out_specs, input_output_aliases=aliases,
        compiler_params=_params(("parallel", "arbitrary"), 3 * vmem),
    )(*[t[0] for t in tiles], *[v for v, _ in vecs], *aliased)


def _rmsnorm_fwd(x, w_row, name, rider=None):
    rows, d = x.shape
    tl = _tile(rows, 512, SUBLANES)

    def body(x_ref, w_ref, h_ref):
        xv = x_ref[...]
        rstd = lax.rsqrt(jnp.mean(xv * xv, axis=-1, keepdims=True) + NORM_EPS)
        h_ref[...] = (xv * rstd * w_ref[...]).astype(BF16)

    (h,), landed = _call(
        body, [x, w_row], name=name, out_shape=[jax.ShapeDtypeStruct((rows, d), BF16)], grid=(rows // tl,),
        in_specs=[pl.BlockSpec((tl, d), lambda i: (i, 0)), pl.BlockSpec((1, d), lambda i: (0, 0))],
        out_specs=[pl.BlockSpec((tl, d), lambda i: (i, 0))], semantics=("parallel",), rider=rider)
    return h if rider is None else (h, landed)


def _rmsnorm_bwd(x, w_row, dh, dout, name, rider=None):
    rows, d = x.shape
    tl = _tile(rows, 256, SUBLANES)

    def body(x_ref, w_ref, dh_ref, dout_ref, gx_ref, gw_ref):
        xv = x_ref[...]
        rstd = lax.rsqrt(jnp.mean(xv * xv, axis=-1, keepdims=True) + NORM_EPS)
        xn = xv * rstd
        dhv = dh_ref[...]
        dxn = dhv * w_ref[...]
        dx = rstd * (dxn - xn * jnp.mean(dxn * xn, axis=-1, keepdims=True))
        gx_ref[...] = dout_ref[...] + dx
        gw = jnp.sum(dhv * xn, axis=0, keepdims=True)

        @pl.when(pl.program_id(0) == 0)
        def _():
            gw_ref[...] = gw

        @pl.when(pl.program_id(0) > 0)
        def _():
            gw_ref[...] += gw

    tile = pl.BlockSpec((tl, d), lambda i: (i, 0))
    row = pl.BlockSpec((1, d), lambda i: (0, 0))
    res, landed = _call(
        body, [x, w_row, dh, dout], name=name,
        out_shape=[jax.ShapeDtypeStruct((rows, d), F32), jax.ShapeDtypeStruct((1, d), F32)],
        grid=(rows // tl,), in_specs=[tile, row, tile, tile], out_specs=[tile, row],
        semantics=("arbitrary",), rider=rider)
    return res if rider is None else (res, landed)


def _head_mean(x, gmat):
    hi = x.astype(BF16)
    lo = (x - hi.astype(F32)).astype(BF16)
    out = []
    for s in range(x.shape[1] // MXU_DIM):
        sl = slice(s * MXU_DIM, (s + 1) * MXU_DIM)
        out.append(_dot(hi[:, sl], gmat, NN) + _dot(lo[:, sl], gmat, NN))
    return out[0] if len(out) == 1 else jnp.concatenate(out, axis=1)


def _head_mean_matrix():
    blk = jnp.arange(MXU_DIM) // HEAD_DIM
    return jnp.where(blk[:, None] == blk[None, :], 1.0 / HEAD_DIM, 0.0).astype(BF16)


def _spread_head(x, g, width):
    col = x[:, (g // 2) * LANES:(g // 2 + 1) * LANES]
    other = pltpu.roll(col, HEAD_DIM, axis=1)
    low = lax.broadcasted_iota(jnp.int32, col.shape, 1) < HEAD_DIM
    both = jnp.where(low, col, other) if g % 2 == 0 else jnp.where(low, other, col)
    return both if width == LANES else jnp.concatenate([both] * (width // LANES), axis=1)


def _head_diagonal(t, per_kv):
    head = lax.broadcasted_iota(jnp.int32, t.shape, 1) // HEAD_DIM
    zero = jnp.zeros_like(t)
    return jnp.concatenate([jnp.where(head == r, t, zero) for r in range(per_kv)], axis=0)


def _fold_heads(x, per_kv):
    rows = x.shape[0] // per_kv
    head = lax.broadcasted_iota(jnp.int32, (rows, x.shape[1]), 1) // HEAD_DIM
    acc = jnp.where(head == 0, x[0:rows], 0.0)
    for r in range(1, per_kv):
        acc = acc + jnp.where(head == r, x[r * rows:(r + 1) * rows], 0.0)
    while acc.shape[1] > LANES:
        half = acc.shape[1] // 2
        acc = acc[:, :half] + acc[:, half:]
    return acc + pltpu.roll(acc, HEAD_DIM, axis=1)


def _join_heads(parts):
    low = lax.broadcasted_iota(jnp.int32, parts[0].shape, 1) < HEAD_DIM
    cols = [jnp.where(low, parts[2 * j], parts[2 * j + 1]) for j in range(len(parts) // 2)]
    return cols[0] if len(cols) == 1 else jnp.concatenate(cols, axis=1)


def _attn_specs(attn_w, kv_w, block=lambda s: s):
    half = attn_w // 2
    kcol, vcol = attn_w // kv_w, attn_w // kv_w + 1
    gcol = (attn_w + 2 * kv_w) // half
    prev = lambda s: jnp.maximum(block(s) - 1, 0)
    return [
        pl.BlockSpec((BLOCK, attn_w), lambda s: (block(s), 0)),
        pl.BlockSpec((BLOCK, kv_w), lambda s: (prev(s), kcol)),
        pl.BlockSpec((BLOCK, kv_w), lambda s: (block(s), kcol)),
        pl.BlockSpec((BLOCK, kv_w), lambda s: (prev(s), vcol)),
        pl.BlockSpec((BLOCK, kv_w), lambda s: (block(s), vcol)),
        pl.BlockSpec((BLOCK, half), lambda s: (block(s), gcol)),
        pl.BlockSpec((BLOCK, half), lambda s: (block(s), gcol + 1)),
    ]


def _band_mask(i):
    q_loc = lax.broadcasted_iota(jnp.int32, (BLOCK, 2 * BLOCK), 0) + BLOCK
    k_loc = lax.broadcasted_iota(jnp.int32, (BLOCK, 2 * BLOCK), 1)
    diff = q_loc - k_loc
    first_key = jnp.where(i == 0, BLOCK, 0)
    return (diff >= 0) & (diff < BLOCK) & (k_loc >= first_key)


def _softmax_with_sink(s, sink):
    m = jnp.maximum(jnp.max(s, axis=-1, keepdims=True), sink)
    p = jnp.exp(s - m)
    e_sink = jnp.exp(sink - m)
    den = jnp.sum(p, axis=-1, keepdims=True) + e_sink
    inv = 1.0 / den
    return p * inv, e_sink * inv


def _attn_block(i, q, kk, vv, qw, kw, gmat, sink_ref, per_kv):
    scale = 1.0 / math.sqrt(HEAD_DIM)
    keys = 2 * BLOCK
    valid = _band_mask(i)
    q_rstd = lax.rsqrt(_head_mean(q * q, gmat) + NORM_EPS)
    qn = q * q_rstd
    qh = (qn * qw).astype(BF16)
    k_rstd = lax.rsqrt(_head_mean(kk * kk, gmat) + NORM_EPS)
    kn = kk * k_rstd
    kh = kn * kw
    gw = per_kv * HEAD_DIM
    groups = []
    for g in range(N_KV_HEADS):
        kd = _head_diagonal(_spread_head(kh, g, gw).astype(BF16), per_kv)
        vd = _head_diagonal(_spread_head(vv, g, gw).astype(BF16), per_kv)
        qg = qh[:, g * gw:(g + 1) * gw]
        s_all = _dot(qg, kd, NT) * scale
        ps, p_sinks = [], []
        for r in range(per_kv):
            s = jnp.where(valid, s_all[:, r * keys:(r + 1) * keys], -1e30)
            p, p_sink = _softmax_with_sink(s, sink_ref[g * per_kv + r])
            ps.append(p)
            p_sinks.append(p_sink)
        pb = jnp.concatenate(ps, axis=1).astype(BF16)
        groups.append((kd, vd, qg, ps, p_sinks, pb, _dot(pb, vd, NN)))
    return qn, q_rstd, kn, k_rstd, groups


def _attention_fwd(proj, qw_row, kw_row, gmat, sinks, *, attn_w, kv_w, name):
    rows = proj.shape[0]
    per_kv = attn_w // HEAD_DIM // N_KV_HEADS

    def body(q_ref, kp_ref, kc_ref, vp_ref, vc_ref, glo_ref, ghi_ref, qw_ref, kw_ref, gm_ref, sink_ref, o_ref):
        kk = jnp.concatenate([kp_ref[...], kc_ref[...]], axis=0).astype(F32)
        vv = jnp.concatenate([vp_ref[...], vc_ref[...]], axis=0).astype(F32)
        gate = jnp.concatenate([glo_ref[...], ghi_ref[...]], axis=1).astype(F32)
        *_, groups = _attn_block(pl.program_id(0), q_ref[...].astype(F32), kk, vv, qw_ref[...], kw_ref[...], gm_ref[...],
                                 sink_ref, per_kv)
        attn = jnp.concatenate([grp[-1] for grp in groups], axis=1)
        o_ref[...] = (attn * _silu(gate)).astype(BF16)

    const = lambda a: pl.BlockSpec(a.shape, lambda i: (0, 0))
    return _pallas(
        body, name=name, out_shape=jax.ShapeDtypeStruct((rows, attn_w), BF16), grid=(rows // BLOCK,),
        in_specs=_attn_specs(attn_w, kv_w) + [const(qw_row), const(kw_row), const(gmat),
                                              pl.BlockSpec(memory_space=pltpu.SMEM)],
        out_specs=pl.BlockSpec((BLOCK, attn_w), lambda i: (i, 0)),
        compiler_params=_params(("parallel",), 40 * 1024 * 1024),
    )(proj, proj, proj, proj, proj, proj, proj, qw_row, kw_row, gmat, sinks)


def _attention_bwd(proj, d_ag, dproj, qw_row, kw_row, gmat, sinks, *, attn_w, kv_w, name, rider=None):
    rows = proj.shape[0]
    nb = rows // BLOCK
    per_kv = attn_w // HEAD_DIM // N_KV_HEADS
    gw = per_kv * HEAD_DIM
    keys = 2 * BLOCK
    scale = 1.0 / math.sqrt(HEAD_DIM)
    w_out = 2 * attn_w + 2 * kv_w
    rev = lambda s: nb - 1 - s

    def body(q_ref, kp_ref, kc_ref, vp_ref, vc_ref, glo_ref, ghi_ref, dag_ref, qw_ref, kw_ref, gm_ref, sink_ref, _,
             dp_ref, gqw_ref, gkw_ref, gs_ref, carry_ref):
        step = pl.program_id(0)
        i = rev(step)
        kk = jnp.concatenate([kp_ref[...], kc_ref[...]], axis=0).astype(F32)
        vv = jnp.concatenate([vp_ref[...], vc_ref[...]], axis=0).astype(F32)
        gate = jnp.concatenate([glo_ref[...], ghi_ref[...]], axis=1).astype(F32)
        d_ag_v = dag_ref[...].astype(F32)
        qw, kw, gmat_v = qw_ref[...], kw_ref[...], gm_ref[...]
        qn, q_rstd, kn, k_rstd, groups = _attn_block(i, q_ref[...].astype(F32), kk, vv, qw, kw, gmat_v, sink_ref,
                                                     per_kv)
        lane = lax.broadcasted_iota(jnp.int32, (SUBLANES, LANES), 1)
        sub = lax.broadcasted_iota(jnp.int32, (SUBLANES, LANES), 0)
        gsink = jnp.zeros((SUBLANES, LANES), F32)
        dq_groups, dgate_groups, dk_heads, dv_heads = [], [], [], []
        for g, (kd, vd, qg, ps, p_sinks, pb, o) in enumerate(groups):
            cs = slice(g * gw, (g + 1) * gw)
            gate_g, d_ag_g = gate[:, cs], d_ag_v[:, cs]
            dgate_groups.append(d_ag_g * o * _dsilu(gate_g))
            do = (d_ag_g * _silu(gate_g)).astype(BF16)
            dp_all = _dot(do, vd, NT)
            dss = []
            for r in range(per_kv):
                p, dp = ps[r], dp_all[:, r * keys:(r + 1) * keys]
                delta = jnp.sum(p * dp, axis=-1, keepdims=True)
                dss.append(p * (dp - delta) * scale)
                gs_h = jnp.sum(-p_sinks[r] * delta, axis=0, keepdims=True)
                gsink = gsink + jnp.where((lane == g * per_kv + r) & (sub == 0), gs_h, 0.0)
            ds = jnp.concatenate(dss, axis=1).astype(BF16)
            dq_groups.append(_dot(ds, kd, NN))
            dk_heads.append(_fold_heads(_dot(ds, qg, TN), per_kv))
            dv_heads.append(_fold_heads(_dot(pb, do, TN), per_kv))
        dqh = jnp.concatenate(dq_groups, axis=1)
        gqw = jnp.sum(dqh * qn, axis=0, keepdims=True)
        dqn = dqh * qw
        dq = q_rstd * (dqn - qn * _head_mean(dqn * qn, gmat_v))
        dkh = _join_heads(dk_heads)
        gkw = jnp.sum(dkh * kn, axis=0, keepdims=True)
        dkn = dkh * kw
        dk = k_rstd * (dkn - kn * _head_mean(dkn * kn, gmat_v))
        dkv = jnp.concatenate([dk, _join_heads(dv_heads)], axis=1)

        @pl.when(step == 0)
        def _():
            carry_ref[...] = jnp.zeros_like(carry_ref)
            gqw_ref[...] = gqw
            gkw_ref[...] = gkw
            gs_ref[...] = gsink

        @pl.when(step > 0)
        def _():
            gqw_ref[...] += gqw
            gkw_ref[...] += gkw
            gs_ref[...] += gsink

        dp_ref[:, 0:attn_w] = dq.astype(BF16)
        dp_ref[:, attn_w:attn_w + 2 * kv_w] = (dkv[BLOCK:2 * BLOCK, :] + carry_ref[...]).astype(BF16)
        dp_ref[:, attn_w + 2 * kv_w:w_out] = jnp.concatenate(dgate_groups, axis=1).astype(BF16)
        carry_ref[...] = dkv[0:BLOCK, :]

    const = lambda a: pl.BlockSpec(a.shape, lambda s: (0, 0))
    res, landed = _call(
        body, [proj, proj, proj, proj, proj, proj, proj, d_ag, qw_row, kw_row, gmat, sinks, dproj], name=name,
        out_shape=[jax.ShapeDtypeStruct(dproj.shape, BF16),
                   jax.ShapeDtypeStruct(qw_row.shape, F32), jax.ShapeDtypeStruct(kw_row.shape, F32),
                   jax.ShapeDtypeStruct((SUBLANES, LANES), F32)],
        grid=(nb,),
        in_specs=_attn_specs(attn_w, kv_w, rev) + [pl.BlockSpec((BLOCK, attn_w), lambda s: (rev(s), 0)),
                                                   const(qw_row), const(kw_row), const(gmat),
                                                   pl.BlockSpec(memory_space=pltpu.SMEM), _any_spec()],
        out_specs=[pl.BlockSpec((BLOCK, w_out), lambda s: (rev(s), 0)),
                   const(qw_row), const(kw_row), pl.BlockSpec((SUBLANES, LANES), lambda s: (0, 0))],
        scratch_shapes=[pltpu.VMEM((BLOCK, 2 * kv_w), F32)],
        aliases={12: 0}, semantics=("arbitrary",), vmem=48 * 1024 * 1024, rider=rider)
    return res if rider is None else (res, landed)


def _cmul(ar, ai, br, bi):
    return ar * br - ai * bi, ar * bi + ai * br


def _ssm_prep(a_re, a_im, log_dt_col, steps, name):
    def body(are_ref, aim_ref, ldt_ref, abr_ref, abi_ref, cfr_ref, cfi_ref, apr_ref, api_ref, pwr_ref, pwi_ref):
        are, aim = are_ref[...], aim_ref[...]
        dt = jnp.exp(ldt_ref[...])
        mag = jnp.exp(dt * are)
        abr = mag * jnp.cos(dt * aim)
        abi = mag * jnp.sin(dt * aim)
        num_re, num_im = abr - 1.0, abi
        den = are * are + aim * aim
        abr_ref[...] = abr
        abi_ref[...] = abi
        cfr_ref[...] = (num_re * are + num_im * aim) / den
        cfi_ref[...] = (num_im * are - num_re * aim) / den
        pr, pi = jnp.ones_like(abr), jnp.zeros_like(abr)
        for k in range(steps):
            pwr_ref[k] = pr
            pwi_ref[k] = pi
            pr, pi = _cmul(pr, pi, abr, abi)
        apr_ref[...] = pr
        api_ref[...] = pi

    shp = jax.ShapeDtypeStruct(a_re.shape, F32)
    pows = jax.ShapeDtypeStruct((steps,) + a_re.shape, F32)
    return _pallas(body, name=name, out_shape=[shp] * 6 + [pows] * 2)(a_re, a_im, log_dt_col)


def _ssm_param_bwd(a_re, a_im, log_dt_col, d_ab_re, d_ab_im, b_re, b_im, dbt_re, dbt_im, name):
    def body(are_ref, aim_ref, ldt_ref, gabr_ref, gabi_ref, br_ref, bi_ref, tr_ref, ti_ref,
             dar_ref, dai_ref, dldt_ref, dbr_ref, dbi_ref):
        are, aim = are_ref[...], aim_ref[...]
        dt = jnp.exp(ldt_ref[...])
        mag = jnp.exp(dt * are)
        abr = mag * jnp.cos(dt * aim)
        abi = mag * jnp.sin(dt * aim)
        den = are * are + aim * aim
        cfr = ((abr - 1.0) * are + abi * aim) / den
        cfi = (abi * are - (abr - 1.0) * aim) / den
        gabr, gabi = jnp.sum(gabr_ref[...], axis=0), jnp.sum(gabi_ref[...], axis=0)
        t_re, t_im = tr_ref[...], ti_ref[...]
        g_r, g_i = _cmul(br_ref[...], -bi_ref[...], t_re, t_im)
        gcfr, gcfi = jnp.sum(g_r, axis=1), jnp.sum(g_i, axis=1)
        dbr, dbi = _cmul(cfr[:, None, :], -cfi[:, None, :], t_re, t_im)
        dbr_ref[...] = dbr
        dbi_ref[...] = dbi
        inv_r, inv_i = are / den, -aim / den
        t_r, t_i = _cmul(inv_r, -inv_i, gcfr, gcfi)
        gabr, gabi = gabr + t_r, gabi + t_i
        q_r, q_i = _cmul(cfr, cfi, inv_r, inv_i)
        da_r, da_i = _cmul(-q_r, q_i, gcfr, gcfi)
        gz_r, gz_i = _cmul(abr, -abi, gabr, gabi)
        dar_ref[...] = da_r + dt * gz_r
        dai_ref[...] = da_i + dt * gz_i
        dldt_ref[...] = dt * jnp.sum(are * gz_r + aim * gz_i, axis=-1, keepdims=True)

    shp = jax.ShapeDtypeStruct(a_re.shape, F32)
    bshp = jax.ShapeDtypeStruct(b_re.shape, F32)
    return _pallas(body, name=name,
                   out_shape=[shp, shp, jax.ShapeDtypeStruct(log_dt_col.shape, F32), bshp, bshp])(
        a_re, a_im, log_dt_col, d_ab_re, d_ab_im, b_re, b_im, dbt_re, dbt_im)


SCAN_LANES = 1024
SSM_CHUNK = 256
SCAN_UNROLL = 8
W_IN_PARTS = 2


def _scan_segments(xr_ref, xi_ref, a_re, a_im, ap_re, ap_im, pw_re, pw_im, carry_re, carry_im, cm_re, cm_im, steps,
                   reverse, base):
    n = xr_ref.shape[1]
    seg_order = range(SUBLANES - 1, -1, -1) if reverse else range(SUBLANES)
    sign = -1.0 if reverse else 1.0
    for c0 in range(0, n, SCAN_LANES):
        ls = slice(c0, c0 + SCAN_LANES)
        gs = slice(base + c0, base + c0 + SCAN_LANES)
        ar = jnp.broadcast_to(a_re[:, gs], (SUBLANES, SCAN_LANES))
        ai = jnp.broadcast_to(a_im[:, gs], (SUBLANES, SCAN_LANES))
        end_r = jnp.zeros((SUBLANES, SCAN_LANES), F32)
        end_i = jnp.zeros((SUBLANES, SCAN_LANES), F32)
        for j in range(steps):
            k = j if reverse else steps - 1 - j
            rws = slice(j * SUBLANES, (j + 1) * SUBLANES)
            tr, ti = _cmul(pw_re[k:k + 1, gs], sign * pw_im[k:k + 1, gs], xr_ref[rws, ls], xi_ref[rws, ls])
            end_r, end_i = end_r + tr, end_i + ti
        cr, ci = carry_re[:, gs], carry_im[:, gs]
        apr, api = ap_re[:, gs], ap_im[:, gs]
        for r in seg_order:
            cm_re[r:r + 1, gs] = cr
            cm_im[r:r + 1, gs] = ci
            tr, ti = _cmul(apr, api, cr, ci)
            cr, ci = end_r[r:r + 1, :] + tr, end_i[r:r + 1, :] + ti
        carry_re[:, gs] = cr
        carry_im[:, gs] = ci

        def run(t, s, ar=ar, ai=ai, ls=ls):
            sr, si = s
            for k in range(SCAN_UNROLL):
                j = steps - 1 - (t * SCAN_UNROLL + k) if reverse else t * SCAN_UNROLL + k
                r0 = pl.multiple_of(j * SUBLANES, SUBLANES)
                sr, si = _cmul(ar, ai, sr, si)
                sr = sr + xr_ref[pl.ds(r0, SUBLANES), ls]
                si = si + xi_ref[pl.ds(r0, SUBLANES), ls]
                xr_ref[pl.ds(r0, SUBLANES), ls] = sr
                xi_ref[pl.ds(r0, SUBLANES), ls] = si
            return sr, si

        assert steps % SCAN_UNROLL == 0
        lax.fori_loop(0, steps // SCAN_UNROLL, run, (cm_re[:, gs], cm_im[:, gs]))


SB_GROUPS = MXU_DIM // GROUP
SB_STATE = SB_GROUPS * STATE


def _ssm_rows(m):
    flat = m.reshape(-1, STATE).astype(F32)
    return jnp.concatenate([flat, flat], axis=1)


def _from_ssm_rows(rows):
    return rows[:, :STATE].reshape(-1, GROUP, STATE)


def _own_group(shape):
    row_g = lax.broadcasted_iota(jnp.int32, shape, 0) // GROUP
    col_g = lax.broadcasted_iota(jnp.int32, shape, 1) // STATE
    return row_g == col_g


def _block_diagonal(rows):
    tiled = jnp.concatenate([rows] * (SB_STATE // LANES), axis=1)
    return jnp.where(_own_group(tiled.shape), tiled, 0.0).astype(BF16)


def _block_rows(acc):
    x = jnp.where(_own_group(acc.shape), acc, 0.0)
    while x.shape[1] > LANES:
        half = x.shape[1] // 2
        x = x[:, :half] + x[:, half:]
    return x + pltpu.roll(x, STATE, axis=1)


def _rows_to_segments(dst, srcs, steps, stage):
    for ref, off in srcs:
        for k in range(ref.shape[1] // LANES):
            stage[off // LANES + k] = ref[:, k * LANES:(k + 1) * LANES].astype(F32)
    for k in range(dst.shape[1] // LANES):
        for j in range(steps):
            dst[j * SUBLANES:(j + 1) * SUBLANES, k * LANES:(k + 1) * LANES] = (
                stage[k, pl.ds(j, SUBLANES, stride=steps), :])


def _segments_to_rows(dst, src, steps, stage):
    for k in range(src.shape[1] // LANES):
        for j in range(steps):
            stage[k, pl.ds(j, SUBLANES, stride=steps), :] = (
                src[j * SUBLANES:(j + 1) * SUBLANES, k * LANES:(k + 1) * LANES])
    for k in range(src.shape[1] // LANES):
        dst[:, k * LANES:(k + 1) * LANES] = stage[k]


def _u_specs(w, o_u, chunk, index):
    half = w // 2
    assert o_u % half == 0
    return [pl.BlockSpec((chunk, half), lambda c, k=k: (index(c), o_u // half + k)) for k in range(2)]


def _ssm_fwd(proj, o_u, bc_rows, rows_p, d_row, *, chunk, name, rider=None):
    rows = proj.shape[0]
    w = d_row.shape[1]
    nc = rows // chunk
    steps = chunk // SUBLANES
    nsb = w // MXU_DIM
    n_state = nsb * SB_STATE

    def body(ulo_ref, uhi_ref, b2r_ref, b2i_ref, c2r_ref, c2i_ref, abr_ref, abi_ref, cfr_ref, cfi_ref, apr_ref,
             api_ref, pwr_ref, pwi_ref, d_ref, y_ref, str_ref, sti_ref, yg_ref, bre_ref, bim_ref, cre_ref, cim_ref,
             useg, yseg, stage, sr, si,
             carry_r, carry_i, cm_r, cm_i):
        @pl.when(pl.program_id(0) == 0)
        def _():
            for src, dst in ((b2r_ref, bre_ref), (b2i_ref, bim_ref), (c2r_ref, cre_ref), (c2i_ref, cim_ref)):
                for sb in range(nsb):
                    dst[sb] = _block_diagonal(src[sb * MXU_DIM:(sb + 1) * MXU_DIM, :])
            carry_r[...] = jnp.zeros_like(carry_r)
            carry_i[...] = jnp.zeros_like(carry_i)

        str_ref[0] = carry_r[...]
        sti_ref[0] = carry_i[...]
        _rows_to_segments(useg, [(ulo_ref, 0), (uhi_ref, w // 2)], steps, stage)
        for sb in range(nsb):
            us = slice(sb * MXU_DIM, (sb + 1) * MXU_DIM)
            ss = slice(sb * SB_STATE, (sb + 1) * SB_STATE)
            ub = useg[:, us].astype(BF16)
            bur = _dot(ub, bre_ref[sb], NN)
            bui = _dot(ub, bim_ref[sb], NN)
            xr, xi = _cmul(cfr_ref[:, ss], cfi_ref[:, ss], bur, bui)
            sr[...] = xr
            si[...] = xi
            _scan_segments(sr, si, abr_ref[...], abi_ref[...], apr_ref[...], api_ref[...], pwr_ref, pwi_ref,
                           carry_r, carry_i, cm_r, cm_i, steps, False, sb * SB_STATE)
            y = _dot(sr[...].astype(BF16), cre_ref[sb], NT) - _dot(si[...].astype(BF16), cim_ref[sb], NT)
            yseg[:, us] = y + d_ref[:, us] * useg[:, us]
        _segments_to_rows(y_ref, yseg, steps, stage)
        yg_ref[...] = _gelu(y_ref[...]).astype(BF16)

    const = lambda a: pl.BlockSpec(a.shape, lambda c: (0,) * a.ndim)
    row_n = pl.BlockSpec((1, n_state), lambda c: (0, 0))
    st = pl.BlockSpec((1, 1, n_state), lambda c: (c, 0, 0))
    held = [pltpu.VMEM((nsb, MXU_DIM, SB_STATE), BF16)] * 4
    vmem = (4 * _nbytes((nsb, MXU_DIM, SB_STATE), BF16) + 4 * _nbytes((chunk, SB_STATE), F32)
            + 12 * _nbytes((chunk, w), F32) + 8 * _nbytes(bc_rows[0].shape, F32))
    res, landed = _call(
        body, [proj, proj, *bc_rows, *rows_p, d_row], name=name,
        out_shape=[jax.ShapeDtypeStruct((rows, w), F32), jax.ShapeDtypeStruct((nc, 1, n_state), F32),
                   jax.ShapeDtypeStruct((nc, 1, n_state), F32), jax.ShapeDtypeStruct((rows, w), BF16)],
        grid=(nc,),
        in_specs=_u_specs(w, o_u, chunk, lambda c: c) + [const(b) for b in bc_rows]
        + [row_n] * 6 + [pl.BlockSpec((steps, n_state), lambda c: (0, 0))] * 2 + [pl.BlockSpec((1, w), lambda c: (0, 0))],
        out_specs=[pl.BlockSpec((chunk, w), lambda c: (c, 0)), st, st, pl.BlockSpec((chunk, w), lambda c: (c, 0))],
        scratch_shapes=held + [pltpu.VMEM((chunk, w), F32), pltpu.VMEM((chunk, w), F32),
                               pltpu.VMEM((w // LANES, chunk, LANES), F32),
                               pltpu.VMEM((chunk, SB_STATE), F32), pltpu.VMEM((chunk, SB_STATE), F32),
                               pltpu.VMEM((1, n_state), F32), pltpu.VMEM((1, n_state), F32),
                               pltpu.VMEM((SUBLANES, n_state), F32), pltpu.VMEM((SUBLANES, n_state), F32)],
        semantics=("arbitrary",), vmem=vmem, rider=rider)
    return res if rider is None else (res, landed)


def _ssm_bwd(proj, o_u, y, dyg, st_re, st_im, bc_rows, rows_p, d_row, *, chunk, name, rider=None):
    rows = proj.shape[0]
    w = d_row.shape[1]
    nc = rows // chunk
    steps = chunk // SUBLANES
    nsb = w // MXU_DIM
    n_state = nsb * SB_STATE

    def body(ulo_ref, uhi_ref, y_ref, dyg_ref, str_ref, sti_ref, b2r_ref, b2i_ref, c2r_ref, c2i_ref, t2r_ref,
             t2i_ref, abr_ref, abi_ref, cfr_ref, cfi_ref, apr_ref, api_ref, pwr_ref, pwi_ref, d_ref,
             du_ref, gb2r_ref, gb2i_ref, gc2r_ref, gc2i_ref, gabr_ref, gabi_ref, dd_ref,
             bre_ref, bim_ref, cre_ref, cim_ref, btr_ref, bti_ref, dbre_ref, dbim_ref, dcre_ref, dcim_ref,
             useg, dyseg, dynat, stage, sr, si, lr, li, carry_r, carry_i, lam_r, lam_i, cm_r, cm_i, cl_r, cl_i):
        first = pl.program_id(0) == 0

        @pl.when(first)
        def _():
            for src, dst in ((b2r_ref, bre_ref), (b2i_ref, bim_ref), (c2r_ref, cre_ref), (c2i_ref, cim_ref),
                             (t2r_ref, btr_ref), (t2i_ref, bti_ref)):
                for sb in range(nsb):
                    dst[sb] = _block_diagonal(src[sb * MXU_DIM:(sb + 1) * MXU_DIM, :])
            lam_r[...] = jnp.zeros_like(lam_r)
            lam_i[...] = jnp.zeros_like(lam_i)
            for ref in (dbre_ref, dbim_ref, dcre_ref, dcim_ref, gabr_ref, gabi_ref, dd_ref):
                ref[...] = jnp.zeros_like(ref)

        dynat[...] = dyg_ref[...].astype(F32) * _dgelu(y_ref[...])
        half = w // 2
        dd_ref[:, :half] += jnp.sum(dynat[:, :half] * ulo_ref[...].astype(F32), axis=0, keepdims=True)
        dd_ref[:, half:] += jnp.sum(dynat[:, half:] * uhi_ref[...].astype(F32), axis=0, keepdims=True)
        _rows_to_segments(useg, [(ulo_ref, 0), (uhi_ref, half)], steps, stage)
        _rows_to_segments(dyseg, [(dynat, 0)], steps, stage)
        dy = dyseg[...]
        dyb = dy.astype(BF16)
        ub = useg[...].astype(BF16)
        carry_r[...] = str_ref[0]
        carry_i[...] = sti_ref[0]
        abr, abi = abr_ref[...], abi_ref[...]
        apr, api = apr_ref[...], api_ref[...]
        for sb in range(nsb):
            us = slice(sb * MXU_DIM, (sb + 1) * MXU_DIM)
            ss = slice(sb * SB_STATE, (sb + 1) * SB_STATE)
            base = sb * SB_STATE
            br = _dot(ub[:, us], bre_ref[sb], NN)
            bi = _dot(ub[:, us], bim_ref[sb], NN)
            xr, xi = _cmul(cfr_ref[:, ss], cfi_ref[:, ss], br, bi)
            sr[...] = xr
            si[...] = xi
            lr[...] = _dot(dyb[:, us], cre_ref[sb], NN)
            li[...] = -_dot(dyb[:, us], cim_ref[sb], NN)
            _scan_segments(sr, si, abr, abi, apr, api, pwr_ref, pwi_ref, carry_r, carry_i, cm_r, cm_i, steps, False,
                           base)
            dcre_ref[sb] += _dot(dyb[:, us], sr[...].astype(BF16), TN)
            dcim_ref[sb] -= _dot(dyb[:, us], si[...].astype(BF16), TN)
            _scan_segments(lr, li, abr, -abi, apr, -api, pwr_ref, pwi_ref, lam_r, lam_i, cl_r, cl_i, steps, True,
                           base)
            for c0 in range(0, SB_STATE, SCAN_LANES):
                ls = slice(c0, c0 + SCAN_LANES)
                gs = slice(base + c0, base + c0 + SCAN_LANES)

                def step(t, acc, ls=ls):
                    gar, gai, pr, pi = acc
                    for k in range(SCAN_UNROLL):
                        r0 = pl.multiple_of((t * SCAN_UNROLL + k) * SUBLANES, SUBLANES)
                        rws = pl.ds(r0, SUBLANES)
                        t_r, t_i = _cmul(pr, -pi, lr[rws, ls], li[rws, ls])
                        gar, gai, pr, pi = gar + t_r, gai + t_i, sr[rws, ls], si[rws, ls]
                    return gar, gai, pr, pi

                zero = jnp.zeros((SUBLANES, SCAN_LANES), F32)
                gar, gai, _, _ = lax.fori_loop(0, steps // SCAN_UNROLL, step,
                                               (zero, zero, cm_r[:, gs], cm_i[:, gs]))
                gabr_ref[:, gs] += gar
                gabi_ref[:, gs] += gai
            xr, xi = lr[...].astype(BF16), li[...].astype(BF16)
            du = _dot(xr, btr_ref[sb], NT) + _dot(xi, bti_ref[sb], NT)
            useg[:, us] = du + d_ref[:, us] * dy[:, us]
            dbre_ref[sb] += _dot(ub[:, us], xr, TN)
            dbim_ref[sb] += _dot(ub[:, us], xi, TN)
        _segments_to_rows(du_ref, useg, steps, stage)

        @pl.when(pl.program_id(0) == nc - 1)
        def _():
            for src, dst in ((dbre_ref, gb2r_ref), (dbim_ref, gb2i_ref), (dcre_ref, gc2r_ref), (dcim_ref, gc2i_ref)):
                for sb in range(nsb):
                    dst[sb * MXU_DIM:(sb + 1) * MXU_DIM, :] = _block_rows(src[sb])

    rev = lambda c: nc - 1 - c
    const = lambda a: pl.BlockSpec(a.shape, lambda c: (0,) * a.ndim)
    tile = pl.BlockSpec((chunk, w), lambda c: (rev(c), 0))
    row_n = pl.BlockSpec((1, n_state), lambda c: (0, 0))
    row_w = pl.BlockSpec((1, w), lambda c: (0, 0))
    st = pl.BlockSpec((1, 1, n_state), lambda c: (rev(c), 0, 0))
    acc8 = pl.BlockSpec((SUBLANES, n_state), lambda c: (0, 0))
    big = pltpu.VMEM((chunk, SB_STATE), F32)
    small = pltpu.VMEM((chunk, w), F32)
    row = pltpu.VMEM((1, n_state), F32)
    eight = pltpu.VMEM((SUBLANES, n_state), F32)
    blk = (nsb, MXU_DIM, SB_STATE)
    held = [pltpu.VMEM(blk, BF16)] * 6 + [pltpu.VMEM(blk, F32)] * 4
    vmem = (6 * _nbytes(blk, BF16) + 4 * _nbytes(blk, F32) + 5 * _nbytes((chunk, SB_STATE), F32)
            + 12 * _nbytes((chunk, w), F32) + 20 * _nbytes(bc_rows[0].shape, F32))
    res, landed = _call(
        body, [proj, proj, y, dyg, st_re, st_im, *bc_rows, *rows_p, d_row], name=name,
        out_shape=[jax.ShapeDtypeStruct((rows, w), F32)] + [jax.ShapeDtypeStruct(b.shape, F32) for b in bc_rows[:4]]
        + [jax.ShapeDtypeStruct((SUBLANES, n_state), F32)] * 2 + [jax.ShapeDtypeStruct((1, w), F32)],
        grid=(nc,),
        in_specs=_u_specs(w, o_u, chunk, rev) + [tile, tile, st, st] + [const(b) for b in bc_rows]
        + [row_n] * 6 + [pl.BlockSpec((steps, n_state), lambda c: (0, 0))] * 2 + [row_w],
        out_specs=[tile] + [const(b) for b in bc_rows[:4]] + [acc8] * 2 + [row_w],
        scratch_shapes=held + [small] * 3 + [pltpu.VMEM((w // LANES, chunk, LANES), F32)] + [big] * 4 + [row] * 4
        + [eight] * 4,
        semantics=("arbitrary",), vmem=vmem, rider=rider)
    return res if rider is None else (res, landed)


def _out_proj_loss(merged, w_o, x, target, name):
    rows, d = x.shape
    tm, tn = _tile(rows, 1024, SUBLANES), _tile(d, 1024)

    def body(a_ref, b_ref, x_ref, t_ref, g_ref, gb_ref, l_ref):
        err = x_ref[...] + _dot(a_ref[...], b_ref[...], NN) - t_ref[...]
        g = err * (1.0 / d)
        g_ref[...] = g
        gb_ref[...] = g.astype(BF16)
        part = jnp.sum(0.5 * err * g, axis=0, keepdims=True)
        first = pl.program_id(1) == 0

        @pl.when(first)
        def _():
            l_ref[...] = part

        @pl.when(jnp.logical_not(first))
        def _():
            l_ref[...] += part

    tile = pl.BlockSpec((tm, tn), lambda j, i: (i, j))
    vmem = 2 * (_nbytes((tm, d), BF16) + _nbytes((d, tn), BF16)) + 12 * _nbytes((tm, tn), F32)
    return _pallas(
        body, name=name,
        out_shape=[jax.ShapeDtypeStruct((rows, d), F32), jax.ShapeDtypeStruct((rows, d), BF16),
                   jax.ShapeDtypeStruct((1, d), F32)],
        grid=(d // tn, rows // tm),
        in_specs=[pl.BlockSpec((tm, d), lambda j, i: (i, 0)), pl.BlockSpec((d, tn), lambda j, i: (0, j)), tile, tile],
        out_specs=[tile, tile, pl.BlockSpec((1, tn), lambda j, i: (0, j))],
        compiler_params=_params(("parallel", "arbitrary"), vmem),
    )(merged, w_o, x, target)


def _pair_sum(grad, recv, name):
    r4, cdim = recv.shape
    r = r4 // N_CHIPS
    tr = _tile(r, 544, 16)
    g4 = grad.reshape(N_CHIPS, 2, r, cdim)
    r3 = recv.reshape(N_CHIPS, r, cdim)
    core = jnp.reshape(lax.axis_index("c"), (1,)).astype(jnp.int32)

    def body(c_ref, g_ref, r_ref, o_ref):
        o_ref[...] = (g_ref[0] + r_ref[...]).astype(BF16)

    out = _pallas(
        body, name=name, out_shape=jax.ShapeDtypeStruct((N_CHIPS, r, cdim), BF16),
        grid_spec=pltpu.PrefetchScalarGridSpec(
            num_scalar_prefetch=1, grid=(N_CHIPS, r // tr),
            in_specs=[pl.BlockSpec((1, 1, tr, cdim), lambda j, i, c: (j, c[0], i, 0)),
                      pl.BlockSpec((1, tr, cdim), lambda j, i, c: (j, i, 0))],
            out_specs=pl.BlockSpec((1, tr, cdim), lambda j, i, c: (j, i, 0))),
        compiler_params=_params(("parallel", "parallel"), 6 * _nbytes((tr, cdim), F32)),
    )(core, g4, r3)
    return out.reshape(r4, cdim)


def _chip_sum(recv, name):
    r4, cdim = recv.shape
    r = r4 // N_CHIPS
    tr = _tile(r, 544, 16)
    r3 = recv.reshape(N_CHIPS, r, cdim)

    def body(r_ref, o_ref):
        acc = r_ref[0].astype(F32)
        for j in range(1, N_CHIPS):
            acc = acc + r_ref[j].astype(F32)
        o_ref[...] = acc

    return _pallas(
        body, name=name, out_shape=jax.ShapeDtypeStruct((r, cdim), F32), grid=(r // tr,),
        in_specs=[pl.BlockSpec((N_CHIPS, tr, cdim), lambda i: (0, i, 0))],
        out_specs=pl.BlockSpec((tr, cdim), lambda i: (i, 0)),
        compiler_params=_params(("parallel",), 8 * _nbytes((tr, cdim), F32)),
    )(r3)


def _adamw_math(w, g, m, v):
    m = ADAM_B1 * m + (1.0 - ADAM_B1) * g
    v = ADAM_B2 * v + (1.0 - ADAM_B2) * (g * g)
    m_hat = m / (1.0 - ADAM_B1 ** ADAM_STEP)
    v_hat = v / (1.0 - ADAM_B2 ** ADAM_STEP)
    delta = -ADAM_LR * (m_hat / (jnp.sqrt(v_hat) + ADAM_EPS) + ADAM_WD * w)
    return delta, m, v


def _adamw(w, g, m, v, name):
    rows, rest = w.shape[0], w.shape[1:]
    tr = _tile(rows, 256, SUBLANES) if len(rest) == 1 else SUBLANES
    assert rows % tr == 0

    def body(w_ref, g_ref, m_ref, v_ref, d_ref, nm_ref, nv_ref):
        d, nm, nv = _adamw_math(w_ref[...], g_ref[...], m_ref[...], v_ref[...])
        d_ref[...] = d
        nm_ref[...] = nm
        nv_ref[...] = nv

    spec = pl.BlockSpec((tr,) + rest, lambda i: (i,) + (0,) * len(rest))
    shp = jax.ShapeDtypeStruct(w.shape, F32)
    return _pallas(
        body, name=name, out_shape=[shp] * 3, grid=(rows // tr,), in_specs=[spec] * 4, out_specs=[spec] * 3,
        compiler_params=_params(("parallel",)),
    )(w, g, m, v)


def _adamw_chips(w, parts, m, v, name):
    rows, cols = w.shape
    assert sum(p.shape[1] for p in parts) == cols
    tr = _tile(rows, 64, 16)
    n = len(parts)

    def body(*refs):
        w_ref, m_ref, v_ref = refs[0], refs[1 + n], refs[2 + n]
        g_ref, d_ref, nm_ref, nv_ref = refs[3 + n:]
        cols_g = []
        for p_ref in refs[1:1 + n]:
            acc = p_ref[0].astype(F32)
            for j in range(1, N_CHIPS):
                acc = acc + p_ref[j].astype(F32)
            cols_g.append(acc)
        g = cols_g[0] if n == 1 else jnp.concatenate(cols_g, axis=1)
        d, nm, nv = _adamw_math(w_ref[...], g, m_ref[...], v_ref[...])
        g_ref[...] = g
        d_ref[...] = d
        nm_ref[...] = nm
        nv_ref[...] = nv

    spec = pl.BlockSpec((tr, cols), lambda i: (i, 0))
    part_specs = [pl.BlockSpec((N_CHIPS, tr, p.shape[1]), lambda i: (0, i, 0)) for p in parts]
    shp = jax.ShapeDtypeStruct((rows, cols), F32)
    return _pallas(
        body, name=name, out_shape=[shp] * 4, grid=(rows // tr,),
        in_specs=[spec] + part_specs + [spec, spec], out_specs=[spec] * 4,
        compiler_params=_params(("parallel",)),
    )(w, *[p.reshape(N_CHIPS, rows, p.shape[1]) for p in parts], m, v)


def _device_sum(parts, name):
    rows = parts.shape[0] // N_DEV

    def body(p_ref, g_ref):
        g = p_ref[0]
        for k in range(1, N_DEV):
            g = g + p_ref[k]
        g_ref[...] = g

    return _pallas(body, name=name, out_shape=jax.ShapeDtypeStruct((rows, LANES), F32))(
        parts.reshape(N_DEV, rows, LANES))


def _vmem_footprint(shape):
    dims = (1,) * (2 - len(shape)) + tuple(shape)
    padded = dims[:-2] + (-(-dims[-2] // SUBLANES) * SUBLANES, -(-dims[-1] // LANES) * LANES)
    return _nbytes(padded, F32)


def _adamw_native(ws, gs, ms, vs, name):
    n = len(ws)

    def body(*refs):
        ins, outs = refs[:4 * n], refs[4 * n:]
        for i in range(n):
            d, nm, nv = _adamw_math(ins[i][...], ins[n + i][...], ins[2 * n + i][...], ins[3 * n + i][...])
            outs[i][...] = d
            outs[n + i][...] = nm
            outs[2 * n + i][...] = nv

    shapes = [jax.ShapeDtypeStruct(w.shape, F32) for w in ws] * 3
    vmem = 8 * sum(_vmem_footprint(w.shape) for w in ws)
    res = _pallas(body, name=name, out_shape=shapes, compiler_params=_params(vmem=vmem))(*ws, *gs, *ms, *vs)
    return res[:n], res[n:2 * n], res[2 * n:]


SMALL = ("norm_w", "q_norm_w", "k_norm_w", "sinks", "A_re", "A_im", "log_dt", "B_re", "B_im", "C_re", "C_im",
         "D_skip", "b_glu")
LARGE = ("w_in", "w_attn_proj", "w_glu", "w_ssm_proj", "w_out")
ORDER = ("norm_w", "w_in", "q_norm_w", "k_norm_w", "sinks", "w_attn_proj", "A_re", "A_im", "log_dt", "B_re", "B_im",
         "C_re", "C_im", "D_skip", "w_glu", "b_glu", "w_ssm_proj", "w_out")


SMALL_REST = ("loss",) + SMALL[1:]


def _pack(named, keys):
    flat = jnp.concatenate([named[k].reshape(-1).astype(F32) for k in keys])
    n = flat.shape[0]
    rows = -(-n // (LANES * SUBLANES)) * SUBLANES
    return jnp.pad(flat, (0, rows * LANES - n)).reshape(rows, LANES)


def _unpack(packed, like, keys):
    flat = packed.reshape(-1)
    out, o = {}, 0
    for k in keys:
        n = like[k].size
        out[k] = flat[o:o + n].reshape(like[k].shape)
        o += n
    return out


def _step(xs, target, p, shards):
    s_in, s_ap, s_glu, s_sp, s_o = shards
    seq, d = xs.shape
    attn_w = (d // 128) * HEAD_DIM
    n_q = attn_w // HEAD_DIM
    kv_w = N_KV_HEADS * HEAD_DIM
    ssm_w = d // 2
    n_groups = ssm_w // GROUP
    n_state = n_groups * STATE
    in_w = N_DEV * s_in.shape[0]
    assert in_w == 2 * attn_w + 2 * kv_w + 2 * ssm_w + 2 * d
    o_u = 2 * attn_w + 2 * kv_w
    o_z = o_u + ssm_w
    o_ga = o_z + ssm_w
    chunk = min(SSM_CHUNK, seq)
    cw = d // 4

    norm_row = p["norm_w"].reshape(1, d)
    half = d // W_IN_PARTS
    assert W_IN_PARTS == 2
    s_in_parts = [s_in[:, :half], s_in[:, half:]]
    h, (w_lo,) = _rmsnorm_fwd(xs, norm_row, "rmsnorm_fwd", rider=_all_gather(s_in_parts[:1]))
    part, (w_hi,) = _matmul(Cols(h, 0, half), w_lo, mode="nt", name="in_proj_0", tn=2176, out_dtype=BF16,
                            rider=_all_gather(s_in_parts[1:]))
    proj = _matmul(Cols(h, half, half), w_hi, mode="nt", name="in_proj_1", tn=2176, out_dtype=BF16, add=part)
    w_in_parts = [w_lo, w_hi]
    qw_row = jnp.tile(p["q_norm_w"], n_q).reshape(1, attn_w)
    kw_row = jnp.tile(p["k_norm_w"], N_KV_HEADS).reshape(1, kv_w)
    gmat = _head_mean_matrix()
    ag = _attention_fwd(proj, qw_row, kw_row, gmat, p["sinks"], attn_w=attn_w, kv_w=kv_w, name="attention_fwd")

    log_dt_col = p["log_dt"].reshape(n_groups, 1)
    prep = _ssm_prep(p["A_re"], p["A_im"], log_dt_col, chunk // SUBLANES, "ssm_prep")
    rows_p = [v.reshape(1, n_state) for v in prep[:6]] + [v.reshape(-1, n_state) for v in prep[6:]]
    bt_re, bt_im = p["B_re"].transpose(0, 2, 1), p["B_im"].transpose(0, 2, 1)
    cf_re, cf_im = prep[2][:, None, :], prep[3][:, None, :]
    bc_rows = [_ssm_rows(m) for m in (bt_re, bt_im, p["C_re"], p["C_im"],
                                      cf_re * bt_re - cf_im * bt_im, cf_re * bt_im + cf_im * bt_re)]
    d_row = p["D_skip"].reshape(1, ssm_w)
    (y_ssm, st_re, st_im, yg), (w_ap_t, w_glu_t, w_sp_t, w_o) = _ssm_fwd(
        proj, o_u, bc_rows[:4], rows_p, d_row, chunk=chunk, name="ssm_fwd",
        rider=_all_gather([s_ap, s_glu, s_sp, s_o]))
    glu = _matmul(yg, w_glu_t, mode="nt", name="glu_proj", out_dtype=BF16, bias=p["b_glu"].reshape(1, 2 * ssm_w))
    (ts,) = _ew(lambda ga, gb, z: ga * _sigmoid(gb) * _silu(z), name="glu_gate", rows=seq, width=ssm_w,
                tiles=[(glu, 0), (glu, ssm_w), (proj, o_z)], outs=[(BF16, ssm_w, 0)], cw=cw)
    yy = _matmul(ag, w_ap_t, mode="nt", name="attn_proj", out_dtype=BF16, out_cols=(2 * d, 0))
    yy = _matmul(ts, w_sp_t, mode="nt", name="ssm_proj", out_dtype=BF16, out_cols=(2 * d, d), into=yy)
    (merged,) = _ew(lambda ya, ys, ga, gs: _sigmoid(ga) * ya + _sigmoid(gs) * ys, name="merge", rows=seq, width=d,
                    tiles=[(yy, 0), (yy, d), (proj, o_ga), (proj, o_ga + d)], outs=[(BF16, d, 0)], cw=cw)
    dout, dout_b, loss_cols = _out_proj_loss(merged, w_o, xs, target, "out_proj_loss")
    loss_local = jnp.sum(loss_cols)

    g_w_o = _matmul(merged, dout_b, mode="tn", name="grad_w_out", tm=512, tk=4096)
    dmerged = _matmul(dout_b, w_o, mode="nt", name="d_merged", out_dtype=BF16)

    def merge_bwd(dm, y, g):
        s = _sigmoid(g)
        return dm * s, dm * y * s * (1.0 - s)

    dyy, dproj = _ew(merge_bwd, name="merge_bwd", rows=seq, width=2 * d,
                     tiles=[(dmerged, 0, d), (yy, 0), (proj, o_ga)],
                     outs=[(BF16, 2 * d, 0), (BF16, in_w, o_ga)], cw=cw)
    dy_a, dy_s = Cols(dyy, 0, d), Cols(dyy, d, d)
    g_w_ap_t = _matmul(dy_a, ag, mode="tn", name="grad_w_attn_proj", tm=512, tk=4096)
    g_w_sp_t = _matmul(dy_s, ts, mode="tn", name="grad_w_ssm_proj", tm=512, tk=4096)
    d_ag = _matmul(dy_a, w_ap_t, mode="nn", name="d_attn_gated", out_dtype=BF16)
    d_ts = _matmul(dy_s, w_sp_t, mode="nn", name="d_ssm_gated", out_dtype=BF16)

    (dproj, g_qw, g_kw, g_sinks), (sib_o, sib_ap, sib_sp) = _attention_bwd(
        proj, d_ag, dproj, qw_row, kw_row, gmat, p["sinks"], attn_w=attn_w, kv_w=kv_w, name="attention_bwd",
        rider=_sibling_exchange([g_w_o, g_w_ap_t, g_w_sp_t]))
    pair_o = _pair_sum(g_w_o, sib_o, "pair_sum_w_out")
    pair_ap = _pair_sum(g_w_ap_t, sib_ap, "pair_sum_w_attn_proj")
    pair_sp = _pair_sum(g_w_sp_t, sib_sp, "pair_sum_w_ssm_proj")

    n_half = ssm_w // _tile(2 * ssm_w, cw)

    def glu_bwd(j, dt, ga, gb, z):
        sb, sz = _sigmoid(gb), _silu(z)
        dg = jnp.where(j < n_half, dt * sb * sz, dt * ga * sb * (1.0 - sb) * sz)
        return dg, dg

    glu_ops = [(d_ts, 0, ssm_w), (glu, 0, ssm_w), (glu, ssm_w, ssm_w), (proj, o_z, ssm_w)]
    dglu, g_bglu = _ew(glu_bwd, name="glu_bwd", rows=seq, width=2 * ssm_w, tiles=glu_ops,
                       outs=[(BF16, 2 * ssm_w, 0)], accs=1, cw=cw, with_col=True)
    (dproj,) = _ew(lambda dt, ga, gb, z: dt * ga * _sigmoid(gb) * _dsilu(z), name="glu_bwd_z", rows=seq,
                   width=ssm_w, tiles=glu_ops, outs=[(BF16, in_w, o_z)], into=[dproj], cw=cw)
    g_w_glu_t = _matmul(dglu, yg, mode="tn", name="grad_w_glu", tm=512, tk=4096)
    d_yg = _matmul(dglu, w_glu_t, mode="nn", name="d_gelu", out_dtype=BF16)
    ((du, dbt_re, dbt_im, dc_re, dc_im, gabr, gabi, g_d), (chips_o, chips_ap, chips_sp, sib_glu)) = _ssm_bwd(
        proj, o_u, y_ssm, d_yg, st_re, st_im, bc_rows, rows_p, d_row, chunk=chunk, name="ssm_bwd",
        rider=_join(_chip_exchange([pair_o, pair_ap, pair_sp]), _sibling_exchange([g_w_glu_t])))
    pair_glu = _pair_sum(g_w_glu_t, sib_glu, "pair_sum_w_glu")
    (dproj,) = _ew(lambda v: v, name="du_store", rows=seq, width=ssm_w, tiles=[(du, 0)],
                   outs=[(BF16, in_w, o_u)], into=[dproj], cw=cw)
    g_a_re, g_a_im, g_log_dt, g_bt_re, g_bt_im = _ssm_param_bwd(
        p["A_re"], p["A_im"], log_dt_col, *[g.reshape(SUBLANES, n_groups, STATE) for g in (gabr, gabi)],
        bt_re, bt_im, _from_ssm_rows(dbt_re), _from_ssm_rows(dbt_im), "ssm_param_bwd")
    small_grads = dict(
        loss=loss_local, q_norm_w=g_qw.reshape(n_q, HEAD_DIM).sum(0), k_norm_w=g_kw.reshape(N_KV_HEADS, HEAD_DIM).sum(0),
        sinks=g_sinks[0, :n_q], A_re=g_a_re, A_im=g_a_im, log_dt=g_log_dt.reshape(n_groups),
        B_re=g_bt_re.transpose(0, 2, 1), B_im=g_bt_im.transpose(0, 2, 1),
        C_re=_from_ssm_rows(dc_re), C_im=_from_ssm_rows(dc_im),
        D_skip=g_d.reshape(n_groups, GROUP), b_glu=g_bglu.reshape(2 * ssm_w))

    n_parts = W_IN_PARTS
    wq = d // n_parts
    g_parts, pair_parts, chip_parts = [], [], []
    extra = [_chip_exchange([pair_glu]), _all_gather([_pack(small_grads, SMALL_REST)])]
    chips_glu = small_parts = dh = None
    for step in range(n_parts + 2):
        riders = list(extra) if step == 0 else []
        if 0 <= step - 2 < n_parts:
            riders.append(_chip_exchange([pair_parts[step - 2]]))
        if 0 <= step - 1 < n_parts:
            riders.append(_sibling_exchange([g_parts[step - 1]]))
        rider = _join(*riders) if riders else None
        if step < n_parts:
            res = _matmul(dproj, Cols(h, step * wq, wq), mode="tn", name="grad_w_in_%d" % step, tk=4096, rider=rider)
            out, landed = res if rider is not None else (res, [])
            g_parts.append(out)
        else:
            q = step - n_parts
            dh, landed = _matmul(dproj, w_in_parts[q], mode="nn", name="d_normed_%d" % q, tk=2176,
                                 out_cols=(d, q * wq), into=dh, rider=rider)
        landed = list(landed)
        if step == 0:
            chips_glu, small_parts = landed[:2]
            landed = landed[2:]
        if 0 <= step - 2 < n_parts:
            chip_parts.append(landed.pop(0))
        if 0 <= step - 1 < n_parts:
            pair_parts.append(_pair_sum(g_parts[step - 1], landed.pop(0), "pair_sum_w_in_%d" % (step - 1)))
    grad_x, g_norm = _rmsnorm_bwd(xs, norm_row, dh, dout, "rmsnorm_bwd")
    (norm_parts,) = _exchange(_all_gather([_pack(dict(norm_w=g_norm), ("norm_w",))]), "gather_norm_grad")
    from_chips = dict(zip(LARGE, (chip_parts, [chips_ap], [chips_glu], [chips_sp], [chips_o])))
    return grad_x, from_chips, small_parts, norm_parts


def kernel(x, norm_w, w_in, q_norm_w, k_norm_w, sinks, w_attn_proj, A_re, A_im, log_dt, B_re, B_im, C_re, C_im, D_skip, w_glu, b_glu, w_ssm_proj, w_out, loss_target, m_norm_w, m_w_in, m_q_norm_w, m_k_norm_w, m_sinks, m_w_attn_proj, m_A_re, m_A_im, m_log_dt, m_B_re, m_B_im, m_C_re, m_C_im, m_D_skip, m_w_glu, m_b_glu, m_w_ssm_proj, m_w_out, v_norm_w, v_w_in, v_q_norm_w, v_k_norm_w, v_sinks, v_w_attn_proj, v_A_re, v_A_im, v_log_dt, v_B_re, v_B_im, v_C_re, v_C_im, v_D_skip, v_w_glu, v_b_glu, v_w_ssm_proj, v_w_out):
    weights = dict(norm_w=norm_w, w_in=w_in, q_norm_w=q_norm_w, k_norm_w=k_norm_w, sinks=sinks,
                   w_attn_proj=w_attn_proj, A_re=A_re, A_im=A_im, log_dt=log_dt, B_re=B_re, B_im=B_im, C_re=C_re,
                   C_im=C_im, D_skip=D_skip, w_glu=w_glu, b_glu=b_glu, w_ssm_proj=w_ssm_proj, w_out=w_out)
    m_in = dict(norm_w=m_norm_w, w_in=m_w_in, q_norm_w=m_q_norm_w, k_norm_w=m_k_norm_w, sinks=m_sinks,
                w_attn_proj=m_w_attn_proj, A_re=m_A_re, A_im=m_A_im, log_dt=m_log_dt, B_re=m_B_re, B_im=m_B_im,
                C_re=m_C_re, C_im=m_C_im, D_skip=m_D_skip, w_glu=m_w_glu, b_glu=m_b_glu, w_ssm_proj=m_w_ssm_proj,
                w_out=m_w_out)
    v_in = dict(norm_w=v_norm_w, w_in=v_w_in, q_norm_w=v_q_norm_w, k_norm_w=v_k_norm_w, sinks=v_sinks,
                w_attn_proj=v_w_attn_proj, A_re=v_A_re, A_im=v_A_im, log_dt=v_log_dt, B_re=v_B_re, B_im=v_B_im,
                C_re=v_C_re, C_im=v_C_im, D_skip=v_D_skip, w_glu=v_w_glu, b_glu=v_b_glu, w_ssm_proj=v_w_ssm_proj,
                w_out=v_w_out)

    _, seq, d = x.shape
    column_sharded = LARGE[:4]
    as_rows = lambda k, a: a.T if k in column_sharded else a
    shards = [as_rows(k, weights[k]).astype(BF16) for k in LARGE]
    small = {k: weights[k] for k in SMALL}
    grad_x, from_chips, small_parts, norm_parts = _step(x.reshape(seq, d), loss_target.reshape(seq, d), small,
                                                        shards)

    grads, delta, new_m, new_v = {}, {}, {}, {}
    for k in LARGE:
        if k == "w_in":
            res = _adamw_chips(weights[k].T, from_chips[k], m_in[k].T, v_in[k].T, "adamw_" + k)
            grads[k], delta[k], new_m[k], new_v[k] = [a.T for a in res]
        elif k == "w_out":
            grads[k], delta[k], new_m[k], new_v[k] = _adamw_chips(weights[k], from_chips[k], m_in[k], v_in[k],
                                                                  "adamw_" + k)
        else:
            grads[k] = _chip_sum(from_chips[k][0], "chip_sum_" + k).T
            delta[k], new_m[k], new_v[k] = _adamw(weights[k], grads[k], m_in[k], v_in[k], "adamw_" + k)

    like = dict(small, loss=jnp.zeros((), F32))
    for keys, parts in ((SMALL_REST, small_parts), (("norm_w",), norm_parts)):
        grads.update(_unpack(_device_sum(parts, "device_sum_%d" % len(keys)), like, keys))
    loss = grads["loss"]
    tiled = [k for k in SMALL if weights[k].ndim == 3]
    for k in tiled:
        delta[k], new_m[k], new_v[k] = _adamw(weights[k], grads[k], m_in[k], v_in[k], "adamw_" + k)
    whole = [k for k in SMALL if k not in tiled]
    res = _adamw_native(*[[src[k] for k in whole] for src in (weights, grads, m_in, v_in)], "adamw_small")
    for dst, r in zip((delta, new_m, new_v), res):
        dst.update(zip(whole, r))

    return (loss, grad_x.reshape(x.shape), *[grads[k] for k in ORDER], *[delta[k] for k in ORDER],
            *[new_m[k] for k in ORDER], *[new_v[k] for k in ORDER])
```

```python
import math
from typing import Callable, NamedTuple

import jax
import jax.numpy as jnp
import numpy as np
from jax import lax
from jax.experimental import pallas as pl
from jax.experimental.pallas import tpu as pltpu

F32 = jnp.float32
BF16 = jnp.bfloat16
MESH = pl.DeviceIdType.MESH

HEAD_DIM = 64
N_KV_HEADS = 4
GROUP = 16
STATE = 64
BLOCK = 128
NORM_EPS = 1e-6
N_DEV = 8
N_CHIPS = 4
LANES = 128
SUBLANES = 8
MXU_DIM = 256
VMEM_BYTES = 64 * 1024 * 1024
VMEM_CAP = VMEM_BYTES - 8 * 1024 * 1024

ADAM_LR = 0.001
ADAM_B1 = 0.9
ADAM_B2 = 0.999
ADAM_EPS = 1e-08
ADAM_WD = 0.01
ADAM_STEP = 10

GELU_C = math.sqrt(2.0 / math.pi)
GELU_K = 0.044715


def _tile(dim, pref, mult=LANES):
    if dim <= pref:
        return dim
    best = None
    for d in range(mult, pref + 1, mult):
        if dim % d == 0:
            best = d
    assert best is not None, (dim, pref, mult)
    return best


def _params(semantics=None, vmem=None):
    kw = {}
    if semantics is not None:
        kw["dimension_semantics"] = semantics
    if vmem is not None:
        kw["vmem_limit_bytes"] = int(min(VMEM_CAP, max(vmem, 32 * 1024 * 1024)))
    return pltpu.CompilerParams(**kw)


def _nbytes(shape, dtype):
    return math.prod(shape) * jnp.dtype(dtype).itemsize


def _sigmoid(x):
    return 1.0 / (1.0 + jnp.exp(-x))


def _silu(x):
    return x * _sigmoid(x)


def _dsilu(x):
    s = _sigmoid(x)
    return s * (1.0 + x * (1.0 - s))


def _gelu(x):
    return 0.5 * x * (1.0 + jnp.tanh(GELU_C * (x + GELU_K * x * x * x)))


def _dgelu(x):
    t = jnp.tanh(GELU_C * (x + GELU_K * x * x * x))
    return 0.5 * (1.0 + t) + 0.5 * x * (1.0 - t * t) * GELU_C * (1.0 + 3.0 * GELU_K * x * x)


def _dot(a, b, dims):
    return lax.dot_general(a, b, (dims, ((), ())), preferred_element_type=F32)


NN = ((1,), (0,))
NT = ((1,), (1,))
TN = ((0,), (0,))


def _any_spec():
    return pl.BlockSpec(memory_space=pl.ANY)


def _pallas(body, **kw):
    pin = lambda s: pltpu.HBM(s.shape, s.dtype) if isinstance(s, jax.ShapeDtypeStruct) else s
    out_shape = kw.pop("out_shape")
    out_shape = [pin(s) for s in out_shape] if isinstance(out_shape, (list, tuple)) else pin(out_shape)
    call = pl.pallas_call(body, out_shape=out_shape, **kw)

    def run(*operands):
        pinned = [pltpu.with_memory_space_constraint(o, pltpu.HBM) if jnp.issubdtype(o.dtype, jnp.floating) else o
                  for o in operands]
        return call(*pinned)

    return run


class Rider(NamedTuple):
    operands: tuple
    out_shapes: tuple
    sems: tuple
    start: Callable
    finish: Callable


def _all_gather(shards):
    n = len(shards)

    def copies(ins, outs, sems):
        send_sems, recv_sems, local_sems = sems
        x, y, c = lax.axis_index("x"), lax.axis_index("y"), lax.axis_index("c")
        me, sibling = (x, y, c), (x, y, 1 - c)
        chips = [(1 - x, y), (x, 1 - y), (1 - x, 1 - y)]

        def rows(k, px, py, pc):
            r = shards[k].shape[0]
            return outs[k].at[pl.ds((4 * px + 2 * py + pc) * r, r), :]

        def copy(k, s, block, to, src=None):
            return pltpu.make_async_remote_copy(
                src_ref=rows(k, *block) if src is None else src, dst_ref=rows(k, *block),
                send_sem=send_sems.at[7 * k + s], recv_sem=recv_sems.at[7 * k + s],
                device_id=to, device_id_type=MESH)

        mine = [pltpu.make_async_copy(ins[k], rows(k, *me), local_sems.at[k]) for k in range(n)]
        first = []
        for k in range(n):
            first.append(copy(k, 0, me, sibling, src=ins[k]))
            first += [copy(k, 1 + j, me, (*chip, c), src=ins[k]) for j, chip in enumerate(chips)]
        return me, sibling, chips, c, copy, mine, first

    def start(ins, outs, sems):
        *_, mine, first = copies(ins, outs, sems)
        for cp in mine + first:
            cp.start()

    def finish(ins, outs, sems):
        me, sibling, chips, c, copy, mine, first = copies(ins, outs, sems)
        passed = []
        for j, chip in enumerate(chips):
            for k in range(n):
                copy(k, 1 + j, (*chip, c), me).wait_recv()
                fwd = copy(k, 4 + j, (*chip, c), sibling)
                fwd.start()
                passed.append(fwd)
        for k in range(n):
            copy(k, 0, sibling, me).wait_recv()
            for j, chip in enumerate(chips):
                copy(k, 4 + j, (*chip, 1 - c), me).wait_recv()
        for cp in first + passed:
            cp.wait_send()
        for cp in mine:
            cp.wait()

    return Rider(
        tuple(shards),
        tuple(jax.ShapeDtypeStruct((N_DEV * s.shape[0], s.shape[1]), s.dtype) for s in shards),
        (pltpu.SemaphoreType.DMA((7 * n,)), pltpu.SemaphoreType.DMA((7 * n,)), pltpu.SemaphoreType.DMA((n,))),
        start, finish)


def _sibling_exchange(grads):
    n = len(grads)

    def copies(ins, outs, sems):
        send_sems, recv_sems = sems
        x, y, c = lax.axis_index("x"), lax.axis_index("y"), lax.axis_index("c")
        out = []
        for k in range(n):
            r = grads[k].shape[0] // N_DEV
            for j in range(N_CHIPS):
                out.append(pltpu.make_async_remote_copy(
                    src_ref=ins[k].at[pl.ds((2 * j + 1 - c) * r, r), :],
                    dst_ref=outs[k].at[pl.ds(j * r, r), :],
                    send_sem=send_sems.at[N_CHIPS * k + j], recv_sem=recv_sems.at[N_CHIPS * k + j],
                    device_id=(x, y, 1 - c), device_id_type=MESH))
        return out

    def start(ins, outs, sems):
        for cp in copies(ins, outs, sems):
            cp.start()

    def finish(ins, outs, sems):
        for cp in copies(ins, outs, sems):
            cp.wait()

    return Rider(
        tuple(grads), tuple(jax.ShapeDtypeStruct((g.shape[0] // 2, g.shape[1]), g.dtype) for g in grads),
        (pltpu.SemaphoreType.DMA((N_CHIPS * n,)), pltpu.SemaphoreType.DMA((N_CHIPS * n,))), start, finish)


def _chip_exchange(parts):
    n = len(parts)

    def copies(ins, outs, sems):
        send_sems, recv_sems, local_sems = sems
        x, y, c = lax.axis_index("x"), lax.axis_index("y"), lax.axis_index("c")
        my_chip = 2 * x + y
        chips = [(1 - x, y), (x, 1 - y), (1 - x, 1 - y)]
        local, sent = [], []
        for k in range(n):
            r = parts[k].shape[0] // N_CHIPS
            mine = pl.ds(my_chip * r, r)
            local.append(pltpu.make_async_copy(ins[k].at[mine, :], outs[k].at[mine, :], local_sems.at[k]))
            for s, (px, py) in enumerate(chips):
                sent.append(pltpu.make_async_remote_copy(
                    src_ref=ins[k].at[pl.ds((2 * px + py) * r, r), :], dst_ref=outs[k].at[mine, :],
                    send_sem=send_sems.at[3 * k + s], recv_sem=recv_sems.at[3 * k + s],
                    device_id=(px, py, c), device_id_type=MESH))
        return local, sent

    def start(ins, outs, sems):
        local, sent = copies(ins, outs, sems)
        for cp in local + sent:
            cp.start()

    def finish(ins, outs, sems):
        local, sent = copies(ins, outs, sems)
        for cp in sent + local:
            cp.wait()

    return Rider(
        tuple(parts), tuple(jax.ShapeDtypeStruct(p.shape, p.dtype) for p in parts),
        (pltpu.SemaphoreType.DMA((3 * n,)), pltpu.SemaphoreType.DMA((3 * n,)), pltpu.SemaphoreType.DMA((n,))),
        start, finish)


def _join(*riders):
    cuts_in, cuts_out, cuts_sem = [0], [0], [0]
    for r in riders:
        cuts_in.append(cuts_in[-1] + len(r.operands))
        cuts_out.append(cuts_out[-1] + len(r.out_shapes))
        cuts_sem.append(cuts_sem[-1] + len(r.sems))

    def each(which):
        def run(ins, outs, sems):
            for i, r in enumerate(riders):
                getattr(r, which)(ins[cuts_in[i]:cuts_in[i + 1]], outs[cuts_out[i]:cuts_out[i + 1]],
                                  sems[cuts_sem[i]:cuts_sem[i + 1]])
        return run

    return Rider(sum((r.operands for r in riders), ()), sum((r.out_shapes for r in riders), ()),
                 sum((r.sems for r in riders), ()), each("start"), each("finish"))


def _call(body, operands, *, name, out_shape, grid, in_specs, out_specs, scratch_shapes=(), aliases=None,
          semantics=None, vmem=None, rider=None):
    operands, out_shape, scratch_shapes = list(operands), list(out_shape), list(scratch_shapes)
    in_specs, out_specs = list(in_specs), list(out_specs)
    if rider is None:
        res = _pallas(
            body, name=name, out_shape=out_shape, grid=grid, in_specs=in_specs, out_specs=out_specs,
            scratch_shapes=scratch_shapes, input_output_aliases=aliases or {},
            compiler_params=_params(semantics, vmem))(*operands)
        return list(res), []
    n_in, n_out, n_scr = len(operands), len(out_shape), len(scratch_shapes)
    ri, ro = len(rider.operands), len(rider.out_shapes)

    def carried(*refs):
        a, b = n_in, n_in + ri
        c, d = b + n_out, b + n_out + ro
        e = d + n_scr
        ids = [pl.program_id(k) for k in range(len(grid))]
        first = ids[0] == 0
        last = ids[0] == grid[0] - 1
        for k in range(1, len(grid)):
            first = jnp.logical_and(first, ids[k] == 0)
            last = jnp.logical_and(last, ids[k] == grid[k] - 1)

        @pl.when(first)
        def _():
            rider.start(refs[a:b], refs[c:d], refs[e:])

        body(*refs[:a], *refs[b:c], *refs[d:e])

        @pl.when(last)
        def _():
            rider.finish(refs[a:b], refs[c:d], refs[e:])

    res = _pallas(
        carried, name=name, out_shape=out_shape + list(rider.out_shapes), grid=grid,
        in_specs=in_specs + [_any_spec()] * ri, out_specs=out_specs + [_any_spec()] * ro,
        scratch_shapes=scratch_shapes + list(rider.sems), input_output_aliases=aliases or {},
        compiler_params=_params(("arbitrary",) * len(grid), vmem))(*operands, *rider.operands)
    return list(res[:n_out]), list(res[n_out:])


def _exchange(rider, name):
    ri, ro = len(rider.operands), len(rider.out_shapes)

    def body(*refs):
        rider.start(refs[:ri], refs[ri:ri + ro], refs[ri + ro:])
        rider.finish(refs[:ri], refs[ri:ri + ro], refs[ri + ro:])

    return _pallas(
        body, name=name, out_shape=list(rider.out_shapes), in_specs=[_any_spec()] * ri,
        out_specs=[_any_spec()] * ro, scratch_shapes=list(rider.sems))(*rider.operands)


class Cols(NamedTuple):
    arr: jax.Array
    off: int
    width: int


def _cols(a):
    return a if isinstance(a, Cols) else Cols(a, 0, a.shape[1])


def _matmul(a, b, *, mode, name, out_dtype=F32, tm=1024, tn=1024, tk=2048, bias=None, add=None, out_cols=None,
            into=None, rider=None):
    a, b = _cols(a), _cols(b)
    if mode == "nn":
        (m, k), (k2, n) = (a.arr.shape[0], a.width), (b.arr.shape[0], b.width)
    elif mode == "nt":
        (m, k), (n, k2) = (a.arr.shape[0], a.width), (b.arr.shape[0], b.width)
    else:
        (k, m), (k2, n) = (a.arr.shape[0], a.width), (b.arr.shape[0], b.width)
    assert k == k2, (a.arr.shape, b.arr.shape, mode)
    tm, tn, tk = _tile(m, tm), _tile(n, tn), _tile(k, tk)
    nk = k // tk
    dims = {"nn": NN, "nt": NT, "tn": TN}[mode]
    if mode == "tn":
        assert a.off % tm == 0
        a_spec = pl.BlockSpec((tk, tm), lambda i, j, kk, o=a.off // tm: (kk, i + o))
    else:
        assert a.off % tk == 0
        a_spec = pl.BlockSpec((tm, tk), lambda i, j, kk, o=a.off // tk: (i, kk + o))
    if mode == "nt":
        assert b.off % tk == 0
        b_spec = pl.BlockSpec((tn, tk), lambda i, j, kk, o=b.off // tk: (j, kk + o))
    else:
        assert b.off % tn == 0
        b_spec = pl.BlockSpec((tk, tn), lambda i, j, kk, o=b.off // tn: (kk, j + o))
    in_specs, operands = [a_spec, b_spec], [a.arr, b.arr]
    assert bias is None or add is None
    if bias is not None:
        in_specs.append(pl.BlockSpec((1, tn), lambda i, j, kk: (0, j)))
        operands.append(bias)
    if add is not None:
        assert add.shape == (m, n)
        in_specs.append(pl.BlockSpec((tm, tn), lambda i, j, kk: (i, j)))
        operands.append(add)
    total_w, o_off = out_cols if out_cols is not None else (n, 0)
    assert o_off % tn == 0
    aliases = {}
    if into is not None:
        assert into.shape == (m, total_w) and into.dtype == out_dtype
        in_specs.append(_any_spec())
        operands.append(into)
        aliases = {len(operands) - 1: 0}
    n_in = len(operands)

    def body(*refs):
        a_ref, b_ref = refs[0], refs[1]
        bias_ref = refs[2] if bias is not None or add is not None else None
        o_ref = refs[n_in]
        acc_ref = refs[-1] if nk > 1 else None
        part = _dot(a_ref[...].astype(BF16), b_ref[...].astype(BF16), dims)

        def finish(acc):
            if bias_ref is not None:
                acc = acc + bias_ref[...]
            o_ref[...] = acc.astype(out_dtype)

        if nk == 1:
            finish(part)
        else:
            kk = pl.program_id(2)

            @pl.when(kk == 0)
            def _():
                acc_ref[...] = part

            @pl.when(kk > 0)
            def _():
                acc_ref[...] += part

            @pl.when(kk == nk - 1)
            def _():
                finish(acc_ref[...])

    vmem = 2 * (_nbytes((tm, tk), a.arr.dtype) + _nbytes((tk, tn), b.arr.dtype) + _nbytes((tm, tn), out_dtype))
    vmem += 3 * _nbytes((tm, tn), F32)
    (out,), landed = _call(
        body, operands, name=name, out_shape=[jax.ShapeDtypeStruct((m, total_w), out_dtype)],
        grid=(m // tm, n // tn, nk), in_specs=in_specs,
        out_specs=[pl.BlockSpec((tm, tn), lambda i, j, kk, o=o_off // tn: (i, j + o))],
        scratch_shapes=[pltpu.VMEM((tm, tn), F32)] if nk > 1 else [], aliases=aliases,
        semantics=("parallel", "parallel", "arbitrary"), vmem=vmem, rider=rider)
    return out if rider is None else (out, landed)


def _ew(fn, *, name, rows, width, tiles, vecs=(), outs, accs=0, tl=1024, cw=512, into=None, with_col=False):
    tl, cw = _tile(rows, tl, SUBLANES), _tile(width, cw)
    ncol = width // cw
    nt_, nv = len(tiles), len(vecs)
    into = list(into) if into is not None else [None] * len(outs)
    aliased = [t for t in into if t is not None]

    def off(o):
        assert o % cw == 0, (name, o, cw)
        return o // cw

    in_specs, vmem = [], 0
    for t in tiles:
        arr, o = t[0], off(t[1])
        wrap = t[2] // cw if len(t) > 2 else ncol
        in_specs.append(pl.BlockSpec((tl, cw), lambda j, i, o=o, wrap=wrap: (i, o + j % wrap)))
        vmem += _nbytes((tl, cw), arr.dtype)
    in_specs += [pl.BlockSpec((1, cw), lambda j, i, o=off(o): (0, j + o)) for _, o in vecs]
    in_specs += [_any_spec() for _ in aliased]
    out_shape, out_specs, aliases = [], [], {}
    n_in = nt_ + nv
    for idx, ((dt, tw, o), tgt) in enumerate(zip(outs, into)):
        out_shape.append(jax.ShapeDtypeStruct((rows, tw), dt))
        out_specs.append(pl.BlockSpec((tl, cw), lambda j, i, o=off(o): (i, j + o)))
        vmem += _nbytes((tl, cw), dt)
        if tgt is not None:
            assert tgt.shape == (rows, tw) and tgt.dtype == dt, (name, tgt.shape, tgt.dtype)
            aliases[n_in + len(aliases)] = idx
    for _ in range(accs):
        out_shape.append(jax.ShapeDtypeStruct((1, width), F32))
        out_specs.append(pl.BlockSpec((1, cw), lambda j, i: (0, j)))
    n_out = len(outs)

    def body(*refs):
        vals = [r[...].astype(F32) for r in refs[:n_in]]
        out_refs = refs[n_in + len(aliased):]
        res = fn(pl.program_id(0), *vals) if with_col else fn(*vals)
        res = res if isinstance(res, (tuple, list)) else (res,)
        assert len(res) == n_out + accs, (name, len(res))
        for r, v in zip(out_refs[:n_out], res[:n_out]):
            r[...] = v.astype(r.dtype)
        first = pl.program_id(1) == 0
        for r, v in zip(out_refs[n_out:], res[n_out:]):
            s = jnp.sum(v, axis=0, keepdims=True)

            @pl.when(first)
            def _(r=r, s=s):
                r[...] = s

            @pl.when(jnp.logical_not(first))
            def _(r=r, s=s):
                r[...] += s

    return _pallas(
        body, name=name, out_shape=out_shape, grid=(ncol, rows // tl),
        in_specs=in_specs, out_specs=out_specs, input_output_aliases=aliases,
        compiler_params=_params(("parallel", "arbitrary"), 3 * vmem),
    )(*[t[0] for t in tiles], *[v for v, _ in vecs], *aliased)


def _rmsnorm_fwd(x, w_row, name, rider=None):
    rows, d = x.shape
    tl = _tile(rows, 512, SUBLANES)

    def body(x_ref, w_ref, h_ref):
        xv = x_ref[...]
        rstd = lax.rsqrt(jnp.mean(xv * xv, axis=-1, keepdims=True) + NORM_EPS)
        h_ref[...] = (xv * rstd * w_ref[...]).astype(BF16)

    (h,), landed = _call(
        body, [x, w_row], name=name, out_shape=[jax.ShapeDtypeStruct((rows, d), BF16)], grid=(rows // tl,),
        in_specs=[pl.BlockSpec((tl, d), lambda i: (i, 0)), pl.BlockSpec((1, d), lambda i: (0, 0))],
        out_specs=[pl.BlockSpec((tl, d), lambda i: (i, 0))], semantics=("parallel",), rider=rider)
    return h if rider is None else (h, landed)


def _rmsnorm_bwd(x, w_row, dh, dout, name, rider=None):
    rows, d = x.shape
    tl = _tile(rows, 256, SUBLANES)

    def body(x_ref, w_ref, dh_ref, dout_ref, gx_ref, gw_ref):
        xv = x_ref[...]
        rstd = lax.rsqrt(jnp.mean(xv * xv, axis=-1, keepdims=True) + NORM_EPS)
        xn = xv * rstd
        dhv = dh_ref[...]
        dxn = dhv * w_ref[...]
        dx = rstd * (dxn - xn * jnp.mean(dxn * xn, axis=-1, keepdims=True))
        gx_ref[...] = dout_ref[...] + dx
        gw = jnp.sum(dhv * xn, axis=0, keepdims=True)

        @pl.when(pl.program_id(0) == 0)
        def _():
            gw_ref[...] = gw

        @pl.when(pl.program_id(0) > 0)
        def _():
            gw_ref[...] += gw

    tile = pl.BlockSpec((tl, d), lambda i: (i, 0))
    row = pl.BlockSpec((1, d), lambda i: (0, 0))
    res, landed = _call(
        body, [x, w_row, dh, dout], name=name,
        out_shape=[jax.ShapeDtypeStruct((rows, d), F32), jax.ShapeDtypeStruct((1, d), F32)],
        grid=(rows // tl,), in_specs=[tile, row, tile, tile], out_specs=[tile, row],
        semantics=("arbitrary",), rider=rider)
    return res if rider is None else (res, landed)


def _head_mean(x, gmat):
    hi = x.astype(BF16)
    lo = (x - hi.astype(F32)).astype(BF16)
    out = []
    for s in range(x.shape[1] // MXU_DIM):
        sl = slice(s * MXU_DIM, (s + 1) * MXU_DIM)
        out.append(_dot(hi[:, sl], gmat, NN) + _dot(lo[:, sl], gmat, NN))
    return out[0] if len(out) == 1 else jnp.concatenate(out, axis=1)


def _head_mean_matrix():
    blk = jnp.arange(MXU_DIM) // HEAD_DIM
    return jnp.where(blk[:, None] == blk[None, :], 1.0 / HEAD_DIM, 0.0).astype(BF16)


def _spread_head(x, g, width):
    col = x[:, (g // 2) * LANES:(g // 2 + 1) * LANES]
    other = pltpu.roll(col, HEAD_DIM, axis=1)
    low = lax.broadcasted_iota(jnp.int32, col.shape, 1) < HEAD_DIM
    both = jnp.where(low, col, other) if g % 2 == 0 else jnp.where(low, other, col)
    return both if width == LANES else jnp.concatenate([both] * (width // LANES), axis=1)


def _head_diagonal(t, per_kv):
    head = lax.broadcasted_iota(jnp.int32, t.shape, 1) // HEAD_DIM
    zero = jnp.zeros_like(t)
    return jnp.concatenate([jnp.where(head == r, t, zero) for r in range(per_kv)], axis=0)


def _fold_heads(x, per_kv):
    rows = x.shape[0] // per_kv
    head = lax.broadcasted_iota(jnp.int32, (rows, x.shape[1]), 1) // HEAD_DIM
    acc = jnp.where(head == 0, x[0:rows], 0.0)
    for r in range(1, per_kv):
        acc = acc + jnp.where(head == r, x[r * rows:(r + 1) * rows], 0.0)
    while acc.shape[1] > LANES:
        half = acc.shape[1] // 2
        acc = acc[:, :half] + acc[:, half:]
    return acc + pltpu.roll(acc, HEAD_DIM, axis=1)


def _join_heads(parts):
    low = lax.broadcasted_iota(jnp.int32, parts[0].shape, 1) < HEAD_DIM
    cols = [jnp.where(low, parts[2 * j], parts[2 * j + 1]) for j in range(len(parts) // 2)]
    return cols[0] if len(cols) == 1 else jnp.concatenate(cols, axis=1)


def _attn_specs(attn_w, kv_w, block=lambda s: s):
    half = attn_w // 2
    kcol, vcol = attn_w // kv_w, attn_w // kv_w + 1
    gcol = (attn_w + 2 * kv_w) // half
    prev = lambda s: jnp.maximum(block(s) - 1, 0)
    return [
        pl.BlockSpec((BLOCK, attn_w), lambda s: (block(s), 0)),
        pl.BlockSpec((BLOCK, kv_w), lambda s: (prev(s), kcol)),
        pl.BlockSpec((BLOCK, kv_w), lambda s: (block(s), kcol)),
        pl.BlockSpec((BLOCK, kv_w), lambda s: (prev(s), vcol)),
        pl.BlockSpec((BLOCK, kv_w), lambda s: (block(s), vcol)),
        pl.BlockSpec((BLOCK, half), lambda s: (block(s), gcol)),
        pl.BlockSpec((BLOCK, half), lambda s: (block(s), gcol + 1)),
    ]


def _band_mask(i):
    q_loc = lax.broadcasted_iota(jnp.int32, (BLOCK, 2 * BLOCK), 0) + BLOCK
    k_loc = lax.broadcasted_iota(jnp.int32, (BLOCK, 2 * BLOCK), 1)
    diff = q_loc - k_loc
    first_key = jnp.where(i == 0, BLOCK, 0)
    return (diff >= 0) & (diff < BLOCK) & (k_loc >= first_key)


def _softmax_with_sink(s, sink):
    m = jnp.maximum(jnp.max(s, axis=-1, keepdims=True), sink)
    p = jnp.exp(s - m)
    e_sink = jnp.exp(sink - m)
    den = jnp.sum(p, axis=-1, keepdims=True) + e_sink
    inv = 1.0 / den
    return p * inv, e_sink * inv


def _attn_block(i, q, kk, vv, qw, kw, gmat, sink_ref, per_kv):
    scale = 1.0 / math.sqrt(HEAD_DIM)
    keys = 2 * BLOCK
    valid = _band_mask(i)
    q_rstd = lax.rsqrt(_head_mean(q * q, gmat) + NORM_EPS)
    qn = q * q_rstd
    qh = (qn * qw).astype(BF16)
    k_rstd = lax.rsqrt(_head_mean(kk * kk, gmat) + NORM_EPS)
    kn = kk * k_rstd
    kh = kn * kw
    gw = per_kv * HEAD_DIM
    groups = []
    for g in range(N_KV_HEADS):
        kd = _head_diagonal(_spread_head(kh, g, gw).astype(BF16), per_kv)
        vd = _head_diagonal(_spread_head(vv, g, gw).astype(BF16), per_kv)
        qg = qh[:, g * gw:(g + 1) * gw]
        s_all = _dot(qg, kd, NT) * scale
        ps, p_sinks = [], []
        for r in range(per_kv):
            s = jnp.where(valid, s_all[:, r * keys:(r + 1) * keys], -1e30)
            p, p_sink = _softmax_with_sink(s, sink_ref[g * per_kv + r])
            ps.append(p)
            p_sinks.append(p_sink)
        pb = jnp.concatenate(ps, axis=1).astype(BF16)
        groups.append((kd, vd, qg, ps, p_sinks, pb, _dot(pb, vd, NN)))
    return qn, q_rstd, kn, k_rstd, groups


def _attention_fwd(proj, qw_row, kw_row, gmat, sinks, *, attn_w, kv_w, name):
    rows = proj.shape[0]
    per_kv = attn_w // HEAD_DIM // N_KV_HEADS

    def body(q_ref, kp_ref, kc_ref, vp_ref, vc_ref, glo_ref, ghi_ref, qw_ref, kw_ref, gm_ref, sink_ref, o_ref):
        kk = jnp.concatenate([kp_ref[...], kc_ref[...]], axis=0).astype(F32)
        vv = jnp.concatenate([vp_ref[...], vc_ref[...]], axis=0).astype(F32)
        gate = jnp.concatenate([glo_ref[...], ghi_ref[...]], axis=1).astype(F32)
        *_, groups = _attn_block(pl.program_id(0), q_ref[...].astype(F32), kk, vv, qw_ref[...], kw_ref[...], gm_ref[...],
                                 sink_ref, per_kv)
        attn = jnp.concatenate([grp[-1] for grp in groups], axis=1)
        o_ref[...] = (attn * _silu(gate)).astype(BF16)

    const = lambda a: pl.BlockSpec(a.shape, lambda i: (0, 0))
    return _pallas(
        body, name=name, out_shape=jax.ShapeDtypeStruct((rows, attn_w), BF16), grid=(rows // BLOCK,),
        in_specs=_attn_specs(attn_w, kv_w) + [const(qw_row), const(kw_row), const(gmat),
                                              pl.BlockSpec(memory_space=pltpu.SMEM)],
        out_specs=pl.BlockSpec((BLOCK, attn_w), lambda i: (i, 0)),
        compiler_params=_params(("parallel",), 40 * 1024 * 1024),
    )(proj, proj, proj, proj, proj, proj, proj, qw_row, kw_row, gmat, sinks)


def _attention_bwd(proj, d_ag, dproj, qw_row, kw_row, gmat, sinks, *, attn_w, kv_w, name, rider=None):
    rows = proj.shape[0]
    nb = rows // BLOCK
    per_kv = attn_w // HEAD_DIM // N_KV_HEADS
    gw = per_kv * HEAD_DIM
    keys = 2 * BLOCK
    scale = 1.0 / math.sqrt(HEAD_DIM)
    w_out = 2 * attn_w + 2 * kv_w
    rev = lambda s: nb - 1 - s

    def body(q_ref, kp_ref, kc_ref, vp_ref, vc_ref, glo_ref, ghi_ref, dag_ref, qw_ref, kw_ref, gm_ref, sink_ref, _,
             dp_ref, gqw_ref, gkw_ref, gs_ref, carry_ref):
        step = pl.program_id(0)
        i = rev(step)
        kk = jnp.concatenate([kp_ref[...], kc_ref[...]], axis=0).astype(F32)
        vv = jnp.concatenate([vp_ref[...], vc_ref[...]], axis=0).astype(F32)
        gate = jnp.concatenate([glo_ref[...], ghi_ref[...]], axis=1).astype(F32)
        d_ag_v = dag_ref[...].astype(F32)
        qw, kw, gmat_v = qw_ref[...], kw_ref[...], gm_ref[...]
        qn, q_rstd, kn, k_rstd, groups = _attn_block(i, q_ref[...].astype(F32), kk, vv, qw, kw, gmat_v, sink_ref,
                                                     per_kv)
        lane = lax.broadcasted_iota(jnp.int32, (SUBLANES, LANES), 1)
        sub = lax.broadcasted_iota(jnp.int32, (SUBLANES, LANES), 0)
        gsink = jnp.zeros((SUBLANES, LANES), F32)
        dq_groups, dgate_groups, dk_heads, dv_heads = [], [], [], []
        for g, (kd, vd, qg, ps, p_sinks, pb, o) in enumerate(groups):
            cs = slice(g * gw, (g + 1) * gw)
            gate_g, d_ag_g = gate[:, cs], d_ag_v[:, cs]
            dgate_groups.append(d_ag_g * o * _dsilu(gate_g))
            do = (d_ag_g * _silu(gate_g)).astype(BF16)
            dp_all = _dot(do, vd, NT)
            dss = []
            for r in range(per_kv):
                p, dp = ps[r], dp_all[:, r * keys:(r + 1) * keys]
                delta = jnp.sum(p * dp, axis=-1, keepdims=True)
                dss.append(p * (dp - delta) * scale)
                gs_h = jnp.sum(-p_sinks[r] * delta, axis=0, keepdims=True)
                gsink = gsink + jnp.where((lane == g * per_kv + r) & (sub == 0), gs_h, 0.0)
            ds = jnp.concatenate(dss, axis=1).astype(BF16)
            dq_groups.append(_dot(ds, kd, NN))
            dk_heads.append(_fold_heads(_dot(ds, qg, TN), per_kv))
            dv_heads.append(_fold_heads(_dot(pb, do, TN), per_kv))
        dqh = jnp.concatenate(dq_groups, axis=1)
        gqw = jnp.sum(dqh * qn, axis=0, keepdims=True)
        dqn = dqh * qw
        dq = q_rstd * (dqn - qn * _head_mean(dqn * qn, gmat_v))
        dkh = _join_heads(dk_heads)
        gkw = jnp.sum(dkh * kn, axis=0, keepdims=True)
        dkn = dkh * kw
        dk = k_rstd * (dkn - kn * _head_mean(dkn * kn, gmat_v))
        dkv = jnp.concatenate([dk, _join_heads(dv_heads)], axis=1)

        @pl.when(step == 0)
        def _():
            carry_ref[...] = jnp.zeros_like(carry_ref)
            gqw_ref[...] = gqw
            gkw_ref[...] = gkw
            gs_ref[...] = gsink

        @pl.when(step > 0)
        def _():
            gqw_ref[...] += gqw
            gkw_ref[...] += gkw
            gs_ref[...] += gsink

        dp_ref[:, 0:attn_w] = dq.astype(BF16)
        dp_ref[:, attn_w:attn_w + 2 * kv_w] = (dkv[BLOCK:2 * BLOCK, :] + carry_ref[...]).astype(BF16)
        dp_ref[:, attn_w + 2 * kv_w:w_out] = jnp.concatenate(dgate_groups, axis=1).astype(BF16)
        carry_ref[...] = dkv[0:BLOCK, :]

    const = lambda a: pl.BlockSpec(a.shape, lambda s: (0, 0))
    res, landed = _call(
        body, [proj, proj, proj, proj, proj, proj, proj, d_ag, qw_row, kw_row, gmat, sinks, dproj], name=name,
        out_shape=[jax.ShapeDtypeStruct(dproj.shape, BF16),
                   jax.ShapeDtypeStruct(qw_row.shape, F32), jax.ShapeDtypeStruct(kw_row.shape, F32),
                   jax.ShapeDtypeStruct((SUBLANES, LANES), F32)],
        grid=(nb,),
        in_specs=_attn_specs(attn_w, kv_w, rev) + [pl.BlockSpec((BLOCK, attn_w), lambda s: (rev(s), 0)),
                                                   const(qw_row), const(kw_row), const(gmat),
                                                   pl.BlockSpec(memory_space=pltpu.SMEM), _any_spec()],
        out_specs=[pl.BlockSpec((BLOCK, w_out), lambda s: (rev(s), 0)),
                   const(qw_row), const(kw_row), pl.BlockSpec((SUBLANES, LANES), lambda s: (0, 0))],
        scratch_shapes=[pltpu.VMEM((BLOCK, 2 * kv_w), F32)],
        aliases={12: 0}, semantics=("arbitrary",), vmem=48 * 1024 * 1024, rider=rider)
    return res if rider is None else (res, landed)


def _cmul(ar, ai, br, bi):
    return ar * br - ai * bi, ar * bi + ai * br


def _ssm_prep(a_re, a_im, log_dt_col, steps, name):
    def body(are_ref, aim_ref, ldt_ref, abr_ref, abi_ref, cfr_ref, cfi_ref, apr_ref, api_ref, pwr_ref, pwi_ref):
        are, aim = are_ref[...], aim_ref[...]
        dt = jnp.exp(ldt_ref[...])
        mag = jnp.exp(dt * are)
        abr = mag * jnp.cos(dt * aim)
        abi = mag * jnp.sin(dt * aim)
        num_re, num_im = abr - 1.0, abi
        den = are * are + aim * aim
        abr_ref[...] = abr
        abi_ref[...] = abi
        cfr_ref[...] = (num_re * are + num_im * aim) / den
        cfi_ref[...] = (num_im * are - num_re * aim) / den
        pr, pi = jnp.ones_like(abr), jnp.zeros_like(abr)
        for k in range(steps):
            pwr_ref[k] = pr
            pwi_ref[k] = pi
            pr, pi = _cmul(pr, pi, abr, abi)
        apr_ref[...] = pr
        api_ref[...] = pi

    shp = jax.ShapeDtypeStruct(a_re.shape, F32)
    pows = jax.ShapeDtypeStruct((steps,) + a_re.shape, F32)
    return _pallas(body, name=name, out_shape=[shp] * 6 + [pows] * 2)(a_re, a_im, log_dt_col)


def _ssm_param_bwd(a_re, a_im, log_dt_col, d_ab_re, d_ab_im, b_re, b_im, dbt_re, dbt_im, name):
    def body(are_ref, aim_ref, ldt_ref, gabr_ref, gabi_ref, br_ref, bi_ref, tr_ref, ti_ref,
             dar_ref, dai_ref, dldt_ref, dbr_ref, dbi_ref):
        are, aim = are_ref[...], aim_ref[...]
        dt = jnp.exp(ldt_ref[...])
        mag = jnp.exp(dt * are)
        abr = mag * jnp.cos(dt * aim)
        abi = mag * jnp.sin(dt * aim)
        den = are * are + aim * aim
        cfr = ((abr - 1.0) * are + abi * aim) / den
        cfi = (abi * are - (abr - 1.0) * aim) / den
        gabr, gabi = jnp.sum(gabr_ref[...], axis=0), jnp.sum(gabi_ref[...], axis=0)
        t_re, t_im = tr_ref[...], ti_ref[...]
        g_r, g_i = _cmul(br_ref[...], -bi_ref[...], t_re, t_im)
        gcfr, gcfi = jnp.sum(g_r, axis=1), jnp.sum(g_i, axis=1)
        dbr, dbi = _cmul(cfr[:, None, :], -cfi[:, None, :], t_re, t_im)
        dbr_ref[...] = dbr
        dbi_ref[...] = dbi
        inv_r, inv_i = are / den, -aim / den
        t_r, t_i = _cmul(inv_r, -inv_i, gcfr, gcfi)
        gabr, gabi = gabr + t_r, gabi + t_i
        q_r, q_i = _cmul(cfr, cfi, inv_r, inv_i)
        da_r, da_i = _cmul(-q_r, q_i, gcfr, gcfi)
        gz_r, gz_i = _cmul(abr, -abi, gabr, gabi)
        dar_ref[...] = da_r + dt * gz_r
        dai_ref[...] = da_i + dt * gz_i
        dldt_ref[...] = dt * jnp.sum(are * gz_r + aim * gz_i, axis=-1, keepdims=True)

    shp = jax.ShapeDtypeStruct(a_re.shape, F32)
    bshp = jax.ShapeDtypeStruct(b_re.shape, F32)
    return _pallas(body, name=name,
                   out_shape=[shp, shp, jax.ShapeDtypeStruct(log_dt_col.shape, F32), bshp, bshp])(
        a_re, a_im, log_dt_col, d_ab_re, d_ab_im, b_re, b_im, dbt_re, dbt_im)


SCAN_LANES = 1024
SSM_CHUNK = 256
SCAN_UNROLL = 8
W_IN_PARTS = 2


def _scan_segments(xr_ref, xi_ref, a_re, a_im, ap_re, ap_im, pw_re, pw_im, carry_re, carry_im, cm_re, cm_im, steps,
                   reverse, base):
    n = xr_ref.shape[1]
    seg_order = range(SUBLANES - 1, -1, -1) if reverse else range(SUBLANES)
    sign = -1.0 if reverse else 1.0
    for c0 in range(0, n, SCAN_LANES):
        ls = slice(c0, c0 + SCAN_LANES)
        gs = slice(base + c0, base + c0 + SCAN_LANES)
        ar = jnp.broadcast_to(a_re[:, gs], (SUBLANES, SCAN_LANES))
        ai = jnp.broadcast_to(a_im[:, gs], (SUBLANES, SCAN_LANES))
        end_r = jnp.zeros((SUBLANES, SCAN_LANES), F32)
        end_i = jnp.zeros((SUBLANES, SCAN_LANES), F32)
        for j in range(steps):
            k = j if reverse else steps - 1 - j
            rws = slice(j * SUBLANES, (j + 1) * SUBLANES)
            tr, ti = _cmul(pw_re[k:k + 1, gs], sign * pw_im[k:k + 1, gs], xr_ref[rws, ls], xi_ref[rws, ls])
            end_r, end_i = end_r + tr, end_i + ti
        cr, ci = carry_re[:, gs], carry_im[:, gs]
        apr, api = ap_re[:, gs], ap_im[:, gs]
        for r in seg_order:
            cm_re[r:r + 1, gs] = cr
            cm_im[r:r + 1, gs] = ci
            tr, ti = _cmul(apr, api, cr, ci)
            cr, ci = end_r[r:r + 1, :] + tr, end_i[r:r + 1, :] + ti
        carry_re[:, gs] = cr
        carry_im[:, gs] = ci

        def run(t, s, ar=ar, ai=ai, ls=ls):
            sr, si = s
            for k in range(SCAN_UNROLL):
                j = steps - 1 - (t * SCAN_UNROLL + k) if reverse else t * SCAN_UNROLL + k
                r0 = pl.multiple_of(j * SUBLANES, SUBLANES)
                sr, si = _cmul(ar, ai, sr, si)
                sr = sr + xr_ref[pl.ds(r0, SUBLANES), ls]
                si = si + xi_ref[pl.ds(r0, SUBLANES), ls]
                xr_ref[pl.ds(r0, SUBLANES), ls] = sr
                xi_ref[pl.ds(r0, SUBLANES), ls] = si
            return sr, si

        assert steps % SCAN_UNROLL == 0
        lax.fori_loop(0, steps // SCAN_UNROLL, run, (cm_re[:, gs], cm_im[:, gs]))


SB_GROUPS = MXU_DIM // GROUP
SB_STATE = SB_GROUPS * STATE


def _ssm_rows(m):
    flat = m.reshape(-1, STATE).astype(F32)
    return jnp.concatenate([flat, flat], axis=1)


def _from_ssm_rows(rows):
    return rows[:, :STATE].reshape(-1, GROUP, STATE)


def _own_group(shape):
    row_g = lax.broadcasted_iota(jnp.int32, shape, 0) // GROUP
    col_g = lax.broadcasted_iota(jnp.int32, shape, 1) // STATE
    return row_g == col_g


def _block_diagonal(rows):
    tiled = jnp.concatenate([rows] * (SB_STATE // LANES), axis=1)
    return jnp.where(_own_group(tiled.shape), tiled, 0.0).astype(BF16)


def _block_rows(acc):
    x = jnp.where(_own_group(acc.shape), acc, 0.0)
    while x.shape[1] > LANES:
        half = x.shape[1] // 2
        x = x[:, :half] + x[:, half:]
    return x + pltpu.roll(x, STATE, axis=1)


def _rows_to_segments(dst, srcs, steps, stage):
    for ref, off in srcs:
        for k in range(ref.shape[1] // LANES):
            stage[off // LANES + k] = ref[:, k * LANES:(k + 1) * LANES].astype(F32)
    for k in range(dst.shape[1] // LANES):
        for j in range(steps):
            dst[j * SUBLANES:(j + 1) * SUBLANES, k * LANES:(k + 1) * LANES] = (
                stage[k, pl.ds(j, SUBLANES, stride=steps), :])


def _segments_to_rows(dst, src, steps, stage):
    for k in range(src.shape[1] // LANES):
        for j in range(steps):
            stage[k, pl.ds(j, SUBLANES, stride=steps), :] = (
                src[j * SUBLANES:(j + 1) * SUBLANES, k * LANES:(k + 1) * LANES])
    for k in range(src.shape[1] // LANES):
        dst[:, k * LANES:(k + 1) * LANES] = stage[k]


def _u_specs(w, o_u, chunk, index):
    half = w // 2
    assert o_u % half == 0
    return [pl.BlockSpec((chunk, half), lambda c, k=k: (index(c), o_u // half + k)) for k in range(2)]


def _ssm_fwd(proj, o_u, bc_rows, rows_p, d_row, *, chunk, name, rider=None):
    rows = proj.shape[0]
    w = d_row.shape[1]
    nc = rows // chunk
    steps = chunk // SUBLANES
    nsb = w // MXU_DIM
    n_state = nsb * SB_STATE

    def body(ulo_ref, uhi_ref, b2r_ref, b2i_ref, c2r_ref, c2i_ref, abr_ref, abi_ref, cfr_ref, cfi_ref, apr_ref,
             api_ref, pwr_ref, pwi_ref, d_ref, y_ref, str_ref, sti_ref, yg_ref, bre_ref, bim_ref, cre_ref, cim_ref,
             useg, yseg, stage, sr, si,
             carry_r, carry_i, cm_r, cm_i):
        @pl.when(pl.program_id(0) == 0)
        def _():
            for src, dst in ((b2r_ref, bre_ref), (b2i_ref, bim_ref), (c2r_ref, cre_ref), (c2i_ref, cim_ref)):
                for sb in range(nsb):
                    dst[sb] = _block_diagonal(src[sb * MXU_DIM:(sb + 1) * MXU_DIM, :])
            carry_r[...] = jnp.zeros_like(carry_r)
            carry_i[...] = jnp.zeros_like(carry_i)

        str_ref[0] = carry_r[...]
        sti_ref[0] = carry_i[...]
        _rows_to_segments(useg, [(ulo_ref, 0), (uhi_ref, w // 2)], steps, stage)
        for sb in range(nsb):
            us = slice(sb * MXU_DIM, (sb + 1) * MXU_DIM)
            ss = slice(sb * SB_STATE, (sb + 1) * SB_STATE)
            ub = useg[:, us].astype(BF16)
            bur = _dot(ub, bre_ref[sb], NN)
            bui = _dot(ub, bim_ref[sb], NN)
            xr, xi = _cmul(cfr_ref[:, ss], cfi_ref[:, ss], bur, bui)
            sr[...] = xr
            si[...] = xi
            _scan_segments(sr, si, abr_ref[...], abi_ref[...], apr_ref[...], api_ref[...], pwr_ref, pwi_ref,
                           carry_r, carry_i, cm_r, cm_i, steps, False, sb * SB_STATE)
            y = _dot(sr[...].astype(BF16), cre_ref[sb], NT) - _dot(si[...].astype(BF16), cim_ref[sb], NT)
            yseg[:, us] = y + d_ref[:, us] * useg[:, us]
        _segments_to_rows(y_ref, yseg, steps, stage)
        yg_ref[...] = _gelu(y_ref[...]).astype(BF16)

    const = lambda a: pl.BlockSpec(a.shape, lambda c: (0,) * a.ndim)
    row_n = pl.BlockSpec((1, n_state), lambda c: (0, 0))
    st = pl.BlockSpec((1, 1, n_state), lambda c: (c, 0, 0))
    held = [pltpu.VMEM((nsb, MXU_DIM, SB_STATE), BF16)] * 4
    vmem = (4 * _nbytes((nsb, MXU_DIM, SB_STATE), BF16) + 4 * _nbytes((chunk, SB_STATE), F32)
            + 12 * _nbytes((chunk, w), F32) + 8 * _nbytes(bc_rows[0].shape, F32))
    res, landed = _call(
        body, [proj, proj, *bc_rows, *rows_p, d_row], name=name,
        out_shape=[jax.ShapeDtypeStruct((rows, w), F32), jax.ShapeDtypeStruct((nc, 1, n_state), F32),
                   jax.ShapeDtypeStruct((nc, 1, n_state), F32), jax.ShapeDtypeStruct((rows, w), BF16)],
        grid=(nc,),
        in_specs=_u_specs(w, o_u, chunk, lambda c: c) + [const(b) for b in bc_rows]
        + [row_n] * 6 + [pl.BlockSpec((steps, n_state), lambda c: (0, 0))] * 2 + [pl.BlockSpec((1, w), lambda c: (0, 0))],
        out_specs=[pl.BlockSpec((chunk, w), lambda c: (c, 0)), st, st, pl.BlockSpec((chunk, w), lambda c: (c, 0))],
        scratch_shapes=held + [pltpu.VMEM((chunk, w), F32), pltpu.VMEM((chunk, w), F32),
                               pltpu.VMEM((w // LANES, chunk, LANES), F32),
                               pltpu.VMEM((chunk, SB_STATE), F32), pltpu.VMEM((chunk, SB_STATE), F32),
                               pltpu.VMEM((1, n_state), F32), pltpu.VMEM((1, n_state), F32),
                               pltpu.VMEM((SUBLANES, n_state), F32), pltpu.VMEM((SUBLANES, n_state), F32)],
        semantics=("arbitrary",), vmem=vmem, rider=rider)
    return res if rider is None else (res, landed)


def _ssm_bwd(proj, o_u, y, dyg, st_re, st_im, bc_rows, rows_p, d_row, *, chunk, name, rider=None):
    rows = proj.shape[0]
    w = d_row.shape[1]
    nc = rows // chunk
    steps = chunk // SUBLANES
    nsb = w // MXU_DIM
    n_state = nsb * SB_STATE

    def body(ulo_ref, uhi_ref, y_ref, dyg_ref, str_ref, sti_ref, b2r_ref, b2i_ref, c2r_ref, c2i_ref, t2r_ref,
             t2i_ref, abr_ref, abi_ref, cfr_ref, cfi_ref, apr_ref, api_ref, pwr_ref, pwi_ref, d_ref,
             du_ref, gb2r_ref, gb2i_ref, gc2r_ref, gc2i_ref, gabr_ref, gabi_ref, dd_ref,
             bre_ref, bim_ref, cre_ref, cim_ref, btr_ref, bti_ref, dbre_ref, dbim_ref, dcre_ref, dcim_ref,
             useg, dyseg, dynat, stage, sr, si, lr, li, carry_r, carry_i, lam_r, lam_i, cm_r, cm_i, cl_r, cl_i):
        first = pl.program_id(0) == 0

        @pl.when(first)
        def _():
            for src, dst in ((b2r_ref, bre_ref), (b2i_ref, bim_ref), (c2r_ref, cre_ref), (c2i_ref, cim_ref),
                             (t2r_ref, btr_ref), (t2i_ref, bti_ref)):
                for sb in range(nsb):
                    dst[sb] = _block_diagonal(src[sb * MXU_DIM:(sb + 1) * MXU_DIM, :])
            lam_r[...] = jnp.zeros_like(lam_r)
            lam_i[...] = jnp.zeros_like(lam_i)
            for ref in (dbre_ref, dbim_ref, dcre_ref, dcim_ref, gabr_ref, gabi_ref, dd_ref):
                ref[...] = jnp.zeros_like(ref)

        dynat[...] = dyg_ref[...].astype(F32) * _dgelu(y_ref[...])
        half = w // 2
        dd_ref[:, :half] += jnp.sum(dynat[:, :half] * ulo_ref[...].astype(F32), axis=0, keepdims=True)
        dd_ref[:, half:] += jnp.sum(dynat[:, half:] * uhi_ref[...].astype(F32), axis=0, keepdims=True)
        _rows_to_segments(useg, [(ulo_ref, 0), (uhi_ref, half)], steps, stage)
        _rows_to_segments(dyseg, [(dynat, 0)], steps, stage)
        dy = dyseg[...]
        dyb = dy.astype(BF16)
        ub = useg[...].astype(BF16)
        carry_r[...] = str_ref[0]
        carry_i[...] = sti_ref[0]
        abr, abi = abr_ref[...], abi_ref[...]
        apr, api = apr_ref[...], api_ref[...]
        for sb in range(nsb):
            us = slice(sb * MXU_DIM, (sb + 1) * MXU_DIM)
            ss = slice(sb * SB_STATE, (sb + 1) * SB_STATE)
            base = sb * SB_STATE
            br = _dot(ub[:, us], bre_ref[sb], NN)
            bi = _dot(ub[:, us], bim_ref[sb], NN)
            xr, xi = _cmul(cfr_ref[:, ss], cfi_ref[:, ss], br, bi)
            sr[...] = xr
            si[...] = xi
            lr[...] = _dot(dyb[:, us], cre_ref[sb], NN)
            li[...] = -_dot(dyb[:, us], cim_ref[sb], NN)
            _scan_segments(sr, si, abr, abi, apr, api, pwr_ref, pwi_ref, carry_r, carry_i, cm_r, cm_i, steps, False,
                           base)
            dcre_ref[sb] += _dot(dyb[:, us], sr[...].astype(BF16), TN)
            dcim_ref[sb] -= _dot(dyb[:, us], si[...].astype(BF16), TN)
            _scan_segments(lr, li, abr, -abi, apr, -api, pwr_ref, pwi_ref, lam_r, lam_i, cl_r, cl_i, steps, True,
                           base)
            for c0 in range(0, SB_STATE, SCAN_LANES):
                ls = slice(c0, c0 + SCAN_LANES)
                gs = slice(base + c0, base + c0 + SCAN_LANES)

                def step(t, acc, ls=ls):
                    gar, gai, pr, pi = acc
                    for k in range(SCAN_UNROLL):
                        r0 = pl.multiple_of((t * SCAN_UNROLL + k) * SUBLANES, SUBLANES)
                        rws = pl.ds(r0, SUBLANES)
                        t_r, t_i = _cmul(pr, -pi, lr[rws, ls], li[rws, ls])
                        gar, gai, pr, pi = gar + t_r, gai + t_i, sr[rws, ls], si[rws, ls]
                    return gar, gai, pr, pi

                zero = jnp.zeros((SUBLANES, SCAN_LANES), F32)
                gar, gai, _, _ = lax.fori_loop(0, steps // SCAN_UNROLL, step,
                                               (zero, zero, cm_r[:, gs], cm_i[:, gs]))
                gabr_ref[:, gs] += gar
                gabi_ref[:, gs] += gai
            xr, xi = lr[...].astype(BF16), li[...].astype(BF16)
            du = _dot(xr, btr_ref[sb], NT) + _dot(xi, bti_ref[sb], NT)
            useg[:, us] = du + d_ref[:, us] * dy[:, us]
            dbre_ref[sb] += _dot(ub[:, us], xr, TN)
            dbim_ref[sb] += _dot(ub[:, us], xi, TN)
        _segments_to_rows(du_ref, useg, steps, stage)

        @pl.when(pl.program_id(0) == nc - 1)
        def _():
            for src, dst in ((dbre_ref, gb2r_ref), (dbim_ref, gb2i_ref), (dcre_ref, gc2r_ref), (dcim_ref, gc2i_ref)):
                for sb in range(nsb):
                    dst[sb * MXU_DIM:(sb + 1) * MXU_DIM, :] = _block_rows(src[sb])

    rev = lambda c: nc - 1 - c
    const = lambda a: pl.BlockSpec(a.shape, lambda c: (0,) * a.ndim)
    tile = pl.BlockSpec((chunk, w), lambda c: (rev(c), 0))
    row_n = pl.BlockSpec((1, n_state), lambda c: (0, 0))
    row_w = pl.BlockSpec((1, w), lambda c: (0, 0))
    st = pl.BlockSpec((1, 1, n_state), lambda c: (rev(c), 0, 0))
    acc8 = pl.BlockSpec((SUBLANES, n_state), lambda c: (0, 0))
    big = pltpu.VMEM((chunk, SB_STATE), F32)
    small = pltpu.VMEM((chunk, w), F32)
    row = pltpu.VMEM((1, n_state), F32)
    eight = pltpu.VMEM((SUBLANES, n_state), F32)
    blk = (nsb, MXU_DIM, SB_STATE)
    held = [pltpu.VMEM(blk, BF16)] * 6 + [pltpu.VMEM(blk, F32)] * 4
    vmem = (6 * _nbytes(blk, BF16) + 4 * _nbytes(blk, F32) + 5 * _nbytes((chunk, SB_STATE), F32)
            + 12 * _nbytes((chunk, w), F32) + 20 * _nbytes(bc_rows[0].shape, F32))
    res, landed = _call(
        body, [proj, proj, y, dyg, st_re, st_im, *bc_rows, *rows_p, d_row], name=name,
        out_shape=[jax.ShapeDtypeStruct((rows, w), F32)] + [jax.ShapeDtypeStruct(b.shape, F32) for b in bc_rows[:4]]
        + [jax.ShapeDtypeStruct((SUBLANES, n_state), F32)] * 2 + [jax.ShapeDtypeStruct((1, w), F32)],
        grid=(nc,),
        in_specs=_u_specs(w, o_u, chunk, rev) + [tile, tile, st, st] + [const(b) for b in bc_rows]
        + [row_n] * 6 + [pl.BlockSpec((steps, n_state), lambda c: (0, 0))] * 2 + [row_w],
        out_specs=[tile] + [const(b) for b in bc_rows[:4]] + [acc8] * 2 + [row_w],
        scratch_shapes=held + [small] * 3 + [pltpu.VMEM((w // LANES, chunk, LANES), F32)] + [big] * 4 + [row] * 4
        + [eight] * 4,
        semantics=("arbitrary",), vmem=vmem, rider=rider)
    return res if rider is None else (res, landed)


def _out_proj_loss(merged, w_o, x, target, name):
    rows, d = x.shape
    tm, tn = _tile(rows, 1024, SUBLANES), _tile(d, 1024)

    def body(a_ref, b_ref, x_ref, t_ref, g_ref, gb_ref, l_ref):
        err = x_ref[...] + _dot(a_ref[...], b_ref[...], NN) - t_ref[...]
        g = err * (1.0 / d)
        g_ref[...] = g
        gb_ref[...] = g.astype(BF16)
        part = jnp.sum(0.5 * err * g, axis=0, keepdims=True)
        first = pl.program_id(1) == 0

        @pl.when(first)
        def _():
            l_ref[...] = part

        @pl.when(jnp.logical_not(first))
        def _():
            l_ref[...] += part

    tile = pl.BlockSpec((tm, tn), lambda j, i: (i, j))
    vmem = 2 * (_nbytes((tm, d), BF16) + _nbytes((d, tn), BF16)) + 12 * _nbytes((tm, tn), F32)
    return _pallas(
        body, name=name,
        out_shape=[jax.ShapeDtypeStruct((rows, d), F32), jax.ShapeDtypeStruct((rows, d), BF16),
                   jax.ShapeDtypeStruct((1, d), F32)],
        grid=(d // tn, rows // tm),
        in_specs=[pl.BlockSpec((tm, d), lambda j, i: (i, 0)), pl.BlockSpec((d, tn), lambda j, i: (0, j)), tile, tile],
        out_specs=[tile, tile, pl.BlockSpec((1, tn), lambda j, i: (0, j))],
        compiler_params=_params(("parallel", "arbitrary"), vmem),
    )(merged, w_o, x, target)


def _pair_sum(grad, recv, name):
    r4, cdim = recv.shape
    r = r4 // N_CHIPS
    tr = _tile(r, 544, 16)
    g4 = grad.reshape(N_CHIPS, 2, r, cdim)
    r3 = recv.reshape(N_CHIPS, r, cdim)
    core = jnp.reshape(lax.axis_index("c"), (1,)).astype(jnp.int32)

    def body(c_ref, g_ref, r_ref, o_ref):
        o_ref[...] = (g_ref[0] + r_ref[...]).astype(BF16)

    out = _pallas(
        body, name=name, out_shape=jax.ShapeDtypeStruct((N_CHIPS, r, cdim), BF16),
        grid_spec=pltpu.PrefetchScalarGridSpec(
            num_scalar_prefetch=1, grid=(N_CHIPS, r // tr),
            in_specs=[pl.BlockSpec((1, 1, tr, cdim), lambda j, i, c: (j, c[0], i, 0)),
                      pl.BlockSpec((1, tr, cdim), lambda j, i, c: (j, i, 0))],
            out_specs=pl.BlockSpec((1, tr, cdim), lambda j, i, c: (j, i, 0))),
        compiler_params=_params(("parallel", "parallel"), 6 * _nbytes((tr, cdim), F32)),
    )(core, g4, r3)
    return out.reshape(r4, cdim)


def _chip_sum(recv, name):
    r4, cdim = recv.shape
    r = r4 // N_CHIPS
    tr = _tile(r, 544, 16)
    r3 = recv.reshape(N_CHIPS, r, cdim)

    def body(r_ref, o_ref):
        acc = r_ref[0].astype(F32)
        for j in range(1, N_CHIPS):
            acc = acc + r_ref[j].astype(F32)
        o_ref[...] = acc

    return _pallas(
        body, name=name, out_shape=jax.ShapeDtypeStruct((r, cdim), F32), grid=(r // tr,),
        in_specs=[pl.BlockSpec((N_CHIPS, tr, cdim), lambda i: (0, i, 0))],
        out_specs=pl.BlockSpec((tr, cdim), lambda i: (i, 0)),
        compiler_params=_params(("parallel",), 8 * _nbytes((tr, cdim), F32)),
    )(r3)


def _adamw_math(w, g, m, v):
    m = ADAM_B1 * m + (1.0 - ADAM_B1) * g
    v = ADAM_B2 * v + (1.0 - ADAM_B2) * (g * g)
    m_hat = m / (1.0 - ADAM_B1 ** ADAM_STEP)
    v_hat = v / (1.0 - ADAM_B2 ** ADAM_STEP)
    delta = -ADAM_LR * (m_hat / (jnp.sqrt(v_hat) + ADAM_EPS) + ADAM_WD * w)
    return delta, m, v


def _adamw(w, g, m, v, name):
    rows, rest = w.shape[0], w.shape[1:]
    tr = _tile(rows, 256, SUBLANES) if len(rest) == 1 else SUBLANES
    assert rows % tr == 0

    def body(w_ref, g_ref, m_ref, v_ref, d_ref, nm_ref, nv_ref):
        d, nm, nv = _adamw_math(w_ref[...], g_ref[...], m_ref[...], v_ref[...])
        d_ref[...] = d
        nm_ref[...] = nm
        nv_ref[...] = nv

    spec = pl.BlockSpec((tr,) + rest, lambda i: (i,) + (0,) * len(rest))
    shp = jax.ShapeDtypeStruct(w.shape, F32)
    return _pallas(
        body, name=name, out_shape=[shp] * 3, grid=(rows // tr,), in_specs=[spec] * 4, out_specs=[spec] * 3,
        compiler_params=_params(("parallel",)),
    )(w, g, m, v)


def _adamw_chips(w, parts, m, v, name):
    rows, cols = w.shape
    assert sum(p.shape[1] for p in parts) == cols
    tr = _tile(rows, 64, 16)
    n = len(parts)

    def body(*refs):
        w_ref, m_ref, v_ref = refs[0], refs[1 + n], refs[2 + n]
        g_ref, d_ref, nm_ref, nv_ref = refs[3 + n:]
        cols_g = []
        for p_ref in refs[1:1 + n]:
            acc = p_ref[0].astype(F32)
            for j in range(1, N_CHIPS):
                acc = acc + p_ref[j].astype(F32)
            cols_g.append(acc)
        g = cols_g[0] if n == 1 else jnp.concatenate(cols_g, axis=1)
        d, nm, nv = _adamw_math(w_ref[...], g, m_ref[...], v_ref[...])
        g_ref[...] = g
        d_ref[...] = d
        nm_ref[...] = nm
        nv_ref[...] = nv

    spec = pl.BlockSpec((tr, cols), lambda i: (i, 0))
    part_specs = [pl.BlockSpec((N_CHIPS, tr, p.shape[1]), lambda i: (0, i, 0)) for p in parts]
    shp = jax.ShapeDtypeStruct((rows, cols), F32)
    return _pallas(
        body, name=name, out_shape=[shp] * 4, grid=(rows // tr,),
        in_specs=[spec] + part_specs + [spec, spec], out_specs=[spec] * 4,
        compiler_params=_params(("parallel",)),
    )(w, *[p.reshape(N_CHIPS, rows, p.shape[1]) for p in parts], m, v)


def _device_sum(parts, name):
    rows = parts.shape[0] // N_DEV

    def body(p_ref, g_ref):
        g = p_ref[0]
        for k in range(1, N_DEV):
            g = g + p_ref[k]
        g_ref[...] = g

    return _pallas(body, name=name, out_shape=jax.ShapeDtypeStruct((rows, LANES), F32))(
        parts.reshape(N_DEV, rows, LANES))


def _vmem_footprint(shape):
    dims = (1,) * (2 - len(shape)) + tuple(shape)
    padded = dims[:-2] + (-(-dims[-2] // SUBLANES) * SUBLANES, -(-dims[-1] // LANES) * LANES)
    return _nbytes(padded, F32)


def _adamw_native(ws, gs, ms, vs, name):
    n = len(ws)

    def body(*refs):
        ins, outs = refs[:4 * n], refs[4 * n:]
        for i in range(n):
            d, nm, nv = _adamw_math(ins[i][...], ins[n + i][...], ins[2 * n + i][...], ins[3 * n + i][...])
            outs[i][...] = d
            outs[n + i][...] = nm
            outs[2 * n + i][...] = nv

    shapes = [jax.ShapeDtypeStruct(w.shape, F32) for w in ws] * 3
    vmem = 8 * sum(_vmem_footprint(w.shape) for w in ws)
    res = _pallas(body, name=name, out_shape=shapes, compiler_params=_params(vmem=vmem))(*ws, *gs, *ms, *vs)
    return res[:n], res[n:2 * n], res[2 * n:]


SMALL = ("norm_w", "q_norm_w", "k_norm_w", "sinks", "A_re", "A_im", "log_dt", "B_re", "B_im", "C_re", "C_im",
         "D_skip", "b_glu")
LARGE = ("w_in", "w_attn_proj", "w_glu", "w_ssm_proj", "w_out")
ORDER = ("norm_w", "w_in", "q_norm_w", "k_norm_w", "sinks", "w_attn_proj", "A_re", "A_im", "log_dt", "B_re", "B_im",
         "C_re", "C_im", "D_skip", "w_glu", "b_glu", "w_ssm_proj", "w_out")


SMALL_REST = ("loss",) + SMALL[1:]


def _pack(named, keys):
    flat = jnp.concatenate([named[k].reshape(-1).astype(F32) for k in keys])
    n = flat.shape[0]
    rows = -(-n // (LANES * SUBLANES)) * SUBLANES
    return jnp.pad(flat, (0, rows * LANES - n)).reshape(rows, LANES)


def _unpack(packed, like, keys):
    flat = packed.reshape(-1)
    out, o = {}, 0
    for k in keys:
        n = like[k].size
        out[k] = flat[o:o + n].reshape(like[k].shape)
        o += n
    return out


def _step(xs, target, p, shards):
    s_in, s_ap, s_glu, s_sp, s_o = shards
    seq, d = xs.shape
    attn_w = (d // 128) * HEAD_DIM
    n_q = attn_w // HEAD_DIM
    kv_w = N_KV_HEADS * HEAD_DIM
    ssm_w = d // 2
    n_groups = ssm_w // GROUP
    n_state = n_groups * STATE
    in_w = N_DEV * s_in.shape[0]
    assert in_w == 2 * attn_w + 2 * kv_w + 2 * ssm_w + 2 * d
    o_u = 2 * attn_w + 2 * kv_w
    o_z = o_u + ssm_w
    o_ga = o_z + ssm_w
    chunk = min(SSM_CHUNK, seq)
    cw = d // 4

    norm_row = p["norm_w"].reshape(1, d)
    half = d // W_IN_PARTS
    assert W_IN_PARTS == 2
    s_in_parts = [s_in[:, :half], s_in[:, half:]]
    h, (w_lo,) = _rmsnorm_fwd(xs, norm_row, "rmsnorm_fwd", rider=_all_gather(s_in_parts[:1]))
    part, (w_hi,) = _matmul(Cols(h, 0, half), w_lo, mode="nt", name="in_proj_0", tn=2176, out_dtype=BF16,
                            rider=_all_gather(s_in_parts[1:]))
    proj = _matmul(Cols(h, half, half), w_hi, mode="nt", name="in_proj_1", tn=2176, out_dtype=BF16, add=part)
    w_in_parts = [w_lo, w_hi]
    qw_row = jnp.tile(p["q_norm_w"], n_q).reshape(1, attn_w)
    kw_row = jnp.tile(p["k_norm_w"], N_KV_HEADS).reshape(1, kv_w)
    gmat = _head_mean_matrix()
    ag = _attention_fwd(proj, qw_row, kw_row, gmat, p["sinks"], attn_w=attn_w, kv_w=kv_w, name="attention_fwd")

    log_dt_col = p["log_dt"].reshape(n_groups, 1)
    prep = _ssm_prep(p["A_re"], p["A_im"], log_dt_col, chunk // SUBLANES, "ssm_prep")
    rows_p = [v.reshape(1, n_state) for v in prep[:6]] + [v.reshape(-1, n_state) for v in prep[6:]]
    bt_re, bt_im = p["B_re"].transpose(0, 2, 1), p["B_im"].transpose(0, 2, 1)
    cf_re, cf_im = prep[2][:, None, :], prep[3][:, None, :]
    bc_rows = [_ssm_rows(m) for m in (bt_re, bt_im, p["C_re"], p["C_im"],
                                      cf_re * bt_re - cf_im * bt_im, cf_re * bt_im + cf_im * bt_re)]
    d_row = p["D_skip"].reshape(1, ssm_w)
    (y_ssm, st_re, st_im, yg), (w_ap_t, w_glu_t, w_sp_t, w_o) = _ssm_fwd(
        proj, o_u, bc_rows[:4], rows_p, d_row, chunk=chunk, name="ssm_fwd",
        rider=_all_gather([s_ap, s_glu, s_sp, s_o]))
    glu = _matmul(yg, w_glu_t, mode="nt", name="glu_proj", out_dtype=BF16, bias=p["b_glu"].reshape(1, 2 * ssm_w))
    (ts,) = _ew(lambda ga, gb, z: ga * _sigmoid(gb) * _silu(z), name="glu_gate", rows=seq, width=ssm_w,
                tiles=[(glu, 0), (glu, ssm_w), (proj, o_z)], outs=[(BF16, ssm_w, 0)], cw=cw)
    yy = _matmul(ag, w_ap_t, mode="nt", name="attn_proj", out_dtype=BF16, out_cols=(2 * d, 0))
    yy = _matmul(ts, w_sp_t, mode="nt", name="ssm_proj", out_dtype=BF16, out_cols=(2 * d, d), into=yy)
    (merged,) = _ew(lambda ya, ys, ga, gs: _sigmoid(ga) * ya + _sigmoid(gs) * ys, name="merge", rows=seq, width=d,
                    tiles=[(yy, 0), (yy, d), (proj, o_ga), (proj, o_ga + d)], outs=[(BF16, d, 0)], cw=cw)
    dout, dout_b, loss_cols = _out_proj_loss(merged, w_o, xs, target, "out_proj_loss")
    loss_local = jnp.sum(loss_cols)

    g_w_o = _matmul(merged, dout_b, mode="tn", name="grad_w_out", tm=512, tk=4096)
    dmerged = _matmul(dout_b, w_o, mode="nt", name="d_merged", out_dtype=BF16)

    def merge_bwd(dm, y, g):
        s = _sigmoid(g)
        return dm * s, dm * y * s * (1.0 - s)

    dyy, dproj = _ew(merge_bwd, name="merge_bwd", rows=seq, width=2 * d,
                     tiles=[(dmerged, 0, d), (yy, 0), (proj, o_ga)],
                     outs=[(BF16, 2 * d, 0), (BF16, in_w, o_ga)], cw=cw)
    dy_a, dy_s = Cols(dyy, 0, d), Cols(dyy, d, d)
    g_w_ap_t = _matmul(dy_a, ag, mode="tn", name="grad_w_attn_proj", tm=512, tk=4096)
    g_w_sp_t = _matmul(dy_s, ts, mode="tn", name="grad_w_ssm_proj", tm=512, tk=4096)
    d_ag = _matmul(dy_a, w_ap_t, mode="nn", name="d_attn_gated", out_dtype=BF16)
    d_ts = _matmul(dy_s, w_sp_t, mode="nn", name="d_ssm_gated", out_dtype=BF16)

    (dproj, g_qw, g_kw, g_sinks), (sib_o, sib_ap, sib_sp) = _attention_bwd(
        proj, d_ag, dproj, qw_row, kw_row, gmat, p["sinks"], attn_w=attn_w, kv_w=kv_w, name="attention_bwd",
        rider=_sibling_exchange([g_w_o, g_w_ap_t, g_w_sp_t]))
    pair_o = _pair_sum(g_w_o, sib_o, "pair_sum_w_out")
    pair_ap = _pair_sum(g_w_ap_t, sib_ap, "pair_sum_w_attn_proj")
    pair_sp = _pair_sum(g_w_sp_t, sib_sp, "pair_sum_w_ssm_proj")

    n_half = ssm_w // _tile(2 * ssm_w, cw)

    def glu_bwd(j, dt, ga, gb, z):
        sb, sz = _sigmoid(gb), _silu(z)
        dg = jnp.where(j < n_half, dt * sb * sz, dt * ga * sb * (1.0 - sb) * sz)
        return dg, dg

    glu_ops = [(d_ts, 0, ssm_w), (glu, 0, ssm_w), (glu, ssm_w, ssm_w), (proj, o_z, ssm_w)]
    dglu, g_bglu = _ew(glu_bwd, name="glu_bwd", rows=seq, width=2 * ssm_w, tiles=glu_ops,
                       outs=[(BF16, 2 * ssm_w, 0)], accs=1, cw=cw, with_col=True)
    (dproj,) = _ew(lambda dt, ga, gb, z: dt * ga * _sigmoid(gb) * _dsilu(z), name="glu_bwd_z", rows=seq,
                   width=ssm_w, tiles=glu_ops, outs=[(BF16, in_w, o_z)], into=[dproj], cw=cw)
    g_w_glu_t = _matmul(dglu, yg, mode="tn", name="grad_w_glu", tm=512, tk=4096)
    d_yg = _matmul(dglu, w_glu_t, mode="nn", name="d_gelu", out_dtype=BF16)
    ((du, dbt_re, dbt_im, dc_re, dc_im, gabr, gabi, g_d), (chips_o, chips_ap, chips_sp, sib_glu)) = _ssm_bwd(
        proj, o_u, y_ssm, d_yg, st_re, st_im, bc_rows, rows_p, d_row, chunk=chunk, name="ssm_bwd",
        rider=_join(_chip_exchange([pair_o, pair_ap, pair_sp]), _sibling_exchange([g_w_glu_t])))
    pair_glu = _pair_sum(g_w_glu_t, sib_glu, "pair_sum_w_glu")
    (dproj,) = _ew(lambda v: v, name="du_store", rows=seq, width=ssm_w, tiles=[(du, 0)],
                   outs=[(BF16, in_w, o_u)], into=[dproj], cw=cw)
    g_a_re, g_a_im, g_log_dt, g_bt_re, g_bt_im = _ssm_param_bwd(
        p["A_re"], p["A_im"], log_dt_col, *[g.reshape(SUBLANES, n_groups, STATE) for g in (gabr, gabi)],
        bt_re, bt_im, _from_ssm_rows(dbt_re), _from_ssm_rows(dbt_im), "ssm_param_bwd")
    small_grads = dict(
        loss=loss_local, q_norm_w=g_qw.reshape(n_q, HEAD_DIM).sum(0), k_norm_w=g_kw.reshape(N_KV_HEADS, HEAD_DIM).sum(0),
        sinks=g_sinks[0, :n_q], A_re=g_a_re, A_im=g_a_im, log_dt=g_log_dt.reshape(n_groups),
        B_re=g_bt_re, B_im=g_bt_im,
        C_re=_from_ssm_rows(dc_re), C_im=_from_ssm_rows(dc_im),
        D_skip=g_d.reshape(n_groups, GROUP), b_glu=g_bglu.reshape(2 * ssm_w))

    n_parts = W_IN_PARTS
    wq = d // n_parts
    g_parts, pair_parts, chip_parts = [], [], []
    extra = [_chip_exchange([pair_glu]), _all_gather([_pack(small_grads, SMALL_REST)])]
    chips_glu = small_parts = dh = None
    for step in range(n_parts + 2):
        riders = list(extra) if step == 0 else []
        if 0 <= step - 2 < n_parts:
            riders.append(_chip_exchange([pair_parts[step - 2]]))
        if 0 <= step - 1 < n_parts:
            riders.append(_sibling_exchange([g_parts[step - 1]]))
        rider = _join(*riders) if riders else None
        if step < n_parts:
            res = _matmul(dproj, Cols(h, step * wq, wq), mode="tn", name="grad_w_in_%d" % step, tk=4096, rider=rider)
            out, landed = res if rider is not None else (res, [])
            g_parts.append(out)
        else:
            q = step - n_parts
            dh, landed = _matmul(dproj, w_in_parts[q], mode="nn", name="d_normed_%d" % q, tk=2176,
                                 out_cols=(d, q * wq), into=dh, rider=rider)
        landed = list(landed)
        if step == 0:
            chips_glu, small_parts = landed[:2]
            landed = landed[2:]
        if 0 <= step - 2 < n_parts:
            chip_parts.append(landed.pop(0))
        if 0 <= step - 1 < n_parts:
            pair_parts.append(_pair_sum(g_parts[step - 1], landed.pop(0), "pair_sum_w_in_%d" % (step - 1)))
    grad_x, g_norm = _rmsnorm_bwd(xs, norm_row, dh, dout, "rmsnorm_bwd")
    (norm_parts,) = _exchange(_all_gather([_pack(dict(norm_w=g_norm), ("norm_w",))]), "gather_norm_grad")
    from_chips = dict(zip(LARGE, (chip_parts, [chips_ap], [chips_glu], [chips_sp], [chips_o])))
    return grad_x, from_chips, small_parts, norm_parts


def kernel(x, norm_w, w_in, q_norm_w, k_norm_w, sinks, w_attn_proj, A_re, A_im, log_dt, B_re, B_im, C_re, C_im, D_skip, w_glu, b_glu, w_ssm_proj, w_out, loss_target, m_norm_w, m_w_in, m_q_norm_w, m_k_norm_w, m_sinks, m_w_attn_proj, m_A_re, m_A_im, m_log_dt, m_B_re, m_B_im, m_C_re, m_C_im, m_D_skip, m_w_glu, m_b_glu, m_w_ssm_proj, m_w_out, v_norm_w, v_w_in, v_q_norm_w, v_k_norm_w, v_sinks, v_w_attn_proj, v_A_re, v_A_im, v_log_dt, v_B_re, v_B_im, v_C_re, v_C_im, v_D_skip, v_w_glu, v_b_glu, v_w_ssm_proj, v_w_out):
    weights = dict(norm_w=norm_w, w_in=w_in, q_norm_w=q_norm_w, k_norm_w=k_norm_w, sinks=sinks,
                   w_attn_proj=w_attn_proj, A_re=A_re, A_im=A_im, log_dt=log_dt, B_re=B_re, B_im=B_im, C_re=C_re,
                   C_im=C_im, D_skip=D_skip, w_glu=w_glu, b_glu=b_glu, w_ssm_proj=w_ssm_proj, w_out=w_out)
    m_in = dict(norm_w=m_norm_w, w_in=m_w_in, q_norm_w=m_q_norm_w, k_norm_w=m_k_norm_w, sinks=m_sinks,
                w_attn_proj=m_w_attn_proj, A_re=m_A_re, A_im=m_A_im, log_dt=m_log_dt, B_re=m_B_re, B_im=m_B_im,
                C_re=m_C_re, C_im=m_C_im, D_skip=m_D_skip, w_glu=m_w_glu, b_glu=m_b_glu, w_ssm_proj=m_w_ssm_proj,
                w_out=m_w_out)
    v_in = dict(norm_w=v_norm_w, w_in=v_w_in, q_norm_w=v_q_norm_w, k_norm_w=v_k_norm_w, sinks=v_sinks,
                w_attn_proj=v_w_attn_proj, A_re=v_A_re, A_im=v_A_im, log_dt=v_log_dt, B_re=v_B_re, B_im=v_B_im,
                C_re=v_C_re, C_im=v_C_im, D_skip=v_D_skip, w_glu=v_w_glu, b_glu=v_b_glu, w_ssm_proj=v_w_ssm_proj,
                w_out=v_w_out)

    _, seq, d = x.shape
    column_sharded = LARGE[:4]
    as_rows = lambda k, a: a.T if k in column_sharded else a
    shards = [as_rows(k, weights[k]).astype(BF16) for k in LARGE]
    small = {k: weights[k] for k in SMALL}
    grad_x, from_chips, small_parts, norm_parts = _step(x.reshape(seq, d), loss_target.reshape(seq, d), small,
                                                        shards)

    grads, delta, new_m, new_v = {}, {}, {}, {}
    for k in LARGE:
        if k == "w_in":
            res = _adamw_chips(weights[k].T, from_chips[k], m_in[k].T, v_in[k].T, "adamw_" + k)
            grads[k], delta[k], new_m[k], new_v[k] = [a.T for a in res]
        elif k == "w_out":
            grads[k], delta[k], new_m[k], new_v[k] = _adamw_chips(weights[k], from_chips[k], m_in[k], v_in[k],
                                                                  "adamw_" + k)
        else:
            grads[k] = _chip_sum(from_chips[k][0], "chip_sum_" + k).T
            delta[k], new_m[k], new_v[k] = _adamw(weights[k], grads[k], m_in[k], v_in[k], "adamw_" + k)

    stored = lambda k, a: a.transpose(0, 2, 1) if k in ("B_re", "B_im") else a
    like = dict({k: stored(k, a) for k, a in small.items()}, loss=jnp.zeros((), F32))
    for keys, parts in ((SMALL_REST, small_parts), (("norm_w",), norm_parts)):
        grads.update(_unpack(_device_sum(parts, "device_sum_%d" % len(keys)), like, keys))
    loss = grads["loss"]
    tiled = [k for k in SMALL if weights[k].ndim == 3]
    for k in tiled:
        res = _adamw(stored(k, weights[k]), grads[k], stored(k, m_in[k]), stored(k, v_in[k]), "adamw_" + k)
        delta[k], new_m[k], new_v[k] = [stored(k, r) for r in res]
        grads[k] = stored(k, grads[k])
    whole = [k for k in SMALL if k not in tiled]
    res = _adamw_native(*[[src[k] for k in whole] for src in (weights, grads, m_in, v_in)], "adamw_small")
    for dst, r in zip((delta, new_m, new_v), res):
        dst.update(zip(whole, r))

    return (loss, grad_x.reshape(x.shape), *[grads[k] for k in ORDER], *[delta[k] for k in ORDER],
            *[new_m[k] for k in ORDER], *[new_v[k] for k in ORDER])
```

```python
import math
from typing import Callable, NamedTuple

import jax
import jax.numpy as jnp
import numpy as np
from jax import lax
from jax.experimental import pallas as pl
from jax.experimental.pallas import tpu as pltpu

F32 = jnp.float32
BF16 = jnp.bfloat16
MESH = pl.DeviceIdType.MESH

HEAD_DIM = 64
N_KV_HEADS = 4
GROUP = 16
STATE = 64
BLOCK = 128
NORM_EPS = 1e-6
N_DEV = 8
N_CHIPS = 4
LANES = 128
SUBLANES = 8
MXU_DIM = 256
VMEM_BYTES = 64 * 1024 * 1024
VMEM_CAP = VMEM_BYTES - 8 * 1024 * 1024

ADAM_LR = 0.001
ADAM_B1 = 0.9
ADAM_B2 = 0.999
ADAM_EPS = 1e-08
ADAM_WD = 0.01
ADAM_STEP = 10

GELU_C = math.sqrt(2.0 / math.pi)
GELU_K = 0.044715


def _tile(dim, pref, mult=LANES):
    if dim <= pref:
        return dim
    best = None
    for d in range(mult, pref + 1, mult):
        if dim % d == 0:
            best = d
    assert best is not None, (dim, pref, mult)
    return best


def _params(semantics=None, vmem=None):
    kw = {}
    if semantics is not None:
        kw["dimension_semantics"] = semantics
    if vmem is not None:
        kw["vmem_limit_bytes"] = int(min(VMEM_CAP, max(vmem, 32 * 1024 * 1024)))
    return pltpu.CompilerParams(**kw)


def _nbytes(shape, dtype):
    return math.prod(shape) * jnp.dtype(dtype).itemsize


def _sigmoid(x):
    return 1.0 / (1.0 + jnp.exp(-x))


def _silu(x):
    return x * _sigmoid(x)


def _dsilu(x):
    s = _sigmoid(x)
    return s * (1.0 + x * (1.0 - s))


def _gelu(x):
    return 0.5 * x * (1.0 + jnp.tanh(GELU_C * (x + GELU_K * x * x * x)))


def _dgelu(x):
    t = jnp.tanh(GELU_C * (x + GELU_K * x * x * x))
    return 0.5 * (1.0 + t) + 0.5 * x * (1.0 - t * t) * GELU_C * (1.0 + 3.0 * GELU_K * x * x)


def _dot(a, b, dims):
    return lax.dot_general(a, b, (dims, ((), ())), preferred_element_type=F32)


NN = ((1,), (0,))
NT = ((1,), (1,))
TN = ((0,), (0,))


def _any_spec():
    return pl.BlockSpec(memory_space=pl.ANY)


def _pallas(body, **kw):
    pin = lambda s: pltpu.HBM(s.shape, s.dtype) if isinstance(s, jax.ShapeDtypeStruct) else s
    out_shape = kw.pop("out_shape")
    out_shape = [pin(s) for s in out_shape] if isinstance(out_shape, (list, tuple)) else pin(out_shape)
    call = pl.pallas_call(body, out_shape=out_shape, **kw)

    def run(*operands):
        pinned = [pltpu.with_memory_space_constraint(o, pltpu.HBM) if jnp.issubdtype(o.dtype, jnp.floating) else o
                  for o in operands]
        return call(*pinned)

    return run


class Rider(NamedTuple):
    operands: tuple
    out_shapes: tuple
    sems: tuple
    start: Callable
    finish: Callable


def _all_gather(shards):
    n = len(shards)

    def copies(ins, outs, sems):
        send_sems, recv_sems, local_sems = sems
        x, y, c = lax.axis_index("x"), lax.axis_index("y"), lax.axis_index("c")
        me, sibling = (x, y, c), (x, y, 1 - c)
        chips = [(1 - x, y), (x, 1 - y), (1 - x, 1 - y)]

        def rows(k, px, py, pc):
            r = shards[k].shape[0]
            return outs[k].at[pl.ds((4 * px + 2 * py + pc) * r, r), :]

        def copy(k, s, block, to, src=None):
            return pltpu.make_async_remote_copy(
                src_ref=rows(k, *block) if src is None else src, dst_ref=rows(k, *block),
                send_sem=send_sems.at[7 * k + s], recv_sem=recv_sems.at[7 * k + s],
                device_id=to, device_id_type=MESH)

        mine = [pltpu.make_async_copy(ins[k], rows(k, *me), local_sems.at[k]) for k in range(n)]
        first = []
        for k in range(n):
            first.append(copy(k, 0, me, sibling, src=ins[k]))
            first += [copy(k, 1 + j, me, (*chip, c), src=ins[k]) for j, chip in enumerate(chips)]
        return me, sibling, chips, c, copy, mine, first

    def start(ins, outs, sems):
        *_, mine, first = copies(ins, outs, sems)
        for cp in mine + first:
            cp.start()

    def finish(ins, outs, sems):
        me, sibling, chips, c, copy, mine, first = copies(ins, outs, sems)
        passed = []
        for j, chip in enumerate(chips):
            for k in range(n):
                copy(k, 1 + j, (*chip, c), me).wait_recv()
                fwd = copy(k, 4 + j, (*chip, c), sibling)
                fwd.start()
                passed.append(fwd)
        for k in range(n):
            copy(k, 0, sibling, me).wait_recv()
            for j, chip in enumerate(chips):
                copy(k, 4 + j, (*chip, 1 - c), me).wait_recv()
        for cp in first + passed:
            cp.wait_send()
        for cp in mine:
            cp.wait()

    return Rider(
        tuple(shards),
        tuple(jax.ShapeDtypeStruct((N_DEV * s.shape[0], s.shape[1]), s.dtype) for s in shards),
        (pltpu.SemaphoreType.DMA((7 * n,)), pltpu.SemaphoreType.DMA((7 * n,)), pltpu.SemaphoreType.DMA((n,))),
        start, finish)


def _sibling_exchange(grads):
    n = len(grads)

    def copies(ins, outs, sems):
        send_sems, recv_sems = sems
        x, y, c = lax.axis_index("x"), lax.axis_index("y"), lax.axis_index("c")
        out = []
        for k in range(n):
            r = grads[k].shape[0] // N_DEV
            for j in range(N_CHIPS):
                out.append(pltpu.make_async_remote_copy(
                    src_ref=ins[k].at[pl.ds((2 * j + 1 - c) * r, r), :],
                    dst_ref=outs[k].at[pl.ds(j * r, r), :],
                    send_sem=send_sems.at[N_CHIPS * k + j], recv_sem=recv_sems.at[N_CHIPS * k + j],
                    device_id=(x, y, 1 - c), device_id_type=MESH))
        return out

    def start(ins, outs, sems):
        for cp in copies(ins, outs, sems):
            cp.start()

    def finish(ins, outs, sems):
        for cp in copies(ins, outs, sems):
            cp.wait()

    return Rider(
        tuple(grads), tuple(jax.ShapeDtypeStruct((g.shape[0] // 2, g.shape[1]), g.dtype) for g in grads),
        (pltpu.SemaphoreType.DMA((N_CHIPS * n,)), pltpu.SemaphoreType.DMA((N_CHIPS * n,))), start, finish)


def _chip_exchange(parts):
    n = len(parts)

    def copies(ins, outs, sems):
        send_sems, recv_sems, local_sems = sems
        x, y, c = lax.axis_index("x"), lax.axis_index("y"), lax.axis_index("c")
        my_chip = 2 * x + y
        chips = [(1 - x, y), (x, 1 - y), (1 - x, 1 - y)]
        local, sent = [], []
        for k in range(n):
            r = parts[k].shape[0] // N_CHIPS
            mine = pl.ds(my_chip * r, r)
            local.append(pltpu.make_async_copy(ins[k].at[mine, :], outs[k].at[mine, :], local_sems.at[k]))
            for s, (px, py) in enumerate(chips):
                sent.append(pltpu.make_async_remote_copy(
                    src_ref=ins[k].at[pl.ds((2 * px + py) * r, r), :], dst_ref=outs[k].at[mine, :],
                    send_sem=send_sems.at[3 * k + s], recv_sem=recv_sems.at[3 * k + s],
                    device_id=(px, py, c), device_id_type=MESH))
        return local, sent

    def start(ins, outs, sems):
        local, sent = copies(ins, outs, sems)
        for cp in local + sent:
            cp.start()

    def finish(ins, outs, sems):
        local, sent = copies(ins, outs, sems)
        for cp in sent + local:
            cp.wait()

    return Rider(
        tuple(parts), tuple(jax.ShapeDtypeStruct(p.shape, p.dtype) for p in parts),
        (pltpu.SemaphoreType.DMA((3 * n,)), pltpu.SemaphoreType.DMA((3 * n,)), pltpu.SemaphoreType.DMA((n,))),
        start, finish)


def _join(*riders):
    cuts_in, cuts_out, cuts_sem = [0], [0], [0]
    for r in riders:
        cuts_in.append(cuts_in[-1] + len(r.operands))
        cuts_out.append(cuts_out[-1] + len(r.out_shapes))
        cuts_sem.append(cuts_sem[-1] + len(r.sems))

    def each(which):
        def run(ins, outs, sems):
            for i, r in enumerate(riders):
                getattr(r, which)(ins[cuts_in[i]:cuts_in[i + 1]], outs[cuts_out[i]:cuts_out[i + 1]],
                                  sems[cuts_sem[i]:cuts_sem[i + 1]])
        return run

    return Rider(sum((r.operands for r in riders), ()), sum((r.out_shapes for r in riders), ()),
                 sum((r.sems for r in riders), ()), each("start"), each("finish"))


def _call(body, operands, *, name, out_shape, grid, in_specs, out_specs, scratch_shapes=(), aliases=None,
          semantics=None, vmem=None, rider=None):
    operands, out_shape, scratch_shapes = list(operands), list(out_shape), list(scratch_shapes)
    in_specs, out_specs = list(in_specs), list(out_specs)
    if rider is None:
        res = _pallas(
            body, name=name, out_shape=out_shape, grid=grid, in_specs=in_specs, out_specs=out_specs,
            scratch_shapes=scratch_shapes, input_output_aliases=aliases or {},
            compiler_params=_params(semantics, vmem))(*operands)
        return list(res), []
    n_in, n_out, n_scr = len(operands), len(out_shape), len(scratch_shapes)
    ri, ro = len(rider.operands), len(rider.out_shapes)

    def carried(*refs):
        a, b = n_in, n_in + ri
        c, d = b + n_out, b + n_out + ro
        e = d + n_scr
        ids = [pl.program_id(k) for k in range(len(grid))]
        first = ids[0] == 0
        last = ids[0] == grid[0] - 1
        for k in range(1, len(grid)):
            first = jnp.logical_and(first, ids[k] == 0)
            last = jnp.logical_and(last, ids[k] == grid[k] - 1)

        @pl.when(first)
        def _():
            rider.start(refs[a:b], refs[c:d], refs[e:])

        body(*refs[:a], *refs[b:c], *refs[d:e])

        @pl.when(last)
        def _():
            rider.finish(refs[a:b], refs[c:d], refs[e:])

    res = _pallas(
        carried, name=name, out_shape=out_shape + list(rider.out_shapes), grid=grid,
        in_specs=in_specs + [_any_spec()] * ri, out_specs=out_specs + [_any_spec()] * ro,
        scratch_shapes=scratch_shapes + list(rider.sems), input_output_aliases=aliases or {},
        compiler_params=_params(("arbitrary",) * len(grid), vmem))(*operands, *rider.operands)
    return list(res[:n_out]), list(res[n_out:])


def _exchange(rider, name):
    ri, ro = len(rider.operands), len(rider.out_shapes)

    def body(*refs):
        rider.start(refs[:ri], refs[ri:ri + ro], refs[ri + ro:])
        rider.finish(refs[:ri], refs[ri:ri + ro], refs[ri + ro:])

    return _pallas(
        body, name=name, out_shape=list(rider.out_shapes), in_specs=[_any_spec()] * ri,
        out_specs=[_any_spec()] * ro, scratch_shapes=list(rider.sems))(*rider.operands)


class Cols(NamedTuple):
    arr: jax.Array
    off: int
    width: int


def _cols(a):
    return a if isinstance(a, Cols) else Cols(a, 0, a.shape[1])


def _matmul(a, b, *, mode, name, out_dtype=F32, tm=1024, tn=1024, tk=2048, bias=None, add=None, out_cols=None,
            into=None, rider=None):
    a, b = _cols(a), _cols(b)
    if mode == "nn":
        (m, k), (k2, n) = (a.arr.shape[0], a.width), (b.arr.shape[0], b.width)
    elif mode == "nt":
        (m, k), (n, k2) = (a.arr.shape[0], a.width), (b.arr.shape[0], b.width)
    else:
        (k, m), (k2, n) = (a.arr.shape[0], a.width), (b.arr.shape[0], b.width)
    assert k == k2, (a.arr.shape, b.arr.shape, mode)
    tm, tn, tk = _tile(m, tm), _tile(n, tn), _tile(k, tk)
    nk = k // tk
    dims = {"nn": NN, "nt": NT, "tn": TN}[mode]
    if mode == "tn":
        assert a.off % tm == 0
        a_spec = pl.BlockSpec((tk, tm), lambda i, j, kk, o=a.off // tm: (kk, i + o))
    else:
        assert a.off % tk == 0
        a_spec = pl.BlockSpec((tm, tk), lambda i, j, kk, o=a.off // tk: (i, kk + o))
    if mode == "nt":
        assert b.off % tk == 0
        b_spec = pl.BlockSpec((tn, tk), lambda i, j, kk, o=b.off // tk: (j, kk + o))
    else:
        assert b.off % tn == 0
        b_spec = pl.BlockSpec((tk, tn), lambda i, j, kk, o=b.off // tn: (kk, j + o))
    in_specs, operands = [a_spec, b_spec], [a.arr, b.arr]
    assert bias is None or add is None
    if bias is not None:
        in_specs.append(pl.BlockSpec((1, tn), lambda i, j, kk: (0, j)))
        operands.append(bias)
    if add is not None:
        assert add.shape == (m, n)
        in_specs.append(pl.BlockSpec((tm, tn), lambda i, j, kk: (i, j)))
        operands.append(add)
    total_w, o_off = out_cols if out_cols is not None else (n, 0)
    assert o_off % tn == 0
    aliases = {}
    if into is not None:
        assert into.shape == (m, total_w) and into.dtype == out_dtype
        in_specs.append(_any_spec())
        operands.append(into)
        aliases = {len(operands) - 1: 0}
    n_in = len(operands)

    def body(*refs):
        a_ref, b_ref = refs[0], refs[1]
        bias_ref = refs[2] if bias is not None or add is not None else None
        o_ref = refs[n_in]
        acc_ref = refs[-1] if nk > 1 else None
        part = _dot(a_ref[...].astype(BF16), b_ref[...].astype(BF16), dims)

        def finish(acc):
            if bias_ref is not None:
                acc = acc + bias_ref[...]
            o_ref[...] = acc.astype(out_dtype)

        if nk == 1:
            finish(part)
        else:
            kk = pl.program_id(2)

            @pl.when(kk == 0)
            def _():
                acc_ref[...] = part

            @pl.when(kk > 0)
            def _():
                acc_ref[...] += part

            @pl.when(kk == nk - 1)
            def _():
                finish(acc_ref[...])

    vmem = 2 * (_nbytes((tm, tk), a.arr.dtype) + _nbytes((tk, tn), b.arr.dtype) + _nbytes((tm, tn), out_dtype))
    vmem += 3 * _nbytes((tm, tn), F32)
    (out,), landed = _call(
        body, operands, name=name, out_shape=[jax.ShapeDtypeStruct((m, total_w), out_dtype)],
        grid=(m // tm, n // tn, nk), in_specs=in_specs,
        out_specs=[pl.BlockSpec((tm, tn), lambda i, j, kk, o=o_off // tn: (i, j + o))],
        scratch_shapes=[pltpu.VMEM((tm, tn), F32)] if nk > 1 else [], aliases=aliases,
        semantics=("parallel", "parallel", "arbitrary"), vmem=vmem, rider=rider)
    return out if rider is None else (out, landed)


def _ew(fn, *, name, rows, width, tiles, vecs=(), outs, accs=0, tl=1024, cw=512, into=None, with_col=False):
    tl, cw = _tile(rows, tl, SUBLANES), _tile(width, cw)
    ncol = width // cw
    nt_, nv = len(tiles), len(vecs)
    into = list(into) if into is not None else [None] * len(outs)
    aliased = [t for t in into if t is not None]

    def off(o):
        assert o % cw == 0, (name, o, cw)
        return o // cw

    in_specs, vmem = [], 0
    for t in tiles:
        arr, o = t[0], off(t[1])
        wrap = t[2] // cw if len(t) > 2 else ncol
        in_specs.append(pl.BlockSpec((tl, cw), lambda j, i, o=o, wrap=wrap: (i, o + j % wrap)))
        vmem += _nbytes((tl, cw), arr.dtype)
    in_specs += [pl.BlockSpec((1, cw), lambda j, i, o=off(o): (0, j + o)) for _, o in vecs]
    in_specs += [_any_spec() for _ in aliased]
    out_shape, out_specs, aliases = [], [], {}
    n_in = nt_ + nv
    for idx, ((dt, tw, o), tgt) in enumerate(zip(outs, into)):
        out_shape.append(jax.ShapeDtypeStruct((rows, tw), dt))
        out_specs.append(pl.BlockSpec((tl, cw), lambda j, i, o=off(o): (i, j + o)))
        vmem += _nbytes((tl, cw), dt)
        if tgt is not None:
            assert tgt.shape == (rows, tw) and tgt.dtype == dt, (name, tgt.shape, tgt.dtype)
            aliases[n_in + len(aliases)] = idx
    for _ in range(accs):
        out_shape.append(jax.ShapeDtypeStruct((1, width), F32))
        out_specs.append(pl.BlockSpec((1, cw), lambda j, i: (0, j)))
    n_out = len(outs)

    def body(*refs):
        vals = [r[...].astype(F32) for r in refs[:n_in]]
        out_refs = refs[n_in + len(aliased):]
        res = fn(pl.program_id(0), *vals) if with_col else fn(*vals)
        res = res if isinstance(res, (tuple, list)) else (res,)
        assert len(res) == n_out + accs, (name, len(res))
        for r, v in zip(out_refs[:n_out], res[:n_out]):
            r[...] = v.astype(r.dtype)
        first = pl.program_id(1) == 0
        for r, v in zip(out_refs[n_out:], res[n_out:]):
            s = jnp.sum(v, axis=0, keepdims=True)

            @pl.when(first)
            def _(r=r, s=s):
                r[...] = s

            @pl.when(jnp.logical_not(first))
            def _(r=r, s=s):
                r[...] += s

    return _pallas(
        body, name=name, out_shape=out_shape, grid=(ncol, rows // tl),
        in_specs=in_specs, out_specs=out_specs, input_output_aliases=aliases,
        compiler_params=_params(("parallel", "arbitrary"), 3 * vmem),
    )(*[t[0] for t in tiles], *[v for v, _ in vecs], *aliased)


def _rmsnorm_fwd(x, w_row, name, rider=None):
    rows, d = x.shape
    tl = _tile(rows, 512, SUBLANES)

    def body(x_ref, w_ref, h_ref):
        xv = x_ref[...]
        rstd = lax.rsqrt(jnp.mean(xv * xv, axis=-1, keepdims=True) + NORM_EPS)
        h_ref[...] = (xv * rstd * w_ref[...]).astype(BF16)

    (h,), landed = _call(
        body, [x, w_row], name=name, out_shape=[jax.ShapeDtypeStruct((rows, d), BF16)], grid=(rows // tl,),
        in_specs=[pl.BlockSpec((tl, d), lambda i: (i, 0)), pl.BlockSpec((1, d), lambda i: (0, 0))],
        out_specs=[pl.BlockSpec((tl, d), lambda i: (i, 0))], semantics=("parallel",), rider=rider)
    return h if rider is None else (h, landed)


def _rmsnorm_bwd(x, w_row, dh, dout, name, rider=None):
    rows, d = x.shape
    tl = _tile(rows, 256, SUBLANES)

    def body(x_ref, w_ref, dh_ref, dout_ref, gx_ref, gw_ref):
        xv = x_ref[...]
        rstd = lax.rsqrt(jnp.mean(xv * xv, axis=-1, keepdims=True) + NORM_EPS)
        xn = xv * rstd
        dhv = dh_ref[...]
        dxn = dhv * w_ref[...]
        dx = rstd * (dxn - xn * jnp.mean(dxn * xn, axis=-1, keepdims=True))
        gx_ref[...] = dout_ref[...] + dx
        gw = jnp.sum(dhv * xn, axis=0, keepdims=True)

        @pl.when(pl.program_id(0) == 0)
        def _():
            gw_ref[...] = gw

        @pl.when(pl.program_id(0) > 0)
        def _():
            gw_ref[...] += gw

    tile = pl.BlockSpec((tl, d), lambda i: (i, 0))
    row = pl.BlockSpec((1, d), lambda i: (0, 0))
    res, landed = _call(
        body, [x, w_row, dh, dout], name=name,
        out_shape=[jax.ShapeDtypeStruct((rows, d), F32), jax.ShapeDtypeStruct((1, d), F32)],
        grid=(rows // tl,), in_specs=[tile, row, tile, tile], out_specs=[tile, row],
        semantics=("arbitrary",), rider=rider)
    return res if rider is None else (res, landed)


def _head_mean(x, gmat):
    hi = x.astype(BF16)
    lo = (x - hi.astype(F32)).astype(BF16)
    out = []
    for s in range(x.shape[1] // MXU_DIM):
        sl = slice(s * MXU_DIM, (s + 1) * MXU_DIM)
        out.append(_dot(hi[:, sl], gmat, NN) + _dot(lo[:, sl], gmat, NN))
    return out[0] if len(out) == 1 else jnp.concatenate(out, axis=1)


def _head_mean_matrix():
    blk = jnp.arange(MXU_DIM) // HEAD_DIM
    return jnp.where(blk[:, None] == blk[None, :], 1.0 / HEAD_DIM, 0.0).astype(BF16)


def _spread_head(x, g, width):
    col = x[:, (g // 2) * LANES:(g // 2 + 1) * LANES]
    other = pltpu.roll(col, HEAD_DIM, axis=1)
    low = lax.broadcasted_iota(jnp.int32, col.shape, 1) < HEAD_DIM
    both = jnp.where(low, col, other) if g % 2 == 0 else jnp.where(low, other, col)
    return both if width == LANES else jnp.concatenate([both] * (width // LANES), axis=1)


def _head_diagonal(t, per_kv):
    head = lax.broadcasted_iota(jnp.int32, t.shape, 1) // HEAD_DIM
    zero = jnp.zeros_like(t)
    return jnp.concatenate([jnp.where(head == r, t, zero) for r in range(per_kv)], axis=0)


def _fold_heads(x, per_kv):
    rows = x.shape[0] // per_kv
    head = lax.broadcasted_iota(jnp.int32, (rows, x.shape[1]), 1) // HEAD_DIM
    acc = jnp.where(head == 0, x[0:rows], 0.0)
    for r in range(1, per_kv):
        acc = acc + jnp.where(head == r, x[r * rows:(r + 1) * rows], 0.0)
    while acc.shape[1] > LANES:
        half = acc.shape[1] // 2
        acc = acc[:, :half] + acc[:, half:]
    return acc + pltpu.roll(acc, HEAD_DIM, axis=1)


def _join_heads(parts):
    low = lax.broadcasted_iota(jnp.int32, parts[0].shape, 1) < HEAD_DIM
    cols = [jnp.where(low, parts[2 * j], parts[2 * j + 1]) for j in range(len(parts) // 2)]
    return cols[0] if len(cols) == 1 else jnp.concatenate(cols, axis=1)


def _attn_specs(attn_w, kv_w, block=lambda s: s):
    half = attn_w // 2
    kcol, vcol = attn_w // kv_w, attn_w // kv_w + 1
    gcol = (attn_w + 2 * kv_w) // half
    prev = lambda s: jnp.maximum(block(s) - 1, 0)
    return [
        pl.BlockSpec((BLOCK, attn_w), lambda s: (block(s), 0)),
        pl.BlockSpec((BLOCK, kv_w), lambda s: (prev(s), kcol)),
        pl.BlockSpec((BLOCK, kv_w), lambda s: (block(s), kcol)),
        pl.BlockSpec((BLOCK, kv_w), lambda s: (prev(s), vcol)),
        pl.BlockSpec((BLOCK, kv_w), lambda s: (block(s), vcol)),
        pl.BlockSpec((BLOCK, half), lambda s: (block(s), gcol)),
        pl.BlockSpec((BLOCK, half), lambda s: (block(s), gcol + 1)),
    ]


def _band_mask(i):
    q_loc = lax.broadcasted_iota(jnp.int32, (BLOCK, 2 * BLOCK), 0) + BLOCK
    k_loc = lax.broadcasted_iota(jnp.int32, (BLOCK, 2 * BLOCK), 1)
    diff = q_loc - k_loc
    first_key = jnp.where(i == 0, BLOCK, 0)
    return (diff >= 0) & (diff < BLOCK) & (k_loc >= first_key)


def _softmax_with_sink(s, sink):
    m = jnp.maximum(jnp.max(s, axis=-1, keepdims=True), sink)
    p = jnp.exp(s - m)
    e_sink = jnp.exp(sink - m)
    den = jnp.sum(p, axis=-1, keepdims=True) + e_sink
    inv = 1.0 / den
    return p * inv, e_sink * inv


def _attn_block(i, q, kk, vv, qw, kw, gmat, sink_ref, per_kv):
    scale = 1.0 / math.sqrt(HEAD_DIM)
    keys = 2 * BLOCK
    valid = _band_mask(i)
    q_rstd = lax.rsqrt(_head_mean(q * q, gmat) + NORM_EPS)
    qn = q * q_rstd
    qh = (qn * qw).astype(BF16)
    k_rstd = lax.rsqrt(_head_mean(kk * kk, gmat) + NORM_EPS)
    kn = kk * k_rstd
    kh = kn * kw
    gw = per_kv * HEAD_DIM
    groups = []
    for g in range(N_KV_HEADS):
        kd = _head_diagonal(_spread_head(kh, g, gw).astype(BF16), per_kv)
        vd = _head_diagonal(_spread_head(vv, g, gw).astype(BF16), per_kv)
        qg = qh[:, g * gw:(g + 1) * gw]
        s_all = _dot(qg, kd, NT) * scale
        ps, p_sinks = [], []
        for r in range(per_kv):
            s = jnp.where(valid, s_all[:, r * keys:(r + 1) * keys], -1e30)
            p, p_sink = _softmax_with_sink(s, sink_ref[g * per_kv + r])
            ps.append(p)
            p_sinks.append(p_sink)
        pb = jnp.concatenate(ps, axis=1).astype(BF16)
        groups.append((kd, vd, qg, ps, p_sinks, pb, _dot(pb, vd, NN)))
    return qn, q_rstd, kn, k_rstd, groups


def _attention_fwd(proj, qw_row, kw_row, gmat, sinks, *, attn_w, kv_w, name):
    rows = proj.shape[0]
    per_kv = attn_w // HEAD_DIM // N_KV_HEADS

    def body(q_ref, kp_ref, kc_ref, vp_ref, vc_ref, glo_ref, ghi_ref, qw_ref, kw_ref, gm_ref, sink_ref, o_ref):
        kk = jnp.concatenate([kp_ref[...], kc_ref[...]], axis=0).astype(F32)
        vv = jnp.concatenate([vp_ref[...], vc_ref[...]], axis=0).astype(F32)
        gate = jnp.concatenate([glo_ref[...], ghi_ref[...]], axis=1).astype(F32)
        *_, groups = _attn_block(pl.program_id(0), q_ref[...].astype(F32), kk, vv, qw_ref[...], kw_ref[...], gm_ref[...],
                                 sink_ref, per_kv)
        attn = jnp.concatenate([grp[-1] for grp in groups], axis=1)
        o_ref[...] = (attn * _silu(gate)).astype(BF16)

    const = lambda a: pl.BlockSpec(a.shape, lambda i: (0, 0))
    return _pallas(
        body, name=name, out_shape=jax.ShapeDtypeStruct((rows, attn_w), BF16), grid=(rows // BLOCK,),
        in_specs=_attn_specs(attn_w, kv_w) + [const(qw_row), const(kw_row), const(gmat),
                                              pl.BlockSpec(memory_space=pltpu.SMEM)],
        out_specs=pl.BlockSpec((BLOCK, attn_w), lambda i: (i, 0)),
        compiler_params=_params(("parallel",), 40 * 1024 * 1024),
    )(proj, proj, proj, proj, proj, proj, proj, qw_row, kw_row, gmat, sinks)


def _attention_bwd(proj, d_ag, dproj, qw_row, kw_row, gmat, sinks, *, attn_w, kv_w, name, rider=None):
    rows = proj.shape[0]
    nb = rows // BLOCK
    per_kv = attn_w // HEAD_DIM // N_KV_HEADS
    gw = per_kv * HEAD_DIM
    keys = 2 * BLOCK
    scale = 1.0 / math.sqrt(HEAD_DIM)
    w_out = 2 * attn_w + 2 * kv_w
    rev = lambda s: nb - 1 - s

    def body(q_ref, kp_ref, kc_ref, vp_ref, vc_ref, glo_ref, ghi_ref, dag_ref, qw_ref, kw_ref, gm_ref, sink_ref, _,
             dp_ref, gqw_ref, gkw_ref, gs_ref, carry_ref):
        step = pl.program_id(0)
        i = rev(step)
        kk = jnp.concatenate([kp_ref[...], kc_ref[...]], axis=0).astype(F32)
        vv = jnp.concatenate([vp_ref[...], vc_ref[...]], axis=0).astype(F32)
        gate = jnp.concatenate([glo_ref[...], ghi_ref[...]], axis=1).astype(F32)
        d_ag_v = dag_ref[...].astype(F32)
        qw, kw, gmat_v = qw_ref[...], kw_ref[...], gm_ref[...]
        qn, q_rstd, kn, k_rstd, groups = _attn_block(i, q_ref[...].astype(F32), kk, vv, qw, kw, gmat_v, sink_ref,
                                                     per_kv)
        lane = lax.broadcasted_iota(jnp.int32, (SUBLANES, LANES), 1)
        sub = lax.broadcasted_iota(jnp.int32, (SUBLANES, LANES), 0)
        gsink = jnp.zeros((SUBLANES, LANES), F32)
        dq_groups, dgate_groups, dk_heads, dv_heads = [], [], [], []
        for g, (kd, vd, qg, ps, p_sinks, pb, o) in enumerate(groups):
            cs = slice(g * gw, (g + 1) * gw)
            gate_g, d_ag_g = gate[:, cs], d_ag_v[:, cs]
            dgate_groups.append(d_ag_g * o * _dsilu(gate_g))
            do = (d_ag_g * _silu(gate_g)).astype(BF16)
            dp_all = _dot(do, vd, NT)
            dss = []
            for r in range(per_kv):
                p, dp = ps[r], dp_all[:, r * keys:(r + 1) * keys]
                delta = jnp.sum(p * dp, axis=-1, keepdims=True)
                dss.append(p * (dp - delta) * scale)
                gs_h = jnp.sum(-p_sinks[r] * delta, axis=0, keepdims=True)
                gsink = gsink + jnp.where((lane == g * per_kv + r) & (sub == 0), gs_h, 0.0)
            ds = jnp.concatenate(dss, axis=1).astype(BF16)
            dq_groups.append(_dot(ds, kd, NN))
            dk_heads.append(_fold_heads(_dot(ds, qg, TN), per_kv))
            dv_heads.append(_fold_heads(_dot(pb, do, TN), per_kv))
        dqh = jnp.concatenate(dq_groups, axis=1)
        gqw = jnp.sum(dqh * qn, axis=0, keepdims=True)
        dqn = dqh * qw
        dq = q_rstd * (dqn - qn * _head_mean(dqn * qn, gmat_v))
        dkh = _join_heads(dk_heads)
        gkw = jnp.sum(dkh * kn, axis=0, keepdims=True)
        dkn = dkh * kw
        dk = k_rstd * (dkn - kn * _head_mean(dkn * kn, gmat_v))
        dkv = jnp.concatenate([dk, _join_heads(dv_heads)], axis=1)

        @pl.when(step == 0)
        def _():
            carry_ref[...] = jnp.zeros_like(carry_ref)
            gqw_ref[...] = gqw
            gkw_ref[...] = gkw
            gs_ref[...] = gsink

        @pl.when(step > 0)
        def _():
            gqw_ref[...] += gqw
            gkw_ref[...] += gkw
            gs_ref[...] += gsink

        dp_ref[:, 0:attn_w] = dq.astype(BF16)
        dp_ref[:, attn_w:attn_w + 2 * kv_w] = (dkv[BLOCK:2 * BLOCK, :] + carry_ref[...]).astype(BF16)
        dp_ref[:, attn_w + 2 * kv_w:w_out] = jnp.concatenate(dgate_groups, axis=1).astype(BF16)
        carry_ref[...] = dkv[0:BLOCK, :]

    const = lambda a: pl.BlockSpec(a.shape, lambda s: (0, 0))
    res, landed = _call(
        body, [proj, proj, proj, proj, proj, proj, proj, d_ag, qw_row, kw_row, gmat, sinks, dproj], name=name,
        out_shape=[jax.ShapeDtypeStruct(dproj.shape, BF16),
                   jax.ShapeDtypeStruct(qw_row.shape, F32), jax.ShapeDtypeStruct(kw_row.shape, F32),
                   jax.ShapeDtypeStruct((SUBLANES, LANES), F32)],
        grid=(nb,),
        in_specs=_attn_specs(attn_w, kv_w, rev) + [pl.BlockSpec((BLOCK, attn_w), lambda s: (rev(s), 0)),
                                                   const(qw_row), const(kw_row), const(gmat),
                                                   pl.BlockSpec(memory_space=pltpu.SMEM), _any_spec()],
        out_specs=[pl.BlockSpec((BLOCK, w_out), lambda s: (rev(s), 0)),
                   const(qw_row), const(kw_row), pl.BlockSpec((SUBLANES, LANES), lambda s: (0, 0))],
        scratch_shapes=[pltpu.VMEM((BLOCK, 2 * kv_w), F32)],
        aliases={12: 0}, semantics=("arbitrary",), vmem=48 * 1024 * 1024, rider=rider)
    return res if rider is None else (res, landed)


def _cmul(ar, ai, br, bi):
    return ar * br - ai * bi, ar * bi + ai * br


def _ssm_prep(a_re, a_im, log_dt_col, steps, name):
    def body(are_ref, aim_ref, ldt_ref, abr_ref, abi_ref, cfr_ref, cfi_ref, apr_ref, api_ref, pwr_ref, pwi_ref):
        are, aim = are_ref[...], aim_ref[...]
        dt = jnp.exp(ldt_ref[...])
        mag = jnp.exp(dt * are)
        abr = mag * jnp.cos(dt * aim)
        abi = mag * jnp.sin(dt * aim)
        num_re, num_im = abr - 1.0, abi
        den = are * are + aim * aim
        abr_ref[...] = abr
        abi_ref[...] = abi
        cfr_ref[...] = (num_re * are + num_im * aim) / den
        cfi_ref[...] = (num_im * are - num_re * aim) / den
        pr, pi = jnp.ones_like(abr), jnp.zeros_like(abr)
        for k in range(steps):
            pwr_ref[k] = pr
            pwi_ref[k] = pi
            pr, pi = _cmul(pr, pi, abr, abi)
        apr_ref[...] = pr
        api_ref[...] = pi

    shp = jax.ShapeDtypeStruct(a_re.shape, F32)
    pows = jax.ShapeDtypeStruct((steps,) + a_re.shape, F32)
    return _pallas(body, name=name, out_shape=[shp] * 6 + [pows] * 2)(a_re, a_im, log_dt_col)


def _ssm_param_bwd(a_re, a_im, log_dt_col, d_ab_re, d_ab_im, b_re, b_im, dbt_re, dbt_im, name):
    def body(are_ref, aim_ref, ldt_ref, gabr_ref, gabi_ref, br_ref, bi_ref, tr_ref, ti_ref,
             dar_ref, dai_ref, dldt_ref, dbr_ref, dbi_ref):
        are, aim = are_ref[...], aim_ref[...]
        dt = jnp.exp(ldt_ref[...])
        mag = jnp.exp(dt * are)
        abr = mag * jnp.cos(dt * aim)
        abi = mag * jnp.sin(dt * aim)
        den = are * are + aim * aim
        cfr = ((abr - 1.0) * are + abi * aim) / den
        cfi = (abi * are - (abr - 1.0) * aim) / den
        gabr, gabi = jnp.sum(gabr_ref[...], axis=0), jnp.sum(gabi_ref[...], axis=0)
        t_re, t_im = tr_ref[...], ti_ref[...]
        g_r, g_i = _cmul(br_ref[...], -bi_ref[...], t_re, t_im)
        gcfr, gcfi = jnp.sum(g_r, axis=1), jnp.sum(g_i, axis=1)
        dbr, dbi = _cmul(cfr[:, None, :], -cfi[:, None, :], t_re, t_im)
        dbr_ref[...] = dbr
        dbi_ref[...] = dbi
        inv_r, inv_i = are / den, -aim / den
        t_r, t_i = _cmul(inv_r, -inv_i, gcfr, gcfi)
        gabr, gabi = gabr + t_r, gabi + t_i
        q_r, q_i = _cmul(cfr, cfi, inv_r, inv_i)
        da_r, da_i = _cmul(-q_r, q_i, gcfr, gcfi)
        gz_r, gz_i = _cmul(abr, -abi, gabr, gabi)
        dar_ref[...] = da_r + dt * gz_r
        dai_ref[...] = da_i + dt * gz_i
        dldt_ref[...] = dt * jnp.sum(are * gz_r + aim * gz_i, axis=-1, keepdims=True)

    shp = jax.ShapeDtypeStruct(a_re.shape, F32)
    bshp = jax.ShapeDtypeStruct(b_re.shape, F32)
    return _pallas(body, name=name,
                   out_shape=[shp, shp, jax.ShapeDtypeStruct(log_dt_col.shape, F32), bshp, bshp])(
        a_re, a_im, log_dt_col, d_ab_re, d_ab_im, b_re, b_im, dbt_re, dbt_im)


SCAN_LANES = 1024
SSM_CHUNK = 256
SCAN_UNROLL = 8
W_IN_PARTS = 2


def _scan_segments(xr_ref, xi_ref, a_re, a_im, ap_re, ap_im, pw_re, pw_im, carry_re, carry_im, cm_re, cm_im, steps,
                   reverse, base):
    n = xr_ref.shape[1]
    seg_order = range(SUBLANES - 1, -1, -1) if reverse else range(SUBLANES)
    sign = -1.0 if reverse else 1.0
    for c0 in range(0, n, SCAN_LANES):
        ls = slice(c0, c0 + SCAN_LANES)
        gs = slice(base + c0, base + c0 + SCAN_LANES)
        ar = jnp.broadcast_to(a_re[:, gs], (SUBLANES, SCAN_LANES))
        ai = jnp.broadcast_to(a_im[:, gs], (SUBLANES, SCAN_LANES))
        end_r = jnp.zeros((SUBLANES, SCAN_LANES), F32)
        end_i = jnp.zeros((SUBLANES, SCAN_LANES), F32)
        for j in range(steps):
            k = j if reverse else steps - 1 - j
            rws = slice(j * SUBLANES, (j + 1) * SUBLANES)
            tr, ti = _cmul(pw_re[k:k + 1, gs], sign * pw_im[k:k + 1, gs], xr_ref[rws, ls], xi_ref[rws, ls])
            end_r, end_i = end_r + tr, end_i + ti
        cr, ci = carry_re[:, gs], carry_im[:, gs]
        apr, api = ap_re[:, gs], ap_im[:, gs]
        for r in seg_order:
            cm_re[r:r + 1, gs] = cr
            cm_im[r:r + 1, gs] = ci
            tr, ti = _cmul(apr, api, cr, ci)
            cr, ci = end_r[r:r + 1, :] + tr, end_i[r:r + 1, :] + ti
        carry_re[:, gs] = cr
        carry_im[:, gs] = ci

        def run(t, s, ar=ar, ai=ai, ls=ls):
            sr, si = s
            for k in range(SCAN_UNROLL):
                j = steps - 1 - (t * SCAN_UNROLL + k) if reverse else t * SCAN_UNROLL + k
                r0 = pl.multiple_of(j * SUBLANES, SUBLANES)
                sr, si = _cmul(ar, ai, sr, si)
                sr = sr + xr_ref[pl.ds(r0, SUBLANES), ls]
                si = si + xi_ref[pl.ds(r0, SUBLANES), ls]
                xr_ref[pl.ds(r0, SUBLANES), ls] = sr
                xi_ref[pl.ds(r0, SUBLANES), ls] = si
            return sr, si

        assert steps % SCAN_UNROLL == 0
        lax.fori_loop(0, steps // SCAN_UNROLL, run, (cm_re[:, gs], cm_im[:, gs]))


SB_GROUPS = MXU_DIM // GROUP
SB_STATE = SB_GROUPS * STATE


def _ssm_rows(m):
    flat = m.reshape(-1, STATE).astype(F32)
    return jnp.concatenate([flat, flat], axis=1)


def _from_ssm_rows(rows):
    return rows[:, :STATE].reshape(-1, GROUP, STATE)


def _own_group(shape):
    row_g = lax.broadcasted_iota(jnp.int32, shape, 0) // GROUP
    col_g = lax.broadcasted_iota(jnp.int32, shape, 1) // STATE
    return row_g == col_g


def _block_diagonal(rows):
    tiled = jnp.concatenate([rows] * (SB_STATE // LANES), axis=1)
    return jnp.where(_own_group(tiled.shape), tiled, 0.0).astype(BF16)


def _block_rows(acc):
    x = jnp.where(_own_group(acc.shape), acc, 0.0)
    while x.shape[1] > LANES:
        half = x.shape[1] // 2
        x = x[:, :half] + x[:, half:]
    return x + pltpu.roll(x, STATE, axis=1)


def _rows_to_segments(dst, srcs, steps, stage):
    for ref, off in srcs:
        for k in range(ref.shape[1] // LANES):
            stage[off // LANES + k] = ref[:, k * LANES:(k + 1) * LANES].astype(F32)
    for k in range(dst.shape[1] // LANES):
        for j in range(steps):
            dst[j * SUBLANES:(j + 1) * SUBLANES, k * LANES:(k + 1) * LANES] = (
                stage[k, pl.ds(j, SUBLANES, stride=steps), :])


def _segments_to_rows(dst, src, steps, stage):
    for k in range(src.shape[1] // LANES):
        for j in range(steps):
            stage[k, pl.ds(j, SUBLANES, stride=steps), :] = (
                src[j * SUBLANES:(j + 1) * SUBLANES, k * LANES:(k + 1) * LANES])
    for k in range(src.shape[1] // LANES):
        dst[:, k * LANES:(k + 1) * LANES] = stage[k]


def _u_specs(w, o_u, chunk, index):
    half = w // 2
    assert o_u % half == 0
    return [pl.BlockSpec((chunk, half), lambda c, k=k: (index(c), o_u // half + k)) for k in range(2)]


def _ssm_fwd(proj, o_u, bc_rows, rows_p, d_row, *, chunk, name, rider=None):
    rows = proj.shape[0]
    w = d_row.shape[1]
    nc = rows // chunk
    steps = chunk // SUBLANES
    nsb = w // MXU_DIM
    n_state = nsb * SB_STATE

    def body(ulo_ref, uhi_ref, b2r_ref, b2i_ref, c2r_ref, c2i_ref, abr_ref, abi_ref, cfr_ref, cfi_ref, apr_ref,
             api_ref, pwr_ref, pwi_ref, d_ref, y_ref, str_ref, sti_ref, yg_ref, bre_ref, bim_ref, cre_ref, cim_ref,
             useg, yseg, stage, sr, si,
             carry_r, carry_i, cm_r, cm_i):
        @pl.when(pl.program_id(0) == 0)
        def _():
            for src, dst in ((b2r_ref, bre_ref), (b2i_ref, bim_ref), (c2r_ref, cre_ref), (c2i_ref, cim_ref)):
                for sb in range(nsb):
                    dst[sb] = _block_diagonal(src[sb * MXU_DIM:(sb + 1) * MXU_DIM, :])
            carry_r[...] = jnp.zeros_like(carry_r)
            carry_i[...] = jnp.zeros_like(carry_i)

        str_ref[0] = carry_r[...]
        sti_ref[0] = carry_i[...]
        _rows_to_segments(useg, [(ulo_ref, 0), (uhi_ref, w // 2)], steps, stage)
        for sb in range(nsb):
            us = slice(sb * MXU_DIM, (sb + 1) * MXU_DIM)
            ss = slice(sb * SB_STATE, (sb + 1) * SB_STATE)
            ub = useg[:, us].astype(BF16)
            bur = _dot(ub, bre_ref[sb], NN)
            bui = _dot(ub, bim_ref[sb], NN)
            xr, xi = _cmul(cfr_ref[:, ss], cfi_ref[:, ss], bur, bui)
            sr[...] = xr
            si[...] = xi
            _scan_segments(sr, si, abr_ref[...], abi_ref[...], apr_ref[...], api_ref[...], pwr_ref, pwi_ref,
                           carry_r, carry_i, cm_r, cm_i, steps, False, sb * SB_STATE)
            y = _dot(sr[...].astype(BF16), cre_ref[sb], NT) - _dot(si[...].astype(BF16), cim_ref[sb], NT)
            yseg[:, us] = y + d_ref[:, us] * useg[:, us]
        _segments_to_rows(y_ref, yseg, steps, stage)
        yg_ref[...] = _gelu(y_ref[...]).astype(BF16)

    const = lambda a: pl.BlockSpec(a.shape, lambda c: (0,) * a.ndim)
    row_n = pl.BlockSpec((1, n_state), lambda c: (0, 0))
    st = pl.BlockSpec((1, 1, n_state), lambda c: (c, 0, 0))
    held = [pltpu.VMEM((nsb, MXU_DIM, SB_STATE), BF16)] * 4
    vmem = (4 * _nbytes((nsb, MXU_DIM, SB_STATE), BF16) + 4 * _nbytes((chunk, SB_STATE), F32)
            + 12 * _nbytes((chunk, w), F32) + 8 * _nbytes(bc_rows[0].shape, F32))
    res, landed = _call(
        body, [proj, proj, *bc_rows, *rows_p, d_row], name=name,
        out_shape=[jax.ShapeDtypeStruct((rows, w), F32), jax.ShapeDtypeStruct((nc, 1, n_state), F32),
                   jax.ShapeDtypeStruct((nc, 1, n_state), F32), jax.ShapeDtypeStruct((rows, w), BF16)],
        grid=(nc,),
        in_specs=_u_specs(w, o_u, chunk, lambda c: c) + [const(b) for b in bc_rows]
        + [row_n] * 6 + [pl.BlockSpec((steps, n_state), lambda c: (0, 0))] * 2 + [pl.BlockSpec((1, w), lambda c: (0, 0))],
        out_specs=[pl.BlockSpec((chunk, w), lambda c: (c, 0)), st, st, pl.BlockSpec((chunk, w), lambda c: (c, 0))],
        scratch_shapes=held + [pltpu.VMEM((chunk, w), F32), pltpu.VMEM((chunk, w), F32),
                               pltpu.VMEM((w // LANES, chunk, LANES), F32),
                               pltpu.VMEM((chunk, SB_STATE), F32), pltpu.VMEM((chunk, SB_STATE), F32),
                               pltpu.VMEM((1, n_state), F32), pltpu.VMEM((1, n_state), F32),
                               pltpu.VMEM((SUBLANES, n_state), F32), pltpu.VMEM((SUBLANES, n_state), F32)],
        semantics=("arbitrary",), vmem=vmem, rider=rider)
    return res if rider is None else (res, landed)


def _ssm_bwd(proj, o_u, y, dyg, st_re, st_im, bc_rows, rows_p, d_row, *, chunk, name, rider=None):
    rows = proj.shape[0]
    w = d_row.shape[1]
    nc = rows // chunk
    steps = chunk // SUBLANES
    nsb = w // MXU_DIM
    n_state = nsb * SB_STATE

    def body(ulo_ref, uhi_ref, y_ref, dyg_ref, str_ref, sti_ref, b2r_ref, b2i_ref, c2r_ref, c2i_ref, t2r_ref,
             t2i_ref, abr_ref, abi_ref, cfr_ref, cfi_ref, apr_ref, api_ref, pwr_ref, pwi_ref, d_ref,
             du_ref, gb2r_ref, gb2i_ref, gc2r_ref, gc2i_ref, gabr_ref, gabi_ref, dd_ref,
             bre_ref, bim_ref, cre_ref, cim_ref, btr_ref, bti_ref, dbre_ref, dbim_ref, dcre_ref, dcim_ref,
             useg, dyseg, dynat, stage, sr, si, lr, li, carry_r, carry_i, lam_r, lam_i, cm_r, cm_i, cl_r, cl_i):
        first = pl.program_id(0) == 0

        @pl.when(first)
        def _():
            for src, dst in ((b2r_ref, bre_ref), (b2i_ref, bim_ref), (c2r_ref, cre_ref), (c2i_ref, cim_ref),
                             (t2r_ref, btr_ref), (t2i_ref, bti_ref)):
                for sb in range(nsb):
                    dst[sb] = _block_diagonal(src[sb * MXU_DIM:(sb + 1) * MXU_DIM, :])
            lam_r[...] = jnp.zeros_like(lam_r)
            lam_i[...] = jnp.zeros_like(lam_i)
            for ref in (dbre_ref, dbim_ref, dcre_ref, dcim_ref, gabr_ref, gabi_ref, dd_ref):
                ref[...] = jnp.zeros_like(ref)

        dynat[...] = dyg_ref[...].astype(F32) * _dgelu(y_ref[...])
        half = w // 2
        dd_ref[:, :half] += jnp.sum(dynat[:, :half] * ulo_ref[...].astype(F32), axis=0, keepdims=True)
        dd_ref[:, half:] += jnp.sum(dynat[:, half:] * uhi_ref[...].astype(F32), axis=0, keepdims=True)
        _rows_to_segments(useg, [(ulo_ref, 0), (uhi_ref, half)], steps, stage)
        _rows_to_segments(dyseg, [(dynat, 0)], steps, stage)
        dy = dyseg[...]
        dyb = dy.astype(BF16)
        ub = useg[...].astype(BF16)
        carry_r[...] = str_ref[0]
        carry_i[...] = sti_ref[0]
        abr, abi = abr_ref[...], abi_ref[...]
        apr, api = apr_ref[...], api_ref[...]
        for sb in range(nsb):
            us = slice(sb * MXU_DIM, (sb + 1) * MXU_DIM)
            ss = slice(sb * SB_STATE, (sb + 1) * SB_STATE)
            base = sb * SB_STATE
            br = _dot(ub[:, us], bre_ref[sb], NN)
            bi = _dot(ub[:, us], bim_ref[sb], NN)
            xr, xi = _cmul(cfr_ref[:, ss], cfi_ref[:, ss], br, bi)
            sr[...] = xr
            si[...] = xi
            lr[...] = _dot(dyb[:, us], cre_ref[sb], NN)
            li[...] = -_dot(dyb[:, us], cim_ref[sb], NN)
            _scan_segments(sr, si, abr, abi, apr, api, pwr_ref, pwi_ref, carry_r, carry_i, cm_r, cm_i, steps, False,
                           base)
            dcre_ref[sb] += _dot(dyb[:, us], sr[...].astype(BF16), TN)
            dcim_ref[sb] -= _dot(dyb[:, us], si[...].astype(BF16), TN)
            _scan_segments(lr, li, abr, -abi, apr, -api, pwr_ref, pwi_ref, lam_r, lam_i, cl_r, cl_i, steps, True,
                           base)
            for c0 in range(0, SB_STATE, SCAN_LANES):
                ls = slice(c0, c0 + SCAN_LANES)
                gs = slice(base + c0, base + c0 + SCAN_LANES)

                def step(t, acc, ls=ls):
                    gar, gai, pr, pi = acc
                    for k in range(SCAN_UNROLL):
                        r0 = pl.multiple_of((t * SCAN_UNROLL + k) * SUBLANES, SUBLANES)
                        rws = pl.ds(r0, SUBLANES)
                        t_r, t_i = _cmul(pr, -pi, lr[rws, ls], li[rws, ls])
                        gar, gai, pr, pi = gar + t_r, gai + t_i, sr[rws, ls], si[rws, ls]
                    return gar, gai, pr, pi

                zero = jnp.zeros((SUBLANES, SCAN_LANES), F32)
                gar, gai, _, _ = lax.fori_loop(0, steps // SCAN_UNROLL, step,
                                               (zero, zero, cm_r[:, gs], cm_i[:, gs]))
                gabr_ref[:, gs] += gar
                gabi_ref[:, gs] += gai
            xr, xi = lr[...].astype(BF16), li[...].astype(BF16)
            du = _dot(xr, btr_ref[sb], NT) + _dot(xi, bti_ref[sb], NT)
            useg[:, us] = du + d_ref[:, us] * dy[:, us]
            dbre_ref[sb] += _dot(ub[:, us], xr, TN)
            dbim_ref[sb] += _dot(ub[:, us], xi, TN)
        _segments_to_rows(du_ref, useg, steps, stage)

        @pl.when(pl.program_id(0) == nc - 1)
        def _():
            for src, dst in ((dbre_ref, gb2r_ref), (dbim_ref, gb2i_ref), (dcre_ref, gc2r_ref), (dcim_ref, gc2i_ref)):
                for sb in range(nsb):
                    dst[sb * MXU_DIM:(sb + 1) * MXU_DIM, :] = _block_rows(src[sb])

    rev = lambda c: nc - 1 - c
    const = lambda a: pl.BlockSpec(a.shape, lambda c: (0,) * a.ndim)
    tile = pl.BlockSpec((chunk, w), lambda c: (rev(c), 0))
    row_n = pl.BlockSpec((1, n_state), lambda c: (0, 0))
    row_w = pl.BlockSpec((1, w), lambda c: (0, 0))
    st = pl.BlockSpec((1, 1, n_state), lambda c: (rev(c), 0, 0))
    acc8 = pl.BlockSpec((SUBLANES, n_state), lambda c: (0, 0))
    big = pltpu.VMEM((chunk, SB_STATE), F32)
    small = pltpu.VMEM((chunk, w), F32)
    row = pltpu.VMEM((1, n_state), F32)
    eight = pltpu.VMEM((SUBLANES, n_state), F32)
    blk = (nsb, MXU_DIM, SB_STATE)
    held = [pltpu.VMEM(blk, BF16)] * 6 + [pltpu.VMEM(blk, F32)] * 4
    vmem = (6 * _nbytes(blk, BF16) + 4 * _nbytes(blk, F32) + 5 * _nbytes((chunk, SB_STATE), F32)
            + 12 * _nbytes((chunk, w), F32) + 20 * _nbytes(bc_rows[0].shape, F32))
    res, landed = _call(
        body, [proj, proj, y, dyg, st_re, st_im, *bc_rows, *rows_p, d_row], name=name,
        out_shape=[jax.ShapeDtypeStruct((rows, w), F32)] + [jax.ShapeDtypeStruct(b.shape, F32) for b in bc_rows[:4]]
        + [jax.ShapeDtypeStruct((SUBLANES, n_state), F32)] * 2 + [jax.ShapeDtypeStruct((1, w), F32)],
        grid=(nc,),
        in_specs=_u_specs(w, o_u, chunk, rev) + [tile, tile, st, st] + [const(b) for b in bc_rows]
        + [row_n] * 6 + [pl.BlockSpec((steps, n_state), lambda c: (0, 0))] * 2 + [row_w],
        out_specs=[tile] + [const(b) for b in bc_rows[:4]] + [acc8] * 2 + [row_w],
        scratch_shapes=held + [small] * 3 + [pltpu.VMEM((w // LANES, chunk, LANES), F32)] + [big] * 4 + [row] * 4
        + [eight] * 4,
        semantics=("arbitrary",), vmem=vmem, rider=rider)
    return res if rider is None else (res, landed)


def _out_proj_loss(merged, w_o, x, target, name):
    rows, d = x.shape
    tm, tn = _tile(rows, 1024, SUBLANES), _tile(d, 1024)

    def body(a_ref, b_ref, x_ref, t_ref, g_ref, gb_ref, l_ref):
        err = x_ref[...] + _dot(a_ref[...], b_ref[...], NN) - t_ref[...]
        g = err * (1.0 / d)
        g_ref[...] = g
        gb_ref[...] = g.astype(BF16)
        part = jnp.sum(0.5 * err * g, axis=0, keepdims=True)
        first = pl.program_id(1) == 0

        @pl.when(first)
        def _():
            l_ref[...] = part

        @pl.when(jnp.logical_not(first))
        def _():
            l_ref[...] += part

    tile = pl.BlockSpec((tm, tn), lambda j, i: (i, j))
    vmem = 2 * (_nbytes((tm, d), BF16) + _nbytes((d, tn), BF16)) + 12 * _nbytes((tm, tn), F32)
    return _pallas(
        body, name=name,
        out_shape=[jax.ShapeDtypeStruct((rows, d), F32), jax.ShapeDtypeStruct((rows, d), BF16),
                   jax.ShapeDtypeStruct((1, d), F32)],
        grid=(d // tn, rows // tm),
        in_specs=[pl.BlockSpec((tm, d), lambda j, i: (i, 0)), pl.BlockSpec((d, tn), lambda j, i: (0, j)), tile, tile],
        out_specs=[tile, tile, pl.BlockSpec((1, tn), lambda j, i: (0, j))],
        compiler_params=_params(("parallel", "arbitrary"), vmem),
    )(merged, w_o, x, target)


def _pair_sum(grad, recv, name):
    r4, cdim = recv.shape
    r = r4 // N_CHIPS
    tr = _tile(r, 544, 16)
    g4 = grad.reshape(N_CHIPS, 2, r, cdim)
    r3 = recv.reshape(N_CHIPS, r, cdim)
    core = jnp.reshape(lax.axis_index("c"), (1,)).astype(jnp.int32)

    def body(c_ref, g_ref, r_ref, o_ref):
        o_ref[...] = (g_ref[0] + r_ref[...]).astype(BF16)

    out = _pallas(
        body, name=name, out_shape=jax.ShapeDtypeStruct((N_CHIPS, r, cdim), BF16),
        grid_spec=pltpu.PrefetchScalarGridSpec(
            num_scalar_prefetch=1, grid=(N_CHIPS, r // tr),
            in_specs=[pl.BlockSpec((1, 1, tr, cdim), lambda j, i, c: (j, c[0], i, 0)),
                      pl.BlockSpec((1, tr, cdim), lambda j, i, c: (j, i, 0))],
            out_specs=pl.BlockSpec((1, tr, cdim), lambda j, i, c: (j, i, 0))),
        compiler_params=_params(("parallel", "parallel"), 6 * _nbytes((tr, cdim), F32)),
    )(core, g4, r3)
    return out.reshape(r4, cdim)


def _adamw_chips_transposed(w, recv, m, v, name):
    rows, cols = w.shape
    tr = _tile(rows, MXU_DIM, LANES)

    def body(w_ref, r_ref, m_ref, v_ref, g_ref, d_ref, nm_ref, nv_ref):
        acc = r_ref[0].astype(F32)
        for j in range(1, N_CHIPS):
            acc = acc + r_ref[j].astype(F32)
        g = acc.T
        d, nm, nv = _adamw_math(w_ref[...], g, m_ref[...], v_ref[...])
        g_ref[...] = g
        d_ref[...] = d
        nm_ref[...] = nm
        nv_ref[...] = nv

    spec = pl.BlockSpec((tr, cols), lambda i: (i, 0))
    shp = jax.ShapeDtypeStruct((rows, cols), F32)
    return _pallas(
        body, name=name, out_shape=[shp] * 4, grid=(rows // tr,),
        in_specs=[spec, pl.BlockSpec((N_CHIPS, cols, tr), lambda i: (0, 0, i)), spec, spec], out_specs=[spec] * 4,
        compiler_params=_params(("parallel",)),
    )(w, recv.reshape(N_CHIPS, cols, rows), m, v)


def _adamw_math(w, g, m, v):
    m = ADAM_B1 * m + (1.0 - ADAM_B1) * g
    v = ADAM_B2 * v + (1.0 - ADAM_B2) * (g * g)
    m_hat = m / (1.0 - ADAM_B1 ** ADAM_STEP)
    v_hat = v / (1.0 - ADAM_B2 ** ADAM_STEP)
    delta = -ADAM_LR * (m_hat / (jnp.sqrt(v_hat) + ADAM_EPS) + ADAM_WD * w)
    return delta, m, v


def _adamw(w, g, m, v, name):
    rows, rest = w.shape[0], w.shape[1:]
    tr = _tile(rows, 256, SUBLANES) if len(rest) == 1 else SUBLANES
    assert rows % tr == 0

    def body(w_ref, g_ref, m_ref, v_ref, d_ref, nm_ref, nv_ref):
        d, nm, nv = _adamw_math(w_ref[...], g_ref[...], m_ref[...], v_ref[...])
        d_ref[...] = d
        nm_ref[...] = nm
        nv_ref[...] = nv

    spec = pl.BlockSpec((tr,) + rest, lambda i: (i,) + (0,) * len(rest))
    shp = jax.ShapeDtypeStruct(w.shape, F32)
    return _pallas(
        body, name=name, out_shape=[shp] * 3, grid=(rows // tr,), in_specs=[spec] * 4, out_specs=[spec] * 3,
        compiler_params=_params(("parallel",)),
    )(w, g, m, v)


def _adamw_chips(w, parts, m, v, name):
    rows, cols = w.shape
    assert sum(p.shape[1] for p in parts) == cols
    tr = _tile(rows, 64, 16)
    n = len(parts)

    def body(*refs):
        w_ref, m_ref, v_ref = refs[0], refs[1 + n], refs[2 + n]
        g_ref, d_ref, nm_ref, nv_ref = refs[3 + n:]
        cols_g = []
        for p_ref in refs[1:1 + n]:
            acc = p_ref[0].astype(F32)
            for j in range(1, N_CHIPS):
                acc = acc + p_ref[j].astype(F32)
            cols_g.append(acc)
        g = cols_g[0] if n == 1 else jnp.concatenate(cols_g, axis=1)
        d, nm, nv = _adamw_math(w_ref[...], g, m_ref[...], v_ref[...])
        g_ref[...] = g
        d_ref[...] = d
        nm_ref[...] = nm
        nv_ref[...] = nv

    spec = pl.BlockSpec((tr, cols), lambda i: (i, 0))
    part_specs = [pl.BlockSpec((N_CHIPS, tr, p.shape[1]), lambda i: (0, i, 0)) for p in parts]
    shp = jax.ShapeDtypeStruct((rows, cols), F32)
    return _pallas(
        body, name=name, out_shape=[shp] * 4, grid=(rows // tr,),
        in_specs=[spec] + part_specs + [spec, spec], out_specs=[spec] * 4,
        compiler_params=_params(("parallel",)),
    )(w, *[p.reshape(N_CHIPS, rows, p.shape[1]) for p in parts], m, v)


def _device_sum(parts, name):
    rows = parts.shape[0] // N_DEV

    def body(p_ref, g_ref):
        g = p_ref[0]
        for k in range(1, N_DEV):
            g = g + p_ref[k]
        g_ref[...] = g

    return _pallas(body, name=name, out_shape=jax.ShapeDtypeStruct((rows, LANES), F32))(
        parts.reshape(N_DEV, rows, LANES))


def _vmem_footprint(shape):
    dims = (1,) * (2 - len(shape)) + tuple(shape)
    padded = dims[:-2] + (-(-dims[-2] // SUBLANES) * SUBLANES, -(-dims[-1] // LANES) * LANES)
    return _nbytes(padded, F32)


def _adamw_native(ws, gs, ms, vs, name):
    n = len(ws)

    def body(*refs):
        ins, outs = refs[:4 * n], refs[4 * n:]
        for i in range(n):
            d, nm, nv = _adamw_math(ins[i][...], ins[n + i][...], ins[2 * n + i][...], ins[3 * n + i][...])
            outs[i][...] = d
            outs[n + i][...] = nm
            outs[2 * n + i][...] = nv

    shapes = [jax.ShapeDtypeStruct(w.shape, F32) for w in ws] * 3
    vmem = 8 * sum(_vmem_footprint(w.shape) for w in ws)
    res = _pallas(body, name=name, out_shape=shapes, compiler_params=_params(vmem=vmem))(*ws, *gs, *ms, *vs)
    return res[:n], res[n:2 * n], res[2 * n:]


SMALL = ("norm_w", "q_norm_w", "k_norm_w", "sinks", "A_re", "A_im", "log_dt", "B_re", "B_im", "C_re", "C_im",
         "D_skip", "b_glu")
LARGE = ("w_in", "w_attn_proj", "w_glu", "w_ssm_proj", "w_out")
ORDER = ("norm_w", "w_in", "q_norm_w", "k_norm_w", "sinks", "w_attn_proj", "A_re", "A_im", "log_dt", "B_re", "B_im",
         "C_re", "C_im", "D_skip", "w_glu", "b_glu", "w_ssm_proj", "w_out")


SMALL_REST = ("loss",) + SMALL[1:]


def _pack(named, keys):
    flat = jnp.concatenate([named[k].reshape(-1).astype(F32) for k in keys])
    n = flat.shape[0]
    rows = -(-n // (LANES * SUBLANES)) * SUBLANES
    return jnp.pad(flat, (0, rows * LANES - n)).reshape(rows, LANES)


def _unpack(packed, like, keys):
    flat = packed.reshape(-1)
    out, o = {}, 0
    for k in keys:
        n = like[k].size
        out[k] = flat[o:o + n].reshape(like[k].shape)
        o += n
    return out


def _step(xs, target, p, shards):
    s_in, s_ap, s_glu, s_sp, s_o = shards
    seq, d = xs.shape
    attn_w = (d // 128) * HEAD_DIM
    n_q = attn_w // HEAD_DIM
    kv_w = N_KV_HEADS * HEAD_DIM
    ssm_w = d // 2
    n_groups = ssm_w // GROUP
    n_state = n_groups * STATE
    in_w = N_DEV * s_in.shape[0]
    assert in_w == 2 * attn_w + 2 * kv_w + 2 * ssm_w + 2 * d
    o_u = 2 * attn_w + 2 * kv_w
    o_z = o_u + ssm_w
    o_ga = o_z + ssm_w
    chunk = min(SSM_CHUNK, seq)
    cw = d // 4

    norm_row = p["norm_w"].reshape(1, d)
    half = d // W_IN_PARTS
    assert W_IN_PARTS == 2
    s_in_parts = [s_in[:, :half], s_in[:, half:]]
    h, (w_lo,) = _rmsnorm_fwd(xs, norm_row, "rmsnorm_fwd", rider=_all_gather(s_in_parts[:1]))
    part, (w_hi,) = _matmul(Cols(h, 0, half), w_lo, mode="nt", name="in_proj_0", tn=2176, out_dtype=BF16,
                            rider=_all_gather(s_in_parts[1:]))
    proj = _matmul(Cols(h, half, half), w_hi, mode="nt", name="in_proj_1", tn=2176, out_dtype=BF16, add=part)
    w_in_parts = [w_lo, w_hi]
    qw_row = jnp.tile(p["q_norm_w"], n_q).reshape(1, attn_w)
    kw_row = jnp.tile(p["k_norm_w"], N_KV_HEADS).reshape(1, kv_w)
    gmat = _head_mean_matrix()
    ag = _attention_fwd(proj, qw_row, kw_row, gmat, p["sinks"], attn_w=attn_w, kv_w=kv_w, name="attention_fwd")

    log_dt_col = p["log_dt"].reshape(n_groups, 1)
    prep = _ssm_prep(p["A_re"], p["A_im"], log_dt_col, chunk // SUBLANES, "ssm_prep")
    rows_p = [v.reshape(1, n_state) for v in prep[:6]] + [v.reshape(-1, n_state) for v in prep[6:]]
    bt_re, bt_im = p["B_re"].transpose(0, 2, 1), p["B_im"].transpose(0, 2, 1)
    cf_re, cf_im = prep[2][:, None, :], prep[3][:, None, :]
    bc_rows = [_ssm_rows(m) for m in (bt_re, bt_im, p["C_re"], p["C_im"],
                                      cf_re * bt_re - cf_im * bt_im, cf_re * bt_im + cf_im * bt_re)]
    d_row = p["D_skip"].reshape(1, ssm_w)
    (y_ssm, st_re, st_im, yg), (w_ap_t, w_glu_t, w_sp_t, w_o) = _ssm_fwd(
        proj, o_u, bc_rows[:4], rows_p, d_row, chunk=chunk, name="ssm_fwd",
        rider=_all_gather([s_ap, s_glu, s_sp, s_o]))
    glu = _matmul(yg, w_glu_t, mode="nt", name="glu_proj", out_dtype=BF16, bias=p["b_glu"].reshape(1, 2 * ssm_w))
    (ts,) = _ew(lambda ga, gb, z: ga * _sigmoid(gb) * _silu(z), name="glu_gate", rows=seq, width=ssm_w,
                tiles=[(glu, 0), (glu, ssm_w), (proj, o_z)], outs=[(BF16, ssm_w, 0)], cw=cw)
    yy = _matmul(ag, w_ap_t, mode="nt", name="attn_proj", out_dtype=BF16, out_cols=(2 * d, 0))
    yy = _matmul(ts, w_sp_t, mode="nt", name="ssm_proj", out_dtype=BF16, out_cols=(2 * d, d), into=yy)
    (merged,) = _ew(lambda ya, ys, ga, gs: _sigmoid(ga) * ya + _sigmoid(gs) * ys, name="merge", rows=seq, width=d,
                    tiles=[(yy, 0), (yy, d), (proj, o_ga), (proj, o_ga + d)], outs=[(BF16, d, 0)], cw=cw)
    dout, dout_b, loss_cols = _out_proj_loss(merged, w_o, xs, target, "out_proj_loss")
    loss_local = jnp.sum(loss_cols)

    g_w_o = _matmul(merged, dout_b, mode="tn", name="grad_w_out", tm=512, tk=4096)
    dmerged = _matmul(dout_b, w_o, mode="nt", name="d_merged", out_dtype=BF16)

    def merge_bwd(dm, y, g):
        s = _sigmoid(g)
        return dm * s, dm * y * s * (1.0 - s)

    dyy, dproj = _ew(merge_bwd, name="merge_bwd", rows=seq, width=2 * d,
                     tiles=[(dmerged, 0, d), (yy, 0), (proj, o_ga)],
                     outs=[(BF16, 2 * d, 0), (BF16, in_w, o_ga)], cw=cw)
    dy_a, dy_s = Cols(dyy, 0, d), Cols(dyy, d, d)
    g_w_ap_t = _matmul(dy_a, ag, mode="tn", name="grad_w_attn_proj", tm=512, tk=4096)
    g_w_sp_t = _matmul(dy_s, ts, mode="tn", name="grad_w_ssm_proj", tm=512, tk=4096)
    d_ag = _matmul(dy_a, w_ap_t, mode="nn", name="d_attn_gated", out_dtype=BF16)
    d_ts = _matmul(dy_s, w_sp_t, mode="nn", name="d_ssm_gated", out_dtype=BF16)

    (dproj, g_qw, g_kw, g_sinks), (sib_o, sib_ap, sib_sp) = _attention_bwd(
        proj, d_ag, dproj, qw_row, kw_row, gmat, p["sinks"], attn_w=attn_w, kv_w=kv_w, name="attention_bwd",
        rider=_sibling_exchange([g_w_o, g_w_ap_t, g_w_sp_t]))
    pair_o = _pair_sum(g_w_o, sib_o, "pair_sum_w_out")
    pair_ap = _pair_sum(g_w_ap_t, sib_ap, "pair_sum_w_attn_proj")
    pair_sp = _pair_sum(g_w_sp_t, sib_sp, "pair_sum_w_ssm_proj")

    n_half = ssm_w // _tile(2 * ssm_w, cw)

    def glu_bwd(j, dt, ga, gb, z):
        sb, sz = _sigmoid(gb), _silu(z)
        dg = jnp.where(j < n_half, dt * sb * sz, dt * ga * sb * (1.0 - sb) * sz)
        return dg, dg

    glu_ops = [(d_ts, 0, ssm_w), (glu, 0, ssm_w), (glu, ssm_w, ssm_w), (proj, o_z, ssm_w)]
    dglu, g_bglu = _ew(glu_bwd, name="glu_bwd", rows=seq, width=2 * ssm_w, tiles=glu_ops,
                       outs=[(BF16, 2 * ssm_w, 0)], accs=1, cw=cw, with_col=True)
    (dproj,) = _ew(lambda dt, ga, gb, z: dt * ga * _sigmoid(gb) * _dsilu(z), name="glu_bwd_z", rows=seq,
                   width=ssm_w, tiles=glu_ops, outs=[(BF16, in_w, o_z)], into=[dproj], cw=cw)
    g_w_glu_t = _matmul(dglu, yg, mode="tn", name="grad_w_glu", tm=512, tk=4096)
    d_yg = _matmul(dglu, w_glu_t, mode="nn", name="d_gelu", out_dtype=BF16)
    ((du, dbt_re, dbt_im, dc_re, dc_im, gabr, gabi, g_d), (chips_o, chips_ap, chips_sp, sib_glu)) = _ssm_bwd(
        proj, o_u, y_ssm, d_yg, st_re, st_im, bc_rows, rows_p, d_row, chunk=chunk, name="ssm_bwd",
        rider=_join(_chip_exchange([pair_o, pair_ap, pair_sp]), _sibling_exchange([g_w_glu_t])))
    pair_glu = _pair_sum(g_w_glu_t, sib_glu, "pair_sum_w_glu")
    (dproj,) = _ew(lambda v: v, name="du_store", rows=seq, width=ssm_w, tiles=[(du, 0)],
                   outs=[(BF16, in_w, o_u)], into=[dproj], cw=cw)
    g_a_re, g_a_im, g_log_dt, g_bt_re, g_bt_im = _ssm_param_bwd(
        p["A_re"], p["A_im"], log_dt_col, *[g.reshape(SUBLANES, n_groups, STATE) for g in (gabr, gabi)],
        bt_re, bt_im, _from_ssm_rows(dbt_re), _from_ssm_rows(dbt_im), "ssm_param_bwd")
    small_grads = dict(
        loss=loss_local, q_norm_w=g_qw.reshape(n_q, HEAD_DIM).sum(0), k_norm_w=g_kw.reshape(N_KV_HEADS, HEAD_DIM).sum(0),
        sinks=g_sinks[0, :n_q], A_re=g_a_re, A_im=g_a_im, log_dt=g_log_dt.reshape(n_groups),
        B_re=g_bt_re, B_im=g_bt_im,
        C_re=_from_ssm_rows(dc_re), C_im=_from_ssm_rows(dc_im),
        D_skip=g_d.reshape(n_groups, GROUP), b_glu=g_bglu.reshape(2 * ssm_w))

    n_parts = W_IN_PARTS
    wq = d // n_parts
    g_parts, pair_parts, chip_parts = [], [], []
    extra = [_chip_exchange([pair_glu]), _all_gather([_pack(small_grads, SMALL_REST)])]
    chips_glu = small_parts = dh = None
    for step in range(n_parts + 2):
        riders = list(extra) if step == 0 else []
        if 0 <= step - 2 < n_parts:
            riders.append(_chip_exchange([pair_parts[step - 2]]))
        if 0 <= step - 1 < n_parts:
            riders.append(_sibling_exchange([g_parts[step - 1]]))
        rider = _join(*riders) if riders else None
        if step < n_parts:
            res = _matmul(dproj, Cols(h, step * wq, wq), mode="tn", name="grad_w_in_%d" % step, tk=4096, rider=rider)
            out, landed = res if rider is not None else (res, [])
            g_parts.append(out)
        else:
            q = step - n_parts
            dh, landed = _matmul(dproj, w_in_parts[q], mode="nn", name="d_normed_%d" % q, tk=2176,
                                 out_cols=(d, q * wq), into=dh, rider=rider)
        landed = list(landed)
        if step == 0:
            chips_glu, small_parts = landed[:2]
            landed = landed[2:]
        if 0 <= step - 2 < n_parts:
            chip_parts.append(landed.pop(0))
        if 0 <= step - 1 < n_parts:
            pair_parts.append(_pair_sum(g_parts[step - 1], landed.pop(0), "pair_sum_w_in_%d" % (step - 1)))
    grad_x, g_norm = _rmsnorm_bwd(xs, norm_row, dh, dout, "rmsnorm_bwd")
    (norm_parts,) = _exchange(_all_gather([_pack(dict(norm_w=g_norm), ("norm_w",))]), "gather_norm_grad")
    from_chips = dict(zip(LARGE, (chip_parts, [chips_ap], [chips_glu], [chips_sp], [chips_o])))
    return grad_x, from_chips, small_parts, norm_parts


def kernel(x, norm_w, w_in, q_norm_w, k_norm_w, sinks, w_attn_proj, A_re, A_im, log_dt, B_re, B_im, C_re, C_im, D_skip, w_glu, b_glu, w_ssm_proj, w_out, loss_target, m_norm_w, m_w_in, m_q_norm_w, m_k_norm_w, m_sinks, m_w_attn_proj, m_A_re, m_A_im, m_log_dt, m_B_re, m_B_im, m_C_re, m_C_im, m_D_skip, m_w_glu, m_b_glu, m_w_ssm_proj, m_w_out, v_norm_w, v_w_in, v_q_norm_w, v_k_norm_w, v_sinks, v_w_attn_proj, v_A_re, v_A_im, v_log_dt, v_B_re, v_B_im, v_C_re, v_C_im, v_D_skip, v_w_glu, v_b_glu, v_w_ssm_proj, v_w_out):
    weights = dict(norm_w=norm_w, w_in=w_in, q_norm_w=q_norm_w, k_norm_w=k_norm_w, sinks=sinks,
                   w_attn_proj=w_attn_proj, A_re=A_re, A_im=A_im, log_dt=log_dt, B_re=B_re, B_im=B_im, C_re=C_re,
                   C_im=C_im, D_skip=D_skip, w_glu=w_glu, b_glu=b_glu, w_ssm_proj=w_ssm_proj, w_out=w_out)
    m_in = dict(norm_w=m_norm_w, w_in=m_w_in, q_norm_w=m_q_norm_w, k_norm_w=m_k_norm_w, sinks=m_sinks,
                w_attn_proj=m_w_attn_proj, A_re=m_A_re, A_im=m_A_im, log_dt=m_log_dt, B_re=m_B_re, B_im=m_B_im,
                C_re=m_C_re, C_im=m_C_im, D_skip=m_D_skip, w_glu=m_w_glu, b_glu=m_b_glu, w_ssm_proj=m_w_ssm_proj,
                w_out=m_w_out)
    v_in = dict(norm_w=v_norm_w, w_in=v_w_in, q_norm_w=v_q_norm_w, k_norm_w=v_k_norm_w, sinks=v_sinks,
                w_attn_proj=v_w_attn_proj, A_re=v_A_re, A_im=v_A_im, log_dt=v_log_dt, B_re=v_B_re, B_im=v_B_im,
                C_re=v_C_re, C_im=v_C_im, D_skip=v_D_skip, w_glu=v_w_glu, b_glu=v_b_glu, w_ssm_proj=v_w_ssm_proj,
                w_out=v_w_out)

    _, seq, d = x.shape
    column_sharded = LARGE[:4]
    as_rows = lambda k, a: a.T if k in column_sharded else a
    shards = [as_rows(k, weights[k]).astype(BF16) for k in LARGE]
    small = {k: weights[k] for k in SMALL}
    grad_x, from_chips, small_parts, norm_parts = _step(x.reshape(seq, d), loss_target.reshape(seq, d), small,
                                                        shards)

    grads, delta, new_m, new_v = {}, {}, {}, {}
    for k in LARGE:
        if k == "w_in":
            res = _adamw_chips(weights[k].T, from_chips[k], m_in[k].T, v_in[k].T, "adamw_" + k)
            grads[k], delta[k], new_m[k], new_v[k] = [a.T for a in res]
        elif k == "w_out":
            grads[k], delta[k], new_m[k], new_v[k] = _adamw_chips(weights[k], from_chips[k], m_in[k], v_in[k],
                                                                  "adamw_" + k)
        else:
            grads[k], delta[k], new_m[k], new_v[k] = _adamw_chips_transposed(
                weights[k], from_chips[k][0], m_in[k], v_in[k], "adamw_" + k)

    stored = lambda k, a: a.transpose(0, 2, 1) if k in ("B_re", "B_im") else a
    like = dict({k: stored(k, a) for k, a in small.items()}, loss=jnp.zeros((), F32))
    for keys, parts in ((SMALL_REST, small_parts), (("norm_w",), norm_parts)):
        grads.update(_unpack(_device_sum(parts, "device_sum_%d" % len(keys)), like, keys))
    loss = grads["loss"]
    tiled = [k for k in SMALL if weights[k].ndim == 3]
    for k in tiled:
        res = _adamw(stored(k, weights[k]), grads[k], stored(k, m_in[k]), stored(k, v_in[k]), "adamw_" + k)
        delta[k], new_m[k], new_v[k] = [stored(k, r) for r in res]
        grads[k] = stored(k, grads[k])
    whole = [k for k in SMALL if k not in tiled]
    res = _adamw_native(*[[src[k] for k in whole] for src in (weights, grads, m_in, v_in)], "adamw_small")
    for dst, r in zip((delta, new_m, new_v), res):
        dst.update(zip(whole, r))

    return (loss, grad_x.reshape(x.shape), *[grads[k] for k in ORDER], *[delta[k] for k in ORDER],
            *[new_m[k] for k in ORDER], *[new_v[k] for k in ORDER])
```
